```python
import jax, jax.numpy as jnp
from jax import lax
import numpy as np

D_MODEL = 1024
BATCH = 8
SEQ = 4096
DEPTH = 2

CHUNK = 128
N_SGU_GROUPS = 8
SGU_WIDTH = D_MODEL
SGU_GROUP_DIM = SGU_WIDTH // N_SGU_GROUPS
N_HEADS = 8
HEAD_DIM = 128
ATTN_WIDTH = N_HEADS * HEAD_DIM
Q_BLOCK = 128
D_FF = -(-8 * D_MODEL // (3 * 256)) * 256
EPS = 1e-6

_IN_SIZES = (SGU_WIDTH, SGU_WIDTH, ATTN_WIDTH, ATTN_WIDTH, ATTN_WIDTH, D_MODEL, D_MODEL, N_HEADS)
IN_WIDTH = sum(_IN_SIZES)
IN_SPLITS = tuple(int(s) for s in np.cumsum(_IN_SIZES)[:-1])

kernel_name = "hybrid_gmlp_fox_gated_block"


def rmsnorm(x, g):
    xf = x.astype(jnp.float32)
    r = lax.rsqrt(jnp.mean(xf * xf, axis=-1, keepdims=True) + EPS)
    return (xf * r * g.astype(jnp.float32)).astype(x.dtype)


def layernorm(x, g):
    xf = x.astype(jnp.float32)
    mu = jnp.mean(xf, axis=-1, keepdims=True)
    xc = xf - mu
    r = lax.rsqrt(jnp.mean(xc * xc, axis=-1, keepdims=True) + EPS)
    return (xc * r * g.astype(jnp.float32)).astype(x.dtype)


def spatial_gating(u, v, w_s, b_s, g_v):
    bsz, seq = v.shape[0], v.shape[1]
    v = layernorm(v, g_v)
    vc = v.reshape(bsz, seq // CHUNK, CHUNK, N_SGU_GROUPS, SGU_GROUP_DIM)
    causal = jnp.tril(jnp.ones((CHUNK, CHUNK), dtype=bool))
    w = jnp.where(causal[None], w_s, jnp.zeros_like(w_s))
    mixed = jnp.einsum('gts,bcsgd->bctgd', w, vc) + b_s.T[None, None, :, :, None]
    return u * mixed.reshape(bsz, seq, SGU_WIDTH)


def forgetting_attention(q, k, v, f_logit, b_f):
    bsz, seq = q.shape[0], q.shape[1]

    def heads(t):
        return t.reshape(bsz, seq, N_HEADS, HEAD_DIM).transpose(0, 2, 1, 3)

    q = heads(q) * (HEAD_DIM ** -0.5)
    k = heads(k)
    v = heads(v)
    log_f = jax.nn.log_sigmoid((f_logit + b_f).astype(jnp.float32))
    c = jnp.cumsum(log_f, axis=1).transpose(0, 2, 1)
    k_pos = jnp.arange(seq)

    def block(i):
        start = i * Q_BLOCK
        qi = lax.dynamic_slice_in_dim(q, start, Q_BLOCK, axis=2)
        ci = lax.dynamic_slice_in_dim(c, start, Q_BLOCK, axis=2)
        s = jnp.einsum('bhqd,bhkd->bhqk', qi, k).astype(jnp.float32)
        s = s + ci[..., None] - c[:, :, None, :]
        q_pos = start + jnp.arange(Q_BLOCK)
        s = jnp.where(k_pos[None, :] <= q_pos[:, None], s, -jnp.inf)
        p = jax.nn.softmax(s, axis=-1).astype(v.dtype)
        return jnp.einsum('bhqk,bhkd->bhqd', p, v)

    out = lax.map(block, jnp.arange(seq // Q_BLOCK))
    return out.transpose(1, 0, 3, 2, 4).reshape(bsz, seq, ATTN_WIDTH)


def swiglu(h, w_gate, w_up, w_down):
    return (jax.nn.silu(h @ w_gate) * (h @ w_up)) @ w_down


def _fwd_setup_inputs(seed: int = 0) -> dict:
    key = jax.random.key(seed)
    ks = jax.random.split(key, 15)
    f32 = jnp.float32

    def nrm(k, shape, scale):
        return jax.random.normal(k, shape, f32) * scale

    def gain(k, shape):
        return 1.0 + 0.1 * jax.random.normal(k, shape, f32)

    x = jax.random.normal(ks[0], (BATCH, SEQ, D_MODEL), f32)
    mix_pre_g = gain(ks[1], (DEPTH, D_MODEL))
    w_in = nrm(ks[2], (DEPTH, D_MODEL, IN_WIDTH), D_MODEL ** -0.5)
    b_forget = jnp.linspace(1.0, 6.0, N_HEADS, dtype=f32)[None, :] + 0.1 * jax.random.normal(ks[3], (DEPTH, N_HEADS), f32)
    sgu_norm_g = gain(ks[4], (DEPTH, SGU_WIDTH))
    w_spatial = nrm(ks[5], (DEPTH, N_SGU_GROUPS, CHUNK, CHUNK), CHUNK ** -0.5)
    b_spatial = 1.0 + 0.1 * jax.random.normal(ks[6], (DEPTH, N_SGU_GROUPS, CHUNK), f32)
    w_out = nrm(ks[7], (DEPTH, D_MODEL, D_MODEL), D_MODEL ** -0.5)
    mix_post_g = gain(ks[8], (DEPTH, D_MODEL))
    ffn_pre_g = gain(ks[9], (DEPTH, D_MODEL))
    w_gate = nrm(ks[10], (DEPTH, D_MODEL, D_FF), D_MODEL ** -0.5)
    w_up = nrm(ks[11], (DEPTH, D_MODEL, D_FF), D_MODEL ** -0.5)
    w_down = nrm(ks[12], (DEPTH, D_FF, D_MODEL), D_FF ** -0.5)
    ffn_post_g = gain(ks[13], (DEPTH, D_MODEL))
    return {"x": x, "mix_pre_g": mix_pre_g, "w_in": w_in, "b_forget": b_forget,
            "sgu_norm_g": sgu_norm_g, "w_spatial": w_spatial, "b_spatial": b_spatial,
            "w_out": w_out, "mix_post_g": mix_post_g, "ffn_pre_g": ffn_pre_g,
            "w_gate": w_gate, "w_up": w_up, "w_down": w_down, "ffn_post_g": ffn_post_g}


def _fwd_reference(x, mix_pre_g, w_in, b_forget, sgu_norm_g, w_spatial, b_spatial, w_out,
              mix_post_g, ffn_pre_g, w_gate, w_up, w_down, ffn_post_g):
    for l in range(DEPTH):
        h = rmsnorm(x, mix_pre_g[l])
        proj = h @ w_in[l]
        u, v_s, q, k, v_a, g_a, g_b, f_logit = jnp.split(proj, IN_SPLITS, axis=-1)
        y_a = spatial_gating(jax.nn.gelu(u), jax.nn.gelu(v_s), w_spatial[l], b_spatial[l], sgu_norm_g[l])
        y_b = forgetting_attention(q, k, v_a, f_logit, b_forget[l])
        merged = jax.nn.sigmoid(g_a) * y_a + jax.nn.sigmoid(g_b) * y_b
        x = x + rmsnorm(merged @ w_out[l], mix_post_g[l])
        h = rmsnorm(x, ffn_pre_g[l])
        x = x + rmsnorm(swiglu(h, w_gate[l], w_up[l], w_down[l]), ffn_post_g[l])
    return x


import jax as _jax
import jax.numpy as _jnp

TWIN_FORMAT = 'train_step'
FWD_PARAMS = ['x', 'mix_pre_g', 'w_in', 'b_forget', 'sgu_norm_g', 'w_spatial', 'b_spatial', 'w_out', 'mix_post_g', 'ffn_pre_g', 'w_gate', 'w_up', 'w_down', 'ffn_post_g']
TWIN_WEIGHTS = ['mix_pre_g', 'w_in', 'b_forget', 'sgu_norm_g', 'w_spatial', 'b_spatial', 'w_out', 'mix_post_g', 'ffn_pre_g', 'w_gate', 'w_up', 'w_down', 'ffn_post_g']
TWIN_DIFF_INPUT = 'x'
TWIN_INPUTS = ['x', 'mix_pre_g', 'w_in', 'b_forget', 'sgu_norm_g', 'w_spatial', 'b_spatial', 'w_out', 'mix_post_g', 'ffn_pre_g', 'w_gate', 'w_up', 'w_down', 'ffn_post_g', 'loss_target', 'm_mix_pre_g', 'm_w_in', 'm_b_forget', 'm_sgu_norm_g', 'm_w_spatial', 'm_b_spatial', 'm_w_out', 'm_mix_post_g', 'm_ffn_pre_g', 'm_w_gate', 'm_w_up', 'm_w_down', 'm_ffn_post_g', 'v_mix_pre_g', 'v_w_in', 'v_b_forget', 'v_sgu_norm_g', 'v_w_spatial', 'v_b_spatial', 'v_w_out', 'v_mix_post_g', 'v_ffn_pre_g', 'v_w_gate', 'v_w_up', 'v_w_down', 'v_ffn_post_g']
TWIN_OUTPUTS = ['loss', 'grad_x', 'grad_mix_pre_g', 'grad_w_in', 'grad_b_forget', 'grad_sgu_norm_g', 'grad_w_spatial', 'grad_b_spatial', 'grad_w_out', 'grad_mix_post_g', 'grad_ffn_pre_g', 'grad_w_gate', 'grad_w_up', 'grad_w_down', 'grad_ffn_post_g', 'delta_mix_pre_g', 'delta_w_in', 'delta_b_forget', 'delta_sgu_norm_g', 'delta_w_spatial', 'delta_b_spatial', 'delta_w_out', 'delta_mix_post_g', 'delta_ffn_pre_g', 'delta_w_gate', 'delta_w_up', 'delta_w_down', 'delta_ffn_post_g', 'new_m_mix_pre_g', 'new_m_w_in', 'new_m_b_forget', 'new_m_sgu_norm_g', 'new_m_w_spatial', 'new_m_b_spatial', 'new_m_w_out', 'new_m_mix_post_g', 'new_m_ffn_pre_g', 'new_m_w_gate', 'new_m_w_up', 'new_m_w_down', 'new_m_ffn_post_g', 'new_v_mix_pre_g', 'new_v_w_in', 'new_v_b_forget', 'new_v_sgu_norm_g', 'new_v_w_spatial', 'new_v_b_spatial', 'new_v_w_out', 'new_v_mix_post_g', 'new_v_ffn_pre_g', 'new_v_w_gate', 'new_v_w_up', 'new_v_w_down', 'new_v_ffn_post_g']
TWIN_LEAF_KINDS = {'loss': 'loss', 'grad_x': 'grad_x', 'grad_mix_pre_g': 'grad_w', 'grad_w_in': 'grad_w', 'grad_b_forget': 'grad_w', 'grad_sgu_norm_g': 'grad_w', 'grad_w_spatial': 'grad_w', 'grad_b_spatial': 'grad_w', 'grad_w_out': 'grad_w', 'grad_mix_post_g': 'grad_w', 'grad_ffn_pre_g': 'grad_w', 'grad_w_gate': 'grad_w', 'grad_w_up': 'grad_w', 'grad_w_down': 'grad_w', 'grad_ffn_post_g': 'grad_w', 'delta_mix_pre_g': 'delta_w', 'delta_w_in': 'delta_w', 'delta_b_forget': 'delta_w', 'delta_sgu_norm_g': 'delta_w', 'delta_w_spatial': 'delta_w', 'delta_b_spatial': 'delta_w', 'delta_w_out': 'delta_w', 'delta_mix_post_g': 'delta_w', 'delta_ffn_pre_g': 'delta_w', 'delta_w_gate': 'delta_w', 'delta_w_up': 'delta_w', 'delta_w_down': 'delta_w', 'delta_ffn_post_g': 'delta_w', 'new_m_mix_pre_g': 'new_m', 'new_m_w_in': 'new_m', 'new_m_b_forget': 'new_m', 'new_m_sgu_norm_g': 'new_m', 'new_m_w_spatial': 'new_m', 'new_m_b_spatial': 'new_m', 'new_m_w_out': 'new_m', 'new_m_mix_post_g': 'new_m', 'new_m_ffn_pre_g': 'new_m', 'new_m_w_gate': 'new_m', 'new_m_w_up': 'new_m', 'new_m_w_down': 'new_m', 'new_m_ffn_post_g': 'new_m', 'new_v_mix_pre_g': 'new_v', 'new_v_w_in': 'new_v', 'new_v_b_forget': 'new_v', 'new_v_sgu_norm_g': 'new_v', 'new_v_w_spatial': 'new_v', 'new_v_b_spatial': 'new_v', 'new_v_w_out': 'new_v', 'new_v_mix_post_g': 'new_v', 'new_v_ffn_pre_g': 'new_v', 'new_v_w_gate': 'new_v', 'new_v_w_up': 'new_v', 'new_v_w_down': 'new_v', 'new_v_ffn_post_g': 'new_v'}


def _forward(args):
    return _fwd_reference(*[args[k] for k in FWD_PARAMS])


def _output_shape():
    out = _jax.eval_shape(lambda: _forward(_fwd_setup_inputs(0)))
    return out.shape, out.dtype

N_MICROBATCH = 1
ADAM_LR = 0.001
ADAM_B1 = 0.9
ADAM_B2 = 0.999
ADAM_EPS = 1e-08
ADAM_WD = 0.01
ADAM_STEP = 10
PER_EXAMPLE_BATCH_AXIS = {'x': 0, 'loss_target': 0}
SHARED_INPUTS = []
_WEIGHT_DTYPES = {'mix_pre_g': _jnp.float32, 'w_in': _jnp.float32, 'b_forget': _jnp.float32, 'sgu_norm_g': _jnp.float32, 'w_spatial': _jnp.float32, 'b_spatial': _jnp.float32, 'w_out': _jnp.float32, 'mix_post_g': _jnp.float32, 'ffn_pre_g': _jnp.float32, 'w_gate': _jnp.float32, 'w_up': _jnp.float32, 'w_down': _jnp.float32, 'ffn_post_g': _jnp.float32}
MOMENT_SCALE = {'mix_pre_g': 1.598887e+00, 'w_in': 6.130703e-01, 'b_forget': 2.108447e+00, 'sgu_norm_g': 3.895618e-01, 'w_spatial': 3.703607e-01, 'b_spatial': 5.266255e-01, 'w_out': 4.277333e+00, 'mix_post_g': 3.241670e+01, 'ffn_pre_g': 1.732037e+00, 'w_gate': 5.152966e-01, 'w_up': 8.478830e-01, 'w_down': 1.450175e+00, 'ffn_post_g': 3.219390e+01}


def _to_microbatches(a, axis):
    t = _jnp.moveaxis(a, axis, 0)
    t = t.reshape((N_MICROBATCH, t.shape[0] // N_MICROBATCH) + t.shape[1:])
    return _jnp.moveaxis(t, 1, axis + 1)


def setup_inputs(seed: int = 0) -> dict:
    inp = _fwd_setup_inputs(seed)
    key = _jax.random.fold_in(_jax.random.key(seed), 7919)
    shape, _ = _output_shape()
    out = dict(inp)
    out["loss_target"] = _jax.random.normal(_jax.random.fold_in(key, 0), shape, _jnp.float32)
    for i, name in enumerate(TWIN_WEIGHTS):
        w = inp[name].astype(_jnp.float32)
        if MOMENT_SCALE is None:
            s = _jnp.sqrt(_jnp.mean(_jnp.square(w)) + 1e-30)
        else:
            s = MOMENT_SCALE[name]
        km, kv = _jax.random.split(_jax.random.fold_in(key, i + 1))
        out[name] = w
        out["m_" + name] = s * _jax.random.normal(km, w.shape, _jnp.float32)
        out["v_" + name] = (s * s) * _jax.random.uniform(kv, w.shape, _jnp.float32, 0.5, 1.5)
    if N_MICROBATCH > 1:
        for name, axis in PER_EXAMPLE_BATCH_AXIS.items():
            out[name] = _to_microbatches(out[name], axis)
    return {'x': out['x'], 'mix_pre_g': out['mix_pre_g'], 'w_in': out['w_in'], 'b_forget': out['b_forget'], 'sgu_norm_g': out['sgu_norm_g'], 'w_spatial': out['w_spatial'], 'b_spatial': out['b_spatial'], 'w_out': out['w_out'], 'mix_post_g': out['mix_post_g'], 'ffn_pre_g': out['ffn_pre_g'], 'w_gate': out['w_gate'], 'w_up': out['w_up'], 'w_down': out['w_down'], 'ffn_post_g': out['ffn_post_g'], 'loss_target': out['loss_target'], 'm_mix_pre_g': out['m_mix_pre_g'], 'm_w_in': out['m_w_in'], 'm_b_forget': out['m_b_forget'], 'm_sgu_norm_g': out['m_sgu_norm_g'], 'm_w_spatial': out['m_w_spatial'], 'm_b_spatial': out['m_b_spatial'], 'm_w_out': out['m_w_out'], 'm_mix_post_g': out['m_mix_post_g'], 'm_ffn_pre_g': out['m_ffn_pre_g'], 'm_w_gate': out['m_w_gate'], 'm_w_up': out['m_w_up'], 'm_w_down': out['m_w_down'], 'm_ffn_post_g': out['m_ffn_post_g'], 'v_mix_pre_g': out['v_mix_pre_g'], 'v_w_in': out['v_w_in'], 'v_b_forget': out['v_b_forget'], 'v_sgu_norm_g': out['v_sgu_norm_g'], 'v_w_spatial': out['v_w_spatial'], 'v_b_spatial': out['v_b_spatial'], 'v_w_out': out['v_w_out'], 'v_mix_post_g': out['v_mix_post_g'], 'v_ffn_pre_g': out['v_ffn_pre_g'], 'v_w_gate': out['v_w_gate'], 'v_w_up': out['v_w_up'], 'v_w_down': out['v_w_down'], 'v_ffn_post_g': out['v_ffn_post_g']}


def _loss(weights, diff, rest, loss_target):
    with _jax.named_scope("forward"):
        args = {**rest, TWIN_DIFF_INPUT: diff, **{k: w.astype(_WEIGHT_DTYPES[k]) for k, w in weights.items()}}
        y = _forward(args)
    with _jax.named_scope("loss_head"):
        err = _jnp.square(y.astype(_jnp.float32) - loss_target)
        return 0.5 * _jnp.sum(_jnp.mean(err, axis=-1)) if err.ndim else 0.5 * err


def _adamw(w, g, m, v):
    m = ADAM_B1 * m + (1.0 - ADAM_B1) * g
    v = ADAM_B2 * v + (1.0 - ADAM_B2) * _jnp.square(g)
    m_hat = m / (1.0 - ADAM_B1 ** ADAM_STEP)
    v_hat = v / (1.0 - ADAM_B2 ** ADAM_STEP)
    delta = -ADAM_LR * (m_hat / (_jnp.sqrt(v_hat) + ADAM_EPS) + ADAM_WD * w)
    return delta, m, v


def reference(x, mix_pre_g, w_in, b_forget, sgu_norm_g, w_spatial, b_spatial, w_out, mix_post_g, ffn_pre_g, w_gate, w_up, w_down, ffn_post_g, loss_target, m_mix_pre_g, m_w_in, m_b_forget, m_sgu_norm_g, m_w_spatial, m_b_spatial, m_w_out, m_mix_post_g, m_ffn_pre_g, m_w_gate, m_w_up, m_w_down, m_ffn_post_g, v_mix_pre_g, v_w_in, v_b_forget, v_sgu_norm_g, v_w_spatial, v_b_spatial, v_w_out, v_mix_post_g, v_ffn_pre_g, v_w_gate, v_w_up, v_w_down, v_ffn_post_g):
    given = dict(x=x, mix_pre_g=mix_pre_g, w_in=w_in, b_forget=b_forget, sgu_norm_g=sgu_norm_g, w_spatial=w_spatial, b_spatial=b_spatial, w_out=w_out, mix_post_g=mix_post_g, ffn_pre_g=ffn_pre_g, w_gate=w_gate, w_up=w_up, w_down=w_down, ffn_post_g=ffn_post_g, loss_target=loss_target, m_mix_pre_g=m_mix_pre_g, m_w_in=m_w_in, m_b_forget=m_b_forget, m_sgu_norm_g=m_sgu_norm_g, m_w_spatial=m_w_spatial, m_b_spatial=m_b_spatial, m_w_out=m_w_out, m_mix_post_g=m_mix_post_g, m_ffn_pre_g=m_ffn_pre_g, m_w_gate=m_w_gate, m_w_up=m_w_up, m_w_down=m_w_down, m_ffn_post_g=m_ffn_post_g, v_mix_pre_g=v_mix_pre_g, v_w_in=v_w_in, v_b_forget=v_b_forget, v_sgu_norm_g=v_sgu_norm_g, v_w_spatial=v_w_spatial, v_b_spatial=v_b_spatial, v_w_out=v_w_out, v_mix_post_g=v_mix_post_g, v_ffn_pre_g=v_ffn_pre_g, v_w_gate=v_w_gate, v_w_up=v_w_up, v_w_down=v_w_down, v_ffn_post_g=v_ffn_post_g)
    weights = {n: given[n] for n in TWIN_WEIGHTS}
    shared = {n: given[n] for n in SHARED_INPUTS}
    per_example = {n: given[n] for n in ['x']}
    grad_fn = _jax.value_and_grad(_loss, argnums=(0, 1))

    def one_microbatch(ex, loss_target):
        ex = dict(ex)
        diff = ex.pop(TWIN_DIFF_INPUT)
        return grad_fn(weights, diff, {**shared, **ex}, loss_target)

    if N_MICROBATCH == 1:
        loss, (grad_w, grad_x) = one_microbatch(per_example, given["loss_target"])
    else:
        def body(carry, xs):
            loss_sum, grad_sum = carry
            l_k, (gw_k, gx_k) = one_microbatch(xs[0], xs[1])
            with _jax.named_scope("update"):
                return (loss_sum + l_k, _jax.tree.map(_jnp.add, grad_sum, gw_k)), gx_k

        init = (_jnp.zeros((), _jnp.float32), _jax.tree.map(_jnp.zeros_like, weights))
        (loss, grad_w), grad_x = _jax.lax.scan(body, init, (per_example, given["loss_target"]))
    with _jax.named_scope("update"):
        delta_w, new_m, new_v = {}, {}, {}
        for n in TWIN_WEIGHTS:
            delta_w[n], new_m[n], new_v[n] = _adamw(weights[n], grad_w[n], given["m_" + n], given["v_" + n])
    return (loss, grad_x, *[grad_w[n] for n in TWIN_WEIGHTS], *[delta_w[n] for n in TWIN_WEIGHTS],
            *[new_m[n] for n in TWIN_WEIGHTS], *[new_v[n] for n in TWIN_WEIGHTS])
```

```python
import math

import jax
import jax.numpy as jnp
from jax import lax
from jax.experimental import pallas as pl
from jax.experimental.pallas import tpu as pltpu

F32 = jnp.float32
BF16 = jnp.bfloat16

N_DEV = 8
D_MODEL = 1024
N_HEADS = 8
HEAD_DIM = 128
CHUNK = 128
N_GROUPS = 8
D_FF = 2816
IN_WIDTH = 7 * D_MODEL + N_HEADS
IN_PAD = 7680
EPS = 1e-6
CB_U, CB_VS, CB_Q, CB_K, CB_V, CB_GA, CB_GB, CB_F = 0, 8, 16, 24, 32, 40, 48, 56

ADAM_LR, ADAM_B1, ADAM_B2, ADAM_EPS, ADAM_WD, ADAM_STEP = 0.001, 0.9, 0.999, 1e-08, 0.01, 10

SH_IN = IN_WIDTH // N_DEV
SH_OUT = D_MODEL // N_DEV
SH_FF = D_FF // N_DEV
R_IN, R_OUT, R_G, R_U, R_D = 0, SH_IN, SH_IN + SH_OUT, SH_IN + SH_OUT + SH_FF, SH_IN + SH_OUT + 2 * SH_FF
R_USED = R_D + SH_FF
R_PACK = 2112
S_ROWS = 136

ATTN_BLOCK = 512
ROW_BLOCK = 512
MM_TM, MM_TN, MM_TK = 1536, 1536, 2048
VMEM_CAP = 56 << 20

NEG = -1e30


def _tile(n, cap, unit=128):
    if n <= cap:
        return n
    best = None
    for t in range(unit, cap + 1, unit):
        if n % t == 0:
            best = t
    assert best is not None, (n, cap)
    return best


def _params(sem, nbytes):
    limit = int(min(max(2 * nbytes + (8 << 20), 32 << 20), VMEM_CAP))
    return pltpu.CompilerParams(dimension_semantics=sem, vmem_limit_bytes=limit)


def _nbytes(shape, dtype):
    return math.prod(shape) * jnp.dtype(dtype).itemsize


def _mesh_pos():
    return lax.axis_index("x"), lax.axis_index("y"), lax.axis_index("c")


def _peer(pos, j):
    x, y, c = pos
    return (1 - x if j & 4 else x, 1 - y if j & 2 else y, 1 - c if j & 1 else c)


def _index(pos):
    return 4 * pos[0] + 2 * pos[1] + pos[2]


_HBM = pl.BlockSpec(memory_space=pltpu.HBM)


def _all_gather(x, name):
    def body(x_ref, out_ref, send_sems, recv_sems, local_sem):
        pos = _mesh_pos()
        me = _index(pos)
        mine = pltpu.make_async_copy(x_ref, out_ref.at[me], local_sem)
        mine.start()
        sends, recvs = [], []
        for j in range(1, N_DEV):
            peer = _peer(pos, j)
            sends.append(pltpu.make_async_remote_copy(
                src_ref=x_ref, dst_ref=out_ref.at[me], send_sem=send_sems.at[j - 1], recv_sem=recv_sems.at[j - 1],
                device_id=peer, device_id_type=pl.DeviceIdType.MESH))
            recvs.append(pltpu.make_async_remote_copy(
                src_ref=x_ref, dst_ref=out_ref.at[_index(peer)], send_sem=send_sems.at[j - 1],
                recv_sem=recv_sems.at[j - 1], device_id=peer, device_id_type=pl.DeviceIdType.MESH))
        for cp in sends:
            cp.start()
        for cp in recvs:
            cp.wait_recv()
        for cp in sends:
            cp.wait_send()
        mine.wait()

    return pl.pallas_call(
        body, name=name,
        out_shape=jax.ShapeDtypeStruct((N_DEV,) + x.shape, x.dtype),
        in_specs=[_HBM], out_specs=_HBM,
        scratch_shapes=[pltpu.SemaphoreType.DMA((N_DEV - 1,)), pltpu.SemaphoreType.DMA((N_DEV - 1,)),
                        pltpu.SemaphoreType.DMA],
    )(x)


def _all_to_all(x, name):
    def body(x_ref, out_ref, send_sems, recv_sems, local_sem):
        pos = _mesh_pos()
        me = _index(pos)
        mine = pltpu.make_async_copy(x_ref.at[me], out_ref.at[me], local_sem)
        mine.start()
        sends, recvs = [], []
        for j in range(1, N_DEV):
            peer = _peer(pos, j)
            them = _index(peer)
            sends.append(pltpu.make_async_remote_copy(
                src_ref=x_ref.at[them], dst_ref=out_ref.at[me], send_sem=send_sems.at[j - 1],
                recv_sem=recv_sems.at[j - 1], device_id=peer, device_id_type=pl.DeviceIdType.MESH))
            recvs.append(pltpu.make_async_remote_copy(
                src_ref=x_ref.at[them], dst_ref=out_ref.at[them], send_sem=send_sems.at[j - 1],
                recv_sem=recv_sems.at[j - 1], device_id=peer, device_id_type=pl.DeviceIdType.MESH))
        for cp in sends:
            cp.start()
        for cp in recvs:
            cp.wait_recv()
        for cp in sends:
            cp.wait_send()
        mine.wait()

    return pl.pallas_call(
        body, name=name,
        out_shape=jax.ShapeDtypeStruct(x.shape, x.dtype),
        in_specs=[_HBM], out_specs=_HBM,
        scratch_shapes=[pltpu.SemaphoreType.DMA((N_DEV - 1,)), pltpu.SemaphoreType.DMA((N_DEV - 1,)),
                        pltpu.SemaphoreType.DMA],
    )(x)


def _sum_slots(x, tr, name):
    _, r, c = x.shape

    def body(x_ref, o_ref):
        acc = x_ref[0].astype(F32)
        for d in range(1, N_DEV):
            acc = acc + x_ref[d].astype(F32)
        o_ref[...] = acc

    return pl.pallas_call(
        body, name=name, grid=(r // tr,),
        out_shape=jax.ShapeDtypeStruct((r, c), F32),
        in_specs=[pl.BlockSpec((N_DEV, tr, c), lambda i: (0, i, 0))],
        out_specs=pl.BlockSpec((tr, c), lambda i: (i, 0)),
        compiler_params=_params(("parallel",), _nbytes((N_DEV, tr, c), x.dtype) + _nbytes((tr, c), F32)),
    )(x)


_DIMS = {"nn": (((1,), (0,)), ((), ())), "nt": (((1,), (1,)), ((), ())), "tn": (((0,), (0,)), ((), ()))}


def _matmul(a, b, mode, out_dtype, name):
    if mode == "nn":
        (m, k), (k2, n) = a.shape, b.shape
    elif mode == "nt":
        (m, k), (n, k2) = a.shape, b.shape
    else:
        (k, m), (k2, n) = a.shape, b.shape
    assert k == k2, (a.shape, b.shape, mode)
    tm, tn, tk = _tile(m, MM_TM), _tile(n, MM_TN), _tile(k, MM_TK)
    nk = k // tk
    dims = _DIMS[mode]
    a_spec = {"nn": pl.BlockSpec((tm, tk), lambda i, j, kk: (i, kk)),
              "nt": pl.BlockSpec((tm, tk), lambda i, j, kk: (i, kk)),
              "tn": pl.BlockSpec((tk, tm), lambda i, j, kk: (kk, i))}[mode]
    b_spec = {"nn": pl.BlockSpec((tk, tn), lambda i, j, kk: (kk, j)),
              "nt": pl.BlockSpec((tn, tk), lambda i, j, kk: (j, kk)),
              "tn": pl.BlockSpec((tk, tn), lambda i, j, kk: (kk, j))}[mode]

    def partial_product(a_ref, b_ref):
        return lax.dot_general(a_ref[...].astype(BF16), b_ref[...].astype(BF16), dims, preferred_element_type=F32)

    if nk == 1:
        def body(a_ref, b_ref, o_ref):
            o_ref[...] = partial_product(a_ref, b_ref).astype(o_ref.dtype)
        scratch = []
    else:
        def body(a_ref, b_ref, o_ref, acc_ref):
            kk = pl.program_id(2)

            @pl.when(kk == 0)
            def _():
                acc_ref[...] = jnp.zeros_like(acc_ref)

            acc_ref[...] += partial_product(a_ref, b_ref)

            @pl.when(kk == nk - 1)
            def _():
                o_ref[...] = acc_ref[...].astype(o_ref.dtype)
        scratch = [pltpu.VMEM((tm, tn), F32)]

    nbytes = (_nbytes((tm, tk), a.dtype) + _nbytes((tk, tn), b.dtype) + _nbytes((tm, tn), out_dtype)
              + _nbytes((tm, tn), F32))
    return pl.pallas_call(
        body, name=name, grid=(m // tm, n // tn, nk),
        out_shape=jax.ShapeDtypeStruct((m, n), out_dtype),
        in_specs=[a_spec, b_spec], out_specs=pl.BlockSpec((tm, tn), lambda i, j, kk: (i, j)),
        scratch_shapes=scratch,
        compiler_params=_params(("parallel", "parallel", "arbitrary"), nbytes),
    )(a, b)


def _rms(x):
    return lax.rsqrt(jnp.mean(x * x, axis=-1, keepdims=True) + EPS)


def _rms_bwd(dz, a, g):
    r = _rms(a)
    dzg = dz * g
    da = r * dzg - a * (r * r * r) * jnp.mean(dzg * a, axis=-1, keepdims=True)
    return da, dz * (a * r)


def _row_spec(tr, width):
    return pl.BlockSpec((tr, width), lambda i: (i, 0))


def _vec_spec(width):
    return pl.BlockSpec((1, width), lambda i: (0, 0))


def _rms_fwd(x, g, name):
    t, d = x.shape
    tr = min(ROW_BLOCK, t)

    def body(x_ref, g_ref, h_ref):
        xv = x_ref[...]
        h_ref[...] = ((xv * _rms(xv)) * g_ref[...]).astype(BF16)

    return pl.pallas_call(
        body, name=name, grid=(t // tr,),
        out_shape=jax.ShapeDtypeStruct((t, d), BF16),
        in_specs=[_row_spec(tr, d), _vec_spec(d)], out_specs=_row_spec(tr, d),
        compiler_params=_params(("parallel",), 3 * _nbytes((tr, d), F32)),
    )(x, g)


def _post_norm_fwd(x, o, g_post, g_next, name):
    t, d = x.shape
    tr = min(ROW_BLOCK, t)

    def body(x_ref, o_ref, gp_ref, gn_ref, xn_ref, h_ref):
        ov = o_ref[...]
        xn = x_ref[...] + (ov * _rms(ov)) * gp_ref[...]
        xn_ref[...] = xn
        h_ref[...] = ((xn * _rms(xn)) * gn_ref[...]).astype(BF16)

    return pl.pallas_call(
        body, name=name, grid=(t // tr,),
        out_shape=(jax.ShapeDtypeStruct((t, d), F32), jax.ShapeDtypeStruct((t, d), BF16)),
        in_specs=[_row_spec(tr, d), _row_spec(tr, d), _vec_spec(d), _vec_spec(d)],
        out_specs=(_row_spec(tr, d), _row_spec(tr, d)),
        compiler_params=_params(("parallel",), 5 * _nbytes((tr, d), F32)),
    )(x, o, g_post, g_next)


def _post_norm_loss(x, o, g_post, target, name):
    t, d = x.shape
    tr = min(ROW_BLOCK, t)

    def body(x_ref, o_ref, gp_ref, t_ref, dy_ref, loss_ref):
        ov = o_ref[...]
        err = x_ref[...] + (ov * _rms(ov)) * gp_ref[...] - t_ref[...]
        dy_ref[...] = err / d
        part = 0.5 * jnp.sum(jnp.mean(err * err, axis=-1, keepdims=True), axis=0, keepdims=True)

        @pl.when(pl.program_id(0) == 0)
        def _():
            loss_ref[...] = jnp.zeros_like(loss_ref)

        loss_ref[...] += part

    return pl.pallas_call(
        body, name=name, grid=(t // tr,),
        out_shape=(jax.ShapeDtypeStruct((t, d), F32), jax.ShapeDtypeStruct((1, 1), F32)),
        in_specs=[_row_spec(tr, d), _row_spec(tr, d), _vec_spec(d), _row_spec(tr, d)],
        out_specs=(_row_spec(tr, d), pl.BlockSpec((1, 1), lambda i: (0, 0))),
        compiler_params=_params(("arbitrary",), 5 * _nbytes((tr, d), F32)),
    )(x, o, g_post, target)


def _norm_bwd(dskip, name, pre=None, post=None):
    t, d = dskip.shape
    tr = min(ROW_BLOCK, t)
    n_in = 1 + (3 if pre else 0) + (2 if post else 0)

    def body(*refs):
        ins, outs = list(refs[:n_in]), list(refs[n_in:])
        first = pl.program_id(0) == 0
        dx = ins.pop(0)[...]
        if pre:
            dh_ref, xin_ref, gpre_ref = ins.pop(0), ins.pop(0), ins.pop(0)
            dxin, dg_rows = _rms_bwd(dh_ref[...].astype(F32), xin_ref[...], gpre_ref[...])
            dx = dx + dxin
            dx_ref, dgpre_ref = outs.pop(0), outs.pop(0)
            dx_ref[...] = dx

            @pl.when(first)
            def _():
                dgpre_ref[...] = jnp.zeros_like(dgpre_ref)

            dgpre_ref[...] += jnp.sum(dg_rows, axis=0, keepdims=True)
        if post:
            a_ref, gpost_ref = ins.pop(0), ins.pop(0)
            da, dg_rows = _rms_bwd(dx, a_ref[...], gpost_ref[...])
            da_ref, dgpost_ref = outs.pop(0), outs.pop(0)
            da_ref[...] = da.astype(BF16)

            @pl.when(first)
            def _():
                dgpost_ref[...] = jnp.zeros_like(dgpost_ref)

            dgpost_ref[...] += jnp.sum(dg_rows, axis=0, keepdims=True)

    args, in_specs, out_shape, out_specs = [dskip], [_row_spec(tr, d)], [], []
    if pre:
        args += list(pre)
        in_specs += [_row_spec(tr, d), _row_spec(tr, d), _vec_spec(d)]
        out_shape += [jax.ShapeDtypeStruct((t, d), F32), jax.ShapeDtypeStruct((1, d), F32)]
        out_specs += [_row_spec(tr, d), _vec_spec(d)]
    if post:
        args += list(post)
        in_specs += [_row_spec(tr, d), _vec_spec(d)]
        out_shape += [jax.ShapeDtypeStruct((t, d), BF16), jax.ShapeDtypeStruct((1, d), F32)]
        out_specs += [_row_spec(tr, d), _vec_spec(d)]
    return pl.pallas_call(
        body, name=name, grid=(t // tr,),
        out_shape=tuple(out_shape), in_specs=in_specs, out_specs=tuple(out_specs),
        compiler_params=_params(("arbitrary",), 7 * _nbytes((tr, d), F32)),
    )(*args)


def _sigmoid(x):
    return 1.0 / (1.0 + jnp.exp(-x))


def _swiglu_fwd(gu, name):
    t = gu.shape[0]
    tr = min(ROW_BLOCK, t)

    def body(gu_ref, a_ref):
        gate, up = gu_ref[:, :D_FF], gu_ref[:, D_FF:]
        a_ref[...] = ((gate * _sigmoid(gate)) * up).astype(BF16)

    return pl.pallas_call(
        body, name=name, grid=(t // tr,),
        out_shape=jax.ShapeDtypeStruct((t, D_FF), BF16),
        in_specs=[_row_spec(tr, 2 * D_FF)], out_specs=_row_spec(tr, D_FF),
        compiler_params=_params(("parallel",), 2 * _nbytes((tr, 2 * D_FF), F32)),
    )(gu)


def _swiglu_bwd(gu, dact, name):
    t = gu.shape[0]
    tr = min(ROW_BLOCK, t)

    def body(gu_ref, da_ref, o_ref):
        gate, up = gu_ref[:, :D_FF], gu_ref[:, D_FF:]
        da = da_ref[...]
        s = _sigmoid(gate)
        o_ref[:, :D_FF] = (da * up * (s * (1.0 + gate * (1.0 - s)))).astype(BF16)
        o_ref[:, D_FF:] = (da * (gate * s)).astype(BF16)

    return pl.pallas_call(
        body, name=name, grid=(t // tr,),
        out_shape=jax.ShapeDtypeStruct((t, 2 * D_FF), BF16),
        in_specs=[_row_spec(tr, 2 * D_FF), _row_spec(tr, D_FF)], out_specs=_row_spec(tr, 2 * D_FF),
        compiler_params=_params(("parallel",), 3 * _nbytes((tr, 2 * D_FF), F32)),
    )(gu, dact)


def _split3(x):
    hi = x.astype(BF16)
    r1 = x - hi.astype(F32)
    mid = r1.astype(BF16)
    lo = (r1 - mid.astype(F32)).astype(BF16)
    return hi, mid, lo


def _dot_exact(mat01, x, dims=_DIMS["nn"]):
    hi, mid, lo = _split3(x)
    out = lax.dot_general(mat01, hi, dims, preferred_element_type=F32)
    out = out + lax.dot_general(mat01, mid, dims, preferred_element_type=F32)
    return out + lax.dot_general(mat01, lo, dims, preferred_element_type=F32)


def _dot_exact_rhs(x, mat01):
    hi, mid, lo = _split3(x)
    out = jnp.dot(hi, mat01, preferred_element_type=F32)
    out = out + jnp.dot(mid, mat01, preferred_element_type=F32)
    return out + jnp.dot(lo, mat01, preferred_element_type=F32)


def _tri(lower):
    r = lax.broadcasted_iota(jnp.int32, (CHUNK, CHUNK), 0)
    c = lax.broadcasted_iota(jnp.int32, (CHUNK, CHUNK), 1)
    return jnp.where(r >= c if lower else r <= c, 1.0, 0.0).astype(BF16)


def _log_sigmoid(z):
    return jnp.minimum(z, 0.0) - jnp.log(1.0 + jnp.exp(-jnp.abs(z)))


def _forget_cumsum(proj, b_pad, name):
    t = proj.shape[0]
    nb = t // CHUNK

    def body(f_ref, b_ref, c_ref):
        tri = _tri(True)
        b = b_ref[...]

        def blk(i, carry):
            rows = pl.ds(pl.multiple_of(i * CHUNK, CHUNK), CHUNK)
            cs = _dot_exact(tri, _log_sigmoid(f_ref[rows, :] + b)) + carry
            c_ref[rows, :] = cs
            return cs[CHUNK - 1:CHUNK, :]

        lax.fori_loop(0, nb, blk, jnp.zeros((1, HEAD_DIM), F32))

    return pl.pallas_call(
        body, name=name, grid=(1,),
        out_shape=jax.ShapeDtypeStruct((t, HEAD_DIM), F32),
        in_specs=[pl.BlockSpec((t, HEAD_DIM), lambda i: (0, CB_F)), pl.BlockSpec((1, HEAD_DIM), lambda i: (0, 0))],
        out_specs=pl.BlockSpec((t, HEAD_DIM), lambda i: (0, 0)),
        compiler_params=_params(("arbitrary",), 2 * _nbytes((t, HEAD_DIM), F32)),
    )(proj, b_pad)


def _forget_bwd(proj, b_pad, dc_pad, name):
    t = proj.shape[0]
    nb = t // CHUNK
    width = IN_PAD - CB_F * HEAD_DIM

    def body(f_ref, b_ref, dc_ref, df_ref, db_ref):
        tri = _tri(False)
        b = b_ref[...]
        valid = lax.broadcasted_iota(jnp.int32, (CHUNK, HEAD_DIM), 1) < N_HEADS
        df_ref[...] = jnp.zeros_like(df_ref)

        def blk(n, carry):
            run, db = carry
            rows = pl.ds(pl.multiple_of((nb - 1 - n) * CHUNK, CHUNK), CHUNK)
            dlogf = _dot_exact(tri, dc_ref[rows, :]) + run
            z = f_ref[rows, :] + b
            e = jnp.exp(-jnp.abs(z))
            sig_neg = jnp.where(z >= 0.0, e, 1.0) / (1.0 + e)
            df = jnp.where(valid, dlogf * sig_neg, 0.0)
            df_ref[rows, :HEAD_DIM] = df.astype(BF16)
            return dlogf[0:1, :], db + jnp.sum(df, axis=0, keepdims=True)

        zero = jnp.zeros((1, HEAD_DIM), F32)
        _, db = lax.fori_loop(0, nb, blk, (zero, zero))
        db_ref[...] = db

    return pl.pallas_call(
        body, name=name, grid=(1,),
        out_shape=(jax.ShapeDtypeStruct((t, width), BF16), jax.ShapeDtypeStruct((1, HEAD_DIM), F32)),
        in_specs=[pl.BlockSpec((t, HEAD_DIM), lambda i: (0, CB_F)), pl.BlockSpec((1, HEAD_DIM), lambda i: (0, 0)),
                  pl.BlockSpec((t, HEAD_DIM), lambda i: (0, 0))],
        out_specs=(pl.BlockSpec((t, width), lambda i: (0, 0)), pl.BlockSpec((1, HEAD_DIM), lambda i: (0, 0))),
        compiler_params=_params(("arbitrary",), 4 * _nbytes((t, HEAD_DIM), F32)),
    )(proj, b_pad, dc_pad)


def _causal(block):
    r = lax.broadcasted_iota(jnp.int32, (block, block), 0)
    c = lax.broadcasted_iota(jnp.int32, (block, block), 1)
    return c <= r


def _attn_fwd(proj, c_col, c_row, name):
    t = proj.shape[0]
    blk = min(ATTN_BLOCK, t)
    nq = t // blk
    scale = HEAD_DIM ** -0.5

    def body(q_ref, k_ref, v_ref, cc_ref, cr_ref, y_ref, e_ref):
        i = pl.program_id(1)
        q = (q_ref[...] * scale).astype(BF16)
        ci = cc_ref[...]

        def step(j, carry, diagonal):
            m, l, acc = carry
            off = pl.multiple_of(j * blk, blk)
            k = k_ref[pl.ds(off, blk), :].astype(BF16)
            v = v_ref[pl.ds(off, blk), :].astype(BF16)
            s = lax.dot_general(q, k, _DIMS["nt"], preferred_element_type=F32)
            s = s + ci - cr_ref[:, pl.ds(off, blk)]
            if diagonal:
                s = jnp.where(_causal(blk), s, NEG)
            m_new = jnp.maximum(m, jnp.max(s, axis=1, keepdims=True))
            alpha = jnp.exp(m - m_new)
            p = jnp.exp(s - m_new)
            l = alpha * l + jnp.sum(p, axis=1, keepdims=True)
            acc = alpha * acc + jnp.dot(p.astype(BF16), v, preferred_element_type=F32)
            return m_new, l, acc

        init = (jnp.full((blk, 1), NEG, F32), jnp.zeros((blk, 1), F32), jnp.zeros((blk, HEAD_DIM), F32))
        carry = lax.fori_loop(0, i, lambda j, c: step(j, c, False), init)
        m, l, acc = step(i, carry, True)
        y_ref[...] = acc / l
        e_ref[...] = ci - (m + jnp.log(l))

    nbytes = 2 * _nbytes((t, HEAD_DIM), F32) + 3 * _nbytes((blk, HEAD_DIM), F32) + 4 * _nbytes((blk, blk), F32)
    return pl.pallas_call(
        body, name=name, grid=(N_HEADS, nq),
        out_shape=(jax.ShapeDtypeStruct((t, D_MODEL), F32), jax.ShapeDtypeStruct((N_HEADS, t, 1), F32)),
        in_specs=[pl.BlockSpec((blk, HEAD_DIM), lambda h, i: (i, CB_Q + h)),
                  pl.BlockSpec((t, HEAD_DIM), lambda h, i: (0, CB_K + h)),
                  pl.BlockSpec((t, HEAD_DIM), lambda h, i: (0, CB_V + h)),
                  pl.BlockSpec((None, blk, 1), lambda h, i: (h, i, 0)),
                  pl.BlockSpec((None, 1, t), lambda h, i: (h, 0, 0))],
        out_specs=(pl.BlockSpec((blk, HEAD_DIM), lambda h, i: (i, h)),
                   pl.BlockSpec((None, blk, 1), lambda h, i: (h, i, 0))),
        compiler_params=_params(("parallel", "arbitrary"), nbytes),
    )(proj, proj, proj, c_col, c_row)


def _attn_bwd(proj, dy, e_col, delta_col, c_row, name):
    t = proj.shape[0]
    blk = min(ATTN_BLOCK, t)
    nb = t // blk
    scale = HEAD_DIM ** -0.5

    def body(q_ref, k_ref, v_ref, dy_ref, e_ref, dl_ref, cr_ref, dq_ref, dk_ref, dv_ref, dc_ref, dr_ref, dq_acc):
        j = pl.program_id(1)

        @pl.when(j == 0)
        def _():
            dq_acc[...] = jnp.zeros_like(dq_acc)
            dr_ref[...] = jnp.zeros_like(dr_ref)

        k = k_ref[...].astype(BF16)
        v = v_ref[...].astype(BF16)
        crow = cr_ref[...]

        def step(i, carry, diagonal):
            dk, dv, dcs = carry
            rows = pl.ds(pl.multiple_of(i * blk, blk), blk)
            q = (q_ref[rows, :] * scale).astype(BF16)
            dyi = dy_ref[rows, :].astype(BF16)
            s = lax.dot_general(q, k, _DIMS["nt"], preferred_element_type=F32) + e_ref[rows, :] - crow
            if diagonal:
                s = jnp.where(_causal(blk), s, NEG)
            p = jnp.exp(s)
            dv = dv + lax.dot_general(p.astype(BF16), dyi, _DIMS["tn"], preferred_element_type=F32)
            dp = lax.dot_general(dyi, v, _DIMS["nt"], preferred_element_type=F32)
            ds = p * (dp - dl_ref[rows, :])
            dsb = ds.astype(BF16)
            dk = dk + lax.dot_general(dsb, q, _DIMS["tn"], preferred_element_type=F32)
            dq_acc[rows, :] += jnp.dot(dsb, k, preferred_element_type=F32) * scale
            dr_ref[rows, :] += jnp.sum(ds, axis=1, keepdims=True)
            return dk, dv, dcs + jnp.sum(ds, axis=0, keepdims=True)

        zeros = jnp.zeros((blk, HEAD_DIM), F32)
        carry = step(j, (zeros, zeros, jnp.zeros((1, blk), F32)), True)
        dk, dv, dcs = lax.fori_loop(j + 1, nb, lambda i, c: step(i, c, False), carry)
        dk_ref[...] = dk.astype(BF16)
        dv_ref[...] = dv.astype(BF16)
        dc_ref[...] = -dcs

        @pl.when(j == nb - 1)
        def _():
            dq_ref[...] = dq_acc[...].astype(BF16)

    full = lambda cb: pl.BlockSpec((t, HEAD_DIM), lambda h, j: (0, cb + h))
    col = pl.BlockSpec((None, t, 1), lambda h, j: (h, 0, 0))
    nbytes = (3 * _nbytes((t, HEAD_DIM), F32) + 2 * _nbytes((t, HEAD_DIM), F32) + 4 * _nbytes((blk, HEAD_DIM), F32)
              + 5 * _nbytes((blk, blk), F32))
    return pl.pallas_call(
        body, name=name, grid=(N_HEADS, nb),
        out_shape=(jax.ShapeDtypeStruct((t, D_MODEL), BF16), jax.ShapeDtypeStruct((t, D_MODEL), BF16),
                   jax.ShapeDtypeStruct((t, D_MODEL), BF16), jax.ShapeDtypeStruct((N_HEADS, 1, t), F32),
                   jax.ShapeDtypeStruct((N_HEADS, t, 1), F32)),
        in_specs=[full(CB_Q),
                  pl.BlockSpec((blk, HEAD_DIM), lambda h, j: (j, CB_K + h)),
                  pl.BlockSpec((blk, HEAD_DIM), lambda h, j: (j, CB_V + h)),
                  full(0), col, col,
                  pl.BlockSpec((None, 1, blk), lambda h, j: (h, 0, j))],
        out_specs=(full(0),
                   pl.BlockSpec((blk, HEAD_DIM), lambda h, j: (j, h)),
                   pl.BlockSpec((blk, HEAD_DIM), lambda h, j: (j, h)),
                   pl.BlockSpec((None, 1, blk), lambda h, j: (h, 0, j)), col),
        scratch_shapes=[pltpu.VMEM((t, HEAD_DIM), F32)],
        compiler_params=_params(("parallel", "arbitrary"), nbytes),
    )(proj, proj, proj, dy, e_col, delta_col, c_row)


_GELU_K = math.sqrt(2.0 / math.pi)
_GELU_C = 0.044715


def _gelu(x):
    t = jnp.tanh(_GELU_K * (x + _GELU_C * (x * x * x)))
    return 0.5 * x * (1.0 + t), t


def _gelu_grad(x, t):
    return 0.5 * (1.0 + t) + 0.5 * x * (1.0 - t * t) * (_GELU_K * (1.0 + 3.0 * _GELU_C * (x * x)))


def _layernorm_stats(a):
    mu = jnp.mean(a, axis=-1, keepdims=True)
    xc = a - mu
    r = lax.rsqrt(jnp.mean(xc * xc, axis=-1, keepdims=True) + EPS)
    return xc * r, r


def _group(g):
    return slice(g * CHUNK, (g + 1) * CHUNK)


def _mix_specs(t):
    chunk_cols = lambda cb: pl.BlockSpec((CHUNK, D_MODEL), lambda i: (i, cb // N_GROUPS))
    fixed = lambda shape: pl.BlockSpec(shape, lambda i: (0,) * len(shape))
    return chunk_cols, fixed


def _mix_fwd(proj, y_b, w_s, b_cols, g_v, name):
    t = proj.shape[0]
    chunk_cols, fixed = _mix_specs(t)

    def body(u_ref, vs_ref, ga_ref, gb_ref, yb_ref, w_ref, b_ref, gv_ref, o_ref):
        a_u, _ = _gelu(u_ref[...])
        a_v, _ = _gelu(vs_ref[...])
        xhat, _ = _layernorm_stats(a_v)
        vn = (xhat * gv_ref[...]).astype(BF16)
        sa, sb = _sigmoid(ga_ref[...]), _sigmoid(gb_ref[...])
        yb = yb_ref[...]
        mask = _causal(CHUNK)
        for g in range(N_GROUPS):
            cols = _group(g)
            w = jnp.where(mask, w_ref[g], 0.0).astype(BF16)
            mixed = jnp.dot(w, vn[:, cols], preferred_element_type=F32) + b_ref[:, g:g + 1]
            o_ref[:, cols] = (sa[:, cols] * (a_u[:, cols] * mixed) + sb[:, cols] * yb[:, cols]).astype(BF16)

    return pl.pallas_call(
        body, name=name, grid=(t // CHUNK,),
        out_shape=jax.ShapeDtypeStruct((t, D_MODEL), BF16),
        in_specs=[chunk_cols(CB_U), chunk_cols(CB_VS), chunk_cols(CB_GA), chunk_cols(CB_GB),
                  pl.BlockSpec((CHUNK, D_MODEL), lambda i: (i, 0)),
                  fixed((N_GROUPS, CHUNK, CHUNK)), fixed((CHUNK, N_GROUPS)), fixed((1, D_MODEL))],
        out_specs=pl.BlockSpec((CHUNK, D_MODEL), lambda i: (i, 0)),
        compiler_params=_params(("parallel",), 16 * _nbytes((CHUNK, D_MODEL), F32)),
    )(proj, proj, proj, proj, y_b, w_s, b_cols, g_v)


def _mix_bwd(proj, y_b, dmerged, w_s, b_cols, g_v, sel, name):
    t = proj.shape[0]
    chunk_cols, fixed = _mix_specs(t)

    def body(u_ref, vs_ref, ga_ref, gb_ref, yb_ref, dm_ref, w_ref, b_ref, gv_ref, sel_ref,
             duv_ref, dgab_ref, dyb_ref, delta_ref, dw_ref, dbt_ref, dgv_ref):
        @pl.when(pl.program_id(0) == 0)
        def _():
            dw_ref[...] = jnp.zeros_like(dw_ref)
            dbt_ref[...] = jnp.zeros_like(dbt_ref)
            dgv_ref[...] = jnp.zeros_like(dgv_ref)

        u, vs = u_ref[...], vs_ref[...]
        a_u, t_u = _gelu(u)
        a_v, t_v = _gelu(vs)
        xhat, r = _layernorm_stats(a_v)
        gv = gv_ref[...]
        vn = (xhat * gv).astype(BF16)
        sa, sb = _sigmoid(ga_ref[...]), _sigmoid(gb_ref[...])
        yb, dm = yb_ref[...], dm_ref[...].astype(F32)
        dyb = dm * sb
        dyb_ref[...] = dyb
        delta_ref[...] = _dot_exact_rhs(dyb * yb, sel_ref[...])
        dgab_ref[:, D_MODEL:] = (dm * yb * (sb * (1.0 - sb))).astype(BF16)
        dya = dm * sa
        mask = _causal(CHUNK)
        dmixed_parts, dvn_parts = [], []
        for g in range(N_GROUPS):
            cols = _group(g)
            w = jnp.where(mask, w_ref[g], 0.0).astype(BF16)
            mixed = jnp.dot(w, vn[:, cols], preferred_element_type=F32) + b_ref[:, g:g + 1]
            y_a = a_u[:, cols] * mixed
            dgab_ref[:, cols] = (dm[:, cols] * y_a * (sa[:, cols] * (1.0 - sa[:, cols]))).astype(BF16)
            duv_ref[:, cols] = (dya[:, cols] * mixed * _gelu_grad(u[:, cols], t_u[:, cols])).astype(BF16)
            dmixed = dya[:, cols] * a_u[:, cols]
            dmb = dmixed.astype(BF16)
            dw = lax.dot_general(dmb, vn[:, cols], _DIMS["nt"], preferred_element_type=F32)
            dw_ref[g] += jnp.where(mask, dw, 0.0)
            dvn_parts.append(lax.dot_general(w, dmb, _DIMS["tn"], preferred_element_type=F32))
            dmixed_parts.append(dmixed)
        dmixed_all = jnp.concatenate(dmixed_parts, axis=1)
        dvn = jnp.concatenate(dvn_parts, axis=1)
        dbt_ref[...] += _dot_exact_rhs(dmixed_all, sel_ref[...])
        dgv_ref[...] += jnp.sum(dvn * xhat, axis=0, keepdims=True)
        dxh = dvn * gv
        da_v = r * (dxh - jnp.mean(dxh, axis=-1, keepdims=True)
                    - xhat * jnp.mean(dxh * xhat, axis=-1, keepdims=True))
        duv_ref[:, D_MODEL:] = (da_v * _gelu_grad(vs, t_v)).astype(BF16)

    row = lambda width: pl.BlockSpec((CHUNK, width), lambda i: (i, 0))
    return pl.pallas_call(
        body, name=name, grid=(t // CHUNK,),
        out_shape=(jax.ShapeDtypeStruct((t, 2 * D_MODEL), BF16), jax.ShapeDtypeStruct((t, 2 * D_MODEL), BF16),
                   jax.ShapeDtypeStruct((t, D_MODEL), F32), jax.ShapeDtypeStruct((t, HEAD_DIM), F32),
                   jax.ShapeDtypeStruct((N_GROUPS, CHUNK, CHUNK), F32), jax.ShapeDtypeStruct((CHUNK, HEAD_DIM), F32),
                   jax.ShapeDtypeStruct((1, D_MODEL), F32)),
        in_specs=[chunk_cols(CB_U), chunk_cols(CB_VS), chunk_cols(CB_GA), chunk_cols(CB_GB),
                  row(D_MODEL), row(D_MODEL),
                  fixed((N_GROUPS, CHUNK, CHUNK)), fixed((CHUNK, N_GROUPS)), fixed((1, D_MODEL)),
                  fixed((D_MODEL, HEAD_DIM))],
        out_specs=(row(2 * D_MODEL), row(2 * D_MODEL), row(D_MODEL), row(HEAD_DIM),
                   fixed((N_GROUPS, CHUNK, CHUNK)), fixed((CHUNK, HEAD_DIM)), fixed((1, D_MODEL))),
        compiler_params=_params(("arbitrary",), 32 * _nbytes((CHUNK, D_MODEL), F32)),
    )(proj, proj, proj, proj, y_b, dmerged, w_s, b_cols, g_v, sel)


def _adamw(w, g, m, v, name):
    r, c = w.shape
    tr = _tile(r, 256, unit=8)

    def body(w_ref, g_ref, m_ref, v_ref, d_ref, nm_ref, nv_ref):
        gv = g_ref[...]
        nm = ADAM_B1 * m_ref[...] + (1.0 - ADAM_B1) * gv
        nv = ADAM_B2 * v_ref[...] + (1.0 - ADAM_B2) * (gv * gv)
        m_hat = nm / (1.0 - ADAM_B1 ** ADAM_STEP)
        v_hat = nv / (1.0 - ADAM_B2 ** ADAM_STEP)
        d_ref[...] = -ADAM_LR * (m_hat / (jnp.sqrt(v_hat) + ADAM_EPS) + ADAM_WD * w_ref[...])
        nm_ref[...] = nm
        nv_ref[...] = nv

    spec = pl.BlockSpec((tr, c), lambda i: (i, 0))
    shape = jax.ShapeDtypeStruct((r, c), F32)
    return pl.pallas_call(
        body, name=name, grid=(r // tr,),
        out_shape=(shape, shape, shape), in_specs=[spec] * 4, out_specs=(spec,) * 3,
        compiler_params=_params(("parallel",), 7 * _nbytes((tr, max(c, 128)), F32)),
    )(w, g, m, v)


def _pad_rows(a, rows):
    return jnp.pad(a, ((0, rows - a.shape[0]), (0, 0)))


def _pack_weights(w_in, w_out, w_gate, w_up, w_down):
    rows = jnp.concatenate([w_in.T, w_out, w_gate.T, w_up.T, w_down], axis=0).astype(BF16)
    return _pad_rows(rows, R_PACK)


def _unpack_weights(gathered):
    take = lambda lo, n: gathered[:, lo:lo + n, :].reshape(N_DEV * n, D_MODEL)
    w_in_t = _pad_rows(take(R_IN, SH_IN), IN_PAD)
    w_gu_t = jnp.concatenate([take(R_G, SH_FF), take(R_U, SH_FF)], axis=0)
    return w_in_t, take(R_OUT, SH_OUT), w_gu_t, take(R_D, SH_FF)


def _pack_grads(dw_in_t, dw_out, dw_gu_t, dw_down):
    split = lambda a, n: a.reshape(N_DEV, n, D_MODEL)
    rows = jnp.concatenate([split(dw_in_t[:IN_WIDTH], SH_IN), split(dw_out, SH_OUT), split(dw_gu_t[:D_FF], SH_FF),
                            split(dw_gu_t[D_FF:], SH_FF), split(dw_down, SH_FF)], axis=1).astype(BF16)
    return jnp.pad(rows, ((0, 0), (0, R_PACK - R_USED), (0, 0)))


def kernel(x, mix_pre_g, w_in, b_forget, sgu_norm_g, w_spatial, b_spatial, w_out, mix_post_g, ffn_pre_g, w_gate, w_up, w_down, ffn_post_g, loss_target, m_mix_pre_g, m_w_in, m_b_forget, m_sgu_norm_g, m_w_spatial, m_b_spatial, m_w_out, m_mix_post_g, m_ffn_pre_g, m_w_gate, m_w_up, m_w_down, m_ffn_post_g, v_mix_pre_g, v_w_in, v_b_forget, v_sgu_norm_g, v_w_spatial, v_b_spatial, v_w_out, v_mix_post_g, v_ffn_pre_g, v_w_gate, v_w_up, v_w_down, v_ffn_post_g):
    depth = w_in.shape[0]
    t = x.shape[1]
    x0 = x.reshape(t, D_MODEL)
    target = loss_target.reshape(t, D_MODEL)
    sel = (jnp.arange(D_MODEL)[:, None] // HEAD_DIM == jnp.arange(HEAD_DIM)[None, :]).astype(BF16)
    vec = lambda a, l: a[l][None, :]

    weights = []
    for l in range(depth):
        packed = _pack_weights(w_in[l], w_out[l], w_gate[l], w_up[l], w_down[l])
        weights.append(_unpack_weights(_all_gather(packed, f"gather_weights_{l}")))

    saved = []
    xl = x0
    h = _rms_fwd(xl, vec(mix_pre_g, 0), "rms_in_0")
    dy = loss = None
    for l in range(depth):
        w_in_t, w_o, w_gu_t, w_d = weights[l]
        b_pad = jnp.pad(b_forget[l], (0, HEAD_DIM - N_HEADS))[None, :]
        b_cols = b_spatial[l].T
        proj = _matmul(h, w_in_t, "nt", F32, f"proj_{l}")
        c_pad = _forget_cumsum(proj, b_pad, f"forget_cumsum_{l}")
        c_heads = c_pad[:, :N_HEADS].T
        c_col, c_row = c_heads[:, :, None], c_heads[:, None, :]
        y_b, e_col = _attn_fwd(proj, c_col, c_row, f"attn_fwd_{l}")
        merged = _mix_fwd(proj, y_b, w_spatial[l], b_cols, vec(sgu_norm_g, l), f"mix_fwd_{l}")
        o = _matmul(merged, w_o, "nn", F32, f"out_proj_{l}")
        x1, h2 = _post_norm_fwd(xl, o, vec(mix_post_g, l), vec(ffn_pre_g, l), f"post_mix_{l}")
        gu = _matmul(h2, w_gu_t, "nt", F32, f"ffn_gu_{l}")
        act = _swiglu_fwd(gu, f"swiglu_fwd_{l}")
        dn = _matmul(act, w_d, "nn", F32, f"ffn_down_{l}")
        saved.append(dict(x0=xl, h=h, proj=proj, b_pad=b_pad, b_cols=b_cols, c_row=c_row, y_b=y_b, e_col=e_col,
                          merged=merged, o=o, x1=x1, h2=h2, gu=gu, act=act, dn=dn))
        if l + 1 < depth:
            xl, h = _post_norm_fwd(x1, dn, vec(ffn_post_g, l), vec(mix_pre_g, l + 1), f"post_ffn_{l}")
        else:
            dy, loss = _post_norm_loss(x1, dn, vec(ffn_post_g, l), target, "loss")

    small = [None] * depth
    big = [None] * depth
    dx = dy
    dd, dg_ffn_post = _norm_bwd(dx, f"bwd_ffn_post_{depth - 1}", post=(saved[-1]["dn"], vec(ffn_post_g, depth - 1)))
    for l in reversed(range(depth)):
        s = saved[l]
        w_in_t, w_o, w_gu_t, w_d = weights[l]
        dact = _matmul(dd, w_d, "nt", F32, f"d_act_{l}")
        dw_d = _matmul(s["act"], dd, "tn", F32, f"dw_down_{l}")
        dgu = _swiglu_bwd(s["gu"], dact, f"swiglu_bwd_{l}")
        dh2 = _matmul(dgu, w_gu_t, "nn", F32, f"d_h2_{l}")
        dw_gu_t = _matmul(dgu, s["h2"], "tn", F32, f"dw_gu_{l}")
        dx1, dg_ffn_pre, do, dg_mix_post = _norm_bwd(
            dx, f"bwd_post_mix_{l}", pre=(dh2, s["x1"], vec(ffn_pre_g, l)), post=(s["o"], vec(mix_post_g, l)))
        dmerged = _matmul(do, w_o, "nt", F32, f"d_merged_{l}")
        dw_o = _matmul(s["merged"], do, "tn", F32, f"dw_out_{l}")
        duv, dgab, dyb, delta, dw_s, dbt, dg_v = _mix_bwd(
            s["proj"], s["y_b"], dmerged, w_spatial[l], s["b_cols"], vec(sgu_norm_g, l), sel, f"mix_bwd_{l}")
        delta_col = delta[:, :N_HEADS].T[:, :, None]
        dq, dk, dv, dc_key, dc_query = _attn_bwd(s["proj"], dyb, s["e_col"], delta_col, s["c_row"], f"attn_bwd_{l}")
        dc_pad = jnp.pad((dc_key[:, 0, :] + dc_query[:, :, 0]).T, ((0, 0), (0, HEAD_DIM - N_HEADS)))
        df, db_f = _forget_bwd(s["proj"], s["b_pad"], dc_pad, f"forget_bwd_{l}")
        dproj = jnp.concatenate([duv, dq, dk, dv, dgab, df], axis=1)
        dh = _matmul(dproj, w_in_t, "nn", F32, f"d_h_{l}")
        dw_in_t = _matmul(dproj, s["h"], "tn", F32, f"dw_in_{l}")
        if l > 0:
            dx, dg_mix_pre, dd_next, dg_ffn_post_next = _norm_bwd(
                dx1, f"bwd_pre_mix_{l}", pre=(dh, s["x0"], vec(mix_pre_g, l)),
                post=(saved[l - 1]["dn"], vec(ffn_post_g, l - 1)))
        else:
            dx, dg_mix_pre = _norm_bwd(dx1, f"bwd_pre_mix_{l}", pre=(dh, s["x0"], vec(mix_pre_g, l)))
            dd_next = dg_ffn_post_next = None
        small[l] = jnp.concatenate([
            dg_mix_pre, dg_v, dg_mix_post, dg_ffn_pre, dg_ffn_post,
            dbt[:, :N_GROUPS].T.reshape(1, D_MODEL), jnp.pad(db_f, ((0, 0), (0, D_MODEL - HEAD_DIM))),
            jnp.zeros((1, D_MODEL), F32), dw_s.reshape(CHUNK, D_MODEL)], axis=0)
        received = _all_to_all(_pack_grads(dw_in_t, dw_o, dw_gu_t, dw_d), f"exchange_grads_{l}")
        big[l] = _sum_slots(received, 192, f"sum_grads_{l}")
        dd, dg_ffn_post = dd_next, dg_ffn_post_next
    grad_x = dx.reshape(x.shape)

    small_sum = _sum_slots(_all_gather(jnp.concatenate(small, axis=0), "gather_small_grads"), S_ROWS,
                           "sum_small_grads")
    loss = lax.psum(loss.reshape(()), ("x", "y", "c"))

    sm = small_sum.reshape(depth, S_ROWS, D_MODEL)
    g = {
        "mix_pre_g": sm[:, 0], "sgu_norm_g": sm[:, 1], "mix_post_g": sm[:, 2], "ffn_pre_g": sm[:, 3],
        "ffn_post_g": sm[:, 4], "b_spatial": sm[:, 5].reshape(depth, N_GROUPS, CHUNK), "b_forget": sm[:, 6, :N_HEADS],
        "w_spatial": sm[:, 8:].reshape(depth, N_GROUPS, CHUNK, CHUNK),
    }
    bg = jnp.stack(big)
    g["w_in"] = jnp.swapaxes(bg[:, R_IN:R_IN + SH_IN], 1, 2)
    g["w_out"] = bg[:, R_OUT:R_OUT + SH_OUT]
    g["w_gate"] = jnp.swapaxes(bg[:, R_G:R_G + SH_FF], 1, 2)
    g["w_up"] = jnp.swapaxes(bg[:, R_U:R_U + SH_FF], 1, 2)
    g["w_down"] = bg[:, R_D:R_D + SH_FF]

    names = ["mix_pre_g", "w_in", "b_forget", "sgu_norm_g", "w_spatial", "b_spatial", "w_out", "mix_post_g",
             "ffn_pre_g", "w_gate", "w_up", "w_down", "ffn_post_g"]
    ws = dict(mix_pre_g=mix_pre_g, w_in=w_in, b_forget=b_forget, sgu_norm_g=sgu_norm_g, w_spatial=w_spatial,
              b_spatial=b_spatial, w_out=w_out, mix_post_g=mix_post_g, ffn_pre_g=ffn_pre_g, w_gate=w_gate, w_up=w_up,
              w_down=w_down, ffn_post_g=ffn_post_g)
    ms = dict(mix_pre_g=m_mix_pre_g, w_in=m_w_in, b_forget=m_b_forget, sgu_norm_g=m_sgu_norm_g, w_spatial=m_w_spatial,
              b_spatial=m_b_spatial, w_out=m_w_out, mix_post_g=m_mix_post_g, ffn_pre_g=m_ffn_pre_g, w_gate=m_w_gate,
              w_up=m_w_up, w_down=m_w_down, ffn_post_g=m_ffn_post_g)
    vs = dict(mix_pre_g=v_mix_pre_g, w_in=v_w_in, b_forget=v_b_forget, sgu_norm_g=v_sgu_norm_g, w_spatial=v_w_spatial,
              b_spatial=v_b_spatial, w_out=v_w_out, mix_post_g=v_mix_post_g, ffn_pre_g=v_ffn_pre_g, w_gate=v_w_gate,
              w_up=v_w_up, w_down=v_w_down, ffn_post_g=v_ffn_post_g)
    deltas, new_m, new_v = [], [], []
    for n in names:
        shape = ws[n].shape
        flat = (-1, shape[-1])
        d_w, n_m, n_v = _adamw(ws[n].reshape(flat), g[n].reshape(flat), ms[n].reshape(flat), vs[n].reshape(flat),
                               f"adamw_{n}")
        deltas.append(d_w.reshape(shape))
        new_m.append(n_m.reshape(shape))
        new_v.append(n_v.reshape(shape))
    grads = [g[n].reshape(ws[n].shape) for n in names]
    return (loss, grad_x, *grads, *deltas, *new_m, *new_v)
```

```python
import math

import jax
import jax.numpy as jnp
from jax import lax
from jax.experimental import pallas as pl
from jax.experimental.pallas import tpu as pltpu

F32 = jnp.float32
BF16 = jnp.bfloat16

N_DEV = 8
D_MODEL = 1024
N_HEADS = 8
HEAD_DIM = 128
CHUNK = 128
N_GROUPS = 8
D_FF = 2816
IN_WIDTH = 7 * D_MODEL + N_HEADS
IN_PAD = 7680
EPS = 1e-6
GATED = 4 * D_MODEL
F_WIDTH = 512
QKV_WIDTH = 3 * HEAD_DIM
CB_F = GATED // HEAD_DIM
CB_QKV = (GATED + F_WIDTH) // HEAD_DIM
assert GATED + F_WIDTH + N_HEADS * QKV_WIDTH == IN_PAD and (GATED + F_WIDTH) % QKV_WIDTH == 0

ADAM_LR, ADAM_B1, ADAM_B2, ADAM_EPS, ADAM_WD, ADAM_STEP = 0.001, 0.9, 0.999, 1e-08, 0.01, 10

SH_IN = IN_WIDTH // N_DEV
SH_OUT = D_MODEL // N_DEV
SH_FF = D_FF // N_DEV
P_IN = 912
R_OUT, R_G, R_U, R_D = 0, SH_OUT, SH_OUT + SH_FF, SH_OUT + 2 * SH_FF
P_REST = SH_OUT + 3 * SH_FF
S_ROWS = 136

ATTN_BLOCK = 512
ROW_BLOCK = 512
MM_TM, MM_TN, MM_TK = 1536, 1536, 2048
VMEM_CAP = 56 << 20
NEG = -1e30


def _tile(n, cap, unit=128):
    if n <= cap:
        return n
    best = None
    for t in range(unit, cap + 1, unit):
        if n % t == 0:
            best = t
    assert best is not None, (n, cap)
    return best


def _nbytes(shape, dtype):
    return math.prod(shape) * jnp.dtype(dtype).itemsize


_HBM = pl.BlockSpec(memory_space=pltpu.HBM)
_SEMS = [pltpu.SemaphoreType.DMA((N_DEV - 1,)), pltpu.SemaphoreType.DMA((N_DEV - 1,)), pltpu.SemaphoreType.DMA]


def _mesh_pos():
    return lax.axis_index("x"), lax.axis_index("y"), lax.axis_index("c")


def _peer(pos, j):
    x, y, c = pos
    return (1 - x if j & 4 else x, 1 - y if j & 2 else y, 1 - c if j & 1 else c)


def _index(pos):
    return 4 * pos[0] + 2 * pos[1] + pos[2]


def _exchange_copies(kind, x_ref, out_ref, send_sems, recv_sems, local_sem):
    pos = _mesh_pos()
    me = _index(pos)
    src = (lambda d: x_ref) if kind == "gather" else (lambda d: x_ref.at[d])
    local = pltpu.make_async_copy(src(me), out_ref.at[me], local_sem)
    sends, recvs = [], []
    for j in range(1, N_DEV):
        peer = _peer(pos, j)
        them = _index(peer)
        sems = dict(send_sem=send_sems.at[j - 1], recv_sem=recv_sems.at[j - 1], device_id=peer,
                    device_id_type=pl.DeviceIdType.MESH)
        sends.append(pltpu.make_async_remote_copy(src_ref=src(them), dst_ref=out_ref.at[me], **sems))
        recvs.append(pltpu.make_async_remote_copy(src_ref=src(them), dst_ref=out_ref.at[them], **sems))
    return local, sends, recvs


def _start(copies):
    local, sends, _ = copies
    local.start()
    for cp in sends:
        cp.start()


def _finish(copies):
    local, sends, recvs = copies
    for cp in recvs:
        cp.wait_recv()
    for cp in sends:
        cp.wait_send()
    local.wait()


def _exchange_shape(kind, x):
    return jax.ShapeDtypeStruct((N_DEV,) + x.shape if kind == "gather" else x.shape, x.dtype)


def _exchange(kind, x, name):
    def body(x_ref, out_ref, send_sems, recv_sems, local_sem):
        copies = _exchange_copies(kind, x_ref, out_ref, send_sems, recv_sems, local_sem)
        _start(copies)
        _finish(copies)

    return pl.pallas_call(body, name=name, out_shape=_exchange_shape(kind, x), in_specs=[_HBM], out_specs=_HBM,
                          scratch_shapes=list(_SEMS))(x)


def _call(body, name, grid, in_specs, out_specs, out_shape, args, scratch=(), sem=None, nbytes=0, aliases=None,
          comm=()):
    in_specs, out_specs, out_shape, args, scratch = (list(in_specs), list(out_specs), list(out_shape), list(args),
                                                     list(scratch))
    if comm:
        n_in, n_out, n_scr, n_ops = len(args), len(out_shape), len(scratch), len(comm)
        kinds = [kind for kind, _ in comm]
        inner = body

        def body(*refs):
            ins, cins = refs[:n_in], refs[n_in:n_in + n_ops]
            outs, couts = refs[n_in + n_ops:n_in + n_ops + n_out], refs[n_in + n_ops + n_out:n_in + 2 * n_ops + n_out]
            rest = refs[n_in + 2 * n_ops + n_out:]
            own_scratch, sems = rest[:n_scr], rest[n_scr:]
            first = last = None
            for axis, size in enumerate(grid):
                at_start, at_end = pl.program_id(axis) == 0, pl.program_id(axis) == size - 1
                first = at_start if first is None else first & at_start
                last = at_end if last is None else last & at_end
            copies = [_exchange_copies(kinds[o], cins[o], couts[o], *sems[3 * o:3 * o + 3]) for o in range(n_ops)]

            @pl.when(first)
            def _():
                for c in copies:
                    _start(c)

            inner(*ins, *outs, *own_scratch)

            @pl.when(last)
            def _():
                for c in copies:
                    _finish(c)

        in_specs += [_HBM] * n_ops
        args += [x for _, x in comm]
        out_shape += [_exchange_shape(kind, x) for kind, x in comm]
        out_specs += [_HBM] * n_ops
        scratch += list(_SEMS) * n_ops
        sem = ("arbitrary",) * len(grid)
    limit = int(min(max(2 * nbytes + (8 << 20), 32 << 20), VMEM_CAP))
    return pl.pallas_call(
        body, name=name, grid=grid, in_specs=in_specs, out_specs=out_specs, out_shape=out_shape,
        scratch_shapes=scratch, input_output_aliases=aliases or {},
        compiler_params=pltpu.CompilerParams(dimension_semantics=sem, vmem_limit_bytes=limit),
    )(*args)


def _sum_slots(x, tr, name):
    _, r, c = x.shape

    def body(x_ref, o_ref):
        acc = x_ref[0].astype(F32)
        for d in range(1, N_DEV):
            acc = acc + x_ref[d].astype(F32)
        o_ref[...] = acc

    return _call(body, name, (r // tr,), [pl.BlockSpec((N_DEV, tr, c), lambda i: (0, i, 0))],
                 [pl.BlockSpec((tr, c), lambda i: (i, 0))], [jax.ShapeDtypeStruct((r, c), F32)], [x],
                 sem=("parallel",), nbytes=_nbytes((N_DEV, tr, c), x.dtype) + _nbytes((tr, c), F32))[0]


_DIMS = {"nn": (((1,), (0,)), ((), ())), "nt": (((1,), (1,)), ((), ())), "tn": (((0,), (0,)), ((), ()))}


def _matmul(a, b, mode, out_dtype, name, comm=()):
    if mode == "nn":
        (m, k), (k2, n) = a.shape, b.shape
    elif mode == "nt":
        (m, k), (n, k2) = a.shape, b.shape
    else:
        (k, m), (k2, n) = a.shape, b.shape
    assert k == k2, (a.shape, b.shape, mode)
    tm, tn, tk = _tile(m, MM_TM), _tile(n, MM_TN), _tile(k, MM_TK)
    nk = k // tk
    dims = _DIMS[mode]
    a_spec = {"nn": pl.BlockSpec((tm, tk), lambda i, j, kk: (i, kk)),
              "nt": pl.BlockSpec((tm, tk), lambda i, j, kk: (i, kk)),
              "tn": pl.BlockSpec((tk, tm), lambda i, j, kk: (kk, i))}[mode]
    b_spec = {"nn": pl.BlockSpec((tk, tn), lambda i, j, kk: (kk, j)),
              "nt": pl.BlockSpec((tn, tk), lambda i, j, kk: (j, kk)),
              "tn": pl.BlockSpec((tk, tn), lambda i, j, kk: (kk, j))}[mode]

    def partial_product(a_ref, b_ref):
        return lax.dot_general(a_ref[...].astype(BF16), b_ref[...].astype(BF16), dims, preferred_element_type=F32)

    if nk == 1:
        def body(a_ref, b_ref, o_ref):
            o_ref[...] = partial_product(a_ref, b_ref).astype(o_ref.dtype)
        scratch = []
    else:
        def body(a_ref, b_ref, o_ref, acc_ref):
            kk = pl.program_id(2)

            @pl.when(kk == 0)
            def _():
                acc_ref[...] = jnp.zeros_like(acc_ref)

            acc_ref[...] += partial_product(a_ref, b_ref)

            @pl.when(kk == nk - 1)
            def _():
                o_ref[...] = acc_ref[...].astype(o_ref.dtype)
        scratch = [pltpu.VMEM((tm, tn), F32)]

    nbytes = (_nbytes((tm, tk), a.dtype) + _nbytes((tk, tn), b.dtype) + _nbytes((tm, tn), out_dtype)
              + _nbytes((tm, tn), F32))
    res = _call(body, name, (m // tm, n // tn, nk), [a_spec, b_spec],
                [pl.BlockSpec((tm, tn), lambda i, j, kk: (i, j))], [jax.ShapeDtypeStruct((m, n), out_dtype)], [a, b],
                scratch=scratch, sem=("parallel", "parallel", "arbitrary"), nbytes=nbytes, comm=comm)
    return res if comm else res[0]


def _rms(x):
    return lax.rsqrt(jnp.mean(x * x, axis=-1, keepdims=True) + EPS)


def _rms_bwd(dz, a, g):
    r = _rms(a)
    dzg = dz * g
    da = r * dzg - a * (r * r * r) * jnp.mean(dzg * a, axis=-1, keepdims=True)
    return da, dz * (a * r)


def _row_spec(tr, width):
    return pl.BlockSpec((tr, width), lambda i: (i, 0))


def _vec_spec(width):
    return pl.BlockSpec((1, width), lambda i: (0, 0))


def _rms_fwd(x, g, name):
    t, d = x.shape
    tr = min(ROW_BLOCK, t)

    def body(x_ref, g_ref, h_ref):
        xv = x_ref[...]
        h_ref[...] = ((xv * _rms(xv)) * g_ref[...]).astype(BF16)

    return _call(body, name, (t // tr,), [_row_spec(tr, d), _vec_spec(d)], [_row_spec(tr, d)],
                 [jax.ShapeDtypeStruct((t, d), BF16)], [x, g], sem=("parallel",), nbytes=3 * _nbytes((tr, d), F32))[0]


def _post_norm_fwd(x, o, g_post, g_next, name):
    t, d = x.shape
    tr = min(ROW_BLOCK, t)

    def body(x_ref, o_ref, gp_ref, gn_ref, xn_ref, h_ref):
        ov = o_ref[...]
        xn = x_ref[...] + (ov * _rms(ov)) * gp_ref[...]
        xn_ref[...] = xn
        h_ref[...] = ((xn * _rms(xn)) * gn_ref[...]).astype(BF16)

    return _call(body, name, (t // tr,), [_row_spec(tr, d), _row_spec(tr, d), _vec_spec(d), _vec_spec(d)],
                 [_row_spec(tr, d), _row_spec(tr, d)],
                 [jax.ShapeDtypeStruct((t, d), F32), jax.ShapeDtypeStruct((t, d), BF16)], [x, o, g_post, g_next],
                 sem=("parallel",), nbytes=5 * _nbytes((tr, d), F32))


def _post_norm_loss(x, o, g_post, target, name):
    t, d = x.shape
    tr = min(ROW_BLOCK, t)

    def body(x_ref, o_ref, gp_ref, t_ref, dy_ref, loss_ref):
        ov = o_ref[...]
        err = x_ref[...] + (ov * _rms(ov)) * gp_ref[...] - t_ref[...]
        dy_ref[...] = err / d
        part = 0.5 * jnp.sum(jnp.mean(err * err, axis=-1, keepdims=True), axis=0, keepdims=True)

        @pl.when(pl.program_id(0) == 0)
        def _():
            loss_ref[...] = jnp.zeros_like(loss_ref)

        loss_ref[...] += part

    return _call(body, name, (t // tr,), [_row_spec(tr, d), _row_spec(tr, d), _vec_spec(d), _row_spec(tr, d)],
                 [_row_spec(tr, d), pl.BlockSpec((1, 1), lambda i: (0, 0))],
                 [jax.ShapeDtypeStruct((t, d), F32), jax.ShapeDtypeStruct((1, 1), F32)], [x, o, g_post, target],
                 sem=("arbitrary",), nbytes=5 * _nbytes((tr, d), F32))


def _norm_bwd(dskip, name, pre=None, post=None):
    t, d = dskip.shape
    tr = min(ROW_BLOCK, t)
    n_in = 1 + (3 if pre else 0) + (2 if post else 0)

    def body(*refs):
        ins, outs = list(refs[:n_in]), list(refs[n_in:])
        first = pl.program_id(0) == 0
        dx = ins.pop(0)[...]
        if pre:
            dh_ref, xin_ref, gpre_ref = ins.pop(0), ins.pop(0), ins.pop(0)
            dxin, dg_rows = _rms_bwd(dh_ref[...].astype(F32), xin_ref[...], gpre_ref[...])
            dx = dx + dxin
            dx_ref, dgpre_ref = outs.pop(0), outs.pop(0)
            dx_ref[...] = dx

            @pl.when(first)
            def _():
                dgpre_ref[...] = jnp.zeros_like(dgpre_ref)

            dgpre_ref[...] += jnp.sum(dg_rows, axis=0, keepdims=True)
        if post:
            a_ref, gpost_ref = ins.pop(0), ins.pop(0)
            da, dg_rows = _rms_bwd(dx, a_ref[...], gpost_ref[...])
            da_ref, dgpost_ref = outs.pop(0), outs.pop(0)
            da_ref[...] = da.astype(BF16)

            @pl.when(first)
            def _():
                dgpost_ref[...] = jnp.zeros_like(dgpost_ref)

            dgpost_ref[...] += jnp.sum(dg_rows, axis=0, keepdims=True)

    args, in_specs, out_shape, out_specs = [dskip], [_row_spec(tr, d)], [], []
    if pre:
        args += list(pre)
        in_specs += [_row_spec(tr, d), _row_spec(tr, d), _vec_spec(d)]
        out_shape += [jax.ShapeDtypeStruct((t, d), F32), jax.ShapeDtypeStruct((1, d), F32)]
        out_specs += [_row_spec(tr, d), _vec_spec(d)]
    if post:
        args += list(post)
        in_specs += [_row_spec(tr, d), _vec_spec(d)]
        out_shape += [jax.ShapeDtypeStruct((t, d), BF16), jax.ShapeDtypeStruct((1, d), F32)]
        out_specs += [_row_spec(tr, d), _vec_spec(d)]
    return _call(body, name, (t // tr,), in_specs, out_specs, out_shape, args, sem=("arbitrary",),
                 nbytes=7 * _nbytes((tr, d), F32))


def _sigmoid(x):
    return 1.0 / (1.0 + jnp.exp(-x))


def _swiglu_fwd(gu, name):
    t = gu.shape[0]
    tr = min(ROW_BLOCK, t)

    def body(gu_ref, a_ref):
        gate, up = gu_ref[:, :D_FF], gu_ref[:, D_FF:]
        a_ref[...] = ((gate * _sigmoid(gate)) * up).astype(BF16)

    return _call(body, name, (t // tr,), [_row_spec(tr, 2 * D_FF)], [_row_spec(tr, D_FF)],
                 [jax.ShapeDtypeStruct((t, D_FF), BF16)], [gu], sem=("parallel",),
                 nbytes=2 * _nbytes((tr, 2 * D_FF), F32))[0]


def _swiglu_bwd(gu, dact, name):
    t = gu.shape[0]
    tr = min(ROW_BLOCK, t)

    def body(gu_ref, da_ref, o_ref):
        gate, up = gu_ref[:, :D_FF], gu_ref[:, D_FF:]
        da = da_ref[...]
        s = _sigmoid(gate)
        o_ref[:, :D_FF] = (da * up * (s * (1.0 + gate * (1.0 - s)))).astype(BF16)
        o_ref[:, D_FF:] = (da * (gate * s)).astype(BF16)

    return _call(body, name, (t // tr,), [_row_spec(tr, 2 * D_FF), _row_spec(tr, D_FF)], [_row_spec(tr, 2 * D_FF)],
                 [jax.ShapeDtypeStruct((t, 2 * D_FF), BF16)], [gu, dact], sem=("parallel",),
                 nbytes=3 * _nbytes((tr, 2 * D_FF), F32))[0]


def _split3(x):
    hi = x.astype(BF16)
    r1 = x - hi.astype(F32)
    mid = r1.astype(BF16)
    lo = (r1 - mid.astype(F32)).astype(BF16)
    return hi, mid, lo


def _dot_exact(mat01, x):
    hi, mid, lo = _split3(x)
    out = jnp.dot(mat01, hi, preferred_element_type=F32)
    out = out + jnp.dot(mat01, mid, preferred_element_type=F32)
    return out + jnp.dot(mat01, lo, preferred_element_type=F32)


def _dot_exact_rhs(x, mat01):
    hi, mid, lo = _split3(x)
    out = jnp.dot(hi, mat01, preferred_element_type=F32)
    out = out + jnp.dot(mid, mat01, preferred_element_type=F32)
    return out + jnp.dot(lo, mat01, preferred_element_type=F32)


def _tri(lower):
    r = lax.broadcasted_iota(jnp.int32, (CHUNK, CHUNK), 0)
    c = lax.broadcasted_iota(jnp.int32, (CHUNK, CHUNK), 1)
    return jnp.where(r >= c if lower else r <= c, 1.0, 0.0).astype(BF16)


def _log_sigmoid(z):
    return jnp.minimum(z, 0.0) - jnp.log(1.0 + jnp.exp(-jnp.abs(z)))


def _forget_cumsum(proj, b_pad, name):
    t = proj.shape[0]
    nb = t // CHUNK

    def body(f_ref, b_ref, c_ref):
        tri = _tri(True)
        b = b_ref[...]

        def blk(i, carry):
            rows = pl.ds(pl.multiple_of(i * CHUNK, CHUNK), CHUNK)
            cs = _dot_exact(tri, _log_sigmoid(f_ref[rows, :] + b)) + carry
            c_ref[rows, :] = cs
            return cs[CHUNK - 1:CHUNK, :]

        lax.fori_loop(0, nb, blk, jnp.zeros((1, HEAD_DIM), F32))

    return _call(body, name, (1,),
                 [pl.BlockSpec((t, HEAD_DIM), lambda i: (0, CB_F)), pl.BlockSpec((1, HEAD_DIM), lambda i: (0, 0))],
                 [pl.BlockSpec((t, HEAD_DIM), lambda i: (0, 0))], [jax.ShapeDtypeStruct((t, HEAD_DIM), F32)],
                 [proj, b_pad], sem=("arbitrary",), nbytes=2 * _nbytes((t, HEAD_DIM), F32))[0]


def _forget_bwd(proj, b_pad, dc_key, dc_query, dproj, name):
    t = proj.shape[0]
    tr = min(ROW_BLOCK, t)
    nb = t // tr
    rev = lambda i: nb - 1 - i

    def body(f_ref, b_ref, dck_ref, dcq_ref, dproj_in, df_ref, db_ref, run_ref):
        del dproj_in

        @pl.when(pl.program_id(0) == 0)
        def _():
            run_ref[...] = jnp.zeros_like(run_ref)
            db_ref[...] = jnp.zeros_like(db_ref)

        tri = _tri(False)
        b = b_ref[...]
        lane = lax.broadcasted_iota(jnp.int32, (CHUNK, HEAD_DIM), 1)
        df_ref[...] = jnp.zeros_like(df_ref)
        for n in reversed(range(tr // CHUNK)):
            rows = slice(n * CHUNK, (n + 1) * CHUNK)
            dc = dck_ref[rows, :]
            for h in range(N_HEADS):
                dc = dc + jnp.where(lane == h, dcq_ref[rows, h * HEAD_DIM:(h + 1) * HEAD_DIM], 0.0)
            dlogf = _dot_exact(tri, dc) + run_ref[...]
            run_ref[...] = dlogf[0:1, :]
            z = f_ref[rows, :] + b
            e = jnp.exp(-jnp.abs(z))
            sig_neg = jnp.where(z >= 0.0, e, 1.0) / (1.0 + e)
            df = jnp.where(lane < N_HEADS, dlogf * sig_neg, 0.0)
            df_ref[rows, :HEAD_DIM] = df.astype(BF16)
            db_ref[...] += jnp.sum(df, axis=0, keepdims=True)

    return _call(body, name, (nb,),
                 [pl.BlockSpec((tr, HEAD_DIM), lambda i: (rev(i), CB_F)), pl.BlockSpec((1, HEAD_DIM), lambda i: (0, 0)),
                  pl.BlockSpec((tr, HEAD_DIM), lambda i: (rev(i), 0)), pl.BlockSpec((tr, D_MODEL), lambda i: (rev(i), 0)),
                  pl.BlockSpec(memory_space=pl.ANY)],
                 [pl.BlockSpec((tr, F_WIDTH), lambda i: (rev(i), GATED // F_WIDTH)),
                  pl.BlockSpec((1, HEAD_DIM), lambda i: (0, 0))],
                 [jax.ShapeDtypeStruct(dproj.shape, dproj.dtype), jax.ShapeDtypeStruct((1, HEAD_DIM), F32)],
                 [proj, b_pad, dc_key, dc_query, dproj], scratch=[pltpu.VMEM((1, HEAD_DIM), F32)],
                 sem=("arbitrary",), nbytes=4 * _nbytes((tr, D_MODEL), F32), aliases={4: 0})


def _causal(block):
    r = lax.broadcasted_iota(jnp.int32, (block, block), 0)
    c = lax.broadcasted_iota(jnp.int32, (block, block), 1)
    return c <= r


def _lanes(x, width):
    return jnp.concatenate([x] * (width // HEAD_DIM), axis=1)


def _attn_fwd(proj, c_pad, c_row, name, comm=()):
    t = proj.shape[0]
    blk = min(ATTN_BLOCK, t)
    nq = t // blk
    scale = HEAD_DIM ** -0.5

    def body(q_ref, k_ref, v_ref, cp_ref, cr_ref, y_ref, e_ref):
        h, i = pl.program_id(0), pl.program_id(1)
        q = (q_ref[...] * scale).astype(BF16)
        lane = lax.broadcasted_iota(jnp.int32, (blk, HEAD_DIM), 1)
        ci = jnp.sum(jnp.where(lane == h, cp_ref[...], 0.0), axis=1, keepdims=True)

        def step(j, carry, diagonal):
            m, l, acc = carry
            off = pl.multiple_of(j * blk, blk)
            k = k_ref[pl.ds(off, blk), :].astype(BF16)
            v = v_ref[pl.ds(off, blk), :].astype(BF16)
            s = lax.dot_general(q, k, _DIMS["nt"], preferred_element_type=F32)
            s = s + ci - cr_ref[:, pl.ds(off, blk)]
            if diagonal:
                s = jnp.where(_causal(blk), s, NEG)
            m_new = jnp.maximum(m, jnp.max(s, axis=1, keepdims=True))
            alpha = jnp.exp(m - m_new)
            p = jnp.exp(s - m_new)
            l = alpha * l + jnp.sum(p, axis=1, keepdims=True)
            acc = alpha * acc + jnp.dot(p.astype(BF16), v, preferred_element_type=F32)
            return m_new, l, acc

        init = (jnp.full((blk, 1), NEG, F32), jnp.zeros((blk, 1), F32), jnp.zeros((blk, HEAD_DIM), F32))
        carry = lax.fori_loop(0, i, lambda j, c: step(j, c, False), init)
        m, l, acc = step(i, carry, True)
        y_ref[...] = acc / l
        e_ref[...] = jnp.broadcast_to(ci - (m + jnp.log(l)), (blk, HEAD_DIM))

    nbytes = 2 * _nbytes((t, HEAD_DIM), F32) + 4 * _nbytes((blk, HEAD_DIM), F32) + 4 * _nbytes((blk, blk), F32)
    head_block = pl.BlockSpec((blk, HEAD_DIM), lambda h, i: (i, h))
    return _call(body, name, (N_HEADS, nq),
                 [pl.BlockSpec((blk, HEAD_DIM), lambda h, i: (i, CB_QKV + 3 * h)),
                  pl.BlockSpec((t, HEAD_DIM), lambda h, i: (0, CB_QKV + 3 * h + 1)),
                  pl.BlockSpec((t, HEAD_DIM), lambda h, i: (0, CB_QKV + 3 * h + 2)),
                  pl.BlockSpec((blk, HEAD_DIM), lambda h, i: (i, 0)),
                  pl.BlockSpec((None, 1, t), lambda h, i: (h, 0, 0))],
                 [head_block, head_block],
                 [jax.ShapeDtypeStruct((t, D_MODEL), F32), jax.ShapeDtypeStruct((t, D_MODEL), F32)],
                 [proj, proj, proj, c_pad, c_row], sem=("parallel", "arbitrary"), nbytes=nbytes, comm=comm)


def _attn_bwd(proj, dy, e_b, delta_b, c_row, dproj, name, comm=()):
    t = proj.shape[0]
    blk = min(ATTN_BLOCK, t)
    nb = t // blk
    scale = HEAD_DIM ** -0.5

    def body(q_ref, k_ref, v_ref, dy_ref, e_ref, dl_ref, cr_ref, dproj_in, dqkv_ref, dc_ref, dr_ref, dq_acc):
        del dproj_in
        j = pl.program_id(1)

        @pl.when(j == 0)
        def _():
            dq_acc[...] = jnp.zeros_like(dq_acc)
            dr_ref[...] = jnp.zeros_like(dr_ref)

        k = k_ref[...].astype(BF16)
        v = v_ref[...].astype(BF16)
        crow = cr_ref[...]

        def step(i, carry, diagonal):
            dk, dv, dcs = carry
            rows = pl.ds(pl.multiple_of(i * blk, blk), blk)
            q = (q_ref[rows, :] * scale).astype(BF16)
            dyi = dy_ref[rows, :].astype(BF16)
            s = lax.dot_general(q, k, _DIMS["nt"], preferred_element_type=F32) + _lanes(e_ref[rows, :], blk) - crow
            if diagonal:
                s = jnp.where(_causal(blk), s, NEG)
            p = jnp.exp(s)
            dv = dv + lax.dot_general(p.astype(BF16), dyi, _DIMS["tn"], preferred_element_type=F32)
            dp = lax.dot_general(dyi, v, _DIMS["nt"], preferred_element_type=F32)
            ds = p * (dp - _lanes(dl_ref[rows, :], blk))
            dsb = ds.astype(BF16)
            dk = dk + lax.dot_general(dsb, q, _DIMS["tn"], preferred_element_type=F32)
            dq_acc[rows, :] += jnp.dot(dsb, k, preferred_element_type=F32) * scale
            dr_ref[rows, :] += jnp.sum(ds, axis=1, keepdims=True)
            return dk, dv, dcs + jnp.sum(ds, axis=0, keepdims=True)

        zeros = jnp.zeros((blk, HEAD_DIM), F32)
        carry = step(j, (zeros, zeros, jnp.zeros((1, blk), F32)), True)
        dk, dv, dcs = lax.fori_loop(j + 1, nb, lambda i, c: step(i, c, False), carry)
        mine = pl.ds(pl.multiple_of(j * blk, blk), blk)
        dqkv_ref[:, :HEAD_DIM] = dq_acc[mine, :].astype(BF16)
        dqkv_ref[:, HEAD_DIM:2 * HEAD_DIM] = dk.astype(BF16)
        dqkv_ref[:, 2 * HEAD_DIM:] = dv.astype(BF16)
        dc_ref[...] = -dcs

    full = lambda cb: pl.BlockSpec((t, HEAD_DIM), lambda h, j: (0, cb(h)))
    head = lambda h: h
    nbytes = (6 * _nbytes((t, HEAD_DIM), F32) + 4 * _nbytes((blk, HEAD_DIM), F32) + 5 * _nbytes((blk, blk), F32))
    return _call(body, name, (N_HEADS, nb),
                 [full(lambda h: CB_QKV + 3 * h),
                  pl.BlockSpec((blk, HEAD_DIM), lambda h, j: (j, CB_QKV + 3 * h + 1)),
                  pl.BlockSpec((blk, HEAD_DIM), lambda h, j: (j, CB_QKV + 3 * h + 2)),
                  full(head), full(head), full(head),
                  pl.BlockSpec((None, 1, blk), lambda h, j: (h, 0, j)),
                  pl.BlockSpec(memory_space=pl.ANY)],
                 [pl.BlockSpec((blk, QKV_WIDTH), lambda h, j: (j, (GATED + F_WIDTH) // QKV_WIDTH + h)),
                  pl.BlockSpec((None, 1, blk), lambda h, j: (h, 0, j)),
                  full(head)],
                 [jax.ShapeDtypeStruct(dproj.shape, dproj.dtype), jax.ShapeDtypeStruct((N_HEADS, 1, t), F32),
                  jax.ShapeDtypeStruct((t, D_MODEL), F32)],
                 [proj, proj, proj, dy, e_b, delta_b, c_row, dproj], scratch=[pltpu.VMEM((t, HEAD_DIM), F32)],
                 sem=("parallel", "arbitrary"), nbytes=nbytes, aliases={7: 0}, comm=comm)


_GELU_K = math.sqrt(2.0 / math.pi)
_GELU_C = 0.044715


def _gelu(x):
    t = jnp.tanh(_GELU_K * (x + _GELU_C * (x * x * x)))
    return 0.5 * x * (1.0 + t), t


def _gelu_grad(x, t):
    return 0.5 * (1.0 + t) + 0.5 * x * (1.0 - t * t) * (_GELU_K * (1.0 + 3.0 * _GELU_C * (x * x)))


def _layernorm_stats(a):
    mu = jnp.mean(a, axis=-1, keepdims=True)
    xc = a - mu
    r = lax.rsqrt(jnp.mean(xc * xc, axis=-1, keepdims=True) + EPS)
    return xc * r, r


def _group(g):
    return slice(g * CHUNK, (g + 1) * CHUNK)


def _gated_cols(k):
    return pl.BlockSpec((CHUNK, D_MODEL), lambda i: (i, k))


def _fixed(shape):
    return pl.BlockSpec(shape, lambda i: (0,) * len(shape))


def _mix_fwd(proj, y_b, w_s, b_cols, g_v, name):
    t = proj.shape[0]

    def body(u_ref, vs_ref, ga_ref, gb_ref, yb_ref, w_ref, b_ref, gv_ref, o_ref):
        a_u, _ = _gelu(u_ref[...])
        a_v, _ = _gelu(vs_ref[...])
        xhat, _ = _layernorm_stats(a_v)
        vn = (xhat * gv_ref[...]).astype(BF16)
        sa, sb = _sigmoid(ga_ref[...]), _sigmoid(gb_ref[...])
        yb = yb_ref[...]
        mask = _causal(CHUNK)
        for g in range(N_GROUPS):
            cols = _group(g)
            w = jnp.where(mask, w_ref[g], 0.0).astype(BF16)
            mixed = jnp.dot(w, vn[:, cols], preferred_element_type=F32) + b_ref[:, g:g + 1]
            o_ref[:, cols] = (sa[:, cols] * (a_u[:, cols] * mixed) + sb[:, cols] * yb[:, cols]).astype(BF16)

    return _call(body, name, (t // CHUNK,),
                 [_gated_cols(0), _gated_cols(1), _gated_cols(2), _gated_cols(3), _row_spec(CHUNK, D_MODEL),
                  _fixed((N_GROUPS, CHUNK, CHUNK)), _fixed((CHUNK, N_GROUPS)), _fixed((1, D_MODEL))],
                 [_row_spec(CHUNK, D_MODEL)], [jax.ShapeDtypeStruct((t, D_MODEL), BF16)],
                 [proj, proj, proj, proj, y_b, w_s, b_cols, g_v], sem=("parallel",),
                 nbytes=16 * _nbytes((CHUNK, D_MODEL), F32))[0]


def _mix_bwd(proj, y_b, dmerged, w_s, b_cols, g_v, sel, name):
    t = proj.shape[0]

    def body(u_ref, vs_ref, ga_ref, gb_ref, yb_ref, dm_ref, w_ref, b_ref, gv_ref, sel_ref,
             dg_ref, dyb_ref, delta_ref, dw_ref, dbt_ref, dgv_ref):
        @pl.when(pl.program_id(0) == 0)
        def _():
            dw_ref[...] = jnp.zeros_like(dw_ref)
            dbt_ref[...] = jnp.zeros_like(dbt_ref)
            dgv_ref[...] = jnp.zeros_like(dgv_ref)

        u, vs = u_ref[...], vs_ref[...]
        a_u, t_u = _gelu(u)
        a_v, t_v = _gelu(vs)
        xhat, r = _layernorm_stats(a_v)
        gv = gv_ref[...]
        vn = (xhat * gv).astype(BF16)
        sa, sb = _sigmoid(ga_ref[...]), _sigmoid(gb_ref[...])
        yb, dm = yb_ref[...], dm_ref[...].astype(F32)
        dyb = dm * sb
        dyb_ref[...] = dyb
        dyb_yb = dyb * yb
        dg_ref[:, 3 * D_MODEL:] = (dm * yb * (sb * (1.0 - sb))).astype(BF16)
        dya = dm * sa
        mask = _causal(CHUNK)
        dmixed_parts, dvn_parts = [], []
        for g in range(N_GROUPS):
            cols = _group(g)
            delta_ref[:, cols] = jnp.broadcast_to(jnp.sum(dyb_yb[:, cols], axis=1, keepdims=True), (CHUNK, CHUNK))
            w = jnp.where(mask, w_ref[g], 0.0).astype(BF16)
            mixed = jnp.dot(w, vn[:, cols], preferred_element_type=F32) + b_ref[:, g:g + 1]
            y_a = a_u[:, cols] * mixed
            dg_ref[:, 2 * D_MODEL + g * CHUNK:2 * D_MODEL + (g + 1) * CHUNK] = (
                dm[:, cols] * y_a * (sa[:, cols] * (1.0 - sa[:, cols]))).astype(BF16)
            dg_ref[:, cols] = (dya[:, cols] * mixed * _gelu_grad(u[:, cols], t_u[:, cols])).astype(BF16)
            dmixed = dya[:, cols] * a_u[:, cols]
            dmb = dmixed.astype(BF16)
            dw = lax.dot_general(dmb, vn[:, cols], _DIMS["nt"], preferred_element_type=F32)
            dw_ref[g] += jnp.where(mask, dw, 0.0)
            dvn_parts.append(lax.dot_general(w, dmb, _DIMS["tn"], preferred_element_type=F32))
            dmixed_parts.append(dmixed)
        dmixed_all = jnp.concatenate(dmixed_parts, axis=1)
        dvn = jnp.concatenate(dvn_parts, axis=1)
        dbt_ref[...] += _dot_exact_rhs(dmixed_all, sel_ref[...])
        dgv_ref[...] += jnp.sum(dvn * xhat, axis=0, keepdims=True)
        dxh = dvn * gv
        da_v = r * (dxh - jnp.mean(dxh, axis=-1, keepdims=True)
                    - xhat * jnp.mean(dxh * xhat, axis=-1, keepdims=True))
        dg_ref[:, D_MODEL:2 * D_MODEL] = (da_v * _gelu_grad(vs, t_v)).astype(BF16)

    row = lambda width: _row_spec(CHUNK, width)
    return _call(body, name, (t // CHUNK,),
                 [_gated_cols(0), _gated_cols(1), _gated_cols(2), _gated_cols(3), row(D_MODEL), row(D_MODEL),
                  _fixed((N_GROUPS, CHUNK, CHUNK)), _fixed((CHUNK, N_GROUPS)), _fixed((1, D_MODEL)),
                  _fixed((D_MODEL, HEAD_DIM))],
                 [row(GATED), row(D_MODEL), row(D_MODEL),
                  _fixed((N_GROUPS, CHUNK, CHUNK)), _fixed((CHUNK, HEAD_DIM)), _fixed((1, D_MODEL))],
                 [jax.ShapeDtypeStruct((t, IN_PAD), BF16), jax.ShapeDtypeStruct((t, D_MODEL), F32),
                  jax.ShapeDtypeStruct((t, D_MODEL), F32), jax.ShapeDtypeStruct((N_GROUPS, CHUNK, CHUNK), F32),
                  jax.ShapeDtypeStruct((CHUNK, HEAD_DIM), F32), jax.ShapeDtypeStruct((1, D_MODEL), F32)],
                 [proj, proj, proj, proj, y_b, dmerged, w_s, b_cols, g_v, sel], sem=("arbitrary",),
                 nbytes=40 * _nbytes((CHUNK, D_MODEL), F32))


def _adamw(w, g, m, v, name):
    r, c = w.shape
    tr = _tile(r, 256, unit=8)

    def body(w_ref, g_ref, m_ref, v_ref, d_ref, nm_ref, nv_ref):
        gv = g_ref[...]
        nm = ADAM_B1 * m_ref[...] + (1.0 - ADAM_B1) * gv
        nv = ADAM_B2 * v_ref[...] + (1.0 - ADAM_B2) * (gv * gv)
        m_hat = nm / (1.0 - ADAM_B1 ** ADAM_STEP)
        v_hat = nv / (1.0 - ADAM_B2 ** ADAM_STEP)
        d_ref[...] = -ADAM_LR * (m_hat / (jnp.sqrt(v_hat) + ADAM_EPS) + ADAM_WD * w_ref[...])
        nm_ref[...] = nm
        nv_ref[...] = nv

    spec = pl.BlockSpec((tr, c), lambda i: (i, 0))
    shape = jax.ShapeDtypeStruct((r, c), F32)
    return _call(body, name, (r // tr,), [spec] * 4, [spec] * 3, [shape] * 3, [w, g, m, v], sem=("parallel",),
                 nbytes=7 * _nbytes((tr, max(c, 128)), F32))


def _pad_rows(a, rows):
    return jnp.pad(a, ((0, rows - a.shape[0]), (0, 0)))


def _pack_rest(w_out, w_gate, w_up, w_down):
    return jnp.concatenate([w_out, w_gate.T, w_up.T, w_down], axis=0).astype(BF16)


def _unpack_in(gathered):
    nat = gathered[:, :SH_IN, :].reshape(IN_WIDTH, D_MODEL)
    qkv = nat[2 * D_MODEL:5 * D_MODEL].reshape(3, N_HEADS, HEAD_DIM, D_MODEL)
    qkv = jnp.transpose(qkv, (1, 0, 2, 3)).reshape(3 * D_MODEL, D_MODEL)
    f = _pad_rows(nat[7 * D_MODEL:], F_WIDTH)
    return jnp.concatenate([nat[:2 * D_MODEL], nat[5 * D_MODEL:7 * D_MODEL], f, qkv], axis=0)


def _unpack_rest(gathered):
    take = lambda lo, n: gathered[:, lo:lo + n, :].reshape(N_DEV * n, D_MODEL)
    return take(R_OUT, SH_OUT), jnp.concatenate([take(R_G, SH_FF), take(R_U, SH_FF)], axis=0), take(R_D, SH_FF)


def _pack_grad_in(dw_in_t):
    qkv = dw_in_t[GATED + F_WIDTH:].reshape(N_HEADS, 3, HEAD_DIM, D_MODEL)
    qkv = jnp.transpose(qkv, (1, 0, 2, 3)).reshape(3 * D_MODEL, D_MODEL)
    nat = jnp.concatenate([dw_in_t[:2 * D_MODEL], qkv, dw_in_t[2 * D_MODEL:GATED], dw_in_t[GATED:GATED + N_HEADS]],
                          axis=0)
    return jnp.pad(nat.reshape(N_DEV, SH_IN, D_MODEL).astype(BF16), ((0, 0), (0, P_IN - SH_IN), (0, 0)))


def _pack_grad_rest(dw_out, dw_gu_t, dw_down):
    split = lambda a, n: a.reshape(N_DEV, n, D_MODEL)
    return jnp.concatenate([split(dw_out, SH_OUT), split(dw_gu_t[:D_FF], SH_FF), split(dw_gu_t[D_FF:], SH_FF),
                            split(dw_down, SH_FF)], axis=1).astype(BF16)


def kernel(x, mix_pre_g, w_in, b_forget, sgu_norm_g, w_spatial, b_spatial, w_out, mix_post_g, ffn_pre_g, w_gate, w_up, w_down, ffn_post_g, loss_target, m_mix_pre_g, m_w_in, m_b_forget, m_sgu_norm_g, m_w_spatial, m_b_spatial, m_w_out, m_mix_post_g, m_ffn_pre_g, m_w_gate, m_w_up, m_w_down, m_ffn_post_g, v_mix_pre_g, v_w_in, v_b_forget, v_sgu_norm_g, v_w_spatial, v_b_spatial, v_w_out, v_mix_post_g, v_ffn_pre_g, v_w_gate, v_w_up, v_w_down, v_ffn_post_g):
    depth = w_in.shape[0]
    assert depth == 2
    t = x.shape[1]
    x0 = x.reshape(t, D_MODEL)
    target = loss_target.reshape(t, D_MODEL)
    sel = (jnp.arange(D_MODEL)[:, None] // HEAD_DIM == jnp.arange(HEAD_DIM)[None, :]).astype(BF16)
    vec = lambda a, l: a[l][None, :]
    zero_row = jnp.zeros((1, D_MODEL), F32)

    p_in = [_pad_rows(w_in[l].T.astype(BF16), P_IN) for l in range(depth)]
    p_rest = [_pack_rest(w_out[l], w_gate[l], w_up[l], w_down[l]) for l in range(depth)]
    w_in_t = [None] * depth
    w_rest = [None] * depth
    w_in_t[0] = _unpack_in(_exchange("gather", p_in[0], "gather_w_in_0"))

    saved = []
    xl = x0
    h = _rms_fwd(xl, vec(mix_pre_g, 0), "rms_in_0")
    dy = loss = None
    for l in range(depth):
        b_pad = jnp.pad(b_forget[l], (0, HEAD_DIM - N_HEADS))[None, :]
        b_cols = b_spatial[l].T
        proj = _matmul(h, w_in_t[l], "nt", F32, f"proj_{l}")
        c_pad = _forget_cumsum(proj, b_pad, f"forget_cumsum_{l}")
        c_row = c_pad[:, :N_HEADS].T[:, None, :]
        riders = [("gather", p_rest[l])] + ([("gather", p_in[l + 1])] if l + 1 < depth else [])
        y_b, e_b, *arrived = _attn_fwd(proj, c_pad, c_row, f"attn_fwd_{l}", comm=riders)
        w_rest[l] = _unpack_rest(arrived[0])
        if l + 1 < depth:
            w_in_t[l + 1] = _unpack_in(arrived[1])
        w_o, w_gu_t, w_d = w_rest[l]
        merged = _mix_fwd(proj, y_b, w_spatial[l], b_cols, vec(sgu_norm_g, l), f"mix_fwd_{l}")
        o = _matmul(merged, w_o, "nn", F32, f"out_proj_{l}")
        x1, h2 = _post_norm_fwd(xl, o, vec(mix_post_g, l), vec(ffn_pre_g, l), f"post_mix_{l}")
        gu = _matmul(h2, w_gu_t, "nt", F32, f"ffn_gu_{l}")
        act = _swiglu_fwd(gu, f"swiglu_fwd_{l}")
        dn = _matmul(act, w_d, "nn", F32, f"ffn_down_{l}")
        saved.append(dict(x0=xl, h=h, proj=proj, b_pad=b_pad, b_cols=b_cols, c_row=c_row, y_b=y_b, e_b=e_b,
                          merged=merged, o=o, x1=x1, h2=h2, gu=gu, act=act, dn=dn))
        if l + 1 < depth:
            xl, h = _post_norm_fwd(x1, dn, vec(ffn_post_g, l), vec(mix_pre_g, l + 1), f"post_ffn_{l}")
        else:
            dy, loss = _post_norm_loss(x1, dn, vec(ffn_post_g, l), target, "loss")

    g_small = [None] * depth
    grad_sums = {}
    pending = []
    late_rows = None
    dx = dy
    dd, dg_ffn_post = _norm_bwd(dx, f"bwd_ffn_post_{depth - 1}", post=(saved[-1]["dn"], vec(ffn_post_g, depth - 1)))
    for l in reversed(range(depth)):
        s = saved[l]
        w_o, w_gu_t, w_d = w_rest[l]
        dact = _matmul(dd, w_d, "nt", F32, f"d_act_{l}")
        dw_d = _matmul(s["act"], dd, "tn", F32, f"dw_down_{l}")
        dgu = _swiglu_bwd(s["gu"], dact, f"swiglu_bwd_{l}")
        dh2 = _matmul(dgu, w_gu_t, "nn", F32, f"d_h2_{l}")
        dw_gu_t = _matmul(dgu, s["h2"], "tn", F32, f"dw_gu_{l}")
        dx1, dg_ffn_pre, do, dg_mix_post = _norm_bwd(
            dx, f"bwd_post_mix_{l}", pre=(dh2, s["x1"], vec(ffn_pre_g, l)), post=(s["o"], vec(mix_post_g, l)))
        dmerged = _matmul(do, w_o, "nt", F32, f"d_merged_{l}")
        dw_o = _matmul(s["merged"], do, "tn", F32, f"dw_out_{l}")
        dproj, dyb, delta_b, dw_s, dbt, dg_v = _mix_bwd(
            s["proj"], s["y_b"], dmerged, w_spatial[l], s["b_cols"], vec(sgu_norm_g, l), sel, f"mix_bwd_{l}")
        db_s = dbt[:, :N_GROUPS].T.reshape(1, D_MODEL)
        riders = pending + [("scatter", _pack_grad_rest(dw_o, dw_gu_t, dw_d))]
        keys = ([("in", l + 1)] if pending else []) + [("rest", l)] + (["small"] if l == 0 else [])
        if l == 0:
            g_small[0] = jnp.concatenate([zero_row, dg_v, dg_mix_post, dg_ffn_pre, dg_ffn_post, db_s, zero_row,
                                          zero_row, dw_s.reshape(CHUNK, D_MODEL)], axis=0)
            riders = riders + [("gather", jnp.concatenate([g_small[1], g_small[0]], axis=0))]
        dproj, dc_key, dc_query, *arrived = _attn_bwd(s["proj"], dyb, s["e_b"], delta_b, s["c_row"], dproj,
                                                        f"attn_bwd_{l}", comm=riders)
        for key, got in zip(keys, arrived):
            grad_sums[key] = got
        dc_key_t = jnp.pad(dc_key[:, 0, :].T, ((0, 0), (0, HEAD_DIM - N_HEADS)))
        dproj, db_f = _forget_bwd(s["proj"], s["b_pad"], dc_key_t, dc_query, dproj, f"forget_bwd_{l}")
        db_f_row = jnp.pad(db_f, ((0, 0), (0, D_MODEL - HEAD_DIM)))
        dw_in_t = _matmul(dproj, s["h"], "tn", F32, f"dw_in_{l}")
        if l > 0:
            dh = _matmul(dproj, w_in_t[l], "nn", F32, f"d_h_{l}")
            pending = [("scatter", _pack_grad_in(dw_in_t))]
            dx, dg_mix_pre, dd, dg_ffn_post_below = _norm_bwd(
                dx1, f"bwd_pre_mix_{l}", pre=(dh, s["x0"], vec(mix_pre_g, l)),
                post=(saved[l - 1]["dn"], vec(ffn_post_g, l - 1)))
            g_small[l] = jnp.concatenate([dg_mix_pre, dg_v, dg_mix_post, dg_ffn_pre, dg_ffn_post, db_s, db_f_row,
                                          zero_row, dw_s.reshape(CHUNK, D_MODEL)], axis=0)
            dg_ffn_post = dg_ffn_post_below
        else:
            dh, got = _matmul(dproj, w_in_t[l], "nn", F32, f"d_h_{l}", comm=[("scatter", _pack_grad_in(dw_in_t))])
            grad_sums[("in", 0)] = got
            dx, dg_mix_pre = _norm_bwd(dx1, f"bwd_pre_mix_{l}", pre=(dh, s["x0"], vec(mix_pre_g, l)))
            late_rows = jnp.concatenate([dg_mix_pre, db_f_row] + [zero_row] * 6, axis=0)
    grad_x = dx.reshape(x.shape)

    late = _sum_slots(_exchange("gather", late_rows, "gather_late_small_grads"), 8, "sum_late_small_grads")
    small_sum = _sum_slots(grad_sums["small"], S_ROWS, "sum_small_grads")
    loss = lax.psum(loss.reshape(()), ("x", "y", "c"))

    sm = small_sum.reshape(depth, S_ROWS, D_MODEL)[::-1]
    g = {
        "mix_pre_g": jnp.stack([late[0], sm[1, 0]]), "sgu_norm_g": sm[:, 1], "mix_post_g": sm[:, 2],
        "ffn_pre_g": sm[:, 3], "ffn_post_g": sm[:, 4], "b_spatial": sm[:, 5].reshape(depth, N_GROUPS, CHUNK),
        "b_forget": jnp.stack([late[1, :N_HEADS], sm[1, 6, :N_HEADS]]),
        "w_spatial": sm[:, 8:].reshape(depth, N_GROUPS, CHUNK, CHUNK),
    }
    g_in = jnp.stack([_sum_slots(grad_sums[("in", l)], 304, f"sum_grad_in_{l}") for l in range(depth)])
    g_rest = jnp.stack([_sum_slots(grad_sums[("rest", l)], 592, f"sum_grad_rest_{l}") for l in range(depth)])
    g["w_in"] = jnp.swapaxes(g_in[:, :SH_IN], 1, 2)
    g["w_out"] = g_rest[:, R_OUT:R_OUT + SH_OUT]
    g["w_gate"] = jnp.swapaxes(g_rest[:, R_G:R_G + SH_FF], 1, 2)
    g["w_up"] = jnp.swapaxes(g_rest[:, R_U:R_U + SH_FF], 1, 2)
    g["w_down"] = g_rest[:, R_D:R_D + SH_FF]

    names = ["mix_pre_g", "w_in", "b_forget", "sgu_norm_g", "w_spatial", "b_spatial", "w_out", "mix_post_g",
             "ffn_pre_g", "w_gate", "w_up", "w_down", "ffn_post_g"]
    ws = dict(mix_pre_g=mix_pre_g, w_in=w_in, b_forget=b_forget, sgu_norm_g=sgu_norm_g, w_spatial=w_spatial,
              b_spatial=b_spatial, w_out=w_out, mix_post_g=mix_post_g, ffn_pre_g=ffn_pre_g, w_gate=w_gate, w_up=w_up,
              w_down=w_down, ffn_post_g=ffn_post_g)
    ms = dict(mix_pre_g=m_mix_pre_g, w_in=m_w_in, b_forget=m_b_forget, sgu_norm_g=m_sgu_norm_g, w_spatial=m_w_spatial,
              b_spatial=m_b_spatial, w_out=m_w_out, mix_post_g=m_mix_post_g, ffn_pre_g=m_ffn_pre_g, w_gate=m_w_gate,
              w_up=m_w_up, w_down=m_w_down, ffn_post_g=m_ffn_post_g)
    vs = dict(mix_pre_g=v_mix_pre_g, w_in=v_w_in, b_forget=v_b_forget, sgu_norm_g=v_sgu_norm_g, w_spatial=v_w_spatial,
              b_spatial=v_b_spatial, w_out=v_w_out, mix_post_g=v_mix_post_g, ffn_pre_g=v_ffn_pre_g, w_gate=v_w_gate,
              w_up=v_w_up, w_down=v_w_down, ffn_post_g=v_ffn_post_g)
    deltas, new_m, new_v = [], [], []
    for n in names:
        shape = ws[n].shape
        flat = (-1, shape[-1])
        d_w, n_m, n_v = _adamw(ws[n].reshape(flat), g[n].reshape(flat), ms[n].reshape(flat), vs[n].reshape(flat),
                               f"adamw_{n}")
        deltas.append(d_w.reshape(shape))
        new_m.append(n_m.reshape(shape))
        new_v.append(n_v.reshape(shape))
    grads = [g[n].reshape(ws[n].shape) for n in names]
    return (loss, grad_x, *grads, *deltas, *new_m, *new_v)
```

```python
import math

import jax
import jax.numpy as jnp
from jax import lax
from jax.experimental import pallas as pl
from jax.experimental.pallas import tpu as pltpu

F32 = jnp.float32
BF16 = jnp.bfloat16

N_DEV = 8
D_MODEL = 1024
N_HEADS = 8
HEAD_DIM = 128
CHUNK = 128
N_GROUPS = 8
D_FF = 2816
IN_WIDTH = 7 * D_MODEL + N_HEADS
IN_PAD = 7680
EPS = 1e-6
GATED = 4 * D_MODEL
F_WIDTH = 512
QKV_WIDTH = 3 * HEAD_DIM
CB_F = GATED // HEAD_DIM
CB_QKV = (GATED + F_WIDTH) // HEAD_DIM
assert GATED + F_WIDTH + N_HEADS * QKV_WIDTH == IN_PAD and (GATED + F_WIDTH) % QKV_WIDTH == 0

ADAM_LR, ADAM_B1, ADAM_B2, ADAM_EPS, ADAM_WD, ADAM_STEP = 0.001, 0.9, 0.999, 1e-08, 0.01, 10

SH_IN = IN_WIDTH // N_DEV
SH_OUT = D_MODEL // N_DEV
SH_FF = D_FF // N_DEV
P_IN = 912
R_OUT, R_G, R_U, R_D = 0, SH_OUT, SH_OUT + SH_FF, SH_OUT + 2 * SH_FF
P_REST = SH_OUT + 3 * SH_FF
S_ROWS = 136

ATTN_BLOCK = 512
ROW_BLOCK = 512
MM_TM, MM_TN, MM_TK = 1536, 1536, 2048
VMEM_CAP = 56 << 20
NEG = -1e30


def _tile(n, cap, unit=128):
    if n <= cap:
        return n
    best = None
    for t in range(unit, cap + 1, unit):
        if n % t == 0:
            best = t
    assert best is not None, (n, cap)
    return best


def _nbytes(shape, dtype):
    return math.prod(shape) * jnp.dtype(dtype).itemsize


_HBM = pl.BlockSpec(memory_space=pltpu.HBM)
MESH = pl.DeviceIdType.MESH

_N_REMOTE = {"gather": 7, "halves": 1, "chips": 3}


def _exchange_shapes(kind, x):
    if kind == "gather":
        return [jax.ShapeDtypeStruct((N_DEV,) + x.shape, x.dtype)]
    if kind == "halves":
        return [jax.ShapeDtypeStruct(x.shape[1:], x.dtype)] * 2
    return [jax.ShapeDtypeStruct(x.shape, x.dtype)]


def _exchange_sems(kind):
    n = _N_REMOTE[kind]
    return [pltpu.SemaphoreType.DMA((n,)), pltpu.SemaphoreType.DMA((n,)), pltpu.SemaphoreType.DMA]


def _plan(kind, x_ref, outs, sems):
    send_sems, recv_sems, local_sem = sems
    x, y, c = lax.axis_index("x"), lax.axis_index("y"), lax.axis_index("c")

    def remote(src, dst, k, to):
        return pltpu.make_async_remote_copy(src_ref=src, dst_ref=dst, send_sem=send_sems.at[k], recv_sem=recv_sems.at[k],
                                            device_id=to, device_id_type=MESH)

    sibling = (x, y, 1 - c)
    if kind == "gather":
        out, = outs
        slot = lambda px, py, pc: out.at[4 * px + 2 * py + pc]
        chips = [(1 - x, y), (x, 1 - y), (1 - x, 1 - y)]
        local = [pltpu.make_async_copy(x_ref, slot(x, y, c), local_sem)]
        first = [remote(x_ref, slot(x, y, c), 0, sibling)]
        first += [remote(x_ref, slot(x, y, c), 1 + k, (*chip, c)) for k, chip in enumerate(chips)]
        relays = [(remote(x_ref, slot(*chip, c), 1 + k, (*chip, c)), remote(slot(*chip, c), slot(*chip, c), 4 + k, sibling))
                  for k, chip in enumerate(chips)]
        arrivals = [remote(x_ref, slot(x, y, 1 - c), 0, sibling)]
        arrivals += [remote(x_ref, slot(*chip, 1 - c), 4 + k, sibling) for k, chip in enumerate(chips)]
        return local, first, relays, arrivals
    if kind == "halves":
        got, own = outs
        to_sibling = remote(x_ref.at[1 - c], got, 0, sibling)
        return [pltpu.make_async_copy(x_ref.at[c], own, local_sem)], [to_sibling], [], [to_sibling]
    out, = outs
    mine = 2 * x + y
    first, arrivals = [], []
    for j in (1, 2, 3):
        px, py = (1 - x if j & 2 else x), (1 - y if j & 1 else y)
        theirs = 2 * px + py
        first.append(remote(x_ref.at[theirs], out.at[mine], j - 1, (px, py, c)))
        arrivals.append(remote(x_ref.at[theirs], out.at[theirs], j - 1, (px, py, c)))
    return [pltpu.make_async_copy(x_ref.at[mine], out.at[mine], local_sem)], first, [], arrivals


def _start(plan):
    local, first, _, _ = plan
    for cp in local + first:
        cp.start()


def _finish(plan):
    local, first, relays, arrivals = plan
    for arrival, onward in relays:
        arrival.wait_recv()
        onward.start()
    for cp in arrivals:
        cp.wait_recv()
    for cp in first + [onward for _, onward in relays]:
        cp.wait_send()
    for cp in local:
        cp.wait()


def _exchange(kind, x, name):
    n_out = len(_exchange_shapes(kind, x))

    def body(x_ref, *refs):
        plan = _plan(kind, x_ref, refs[:n_out], refs[n_out:])
        _start(plan)
        _finish(plan)

    return pl.pallas_call(body, name=name, out_shape=_exchange_shapes(kind, x), in_specs=[_HBM],
                          out_specs=[_HBM] * n_out, scratch_shapes=_exchange_sems(kind))(x)


def _call(body, name, grid, in_specs, out_specs, out_shape, args, scratch=(), sem=None, nbytes=0, aliases=None,
          comm=()):
    in_specs, out_specs, out_shape, args, scratch = (list(in_specs), list(out_specs), list(out_shape), list(args),
                                                     list(scratch))
    if comm:
        n_in, n_out, n_scr, n_ops = len(args), len(out_shape), len(scratch), len(comm)
        kinds = [kind for kind, _ in comm]
        shapes = [_exchange_shapes(kind, x) for kind, x in comm]
        inner = body

        def body(*refs):
            ins, refs = refs[:n_in], refs[n_in:]
            cins, refs = refs[:n_ops], refs[n_ops:]
            outs, refs = refs[:n_out], refs[n_out:]
            couts = []
            for sh in shapes:
                couts.append(refs[:len(sh)])
                refs = refs[len(sh):]
            own_scratch, sems = refs[:n_scr], refs[n_scr:]
            first = last = None
            for axis, size in enumerate(grid):
                at_start, at_end = pl.program_id(axis) == 0, pl.program_id(axis) == size - 1
                first = at_start if first is None else first & at_start
                last = at_end if last is None else last & at_end
            plans = [_plan(kinds[o], cins[o], couts[o], sems[3 * o:3 * o + 3]) for o in range(n_ops)]

            @pl.when(first)
            def _():
                for p in plans:
                    _start(p)

            inner(*ins, *outs, *own_scratch)

            @pl.when(last)
            def _():
                for p in plans:
                    _finish(p)

        in_specs += [_HBM] * n_ops
        args += [x for _, x in comm]
        for kind, sh in zip(kinds, shapes):
            out_shape += sh
            out_specs += [_HBM] * len(sh)
            scratch += _exchange_sems(kind)
        sem = ("arbitrary",) * len(grid)
    limit = int(min(max(2 * nbytes + (8 << 20), 32 << 20), VMEM_CAP))
    return pl.pallas_call(
        body, name=name, grid=grid, in_specs=in_specs, out_specs=out_specs, out_shape=out_shape,
        scratch_shapes=scratch, input_output_aliases=aliases or {},
        compiler_params=pltpu.CompilerParams(dimension_semantics=sem, vmem_limit_bytes=limit),
    )(*args)


def _sum_slots(x, tr, name):
    n, r, c = x.shape

    def body(x_ref, o_ref):
        acc = x_ref[0].astype(F32)
        for d in range(1, n):
            acc = acc + x_ref[d].astype(F32)
        o_ref[...] = acc

    return _call(body, name, (r // tr,), [pl.BlockSpec((n, tr, c), lambda i: (0, i, 0))],
                 [pl.BlockSpec((tr, c), lambda i: (i, 0))], [jax.ShapeDtypeStruct((r, c), F32)], [x],
                 sem=("parallel",), nbytes=_nbytes((n, tr, c), x.dtype) + _nbytes((tr, c), F32))[0]


def _add_halves(own, got, tr, name):
    n, r, c = own.shape

    def body(a_ref, b_ref, o_ref):
        o_ref[...] = (a_ref[...].astype(F32) + b_ref[...].astype(F32)).astype(o_ref.dtype)

    spec = pl.BlockSpec((n, tr, c), lambda i: (0, i, 0))
    return _call(body, name, (r // tr,), [spec, spec], [spec], [jax.ShapeDtypeStruct(own.shape, own.dtype)],
                 [own, got], sem=("parallel",), nbytes=3 * _nbytes((n, tr, c), own.dtype))[0]


_DIMS = {"nn": (((1,), (0,)), ((), ())), "nt": (((1,), (1,)), ((), ())), "tn": (((0,), (0,)), ((), ()))}


def _matmul(a, b, mode, out_dtype, name, comm=()):
    if mode == "nn":
        (m, k), (k2, n) = a.shape, b.shape
    elif mode == "nt":
        (m, k), (n, k2) = a.shape, b.shape
    else:
        (k, m), (k2, n) = a.shape, b.shape
    assert k == k2, (a.shape, b.shape, mode)
    tm, tn, tk = _tile(m, MM_TM), _tile(n, MM_TN), _tile(k, MM_TK)
    nk = k // tk
    dims = _DIMS[mode]
    a_spec = {"nn": pl.BlockSpec((tm, tk), lambda i, j, kk: (i, kk)),
              "nt": pl.BlockSpec((tm, tk), lambda i, j, kk: (i, kk)),
              "tn": pl.BlockSpec((tk, tm), lambda i, j, kk: (kk, i))}[mode]
    b_spec = {"nn": pl.BlockSpec((tk, tn), lambda i, j, kk: (kk, j)),
              "nt": pl.BlockSpec((tn, tk), lambda i, j, kk: (j, kk)),
              "tn": pl.BlockSpec((tk, tn), lambda i, j, kk: (kk, j))}[mode]

    def partial_product(a_ref, b_ref):
        return lax.dot_general(a_ref[...].astype(BF16), b_ref[...].astype(BF16), dims, preferred_element_type=F32)

    if nk == 1:
        def body(a_ref, b_ref, o_ref):
            o_ref[...] = partial_product(a_ref, b_ref).astype(o_ref.dtype)
        scratch = []
    else:
        def body(a_ref, b_ref, o_ref, acc_ref):
            kk = pl.program_id(2)

            @pl.when(kk == 0)
            def _():
                acc_ref[...] = jnp.zeros_like(acc_ref)

            acc_ref[...] += partial_product(a_ref, b_ref)

            @pl.when(kk == nk - 1)
            def _():
                o_ref[...] = acc_ref[...].astype(o_ref.dtype)
        scratch = [pltpu.VMEM((tm, tn), F32)]

    nbytes = (_nbytes((tm, tk), a.dtype) + _nbytes((tk, tn), b.dtype) + _nbytes((tm, tn), out_dtype)
              + _nbytes((tm, tn), F32))
    res = _call(body, name, (m // tm, n // tn, nk), [a_spec, b_spec],
                [pl.BlockSpec((tm, tn), lambda i, j, kk: (i, j))], [jax.ShapeDtypeStruct((m, n), out_dtype)], [a, b],
                scratch=scratch, sem=("parallel", "parallel", "arbitrary"), nbytes=nbytes, comm=comm)
    return res if comm else res[0]


def _rms(x):
    return lax.rsqrt(jnp.mean(x * x, axis=-1, keepdims=True) + EPS)


def _rms_bwd(dz, a, g):
    r = _rms(a)
    dzg = dz * g
    da = r * dzg - a * (r * r * r) * jnp.mean(dzg * a, axis=-1, keepdims=True)
    return da, dz * (a * r)


def _row_spec(tr, width):
    return pl.BlockSpec((tr, width), lambda i: (i, 0))


def _vec_spec(width):
    return pl.BlockSpec((1, width), lambda i: (0, 0))


def _rms_fwd(x, g, name):
    t, d = x.shape
    tr = min(ROW_BLOCK, t)

    def body(x_ref, g_ref, h_ref):
        xv = x_ref[...]
        h_ref[...] = ((xv * _rms(xv)) * g_ref[...]).astype(BF16)

    return _call(body, name, (t // tr,), [_row_spec(tr, d), _vec_spec(d)], [_row_spec(tr, d)],
                 [jax.ShapeDtypeStruct((t, d), BF16)], [x, g], sem=("parallel",), nbytes=3 * _nbytes((tr, d), F32))[0]


def _post_norm_fwd(x, o, g_post, g_next, name):
    t, d = x.shape
    tr = min(ROW_BLOCK, t)

    def body(x_ref, o_ref, gp_ref, gn_ref, xn_ref, h_ref):
        ov = o_ref[...]
        xn = x_ref[...] + (ov * _rms(ov)) * gp_ref[...]
        xn_ref[...] = xn
        h_ref[...] = ((xn * _rms(xn)) * gn_ref[...]).astype(BF16)

    return _call(body, name, (t // tr,), [_row_spec(tr, d), _row_spec(tr, d), _vec_spec(d), _vec_spec(d)],
                 [_row_spec(tr, d), _row_spec(tr, d)],
                 [jax.ShapeDtypeStruct((t, d), F32), jax.ShapeDtypeStruct((t, d), BF16)], [x, o, g_post, g_next],
                 sem=("parallel",), nbytes=5 * _nbytes((tr, d), F32))


def _post_norm_loss(x, o, g_post, target, name):
    t, d = x.shape
    tr = min(ROW_BLOCK, t)

    def body(x_ref, o_ref, gp_ref, t_ref, dy_ref, loss_ref):
        ov = o_ref[...]
        err = x_ref[...] + (ov * _rms(ov)) * gp_ref[...] - t_ref[...]
        dy_ref[...] = err / d
        part = 0.5 * jnp.sum(jnp.mean(err * err, axis=-1, keepdims=True), axis=0, keepdims=True)

        @pl.when(pl.program_id(0) == 0)
        def _():
            loss_ref[...] = jnp.zeros_like(loss_ref)

        loss_ref[...] += part

    return _call(body, name, (t // tr,), [_row_spec(tr, d), _row_spec(tr, d), _vec_spec(d), _row_spec(tr, d)],
                 [_row_spec(tr, d), pl.BlockSpec((1, 1), lambda i: (0, 0))],
                 [jax.ShapeDtypeStruct((t, d), F32), jax.ShapeDtypeStruct((1, 1), F32)], [x, o, g_post, target],
                 sem=("arbitrary",), nbytes=5 * _nbytes((tr, d), F32))


def _norm_bwd(dskip, name, pre=None, post=None):
    t, d = dskip.shape
    tr = min(ROW_BLOCK, t)
    n_in = 1 + (3 if pre else 0) + (2 if post else 0)

    def body(*refs):
        ins, outs = list(refs[:n_in]), list(refs[n_in:])
        first = pl.program_id(0) == 0
        dx = ins.pop(0)[...]
        if pre:
            dh_ref, xin_ref, gpre_ref = ins.pop(0), ins.pop(0), ins.pop(0)
            dxin, dg_rows = _rms_bwd(dh_ref[...].astype(F32), xin_ref[...], gpre_ref[...])
            dx = dx + dxin
            dx_ref, dgpre_ref = outs.pop(0), outs.pop(0)
            dx_ref[...] = dx

            @pl.when(first)
            def _():
                dgpre_ref[...] = jnp.zeros_like(dgpre_ref)

            dgpre_ref[...] += jnp.sum(dg_rows, axis=0, keepdims=True)
        if post:
            a_ref, gpost_ref = ins.pop(0), ins.pop(0)
            da, dg_rows = _rms_bwd(dx, a_ref[...], gpost_ref[...])
            da_ref, dgpost_ref = outs.pop(0), outs.pop(0)
            da_ref[...] = da.astype(BF16)

            @pl.when(first)
            def _():
                dgpost_ref[...] = jnp.zeros_like(dgpost_ref)

            dgpost_ref[...] += jnp.sum(dg_rows, axis=0, keepdims=True)

    args, in_specs, out_shape, out_specs = [dskip], [_row_spec(tr, d)], [], []
    if pre:
        args += list(pre)
        in_specs += [_row_spec(tr, d), _row_spec(tr, d), _vec_spec(d)]
        out_shape += [jax.ShapeDtypeStruct((t, d), F32), jax.ShapeDtypeStruct((1, d), F32)]
        out_specs += [_row_spec(tr, d), _vec_spec(d)]
    if post:
        args += list(post)
        in_specs += [_row_spec(tr, d), _vec_spec(d)]
        out_shape += [jax.ShapeDtypeStruct((t, d), BF16), jax.ShapeDtypeStruct((1, d), F32)]
        out_specs += [_row_spec(tr, d), _vec_spec(d)]
    return _call(body, name, (t // tr,), in_specs, out_specs, out_shape, args, sem=("arbitrary",),
                 nbytes=7 * _nbytes((tr, d), F32))


def _sigmoid(x):
    return 1.0 / (1.0 + jnp.exp(-x))


def _swiglu_fwd(gu, name):
    t = gu.shape[0]
    tr = min(ROW_BLOCK, t)

    def body(gu_ref, a_ref):
        gate, up = gu_ref[:, :D_FF], gu_ref[:, D_FF:]
        a_ref[...] = ((gate * _sigmoid(gate)) * up).astype(BF16)

    return _call(body, name, (t // tr,), [_row_spec(tr, 2 * D_FF)], [_row_spec(tr, D_FF)],
                 [jax.ShapeDtypeStruct((t, D_FF), BF16)], [gu], sem=("parallel",),
                 nbytes=2 * _nbytes((tr, 2 * D_FF), F32))[0]


def _swiglu_bwd(gu, dact, name):
    t = gu.shape[0]
    tr = min(ROW_BLOCK, t)

    def body(gu_ref, da_ref, o_ref):
        gate, up = gu_ref[:, :D_FF], gu_ref[:, D_FF:]
        da = da_ref[...]
        s = _sigmoid(gate)
        o_ref[:, :D_FF] = (da * up * (s * (1.0 + gate * (1.0 - s)))).astype(BF16)
        o_ref[:, D_FF:] = (da * (gate * s)).astype(BF16)

    return _call(body, name, (t // tr,), [_row_spec(tr, 2 * D_FF), _row_spec(tr, D_FF)], [_row_spec(tr, 2 * D_FF)],
                 [jax.ShapeDtypeStruct((t, 2 * D_FF), BF16)], [gu, dact], sem=("parallel",),
                 nbytes=3 * _nbytes((tr, 2 * D_FF), F32))[0]


def _split3(x):
    hi = x.astype(BF16)
    r1 = x - hi.astype(F32)
    mid = r1.astype(BF16)
    lo = (r1 - mid.astype(F32)).astype(BF16)
    return hi, mid, lo


def _dot_exact(mat01, x):
    hi, mid, lo = _split3(x)
    out = jnp.dot(mat01, hi, preferred_element_type=F32)
    out = out + jnp.dot(mat01, mid, preferred_element_type=F32)
    return out + jnp.dot(mat01, lo, preferred_element_type=F32)


def _dot_exact_rhs(x, mat01):
    hi, mid, lo = _split3(x)
    out = jnp.dot(hi, mat01, preferred_element_type=F32)
    out = out + jnp.dot(mid, mat01, preferred_element_type=F32)
    return out + jnp.dot(lo, mat01, preferred_element_type=F32)


def _tri(lower):
    r = lax.broadcasted_iota(jnp.int32, (CHUNK, CHUNK), 0)
    c = lax.broadcasted_iota(jnp.int32, (CHUNK, CHUNK), 1)
    return jnp.where(r >= c if lower else r <= c, 1.0, 0.0).astype(BF16)


def _log_sigmoid(z):
    return jnp.minimum(z, 0.0) - jnp.log(1.0 + jnp.exp(-jnp.abs(z)))


def _forget_cumsum(proj, b_pad, name):
    t = proj.shape[0]
    nb = t // CHUNK

    def body(f_ref, b_ref, c_ref):
        tri = _tri(True)
        b = b_ref[...]

        def blk(i, carry):
            rows = pl.ds(pl.multiple_of(i * CHUNK, CHUNK), CHUNK)
            cs = _dot_exact(tri, _log_sigmoid(f_ref[rows, :] + b)) + carry
            c_ref[rows, :] = cs
            return cs[CHUNK - 1:CHUNK, :]

        lax.fori_loop(0, nb, blk, jnp.zeros((1, HEAD_DIM), F32))

    return _call(body, name, (1,),
                 [pl.BlockSpec((t, HEAD_DIM), lambda i: (0, CB_F)), pl.BlockSpec((1, HEAD_DIM), lambda i: (0, 0))],
                 [pl.BlockSpec((t, HEAD_DIM), lambda i: (0, 0))], [jax.ShapeDtypeStruct((t, HEAD_DIM), F32)],
                 [proj, b_pad], sem=("arbitrary",), nbytes=2 * _nbytes((t, HEAD_DIM), F32))[0]


def _forget_bwd(proj, b_pad, dc_key, dc_query, dproj, name):
    t = proj.shape[0]
    tr = min(ROW_BLOCK, t)
    nb = t // tr
    rev = lambda i: nb - 1 - i

    def body(f_ref, b_ref, dck_ref, dcq_ref, dproj_in, df_ref, db_ref, run_ref):
        del dproj_in

        @pl.when(pl.program_id(0) == 0)
        def _():
            run_ref[...] = jnp.zeros_like(run_ref)
            db_ref[...] = jnp.zeros_like(db_ref)

        tri = _tri(False)
        b = b_ref[...]
        lane = lax.broadcasted_iota(jnp.int32, (CHUNK, HEAD_DIM), 1)
        df_ref[...] = jnp.zeros_like(df_ref)
        for n in reversed(range(tr // CHUNK)):
            rows = slice(n * CHUNK, (n + 1) * CHUNK)
            dc = dck_ref[rows, :]
            for h in range(N_HEADS):
                dc = dc + jnp.where(lane == h, dcq_ref[rows, h * HEAD_DIM:(h + 1) * HEAD_DIM], 0.0)
            dlogf = _dot_exact(tri, dc) + run_ref[...]
            run_ref[...] = dlogf[0:1, :]
            z = f_ref[rows, :] + b
            e = jnp.exp(-jnp.abs(z))
            sig_neg = jnp.where(z >= 0.0, e, 1.0) / (1.0 + e)
            df = jnp.where(lane < N_HEADS, dlogf * sig_neg, 0.0)
            df_ref[rows, :HEAD_DIM] = df.astype(BF16)
            db_ref[...] += jnp.sum(df, axis=0, keepdims=True)

    return _call(body, name, (nb,),
                 [pl.BlockSpec((tr, HEAD_DIM), lambda i: (rev(i), CB_F)), pl.BlockSpec((1, HEAD_DIM), lambda i: (0, 0)),
                  pl.BlockSpec((tr, HEAD_DIM), lambda i: (rev(i), 0)), pl.BlockSpec((tr, D_MODEL), lambda i: (rev(i), 0)),
                  pl.BlockSpec(memory_space=pl.ANY)],
                 [pl.BlockSpec((tr, F_WIDTH), lambda i: (rev(i), GATED // F_WIDTH)),
                  pl.BlockSpec((1, HEAD_DIM), lambda i: (0, 0))],
                 [jax.ShapeDtypeStruct(dproj.shape, dproj.dtype), jax.ShapeDtypeStruct((1, HEAD_DIM), F32)],
                 [proj, b_pad, dc_key, dc_query, dproj], scratch=[pltpu.VMEM((1, HEAD_DIM), F32)],
                 sem=("arbitrary",), nbytes=4 * _nbytes((tr, D_MODEL), F32), aliases={4: 0})


def _causal(block):
    r = lax.broadcasted_iota(jnp.int32, (block, block), 0)
    c = lax.broadcasted_iota(jnp.int32, (block, block), 1)
    return c <= r


def _lanes(x, width):
    return jnp.concatenate([x] * (width // HEAD_DIM), axis=1)


def _attn_fwd(proj, c_pad, c_row, name, comm=()):
    t = proj.shape[0]
    blk = min(ATTN_BLOCK, t)
    nq = t // blk
    scale = HEAD_DIM ** -0.5

    def body(q_ref, k_ref, v_ref, cp_ref, cr_ref, y_ref, e_ref):
        h, i = pl.program_id(0), pl.program_id(1)
        q = (q_ref[...] * scale).astype(BF16)
        lane = lax.broadcasted_iota(jnp.int32, (blk, HEAD_DIM), 1)
        ci = jnp.sum(jnp.where(lane == h, cp_ref[...], 0.0), axis=1, keepdims=True)

        def step(j, carry, diagonal):
            m, l, acc = carry
            off = pl.multiple_of(j * blk, blk)
            k = k_ref[pl.ds(off, blk), :].astype(BF16)
            v = v_ref[pl.ds(off, blk), :].astype(BF16)
            s = lax.dot_general(q, k, _DIMS["nt"], preferred_element_type=F32)
            s = s + ci - cr_ref[:, pl.ds(off, blk)]
            if diagonal:
                s = jnp.where(_causal(blk), s, NEG)
            m_new = jnp.maximum(m, jnp.max(s, axis=1, keepdims=True))
            alpha = jnp.exp(m - m_new)
            p = jnp.exp(s - m_new)
            l = alpha * l + jnp.sum(p, axis=1, keepdims=True)
            acc = alpha * acc + jnp.dot(p.astype(BF16), v, preferred_element_type=F32)
            return m_new, l, acc

        init = (jnp.full((blk, 1), NEG, F32), jnp.zeros((blk, 1), F32), jnp.zeros((blk, HEAD_DIM), F32))
        carry = lax.fori_loop(0, i, lambda j, c: step(j, c, False), init)
        m, l, acc = step(i, carry, True)
        y_ref[...] = acc / l
        e_ref[...] = jnp.broadcast_to(ci - (m + jnp.log(l)), (blk, HEAD_DIM))

    nbytes = 2 * _nbytes((t, HEAD_DIM), F32) + 4 * _nbytes((blk, HEAD_DIM), F32) + 4 * _nbytes((blk, blk), F32)
    head_block = pl.BlockSpec((blk, HEAD_DIM), lambda h, i: (i, h))
    return _call(body, name, (N_HEADS, nq),
                 [pl.BlockSpec((blk, HEAD_DIM), lambda h, i: (i, CB_QKV + 3 * h)),
                  pl.BlockSpec((t, HEAD_DIM), lambda h, i: (0, CB_QKV + 3 * h + 1)),
                  pl.BlockSpec((t, HEAD_DIM), lambda h, i: (0, CB_QKV + 3 * h + 2)),
                  pl.BlockSpec((blk, HEAD_DIM), lambda h, i: (i, 0)),
                  pl.BlockSpec((None, 1, t), lambda h, i: (h, 0, 0))],
                 [head_block, head_block],
                 [jax.ShapeDtypeStruct((t, D_MODEL), F32), jax.ShapeDtypeStruct((t, D_MODEL), F32)],
                 [proj, proj, proj, c_pad, c_row], sem=("parallel", "arbitrary"), nbytes=nbytes, comm=comm)


def _attn_bwd(proj, dy, e_b, delta_b, c_row, dproj, name, comm=()):
    t = proj.shape[0]
    blk = min(ATTN_BLOCK, t)
    nb = t // blk
    scale = HEAD_DIM ** -0.5

    def body(q_ref, k_ref, v_ref, dy_ref, e_ref, dl_ref, cr_ref, dproj_in, dqkv_ref, dc_ref, dr_ref, dq_acc):
        del dproj_in
        j = pl.program_id(1)

        @pl.when(j == 0)
        def _():
            dq_acc[...] = jnp.zeros_like(dq_acc)
            dr_ref[...] = jnp.zeros_like(dr_ref)

        k = k_ref[...].astype(BF16)
        v = v_ref[...].astype(BF16)
        crow = cr_ref[...]

        def step(i, carry, diagonal):
            dk, dv, dcs = carry
            rows = pl.ds(pl.multiple_of(i * blk, blk), blk)
            q = (q_ref[rows, :] * scale).astype(BF16)
            dyi = dy_ref[rows, :].astype(BF16)
            s = lax.dot_general(q, k, _DIMS["nt"], preferred_element_type=F32) + _lanes(e_ref[rows, :], blk) - crow
            if diagonal:
                s = jnp.where(_causal(blk), s, NEG)
            p = jnp.exp(s)
            dv = dv + lax.dot_general(p.astype(BF16), dyi, _DIMS["tn"], preferred_element_type=F32)
            dp = lax.dot_general(dyi, v, _DIMS["nt"], preferred_element_type=F32)
            ds = p * (dp - _lanes(dl_ref[rows, :], blk))
            dsb = ds.astype(BF16)
            dk = dk + lax.dot_general(dsb, q, _DIMS["tn"], preferred_element_type=F32)
            dq_acc[rows, :] += jnp.dot(dsb, k, preferred_element_type=F32) * scale
            dr_ref[rows, :] += jnp.sum(ds, axis=1, keepdims=True)
            return dk, dv, dcs + jnp.sum(ds, axis=0, keepdims=True)

        zeros = jnp.zeros((blk, HEAD_DIM), F32)
        carry = step(j, (zeros, zeros, jnp.zeros((1, blk), F32)), True)
        dk, dv, dcs = lax.fori_loop(j + 1, nb, lambda i, c: step(i, c, False), carry)
        mine = pl.ds(pl.multiple_of(j * blk, blk), blk)
        dqkv_ref[:, :HEAD_DIM] = dq_acc[mine, :].astype(BF16)
        dqkv_ref[:, HEAD_DIM:2 * HEAD_DIM] = dk.astype(BF16)
        dqkv_ref[:, 2 * HEAD_DIM:] = dv.astype(BF16)
        dc_ref[...] = -dcs

    full = lambda cb: pl.BlockSpec((t, HEAD_DIM), lambda h, j: (0, cb(h)))
    head = lambda h: h
    nbytes = (6 * _nbytes((t, HEAD_DIM), F32) + 4 * _nbytes((blk, HEAD_DIM), F32) + 5 * _nbytes((blk, blk), F32))
    return _call(body, name, (N_HEADS, nb),
                 [full(lambda h: CB_QKV + 3 * h),
                  pl.BlockSpec((blk, HEAD_DIM), lambda h, j: (j, CB_QKV + 3 * h + 1)),
                  pl.BlockSpec((blk, HEAD_DIM), lambda h, j: (j, CB_QKV + 3 * h + 2)),
                  full(head), full(head), full(head),
                  pl.BlockSpec((None, 1, blk), lambda h, j: (h, 0, j)),
                  pl.BlockSpec(memory_space=pl.ANY)],
                 [pl.BlockSpec((blk, QKV_WIDTH), lambda h, j: (j, (GATED + F_WIDTH) // QKV_WIDTH + h)),
                  pl.BlockSpec((None, 1, blk), lambda h, j: (h, 0, j)),
                  full(head)],
                 [jax.ShapeDtypeStruct(dproj.shape, dproj.dtype), jax.ShapeDtypeStruct((N_HEADS, 1, t), F32),
                  jax.ShapeDtypeStruct((t, D_MODEL), F32)],
                 [proj, proj, proj, dy, e_b, delta_b, c_row, dproj], scratch=[pltpu.VMEM((t, HEAD_DIM), F32)],
                 sem=("parallel", "arbitrary"), nbytes=nbytes, aliases={7: 0}, comm=comm)


_GELU_K = math.sqrt(2.0 / math.pi)
_GELU_C = 0.044715


def _gelu(x):
    t = jnp.tanh(_GELU_K * (x + _GELU_C * (x * x * x)))
    return 0.5 * x * (1.0 + t), t


def _gelu_grad(x, t):
    return 0.5 * (1.0 + t) + 0.5 * x * (1.0 - t * t) * (_GELU_K * (1.0 + 3.0 * _GELU_C * (x * x)))


def _layernorm_stats(a):
    mu = jnp.mean(a, axis=-1, keepdims=True)
    xc = a - mu
    r = lax.rsqrt(jnp.mean(xc * xc, axis=-1, keepdims=True) + EPS)
    return xc * r, r


def _group(g):
    return slice(g * CHUNK, (g + 1) * CHUNK)


def _gated_cols(k):
    return pl.BlockSpec((CHUNK, D_MODEL), lambda i: (i, k))


def _fixed(shape):
    return pl.BlockSpec(shape, lambda i: (0,) * len(shape))


def _mix_fwd(proj, y_b, w_s, b_cols, g_v, name):
    t = proj.shape[0]

    def body(u_ref, vs_ref, ga_ref, gb_ref, yb_ref, w_ref, b_ref, gv_ref, o_ref):
        a_u, _ = _gelu(u_ref[...])
        a_v, _ = _gelu(vs_ref[...])
        xhat, _ = _layernorm_stats(a_v)
        vn = (xhat * gv_ref[...]).astype(BF16)
        sa, sb = _sigmoid(ga_ref[...]), _sigmoid(gb_ref[...])
        yb = yb_ref[...]
        mask = _causal(CHUNK)
        for g in range(N_GROUPS):
            cols = _group(g)
            w = jnp.where(mask, w_ref[g], 0.0).astype(BF16)
            mixed = jnp.dot(w, vn[:, cols], preferred_element_type=F32) + b_ref[:, g:g + 1]
            o_ref[:, cols] = (sa[:, cols] * (a_u[:, cols] * mixed) + sb[:, cols] * yb[:, cols]).astype(BF16)

    return _call(body, name, (t // CHUNK,),
                 [_gated_cols(0), _gated_cols(1), _gated_cols(2), _gated_cols(3), _row_spec(CHUNK, D_MODEL),
                  _fixed((N_GROUPS, CHUNK, CHUNK)), _fixed((CHUNK, N_GROUPS)), _fixed((1, D_MODEL))],
                 [_row_spec(CHUNK, D_MODEL)], [jax.ShapeDtypeStruct((t, D_MODEL), BF16)],
                 [proj, proj, proj, proj, y_b, w_s, b_cols, g_v], sem=("parallel",),
                 nbytes=16 * _nbytes((CHUNK, D_MODEL), F32))[0]


def _mix_bwd(proj, y_b, dmerged, w_s, b_cols, g_v, sel, name, comm=()):
    t = proj.shape[0]

    def body(u_ref, vs_ref, ga_ref, gb_ref, yb_ref, dm_ref, w_ref, b_ref, gv_ref, sel_ref,
             dg_ref, dyb_ref, delta_ref, dw_ref, dbt_ref, dgv_ref):
        @pl.when(pl.program_id(0) == 0)
        def _():
            dw_ref[...] = jnp.zeros_like(dw_ref)
            dbt_ref[...] = jnp.zeros_like(dbt_ref)
            dgv_ref[...] = jnp.zeros_like(dgv_ref)

        u, vs = u_ref[...], vs_ref[...]
        a_u, t_u = _gelu(u)
        a_v, t_v = _gelu(vs)
        xhat, r = _layernorm_stats(a_v)
        gv = gv_ref[...]
        vn = (xhat * gv).astype(BF16)
        sa, sb = _sigmoid(ga_ref[...]), _sigmoid(gb_ref[...])
        yb, dm = yb_ref[...], dm_ref[...].astype(F32)
        dyb = dm * sb
        dyb_ref[...] = dyb
        dyb_yb = dyb * yb
        dg_ref[:, 3 * D_MODEL:] = (dm * yb * (sb * (1.0 - sb))).astype(BF16)
        dya = dm * sa
        mask = _causal(CHUNK)
        dmixed_parts, dvn_parts = [], []
        for g in range(N_GROUPS):
            cols = _group(g)
            delta_ref[:, cols] = jnp.broadcast_to(jnp.sum(dyb_yb[:, cols], axis=1, keepdims=True), (CHUNK, CHUNK))
            w = jnp.where(mask, w_ref[g], 0.0).astype(BF16)
            mixed = jnp.dot(w, vn[:, cols], preferred_element_type=F32) + b_ref[:, g:g + 1]
            y_a = a_u[:, cols] * mixed
            dg_ref[:, 2 * D_MODEL + g * CHUNK:2 * D_MODEL + (g + 1) * CHUNK] = (
                dm[:, cols] * y_a * (sa[:, cols] * (1.0 - sa[:, cols]))).astype(BF16)
            dg_ref[:, cols] = (dya[:, cols] * mixed * _gelu_grad(u[:, cols], t_u[:, cols])).astype(BF16)
            dmixed = dya[:, cols] * a_u[:, cols]
            dmb = dmixed.astype(BF16)
            dw = lax.dot_general(dmb, vn[:, cols], _DIMS["nt"], preferred_element_type=F32)
            dw_ref[g] += jnp.where(mask, dw, 0.0)
            dvn_parts.append(lax.dot_general(w, dmb, _DIMS["tn"], preferred_element_type=F32))
            dmixed_parts.append(dmixed)
        dmixed_all = jnp.concatenate(dmixed_parts, axis=1)
        dvn = jnp.concatenate(dvn_parts, axis=1)
        dbt_ref[...] += _dot_exact_rhs(dmixed_all, sel_ref[...])
        dgv_ref[...] += jnp.sum(dvn * xhat, axis=0, keepdims=True)
        dxh = dvn * gv
        da_v = r * (dxh - jnp.mean(dxh, axis=-1, keepdims=True)
                    - xhat * jnp.mean(dxh * xhat, axis=-1, keepdims=True))
        dg_ref[:, D_MODEL:2 * D_MODEL] = (da_v * _gelu_grad(vs, t_v)).astype(BF16)

    row = lambda width: _row_spec(CHUNK, width)
    return _call(body, name, (t // CHUNK,),
                 [_gated_cols(0), _gated_cols(1), _gated_cols(2), _gated_cols(3), row(D_MODEL), row(D_MODEL),
                  _fixed((N_GROUPS, CHUNK, CHUNK)), _fixed((CHUNK, N_GROUPS)), _fixed((1, D_MODEL)),
                  _fixed((D_MODEL, HEAD_DIM))],
                 [row(GATED), row(D_MODEL), row(D_MODEL),
                  _fixed((N_GROUPS, CHUNK, CHUNK)), _fixed((CHUNK, HEAD_DIM)), _fixed((1, D_MODEL))],
                 [jax.ShapeDtypeStruct((t, IN_PAD), BF16), jax.ShapeDtypeStruct((t, D_MODEL), F32),
                  jax.ShapeDtypeStruct((t, D_MODEL), F32), jax.ShapeDtypeStruct((N_GROUPS, CHUNK, CHUNK), F32),
                  jax.ShapeDtypeStruct((CHUNK, HEAD_DIM), F32), jax.ShapeDtypeStruct((1, D_MODEL), F32)],
                 [proj, proj, proj, proj, y_b, dmerged, w_s, b_cols, g_v, sel], sem=("arbitrary",),
                 nbytes=40 * _nbytes((CHUNK, D_MODEL), F32), comm=comm)


def _adamw(w, g, m, v, name):
    r, c = w.shape
    tr = _tile(r, 256, unit=8)

    def body(w_ref, g_ref, m_ref, v_ref, d_ref, nm_ref, nv_ref):
        gv = g_ref[...]
        nm = ADAM_B1 * m_ref[...] + (1.0 - ADAM_B1) * gv
        nv = ADAM_B2 * v_ref[...] + (1.0 - ADAM_B2) * (gv * gv)
        m_hat = nm / (1.0 - ADAM_B1 ** ADAM_STEP)
        v_hat = nv / (1.0 - ADAM_B2 ** ADAM_STEP)
        d_ref[...] = -ADAM_LR * (m_hat / (jnp.sqrt(v_hat) + ADAM_EPS) + ADAM_WD * w_ref[...])
        nm_ref[...] = nm
        nv_ref[...] = nv

    spec = pl.BlockSpec((tr, c), lambda i: (i, 0))
    shape = jax.ShapeDtypeStruct((r, c), F32)
    return _call(body, name, (r // tr,), [spec] * 4, [spec] * 3, [shape] * 3, [w, g, m, v], sem=("parallel",),
                 nbytes=7 * _nbytes((tr, max(c, 128)), F32))


def _pad_rows(a, rows):
    return jnp.pad(a, ((0, rows - a.shape[0]), (0, 0)))


def _pack_rest(w_out, w_gate, w_up, w_down):
    return jnp.concatenate([w_out, w_gate.T, w_up.T, w_down], axis=0).astype(BF16)


def _unpack_in(gathered):
    nat = gathered[:, :SH_IN, :].reshape(IN_WIDTH, D_MODEL)
    qkv = nat[2 * D_MODEL:5 * D_MODEL].reshape(3, N_HEADS, HEAD_DIM, D_MODEL)
    qkv = jnp.transpose(qkv, (1, 0, 2, 3)).reshape(3 * D_MODEL, D_MODEL)
    f = _pad_rows(nat[7 * D_MODEL:], F_WIDTH)
    return jnp.concatenate([nat[:2 * D_MODEL], nat[5 * D_MODEL:7 * D_MODEL], f, qkv], axis=0)


def _unpack_rest(gathered):
    take = lambda lo, n: gathered[:, lo:lo + n, :].reshape(N_DEV * n, D_MODEL)
    return take(R_OUT, SH_OUT), jnp.concatenate([take(R_G, SH_FF), take(R_U, SH_FF)], axis=0), take(R_D, SH_FF)


def _pack_grad_in(dw_in_t):
    qkv = dw_in_t[GATED + F_WIDTH:].reshape(N_HEADS, 3, HEAD_DIM, D_MODEL)
    qkv = jnp.transpose(qkv, (1, 0, 2, 3)).reshape(3 * D_MODEL, D_MODEL)
    nat = jnp.concatenate([dw_in_t[:2 * D_MODEL], qkv, dw_in_t[2 * D_MODEL:GATED], dw_in_t[GATED:GATED + N_HEADS]],
                          axis=0)
    by_dest = jnp.swapaxes(nat.reshape(N_DEV // 2, 2, SH_IN, D_MODEL), 0, 1)
    return jnp.pad(by_dest, ((0, 0), (0, 0), (0, P_IN - SH_IN), (0, 0)))


def _pack_grad_rest(dw_out, dw_gu_t, dw_down):
    split = lambda a, n: a.reshape(N_DEV // 2, 2, n, D_MODEL)
    rows = jnp.concatenate([split(dw_out, SH_OUT), split(dw_gu_t[:D_FF], SH_FF), split(dw_gu_t[D_FF:], SH_FF),
                            split(dw_down, SH_FF)], axis=2)
    return jnp.swapaxes(rows, 0, 1)


def kernel(x, mix_pre_g, w_in, b_forget, sgu_norm_g, w_spatial, b_spatial, w_out, mix_post_g, ffn_pre_g, w_gate, w_up, w_down, ffn_post_g, loss_target, m_mix_pre_g, m_w_in, m_b_forget, m_sgu_norm_g, m_w_spatial, m_b_spatial, m_w_out, m_mix_post_g, m_ffn_pre_g, m_w_gate, m_w_up, m_w_down, m_ffn_post_g, v_mix_pre_g, v_w_in, v_b_forget, v_sgu_norm_g, v_w_spatial, v_b_spatial, v_w_out, v_mix_post_g, v_ffn_pre_g, v_w_gate, v_w_up, v_w_down, v_ffn_post_g):
    depth = w_in.shape[0]
    assert depth == 2
    t = x.shape[1]
    x0 = x.reshape(t, D_MODEL)
    target = loss_target.reshape(t, D_MODEL)
    sel = (jnp.arange(D_MODEL)[:, None] // HEAD_DIM == jnp.arange(HEAD_DIM)[None, :]).astype(BF16)
    vec = lambda a, l: a[l][None, :]
    zero_row = jnp.zeros((1, D_MODEL), F32)

    p_in = [_pad_rows(w_in[l].T.astype(BF16), P_IN) for l in range(depth)]
    p_rest = [_pack_rest(w_out[l], w_gate[l], w_up[l], w_down[l]) for l in range(depth)]
    w_in_t = [None] * depth
    w_rest = [None] * depth
    w_in_t[0] = _unpack_in(_exchange("gather", p_in[0], "gather_w_in_0")[0])

    saved = []
    xl = x0
    h = _rms_fwd(xl, vec(mix_pre_g, 0), "rms_in_0")
    dy = loss = None
    for l in range(depth):
        b_pad = jnp.pad(b_forget[l], (0, HEAD_DIM - N_HEADS))[None, :]
        b_cols = b_spatial[l].T
        proj = _matmul(h, w_in_t[l], "nt", F32, f"proj_{l}")
        c_pad = _forget_cumsum(proj, b_pad, f"forget_cumsum_{l}")
        c_row = c_pad[:, :N_HEADS].T[:, None, :]
        riders = [("gather", p_rest[l])] + ([("gather", p_in[l + 1])] if l + 1 < depth else [])
        y_b, e_b, *arrived = _attn_fwd(proj, c_pad, c_row, f"attn_fwd_{l}", comm=riders)
        w_rest[l] = _unpack_rest(arrived[0])
        if l + 1 < depth:
            w_in_t[l + 1] = _unpack_in(arrived[1])
        w_o, w_gu_t, w_d = w_rest[l]
        merged = _mix_fwd(proj, y_b, w_spatial[l], b_cols, vec(sgu_norm_g, l), f"mix_fwd_{l}")
        o = _matmul(merged, w_o, "nn", F32, f"out_proj_{l}")
        x1, h2 = _post_norm_fwd(xl, o, vec(mix_post_g, l), vec(ffn_pre_g, l), f"post_mix_{l}")
        gu = _matmul(h2, w_gu_t, "nt", F32, f"ffn_gu_{l}")
        act = _swiglu_fwd(gu, f"swiglu_fwd_{l}")
        dn = _matmul(act, w_d, "nn", F32, f"ffn_down_{l}")
        saved.append(dict(x0=xl, h=h, proj=proj, b_pad=b_pad, b_cols=b_cols, c_row=c_row, y_b=y_b, e_b=e_b,
                          merged=merged, o=o, x1=x1, h2=h2, gu=gu, act=act, dn=dn))
        if l + 1 < depth:
            xl, h = _post_norm_fwd(x1, dn, vec(ffn_post_g, l), vec(mix_pre_g, l + 1), f"post_ffn_{l}")
        else:
            dy, loss = _post_norm_loss(x1, dn, vec(ffn_post_g, l), target, "loss")

    g_small = [None] * depth
    grad_sums = {}
    pending = []
    late_rows = None
    dx = dy
    dd, dg_ffn_post = _norm_bwd(dx, f"bwd_ffn_post_{depth - 1}", post=(saved[-1]["dn"], vec(ffn_post_g, depth - 1)))
    for l in reversed(range(depth)):
        s = saved[l]
        w_o, w_gu_t, w_d = w_rest[l]
        dact = _matmul(dd, w_d, "nt", F32, f"d_act_{l}")
        dw_d = _matmul(s["act"], dd, "tn", BF16, f"dw_down_{l}")
        dgu = _swiglu_bwd(s["gu"], dact, f"swiglu_bwd_{l}")
        dh2 = _matmul(dgu, w_gu_t, "nn", F32, f"d_h2_{l}")
        dw_gu_t = _matmul(dgu, s["h2"], "tn", BF16, f"dw_gu_{l}")
        dx1, dg_ffn_pre, do, dg_mix_post = _norm_bwd(
            dx, f"bwd_post_mix_{l}", pre=(dh2, s["x1"], vec(ffn_pre_g, l)), post=(s["o"], vec(mix_post_g, l)))
        dmerged = _matmul(do, w_o, "nt", F32, f"d_merged_{l}")
        dw_o = _matmul(s["merged"], do, "tn", BF16, f"dw_out_{l}")
        dproj, dyb, delta_b, dw_s, dbt, dg_v, got, own = _mix_bwd(
            s["proj"], s["y_b"], dmerged, w_spatial[l], s["b_cols"], vec(sgu_norm_g, l), sel, f"mix_bwd_{l}",
            comm=[("halves", _pack_grad_rest(dw_o, dw_gu_t, dw_d))])
        db_s = dbt[:, :N_GROUPS].T.reshape(1, D_MODEL)
        riders = pending + [("chips", _add_halves(own, got, 592, f"chip_sum_rest_{l}"))]
        keys = ([("in", l + 1)] if pending else []) + [("rest", l)] + (["small"] if l == 0 else [])
        if l == 0:
            g_small[0] = jnp.concatenate([zero_row, dg_v, dg_mix_post, dg_ffn_pre, dg_ffn_post, db_s, zero_row,
                                          zero_row, dw_s.reshape(CHUNK, D_MODEL)], axis=0)
            riders = riders + [("gather", jnp.concatenate([g_small[1], g_small[0]], axis=0))]
        dproj, dc_key, dc_query, *arrived = _attn_bwd(s["proj"], dyb, s["e_b"], delta_b, s["c_row"], dproj,
                                                        f"attn_bwd_{l}", comm=riders)
        for key, got in zip(keys, arrived):
            grad_sums[key] = got
        dc_key_t = jnp.pad(dc_key[:, 0, :].T, ((0, 0), (0, HEAD_DIM - N_HEADS)))
        dproj, db_f = _forget_bwd(s["proj"], s["b_pad"], dc_key_t, dc_query, dproj, f"forget_bwd_{l}")
        db_f_row = jnp.pad(db_f, ((0, 0), (0, D_MODEL - HEAD_DIM)))
        dw_in_t = _matmul(dproj, s["h"], "tn", BF16, f"dw_in_{l}")
        dh, got, own = _matmul(dproj, w_in_t[l], "nn", F32, f"d_h_{l}", comm=[("halves", _pack_grad_in(dw_in_t))])
        chip_sum_in = _add_halves(own, got, 304, f"chip_sum_in_{l}")
        if l > 0:
            pending = [("chips", chip_sum_in)]
            dx, dg_mix_pre, dd, dg_ffn_post_below = _norm_bwd(
                dx1, f"bwd_pre_mix_{l}", pre=(dh, s["x0"], vec(mix_pre_g, l)),
                post=(saved[l - 1]["dn"], vec(ffn_post_g, l - 1)))
            g_small[l] = jnp.concatenate([dg_mix_pre, dg_v, dg_mix_post, dg_ffn_pre, dg_ffn_post, db_s, db_f_row,
                                          zero_row, dw_s.reshape(CHUNK, D_MODEL)], axis=0)
            dg_ffn_post = dg_ffn_post_below
        else:
            grad_sums[("in", 0)], = _exchange("chips", chip_sum_in, "exchange_grad_in_0")
            dx, dg_mix_pre = _norm_bwd(dx1, f"bwd_pre_mix_{l}", pre=(dh, s["x0"], vec(mix_pre_g, l)))
            late_rows = jnp.concatenate([dg_mix_pre, db_f_row] + [zero_row] * 6, axis=0)
    grad_x = dx.reshape(x.shape)

    late = _sum_slots(_exchange("gather", late_rows, "gather_late_small_grads")[0], 8, "sum_late_small_grads")
    small_sum = _sum_slots(grad_sums["small"], S_ROWS, "sum_small_grads")
    loss = lax.psum(loss.reshape(()), ("x", "y", "c"))

    sm = small_sum.reshape(depth, S_ROWS, D_MODEL)[::-1]
    g = {
        "mix_pre_g": jnp.stack([late[0], sm[1, 0]]), "sgu_norm_g": sm[:, 1], "mix_post_g": sm[:, 2],
        "ffn_pre_g": sm[:, 3], "ffn_post_g": sm[:, 4], "b_spatial": sm[:, 5].reshape(depth, N_GROUPS, CHUNK),
        "b_forget": jnp.stack([late[1, :N_HEADS], sm[1, 6, :N_HEADS]]),
        "w_spatial": sm[:, 8:].reshape(depth, N_GROUPS, CHUNK, CHUNK),
    }
    g_in = jnp.stack([_sum_slots(grad_sums[("in", l)], 304, f"sum_grad_in_{l}") for l in range(depth)])
    g_rest = jnp.stack([_sum_slots(grad_sums[("rest", l)], 592, f"sum_grad_rest_{l}") for l in range(depth)])
    g["w_in"] = jnp.swapaxes(g_in[:, :SH_IN], 1, 2)
    g["w_out"] = g_rest[:, R_OUT:R_OUT + SH_OUT]
    g["w_gate"] = jnp.swapaxes(g_rest[:, R_G:R_G + SH_FF], 1, 2)
    g["w_up"] = jnp.swapaxes(g_rest[:, R_U:R_U + SH_FF], 1, 2)
    g["w_down"] = g_rest[:, R_D:R_D + SH_FF]

    names = ["mix_pre_g", "w_in", "b_forget", "sgu_norm_g", "w_spatial", "b_spatial", "w_out", "mix_post_g",
             "ffn_pre_g", "w_gate", "w_up", "w_down", "ffn_post_g"]
    ws = dict(mix_pre_g=mix_pre_g, w_in=w_in, b_forget=b_forget, sgu_norm_g=sgu_norm_g, w_spatial=w_spatial,
              b_spatial=b_spatial, w_out=w_out, mix_post_g=mix_post_g, ffn_pre_g=ffn_pre_g, w_gate=w_gate, w_up=w_up,
              w_down=w_down, ffn_post_g=ffn_post_g)
    ms = dict(mix_pre_g=m_mix_pre_g, w_in=m_w_in, b_forget=m_b_forget, sgu_norm_g=m_sgu_norm_g, w_spatial=m_w_spatial,
              b_spatial=m_b_spatial, w_out=m_w_out, mix_post_g=m_mix_post_g, ffn_pre_g=m_ffn_pre_g, w_gate=m_w_gate,
              w_up=m_w_up, w_down=m_w_down, ffn_post_g=m_ffn_post_g)
    vs = dict(mix_pre_g=v_mix_pre_g, w_in=v_w_in, b_forget=v_b_forget, sgu_norm_g=v_sgu_norm_g, w_spatial=v_w_spatial,
              b_spatial=v_b_spatial, w_out=v_w_out, mix_post_g=v_mix_post_g, ffn_pre_g=v_ffn_pre_g, w_gate=v_w_gate,
              w_up=v_w_up, w_down=v_w_down, ffn_post_g=v_ffn_post_g)
    deltas, new_m, new_v = [], [], []
    for n in names:
        shape = ws[n].shape
        flat = (-1, shape[-1])
        d_w, n_m, n_v = _adamw(ws[n].reshape(flat), g[n].reshape(flat), ms[n].reshape(flat), vs[n].reshape(flat),
                               f"adamw_{n}")
        deltas.append(d_w.reshape(shape))
        new_m.append(n_m.reshape(shape))
        new_v.append(n_v.reshape(shape))
    grads = [g[n].reshape(ws[n].shape) for n in names]
    return (loss, grad_x, *grads, *deltas, *new_m, *new_v)
```

```python
import math

import jax
import jax.numpy as jnp
from jax import lax
from jax.experimental import pallas as pl
from jax.experimental.pallas import tpu as pltpu

F32 = jnp.float32
BF16 = jnp.bfloat16

N_DEV = 8
D_MODEL = 1024
N_HEADS = 8
HEAD_DIM = 128
CHUNK = 128
N_GROUPS = 8
D_FF = 2816
IN_WIDTH = 7 * D_MODEL + N_HEADS
IN_PAD = 7680
EPS = 1e-6
GATED = 4 * D_MODEL
F_WIDTH = 512
QKV_WIDTH = 3 * HEAD_DIM
CB_F = GATED // HEAD_DIM
CB_QKV = (GATED + F_WIDTH) // HEAD_DIM
assert GATED + F_WIDTH + N_HEADS * QKV_WIDTH == IN_PAD and (GATED + F_WIDTH) % QKV_WIDTH == 0

ADAM_LR, ADAM_B1, ADAM_B2, ADAM_EPS, ADAM_WD, ADAM_STEP = 0.001, 0.9, 0.999, 1e-08, 0.01, 10

SH_IN = IN_WIDTH // N_DEV
SH_OUT = D_MODEL // N_DEV
SH_FF = D_FF // N_DEV
P_IN = 912
R_OUT, R_G, R_U, R_D = 0, SH_OUT, SH_OUT + SH_FF, SH_OUT + 2 * SH_FF
P_REST = SH_OUT + 3 * SH_FF
S_ROWS = 136

ATTN_BLOCK = 512
ROW_BLOCK = 512
MM_TM, MM_TN, MM_TK = 1536, 1536, 2048
VMEM_CAP = 56 << 20
NEG = -1e30


def _tile(n, cap, unit=128):
    if n <= cap:
        return n
    best = None
    for t in range(unit, cap + 1, unit):
        if n % t == 0:
            best = t
    assert best is not None, (n, cap)
    return best


def _nbytes(shape, dtype):
    return math.prod(shape) * jnp.dtype(dtype).itemsize


_HBM = pl.BlockSpec(memory_space=pltpu.HBM)
MESH = pl.DeviceIdType.MESH

_N_REMOTE = N_DEV - 1


def _exchange_shapes(kind, x):
    return [jax.ShapeDtypeStruct((N_DEV,) + x.shape if kind == "gather" else x.shape, x.dtype)]


def _exchange_sems(kind):
    del kind
    return [pltpu.SemaphoreType.DMA((_N_REMOTE,)), pltpu.SemaphoreType.DMA((_N_REMOTE,)), pltpu.SemaphoreType.DMA]


def _plan(kind, x_ref, outs, sems):
    send_sems, recv_sems, local_sem = sems
    x, y, c = lax.axis_index("x"), lax.axis_index("y"), lax.axis_index("c")

    def remote(src, dst, k, to):
        return pltpu.make_async_remote_copy(src_ref=src, dst_ref=dst, send_sem=send_sems.at[k], recv_sem=recv_sems.at[k],
                                            device_id=to, device_id_type=MESH)

    sibling = (x, y, 1 - c)
    if kind == "gather":
        out, = outs
        slot = lambda px, py, pc: out.at[4 * px + 2 * py + pc]
        chips = [(1 - x, y), (x, 1 - y), (1 - x, 1 - y)]
        local = [pltpu.make_async_copy(x_ref, slot(x, y, c), local_sem)]
        first = [remote(x_ref, slot(x, y, c), 0, sibling)]
        first += [remote(x_ref, slot(x, y, c), 1 + k, (*chip, c)) for k, chip in enumerate(chips)]
        relays = [(remote(x_ref, slot(*chip, c), 1 + k, (*chip, c)), remote(slot(*chip, c), slot(*chip, c), 4 + k, sibling))
                  for k, chip in enumerate(chips)]
        arrivals = [remote(x_ref, slot(x, y, 1 - c), 0, sibling)]
        arrivals += [remote(x_ref, slot(*chip, 1 - c), 4 + k, sibling) for k, chip in enumerate(chips)]
        return local, first, relays, arrivals
    out, = outs
    mine = 4 * x + 2 * y + c
    first, arrivals = [], []
    for j in range(1, N_DEV):
        peer = (1 - x if j & 4 else x, 1 - y if j & 2 else y, 1 - c if j & 1 else c)
        theirs = 4 * peer[0] + 2 * peer[1] + peer[2]
        first.append(remote(x_ref.at[theirs], out.at[mine], j - 1, peer))
        arrivals.append(remote(x_ref.at[theirs], out.at[theirs], j - 1, peer))
    return [pltpu.make_async_copy(x_ref.at[mine], out.at[mine], local_sem)], first, [], arrivals


def _start(plan):
    local, first, _, _ = plan
    for cp in local + first:
        cp.start()


def _finish(plan):
    local, first, relays, arrivals = plan
    for arrival, onward in relays:
        arrival.wait_recv()
        onward.start()
    for cp in arrivals:
        cp.wait_recv()
    for cp in first + [onward for _, onward in relays]:
        cp.wait_send()
    for cp in local:
        cp.wait()


def _exchange(kind, x, name):
    n_out = len(_exchange_shapes(kind, x))

    def body(x_ref, *refs):
        plan = _plan(kind, x_ref, refs[:n_out], refs[n_out:])
        _start(plan)
        _finish(plan)

    return pl.pallas_call(body, name=name, out_shape=_exchange_shapes(kind, x), in_specs=[_HBM],
                          out_specs=[_HBM] * n_out, scratch_shapes=_exchange_sems(kind))(x)


def _call(body, name, grid, in_specs, out_specs, out_shape, args, scratch=(), sem=None, nbytes=0, aliases=None,
          comm=()):
    in_specs, out_specs, out_shape, args, scratch = (list(in_specs), list(out_specs), list(out_shape), list(args),
                                                     list(scratch))
    if comm:
        n_in, n_out, n_scr, n_ops = len(args), len(out_shape), len(scratch), len(comm)
        kinds = [kind for kind, _ in comm]
        shapes = [_exchange_shapes(kind, x) for kind, x in comm]
        inner = body

        def body(*refs):
            ins, refs = refs[:n_in], refs[n_in:]
            cins, refs = refs[:n_ops], refs[n_ops:]
            outs, refs = refs[:n_out], refs[n_out:]
            couts = []
            for sh in shapes:
                couts.append(refs[:len(sh)])
                refs = refs[len(sh):]
            own_scratch, sems = refs[:n_scr], refs[n_scr:]
            first = last = None
            for axis, size in enumerate(grid):
                at_start, at_end = pl.program_id(axis) == 0, pl.program_id(axis) == size - 1
                first = at_start if first is None else first & at_start
                last = at_end if last is None else last & at_end
            plans = [_plan(kinds[o], cins[o], couts[o], sems[3 * o:3 * o + 3]) for o in range(n_ops)]

            @pl.when(first)
            def _():
                for p in plans:
                    _start(p)

            inner(*ins, *outs, *own_scratch)

            @pl.when(last)
            def _():
                for p in plans:
                    _finish(p)

        in_specs += [_HBM] * n_ops
        args += [x for _, x in comm]
        for kind, sh in zip(kinds, shapes):
            out_shape += sh
            out_specs += [_HBM] * len(sh)
            scratch += _exchange_sems(kind)
        sem = ("arbitrary",) * len(grid)
    limit = int(min(max(2 * nbytes + (8 << 20), 32 << 20), VMEM_CAP))
    return pl.pallas_call(
        body, name=name, grid=grid, in_specs=in_specs, out_specs=out_specs, out_shape=out_shape,
        scratch_shapes=scratch, input_output_aliases=aliases or {},
        compiler_params=pltpu.CompilerParams(dimension_semantics=sem, vmem_limit_bytes=limit),
    )(*args)


def _sum_slots(x, tr, name):
    n, r, c = x.shape

    def body(x_ref, o_ref):
        acc = x_ref[0].astype(F32)
        for d in range(1, n):
            acc = acc + x_ref[d].astype(F32)
        o_ref[...] = acc

    return _call(body, name, (r // tr,), [pl.BlockSpec((n, tr, c), lambda i: (0, i, 0))],
                 [pl.BlockSpec((tr, c), lambda i: (i, 0))], [jax.ShapeDtypeStruct((r, c), F32)], [x],
                 sem=("parallel",), nbytes=_nbytes((n, tr, c), x.dtype) + _nbytes((tr, c), F32))[0]


_DIMS = {"nn": (((1,), (0,)), ((), ())), "nt": (((1,), (1,)), ((), ())), "tn": (((0,), (0,)), ((), ()))}


def _matmul(a, b, mode, out_dtype, name, comm=()):
    if mode == "nn":
        (m, k), (k2, n) = a.shape, b.shape
    elif mode == "nt":
        (m, k), (n, k2) = a.shape, b.shape
    else:
        (k, m), (k2, n) = a.shape, b.shape
    assert k == k2, (a.shape, b.shape, mode)
    tm, tn, tk = _tile(m, MM_TM), _tile(n, MM_TN), _tile(k, MM_TK)
    nk = k // tk
    dims = _DIMS[mode]
    a_spec = {"nn": pl.BlockSpec((tm, tk), lambda i, j, kk: (i, kk)),
              "nt": pl.BlockSpec((tm, tk), lambda i, j, kk: (i, kk)),
              "tn": pl.BlockSpec((tk, tm), lambda i, j, kk: (kk, i))}[mode]
    b_spec = {"nn": pl.BlockSpec((tk, tn), lambda i, j, kk: (kk, j)),
              "nt": pl.BlockSpec((tn, tk), lambda i, j, kk: (j, kk)),
              "tn": pl.BlockSpec((tk, tn), lambda i, j, kk: (kk, j))}[mode]

    def partial_product(a_ref, b_ref):
        return lax.dot_general(a_ref[...].astype(BF16), b_ref[...].astype(BF16), dims, preferred_element_type=F32)

    if nk == 1:
        def body(a_ref, b_ref, o_ref):
            o_ref[...] = partial_product(a_ref, b_ref).astype(o_ref.dtype)
        scratch = []
    else:
        def body(a_ref, b_ref, o_ref, acc_ref):
            kk = pl.program_id(2)

            @pl.when(kk == 0)
            def _():
                acc_ref[...] = jnp.zeros_like(acc_ref)

            acc_ref[...] += partial_product(a_ref, b_ref)

            @pl.when(kk == nk - 1)
            def _():
                o_ref[...] = acc_ref[...].astype(o_ref.dtype)
        scratch = [pltpu.VMEM((tm, tn), F32)]

    nbytes = (_nbytes((tm, tk), a.dtype) + _nbytes((tk, tn), b.dtype) + _nbytes((tm, tn), out_dtype)
              + _nbytes((tm, tn), F32))
    res = _call(body, name, (m // tm, n // tn, nk), [a_spec, b_spec],
                [pl.BlockSpec((tm, tn), lambda i, j, kk: (i, j))], [jax.ShapeDtypeStruct((m, n), out_dtype)], [a, b],
                scratch=scratch, sem=("parallel", "parallel", "arbitrary"), nbytes=nbytes, comm=comm)
    return res if comm else res[0]


def _rms(x):
    return lax.rsqrt(jnp.mean(x * x, axis=-1, keepdims=True) + EPS)


def _rms_bwd(dz, a, g):
    r = _rms(a)
    dzg = dz * g
    da = r * dzg - a * (r * r * r) * jnp.mean(dzg * a, axis=-1, keepdims=True)
    return da, dz * (a * r)


def _row_spec(tr, width):
    return pl.BlockSpec((tr, width), lambda i: (i, 0))


def _vec_spec(width):
    return pl.BlockSpec((1, width), lambda i: (0, 0))


def _rms_fwd(x, g, name):
    t, d = x.shape
    tr = min(ROW_BLOCK, t)

    def body(x_ref, g_ref, h_ref):
        xv = x_ref[...]
        h_ref[...] = ((xv * _rms(xv)) * g_ref[...]).astype(BF16)

    return _call(body, name, (t // tr,), [_row_spec(tr, d), _vec_spec(d)], [_row_spec(tr, d)],
                 [jax.ShapeDtypeStruct((t, d), BF16)], [x, g], sem=("parallel",), nbytes=3 * _nbytes((tr, d), F32))[0]


def _post_norm_fwd(x, o, g_post, g_next, name):
    t, d = x.shape
    tr = min(ROW_BLOCK, t)

    def body(x_ref, o_ref, gp_ref, gn_ref, xn_ref, h_ref):
        ov = o_ref[...]
        xn = x_ref[...] + (ov * _rms(ov)) * gp_ref[...]
        xn_ref[...] = xn
        h_ref[...] = ((xn * _rms(xn)) * gn_ref[...]).astype(BF16)

    return _call(body, name, (t // tr,), [_row_spec(tr, d), _row_spec(tr, d), _vec_spec(d), _vec_spec(d)],
                 [_row_spec(tr, d), _row_spec(tr, d)],
                 [jax.ShapeDtypeStruct((t, d), F32), jax.ShapeDtypeStruct((t, d), BF16)], [x, o, g_post, g_next],
                 sem=("parallel",), nbytes=5 * _nbytes((tr, d), F32))


def _post_norm_loss(x, o, g_post, target, name):
    t, d = x.shape
    tr = min(ROW_BLOCK, t)

    def body(x_ref, o_ref, gp_ref, t_ref, dy_ref, loss_ref):
        ov = o_ref[...]
        err = x_ref[...] + (ov * _rms(ov)) * gp_ref[...] - t_ref[...]
        dy_ref[...] = err / d
        part = 0.5 * jnp.sum(jnp.mean(err * err, axis=-1, keepdims=True), axis=0, keepdims=True)

        @pl.when(pl.program_id(0) == 0)
        def _():
            loss_ref[...] = jnp.zeros_like(loss_ref)

        loss_ref[...] += part

    return _call(body, name, (t // tr,), [_row_spec(tr, d), _row_spec(tr, d), _vec_spec(d), _row_spec(tr, d)],
                 [_row_spec(tr, d), pl.BlockSpec((1, 1), lambda i: (0, 0))],
                 [jax.ShapeDtypeStruct((t, d), F32), jax.ShapeDtypeStruct((1, 1), F32)], [x, o, g_post, target],
                 sem=("arbitrary",), nbytes=5 * _nbytes((tr, d), F32))


def _norm_bwd(dskip, name, pre=None, post=None):
    t, d = dskip.shape
    tr = min(ROW_BLOCK, t)
    n_in = 1 + (3 if pre else 0) + (2 if post else 0)

    def body(*refs):
        ins, outs = list(refs[:n_in]), list(refs[n_in:])
        first = pl.program_id(0) == 0
        dx = ins.pop(0)[...]
        if pre:
            dh_ref, xin_ref, gpre_ref = ins.pop(0), ins.pop(0), ins.pop(0)
            dxin, dg_rows = _rms_bwd(dh_ref[...].astype(F32), xin_ref[...], gpre_ref[...])
            dx = dx + dxin
            dx_ref, dgpre_ref = outs.pop(0), outs.pop(0)
            dx_ref[...] = dx

            @pl.when(first)
            def _():
                dgpre_ref[...] = jnp.zeros_like(dgpre_ref)

            dgpre_ref[...] += jnp.sum(dg_rows, axis=0, keepdims=True)
        if post:
            a_ref, gpost_ref = ins.pop(0), ins.pop(0)
            da, dg_rows = _rms_bwd(dx, a_ref[...], gpost_ref[...])
            da_ref, dgpost_ref = outs.pop(0), outs.pop(0)
            da_ref[...] = da.astype(BF16)

            @pl.when(first)
            def _():
                dgpost_ref[...] = jnp.zeros_like(dgpost_ref)

            dgpost_ref[...] += jnp.sum(dg_rows, axis=0, keepdims=True)

    args, in_specs, out_shape, out_specs = [dskip], [_row_spec(tr, d)], [], []
    if pre:
        args += list(pre)
        in_specs += [_row_spec(tr, d), _row_spec(tr, d), _vec_spec(d)]
        out_shape += [jax.ShapeDtypeStruct((t, d), F32), jax.ShapeDtypeStruct((1, d), F32)]
        out_specs += [_row_spec(tr, d), _vec_spec(d)]
    if post:
        args += list(post)
        in_specs += [_row_spec(tr, d), _vec_spec(d)]
        out_shape += [jax.ShapeDtypeStruct((t, d), BF16), jax.ShapeDtypeStruct((1, d), F32)]
        out_specs += [_row_spec(tr, d), _vec_spec(d)]
    return _call(body, name, (t // tr,), in_specs, out_specs, out_shape, args, sem=("arbitrary",),
                 nbytes=7 * _nbytes((tr, d), F32))


def _sigmoid(x):
    return 1.0 / (1.0 + jnp.exp(-x))


def _swiglu_fwd(gu, name):
    t = gu.shape[0]
    tr = min(ROW_BLOCK, t)

    def body(gu_ref, a_ref):
        gate, up = gu_ref[:, :D_FF], gu_ref[:, D_FF:]
        a_ref[...] = ((gate * _sigmoid(gate)) * up).astype(BF16)

    return _call(body, name, (t // tr,), [_row_spec(tr, 2 * D_FF)], [_row_spec(tr, D_FF)],
                 [jax.ShapeDtypeStruct((t, D_FF), BF16)], [gu], sem=("parallel",),
                 nbytes=2 * _nbytes((tr, 2 * D_FF), F32))[0]


def _swiglu_bwd(gu, dact, name):
    t = gu.shape[0]
    tr = min(ROW_BLOCK, t)

    def body(gu_ref, da_ref, o_ref):
        gate, up = gu_ref[:, :D_FF], gu_ref[:, D_FF:]
        da = da_ref[...]
        s = _sigmoid(gate)
        o_ref[:, :D_FF] = (da * up * (s * (1.0 + gate * (1.0 - s)))).astype(BF16)
        o_ref[:, D_FF:] = (da * (gate * s)).astype(BF16)

    return _call(body, name, (t // tr,), [_row_spec(tr, 2 * D_FF), _row_spec(tr, D_FF)], [_row_spec(tr, 2 * D_FF)],
                 [jax.ShapeDtypeStruct((t, 2 * D_FF), BF16)], [gu, dact], sem=("parallel",),
                 nbytes=3 * _nbytes((tr, 2 * D_FF), F32))[0]


def _split3(x):
    hi = x.astype(BF16)
    r1 = x - hi.astype(F32)
    mid = r1.astype(BF16)
    lo = (r1 - mid.astype(F32)).astype(BF16)
    return hi, mid, lo


def _dot_exact(mat01, x):
    hi, mid, lo = _split3(x)
    out = jnp.dot(mat01, hi, preferred_element_type=F32)
    out = out + jnp.dot(mat01, mid, preferred_element_type=F32)
    return out + jnp.dot(mat01, lo, preferred_element_type=F32)


def _dot_exact_rhs(x, mat01):
    hi, mid, lo = _split3(x)
    out = jnp.dot(hi, mat01, preferred_element_type=F32)
    out = out + jnp.dot(mid, mat01, preferred_element_type=F32)
    return out + jnp.dot(lo, mat01, preferred_element_type=F32)


def _tri(lower):
    r = lax.broadcasted_iota(jnp.int32, (CHUNK, CHUNK), 0)
    c = lax.broadcasted_iota(jnp.int32, (CHUNK, CHUNK), 1)
    return jnp.where(r >= c if lower else r <= c, 1.0, 0.0).astype(BF16)


def _log_sigmoid(z):
    return jnp.minimum(z, 0.0) - jnp.log(1.0 + jnp.exp(-jnp.abs(z)))


def _forget_cumsum(proj, b_pad, name):
    t = proj.shape[0]
    nb = t // CHUNK

    def body(f_ref, b_ref, c_ref):
        tri = _tri(True)
        b = b_ref[...]

        def blk(i, carry):
            rows = pl.ds(pl.multiple_of(i * CHUNK, CHUNK), CHUNK)
            cs = _dot_exact(tri, _log_sigmoid(f_ref[rows, :] + b)) + carry
            c_ref[rows, :] = cs
            return cs[CHUNK - 1:CHUNK, :]

        lax.fori_loop(0, nb, blk, jnp.zeros((1, HEAD_DIM), F32))

    return _call(body, name, (1,),
                 [pl.BlockSpec((t, HEAD_DIM), lambda i: (0, CB_F)), pl.BlockSpec((1, HEAD_DIM), lambda i: (0, 0))],
                 [pl.BlockSpec((t, HEAD_DIM), lambda i: (0, 0))], [jax.ShapeDtypeStruct((t, HEAD_DIM), F32)],
                 [proj, b_pad], sem=("arbitrary",), nbytes=2 * _nbytes((t, HEAD_DIM), F32))[0]


def _forget_bwd(proj, b_pad, dc_key, dc_query, dproj, name):
    t = proj.shape[0]
    tr = min(ROW_BLOCK, t)
    nb = t // tr
    rev = lambda i: nb - 1 - i

    def body(f_ref, b_ref, dck_ref, dcq_ref, dproj_in, df_ref, db_ref, run_ref):
        del dproj_in

        @pl.when(pl.program_id(0) == 0)
        def _():
            run_ref[...] = jnp.zeros_like(run_ref)
            db_ref[...] = jnp.zeros_like(db_ref)

        tri = _tri(False)
        b = b_ref[...]
        lane = lax.broadcasted_iota(jnp.int32, (CHUNK, HEAD_DIM), 1)
        df_ref[...] = jnp.zeros_like(df_ref)
        for n in reversed(range(tr // CHUNK)):
            rows = slice(n * CHUNK, (n + 1) * CHUNK)
            dc = dck_ref[rows, :]
            for h in range(N_HEADS):
                dc = dc + jnp.where(lane == h, dcq_ref[rows, h * HEAD_DIM:(h + 1) * HEAD_DIM], 0.0)
            dlogf = _dot_exact(tri, dc) + run_ref[...]
            run_ref[...] = dlogf[0:1, :]
            z = f_ref[rows, :] + b
            e = jnp.exp(-jnp.abs(z))
            sig_neg = jnp.where(z >= 0.0, e, 1.0) / (1.0 + e)
            df = jnp.where(lane < N_HEADS, dlogf * sig_neg, 0.0)
            df_ref[rows, :HEAD_DIM] = df.astype(BF16)
            db_ref[...] += jnp.sum(df, axis=0, keepdims=True)

    return _call(body, name, (nb,),
                 [pl.BlockSpec((tr, HEAD_DIM), lambda i: (rev(i), CB_F)), pl.BlockSpec((1, HEAD_DIM), lambda i: (0, 0)),
                  pl.BlockSpec((tr, HEAD_DIM), lambda i: (rev(i), 0)), pl.BlockSpec((tr, D_MODEL), lambda i: (rev(i), 0)),
                  pl.BlockSpec(memory_space=pl.ANY)],
                 [pl.BlockSpec((tr, F_WIDTH), lambda i: (rev(i), GATED // F_WIDTH)),
                  pl.BlockSpec((1, HEAD_DIM), lambda i: (0, 0))],
                 [jax.ShapeDtypeStruct(dproj.shape, dproj.dtype), jax.ShapeDtypeStruct((1, HEAD_DIM), F32)],
                 [proj, b_pad, dc_key, dc_query, dproj], scratch=[pltpu.VMEM((1, HEAD_DIM), F32)],
                 sem=("arbitrary",), nbytes=4 * _nbytes((tr, D_MODEL), F32), aliases={4: 0})


def _causal(block):
    r = lax.broadcasted_iota(jnp.int32, (block, block), 0)
    c = lax.broadcasted_iota(jnp.int32, (block, block), 1)
    return c <= r


def _lanes(x, width):
    return jnp.concatenate([x] * (width // HEAD_DIM), axis=1)


def _attn_fwd(proj, c_pad, c_row, name, comm=()):
    t = proj.shape[0]
    blk = min(ATTN_BLOCK, t)
    nq = t // blk
    scale = HEAD_DIM ** -0.5

    def body(q_ref, k_ref, v_ref, cp_ref, cr_ref, y_ref, e_ref):
        h, i = pl.program_id(0), pl.program_id(1)
        q = (q_ref[...] * scale).astype(BF16)
        lane = lax.broadcasted_iota(jnp.int32, (blk, HEAD_DIM), 1)
        ci = jnp.sum(jnp.where(lane == h, cp_ref[...], 0.0), axis=1, keepdims=True)

        def step(j, carry, diagonal):
            m, l, acc = carry
            off = pl.multiple_of(j * blk, blk)
            k = k_ref[pl.ds(off, blk), :].astype(BF16)
            v = v_ref[pl.ds(off, blk), :].astype(BF16)
            s = lax.dot_general(q, k, _DIMS["nt"], preferred_element_type=F32)
            s = s + ci - cr_ref[:, pl.ds(off, blk)]
            if diagonal:
                s = jnp.where(_causal(blk), s, NEG)
            m_new = jnp.maximum(m, jnp.max(s, axis=1, keepdims=True))
            alpha = jnp.exp(m - m_new)
            p = jnp.exp(s - m_new)
            l = alpha * l + jnp.sum(p, axis=1, keepdims=True)
            acc = alpha * acc + jnp.dot(p.astype(BF16), v, preferred_element_type=F32)
            return m_new, l, acc

        init = (jnp.full((blk, 1), NEG, F32), jnp.zeros((blk, 1), F32), jnp.zeros((blk, HEAD_DIM), F32))
        carry = lax.fori_loop(0, i, lambda j, c: step(j, c, False), init)
        m, l, acc = step(i, carry, True)
        y_ref[...] = acc / l
        e_ref[...] = jnp.broadcast_to(ci - (m + jnp.log(l)), (blk, HEAD_DIM))

    nbytes = 2 * _nbytes((t, HEAD_DIM), F32) + 4 * _nbytes((blk, HEAD_DIM), F32) + 4 * _nbytes((blk, blk), F32)
    head_block = pl.BlockSpec((blk, HEAD_DIM), lambda h, i: (i, h))
    return _call(body, name, (N_HEADS, nq),
                 [pl.BlockSpec((blk, HEAD_DIM), lambda h, i: (i, CB_QKV + 3 * h)),
                  pl.BlockSpec((t, HEAD_DIM), lambda h, i: (0, CB_QKV + 3 * h + 1)),
                  pl.BlockSpec((t, HEAD_DIM), lambda h, i: (0, CB_QKV + 3 * h + 2)),
                  pl.BlockSpec((blk, HEAD_DIM), lambda h, i: (i, 0)),
                  pl.BlockSpec((None, 1, t), lambda h, i: (h, 0, 0))],
                 [head_block, head_block],
                 [jax.ShapeDtypeStruct((t, D_MODEL), F32), jax.ShapeDtypeStruct((t, D_MODEL), F32)],
                 [proj, proj, proj, c_pad, c_row], sem=("parallel", "arbitrary"), nbytes=nbytes, comm=comm)


def _attn_bwd(proj, dy, e_b, delta_b, c_row, dproj, name, comm=()):
    t = proj.shape[0]
    blk = min(ATTN_BLOCK, t)
    nb = t // blk
    scale = HEAD_DIM ** -0.5

    def body(q_ref, k_ref, v_ref, dy_ref, e_ref, dl_ref, cr_ref, dproj_in, dqkv_ref, dc_ref, dr_ref, dq_acc):
        del dproj_in
        j = pl.program_id(1)

        @pl.when(j == 0)
        def _():
            dq_acc[...] = jnp.zeros_like(dq_acc)
            dr_ref[...] = jnp.zeros_like(dr_ref)

        k = k_ref[...].astype(BF16)
        v = v_ref[...].astype(BF16)
        crow = cr_ref[...]

        def step(i, carry, diagonal):
            dk, dv, dcs = carry
            rows = pl.ds(pl.multiple_of(i * blk, blk), blk)
            q = (q_ref[rows, :] * scale).astype(BF16)
            dyi = dy_ref[rows, :].astype(BF16)
            s = lax.dot_general(q, k, _DIMS["nt"], preferred_element_type=F32) + _lanes(e_ref[rows, :], blk) - crow
            if diagonal:
                s = jnp.where(_causal(blk), s, NEG)
            p = jnp.exp(s)
            dv = dv + lax.dot_general(p.astype(BF16), dyi, _DIMS["tn"], preferred_element_type=F32)
            dp = lax.dot_general(dyi, v, _DIMS["nt"], preferred_element_type=F32)
            ds = p * (dp - _lanes(dl_ref[rows, :], blk))
            dsb = ds.astype(BF16)
            dk = dk + lax.dot_general(dsb, q, _DIMS["tn"], preferred_element_type=F32)
            dq_acc[rows, :] += jnp.dot(dsb, k, preferred_element_type=F32) * scale
            dr_ref[rows, :] += jnp.sum(ds, axis=1, keepdims=True)
            return dk, dv, dcs + jnp.sum(ds, axis=0, keepdims=True)

        zeros = jnp.zeros((blk, HEAD_DIM), F32)
        carry = step(j, (zeros, zeros, jnp.zeros((1, blk), F32)), True)
        dk, dv, dcs = lax.fori_loop(j + 1, nb, lambda i, c: step(i, c, False), carry)
        mine = pl.ds(pl.multiple_of(j * blk, blk), blk)
        dqkv_ref[:, :HEAD_DIM] = dq_acc[mine, :].astype(BF16)
        dqkv_ref[:, HEAD_DIM:2 * HEAD_DIM] = dk.astype(BF16)
        dqkv_ref[:, 2 * HEAD_DIM:] = dv.astype(BF16)
        dc_ref[...] = -dcs

    full = lambda cb: pl.BlockSpec((t, HEAD_DIM), lambda h, j: (0, cb(h)))
    head = lambda h: h
    nbytes = (6 * _nbytes((t, HEAD_DIM), F32) + 4 * _nbytes((blk, HEAD_DIM), F32) + 5 * _nbytes((blk, blk), F32))
    return _call(body, name, (N_HEADS, nb),
                 [full(lambda h: CB_QKV + 3 * h),
                  pl.BlockSpec((blk, HEAD_DIM), lambda h, j: (j, CB_QKV + 3 * h + 1)),
                  pl.BlockSpec((blk, HEAD_DIM), lambda h, j: (j, CB_QKV + 3 * h + 2)),
                  full(head), full(head), full(head),
                  pl.BlockSpec((None, 1, blk), lambda h, j: (h, 0, j)),
                  pl.BlockSpec(memory_space=pl.ANY)],
                 [pl.BlockSpec((blk, QKV_WIDTH), lambda h, j: (j, (GATED + F_WIDTH) // QKV_WIDTH + h)),
                  pl.BlockSpec((None, 1, blk), lambda h, j: (h, 0, j)),
                  full(head)],
                 [jax.ShapeDtypeStruct(dproj.shape, dproj.dtype), jax.ShapeDtypeStruct((N_HEADS, 1, t), F32),
                  jax.ShapeDtypeStruct((t, D_MODEL), F32)],
                 [proj, proj, proj, dy, e_b, delta_b, c_row, dproj], scratch=[pltpu.VMEM((t, HEAD_DIM), F32)],
                 sem=("parallel", "arbitrary"), nbytes=nbytes, aliases={7: 0}, comm=comm)


_GELU_K = math.sqrt(2.0 / math.pi)
_GELU_C = 0.044715


def _gelu(x):
    t = jnp.tanh(_GELU_K * (x + _GELU_C * (x * x * x)))
    return 0.5 * x * (1.0 + t), t


def _gelu_grad(x, t):
    return 0.5 * (1.0 + t) + 0.5 * x * (1.0 - t * t) * (_GELU_K * (1.0 + 3.0 * _GELU_C * (x * x)))


def _layernorm_stats(a):
    mu = jnp.mean(a, axis=-1, keepdims=True)
    xc = a - mu
    r = lax.rsqrt(jnp.mean(xc * xc, axis=-1, keepdims=True) + EPS)
    return xc * r, r


def _group(g):
    return slice(g * CHUNK, (g + 1) * CHUNK)


def _gated_cols(k):
    return pl.BlockSpec((CHUNK, D_MODEL), lambda i: (i, k))


def _fixed(shape):
    return pl.BlockSpec(shape, lambda i: (0,) * len(shape))


def _mix_fwd(proj, y_b, w_s, b_cols, g_v, name):
    t = proj.shape[0]

    def body(u_ref, vs_ref, ga_ref, gb_ref, yb_ref, w_ref, b_ref, gv_ref, o_ref):
        a_u, _ = _gelu(u_ref[...])
        a_v, _ = _gelu(vs_ref[...])
        xhat, _ = _layernorm_stats(a_v)
        vn = (xhat * gv_ref[...]).astype(BF16)
        sa, sb = _sigmoid(ga_ref[...]), _sigmoid(gb_ref[...])
        yb = yb_ref[...]
        mask = _causal(CHUNK)
        for g in range(N_GROUPS):
            cols = _group(g)
            w = jnp.where(mask, w_ref[g], 0.0).astype(BF16)
            mixed = jnp.dot(w, vn[:, cols], preferred_element_type=F32) + b_ref[:, g:g + 1]
            o_ref[:, cols] = (sa[:, cols] * (a_u[:, cols] * mixed) + sb[:, cols] * yb[:, cols]).astype(BF16)

    return _call(body, name, (t // CHUNK,),
                 [_gated_cols(0), _gated_cols(1), _gated_cols(2), _gated_cols(3), _row_spec(CHUNK, D_MODEL),
                  _fixed((N_GROUPS, CHUNK, CHUNK)), _fixed((CHUNK, N_GROUPS)), _fixed((1, D_MODEL))],
                 [_row_spec(CHUNK, D_MODEL)], [jax.ShapeDtypeStruct((t, D_MODEL), BF16)],
                 [proj, proj, proj, proj, y_b, w_s, b_cols, g_v], sem=("parallel",),
                 nbytes=16 * _nbytes((CHUNK, D_MODEL), F32))[0]


def _mix_bwd(proj, y_b, dmerged, w_s, b_cols, g_v, sel, name, comm=()):
    t = proj.shape[0]

    def body(u_ref, vs_ref, ga_ref, gb_ref, yb_ref, dm_ref, w_ref, b_ref, gv_ref, sel_ref,
             dg_ref, dyb_ref, delta_ref, dw_ref, dbt_ref, dgv_ref):
        @pl.when(pl.program_id(0) == 0)
        def _():
            dw_ref[...] = jnp.zeros_like(dw_ref)
            dbt_ref[...] = jnp.zeros_like(dbt_ref)
            dgv_ref[...] = jnp.zeros_like(dgv_ref)

        u, vs = u_ref[...], vs_ref[...]
        a_u, t_u = _gelu(u)
        a_v, t_v = _gelu(vs)
        xhat, r = _layernorm_stats(a_v)
        gv = gv_ref[...]
        vn = (xhat * gv).astype(BF16)
        sa, sb = _sigmoid(ga_ref[...]), _sigmoid(gb_ref[...])
        yb, dm = yb_ref[...], dm_ref[...].astype(F32)
        dyb = dm * sb
        dyb_ref[...] = dyb
        dyb_yb = dyb * yb
        dg_ref[:, 3 * D_MODEL:] = (dm * yb * (sb * (1.0 - sb))).astype(BF16)
        dya = dm * sa
        mask = _causal(CHUNK)
        dmixed_parts, dvn_parts = [], []
        for g in range(N_GROUPS):
            cols = _group(g)
            delta_ref[:, cols] = jnp.broadcast_to(jnp.sum(dyb_yb[:, cols], axis=1, keepdims=True), (CHUNK, CHUNK))
            w = jnp.where(mask, w_ref[g], 0.0).astype(BF16)
            mixed = jnp.dot(w, vn[:, cols], preferred_element_type=F32) + b_ref[:, g:g + 1]
            y_a = a_u[:, cols] * mixed
            dg_ref[:, 2 * D_MODEL + g * CHUNK:2 * D_MODEL + (g + 1) * CHUNK] = (
                dm[:, cols] * y_a * (sa[:, cols] * (1.0 - sa[:, cols]))).astype(BF16)
            dg_ref[:, cols] = (dya[:, cols] * mixed * _gelu_grad(u[:, cols], t_u[:, cols])).astype(BF16)
            dmixed = dya[:, cols] * a_u[:, cols]
            dmb = dmixed.astype(BF16)
            dw = lax.dot_general(dmb, vn[:, cols], _DIMS["nt"], preferred_element_type=F32)
            dw_ref[g] += jnp.where(mask, dw, 0.0)
            dvn_parts.append(lax.dot_general(w, dmb, _DIMS["tn"], preferred_element_type=F32))
            dmixed_parts.append(dmixed)
        dmixed_all = jnp.concatenate(dmixed_parts, axis=1)
        dvn = jnp.concatenate(dvn_parts, axis=1)
        dbt_ref[...] += _dot_exact_rhs(dmixed_all, sel_ref[...])
        dgv_ref[...] += jnp.sum(dvn * xhat, axis=0, keepdims=True)
        dxh = dvn * gv
        da_v = r * (dxh - jnp.mean(dxh, axis=-1, keepdims=True)
                    - xhat * jnp.mean(dxh * xhat, axis=-1, keepdims=True))
        dg_ref[:, D_MODEL:2 * D_MODEL] = (da_v * _gelu_grad(vs, t_v)).astype(BF16)

    row = lambda width: _row_spec(CHUNK, width)
    return _call(body, name, (t // CHUNK,),
                 [_gated_cols(0), _gated_cols(1), _gated_cols(2), _gated_cols(3), row(D_MODEL), row(D_MODEL),
                  _fixed((N_GROUPS, CHUNK, CHUNK)), _fixed((CHUNK, N_GROUPS)), _fixed((1, D_MODEL)),
                  _fixed((D_MODEL, HEAD_DIM))],
                 [row(GATED), row(D_MODEL), row(D_MODEL),
                  _fixed((N_GROUPS, CHUNK, CHUNK)), _fixed((CHUNK, HEAD_DIM)), _fixed((1, D_MODEL))],
                 [jax.ShapeDtypeStruct((t, IN_PAD), BF16), jax.ShapeDtypeStruct((t, D_MODEL), F32),
                  jax.ShapeDtypeStruct((t, D_MODEL), F32), jax.ShapeDtypeStruct((N_GROUPS, CHUNK, CHUNK), F32),
                  jax.ShapeDtypeStruct((CHUNK, HEAD_DIM), F32), jax.ShapeDtypeStruct((1, D_MODEL), F32)],
                 [proj, proj, proj, proj, y_b, dmerged, w_s, b_cols, g_v, sel], sem=("arbitrary",),
                 nbytes=40 * _nbytes((CHUNK, D_MODEL), F32), comm=comm)


def _adamw(w, g, m, v, name):
    r, c = w.shape
    tr = _tile(r, 256, unit=8)

    def body(w_ref, g_ref, m_ref, v_ref, d_ref, nm_ref, nv_ref):
        gv = g_ref[...]
        nm = ADAM_B1 * m_ref[...] + (1.0 - ADAM_B1) * gv
        nv = ADAM_B2 * v_ref[...] + (1.0 - ADAM_B2) * (gv * gv)
        m_hat = nm / (1.0 - ADAM_B1 ** ADAM_STEP)
        v_hat = nv / (1.0 - ADAM_B2 ** ADAM_STEP)
        d_ref[...] = -ADAM_LR * (m_hat / (jnp.sqrt(v_hat) + ADAM_EPS) + ADAM_WD * w_ref[...])
        nm_ref[...] = nm
        nv_ref[...] = nv

    spec = pl.BlockSpec((tr, c), lambda i: (i, 0))
    shape = jax.ShapeDtypeStruct((r, c), F32)
    return _call(body, name, (r // tr,), [spec] * 4, [spec] * 3, [shape] * 3, [w, g, m, v], sem=("parallel",),
                 nbytes=7 * _nbytes((tr, max(c, 128)), F32))


def _pad_rows(a, rows):
    return jnp.pad(a, ((0, rows - a.shape[0]), (0, 0)))


def _pack_rest(w_out, w_gate, w_up, w_down):
    return jnp.concatenate([w_out, w_gate.T, w_up.T, w_down], axis=0).astype(BF16)


def _unpack_in(gathered):
    nat = gathered[:, :SH_IN, :].reshape(IN_WIDTH, D_MODEL)
    qkv = nat[2 * D_MODEL:5 * D_MODEL].reshape(3, N_HEADS, HEAD_DIM, D_MODEL)
    qkv = jnp.transpose(qkv, (1, 0, 2, 3)).reshape(3 * D_MODEL, D_MODEL)
    f = _pad_rows(nat[7 * D_MODEL:], F_WIDTH)
    return jnp.concatenate([nat[:2 * D_MODEL], nat[5 * D_MODEL:7 * D_MODEL], f, qkv], axis=0)


def _unpack_rest(gathered):
    take = lambda lo, n: gathered[:, lo:lo + n, :].reshape(N_DEV * n, D_MODEL)
    return take(R_OUT, SH_OUT), jnp.concatenate([take(R_G, SH_FF), take(R_U, SH_FF)], axis=0), take(R_D, SH_FF)


def _pack_grad_in(dw_in_t):
    qkv = dw_in_t[GATED + F_WIDTH:].reshape(N_HEADS, 3, HEAD_DIM, D_MODEL)
    qkv = jnp.transpose(qkv, (1, 0, 2, 3)).reshape(3 * D_MODEL, D_MODEL)
    nat = jnp.concatenate([dw_in_t[:2 * D_MODEL], qkv, dw_in_t[2 * D_MODEL:GATED], dw_in_t[GATED:GATED + N_HEADS]],
                          axis=0)
    return jnp.pad(nat.reshape(N_DEV, SH_IN, D_MODEL), ((0, 0), (0, P_IN - SH_IN), (0, 0)))


def _pack_grad_rest(dw_out, dw_gu_t, dw_down):
    split = lambda a, n: a.reshape(N_DEV, n, D_MODEL)
    return jnp.concatenate([split(dw_out, SH_OUT), split(dw_gu_t[:D_FF], SH_FF), split(dw_gu_t[D_FF:], SH_FF),
                            split(dw_down, SH_FF)], axis=1)


def kernel(x, mix_pre_g, w_in, b_forget, sgu_norm_g, w_spatial, b_spatial, w_out, mix_post_g, ffn_pre_g, w_gate, w_up, w_down, ffn_post_g, loss_target, m_mix_pre_g, m_w_in, m_b_forget, m_sgu_norm_g, m_w_spatial, m_b_spatial, m_w_out, m_mix_post_g, m_ffn_pre_g, m_w_gate, m_w_up, m_w_down, m_ffn_post_g, v_mix_pre_g, v_w_in, v_b_forget, v_sgu_norm_g, v_w_spatial, v_b_spatial, v_w_out, v_mix_post_g, v_ffn_pre_g, v_w_gate, v_w_up, v_w_down, v_ffn_post_g):
    depth = w_in.shape[0]
    assert depth == 2
    t = x.shape[1]
    x0 = x.reshape(t, D_MODEL)
    target = loss_target.reshape(t, D_MODEL)
    sel = (jnp.arange(D_MODEL)[:, None] // HEAD_DIM == jnp.arange(HEAD_DIM)[None, :]).astype(BF16)
    vec = lambda a, l: a[l][None, :]
    zero_row = jnp.zeros((1, D_MODEL), F32)

    p_in = [_pad_rows(w_in[l].T.astype(BF16), P_IN) for l in range(depth)]
    p_rest = [_pack_rest(w_out[l], w_gate[l], w_up[l], w_down[l]) for l in range(depth)]
    w_in_t = [None] * depth
    w_rest = [None] * depth
    w_in_t[0] = _unpack_in(_exchange("gather", p_in[0], "gather_w_in_0")[0])

    saved = []
    xl = x0
    h = _rms_fwd(xl, vec(mix_pre_g, 0), "rms_in_0")
    dy = loss = None
    for l in range(depth):
        b_pad = jnp.pad(b_forget[l], (0, HEAD_DIM - N_HEADS))[None, :]
        b_cols = b_spatial[l].T
        proj = _matmul(h, w_in_t[l], "nt", F32, f"proj_{l}")
        c_pad = _forget_cumsum(proj, b_pad, f"forget_cumsum_{l}")
        c_row = c_pad[:, :N_HEADS].T[:, None, :]
        riders = [("gather", p_rest[l])] + ([("gather", p_in[l + 1])] if l + 1 < depth else [])
        y_b, e_b, *arrived = _attn_fwd(proj, c_pad, c_row, f"attn_fwd_{l}", comm=riders)
        w_rest[l] = _unpack_rest(arrived[0])
        if l + 1 < depth:
            w_in_t[l + 1] = _unpack_in(arrived[1])
        w_o, w_gu_t, w_d = w_rest[l]
        merged = _mix_fwd(proj, y_b, w_spatial[l], b_cols, vec(sgu_norm_g, l), f"mix_fwd_{l}")
        o = _matmul(merged, w_o, "nn", F32, f"out_proj_{l}")
        x1, h2 = _post_norm_fwd(xl, o, vec(mix_post_g, l), vec(ffn_pre_g, l), f"post_mix_{l}")
        gu = _matmul(h2, w_gu_t, "nt", F32, f"ffn_gu_{l}")
        act = _swiglu_fwd(gu, f"swiglu_fwd_{l}")
        dn = _matmul(act, w_d, "nn", F32, f"ffn_down_{l}")
        saved.append(dict(x0=xl, h=h, proj=proj, b_pad=b_pad, b_cols=b_cols, c_row=c_row, y_b=y_b, e_b=e_b,
                          merged=merged, o=o, x1=x1, h2=h2, gu=gu, act=act, dn=dn))
        if l + 1 < depth:
            xl, h = _post_norm_fwd(x1, dn, vec(ffn_post_g, l), vec(mix_pre_g, l + 1), f"post_ffn_{l}")
        else:
            dy, loss = _post_norm_loss(x1, dn, vec(ffn_post_g, l), target, "loss")

    g_small = [None] * depth
    grad_sums = {}
    pending = []
    late_rows = None
    dx = dy
    dd, dg_ffn_post = _norm_bwd(dx, f"bwd_ffn_post_{depth - 1}", post=(saved[-1]["dn"], vec(ffn_post_g, depth - 1)))
    for l in reversed(range(depth)):
        s = saved[l]
        w_o, w_gu_t, w_d = w_rest[l]
        dact = _matmul(dd, w_d, "nt", F32, f"d_act_{l}")
        dw_d = _matmul(s["act"], dd, "tn", BF16, f"dw_down_{l}")
        dgu = _swiglu_bwd(s["gu"], dact, f"swiglu_bwd_{l}")
        dh2 = _matmul(dgu, w_gu_t, "nn", F32, f"d_h2_{l}")
        dw_gu_t = _matmul(dgu, s["h2"], "tn", BF16, f"dw_gu_{l}")
        dx1, dg_ffn_pre, do, dg_mix_post = _norm_bwd(
            dx, f"bwd_post_mix_{l}", pre=(dh2, s["x1"], vec(ffn_pre_g, l)), post=(s["o"], vec(mix_post_g, l)))
        dmerged = _matmul(do, w_o, "nt", F32, f"d_merged_{l}")
        dw_o = _matmul(s["merged"], do, "tn", BF16, f"dw_out_{l}")
        dproj, dyb, delta_b, dw_s, dbt, dg_v = _mix_bwd(
            s["proj"], s["y_b"], dmerged, w_spatial[l], s["b_cols"], vec(sgu_norm_g, l), sel, f"mix_bwd_{l}")
        db_s = dbt[:, :N_GROUPS].T.reshape(1, D_MODEL)
        riders = pending + [("scatter", _pack_grad_rest(dw_o, dw_gu_t, dw_d))]
        keys = ([("in", l + 1)] if pending else []) + [("rest", l)] + (["small"] if l == 0 else [])
        if l == 0:
            g_small[0] = jnp.concatenate([zero_row, dg_v, dg_mix_post, dg_ffn_pre, dg_ffn_post, db_s, zero_row,
                                          zero_row, dw_s.reshape(CHUNK, D_MODEL)], axis=0)
            riders = riders + [("gather", jnp.concatenate([g_small[1], g_small[0]], axis=0))]
        dproj, dc_key, dc_query, *arrived = _attn_bwd(s["proj"], dyb, s["e_b"], delta_b, s["c_row"], dproj,
                                                        f"attn_bwd_{l}", comm=riders)
        for key, got in zip(keys, arrived):
            grad_sums[key] = got
        dc_key_t = jnp.pad(dc_key[:, 0, :].T, ((0, 0), (0, HEAD_DIM - N_HEADS)))
        dproj, db_f = _forget_bwd(s["proj"], s["b_pad"], dc_key_t, dc_query, dproj, f"forget_bwd_{l}")
        db_f_row = jnp.pad(db_f, ((0, 0), (0, D_MODEL - HEAD_DIM)))
        dw_in_t = _matmul(dproj, s["h"], "tn", BF16, f"dw_in_{l}")
        if l > 0:
            dh = _matmul(dproj, w_in_t[l], "nn", F32, f"d_h_{l}")
            pending = [("scatter", _pack_grad_in(dw_in_t))]
            dx, dg_mix_pre, dd, dg_ffn_post_below = _norm_bwd(
                dx1, f"bwd_pre_mix_{l}", pre=(dh, s["x0"], vec(mix_pre_g, l)),
                post=(saved[l - 1]["dn"], vec(ffn_post_g, l - 1)))
            g_small[l] = jnp.concatenate([dg_mix_pre, dg_v, dg_mix_post, dg_ffn_pre, dg_ffn_post, db_s, db_f_row,
                                          zero_row, dw_s.reshape(CHUNK, D_MODEL)], axis=0)
            dg_ffn_post = dg_ffn_post_below
        else:
            dh, grad_sums[("in", 0)] = _matmul(dproj, w_in_t[l], "nn", F32, f"d_h_{l}",
                                               comm=[("scatter", _pack_grad_in(dw_in_t))])
            dx, dg_mix_pre = _norm_bwd(dx1, f"bwd_pre_mix_{l}", pre=(dh, s["x0"], vec(mix_pre_g, l)))
            late_rows = jnp.concatenate([dg_mix_pre, db_f_row] + [zero_row] * 6, axis=0)
    grad_x = dx.reshape(x.shape)

    late = _sum_slots(_exchange("gather", late_rows, "gather_late_small_grads")[0], 8, "sum_late_small_grads")
    small_sum = _sum_slots(grad_sums["small"], S_ROWS, "sum_small_grads")
    loss = lax.psum(loss.reshape(()), ("x", "y", "c"))

    sm = small_sum.reshape(depth, S_ROWS, D_MODEL)[::-1]
    g = {
        "mix_pre_g": jnp.stack([late[0], sm[1, 0]]), "sgu_norm_g": sm[:, 1], "mix_post_g": sm[:, 2],
        "ffn_pre_g": sm[:, 3], "ffn_post_g": sm[:, 4], "b_spatial": sm[:, 5].reshape(depth, N_GROUPS, CHUNK),
        "b_forget": jnp.stack([late[1, :N_HEADS], sm[1, 6, :N_HEADS]]),
        "w_spatial": sm[:, 8:].reshape(depth, N_GROUPS, CHUNK, CHUNK),
    }
    g_in = jnp.stack([_sum_slots(grad_sums[("in", l)], 304, f"sum_grad_in_{l}") for l in range(depth)])
    g_rest = jnp.stack([_sum_slots(grad_sums[("rest", l)], 592, f"sum_grad_rest_{l}") for l in range(depth)])
    g["w_in"] = jnp.swapaxes(g_in[:, :SH_IN], 1, 2)
    g["w_out"] = g_rest[:, R_OUT:R_OUT + SH_OUT]
    g["w_gate"] = jnp.swapaxes(g_rest[:, R_G:R_G + SH_FF], 1, 2)
    g["w_up"] = jnp.swapaxes(g_rest[:, R_U:R_U + SH_FF], 1, 2)
    g["w_down"] = g_rest[:, R_D:R_D + SH_FF]

    names = ["mix_pre_g", "w_in", "b_forget", "sgu_norm_g", "w_spatial", "b_spatial", "w_out", "mix_post_g",
             "ffn_pre_g", "w_gate", "w_up", "w_down", "ffn_post_g"]
    ws = dict(mix_pre_g=mix_pre_g, w_in=w_in, b_forget=b_forget, sgu_norm_g=sgu_norm_g, w_spatial=w_spatial,
              b_spatial=b_spatial, w_out=w_out, mix_post_g=mix_post_g, ffn_pre_g=ffn_pre_g, w_gate=w_gate, w_up=w_up,
              w_down=w_down, ffn_post_g=ffn_post_g)
    ms = dict(mix_pre_g=m_mix_pre_g, w_in=m_w_in, b_forget=m_b_forget, sgu_norm_g=m_sgu_norm_g, w_spatial=m_w_spatial,
              b_spatial=m_b_spatial, w_out=m_w_out, mix_post_g=m_mix_post_g, ffn_pre_g=m_ffn_pre_g, w_gate=m_w_gate,
              w_up=m_w_up, w_down=m_w_down, ffn_post_g=m_ffn_post_g)
    vs = dict(mix_pre_g=v_mix_pre_g, w_in=v_w_in, b_forget=v_b_forget, sgu_norm_g=v_sgu_norm_g, w_spatial=v_w_spatial,
              b_spatial=v_b_spatial, w_out=v_w_out, mix_post_g=v_mix_post_g, ffn_pre_g=v_ffn_pre_g, w_gate=v_w_gate,
              w_up=v_w_up, w_down=v_w_down, ffn_post_g=v_ffn_post_g)
    deltas, new_m, new_v = [], [], []
    for n in names:
        shape = ws[n].shape
        flat = (-1, shape[-1])
        d_w, n_m, n_v = _adamw(ws[n].reshape(flat), g[n].reshape(flat), ms[n].reshape(flat), vs[n].reshape(flat),
                               f"adamw_{n}")
        deltas.append(d_w.reshape(shape))
        new_m.append(n_m.reshape(shape))
        new_v.append(n_v.reshape(shape))
    grads = [g[n].reshape(ws[n].shape) for n in names]
    return (loss, grad_x, *grads, *deltas, *new_m, *new_v)
```

```python
import math

import jax
import jax.numpy as jnp
from jax import lax
from jax.experimental import pallas as pl
from jax.experimental.pallas import tpu as pltpu

F32 = jnp.float32
BF16 = jnp.bfloat16

N_DEV = 8
D_MODEL = 1024
N_HEADS = 8
HEAD_DIM = 128
CHUNK = 128
N_GROUPS = 8
D_FF = 2816
IN_WIDTH = 7 * D_MODEL + N_HEADS
IN_PAD = 7680
EPS = 1e-6
GATED = 4 * D_MODEL
F_WIDTH = 512
QKV_WIDTH = 3 * HEAD_DIM
CB_F = GATED // HEAD_DIM
CB_QKV = (GATED + F_WIDTH) // HEAD_DIM
assert GATED + F_WIDTH + N_HEADS * QKV_WIDTH == IN_PAD and (GATED + F_WIDTH) % QKV_WIDTH == 0

ADAM_LR, ADAM_B1, ADAM_B2, ADAM_EPS, ADAM_WD, ADAM_STEP = 0.001, 0.9, 0.999, 1e-08, 0.01, 10

SH_IN = IN_WIDTH // N_DEV
SH_OUT = D_MODEL // N_DEV
SH_FF = D_FF // N_DEV
P_IN = 912
R_OUT, R_G, R_U, R_D = 0, SH_OUT, SH_OUT + SH_FF, SH_OUT + 2 * SH_FF
P_REST = SH_OUT + 3 * SH_FF
S_ROWS = 136

ATTN_BLOCK = 512
ROW_BLOCK = 512
MM_TM, MM_TN, MM_TK = 1536, 1536, 2048
VMEM_CAP = 56 << 20
NEG = -1e30


def _tile(n, cap, unit=128):
    if n <= cap:
        return n
    best = None
    for t in range(unit, cap + 1, unit):
        if n % t == 0:
            best = t
    assert best is not None, (n, cap)
    return best


def _nbytes(shape, dtype):
    return math.prod(shape) * jnp.dtype(dtype).itemsize


_HBM = pl.BlockSpec(memory_space=pltpu.HBM)
MESH = pl.DeviceIdType.MESH

_N_REMOTE = N_DEV - 1


def _exchange_shapes(kind, x):
    return [jax.ShapeDtypeStruct((N_DEV,) + x.shape if kind == "gather" else x.shape, x.dtype)]


def _exchange_sems(kind):
    del kind
    return [pltpu.SemaphoreType.DMA((_N_REMOTE,)), pltpu.SemaphoreType.DMA((_N_REMOTE,)), pltpu.SemaphoreType.DMA]


def _plan(kind, x_ref, outs, sems):
    send_sems, recv_sems, local_sem = sems
    x, y, c = lax.axis_index("x"), lax.axis_index("y"), lax.axis_index("c")

    def remote(src, dst, k, to):
        return pltpu.make_async_remote_copy(src_ref=src, dst_ref=dst, send_sem=send_sems.at[k], recv_sem=recv_sems.at[k],
                                            device_id=to, device_id_type=MESH)

    sibling = (x, y, 1 - c)
    if kind == "gather":
        out, = outs
        slot = lambda px, py, pc: out.at[4 * px + 2 * py + pc]
        chips = [(1 - x, y), (x, 1 - y), (1 - x, 1 - y)]
        local = [pltpu.make_async_copy(x_ref, slot(x, y, c), local_sem)]
        first = [remote(x_ref, slot(x, y, c), 0, sibling)]
        first += [remote(x_ref, slot(x, y, c), 1 + k, (*chip, c)) for k, chip in enumerate(chips)]
        relays = [(remote(x_ref, slot(*chip, c), 1 + k, (*chip, c)), remote(slot(*chip, c), slot(*chip, c), 4 + k, sibling))
                  for k, chip in enumerate(chips)]
        arrivals = [remote(x_ref, slot(x, y, 1 - c), 0, sibling)]
        arrivals += [remote(x_ref, slot(*chip, 1 - c), 4 + k, sibling) for k, chip in enumerate(chips)]
        return local, first, relays, arrivals
    out, = outs
    mine = 4 * x + 2 * y + c
    first, arrivals = [], []
    for j in range(1, N_DEV):
        peer = (1 - x if j & 4 else x, 1 - y if j & 2 else y, 1 - c if j & 1 else c)
        theirs = 4 * peer[0] + 2 * peer[1] + peer[2]
        first.append(remote(x_ref.at[theirs], out.at[mine], j - 1, peer))
        arrivals.append(remote(x_ref.at[theirs], out.at[theirs], j - 1, peer))
    return [pltpu.make_async_copy(x_ref.at[mine], out.at[mine], local_sem)], first, [], arrivals


def _start(plan):
    local, first, _, _ = plan
    for cp in local + first:
        cp.start()


def _finish(plan):
    local, first, relays, arrivals = plan
    for arrival, onward in relays:
        arrival.wait_recv()
        onward.start()
    for cp in arrivals:
        cp.wait_recv()
    for cp in first + [onward for _, onward in relays]:
        cp.wait_send()
    for cp in local:
        cp.wait()


def _exchange(kind, x, name):
    n_out = len(_exchange_shapes(kind, x))

    def body(x_ref, *refs):
        plan = _plan(kind, x_ref, refs[:n_out], refs[n_out:])
        _start(plan)
        _finish(plan)

    return pl.pallas_call(body, name=name, out_shape=_exchange_shapes(kind, x), in_specs=[_HBM],
                          out_specs=[_HBM] * n_out, scratch_shapes=_exchange_sems(kind))(x)


def _call(body, name, grid, in_specs, out_specs, out_shape, args, scratch=(), sem=None, nbytes=0, aliases=None,
          comm=()):
    in_specs, out_specs, out_shape, args, scratch = (list(in_specs), list(out_specs), list(out_shape), list(args),
                                                     list(scratch))
    if comm:
        n_in, n_out, n_scr, n_ops = len(args), len(out_shape), len(scratch), len(comm)
        kinds = [kind for kind, _ in comm]
        shapes = [_exchange_shapes(kind, x) for kind, x in comm]
        inner = body

        def body(*refs):
            ins, refs = refs[:n_in], refs[n_in:]
            cins, refs = refs[:n_ops], refs[n_ops:]
            outs, refs = refs[:n_out], refs[n_out:]
            couts = []
            for sh in shapes:
                couts.append(refs[:len(sh)])
                refs = refs[len(sh):]
            own_scratch, sems = refs[:n_scr], refs[n_scr:]
            first = last = None
            for axis, size in enumerate(grid):
                at_start, at_end = pl.program_id(axis) == 0, pl.program_id(axis) == size - 1
                first = at_start if first is None else first & at_start
                last = at_end if last is None else last & at_end
            plans = [_plan(kinds[o], cins[o], couts[o], sems[3 * o:3 * o + 3]) for o in range(n_ops)]

            @pl.when(first)
            def _():
                for p in plans:
                    _start(p)

            inner(*ins, *outs, *own_scratch)

            @pl.when(last)
            def _():
                for p in plans:
                    _finish(p)

        in_specs += [_HBM] * n_ops
        args += [x for _, x in comm]
        for kind, sh in zip(kinds, shapes):
            out_shape += sh
            out_specs += [_HBM] * len(sh)
            scratch += _exchange_sems(kind)
        sem = ("arbitrary",) * len(grid)
    limit = int(min(max(2 * nbytes + (8 << 20), 32 << 20), VMEM_CAP))
    return pl.pallas_call(
        body, name=name, grid=grid, in_specs=in_specs, out_specs=out_specs, out_shape=out_shape,
        scratch_shapes=scratch, input_output_aliases=aliases or {},
        compiler_params=pltpu.CompilerParams(dimension_semantics=sem, vmem_limit_bytes=limit),
    )(*args)


def _sum_slots(x, tr, name):
    n, r, c = x.shape

    def body(x_ref, o_ref):
        acc = x_ref[0].astype(F32)
        for d in range(1, n):
            acc = acc + x_ref[d].astype(F32)
        o_ref[...] = acc

    return _call(body, name, (r // tr,), [pl.BlockSpec((n, tr, c), lambda i: (0, i, 0))],
                 [pl.BlockSpec((tr, c), lambda i: (i, 0))], [jax.ShapeDtypeStruct((r, c), F32)], [x],
                 sem=("parallel",), nbytes=_nbytes((n, tr, c), x.dtype) + _nbytes((tr, c), F32))[0]


_DIMS = {"nn": (((1,), (0,)), ((), ())), "nt": (((1,), (1,)), ((), ())), "tn": (((0,), (0,)), ((), ()))}


def _matmul(a, b, mode, out_dtype, name, comm=()):
    if mode == "nn":
        (m, k), (k2, n) = a.shape, b.shape
    elif mode == "nt":
        (m, k), (n, k2) = a.shape, b.shape
    else:
        (k, m), (k2, n) = a.shape, b.shape
    assert k == k2, (a.shape, b.shape, mode)
    tm, tn, tk = _tile(m, MM_TM), _tile(n, MM_TN), _tile(k, MM_TK)
    nk = k // tk
    dims = _DIMS[mode]
    a_spec = {"nn": pl.BlockSpec((tm, tk), lambda i, j, kk: (i, kk)),
              "nt": pl.BlockSpec((tm, tk), lambda i, j, kk: (i, kk)),
              "tn": pl.BlockSpec((tk, tm), lambda i, j, kk: (kk, i))}[mode]
    b_spec = {"nn": pl.BlockSpec((tk, tn), lambda i, j, kk: (kk, j)),
              "nt": pl.BlockSpec((tn, tk), lambda i, j, kk: (j, kk)),
              "tn": pl.BlockSpec((tk, tn), lambda i, j, kk: (kk, j))}[mode]

    def partial_product(a_ref, b_ref):
        return lax.dot_general(a_ref[...].astype(BF16), b_ref[...].astype(BF16), dims, preferred_element_type=F32)

    if nk == 1:
        def body(a_ref, b_ref, o_ref):
            o_ref[...] = partial_product(a_ref, b_ref).astype(o_ref.dtype)
        scratch = []
    else:
        def body(a_ref, b_ref, o_ref, acc_ref):
            kk = pl.program_id(2)

            @pl.when(kk == 0)
            def _():
                acc_ref[...] = jnp.zeros_like(acc_ref)

            acc_ref[...] += partial_product(a_ref, b_ref)

            @pl.when(kk == nk - 1)
            def _():
                o_ref[...] = acc_ref[...].astype(o_ref.dtype)
        scratch = [pltpu.VMEM((tm, tn), F32)]

    nbytes = (_nbytes((tm, tk), a.dtype) + _nbytes((tk, tn), b.dtype) + _nbytes((tm, tn), out_dtype)
              + _nbytes((tm, tn), F32))
    res = _call(body, name, (m // tm, n // tn, nk), [a_spec, b_spec],
                [pl.BlockSpec((tm, tn), lambda i, j, kk: (i, j))], [jax.ShapeDtypeStruct((m, n), out_dtype)], [a, b],
                scratch=scratch, sem=("parallel", "parallel", "arbitrary"), nbytes=nbytes, comm=comm)
    return res if comm else res[0]


def _rms(x):
    return lax.rsqrt(jnp.mean(x * x, axis=-1, keepdims=True) + EPS)


def _rms_bwd(dz, a, g):
    r = _rms(a)
    dzg = dz * g
    da = r * dzg - a * (r * r * r) * jnp.mean(dzg * a, axis=-1, keepdims=True)
    return da, dz * (a * r)


def _row_spec(tr, width):
    return pl.BlockSpec((tr, width), lambda i: (i, 0))


def _vec_spec(width):
    return pl.BlockSpec((1, width), lambda i: (0, 0))


def _rms_fwd(x, g, name):
    t, d = x.shape
    tr = min(ROW_BLOCK, t)

    def body(x_ref, g_ref, h_ref):
        xv = x_ref[...]
        h_ref[...] = ((xv * _rms(xv)) * g_ref[...]).astype(BF16)

    return _call(body, name, (t // tr,), [_row_spec(tr, d), _vec_spec(d)], [_row_spec(tr, d)],
                 [jax.ShapeDtypeStruct((t, d), BF16)], [x, g], sem=("parallel",), nbytes=3 * _nbytes((tr, d), F32))[0]


def _post_norm_fwd(x, o, g_post, g_next, name):
    t, d = x.shape
    tr = min(ROW_BLOCK, t)

    def body(x_ref, o_ref, gp_ref, gn_ref, xn_ref, h_ref):
        ov = o_ref[...]
        xn = x_ref[...] + (ov * _rms(ov)) * gp_ref[...]
        xn_ref[...] = xn
        h_ref[...] = ((xn * _rms(xn)) * gn_ref[...]).astype(BF16)

    return _call(body, name, (t // tr,), [_row_spec(tr, d), _row_spec(tr, d), _vec_spec(d), _vec_spec(d)],
                 [_row_spec(tr, d), _row_spec(tr, d)],
                 [jax.ShapeDtypeStruct((t, d), F32), jax.ShapeDtypeStruct((t, d), BF16)], [x, o, g_post, g_next],
                 sem=("parallel",), nbytes=5 * _nbytes((tr, d), F32))


def _post_norm_loss(x, o, g_post, target, name):
    t, d = x.shape
    tr = min(ROW_BLOCK, t)

    def body(x_ref, o_ref, gp_ref, t_ref, dy_ref, loss_ref):
        ov = o_ref[...]
        err = x_ref[...] + (ov * _rms(ov)) * gp_ref[...] - t_ref[...]
        dy_ref[...] = err / d
        part = 0.5 * jnp.sum(jnp.mean(err * err, axis=-1, keepdims=True), axis=0, keepdims=True)

        @pl.when(pl.program_id(0) == 0)
        def _():
            loss_ref[...] = jnp.zeros_like(loss_ref)

        loss_ref[...] += part

    return _call(body, name, (t // tr,), [_row_spec(tr, d), _row_spec(tr, d), _vec_spec(d), _row_spec(tr, d)],
                 [_row_spec(tr, d), pl.BlockSpec((1, 1), lambda i: (0, 0))],
                 [jax.ShapeDtypeStruct((t, d), F32), jax.ShapeDtypeStruct((1, 1), F32)], [x, o, g_post, target],
                 sem=("arbitrary",), nbytes=5 * _nbytes((tr, d), F32))


def _norm_bwd(dskip, name, pre=None, post=None):
    t, d = dskip.shape
    tr = min(ROW_BLOCK, t)
    n_in = 1 + (3 if pre else 0) + (2 if post else 0)

    def body(*refs):
        ins, outs = list(refs[:n_in]), list(refs[n_in:])
        first = pl.program_id(0) == 0
        dx = ins.pop(0)[...]
        if pre:
            dh_ref, xin_ref, gpre_ref = ins.pop(0), ins.pop(0), ins.pop(0)
            dxin, dg_rows = _rms_bwd(dh_ref[...].astype(F32), xin_ref[...], gpre_ref[...])
            dx = dx + dxin
            dx_ref, dgpre_ref = outs.pop(0), outs.pop(0)
            dx_ref[...] = dx

            @pl.when(first)
            def _():
                dgpre_ref[...] = jnp.zeros_like(dgpre_ref)

            dgpre_ref[...] += jnp.sum(dg_rows, axis=0, keepdims=True)
        if post:
            a_ref, gpost_ref = ins.pop(0), ins.pop(0)
            da, dg_rows = _rms_bwd(dx, a_ref[...], gpost_ref[...])
            da_ref, dgpost_ref = outs.pop(0), outs.pop(0)
            da_ref[...] = da.astype(BF16)

            @pl.when(first)
            def _():
                dgpost_ref[...] = jnp.zeros_like(dgpost_ref)

            dgpost_ref[...] += jnp.sum(dg_rows, axis=0, keepdims=True)

    args, in_specs, out_shape, out_specs = [dskip], [_row_spec(tr, d)], [], []
    if pre:
        args += list(pre)
        in_specs += [_row_spec(tr, d), _row_spec(tr, d), _vec_spec(d)]
        out_shape += [jax.ShapeDtypeStruct((t, d), F32), jax.ShapeDtypeStruct((1, d), F32)]
        out_specs += [_row_spec(tr, d), _vec_spec(d)]
    if post:
        args += list(post)
        in_specs += [_row_spec(tr, d), _vec_spec(d)]
        out_shape += [jax.ShapeDtypeStruct((t, d), BF16), jax.ShapeDtypeStruct((1, d), F32)]
        out_specs += [_row_spec(tr, d), _vec_spec(d)]
    return _call(body, name, (t // tr,), in_specs, out_specs, out_shape, args, sem=("arbitrary",),
                 nbytes=7 * _nbytes((tr, d), F32))


def _sigmoid(x):
    return 1.0 / (1.0 + jnp.exp(-x))


def _swiglu_fwd(gu, name):
    t = gu.shape[0]
    tr = min(ROW_BLOCK, t)

    def body(gu_ref, a_ref):
        gate, up = gu_ref[:, :D_FF], gu_ref[:, D_FF:]
        a_ref[...] = ((gate * _sigmoid(gate)) * up).astype(BF16)

    return _call(body, name, (t // tr,), [_row_spec(tr, 2 * D_FF)], [_row_spec(tr, D_FF)],
                 [jax.ShapeDtypeStruct((t, D_FF), BF16)], [gu], sem=("parallel",),
                 nbytes=2 * _nbytes((tr, 2 * D_FF), F32))[0]


def _swiglu_bwd(gu, dact, name):
    t = gu.shape[0]
    tr = min(ROW_BLOCK, t)

    def body(gu_ref, da_ref, o_ref):
        gate, up = gu_ref[:, :D_FF], gu_ref[:, D_FF:]
        da = da_ref[...]
        s = _sigmoid(gate)
        o_ref[:, :D_FF] = (da * up * (s * (1.0 + gate * (1.0 - s)))).astype(BF16)
        o_ref[:, D_FF:] = (da * (gate * s)).astype(BF16)

    return _call(body, name, (t // tr,), [_row_spec(tr, 2 * D_FF), _row_spec(tr, D_FF)], [_row_spec(tr, 2 * D_FF)],
                 [jax.ShapeDtypeStruct((t, 2 * D_FF), BF16)], [gu, dact], sem=("parallel",),
                 nbytes=3 * _nbytes((tr, 2 * D_FF), F32))[0]


def _split3(x):
    hi = x.astype(BF16)
    r1 = x - hi.astype(F32)
    mid = r1.astype(BF16)
    lo = (r1 - mid.astype(F32)).astype(BF16)
    return hi, mid, lo


def _dot_exact(mat01, x):
    hi, mid, lo = _split3(x)
    out = jnp.dot(mat01, hi, preferred_element_type=F32)
    out = out + jnp.dot(mat01, mid, preferred_element_type=F32)
    return out + jnp.dot(mat01, lo, preferred_element_type=F32)


def _dot_exact_rhs(x, mat01):
    hi, mid, lo = _split3(x)
    out = jnp.dot(hi, mat01, preferred_element_type=F32)
    out = out + jnp.dot(mid, mat01, preferred_element_type=F32)
    return out + jnp.dot(lo, mat01, preferred_element_type=F32)


def _tri(lower):
    r = lax.broadcasted_iota(jnp.int32, (CHUNK, CHUNK), 0)
    c = lax.broadcasted_iota(jnp.int32, (CHUNK, CHUNK), 1)
    return jnp.where(r >= c if lower else r <= c, 1.0, 0.0).astype(BF16)


def _log_sigmoid(z):
    return jnp.minimum(z, 0.0) - jnp.log(1.0 + jnp.exp(-jnp.abs(z)))


def _forget_cumsum(proj, b_pad, name):
    t = proj.shape[0]
    nb = t // CHUNK

    def body(f_ref, b_ref, c_ref):
        tri = _tri(True)
        b = b_ref[...]

        def blk(i, carry):
            rows = pl.ds(pl.multiple_of(i * CHUNK, CHUNK), CHUNK)
            cs = _dot_exact(tri, _log_sigmoid(f_ref[rows, :] + b)) + carry
            c_ref[rows, :] = cs
            return cs[CHUNK - 1:CHUNK, :]

        lax.fori_loop(0, nb, blk, jnp.zeros((1, HEAD_DIM), F32))

    return _call(body, name, (1,),
                 [pl.BlockSpec((t, HEAD_DIM), lambda i: (0, CB_F)), pl.BlockSpec((1, HEAD_DIM), lambda i: (0, 0))],
                 [pl.BlockSpec((t, HEAD_DIM), lambda i: (0, 0))], [jax.ShapeDtypeStruct((t, HEAD_DIM), F32)],
                 [proj, b_pad], sem=("arbitrary",), nbytes=2 * _nbytes((t, HEAD_DIM), F32))[0]


def _forget_bwd(proj, b_pad, dc_key, dc_query, dproj, name):
    t = proj.shape[0]
    tr = min(ROW_BLOCK, t)
    nb = t // tr
    rev = lambda i: nb - 1 - i

    def body(f_ref, b_ref, dck_ref, dcq_ref, dproj_in, df_ref, db_ref, run_ref):
        del dproj_in

        @pl.when(pl.program_id(0) == 0)
        def _():
            run_ref[...] = jnp.zeros_like(run_ref)
            db_ref[...] = jnp.zeros_like(db_ref)

        tri = _tri(False)
        b = b_ref[...]
        lane = lax.broadcasted_iota(jnp.int32, (CHUNK, HEAD_DIM), 1)
        df_ref[...] = jnp.zeros_like(df_ref)
        for n in reversed(range(tr // CHUNK)):
            rows = slice(n * CHUNK, (n + 1) * CHUNK)
            dc = dck_ref[rows, :]
            for h in range(N_HEADS):
                dc = dc + jnp.where(lane == h, dcq_ref[rows, h * HEAD_DIM:(h + 1) * HEAD_DIM], 0.0)
            dlogf = _dot_exact(tri, dc) + run_ref[...]
            run_ref[...] = dlogf[0:1, :]
            z = f_ref[rows, :] + b
            e = jnp.exp(-jnp.abs(z))
            sig_neg = jnp.where(z >= 0.0, e, 1.0) / (1.0 + e)
            df = jnp.where(lane < N_HEADS, dlogf * sig_neg, 0.0)
            df_ref[rows, :HEAD_DIM] = df.astype(BF16)
            db_ref[...] += jnp.sum(df, axis=0, keepdims=True)

    return _call(body, name, (nb,),
                 [pl.BlockSpec((tr, HEAD_DIM), lambda i: (rev(i), CB_F)), pl.BlockSpec((1, HEAD_DIM), lambda i: (0, 0)),
                  pl.BlockSpec((tr, HEAD_DIM), lambda i: (rev(i), 0)), pl.BlockSpec((tr, D_MODEL), lambda i: (rev(i), 0)),
                  pl.BlockSpec(memory_space=pl.ANY)],
                 [pl.BlockSpec((tr, F_WIDTH), lambda i: (rev(i), GATED // F_WIDTH)),
                  pl.BlockSpec((1, HEAD_DIM), lambda i: (0, 0))],
                 [jax.ShapeDtypeStruct(dproj.shape, dproj.dtype), jax.ShapeDtypeStruct((1, HEAD_DIM), F32)],
                 [proj, b_pad, dc_key, dc_query, dproj], scratch=[pltpu.VMEM((1, HEAD_DIM), F32)],
                 sem=("arbitrary",), nbytes=4 * _nbytes((tr, D_MODEL), F32), aliases={4: 0})


def _causal(block):
    r = lax.broadcasted_iota(jnp.int32, (block, block), 0)
    c = lax.broadcasted_iota(jnp.int32, (block, block), 1)
    return c <= r


def _lanes(x, width):
    return jnp.concatenate([x] * (width // HEAD_DIM), axis=1)


def _attn_fwd(proj, c_pad, c_row, name, comm=()):
    t = proj.shape[0]
    blk = min(ATTN_BLOCK, t)
    nq = t // blk
    scale = HEAD_DIM ** -0.5

    def body(q_ref, k_ref, v_ref, cp_ref, cr_ref, y_ref, e_ref):
        h, i = pl.program_id(0), pl.program_id(1)
        q = (q_ref[...] * scale).astype(BF16)
        lane = lax.broadcasted_iota(jnp.int32, (blk, HEAD_DIM), 1)
        ci = jnp.sum(jnp.where(lane == h, cp_ref[...], 0.0), axis=1, keepdims=True)

        def step(j, carry, diagonal):
            m, l, acc = carry
            off = pl.multiple_of(j * blk, blk)
            k = k_ref[pl.ds(off, blk), :].astype(BF16)
            v = v_ref[pl.ds(off, blk), :].astype(BF16)
            s = lax.dot_general(q, k, _DIMS["nt"], preferred_element_type=F32)
            s = s + ci - cr_ref[:, pl.ds(off, blk)]
            if diagonal:
                s = jnp.where(_causal(blk), s, NEG)
            m_new = jnp.maximum(m, jnp.max(s, axis=1, keepdims=True))
            alpha = jnp.exp(m - m_new)
            p = jnp.exp(s - m_new)
            l = alpha * l + jnp.sum(p, axis=1, keepdims=True)
            acc = alpha * acc + jnp.dot(p.astype(BF16), v, preferred_element_type=F32)
            return m_new, l, acc

        init = (jnp.full((blk, 1), NEG, F32), jnp.zeros((blk, 1), F32), jnp.zeros((blk, HEAD_DIM), F32))
        carry = lax.fori_loop(0, i, lambda j, c: step(j, c, False), init)
        m, l, acc = step(i, carry, True)
        y_ref[...] = acc / l
        e_ref[...] = jnp.broadcast_to(ci - (m + jnp.log(l)), (blk, HEAD_DIM))

    nbytes = 2 * _nbytes((t, HEAD_DIM), F32) + 4 * _nbytes((blk, HEAD_DIM), F32) + 4 * _nbytes((blk, blk), F32)
    head_block = pl.BlockSpec((blk, HEAD_DIM), lambda h, i: (i, h))
    return _call(body, name, (N_HEADS, nq),
                 [pl.BlockSpec((blk, HEAD_DIM), lambda h, i: (i, CB_QKV + 3 * h)),
                  pl.BlockSpec((t, HEAD_DIM), lambda h, i: (0, CB_QKV + 3 * h + 1)),
                  pl.BlockSpec((t, HEAD_DIM), lambda h, i: (0, CB_QKV + 3 * h + 2)),
                  pl.BlockSpec((blk, HEAD_DIM), lambda h, i: (i, 0)),
                  pl.BlockSpec((None, 1, t), lambda h, i: (h, 0, 0))],
                 [head_block, head_block],
                 [jax.ShapeDtypeStruct((t, D_MODEL), F32), jax.ShapeDtypeStruct((t, D_MODEL), F32)],
                 [proj, proj, proj, c_pad, c_row], sem=("parallel", "arbitrary"), nbytes=nbytes, comm=comm)


def _attn_bwd(proj, dy, e_b, delta_b, c_row, dproj, name, comm=()):
    t = proj.shape[0]
    blk = min(ATTN_BLOCK, t)
    nb = t // blk
    scale = HEAD_DIM ** -0.5

    def body(q_ref, k_ref, v_ref, dy_ref, e_ref, dl_ref, cr_ref, dproj_in, dqkv_ref, dc_ref, dr_ref, dq_acc):
        del dproj_in
        j = pl.program_id(1)

        @pl.when(j == 0)
        def _():
            dq_acc[...] = jnp.zeros_like(dq_acc)
            dr_ref[...] = jnp.zeros_like(dr_ref)

        k = k_ref[...].astype(BF16)
        v = v_ref[...].astype(BF16)
        crow = cr_ref[...]

        def step(i, carry, diagonal):
            dk, dv, dcs = carry
            rows = pl.ds(pl.multiple_of(i * blk, blk), blk)
            q = (q_ref[rows, :] * scale).astype(BF16)
            dyi = dy_ref[rows, :].astype(BF16)
            s = lax.dot_general(q, k, _DIMS["nt"], preferred_element_type=F32) + _lanes(e_ref[rows, :], blk) - crow
            if diagonal:
                s = jnp.where(_causal(blk), s, NEG)
            p = jnp.exp(s)
            dv = dv + lax.dot_general(p.astype(BF16), dyi, _DIMS["tn"], preferred_element_type=F32)
            dp = lax.dot_general(dyi, v, _DIMS["nt"], preferred_element_type=F32)
            ds = p * (dp - _lanes(dl_ref[rows, :], blk))
            dsb = ds.astype(BF16)
            dk = dk + lax.dot_general(dsb, q, _DIMS["tn"], preferred_element_type=F32)
            dq_acc[rows, :] += jnp.dot(dsb, k, preferred_element_type=F32) * scale
            dr_ref[rows, :] += jnp.sum(ds, axis=1, keepdims=True)
            return dk, dv, dcs + jnp.sum(ds, axis=0, keepdims=True)

        zeros = jnp.zeros((blk, HEAD_DIM), F32)
        carry = step(j, (zeros, zeros, jnp.zeros((1, blk), F32)), True)
        dk, dv, dcs = lax.fori_loop(j + 1, nb, lambda i, c: step(i, c, False), carry)
        mine = pl.ds(pl.multiple_of(j * blk, blk), blk)
        dqkv_ref[:, :HEAD_DIM] = dq_acc[mine, :].astype(BF16)
        dqkv_ref[:, HEAD_DIM:2 * HEAD_DIM] = dk.astype(BF16)
        dqkv_ref[:, 2 * HEAD_DIM:] = dv.astype(BF16)
        dc_ref[...] = -dcs

    full = lambda cb: pl.BlockSpec((t, HEAD_DIM), lambda h, j: (0, cb(h)))
    head = lambda h: h
    nbytes = (6 * _nbytes((t, HEAD_DIM), F32) + 4 * _nbytes((blk, HEAD_DIM), F32) + 5 * _nbytes((blk, blk), F32))
    return _call(body, name, (N_HEADS, nb),
                 [full(lambda h: CB_QKV + 3 * h),
                  pl.BlockSpec((blk, HEAD_DIM), lambda h, j: (j, CB_QKV + 3 * h + 1)),
                  pl.BlockSpec((blk, HEAD_DIM), lambda h, j: (j, CB_QKV + 3 * h + 2)),
                  full(head), full(head), full(head),
                  pl.BlockSpec((None, 1, blk), lambda h, j: (h, 0, j)),
                  pl.BlockSpec(memory_space=pl.ANY)],
                 [pl.BlockSpec((blk, QKV_WIDTH), lambda h, j: (j, (GATED + F_WIDTH) // QKV_WIDTH + h)),
                  pl.BlockSpec((None, 1, blk), lambda h, j: (h, 0, j)),
                  full(head)],
                 [jax.ShapeDtypeStruct(dproj.shape, dproj.dtype), jax.ShapeDtypeStruct((N_HEADS, 1, t), F32),
                  jax.ShapeDtypeStruct((t, D_MODEL), F32)],
                 [proj, proj, proj, dy, e_b, delta_b, c_row, dproj], scratch=[pltpu.VMEM((t, HEAD_DIM), F32)],
                 sem=("parallel", "arbitrary"), nbytes=nbytes, aliases={7: 0}, comm=comm)


def _chunks(t, blk):
    return [slice(r * blk, (r + 1) * blk) for r in range(t // blk)]


def _lane_of_head(x, h):
    lane = lax.broadcasted_iota(jnp.int32, x.shape, 1)
    return jnp.broadcast_to(jnp.sum(jnp.where(lane == h, x, 0.0), axis=1, keepdims=True), x.shape)


def _causal_t(block):
    r = lax.broadcasted_iota(jnp.int32, (block, block), 0)
    c = lax.broadcasted_iota(jnp.int32, (block, block), 1)
    return r <= c


def _attn_fwd_t(proj, c_pad, c_row, name, comm=()):
    t = proj.shape[0]
    blk = min(ATTN_BLOCK, t)
    nq = t // blk
    scale = HEAD_DIM ** -0.5

    def body(q_ref, k_ref, v_ref, cp_ref, cr_ref, y_ref, e_ref, kb_s, vt_s, cb_s):
        h, i = pl.program_id(0), pl.program_id(1)

        @pl.when(i == 0)
        def _():
            for rows in _chunks(t, blk):
                kb_s[rows, :] = k_ref[rows, :].astype(BF16)
                vt_s[:, rows] = v_ref[rows, :].T.astype(BF16)
                cb_s[rows, :] = _lane_of_head(cp_ref[rows, :], h)

        q = (q_ref[...] * scale).astype(BF16)
        ci = cr_ref[:, pl.ds(pl.multiple_of(i * blk, blk), blk)]

        def step(j, carry, diagonal):
            m, l, acc = carry
            rows = pl.ds(pl.multiple_of(j * blk, blk), blk)
            s = lax.dot_general(kb_s[rows, :], q, _DIMS["nt"], preferred_element_type=F32)
            s = s + ci - _lanes(cb_s[rows, :], blk)
            if diagonal:
                s = jnp.where(_causal_t(blk), s, NEG)
            m_new = jnp.maximum(m, jnp.max(s, axis=0, keepdims=True))
            alpha = jnp.exp(m - m_new)
            p = jnp.exp(s - m_new)
            l = alpha * l + jnp.sum(p, axis=0, keepdims=True)
            acc = alpha * acc + jnp.dot(vt_s[:, rows], p.astype(BF16), preferred_element_type=F32)
            return m_new, l, acc

        init = (jnp.full((1, blk), NEG, F32), jnp.zeros((1, blk), F32), jnp.zeros((HEAD_DIM, blk), F32))
        carry = lax.fori_loop(0, i, lambda j, c: step(j, c, False), init)
        m, l, acc = step(i, carry, True)
        y_ref[...] = (acc / l).T
        e_ref[...] = ci - (m + jnp.log(l))

    nbytes = 3 * _nbytes((t, HEAD_DIM), F32) + 4 * _nbytes((blk, HEAD_DIM), F32) + 4 * _nbytes((blk, blk), F32)
    return _call(body, name, (N_HEADS, nq),
                 [pl.BlockSpec((blk, HEAD_DIM), lambda h, i: (i, CB_QKV + 3 * h)),
                  pl.BlockSpec((t, HEAD_DIM), lambda h, i: (0, CB_QKV + 3 * h + 1)),
                  pl.BlockSpec((t, HEAD_DIM), lambda h, i: (0, CB_QKV + 3 * h + 2)),
                  pl.BlockSpec((t, HEAD_DIM), lambda h, i: (0, 0)),
                  pl.BlockSpec((None, 1, t), lambda h, i: (h, 0, 0))],
                 [pl.BlockSpec((blk, HEAD_DIM), lambda h, i: (i, h)),
                  pl.BlockSpec((None, 1, blk), lambda h, i: (h, 0, i))],
                 [jax.ShapeDtypeStruct((t, D_MODEL), F32), jax.ShapeDtypeStruct((N_HEADS, 1, t), F32)],
                 [proj, proj, proj, c_pad, c_row],
                 scratch=[pltpu.VMEM((t, HEAD_DIM), BF16), pltpu.VMEM((HEAD_DIM, t), BF16),
                          pltpu.VMEM((t, HEAD_DIM), F32)],
                 sem=("parallel", "arbitrary"), nbytes=nbytes, comm=comm)


def _attn_bwd_t(proj, dy, e_row, delta_b, c_pad, dproj, name, comm=()):
    t = proj.shape[0]
    blk = min(ATTN_BLOCK, t)
    nb = t // blk
    scale = HEAD_DIM ** -0.5

    def body(q_ref, k_ref, v_ref, dy_ref, e_ref, dl_ref, cp_ref, dproj_in, dqkv_ref, dck_ref, dcq_ref,
             qb_s, qt_s, dyb_s, dyt_s, dl_s, dqt_acc):
        del dproj_in
        h, j = pl.program_id(0), pl.program_id(1)

        @pl.when(j == 0)
        def _():
            for rows in _chunks(t, blk):
                qs = q_ref[rows, :] * scale
                qb_s[rows, :] = qs.astype(BF16)
                qt_s[:, rows] = qs.T.astype(BF16)
                dyr = dy_ref[rows, :]
                dyb_s[rows, :] = dyr.astype(BF16)
                dyt_s[:, rows] = dyr.T.astype(BF16)
                dl_s[:, rows] = dl_ref[rows, :].T[0:1, :]
            dqt_acc[...] = jnp.zeros_like(dqt_acc)
            dcq_ref[...] = jnp.zeros_like(dcq_ref)

        kf = k_ref[...]
        kb, kt = kf.astype(BF16), kf.T.astype(BF16)
        vb = v_ref[...].astype(BF16)
        cj = _lanes(_lane_of_head(cp_ref[...], h), blk)

        def step(i, carry, diagonal):
            dkt, dvt, dkey = carry
            cols = pl.ds(pl.multiple_of(i * blk, blk), blk)
            qi, dyi = qb_s[cols, :], dyb_s[cols, :]
            s = lax.dot_general(kb, qi, _DIMS["nt"], preferred_element_type=F32) + e_ref[:, cols] - cj
            if diagonal:
                s = jnp.where(_causal_t(blk), s, NEG)
            p = jnp.exp(s)
            dp = lax.dot_general(vb, dyi, _DIMS["nt"], preferred_element_type=F32)
            ds = p * (dp - dl_s[:, cols])
            pb, dsb = p.astype(BF16), ds.astype(BF16)
            dvt = dvt + lax.dot_general(dyt_s[:, cols], pb, _DIMS["nt"], preferred_element_type=F32)
            dkt = dkt + lax.dot_general(qt_s[:, cols], dsb, _DIMS["nt"], preferred_element_type=F32)
            dqt_acc[:, cols] += jnp.dot(kt, dsb, preferred_element_type=F32) * scale
            dcq_ref[:, cols] += jnp.sum(ds, axis=0, keepdims=True)
            part = ds[:, :HEAD_DIM]
            for g in range(1, blk // HEAD_DIM):
                part = part + ds[:, g * HEAD_DIM:(g + 1) * HEAD_DIM]
            return dkt, dvt, dkey + part

        zeros = jnp.zeros((HEAD_DIM, blk), F32)
        carry = step(j, (zeros, zeros, jnp.zeros((blk, HEAD_DIM), F32)), True)
        dkt, dvt, dkey = lax.fori_loop(j + 1, nb, lambda i, c: step(i, c, False), carry)
        mine = pl.ds(pl.multiple_of(j * blk, blk), blk)
        dqkv_ref[:, :HEAD_DIM] = dqt_acc[:, mine].T.astype(BF16)
        dqkv_ref[:, HEAD_DIM:2 * HEAD_DIM] = dkt.T.astype(BF16)
        dqkv_ref[:, 2 * HEAD_DIM:] = dvt.T.astype(BF16)
        dck_ref[...] = -jnp.sum(dkey.T, axis=0, keepdims=True)

    full = lambda cb: pl.BlockSpec((t, HEAD_DIM), lambda h, j: (0, cb(h)))
    head = lambda h: h
    row = pl.BlockSpec((None, 1, t), lambda h, j: (h, 0, 0))
    nbytes = (8 * _nbytes((t, HEAD_DIM), F32) + 4 * _nbytes((blk, HEAD_DIM), F32) + 6 * _nbytes((blk, blk), F32))
    return _call(body, name, (N_HEADS, nb),
                 [full(lambda h: CB_QKV + 3 * h),
                  pl.BlockSpec((blk, HEAD_DIM), lambda h, j: (j, CB_QKV + 3 * h + 1)),
                  pl.BlockSpec((blk, HEAD_DIM), lambda h, j: (j, CB_QKV + 3 * h + 2)),
                  full(head), row, full(head),
                  pl.BlockSpec((blk, HEAD_DIM), lambda h, j: (j, 0)),
                  pl.BlockSpec(memory_space=pl.ANY)],
                 [pl.BlockSpec((blk, QKV_WIDTH), lambda h, j: (j, (GATED + F_WIDTH) // QKV_WIDTH + h)),
                  pl.BlockSpec((None, 1, blk), lambda h, j: (h, 0, j)),
                  row],
                 [jax.ShapeDtypeStruct(dproj.shape, dproj.dtype), jax.ShapeDtypeStruct((N_HEADS, 1, t), F32),
                  jax.ShapeDtypeStruct((N_HEADS, 1, t), F32)],
                 [proj, proj, proj, dy, e_row, delta_b, c_pad, dproj],
                 scratch=[pltpu.VMEM((t, HEAD_DIM), BF16), pltpu.VMEM((HEAD_DIM, t), BF16),
                          pltpu.VMEM((t, HEAD_DIM), BF16), pltpu.VMEM((HEAD_DIM, t), BF16),
                          pltpu.VMEM((1, t), F32), pltpu.VMEM((HEAD_DIM, t), F32)],
                 sem=("parallel", "arbitrary"), nbytes=nbytes, aliases={7: 0}, comm=comm)


def _forget_bwd_t(proj, b_pad, dc, dproj, name):
    t = proj.shape[0]
    tr = min(ROW_BLOCK, t)
    nb = t // tr
    rev = lambda i: nb - 1 - i

    def body(f_ref, b_ref, dc_ref, dproj_in, df_ref, db_ref, run_ref):
        del dproj_in

        @pl.when(pl.program_id(0) == 0)
        def _():
            run_ref[...] = jnp.zeros_like(run_ref)
            db_ref[...] = jnp.zeros_like(db_ref)

        tri = _tri(False)
        b = b_ref[...]
        lane = lax.broadcasted_iota(jnp.int32, (CHUNK, HEAD_DIM), 1)
        df_ref[...] = jnp.zeros_like(df_ref)
        for n in reversed(range(tr // CHUNK)):
            rows = slice(n * CHUNK, (n + 1) * CHUNK)
            dlogf = _dot_exact(tri, dc_ref[rows, :]) + run_ref[...]
            run_ref[...] = dlogf[0:1, :]
            z = f_ref[rows, :] + b
            e = jnp.exp(-jnp.abs(z))
            sig_neg = jnp.where(z >= 0.0, e, 1.0) / (1.0 + e)
            df = jnp.where(lane < N_HEADS, dlogf * sig_neg, 0.0)
            df_ref[rows, :HEAD_DIM] = df.astype(BF16)
            db_ref[...] += jnp.sum(df, axis=0, keepdims=True)

    return _call(body, name, (nb,),
                 [pl.BlockSpec((tr, HEAD_DIM), lambda i: (rev(i), CB_F)), pl.BlockSpec((1, HEAD_DIM), lambda i: (0, 0)),
                  pl.BlockSpec((tr, HEAD_DIM), lambda i: (rev(i), 0)), pl.BlockSpec(memory_space=pl.ANY)],
                 [pl.BlockSpec((tr, F_WIDTH), lambda i: (rev(i), GATED // F_WIDTH)),
                  pl.BlockSpec((1, HEAD_DIM), lambda i: (0, 0))],
                 [jax.ShapeDtypeStruct(dproj.shape, dproj.dtype), jax.ShapeDtypeStruct((1, HEAD_DIM), F32)],
                 [proj, b_pad, dc, dproj], scratch=[pltpu.VMEM((1, HEAD_DIM), F32)],
                 sem=("arbitrary",), nbytes=4 * _nbytes((tr, HEAD_DIM), F32), aliases={3: 0})


_GELU_K = math.sqrt(2.0 / math.pi)
_GELU_C = 0.044715


def _gelu(x):
    t = jnp.tanh(_GELU_K * (x + _GELU_C * (x * x * x)))
    return 0.5 * x * (1.0 + t), t


def _gelu_grad(x, t):
    return 0.5 * (1.0 + t) + 0.5 * x * (1.0 - t * t) * (_GELU_K * (1.0 + 3.0 * _GELU_C * (x * x)))


def _layernorm_stats(a):
    mu = jnp.mean(a, axis=-1, keepdims=True)
    xc = a - mu
    r = lax.rsqrt(jnp.mean(xc * xc, axis=-1, keepdims=True) + EPS)
    return xc * r, r


def _group(g):
    return slice(g * CHUNK, (g + 1) * CHUNK)


def _gated_cols(k):
    return pl.BlockSpec((CHUNK, D_MODEL), lambda i: (i, k))


def _fixed(shape):
    return pl.BlockSpec(shape, lambda i: (0,) * len(shape))


def _mix_fwd(proj, y_b, w_s, b_cols, g_v, name):
    t = proj.shape[0]

    def body(u_ref, vs_ref, ga_ref, gb_ref, yb_ref, w_ref, b_ref, gv_ref, o_ref):
        a_u, _ = _gelu(u_ref[...])
        a_v, _ = _gelu(vs_ref[...])
        xhat, _ = _layernorm_stats(a_v)
        vn = (xhat * gv_ref[...]).astype(BF16)
        sa, sb = _sigmoid(ga_ref[...]), _sigmoid(gb_ref[...])
        yb = yb_ref[...]
        mask = _causal(CHUNK)
        for g in range(N_GROUPS):
            cols = _group(g)
            w = jnp.where(mask, w_ref[g], 0.0).astype(BF16)
            mixed = jnp.dot(w, vn[:, cols], preferred_element_type=F32) + b_ref[:, g:g + 1]
            o_ref[:, cols] = (sa[:, cols] * (a_u[:, cols] * mixed) + sb[:, cols] * yb[:, cols]).astype(BF16)

    return _call(body, name, (t // CHUNK,),
                 [_gated_cols(0), _gated_cols(1), _gated_cols(2), _gated_cols(3), _row_spec(CHUNK, D_MODEL),
                  _fixed((N_GROUPS, CHUNK, CHUNK)), _fixed((CHUNK, N_GROUPS)), _fixed((1, D_MODEL))],
                 [_row_spec(CHUNK, D_MODEL)], [jax.ShapeDtypeStruct((t, D_MODEL), BF16)],
                 [proj, proj, proj, proj, y_b, w_s, b_cols, g_v], sem=("parallel",),
                 nbytes=16 * _nbytes((CHUNK, D_MODEL), F32))[0]


def _mix_bwd(proj, y_b, dmerged, w_s, b_cols, g_v, sel, name, comm=()):
    t = proj.shape[0]

    def body(u_ref, vs_ref, ga_ref, gb_ref, yb_ref, dm_ref, w_ref, b_ref, gv_ref, sel_ref,
             dg_ref, dyb_ref, delta_ref, dw_ref, dbt_ref, dgv_ref):
        @pl.when(pl.program_id(0) == 0)
        def _():
            dw_ref[...] = jnp.zeros_like(dw_ref)
            dbt_ref[...] = jnp.zeros_like(dbt_ref)
            dgv_ref[...] = jnp.zeros_like(dgv_ref)

        u, vs = u_ref[...], vs_ref[...]
        a_u, t_u = _gelu(u)
        a_v, t_v = _gelu(vs)
        xhat, r = _layernorm_stats(a_v)
        gv = gv_ref[...]
        vn = (xhat * gv).astype(BF16)
        sa, sb = _sigmoid(ga_ref[...]), _sigmoid(gb_ref[...])
        yb, dm = yb_ref[...], dm_ref[...].astype(F32)
        dyb = dm * sb
        dyb_ref[...] = dyb
        dyb_yb = dyb * yb
        dg_ref[:, 3 * D_MODEL:] = (dm * yb * (sb * (1.0 - sb))).astype(BF16)
        dya = dm * sa
        mask = _causal(CHUNK)
        dmixed_parts, dvn_parts = [], []
        for g in range(N_GROUPS):
            cols = _group(g)
            delta_ref[:, cols] = jnp.broadcast_to(jnp.sum(dyb_yb[:, cols], axis=1, keepdims=True), (CHUNK, CHUNK))
            w = jnp.where(mask, w_ref[g], 0.0).astype(BF16)
            mixed = jnp.dot(w, vn[:, cols], preferred_element_type=F32) + b_ref[:, g:g + 1]
            y_a = a_u[:, cols] * mixed
            dg_ref[:, 2 * D_MODEL + g * CHUNK:2 * D_MODEL + (g + 1) * CHUNK] = (
                dm[:, cols] * y_a * (sa[:, cols] * (1.0 - sa[:, cols]))).astype(BF16)
            dg_ref[:, cols] = (dya[:, cols] * mixed * _gelu_grad(u[:, cols], t_u[:, cols])).astype(BF16)
            dmixed = dya[:, cols] * a_u[:, cols]
            dmb = dmixed.astype(BF16)
            dw = lax.dot_general(dmb, vn[:, cols], _DIMS["nt"], preferred_element_type=F32)
            dw_ref[g] += jnp.where(mask, dw, 0.0)
            dvn_parts.append(lax.dot_general(w, dmb, _DIMS["tn"], preferred_element_type=F32))
            dmixed_parts.append(dmixed)
        dmixed_all = jnp.concatenate(dmixed_parts, axis=1)
        dvn = jnp.concatenate(dvn_parts, axis=1)
        dbt_ref[...] += _dot_exact_rhs(dmixed_all, sel_ref[...])
        dgv_ref[...] += jnp.sum(dvn * xhat, axis=0, keepdims=True)
        dxh = dvn * gv
        da_v = r * (dxh - jnp.mean(dxh, axis=-1, keepdims=True)
                    - xhat * jnp.mean(dxh * xhat, axis=-1, keepdims=True))
        dg_ref[:, D_MODEL:2 * D_MODEL] = (da_v * _gelu_grad(vs, t_v)).astype(BF16)

    row = lambda width: _row_spec(CHUNK, width)
    return _call(body, name, (t // CHUNK,),
                 [_gated_cols(0), _gated_cols(1), _gated_cols(2), _gated_cols(3), row(D_MODEL), row(D_MODEL),
                  _fixed((N_GROUPS, CHUNK, CHUNK)), _fixed((CHUNK, N_GROUPS)), _fixed((1, D_MODEL)),
                  _fixed((D_MODEL, HEAD_DIM))],
                 [row(GATED), row(D_MODEL), row(D_MODEL),
                  _fixed((N_GROUPS, CHUNK, CHUNK)), _fixed((CHUNK, HEAD_DIM)), _fixed((1, D_MODEL))],
                 [jax.ShapeDtypeStruct((t, IN_PAD), BF16), jax.ShapeDtypeStruct((t, D_MODEL), F32),
                  jax.ShapeDtypeStruct((t, D_MODEL), F32), jax.ShapeDtypeStruct((N_GROUPS, CHUNK, CHUNK), F32),
                  jax.ShapeDtypeStruct((CHUNK, HEAD_DIM), F32), jax.ShapeDtypeStruct((1, D_MODEL), F32)],
                 [proj, proj, proj, proj, y_b, dmerged, w_s, b_cols, g_v, sel], sem=("arbitrary",),
                 nbytes=40 * _nbytes((CHUNK, D_MODEL), F32), comm=comm)


def _adamw(w, g, m, v, name):
    r, c = w.shape
    tr = _tile(r, 256, unit=8)

    def body(w_ref, g_ref, m_ref, v_ref, d_ref, nm_ref, nv_ref):
        gv = g_ref[...]
        nm = ADAM_B1 * m_ref[...] + (1.0 - ADAM_B1) * gv
        nv = ADAM_B2 * v_ref[...] + (1.0 - ADAM_B2) * (gv * gv)
        m_hat = nm / (1.0 - ADAM_B1 ** ADAM_STEP)
        v_hat = nv / (1.0 - ADAM_B2 ** ADAM_STEP)
        d_ref[...] = -ADAM_LR * (m_hat / (jnp.sqrt(v_hat) + ADAM_EPS) + ADAM_WD * w_ref[...])
        nm_ref[...] = nm
        nv_ref[...] = nv

    spec = pl.BlockSpec((tr, c), lambda i: (i, 0))
    shape = jax.ShapeDtypeStruct((r, c), F32)
    return _call(body, name, (r // tr,), [spec] * 4, [spec] * 3, [shape] * 3, [w, g, m, v], sem=("parallel",),
                 nbytes=7 * _nbytes((tr, max(c, 128)), F32))


def _pad_rows(a, rows):
    return jnp.pad(a, ((0, rows - a.shape[0]), (0, 0)))


def _pack_rest(w_out, w_gate, w_up, w_down):
    return jnp.concatenate([w_out, w_gate.T, w_up.T, w_down], axis=0).astype(BF16)


def _unpack_in(gathered):
    nat = gathered[:, :SH_IN, :].reshape(IN_WIDTH, D_MODEL)
    qkv = nat[2 * D_MODEL:5 * D_MODEL].reshape(3, N_HEADS, HEAD_DIM, D_MODEL)
    qkv = jnp.transpose(qkv, (1, 0, 2, 3)).reshape(3 * D_MODEL, D_MODEL)
    f = _pad_rows(nat[7 * D_MODEL:], F_WIDTH)
    return jnp.concatenate([nat[:2 * D_MODEL], nat[5 * D_MODEL:7 * D_MODEL], f, qkv], axis=0)


def _unpack_rest(gathered):
    take = lambda lo, n: gathered[:, lo:lo + n, :].reshape(N_DEV * n, D_MODEL)
    return take(R_OUT, SH_OUT), jnp.concatenate([take(R_G, SH_FF), take(R_U, SH_FF)], axis=0), take(R_D, SH_FF)


def _pack_grad_in(dw_in_t):
    qkv = dw_in_t[GATED + F_WIDTH:].reshape(N_HEADS, 3, HEAD_DIM, D_MODEL)
    qkv = jnp.transpose(qkv, (1, 0, 2, 3)).reshape(3 * D_MODEL, D_MODEL)
    nat = jnp.concatenate([dw_in_t[:2 * D_MODEL], qkv, dw_in_t[2 * D_MODEL:GATED], dw_in_t[GATED:GATED + N_HEADS]],
                          axis=0)
    return jnp.pad(nat.reshape(N_DEV, SH_IN, D_MODEL), ((0, 0), (0, P_IN - SH_IN), (0, 0)))


def _pack_grad_rest(dw_out, dw_gu_t, dw_down):
    split = lambda a, n: a.reshape(N_DEV, n, D_MODEL)
    return jnp.concatenate([split(dw_out, SH_OUT), split(dw_gu_t[:D_FF], SH_FF), split(dw_gu_t[D_FF:], SH_FF),
                            split(dw_down, SH_FF)], axis=1)


def kernel(x, mix_pre_g, w_in, b_forget, sgu_norm_g, w_spatial, b_spatial, w_out, mix_post_g, ffn_pre_g, w_gate, w_up, w_down, ffn_post_g, loss_target, m_mix_pre_g, m_w_in, m_b_forget, m_sgu_norm_g, m_w_spatial, m_b_spatial, m_w_out, m_mix_post_g, m_ffn_pre_g, m_w_gate, m_w_up, m_w_down, m_ffn_post_g, v_mix_pre_g, v_w_in, v_b_forget, v_sgu_norm_g, v_w_spatial, v_b_spatial, v_w_out, v_mix_post_g, v_ffn_pre_g, v_w_gate, v_w_up, v_w_down, v_ffn_post_g):
    depth = w_in.shape[0]
    assert depth == 2
    t = x.shape[1]
    x0 = x.reshape(t, D_MODEL)
    target = loss_target.reshape(t, D_MODEL)
    sel = (jnp.arange(D_MODEL)[:, None] // HEAD_DIM == jnp.arange(HEAD_DIM)[None, :]).astype(BF16)
    vec = lambda a, l: a[l][None, :]
    zero_row = jnp.zeros((1, D_MODEL), F32)

    p_in = [_pad_rows(w_in[l].T.astype(BF16), P_IN) for l in range(depth)]
    p_rest = [_pack_rest(w_out[l], w_gate[l], w_up[l], w_down[l]) for l in range(depth)]
    w_in_t = [None] * depth
    w_rest = [None] * depth
    w_in_t[0] = _unpack_in(_exchange("gather", p_in[0], "gather_w_in_0")[0])

    saved = []
    xl = x0
    h = _rms_fwd(xl, vec(mix_pre_g, 0), "rms_in_0")
    dy = loss = None
    for l in range(depth):
        b_pad = jnp.pad(b_forget[l], (0, HEAD_DIM - N_HEADS))[None, :]
        b_cols = b_spatial[l].T
        proj = _matmul(h, w_in_t[l], "nt", F32, f"proj_{l}")
        c_pad = _forget_cumsum(proj, b_pad, f"forget_cumsum_{l}")
        c_row = c_pad[:, :N_HEADS].T[:, None, :]
        riders = [("gather", p_rest[l])] + ([("gather", p_in[l + 1])] if l + 1 < depth else [])
        y_b, e_row, *arrived = _attn_fwd_t(proj, c_pad, c_row, f"attn_fwd_{l}", comm=riders)
        w_rest[l] = _unpack_rest(arrived[0])
        if l + 1 < depth:
            w_in_t[l + 1] = _unpack_in(arrived[1])
        w_o, w_gu_t, w_d = w_rest[l]
        merged = _mix_fwd(proj, y_b, w_spatial[l], b_cols, vec(sgu_norm_g, l), f"mix_fwd_{l}")
        o = _matmul(merged, w_o, "nn", F32, f"out_proj_{l}")
        x1, h2 = _post_norm_fwd(xl, o, vec(mix_post_g, l), vec(ffn_pre_g, l), f"post_mix_{l}")
        gu = _matmul(h2, w_gu_t, "nt", F32, f"ffn_gu_{l}")
        act = _swiglu_fwd(gu, f"swiglu_fwd_{l}")
        dn = _matmul(act, w_d, "nn", F32, f"ffn_down_{l}")
        saved.append(dict(x0=xl, h=h, proj=proj, b_pad=b_pad, b_cols=b_cols, c_pad=c_pad, y_b=y_b, e_row=e_row,
                          merged=merged, o=o, x1=x1, h2=h2, gu=gu, act=act, dn=dn))
        if l + 1 < depth:
            xl, h = _post_norm_fwd(x1, dn, vec(ffn_post_g, l), vec(mix_pre_g, l + 1), f"post_ffn_{l}")
        else:
            dy, loss = _post_norm_loss(x1, dn, vec(ffn_post_g, l), target, "loss")

    g_small = [None] * depth
    grad_sums = {}
    pending = []
    late_rows = None
    dx = dy
    dd, dg_ffn_post = _norm_bwd(dx, f"bwd_ffn_post_{depth - 1}", post=(saved[-1]["dn"], vec(ffn_post_g, depth - 1)))
    for l in reversed(range(depth)):
        s = saved[l]
        w_o, w_gu_t, w_d = w_rest[l]
        dact = _matmul(dd, w_d, "nt", F32, f"d_act_{l}")
        dw_d = _matmul(s["act"], dd, "tn", BF16, f"dw_down_{l}")
        dgu = _swiglu_bwd(s["gu"], dact, f"swiglu_bwd_{l}")
        dh2 = _matmul(dgu, w_gu_t, "nn", F32, f"d_h2_{l}")
        dw_gu_t = _matmul(dgu, s["h2"], "tn", BF16, f"dw_gu_{l}")
        dx1, dg_ffn_pre, do, dg_mix_post = _norm_bwd(
            dx, f"bwd_post_mix_{l}", pre=(dh2, s["x1"], vec(ffn_pre_g, l)), post=(s["o"], vec(mix_post_g, l)))
        dmerged = _matmul(do, w_o, "nt", F32, f"d_merged_{l}")
        dw_o = _matmul(s["merged"], do, "tn", BF16, f"dw_out_{l}")
        dproj, dyb, delta_b, dw_s, dbt, dg_v = _mix_bwd(
            s["proj"], s["y_b"], dmerged, w_spatial[l], s["b_cols"], vec(sgu_norm_g, l), sel, f"mix_bwd_{l}")
        db_s = dbt[:, :N_GROUPS].T.reshape(1, D_MODEL)
        riders = pending + [("scatter", _pack_grad_rest(dw_o, dw_gu_t, dw_d))]
        keys = ([("in", l + 1)] if pending else []) + [("rest", l)] + (["small"] if l == 0 else [])
        if l == 0:
            g_small[0] = jnp.concatenate([zero_row, dg_v, dg_mix_post, dg_ffn_pre, dg_ffn_post, db_s, zero_row,
                                          zero_row, dw_s.reshape(CHUNK, D_MODEL)], axis=0)
            riders = riders + [("gather", jnp.concatenate([g_small[1], g_small[0]], axis=0))]
        dproj, dc_key, dc_query, *arrived = _attn_bwd_t(s["proj"], dyb, s["e_row"], delta_b, s["c_pad"], dproj,
                                                          f"attn_bwd_{l}", comm=riders)
        for key, got in zip(keys, arrived):
            grad_sums[key] = got
        dc = jnp.pad((dc_key + dc_query)[:, 0, :].T, ((0, 0), (0, HEAD_DIM - N_HEADS)))
        dproj, db_f = _forget_bwd_t(s["proj"], s["b_pad"], dc, dproj, f"forget_bwd_{l}")
        db_f_row = jnp.pad(db_f, ((0, 0), (0, D_MODEL - HEAD_DIM)))
        dw_in_t = _matmul(dproj, s["h"], "tn", BF16, f"dw_in_{l}")
        if l > 0:
            dh = _matmul(dproj, w_in_t[l], "nn", F32, f"d_h_{l}")
            pending = [("scatter", _pack_grad_in(dw_in_t))]
            dx, dg_mix_pre, dd, dg_ffn_post_below = _norm_bwd(
                dx1, f"bwd_pre_mix_{l}", pre=(dh, s["x0"], vec(mix_pre_g, l)),
                post=(saved[l - 1]["dn"], vec(ffn_post_g, l - 1)))
            g_small[l] = jnp.concatenate([dg_mix_pre, dg_v, dg_mix_post, dg_ffn_pre, dg_ffn_post, db_s, db_f_row,
                                          zero_row, dw_s.reshape(CHUNK, D_MODEL)], axis=0)
            dg_ffn_post = dg_ffn_post_below
        else:
            dh, grad_sums[("in", 0)] = _matmul(dproj, w_in_t[l], "nn", F32, f"d_h_{l}",
                                               comm=[("scatter", _pack_grad_in(dw_in_t))])
            dx, dg_mix_pre = _norm_bwd(dx1, f"bwd_pre_mix_{l}", pre=(dh, s["x0"], vec(mix_pre_g, l)))
            late_rows = jnp.concatenate([dg_mix_pre, db_f_row] + [zero_row] * 6, axis=0)
    grad_x = dx.reshape(x.shape)

    late = _sum_slots(_exchange("gather", late_rows, "gather_late_small_grads")[0], 8, "sum_late_small_grads")
    small_sum = _sum_slots(grad_sums["small"], S_ROWS, "sum_small_grads")
    loss = lax.psum(loss.reshape(()), ("x", "y", "c"))

    sm = small_sum.reshape(depth, S_ROWS, D_MODEL)[::-1]
    g = {
        "mix_pre_g": jnp.stack([late[0], sm[1, 0]]), "sgu_norm_g": sm[:, 1], "mix_post_g": sm[:, 2],
        "ffn_pre_g": sm[:, 3], "ffn_post_g": sm[:, 4], "b_spatial": sm[:, 5].reshape(depth, N_GROUPS, CHUNK),
        "b_forget": jnp.stack([late[1, :N_HEADS], sm[1, 6, :N_HEADS]]),
        "w_spatial": sm[:, 8:].reshape(depth, N_GROUPS, CHUNK, CHUNK),
    }
    g_in = jnp.stack([_sum_slots(grad_sums[("in", l)], 304, f"sum_grad_in_{l}") for l in range(depth)])
    g_rest = jnp.stack([_sum_slots(grad_sums[("rest", l)], 592, f"sum_grad_rest_{l}") for l in range(depth)])
    g["w_in"] = jnp.swapaxes(g_in[:, :SH_IN], 1, 2)
    g["w_out"] = g_rest[:, R_OUT:R_OUT + SH_OUT]
    g["w_gate"] = jnp.swapaxes(g_rest[:, R_G:R_G + SH_FF], 1, 2)
    g["w_up"] = jnp.swapaxes(g_rest[:, R_U:R_U + SH_FF], 1, 2)
    g["w_down"] = g_rest[:, R_D:R_D + SH_FF]

    names = ["mix_pre_g", "w_in", "b_forget", "sgu_norm_g", "w_spatial", "b_spatial", "w_out", "mix_post_g",
             "ffn_pre_g", "w_gate", "w_up", "w_down", "ffn_post_g"]
    ws = dict(mix_pre_g=mix_pre_g, w_in=w_in, b_forget=b_forget, sgu_norm_g=sgu_norm_g, w_spatial=w_spatial,
              b_spatial=b_spatial, w_out=w_out, mix_post_g=mix_post_g, ffn_pre_g=ffn_pre_g, w_gate=w_gate, w_up=w_up,
              w_down=w_down, ffn_post_g=ffn_post_g)
    ms = dict(mix_pre_g=m_mix_pre_g, w_in=m_w_in, b_forget=m_b_forget, sgu_norm_g=m_sgu_norm_g, w_spatial=m_w_spatial,
              b_spatial=m_b_spatial, w_out=m_w_out, mix_post_g=m_mix_post_g, ffn_pre_g=m_ffn_pre_g, w_gate=m_w_gate,
              w_up=m_w_up, w_down=m_w_down, ffn_post_g=m_ffn_post_g)
    vs = dict(mix_pre_g=v_mix_pre_g, w_in=v_w_in, b_forget=v_b_forget, sgu_norm_g=v_sgu_norm_g, w_spatial=v_w_spatial,
              b_spatial=v_b_spatial, w_out=v_w_out, mix_post_g=v_mix_post_g, ffn_pre_g=v_ffn_pre_g, w_gate=v_w_gate,
              w_up=v_w_up, w_down=v_w_down, ffn_post_g=v_ffn_post_g)
    deltas, new_m, new_v = [], [], []
    for n in names:
        shape = ws[n].shape
        flat = (-1, shape[-1])
        d_w, n_m, n_v = _adamw(ws[n].reshape(flat), g[n].reshape(flat), ms[n].reshape(flat), vs[n].reshape(flat),
                               f"adamw_{n}")
        deltas.append(d_w.reshape(shape))
        new_m.append(n_m.reshape(shape))
        new_v.append(n_v.reshape(shape))
    grads = [g[n].reshape(ws[n].shape) for n in names]
    return (loss, grad_x, *grads, *deltas, *new_m, *new_v)
```

```python
import math

import jax
import jax.numpy as jnp
from jax import lax
from jax.experimental import pallas as pl
from jax.experimental.pallas import tpu as pltpu

F32 = jnp.float32
BF16 = jnp.bfloat16

N_DEV = 8
D_MODEL = 1024
N_HEADS = 8
HEAD_DIM = 128
CHUNK = 128
N_GROUPS = 8
D_FF = 2816
IN_WIDTH = 7 * D_MODEL + N_HEADS
IN_PAD = 7680
EPS = 1e-6
GATED = 4 * D_MODEL
F_WIDTH = 512
QKV_WIDTH = 3 * HEAD_DIM
CB_F = GATED // HEAD_DIM
CB_QKV = (GATED + F_WIDTH) // HEAD_DIM
assert GATED + F_WIDTH + N_HEADS * QKV_WIDTH == IN_PAD and (GATED + F_WIDTH) % QKV_WIDTH == 0

ADAM_LR, ADAM_B1, ADAM_B2, ADAM_EPS, ADAM_WD, ADAM_STEP = 0.001, 0.9, 0.999, 1e-08, 0.01, 10

SH_IN = IN_WIDTH // N_DEV
SH_OUT = D_MODEL // N_DEV
SH_FF = D_FF // N_DEV
P_IN = 912
IN_STRIDE = 7 * HEAD_DIM
IN_ROWS = IN_STRIDE * (N_DEV - 1) + P_IN
R_OUT, R_G, R_U, R_D = 0, SH_OUT, SH_OUT + SH_FF, SH_OUT + 2 * SH_FF
P_REST = SH_OUT + 3 * SH_FF
S_ROWS = 136

ATTN_BLOCK = 512
ROW_BLOCK = 512
MM_TM, MM_TN, MM_TK = 1536, 1536, 2048
VMEM_CAP = 56 << 20
NEG = -1e30


def _tile(n, cap, unit=128):
    if n <= cap:
        return n
    best = None
    for t in range(unit, cap + 1, unit):
        if n % t == 0:
            best = t
    assert best is not None, (n, cap)
    return best


def _nbytes(shape, dtype):
    return math.prod(shape) * jnp.dtype(dtype).itemsize


_HBM = pl.BlockSpec(memory_space=pltpu.HBM)
MESH = pl.DeviceIdType.MESH

_N_REMOTE = N_DEV - 1


def _exchange_shapes(kind, x):
    shape = {"gather": (N_DEV,) + x.shape, "scatter": x.shape, "windows": (N_DEV, P_IN) + x.shape[1:]}[kind]
    return [jax.ShapeDtypeStruct(shape, x.dtype)]


def _exchange_sems(kind):
    del kind
    return [pltpu.SemaphoreType.DMA((_N_REMOTE,)), pltpu.SemaphoreType.DMA((_N_REMOTE,)), pltpu.SemaphoreType.DMA]


def _plan(kind, x_ref, outs, sems):
    send_sems, recv_sems, local_sem = sems
    x, y, c = lax.axis_index("x"), lax.axis_index("y"), lax.axis_index("c")

    def remote(src, dst, k, to):
        return pltpu.make_async_remote_copy(src_ref=src, dst_ref=dst, send_sem=send_sems.at[k], recv_sem=recv_sems.at[k],
                                            device_id=to, device_id_type=MESH)

    sibling = (x, y, 1 - c)
    if kind == "gather":
        out, = outs
        slot = lambda px, py, pc: out.at[4 * px + 2 * py + pc]
        chips = [(1 - x, y), (x, 1 - y), (1 - x, 1 - y)]
        local = [pltpu.make_async_copy(x_ref, slot(x, y, c), local_sem)]
        first = [remote(x_ref, slot(x, y, c), 0, sibling)]
        first += [remote(x_ref, slot(x, y, c), 1 + k, (*chip, c)) for k, chip in enumerate(chips)]
        relays = [(remote(x_ref, slot(*chip, c), 1 + k, (*chip, c)), remote(slot(*chip, c), slot(*chip, c), 4 + k, sibling))
                  for k, chip in enumerate(chips)]
        arrivals = [remote(x_ref, slot(x, y, 1 - c), 0, sibling)]
        arrivals += [remote(x_ref, slot(*chip, 1 - c), 4 + k, sibling) for k, chip in enumerate(chips)]
        return local, first, relays, arrivals
    out, = outs
    mine = 4 * x + 2 * y + c
    if kind == "scatter":
        src = lambda d: x_ref.at[d]
    else:
        src = lambda d: x_ref.at[pl.ds(pl.multiple_of(IN_STRIDE * d, 16), P_IN)]
    first, arrivals = [], []
    for j in range(1, N_DEV):
        peer = (1 - x if j & 4 else x, 1 - y if j & 2 else y, 1 - c if j & 1 else c)
        theirs = 4 * peer[0] + 2 * peer[1] + peer[2]
        first.append(remote(src(theirs), out.at[mine], j - 1, peer))
        arrivals.append(remote(src(theirs), out.at[theirs], j - 1, peer))
    return [pltpu.make_async_copy(src(mine), out.at[mine], local_sem)], first, [], arrivals


def _start(plan):
    local, first, _, _ = plan
    for cp in local + first:
        cp.start()


def _finish(plan):
    local, first, relays, arrivals = plan
    for arrival, onward in relays:
        arrival.wait_recv()
        onward.start()
    for cp in arrivals:
        cp.wait_recv()
    for cp in first + [onward for _, onward in relays]:
        cp.wait_send()
    for cp in local:
        cp.wait()


def _exchange(kind, x, name):
    n_out = len(_exchange_shapes(kind, x))

    def body(x_ref, *refs):
        plan = _plan(kind, x_ref, refs[:n_out], refs[n_out:])
        _start(plan)
        _finish(plan)

    return pl.pallas_call(body, name=name, out_shape=_exchange_shapes(kind, x), in_specs=[_HBM],
                          out_specs=[_HBM] * n_out, scratch_shapes=_exchange_sems(kind))(x)


def _call(body, name, grid, in_specs, out_specs, out_shape, args, scratch=(), sem=None, nbytes=0, aliases=None,
          comm=()):
    in_specs, out_specs, out_shape, args, scratch = (list(in_specs), list(out_specs), list(out_shape), list(args),
                                                     list(scratch))
    if comm:
        n_in, n_out, n_scr, n_ops = len(args), len(out_shape), len(scratch), len(comm)
        kinds = [kind for kind, _ in comm]
        shapes = [_exchange_shapes(kind, x) for kind, x in comm]
        inner = body

        def body(*refs):
            ins, refs = refs[:n_in], refs[n_in:]
            cins, refs = refs[:n_ops], refs[n_ops:]
            outs, refs = refs[:n_out], refs[n_out:]
            couts = []
            for sh in shapes:
                couts.append(refs[:len(sh)])
                refs = refs[len(sh):]
            own_scratch, sems = refs[:n_scr], refs[n_scr:]
            first = last = None
            for axis, size in enumerate(grid):
                at_start, at_end = pl.program_id(axis) == 0, pl.program_id(axis) == size - 1
                first = at_start if first is None else first & at_start
                last = at_end if last is None else last & at_end
            plans = [_plan(kinds[o], cins[o], couts[o], sems[3 * o:3 * o + 3]) for o in range(n_ops)]

            @pl.when(first)
            def _():
                for p in plans:
                    _start(p)

            inner(*ins, *outs, *own_scratch)

            @pl.when(last)
            def _():
                for p in plans:
                    _finish(p)

        in_specs += [_HBM] * n_ops
        args += [x for _, x in comm]
        for kind, sh in zip(kinds, shapes):
            out_shape += sh
            out_specs += [_HBM] * len(sh)
            scratch += _exchange_sems(kind)
        sem = ("arbitrary",) * len(grid)
    limit = int(min(max(2 * nbytes + (8 << 20), 32 << 20), VMEM_CAP))
    return pl.pallas_call(
        body, name=name, grid=grid, in_specs=in_specs, out_specs=out_specs, out_shape=out_shape,
        scratch_shapes=scratch, input_output_aliases=aliases or {},
        compiler_params=pltpu.CompilerParams(dimension_semantics=sem, vmem_limit_bytes=limit),
    )(*args)


def _sum_slots(x, tr, name):
    n, r, c = x.shape

    def body(x_ref, o_ref):
        acc = x_ref[0].astype(F32)
        for d in range(1, n):
            acc = acc + x_ref[d].astype(F32)
        o_ref[...] = acc

    return _call(body, name, (r // tr,), [pl.BlockSpec((n, tr, c), lambda i: (0, i, 0))],
                 [pl.BlockSpec((tr, c), lambda i: (i, 0))], [jax.ShapeDtypeStruct((r, c), F32)], [x],
                 sem=("parallel",), nbytes=_nbytes((n, tr, c), x.dtype) + _nbytes((tr, c), F32))[0]


_DIMS = {"nn": (((1,), (0,)), ((), ())), "nt": (((1,), (1,)), ((), ())), "tn": (((0,), (0,)), ((), ()))}


def _matmul(a, b, mode, out_dtype, name, comm=()):
    if mode == "nn":
        (m, k), (k2, n) = a.shape, b.shape
    elif mode == "nt":
        (m, k), (n, k2) = a.shape, b.shape
    else:
        (k, m), (k2, n) = a.shape, b.shape
    assert k == k2, (a.shape, b.shape, mode)
    tm, tn, tk = _tile(m, MM_TM), _tile(n, MM_TN), _tile(k, MM_TK)
    nk = k // tk
    dims = _DIMS[mode]
    a_spec = {"nn": pl.BlockSpec((tm, tk), lambda i, j, kk: (i, kk)),
              "nt": pl.BlockSpec((tm, tk), lambda i, j, kk: (i, kk)),
              "tn": pl.BlockSpec((tk, tm), lambda i, j, kk: (kk, i))}[mode]
    b_spec = {"nn": pl.BlockSpec((tk, tn), lambda i, j, kk: (kk, j)),
              "nt": pl.BlockSpec((tn, tk), lambda i, j, kk: (j, kk)),
              "tn": pl.BlockSpec((tk, tn), lambda i, j, kk: (kk, j))}[mode]

    def partial_product(a_ref, b_ref):
        return lax.dot_general(a_ref[...].astype(BF16), b_ref[...].astype(BF16), dims, preferred_element_type=F32)

    if nk == 1:
        def body(a_ref, b_ref, o_ref):
            o_ref[...] = partial_product(a_ref, b_ref).astype(o_ref.dtype)
        scratch = []
    else:
        def body(a_ref, b_ref, o_ref, acc_ref):
            kk = pl.program_id(2)

            @pl.when(kk == 0)
            def _():
                acc_ref[...] = jnp.zeros_like(acc_ref)

            acc_ref[...] += partial_product(a_ref, b_ref)

            @pl.when(kk == nk - 1)
            def _():
                o_ref[...] = acc_ref[...].astype(o_ref.dtype)
        scratch = [pltpu.VMEM((tm, tn), F32)]

    nbytes = (_nbytes((tm, tk), a.dtype) + _nbytes((tk, tn), b.dtype) + _nbytes((tm, tn), out_dtype)
              + _nbytes((tm, tn), F32))
    res = _call(body, name, (m // tm, n // tn, nk), [a_spec, b_spec],
                [pl.BlockSpec((tm, tn), lambda i, j, kk: (i, j))], [jax.ShapeDtypeStruct((m, n), out_dtype)], [a, b],
                scratch=scratch, sem=("parallel", "parallel", "arbitrary"), nbytes=nbytes, comm=comm)
    return res if comm else res[0]


def _rms(x):
    return lax.rsqrt(jnp.mean(x * x, axis=-1, keepdims=True) + EPS)


def _rms_bwd(dz, a, g):
    r = _rms(a)
    dzg = dz * g
    da = r * dzg - a * (r * r * r) * jnp.mean(dzg * a, axis=-1, keepdims=True)
    return da, dz * (a * r)


def _row_spec(tr, width):
    return pl.BlockSpec((tr, width), lambda i: (i, 0))


def _vec_spec(width):
    return pl.BlockSpec((1, width), lambda i: (0, 0))


def _rms_fwd(x, g, name):
    t, d = x.shape
    tr = min(ROW_BLOCK, t)

    def body(x_ref, g_ref, h_ref):
        xv = x_ref[...]
        h_ref[...] = ((xv * _rms(xv)) * g_ref[...]).astype(BF16)

    return _call(body, name, (t // tr,), [_row_spec(tr, d), _vec_spec(d)], [_row_spec(tr, d)],
                 [jax.ShapeDtypeStruct((t, d), BF16)], [x, g], sem=("parallel",), nbytes=3 * _nbytes((tr, d), F32))[0]


def _post_norm_fwd(x, o, g_post, g_next, name):
    t, d = x.shape
    tr = min(ROW_BLOCK, t)

    def body(x_ref, o_ref, gp_ref, gn_ref, xn_ref, h_ref):
        ov = o_ref[...]
        xn = x_ref[...] + (ov * _rms(ov)) * gp_ref[...]
        xn_ref[...] = xn
        h_ref[...] = ((xn * _rms(xn)) * gn_ref[...]).astype(BF16)

    return _call(body, name, (t // tr,), [_row_spec(tr, d), _row_spec(tr, d), _vec_spec(d), _vec_spec(d)],
                 [_row_spec(tr, d), _row_spec(tr, d)],
                 [jax.ShapeDtypeStruct((t, d), F32), jax.ShapeDtypeStruct((t, d), BF16)], [x, o, g_post, g_next],
                 sem=("parallel",), nbytes=5 * _nbytes((tr, d), F32))


def _post_norm_loss(x, o, g_post, target, name):
    t, d = x.shape
    tr = min(ROW_BLOCK, t)

    def body(x_ref, o_ref, gp_ref, t_ref, dy_ref, loss_ref):
        ov = o_ref[...]
        err = x_ref[...] + (ov * _rms(ov)) * gp_ref[...] - t_ref[...]
        dy_ref[...] = err / d
        part = 0.5 * jnp.sum(jnp.mean(err * err, axis=-1, keepdims=True), axis=0, keepdims=True)

        @pl.when(pl.program_id(0) == 0)
        def _():
            loss_ref[...] = jnp.zeros_like(loss_ref)

        loss_ref[...] += part

    return _call(body, name, (t // tr,), [_row_spec(tr, d), _row_spec(tr, d), _vec_spec(d), _row_spec(tr, d)],
                 [_row_spec(tr, d), pl.BlockSpec((1, 1), lambda i: (0, 0))],
                 [jax.ShapeDtypeStruct((t, d), F32), jax.ShapeDtypeStruct((1, 1), F32)], [x, o, g_post, target],
                 sem=("arbitrary",), nbytes=5 * _nbytes((tr, d), F32))


def _norm_bwd(dskip, name, pre=None, post=None):
    t, d = dskip.shape
    tr = min(ROW_BLOCK, t)
    n_in = 1 + (3 if pre else 0) + (2 if post else 0)

    def body(*refs):
        ins, outs = list(refs[:n_in]), list(refs[n_in:])
        first = pl.program_id(0) == 0
        dx = ins.pop(0)[...]
        if pre:
            dh_ref, xin_ref, gpre_ref = ins.pop(0), ins.pop(0), ins.pop(0)
            dxin, dg_rows = _rms_bwd(dh_ref[...].astype(F32), xin_ref[...], gpre_ref[...])
            dx = dx + dxin
            dx_ref, dgpre_ref = outs.pop(0), outs.pop(0)
            dx_ref[...] = dx

            @pl.when(first)
            def _():
                dgpre_ref[...] = jnp.zeros_like(dgpre_ref)

            dgpre_ref[...] += jnp.sum(dg_rows, axis=0, keepdims=True)
        if post:
            a_ref, gpost_ref = ins.pop(0), ins.pop(0)
            da, dg_rows = _rms_bwd(dx, a_ref[...], gpost_ref[...])
            da_ref, dgpost_ref = outs.pop(0), outs.pop(0)
            da_ref[...] = da.astype(BF16)

            @pl.when(first)
            def _():
                dgpost_ref[...] = jnp.zeros_like(dgpost_ref)

            dgpost_ref[...] += jnp.sum(dg_rows, axis=0, keepdims=True)

    args, in_specs, out_shape, out_specs = [dskip], [_row_spec(tr, d)], [], []
    if pre:
        args += list(pre)
        in_specs += [_row_spec(tr, d), _row_spec(tr, d), _vec_spec(d)]
        out_shape += [jax.ShapeDtypeStruct((t, d), F32), jax.ShapeDtypeStruct((1, d), F32)]
        out_specs += [_row_spec(tr, d), _vec_spec(d)]
    if post:
        args += list(post)
        in_specs += [_row_spec(tr, d), _vec_spec(d)]
        out_shape += [jax.ShapeDtypeStruct((t, d), BF16), jax.ShapeDtypeStruct((1, d), F32)]
        out_specs += [_row_spec(tr, d), _vec_spec(d)]
    return _call(body, name, (t // tr,), in_specs, out_specs, out_shape, args, sem=("arbitrary",),
                 nbytes=7 * _nbytes((tr, d), F32))


def _sigmoid(x):
    return 1.0 / (1.0 + jnp.exp(-x))


def _swiglu_fwd(gu, name):
    t = gu.shape[0]
    tr = min(ROW_BLOCK, t)

    def body(gu_ref, a_ref):
        gate, up = gu_ref[:, :D_FF], gu_ref[:, D_FF:]
        a_ref[...] = ((gate * _sigmoid(gate)) * up).astype(BF16)

    return _call(body, name, (t // tr,), [_row_spec(tr, 2 * D_FF)], [_row_spec(tr, D_FF)],
                 [jax.ShapeDtypeStruct((t, D_FF), BF16)], [gu], sem=("parallel",),
                 nbytes=2 * _nbytes((tr, 2 * D_FF), F32))[0]


def _swiglu_bwd(gu, dact, name):
    t = gu.shape[0]
    tr = min(ROW_BLOCK, t)

    def body(gu_ref, da_ref, o_ref):
        gate, up = gu_ref[:, :D_FF], gu_ref[:, D_FF:]
        da = da_ref[...]
        s = _sigmoid(gate)
        o_ref[:, :D_FF] = (da * up * (s * (1.0 + gate * (1.0 - s)))).astype(BF16)
        o_ref[:, D_FF:] = (da * (gate * s)).astype(BF16)

    return _call(body, name, (t // tr,), [_row_spec(tr, 2 * D_FF), _row_spec(tr, D_FF)], [_row_spec(tr, 2 * D_FF)],
                 [jax.ShapeDtypeStruct((t, 2 * D_FF), BF16)], [gu, dact], sem=("parallel",),
                 nbytes=3 * _nbytes((tr, 2 * D_FF), F32))[0]


def _split3(x):
    hi = x.astype(BF16)
    r1 = x - hi.astype(F32)
    mid = r1.astype(BF16)
    lo = (r1 - mid.astype(F32)).astype(BF16)
    return hi, mid, lo


def _dot_exact(mat01, x):
    hi, mid, lo = _split3(x)
    out = jnp.dot(mat01, hi, preferred_element_type=F32)
    out = out + jnp.dot(mat01, mid, preferred_element_type=F32)
    return out + jnp.dot(mat01, lo, preferred_element_type=F32)


def _dot_exact_rhs(x, mat01):
    hi, mid, lo = _split3(x)
    out = jnp.dot(hi, mat01, preferred_element_type=F32)
    out = out + jnp.dot(mid, mat01, preferred_element_type=F32)
    return out + jnp.dot(lo, mat01, preferred_element_type=F32)


def _tri(lower):
    r = lax.broadcasted_iota(jnp.int32, (CHUNK, CHUNK), 0)
    c = lax.broadcasted_iota(jnp.int32, (CHUNK, CHUNK), 1)
    return jnp.where(r >= c if lower else r <= c, 1.0, 0.0).astype(BF16)


def _log_sigmoid(z):
    return jnp.minimum(z, 0.0) - jnp.log(1.0 + jnp.exp(-jnp.abs(z)))


def _forget_cumsum(proj, b_pad, name):
    t = proj.shape[0]
    nb = t // CHUNK

    def body(f_ref, b_ref, c_ref):
        tri = _tri(True)
        b = b_ref[...]

        def blk(i, carry):
            rows = pl.ds(pl.multiple_of(i * CHUNK, CHUNK), CHUNK)
            cs = _dot_exact(tri, _log_sigmoid(f_ref[rows, :] + b)) + carry
            c_ref[rows, :] = cs
            return cs[CHUNK - 1:CHUNK, :]

        lax.fori_loop(0, nb, blk, jnp.zeros((1, HEAD_DIM), F32))

    return _call(body, name, (1,),
                 [pl.BlockSpec((t, HEAD_DIM), lambda i: (0, CB_F)), pl.BlockSpec((1, HEAD_DIM), lambda i: (0, 0))],
                 [pl.BlockSpec((t, HEAD_DIM), lambda i: (0, 0))], [jax.ShapeDtypeStruct((t, HEAD_DIM), F32)],
                 [proj, b_pad], sem=("arbitrary",), nbytes=2 * _nbytes((t, HEAD_DIM), F32))[0]


def _forget_bwd(proj, b_pad, dc_key, dc_query, dproj, name):
    t = proj.shape[0]
    tr = min(ROW_BLOCK, t)
    nb = t // tr
    rev = lambda i: nb - 1 - i

    def body(f_ref, b_ref, dck_ref, dcq_ref, dproj_in, df_ref, db_ref, run_ref):
        del dproj_in

        @pl.when(pl.program_id(0) == 0)
        def _():
            run_ref[...] = jnp.zeros_like(run_ref)
            db_ref[...] = jnp.zeros_like(db_ref)

        tri = _tri(False)
        b = b_ref[...]
        lane = lax.broadcasted_iota(jnp.int32, (CHUNK, HEAD_DIM), 1)
        df_ref[...] = jnp.zeros_like(df_ref)
        for n in reversed(range(tr // CHUNK)):
            rows = slice(n * CHUNK, (n + 1) * CHUNK)
            dc = dck_ref[rows, :]
            for h in range(N_HEADS):
                dc = dc + jnp.where(lane == h, dcq_ref[rows, h * HEAD_DIM:(h + 1) * HEAD_DIM], 0.0)
            dlogf = _dot_exact(tri, dc) + run_ref[...]
            run_ref[...] = dlogf[0:1, :]
            z = f_ref[rows, :] + b
            e = jnp.exp(-jnp.abs(z))
            sig_neg = jnp.where(z >= 0.0, e, 1.0) / (1.0 + e)
            df = jnp.where(lane < N_HEADS, dlogf * sig_neg, 0.0)
            df_ref[rows, :HEAD_DIM] = df.astype(BF16)
            db_ref[...] += jnp.sum(df, axis=0, keepdims=True)

    return _call(body, name, (nb,),
                 [pl.BlockSpec((tr, HEAD_DIM), lambda i: (rev(i), CB_F)), pl.BlockSpec((1, HEAD_DIM), lambda i: (0, 0)),
                  pl.BlockSpec((tr, HEAD_DIM), lambda i: (rev(i), 0)), pl.BlockSpec((tr, D_MODEL), lambda i: (rev(i), 0)),
                  pl.BlockSpec(memory_space=pl.ANY)],
                 [pl.BlockSpec((tr, F_WIDTH), lambda i: (rev(i), GATED // F_WIDTH)),
                  pl.BlockSpec((1, HEAD_DIM), lambda i: (0, 0))],
                 [jax.ShapeDtypeStruct(dproj.shape, dproj.dtype), jax.ShapeDtypeStruct((1, HEAD_DIM), F32)],
                 [proj, b_pad, dc_key, dc_query, dproj], scratch=[pltpu.VMEM((1, HEAD_DIM), F32)],
                 sem=("arbitrary",), nbytes=4 * _nbytes((tr, D_MODEL), F32), aliases={4: 0})


def _causal(block):
    r = lax.broadcasted_iota(jnp.int32, (block, block), 0)
    c = lax.broadcasted_iota(jnp.int32, (block, block), 1)
    return c <= r


def _lanes(x, width):
    return jnp.concatenate([x] * (width // HEAD_DIM), axis=1)


def _attn_fwd(proj, c_pad, c_row, name, comm=()):
    t = proj.shape[0]
    blk = min(ATTN_BLOCK, t)
    nq = t // blk
    scale = HEAD_DIM ** -0.5

    def body(q_ref, k_ref, v_ref, cp_ref, cr_ref, y_ref, e_ref):
        h, i = pl.program_id(0), pl.program_id(1)
        q = (q_ref[...] * scale).astype(BF16)
        lane = lax.broadcasted_iota(jnp.int32, (blk, HEAD_DIM), 1)
        ci = jnp.sum(jnp.where(lane == h, cp_ref[...], 0.0), axis=1, keepdims=True)

        def step(j, carry, diagonal):
            m, l, acc = carry
            off = pl.multiple_of(j * blk, blk)
            k = k_ref[pl.ds(off, blk), :].astype(BF16)
            v = v_ref[pl.ds(off, blk), :].astype(BF16)
            s = lax.dot_general(q, k, _DIMS["nt"], preferred_element_type=F32)
            s = s + ci - cr_ref[:, pl.ds(off, blk)]
            if diagonal:
                s = jnp.where(_causal(blk), s, NEG)
            m_new = jnp.maximum(m, jnp.max(s, axis=1, keepdims=True))
            alpha = jnp.exp(m - m_new)
            p = jnp.exp(s - m_new)
            l = alpha * l + jnp.sum(p, axis=1, keepdims=True)
            acc = alpha * acc + jnp.dot(p.astype(BF16), v, preferred_element_type=F32)
            return m_new, l, acc

        init = (jnp.full((blk, 1), NEG, F32), jnp.zeros((blk, 1), F32), jnp.zeros((blk, HEAD_DIM), F32))
        carry = lax.fori_loop(0, i, lambda j, c: step(j, c, False), init)
        m, l, acc = step(i, carry, True)
        y_ref[...] = acc / l
        e_ref[...] = jnp.broadcast_to(ci - (m + jnp.log(l)), (blk, HEAD_DIM))

    nbytes = 2 * _nbytes((t, HEAD_DIM), F32) + 4 * _nbytes((blk, HEAD_DIM), F32) + 4 * _nbytes((blk, blk), F32)
    head_block = pl.BlockSpec((blk, HEAD_DIM), lambda h, i: (i, h))
    return _call(body, name, (N_HEADS, nq),
                 [pl.BlockSpec((blk, HEAD_DIM), lambda h, i: (i, CB_QKV + 3 * h)),
                  pl.BlockSpec((t, HEAD_DIM), lambda h, i: (0, CB_QKV + 3 * h + 1)),
                  pl.BlockSpec((t, HEAD_DIM), lambda h, i: (0, CB_QKV + 3 * h + 2)),
                  pl.BlockSpec((blk, HEAD_DIM), lambda h, i: (i, 0)),
                  pl.BlockSpec((None, 1, t), lambda h, i: (h, 0, 0))],
                 [head_block, head_block],
                 [jax.ShapeDtypeStruct((t, D_MODEL), F32), jax.ShapeDtypeStruct((t, D_MODEL), F32)],
                 [proj, proj, proj, c_pad, c_row], sem=("parallel", "arbitrary"), nbytes=nbytes, comm=comm)


def _attn_bwd(proj, dy, e_b, delta_b, c_row, dproj, name, comm=()):
    t = proj.shape[0]
    blk = min(ATTN_BLOCK, t)
    nb = t // blk
    scale = HEAD_DIM ** -0.5

    def body(q_ref, k_ref, v_ref, dy_ref, e_ref, dl_ref, cr_ref, dproj_in, dqkv_ref, dc_ref, dr_ref, dq_acc):
        del dproj_in
        j = pl.program_id(1)

        @pl.when(j == 0)
        def _():
            dq_acc[...] = jnp.zeros_like(dq_acc)
            dr_ref[...] = jnp.zeros_like(dr_ref)

        k = k_ref[...].astype(BF16)
        v = v_ref[...].astype(BF16)
        crow = cr_ref[...]

        def step(i, carry, diagonal):
            dk, dv, dcs = carry
            rows = pl.ds(pl.multiple_of(i * blk, blk), blk)
            q = (q_ref[rows, :] * scale).astype(BF16)
            dyi = dy_ref[rows, :].astype(BF16)
            s = lax.dot_general(q, k, _DIMS["nt"], preferred_element_type=F32) + _lanes(e_ref[rows, :], blk) - crow
            if diagonal:
                s = jnp.where(_causal(blk), s, NEG)
            p = jnp.exp(s)
            dv = dv + lax.dot_general(p.astype(BF16), dyi, _DIMS["tn"], preferred_element_type=F32)
            dp = lax.dot_general(dyi, v, _DIMS["nt"], preferred_element_type=F32)
            ds = p * (dp - _lanes(dl_ref[rows, :], blk))
            dsb = ds.astype(BF16)
            dk = dk + lax.dot_general(dsb, q, _DIMS["tn"], preferred_element_type=F32)
            dq_acc[rows, :] += jnp.dot(dsb, k, preferred_element_type=F32) * scale
            dr_ref[rows, :] += jnp.sum(ds, axis=1, keepdims=True)
            return dk, dv, dcs + jnp.sum(ds, axis=0, keepdims=True)

        zeros = jnp.zeros((blk, HEAD_DIM), F32)
        carry = step(j, (zeros, zeros, jnp.zeros((1, blk), F32)), True)
        dk, dv, dcs = lax.fori_loop(j + 1, nb, lambda i, c: step(i, c, False), carry)
        mine = pl.ds(pl.multiple_of(j * blk, blk), blk)
        dqkv_ref[:, :HEAD_DIM] = dq_acc[mine, :].astype(BF16)
        dqkv_ref[:, HEAD_DIM:2 * HEAD_DIM] = dk.astype(BF16)
        dqkv_ref[:, 2 * HEAD_DIM:] = dv.astype(BF16)
        dc_ref[...] = -dcs

    full = lambda cb: pl.BlockSpec((t, HEAD_DIM), lambda h, j: (0, cb(h)))
    head = lambda h: h
    nbytes = (6 * _nbytes((t, HEAD_DIM), F32) + 4 * _nbytes((blk, HEAD_DIM), F32) + 5 * _nbytes((blk, blk), F32))
    return _call(body, name, (N_HEADS, nb),
                 [full(lambda h: CB_QKV + 3 * h),
                  pl.BlockSpec((blk, HEAD_DIM), lambda h, j: (j, CB_QKV + 3 * h + 1)),
                  pl.BlockSpec((blk, HEAD_DIM), lambda h, j: (j, CB_QKV + 3 * h + 2)),
                  full(head), full(head), full(head),
                  pl.BlockSpec((None, 1, blk), lambda h, j: (h, 0, j)),
                  pl.BlockSpec(memory_space=pl.ANY)],
                 [pl.BlockSpec((blk, QKV_WIDTH), lambda h, j: (j, (GATED + F_WIDTH) // QKV_WIDTH + h)),
                  pl.BlockSpec((None, 1, blk), lambda h, j: (h, 0, j)),
                  full(head)],
                 [jax.ShapeDtypeStruct(dproj.shape, dproj.dtype), jax.ShapeDtypeStruct((N_HEADS, 1, t), F32),
                  jax.ShapeDtypeStruct((t, D_MODEL), F32)],
                 [proj, proj, proj, dy, e_b, delta_b, c_row, dproj], scratch=[pltpu.VMEM((t, HEAD_DIM), F32)],
                 sem=("parallel", "arbitrary"), nbytes=nbytes, aliases={7: 0}, comm=comm)


def _chunks(t, blk):
    return [slice(r * blk, (r + 1) * blk) for r in range(t // blk)]


def _lane_of_head(x, h):
    lane = lax.broadcasted_iota(jnp.int32, x.shape, 1)
    return jnp.broadcast_to(jnp.sum(jnp.where(lane == h, x, 0.0), axis=1, keepdims=True), x.shape)


def _causal_t(block):
    r = lax.broadcasted_iota(jnp.int32, (block, block), 0)
    c = lax.broadcasted_iota(jnp.int32, (block, block), 1)
    return r <= c


def _attn_fwd_t(proj, c_pad, c_row, name, comm=()):
    t = proj.shape[0]
    blk = min(ATTN_BLOCK, t)
    nq = t // blk
    scale = HEAD_DIM ** -0.5

    def body(q_ref, k_ref, v_ref, cp_ref, cr_ref, y_ref, e_ref, kb_s, vt_s, cb_s):
        h, i = pl.program_id(0), pl.program_id(1)

        @pl.when(i == 0)
        def _():
            for rows in _chunks(t, blk):
                kb_s[rows, :] = k_ref[rows, :].astype(BF16)
                vt_s[:, rows] = v_ref[rows, :].T.astype(BF16)
                cb_s[rows, :] = _lane_of_head(cp_ref[rows, :], h)

        q = (q_ref[...] * scale).astype(BF16)
        ci = cr_ref[:, pl.ds(pl.multiple_of(i * blk, blk), blk)]

        def step(j, carry, diagonal):
            m, l, acc = carry
            rows = pl.ds(pl.multiple_of(j * blk, blk), blk)
            s = lax.dot_general(kb_s[rows, :], q, _DIMS["nt"], preferred_element_type=F32)
            s = s + ci - _lanes(cb_s[rows, :], blk)
            if diagonal:
                s = jnp.where(_causal_t(blk), s, NEG)
            m_new = jnp.maximum(m, jnp.max(s, axis=0, keepdims=True))
            alpha = jnp.exp(m - m_new)
            p = jnp.exp(s - m_new)
            l = alpha * l + jnp.sum(p, axis=0, keepdims=True)
            acc = alpha * acc + jnp.dot(vt_s[:, rows], p.astype(BF16), preferred_element_type=F32)
            return m_new, l, acc

        init = (jnp.full((1, blk), NEG, F32), jnp.zeros((1, blk), F32), jnp.zeros((HEAD_DIM, blk), F32))
        carry = lax.fori_loop(0, i, lambda j, c: step(j, c, False), init)
        m, l, acc = step(i, carry, True)
        y_ref[...] = (acc / l).T
        e_ref[...] = ci - (m + jnp.log(l))

    nbytes = 3 * _nbytes((t, HEAD_DIM), F32) + 4 * _nbytes((blk, HEAD_DIM), F32) + 4 * _nbytes((blk, blk), F32)
    return _call(body, name, (N_HEADS, nq),
                 [pl.BlockSpec((blk, HEAD_DIM), lambda h, i: (i, CB_QKV + 3 * h)),
                  pl.BlockSpec((t, HEAD_DIM), lambda h, i: (0, CB_QKV + 3 * h + 1)),
                  pl.BlockSpec((t, HEAD_DIM), lambda h, i: (0, CB_QKV + 3 * h + 2)),
                  pl.BlockSpec((t, HEAD_DIM), lambda h, i: (0, 0)),
                  pl.BlockSpec((None, 1, t), lambda h, i: (h, 0, 0))],
                 [pl.BlockSpec((blk, HEAD_DIM), lambda h, i: (i, h)),
                  pl.BlockSpec((None, 1, blk), lambda h, i: (h, 0, i))],
                 [jax.ShapeDtypeStruct((t, D_MODEL), F32), jax.ShapeDtypeStruct((N_HEADS, 1, t), F32)],
                 [proj, proj, proj, c_pad, c_row],
                 scratch=[pltpu.VMEM((t, HEAD_DIM), BF16), pltpu.VMEM((HEAD_DIM, t), BF16),
                          pltpu.VMEM((t, HEAD_DIM), F32)],
                 sem=("parallel", "arbitrary"), nbytes=nbytes, comm=comm)


def _attn_bwd_t(proj, dy, e_row, delta_b, c_pad, dproj, name, comm=()):
    t = proj.shape[0]
    blk = min(ATTN_BLOCK, t)
    nb = t // blk
    scale = HEAD_DIM ** -0.5

    def body(q_ref, k_ref, v_ref, dy_ref, e_ref, dl_ref, cp_ref, dproj_in, dqkv_ref, dck_ref, dcq_ref,
             qb_s, qt_s, dyb_s, dyt_s, dl_s, dqt_acc):
        del dproj_in
        h, j = pl.program_id(0), pl.program_id(1)

        @pl.when(j == 0)
        def _():
            for rows in _chunks(t, blk):
                qs = q_ref[rows, :] * scale
                qb_s[rows, :] = qs.astype(BF16)
                qt_s[:, rows] = qs.T.astype(BF16)
                dyr = dy_ref[rows, :]
                dyb_s[rows, :] = dyr.astype(BF16)
                dyt_s[:, rows] = dyr.T.astype(BF16)
                dl_s[:, rows] = dl_ref[rows, :].T[0:1, :]
            dqt_acc[...] = jnp.zeros_like(dqt_acc)
            dcq_ref[...] = jnp.zeros_like(dcq_ref)

        kf = k_ref[...]
        kb, kt = kf.astype(BF16), kf.T.astype(BF16)
        vb = v_ref[...].astype(BF16)
        cj = _lanes(_lane_of_head(cp_ref[...], h), blk)

        def step(i, carry, diagonal):
            dkt, dvt, dkey = carry
            cols = pl.ds(pl.multiple_of(i * blk, blk), blk)
            qi, dyi = qb_s[cols, :], dyb_s[cols, :]
            s = lax.dot_general(kb, qi, _DIMS["nt"], preferred_element_type=F32) + e_ref[:, cols] - cj
            if diagonal:
                s = jnp.where(_causal_t(blk), s, NEG)
            p = jnp.exp(s)
            dp = lax.dot_general(vb, dyi, _DIMS["nt"], preferred_element_type=F32)
            ds = p * (dp - dl_s[:, cols])
            pb, dsb = p.astype(BF16), ds.astype(BF16)
            dvt = dvt + lax.dot_general(dyt_s[:, cols], pb, _DIMS["nt"], preferred_element_type=F32)
            dkt = dkt + lax.dot_general(qt_s[:, cols], dsb, _DIMS["nt"], preferred_element_type=F32)
            dqt_acc[:, cols] += jnp.dot(kt, dsb, preferred_element_type=F32) * scale
            dcq_ref[:, cols] += jnp.sum(ds, axis=0, keepdims=True)
            part = ds[:, :HEAD_DIM]
            for g in range(1, blk // HEAD_DIM):
                part = part + ds[:, g * HEAD_DIM:(g + 1) * HEAD_DIM]
            return dkt, dvt, dkey + part

        zeros = jnp.zeros((HEAD_DIM, blk), F32)
        carry = step(j, (zeros, zeros, jnp.zeros((blk, HEAD_DIM), F32)), True)
        dkt, dvt, dkey = lax.fori_loop(j + 1, nb, lambda i, c: step(i, c, False), carry)
        mine = pl.ds(pl.multiple_of(j * blk, blk), blk)
        dqkv_ref[:, :HEAD_DIM] = dqt_acc[:, mine].T.astype(BF16)
        dqkv_ref[:, HEAD_DIM:2 * HEAD_DIM] = dkt.T.astype(BF16)
        dqkv_ref[:, 2 * HEAD_DIM:] = dvt.T.astype(BF16)
        dck_ref[...] = -jnp.sum(dkey.T, axis=0, keepdims=True)

    full = lambda cb: pl.BlockSpec((t, HEAD_DIM), lambda h, j: (0, cb(h)))
    head = lambda h: h
    row = pl.BlockSpec((None, 1, t), lambda h, j: (h, 0, 0))
    nbytes = (8 * _nbytes((t, HEAD_DIM), F32) + 4 * _nbytes((blk, HEAD_DIM), F32) + 6 * _nbytes((blk, blk), F32))
    return _call(body, name, (N_HEADS, nb),
                 [full(lambda h: CB_QKV + 3 * h),
                  pl.BlockSpec((blk, HEAD_DIM), lambda h, j: (j, CB_QKV + 3 * h + 1)),
                  pl.BlockSpec((blk, HEAD_DIM), lambda h, j: (j, CB_QKV + 3 * h + 2)),
                  full(head), row, full(head),
                  pl.BlockSpec((blk, HEAD_DIM), lambda h, j: (j, 0)),
                  pl.BlockSpec(memory_space=pl.ANY)],
                 [pl.BlockSpec((blk, QKV_WIDTH), lambda h, j: (j, (GATED + F_WIDTH) // QKV_WIDTH + h)),
                  pl.BlockSpec((None, 1, blk), lambda h, j: (h, 0, j)),
                  row],
                 [jax.ShapeDtypeStruct(dproj.shape, dproj.dtype), jax.ShapeDtypeStruct((N_HEADS, 1, t), F32),
                  jax.ShapeDtypeStruct((N_HEADS, 1, t), F32)],
                 [proj, proj, proj, dy, e_row, delta_b, c_pad, dproj],
                 scratch=[pltpu.VMEM((t, HEAD_DIM), BF16), pltpu.VMEM((HEAD_DIM, t), BF16),
                          pltpu.VMEM((t, HEAD_DIM), BF16), pltpu.VMEM((HEAD_DIM, t), BF16),
                          pltpu.VMEM((1, t), F32), pltpu.VMEM((HEAD_DIM, t), F32)],
                 sem=("parallel", "arbitrary"), nbytes=nbytes, aliases={7: 0}, comm=comm)


def _forget_bwd_t(proj, b_pad, dc, dproj, name):
    t = proj.shape[0]
    tr = min(ROW_BLOCK, t)
    nb = t // tr
    rev = lambda i: nb - 1 - i

    def body(f_ref, b_ref, dc_ref, dproj_in, df_ref, db_ref, run_ref):
        del dproj_in

        @pl.when(pl.program_id(0) == 0)
        def _():
            run_ref[...] = jnp.zeros_like(run_ref)
            db_ref[...] = jnp.zeros_like(db_ref)

        tri = _tri(False)
        b = b_ref[...]
        lane = lax.broadcasted_iota(jnp.int32, (CHUNK, HEAD_DIM), 1)
        df_ref[...] = jnp.zeros_like(df_ref)
        for n in reversed(range(tr // CHUNK)):
            rows = slice(n * CHUNK, (n + 1) * CHUNK)
            dlogf = _dot_exact(tri, dc_ref[rows, :]) + run_ref[...]
            run_ref[...] = dlogf[0:1, :]
            z = f_ref[rows, :] + b
            e = jnp.exp(-jnp.abs(z))
            sig_neg = jnp.where(z >= 0.0, e, 1.0) / (1.0 + e)
            df = jnp.where(lane < N_HEADS, dlogf * sig_neg, 0.0)
            df_ref[rows, :HEAD_DIM] = df.astype(BF16)
            db_ref[...] += jnp.sum(df, axis=0, keepdims=True)

    return _call(body, name, (nb,),
                 [pl.BlockSpec((tr, HEAD_DIM), lambda i: (rev(i), CB_F)), pl.BlockSpec((1, HEAD_DIM), lambda i: (0, 0)),
                  pl.BlockSpec((tr, HEAD_DIM), lambda i: (rev(i), 0)), pl.BlockSpec(memory_space=pl.ANY)],
                 [pl.BlockSpec((tr, F_WIDTH), lambda i: (rev(i), GATED // F_WIDTH)),
                  pl.BlockSpec((1, HEAD_DIM), lambda i: (0, 0))],
                 [jax.ShapeDtypeStruct(dproj.shape, dproj.dtype), jax.ShapeDtypeStruct((1, HEAD_DIM), F32)],
                 [proj, b_pad, dc, dproj], scratch=[pltpu.VMEM((1, HEAD_DIM), F32)],
                 sem=("arbitrary",), nbytes=4 * _nbytes((tr, HEAD_DIM), F32), aliases={3: 0})


_GELU_K = math.sqrt(2.0 / math.pi)
_GELU_C = 0.044715


def _gelu(x):
    t = jnp.tanh(_GELU_K * (x + _GELU_C * (x * x * x)))
    return 0.5 * x * (1.0 + t), t


def _gelu_grad(x, t):
    return 0.5 * (1.0 + t) + 0.5 * x * (1.0 - t * t) * (_GELU_K * (1.0 + 3.0 * _GELU_C * (x * x)))


def _layernorm_stats(a):
    mu = jnp.mean(a, axis=-1, keepdims=True)
    xc = a - mu
    r = lax.rsqrt(jnp.mean(xc * xc, axis=-1, keepdims=True) + EPS)
    return xc * r, r


def _group(g):
    return slice(g * CHUNK, (g + 1) * CHUNK)


def _gated_cols(k):
    return pl.BlockSpec((CHUNK, D_MODEL), lambda i: (i, k))


def _fixed(shape):
    return pl.BlockSpec(shape, lambda i: (0,) * len(shape))


def _mix_fwd(proj, y_b, w_s, b_cols, g_v, name):
    t = proj.shape[0]

    def body(u_ref, vs_ref, ga_ref, gb_ref, yb_ref, w_ref, b_ref, gv_ref, o_ref):
        a_u, _ = _gelu(u_ref[...])
        a_v, _ = _gelu(vs_ref[...])
        xhat, _ = _layernorm_stats(a_v)
        vn = (xhat * gv_ref[...]).astype(BF16)
        sa, sb = _sigmoid(ga_ref[...]), _sigmoid(gb_ref[...])
        yb = yb_ref[...]
        mask = _causal(CHUNK)
        for g in range(N_GROUPS):
            cols = _group(g)
            w = jnp.where(mask, w_ref[g], 0.0).astype(BF16)
            mixed = jnp.dot(w, vn[:, cols], preferred_element_type=F32) + b_ref[:, g:g + 1]
            o_ref[:, cols] = (sa[:, cols] * (a_u[:, cols] * mixed) + sb[:, cols] * yb[:, cols]).astype(BF16)

    return _call(body, name, (t // CHUNK,),
                 [_gated_cols(0), _gated_cols(1), _gated_cols(2), _gated_cols(3), _row_spec(CHUNK, D_MODEL),
                  _fixed((N_GROUPS, CHUNK, CHUNK)), _fixed((CHUNK, N_GROUPS)), _fixed((1, D_MODEL))],
                 [_row_spec(CHUNK, D_MODEL)], [jax.ShapeDtypeStruct((t, D_MODEL), BF16)],
                 [proj, proj, proj, proj, y_b, w_s, b_cols, g_v], sem=("parallel",),
                 nbytes=16 * _nbytes((CHUNK, D_MODEL), F32))[0]


def _mix_bwd(proj, y_b, dmerged, w_s, b_cols, g_v, sel, name, comm=()):
    t = proj.shape[0]

    def body(u_ref, vs_ref, ga_ref, gb_ref, yb_ref, dm_ref, w_ref, b_ref, gv_ref, sel_ref,
             dg_ref, dyb_ref, delta_ref, dw_ref, dbt_ref, dgv_ref):
        @pl.when(pl.program_id(0) == 0)
        def _():
            dw_ref[...] = jnp.zeros_like(dw_ref)
            dbt_ref[...] = jnp.zeros_like(dbt_ref)
            dgv_ref[...] = jnp.zeros_like(dgv_ref)

        u, vs = u_ref[...], vs_ref[...]
        a_u, t_u = _gelu(u)
        a_v, t_v = _gelu(vs)
        xhat, r = _layernorm_stats(a_v)
        gv = gv_ref[...]
        vn = (xhat * gv).astype(BF16)
        sa, sb = _sigmoid(ga_ref[...]), _sigmoid(gb_ref[...])
        yb, dm = yb_ref[...], dm_ref[...].astype(F32)
        dyb = dm * sb
        dyb_ref[...] = dyb
        dyb_yb = dyb * yb
        dg_ref[:, 3 * D_MODEL:] = (dm * yb * (sb * (1.0 - sb))).astype(BF16)
        dya = dm * sa
        mask = _causal(CHUNK)
        dmixed_parts, dvn_parts = [], []
        for g in range(N_GROUPS):
            cols = _group(g)
            delta_ref[:, cols] = jnp.broadcast_to(jnp.sum(dyb_yb[:, cols], axis=1, keepdims=True), (CHUNK, CHUNK))
            w = jnp.where(mask, w_ref[g], 0.0).astype(BF16)
            mixed = jnp.dot(w, vn[:, cols], preferred_element_type=F32) + b_ref[:, g:g + 1]
            y_a = a_u[:, cols] * mixed
            dg_ref[:, 2 * D_MODEL + g * CHUNK:2 * D_MODEL + (g + 1) * CHUNK] = (
                dm[:, cols] * y_a * (sa[:, cols] * (1.0 - sa[:, cols]))).astype(BF16)
            dg_ref[:, cols] = (dya[:, cols] * mixed * _gelu_grad(u[:, cols], t_u[:, cols])).astype(BF16)
            dmixed = dya[:, cols] * a_u[:, cols]
            dmb = dmixed.astype(BF16)
            dw = lax.dot_general(dmb, vn[:, cols], _DIMS["nt"], preferred_element_type=F32)
            dw_ref[g] += jnp.where(mask, dw, 0.0)
            dvn_parts.append(lax.dot_general(w, dmb, _DIMS["tn"], preferred_element_type=F32))
            dmixed_parts.append(dmixed)
        dmixed_all = jnp.concatenate(dmixed_parts, axis=1)
        dvn = jnp.concatenate(dvn_parts, axis=1)
        dbt_ref[...] += _dot_exact_rhs(dmixed_all, sel_ref[...])
        dgv_ref[...] += jnp.sum(dvn * xhat, axis=0, keepdims=True)
        dxh = dvn * gv
        da_v = r * (dxh - jnp.mean(dxh, axis=-1, keepdims=True)
                    - xhat * jnp.mean(dxh * xhat, axis=-1, keepdims=True))
        dg_ref[:, D_MODEL:2 * D_MODEL] = (da_v * _gelu_grad(vs, t_v)).astype(BF16)

    row = lambda width: _row_spec(CHUNK, width)
    return _call(body, name, (t // CHUNK,),
                 [_gated_cols(0), _gated_cols(1), _gated_cols(2), _gated_cols(3), row(D_MODEL), row(D_MODEL),
                  _fixed((N_GROUPS, CHUNK, CHUNK)), _fixed((CHUNK, N_GROUPS)), _fixed((1, D_MODEL)),
                  _fixed((D_MODEL, HEAD_DIM))],
                 [row(GATED), row(D_MODEL), row(D_MODEL),
                  _fixed((N_GROUPS, CHUNK, CHUNK)), _fixed((CHUNK, HEAD_DIM)), _fixed((1, D_MODEL))],
                 [jax.ShapeDtypeStruct((t, IN_PAD), BF16), jax.ShapeDtypeStruct((t, D_MODEL), F32),
                  jax.ShapeDtypeStruct((t, D_MODEL), F32), jax.ShapeDtypeStruct((N_GROUPS, CHUNK, CHUNK), F32),
                  jax.ShapeDtypeStruct((CHUNK, HEAD_DIM), F32), jax.ShapeDtypeStruct((1, D_MODEL), F32)],
                 [proj, proj, proj, proj, y_b, dmerged, w_s, b_cols, g_v, sel], sem=("arbitrary",),
                 nbytes=40 * _nbytes((CHUNK, D_MODEL), F32), comm=comm)


def _adamw(w, g, m, v, name):
    r, c = w.shape
    tr = _tile(r, 256, unit=8)

    def body(w_ref, g_ref, m_ref, v_ref, d_ref, nm_ref, nv_ref):
        gv = g_ref[...]
        nm = ADAM_B1 * m_ref[...] + (1.0 - ADAM_B1) * gv
        nv = ADAM_B2 * v_ref[...] + (1.0 - ADAM_B2) * (gv * gv)
        m_hat = nm / (1.0 - ADAM_B1 ** ADAM_STEP)
        v_hat = nv / (1.0 - ADAM_B2 ** ADAM_STEP)
        d_ref[...] = -ADAM_LR * (m_hat / (jnp.sqrt(v_hat) + ADAM_EPS) + ADAM_WD * w_ref[...])
        nm_ref[...] = nm
        nv_ref[...] = nv

    spec = pl.BlockSpec((tr, c), lambda i: (i, 0))
    shape = jax.ShapeDtypeStruct((r, c), F32)
    return _call(body, name, (r // tr,), [spec] * 4, [spec] * 3, [shape] * 3, [w, g, m, v], sem=("parallel",),
                 nbytes=7 * _nbytes((tr, max(c, 128)), F32))


def _pad_rows(a, rows):
    return jnp.pad(a, ((0, rows - a.shape[0]), (0, 0)))


def _pack_rest(w_out, w_gate, w_up, w_down):
    return jnp.concatenate([w_out, w_gate.T, w_up.T, w_down], axis=0).astype(BF16)


def _unpack_in(gathered):
    i = jnp.arange(16)[None, :, None]
    b = jnp.arange(N_DEV)[:, None, None]
    head = jnp.where(i < b, jnp.roll(gathered[:, IN_STRIDE:IN_STRIDE + 16], 1, axis=0), gathered[:, :16])
    nat = jnp.concatenate([head, gathered[:, 16:IN_STRIDE]], axis=1).reshape(N_DEV * IN_STRIDE, D_MODEL)
    qkv = nat[2 * D_MODEL:5 * D_MODEL].reshape(3, N_HEADS, HEAD_DIM, D_MODEL)
    qkv = jnp.transpose(qkv, (1, 0, 2, 3)).reshape(3 * D_MODEL, D_MODEL)
    f = _pad_rows(gathered[N_DEV - 1, IN_STRIDE:IN_STRIDE + N_HEADS], F_WIDTH)
    return jnp.concatenate([nat[:2 * D_MODEL], nat[5 * D_MODEL:7 * D_MODEL], f, qkv], axis=0)


def _unpack_rest(gathered):
    take = lambda lo, n: gathered[:, lo:lo + n, :].reshape(N_DEV * n, D_MODEL)
    return take(R_OUT, SH_OUT), jnp.concatenate([take(R_G, SH_FF), take(R_U, SH_FF)], axis=0), take(R_D, SH_FF)


def _pack_grad_in(dw_in_t):
    qkv = dw_in_t[GATED + F_WIDTH:].reshape(N_HEADS, 3, HEAD_DIM, D_MODEL)
    qkv = jnp.transpose(qkv, (1, 0, 2, 3)).reshape(3 * D_MODEL, D_MODEL)
    return jnp.concatenate([dw_in_t[:2 * D_MODEL], qkv, dw_in_t[2 * D_MODEL:GATED],
                            dw_in_t[GATED:GATED + IN_ROWS - N_DEV * IN_STRIDE]], axis=0)


def _pack_grad_rest(dw_out, dw_gu_t, dw_down):
    split = lambda a, n: a.reshape(N_DEV, n, D_MODEL)
    return jnp.concatenate([split(dw_out, SH_OUT), split(dw_gu_t[:D_FF], SH_FF), split(dw_gu_t[D_FF:], SH_FF),
                            split(dw_down, SH_FF)], axis=1)


def kernel(x, mix_pre_g, w_in, b_forget, sgu_norm_g, w_spatial, b_spatial, w_out, mix_post_g, ffn_pre_g, w_gate, w_up, w_down, ffn_post_g, loss_target, m_mix_pre_g, m_w_in, m_b_forget, m_sgu_norm_g, m_w_spatial, m_b_spatial, m_w_out, m_mix_post_g, m_ffn_pre_g, m_w_gate, m_w_up, m_w_down, m_ffn_post_g, v_mix_pre_g, v_w_in, v_b_forget, v_sgu_norm_g, v_w_spatial, v_b_spatial, v_w_out, v_mix_post_g, v_ffn_pre_g, v_w_gate, v_w_up, v_w_down, v_ffn_post_g):
    depth = w_in.shape[0]
    assert depth == 2
    t = x.shape[1]
    x0 = x.reshape(t, D_MODEL)
    target = loss_target.reshape(t, D_MODEL)
    sel = (jnp.arange(D_MODEL)[:, None] // HEAD_DIM == jnp.arange(HEAD_DIM)[None, :]).astype(BF16)
    vec = lambda a, l: a[l][None, :]
    zero_row = jnp.zeros((1, D_MODEL), F32)

    me = 4 * lax.axis_index("x") + 2 * lax.axis_index("y") + lax.axis_index("c")
    p_in = [lax.dynamic_update_slice(jnp.zeros((P_IN, D_MODEL), BF16), w_in[l].T.astype(BF16), (me, 0))
            for l in range(depth)]
    p_rest = [_pack_rest(w_out[l], w_gate[l], w_up[l], w_down[l]) for l in range(depth)]
    w_in_t = [None] * depth
    w_rest = [None] * depth
    w_in_t[0] = _unpack_in(_exchange("gather", p_in[0], "gather_w_in_0")[0])

    saved = []
    xl = x0
    h = _rms_fwd(xl, vec(mix_pre_g, 0), "rms_in_0")
    dy = loss = None
    for l in range(depth):
        b_pad = jnp.pad(b_forget[l], (0, HEAD_DIM - N_HEADS))[None, :]
        b_cols = b_spatial[l].T
        proj = _matmul(h, w_in_t[l], "nt", F32, f"proj_{l}")
        c_pad = _forget_cumsum(proj, b_pad, f"forget_cumsum_{l}")
        c_row = c_pad[:, :N_HEADS].T[:, None, :]
        riders = [("gather", p_rest[l])] + ([("gather", p_in[l + 1])] if l + 1 < depth else [])
        y_b, e_row, *arrived = _attn_fwd_t(proj, c_pad, c_row, f"attn_fwd_{l}", comm=riders)
        w_rest[l] = _unpack_rest(arrived[0])
        if l + 1 < depth:
            w_in_t[l + 1] = _unpack_in(arrived[1])
        w_o, w_gu_t, w_d = w_rest[l]
        merged = _mix_fwd(proj, y_b, w_spatial[l], b_cols, vec(sgu_norm_g, l), f"mix_fwd_{l}")
        o = _matmul(merged, w_o, "nn", F32, f"out_proj_{l}")
        x1, h2 = _post_norm_fwd(xl, o, vec(mix_post_g, l), vec(ffn_pre_g, l), f"post_mix_{l}")
        gu = _matmul(h2, w_gu_t, "nt", F32, f"ffn_gu_{l}")
        act = _swiglu_fwd(gu, f"swiglu_fwd_{l}")
        dn = _matmul(act, w_d, "nn", F32, f"ffn_down_{l}")
        saved.append(dict(x0=xl, h=h, proj=proj, b_pad=b_pad, b_cols=b_cols, c_pad=c_pad, y_b=y_b, e_row=e_row,
                          merged=merged, o=o, x1=x1, h2=h2, gu=gu, act=act, dn=dn))
        if l + 1 < depth:
            xl, h = _post_norm_fwd(x1, dn, vec(ffn_post_g, l), vec(mix_pre_g, l + 1), f"post_ffn_{l}")
        else:
            dy, loss = _post_norm_loss(x1, dn, vec(ffn_post_g, l), target, "loss")

    g_small = [None] * depth
    grad_sums = {}
    pending = []
    late_rows = None
    dx = dy
    dd, dg_ffn_post = _norm_bwd(dx, f"bwd_ffn_post_{depth - 1}", post=(saved[-1]["dn"], vec(ffn_post_g, depth - 1)))
    for l in reversed(range(depth)):
        s = saved[l]
        w_o, w_gu_t, w_d = w_rest[l]
        dact = _matmul(dd, w_d, "nt", F32, f"d_act_{l}")
        dw_d = _matmul(s["act"], dd, "tn", BF16, f"dw_down_{l}")
        dgu = _swiglu_bwd(s["gu"], dact, f"swiglu_bwd_{l}")
        dh2 = _matmul(dgu, w_gu_t, "nn", F32, f"d_h2_{l}")
        dw_gu_t = _matmul(dgu, s["h2"], "tn", BF16, f"dw_gu_{l}")
        dx1, dg_ffn_pre, do, dg_mix_post = _norm_bwd(
            dx, f"bwd_post_mix_{l}", pre=(dh2, s["x1"], vec(ffn_pre_g, l)), post=(s["o"], vec(mix_post_g, l)))
        dmerged = _matmul(do, w_o, "nt", F32, f"d_merged_{l}")
        dw_o = _matmul(s["merged"], do, "tn", BF16, f"dw_out_{l}")
        dproj, dyb, delta_b, dw_s, dbt, dg_v = _mix_bwd(
            s["proj"], s["y_b"], dmerged, w_spatial[l], s["b_cols"], vec(sgu_norm_g, l), sel, f"mix_bwd_{l}")
        db_s = dbt[:, :N_GROUPS].T.reshape(1, D_MODEL)
        riders = pending + [("scatter", _pack_grad_rest(dw_o, dw_gu_t, dw_d))]
        keys = ([("in", l + 1)] if pending else []) + [("rest", l)] + (["small"] if l == 0 else [])
        if l == 0:
            g_small[0] = jnp.concatenate([zero_row, dg_v, dg_mix_post, dg_ffn_pre, dg_ffn_post, db_s, zero_row,
                                          zero_row, dw_s.reshape(CHUNK, D_MODEL)], axis=0)
            riders = riders + [("gather", jnp.concatenate([g_small[1], g_small[0]], axis=0))]
        dproj, dc_key, dc_query, *arrived = _attn_bwd_t(s["proj"], dyb, s["e_row"], delta_b, s["c_pad"], dproj,
                                                          f"attn_bwd_{l}", comm=riders)
        for key, got in zip(keys, arrived):
            grad_sums[key] = got
        dc = jnp.pad((dc_key + dc_query)[:, 0, :].T, ((0, 0), (0, HEAD_DIM - N_HEADS)))
        dproj, db_f = _forget_bwd_t(s["proj"], s["b_pad"], dc, dproj, f"forget_bwd_{l}")
        db_f_row = jnp.pad(db_f, ((0, 0), (0, D_MODEL - HEAD_DIM)))
        dw_in_t = _matmul(dproj, s["h"], "tn", BF16, f"dw_in_{l}")
        if l > 0:
            dh = _matmul(dproj, w_in_t[l], "nn", F32, f"d_h_{l}")
            pending = [("windows", _pack_grad_in(dw_in_t))]
            dx, dg_mix_pre, dd, dg_ffn_post_below = _norm_bwd(
                dx1, f"bwd_pre_mix_{l}", pre=(dh, s["x0"], vec(mix_pre_g, l)),
                post=(saved[l - 1]["dn"], vec(ffn_post_g, l - 1)))
            g_small[l] = jnp.concatenate([dg_mix_pre, dg_v, dg_mix_post, dg_ffn_pre, dg_ffn_post, db_s, db_f_row,
                                          zero_row, dw_s.reshape(CHUNK, D_MODEL)], axis=0)
            dg_ffn_post = dg_ffn_post_below
        else:
            dh, grad_sums[("in", 0)] = _matmul(dproj, w_in_t[l], "nn", F32, f"d_h_{l}",
                                               comm=[("windows", _pack_grad_in(dw_in_t))])
            dx, dg_mix_pre = _norm_bwd(dx1, f"bwd_pre_mix_{l}", pre=(dh, s["x0"], vec(mix_pre_g, l)))
            late_rows = jnp.concatenate([dg_mix_pre, db_f_row] + [zero_row] * 6, axis=0)
    grad_x = dx.reshape(x.shape)

    late = _sum_slots(_exchange("gather", late_rows, "gather_late_small_grads")[0], 8, "sum_late_small_grads")
    small_sum = _sum_slots(grad_sums["small"], S_ROWS, "sum_small_grads")
    loss = lax.psum(loss.reshape(()), ("x", "y", "c"))

    sm = small_sum.reshape(depth, S_ROWS, D_MODEL)[::-1]
    g = {
        "mix_pre_g": jnp.stack([late[0], sm[1, 0]]), "sgu_norm_g": sm[:, 1], "mix_post_g": sm[:, 2],
        "ffn_pre_g": sm[:, 3], "ffn_post_g": sm[:, 4], "b_spatial": sm[:, 5].reshape(depth, N_GROUPS, CHUNK),
        "b_forget": jnp.stack([late[1, :N_HEADS], sm[1, 6, :N_HEADS]]),
        "w_spatial": sm[:, 8:].reshape(depth, N_GROUPS, CHUNK, CHUNK),
    }
    g_in = jnp.stack([_sum_slots(grad_sums[("in", l)], 304, f"sum_grad_in_{l}") for l in range(depth)])
    g_rest = jnp.stack([_sum_slots(grad_sums[("rest", l)], 592, f"sum_grad_rest_{l}") for l in range(depth)])
    g["w_in"] = jnp.swapaxes(lax.dynamic_slice(g_in, (0, me, 0), (depth, SH_IN, D_MODEL)), 1, 2)
    g["w_out"] = g_rest[:, R_OUT:R_OUT + SH_OUT]
    g["w_gate"] = jnp.swapaxes(g_rest[:, R_G:R_G + SH_FF], 1, 2)
    g["w_up"] = jnp.swapaxes(g_rest[:, R_U:R_U + SH_FF], 1, 2)
    g["w_down"] = g_rest[:, R_D:R_D + SH_FF]

    names = ["mix_pre_g", "w_in", "b_forget", "sgu_norm_g", "w_spatial", "b_spatial", "w_out", "mix_post_g",
             "ffn_pre_g", "w_gate", "w_up", "w_down", "ffn_post_g"]
    ws = dict(mix_pre_g=mix_pre_g, w_in=w_in, b_forget=b_forget, sgu_norm_g=sgu_norm_g, w_spatial=w_spatial,
              b_spatial=b_spatial, w_out=w_out, mix_post_g=mix_post_g, ffn_pre_g=ffn_pre_g, w_gate=w_gate, w_up=w_up,
              w_down=w_down, ffn_post_g=ffn_post_g)
    ms = dict(mix_pre_g=m_mix_pre_g, w_in=m_w_in, b_forget=m_b_forget, sgu_norm_g=m_sgu_norm_g, w_spatial=m_w_spatial,
              b_spatial=m_b_spatial, w_out=m_w_out, mix_post_g=m_mix_post_g, ffn_pre_g=m_ffn_pre_g, w_gate=m_w_gate,
              w_up=m_w_up, w_down=m_w_down, ffn_post_g=m_ffn_post_g)
    vs = dict(mix_pre_g=v_mix_pre_g, w_in=v_w_in, b_forget=v_b_forget, sgu_norm_g=v_sgu_norm_g, w_spatial=v_w_spatial,
              b_spatial=v_b_spatial, w_out=v_w_out, mix_post_g=v_mix_post_g, ffn_pre_g=v_ffn_pre_g, w_gate=v_w_gate,
              w_up=v_w_up, w_down=v_w_down, ffn_post_g=v_ffn_post_g)
    deltas, new_m, new_v = [], [], []
    for n in names:
        shape = ws[n].shape
        flat = (-1, shape[-1])
        d_w, n_m, n_v = _adamw(ws[n].reshape(flat), g[n].reshape(flat), ms[n].reshape(flat), vs[n].reshape(flat),
                               f"adamw_{n}")
        deltas.append(d_w.reshape(shape))
        new_m.append(n_m.reshape(shape))
        new_v.append(n_v.reshape(shape))
    grads = [g[n].reshape(ws[n].shape) for n in names]
    return (loss, grad_x, *grads, *deltas, *new_m, *new_v)
```

```python
import math

import jax
import jax.numpy as jnp
from jax import lax
from jax.experimental import pallas as pl
from jax.experimental.pallas import tpu as pltpu

F32 = jnp.float32
BF16 = jnp.bfloat16

N_DEV = 8
D_MODEL = 1024
N_HEADS = 8
HEAD_DIM = 128
CHUNK = 128
N_GROUPS = 8
D_FF = 2816
IN_WIDTH = 7 * D_MODEL + N_HEADS
IN_PAD = 7680
EPS = 1e-6
GATED = 4 * D_MODEL
F_WIDTH = 512
QKV_WIDTH = 3 * HEAD_DIM
CB_F = GATED // HEAD_DIM
CB_QKV = (GATED + F_WIDTH) // HEAD_DIM
assert GATED + F_WIDTH + N_HEADS * QKV_WIDTH == IN_PAD and (GATED + F_WIDTH) % QKV_WIDTH == 0

ADAM_LR, ADAM_B1, ADAM_B2, ADAM_EPS, ADAM_WD, ADAM_STEP = 0.001, 0.9, 0.999, 1e-08, 0.01, 10

SH_IN = IN_WIDTH // N_DEV
SH_OUT = D_MODEL // N_DEV
SH_FF = D_FF // N_DEV
P_IN = 912
IN_STRIDE = 7 * HEAD_DIM
IN_ROWS = IN_STRIDE * (N_DEV - 1) + P_IN
R_OUT, R_G, R_U, R_D = 0, SH_OUT, SH_OUT + SH_FF, SH_OUT + 2 * SH_FF
P_REST = SH_OUT + 3 * SH_FF
S_ROWS = 136

ATTN_BLOCK = 512
ROW_BLOCK = 512
MM_TM, MM_TN, MM_TK = 1536, 1536, 2048
VMEM_CAP = 56 << 20
NEG = -1e30


def _tile(n, cap, unit=128):
    if n <= cap:
        return n
    best = None
    for t in range(unit, cap + 1, unit):
        if n % t == 0:
            best = t
    assert best is not None, (n, cap)
    return best


def _nbytes(shape, dtype):
    return math.prod(shape) * jnp.dtype(dtype).itemsize


_HBM = pl.BlockSpec(memory_space=pltpu.HBM)
MESH = pl.DeviceIdType.MESH

_N_REMOTE = N_DEV - 1


def _exchange_shapes(kind, x):
    shape = {"gather": (N_DEV,) + x.shape, "scatter": x.shape, "windows": (N_DEV, P_IN) + x.shape[1:]}[kind]
    return [jax.ShapeDtypeStruct(shape, x.dtype)]


def _exchange_sems(kind):
    del kind
    return [pltpu.SemaphoreType.DMA((_N_REMOTE,)), pltpu.SemaphoreType.DMA((_N_REMOTE,)), pltpu.SemaphoreType.DMA]


def _plan(kind, x_ref, outs, sems):
    send_sems, recv_sems, local_sem = sems
    x, y, c = lax.axis_index("x"), lax.axis_index("y"), lax.axis_index("c")

    def remote(src, dst, k, to):
        return pltpu.make_async_remote_copy(src_ref=src, dst_ref=dst, send_sem=send_sems.at[k], recv_sem=recv_sems.at[k],
                                            device_id=to, device_id_type=MESH)

    sibling = (x, y, 1 - c)
    if kind == "gather":
        out, = outs
        slot = lambda px, py, pc: out.at[4 * px + 2 * py + pc]
        chips = [(1 - x, y), (x, 1 - y), (1 - x, 1 - y)]
        local = [pltpu.make_async_copy(x_ref, slot(x, y, c), local_sem)]
        first = [remote(x_ref, slot(x, y, c), 0, sibling)]
        first += [remote(x_ref, slot(x, y, c), 1 + k, (*chip, c)) for k, chip in enumerate(chips)]
        relays = [(remote(x_ref, slot(*chip, c), 1 + k, (*chip, c)), remote(slot(*chip, c), slot(*chip, c), 4 + k, sibling))
                  for k, chip in enumerate(chips)]
        arrivals = [remote(x_ref, slot(x, y, 1 - c), 0, sibling)]
        arrivals += [remote(x_ref, slot(*chip, 1 - c), 4 + k, sibling) for k, chip in enumerate(chips)]
        return local, first, relays, arrivals
    out, = outs
    mine = 4 * x + 2 * y + c
    if kind == "scatter":
        src = lambda d: x_ref.at[d]
    else:
        src = lambda d: x_ref.at[pl.ds(pl.multiple_of(IN_STRIDE * d, 16), P_IN)]
    first, arrivals = [], []
    for j in range(1, N_DEV):
        peer = (1 - x if j & 4 else x, 1 - y if j & 2 else y, 1 - c if j & 1 else c)
        theirs = 4 * peer[0] + 2 * peer[1] + peer[2]
        first.append(remote(src(theirs), out.at[mine], j - 1, peer))
        arrivals.append(remote(src(theirs), out.at[theirs], j - 1, peer))
    local = [] if local_sem is None else [pltpu.make_async_copy(src(mine), out.at[mine], local_sem)]
    return local, first, [], arrivals


def _start(plan):
    local, first, _, _ = plan
    for cp in local + first:
        cp.start()


def _finish(plan):
    local, first, relays, arrivals = plan
    for arrival, onward in relays:
        arrival.wait_recv()
        onward.start()
    for cp in arrivals:
        cp.wait_recv()
    for cp in first + [onward for _, onward in relays]:
        cp.wait_send()
    for cp in local:
        cp.wait()


def _window_copies(x_ref, land_ref, send_sems, recv_sems):
    _, first, _, arrivals = _plan("windows", x_ref, [land_ref], (send_sems, recv_sems, None))
    return first, arrivals


_SEM = pl.BlockSpec(memory_space=pltpu.SEMAPHORE)
_DATAFLOW = pltpu.SideEffectType.DATAFLOW_SIDE_EFFECTING


def _windows_start(x, land, name):
    def body(x_ref, land_ref, send_sems, recv_sems, x_thru, land_thru, token):
        del x_thru, land_thru
        for cp in _window_copies(x_ref, land_ref, send_sems, recv_sems)[0]:
            cp.start()
        token[...] = jnp.zeros_like(token)

    return pl.pallas_call(
        body, name=name,
        out_shape=(pltpu.SemaphoreType.DMA((_N_REMOTE,)), pltpu.SemaphoreType.DMA((_N_REMOTE,)),
                   pltpu.HBM(x.shape, x.dtype), pltpu.HBM(land.shape, land.dtype), jax.ShapeDtypeStruct((8, 128), F32)),
        in_specs=(_HBM, _HBM), out_specs=(_SEM, _SEM, _HBM, _HBM, pl.BlockSpec(memory_space=pltpu.VMEM)),
        input_output_aliases={0: 2, 1: 3}, compiler_params=pltpu.CompilerParams(has_side_effects=_DATAFLOW),
    )(pltpu.with_memory_space_constraint(x, pltpu.HBM), pltpu.with_memory_space_constraint(land, pltpu.HBM))


def _windows_wait(send_sems, recv_sems, x_thru, land_thru, after, name):
    def body(x_ref, land_ref, send_sems, recv_sems, after_ref, x_dead, got_ref):
        del after_ref, x_dead, got_ref
        first, arrivals = _window_copies(x_ref, land_ref, send_sems, recv_sems)
        for cp in first:
            cp.wait_send()
        for cp in arrivals:
            cp.wait_recv()

    return pl.pallas_call(
        body, name=name,
        out_shape=(pltpu.HBM(x_thru.shape, x_thru.dtype), pltpu.HBM(land_thru.shape, land_thru.dtype)),
        in_specs=(_HBM, _HBM, _SEM, _SEM, pl.BlockSpec(memory_space=pl.ANY)), out_specs=(_HBM, _HBM),
        input_output_aliases={0: 0, 1: 1}, compiler_params=pltpu.CompilerParams(has_side_effects=_DATAFLOW),
    )(x_thru, land_thru, send_sems, recv_sems, after)[1]


def _exchange(kind, x, name):
    n_out = len(_exchange_shapes(kind, x))

    def body(x_ref, *refs):
        plan = _plan(kind, x_ref, refs[:n_out], refs[n_out:])
        _start(plan)
        _finish(plan)

    return pl.pallas_call(body, name=name, out_shape=_exchange_shapes(kind, x), in_specs=[_HBM],
                          out_specs=[_HBM] * n_out, scratch_shapes=_exchange_sems(kind))(x)


def _call(body, name, grid, in_specs, out_specs, out_shape, args, scratch=(), sem=None, nbytes=0, aliases=None,
          comm=(), deps=()):
    in_specs, out_specs, out_shape, args, scratch = (list(in_specs), list(out_specs), list(out_shape), list(args),
                                                     list(scratch))
    if deps:
        n_real, n_deps, unordered = len(args), len(deps), body

        def body(*refs):
            unordered(*refs[:n_real], *refs[n_real + n_deps:])

        in_specs += [pl.BlockSpec(memory_space=pl.ANY)] * n_deps
        args += list(deps)
    if comm:
        n_in, n_out, n_scr, n_ops = len(args), len(out_shape), len(scratch), len(comm)
        kinds = [kind for kind, _ in comm]
        shapes = [_exchange_shapes(kind, x) for kind, x in comm]
        inner = body

        def body(*refs):
            ins, refs = refs[:n_in], refs[n_in:]
            cins, refs = refs[:n_ops], refs[n_ops:]
            outs, refs = refs[:n_out], refs[n_out:]
            couts = []
            for sh in shapes:
                couts.append(refs[:len(sh)])
                refs = refs[len(sh):]
            own_scratch, sems = refs[:n_scr], refs[n_scr:]
            first = last = None
            for axis, size in enumerate(grid):
                at_start, at_end = pl.program_id(axis) == 0, pl.program_id(axis) == size - 1
                first = at_start if first is None else first & at_start
                last = at_end if last is None else last & at_end
            plans = [_plan(kinds[o], cins[o], couts[o], sems[3 * o:3 * o + 3]) for o in range(n_ops)]

            @pl.when(first)
            def _():
                for p in plans:
                    _start(p)

            inner(*ins, *outs, *own_scratch)

            @pl.when(last)
            def _():
                for p in plans:
                    _finish(p)

        in_specs += [_HBM] * n_ops
        args += [x for _, x in comm]
        for kind, sh in zip(kinds, shapes):
            out_shape += sh
            out_specs += [_HBM] * len(sh)
            scratch += _exchange_sems(kind)
        sem = ("arbitrary",) * len(grid)
    limit = int(min(max(2 * nbytes + (8 << 20), 32 << 20), VMEM_CAP))
    return pl.pallas_call(
        body, name=name, grid=grid, in_specs=in_specs, out_specs=out_specs, out_shape=out_shape,
        scratch_shapes=scratch, input_output_aliases=aliases or {},
        compiler_params=pltpu.CompilerParams(dimension_semantics=sem, vmem_limit_bytes=limit),
    )(*args)


def _sum_slots(x, tr, name):
    n, r, c = x.shape

    def body(x_ref, o_ref):
        acc = x_ref[0].astype(F32)
        for d in range(1, n):
            acc = acc + x_ref[d].astype(F32)
        o_ref[...] = acc

    return _call(body, name, (r // tr,), [pl.BlockSpec((n, tr, c), lambda i: (0, i, 0))],
                 [pl.BlockSpec((tr, c), lambda i: (i, 0))], [jax.ShapeDtypeStruct((r, c), F32)], [x],
                 sem=("parallel",), nbytes=_nbytes((n, tr, c), x.dtype) + _nbytes((tr, c), F32))[0]


_DIMS = {"nn": (((1,), (0,)), ((), ())), "nt": (((1,), (1,)), ((), ())), "tn": (((0,), (0,)), ((), ()))}


def _matmul(a, b, mode, out_dtype, name, comm=(), deps=()):
    if mode == "nn":
        (m, k), (k2, n) = a.shape, b.shape
    elif mode == "nt":
        (m, k), (n, k2) = a.shape, b.shape
    else:
        (k, m), (k2, n) = a.shape, b.shape
    assert k == k2, (a.shape, b.shape, mode)
    tm, tn, tk = _tile(m, MM_TM), _tile(n, MM_TN), _tile(k, MM_TK)
    nk = k // tk
    dims = _DIMS[mode]
    a_spec = {"nn": pl.BlockSpec((tm, tk), lambda i, j, kk: (i, kk)),
              "nt": pl.BlockSpec((tm, tk), lambda i, j, kk: (i, kk)),
              "tn": pl.BlockSpec((tk, tm), lambda i, j, kk: (kk, i))}[mode]
    b_spec = {"nn": pl.BlockSpec((tk, tn), lambda i, j, kk: (kk, j)),
              "nt": pl.BlockSpec((tn, tk), lambda i, j, kk: (j, kk)),
              "tn": pl.BlockSpec((tk, tn), lambda i, j, kk: (kk, j))}[mode]

    def partial_product(a_ref, b_ref):
        return lax.dot_general(a_ref[...].astype(BF16), b_ref[...].astype(BF16), dims, preferred_element_type=F32)

    if nk == 1:
        def body(a_ref, b_ref, o_ref):
            o_ref[...] = partial_product(a_ref, b_ref).astype(o_ref.dtype)
        scratch = []
    else:
        def body(a_ref, b_ref, o_ref, acc_ref):
            kk = pl.program_id(2)

            @pl.when(kk == 0)
            def _():
                acc_ref[...] = jnp.zeros_like(acc_ref)

            acc_ref[...] += partial_product(a_ref, b_ref)

            @pl.when(kk == nk - 1)
            def _():
                o_ref[...] = acc_ref[...].astype(o_ref.dtype)
        scratch = [pltpu.VMEM((tm, tn), F32)]

    nbytes = (_nbytes((tm, tk), a.dtype) + _nbytes((tk, tn), b.dtype) + _nbytes((tm, tn), out_dtype)
              + _nbytes((tm, tn), F32))
    res = _call(body, name, (m // tm, n // tn, nk), [a_spec, b_spec],
                [pl.BlockSpec((tm, tn), lambda i, j, kk: (i, j))], [jax.ShapeDtypeStruct((m, n), out_dtype)], [a, b],
                scratch=scratch, sem=("parallel", "parallel", "arbitrary"), nbytes=nbytes, comm=comm, deps=deps)
    return res if comm else res[0]


def _rms(x):
    return lax.rsqrt(jnp.mean(x * x, axis=-1, keepdims=True) + EPS)


def _rms_bwd(dz, a, g):
    r = _rms(a)
    dzg = dz * g
    da = r * dzg - a * (r * r * r) * jnp.mean(dzg * a, axis=-1, keepdims=True)
    return da, dz * (a * r)


def _row_spec(tr, width):
    return pl.BlockSpec((tr, width), lambda i: (i, 0))


def _vec_spec(width):
    return pl.BlockSpec((1, width), lambda i: (0, 0))


def _rms_fwd(x, g, name):
    t, d = x.shape
    tr = min(ROW_BLOCK, t)

    def body(x_ref, g_ref, h_ref):
        xv = x_ref[...]
        h_ref[...] = ((xv * _rms(xv)) * g_ref[...]).astype(BF16)

    return _call(body, name, (t // tr,), [_row_spec(tr, d), _vec_spec(d)], [_row_spec(tr, d)],
                 [jax.ShapeDtypeStruct((t, d), BF16)], [x, g], sem=("parallel",), nbytes=3 * _nbytes((tr, d), F32))[0]


def _post_norm_fwd(x, o, g_post, g_next, name):
    t, d = x.shape
    tr = min(ROW_BLOCK, t)

    def body(x_ref, o_ref, gp_ref, gn_ref, xn_ref, h_ref):
        ov = o_ref[...]
        xn = x_ref[...] + (ov * _rms(ov)) * gp_ref[...]
        xn_ref[...] = xn
        h_ref[...] = ((xn * _rms(xn)) * gn_ref[...]).astype(BF16)

    return _call(body, name, (t // tr,), [_row_spec(tr, d), _row_spec(tr, d), _vec_spec(d), _vec_spec(d)],
                 [_row_spec(tr, d), _row_spec(tr, d)],
                 [jax.ShapeDtypeStruct((t, d), F32), jax.ShapeDtypeStruct((t, d), BF16)], [x, o, g_post, g_next],
                 sem=("parallel",), nbytes=5 * _nbytes((tr, d), F32))


def _post_norm_loss(x, o, g_post, target, name):
    t, d = x.shape
    tr = min(ROW_BLOCK, t)

    def body(x_ref, o_ref, gp_ref, t_ref, dy_ref, loss_ref):
        ov = o_ref[...]
        err = x_ref[...] + (ov * _rms(ov)) * gp_ref[...] - t_ref[...]
        dy_ref[...] = err / d
        part = 0.5 * jnp.sum(jnp.mean(err * err, axis=-1, keepdims=True), axis=0, keepdims=True)

        @pl.when(pl.program_id(0) == 0)
        def _():
            loss_ref[...] = jnp.zeros_like(loss_ref)

        loss_ref[...] += part

    return _call(body, name, (t // tr,), [_row_spec(tr, d), _row_spec(tr, d), _vec_spec(d), _row_spec(tr, d)],
                 [_row_spec(tr, d), pl.BlockSpec((1, 1), lambda i: (0, 0))],
                 [jax.ShapeDtypeStruct((t, d), F32), jax.ShapeDtypeStruct((1, 1), F32)], [x, o, g_post, target],
                 sem=("arbitrary",), nbytes=5 * _nbytes((tr, d), F32))


def _norm_bwd(dskip, name, pre=None, post=None):
    t, d = dskip.shape
    tr = min(ROW_BLOCK, t)
    n_in = 1 + (3 if pre else 0) + (2 if post else 0)

    def body(*refs):
        ins, outs = list(refs[:n_in]), list(refs[n_in:])
        first = pl.program_id(0) == 0
        dx = ins.pop(0)[...]
        if pre:
            dh_ref, xin_ref, gpre_ref = ins.pop(0), ins.pop(0), ins.pop(0)
            dxin, dg_rows = _rms_bwd(dh_ref[...].astype(F32), xin_ref[...], gpre_ref[...])
            dx = dx + dxin
            dx_ref, dgpre_ref = outs.pop(0), outs.pop(0)
            dx_ref[...] = dx

            @pl.when(first)
            def _():
                dgpre_ref[...] = jnp.zeros_like(dgpre_ref)

            dgpre_ref[...] += jnp.sum(dg_rows, axis=0, keepdims=True)
        if post:
            a_ref, gpost_ref = ins.pop(0), ins.pop(0)
            da, dg_rows = _rms_bwd(dx, a_ref[...], gpost_ref[...])
            da_ref, dgpost_ref = outs.pop(0), outs.pop(0)
            da_ref[...] = da.astype(BF16)

            @pl.when(first)
            def _():
                dgpost_ref[...] = jnp.zeros_like(dgpost_ref)

            dgpost_ref[...] += jnp.sum(dg_rows, axis=0, keepdims=True)

    args, in_specs, out_shape, out_specs = [dskip], [_row_spec(tr, d)], [], []
    if pre:
        args += list(pre)
        in_specs += [_row_spec(tr, d), _row_spec(tr, d), _vec_spec(d)]
        out_shape += [jax.ShapeDtypeStruct((t, d), F32), jax.ShapeDtypeStruct((1, d), F32)]
        out_specs += [_row_spec(tr, d), _vec_spec(d)]
    if post:
        args += list(post)
        in_specs += [_row_spec(tr, d), _vec_spec(d)]
        out_shape += [jax.ShapeDtypeStruct((t, d), BF16), jax.ShapeDtypeStruct((1, d), F32)]
        out_specs += [_row_spec(tr, d), _vec_spec(d)]
    return _call(body, name, (t // tr,), in_specs, out_specs, out_shape, args, sem=("arbitrary",),
                 nbytes=7 * _nbytes((tr, d), F32))


def _sigmoid(x):
    return 1.0 / (1.0 + jnp.exp(-x))


def _swiglu_fwd(gu, name):
    t = gu.shape[0]
    tr = min(ROW_BLOCK, t)

    def body(gu_ref, a_ref):
        gate, up = gu_ref[:, :D_FF], gu_ref[:, D_FF:]
        a_ref[...] = ((gate * _sigmoid(gate)) * up).astype(BF16)

    return _call(body, name, (t // tr,), [_row_spec(tr, 2 * D_FF)], [_row_spec(tr, D_FF)],
                 [jax.ShapeDtypeStruct((t, D_FF), BF16)], [gu], sem=("parallel",),
                 nbytes=2 * _nbytes((tr, 2 * D_FF), F32))[0]


def _swiglu_bwd(gu, dact, name):
    t = gu.shape[0]
    tr = min(ROW_BLOCK, t)

    def body(gu_ref, da_ref, o_ref):
        gate, up = gu_ref[:, :D_FF], gu_ref[:, D_FF:]
        da = da_ref[...]
        s = _sigmoid(gate)
        o_ref[:, :D_FF] = (da * up * (s * (1.0 + gate * (1.0 - s)))).astype(BF16)
        o_ref[:, D_FF:] = (da * (gate * s)).astype(BF16)

    return _call(body, name, (t // tr,), [_row_spec(tr, 2 * D_FF), _row_spec(tr, D_FF)], [_row_spec(tr, 2 * D_FF)],
                 [jax.ShapeDtypeStruct((t, 2 * D_FF), BF16)], [gu, dact], sem=("parallel",),
                 nbytes=3 * _nbytes((tr, 2 * D_FF), F32))[0]


def _split3(x):
    hi = x.astype(BF16)
    r1 = x - hi.astype(F32)
    mid = r1.astype(BF16)
    lo = (r1 - mid.astype(F32)).astype(BF16)
    return hi, mid, lo


def _dot_exact(mat01, x):
    hi, mid, lo = _split3(x)
    out = jnp.dot(mat01, hi, preferred_element_type=F32)
    out = out + jnp.dot(mat01, mid, preferred_element_type=F32)
    return out + jnp.dot(mat01, lo, preferred_element_type=F32)


def _dot_exact_rhs(x, mat01):
    hi, mid, lo = _split3(x)
    out = jnp.dot(hi, mat01, preferred_element_type=F32)
    out = out + jnp.dot(mid, mat01, preferred_element_type=F32)
    return out + jnp.dot(lo, mat01, preferred_element_type=F32)


def _tri(lower):
    r = lax.broadcasted_iota(jnp.int32, (CHUNK, CHUNK), 0)
    c = lax.broadcasted_iota(jnp.int32, (CHUNK, CHUNK), 1)
    return jnp.where(r >= c if lower else r <= c, 1.0, 0.0).astype(BF16)


def _log_sigmoid(z):
    return jnp.minimum(z, 0.0) - jnp.log(1.0 + jnp.exp(-jnp.abs(z)))


def _forget_cumsum(proj, b_pad, name):
    t = proj.shape[0]
    nb = t // CHUNK

    def body(f_ref, b_ref, c_ref):
        tri = _tri(True)
        b = b_ref[...]

        def blk(i, carry):
            rows = pl.ds(pl.multiple_of(i * CHUNK, CHUNK), CHUNK)
            cs = _dot_exact(tri, _log_sigmoid(f_ref[rows, :] + b)) + carry
            c_ref[rows, :] = cs
            return cs[CHUNK - 1:CHUNK, :]

        lax.fori_loop(0, nb, blk, jnp.zeros((1, HEAD_DIM), F32))

    return _call(body, name, (1,),
                 [pl.BlockSpec((t, HEAD_DIM), lambda i: (0, CB_F)), pl.BlockSpec((1, HEAD_DIM), lambda i: (0, 0))],
                 [pl.BlockSpec((t, HEAD_DIM), lambda i: (0, 0))], [jax.ShapeDtypeStruct((t, HEAD_DIM), F32)],
                 [proj, b_pad], sem=("arbitrary",), nbytes=2 * _nbytes((t, HEAD_DIM), F32))[0]


def _forget_bwd(proj, b_pad, dc_key, dc_query, dproj, name):
    t = proj.shape[0]
    tr = min(ROW_BLOCK, t)
    nb = t // tr
    rev = lambda i: nb - 1 - i

    def body(f_ref, b_ref, dck_ref, dcq_ref, dproj_in, df_ref, db_ref, run_ref):
        del dproj_in

        @pl.when(pl.program_id(0) == 0)
        def _():
            run_ref[...] = jnp.zeros_like(run_ref)
            db_ref[...] = jnp.zeros_like(db_ref)

        tri = _tri(False)
        b = b_ref[...]
        lane = lax.broadcasted_iota(jnp.int32, (CHUNK, HEAD_DIM), 1)
        df_ref[...] = jnp.zeros_like(df_ref)
        for n in reversed(range(tr // CHUNK)):
            rows = slice(n * CHUNK, (n + 1) * CHUNK)
            dc = dck_ref[rows, :]
            for h in range(N_HEADS):
                dc = dc + jnp.where(lane == h, dcq_ref[rows, h * HEAD_DIM:(h + 1) * HEAD_DIM], 0.0)
            dlogf = _dot_exact(tri, dc) + run_ref[...]
            run_ref[...] = dlogf[0:1, :]
            z = f_ref[rows, :] + b
            e = jnp.exp(-jnp.abs(z))
            sig_neg = jnp.where(z >= 0.0, e, 1.0) / (1.0 + e)
            df = jnp.where(lane < N_HEADS, dlogf * sig_neg, 0.0)
            df_ref[rows, :HEAD_DIM] = df.astype(BF16)
            db_ref[...] += jnp.sum(df, axis=0, keepdims=True)

    return _call(body, name, (nb,),
                 [pl.BlockSpec((tr, HEAD_DIM), lambda i: (rev(i), CB_F)), pl.BlockSpec((1, HEAD_DIM), lambda i: (0, 0)),
                  pl.BlockSpec((tr, HEAD_DIM), lambda i: (rev(i), 0)), pl.BlockSpec((tr, D_MODEL), lambda i: (rev(i), 0)),
                  pl.BlockSpec(memory_space=pl.ANY)],
                 [pl.BlockSpec((tr, F_WIDTH), lambda i: (rev(i), GATED // F_WIDTH)),
                  pl.BlockSpec((1, HEAD_DIM), lambda i: (0, 0))],
                 [jax.ShapeDtypeStruct(dproj.shape, dproj.dtype), jax.ShapeDtypeStruct((1, HEAD_DIM), F32)],
                 [proj, b_pad, dc_key, dc_query, dproj], scratch=[pltpu.VMEM((1, HEAD_DIM), F32)],
                 sem=("arbitrary",), nbytes=4 * _nbytes((tr, D_MODEL), F32), aliases={4: 0})


def _causal(block):
    r = lax.broadcasted_iota(jnp.int32, (block, block), 0)
    c = lax.broadcasted_iota(jnp.int32, (block, block), 1)
    return c <= r


def _lanes(x, width):
    return jnp.concatenate([x] * (width // HEAD_DIM), axis=1)


def _attn_fwd(proj, c_pad, c_row, name, comm=()):
    t = proj.shape[0]
    blk = min(ATTN_BLOCK, t)
    nq = t // blk
    scale = HEAD_DIM ** -0.5

    def body(q_ref, k_ref, v_ref, cp_ref, cr_ref, y_ref, e_ref):
        h, i = pl.program_id(0), pl.program_id(1)
        q = (q_ref[...] * scale).astype(BF16)
        lane = lax.broadcasted_iota(jnp.int32, (blk, HEAD_DIM), 1)
        ci = jnp.sum(jnp.where(lane == h, cp_ref[...], 0.0), axis=1, keepdims=True)

        def step(j, carry, diagonal):
            m, l, acc = carry
            off = pl.multiple_of(j * blk, blk)
            k = k_ref[pl.ds(off, blk), :].astype(BF16)
            v = v_ref[pl.ds(off, blk), :].astype(BF16)
            s = lax.dot_general(q, k, _DIMS["nt"], preferred_element_type=F32)
            s = s + ci - cr_ref[:, pl.ds(off, blk)]
            if diagonal:
                s = jnp.where(_causal(blk), s, NEG)
            m_new = jnp.maximum(m, jnp.max(s, axis=1, keepdims=True))
            alpha = jnp.exp(m - m_new)
            p = jnp.exp(s - m_new)
            l = alpha * l + jnp.sum(p, axis=1, keepdims=True)
            acc = alpha * acc + jnp.dot(p.astype(BF16), v, preferred_element_type=F32)
            return m_new, l, acc

        init = (jnp.full((blk, 1), NEG, F32), jnp.zeros((blk, 1), F32), jnp.zeros((blk, HEAD_DIM), F32))
        carry = lax.fori_loop(0, i, lambda j, c: step(j, c, False), init)
        m, l, acc = step(i, carry, True)
        y_ref[...] = acc / l
        e_ref[...] = jnp.broadcast_to(ci - (m + jnp.log(l)), (blk, HEAD_DIM))

    nbytes = 2 * _nbytes((t, HEAD_DIM), F32) + 4 * _nbytes((blk, HEAD_DIM), F32) + 4 * _nbytes((blk, blk), F32)
    head_block = pl.BlockSpec((blk, HEAD_DIM), lambda h, i: (i, h))
    return _call(body, name, (N_HEADS, nq),
                 [pl.BlockSpec((blk, HEAD_DIM), lambda h, i: (i, CB_QKV + 3 * h)),
                  pl.BlockSpec((t, HEAD_DIM), lambda h, i: (0, CB_QKV + 3 * h + 1)),
                  pl.BlockSpec((t, HEAD_DIM), lambda h, i: (0, CB_QKV + 3 * h + 2)),
                  pl.BlockSpec((blk, HEAD_DIM), lambda h, i: (i, 0)),
                  pl.BlockSpec((None, 1, t), lambda h, i: (h, 0, 0))],
                 [head_block, head_block],
                 [jax.ShapeDtypeStruct((t, D_MODEL), F32), jax.ShapeDtypeStruct((t, D_MODEL), F32)],
                 [proj, proj, proj, c_pad, c_row], sem=("parallel", "arbitrary"), nbytes=nbytes, comm=comm)


def _attn_bwd(proj, dy, e_b, delta_b, c_row, dproj, name, comm=()):
    t = proj.shape[0]
    blk = min(ATTN_BLOCK, t)
    nb = t // blk
    scale = HEAD_DIM ** -0.5

    def body(q_ref, k_ref, v_ref, dy_ref, e_ref, dl_ref, cr_ref, dproj_in, dqkv_ref, dc_ref, dr_ref, dq_acc):
        del dproj_in
        j = pl.program_id(1)

        @pl.when(j == 0)
        def _():
            dq_acc[...] = jnp.zeros_like(dq_acc)
            dr_ref[...] = jnp.zeros_like(dr_ref)

        k = k_ref[...].astype(BF16)
        v = v_ref[...].astype(BF16)
        crow = cr_ref[...]

        def step(i, carry, diagonal):
            dk, dv, dcs = carry
            rows = pl.ds(pl.multiple_of(i * blk, blk), blk)
            q = (q_ref[rows, :] * scale).astype(BF16)
            dyi = dy_ref[rows, :].astype(BF16)
            s = lax.dot_general(q, k, _DIMS["nt"], preferred_element_type=F32) + _lanes(e_ref[rows, :], blk) - crow
            if diagonal:
                s = jnp.where(_causal(blk), s, NEG)
            p = jnp.exp(s)
            dv = dv + lax.dot_general(p.astype(BF16), dyi, _DIMS["tn"], preferred_element_type=F32)
            dp = lax.dot_general(dyi, v, _DIMS["nt"], preferred_element_type=F32)
            ds = p * (dp - _lanes(dl_ref[rows, :], blk))
            dsb = ds.astype(BF16)
            dk = dk + lax.dot_general(dsb, q, _DIMS["tn"], preferred_element_type=F32)
            dq_acc[rows, :] += jnp.dot(dsb, k, preferred_element_type=F32) * scale
            dr_ref[rows, :] += jnp.sum(ds, axis=1, keepdims=True)
            return dk, dv, dcs + jnp.sum(ds, axis=0, keepdims=True)

        zeros = jnp.zeros((blk, HEAD_DIM), F32)
        carry = step(j, (zeros, zeros, jnp.zeros((1, blk), F32)), True)
        dk, dv, dcs = lax.fori_loop(j + 1, nb, lambda i, c: step(i, c, False), carry)
        mine = pl.ds(pl.multiple_of(j * blk, blk), blk)
        dqkv_ref[:, :HEAD_DIM] = dq_acc[mine, :].astype(BF16)
        dqkv_ref[:, HEAD_DIM:2 * HEAD_DIM] = dk.astype(BF16)
        dqkv_ref[:, 2 * HEAD_DIM:] = dv.astype(BF16)
        dc_ref[...] = -dcs

    full = lambda cb: pl.BlockSpec((t, HEAD_DIM), lambda h, j: (0, cb(h)))
    head = lambda h: h
    nbytes = (6 * _nbytes((t, HEAD_DIM), F32) + 4 * _nbytes((blk, HEAD_DIM), F32) + 5 * _nbytes((blk, blk), F32))
    return _call(body, name, (N_HEADS, nb),
                 [full(lambda h: CB_QKV + 3 * h),
                  pl.BlockSpec((blk, HEAD_DIM), lambda h, j: (j, CB_QKV + 3 * h + 1)),
                  pl.BlockSpec((blk, HEAD_DIM), lambda h, j: (j, CB_QKV + 3 * h + 2)),
                  full(head), full(head), full(head),
                  pl.BlockSpec((None, 1, blk), lambda h, j: (h, 0, j)),
                  pl.BlockSpec(memory_space=pl.ANY)],
                 [pl.BlockSpec((blk, QKV_WIDTH), lambda h, j: (j, (GATED + F_WIDTH) // QKV_WIDTH + h)),
                  pl.BlockSpec((None, 1, blk), lambda h, j: (h, 0, j)),
                  full(head)],
                 [jax.ShapeDtypeStruct(dproj.shape, dproj.dtype), jax.ShapeDtypeStruct((N_HEADS, 1, t), F32),
                  jax.ShapeDtypeStruct((t, D_MODEL), F32)],
                 [proj, proj, proj, dy, e_b, delta_b, c_row, dproj], scratch=[pltpu.VMEM((t, HEAD_DIM), F32)],
                 sem=("parallel", "arbitrary"), nbytes=nbytes, aliases={7: 0}, comm=comm)


def _chunks(t, blk):
    return [slice(r * blk, (r + 1) * blk) for r in range(t // blk)]


def _lane_of_head(x, h):
    lane = lax.broadcasted_iota(jnp.int32, x.shape, 1)
    return jnp.broadcast_to(jnp.sum(jnp.where(lane == h, x, 0.0), axis=1, keepdims=True), x.shape)


def _causal_t(block):
    r = lax.broadcasted_iota(jnp.int32, (block, block), 0)
    c = lax.broadcasted_iota(jnp.int32, (block, block), 1)
    return r <= c


def _attn_fwd_t(proj, c_pad, c_row, name, comm=()):
    t = proj.shape[0]
    blk = min(ATTN_BLOCK, t)
    nq = t // blk
    scale = HEAD_DIM ** -0.5

    def body(q_ref, k_ref, v_ref, cp_ref, cr_ref, y_ref, e_ref, kb_s, vt_s, cb_s):
        h, i = pl.program_id(0), pl.program_id(1)

        @pl.when(i == 0)
        def _():
            for rows in _chunks(t, blk):
                kb_s[rows, :] = k_ref[rows, :].astype(BF16)
                vt_s[:, rows] = v_ref[rows, :].T.astype(BF16)
                cb_s[rows, :] = _lane_of_head(cp_ref[rows, :], h)

        q = (q_ref[...] * scale).astype(BF16)
        ci = cr_ref[:, pl.ds(pl.multiple_of(i * blk, blk), blk)]

        def step(j, carry, diagonal):
            m, l, acc = carry
            rows = pl.ds(pl.multiple_of(j * blk, blk), blk)
            s = lax.dot_general(kb_s[rows, :], q, _DIMS["nt"], preferred_element_type=F32)
            s = s + ci - _lanes(cb_s[rows, :], blk)
            if diagonal:
                s = jnp.where(_causal_t(blk), s, NEG)
            m_new = jnp.maximum(m, jnp.max(s, axis=0, keepdims=True))
            alpha = jnp.exp(m - m_new)
            p = jnp.exp(s - m_new)
            l = alpha * l + jnp.sum(p, axis=0, keepdims=True)
            acc = alpha * acc + jnp.dot(vt_s[:, rows], p.astype(BF16), preferred_element_type=F32)
            return m_new, l, acc

        init = (jnp.full((1, blk), NEG, F32), jnp.zeros((1, blk), F32), jnp.zeros((HEAD_DIM, blk), F32))
        carry = lax.fori_loop(0, i, lambda j, c: step(j, c, False), init)
        m, l, acc = step(i, carry, True)
        y_ref[...] = (acc / l).T
        e_ref[...] = ci - (m + jnp.log(l))

    nbytes = 3 * _nbytes((t, HEAD_DIM), F32) + 4 * _nbytes((blk, HEAD_DIM), F32) + 4 * _nbytes((blk, blk), F32)
    return _call(body, name, (N_HEADS, nq),
                 [pl.BlockSpec((blk, HEAD_DIM), lambda h, i: (i, CB_QKV + 3 * h)),
                  pl.BlockSpec((t, HEAD_DIM), lambda h, i: (0, CB_QKV + 3 * h + 1)),
                  pl.BlockSpec((t, HEAD_DIM), lambda h, i: (0, CB_QKV + 3 * h + 2)),
                  pl.BlockSpec((t, HEAD_DIM), lambda h, i: (0, 0)),
                  pl.BlockSpec((None, 1, t), lambda h, i: (h, 0, 0))],
                 [pl.BlockSpec((blk, HEAD_DIM), lambda h, i: (i, h)),
                  pl.BlockSpec((None, 1, blk), lambda h, i: (h, 0, i))],
                 [jax.ShapeDtypeStruct((t, D_MODEL), F32), jax.ShapeDtypeStruct((N_HEADS, 1, t), F32)],
                 [proj, proj, proj, c_pad, c_row],
                 scratch=[pltpu.VMEM((t, HEAD_DIM), BF16), pltpu.VMEM((HEAD_DIM, t), BF16),
                          pltpu.VMEM((t, HEAD_DIM), F32)],
                 sem=("parallel", "arbitrary"), nbytes=nbytes, comm=comm)


def _attn_bwd_t(proj, dy, e_row, delta_b, c_pad, dproj, name, comm=()):
    t = proj.shape[0]
    blk = min(ATTN_BLOCK, t)
    nb = t // blk
    scale = HEAD_DIM ** -0.5

    def body(q_ref, k_ref, v_ref, dy_ref, e_ref, dl_ref, cp_ref, dproj_in, dqkv_ref, dck_ref, dcq_ref,
             qb_s, qt_s, dyb_s, dyt_s, dl_s, dqt_acc):
        del dproj_in
        h, j = pl.program_id(0), pl.program_id(1)

        @pl.when(j == 0)
        def _():
            for rows in _chunks(t, blk):
                qs = q_ref[rows, :] * scale
                qb_s[rows, :] = qs.astype(BF16)
                qt_s[:, rows] = qs.T.astype(BF16)
                dyr = dy_ref[rows, :]
                dyb_s[rows, :] = dyr.astype(BF16)
                dyt_s[:, rows] = dyr.T.astype(BF16)
                dl_s[:, rows] = dl_ref[rows, :].T[0:1, :]
            dqt_acc[...] = jnp.zeros_like(dqt_acc)
            dcq_ref[...] = jnp.zeros_like(dcq_ref)

        kf = k_ref[...]
        kb, kt = kf.astype(BF16), kf.T.astype(BF16)
        vb = v_ref[...].astype(BF16)
        cj = _lanes(_lane_of_head(cp_ref[...], h), blk)

        def step(i, carry, diagonal):
            dkt, dvt, dkey = carry
            cols = pl.ds(pl.multiple_of(i * blk, blk), blk)
            qi, dyi = qb_s[cols, :], dyb_s[cols, :]
            s = lax.dot_general(kb, qi, _DIMS["nt"], preferred_element_type=F32) + e_ref[:, cols] - cj
            if diagonal:
                s = jnp.where(_causal_t(blk), s, NEG)
            p = jnp.exp(s)
            dp = lax.dot_general(vb, dyi, _DIMS["nt"], preferred_element_type=F32)
            ds = p * (dp - dl_s[:, cols])
            pb, dsb = p.astype(BF16), ds.astype(BF16)
            dvt = dvt + lax.dot_general(dyt_s[:, cols], pb, _DIMS["nt"], preferred_element_type=F32)
            dkt = dkt + lax.dot_general(qt_s[:, cols], dsb, _DIMS["nt"], preferred_element_type=F32)
            dqt_acc[:, cols] += jnp.dot(kt, dsb, preferred_element_type=F32) * scale
            dcq_ref[:, cols] += jnp.sum(ds, axis=0, keepdims=True)
            part = ds[:, :HEAD_DIM]
            for g in range(1, blk // HEAD_DIM):
                part = part + ds[:, g * HEAD_DIM:(g + 1) * HEAD_DIM]
            return dkt, dvt, dkey + part

        zeros = jnp.zeros((HEAD_DIM, blk), F32)
        carry = step(j, (zeros, zeros, jnp.zeros((blk, HEAD_DIM), F32)), True)
        dkt, dvt, dkey = lax.fori_loop(j + 1, nb, lambda i, c: step(i, c, False), carry)
        mine = pl.ds(pl.multiple_of(j * blk, blk), blk)
        dqkv_ref[:, :HEAD_DIM] = dqt_acc[:, mine].T.astype(BF16)
        dqkv_ref[:, HEAD_DIM:2 * HEAD_DIM] = dkt.T.astype(BF16)
        dqkv_ref[:, 2 * HEAD_DIM:] = dvt.T.astype(BF16)
        dck_ref[...] = -jnp.sum(dkey.T, axis=0, keepdims=True)

    full = lambda cb: pl.BlockSpec((t, HEAD_DIM), lambda h, j: (0, cb(h)))
    head = lambda h: h
    row = pl.BlockSpec((None, 1, t), lambda h, j: (h, 0, 0))
    nbytes = (8 * _nbytes((t, HEAD_DIM), F32) + 4 * _nbytes((blk, HEAD_DIM), F32) + 6 * _nbytes((blk, blk), F32))
    return _call(body, name, (N_HEADS, nb),
                 [full(lambda h: CB_QKV + 3 * h),
                  pl.BlockSpec((blk, HEAD_DIM), lambda h, j: (j, CB_QKV + 3 * h + 1)),
                  pl.BlockSpec((blk, HEAD_DIM), lambda h, j: (j, CB_QKV + 3 * h + 2)),
                  full(head), row, full(head),
                  pl.BlockSpec((blk, HEAD_DIM), lambda h, j: (j, 0)),
                  pl.BlockSpec(memory_space=pl.ANY)],
                 [pl.BlockSpec((blk, QKV_WIDTH), lambda h, j: (j, (GATED + F_WIDTH) // QKV_WIDTH + h)),
                  pl.BlockSpec((None, 1, blk), lambda h, j: (h, 0, j)),
                  row],
                 [jax.ShapeDtypeStruct(dproj.shape, dproj.dtype), jax.ShapeDtypeStruct((N_HEADS, 1, t), F32),
                  jax.ShapeDtypeStruct((N_HEADS, 1, t), F32)],
                 [proj, proj, proj, dy, e_row, delta_b, c_pad, dproj],
                 scratch=[pltpu.VMEM((t, HEAD_DIM), BF16), pltpu.VMEM((HEAD_DIM, t), BF16),
                          pltpu.VMEM((t, HEAD_DIM), BF16), pltpu.VMEM((HEAD_DIM, t), BF16),
                          pltpu.VMEM((1, t), F32), pltpu.VMEM((HEAD_DIM, t), F32)],
                 sem=("parallel", "arbitrary"), nbytes=nbytes, aliases={7: 0}, comm=comm)


def _forget_bwd_t(proj, b_pad, dc, dproj, name):
    t = proj.shape[0]
    tr = min(ROW_BLOCK, t)
    nb = t // tr
    rev = lambda i: nb - 1 - i

    def body(f_ref, b_ref, dc_ref, dproj_in, df_ref, db_ref, run_ref):
        del dproj_in

        @pl.when(pl.program_id(0) == 0)
        def _():
            run_ref[...] = jnp.zeros_like(run_ref)
            db_ref[...] = jnp.zeros_like(db_ref)

        tri = _tri(False)
        b = b_ref[...]
        lane = lax.broadcasted_iota(jnp.int32, (CHUNK, HEAD_DIM), 1)
        df_ref[...] = jnp.zeros_like(df_ref)
        for n in reversed(range(tr // CHUNK)):
            rows = slice(n * CHUNK, (n + 1) * CHUNK)
            dlogf = _dot_exact(tri, dc_ref[rows, :]) + run_ref[...]
            run_ref[...] = dlogf[0:1, :]
            z = f_ref[rows, :] + b
            e = jnp.exp(-jnp.abs(z))
            sig_neg = jnp.where(z >= 0.0, e, 1.0) / (1.0 + e)
            df = jnp.where(lane < N_HEADS, dlogf * sig_neg, 0.0)
            df_ref[rows, :HEAD_DIM] = df.astype(BF16)
            db_ref[...] += jnp.sum(df, axis=0, keepdims=True)

    return _call(body, name, (nb,),
                 [pl.BlockSpec((tr, HEAD_DIM), lambda i: (rev(i), CB_F)), pl.BlockSpec((1, HEAD_DIM), lambda i: (0, 0)),
                  pl.BlockSpec((tr, HEAD_DIM), lambda i: (rev(i), 0)), pl.BlockSpec(memory_space=pl.ANY)],
                 [pl.BlockSpec((tr, F_WIDTH), lambda i: (rev(i), GATED // F_WIDTH)),
                  pl.BlockSpec((1, HEAD_DIM), lambda i: (0, 0))],
                 [jax.ShapeDtypeStruct(dproj.shape, dproj.dtype), jax.ShapeDtypeStruct((1, HEAD_DIM), F32)],
                 [proj, b_pad, dc, dproj], scratch=[pltpu.VMEM((1, HEAD_DIM), F32)],
                 sem=("arbitrary",), nbytes=4 * _nbytes((tr, HEAD_DIM), F32), aliases={3: 0})


_GELU_K = math.sqrt(2.0 / math.pi)
_GELU_C = 0.044715


def _gelu(x):
    t = jnp.tanh(_GELU_K * (x + _GELU_C * (x * x * x)))
    return 0.5 * x * (1.0 + t), t


def _gelu_grad(x, t):
    return 0.5 * (1.0 + t) + 0.5 * x * (1.0 - t * t) * (_GELU_K * (1.0 + 3.0 * _GELU_C * (x * x)))


def _layernorm_stats(a):
    mu = jnp.mean(a, axis=-1, keepdims=True)
    xc = a - mu
    r = lax.rsqrt(jnp.mean(xc * xc, axis=-1, keepdims=True) + EPS)
    return xc * r, r


def _group(g):
    return slice(g * CHUNK, (g + 1) * CHUNK)


def _gated_cols(k):
    return pl.BlockSpec((CHUNK, D_MODEL), lambda i: (i, k))


def _fixed(shape):
    return pl.BlockSpec(shape, lambda i: (0,) * len(shape))


def _mix_fwd(proj, y_b, w_s, b_cols, g_v, name):
    t = proj.shape[0]

    def body(u_ref, vs_ref, ga_ref, gb_ref, yb_ref, w_ref, b_ref, gv_ref, o_ref):
        a_u, _ = _gelu(u_ref[...])
        a_v, _ = _gelu(vs_ref[...])
        xhat, _ = _layernorm_stats(a_v)
        vn = (xhat * gv_ref[...]).astype(BF16)
        sa, sb = _sigmoid(ga_ref[...]), _sigmoid(gb_ref[...])
        yb = yb_ref[...]
        mask = _causal(CHUNK)
        for g in range(N_GROUPS):
            cols = _group(g)
            w = jnp.where(mask, w_ref[g], 0.0).astype(BF16)
            mixed = jnp.dot(w, vn[:, cols], preferred_element_type=F32) + b_ref[:, g:g + 1]
            o_ref[:, cols] = (sa[:, cols] * (a_u[:, cols] * mixed) + sb[:, cols] * yb[:, cols]).astype(BF16)

    return _call(body, name, (t // CHUNK,),
                 [_gated_cols(0), _gated_cols(1), _gated_cols(2), _gated_cols(3), _row_spec(CHUNK, D_MODEL),
                  _fixed((N_GROUPS, CHUNK, CHUNK)), _fixed((CHUNK, N_GROUPS)), _fixed((1, D_MODEL))],
                 [_row_spec(CHUNK, D_MODEL)], [jax.ShapeDtypeStruct((t, D_MODEL), BF16)],
                 [proj, proj, proj, proj, y_b, w_s, b_cols, g_v], sem=("parallel",),
                 nbytes=16 * _nbytes((CHUNK, D_MODEL), F32))[0]


def _mix_bwd(proj, y_b, dmerged, w_s, b_cols, g_v, sel, name, comm=()):
    t = proj.shape[0]

    def body(u_ref, vs_ref, ga_ref, gb_ref, yb_ref, dm_ref, w_ref, b_ref, gv_ref, sel_ref,
             dg_ref, dyb_ref, delta_ref, dw_ref, dbt_ref, dgv_ref):
        @pl.when(pl.program_id(0) == 0)
        def _():
            dw_ref[...] = jnp.zeros_like(dw_ref)
            dbt_ref[...] = jnp.zeros_like(dbt_ref)
            dgv_ref[...] = jnp.zeros_like(dgv_ref)

        u, vs = u_ref[...], vs_ref[...]
        a_u, t_u = _gelu(u)
        a_v, t_v = _gelu(vs)
        xhat, r = _layernorm_stats(a_v)
        gv = gv_ref[...]
        vn = (xhat * gv).astype(BF16)
        sa, sb = _sigmoid(ga_ref[...]), _sigmoid(gb_ref[...])
        yb, dm = yb_ref[...], dm_ref[...].astype(F32)
        dyb = dm * sb
        dyb_ref[...] = dyb
        dyb_yb = dyb * yb
        dg_ref[:, 3 * D_MODEL:] = (dm * yb * (sb * (1.0 - sb))).astype(BF16)
        dya = dm * sa
        mask = _causal(CHUNK)
        dmixed_parts, dvn_parts = [], []
        for g in range(N_GROUPS):
            cols = _group(g)
            delta_ref[:, cols] = jnp.broadcast_to(jnp.sum(dyb_yb[:, cols], axis=1, keepdims=True), (CHUNK, CHUNK))
            w = jnp.where(mask, w_ref[g], 0.0).astype(BF16)
            mixed = jnp.dot(w, vn[:, cols], preferred_element_type=F32) + b_ref[:, g:g + 1]
            y_a = a_u[:, cols] * mixed
            dg_ref[:, 2 * D_MODEL + g * CHUNK:2 * D_MODEL + (g + 1) * CHUNK] = (
                dm[:, cols] * y_a * (sa[:, cols] * (1.0 - sa[:, cols]))).astype(BF16)
            dg_ref[:, cols] = (dya[:, cols] * mixed * _gelu_grad(u[:, cols], t_u[:, cols])).astype(BF16)
            dmixed = dya[:, cols] * a_u[:, cols]
            dmb = dmixed.astype(BF16)
            dw = lax.dot_general(dmb, vn[:, cols], _DIMS["nt"], preferred_element_type=F32)
            dw_ref[g] += jnp.where(mask, dw, 0.0)
            dvn_parts.append(lax.dot_general(w, dmb, _DIMS["tn"], preferred_element_type=F32))
            dmixed_parts.append(dmixed)
        dmixed_all = jnp.concatenate(dmixed_parts, axis=1)
        dvn = jnp.concatenate(dvn_parts, axis=1)
        dbt_ref[...] += _dot_exact_rhs(dmixed_all, sel_ref[...])
        dgv_ref[...] += jnp.sum(dvn * xhat, axis=0, keepdims=True)
        dxh = dvn * gv
        da_v = r * (dxh - jnp.mean(dxh, axis=-1, keepdims=True)
                    - xhat * jnp.mean(dxh * xhat, axis=-1, keepdims=True))
        dg_ref[:, D_MODEL:2 * D_MODEL] = (da_v * _gelu_grad(vs, t_v)).astype(BF16)

    row = lambda width: _row_spec(CHUNK, width)
    return _call(body, name, (t // CHUNK,),
                 [_gated_cols(0), _gated_cols(1), _gated_cols(2), _gated_cols(3), row(D_MODEL), row(D_MODEL),
                  _fixed((N_GROUPS, CHUNK, CHUNK)), _fixed((CHUNK, N_GROUPS)), _fixed((1, D_MODEL)),
                  _fixed((D_MODEL, HEAD_DIM))],
                 [row(GATED), row(D_MODEL), row(D_MODEL),
                  _fixed((N_GROUPS, CHUNK, CHUNK)), _fixed((CHUNK, HEAD_DIM)), _fixed((1, D_MODEL))],
                 [jax.ShapeDtypeStruct((t, IN_PAD), BF16), jax.ShapeDtypeStruct((t, D_MODEL), F32),
                  jax.ShapeDtypeStruct((t, D_MODEL), F32), jax.ShapeDtypeStruct((N_GROUPS, CHUNK, CHUNK), F32),
                  jax.ShapeDtypeStruct((CHUNK, HEAD_DIM), F32), jax.ShapeDtypeStruct((1, D_MODEL), F32)],
                 [proj, proj, proj, proj, y_b, dmerged, w_s, b_cols, g_v, sel], sem=("arbitrary",),
                 nbytes=40 * _nbytes((CHUNK, D_MODEL), F32), comm=comm)


def _adamw(w, g, m, v, name):
    r, c = w.shape
    tr = _tile(r, 256, unit=8)

    def body(w_ref, g_ref, m_ref, v_ref, d_ref, nm_ref, nv_ref):
        gv = g_ref[...]
        nm = ADAM_B1 * m_ref[...] + (1.0 - ADAM_B1) * gv
        nv = ADAM_B2 * v_ref[...] + (1.0 - ADAM_B2) * (gv * gv)
        m_hat = nm / (1.0 - ADAM_B1 ** ADAM_STEP)
        v_hat = nv / (1.0 - ADAM_B2 ** ADAM_STEP)
        d_ref[...] = -ADAM_LR * (m_hat / (jnp.sqrt(v_hat) + ADAM_EPS) + ADAM_WD * w_ref[...])
        nm_ref[...] = nm
        nv_ref[...] = nv

    spec = pl.BlockSpec((tr, c), lambda i: (i, 0))
    shape = jax.ShapeDtypeStruct((r, c), F32)
    return _call(body, name, (r // tr,), [spec] * 4, [spec] * 3, [shape] * 3, [w, g, m, v], sem=("parallel",),
                 nbytes=7 * _nbytes((tr, max(c, 128)), F32))


def _pad_rows(a, rows):
    return jnp.pad(a, ((0, rows - a.shape[0]), (0, 0)))


def _pack_rest(w_out, w_gate, w_up, w_down):
    return jnp.concatenate([w_out, w_gate.T, w_up.T, w_down], axis=0).astype(BF16)


def _unpack_in(gathered):
    i = jnp.arange(16)[None, :, None]
    b = jnp.arange(N_DEV)[:, None, None]
    head = jnp.where(i < b, jnp.roll(gathered[:, IN_STRIDE:IN_STRIDE + 16], 1, axis=0), gathered[:, :16])
    nat = jnp.concatenate([head, gathered[:, 16:IN_STRIDE]], axis=1).reshape(N_DEV * IN_STRIDE, D_MODEL)
    qkv = nat[2 * D_MODEL:5 * D_MODEL].reshape(3, N_HEADS, HEAD_DIM, D_MODEL)
    qkv = jnp.transpose(qkv, (1, 0, 2, 3)).reshape(3 * D_MODEL, D_MODEL)
    f = _pad_rows(gathered[N_DEV - 1, IN_STRIDE:IN_STRIDE + N_HEADS], F_WIDTH)
    return jnp.concatenate([nat[:2 * D_MODEL], nat[5 * D_MODEL:7 * D_MODEL], f, qkv], axis=0)


def _unpack_rest(gathered):
    take = lambda lo, n: gathered[:, lo:lo + n, :].reshape(N_DEV * n, D_MODEL)
    return take(R_OUT, SH_OUT), jnp.concatenate([take(R_G, SH_FF), take(R_U, SH_FF)], axis=0), take(R_D, SH_FF)


def _pack_grad_in(dw_in_t):
    qkv = dw_in_t[GATED + F_WIDTH:].reshape(N_HEADS, 3, HEAD_DIM, D_MODEL)
    qkv = jnp.transpose(qkv, (1, 0, 2, 3)).reshape(3 * D_MODEL, D_MODEL)
    return jnp.concatenate([dw_in_t[:2 * D_MODEL], qkv, dw_in_t[2 * D_MODEL:GATED],
                            dw_in_t[GATED:GATED + IN_ROWS - N_DEV * IN_STRIDE]], axis=0)


def _pack_grad_rest(dw_out, dw_gu_t, dw_down):
    split = lambda a, n: a.reshape(N_DEV, n, D_MODEL)
    return jnp.concatenate([split(dw_out, SH_OUT), split(dw_gu_t[:D_FF], SH_FF), split(dw_gu_t[D_FF:], SH_FF),
                            split(dw_down, SH_FF)], axis=1)


def kernel(x, mix_pre_g, w_in, b_forget, sgu_norm_g, w_spatial, b_spatial, w_out, mix_post_g, ffn_pre_g, w_gate, w_up, w_down, ffn_post_g, loss_target, m_mix_pre_g, m_w_in, m_b_forget, m_sgu_norm_g, m_w_spatial, m_b_spatial, m_w_out, m_mix_post_g, m_ffn_pre_g, m_w_gate, m_w_up, m_w_down, m_ffn_post_g, v_mix_pre_g, v_w_in, v_b_forget, v_sgu_norm_g, v_w_spatial, v_b_spatial, v_w_out, v_mix_post_g, v_ffn_pre_g, v_w_gate, v_w_up, v_w_down, v_ffn_post_g):
    depth = w_in.shape[0]
    assert depth == 2
    t = x.shape[1]
    x0 = x.reshape(t, D_MODEL)
    target = loss_target.reshape(t, D_MODEL)
    sel = (jnp.arange(D_MODEL)[:, None] // HEAD_DIM == jnp.arange(HEAD_DIM)[None, :]).astype(BF16)
    vec = lambda a, l: a[l][None, :]
    zero_row = jnp.zeros((1, D_MODEL), F32)

    me = 4 * lax.axis_index("x") + 2 * lax.axis_index("y") + lax.axis_index("c")
    p_in = [lax.dynamic_update_slice(jnp.zeros((P_IN, D_MODEL), BF16), w_in[l].T.astype(BF16), (me, 0))
            for l in range(depth)]
    p_rest = [_pack_rest(w_out[l], w_gate[l], w_up[l], w_down[l]) for l in range(depth)]
    w_in_t = [None] * depth
    w_rest = [None] * depth
    w_in_t[0] = _unpack_in(_exchange("gather", p_in[0], "gather_w_in_0")[0])

    saved = []
    xl = x0
    h = _rms_fwd(xl, vec(mix_pre_g, 0), "rms_in_0")
    dy = loss = None
    for l in range(depth):
        b_pad = jnp.pad(b_forget[l], (0, HEAD_DIM - N_HEADS))[None, :]
        b_cols = b_spatial[l].T
        proj = _matmul(h, w_in_t[l], "nt", F32, f"proj_{l}")
        c_pad = _forget_cumsum(proj, b_pad, f"forget_cumsum_{l}")
        c_row = c_pad[:, :N_HEADS].T[:, None, :]
        riders = [("gather", p_rest[l])] + ([("gather", p_in[l + 1])] if l + 1 < depth else [])
        y_b, e_row, *arrived = _attn_fwd_t(proj, c_pad, c_row, f"attn_fwd_{l}", comm=riders)
        w_rest[l] = _unpack_rest(arrived[0])
        if l + 1 < depth:
            w_in_t[l + 1] = _unpack_in(arrived[1])
        w_o, w_gu_t, w_d = w_rest[l]
        merged = _mix_fwd(proj, y_b, w_spatial[l], b_cols, vec(sgu_norm_g, l), f"mix_fwd_{l}")
        o = _matmul(merged, w_o, "nn", F32, f"out_proj_{l}")
        x1, h2 = _post_norm_fwd(xl, o, vec(mix_post_g, l), vec(ffn_pre_g, l), f"post_mix_{l}")
        gu = _matmul(h2, w_gu_t, "nt", F32, f"ffn_gu_{l}")
        act = _swiglu_fwd(gu, f"swiglu_fwd_{l}")
        dn = _matmul(act, w_d, "nn", F32, f"ffn_down_{l}")
        saved.append(dict(x0=xl, h=h, proj=proj, b_pad=b_pad, b_cols=b_cols, c_pad=c_pad, y_b=y_b, e_row=e_row,
                          merged=merged, o=o, x1=x1, h2=h2, gu=gu, act=act, dn=dn))
        if l + 1 < depth:
            xl, h = _post_norm_fwd(x1, dn, vec(ffn_post_g, l), vec(mix_pre_g, l + 1), f"post_ffn_{l}")
        else:
            dy, loss = _post_norm_loss(x1, dn, vec(ffn_post_g, l), target, "loss")

    g_small = [None] * depth
    grad_sums = {}
    pending = []
    late_rows = None
    dx = dy
    dd, dg_ffn_post = _norm_bwd(dx, f"bwd_ffn_post_{depth - 1}", post=(saved[-1]["dn"], vec(ffn_post_g, depth - 1)))
    for l in reversed(range(depth)):
        s = saved[l]
        w_o, w_gu_t, w_d = w_rest[l]
        dact = _matmul(dd, w_d, "nt", F32, f"d_act_{l}")
        dw_d = _matmul(s["act"], dd, "tn", BF16, f"dw_down_{l}")
        dgu = _swiglu_bwd(s["gu"], dact, f"swiglu_bwd_{l}")
        dh2 = _matmul(dgu, w_gu_t, "nn", F32, f"d_h2_{l}")
        dw_gu_t = _matmul(dgu, s["h2"], "tn", BF16, f"dw_gu_{l}")
        dx1, dg_ffn_pre, do, dg_mix_post = _norm_bwd(
            dx, f"bwd_post_mix_{l}", pre=(dh2, s["x1"], vec(ffn_pre_g, l)), post=(s["o"], vec(mix_post_g, l)))
        dmerged = _matmul(do, w_o, "nt", F32, f"d_merged_{l}")
        dw_o = _matmul(s["merged"], do, "tn", BF16, f"dw_out_{l}")
        dproj, dyb, delta_b, dw_s, dbt, dg_v = _mix_bwd(
            s["proj"], s["y_b"], dmerged, w_spatial[l], s["b_cols"], vec(sgu_norm_g, l), sel, f"mix_bwd_{l}")
        db_s = dbt[:, :N_GROUPS].T.reshape(1, D_MODEL)
        riders = pending + [("scatter", _pack_grad_rest(dw_o, dw_gu_t, dw_d))]
        keys = ([("in", l + 1)] if pending else []) + [("rest", l)] + (["small"] if l == 0 else [])
        if l == 0:
            g_small[0] = jnp.concatenate([zero_row, dg_v, dg_mix_post, dg_ffn_pre, dg_ffn_post, db_s, zero_row,
                                          zero_row, dw_s.reshape(CHUNK, D_MODEL)], axis=0)
            riders = riders + [("gather", jnp.concatenate([g_small[1], g_small[0]], axis=0))]
        dproj, dc_key, dc_query, *arrived = _attn_bwd_t(s["proj"], dyb, s["e_row"], delta_b, s["c_pad"], dproj,
                                                          f"attn_bwd_{l}", comm=riders)
        for key, got in zip(keys, arrived):
            grad_sums[key] = got
        dc = jnp.pad((dc_key + dc_query)[:, 0, :].T, ((0, 0), (0, HEAD_DIM - N_HEADS)))
        dproj, db_f = _forget_bwd_t(s["proj"], s["b_pad"], dc, dproj, f"forget_bwd_{l}")
        db_f_row = jnp.pad(db_f, ((0, 0), (0, D_MODEL - HEAD_DIM)))
        dw_in_t = _matmul(dproj, s["h"], "tn", BF16, f"dw_in_{l}")
        if l > 0:
            dh = _matmul(dproj, w_in_t[l], "nn", F32, f"d_h_{l}")
            pending = [("windows", _pack_grad_in(dw_in_t))]
            dx, dg_mix_pre, dd, dg_ffn_post_below = _norm_bwd(
                dx1, f"bwd_pre_mix_{l}", pre=(dh, s["x0"], vec(mix_pre_g, l)),
                post=(saved[l - 1]["dn"], vec(ffn_post_g, l - 1)))
            g_small[l] = jnp.concatenate([dg_mix_pre, dg_v, dg_mix_post, dg_ffn_pre, dg_ffn_post, db_s, db_f_row,
                                          zero_row, dw_s.reshape(CHUNK, D_MODEL)], axis=0)
            dg_ffn_post = dg_ffn_post_below
        else:
            dw_nat = _pack_grad_in(dw_in_t)
            own = lax.dynamic_slice(dw_nat, (IN_STRIDE * me, 0), (P_IN, D_MODEL))
            land = lax.dynamic_update_slice(jnp.zeros((N_DEV, P_IN, D_MODEL), BF16), own[None], (me, 0, 0))
            *in_flight, token = _windows_start(dw_nat, land, "start_grad_in_0")
            dh = _matmul(dproj, w_in_t[l], "nn", F32, f"d_h_{l}", deps=[token])
            dx, dg_mix_pre = _norm_bwd(dx1, f"bwd_pre_mix_{l}", pre=(dh, s["x0"], vec(mix_pre_g, l)))
            late_rows = jnp.concatenate([dg_mix_pre, db_f_row] + [zero_row] * 6, axis=0)
    grad_x = dx.reshape(x.shape)

    late = _sum_slots(_exchange("gather", late_rows, "gather_late_small_grads")[0], 8, "sum_late_small_grads")
    small_sum = _sum_slots(grad_sums["small"], S_ROWS, "sum_small_grads")
    loss = lax.psum(loss.reshape(()), ("x", "y", "c"))

    sm = small_sum.reshape(depth, S_ROWS, D_MODEL)[::-1]
    g = {
        "mix_pre_g": jnp.stack([late[0], sm[1, 0]]), "sgu_norm_g": sm[:, 1], "mix_post_g": sm[:, 2],
        "ffn_pre_g": sm[:, 3], "ffn_post_g": sm[:, 4], "b_spatial": sm[:, 5].reshape(depth, N_GROUPS, CHUNK),
        "b_forget": jnp.stack([late[1, :N_HEADS], sm[1, 6, :N_HEADS]]),
        "w_spatial": sm[:, 8:].reshape(depth, N_GROUPS, CHUNK, CHUNK),
    }
    g_rest = jnp.stack([_sum_slots(grad_sums[("rest", l)], 592, f"sum_grad_rest_{l}") for l in range(depth)])
    g["w_out"] = g_rest[:, R_OUT:R_OUT + SH_OUT]
    g["w_gate"] = jnp.swapaxes(g_rest[:, R_G:R_G + SH_FF], 1, 2)
    g["w_up"] = jnp.swapaxes(g_rest[:, R_U:R_U + SH_FF], 1, 2)
    g["w_down"] = g_rest[:, R_D:R_D + SH_FF]

    names = ["mix_pre_g", "w_in", "b_forget", "sgu_norm_g", "w_spatial", "b_spatial", "w_out", "mix_post_g",
             "ffn_pre_g", "w_gate", "w_up", "w_down", "ffn_post_g"]
    ws = dict(mix_pre_g=mix_pre_g, w_in=w_in, b_forget=b_forget, sgu_norm_g=sgu_norm_g, w_spatial=w_spatial,
              b_spatial=b_spatial, w_out=w_out, mix_post_g=mix_post_g, ffn_pre_g=ffn_pre_g, w_gate=w_gate, w_up=w_up,
              w_down=w_down, ffn_post_g=ffn_post_g)
    ms = dict(mix_pre_g=m_mix_pre_g, w_in=m_w_in, b_forget=m_b_forget, sgu_norm_g=m_sgu_norm_g, w_spatial=m_w_spatial,
              b_spatial=m_b_spatial, w_out=m_w_out, mix_post_g=m_mix_post_g, ffn_pre_g=m_ffn_pre_g, w_gate=m_w_gate,
              w_up=m_w_up, w_down=m_w_down, ffn_post_g=m_ffn_post_g)
    vs = dict(mix_pre_g=v_mix_pre_g, w_in=v_w_in, b_forget=v_b_forget, sgu_norm_g=v_sgu_norm_g, w_spatial=v_w_spatial,
              b_spatial=v_b_spatial, w_out=v_w_out, mix_post_g=v_mix_post_g, ffn_pre_g=v_ffn_pre_g, w_gate=v_w_gate,
              w_up=v_w_up, w_down=v_w_down, ffn_post_g=v_ffn_post_g)
    updates = {}
    for n in [n for n in names if n != "w_in"] + ["w_in"]:
        if n == "w_in":
            grad_sums[("in", 0)] = _windows_wait(*in_flight, updates["ffn_post_g"][0], "wait_grad_in_0")
            g_in = jnp.stack([_sum_slots(grad_sums[("in", l)], 304, f"sum_grad_in_{l}") for l in range(depth)])
            g["w_in"] = jnp.swapaxes(lax.dynamic_slice(g_in, (0, me, 0), (depth, SH_IN, D_MODEL)), 1, 2)
        flat = (-1, ws[n].shape[-1])
        updates[n] = _adamw(ws[n].reshape(flat), g[n].reshape(flat), ms[n].reshape(flat), vs[n].reshape(flat),
                            f"adamw_{n}")
    deltas = [updates[n][0].reshape(ws[n].shape) for n in names]
    new_m = [updates[n][1].reshape(ws[n].shape) for n in names]
    new_v = [updates[n][2].reshape(ws[n].shape) for n in names]
    grads = [g[n].reshape(ws[n].shape) for n in names]
    return (loss, grad_x, *grads, *deltas, *new_m, *new_v)
```

```python
import math

import jax
import jax.numpy as jnp
from jax import lax
from jax.experimental import pallas as pl
from jax.experimental.pallas import tpu as pltpu

F32 = jnp.float32
BF16 = jnp.bfloat16

N_DEV = 8
D_MODEL = 1024
N_HEADS = 8
HEAD_DIM = 128
CHUNK = 128
N_GROUPS = 8
D_FF = 2816
IN_WIDTH = 7 * D_MODEL + N_HEADS
IN_PAD = 7680
EPS = 1e-6
GATED = 4 * D_MODEL
F_WIDTH = 512
QKV_WIDTH = 3 * HEAD_DIM
CB_F = GATED // HEAD_DIM
CB_QKV = (GATED + F_WIDTH) // HEAD_DIM
assert GATED + F_WIDTH + N_HEADS * QKV_WIDTH == IN_PAD and (GATED + F_WIDTH) % QKV_WIDTH == 0

ADAM_LR, ADAM_B1, ADAM_B2, ADAM_EPS, ADAM_WD, ADAM_STEP = 0.001, 0.9, 0.999, 1e-08, 0.01, 10

SH_IN = IN_WIDTH // N_DEV
SH_OUT = D_MODEL // N_DEV
SH_FF = D_FF // N_DEV
P_IN = 912
IN_STRIDE = 7 * HEAD_DIM
IN_ROWS = IN_STRIDE * (N_DEV - 1) + P_IN
R_OUT, R_G, R_U, R_D = 0, SH_OUT, SH_OUT + SH_FF, SH_OUT + 2 * SH_FF
P_REST = SH_OUT + 3 * SH_FF
S_ROWS = 136

ATTN_BLOCK = 512
ROW_BLOCK = 512
MM_TM, MM_TN, MM_TK = 1536, 1536, 2048
VMEM_CAP = 56 << 20
NEG = -1e30


def _tile(n, cap, unit=128):
    if n <= cap:
        return n
    best = None
    for t in range(unit, cap + 1, unit):
        if n % t == 0:
            best = t
    assert best is not None, (n, cap)
    return best


def _nbytes(shape, dtype):
    return math.prod(shape) * jnp.dtype(dtype).itemsize


_HBM = pl.BlockSpec(memory_space=pltpu.HBM)
MESH = pl.DeviceIdType.MESH

_N_REMOTE = N_DEV - 1


def _exchange_shapes(kind, x):
    shape = {"gather": (N_DEV,) + x.shape, "scatter": x.shape, "windows": (N_DEV, P_IN) + x.shape[1:]}[kind]
    return [jax.ShapeDtypeStruct(shape, x.dtype)]


def _exchange_sems(kind):
    del kind
    return [pltpu.SemaphoreType.DMA((_N_REMOTE,)), pltpu.SemaphoreType.DMA((_N_REMOTE,)), pltpu.SemaphoreType.DMA]


def _plan(kind, x_ref, outs, sems):
    send_sems, recv_sems, local_sem = sems
    x, y, c = lax.axis_index("x"), lax.axis_index("y"), lax.axis_index("c")

    def remote(src, dst, k, to):
        return pltpu.make_async_remote_copy(src_ref=src, dst_ref=dst, send_sem=send_sems.at[k], recv_sem=recv_sems.at[k],
                                            device_id=to, device_id_type=MESH)

    sibling = (x, y, 1 - c)
    if kind == "gather":
        out, = outs
        slot = lambda px, py, pc: out.at[4 * px + 2 * py + pc]
        chips = [(1 - x, y), (x, 1 - y), (1 - x, 1 - y)]
        local = [pltpu.make_async_copy(x_ref, slot(x, y, c), local_sem)]
        first = [remote(x_ref, slot(x, y, c), 0, sibling)]
        first += [remote(x_ref, slot(x, y, c), 1 + k, (*chip, c)) for k, chip in enumerate(chips)]
        relays = [(remote(x_ref, slot(*chip, c), 1 + k, (*chip, c)), remote(slot(*chip, c), slot(*chip, c), 4 + k, sibling))
                  for k, chip in enumerate(chips)]
        arrivals = [remote(x_ref, slot(x, y, 1 - c), 0, sibling)]
        arrivals += [remote(x_ref, slot(*chip, 1 - c), 4 + k, sibling) for k, chip in enumerate(chips)]
        return local, first, relays, arrivals
    out, = outs
    mine = 4 * x + 2 * y + c
    if kind == "scatter":
        src = lambda d: x_ref.at[d]
    else:
        src = lambda d: x_ref.at[pl.ds(pl.multiple_of(IN_STRIDE * d, 16), P_IN)]
    first, arrivals = [], []
    for j in range(1, N_DEV):
        peer = (1 - x if j & 4 else x, 1 - y if j & 2 else y, 1 - c if j & 1 else c)
        theirs = 4 * peer[0] + 2 * peer[1] + peer[2]
        first.append(remote(src(theirs), out.at[mine], j - 1, peer))
        arrivals.append(remote(src(theirs), out.at[theirs], j - 1, peer))
    local = [] if local_sem is None else [pltpu.make_async_copy(src(mine), out.at[mine], local_sem)]
    return local, first, [], arrivals


def _start(plan):
    local, first, _, _ = plan
    for cp in local + first:
        cp.start()


def _finish(plan):
    local, first, relays, arrivals = plan
    for arrival, onward in relays:
        arrival.wait_recv()
        onward.start()
    for cp in arrivals:
        cp.wait_recv()
    for cp in first + [onward for _, onward in relays]:
        cp.wait_send()
    for cp in local:
        cp.wait()


def _window_copies(x_ref, land_ref, send_sems, recv_sems):
    _, first, _, arrivals = _plan("windows", x_ref, [land_ref], (send_sems, recv_sems, None))
    return first, arrivals


_SEM = pl.BlockSpec(memory_space=pltpu.SEMAPHORE)
_DATAFLOW = pltpu.SideEffectType.DATAFLOW_SIDE_EFFECTING


def _windows_start(x, land, name):
    def body(x_ref, land_ref, send_sems, recv_sems, x_thru, land_thru, token):
        del x_thru, land_thru
        for cp in _window_copies(x_ref, land_ref, send_sems, recv_sems)[0]:
            cp.start()
        token[...] = jnp.zeros_like(token)

    return pl.pallas_call(
        body, name=name,
        out_shape=(pltpu.SemaphoreType.DMA((_N_REMOTE,)), pltpu.SemaphoreType.DMA((_N_REMOTE,)),
                   pltpu.HBM(x.shape, x.dtype), pltpu.HBM(land.shape, land.dtype), jax.ShapeDtypeStruct((8, 128), F32)),
        in_specs=(_HBM, _HBM), out_specs=(_SEM, _SEM, _HBM, _HBM, pl.BlockSpec(memory_space=pltpu.VMEM)),
        input_output_aliases={0: 2, 1: 3}, compiler_params=pltpu.CompilerParams(has_side_effects=_DATAFLOW),
    )(pltpu.with_memory_space_constraint(x, pltpu.HBM), pltpu.with_memory_space_constraint(land, pltpu.HBM))


def _windows_wait(send_sems, recv_sems, x_thru, land_thru, after, name):
    def body(x_ref, land_ref, send_sems, recv_sems, *rest):
        first, arrivals = _window_copies(x_ref, land_ref, send_sems, recv_sems)
        for cp in first:
            cp.wait_send()
        for cp in arrivals:
            cp.wait_recv()

    return pl.pallas_call(
        body, name=name,
        out_shape=(pltpu.HBM(x_thru.shape, x_thru.dtype), pltpu.HBM(land_thru.shape, land_thru.dtype)),
        in_specs=(_HBM, _HBM, _SEM, _SEM) + (pl.BlockSpec(memory_space=pl.ANY),) * len(after), out_specs=(_HBM, _HBM),
        input_output_aliases={0: 0, 1: 1}, compiler_params=pltpu.CompilerParams(has_side_effects=_DATAFLOW),
    )(x_thru, land_thru, send_sems, recv_sems, *after)[1]


def _exchange(kind, x, name):
    n_out = len(_exchange_shapes(kind, x))

    def body(x_ref, *refs):
        plan = _plan(kind, x_ref, refs[:n_out], refs[n_out:])
        _start(plan)
        _finish(plan)

    return pl.pallas_call(body, name=name, out_shape=_exchange_shapes(kind, x), in_specs=[_HBM],
                          out_specs=[_HBM] * n_out, scratch_shapes=_exchange_sems(kind))(x)


def _call(body, name, grid, in_specs, out_specs, out_shape, args, scratch=(), sem=None, nbytes=0, aliases=None,
          comm=(), deps=()):
    in_specs, out_specs, out_shape, args, scratch = (list(in_specs), list(out_specs), list(out_shape), list(args),
                                                     list(scratch))
    if deps:
        n_real, n_deps, unordered = len(args), len(deps), body

        def body(*refs):
            unordered(*refs[:n_real], *refs[n_real + n_deps:])

        in_specs += [pl.BlockSpec(memory_space=pl.ANY)] * n_deps
        args += list(deps)
    if comm:
        n_in, n_out, n_scr, n_ops = len(args), len(out_shape), len(scratch), len(comm)
        kinds = [kind for kind, _ in comm]
        shapes = [_exchange_shapes(kind, x) for kind, x in comm]
        inner = body

        def body(*refs):
            ins, refs = refs[:n_in], refs[n_in:]
            cins, refs = refs[:n_ops], refs[n_ops:]
            outs, refs = refs[:n_out], refs[n_out:]
            couts = []
            for sh in shapes:
                couts.append(refs[:len(sh)])
                refs = refs[len(sh):]
            own_scratch, sems = refs[:n_scr], refs[n_scr:]
            first = last = None
            for axis, size in enumerate(grid):
                at_start, at_end = pl.program_id(axis) == 0, pl.program_id(axis) == size - 1
                first = at_start if first is None else first & at_start
                last = at_end if last is None else last & at_end
            plans = [_plan(kinds[o], cins[o], couts[o], sems[3 * o:3 * o + 3]) for o in range(n_ops)]

            @pl.when(first)
            def _():
                for p in plans:
                    _start(p)

            inner(*ins, *outs, *own_scratch)

            @pl.when(last)
            def _():
                for p in plans:
                    _finish(p)

        in_specs += [_HBM] * n_ops
        args += [x for _, x in comm]
        for kind, sh in zip(kinds, shapes):
            out_shape += sh
            out_specs += [_HBM] * len(sh)
            scratch += _exchange_sems(kind)
        sem = ("arbitrary",) * len(grid)
    limit = int(min(max(2 * nbytes + (8 << 20), 32 << 20), VMEM_CAP))
    return pl.pallas_call(
        body, name=name, grid=grid, in_specs=in_specs, out_specs=out_specs, out_shape=out_shape,
        scratch_shapes=scratch, input_output_aliases=aliases or {},
        compiler_params=pltpu.CompilerParams(dimension_semantics=sem, vmem_limit_bytes=limit),
    )(*args)


def _sum_slots(x, tr, name):
    n, r, c = x.shape

    def body(x_ref, o_ref):
        acc = x_ref[0].astype(F32)
        for d in range(1, n):
            acc = acc + x_ref[d].astype(F32)
        o_ref[...] = acc

    return _call(body, name, (r // tr,), [pl.BlockSpec((n, tr, c), lambda i: (0, i, 0))],
                 [pl.BlockSpec((tr, c), lambda i: (i, 0))], [jax.ShapeDtypeStruct((r, c), F32)], [x],
                 sem=("parallel",), nbytes=_nbytes((n, tr, c), x.dtype) + _nbytes((tr, c), F32))[0]


_DIMS = {"nn": (((1,), (0,)), ((), ())), "nt": (((1,), (1,)), ((), ())), "tn": (((0,), (0,)), ((), ()))}


def _matmul(a, b, mode, out_dtype, name, comm=(), deps=()):
    if mode == "nn":
        (m, k), (k2, n) = a.shape, b.shape
    elif mode == "nt":
        (m, k), (n, k2) = a.shape, b.shape
    else:
        (k, m), (k2, n) = a.shape, b.shape
    assert k == k2, (a.shape, b.shape, mode)
    tm, tn, tk = _tile(m, MM_TM), _tile(n, MM_TN), _tile(k, MM_TK)
    nk = k // tk
    dims = _DIMS[mode]
    a_spec = {"nn": pl.BlockSpec((tm, tk), lambda i, j, kk: (i, kk)),
              "nt": pl.BlockSpec((tm, tk), lambda i, j, kk: (i, kk)),
              "tn": pl.BlockSpec((tk, tm), lambda i, j, kk: (kk, i))}[mode]
    b_spec = {"nn": pl.BlockSpec((tk, tn), lambda i, j, kk: (kk, j)),
              "nt": pl.BlockSpec((tn, tk), lambda i, j, kk: (j, kk)),
              "tn": pl.BlockSpec((tk, tn), lambda i, j, kk: (kk, j))}[mode]

    def partial_product(a_ref, b_ref):
        return lax.dot_general(a_ref[...].astype(BF16), b_ref[...].astype(BF16), dims, preferred_element_type=F32)

    if nk == 1:
        def body(a_ref, b_ref, o_ref):
            o_ref[...] = partial_product(a_ref, b_ref).astype(o_ref.dtype)
        scratch = []
    else:
        def body(a_ref, b_ref, o_ref, acc_ref):
            kk = pl.program_id(2)

            @pl.when(kk == 0)
            def _():
                acc_ref[...] = jnp.zeros_like(acc_ref)

            acc_ref[...] += partial_product(a_ref, b_ref)

            @pl.when(kk == nk - 1)
            def _():
                o_ref[...] = acc_ref[...].astype(o_ref.dtype)
        scratch = [pltpu.VMEM((tm, tn), F32)]

    nbytes = (_nbytes((tm, tk), a.dtype) + _nbytes((tk, tn), b.dtype) + _nbytes((tm, tn), out_dtype)
              + _nbytes((tm, tn), F32))
    res = _call(body, name, (m // tm, n // tn, nk), [a_spec, b_spec],
                [pl.BlockSpec((tm, tn), lambda i, j, kk: (i, j))], [jax.ShapeDtypeStruct((m, n), out_dtype)], [a, b],
                scratch=scratch, sem=("parallel", "parallel", "arbitrary"), nbytes=nbytes, comm=comm, deps=deps)
    return res if comm else res[0]


def _rms(x):
    return lax.rsqrt(jnp.mean(x * x, axis=-1, keepdims=True) + EPS)


def _rms_bwd(dz, a, g):
    r = _rms(a)
    dzg = dz * g
    da = r * dzg - a * (r * r * r) * jnp.mean(dzg * a, axis=-1, keepdims=True)
    return da, dz * (a * r)


def _row_spec(tr, width):
    return pl.BlockSpec((tr, width), lambda i: (i, 0))


def _vec_spec(width):
    return pl.BlockSpec((1, width), lambda i: (0, 0))


def _rms_fwd(x, g, name):
    t, d = x.shape
    tr = min(ROW_BLOCK, t)

    def body(x_ref, g_ref, h_ref):
        xv = x_ref[...]
        h_ref[...] = ((xv * _rms(xv)) * g_ref[...]).astype(BF16)

    return _call(body, name, (t // tr,), [_row_spec(tr, d), _vec_spec(d)], [_row_spec(tr, d)],
                 [jax.ShapeDtypeStruct((t, d), BF16)], [x, g], sem=("parallel",), nbytes=3 * _nbytes((tr, d), F32))[0]


def _post_norm_fwd(x, o, g_post, g_next, name):
    t, d = x.shape
    tr = min(ROW_BLOCK, t)

    def body(x_ref, o_ref, gp_ref, gn_ref, xn_ref, h_ref):
        ov = o_ref[...]
        xn = x_ref[...] + (ov * _rms(ov)) * gp_ref[...]
        xn_ref[...] = xn
        h_ref[...] = ((xn * _rms(xn)) * gn_ref[...]).astype(BF16)

    return _call(body, name, (t // tr,), [_row_spec(tr, d), _row_spec(tr, d), _vec_spec(d), _vec_spec(d)],
                 [_row_spec(tr, d), _row_spec(tr, d)],
                 [jax.ShapeDtypeStruct((t, d), F32), jax.ShapeDtypeStruct((t, d), BF16)], [x, o, g_post, g_next],
                 sem=("parallel",), nbytes=5 * _nbytes((tr, d), F32))


def _post_norm_loss(x, o, g_post, target, name):
    t, d = x.shape
    tr = min(ROW_BLOCK, t)

    def body(x_ref, o_ref, gp_ref, t_ref, dy_ref, loss_ref):
        ov = o_ref[...]
        err = x_ref[...] + (ov * _rms(ov)) * gp_ref[...] - t_ref[...]
        dy_ref[...] = err / d
        part = 0.5 * jnp.sum(jnp.mean(err * err, axis=-1, keepdims=True), axis=0, keepdims=True)

        @pl.when(pl.program_id(0) == 0)
        def _():
            loss_ref[...] = jnp.zeros_like(loss_ref)

        loss_ref[...] += part

    return _call(body, name, (t // tr,), [_row_spec(tr, d), _row_spec(tr, d), _vec_spec(d), _row_spec(tr, d)],
                 [_row_spec(tr, d), pl.BlockSpec((1, 1), lambda i: (0, 0))],
                 [jax.ShapeDtypeStruct((t, d), F32), jax.ShapeDtypeStruct((1, 1), F32)], [x, o, g_post, target],
                 sem=("arbitrary",), nbytes=5 * _nbytes((tr, d), F32))


def _norm_bwd(dskip, name, pre=None, post=None):
    t, d = dskip.shape
    tr = min(ROW_BLOCK, t)
    n_in = 1 + (3 if pre else 0) + (2 if post else 0)

    def body(*refs):
        ins, outs = list(refs[:n_in]), list(refs[n_in:])
        first = pl.program_id(0) == 0
        dx = ins.pop(0)[...]
        if pre:
            dh_ref, xin_ref, gpre_ref = ins.pop(0), ins.pop(0), ins.pop(0)
            dxin, dg_rows = _rms_bwd(dh_ref[...].astype(F32), xin_ref[...], gpre_ref[...])
            dx = dx + dxin
            dx_ref, dgpre_ref = outs.pop(0), outs.pop(0)
            dx_ref[...] = dx

            @pl.when(first)
            def _():
                dgpre_ref[...] = jnp.zeros_like(dgpre_ref)

            dgpre_ref[...] += jnp.sum(dg_rows, axis=0, keepdims=True)
        if post:
            a_ref, gpost_ref = ins.pop(0), ins.pop(0)
            da, dg_rows = _rms_bwd(dx, a_ref[...], gpost_ref[...])
            da_ref, dgpost_ref = outs.pop(0), outs.pop(0)
            da_ref[...] = da.astype(BF16)

            @pl.when(first)
            def _():
                dgpost_ref[...] = jnp.zeros_like(dgpost_ref)

            dgpost_ref[...] += jnp.sum(dg_rows, axis=0, keepdims=True)

    args, in_specs, out_shape, out_specs = [dskip], [_row_spec(tr, d)], [], []
    if pre:
        args += list(pre)
        in_specs += [_row_spec(tr, d), _row_spec(tr, d), _vec_spec(d)]
        out_shape += [jax.ShapeDtypeStruct((t, d), F32), jax.ShapeDtypeStruct((1, d), F32)]
        out_specs += [_row_spec(tr, d), _vec_spec(d)]
    if post:
        args += list(post)
        in_specs += [_row_spec(tr, d), _vec_spec(d)]
        out_shape += [jax.ShapeDtypeStruct((t, d), BF16), jax.ShapeDtypeStruct((1, d), F32)]
        out_specs += [_row_spec(tr, d), _vec_spec(d)]
    return _call(body, name, (t // tr,), in_specs, out_specs, out_shape, args, sem=("arbitrary",),
                 nbytes=7 * _nbytes((tr, d), F32))


def _sigmoid(x):
    return 1.0 / (1.0 + jnp.exp(-x))


def _swiglu_fwd(gu, name):
    t = gu.shape[0]
    tr = min(ROW_BLOCK, t)

    def body(gu_ref, a_ref):
        gate, up = gu_ref[:, :D_FF], gu_ref[:, D_FF:]
        a_ref[...] = ((gate * _sigmoid(gate)) * up).astype(BF16)

    return _call(body, name, (t // tr,), [_row_spec(tr, 2 * D_FF)], [_row_spec(tr, D_FF)],
                 [jax.ShapeDtypeStruct((t, D_FF), BF16)], [gu], sem=("parallel",),
                 nbytes=2 * _nbytes((tr, 2 * D_FF), F32))[0]


def _swiglu_bwd(gu, dact, name):
    t = gu.shape[0]
    tr = min(ROW_BLOCK, t)

    def body(gu_ref, da_ref, o_ref):
        gate, up = gu_ref[:, :D_FF], gu_ref[:, D_FF:]
        da = da_ref[...]
        s = _sigmoid(gate)
        o_ref[:, :D_FF] = (da * up * (s * (1.0 + gate * (1.0 - s)))).astype(BF16)
        o_ref[:, D_FF:] = (da * (gate * s)).astype(BF16)

    return _call(body, name, (t // tr,), [_row_spec(tr, 2 * D_FF), _row_spec(tr, D_FF)], [_row_spec(tr, 2 * D_FF)],
                 [jax.ShapeDtypeStruct((t, 2 * D_FF), BF16)], [gu, dact], sem=("parallel",),
                 nbytes=3 * _nbytes((tr, 2 * D_FF), F32))[0]


def _split3(x):
    hi = x.astype(BF16)
    r1 = x - hi.astype(F32)
    mid = r1.astype(BF16)
    lo = (r1 - mid.astype(F32)).astype(BF16)
    return hi, mid, lo


def _dot_exact(mat01, x):
    hi, mid, lo = _split3(x)
    out = jnp.dot(mat01, hi, preferred_element_type=F32)
    out = out + jnp.dot(mat01, mid, preferred_element_type=F32)
    return out + jnp.dot(mat01, lo, preferred_element_type=F32)


def _dot_exact_rhs(x, mat01):
    hi, mid, lo = _split3(x)
    out = jnp.dot(hi, mat01, preferred_element_type=F32)
    out = out + jnp.dot(mid, mat01, preferred_element_type=F32)
    return out + jnp.dot(lo, mat01, preferred_element_type=F32)


def _tri(lower):
    r = lax.broadcasted_iota(jnp.int32, (CHUNK, CHUNK), 0)
    c = lax.broadcasted_iota(jnp.int32, (CHUNK, CHUNK), 1)
    return jnp.where(r >= c if lower else r <= c, 1.0, 0.0).astype(BF16)


def _log_sigmoid(z):
    return jnp.minimum(z, 0.0) - jnp.log(1.0 + jnp.exp(-jnp.abs(z)))


def _forget_cumsum(proj, b_pad, name):
    t = proj.shape[0]
    nb = t // CHUNK

    def body(f_ref, b_ref, c_ref):
        tri = _tri(True)
        b = b_ref[...]

        def blk(i, carry):
            rows = pl.ds(pl.multiple_of(i * CHUNK, CHUNK), CHUNK)
            cs = _dot_exact(tri, _log_sigmoid(f_ref[rows, :] + b)) + carry
            c_ref[rows, :] = cs
            return cs[CHUNK - 1:CHUNK, :]

        lax.fori_loop(0, nb, blk, jnp.zeros((1, HEAD_DIM), F32))

    return _call(body, name, (1,),
                 [pl.BlockSpec((t, HEAD_DIM), lambda i: (0, CB_F)), pl.BlockSpec((1, HEAD_DIM), lambda i: (0, 0))],
                 [pl.BlockSpec((t, HEAD_DIM), lambda i: (0, 0))], [jax.ShapeDtypeStruct((t, HEAD_DIM), F32)],
                 [proj, b_pad], sem=("arbitrary",), nbytes=2 * _nbytes((t, HEAD_DIM), F32))[0]


def _forget_bwd(proj, b_pad, dc_key, dc_query, dproj, name):
    t = proj.shape[0]
    tr = min(ROW_BLOCK, t)
    nb = t // tr
    rev = lambda i: nb - 1 - i

    def body(f_ref, b_ref, dck_ref, dcq_ref, dproj_in, df_ref, db_ref, run_ref):
        del dproj_in

        @pl.when(pl.program_id(0) == 0)
        def _():
            run_ref[...] = jnp.zeros_like(run_ref)
            db_ref[...] = jnp.zeros_like(db_ref)

        tri = _tri(False)
        b = b_ref[...]
        lane = lax.broadcasted_iota(jnp.int32, (CHUNK, HEAD_DIM), 1)
        df_ref[...] = jnp.zeros_like(df_ref)
        for n in reversed(range(tr // CHUNK)):
            rows = slice(n * CHUNK, (n + 1) * CHUNK)
            dc = dck_ref[rows, :]
            for h in range(N_HEADS):
                dc = dc + jnp.where(lane == h, dcq_ref[rows, h * HEAD_DIM:(h + 1) * HEAD_DIM], 0.0)
            dlogf = _dot_exact(tri, dc) + run_ref[...]
            run_ref[...] = dlogf[0:1, :]
            z = f_ref[rows, :] + b
            e = jnp.exp(-jnp.abs(z))
            sig_neg = jnp.where(z >= 0.0, e, 1.0) / (1.0 + e)
            df = jnp.where(lane < N_HEADS, dlogf * sig_neg, 0.0)
            df_ref[rows, :HEAD_DIM] = df.astype(BF16)
            db_ref[...] += jnp.sum(df, axis=0, keepdims=True)

    return _call(body, name, (nb,),
                 [pl.BlockSpec((tr, HEAD_DIM), lambda i: (rev(i), CB_F)), pl.BlockSpec((1, HEAD_DIM), lambda i: (0, 0)),
                  pl.BlockSpec((tr, HEAD_DIM), lambda i: (rev(i), 0)), pl.BlockSpec((tr, D_MODEL), lambda i: (rev(i), 0)),
                  pl.BlockSpec(memory_space=pl.ANY)],
                 [pl.BlockSpec((tr, F_WIDTH), lambda i: (rev(i), GATED // F_WIDTH)),
                  pl.BlockSpec((1, HEAD_DIM), lambda i: (0, 0))],
                 [jax.ShapeDtypeStruct(dproj.shape, dproj.dtype), jax.ShapeDtypeStruct((1, HEAD_DIM), F32)],
                 [proj, b_pad, dc_key, dc_query, dproj], scratch=[pltpu.VMEM((1, HEAD_DIM), F32)],
                 sem=("arbitrary",), nbytes=4 * _nbytes((tr, D_MODEL), F32), aliases={4: 0})


def _causal(block):
    r = lax.broadcasted_iota(jnp.int32, (block, block), 0)
    c = lax.broadcasted_iota(jnp.int32, (block, block), 1)
    return c <= r


def _lanes(x, width):
    return jnp.concatenate([x] * (width // HEAD_DIM), axis=1)


def _attn_fwd(proj, c_pad, c_row, name, comm=()):
    t = proj.shape[0]
    blk = min(ATTN_BLOCK, t)
    nq = t // blk
    scale = HEAD_DIM ** -0.5

    def body(q_ref, k_ref, v_ref, cp_ref, cr_ref, y_ref, e_ref):
        h, i = pl.program_id(0), pl.program_id(1)
        q = (q_ref[...] * scale).astype(BF16)
        lane = lax.broadcasted_iota(jnp.int32, (blk, HEAD_DIM), 1)
        ci = jnp.sum(jnp.where(lane == h, cp_ref[...], 0.0), axis=1, keepdims=True)

        def step(j, carry, diagonal):
            m, l, acc = carry
            off = pl.multiple_of(j * blk, blk)
            k = k_ref[pl.ds(off, blk), :].astype(BF16)
            v = v_ref[pl.ds(off, blk), :].astype(BF16)
            s = lax.dot_general(q, k, _DIMS["nt"], preferred_element_type=F32)
            s = s + ci - cr_ref[:, pl.ds(off, blk)]
            if diagonal:
                s = jnp.where(_causal(blk), s, NEG)
            m_new = jnp.maximum(m, jnp.max(s, axis=1, keepdims=True))
            alpha = jnp.exp(m - m_new)
            p = jnp.exp(s - m_new)
            l = alpha * l + jnp.sum(p, axis=1, keepdims=True)
            acc = alpha * acc + jnp.dot(p.astype(BF16), v, preferred_element_type=F32)
            return m_new, l, acc

        init = (jnp.full((blk, 1), NEG, F32), jnp.zeros((blk, 1), F32), jnp.zeros((blk, HEAD_DIM), F32))
        carry = lax.fori_loop(0, i, lambda j, c: step(j, c, False), init)
        m, l, acc = step(i, carry, True)
        y_ref[...] = acc / l
        e_ref[...] = jnp.broadcast_to(ci - (m + jnp.log(l)), (blk, HEAD_DIM))

    nbytes = 2 * _nbytes((t, HEAD_DIM), F32) + 4 * _nbytes((blk, HEAD_DIM), F32) + 4 * _nbytes((blk, blk), F32)
    head_block = pl.BlockSpec((blk, HEAD_DIM), lambda h, i: (i, h))
    return _call(body, name, (N_HEADS, nq),
                 [pl.BlockSpec((blk, HEAD_DIM), lambda h, i: (i, CB_QKV + 3 * h)),
                  pl.BlockSpec((t, HEAD_DIM), lambda h, i: (0, CB_QKV + 3 * h + 1)),
                  pl.BlockSpec((t, HEAD_DIM), lambda h, i: (0, CB_QKV + 3 * h + 2)),
                  pl.BlockSpec((blk, HEAD_DIM), lambda h, i: (i, 0)),
                  pl.BlockSpec((None, 1, t), lambda h, i: (h, 0, 0))],
                 [head_block, head_block],
                 [jax.ShapeDtypeStruct((t, D_MODEL), F32), jax.ShapeDtypeStruct((t, D_MODEL), F32)],
                 [proj, proj, proj, c_pad, c_row], sem=("parallel", "arbitrary"), nbytes=nbytes, comm=comm)


def _attn_bwd(proj, dy, e_b, delta_b, c_row, dproj, name, comm=()):
    t = proj.shape[0]
    blk = min(ATTN_BLOCK, t)
    nb = t // blk
    scale = HEAD_DIM ** -0.5

    def body(q_ref, k_ref, v_ref, dy_ref, e_ref, dl_ref, cr_ref, dproj_in, dqkv_ref, dc_ref, dr_ref, dq_acc):
        del dproj_in
        j = pl.program_id(1)

        @pl.when(j == 0)
        def _():
            dq_acc[...] = jnp.zeros_like(dq_acc)
            dr_ref[...] = jnp.zeros_like(dr_ref)

        k = k_ref[...].astype(BF16)
        v = v_ref[...].astype(BF16)
        crow = cr_ref[...]

        def step(i, carry, diagonal):
            dk, dv, dcs = carry
            rows = pl.ds(pl.multiple_of(i * blk, blk), blk)
            q = (q_ref[rows, :] * scale).astype(BF16)
            dyi = dy_ref[rows, :].astype(BF16)
            s = lax.dot_general(q, k, _DIMS["nt"], preferred_element_type=F32) + _lanes(e_ref[rows, :], blk) - crow
            if diagonal:
                s = jnp.where(_causal(blk), s, NEG)
            p = jnp.exp(s)
            dv = dv + lax.dot_general(p.astype(BF16), dyi, _DIMS["tn"], preferred_element_type=F32)
            dp = lax.dot_general(dyi, v, _DIMS["nt"], preferred_element_type=F32)
            ds = p * (dp - _lanes(dl_ref[rows, :], blk))
            dsb = ds.astype(BF16)
            dk = dk + lax.dot_general(dsb, q, _DIMS["tn"], preferred_element_type=F32)
            dq_acc[rows, :] += jnp.dot(dsb, k, preferred_element_type=F32) * scale
            dr_ref[rows, :] += jnp.sum(ds, axis=1, keepdims=True)
            return dk, dv, dcs + jnp.sum(ds, axis=0, keepdims=True)

        zeros = jnp.zeros((blk, HEAD_DIM), F32)
        carry = step(j, (zeros, zeros, jnp.zeros((1, blk), F32)), True)
        dk, dv, dcs = lax.fori_loop(j + 1, nb, lambda i, c: step(i, c, False), carry)
        mine = pl.ds(pl.multiple_of(j * blk, blk), blk)
        dqkv_ref[:, :HEAD_DIM] = dq_acc[mine, :].astype(BF16)
        dqkv_ref[:, HEAD_DIM:2 * HEAD_DIM] = dk.astype(BF16)
        dqkv_ref[:, 2 * HEAD_DIM:] = dv.astype(BF16)
        dc_ref[...] = -dcs

    full = lambda cb: pl.BlockSpec((t, HEAD_DIM), lambda h, j: (0, cb(h)))
    head = lambda h: h
    nbytes = (6 * _nbytes((t, HEAD_DIM), F32) + 4 * _nbytes((blk, HEAD_DIM), F32) + 5 * _nbytes((blk, blk), F32))
    return _call(body, name, (N_HEADS, nb),
                 [full(lambda h: CB_QKV + 3 * h),
                  pl.BlockSpec((blk, HEAD_DIM), lambda h, j: (j, CB_QKV + 3 * h + 1)),
                  pl.BlockSpec((blk, HEAD_DIM), lambda h, j: (j, CB_QKV + 3 * h + 2)),
                  full(head), full(head), full(head),
                  pl.BlockSpec((None, 1, blk), lambda h, j: (h, 0, j)),
                  pl.BlockSpec(memory_space=pl.ANY)],
                 [pl.BlockSpec((blk, QKV_WIDTH), lambda h, j: (j, (GATED + F_WIDTH) // QKV_WIDTH + h)),
                  pl.BlockSpec((None, 1, blk), lambda h, j: (h, 0, j)),
                  full(head)],
                 [jax.ShapeDtypeStruct(dproj.shape, dproj.dtype), jax.ShapeDtypeStruct((N_HEADS, 1, t), F32),
                  jax.ShapeDtypeStruct((t, D_MODEL), F32)],
                 [proj, proj, proj, dy, e_b, delta_b, c_row, dproj], scratch=[pltpu.VMEM((t, HEAD_DIM), F32)],
                 sem=("parallel", "arbitrary"), nbytes=nbytes, aliases={7: 0}, comm=comm)


def _chunks(t, blk):
    return [slice(r * blk, (r + 1) * blk) for r in range(t // blk)]


def _lane_of_head(x, h):
    lane = lax.broadcasted_iota(jnp.int32, x.shape, 1)
    return jnp.broadcast_to(jnp.sum(jnp.where(lane == h, x, 0.0), axis=1, keepdims=True), x.shape)


def _causal_t(block):
    r = lax.broadcasted_iota(jnp.int32, (block, block), 0)
    c = lax.broadcasted_iota(jnp.int32, (block, block), 1)
    return r <= c


def _attn_fwd_t(proj, c_pad, c_row, name, comm=()):
    t = proj.shape[0]
    blk = min(ATTN_BLOCK, t)
    nq = t // blk
    scale = HEAD_DIM ** -0.5

    def body(q_ref, k_ref, v_ref, cp_ref, cr_ref, y_ref, e_ref, kb_s, vt_s, cb_s):
        h, i = pl.program_id(0), pl.program_id(1)

        @pl.when(i == 0)
        def _():
            for rows in _chunks(t, blk):
                kb_s[rows, :] = k_ref[rows, :].astype(BF16)
                vt_s[:, rows] = v_ref[rows, :].T.astype(BF16)
                cb_s[rows, :] = _lane_of_head(cp_ref[rows, :], h)

        q = (q_ref[...] * scale).astype(BF16)
        ci = cr_ref[:, pl.ds(pl.multiple_of(i * blk, blk), blk)]

        def step(j, carry, diagonal):
            m, l, acc = carry
            rows = pl.ds(pl.multiple_of(j * blk, blk), blk)
            s = lax.dot_general(kb_s[rows, :], q, _DIMS["nt"], preferred_element_type=F32)
            s = s + ci - _lanes(cb_s[rows, :], blk)
            if diagonal:
                s = jnp.where(_causal_t(blk), s, NEG)
            m_new = jnp.maximum(m, jnp.max(s, axis=0, keepdims=True))
            alpha = jnp.exp(m - m_new)
            p = jnp.exp(s - m_new)
            l = alpha * l + jnp.sum(p, axis=0, keepdims=True)
            acc = alpha * acc + jnp.dot(vt_s[:, rows], p.astype(BF16), preferred_element_type=F32)
            return m_new, l, acc

        init = (jnp.full((1, blk), NEG, F32), jnp.zeros((1, blk), F32), jnp.zeros((HEAD_DIM, blk), F32))
        carry = lax.fori_loop(0, i, lambda j, c: step(j, c, False), init)
        m, l, acc = step(i, carry, True)
        y_ref[...] = (acc / l).T
        e_ref[...] = ci - (m + jnp.log(l))

    nbytes = 3 * _nbytes((t, HEAD_DIM), F32) + 4 * _nbytes((blk, HEAD_DIM), F32) + 4 * _nbytes((blk, blk), F32)
    return _call(body, name, (N_HEADS, nq),
                 [pl.BlockSpec((blk, HEAD_DIM), lambda h, i: (i, CB_QKV + 3 * h)),
                  pl.BlockSpec((t, HEAD_DIM), lambda h, i: (0, CB_QKV + 3 * h + 1)),
                  pl.BlockSpec((t, HEAD_DIM), lambda h, i: (0, CB_QKV + 3 * h + 2)),
                  pl.BlockSpec((t, HEAD_DIM), lambda h, i: (0, 0)),
                  pl.BlockSpec((None, 1, t), lambda h, i: (h, 0, 0))],
                 [pl.BlockSpec((blk, HEAD_DIM), lambda h, i: (i, h)),
                  pl.BlockSpec((None, 1, blk), lambda h, i: (h, 0, i))],
                 [jax.ShapeDtypeStruct((t, D_MODEL), F32), jax.ShapeDtypeStruct((N_HEADS, 1, t), F32)],
                 [proj, proj, proj, c_pad, c_row],
                 scratch=[pltpu.VMEM((t, HEAD_DIM), BF16), pltpu.VMEM((HEAD_DIM, t), BF16),
                          pltpu.VMEM((t, HEAD_DIM), F32)],
                 sem=("parallel", "arbitrary"), nbytes=nbytes, comm=comm)


def _attn_bwd_t(proj, dy, e_row, delta_b, c_pad, dproj, name, comm=()):
    t = proj.shape[0]
    blk = min(ATTN_BLOCK, t)
    nb = t // blk
    scale = HEAD_DIM ** -0.5

    def body(q_ref, k_ref, v_ref, dy_ref, e_ref, dl_ref, cp_ref, dproj_in, dqkv_ref, dck_ref, dcq_ref,
             qb_s, qt_s, dyb_s, dyt_s, dl_s, dqt_acc):
        del dproj_in
        h, j = pl.program_id(0), pl.program_id(1)

        @pl.when(j == 0)
        def _():
            for rows in _chunks(t, blk):
                qs = q_ref[rows, :] * scale
                qb_s[rows, :] = qs.astype(BF16)
                qt_s[:, rows] = qs.T.astype(BF16)
                dyr = dy_ref[rows, :]
                dyb_s[rows, :] = dyr.astype(BF16)
                dyt_s[:, rows] = dyr.T.astype(BF16)
                dl_s[:, rows] = dl_ref[rows, :].T[0:1, :]
            dqt_acc[...] = jnp.zeros_like(dqt_acc)
            dcq_ref[...] = jnp.zeros_like(dcq_ref)

        kf = k_ref[...]
        kb, kt = kf.astype(BF16), kf.T.astype(BF16)
        vb = v_ref[...].astype(BF16)
        cj = _lanes(_lane_of_head(cp_ref[...], h), blk)

        def step(i, carry, diagonal):
            dkt, dvt, dkey = carry
            cols = pl.ds(pl.multiple_of(i * blk, blk), blk)
            qi, dyi = qb_s[cols, :], dyb_s[cols, :]
            s = lax.dot_general(kb, qi, _DIMS["nt"], preferred_element_type=F32) + e_ref[:, cols] - cj
            if diagonal:
                s = jnp.where(_causal_t(blk), s, NEG)
            p = jnp.exp(s)
            dp = lax.dot_general(vb, dyi, _DIMS["nt"], preferred_element_type=F32)
            ds = p * (dp - dl_s[:, cols])
            pb, dsb = p.astype(BF16), ds.astype(BF16)
            dvt = dvt + lax.dot_general(dyt_s[:, cols], pb, _DIMS["nt"], preferred_element_type=F32)
            dkt = dkt + lax.dot_general(qt_s[:, cols], dsb, _DIMS["nt"], preferred_element_type=F32)
            dqt_acc[:, cols] += jnp.dot(kt, dsb, preferred_element_type=F32) * scale
            dcq_ref[:, cols] += jnp.sum(ds, axis=0, keepdims=True)
            part = ds[:, :HEAD_DIM]
            for g in range(1, blk // HEAD_DIM):
                part = part + ds[:, g * HEAD_DIM:(g + 1) * HEAD_DIM]
            return dkt, dvt, dkey + part

        zeros = jnp.zeros((HEAD_DIM, blk), F32)
        carry = step(j, (zeros, zeros, jnp.zeros((blk, HEAD_DIM), F32)), True)
        dkt, dvt, dkey = lax.fori_loop(j + 1, nb, lambda i, c: step(i, c, False), carry)
        mine = pl.ds(pl.multiple_of(j * blk, blk), blk)
        dqkv_ref[:, :HEAD_DIM] = dqt_acc[:, mine].T.astype(BF16)
        dqkv_ref[:, HEAD_DIM:2 * HEAD_DIM] = dkt.T.astype(BF16)
        dqkv_ref[:, 2 * HEAD_DIM:] = dvt.T.astype(BF16)
        dck_ref[...] = -jnp.sum(dkey.T, axis=0, keepdims=True)

    full = lambda cb: pl.BlockSpec((t, HEAD_DIM), lambda h, j: (0, cb(h)))
    head = lambda h: h
    row = pl.BlockSpec((None, 1, t), lambda h, j: (h, 0, 0))
    nbytes = (8 * _nbytes((t, HEAD_DIM), F32) + 4 * _nbytes((blk, HEAD_DIM), F32) + 6 * _nbytes((blk, blk), F32))
    return _call(body, name, (N_HEADS, nb),
                 [full(lambda h: CB_QKV + 3 * h),
                  pl.BlockSpec((blk, HEAD_DIM), lambda h, j: (j, CB_QKV + 3 * h + 1)),
                  pl.BlockSpec((blk, HEAD_DIM), lambda h, j: (j, CB_QKV + 3 * h + 2)),
                  full(head), row, full(head),
                  pl.BlockSpec((blk, HEAD_DIM), lambda h, j: (j, 0)),
                  pl.BlockSpec(memory_space=pl.ANY)],
                 [pl.BlockSpec((blk, QKV_WIDTH), lambda h, j: (j, (GATED + F_WIDTH) // QKV_WIDTH + h)),
                  pl.BlockSpec((None, 1, blk), lambda h, j: (h, 0, j)),
                  row],
                 [jax.ShapeDtypeStruct(dproj.shape, dproj.dtype), jax.ShapeDtypeStruct((N_HEADS, 1, t), F32),
                  jax.ShapeDtypeStruct((N_HEADS, 1, t), F32)],
                 [proj, proj, proj, dy, e_row, delta_b, c_pad, dproj],
                 scratch=[pltpu.VMEM((t, HEAD_DIM), BF16), pltpu.VMEM((HEAD_DIM, t), BF16),
                          pltpu.VMEM((t, HEAD_DIM), BF16), pltpu.VMEM((HEAD_DIM, t), BF16),
                          pltpu.VMEM((1, t), F32), pltpu.VMEM((HEAD_DIM, t), F32)],
                 sem=("parallel", "arbitrary"), nbytes=nbytes, aliases={7: 0}, comm=comm)


def _forget_bwd_t(proj, b_pad, dc, dproj, name):
    t = proj.shape[0]
    tr = min(ROW_BLOCK, t)
    nb = t // tr
    rev = lambda i: nb - 1 - i

    def body(f_ref, b_ref, dc_ref, dproj_in, df_ref, db_ref, run_ref):
        del dproj_in

        @pl.when(pl.program_id(0) == 0)
        def _():
            run_ref[...] = jnp.zeros_like(run_ref)
            db_ref[...] = jnp.zeros_like(db_ref)

        tri = _tri(False)
        b = b_ref[...]
        lane = lax.broadcasted_iota(jnp.int32, (CHUNK, HEAD_DIM), 1)
        df_ref[...] = jnp.zeros_like(df_ref)
        for n in reversed(range(tr // CHUNK)):
            rows = slice(n * CHUNK, (n + 1) * CHUNK)
            dlogf = _dot_exact(tri, dc_ref[rows, :]) + run_ref[...]
            run_ref[...] = dlogf[0:1, :]
            z = f_ref[rows, :] + b
            e = jnp.exp(-jnp.abs(z))
            sig_neg = jnp.where(z >= 0.0, e, 1.0) / (1.0 + e)
            df = jnp.where(lane < N_HEADS, dlogf * sig_neg, 0.0)
            df_ref[rows, :HEAD_DIM] = df.astype(BF16)
            db_ref[...] += jnp.sum(df, axis=0, keepdims=True)

    return _call(body, name, (nb,),
                 [pl.BlockSpec((tr, HEAD_DIM), lambda i: (rev(i), CB_F)), pl.BlockSpec((1, HEAD_DIM), lambda i: (0, 0)),
                  pl.BlockSpec((tr, HEAD_DIM), lambda i: (rev(i), 0)), pl.BlockSpec(memory_space=pl.ANY)],
                 [pl.BlockSpec((tr, F_WIDTH), lambda i: (rev(i), GATED // F_WIDTH)),
                  pl.BlockSpec((1, HEAD_DIM), lambda i: (0, 0))],
                 [jax.ShapeDtypeStruct(dproj.shape, dproj.dtype), jax.ShapeDtypeStruct((1, HEAD_DIM), F32)],
                 [proj, b_pad, dc, dproj], scratch=[pltpu.VMEM((1, HEAD_DIM), F32)],
                 sem=("arbitrary",), nbytes=4 * _nbytes((tr, HEAD_DIM), F32), aliases={3: 0})


_GELU_K = math.sqrt(2.0 / math.pi)
_GELU_C = 0.044715


def _gelu(x):
    t = jnp.tanh(_GELU_K * (x + _GELU_C * (x * x * x)))
    return 0.5 * x * (1.0 + t), t


def _gelu_grad(x, t):
    return 0.5 * (1.0 + t) + 0.5 * x * (1.0 - t * t) * (_GELU_K * (1.0 + 3.0 * _GELU_C * (x * x)))


def _layernorm_stats(a):
    mu = jnp.mean(a, axis=-1, keepdims=True)
    xc = a - mu
    r = lax.rsqrt(jnp.mean(xc * xc, axis=-1, keepdims=True) + EPS)
    return xc * r, r


def _group(g):
    return slice(g * CHUNK, (g + 1) * CHUNK)


def _gated_cols(k):
    return pl.BlockSpec((CHUNK, D_MODEL), lambda i: (i, k))


def _fixed(shape):
    return pl.BlockSpec(shape, lambda i: (0,) * len(shape))


def _mix_fwd(proj, y_b, w_s, b_cols, g_v, name):
    t = proj.shape[0]

    def body(u_ref, vs_ref, ga_ref, gb_ref, yb_ref, w_ref, b_ref, gv_ref, o_ref):
        a_u, _ = _gelu(u_ref[...])
        a_v, _ = _gelu(vs_ref[...])
        xhat, _ = _layernorm_stats(a_v)
        vn = (xhat * gv_ref[...]).astype(BF16)
        sa, sb = _sigmoid(ga_ref[...]), _sigmoid(gb_ref[...])
        yb = yb_ref[...]
        mask = _causal(CHUNK)
        for g in range(N_GROUPS):
            cols = _group(g)
            w = jnp.where(mask, w_ref[g], 0.0).astype(BF16)
            mixed = jnp.dot(w, vn[:, cols], preferred_element_type=F32) + b_ref[:, g:g + 1]
            o_ref[:, cols] = (sa[:, cols] * (a_u[:, cols] * mixed) + sb[:, cols] * yb[:, cols]).astype(BF16)

    return _call(body, name, (t // CHUNK,),
                 [_gated_cols(0), _gated_cols(1), _gated_cols(2), _gated_cols(3), _row_spec(CHUNK, D_MODEL),
                  _fixed((N_GROUPS, CHUNK, CHUNK)), _fixed((CHUNK, N_GROUPS)), _fixed((1, D_MODEL))],
                 [_row_spec(CHUNK, D_MODEL)], [jax.ShapeDtypeStruct((t, D_MODEL), BF16)],
                 [proj, proj, proj, proj, y_b, w_s, b_cols, g_v], sem=("parallel",),
                 nbytes=16 * _nbytes((CHUNK, D_MODEL), F32))[0]


def _mix_bwd(proj, y_b, dmerged, w_s, b_cols, g_v, sel, name, comm=()):
    t = proj.shape[0]

    def body(u_ref, vs_ref, ga_ref, gb_ref, yb_ref, dm_ref, w_ref, b_ref, gv_ref, sel_ref,
             dg_ref, dyb_ref, delta_ref, dw_ref, dbt_ref, dgv_ref):
        @pl.when(pl.program_id(0) == 0)
        def _():
            dw_ref[...] = jnp.zeros_like(dw_ref)
            dbt_ref[...] = jnp.zeros_like(dbt_ref)
            dgv_ref[...] = jnp.zeros_like(dgv_ref)

        u, vs = u_ref[...], vs_ref[...]
        a_u, t_u = _gelu(u)
        a_v, t_v = _gelu(vs)
        xhat, r = _layernorm_stats(a_v)
        gv = gv_ref[...]
        vn = (xhat * gv).astype(BF16)
        sa, sb = _sigmoid(ga_ref[...]), _sigmoid(gb_ref[...])
        yb, dm = yb_ref[...], dm_ref[...].astype(F32)
        dyb = dm * sb
        dyb_ref[...] = dyb
        dyb_yb = dyb * yb
        dg_ref[:, 3 * D_MODEL:] = (dm * yb * (sb * (1.0 - sb))).astype(BF16)
        dya = dm * sa
        mask = _causal(CHUNK)
        dmixed_parts, dvn_parts = [], []
        for g in range(N_GROUPS):
            cols = _group(g)
            delta_ref[:, cols] = jnp.broadcast_to(jnp.sum(dyb_yb[:, cols], axis=1, keepdims=True), (CHUNK, CHUNK))
            w = jnp.where(mask, w_ref[g], 0.0).astype(BF16)
            mixed = jnp.dot(w, vn[:, cols], preferred_element_type=F32) + b_ref[:, g:g + 1]
            y_a = a_u[:, cols] * mixed
            dg_ref[:, 2 * D_MODEL + g * CHUNK:2 * D_MODEL + (g + 1) * CHUNK] = (
                dm[:, cols] * y_a * (sa[:, cols] * (1.0 - sa[:, cols]))).astype(BF16)
            dg_ref[:, cols] = (dya[:, cols] * mixed * _gelu_grad(u[:, cols], t_u[:, cols])).astype(BF16)
            dmixed = dya[:, cols] * a_u[:, cols]
            dmb = dmixed.astype(BF16)
            dw = lax.dot_general(dmb, vn[:, cols], _DIMS["nt"], preferred_element_type=F32)
            dw_ref[g] += jnp.where(mask, dw, 0.0)
            dvn_parts.append(lax.dot_general(w, dmb, _DIMS["tn"], preferred_element_type=F32))
            dmixed_parts.append(dmixed)
        dmixed_all = jnp.concatenate(dmixed_parts, axis=1)
        dvn = jnp.concatenate(dvn_parts, axis=1)
        dbt_ref[...] += _dot_exact_rhs(dmixed_all, sel_ref[...])
        dgv_ref[...] += jnp.sum(dvn * xhat, axis=0, keepdims=True)
        dxh = dvn * gv
        da_v = r * (dxh - jnp.mean(dxh, axis=-1, keepdims=True)
                    - xhat * jnp.mean(dxh * xhat, axis=-1, keepdims=True))
        dg_ref[:, D_MODEL:2 * D_MODEL] = (da_v * _gelu_grad(vs, t_v)).astype(BF16)

    row = lambda width: _row_spec(CHUNK, width)
    return _call(body, name, (t // CHUNK,),
                 [_gated_cols(0), _gated_cols(1), _gated_cols(2), _gated_cols(3), row(D_MODEL), row(D_MODEL),
                  _fixed((N_GROUPS, CHUNK, CHUNK)), _fixed((CHUNK, N_GROUPS)), _fixed((1, D_MODEL)),
                  _fixed((D_MODEL, HEAD_DIM))],
                 [row(GATED), row(D_MODEL), row(D_MODEL),
                  _fixed((N_GROUPS, CHUNK, CHUNK)), _fixed((CHUNK, HEAD_DIM)), _fixed((1, D_MODEL))],
                 [jax.ShapeDtypeStruct((t, IN_PAD), BF16), jax.ShapeDtypeStruct((t, D_MODEL), F32),
                  jax.ShapeDtypeStruct((t, D_MODEL), F32), jax.ShapeDtypeStruct((N_GROUPS, CHUNK, CHUNK), F32),
                  jax.ShapeDtypeStruct((CHUNK, HEAD_DIM), F32), jax.ShapeDtypeStruct((1, D_MODEL), F32)],
                 [proj, proj, proj, proj, y_b, dmerged, w_s, b_cols, g_v, sel], sem=("arbitrary",),
                 nbytes=40 * _nbytes((CHUNK, D_MODEL), F32), comm=comm)


def _adamw(w, g, m, v, name):
    r, c = w.shape
    tr = _tile(r, 256, unit=8)

    def body(w_ref, g_ref, m_ref, v_ref, d_ref, nm_ref, nv_ref):
        gv = g_ref[...]
        nm = ADAM_B1 * m_ref[...] + (1.0 - ADAM_B1) * gv
        nv = ADAM_B2 * v_ref[...] + (1.0 - ADAM_B2) * (gv * gv)
        m_hat = nm / (1.0 - ADAM_B1 ** ADAM_STEP)
        v_hat = nv / (1.0 - ADAM_B2 ** ADAM_STEP)
        d_ref[...] = -ADAM_LR * (m_hat / (jnp.sqrt(v_hat) + ADAM_EPS) + ADAM_WD * w_ref[...])
        nm_ref[...] = nm
        nv_ref[...] = nv

    spec = pl.BlockSpec((tr, c), lambda i: (i, 0))
    shape = jax.ShapeDtypeStruct((r, c), F32)
    return _call(body, name, (r // tr,), [spec] * 4, [spec] * 3, [shape] * 3, [w, g, m, v], sem=("parallel",),
                 nbytes=7 * _nbytes((tr, max(c, 128)), F32))


def _pad_rows(a, rows):
    return jnp.pad(a, ((0, rows - a.shape[0]), (0, 0)))


def _pack_rest(w_out, w_gate, w_up, w_down):
    return jnp.concatenate([w_out, w_gate.T, w_up.T, w_down], axis=0).astype(BF16)


def _unpack_in(gathered):
    i = jnp.arange(16)[None, :, None]
    b = jnp.arange(N_DEV)[:, None, None]
    head = jnp.where(i < b, jnp.roll(gathered[:, IN_STRIDE:IN_STRIDE + 16], 1, axis=0), gathered[:, :16])
    nat = jnp.concatenate([head, gathered[:, 16:IN_STRIDE]], axis=1).reshape(N_DEV * IN_STRIDE, D_MODEL)
    qkv = nat[2 * D_MODEL:5 * D_MODEL].reshape(3, N_HEADS, HEAD_DIM, D_MODEL)
    qkv = jnp.transpose(qkv, (1, 0, 2, 3)).reshape(3 * D_MODEL, D_MODEL)
    f = _pad_rows(gathered[N_DEV - 1, IN_STRIDE:IN_STRIDE + N_HEADS], F_WIDTH)
    return jnp.concatenate([nat[:2 * D_MODEL], nat[5 * D_MODEL:7 * D_MODEL], f, qkv], axis=0)


def _unpack_rest(gathered):
    take = lambda lo, n: gathered[:, lo:lo + n, :].reshape(N_DEV * n, D_MODEL)
    return take(R_OUT, SH_OUT), jnp.concatenate([take(R_G, SH_FF), take(R_U, SH_FF)], axis=0), take(R_D, SH_FF)


def _pack_grad_in(dw_in_t):
    qkv = dw_in_t[GATED + F_WIDTH:].reshape(N_HEADS, 3, HEAD_DIM, D_MODEL)
    qkv = jnp.transpose(qkv, (1, 0, 2, 3)).reshape(3 * D_MODEL, D_MODEL)
    return jnp.concatenate([dw_in_t[:2 * D_MODEL], qkv, dw_in_t[2 * D_MODEL:GATED],
                            dw_in_t[GATED:GATED + IN_ROWS - N_DEV * IN_STRIDE]], axis=0)


def _pack_grad_rest(dw_out, dw_gu_t, dw_down):
    split = lambda a, n: a.reshape(N_DEV, n, D_MODEL)
    return jnp.concatenate([split(dw_out, SH_OUT), split(dw_gu_t[:D_FF], SH_FF), split(dw_gu_t[D_FF:], SH_FF),
                            split(dw_down, SH_FF)], axis=1)


def kernel(x, mix_pre_g, w_in, b_forget, sgu_norm_g, w_spatial, b_spatial, w_out, mix_post_g, ffn_pre_g, w_gate, w_up, w_down, ffn_post_g, loss_target, m_mix_pre_g, m_w_in, m_b_forget, m_sgu_norm_g, m_w_spatial, m_b_spatial, m_w_out, m_mix_post_g, m_ffn_pre_g, m_w_gate, m_w_up, m_w_down, m_ffn_post_g, v_mix_pre_g, v_w_in, v_b_forget, v_sgu_norm_g, v_w_spatial, v_b_spatial, v_w_out, v_mix_post_g, v_ffn_pre_g, v_w_gate, v_w_up, v_w_down, v_ffn_post_g):
    depth = w_in.shape[0]
    assert depth == 2
    t = x.shape[1]
    x0 = x.reshape(t, D_MODEL)
    target = loss_target.reshape(t, D_MODEL)
    sel = (jnp.arange(D_MODEL)[:, None] // HEAD_DIM == jnp.arange(HEAD_DIM)[None, :]).astype(BF16)
    vec = lambda a, l: a[l][None, :]
    zero_row = jnp.zeros((1, D_MODEL), F32)

    me = 4 * lax.axis_index("x") + 2 * lax.axis_index("y") + lax.axis_index("c")
    p_in = [lax.dynamic_update_slice(jnp.zeros((P_IN, D_MODEL), BF16), w_in[l].T.astype(BF16), (me, 0))
            for l in range(depth)]
    p_rest = [_pack_rest(w_out[l], w_gate[l], w_up[l], w_down[l]) for l in range(depth)]
    w_in_t = [None] * depth
    w_rest = [None] * depth
    w_in_t[0] = _unpack_in(_exchange("gather", p_in[0], "gather_w_in_0")[0])

    saved = []
    xl = x0
    h = _rms_fwd(xl, vec(mix_pre_g, 0), "rms_in_0")
    dy = loss = None
    for l in range(depth):
        b_pad = jnp.pad(b_forget[l], (0, HEAD_DIM - N_HEADS))[None, :]
        b_cols = b_spatial[l].T
        proj = _matmul(h, w_in_t[l], "nt", F32, f"proj_{l}")
        c_pad = _forget_cumsum(proj, b_pad, f"forget_cumsum_{l}")
        c_row = c_pad[:, :N_HEADS].T[:, None, :]
        riders = [("gather", p_rest[l])] + ([("gather", p_in[l + 1])] if l + 1 < depth else [])
        y_b, e_row, *arrived = _attn_fwd_t(proj, c_pad, c_row, f"attn_fwd_{l}", comm=riders)
        w_rest[l] = _unpack_rest(arrived[0])
        if l + 1 < depth:
            w_in_t[l + 1] = _unpack_in(arrived[1])
        w_o, w_gu_t, w_d = w_rest[l]
        merged = _mix_fwd(proj, y_b, w_spatial[l], b_cols, vec(sgu_norm_g, l), f"mix_fwd_{l}")
        o = _matmul(merged, w_o, "nn", F32, f"out_proj_{l}")
        x1, h2 = _post_norm_fwd(xl, o, vec(mix_post_g, l), vec(ffn_pre_g, l), f"post_mix_{l}")
        gu = _matmul(h2, w_gu_t, "nt", F32, f"ffn_gu_{l}")
        act = _swiglu_fwd(gu, f"swiglu_fwd_{l}")
        dn = _matmul(act, w_d, "nn", F32, f"ffn_down_{l}")
        saved.append(dict(x0=xl, h=h, proj=proj, b_pad=b_pad, b_cols=b_cols, c_pad=c_pad, y_b=y_b, e_row=e_row,
                          merged=merged, o=o, x1=x1, h2=h2, gu=gu, act=act, dn=dn))
        if l + 1 < depth:
            xl, h = _post_norm_fwd(x1, dn, vec(ffn_post_g, l), vec(mix_pre_g, l + 1), f"post_ffn_{l}")
        else:
            dy, loss = _post_norm_loss(x1, dn, vec(ffn_post_g, l), target, "loss")

    g_small = [None] * depth
    grad_sums = {}
    pending = []
    late_rows = None
    dx = dy
    dd, dg_ffn_post = _norm_bwd(dx, f"bwd_ffn_post_{depth - 1}", post=(saved[-1]["dn"], vec(ffn_post_g, depth - 1)))
    for l in reversed(range(depth)):
        s = saved[l]
        w_o, w_gu_t, w_d = w_rest[l]
        dact = _matmul(dd, w_d, "nt", F32, f"d_act_{l}")
        dw_d = _matmul(s["act"], dd, "tn", BF16, f"dw_down_{l}")
        dgu = _swiglu_bwd(s["gu"], dact, f"swiglu_bwd_{l}")
        dh2 = _matmul(dgu, w_gu_t, "nn", F32, f"d_h2_{l}")
        dw_gu_t = _matmul(dgu, s["h2"], "tn", BF16, f"dw_gu_{l}")
        dx1, dg_ffn_pre, do, dg_mix_post = _norm_bwd(
            dx, f"bwd_post_mix_{l}", pre=(dh2, s["x1"], vec(ffn_pre_g, l)), post=(s["o"], vec(mix_post_g, l)))
        dmerged = _matmul(do, w_o, "nt", F32, f"d_merged_{l}")
        dw_o = _matmul(s["merged"], do, "tn", BF16, f"dw_out_{l}")
        dproj, dyb, delta_b, dw_s, dbt, dg_v = _mix_bwd(
            s["proj"], s["y_b"], dmerged, w_spatial[l], s["b_cols"], vec(sgu_norm_g, l), sel, f"mix_bwd_{l}")
        db_s = dbt[:, :N_GROUPS].T.reshape(1, D_MODEL)
        riders = pending + [("scatter", _pack_grad_rest(dw_o, dw_gu_t, dw_d))]
        keys = ([("in", l + 1)] if pending else []) + [("rest", l)] + (["small"] if l == 0 else [])
        if l == 0:
            g_small[0] = jnp.concatenate([zero_row, dg_v, dg_mix_post, dg_ffn_pre, dg_ffn_post, db_s, zero_row,
                                          zero_row, dw_s.reshape(CHUNK, D_MODEL)], axis=0)
            riders = riders + [("gather", jnp.concatenate([g_small[1], g_small[0]], axis=0))]
        dproj, dc_key, dc_query, *arrived = _attn_bwd_t(s["proj"], dyb, s["e_row"], delta_b, s["c_pad"], dproj,
                                                          f"attn_bwd_{l}", comm=riders)
        for key, got in zip(keys, arrived):
            grad_sums[key] = got
        dc = jnp.pad((dc_key + dc_query)[:, 0, :].T, ((0, 0), (0, HEAD_DIM - N_HEADS)))
        dproj, db_f = _forget_bwd_t(s["proj"], s["b_pad"], dc, dproj, f"forget_bwd_{l}")
        db_f_row = jnp.pad(db_f, ((0, 0), (0, D_MODEL - HEAD_DIM)))
        dw_in_t = _matmul(dproj, s["h"], "tn", BF16, f"dw_in_{l}")
        if l > 0:
            dh = _matmul(dproj, w_in_t[l], "nn", F32, f"d_h_{l}")
            pending = [("windows", _pack_grad_in(dw_in_t))]
            dx, dg_mix_pre, dd, dg_ffn_post_below = _norm_bwd(
                dx1, f"bwd_pre_mix_{l}", pre=(dh, s["x0"], vec(mix_pre_g, l)),
                post=(saved[l - 1]["dn"], vec(ffn_post_g, l - 1)))
            g_small[l] = jnp.concatenate([dg_mix_pre, dg_v, dg_mix_post, dg_ffn_pre, dg_ffn_post, db_s, db_f_row,
                                          zero_row, dw_s.reshape(CHUNK, D_MODEL)], axis=0)
            dg_ffn_post = dg_ffn_post_below
        else:
            dw_nat = _pack_grad_in(dw_in_t)
            own = lax.dynamic_slice(dw_nat, (IN_STRIDE * me, 0), (P_IN, D_MODEL))
            land = lax.dynamic_update_slice(jnp.zeros((N_DEV, P_IN, D_MODEL), BF16), own[None], (me, 0, 0))
            *in_flight, token = _windows_start(dw_nat, land, "start_grad_in_0")
            dh = _matmul(dproj, w_in_t[l], "nn", F32, f"d_h_{l}", deps=[token])
            dx, dg_mix_pre = _norm_bwd(dx1, f"bwd_pre_mix_{l}", pre=(dh, s["x0"], vec(mix_pre_g, l)))
            late_rows = jnp.concatenate([dg_mix_pre, db_f_row] + [zero_row] * 6, axis=0)
    grad_x = dx.reshape(x.shape)

    late = _sum_slots(_exchange("gather", late_rows, "gather_late_small_grads")[0], 8, "sum_late_small_grads")
    small_sum = _sum_slots(grad_sums["small"], S_ROWS, "sum_small_grads")
    loss = lax.psum(loss.reshape(()), ("x", "y", "c"))

    sm = small_sum.reshape(depth, S_ROWS, D_MODEL)[::-1]
    g = {
        "mix_pre_g": jnp.stack([late[0], sm[1, 0]]), "sgu_norm_g": sm[:, 1], "mix_post_g": sm[:, 2],
        "ffn_pre_g": sm[:, 3], "ffn_post_g": sm[:, 4], "b_spatial": sm[:, 5].reshape(depth, N_GROUPS, CHUNK),
        "b_forget": jnp.stack([late[1, :N_HEADS], sm[1, 6, :N_HEADS]]),
        "w_spatial": sm[:, 8:].reshape(depth, N_GROUPS, CHUNK, CHUNK),
    }
    g_rest = jnp.stack([_sum_slots(grad_sums[("rest", l)], 592, f"sum_grad_rest_{l}") for l in range(depth)])
    g["w_out"] = g_rest[:, R_OUT:R_OUT + SH_OUT]
    g["w_gate"] = jnp.swapaxes(g_rest[:, R_G:R_G + SH_FF], 1, 2)
    g["w_up"] = jnp.swapaxes(g_rest[:, R_U:R_U + SH_FF], 1, 2)
    g["w_down"] = g_rest[:, R_D:R_D + SH_FF]

    names = ["mix_pre_g", "w_in", "b_forget", "sgu_norm_g", "w_spatial", "b_spatial", "w_out", "mix_post_g",
             "ffn_pre_g", "w_gate", "w_up", "w_down", "ffn_post_g"]
    ws = dict(mix_pre_g=mix_pre_g, w_in=w_in, b_forget=b_forget, sgu_norm_g=sgu_norm_g, w_spatial=w_spatial,
              b_spatial=b_spatial, w_out=w_out, mix_post_g=mix_post_g, ffn_pre_g=ffn_pre_g, w_gate=w_gate, w_up=w_up,
              w_down=w_down, ffn_post_g=ffn_post_g)
    ms = dict(mix_pre_g=m_mix_pre_g, w_in=m_w_in, b_forget=m_b_forget, sgu_norm_g=m_sgu_norm_g, w_spatial=m_w_spatial,
              b_spatial=m_b_spatial, w_out=m_w_out, mix_post_g=m_mix_post_g, ffn_pre_g=m_ffn_pre_g, w_gate=m_w_gate,
              w_up=m_w_up, w_down=m_w_down, ffn_post_g=m_ffn_post_g)
    vs = dict(mix_pre_g=v_mix_pre_g, w_in=v_w_in, b_forget=v_b_forget, sgu_norm_g=v_sgu_norm_g, w_spatial=v_w_spatial,
              b_spatial=v_b_spatial, w_out=v_w_out, mix_post_g=v_mix_post_g, ffn_pre_g=v_ffn_pre_g, w_gate=v_w_gate,
              w_up=v_w_up, w_down=v_w_down, ffn_post_g=v_ffn_post_g)
    updates = {}
    for n in [n for n in names if n != "w_in"] + ["w_in"]:
        if n == "w_in":
            sum_in_1 = _sum_slots(grad_sums[("in", 1)], 304, "sum_grad_in_1")
            done = [u[0] for u in updates.values()] + [sum_in_1, grad_x]
            sum_in_0 = _sum_slots(_windows_wait(*in_flight, done, "wait_grad_in_0"), 304, "sum_grad_in_0")
            g_in = jnp.stack([sum_in_0, sum_in_1])
            g["w_in"] = jnp.swapaxes(lax.dynamic_slice(g_in, (0, me, 0), (depth, SH_IN, D_MODEL)), 1, 2)
        flat = (-1, ws[n].shape[-1])
        updates[n] = _adamw(ws[n].reshape(flat), g[n].reshape(flat), ms[n].reshape(flat), vs[n].reshape(flat),
                            f"adamw_{n}")
    deltas = [updates[n][0].reshape(ws[n].shape) for n in names]
    new_m = [updates[n][1].reshape(ws[n].shape) for n in names]
    new_v = [updates[n][2].reshape(ws[n].shape) for n in names]
    grads = [g[n].reshape(ws[n].shape) for n in names]
    return (loss, grad_x, *grads, *deltas, *new_m, *new_v)
```

```python
import math

import jax
import jax.numpy as jnp
from jax import lax
from jax.experimental import pallas as pl
from jax.experimental.pallas import tpu as pltpu

F32 = jnp.float32
BF16 = jnp.bfloat16

N_DEV = 8
D_MODEL = 1024
N_HEADS = 8
HEAD_DIM = 128
CHUNK = 128
N_GROUPS = 8
D_FF = 2816
IN_WIDTH = 7 * D_MODEL + N_HEADS
IN_PAD = 7680
EPS = 1e-6
GATED = 4 * D_MODEL
F_WIDTH = 512
QKV_WIDTH = 3 * HEAD_DIM
CB_F = GATED // HEAD_DIM
CB_QKV = (GATED + F_WIDTH) // HEAD_DIM
assert GATED + F_WIDTH + N_HEADS * QKV_WIDTH == IN_PAD and (GATED + F_WIDTH) % QKV_WIDTH == 0

ADAM_LR, ADAM_B1, ADAM_B2, ADAM_EPS, ADAM_WD, ADAM_STEP = 0.001, 0.9, 0.999, 1e-08, 0.01, 10

SH_IN = IN_WIDTH // N_DEV
SH_OUT = D_MODEL // N_DEV
SH_FF = D_FF // N_DEV
P_IN = 912
IN_STRIDE = 7 * HEAD_DIM
IN_ROWS = IN_STRIDE * (N_DEV - 1) + P_IN
R_OUT, R_G, R_U, R_D = 0, SH_OUT, SH_OUT + SH_FF, SH_OUT + 2 * SH_FF
P_REST = SH_OUT + 3 * SH_FF
S_ROWS = 136

ATTN_BLOCK = 512
ROW_BLOCK = 512
MM_TM, MM_TN, MM_TK = 1536, 1536, 2048
VMEM_CAP = 56 << 20
NEG = -1e30


def _tile(n, cap, unit=128):
    if n <= cap:
        return n
    best = None
    for t in range(unit, cap + 1, unit):
        if n % t == 0:
            best = t
    assert best is not None, (n, cap)
    return best


def _nbytes(shape, dtype):
    return math.prod(shape) * jnp.dtype(dtype).itemsize


_HBM = pl.BlockSpec(memory_space=pltpu.HBM)
MESH = pl.DeviceIdType.MESH

_N_REMOTE = N_DEV - 1


def _exchange_shapes(kind, x):
    shape = {"gather": (N_DEV,) + x.shape, "scatter": x.shape, "windows": (N_DEV, P_IN) + x.shape[1:]}[kind]
    return [jax.ShapeDtypeStruct(shape, x.dtype)]


def _exchange_sems(kind):
    del kind
    return [pltpu.SemaphoreType.DMA((_N_REMOTE,)), pltpu.SemaphoreType.DMA((_N_REMOTE,)), pltpu.SemaphoreType.DMA]


def _plan(kind, x_ref, outs, sems):
    send_sems, recv_sems, local_sem = sems
    x, y, c = lax.axis_index("x"), lax.axis_index("y"), lax.axis_index("c")

    def remote(src, dst, k, to):
        return pltpu.make_async_remote_copy(src_ref=src, dst_ref=dst, send_sem=send_sems.at[k], recv_sem=recv_sems.at[k],
                                            device_id=to, device_id_type=MESH)

    sibling = (x, y, 1 - c)
    if kind == "gather":
        out, = outs
        slot = lambda px, py, pc: out.at[4 * px + 2 * py + pc]
        chips = [(1 - x, y), (x, 1 - y), (1 - x, 1 - y)]
        local = [pltpu.make_async_copy(x_ref, slot(x, y, c), local_sem)]
        first = [remote(x_ref, slot(x, y, c), 0, sibling)]
        first += [remote(x_ref, slot(x, y, c), 1 + k, (*chip, c)) for k, chip in enumerate(chips)]
        relays = [(remote(x_ref, slot(*chip, c), 1 + k, (*chip, c)), remote(slot(*chip, c), slot(*chip, c), 4 + k, sibling))
                  for k, chip in enumerate(chips)]
        arrivals = [remote(x_ref, slot(x, y, 1 - c), 0, sibling)]
        arrivals += [remote(x_ref, slot(*chip, 1 - c), 4 + k, sibling) for k, chip in enumerate(chips)]
        return local, first, relays, arrivals
    out, = outs
    mine = 4 * x + 2 * y + c
    if kind == "spread":
        src = lambda d: x_ref
    elif kind == "scatter":
        src = lambda d: x_ref.at[d]
    else:
        src = lambda d: x_ref.at[pl.ds(pl.multiple_of(IN_STRIDE * d, 16), P_IN)]
    first, arrivals = [], []
    for j in range(1, N_DEV):
        peer = (1 - x if j & 4 else x, 1 - y if j & 2 else y, 1 - c if j & 1 else c)
        theirs = 4 * peer[0] + 2 * peer[1] + peer[2]
        first.append(remote(src(theirs), out.at[mine], j - 1, peer))
        arrivals.append(remote(src(theirs), out.at[theirs], j - 1, peer))
    local = [] if local_sem is None else [pltpu.make_async_copy(src(mine), out.at[mine], local_sem)]
    return local, first, [], arrivals


def _start(plan):
    local, first, _, _ = plan
    for cp in local + first:
        cp.start()


def _finish(plan):
    local, first, relays, arrivals = plan
    for arrival, onward in relays:
        arrival.wait_recv()
        onward.start()
    for cp in arrivals:
        cp.wait_recv()
    for cp in first + [onward for _, onward in relays]:
        cp.wait_send()
    for cp in local:
        cp.wait()


def _direct_copies(kind, x_ref, land_ref, send_sems, recv_sems):
    _, first, _, arrivals = _plan(kind, x_ref, [land_ref], (send_sems, recv_sems, None))
    return first, arrivals


_SEM = pl.BlockSpec(memory_space=pltpu.SEMAPHORE)
_DATAFLOW = pltpu.SideEffectType.DATAFLOW_SIDE_EFFECTING


def _exchange_start(kind, x, name):
    me = 4 * lax.axis_index("x") + 2 * lax.axis_index("y") + lax.axis_index("c")
    if kind == "spread":
        own, shape = x, (N_DEV,) + x.shape
    elif kind == "scatter":
        own, shape = lax.dynamic_index_in_dim(x, me, 0, keepdims=False), x.shape
    else:
        own, shape = lax.dynamic_slice_in_dim(x, IN_STRIDE * me, P_IN, 0), (N_DEV, P_IN) + x.shape[1:]
    land = lax.dynamic_update_slice_in_dim(lax.empty(shape, x.dtype), own[None], me, 0)

    def body(x_ref, land_ref, send_sems, recv_sems, x_thru, land_thru, token):
        del x_thru, land_thru
        for cp in _direct_copies(kind, x_ref, land_ref, send_sems, recv_sems)[0]:
            cp.start()
        token[...] = jnp.zeros_like(token)

    *handle, token = pl.pallas_call(
        body, name=name,
        out_shape=(pltpu.SemaphoreType.DMA((_N_REMOTE,)), pltpu.SemaphoreType.DMA((_N_REMOTE,)),
                   pltpu.HBM(x.shape, x.dtype), pltpu.HBM(land.shape, land.dtype), jax.ShapeDtypeStruct((8, 128), F32)),
        in_specs=(_HBM, _HBM), out_specs=(_SEM, _SEM, _HBM, _HBM, pl.BlockSpec(memory_space=pltpu.VMEM)),
        input_output_aliases={0: 2, 1: 3}, compiler_params=pltpu.CompilerParams(has_side_effects=_DATAFLOW),
    )(pltpu.with_memory_space_constraint(x, pltpu.HBM), pltpu.with_memory_space_constraint(land, pltpu.HBM))
    return (kind, *handle), token


def _exchange_wait(handle, after, name):
    kind, send_sems, recv_sems, x_thru, land_thru = handle

    def body(x_ref, land_ref, send_sems, recv_sems, *rest):
        first, arrivals = _direct_copies(kind, x_ref, land_ref, send_sems, recv_sems)
        for cp in first:
            cp.wait_send()
        for cp in arrivals:
            cp.wait_recv()

    return pl.pallas_call(
        body, name=name,
        out_shape=(pltpu.HBM(x_thru.shape, x_thru.dtype), pltpu.HBM(land_thru.shape, land_thru.dtype)),
        in_specs=(_HBM, _HBM, _SEM, _SEM) + (pl.BlockSpec(memory_space=pl.ANY),) * len(after), out_specs=(_HBM, _HBM),
        input_output_aliases={0: 0, 1: 1}, compiler_params=pltpu.CompilerParams(has_side_effects=_DATAFLOW),
    )(x_thru, land_thru, send_sems, recv_sems, *after)[1]


def _exchange(kind, x, name):
    n_out = len(_exchange_shapes(kind, x))

    def body(x_ref, *refs):
        plan = _plan(kind, x_ref, refs[:n_out], refs[n_out:])
        _start(plan)
        _finish(plan)

    return pl.pallas_call(body, name=name, out_shape=_exchange_shapes(kind, x), in_specs=[_HBM],
                          out_specs=[_HBM] * n_out, scratch_shapes=_exchange_sems(kind))(x)


def _call(body, name, grid, in_specs, out_specs, out_shape, args, scratch=(), sem=None, nbytes=0, aliases=None,
          comm=(), deps=()):
    in_specs, out_specs, out_shape, args, scratch = (list(in_specs), list(out_specs), list(out_shape), list(args),
                                                     list(scratch))
    if deps:
        n_real, n_deps, unordered = len(args), len(deps), body

        def body(*refs):
            unordered(*refs[:n_real], *refs[n_real + n_deps:])

        in_specs += [pl.BlockSpec(memory_space=pl.ANY)] * n_deps
        args += list(deps)
    if comm:
        n_in, n_out, n_scr, n_ops = len(args), len(out_shape), len(scratch), len(comm)
        kinds = [kind for kind, _ in comm]
        shapes = [_exchange_shapes(kind, x) for kind, x in comm]
        inner = body

        def body(*refs):
            ins, refs = refs[:n_in], refs[n_in:]
            cins, refs = refs[:n_ops], refs[n_ops:]
            outs, refs = refs[:n_out], refs[n_out:]
            couts = []
            for sh in shapes:
                couts.append(refs[:len(sh)])
                refs = refs[len(sh):]
            own_scratch, sems = refs[:n_scr], refs[n_scr:]
            first = last = None
            for axis, size in enumerate(grid):
                at_start, at_end = pl.program_id(axis) == 0, pl.program_id(axis) == size - 1
                first = at_start if first is None else first & at_start
                last = at_end if last is None else last & at_end
            plans = [_plan(kinds[o], cins[o], couts[o], sems[3 * o:3 * o + 3]) for o in range(n_ops)]

            @pl.when(first)
            def _():
                for p in plans:
                    _start(p)

            inner(*ins, *outs, *own_scratch)

            @pl.when(last)
            def _():
                for p in plans:
                    _finish(p)

        in_specs += [_HBM] * n_ops
        args += [x for _, x in comm]
        for kind, sh in zip(kinds, shapes):
            out_shape += sh
            out_specs += [_HBM] * len(sh)
            scratch += _exchange_sems(kind)
        sem = ("arbitrary",) * len(grid)
    limit = int(min(max(2 * nbytes + (8 << 20), 32 << 20), VMEM_CAP))
    return pl.pallas_call(
        body, name=name, grid=grid, in_specs=in_specs, out_specs=out_specs, out_shape=out_shape,
        scratch_shapes=scratch, input_output_aliases=aliases or {},
        compiler_params=pltpu.CompilerParams(dimension_semantics=sem, vmem_limit_bytes=limit),
    )(*args)


def _sum_slots(x, tr, name):
    n, r, c = x.shape

    def body(x_ref, o_ref):
        acc = x_ref[0].astype(F32)
        for d in range(1, n):
            acc = acc + x_ref[d].astype(F32)
        o_ref[...] = acc

    return _call(body, name, (r // tr,), [pl.BlockSpec((n, tr, c), lambda i: (0, i, 0))],
                 [pl.BlockSpec((tr, c), lambda i: (i, 0))], [jax.ShapeDtypeStruct((r, c), F32)], [x],
                 sem=("parallel",), nbytes=_nbytes((n, tr, c), x.dtype) + _nbytes((tr, c), F32))[0]


_DIMS = {"nn": (((1,), (0,)), ((), ())), "nt": (((1,), (1,)), ((), ())), "tn": (((0,), (0,)), ((), ()))}


def _matmul(a, b, mode, out_dtype, name, comm=(), deps=()):
    if mode == "nn":
        (m, k), (k2, n) = a.shape, b.shape
    elif mode == "nt":
        (m, k), (n, k2) = a.shape, b.shape
    else:
        (k, m), (k2, n) = a.shape, b.shape
    assert k == k2, (a.shape, b.shape, mode)
    tm, tn, tk = _tile(m, MM_TM), _tile(n, MM_TN), _tile(k, MM_TK)
    nk = k // tk
    dims = _DIMS[mode]
    a_spec = {"nn": pl.BlockSpec((tm, tk), lambda i, j, kk: (i, kk)),
              "nt": pl.BlockSpec((tm, tk), lambda i, j, kk: (i, kk)),
              "tn": pl.BlockSpec((tk, tm), lambda i, j, kk: (kk, i))}[mode]
    b_spec = {"nn": pl.BlockSpec((tk, tn), lambda i, j, kk: (kk, j)),
              "nt": pl.BlockSpec((tn, tk), lambda i, j, kk: (j, kk)),
              "tn": pl.BlockSpec((tk, tn), lambda i, j, kk: (kk, j))}[mode]

    def partial_product(a_ref, b_ref):
        return lax.dot_general(a_ref[...].astype(BF16), b_ref[...].astype(BF16), dims, preferred_element_type=F32)

    if nk == 1:
        def body(a_ref, b_ref, o_ref):
            o_ref[...] = partial_product(a_ref, b_ref).astype(o_ref.dtype)
        scratch = []
    else:
        def body(a_ref, b_ref, o_ref, acc_ref):
            kk = pl.program_id(2)

            @pl.when(kk == 0)
            def _():
                acc_ref[...] = jnp.zeros_like(acc_ref)

            acc_ref[...] += partial_product(a_ref, b_ref)

            @pl.when(kk == nk - 1)
            def _():
                o_ref[...] = acc_ref[...].astype(o_ref.dtype)
        scratch = [pltpu.VMEM((tm, tn), F32)]

    nbytes = (_nbytes((tm, tk), a.dtype) + _nbytes((tk, tn), b.dtype) + _nbytes((tm, tn), out_dtype)
              + _nbytes((tm, tn), F32))
    res = _call(body, name, (m // tm, n // tn, nk), [a_spec, b_spec],
                [pl.BlockSpec((tm, tn), lambda i, j, kk: (i, j))], [jax.ShapeDtypeStruct((m, n), out_dtype)], [a, b],
                scratch=scratch, sem=("parallel", "parallel", "arbitrary"), nbytes=nbytes, comm=comm, deps=deps)
    return res if comm else res[0]


def _rms(x):
    return lax.rsqrt(jnp.mean(x * x, axis=-1, keepdims=True) + EPS)


def _rms_bwd(dz, a, g):
    r = _rms(a)
    dzg = dz * g
    da = r * dzg - a * (r * r * r) * jnp.mean(dzg * a, axis=-1, keepdims=True)
    return da, dz * (a * r)


def _row_spec(tr, width):
    return pl.BlockSpec((tr, width), lambda i: (i, 0))


def _vec_spec(width):
    return pl.BlockSpec((1, width), lambda i: (0, 0))


def _rms_fwd(x, g, name):
    t, d = x.shape
    tr = min(ROW_BLOCK, t)

    def body(x_ref, g_ref, h_ref):
        xv = x_ref[...]
        h_ref[...] = ((xv * _rms(xv)) * g_ref[...]).astype(BF16)

    return _call(body, name, (t // tr,), [_row_spec(tr, d), _vec_spec(d)], [_row_spec(tr, d)],
                 [jax.ShapeDtypeStruct((t, d), BF16)], [x, g], sem=("parallel",), nbytes=3 * _nbytes((tr, d), F32))[0]


def _post_norm_fwd(x, o, g_post, g_next, name):
    t, d = x.shape
    tr = min(ROW_BLOCK, t)

    def body(x_ref, o_ref, gp_ref, gn_ref, xn_ref, h_ref):
        ov = o_ref[...]
        xn = x_ref[...] + (ov * _rms(ov)) * gp_ref[...]
        xn_ref[...] = xn
        h_ref[...] = ((xn * _rms(xn)) * gn_ref[...]).astype(BF16)

    return _call(body, name, (t // tr,), [_row_spec(tr, d), _row_spec(tr, d), _vec_spec(d), _vec_spec(d)],
                 [_row_spec(tr, d), _row_spec(tr, d)],
                 [jax.ShapeDtypeStruct((t, d), F32), jax.ShapeDtypeStruct((t, d), BF16)], [x, o, g_post, g_next],
                 sem=("parallel",), nbytes=5 * _nbytes((tr, d), F32))


def _post_norm_loss(x, o, g_post, target, name):
    t, d = x.shape
    tr = min(ROW_BLOCK, t)

    def body(x_ref, o_ref, gp_ref, t_ref, dy_ref, loss_ref):
        ov = o_ref[...]
        err = x_ref[...] + (ov * _rms(ov)) * gp_ref[...] - t_ref[...]
        dy_ref[...] = err / d
        part = 0.5 * jnp.sum(jnp.mean(err * err, axis=-1, keepdims=True), axis=0, keepdims=True)

        @pl.when(pl.program_id(0) == 0)
        def _():
            loss_ref[...] = jnp.zeros_like(loss_ref)

        loss_ref[...] += part

    return _call(body, name, (t // tr,), [_row_spec(tr, d), _row_spec(tr, d), _vec_spec(d), _row_spec(tr, d)],
                 [_row_spec(tr, d), pl.BlockSpec((1, 1), lambda i: (0, 0))],
                 [jax.ShapeDtypeStruct((t, d), F32), jax.ShapeDtypeStruct((1, 1), F32)], [x, o, g_post, target],
                 sem=("arbitrary",), nbytes=5 * _nbytes((tr, d), F32))


def _norm_bwd(dskip, name, pre=None, post=None):
    t, d = dskip.shape
    tr = min(ROW_BLOCK, t)
    n_in = 1 + (3 if pre else 0) + (2 if post else 0)

    def body(*refs):
        ins, outs = list(refs[:n_in]), list(refs[n_in:])
        first = pl.program_id(0) == 0
        dx = ins.pop(0)[...]
        if pre:
            dh_ref, xin_ref, gpre_ref = ins.pop(0), ins.pop(0), ins.pop(0)
            dxin, dg_rows = _rms_bwd(dh_ref[...].astype(F32), xin_ref[...], gpre_ref[...])
            dx = dx + dxin
            dx_ref, dgpre_ref = outs.pop(0), outs.pop(0)
            dx_ref[...] = dx

            @pl.when(first)
            def _():
                dgpre_ref[...] = jnp.zeros_like(dgpre_ref)

            dgpre_ref[...] += jnp.sum(dg_rows, axis=0, keepdims=True)
        if post:
            a_ref, gpost_ref = ins.pop(0), ins.pop(0)
            da, dg_rows = _rms_bwd(dx, a_ref[...], gpost_ref[...])
            da_ref, dgpost_ref = outs.pop(0), outs.pop(0)
            da_ref[...] = da.astype(BF16)

            @pl.when(first)
            def _():
                dgpost_ref[...] = jnp.zeros_like(dgpost_ref)

            dgpost_ref[...] += jnp.sum(dg_rows, axis=0, keepdims=True)

    args, in_specs, out_shape, out_specs = [dskip], [_row_spec(tr, d)], [], []
    if pre:
        args += list(pre)
        in_specs += [_row_spec(tr, d), _row_spec(tr, d), _vec_spec(d)]
        out_shape += [jax.ShapeDtypeStruct((t, d), F32), jax.ShapeDtypeStruct((1, d), F32)]
        out_specs += [_row_spec(tr, d), _vec_spec(d)]
    if post:
        args += list(post)
        in_specs += [_row_spec(tr, d), _vec_spec(d)]
        out_shape += [jax.ShapeDtypeStruct((t, d), BF16), jax.ShapeDtypeStruct((1, d), F32)]
        out_specs += [_row_spec(tr, d), _vec_spec(d)]
    return _call(body, name, (t // tr,), in_specs, out_specs, out_shape, args, sem=("arbitrary",),
                 nbytes=7 * _nbytes((tr, d), F32))


def _sigmoid(x):
    return 1.0 / (1.0 + jnp.exp(-x))


def _swiglu_fwd(gu, name):
    t = gu.shape[0]
    tr = min(ROW_BLOCK, t)

    def body(gu_ref, a_ref):
        gate, up = gu_ref[:, :D_FF], gu_ref[:, D_FF:]
        a_ref[...] = ((gate * _sigmoid(gate)) * up).astype(BF16)

    return _call(body, name, (t // tr,), [_row_spec(tr, 2 * D_FF)], [_row_spec(tr, D_FF)],
                 [jax.ShapeDtypeStruct((t, D_FF), BF16)], [gu], sem=("parallel",),
                 nbytes=2 * _nbytes((tr, 2 * D_FF), F32))[0]


def _swiglu_bwd(gu, dact, name):
    t = gu.shape[0]
    tr = min(ROW_BLOCK, t)

    def body(gu_ref, da_ref, o_ref):
        gate, up = gu_ref[:, :D_FF], gu_ref[:, D_FF:]
        da = da_ref[...]
        s = _sigmoid(gate)
        o_ref[:, :D_FF] = (da * up * (s * (1.0 + gate * (1.0 - s)))).astype(BF16)
        o_ref[:, D_FF:] = (da * (gate * s)).astype(BF16)

    return _call(body, name, (t // tr,), [_row_spec(tr, 2 * D_FF), _row_spec(tr, D_FF)], [_row_spec(tr, 2 * D_FF)],
                 [jax.ShapeDtypeStruct((t, 2 * D_FF), BF16)], [gu, dact], sem=("parallel",),
                 nbytes=3 * _nbytes((tr, 2 * D_FF), F32))[0]


def _split3(x):
    hi = x.astype(BF16)
    r1 = x - hi.astype(F32)
    mid = r1.astype(BF16)
    lo = (r1 - mid.astype(F32)).astype(BF16)
    return hi, mid, lo


def _dot_exact(mat01, x):
    hi, mid, lo = _split3(x)
    out = jnp.dot(mat01, hi, preferred_element_type=F32)
    out = out + jnp.dot(mat01, mid, preferred_element_type=F32)
    return out + jnp.dot(mat01, lo, preferred_element_type=F32)


def _dot_exact_rhs(x, mat01):
    hi, mid, lo = _split3(x)
    out = jnp.dot(hi, mat01, preferred_element_type=F32)
    out = out + jnp.dot(mid, mat01, preferred_element_type=F32)
    return out + jnp.dot(lo, mat01, preferred_element_type=F32)


def _tri(lower):
    r = lax.broadcasted_iota(jnp.int32, (CHUNK, CHUNK), 0)
    c = lax.broadcasted_iota(jnp.int32, (CHUNK, CHUNK), 1)
    return jnp.where(r >= c if lower else r <= c, 1.0, 0.0).astype(BF16)


def _log_sigmoid(z):
    return jnp.minimum(z, 0.0) - jnp.log(1.0 + jnp.exp(-jnp.abs(z)))


def _forget_cumsum(proj, b_pad, name):
    t = proj.shape[0]
    nb = t // CHUNK

    def body(f_ref, b_ref, c_ref):
        tri = _tri(True)
        b = b_ref[...]

        def blk(i, carry):
            rows = pl.ds(pl.multiple_of(i * CHUNK, CHUNK), CHUNK)
            cs = _dot_exact(tri, _log_sigmoid(f_ref[rows, :] + b)) + carry
            c_ref[rows, :] = cs
            return cs[CHUNK - 1:CHUNK, :]

        lax.fori_loop(0, nb, blk, jnp.zeros((1, HEAD_DIM), F32))

    return _call(body, name, (1,),
                 [pl.BlockSpec((t, HEAD_DIM), lambda i: (0, CB_F)), pl.BlockSpec((1, HEAD_DIM), lambda i: (0, 0))],
                 [pl.BlockSpec((t, HEAD_DIM), lambda i: (0, 0))], [jax.ShapeDtypeStruct((t, HEAD_DIM), F32)],
                 [proj, b_pad], sem=("arbitrary",), nbytes=2 * _nbytes((t, HEAD_DIM), F32))[0]


def _forget_bwd(proj, b_pad, dc_key, dc_query, dproj, name):
    t = proj.shape[0]
    tr = min(ROW_BLOCK, t)
    nb = t // tr
    rev = lambda i: nb - 1 - i

    def body(f_ref, b_ref, dck_ref, dcq_ref, dproj_in, df_ref, db_ref, run_ref):
        del dproj_in

        @pl.when(pl.program_id(0) == 0)
        def _():
            run_ref[...] = jnp.zeros_like(run_ref)
            db_ref[...] = jnp.zeros_like(db_ref)

        tri = _tri(False)
        b = b_ref[...]
        lane = lax.broadcasted_iota(jnp.int32, (CHUNK, HEAD_DIM), 1)
        df_ref[...] = jnp.zeros_like(df_ref)
        for n in reversed(range(tr // CHUNK)):
            rows = slice(n * CHUNK, (n + 1) * CHUNK)
            dc = dck_ref[rows, :]
            for h in range(N_HEADS):
                dc = dc + jnp.where(lane == h, dcq_ref[rows, h * HEAD_DIM:(h + 1) * HEAD_DIM], 0.0)
            dlogf = _dot_exact(tri, dc) + run_ref[...]
            run_ref[...] = dlogf[0:1, :]
            z = f_ref[rows, :] + b
            e = jnp.exp(-jnp.abs(z))
            sig_neg = jnp.where(z >= 0.0, e, 1.0) / (1.0 + e)
            df = jnp.where(lane < N_HEADS, dlogf * sig_neg, 0.0)
            df_ref[rows, :HEAD_DIM] = df.astype(BF16)
            db_ref[...] += jnp.sum(df, axis=0, keepdims=True)

    return _call(body, name, (nb,),
                 [pl.BlockSpec((tr, HEAD_DIM), lambda i: (rev(i), CB_F)), pl.BlockSpec((1, HEAD_DIM), lambda i: (0, 0)),
                  pl.BlockSpec((tr, HEAD_DIM), lambda i: (rev(i), 0)), pl.BlockSpec((tr, D_MODEL), lambda i: (rev(i), 0)),
                  pl.BlockSpec(memory_space=pl.ANY)],
                 [pl.BlockSpec((tr, F_WIDTH), lambda i: (rev(i), GATED // F_WIDTH)),
                  pl.BlockSpec((1, HEAD_DIM), lambda i: (0, 0))],
                 [jax.ShapeDtypeStruct(dproj.shape, dproj.dtype), jax.ShapeDtypeStruct((1, HEAD_DIM), F32)],
                 [proj, b_pad, dc_key, dc_query, dproj], scratch=[pltpu.VMEM((1, HEAD_DIM), F32)],
                 sem=("arbitrary",), nbytes=4 * _nbytes((tr, D_MODEL), F32), aliases={4: 0})


def _causal(block):
    r = lax.broadcasted_iota(jnp.int32, (block, block), 0)
    c = lax.broadcasted_iota(jnp.int32, (block, block), 1)
    return c <= r


def _lanes(x, width):
    return jnp.concatenate([x] * (width // HEAD_DIM), axis=1)


def _attn_fwd(proj, c_pad, c_row, name, comm=()):
    t = proj.shape[0]
    blk = min(ATTN_BLOCK, t)
    nq = t // blk
    scale = HEAD_DIM ** -0.5

    def body(q_ref, k_ref, v_ref, cp_ref, cr_ref, y_ref, e_ref):
        h, i = pl.program_id(0), pl.program_id(1)
        q = (q_ref[...] * scale).astype(BF16)
        lane = lax.broadcasted_iota(jnp.int32, (blk, HEAD_DIM), 1)
        ci = jnp.sum(jnp.where(lane == h, cp_ref[...], 0.0), axis=1, keepdims=True)

        def step(j, carry, diagonal):
            m, l, acc = carry
            off = pl.multiple_of(j * blk, blk)
            k = k_ref[pl.ds(off, blk), :].astype(BF16)
            v = v_ref[pl.ds(off, blk), :].astype(BF16)
            s = lax.dot_general(q, k, _DIMS["nt"], preferred_element_type=F32)
            s = s + ci - cr_ref[:, pl.ds(off, blk)]
            if diagonal:
                s = jnp.where(_causal(blk), s, NEG)
            m_new = jnp.maximum(m, jnp.max(s, axis=1, keepdims=True))
            alpha = jnp.exp(m - m_new)
            p = jnp.exp(s - m_new)
            l = alpha * l + jnp.sum(p, axis=1, keepdims=True)
            acc = alpha * acc + jnp.dot(p.astype(BF16), v, preferred_element_type=F32)
            return m_new, l, acc

        init = (jnp.full((blk, 1), NEG, F32), jnp.zeros((blk, 1), F32), jnp.zeros((blk, HEAD_DIM), F32))
        carry = lax.fori_loop(0, i, lambda j, c: step(j, c, False), init)
        m, l, acc = step(i, carry, True)
        y_ref[...] = acc / l
        e_ref[...] = jnp.broadcast_to(ci - (m + jnp.log(l)), (blk, HEAD_DIM))

    nbytes = 2 * _nbytes((t, HEAD_DIM), F32) + 4 * _nbytes((blk, HEAD_DIM), F32) + 4 * _nbytes((blk, blk), F32)
    head_block = pl.BlockSpec((blk, HEAD_DIM), lambda h, i: (i, h))
    return _call(body, name, (N_HEADS, nq),
                 [pl.BlockSpec((blk, HEAD_DIM), lambda h, i: (i, CB_QKV + 3 * h)),
                  pl.BlockSpec((t, HEAD_DIM), lambda h, i: (0, CB_QKV + 3 * h + 1)),
                  pl.BlockSpec((t, HEAD_DIM), lambda h, i: (0, CB_QKV + 3 * h + 2)),
                  pl.BlockSpec((blk, HEAD_DIM), lambda h, i: (i, 0)),
                  pl.BlockSpec((None, 1, t), lambda h, i: (h, 0, 0))],
                 [head_block, head_block],
                 [jax.ShapeDtypeStruct((t, D_MODEL), F32), jax.ShapeDtypeStruct((t, D_MODEL), F32)],
                 [proj, proj, proj, c_pad, c_row], sem=("parallel", "arbitrary"), nbytes=nbytes, comm=comm)


def _attn_bwd(proj, dy, e_b, delta_b, c_row, dproj, name, comm=()):
    t = proj.shape[0]
    blk = min(ATTN_BLOCK, t)
    nb = t // blk
    scale = HEAD_DIM ** -0.5

    def body(q_ref, k_ref, v_ref, dy_ref, e_ref, dl_ref, cr_ref, dproj_in, dqkv_ref, dc_ref, dr_ref, dq_acc):
        del dproj_in
        j = pl.program_id(1)

        @pl.when(j == 0)
        def _():
            dq_acc[...] = jnp.zeros_like(dq_acc)
            dr_ref[...] = jnp.zeros_like(dr_ref)

        k = k_ref[...].astype(BF16)
        v = v_ref[...].astype(BF16)
        crow = cr_ref[...]

        def step(i, carry, diagonal):
            dk, dv, dcs = carry
            rows = pl.ds(pl.multiple_of(i * blk, blk), blk)
            q = (q_ref[rows, :] * scale).astype(BF16)
            dyi = dy_ref[rows, :].astype(BF16)
            s = lax.dot_general(q, k, _DIMS["nt"], preferred_element_type=F32) + _lanes(e_ref[rows, :], blk) - crow
            if diagonal:
                s = jnp.where(_causal(blk), s, NEG)
            p = jnp.exp(s)
            dv = dv + lax.dot_general(p.astype(BF16), dyi, _DIMS["tn"], preferred_element_type=F32)
            dp = lax.dot_general(dyi, v, _DIMS["nt"], preferred_element_type=F32)
            ds = p * (dp - _lanes(dl_ref[rows, :], blk))
            dsb = ds.astype(BF16)
            dk = dk + lax.dot_general(dsb, q, _DIMS["tn"], preferred_element_type=F32)
            dq_acc[rows, :] += jnp.dot(dsb, k, preferred_element_type=F32) * scale
            dr_ref[rows, :] += jnp.sum(ds, axis=1, keepdims=True)
            return dk, dv, dcs + jnp.sum(ds, axis=0, keepdims=True)

        zeros = jnp.zeros((blk, HEAD_DIM), F32)
        carry = step(j, (zeros, zeros, jnp.zeros((1, blk), F32)), True)
        dk, dv, dcs = lax.fori_loop(j + 1, nb, lambda i, c: step(i, c, False), carry)
        mine = pl.ds(pl.multiple_of(j * blk, blk), blk)
        dqkv_ref[:, :HEAD_DIM] = dq_acc[mine, :].astype(BF16)
        dqkv_ref[:, HEAD_DIM:2 * HEAD_DIM] = dk.astype(BF16)
        dqkv_ref[:, 2 * HEAD_DIM:] = dv.astype(BF16)
        dc_ref[...] = -dcs

    full = lambda cb: pl.BlockSpec((t, HEAD_DIM), lambda h, j: (0, cb(h)))
    head = lambda h: h
    nbytes = (6 * _nbytes((t, HEAD_DIM), F32) + 4 * _nbytes((blk, HEAD_DIM), F32) + 5 * _nbytes((blk, blk), F32))
    return _call(body, name, (N_HEADS, nb),
                 [full(lambda h: CB_QKV + 3 * h),
                  pl.BlockSpec((blk, HEAD_DIM), lambda h, j: (j, CB_QKV + 3 * h + 1)),
                  pl.BlockSpec((blk, HEAD_DIM), lambda h, j: (j, CB_QKV + 3 * h + 2)),
                  full(head), full(head), full(head),
                  pl.BlockSpec((None, 1, blk), lambda h, j: (h, 0, j)),
                  pl.BlockSpec(memory_space=pl.ANY)],
                 [pl.BlockSpec((blk, QKV_WIDTH), lambda h, j: (j, (GATED + F_WIDTH) // QKV_WIDTH + h)),
                  pl.BlockSpec((None, 1, blk), lambda h, j: (h, 0, j)),
                  full(head)],
                 [jax.ShapeDtypeStruct(dproj.shape, dproj.dtype), jax.ShapeDtypeStruct((N_HEADS, 1, t), F32),
                  jax.ShapeDtypeStruct((t, D_MODEL), F32)],
                 [proj, proj, proj, dy, e_b, delta_b, c_row, dproj], scratch=[pltpu.VMEM((t, HEAD_DIM), F32)],
                 sem=("parallel", "arbitrary"), nbytes=nbytes, aliases={7: 0}, comm=comm)


def _chunks(t, blk):
    return [slice(r * blk, (r + 1) * blk) for r in range(t // blk)]


def _lane_of_head(x, h):
    lane = lax.broadcasted_iota(jnp.int32, x.shape, 1)
    return jnp.broadcast_to(jnp.sum(jnp.where(lane == h, x, 0.0), axis=1, keepdims=True), x.shape)


def _causal_t(block):
    r = lax.broadcasted_iota(jnp.int32, (block, block), 0)
    c = lax.broadcasted_iota(jnp.int32, (block, block), 1)
    return r <= c


def _attn_fwd_t(proj, c_pad, c_row, name, comm=()):
    t = proj.shape[0]
    blk = min(ATTN_BLOCK, t)
    nq = t // blk
    scale = HEAD_DIM ** -0.5

    def body(q_ref, k_ref, v_ref, cp_ref, cr_ref, y_ref, e_ref, kb_s, vt_s, cb_s):
        h, i = pl.program_id(0), pl.program_id(1)

        @pl.when(i == 0)
        def _():
            for rows in _chunks(t, blk):
                kb_s[rows, :] = k_ref[rows, :].astype(BF16)
                vt_s[:, rows] = v_ref[rows, :].T.astype(BF16)
                cb_s[rows, :] = _lane_of_head(cp_ref[rows, :], h)

        q = (q_ref[...] * scale).astype(BF16)
        ci = cr_ref[:, pl.ds(pl.multiple_of(i * blk, blk), blk)]

        def step(j, carry, diagonal):
            m, l, acc = carry
            rows = pl.ds(pl.multiple_of(j * blk, blk), blk)
            s = lax.dot_general(kb_s[rows, :], q, _DIMS["nt"], preferred_element_type=F32)
            s = s + ci - _lanes(cb_s[rows, :], blk)
            if diagonal:
                s = jnp.where(_causal_t(blk), s, NEG)
            m_new = jnp.maximum(m, jnp.max(s, axis=0, keepdims=True))
            alpha = jnp.exp(m - m_new)
            p = jnp.exp(s - m_new)
            l = alpha * l + jnp.sum(p, axis=0, keepdims=True)
            acc = alpha * acc + jnp.dot(vt_s[:, rows], p.astype(BF16), preferred_element_type=F32)
            return m_new, l, acc

        init = (jnp.full((1, blk), NEG, F32), jnp.zeros((1, blk), F32), jnp.zeros((HEAD_DIM, blk), F32))
        carry = lax.fori_loop(0, i, lambda j, c: step(j, c, False), init)
        m, l, acc = step(i, carry, True)
        y_ref[...] = (acc / l).T
        e_ref[...] = ci - (m + jnp.log(l))

    nbytes = 3 * _nbytes((t, HEAD_DIM), F32) + 4 * _nbytes((blk, HEAD_DIM), F32) + 4 * _nbytes((blk, blk), F32)
    return _call(body, name, (N_HEADS, nq),
                 [pl.BlockSpec((blk, HEAD_DIM), lambda h, i: (i, CB_QKV + 3 * h)),
                  pl.BlockSpec((t, HEAD_DIM), lambda h, i: (0, CB_QKV + 3 * h + 1)),
                  pl.BlockSpec((t, HEAD_DIM), lambda h, i: (0, CB_QKV + 3 * h + 2)),
                  pl.BlockSpec((t, HEAD_DIM), lambda h, i: (0, 0)),
                  pl.BlockSpec((None, 1, t), lambda h, i: (h, 0, 0))],
                 [pl.BlockSpec((blk, HEAD_DIM), lambda h, i: (i, h)),
                  pl.BlockSpec((None, 1, blk), lambda h, i: (h, 0, i))],
                 [jax.ShapeDtypeStruct((t, D_MODEL), F32), jax.ShapeDtypeStruct((N_HEADS, 1, t), F32)],
                 [proj, proj, proj, c_pad, c_row],
                 scratch=[pltpu.VMEM((t, HEAD_DIM), BF16), pltpu.VMEM((HEAD_DIM, t), BF16),
                          pltpu.VMEM((t, HEAD_DIM), F32)],
                 sem=("parallel", "arbitrary"), nbytes=nbytes, comm=comm)


def _attn_bwd_t(proj, dy, e_row, delta_b, c_pad, dproj, name, deps=()):
    t = proj.shape[0]
    blk = min(ATTN_BLOCK, t)
    nb = t // blk
    scale = HEAD_DIM ** -0.5

    def body(q_ref, k_ref, v_ref, dy_ref, e_ref, dl_ref, cp_ref, dproj_in, dqkv_ref, dck_ref, dcq_ref,
             qb_s, qt_s, dyb_s, dyt_s, dl_s, dqt_acc):
        del dproj_in
        h, j = pl.program_id(0), pl.program_id(1)

        @pl.when(j == 0)
        def _():
            for rows in _chunks(t, blk):
                qs = q_ref[rows, :] * scale
                qb_s[rows, :] = qs.astype(BF16)
                qt_s[:, rows] = qs.T.astype(BF16)
                dyr = dy_ref[rows, :]
                dyb_s[rows, :] = dyr.astype(BF16)
                dyt_s[:, rows] = dyr.T.astype(BF16)
                dl_s[:, rows] = dl_ref[rows, :].T[0:1, :]
            dqt_acc[...] = jnp.zeros_like(dqt_acc)
            dcq_ref[...] = jnp.zeros_like(dcq_ref)

        kf = k_ref[...]
        kb, kt = kf.astype(BF16), kf.T.astype(BF16)
        vb = v_ref[...].astype(BF16)
        cj = _lanes(_lane_of_head(cp_ref[...], h), blk)

        def step(i, carry, diagonal):
            dkt, dvt, dkey = carry
            cols = pl.ds(pl.multiple_of(i * blk, blk), blk)
            qi, dyi = qb_s[cols, :], dyb_s[cols, :]
            s = lax.dot_general(kb, qi, _DIMS["nt"], preferred_element_type=F32) + e_ref[:, cols] - cj
            if diagonal:
                s = jnp.where(_causal_t(blk), s, NEG)
            p = jnp.exp(s)
            dp = lax.dot_general(vb, dyi, _DIMS["nt"], preferred_element_type=F32)
            ds = p * (dp - dl_s[:, cols])
            pb, dsb = p.astype(BF16), ds.astype(BF16)
            dvt = dvt + lax.dot_general(dyt_s[:, cols], pb, _DIMS["nt"], preferred_element_type=F32)
            dkt = dkt + lax.dot_general(qt_s[:, cols], dsb, _DIMS["nt"], preferred_element_type=F32)
            dqt_acc[:, cols] += jnp.dot(kt, dsb, preferred_element_type=F32) * scale
            dcq_ref[:, cols] += jnp.sum(ds, axis=0, keepdims=True)
            part = ds[:, :HEAD_DIM]
            for g in range(1, blk // HEAD_DIM):
                part = part + ds[:, g * HEAD_DIM:(g + 1) * HEAD_DIM]
            return dkt, dvt, dkey + part

        zeros = jnp.zeros((HEAD_DIM, blk), F32)
        carry = step(j, (zeros, zeros, jnp.zeros((blk, HEAD_DIM), F32)), True)
        dkt, dvt, dkey = lax.fori_loop(j + 1, nb, lambda i, c: step(i, c, False), carry)
        mine = pl.ds(pl.multiple_of(j * blk, blk), blk)
        dqkv_ref[:, :HEAD_DIM] = dqt_acc[:, mine].T.astype(BF16)
        dqkv_ref[:, HEAD_DIM:2 * HEAD_DIM] = dkt.T.astype(BF16)
        dqkv_ref[:, 2 * HEAD_DIM:] = dvt.T.astype(BF16)
        dck_ref[...] = -jnp.sum(dkey.T, axis=0, keepdims=True)

    full = lambda cb: pl.BlockSpec((t, HEAD_DIM), lambda h, j: (0, cb(h)))
    head = lambda h: h
    row = pl.BlockSpec((None, 1, t), lambda h, j: (h, 0, 0))
    nbytes = (8 * _nbytes((t, HEAD_DIM), F32) + 4 * _nbytes((blk, HEAD_DIM), F32) + 6 * _nbytes((blk, blk), F32))
    return _call(body, name, (N_HEADS, nb),
                 [full(lambda h: CB_QKV + 3 * h),
                  pl.BlockSpec((blk, HEAD_DIM), lambda h, j: (j, CB_QKV + 3 * h + 1)),
                  pl.BlockSpec((blk, HEAD_DIM), lambda h, j: (j, CB_QKV + 3 * h + 2)),
                  full(head), row, full(head),
                  pl.BlockSpec((blk, HEAD_DIM), lambda h, j: (j, 0)),
                  pl.BlockSpec(memory_space=pl.ANY)],
                 [pl.BlockSpec((blk, QKV_WIDTH), lambda h, j: (j, (GATED + F_WIDTH) // QKV_WIDTH + h)),
                  pl.BlockSpec((None, 1, blk), lambda h, j: (h, 0, j)),
                  row],
                 [jax.ShapeDtypeStruct(dproj.shape, dproj.dtype), jax.ShapeDtypeStruct((N_HEADS, 1, t), F32),
                  jax.ShapeDtypeStruct((N_HEADS, 1, t), F32)],
                 [proj, proj, proj, dy, e_row, delta_b, c_pad, dproj],
                 scratch=[pltpu.VMEM((t, HEAD_DIM), BF16), pltpu.VMEM((HEAD_DIM, t), BF16),
                          pltpu.VMEM((t, HEAD_DIM), BF16), pltpu.VMEM((HEAD_DIM, t), BF16),
                          pltpu.VMEM((1, t), F32), pltpu.VMEM((HEAD_DIM, t), F32)],
                 sem=("parallel", "arbitrary"), nbytes=nbytes, aliases={7: 0}, deps=deps)


def _forget_bwd_t(proj, b_pad, dc, dproj, name):
    t = proj.shape[0]
    tr = min(ROW_BLOCK, t)
    nb = t // tr
    rev = lambda i: nb - 1 - i

    def body(f_ref, b_ref, dc_ref, dproj_in, df_ref, db_ref, run_ref):
        del dproj_in

        @pl.when(pl.program_id(0) == 0)
        def _():
            run_ref[...] = jnp.zeros_like(run_ref)
            db_ref[...] = jnp.zeros_like(db_ref)

        tri = _tri(False)
        b = b_ref[...]
        lane = lax.broadcasted_iota(jnp.int32, (CHUNK, HEAD_DIM), 1)
        df_ref[...] = jnp.zeros_like(df_ref)
        for n in reversed(range(tr // CHUNK)):
            rows = slice(n * CHUNK, (n + 1) * CHUNK)
            dlogf = _dot_exact(tri, dc_ref[rows, :]) + run_ref[...]
            run_ref[...] = dlogf[0:1, :]
            z = f_ref[rows, :] + b
            e = jnp.exp(-jnp.abs(z))
            sig_neg = jnp.where(z >= 0.0, e, 1.0) / (1.0 + e)
            df = jnp.where(lane < N_HEADS, dlogf * sig_neg, 0.0)
            df_ref[rows, :HEAD_DIM] = df.astype(BF16)
            db_ref[...] += jnp.sum(df, axis=0, keepdims=True)

    return _call(body, name, (nb,),
                 [pl.BlockSpec((tr, HEAD_DIM), lambda i: (rev(i), CB_F)), pl.BlockSpec((1, HEAD_DIM), lambda i: (0, 0)),
                  pl.BlockSpec((tr, HEAD_DIM), lambda i: (rev(i), 0)), pl.BlockSpec(memory_space=pl.ANY)],
                 [pl.BlockSpec((tr, F_WIDTH), lambda i: (rev(i), GATED // F_WIDTH)),
                  pl.BlockSpec((1, HEAD_DIM), lambda i: (0, 0))],
                 [jax.ShapeDtypeStruct(dproj.shape, dproj.dtype), jax.ShapeDtypeStruct((1, HEAD_DIM), F32)],
                 [proj, b_pad, dc, dproj], scratch=[pltpu.VMEM((1, HEAD_DIM), F32)],
                 sem=("arbitrary",), nbytes=4 * _nbytes((tr, HEAD_DIM), F32), aliases={3: 0})


_GELU_K = math.sqrt(2.0 / math.pi)
_GELU_C = 0.044715


def _gelu(x):
    t = jnp.tanh(_GELU_K * (x + _GELU_C * (x * x * x)))
    return 0.5 * x * (1.0 + t), t


def _gelu_grad(x, t):
    return 0.5 * (1.0 + t) + 0.5 * x * (1.0 - t * t) * (_GELU_K * (1.0 + 3.0 * _GELU_C * (x * x)))


def _layernorm_stats(a):
    mu = jnp.mean(a, axis=-1, keepdims=True)
    xc = a - mu
    r = lax.rsqrt(jnp.mean(xc * xc, axis=-1, keepdims=True) + EPS)
    return xc * r, r


def _group(g):
    return slice(g * CHUNK, (g + 1) * CHUNK)


def _gated_cols(k):
    return pl.BlockSpec((CHUNK, D_MODEL), lambda i: (i, k))


def _fixed(shape):
    return pl.BlockSpec(shape, lambda i: (0,) * len(shape))


def _mix_fwd(proj, y_b, w_s, b_cols, g_v, name):
    t = proj.shape[0]

    def body(u_ref, vs_ref, ga_ref, gb_ref, yb_ref, w_ref, b_ref, gv_ref, o_ref):
        a_u, _ = _gelu(u_ref[...])
        a_v, _ = _gelu(vs_ref[...])
        xhat, _ = _layernorm_stats(a_v)
        vn = (xhat * gv_ref[...]).astype(BF16)
        sa, sb = _sigmoid(ga_ref[...]), _sigmoid(gb_ref[...])
        yb = yb_ref[...]
        mask = _causal(CHUNK)
        for g in range(N_GROUPS):
            cols = _group(g)
            w = jnp.where(mask, w_ref[g], 0.0).astype(BF16)
            mixed = jnp.dot(w, vn[:, cols], preferred_element_type=F32) + b_ref[:, g:g + 1]
            o_ref[:, cols] = (sa[:, cols] * (a_u[:, cols] * mixed) + sb[:, cols] * yb[:, cols]).astype(BF16)

    return _call(body, name, (t // CHUNK,),
                 [_gated_cols(0), _gated_cols(1), _gated_cols(2), _gated_cols(3), _row_spec(CHUNK, D_MODEL),
                  _fixed((N_GROUPS, CHUNK, CHUNK)), _fixed((CHUNK, N_GROUPS)), _fixed((1, D_MODEL))],
                 [_row_spec(CHUNK, D_MODEL)], [jax.ShapeDtypeStruct((t, D_MODEL), BF16)],
                 [proj, proj, proj, proj, y_b, w_s, b_cols, g_v], sem=("parallel",),
                 nbytes=16 * _nbytes((CHUNK, D_MODEL), F32))[0]


def _mix_bwd(proj, y_b, dmerged, w_s, b_cols, g_v, sel, name, deps=()):
    t = proj.shape[0]

    def body(u_ref, vs_ref, ga_ref, gb_ref, yb_ref, dm_ref, w_ref, b_ref, gv_ref, sel_ref,
             dg_ref, dyb_ref, delta_ref, dw_ref, dbt_ref, dgv_ref):
        @pl.when(pl.program_id(0) == 0)
        def _():
            dw_ref[...] = jnp.zeros_like(dw_ref)
            dbt_ref[...] = jnp.zeros_like(dbt_ref)
            dgv_ref[...] = jnp.zeros_like(dgv_ref)

        u, vs = u_ref[...], vs_ref[...]
        a_u, t_u = _gelu(u)
        a_v, t_v = _gelu(vs)
        xhat, r = _layernorm_stats(a_v)
        gv = gv_ref[...]
        vn = (xhat * gv).astype(BF16)
        sa, sb = _sigmoid(ga_ref[...]), _sigmoid(gb_ref[...])
        yb, dm = yb_ref[...], dm_ref[...].astype(F32)
        dyb = dm * sb
        dyb_ref[...] = dyb
        dyb_yb = dyb * yb
        dg_ref[:, 3 * D_MODEL:] = (dm * yb * (sb * (1.0 - sb))).astype(BF16)
        dya = dm * sa
        mask = _causal(CHUNK)
        dmixed_parts, dvn_parts = [], []
        for g in range(N_GROUPS):
            cols = _group(g)
            delta_ref[:, cols] = jnp.broadcast_to(jnp.sum(dyb_yb[:, cols], axis=1, keepdims=True), (CHUNK, CHUNK))
            w = jnp.where(mask, w_ref[g], 0.0).astype(BF16)
            mixed = jnp.dot(w, vn[:, cols], preferred_element_type=F32) + b_ref[:, g:g + 1]
            y_a = a_u[:, cols] * mixed
            dg_ref[:, 2 * D_MODEL + g * CHUNK:2 * D_MODEL + (g + 1) * CHUNK] = (
                dm[:, cols] * y_a * (sa[:, cols] * (1.0 - sa[:, cols]))).astype(BF16)
            dg_ref[:, cols] = (dya[:, cols] * mixed * _gelu_grad(u[:, cols], t_u[:, cols])).astype(BF16)
            dmixed = dya[:, cols] * a_u[:, cols]
            dmb = dmixed.astype(BF16)
            dw = lax.dot_general(dmb, vn[:, cols], _DIMS["nt"], preferred_element_type=F32)
            dw_ref[g] += jnp.where(mask, dw, 0.0)
            dvn_parts.append(lax.dot_general(w, dmb, _DIMS["tn"], preferred_element_type=F32))
            dmixed_parts.append(dmixed)
        dmixed_all = jnp.concatenate(dmixed_parts, axis=1)
        dvn = jnp.concatenate(dvn_parts, axis=1)
        dbt_ref[...] += _dot_exact_rhs(dmixed_all, sel_ref[...])
        dgv_ref[...] += jnp.sum(dvn * xhat, axis=0, keepdims=True)
        dxh = dvn * gv
        da_v = r * (dxh - jnp.mean(dxh, axis=-1, keepdims=True)
                    - xhat * jnp.mean(dxh * xhat, axis=-1, keepdims=True))
        dg_ref[:, D_MODEL:2 * D_MODEL] = (da_v * _gelu_grad(vs, t_v)).astype(BF16)

    row = lambda width: _row_spec(CHUNK, width)
    return _call(body, name, (t // CHUNK,),
                 [_gated_cols(0), _gated_cols(1), _gated_cols(2), _gated_cols(3), row(D_MODEL), row(D_MODEL),
                  _fixed((N_GROUPS, CHUNK, CHUNK)), _fixed((CHUNK, N_GROUPS)), _fixed((1, D_MODEL)),
                  _fixed((D_MODEL, HEAD_DIM))],
                 [row(GATED), row(D_MODEL), row(D_MODEL),
                  _fixed((N_GROUPS, CHUNK, CHUNK)), _fixed((CHUNK, HEAD_DIM)), _fixed((1, D_MODEL))],
                 [jax.ShapeDtypeStruct((t, IN_PAD), BF16), jax.ShapeDtypeStruct((t, D_MODEL), F32),
                  jax.ShapeDtypeStruct((t, D_MODEL), F32), jax.ShapeDtypeStruct((N_GROUPS, CHUNK, CHUNK), F32),
                  jax.ShapeDtypeStruct((CHUNK, HEAD_DIM), F32), jax.ShapeDtypeStruct((1, D_MODEL), F32)],
                 [proj, proj, proj, proj, y_b, dmerged, w_s, b_cols, g_v, sel], sem=("arbitrary",),
                 nbytes=40 * _nbytes((CHUNK, D_MODEL), F32), deps=deps)


def _adamw(w, g, m, v, name):
    r, c = w.shape
    tr = _tile(r, 256, unit=8)

    def body(w_ref, g_ref, m_ref, v_ref, d_ref, nm_ref, nv_ref):
        gv = g_ref[...]
        nm = ADAM_B1 * m_ref[...] + (1.0 - ADAM_B1) * gv
        nv = ADAM_B2 * v_ref[...] + (1.0 - ADAM_B2) * (gv * gv)
        m_hat = nm / (1.0 - ADAM_B1 ** ADAM_STEP)
        v_hat = nv / (1.0 - ADAM_B2 ** ADAM_STEP)
        d_ref[...] = -ADAM_LR * (m_hat / (jnp.sqrt(v_hat) + ADAM_EPS) + ADAM_WD * w_ref[...])
        nm_ref[...] = nm
        nv_ref[...] = nv

    spec = pl.BlockSpec((tr, c), lambda i: (i, 0))
    shape = jax.ShapeDtypeStruct((r, c), F32)
    return _call(body, name, (r // tr,), [spec] * 4, [spec] * 3, [shape] * 3, [w, g, m, v], sem=("parallel",),
                 nbytes=7 * _nbytes((tr, max(c, 128)), F32))


def _pad_rows(a, rows):
    return jnp.pad(a, ((0, rows - a.shape[0]), (0, 0)))


def _pack_rest(w_out, w_gate, w_up, w_down):
    return jnp.concatenate([w_out, w_gate.T, w_up.T, w_down], axis=0).astype(BF16)


def _unpack_in(gathered):
    i = jnp.arange(16)[None, :, None]
    b = jnp.arange(N_DEV)[:, None, None]
    head = jnp.where(i < b, jnp.roll(gathered[:, IN_STRIDE:IN_STRIDE + 16], 1, axis=0), gathered[:, :16])
    nat = jnp.concatenate([head, gathered[:, 16:IN_STRIDE]], axis=1).reshape(N_DEV * IN_STRIDE, D_MODEL)
    qkv = nat[2 * D_MODEL:5 * D_MODEL].reshape(3, N_HEADS, HEAD_DIM, D_MODEL)
    qkv = jnp.transpose(qkv, (1, 0, 2, 3)).reshape(3 * D_MODEL, D_MODEL)
    f = _pad_rows(gathered[N_DEV - 1, IN_STRIDE:IN_STRIDE + N_HEADS], F_WIDTH)
    return jnp.concatenate([nat[:2 * D_MODEL], nat[5 * D_MODEL:7 * D_MODEL], f, qkv], axis=0)


def _unpack_rest(gathered):
    take = lambda lo, n: gathered[:, lo:lo + n, :].reshape(N_DEV * n, D_MODEL)
    return take(R_OUT, SH_OUT), jnp.concatenate([take(R_G, SH_FF), take(R_U, SH_FF)], axis=0), take(R_D, SH_FF)


def _pack_grad_in(dw_in_t):
    qkv = dw_in_t[GATED + F_WIDTH:].reshape(N_HEADS, 3, HEAD_DIM, D_MODEL)
    qkv = jnp.transpose(qkv, (1, 0, 2, 3)).reshape(3 * D_MODEL, D_MODEL)
    return jnp.concatenate([dw_in_t[:2 * D_MODEL], qkv, dw_in_t[2 * D_MODEL:GATED],
                            dw_in_t[GATED:GATED + IN_ROWS - N_DEV * IN_STRIDE]], axis=0)


def _pack_grad_rest(dw_out, dw_gu_t, dw_down):
    split = lambda a, n: a.reshape(N_DEV, n, D_MODEL)
    return jnp.concatenate([split(dw_out, SH_OUT), split(dw_gu_t[:D_FF], SH_FF), split(dw_gu_t[D_FF:], SH_FF),
                            split(dw_down, SH_FF)], axis=1)


def kernel(x, mix_pre_g, w_in, b_forget, sgu_norm_g, w_spatial, b_spatial, w_out, mix_post_g, ffn_pre_g, w_gate, w_up, w_down, ffn_post_g, loss_target, m_mix_pre_g, m_w_in, m_b_forget, m_sgu_norm_g, m_w_spatial, m_b_spatial, m_w_out, m_mix_post_g, m_ffn_pre_g, m_w_gate, m_w_up, m_w_down, m_ffn_post_g, v_mix_pre_g, v_w_in, v_b_forget, v_sgu_norm_g, v_w_spatial, v_b_spatial, v_w_out, v_mix_post_g, v_ffn_pre_g, v_w_gate, v_w_up, v_w_down, v_ffn_post_g):
    depth = w_in.shape[0]
    assert depth == 2
    t = x.shape[1]
    x0 = x.reshape(t, D_MODEL)
    target = loss_target.reshape(t, D_MODEL)
    sel = (jnp.arange(D_MODEL)[:, None] // HEAD_DIM == jnp.arange(HEAD_DIM)[None, :]).astype(BF16)
    vec = lambda a, l: a[l][None, :]
    zero_row = jnp.zeros((1, D_MODEL), F32)

    me = 4 * lax.axis_index("x") + 2 * lax.axis_index("y") + lax.axis_index("c")
    p_in = [lax.dynamic_update_slice(jnp.zeros((P_IN, D_MODEL), BF16), w_in[l].T.astype(BF16), (me, 0))
            for l in range(depth)]
    p_rest = [_pack_rest(w_out[l], w_gate[l], w_up[l], w_down[l]) for l in range(depth)]
    w_in_t = [None] * depth
    w_rest = [None] * depth
    w_in_t[0] = _unpack_in(_exchange("gather", p_in[0], "gather_w_in_0")[0])

    saved = []
    xl = x0
    h = _rms_fwd(xl, vec(mix_pre_g, 0), "rms_in_0")
    dy = loss = None
    for l in range(depth):
        b_pad = jnp.pad(b_forget[l], (0, HEAD_DIM - N_HEADS))[None, :]
        b_cols = b_spatial[l].T
        proj = _matmul(h, w_in_t[l], "nt", F32, f"proj_{l}")
        c_pad = _forget_cumsum(proj, b_pad, f"forget_cumsum_{l}")
        c_row = c_pad[:, :N_HEADS].T[:, None, :]
        riders = [("gather", p_rest[l])] + ([("gather", p_in[l + 1])] if l + 1 < depth else [])
        y_b, e_row, *arrived = _attn_fwd_t(proj, c_pad, c_row, f"attn_fwd_{l}", comm=riders)
        w_rest[l] = _unpack_rest(arrived[0])
        if l + 1 < depth:
            w_in_t[l + 1] = _unpack_in(arrived[1])
        w_o, w_gu_t, w_d = w_rest[l]
        merged = _mix_fwd(proj, y_b, w_spatial[l], b_cols, vec(sgu_norm_g, l), f"mix_fwd_{l}")
        o = _matmul(merged, w_o, "nn", F32, f"out_proj_{l}")
        x1, h2 = _post_norm_fwd(xl, o, vec(mix_post_g, l), vec(ffn_pre_g, l), f"post_mix_{l}")
        gu = _matmul(h2, w_gu_t, "nt", F32, f"ffn_gu_{l}")
        act = _swiglu_fwd(gu, f"swiglu_fwd_{l}")
        dn = _matmul(act, w_d, "nn", F32, f"ffn_down_{l}")
        saved.append(dict(x0=xl, h=h, proj=proj, b_pad=b_pad, b_cols=b_cols, c_pad=c_pad, y_b=y_b, e_row=e_row,
                          merged=merged, o=o, x1=x1, h2=h2, gu=gu, act=act, dn=dn))
        if l + 1 < depth:
            xl, h = _post_norm_fwd(x1, dn, vec(ffn_post_g, l), vec(mix_pre_g, l + 1), f"post_ffn_{l}")
        else:
            dy, loss = _post_norm_loss(x1, dn, vec(ffn_post_g, l), target, "loss")

    g_small = [None] * depth
    flights = {}
    late_token = None
    dx = dy
    dd, dg_ffn_post = _norm_bwd(dx, f"bwd_ffn_post_{depth - 1}", post=(saved[-1]["dn"], vec(ffn_post_g, depth - 1)))
    for l in reversed(range(depth)):
        s = saved[l]
        w_o, w_gu_t, w_d = w_rest[l]
        dact = _matmul(dd, w_d, "nt", F32, f"d_act_{l}")
        dw_d = _matmul(s["act"], dd, "tn", BF16, f"dw_down_{l}")
        dgu = _swiglu_bwd(s["gu"], dact, f"swiglu_bwd_{l}")
        dh2 = _matmul(dgu, w_gu_t, "nn", F32, f"d_h2_{l}")
        dw_gu_t = _matmul(dgu, s["h2"], "tn", BF16, f"dw_gu_{l}")
        dx1, dg_ffn_pre, do, dg_mix_post = _norm_bwd(
            dx, f"bwd_post_mix_{l}", pre=(dh2, s["x1"], vec(ffn_pre_g, l)), post=(s["o"], vec(mix_post_g, l)))
        dmerged = _matmul(do, w_o, "nt", F32, f"d_merged_{l}")
        dw_o = _matmul(s["merged"], do, "tn", BF16, f"dw_out_{l}")
        flights[("rest", l)], token = _exchange_start("scatter", _pack_grad_rest(dw_o, dw_gu_t, dw_d),
                                                      f"start_grad_rest_{l}")
        dproj, dyb, delta_b, dw_s, dbt, dg_v = _mix_bwd(
            s["proj"], s["y_b"], dmerged, w_spatial[l], s["b_cols"], vec(sgu_norm_g, l), sel, f"mix_bwd_{l}",
            deps=[token])
        db_s = dbt[:, :N_GROUPS].T.reshape(1, D_MODEL)
        tokens = []
        if l == 0:
            g_small[0] = jnp.concatenate([zero_row, dg_v, dg_mix_post, dg_ffn_pre, dg_ffn_post, db_s, zero_row,
                                          zero_row, dw_s.reshape(CHUNK, D_MODEL)], axis=0)
            flights["small"], token = _exchange_start("spread", jnp.concatenate([g_small[1], g_small[0]], axis=0),
                                                      "start_small_grads")
            tokens = [token]
        dproj, dc_key, dc_query = _attn_bwd_t(s["proj"], dyb, s["e_row"], delta_b, s["c_pad"], dproj,
                                              f"attn_bwd_{l}", deps=tokens)
        dc = jnp.pad((dc_key + dc_query)[:, 0, :].T, ((0, 0), (0, HEAD_DIM - N_HEADS)))
        dproj, db_f = _forget_bwd_t(s["proj"], s["b_pad"], dc, dproj, f"forget_bwd_{l}")
        db_f_row = jnp.pad(db_f, ((0, 0), (0, D_MODEL - HEAD_DIM)))
        dw_in_t = _matmul(dproj, s["h"], "tn", BF16, f"dw_in_{l}")
        flights[("in", l)], token = _exchange_start("windows", _pack_grad_in(dw_in_t), f"start_grad_in_{l}")
        dh = _matmul(dproj, w_in_t[l], "nn", F32, f"d_h_{l}", deps=[token])
        if l > 0:
            dx, dg_mix_pre, dd, dg_ffn_post_below = _norm_bwd(
                dx1, f"bwd_pre_mix_{l}", pre=(dh, s["x0"], vec(mix_pre_g, l)),
                post=(saved[l - 1]["dn"], vec(ffn_post_g, l - 1)))
            g_small[l] = jnp.concatenate([dg_mix_pre, dg_v, dg_mix_post, dg_ffn_pre, dg_ffn_post, db_s, db_f_row,
                                          zero_row, dw_s.reshape(CHUNK, D_MODEL)], axis=0)
            dg_ffn_post = dg_ffn_post_below
        else:
            dx, dg_mix_pre = _norm_bwd(dx1, f"bwd_pre_mix_{l}", pre=(dh, s["x0"], vec(mix_pre_g, l)))
            late_rows = jnp.concatenate([dg_mix_pre, db_f_row] + [zero_row] * 6, axis=0)
            flights["late"], late_token = _exchange_start("spread", late_rows, "start_late_small_grads")
    grad_x = dx.reshape(x.shape)
    loss = lax.psum(loss.reshape(()), ("x", "y", "c"))

    landed, after = {}, [grad_x, late_token]
    for key in [("rest", 1), ("in", 1), ("rest", 0), "small"]:
        label = key if isinstance(key, str) else f"grad_{key[0]}_{key[1]}"
        landed[key] = _exchange_wait(flights[key], after, f"wait_{label}")
        after = [landed[key]]
    small_sum = _sum_slots(landed["small"], S_ROWS, "sum_small_grads")
    sm = small_sum.reshape(depth, S_ROWS, D_MODEL)[::-1]
    g = {
        "sgu_norm_g": sm[:, 1], "mix_post_g": sm[:, 2],
        "ffn_pre_g": sm[:, 3], "ffn_post_g": sm[:, 4], "b_spatial": sm[:, 5].reshape(depth, N_GROUPS, CHUNK),
        "w_spatial": sm[:, 8:].reshape(depth, N_GROUPS, CHUNK, CHUNK),
    }
    g_rest = jnp.stack([_sum_slots(landed[("rest", l)], 592, f"sum_grad_rest_{l}") for l in range(depth)])
    g["w_out"] = g_rest[:, R_OUT:R_OUT + SH_OUT]
    g["w_gate"] = jnp.swapaxes(g_rest[:, R_G:R_G + SH_FF], 1, 2)
    g["w_up"] = jnp.swapaxes(g_rest[:, R_U:R_U + SH_FF], 1, 2)
    g["w_down"] = g_rest[:, R_D:R_D + SH_FF]

    names = ["mix_pre_g", "w_in", "b_forget", "sgu_norm_g", "w_spatial", "b_spatial", "w_out", "mix_post_g",
             "ffn_pre_g", "w_gate", "w_up", "w_down", "ffn_post_g"]
    ws = dict(mix_pre_g=mix_pre_g, w_in=w_in, b_forget=b_forget, sgu_norm_g=sgu_norm_g, w_spatial=w_spatial,
              b_spatial=b_spatial, w_out=w_out, mix_post_g=mix_post_g, ffn_pre_g=ffn_pre_g, w_gate=w_gate, w_up=w_up,
              w_down=w_down, ffn_post_g=ffn_post_g)
    ms = dict(mix_pre_g=m_mix_pre_g, w_in=m_w_in, b_forget=m_b_forget, sgu_norm_g=m_sgu_norm_g, w_spatial=m_w_spatial,
              b_spatial=m_b_spatial, w_out=m_w_out, mix_post_g=m_mix_post_g, ffn_pre_g=m_ffn_pre_g, w_gate=m_w_gate,
              w_up=m_w_up, w_down=m_w_down, ffn_post_g=m_ffn_post_g)
    vs = dict(mix_pre_g=v_mix_pre_g, w_in=v_w_in, b_forget=v_b_forget, sgu_norm_g=v_sgu_norm_g, w_spatial=v_w_spatial,
              b_spatial=v_b_spatial, w_out=v_w_out, mix_post_g=v_mix_post_g, ffn_pre_g=v_ffn_pre_g, w_gate=v_w_gate,
              w_up=v_w_up, w_down=v_w_down, ffn_post_g=v_ffn_post_g)
    updates = {}
    last = ["w_in", "mix_pre_g", "b_forget"]
    for n in [n for n in names if n not in last] + last:
        if n == last[0]:
            sum_in_1 = _sum_slots(landed[("in", 1)], 304, "sum_grad_in_1")
            done = [u[0] for u in updates.values()] + [sum_in_1]
            sum_in_0 = _sum_slots(_exchange_wait(flights[("in", 0)], done, "wait_grad_in_0"), 304, "sum_grad_in_0")
            late = _sum_slots(_exchange_wait(flights["late"], [sum_in_0], "wait_late"), 8, "sum_late_small_grads")
            g_in = jnp.stack([sum_in_0, sum_in_1])
            g["w_in"] = jnp.swapaxes(lax.dynamic_slice(g_in, (0, me, 0), (depth, SH_IN, D_MODEL)), 1, 2)
            g["mix_pre_g"] = jnp.stack([late[0], sm[1, 0]])
            g["b_forget"] = jnp.stack([late[1, :N_HEADS], sm[1, 6, :N_HEADS]])
        flat = (-1, ws[n].shape[-1])
        updates[n] = _adamw(ws[n].reshape(flat), g[n].reshape(flat), ms[n].reshape(flat), vs[n].reshape(flat),
                            f"adamw_{n}")
    deltas = [updates[n][0].reshape(ws[n].shape) for n in names]
    new_m = [updates[n][1].reshape(ws[n].shape) for n in names]
    new_v = [updates[n][2].reshape(ws[n].shape) for n in names]
    grads = [g[n].reshape(ws[n].shape) for n in names]
    return (loss, grad_x, *grads, *deltas, *new_m, *new_v)
```

```python
import math

import jax
import jax.numpy as jnp
from jax import lax
from jax.experimental import pallas as pl
from jax.experimental.pallas import tpu as pltpu

F32 = jnp.float32
BF16 = jnp.bfloat16

N_DEV = 8
D_MODEL = 1024
N_HEADS = 8
HEAD_DIM = 128
CHUNK = 128
N_GROUPS = 8
D_FF = 2816
IN_WIDTH = 7 * D_MODEL + N_HEADS
IN_PAD = 7680
EPS = 1e-6
GATED = 4 * D_MODEL
F_WIDTH = 512
QKV_WIDTH = 3 * HEAD_DIM
CB_F = GATED // HEAD_DIM
CB_QKV = (GATED + F_WIDTH) // HEAD_DIM
assert GATED + F_WIDTH + N_HEADS * QKV_WIDTH == IN_PAD and (GATED + F_WIDTH) % QKV_WIDTH == 0

ADAM_LR, ADAM_B1, ADAM_B2, ADAM_EPS, ADAM_WD, ADAM_STEP = 0.001, 0.9, 0.999, 1e-08, 0.01, 10

SH_IN = IN_WIDTH // N_DEV
SH_OUT = D_MODEL // N_DEV
SH_FF = D_FF // N_DEV
P_IN = 912
IN_STRIDE = 7 * HEAD_DIM
IN_ROWS = IN_STRIDE * (N_DEV - 1) + P_IN
R_OUT, R_G, R_U, R_D = 0, SH_OUT, SH_OUT + SH_FF, SH_OUT + 2 * SH_FF
P_REST = SH_OUT + 3 * SH_FF
S_ROWS = 136

ATTN_BLOCK = 512
ROW_BLOCK = 512
MM_TM, MM_TN, MM_TK = 1536, 1536, 2048
VMEM_CAP = 56 << 20
NEG = -1e30


def _tile(n, cap, unit=128):
    if n <= cap:
        return n
    best = None
    for t in range(unit, cap + 1, unit):
        if n % t == 0:
            best = t
    assert best is not None, (n, cap)
    return best


def _nbytes(shape, dtype):
    return math.prod(shape) * jnp.dtype(dtype).itemsize


_HBM = pl.BlockSpec(memory_space=pltpu.HBM)
MESH = pl.DeviceIdType.MESH

_N_REMOTE = N_DEV - 1


def _exchange_shapes(kind, x):
    if kind[0] == "gather":
        shape = x.shape[:kind[1]] + (N_DEV,) + x.shape[kind[1]:]
    elif kind[0] == "spread":
        shape = (N_DEV,) + x.shape
    else:
        shape = (N_DEV, kind[3]) + x.shape[1:]
    return [jax.ShapeDtypeStruct(shape, x.dtype)]


def _exchange_sems(kind):
    del kind
    return [pltpu.SemaphoreType.DMA((_N_REMOTE,)), pltpu.SemaphoreType.DMA((_N_REMOTE,)), pltpu.SemaphoreType.DMA]


def _plan(kind, x_ref, outs, sems):
    send_sems, recv_sems, local_sem = sems
    x, y, c = lax.axis_index("x"), lax.axis_index("y"), lax.axis_index("c")

    def remote(src, dst, k, to):
        return pltpu.make_async_remote_copy(src_ref=src, dst_ref=dst, send_sem=send_sems.at[k], recv_sem=recv_sems.at[k],
                                            device_id=to, device_id_type=MESH)

    sibling = (x, y, 1 - c)
    if kind[0] == "gather":
        out, = outs
        slot = lambda px, py, pc: out.at[(slice(None),) * kind[1] + (4 * px + 2 * py + pc,)]
        chips = [(1 - x, y), (x, 1 - y), (1 - x, 1 - y)]
        local = [pltpu.make_async_copy(x_ref, slot(x, y, c), local_sem)]
        first = [remote(x_ref, slot(x, y, c), 0, sibling)]
        first += [remote(x_ref, slot(x, y, c), 1 + k, (*chip, c)) for k, chip in enumerate(chips)]
        relays = [(remote(x_ref, slot(*chip, c), 1 + k, (*chip, c)), remote(slot(*chip, c), slot(*chip, c), 4 + k, sibling))
                  for k, chip in enumerate(chips)]
        arrivals = [remote(x_ref, slot(x, y, 1 - c), 0, sibling)]
        arrivals += [remote(x_ref, slot(*chip, 1 - c), 4 + k, sibling) for k, chip in enumerate(chips)]
        return local, first, relays, arrivals
    out, = outs
    mine = 4 * x + 2 * y + c
    if kind[0] == "spread":
        src = lambda d: x_ref
    else:
        _, base, stride, length = kind
        src = lambda d: x_ref.at[pl.ds(pl.multiple_of(base + stride * d, 16), length)]
    first, arrivals = [], []
    for j in range(1, N_DEV):
        peer = (1 - x if j & 4 else x, 1 - y if j & 2 else y, 1 - c if j & 1 else c)
        theirs = 4 * peer[0] + 2 * peer[1] + peer[2]
        first.append(remote(src(theirs), out.at[mine], j - 1, peer))
        arrivals.append(remote(src(theirs), out.at[theirs], j - 1, peer))
    local = [] if local_sem is None else [pltpu.make_async_copy(src(mine), out.at[mine], local_sem)]
    return local, first, [], arrivals


def _start(plan):
    local, first, _, _ = plan
    for cp in local + first:
        cp.start()


def _finish(plan):
    local, first, relays, arrivals = plan
    for arrival, onward in relays:
        arrival.wait_recv()
        onward.start()
    for cp in arrivals:
        cp.wait_recv()
    for cp in first + [onward for _, onward in relays]:
        cp.wait_send()
    for cp in local:
        cp.wait()


def _direct_copies(kind, x_ref, land_ref, send_sems, recv_sems):
    _, first, _, arrivals = _plan(kind, x_ref, [land_ref], (send_sems, recv_sems, None))
    return first, arrivals


_SEM = pl.BlockSpec(memory_space=pltpu.SEMAPHORE)
_DATAFLOW = pltpu.SideEffectType.DATAFLOW_SIDE_EFFECTING


def _exchange_start(kind, x, name):
    me = 4 * lax.axis_index("x") + 2 * lax.axis_index("y") + lax.axis_index("c")
    own = x if kind[0] == "spread" else lax.dynamic_slice_in_dim(x, kind[1] + kind[2] * me, kind[3], 0)
    shape = _exchange_shapes(kind, x)[0].shape
    land = lax.dynamic_update_slice_in_dim(lax.empty(shape, x.dtype), own[None], me, 0)

    def body(x_ref, land_ref, send_sems, recv_sems, x_thru, land_thru, token):
        del x_thru, land_thru
        for cp in _direct_copies(kind, x_ref, land_ref, send_sems, recv_sems)[0]:
            cp.start()
        token[...] = jnp.zeros_like(token)

    *handle, token = pl.pallas_call(
        body, name=name,
        out_shape=(pltpu.SemaphoreType.DMA((_N_REMOTE,)), pltpu.SemaphoreType.DMA((_N_REMOTE,)),
                   pltpu.HBM(x.shape, x.dtype), pltpu.HBM(land.shape, land.dtype), jax.ShapeDtypeStruct((8, 128), F32)),
        in_specs=(_HBM, _HBM), out_specs=(_SEM, _SEM, _HBM, _HBM, pl.BlockSpec(memory_space=pltpu.VMEM)),
        input_output_aliases={0: 2, 1: 3}, compiler_params=pltpu.CompilerParams(has_side_effects=_DATAFLOW),
    )(pltpu.with_memory_space_constraint(x, pltpu.HBM), pltpu.with_memory_space_constraint(land, pltpu.HBM))
    return (kind, *handle), token


def _exchange_wait(handle, after, name):
    kind, send_sems, recv_sems, x_thru, land_thru = handle

    def body(x_ref, land_ref, send_sems, recv_sems, *rest):
        first, arrivals = _direct_copies(kind, x_ref, land_ref, send_sems, recv_sems)
        for cp in first:
            cp.wait_send()
        for cp in arrivals:
            cp.wait_recv()

    return pl.pallas_call(
        body, name=name,
        out_shape=(pltpu.HBM(x_thru.shape, x_thru.dtype), pltpu.HBM(land_thru.shape, land_thru.dtype)),
        in_specs=(_HBM, _HBM, _SEM, _SEM) + (pl.BlockSpec(memory_space=pl.ANY),) * len(after), out_specs=(_HBM, _HBM),
        input_output_aliases={0: 0, 1: 1}, compiler_params=pltpu.CompilerParams(has_side_effects=_DATAFLOW),
    )(x_thru, land_thru, send_sems, recv_sems, *after)[1]


def _exchange(kind, x, name):
    n_out = len(_exchange_shapes(kind, x))

    def body(x_ref, *refs):
        plan = _plan(kind, x_ref, refs[:n_out], refs[n_out:])
        _start(plan)
        _finish(plan)

    return pl.pallas_call(body, name=name, out_shape=_exchange_shapes(kind, x), in_specs=[_HBM],
                          out_specs=[_HBM] * n_out, scratch_shapes=_exchange_sems(kind))(x)


def _call(body, name, grid, in_specs, out_specs, out_shape, args, scratch=(), sem=None, nbytes=0, aliases=None,
          comm=(), deps=()):
    in_specs, out_specs, out_shape, args, scratch = (list(in_specs), list(out_specs), list(out_shape), list(args),
                                                     list(scratch))
    if deps:
        n_real, n_deps, unordered = len(args), len(deps), body

        def body(*refs):
            unordered(*refs[:n_real], *refs[n_real + n_deps:])

        in_specs += [pl.BlockSpec(memory_space=pl.ANY)] * n_deps
        args += list(deps)
    if comm:
        n_in, n_out, n_scr, n_ops = len(args), len(out_shape), len(scratch), len(comm)
        kinds = [kind for kind, _ in comm]
        shapes = [_exchange_shapes(kind, x) for kind, x in comm]
        inner = body

        def body(*refs):
            ins, refs = refs[:n_in], refs[n_in:]
            cins, refs = refs[:n_ops], refs[n_ops:]
            outs, refs = refs[:n_out], refs[n_out:]
            couts = []
            for sh in shapes:
                couts.append(refs[:len(sh)])
                refs = refs[len(sh):]
            own_scratch, sems = refs[:n_scr], refs[n_scr:]
            first = last = None
            for axis, size in enumerate(grid):
                at_start, at_end = pl.program_id(axis) == 0, pl.program_id(axis) == size - 1
                first = at_start if first is None else first & at_start
                last = at_end if last is None else last & at_end
            plans = [_plan(kinds[o], cins[o], couts[o], sems[3 * o:3 * o + 3]) for o in range(n_ops)]

            @pl.when(first)
            def _():
                for p in plans:
                    _start(p)

            inner(*ins, *outs, *own_scratch)

            @pl.when(last)
            def _():
                for p in plans:
                    _finish(p)

        in_specs += [_HBM] * n_ops
        args += [x for _, x in comm]
        for kind, sh in zip(kinds, shapes):
            out_shape += sh
            out_specs += [_HBM] * len(sh)
            scratch += _exchange_sems(kind)
        sem = ("arbitrary",) * len(grid)
    limit = int(min(max(2 * nbytes + (8 << 20), 32 << 20), VMEM_CAP))
    return pl.pallas_call(
        body, name=name, grid=grid, in_specs=in_specs, out_specs=out_specs, out_shape=out_shape,
        scratch_shapes=scratch, input_output_aliases=aliases or {},
        compiler_params=pltpu.CompilerParams(dimension_semantics=sem, vmem_limit_bytes=limit),
    )(*args)


def _sum_slots(x, tr, name):
    n, r, c = x.shape

    def body(x_ref, o_ref):
        acc = x_ref[0].astype(F32)
        for d in range(1, n):
            acc = acc + x_ref[d].astype(F32)
        o_ref[...] = acc

    return _call(body, name, (r // tr,), [pl.BlockSpec((n, tr, c), lambda i: (0, i, 0))],
                 [pl.BlockSpec((tr, c), lambda i: (i, 0))], [jax.ShapeDtypeStruct((r, c), F32)], [x],
                 sem=("parallel",), nbytes=_nbytes((n, tr, c), x.dtype) + _nbytes((tr, c), F32))[0]


_DIMS = {"nn": (((1,), (0,)), ((), ())), "nt": (((1,), (1,)), ((), ())), "tn": (((0,), (0,)), ((), ()))}


def _matmul(a, b, mode, out_dtype, name, comm=(), deps=()):
    if mode == "nn":
        (m, k), (k2, n) = a.shape, b.shape
    elif mode == "nt":
        (m, k), (n, k2) = a.shape, b.shape
    else:
        (k, m), (k2, n) = a.shape, b.shape
    assert k == k2, (a.shape, b.shape, mode)
    tm, tn, tk = _tile(m, MM_TM), _tile(n, MM_TN), _tile(k, MM_TK)
    nk = k // tk
    dims = _DIMS[mode]
    a_spec = {"nn": pl.BlockSpec((tm, tk), lambda i, j, kk: (i, kk)),
              "nt": pl.BlockSpec((tm, tk), lambda i, j, kk: (i, kk)),
              "tn": pl.BlockSpec((tk, tm), lambda i, j, kk: (kk, i))}[mode]
    b_spec = {"nn": pl.BlockSpec((tk, tn), lambda i, j, kk: (kk, j)),
              "nt": pl.BlockSpec((tn, tk), lambda i, j, kk: (j, kk)),
              "tn": pl.BlockSpec((tk, tn), lambda i, j, kk: (kk, j))}[mode]

    def partial_product(a_ref, b_ref):
        return lax.dot_general(a_ref[...].astype(BF16), b_ref[...].astype(BF16), dims, preferred_element_type=F32)

    if nk == 1:
        def body(a_ref, b_ref, o_ref):
            o_ref[...] = partial_product(a_ref, b_ref).astype(o_ref.dtype)
        scratch = []
    else:
        def body(a_ref, b_ref, o_ref, acc_ref):
            kk = pl.program_id(2)

            @pl.when(kk == 0)
            def _():
                acc_ref[...] = jnp.zeros_like(acc_ref)

            acc_ref[...] += partial_product(a_ref, b_ref)

            @pl.when(kk == nk - 1)
            def _():
                o_ref[...] = acc_ref[...].astype(o_ref.dtype)
        scratch = [pltpu.VMEM((tm, tn), F32)]

    nbytes = (_nbytes((tm, tk), a.dtype) + _nbytes((tk, tn), b.dtype) + _nbytes((tm, tn), out_dtype)
              + _nbytes((tm, tn), F32))
    res = _call(body, name, (m // tm, n // tn, nk), [a_spec, b_spec],
                [pl.BlockSpec((tm, tn), lambda i, j, kk: (i, j))], [jax.ShapeDtypeStruct((m, n), out_dtype)], [a, b],
                scratch=scratch, sem=("parallel", "parallel", "arbitrary"), nbytes=nbytes, comm=comm, deps=deps)
    return res if comm else res[0]


def _rms(x):
    return lax.rsqrt(jnp.mean(x * x, axis=-1, keepdims=True) + EPS)


def _rms_bwd(dz, a, g):
    r = _rms(a)
    dzg = dz * g
    da = r * dzg - a * (r * r * r) * jnp.mean(dzg * a, axis=-1, keepdims=True)
    return da, dz * (a * r)


def _row_spec(tr, width):
    return pl.BlockSpec((tr, width), lambda i: (i, 0))


def _vec_spec(width):
    return pl.BlockSpec((1, width), lambda i: (0, 0))


def _rms_fwd(x, g, name):
    t, d = x.shape
    tr = min(ROW_BLOCK, t)

    def body(x_ref, g_ref, h_ref):
        xv = x_ref[...]
        h_ref[...] = ((xv * _rms(xv)) * g_ref[...]).astype(BF16)

    return _call(body, name, (t // tr,), [_row_spec(tr, d), _vec_spec(d)], [_row_spec(tr, d)],
                 [jax.ShapeDtypeStruct((t, d), BF16)], [x, g], sem=("parallel",), nbytes=3 * _nbytes((tr, d), F32))[0]


def _post_norm_fwd(x, o, g_post, g_next, name):
    t, d = x.shape
    tr = min(ROW_BLOCK, t)

    def body(x_ref, o_ref, gp_ref, gn_ref, xn_ref, h_ref):
        ov = o_ref[...]
        xn = x_ref[...] + (ov * _rms(ov)) * gp_ref[...]
        xn_ref[...] = xn
        h_ref[...] = ((xn * _rms(xn)) * gn_ref[...]).astype(BF16)

    return _call(body, name, (t // tr,), [_row_spec(tr, d), _row_spec(tr, d), _vec_spec(d), _vec_spec(d)],
                 [_row_spec(tr, d), _row_spec(tr, d)],
                 [jax.ShapeDtypeStruct((t, d), F32), jax.ShapeDtypeStruct((t, d), BF16)], [x, o, g_post, g_next],
                 sem=("parallel",), nbytes=5 * _nbytes((tr, d), F32))


def _post_norm_loss(x, o, g_post, target, name):
    t, d = x.shape
    tr = min(ROW_BLOCK, t)

    def body(x_ref, o_ref, gp_ref, t_ref, dy_ref, loss_ref):
        ov = o_ref[...]
        err = x_ref[...] + (ov * _rms(ov)) * gp_ref[...] - t_ref[...]
        dy_ref[...] = err / d
        part = 0.5 * jnp.sum(jnp.mean(err * err, axis=-1, keepdims=True), axis=0, keepdims=True)

        @pl.when(pl.program_id(0) == 0)
        def _():
            loss_ref[...] = jnp.zeros_like(loss_ref)

        loss_ref[...] += part

    return _call(body, name, (t // tr,), [_row_spec(tr, d), _row_spec(tr, d), _vec_spec(d), _row_spec(tr, d)],
                 [_row_spec(tr, d), pl.BlockSpec((1, 1), lambda i: (0, 0))],
                 [jax.ShapeDtypeStruct((t, d), F32), jax.ShapeDtypeStruct((1, 1), F32)], [x, o, g_post, target],
                 sem=("arbitrary",), nbytes=5 * _nbytes((tr, d), F32))


def _norm_bwd(dskip, name, pre=None, post=None):
    t, d = dskip.shape
    tr = min(ROW_BLOCK, t)
    n_in = 1 + (3 if pre else 0) + (2 if post else 0)

    def body(*refs):
        ins, outs = list(refs[:n_in]), list(refs[n_in:])
        first = pl.program_id(0) == 0
        dx = ins.pop(0)[...]
        if pre:
            dh_ref, xin_ref, gpre_ref = ins.pop(0), ins.pop(0), ins.pop(0)
            dxin, dg_rows = _rms_bwd(dh_ref[...].astype(F32), xin_ref[...], gpre_ref[...])
            dx = dx + dxin
            dx_ref, dgpre_ref = outs.pop(0), outs.pop(0)
            dx_ref[...] = dx

            @pl.when(first)
            def _():
                dgpre_ref[...] = jnp.zeros_like(dgpre_ref)

            dgpre_ref[...] += jnp.sum(dg_rows, axis=0, keepdims=True)
        if post:
            a_ref, gpost_ref = ins.pop(0), ins.pop(0)
            da, dg_rows = _rms_bwd(dx, a_ref[...], gpost_ref[...])
            da_ref, dgpost_ref = outs.pop(0), outs.pop(0)
            da_ref[...] = da.astype(BF16)

            @pl.when(first)
            def _():
                dgpost_ref[...] = jnp.zeros_like(dgpost_ref)

            dgpost_ref[...] += jnp.sum(dg_rows, axis=0, keepdims=True)

    args, in_specs, out_shape, out_specs = [dskip], [_row_spec(tr, d)], [], []
    if pre:
        args += list(pre)
        in_specs += [_row_spec(tr, d), _row_spec(tr, d), _vec_spec(d)]
        out_shape += [jax.ShapeDtypeStruct((t, d), F32), jax.ShapeDtypeStruct((1, d), F32)]
        out_specs += [_row_spec(tr, d), _vec_spec(d)]
    if post:
        args += list(post)
        in_specs += [_row_spec(tr, d), _vec_spec(d)]
        out_shape += [jax.ShapeDtypeStruct((t, d), BF16), jax.ShapeDtypeStruct((1, d), F32)]
        out_specs += [_row_spec(tr, d), _vec_spec(d)]
    return _call(body, name, (t // tr,), in_specs, out_specs, out_shape, args, sem=("arbitrary",),
                 nbytes=7 * _nbytes((tr, d), F32))


def _sigmoid(x):
    return 1.0 / (1.0 + jnp.exp(-x))


def _swiglu_fwd(gu, name):
    t = gu.shape[0]
    tr = min(ROW_BLOCK, t)

    def body(gu_ref, a_ref):
        gate, up = gu_ref[:, :D_FF], gu_ref[:, D_FF:]
        a_ref[...] = ((gate * _sigmoid(gate)) * up).astype(BF16)

    return _call(body, name, (t // tr,), [_row_spec(tr, 2 * D_FF)], [_row_spec(tr, D_FF)],
                 [jax.ShapeDtypeStruct((t, D_FF), BF16)], [gu], sem=("parallel",),
                 nbytes=2 * _nbytes((tr, 2 * D_FF), F32))[0]


def _swiglu_bwd(gu, dact, name):
    t = gu.shape[0]
    tr = min(ROW_BLOCK, t)

    def body(gu_ref, da_ref, o_ref):
        gate, up = gu_ref[:, :D_FF], gu_ref[:, D_FF:]
        da = da_ref[...]
        s = _sigmoid(gate)
        o_ref[:, :D_FF] = (da * up * (s * (1.0 + gate * (1.0 - s)))).astype(BF16)
        o_ref[:, D_FF:] = (da * (gate * s)).astype(BF16)

    return _call(body, name, (t // tr,), [_row_spec(tr, 2 * D_FF), _row_spec(tr, D_FF)], [_row_spec(tr, 2 * D_FF)],
                 [jax.ShapeDtypeStruct((t, 2 * D_FF), BF16)], [gu, dact], sem=("parallel",),
                 nbytes=3 * _nbytes((tr, 2 * D_FF), F32))[0]


def _split3(x):
    hi = x.astype(BF16)
    r1 = x - hi.astype(F32)
    mid = r1.astype(BF16)
    lo = (r1 - mid.astype(F32)).astype(BF16)
    return hi, mid, lo


def _dot_exact(mat01, x):
    hi, mid, lo = _split3(x)
    out = jnp.dot(mat01, hi, preferred_element_type=F32)
    out = out + jnp.dot(mat01, mid, preferred_element_type=F32)
    return out + jnp.dot(mat01, lo, preferred_element_type=F32)


def _dot_exact_rhs(x, mat01):
    hi, mid, lo = _split3(x)
    out = jnp.dot(hi, mat01, preferred_element_type=F32)
    out = out + jnp.dot(mid, mat01, preferred_element_type=F32)
    return out + jnp.dot(lo, mat01, preferred_element_type=F32)


def _tri(lower):
    r = lax.broadcasted_iota(jnp.int32, (CHUNK, CHUNK), 0)
    c = lax.broadcasted_iota(jnp.int32, (CHUNK, CHUNK), 1)
    return jnp.where(r >= c if lower else r <= c, 1.0, 0.0).astype(BF16)


def _log_sigmoid(z):
    return jnp.minimum(z, 0.0) - jnp.log(1.0 + jnp.exp(-jnp.abs(z)))


def _forget_cumsum(proj, b_pad, name):
    t = proj.shape[0]
    nb = t // CHUNK

    def body(f_ref, b_ref, c_ref):
        tri = _tri(True)
        b = b_ref[...]

        def blk(i, carry):
            rows = pl.ds(pl.multiple_of(i * CHUNK, CHUNK), CHUNK)
            cs = _dot_exact(tri, _log_sigmoid(f_ref[rows, :] + b)) + carry
            c_ref[rows, :] = cs
            return cs[CHUNK - 1:CHUNK, :]

        lax.fori_loop(0, nb, blk, jnp.zeros((1, HEAD_DIM), F32))

    return _call(body, name, (1,),
                 [pl.BlockSpec((t, HEAD_DIM), lambda i: (0, CB_F)), pl.BlockSpec((1, HEAD_DIM), lambda i: (0, 0))],
                 [pl.BlockSpec((t, HEAD_DIM), lambda i: (0, 0))], [jax.ShapeDtypeStruct((t, HEAD_DIM), F32)],
                 [proj, b_pad], sem=("arbitrary",), nbytes=2 * _nbytes((t, HEAD_DIM), F32))[0]


def _forget_bwd(proj, b_pad, dc_key, dc_query, dproj, name):
    t = proj.shape[0]
    tr = min(ROW_BLOCK, t)
    nb = t // tr
    rev = lambda i: nb - 1 - i

    def body(f_ref, b_ref, dck_ref, dcq_ref, dproj_in, df_ref, db_ref, run_ref):
        del dproj_in

        @pl.when(pl.program_id(0) == 0)
        def _():
            run_ref[...] = jnp.zeros_like(run_ref)
            db_ref[...] = jnp.zeros_like(db_ref)

        tri = _tri(False)
        b = b_ref[...]
        lane = lax.broadcasted_iota(jnp.int32, (CHUNK, HEAD_DIM), 1)
        df_ref[...] = jnp.zeros_like(df_ref)
        for n in reversed(range(tr // CHUNK)):
            rows = slice(n * CHUNK, (n + 1) * CHUNK)
            dc = dck_ref[rows, :]
            for h in range(N_HEADS):
                dc = dc + jnp.where(lane == h, dcq_ref[rows, h * HEAD_DIM:(h + 1) * HEAD_DIM], 0.0)
            dlogf = _dot_exact(tri, dc) + run_ref[...]
            run_ref[...] = dlogf[0:1, :]
            z = f_ref[rows, :] + b
            e = jnp.exp(-jnp.abs(z))
            sig_neg = jnp.where(z >= 0.0, e, 1.0) / (1.0 + e)
            df = jnp.where(lane < N_HEADS, dlogf * sig_neg, 0.0)
            df_ref[rows, :HEAD_DIM] = df.astype(BF16)
            db_ref[...] += jnp.sum(df, axis=0, keepdims=True)

    return _call(body, name, (nb,),
                 [pl.BlockSpec((tr, HEAD_DIM), lambda i: (rev(i), CB_F)), pl.BlockSpec((1, HEAD_DIM), lambda i: (0, 0)),
                  pl.BlockSpec((tr, HEAD_DIM), lambda i: (rev(i), 0)), pl.BlockSpec((tr, D_MODEL), lambda i: (rev(i), 0)),
                  pl.BlockSpec(memory_space=pl.ANY)],
                 [pl.BlockSpec((tr, F_WIDTH), lambda i: (rev(i), GATED // F_WIDTH)),
                  pl.BlockSpec((1, HEAD_DIM), lambda i: (0, 0))],
                 [jax.ShapeDtypeStruct(dproj.shape, dproj.dtype), jax.ShapeDtypeStruct((1, HEAD_DIM), F32)],
                 [proj, b_pad, dc_key, dc_query, dproj], scratch=[pltpu.VMEM((1, HEAD_DIM), F32)],
                 sem=("arbitrary",), nbytes=4 * _nbytes((tr, D_MODEL), F32), aliases={4: 0})


def _causal(block):
    r = lax.broadcasted_iota(jnp.int32, (block, block), 0)
    c = lax.broadcasted_iota(jnp.int32, (block, block), 1)
    return c <= r


def _lanes(x, width):
    return jnp.concatenate([x] * (width // HEAD_DIM), axis=1)


def _attn_fwd(proj, c_pad, c_row, name, comm=()):
    t = proj.shape[0]
    blk = min(ATTN_BLOCK, t)
    nq = t // blk
    scale = HEAD_DIM ** -0.5

    def body(q_ref, k_ref, v_ref, cp_ref, cr_ref, y_ref, e_ref):
        h, i = pl.program_id(0), pl.program_id(1)
        q = (q_ref[...] * scale).astype(BF16)
        lane = lax.broadcasted_iota(jnp.int32, (blk, HEAD_DIM), 1)
        ci = jnp.sum(jnp.where(lane == h, cp_ref[...], 0.0), axis=1, keepdims=True)

        def step(j, carry, diagonal):
            m, l, acc = carry
            off = pl.multiple_of(j * blk, blk)
            k = k_ref[pl.ds(off, blk), :].astype(BF16)
            v = v_ref[pl.ds(off, blk), :].astype(BF16)
            s = lax.dot_general(q, k, _DIMS["nt"], preferred_element_type=F32)
            s = s + ci - cr_ref[:, pl.ds(off, blk)]
            if diagonal:
                s = jnp.where(_causal(blk), s, NEG)
            m_new = jnp.maximum(m, jnp.max(s, axis=1, keepdims=True))
            alpha = jnp.exp(m - m_new)
            p = jnp.exp(s - m_new)
            l = alpha * l + jnp.sum(p, axis=1, keepdims=True)
            acc = alpha * acc + jnp.dot(p.astype(BF16), v, preferred_element_type=F32)
            return m_new, l, acc

        init = (jnp.full((blk, 1), NEG, F32), jnp.zeros((blk, 1), F32), jnp.zeros((blk, HEAD_DIM), F32))
        carry = lax.fori_loop(0, i, lambda j, c: step(j, c, False), init)
        m, l, acc = step(i, carry, True)
        y_ref[...] = acc / l
        e_ref[...] = jnp.broadcast_to(ci - (m + jnp.log(l)), (blk, HEAD_DIM))

    nbytes = 2 * _nbytes((t, HEAD_DIM), F32) + 4 * _nbytes((blk, HEAD_DIM), F32) + 4 * _nbytes((blk, blk), F32)
    head_block = pl.BlockSpec((blk, HEAD_DIM), lambda h, i: (i, h))
    return _call(body, name, (N_HEADS, nq),
                 [pl.BlockSpec((blk, HEAD_DIM), lambda h, i: (i, CB_QKV + 3 * h)),
                  pl.BlockSpec((t, HEAD_DIM), lambda h, i: (0, CB_QKV + 3 * h + 1)),
                  pl.BlockSpec((t, HEAD_DIM), lambda h, i: (0, CB_QKV + 3 * h + 2)),
                  pl.BlockSpec((blk, HEAD_DIM), lambda h, i: (i, 0)),
                  pl.BlockSpec((None, 1, t), lambda h, i: (h, 0, 0))],
                 [head_block, head_block],
                 [jax.ShapeDtypeStruct((t, D_MODEL), F32), jax.ShapeDtypeStruct((t, D_MODEL), F32)],
                 [proj, proj, proj, c_pad, c_row], sem=("parallel", "arbitrary"), nbytes=nbytes, comm=comm)


def _attn_bwd(proj, dy, e_b, delta_b, c_row, dproj, name, comm=()):
    t = proj.shape[0]
    blk = min(ATTN_BLOCK, t)
    nb = t // blk
    scale = HEAD_DIM ** -0.5

    def body(q_ref, k_ref, v_ref, dy_ref, e_ref, dl_ref, cr_ref, dproj_in, dqkv_ref, dc_ref, dr_ref, dq_acc):
        del dproj_in
        j = pl.program_id(1)

        @pl.when(j == 0)
        def _():
            dq_acc[...] = jnp.zeros_like(dq_acc)
            dr_ref[...] = jnp.zeros_like(dr_ref)

        k = k_ref[...].astype(BF16)
        v = v_ref[...].astype(BF16)
        crow = cr_ref[...]

        def step(i, carry, diagonal):
            dk, dv, dcs = carry
            rows = pl.ds(pl.multiple_of(i * blk, blk), blk)
            q = (q_ref[rows, :] * scale).astype(BF16)
            dyi = dy_ref[rows, :].astype(BF16)
            s = lax.dot_general(q, k, _DIMS["nt"], preferred_element_type=F32) + _lanes(e_ref[rows, :], blk) - crow
            if diagonal:
                s = jnp.where(_causal(blk), s, NEG)
            p = jnp.exp(s)
            dv = dv + lax.dot_general(p.astype(BF16), dyi, _DIMS["tn"], preferred_element_type=F32)
            dp = lax.dot_general(dyi, v, _DIMS["nt"], preferred_element_type=F32)
            ds = p * (dp - _lanes(dl_ref[rows, :], blk))
            dsb = ds.astype(BF16)
            dk = dk + lax.dot_general(dsb, q, _DIMS["tn"], preferred_element_type=F32)
            dq_acc[rows, :] += jnp.dot(dsb, k, preferred_element_type=F32) * scale
            dr_ref[rows, :] += jnp.sum(ds, axis=1, keepdims=True)
            return dk, dv, dcs + jnp.sum(ds, axis=0, keepdims=True)

        zeros = jnp.zeros((blk, HEAD_DIM), F32)
        carry = step(j, (zeros, zeros, jnp.zeros((1, blk), F32)), True)
        dk, dv, dcs = lax.fori_loop(j + 1, nb, lambda i, c: step(i, c, False), carry)
        mine = pl.ds(pl.multiple_of(j * blk, blk), blk)
        dqkv_ref[:, :HEAD_DIM] = dq_acc[mine, :].astype(BF16)
        dqkv_ref[:, HEAD_DIM:2 * HEAD_DIM] = dk.astype(BF16)
        dqkv_ref[:, 2 * HEAD_DIM:] = dv.astype(BF16)
        dc_ref[...] = -dcs

    full = lambda cb: pl.BlockSpec((t, HEAD_DIM), lambda h, j: (0, cb(h)))
    head = lambda h: h
    nbytes = (6 * _nbytes((t, HEAD_DIM), F32) + 4 * _nbytes((blk, HEAD_DIM), F32) + 5 * _nbytes((blk, blk), F32))
    return _call(body, name, (N_HEADS, nb),
                 [full(lambda h: CB_QKV + 3 * h),
                  pl.BlockSpec((blk, HEAD_DIM), lambda h, j: (j, CB_QKV + 3 * h + 1)),
                  pl.BlockSpec((blk, HEAD_DIM), lambda h, j: (j, CB_QKV + 3 * h + 2)),
                  full(head), full(head), full(head),
                  pl.BlockSpec((None, 1, blk), lambda h, j: (h, 0, j)),
                  pl.BlockSpec(memory_space=pl.ANY)],
                 [pl.BlockSpec((blk, QKV_WIDTH), lambda h, j: (j, (GATED + F_WIDTH) // QKV_WIDTH + h)),
                  pl.BlockSpec((None, 1, blk), lambda h, j: (h, 0, j)),
                  full(head)],
                 [jax.ShapeDtypeStruct(dproj.shape, dproj.dtype), jax.ShapeDtypeStruct((N_HEADS, 1, t), F32),
                  jax.ShapeDtypeStruct((t, D_MODEL), F32)],
                 [proj, proj, proj, dy, e_b, delta_b, c_row, dproj], scratch=[pltpu.VMEM((t, HEAD_DIM), F32)],
                 sem=("parallel", "arbitrary"), nbytes=nbytes, aliases={7: 0}, comm=comm)


def _chunks(t, blk):
    return [slice(r * blk, (r + 1) * blk) for r in range(t // blk)]


def _lane_of_head(x, h):
    lane = lax.broadcasted_iota(jnp.int32, x.shape, 1)
    return jnp.broadcast_to(jnp.sum(jnp.where(lane == h, x, 0.0), axis=1, keepdims=True), x.shape)


def _causal_t(block):
    r = lax.broadcasted_iota(jnp.int32, (block, block), 0)
    c = lax.broadcasted_iota(jnp.int32, (block, block), 1)
    return r <= c


def _attn_fwd_t(proj, c_pad, c_row, name, comm=()):
    t = proj.shape[0]
    blk = min(ATTN_BLOCK, t)
    nq = t // blk
    scale = HEAD_DIM ** -0.5

    def body(q_ref, k_ref, v_ref, cp_ref, cr_ref, y_ref, e_ref, kb_s, vt_s, cb_s):
        h, i = pl.program_id(0), pl.program_id(1)

        @pl.when(i == 0)
        def _():
            for rows in _chunks(t, blk):
                kb_s[rows, :] = k_ref[rows, :].astype(BF16)
                vt_s[:, rows] = v_ref[rows, :].T.astype(BF16)
                cb_s[rows, :] = _lane_of_head(cp_ref[rows, :], h)

        q = (q_ref[...] * scale).astype(BF16)
        ci = cr_ref[:, pl.ds(pl.multiple_of(i * blk, blk), blk)]

        def step(j, carry, diagonal):
            m, l, acc = carry
            rows = pl.ds(pl.multiple_of(j * blk, blk), blk)
            s = lax.dot_general(kb_s[rows, :], q, _DIMS["nt"], preferred_element_type=F32)
            s = s + ci - _lanes(cb_s[rows, :], blk)
            if diagonal:
                s = jnp.where(_causal_t(blk), s, NEG)
            m_new = jnp.maximum(m, jnp.max(s, axis=0, keepdims=True))
            alpha = jnp.exp(m - m_new)
            p = jnp.exp(s - m_new)
            l = alpha * l + jnp.sum(p, axis=0, keepdims=True)
            acc = alpha * acc + jnp.dot(vt_s[:, rows], p.astype(BF16), preferred_element_type=F32)
            return m_new, l, acc

        init = (jnp.full((1, blk), NEG, F32), jnp.zeros((1, blk), F32), jnp.zeros((HEAD_DIM, blk), F32))
        carry = lax.fori_loop(0, i, lambda j, c: step(j, c, False), init)
        m, l, acc = step(i, carry, True)
        y_ref[...] = (acc / l).T
        e_ref[...] = ci - (m + jnp.log(l))

    nbytes = 3 * _nbytes((t, HEAD_DIM), F32) + 4 * _nbytes((blk, HEAD_DIM), F32) + 4 * _nbytes((blk, blk), F32)
    return _call(body, name, (N_HEADS, nq),
                 [pl.BlockSpec((blk, HEAD_DIM), lambda h, i: (i, CB_QKV + 3 * h)),
                  pl.BlockSpec((t, HEAD_DIM), lambda h, i: (0, CB_QKV + 3 * h + 1)),
                  pl.BlockSpec((t, HEAD_DIM), lambda h, i: (0, CB_QKV + 3 * h + 2)),
                  pl.BlockSpec((t, HEAD_DIM), lambda h, i: (0, 0)),
                  pl.BlockSpec((None, 1, t), lambda h, i: (h, 0, 0))],
                 [pl.BlockSpec((blk, HEAD_DIM), lambda h, i: (i, h)),
                  pl.BlockSpec((None, 1, blk), lambda h, i: (h, 0, i))],
                 [jax.ShapeDtypeStruct((t, D_MODEL), F32), jax.ShapeDtypeStruct((N_HEADS, 1, t), F32)],
                 [proj, proj, proj, c_pad, c_row],
                 scratch=[pltpu.VMEM((t, HEAD_DIM), BF16), pltpu.VMEM((HEAD_DIM, t), BF16),
                          pltpu.VMEM((t, HEAD_DIM), F32)],
                 sem=("parallel", "arbitrary"), nbytes=nbytes, comm=comm)


def _attn_bwd_t(proj, dy, e_row, delta_b, c_pad, dproj, name, deps=()):
    t = proj.shape[0]
    blk = min(ATTN_BLOCK, t)
    nb = t // blk
    scale = HEAD_DIM ** -0.5

    def body(q_ref, k_ref, v_ref, dy_ref, e_ref, dl_ref, cp_ref, dproj_in, dqkv_ref, dck_ref, dcq_ref,
             qb_s, qt_s, dyb_s, dyt_s, dl_s, dqt_acc):
        del dproj_in
        h, j = pl.program_id(0), pl.program_id(1)

        @pl.when(j == 0)
        def _():
            for rows in _chunks(t, blk):
                qs = q_ref[rows, :] * scale
                qb_s[rows, :] = qs.astype(BF16)
                qt_s[:, rows] = qs.T.astype(BF16)
                dyr = dy_ref[rows, :]
                dyb_s[rows, :] = dyr.astype(BF16)
                dyt_s[:, rows] = dyr.T.astype(BF16)
                dl_s[:, rows] = dl_ref[rows, :].T[0:1, :]
            dqt_acc[...] = jnp.zeros_like(dqt_acc)
            dcq_ref[...] = jnp.zeros_like(dcq_ref)

        kf = k_ref[...]
        kb, kt = kf.astype(BF16), kf.T.astype(BF16)
        vb = v_ref[...].astype(BF16)
        cj = _lanes(_lane_of_head(cp_ref[...], h), blk)

        def step(i, carry, diagonal):
            dkt, dvt, dkey = carry
            cols = pl.ds(pl.multiple_of(i * blk, blk), blk)
            qi, dyi = qb_s[cols, :], dyb_s[cols, :]
            s = lax.dot_general(kb, qi, _DIMS["nt"], preferred_element_type=F32) + e_ref[:, cols] - cj
            if diagonal:
                s = jnp.where(_causal_t(blk), s, NEG)
            p = jnp.exp(s)
            dp = lax.dot_general(vb, dyi, _DIMS["nt"], preferred_element_type=F32)
            ds = p * (dp - dl_s[:, cols])
            pb, dsb = p.astype(BF16), ds.astype(BF16)
            dvt = dvt + lax.dot_general(dyt_s[:, cols], pb, _DIMS["nt"], preferred_element_type=F32)
            dkt = dkt + lax.dot_general(qt_s[:, cols], dsb, _DIMS["nt"], preferred_element_type=F32)
            dqt_acc[:, cols] += jnp.dot(kt, dsb, preferred_element_type=F32) * scale
            dcq_ref[:, cols] += jnp.sum(ds, axis=0, keepdims=True)
            part = ds[:, :HEAD_DIM]
            for g in range(1, blk // HEAD_DIM):
                part = part + ds[:, g * HEAD_DIM:(g + 1) * HEAD_DIM]
            return dkt, dvt, dkey + part

        zeros = jnp.zeros((HEAD_DIM, blk), F32)
        carry = step(j, (zeros, zeros, jnp.zeros((blk, HEAD_DIM), F32)), True)
        dkt, dvt, dkey = lax.fori_loop(j + 1, nb, lambda i, c: step(i, c, False), carry)
        mine = pl.ds(pl.multiple_of(j * blk, blk), blk)
        dqkv_ref[:, :HEAD_DIM] = dqt_acc[:, mine].T.astype(BF16)
        dqkv_ref[:, HEAD_DIM:2 * HEAD_DIM] = dkt.T.astype(BF16)
        dqkv_ref[:, 2 * HEAD_DIM:] = dvt.T.astype(BF16)
        dck_ref[...] = -jnp.sum(dkey.T, axis=0, keepdims=True)

    full = lambda cb: pl.BlockSpec((t, HEAD_DIM), lambda h, j: (0, cb(h)))
    head = lambda h: h
    row = pl.BlockSpec((None, 1, t), lambda h, j: (h, 0, 0))
    nbytes = (8 * _nbytes((t, HEAD_DIM), F32) + 4 * _nbytes((blk, HEAD_DIM), F32) + 6 * _nbytes((blk, blk), F32))
    return _call(body, name, (N_HEADS, nb),
                 [full(lambda h: CB_QKV + 3 * h),
                  pl.BlockSpec((blk, HEAD_DIM), lambda h, j: (j, CB_QKV + 3 * h + 1)),
                  pl.BlockSpec((blk, HEAD_DIM), lambda h, j: (j, CB_QKV + 3 * h + 2)),
                  full(head), row, full(head),
                  pl.BlockSpec((blk, HEAD_DIM), lambda h, j: (j, 0)),
                  pl.BlockSpec(memory_space=pl.ANY)],
                 [pl.BlockSpec((blk, QKV_WIDTH), lambda h, j: (j, (GATED + F_WIDTH) // QKV_WIDTH + h)),
                  pl.BlockSpec((None, 1, blk), lambda h, j: (h, 0, j)),
                  row],
                 [jax.ShapeDtypeStruct(dproj.shape, dproj.dtype), jax.ShapeDtypeStruct((N_HEADS, 1, t), F32),
                  jax.ShapeDtypeStruct((N_HEADS, 1, t), F32)],
                 [proj, proj, proj, dy, e_row, delta_b, c_pad, dproj],
                 scratch=[pltpu.VMEM((t, HEAD_DIM), BF16), pltpu.VMEM((HEAD_DIM, t), BF16),
                          pltpu.VMEM((t, HEAD_DIM), BF16), pltpu.VMEM((HEAD_DIM, t), BF16),
                          pltpu.VMEM((1, t), F32), pltpu.VMEM((HEAD_DIM, t), F32)],
                 sem=("parallel", "arbitrary"), nbytes=nbytes, aliases={7: 0}, deps=deps)


def _forget_bwd_t(proj, b_pad, dc, dproj, name):
    t = proj.shape[0]
    tr = min(ROW_BLOCK, t)
    nb = t // tr
    rev = lambda i: nb - 1 - i

    def body(f_ref, b_ref, dc_ref, dproj_in, df_ref, db_ref, run_ref):
        del dproj_in

        @pl.when(pl.program_id(0) == 0)
        def _():
            run_ref[...] = jnp.zeros_like(run_ref)
            db_ref[...] = jnp.zeros_like(db_ref)

        tri = _tri(False)
        b = b_ref[...]
        lane = lax.broadcasted_iota(jnp.int32, (CHUNK, HEAD_DIM), 1)
        df_ref[...] = jnp.zeros_like(df_ref)
        for n in reversed(range(tr // CHUNK)):
            rows = slice(n * CHUNK, (n + 1) * CHUNK)
            dlogf = _dot_exact(tri, dc_ref[rows, :]) + run_ref[...]
            run_ref[...] = dlogf[0:1, :]
            z = f_ref[rows, :] + b
            e = jnp.exp(-jnp.abs(z))
            sig_neg = jnp.where(z >= 0.0, e, 1.0) / (1.0 + e)
            df = jnp.where(lane < N_HEADS, dlogf * sig_neg, 0.0)
            df_ref[rows, :HEAD_DIM] = df.astype(BF16)
            db_ref[...] += jnp.sum(df, axis=0, keepdims=True)

    return _call(body, name, (nb,),
                 [pl.BlockSpec((tr, HEAD_DIM), lambda i: (rev(i), CB_F)), pl.BlockSpec((1, HEAD_DIM), lambda i: (0, 0)),
                  pl.BlockSpec((tr, HEAD_DIM), lambda i: (rev(i), 0)), pl.BlockSpec(memory_space=pl.ANY)],
                 [pl.BlockSpec((tr, F_WIDTH), lambda i: (rev(i), GATED // F_WIDTH)),
                  pl.BlockSpec((1, HEAD_DIM), lambda i: (0, 0))],
                 [jax.ShapeDtypeStruct(dproj.shape, dproj.dtype), jax.ShapeDtypeStruct((1, HEAD_DIM), F32)],
                 [proj, b_pad, dc, dproj], scratch=[pltpu.VMEM((1, HEAD_DIM), F32)],
                 sem=("arbitrary",), nbytes=4 * _nbytes((tr, HEAD_DIM), F32), aliases={3: 0})


_GELU_K = math.sqrt(2.0 / math.pi)
_GELU_C = 0.044715


def _gelu(x):
    t = jnp.tanh(_GELU_K * (x + _GELU_C * (x * x * x)))
    return 0.5 * x * (1.0 + t), t


def _gelu_grad(x, t):
    return 0.5 * (1.0 + t) + 0.5 * x * (1.0 - t * t) * (_GELU_K * (1.0 + 3.0 * _GELU_C * (x * x)))


def _layernorm_stats(a):
    mu = jnp.mean(a, axis=-1, keepdims=True)
    xc = a - mu
    r = lax.rsqrt(jnp.mean(xc * xc, axis=-1, keepdims=True) + EPS)
    return xc * r, r


def _group(g):
    return slice(g * CHUNK, (g + 1) * CHUNK)


def _gated_cols(k):
    return pl.BlockSpec((CHUNK, D_MODEL), lambda i: (i, k))


def _fixed(shape):
    return pl.BlockSpec(shape, lambda i: (0,) * len(shape))


def _mix_fwd(proj, y_b, w_s, b_cols, g_v, name):
    t = proj.shape[0]

    def body(u_ref, vs_ref, ga_ref, gb_ref, yb_ref, w_ref, b_ref, gv_ref, o_ref):
        a_u, _ = _gelu(u_ref[...])
        a_v, _ = _gelu(vs_ref[...])
        xhat, _ = _layernorm_stats(a_v)
        vn = (xhat * gv_ref[...]).astype(BF16)
        sa, sb = _sigmoid(ga_ref[...]), _sigmoid(gb_ref[...])
        yb = yb_ref[...]
        mask = _causal(CHUNK)
        for g in range(N_GROUPS):
            cols = _group(g)
            w = jnp.where(mask, w_ref[g], 0.0).astype(BF16)
            mixed = jnp.dot(w, vn[:, cols], preferred_element_type=F32) + b_ref[:, g:g + 1]
            o_ref[:, cols] = (sa[:, cols] * (a_u[:, cols] * mixed) + sb[:, cols] * yb[:, cols]).astype(BF16)

    return _call(body, name, (t // CHUNK,),
                 [_gated_cols(0), _gated_cols(1), _gated_cols(2), _gated_cols(3), _row_spec(CHUNK, D_MODEL),
                  _fixed((N_GROUPS, CHUNK, CHUNK)), _fixed((CHUNK, N_GROUPS)), _fixed((1, D_MODEL))],
                 [_row_spec(CHUNK, D_MODEL)], [jax.ShapeDtypeStruct((t, D_MODEL), BF16)],
                 [proj, proj, proj, proj, y_b, w_s, b_cols, g_v], sem=("parallel",),
                 nbytes=16 * _nbytes((CHUNK, D_MODEL), F32))[0]


def _mix_bwd(proj, y_b, dmerged, w_s, b_cols, g_v, sel, name, deps=()):
    t = proj.shape[0]

    def body(u_ref, vs_ref, ga_ref, gb_ref, yb_ref, dm_ref, w_ref, b_ref, gv_ref, sel_ref,
             dg_ref, dyb_ref, delta_ref, dw_ref, dbt_ref, dgv_ref):
        @pl.when(pl.program_id(0) == 0)
        def _():
            dw_ref[...] = jnp.zeros_like(dw_ref)
            dbt_ref[...] = jnp.zeros_like(dbt_ref)
            dgv_ref[...] = jnp.zeros_like(dgv_ref)

        u, vs = u_ref[...], vs_ref[...]
        a_u, t_u = _gelu(u)
        a_v, t_v = _gelu(vs)
        xhat, r = _layernorm_stats(a_v)
        gv = gv_ref[...]
        vn = (xhat * gv).astype(BF16)
        sa, sb = _sigmoid(ga_ref[...]), _sigmoid(gb_ref[...])
        yb, dm = yb_ref[...], dm_ref[...].astype(F32)
        dyb = dm * sb
        dyb_ref[...] = dyb
        dyb_yb = dyb * yb
        dg_ref[:, 3 * D_MODEL:] = (dm * yb * (sb * (1.0 - sb))).astype(BF16)
        dya = dm * sa
        mask = _causal(CHUNK)
        dmixed_parts, dvn_parts = [], []
        for g in range(N_GROUPS):
            cols = _group(g)
            delta_ref[:, cols] = jnp.broadcast_to(jnp.sum(dyb_yb[:, cols], axis=1, keepdims=True), (CHUNK, CHUNK))
            w = jnp.where(mask, w_ref[g], 0.0).astype(BF16)
            mixed = jnp.dot(w, vn[:, cols], preferred_element_type=F32) + b_ref[:, g:g + 1]
            y_a = a_u[:, cols] * mixed
            dg_ref[:, 2 * D_MODEL + g * CHUNK:2 * D_MODEL + (g + 1) * CHUNK] = (
                dm[:, cols] * y_a * (sa[:, cols] * (1.0 - sa[:, cols]))).astype(BF16)
            dg_ref[:, cols] = (dya[:, cols] * mixed * _gelu_grad(u[:, cols], t_u[:, cols])).astype(BF16)
            dmixed = dya[:, cols] * a_u[:, cols]
            dmb = dmixed.astype(BF16)
            dw = lax.dot_general(dmb, vn[:, cols], _DIMS["nt"], preferred_element_type=F32)
            dw_ref[g] += jnp.where(mask, dw, 0.0)
            dvn_parts.append(lax.dot_general(w, dmb, _DIMS["tn"], preferred_element_type=F32))
            dmixed_parts.append(dmixed)
        dmixed_all = jnp.concatenate(dmixed_parts, axis=1)
        dvn = jnp.concatenate(dvn_parts, axis=1)
        dbt_ref[...] += _dot_exact_rhs(dmixed_all, sel_ref[...])
        dgv_ref[...] += jnp.sum(dvn * xhat, axis=0, keepdims=True)
        dxh = dvn * gv
        da_v = r * (dxh - jnp.mean(dxh, axis=-1, keepdims=True)
                    - xhat * jnp.mean(dxh * xhat, axis=-1, keepdims=True))
        dg_ref[:, D_MODEL:2 * D_MODEL] = (da_v * _gelu_grad(vs, t_v)).astype(BF16)

    row = lambda width: _row_spec(CHUNK, width)
    return _call(body, name, (t // CHUNK,),
                 [_gated_cols(0), _gated_cols(1), _gated_cols(2), _gated_cols(3), row(D_MODEL), row(D_MODEL),
                  _fixed((N_GROUPS, CHUNK, CHUNK)), _fixed((CHUNK, N_GROUPS)), _fixed((1, D_MODEL)),
                  _fixed((D_MODEL, HEAD_DIM))],
                 [row(GATED), row(D_MODEL), row(D_MODEL),
                  _fixed((N_GROUPS, CHUNK, CHUNK)), _fixed((CHUNK, HEAD_DIM)), _fixed((1, D_MODEL))],
                 [jax.ShapeDtypeStruct((t, IN_PAD), BF16), jax.ShapeDtypeStruct((t, D_MODEL), F32),
                  jax.ShapeDtypeStruct((t, D_MODEL), F32), jax.ShapeDtypeStruct((N_GROUPS, CHUNK, CHUNK), F32),
                  jax.ShapeDtypeStruct((CHUNK, HEAD_DIM), F32), jax.ShapeDtypeStruct((1, D_MODEL), F32)],
                 [proj, proj, proj, proj, y_b, dmerged, w_s, b_cols, g_v, sel], sem=("arbitrary",),
                 nbytes=40 * _nbytes((CHUNK, D_MODEL), F32), deps=deps)


def _adamw_update(w_ref, g_ref, m_ref, v_ref, d_ref, nm_ref, nv_ref):
    gv = g_ref[...]
    nm = ADAM_B1 * m_ref[...] + (1.0 - ADAM_B1) * gv
    nv = ADAM_B2 * v_ref[...] + (1.0 - ADAM_B2) * (gv * gv)
    m_hat = nm / (1.0 - ADAM_B1 ** ADAM_STEP)
    v_hat = nv / (1.0 - ADAM_B2 ** ADAM_STEP)
    d_ref[...] = -ADAM_LR * (m_hat / (jnp.sqrt(v_hat) + ADAM_EPS) + ADAM_WD * w_ref[...])
    nm_ref[...] = nm
    nv_ref[...] = nv


def _adamw(w, g, m, v, name):
    r, c = w.shape
    tr = _tile(r, 256, unit=8)

    def body(w_ref, g_ref, m_ref, v_ref, d_ref, nm_ref, nv_ref):
        _adamw_update(w_ref, g_ref, m_ref, v_ref, d_ref, nm_ref, nv_ref)

    spec = pl.BlockSpec((tr, c), lambda i: (i, 0))
    shape = jax.ShapeDtypeStruct((r, c), F32)
    return _call(body, name, (r // tr,), [spec] * 4, [spec] * 3, [shape] * 3, [w, g, m, v], sem=("parallel",),
                 nbytes=7 * _nbytes((tr, max(c, 128)), F32))


def _adamw_cols(w, g, m, v, name):
    layers, r, c = w.shape
    tc = 256

    def body(w_ref, g_ref, m_ref, v_ref, d_ref, nm_ref, nv_ref):
        _adamw_update(w_ref, g_ref, m_ref, v_ref, d_ref, nm_ref, nv_ref)

    spec = pl.BlockSpec((None, r, tc), lambda l, j: (l, 0, j))
    shape = jax.ShapeDtypeStruct(w.shape, F32)
    return _call(body, name, (layers, c // tc), [spec] * 4, [spec] * 3, [shape] * 3, [w, g, m, v],
                 sem=("parallel", "parallel"), nbytes=7 * _nbytes((r, tc), F32))


def _pad_rows(a, rows):
    return jnp.pad(a, ((0, rows - a.shape[0]), (0, 0)))


def _pack_rest(w_out, w_gate, w_up, w_down):
    return jnp.concatenate([w_out, w_gate.T, w_up.T, w_down], axis=0).astype(BF16)


def _unpack_in(gathered):
    i = jnp.arange(16)[None, :, None]
    b = jnp.arange(N_DEV)[:, None, None]
    head = jnp.where(i < b, jnp.roll(gathered[:, IN_STRIDE:IN_STRIDE + 16], 1, axis=0), gathered[:, :16])
    nat = jnp.concatenate([head, gathered[:, 16:IN_STRIDE]], axis=1).reshape(N_DEV * IN_STRIDE, D_MODEL)
    qkv = nat[2 * D_MODEL:5 * D_MODEL].reshape(3, N_HEADS, HEAD_DIM, D_MODEL)
    qkv = jnp.transpose(qkv, (1, 0, 2, 3)).reshape(3 * D_MODEL, D_MODEL)
    f = _pad_rows(gathered[N_DEV - 1, IN_STRIDE:IN_STRIDE + N_HEADS], F_WIDTH)
    return jnp.concatenate([nat[:2 * D_MODEL], nat[5 * D_MODEL:7 * D_MODEL], f, qkv], axis=0)


def _unpack_rest(gathered):
    take = lambda lo, n: gathered[:, lo:lo + n, :].reshape(N_DEV * n, D_MODEL)
    return take(R_OUT, SH_OUT), jnp.concatenate([take(R_G, SH_FF), take(R_U, SH_FF)], axis=0), take(R_D, SH_FF)


def _pack_grad_in(dw_in_t):
    qkv = dw_in_t[GATED + F_WIDTH:].reshape(N_HEADS, 3, HEAD_DIM, D_MODEL)
    qkv = jnp.transpose(qkv, (1, 0, 2, 3)).reshape(3 * D_MODEL, D_MODEL)
    return jnp.concatenate([dw_in_t[:2 * D_MODEL], qkv, dw_in_t[2 * D_MODEL:GATED],
                            dw_in_t[GATED:GATED + IN_ROWS - N_DEV * IN_STRIDE]], axis=0)


def _pack_grad_rest(dw_out, dw_gu_t, dw_down):
    split = lambda a, n: a.reshape(N_DEV, n, D_MODEL)
    return jnp.concatenate([split(dw_out, SH_OUT), split(dw_gu_t[:D_FF], SH_FF), split(dw_gu_t[D_FF:], SH_FF),
                            split(dw_down, SH_FF)], axis=1)


def kernel(x, mix_pre_g, w_in, b_forget, sgu_norm_g, w_spatial, b_spatial, w_out, mix_post_g, ffn_pre_g, w_gate, w_up, w_down, ffn_post_g, loss_target, m_mix_pre_g, m_w_in, m_b_forget, m_sgu_norm_g, m_w_spatial, m_b_spatial, m_w_out, m_mix_post_g, m_ffn_pre_g, m_w_gate, m_w_up, m_w_down, m_ffn_post_g, v_mix_pre_g, v_w_in, v_b_forget, v_sgu_norm_g, v_w_spatial, v_b_spatial, v_w_out, v_mix_post_g, v_ffn_pre_g, v_w_gate, v_w_up, v_w_down, v_ffn_post_g):
    depth = w_in.shape[0]
    assert depth == 2
    t = x.shape[1]
    x0 = x.reshape(t, D_MODEL)
    target = loss_target.reshape(t, D_MODEL)
    sel = (jnp.arange(D_MODEL)[:, None] // HEAD_DIM == jnp.arange(HEAD_DIM)[None, :]).astype(BF16)
    vec = lambda a, l: a[l][None, :]
    zero_row = jnp.zeros((1, D_MODEL), F32)

    me = 4 * lax.axis_index("x") + 2 * lax.axis_index("y") + lax.axis_index("c")
    p_in = [lax.dynamic_update_slice(jnp.zeros((P_IN, D_MODEL), BF16), w_in[l].T.astype(BF16), (me, 0))
            for l in range(depth)]
    p_rest = [[w_out[l].astype(BF16), jnp.stack([w_gate[l].T, w_up[l].T]).astype(BF16), w_down[l].astype(BF16)]
              for l in range(depth)]
    w_in_t = [None] * depth
    w_rest = [None] * depth
    w_in_t[0] = _unpack_in(_exchange(("gather", 0), p_in[0], "gather_w_in_0")[0])

    saved = []
    xl = x0
    h = _rms_fwd(xl, vec(mix_pre_g, 0), "rms_in_0")
    dy = loss = None
    for l in range(depth):
        b_pad = jnp.pad(b_forget[l], (0, HEAD_DIM - N_HEADS))[None, :]
        b_cols = b_spatial[l].T
        proj = _matmul(h, w_in_t[l], "nt", F32, f"proj_{l}")
        c_pad = _forget_cumsum(proj, b_pad, f"forget_cumsum_{l}")
        c_row = c_pad[:, :N_HEADS].T[:, None, :]
        riders = [(("gather", 0), p_rest[l][0]), (("gather", 1), p_rest[l][1]), (("gather", 0), p_rest[l][2])]
        riders += [(("gather", 0), p_in[l + 1])] if l + 1 < depth else []
        y_b, e_row, *arrived = _attn_fwd_t(proj, c_pad, c_row, f"attn_fwd_{l}", comm=riders)
        w_rest[l] = (arrived[0].reshape(D_MODEL, D_MODEL), arrived[1].reshape(2 * D_FF, D_MODEL),
                     arrived[2].reshape(D_FF, D_MODEL))
        if l + 1 < depth:
            w_in_t[l + 1] = _unpack_in(arrived[3])
        w_o, w_gu_t, w_d = w_rest[l]
        merged = _mix_fwd(proj, y_b, w_spatial[l], b_cols, vec(sgu_norm_g, l), f"mix_fwd_{l}")
        o = _matmul(merged, w_o, "nn", F32, f"out_proj_{l}")
        x1, h2 = _post_norm_fwd(xl, o, vec(mix_post_g, l), vec(ffn_pre_g, l), f"post_mix_{l}")
        gu = _matmul(h2, w_gu_t, "nt", F32, f"ffn_gu_{l}")
        act = _swiglu_fwd(gu, f"swiglu_fwd_{l}")
        dn = _matmul(act, w_d, "nn", F32, f"ffn_down_{l}")
        saved.append(dict(x0=xl, h=h, proj=proj, b_pad=b_pad, b_cols=b_cols, c_pad=c_pad, y_b=y_b, e_row=e_row,
                          merged=merged, o=o, x1=x1, h2=h2, gu=gu, act=act, dn=dn))
        if l + 1 < depth:
            xl, h = _post_norm_fwd(x1, dn, vec(ffn_post_g, l), vec(mix_pre_g, l + 1), f"post_ffn_{l}")
        else:
            dy, loss = _post_norm_loss(x1, dn, vec(ffn_post_g, l), target, "loss")

    g_small = [None] * depth
    flights = {}
    late_token = None
    dx = dy
    dd, dg_ffn_post = _norm_bwd(dx, f"bwd_ffn_post_{depth - 1}", post=(saved[-1]["dn"], vec(ffn_post_g, depth - 1)))
    for l in reversed(range(depth)):
        s = saved[l]
        w_o, w_gu_t, w_d = w_rest[l]
        dact = _matmul(dd, w_d, "nt", F32, f"d_act_{l}")
        dw_d = _matmul(s["act"], dd, "tn", BF16, f"dw_down_{l}")
        dgu = _swiglu_bwd(s["gu"], dact, f"swiglu_bwd_{l}")
        dh2 = _matmul(dgu, w_gu_t, "nn", F32, f"d_h2_{l}")
        dw_gu_t = _matmul(dgu, s["h2"], "tn", BF16, f"dw_gu_{l}")
        dx1, dg_ffn_pre, do, dg_mix_post = _norm_bwd(
            dx, f"bwd_post_mix_{l}", pre=(dh2, s["x1"], vec(ffn_pre_g, l)), post=(s["o"], vec(mix_post_g, l)))
        dmerged = _matmul(do, w_o, "nt", F32, f"d_merged_{l}")
        dw_o = _matmul(s["merged"], do, "tn", BF16, f"dw_out_{l}")
        tokens = []
        for key, dw, base, rows in [("out", dw_o, 0, SH_OUT), ("gate", dw_gu_t, 0, SH_FF), ("up", dw_gu_t, D_FF, SH_FF),
                                    ("down", dw_d, 0, SH_FF)]:
            flights[(key, l)], token = _exchange_start(("windows", base, rows, rows), dw, f"start_grad_{key}_{l}")
            tokens.append(token)
        dproj, dyb, delta_b, dw_s, dbt, dg_v = _mix_bwd(
            s["proj"], s["y_b"], dmerged, w_spatial[l], s["b_cols"], vec(sgu_norm_g, l), sel, f"mix_bwd_{l}",
            deps=tokens)
        db_s = dbt[:, :N_GROUPS].T.reshape(1, D_MODEL)
        tokens = []
        if l == 0:
            g_small[0] = jnp.concatenate([zero_row, dg_v, dg_mix_post, dg_ffn_pre, dg_ffn_post, db_s, zero_row,
                                          zero_row, dw_s.reshape(CHUNK, D_MODEL)], axis=0)
            flights["small"], token = _exchange_start(("spread",),jnp.concatenate([g_small[1], g_small[0]], axis=0),
                                                      "start_small_grads")
            tokens = [token]
        dproj, dc_key, dc_query = _attn_bwd_t(s["proj"], dyb, s["e_row"], delta_b, s["c_pad"], dproj,
                                              f"attn_bwd_{l}", deps=tokens)
        dc = jnp.pad((dc_key + dc_query)[:, 0, :].T, ((0, 0), (0, HEAD_DIM - N_HEADS)))
        dproj, db_f = _forget_bwd_t(s["proj"], s["b_pad"], dc, dproj, f"forget_bwd_{l}")
        db_f_row = jnp.pad(db_f, ((0, 0), (0, D_MODEL - HEAD_DIM)))
        dw_in_t = _matmul(dproj, s["h"], "tn", BF16, f"dw_in_{l}")
        flights[("in", l)], token = _exchange_start(("windows", 0, IN_STRIDE, P_IN), _pack_grad_in(dw_in_t),
                                                    f"start_grad_in_{l}")
        dh = _matmul(dproj, w_in_t[l], "nn", F32, f"d_h_{l}", deps=[token])
        if l > 0:
            dx, dg_mix_pre, dd, dg_ffn_post_below = _norm_bwd(
                dx1, f"bwd_pre_mix_{l}", pre=(dh, s["x0"], vec(mix_pre_g, l)),
                post=(saved[l - 1]["dn"], vec(ffn_post_g, l - 1)))
            g_small[l] = jnp.concatenate([dg_mix_pre, dg_v, dg_mix_post, dg_ffn_pre, dg_ffn_post, db_s, db_f_row,
                                          zero_row, dw_s.reshape(CHUNK, D_MODEL)], axis=0)
            dg_ffn_post = dg_ffn_post_below
        else:
            dx, dg_mix_pre = _norm_bwd(dx1, f"bwd_pre_mix_{l}", pre=(dh, s["x0"], vec(mix_pre_g, l)))
            late_rows = jnp.concatenate([dg_mix_pre, db_f_row] + [zero_row] * 6, axis=0)
            flights["late"], late_token = _exchange_start(("spread",),late_rows, "start_late_small_grads")
    grad_x = dx.reshape(x.shape)
    loss = lax.psum(loss.reshape(()), ("x", "y", "c"))

    landed, after = {}, [grad_x, late_token]
    rest = ["out", "gate", "up", "down"]
    for key in [(n, 1) for n in rest] + [("in", 1)] + [(n, 0) for n in rest] + ["small"]:
        label = key if isinstance(key, str) else f"grad_{key[0]}_{key[1]}"
        landed[key] = _exchange_wait(flights[key], after, f"wait_{label}")
        after = [landed[key]]
    small_sum = _sum_slots(landed["small"], S_ROWS, "sum_small_grads")
    sm = small_sum.reshape(depth, S_ROWS, D_MODEL)[::-1]
    g = {
        "sgu_norm_g": sm[:, 1], "mix_post_g": sm[:, 2],
        "ffn_pre_g": sm[:, 3], "ffn_post_g": sm[:, 4], "b_spatial": sm[:, 5].reshape(depth, N_GROUPS, CHUNK),
        "w_spatial": sm[:, 8:].reshape(depth, N_GROUPS, CHUNK, CHUNK),
    }
    for n in rest:
        g["w_" + n] = jnp.stack([_sum_slots(landed[(n, l)], landed[(n, l)].shape[1], f"sum_grad_{n}_{l}")
                                 for l in range(depth)])

    names = ["mix_pre_g", "w_in", "b_forget", "sgu_norm_g", "w_spatial", "b_spatial", "w_out", "mix_post_g",
             "ffn_pre_g", "w_gate", "w_up", "w_down", "ffn_post_g"]
    ws = dict(mix_pre_g=mix_pre_g, w_in=w_in, b_forget=b_forget, sgu_norm_g=sgu_norm_g, w_spatial=w_spatial,
              b_spatial=b_spatial, w_out=w_out, mix_post_g=mix_post_g, ffn_pre_g=ffn_pre_g, w_gate=w_gate, w_up=w_up,
              w_down=w_down, ffn_post_g=ffn_post_g)
    ms = dict(mix_pre_g=m_mix_pre_g, w_in=m_w_in, b_forget=m_b_forget, sgu_norm_g=m_sgu_norm_g, w_spatial=m_w_spatial,
              b_spatial=m_b_spatial, w_out=m_w_out, mix_post_g=m_mix_post_g, ffn_pre_g=m_ffn_pre_g, w_gate=m_w_gate,
              w_up=m_w_up, w_down=m_w_down, ffn_post_g=m_ffn_post_g)
    vs = dict(mix_pre_g=v_mix_pre_g, w_in=v_w_in, b_forget=v_b_forget, sgu_norm_g=v_sgu_norm_g, w_spatial=v_w_spatial,
              b_spatial=v_b_spatial, w_out=v_w_out, mix_post_g=v_mix_post_g, ffn_pre_g=v_ffn_pre_g, w_gate=v_w_gate,
              w_up=v_w_up, w_down=v_w_down, ffn_post_g=v_ffn_post_g)
    updates = {}
    last = ["w_in", "mix_pre_g", "b_forget"]
    transposed = ["w_in", "w_gate", "w_up"]
    for n in [n for n in names if n not in last] + last:
        if n == last[0]:
            sum_in_1 = _sum_slots(landed[("in", 1)], 304, "sum_grad_in_1")
            done = [u[0] for u in updates.values()] + [sum_in_1]
            sum_in_0 = _sum_slots(_exchange_wait(flights[("in", 0)], done, "wait_grad_in_0"), 304, "sum_grad_in_0")
            late = _sum_slots(_exchange_wait(flights["late"], [sum_in_0], "wait_late"), 8, "sum_late_small_grads")
            g["w_in"] = lax.dynamic_slice(jnp.stack([sum_in_0, sum_in_1]), (0, me, 0), (depth, SH_IN, D_MODEL))
            g["mix_pre_g"] = jnp.stack([late[0], sm[1, 0]])
            g["b_forget"] = jnp.stack([late[1, :N_HEADS], sm[1, 6, :N_HEADS]])
        if n in transposed:
            view = lambda a: jnp.swapaxes(a, 1, 2)
            updates[n] = [view(u) for u in _adamw_cols(view(ws[n]), g[n], view(ms[n]), view(vs[n]), f"adamw_{n}")]
            g[n] = view(g[n])
        else:
            flat = (-1, ws[n].shape[-1])
            updates[n] = [u.reshape(ws[n].shape) for u in
                          _adamw(ws[n].reshape(flat), g[n].reshape(flat), ms[n].reshape(flat), vs[n].reshape(flat),
                                 f"adamw_{n}")]
    deltas = [updates[n][0] for n in names]
    new_m = [updates[n][1] for n in names]
    new_v = [updates[n][2] for n in names]
    grads = [g[n].reshape(ws[n].shape) for n in names]
    return (loss, grad_x, *grads, *deltas, *new_m, *new_v)
```

```python
import math

import jax
import jax.numpy as jnp
from jax import lax
from jax.experimental import pallas as pl
from jax.experimental.pallas import tpu as pltpu

F32 = jnp.float32
BF16 = jnp.bfloat16

N_DEV = 8
D_MODEL = 1024
N_HEADS = 8
HEAD_DIM = 128
CHUNK = 128
N_GROUPS = 8
D_FF = 2816
IN_WIDTH = 7 * D_MODEL + N_HEADS
IN_PAD = 7680
EPS = 1e-6
GATED = 4 * D_MODEL
F_WIDTH = 512
QKV_WIDTH = 3 * HEAD_DIM
CB_F = GATED // HEAD_DIM
CB_QKV = (GATED + F_WIDTH) // HEAD_DIM
assert GATED + F_WIDTH + N_HEADS * QKV_WIDTH == IN_PAD and (GATED + F_WIDTH) % QKV_WIDTH == 0

ADAM_LR, ADAM_B1, ADAM_B2, ADAM_EPS, ADAM_WD, ADAM_STEP = 0.001, 0.9, 0.999, 1e-08, 0.01, 10

SH_IN = IN_WIDTH // N_DEV
SH_OUT = D_MODEL // N_DEV
SH_FF = D_FF // N_DEV
P_IN = 912
IN_STRIDE = 7 * HEAD_DIM
IN_ROWS = IN_STRIDE * (N_DEV - 1) + P_IN
R_OUT, R_G, R_U, R_D = 0, SH_OUT, SH_OUT + SH_FF, SH_OUT + 2 * SH_FF
P_REST = SH_OUT + 3 * SH_FF
S_ROWS = 136

ATTN_BLOCK = 512
ROW_BLOCK = 512
MM_TM, MM_TN, MM_TK = 1536, 1536, 2048
VMEM_CAP = 56 << 20
NEG = -1e30


def _tile(n, cap, unit=128):
    if n <= cap:
        return n
    best = None
    for t in range(unit, cap + 1, unit):
        if n % t == 0:
            best = t
    assert best is not None, (n, cap)
    return best


def _nbytes(shape, dtype):
    return math.prod(shape) * jnp.dtype(dtype).itemsize


_HBM = pl.BlockSpec(memory_space=pltpu.HBM)
MESH = pl.DeviceIdType.MESH

_N_REMOTE = N_DEV - 1


def _exchange_shapes(kind, x):
    if kind[0] == "gather":
        shape = x.shape[:kind[1]] + (N_DEV,) + x.shape[kind[1]:]
    elif kind[0] == "spread":
        shape = (N_DEV,) + x.shape
    else:
        shape = (N_DEV, kind[3]) + x.shape[1:]
    return [jax.ShapeDtypeStruct(shape, x.dtype)]


def _exchange_sems(kind):
    del kind
    return [pltpu.SemaphoreType.DMA((_N_REMOTE,)), pltpu.SemaphoreType.DMA((_N_REMOTE,)), pltpu.SemaphoreType.DMA]


def _plan(kind, x_ref, outs, sems):
    send_sems, recv_sems, local_sem = sems
    x, y, c = lax.axis_index("x"), lax.axis_index("y"), lax.axis_index("c")

    def remote(src, dst, k, to):
        return pltpu.make_async_remote_copy(src_ref=src, dst_ref=dst, send_sem=send_sems.at[k], recv_sem=recv_sems.at[k],
                                            device_id=to, device_id_type=MESH)

    sibling = (x, y, 1 - c)
    if kind[0] == "gather":
        out, = outs
        slot = lambda px, py, pc: out.at[(slice(None),) * kind[1] + (4 * px + 2 * py + pc,)]
        chips = [(1 - x, y), (x, 1 - y), (1 - x, 1 - y)]
        local = [pltpu.make_async_copy(x_ref, slot(x, y, c), local_sem)]
        first = [remote(x_ref, slot(x, y, c), 0, sibling)]
        first += [remote(x_ref, slot(x, y, c), 1 + k, (*chip, c)) for k, chip in enumerate(chips)]
        relays = [(remote(x_ref, slot(*chip, c), 1 + k, (*chip, c)), remote(slot(*chip, c), slot(*chip, c), 4 + k, sibling))
                  for k, chip in enumerate(chips)]
        arrivals = [remote(x_ref, slot(x, y, 1 - c), 0, sibling)]
        arrivals += [remote(x_ref, slot(*chip, 1 - c), 4 + k, sibling) for k, chip in enumerate(chips)]
        return local, first, relays, arrivals
    out, = outs
    mine = 4 * x + 2 * y + c
    if kind[0] == "spread":
        src = lambda d: x_ref
    else:
        _, base, stride, length = kind
        src = lambda d: x_ref.at[pl.ds(pl.multiple_of(base + stride * d, 16), length)]
    first, arrivals = [], []
    for j in range(1, N_DEV):
        peer = (1 - x if j & 4 else x, 1 - y if j & 2 else y, 1 - c if j & 1 else c)
        theirs = 4 * peer[0] + 2 * peer[1] + peer[2]
        first.append(remote(src(theirs), out.at[mine], j - 1, peer))
        arrivals.append(remote(src(theirs), out.at[theirs], j - 1, peer))
    local = [] if local_sem is None else [pltpu.make_async_copy(src(mine), out.at[mine], local_sem)]
    return local, first, [], arrivals


def _start(plan):
    local, first, _, _ = plan
    for cp in local + first:
        cp.start()


def _finish(plan):
    local, first, relays, arrivals = plan
    for arrival, onward in relays:
        arrival.wait_recv()
        onward.start()
    for cp in arrivals:
        cp.wait_recv()
    for cp in first + [onward for _, onward in relays]:
        cp.wait_send()
    for cp in local:
        cp.wait()


def _direct_copies(kind, x_ref, land_ref, send_sems, recv_sems):
    _, first, _, arrivals = _plan(kind, x_ref, [land_ref], (send_sems, recv_sems, None))
    return first, arrivals


_SEM = pl.BlockSpec(memory_space=pltpu.SEMAPHORE)
_DATAFLOW = pltpu.SideEffectType.DATAFLOW_SIDE_EFFECTING


def _exchange_start(kind, x, name):
    me = 4 * lax.axis_index("x") + 2 * lax.axis_index("y") + lax.axis_index("c")
    own = x if kind[0] == "spread" else lax.dynamic_slice_in_dim(x, kind[1] + kind[2] * me, kind[3], 0)
    shape = _exchange_shapes(kind, x)[0].shape
    land = lax.dynamic_update_slice_in_dim(lax.empty(shape, x.dtype), own[None], me, 0)

    def body(x_ref, land_ref, send_sems, recv_sems, x_thru, land_thru, token):
        del x_thru, land_thru
        for cp in _direct_copies(kind, x_ref, land_ref, send_sems, recv_sems)[0]:
            cp.start()
        token[...] = jnp.zeros_like(token)

    *handle, token = pl.pallas_call(
        body, name=name,
        out_shape=(pltpu.SemaphoreType.DMA((_N_REMOTE,)), pltpu.SemaphoreType.DMA((_N_REMOTE,)),
                   pltpu.HBM(x.shape, x.dtype), pltpu.HBM(land.shape, land.dtype), jax.ShapeDtypeStruct((8, 128), F32)),
        in_specs=(_HBM, _HBM), out_specs=(_SEM, _SEM, _HBM, _HBM, pl.BlockSpec(memory_space=pltpu.VMEM)),
        input_output_aliases={0: 2, 1: 3}, compiler_params=pltpu.CompilerParams(has_side_effects=_DATAFLOW),
    )(pltpu.with_memory_space_constraint(x, pltpu.HBM), pltpu.with_memory_space_constraint(land, pltpu.HBM))
    return (kind, *handle), token


def _exchange_wait(handle, after, name):
    kind, send_sems, recv_sems, x_thru, land_thru = handle

    def body(x_ref, land_ref, send_sems, recv_sems, *rest):
        first, arrivals = _direct_copies(kind, x_ref, land_ref, send_sems, recv_sems)
        for cp in first:
            cp.wait_send()
        for cp in arrivals:
            cp.wait_recv()

    return pl.pallas_call(
        body, name=name,
        out_shape=(pltpu.HBM(x_thru.shape, x_thru.dtype), pltpu.HBM(land_thru.shape, land_thru.dtype)),
        in_specs=(_HBM, _HBM, _SEM, _SEM) + (pl.BlockSpec(memory_space=pl.ANY),) * len(after), out_specs=(_HBM, _HBM),
        input_output_aliases={0: 0, 1: 1}, compiler_params=pltpu.CompilerParams(has_side_effects=_DATAFLOW),
    )(x_thru, land_thru, send_sems, recv_sems, *after)[1]


def _exchange(kind, x, name):
    n_out = len(_exchange_shapes(kind, x))

    def body(x_ref, *refs):
        plan = _plan(kind, x_ref, refs[:n_out], refs[n_out:])
        _start(plan)
        _finish(plan)

    return pl.pallas_call(body, name=name, out_shape=_exchange_shapes(kind, x), in_specs=[_HBM],
                          out_specs=[_HBM] * n_out, scratch_shapes=_exchange_sems(kind))(x)


def _call(body, name, grid, in_specs, out_specs, out_shape, args, scratch=(), sem=None, nbytes=0, aliases=None,
          comm=(), deps=()):
    in_specs, out_specs, out_shape, args, scratch = (list(in_specs), list(out_specs), list(out_shape), list(args),
                                                     list(scratch))
    if deps:
        n_real, n_deps, unordered = len(args), len(deps), body

        def body(*refs):
            unordered(*refs[:n_real], *refs[n_real + n_deps:])

        in_specs += [pl.BlockSpec(memory_space=pl.ANY)] * n_deps
        args += list(deps)
    if comm:
        n_in, n_out, n_scr, n_ops = len(args), len(out_shape), len(scratch), len(comm)
        kinds = [kind for kind, _ in comm]
        shapes = [_exchange_shapes(kind, x) for kind, x in comm]
        inner = body

        def body(*refs):
            ins, refs = refs[:n_in], refs[n_in:]
            cins, refs = refs[:n_ops], refs[n_ops:]
            outs, refs = refs[:n_out], refs[n_out:]
            couts = []
            for sh in shapes:
                couts.append(refs[:len(sh)])
                refs = refs[len(sh):]
            own_scratch, sems = refs[:n_scr], refs[n_scr:]
            first = last = None
            for axis, size in enumerate(grid):
                at_start, at_end = pl.program_id(axis) == 0, pl.program_id(axis) == size - 1
                first = at_start if first is None else first & at_start
                last = at_end if last is None else last & at_end
            plans = [_plan(kinds[o], cins[o], couts[o], sems[3 * o:3 * o + 3]) for o in range(n_ops)]

            @pl.when(first)
            def _():
                for p in plans:
                    _start(p)

            inner(*ins, *outs, *own_scratch)

            @pl.when(last)
            def _():
                for p in plans:
                    _finish(p)

        in_specs += [_HBM] * n_ops
        args += [x for _, x in comm]
        for kind, sh in zip(kinds, shapes):
            out_shape += sh
            out_specs += [_HBM] * len(sh)
            scratch += _exchange_sems(kind)
        sem = ("arbitrary",) * len(grid)
    limit = int(min(max(2 * nbytes + (8 << 20), 32 << 20), VMEM_CAP))
    return pl.pallas_call(
        body, name=name, grid=grid, in_specs=in_specs, out_specs=out_specs, out_shape=out_shape,
        scratch_shapes=scratch, input_output_aliases=aliases or {},
        compiler_params=pltpu.CompilerParams(dimension_semantics=sem, vmem_limit_bytes=limit),
    )(*args)


def _sum_slots(x, tr, name):
    n, r, c = x.shape

    def body(x_ref, o_ref):
        acc = x_ref[0].astype(F32)
        for d in range(1, n):
            acc = acc + x_ref[d].astype(F32)
        o_ref[...] = acc

    return _call(body, name, (r // tr,), [pl.BlockSpec((n, tr, c), lambda i: (0, i, 0))],
                 [pl.BlockSpec((tr, c), lambda i: (i, 0))], [jax.ShapeDtypeStruct((r, c), F32)], [x],
                 sem=("parallel",), nbytes=_nbytes((n, tr, c), x.dtype) + _nbytes((tr, c), F32))[0]


_DIMS = {"nn": (((1,), (0,)), ((), ())), "nt": (((1,), (1,)), ((), ())), "tn": (((0,), (0,)), ((), ()))}


def _matmul(a, b, mode, out_dtype, name, comm=(), deps=()):
    if mode == "nn":
        (m, k), (k2, n) = a.shape, b.shape
    elif mode == "nt":
        (m, k), (n, k2) = a.shape, b.shape
    else:
        (k, m), (k2, n) = a.shape, b.shape
    assert k == k2, (a.shape, b.shape, mode)
    tm, tn, tk = _tile(m, MM_TM), _tile(n, MM_TN), _tile(k, MM_TK)
    nk = k // tk
    dims = _DIMS[mode]
    a_spec = {"nn": pl.BlockSpec((tm, tk), lambda i, j, kk: (i, kk)),
              "nt": pl.BlockSpec((tm, tk), lambda i, j, kk: (i, kk)),
              "tn": pl.BlockSpec((tk, tm), lambda i, j, kk: (kk, i))}[mode]
    b_spec = {"nn": pl.BlockSpec((tk, tn), lambda i, j, kk: (kk, j)),
              "nt": pl.BlockSpec((tn, tk), lambda i, j, kk: (j, kk)),
              "tn": pl.BlockSpec((tk, tn), lambda i, j, kk: (kk, j))}[mode]

    def partial_product(a_ref, b_ref):
        return lax.dot_general(a_ref[...].astype(BF16), b_ref[...].astype(BF16), dims, preferred_element_type=F32)

    if nk == 1:
        def body(a_ref, b_ref, o_ref):
            o_ref[...] = partial_product(a_ref, b_ref).astype(o_ref.dtype)
        scratch = []
    else:
        def body(a_ref, b_ref, o_ref, acc_ref):
            kk = pl.program_id(2)

            @pl.when(kk == 0)
            def _():
                acc_ref[...] = jnp.zeros_like(acc_ref)

            acc_ref[...] += partial_product(a_ref, b_ref)

            @pl.when(kk == nk - 1)
            def _():
                o_ref[...] = acc_ref[...].astype(o_ref.dtype)
        scratch = [pltpu.VMEM((tm, tn), F32)]

    nbytes = (_nbytes((tm, tk), a.dtype) + _nbytes((tk, tn), b.dtype) + _nbytes((tm, tn), out_dtype)
              + _nbytes((tm, tn), F32))
    res = _call(body, name, (m // tm, n // tn, nk), [a_spec, b_spec],
                [pl.BlockSpec((tm, tn), lambda i, j, kk: (i, j))], [jax.ShapeDtypeStruct((m, n), out_dtype)], [a, b],
                scratch=scratch, sem=("parallel", "parallel", "arbitrary"), nbytes=nbytes, comm=comm, deps=deps)
    return res if comm else res[0]


def _rms(x):
    return lax.rsqrt(jnp.mean(x * x, axis=-1, keepdims=True) + EPS)


def _rms_bwd(dz, a, g):
    r = _rms(a)
    dzg = dz * g
    da = r * dzg - a * (r * r * r) * jnp.mean(dzg * a, axis=-1, keepdims=True)
    return da, dz * (a * r)


def _row_spec(tr, width):
    return pl.BlockSpec((tr, width), lambda i: (i, 0))


def _vec_spec(width):
    return pl.BlockSpec((1, width), lambda i: (0, 0))


def _rms_fwd(x, g, name):
    t, d = x.shape
    tr = min(ROW_BLOCK, t)

    def body(x_ref, g_ref, h_ref):
        xv = x_ref[...]
        h_ref[...] = ((xv * _rms(xv)) * g_ref[...]).astype(BF16)

    return _call(body, name, (t // tr,), [_row_spec(tr, d), _vec_spec(d)], [_row_spec(tr, d)],
                 [jax.ShapeDtypeStruct((t, d), BF16)], [x, g], sem=("parallel",), nbytes=3 * _nbytes((tr, d), F32))[0]


def _post_norm_fwd(x, o, g_post, g_next, name):
    t, d = x.shape
    tr = min(ROW_BLOCK, t)

    def body(x_ref, o_ref, gp_ref, gn_ref, xn_ref, h_ref):
        ov = o_ref[...]
        xn = x_ref[...] + (ov * _rms(ov)) * gp_ref[...]
        xn_ref[...] = xn
        h_ref[...] = ((xn * _rms(xn)) * gn_ref[...]).astype(BF16)

    return _call(body, name, (t // tr,), [_row_spec(tr, d), _row_spec(tr, d), _vec_spec(d), _vec_spec(d)],
                 [_row_spec(tr, d), _row_spec(tr, d)],
                 [jax.ShapeDtypeStruct((t, d), F32), jax.ShapeDtypeStruct((t, d), BF16)], [x, o, g_post, g_next],
                 sem=("parallel",), nbytes=5 * _nbytes((tr, d), F32))


def _post_norm_loss(x, o, g_post, target, name):
    t, d = x.shape
    tr = min(ROW_BLOCK, t)

    def body(x_ref, o_ref, gp_ref, t_ref, dy_ref, loss_ref):
        ov = o_ref[...]
        err = x_ref[...] + (ov * _rms(ov)) * gp_ref[...] - t_ref[...]
        dy_ref[...] = err / d
        part = 0.5 * jnp.sum(jnp.mean(err * err, axis=-1, keepdims=True), axis=0, keepdims=True)

        @pl.when(pl.program_id(0) == 0)
        def _():
            loss_ref[...] = jnp.zeros_like(loss_ref)

        loss_ref[...] += part

    return _call(body, name, (t // tr,), [_row_spec(tr, d), _row_spec(tr, d), _vec_spec(d), _row_spec(tr, d)],
                 [_row_spec(tr, d), pl.BlockSpec((1, 1), lambda i: (0, 0))],
                 [jax.ShapeDtypeStruct((t, d), F32), jax.ShapeDtypeStruct((1, 1), F32)], [x, o, g_post, target],
                 sem=("arbitrary",), nbytes=5 * _nbytes((tr, d), F32))


def _norm_bwd(dskip, name, pre=None, post=None):
    t, d = dskip.shape
    tr = min(ROW_BLOCK, t)
    n_in = 1 + (3 if pre else 0) + (2 if post else 0)

    def body(*refs):
        ins, outs = list(refs[:n_in]), list(refs[n_in:])
        first = pl.program_id(0) == 0
        dx = ins.pop(0)[...]
        if pre:
            dh_ref, xin_ref, gpre_ref = ins.pop(0), ins.pop(0), ins.pop(0)
            dxin, dg_rows = _rms_bwd(dh_ref[...].astype(F32), xin_ref[...], gpre_ref[...])
            dx = dx + dxin
            dx_ref, dgpre_ref = outs.pop(0), outs.pop(0)
            dx_ref[...] = dx

            @pl.when(first)
            def _():
                dgpre_ref[...] = jnp.zeros_like(dgpre_ref)

            dgpre_ref[...] += jnp.sum(dg_rows, axis=0, keepdims=True)
        if post:
            a_ref, gpost_ref = ins.pop(0), ins.pop(0)
            da, dg_rows = _rms_bwd(dx, a_ref[...], gpost_ref[...])
            da_ref, dgpost_ref = outs.pop(0), outs.pop(0)
            da_ref[...] = da.astype(BF16)

            @pl.when(first)
            def _():
                dgpost_ref[...] = jnp.zeros_like(dgpost_ref)

            dgpost_ref[...] += jnp.sum(dg_rows, axis=0, keepdims=True)

    args, in_specs, out_shape, out_specs = [dskip], [_row_spec(tr, d)], [], []
    if pre:
        args += list(pre)
        in_specs += [_row_spec(tr, d), _row_spec(tr, d), _vec_spec(d)]
        out_shape += [jax.ShapeDtypeStruct((t, d), F32), jax.ShapeDtypeStruct((1, d), F32)]
        out_specs += [_row_spec(tr, d), _vec_spec(d)]
    if post:
        args += list(post)
        in_specs += [_row_spec(tr, d), _vec_spec(d)]
        out_shape += [jax.ShapeDtypeStruct((t, d), BF16), jax.ShapeDtypeStruct((1, d), F32)]
        out_specs += [_row_spec(tr, d), _vec_spec(d)]
    return _call(body, name, (t // tr,), in_specs, out_specs, out_shape, args, sem=("arbitrary",),
                 nbytes=7 * _nbytes((tr, d), F32))


def _sigmoid(x):
    return 1.0 / (1.0 + jnp.exp(-x))


def _swiglu_fwd(gu, name):
    t = gu.shape[0]
    tr = min(ROW_BLOCK, t)

    def body(gu_ref, a_ref):
        gate, up = gu_ref[:, :D_FF].astype(F32), gu_ref[:, D_FF:].astype(F32)
        a_ref[...] = ((gate * _sigmoid(gate)) * up).astype(BF16)

    return _call(body, name, (t // tr,), [_row_spec(tr, 2 * D_FF)], [_row_spec(tr, D_FF)],
                 [jax.ShapeDtypeStruct((t, D_FF), BF16)], [gu], sem=("parallel",),
                 nbytes=2 * _nbytes((tr, 2 * D_FF), F32))[0]


def _swiglu_bwd(gu, dact, name):
    t = gu.shape[0]
    tr = min(ROW_BLOCK, t)

    def body(gu_ref, da_ref, o_ref):
        gate, up = gu_ref[:, :D_FF].astype(F32), gu_ref[:, D_FF:].astype(F32)
        da = da_ref[...].astype(F32)
        s = _sigmoid(gate)
        o_ref[:, :D_FF] = (da * up * (s * (1.0 + gate * (1.0 - s)))).astype(BF16)
        o_ref[:, D_FF:] = (da * (gate * s)).astype(BF16)

    return _call(body, name, (t // tr,), [_row_spec(tr, 2 * D_FF), _row_spec(tr, D_FF)], [_row_spec(tr, 2 * D_FF)],
                 [jax.ShapeDtypeStruct((t, 2 * D_FF), BF16)], [gu, dact], sem=("parallel",),
                 nbytes=3 * _nbytes((tr, 2 * D_FF), F32))[0]


def _split3(x):
    hi = x.astype(BF16)
    r1 = x - hi.astype(F32)
    mid = r1.astype(BF16)
    lo = (r1 - mid.astype(F32)).astype(BF16)
    return hi, mid, lo


def _dot_exact(mat01, x):
    hi, mid, lo = _split3(x)
    out = jnp.dot(mat01, hi, preferred_element_type=F32)
    out = out + jnp.dot(mat01, mid, preferred_element_type=F32)
    return out + jnp.dot(mat01, lo, preferred_element_type=F32)


def _dot_exact_rhs(x, mat01):
    hi, mid, lo = _split3(x)
    out = jnp.dot(hi, mat01, preferred_element_type=F32)
    out = out + jnp.dot(mid, mat01, preferred_element_type=F32)
    return out + jnp.dot(lo, mat01, preferred_element_type=F32)


def _tri(lower):
    r = lax.broadcasted_iota(jnp.int32, (CHUNK, CHUNK), 0)
    c = lax.broadcasted_iota(jnp.int32, (CHUNK, CHUNK), 1)
    return jnp.where(r >= c if lower else r <= c, 1.0, 0.0).astype(BF16)


def _log_sigmoid(z):
    return jnp.minimum(z, 0.0) - jnp.log(1.0 + jnp.exp(-jnp.abs(z)))


def _forget_cumsum(proj, b_pad, name):
    t = proj.shape[0]
    nb = t // CHUNK

    def body(f_ref, b_ref, c_ref):
        tri = _tri(True)
        b = b_ref[...]

        def blk(i, carry):
            rows = pl.ds(pl.multiple_of(i * CHUNK, CHUNK), CHUNK)
            cs = _dot_exact(tri, _log_sigmoid(f_ref[rows, :] + b)) + carry
            c_ref[rows, :] = cs
            return cs[CHUNK - 1:CHUNK, :]

        lax.fori_loop(0, nb, blk, jnp.zeros((1, HEAD_DIM), F32))

    return _call(body, name, (1,),
                 [pl.BlockSpec((t, HEAD_DIM), lambda i: (0, CB_F)), pl.BlockSpec((1, HEAD_DIM), lambda i: (0, 0))],
                 [pl.BlockSpec((t, HEAD_DIM), lambda i: (0, 0))], [jax.ShapeDtypeStruct((t, HEAD_DIM), F32)],
                 [proj, b_pad], sem=("arbitrary",), nbytes=2 * _nbytes((t, HEAD_DIM), F32))[0]


def _forget_bwd(proj, b_pad, dc_key, dc_query, dproj, name):
    t = proj.shape[0]
    tr = min(ROW_BLOCK, t)
    nb = t // tr
    rev = lambda i: nb - 1 - i

    def body(f_ref, b_ref, dck_ref, dcq_ref, dproj_in, df_ref, db_ref, run_ref):
        del dproj_in

        @pl.when(pl.program_id(0) == 0)
        def _():
            run_ref[...] = jnp.zeros_like(run_ref)
            db_ref[...] = jnp.zeros_like(db_ref)

        tri = _tri(False)
        b = b_ref[...]
        lane = lax.broadcasted_iota(jnp.int32, (CHUNK, HEAD_DIM), 1)
        df_ref[...] = jnp.zeros_like(df_ref)
        for n in reversed(range(tr // CHUNK)):
            rows = slice(n * CHUNK, (n + 1) * CHUNK)
            dc = dck_ref[rows, :]
            for h in range(N_HEADS):
                dc = dc + jnp.where(lane == h, dcq_ref[rows, h * HEAD_DIM:(h + 1) * HEAD_DIM], 0.0)
            dlogf = _dot_exact(tri, dc) + run_ref[...]
            run_ref[...] = dlogf[0:1, :]
            z = f_ref[rows, :] + b
            e = jnp.exp(-jnp.abs(z))
            sig_neg = jnp.where(z >= 0.0, e, 1.0) / (1.0 + e)
            df = jnp.where(lane < N_HEADS, dlogf * sig_neg, 0.0)
            df_ref[rows, :HEAD_DIM] = df.astype(BF16)
            db_ref[...] += jnp.sum(df, axis=0, keepdims=True)

    return _call(body, name, (nb,),
                 [pl.BlockSpec((tr, HEAD_DIM), lambda i: (rev(i), CB_F)), pl.BlockSpec((1, HEAD_DIM), lambda i: (0, 0)),
                  pl.BlockSpec((tr, HEAD_DIM), lambda i: (rev(i), 0)), pl.BlockSpec((tr, D_MODEL), lambda i: (rev(i), 0)),
                  pl.BlockSpec(memory_space=pl.ANY)],
                 [pl.BlockSpec((tr, F_WIDTH), lambda i: (rev(i), GATED // F_WIDTH)),
                  pl.BlockSpec((1, HEAD_DIM), lambda i: (0, 0))],
                 [jax.ShapeDtypeStruct(dproj.shape, dproj.dtype), jax.ShapeDtypeStruct((1, HEAD_DIM), F32)],
                 [proj, b_pad, dc_key, dc_query, dproj], scratch=[pltpu.VMEM((1, HEAD_DIM), F32)],
                 sem=("arbitrary",), nbytes=4 * _nbytes((tr, D_MODEL), F32), aliases={4: 0})


def _causal(block):
    r = lax.broadcasted_iota(jnp.int32, (block, block), 0)
    c = lax.broadcasted_iota(jnp.int32, (block, block), 1)
    return c <= r


def _lanes(x, width):
    return jnp.concatenate([x] * (width // HEAD_DIM), axis=1)


def _attn_fwd(proj, c_pad, c_row, name, comm=()):
    t = proj.shape[0]
    blk = min(ATTN_BLOCK, t)
    nq = t // blk
    scale = HEAD_DIM ** -0.5

    def body(q_ref, k_ref, v_ref, cp_ref, cr_ref, y_ref, e_ref):
        h, i = pl.program_id(0), pl.program_id(1)
        q = (q_ref[...] * scale).astype(BF16)
        lane = lax.broadcasted_iota(jnp.int32, (blk, HEAD_DIM), 1)
        ci = jnp.sum(jnp.where(lane == h, cp_ref[...], 0.0), axis=1, keepdims=True)

        def step(j, carry, diagonal):
            m, l, acc = carry
            off = pl.multiple_of(j * blk, blk)
            k = k_ref[pl.ds(off, blk), :].astype(BF16)
            v = v_ref[pl.ds(off, blk), :].astype(BF16)
            s = lax.dot_general(q, k, _DIMS["nt"], preferred_element_type=F32)
            s = s + ci - cr_ref[:, pl.ds(off, blk)]
            if diagonal:
                s = jnp.where(_causal(blk), s, NEG)
            m_new = jnp.maximum(m, jnp.max(s, axis=1, keepdims=True))
            alpha = jnp.exp(m - m_new)
            p = jnp.exp(s - m_new)
            l = alpha * l + jnp.sum(p, axis=1, keepdims=True)
            acc = alpha * acc + jnp.dot(p.astype(BF16), v, preferred_element_type=F32)
            return m_new, l, acc

        init = (jnp.full((blk, 1), NEG, F32), jnp.zeros((blk, 1), F32), jnp.zeros((blk, HEAD_DIM), F32))
        carry = lax.fori_loop(0, i, lambda j, c: step(j, c, False), init)
        m, l, acc = step(i, carry, True)
        y_ref[...] = acc / l
        e_ref[...] = jnp.broadcast_to(ci - (m + jnp.log(l)), (blk, HEAD_DIM))

    nbytes = 2 * _nbytes((t, HEAD_DIM), F32) + 4 * _nbytes((blk, HEAD_DIM), F32) + 4 * _nbytes((blk, blk), F32)
    head_block = pl.BlockSpec((blk, HEAD_DIM), lambda h, i: (i, h))
    return _call(body, name, (N_HEADS, nq),
                 [pl.BlockSpec((blk, HEAD_DIM), lambda h, i: (i, CB_QKV + 3 * h)),
                  pl.BlockSpec((t, HEAD_DIM), lambda h, i: (0, CB_QKV + 3 * h + 1)),
                  pl.BlockSpec((t, HEAD_DIM), lambda h, i: (0, CB_QKV + 3 * h + 2)),
                  pl.BlockSpec((blk, HEAD_DIM), lambda h, i: (i, 0)),
                  pl.BlockSpec((None, 1, t), lambda h, i: (h, 0, 0))],
                 [head_block, head_block],
                 [jax.ShapeDtypeStruct((t, D_MODEL), F32), jax.ShapeDtypeStruct((t, D_MODEL), F32)],
                 [proj, proj, proj, c_pad, c_row], sem=("parallel", "arbitrary"), nbytes=nbytes, comm=comm)


def _attn_bwd(proj, dy, e_b, delta_b, c_row, dproj, name, comm=()):
    t = proj.shape[0]
    blk = min(ATTN_BLOCK, t)
    nb = t // blk
    scale = HEAD_DIM ** -0.5

    def body(q_ref, k_ref, v_ref, dy_ref, e_ref, dl_ref, cr_ref, dproj_in, dqkv_ref, dc_ref, dr_ref, dq_acc):
        del dproj_in
        j = pl.program_id(1)

        @pl.when(j == 0)
        def _():
            dq_acc[...] = jnp.zeros_like(dq_acc)
            dr_ref[...] = jnp.zeros_like(dr_ref)

        k = k_ref[...].astype(BF16)
        v = v_ref[...].astype(BF16)
        crow = cr_ref[...]

        def step(i, carry, diagonal):
            dk, dv, dcs = carry
            rows = pl.ds(pl.multiple_of(i * blk, blk), blk)
            q = (q_ref[rows, :] * scale).astype(BF16)
            dyi = dy_ref[rows, :].astype(BF16)
            s = lax.dot_general(q, k, _DIMS["nt"], preferred_element_type=F32) + _lanes(e_ref[rows, :], blk) - crow
            if diagonal:
                s = jnp.where(_causal(blk), s, NEG)
            p = jnp.exp(s)
            dv = dv + lax.dot_general(p.astype(BF16), dyi, _DIMS["tn"], preferred_element_type=F32)
            dp = lax.dot_general(dyi, v, _DIMS["nt"], preferred_element_type=F32)
            ds = p * (dp - _lanes(dl_ref[rows, :], blk))
            dsb = ds.astype(BF16)
            dk = dk + lax.dot_general(dsb, q, _DIMS["tn"], preferred_element_type=F32)
            dq_acc[rows, :] += jnp.dot(dsb, k, preferred_element_type=F32) * scale
            dr_ref[rows, :] += jnp.sum(ds, axis=1, keepdims=True)
            return dk, dv, dcs + jnp.sum(ds, axis=0, keepdims=True)

        zeros = jnp.zeros((blk, HEAD_DIM), F32)
        carry = step(j, (zeros, zeros, jnp.zeros((1, blk), F32)), True)
        dk, dv, dcs = lax.fori_loop(j + 1, nb, lambda i, c: step(i, c, False), carry)
        mine = pl.ds(pl.multiple_of(j * blk, blk), blk)
        dqkv_ref[:, :HEAD_DIM] = dq_acc[mine, :].astype(BF16)
        dqkv_ref[:, HEAD_DIM:2 * HEAD_DIM] = dk.astype(BF16)
        dqkv_ref[:, 2 * HEAD_DIM:] = dv.astype(BF16)
        dc_ref[...] = -dcs

    full = lambda cb: pl.BlockSpec((t, HEAD_DIM), lambda h, j: (0, cb(h)))
    head = lambda h: h
    nbytes = (6 * _nbytes((t, HEAD_DIM), F32) + 4 * _nbytes((blk, HEAD_DIM), F32) + 5 * _nbytes((blk, blk), F32))
    return _call(body, name, (N_HEADS, nb),
                 [full(lambda h: CB_QKV + 3 * h),
                  pl.BlockSpec((blk, HEAD_DIM), lambda h, j: (j, CB_QKV + 3 * h + 1)),
                  pl.BlockSpec((blk, HEAD_DIM), lambda h, j: (j, CB_QKV + 3 * h + 2)),
                  full(head), full(head), full(head),
                  pl.BlockSpec((None, 1, blk), lambda h, j: (h, 0, j)),
                  pl.BlockSpec(memory_space=pl.ANY)],
                 [pl.BlockSpec((blk, QKV_WIDTH), lambda h, j: (j, (GATED + F_WIDTH) // QKV_WIDTH + h)),
                  pl.BlockSpec((None, 1, blk), lambda h, j: (h, 0, j)),
                  full(head)],
                 [jax.ShapeDtypeStruct(dproj.shape, dproj.dtype), jax.ShapeDtypeStruct((N_HEADS, 1, t), F32),
                  jax.ShapeDtypeStruct((t, D_MODEL), F32)],
                 [proj, proj, proj, dy, e_b, delta_b, c_row, dproj], scratch=[pltpu.VMEM((t, HEAD_DIM), F32)],
                 sem=("parallel", "arbitrary"), nbytes=nbytes, aliases={7: 0}, comm=comm)


def _chunks(t, blk):
    return [slice(r * blk, (r + 1) * blk) for r in range(t // blk)]


def _lane_of_head(x, h):
    lane = lax.broadcasted_iota(jnp.int32, x.shape, 1)
    return jnp.broadcast_to(jnp.sum(jnp.where(lane == h, x, 0.0), axis=1, keepdims=True), x.shape)


def _causal_t(block):
    r = lax.broadcasted_iota(jnp.int32, (block, block), 0)
    c = lax.broadcasted_iota(jnp.int32, (block, block), 1)
    return r <= c


def _attn_fwd_t(proj, c_pad, c_row, name, comm=()):
    t = proj.shape[0]
    blk = min(ATTN_BLOCK, t)
    nq = t // blk
    scale = HEAD_DIM ** -0.5

    def body(q_ref, k_ref, v_ref, cp_ref, cr_ref, y_ref, e_ref, kb_s, vt_s, cb_s):
        h, i = pl.program_id(0), pl.program_id(1)

        @pl.when(i == 0)
        def _():
            for rows in _chunks(t, blk):
                kb_s[rows, :] = k_ref[rows, :].astype(BF16)
                vt_s[:, rows] = v_ref[rows, :].T.astype(BF16)
                cb_s[rows, :] = _lane_of_head(cp_ref[rows, :], h)

        q = (q_ref[...] * scale).astype(BF16)
        ci = cr_ref[:, pl.ds(pl.multiple_of(i * blk, blk), blk)]

        def step(j, carry, diagonal):
            m, l, acc = carry
            rows = pl.ds(pl.multiple_of(j * blk, blk), blk)
            s = lax.dot_general(kb_s[rows, :], q, _DIMS["nt"], preferred_element_type=F32)
            s = s + ci - _lanes(cb_s[rows, :], blk)
            if diagonal:
                s = jnp.where(_causal_t(blk), s, NEG)
            m_new = jnp.maximum(m, jnp.max(s, axis=0, keepdims=True))
            alpha = jnp.exp(m - m_new)
            p = jnp.exp(s - m_new)
            l = alpha * l + jnp.sum(p, axis=0, keepdims=True)
            acc = alpha * acc + jnp.dot(vt_s[:, rows], p.astype(BF16), preferred_element_type=F32)
            return m_new, l, acc

        init = (jnp.full((1, blk), NEG, F32), jnp.zeros((1, blk), F32), jnp.zeros((HEAD_DIM, blk), F32))
        carry = lax.fori_loop(0, i, lambda j, c: step(j, c, False), init)
        m, l, acc = step(i, carry, True)
        y_ref[...] = (acc / l).T
        e_ref[...] = ci - (m + jnp.log(l))

    nbytes = 3 * _nbytes((t, HEAD_DIM), F32) + 4 * _nbytes((blk, HEAD_DIM), F32) + 4 * _nbytes((blk, blk), F32)
    return _call(body, name, (N_HEADS, nq),
                 [pl.BlockSpec((blk, HEAD_DIM), lambda h, i: (i, CB_QKV + 3 * h)),
                  pl.BlockSpec((t, HEAD_DIM), lambda h, i: (0, CB_QKV + 3 * h + 1)),
                  pl.BlockSpec((t, HEAD_DIM), lambda h, i: (0, CB_QKV + 3 * h + 2)),
                  pl.BlockSpec((t, HEAD_DIM), lambda h, i: (0, 0)),
                  pl.BlockSpec((None, 1, t), lambda h, i: (h, 0, 0))],
                 [pl.BlockSpec((blk, HEAD_DIM), lambda h, i: (i, h)),
                  pl.BlockSpec((None, 1, blk), lambda h, i: (h, 0, i))],
                 [jax.ShapeDtypeStruct((t, D_MODEL), F32), jax.ShapeDtypeStruct((N_HEADS, 1, t), F32)],
                 [proj, proj, proj, c_pad, c_row],
                 scratch=[pltpu.VMEM((t, HEAD_DIM), BF16), pltpu.VMEM((HEAD_DIM, t), BF16),
                          pltpu.VMEM((t, HEAD_DIM), F32)],
                 sem=("parallel", "arbitrary"), nbytes=nbytes, comm=comm)


def _attn_bwd_t(proj, dy, e_row, delta_b, c_pad, dproj, name, deps=()):
    t = proj.shape[0]
    blk = min(ATTN_BLOCK, t)
    nb = t // blk
    scale = HEAD_DIM ** -0.5

    def body(q_ref, k_ref, v_ref, dy_ref, e_ref, dl_ref, cp_ref, dproj_in, dqkv_ref, dck_ref, dcq_ref,
             qb_s, qt_s, dyb_s, dyt_s, dl_s, dqt_acc):
        del dproj_in
        h, j = pl.program_id(0), pl.program_id(1)

        @pl.when(j == 0)
        def _():
            for rows in _chunks(t, blk):
                qs = q_ref[rows, :] * scale
                qb_s[rows, :] = qs.astype(BF16)
                qt_s[:, rows] = qs.T.astype(BF16)
                dyr = dy_ref[rows, :]
                dyb_s[rows, :] = dyr.astype(BF16)
                dyt_s[:, rows] = dyr.T.astype(BF16)
                dl_s[:, rows] = dl_ref[rows, :].T[0:1, :]
            dqt_acc[...] = jnp.zeros_like(dqt_acc)
            dcq_ref[...] = jnp.zeros_like(dcq_ref)

        kf = k_ref[...]
        kb, kt = kf.astype(BF16), kf.T.astype(BF16)
        vb = v_ref[...].astype(BF16)
        cj = _lanes(_lane_of_head(cp_ref[...], h), blk)

        def step(i, carry, diagonal):
            dkt, dvt, dkey = carry
            cols = pl.ds(pl.multiple_of(i * blk, blk), blk)
            qi, dyi = qb_s[cols, :], dyb_s[cols, :]
            s = lax.dot_general(kb, qi, _DIMS["nt"], preferred_element_type=F32) + e_ref[:, cols] - cj
            if diagonal:
                s = jnp.where(_causal_t(blk), s, NEG)
            p = jnp.exp(s)
            dp = lax.dot_general(vb, dyi, _DIMS["nt"], preferred_element_type=F32)
            ds = p * (dp - dl_s[:, cols])
            pb, dsb = p.astype(BF16), ds.astype(BF16)
            dvt = dvt + lax.dot_general(dyt_s[:, cols], pb, _DIMS["nt"], preferred_element_type=F32)
            dkt = dkt + lax.dot_general(qt_s[:, cols], dsb, _DIMS["nt"], preferred_element_type=F32)
            dqt_acc[:, cols] += jnp.dot(kt, dsb, preferred_element_type=F32) * scale
            dcq_ref[:, cols] += jnp.sum(ds, axis=0, keepdims=True)
            part = ds[:, :HEAD_DIM]
            for g in range(1, blk // HEAD_DIM):
                part = part + ds[:, g * HEAD_DIM:(g + 1) * HEAD_DIM]
            return dkt, dvt, dkey + part

        zeros = jnp.zeros((HEAD_DIM, blk), F32)
        carry = step(j, (zeros, zeros, jnp.zeros((blk, HEAD_DIM), F32)), True)
        dkt, dvt, dkey = lax.fori_loop(j + 1, nb, lambda i, c: step(i, c, False), carry)
        mine = pl.ds(pl.multiple_of(j * blk, blk), blk)
        dqkv_ref[:, :HEAD_DIM] = dqt_acc[:, mine].T.astype(BF16)
        dqkv_ref[:, HEAD_DIM:2 * HEAD_DIM] = dkt.T.astype(BF16)
        dqkv_ref[:, 2 * HEAD_DIM:] = dvt.T.astype(BF16)
        dck_ref[...] = -jnp.sum(dkey.T, axis=0, keepdims=True)

    full = lambda cb: pl.BlockSpec((t, HEAD_DIM), lambda h, j: (0, cb(h)))
    head = lambda h: h
    row = pl.BlockSpec((None, 1, t), lambda h, j: (h, 0, 0))
    nbytes = (8 * _nbytes((t, HEAD_DIM), F32) + 4 * _nbytes((blk, HEAD_DIM), F32) + 6 * _nbytes((blk, blk), F32))
    return _call(body, name, (N_HEADS, nb),
                 [full(lambda h: CB_QKV + 3 * h),
                  pl.BlockSpec((blk, HEAD_DIM), lambda h, j: (j, CB_QKV + 3 * h + 1)),
                  pl.BlockSpec((blk, HEAD_DIM), lambda h, j: (j, CB_QKV + 3 * h + 2)),
                  full(head), row, full(head),
                  pl.BlockSpec((blk, HEAD_DIM), lambda h, j: (j, 0)),
                  pl.BlockSpec(memory_space=pl.ANY)],
                 [pl.BlockSpec((blk, QKV_WIDTH), lambda h, j: (j, (GATED + F_WIDTH) // QKV_WIDTH + h)),
                  pl.BlockSpec((None, 1, blk), lambda h, j: (h, 0, j)),
                  row],
                 [jax.ShapeDtypeStruct(dproj.shape, dproj.dtype), jax.ShapeDtypeStruct((N_HEADS, 1, t), F32),
                  jax.ShapeDtypeStruct((N_HEADS, 1, t), F32)],
                 [proj, proj, proj, dy, e_row, delta_b, c_pad, dproj],
                 scratch=[pltpu.VMEM((t, HEAD_DIM), BF16), pltpu.VMEM((HEAD_DIM, t), BF16),
                          pltpu.VMEM((t, HEAD_DIM), BF16), pltpu.VMEM((HEAD_DIM, t), BF16),
                          pltpu.VMEM((1, t), F32), pltpu.VMEM((HEAD_DIM, t), F32)],
                 sem=("parallel", "arbitrary"), nbytes=nbytes, aliases={7: 0}, deps=deps)


def _forget_bwd_t(proj, b_pad, dc, dproj, name):
    t = proj.shape[0]
    tr = min(ROW_BLOCK, t)
    nb = t // tr
    rev = lambda i: nb - 1 - i

    def body(f_ref, b_ref, dc_ref, dproj_in, df_ref, db_ref, run_ref):
        del dproj_in

        @pl.when(pl.program_id(0) == 0)
        def _():
            run_ref[...] = jnp.zeros_like(run_ref)
            db_ref[...] = jnp.zeros_like(db_ref)

        tri = _tri(False)
        b = b_ref[...]
        lane = lax.broadcasted_iota(jnp.int32, (CHUNK, HEAD_DIM), 1)
        df_ref[...] = jnp.zeros_like(df_ref)
        for n in reversed(range(tr // CHUNK)):
            rows = slice(n * CHUNK, (n + 1) * CHUNK)
            dlogf = _dot_exact(tri, dc_ref[rows, :]) + run_ref[...]
            run_ref[...] = dlogf[0:1, :]
            z = f_ref[rows, :] + b
            e = jnp.exp(-jnp.abs(z))
            sig_neg = jnp.where(z >= 0.0, e, 1.0) / (1.0 + e)
            df = jnp.where(lane < N_HEADS, dlogf * sig_neg, 0.0)
            df_ref[rows, :HEAD_DIM] = df.astype(BF16)
            db_ref[...] += jnp.sum(df, axis=0, keepdims=True)

    return _call(body, name, (nb,),
                 [pl.BlockSpec((tr, HEAD_DIM), lambda i: (rev(i), CB_F)), pl.BlockSpec((1, HEAD_DIM), lambda i: (0, 0)),
                  pl.BlockSpec((tr, HEAD_DIM), lambda i: (rev(i), 0)), pl.BlockSpec(memory_space=pl.ANY)],
                 [pl.BlockSpec((tr, F_WIDTH), lambda i: (rev(i), GATED // F_WIDTH)),
                  pl.BlockSpec((1, HEAD_DIM), lambda i: (0, 0))],
                 [jax.ShapeDtypeStruct(dproj.shape, dproj.dtype), jax.ShapeDtypeStruct((1, HEAD_DIM), F32)],
                 [proj, b_pad, dc, dproj], scratch=[pltpu.VMEM((1, HEAD_DIM), F32)],
                 sem=("arbitrary",), nbytes=4 * _nbytes((tr, HEAD_DIM), F32), aliases={3: 0})


_GELU_K = math.sqrt(2.0 / math.pi)
_GELU_C = 0.044715


def _gelu(x):
    t = jnp.tanh(_GELU_K * (x + _GELU_C * (x * x * x)))
    return 0.5 * x * (1.0 + t), t


def _gelu_grad(x, t):
    return 0.5 * (1.0 + t) + 0.5 * x * (1.0 - t * t) * (_GELU_K * (1.0 + 3.0 * _GELU_C * (x * x)))


def _layernorm_stats(a):
    mu = jnp.mean(a, axis=-1, keepdims=True)
    xc = a - mu
    r = lax.rsqrt(jnp.mean(xc * xc, axis=-1, keepdims=True) + EPS)
    return xc * r, r


def _group(g):
    return slice(g * CHUNK, (g + 1) * CHUNK)


def _gated_cols(k):
    return pl.BlockSpec((CHUNK, D_MODEL), lambda i: (i, k))


def _fixed(shape):
    return pl.BlockSpec(shape, lambda i: (0,) * len(shape))


def _mix_fwd(proj, y_b, w_s, b_cols, g_v, name):
    t = proj.shape[0]

    def body(u_ref, vs_ref, ga_ref, gb_ref, yb_ref, w_ref, b_ref, gv_ref, o_ref):
        a_u, _ = _gelu(u_ref[...])
        a_v, _ = _gelu(vs_ref[...])
        xhat, _ = _layernorm_stats(a_v)
        vn = (xhat * gv_ref[...]).astype(BF16)
        sa, sb = _sigmoid(ga_ref[...]), _sigmoid(gb_ref[...])
        yb = yb_ref[...]
        mask = _causal(CHUNK)
        for g in range(N_GROUPS):
            cols = _group(g)
            w = jnp.where(mask, w_ref[g], 0.0).astype(BF16)
            mixed = jnp.dot(w, vn[:, cols], preferred_element_type=F32) + b_ref[:, g:g + 1]
            o_ref[:, cols] = (sa[:, cols] * (a_u[:, cols] * mixed) + sb[:, cols] * yb[:, cols]).astype(BF16)

    return _call(body, name, (t // CHUNK,),
                 [_gated_cols(0), _gated_cols(1), _gated_cols(2), _gated_cols(3), _row_spec(CHUNK, D_MODEL),
                  _fixed((N_GROUPS, CHUNK, CHUNK)), _fixed((CHUNK, N_GROUPS)), _fixed((1, D_MODEL))],
                 [_row_spec(CHUNK, D_MODEL)], [jax.ShapeDtypeStruct((t, D_MODEL), BF16)],
                 [proj, proj, proj, proj, y_b, w_s, b_cols, g_v], sem=("parallel",),
                 nbytes=16 * _nbytes((CHUNK, D_MODEL), F32))[0]


def _mix_bwd(proj, y_b, dmerged, w_s, b_cols, g_v, sel, name, deps=()):
    t = proj.shape[0]

    def body(u_ref, vs_ref, ga_ref, gb_ref, yb_ref, dm_ref, w_ref, b_ref, gv_ref, sel_ref,
             dg_ref, dyb_ref, delta_ref, dw_ref, dbt_ref, dgv_ref):
        @pl.when(pl.program_id(0) == 0)
        def _():
            dw_ref[...] = jnp.zeros_like(dw_ref)
            dbt_ref[...] = jnp.zeros_like(dbt_ref)
            dgv_ref[...] = jnp.zeros_like(dgv_ref)

        u, vs = u_ref[...], vs_ref[...]
        a_u, t_u = _gelu(u)
        a_v, t_v = _gelu(vs)
        xhat, r = _layernorm_stats(a_v)
        gv = gv_ref[...]
        vn = (xhat * gv).astype(BF16)
        sa, sb = _sigmoid(ga_ref[...]), _sigmoid(gb_ref[...])
        yb, dm = yb_ref[...], dm_ref[...].astype(F32)
        dyb = dm * sb
        dyb_ref[...] = dyb
        dyb_yb = dyb * yb
        dg_ref[:, 3 * D_MODEL:] = (dm * yb * (sb * (1.0 - sb))).astype(BF16)
        dya = dm * sa
        mask = _causal(CHUNK)
        dmixed_parts, dvn_parts = [], []
        for g in range(N_GROUPS):
            cols = _group(g)
            delta_ref[:, cols] = jnp.broadcast_to(jnp.sum(dyb_yb[:, cols], axis=1, keepdims=True), (CHUNK, CHUNK))
            w = jnp.where(mask, w_ref[g], 0.0).astype(BF16)
            mixed = jnp.dot(w, vn[:, cols], preferred_element_type=F32) + b_ref[:, g:g + 1]
            y_a = a_u[:, cols] * mixed
            dg_ref[:, 2 * D_MODEL + g * CHUNK:2 * D_MODEL + (g + 1) * CHUNK] = (
                dm[:, cols] * y_a * (sa[:, cols] * (1.0 - sa[:, cols]))).astype(BF16)
            dg_ref[:, cols] = (dya[:, cols] * mixed * _gelu_grad(u[:, cols], t_u[:, cols])).astype(BF16)
            dmixed = dya[:, cols] * a_u[:, cols]
            dmb = dmixed.astype(BF16)
            dw = lax.dot_general(dmb, vn[:, cols], _DIMS["nt"], preferred_element_type=F32)
            dw_ref[g] += jnp.where(mask, dw, 0.0)
            dvn_parts.append(lax.dot_general(w, dmb, _DIMS["tn"], preferred_element_type=F32))
            dmixed_parts.append(dmixed)
        dmixed_all = jnp.concatenate(dmixed_parts, axis=1)
        dvn = jnp.concatenate(dvn_parts, axis=1)
        dbt_ref[...] += _dot_exact_rhs(dmixed_all, sel_ref[...])
        dgv_ref[...] += jnp.sum(dvn * xhat, axis=0, keepdims=True)
        dxh = dvn * gv
        da_v = r * (dxh - jnp.mean(dxh, axis=-1, keepdims=True)
                    - xhat * jnp.mean(dxh * xhat, axis=-1, keepdims=True))
        dg_ref[:, D_MODEL:2 * D_MODEL] = (da_v * _gelu_grad(vs, t_v)).astype(BF16)

    row = lambda width: _row_spec(CHUNK, width)
    return _call(body, name, (t // CHUNK,),
                 [_gated_cols(0), _gated_cols(1), _gated_cols(2), _gated_cols(3), row(D_MODEL), row(D_MODEL),
                  _fixed((N_GROUPS, CHUNK, CHUNK)), _fixed((CHUNK, N_GROUPS)), _fixed((1, D_MODEL)),
                  _fixed((D_MODEL, HEAD_DIM))],
                 [row(GATED), row(D_MODEL), row(D_MODEL),
                  _fixed((N_GROUPS, CHUNK, CHUNK)), _fixed((CHUNK, HEAD_DIM)), _fixed((1, D_MODEL))],
                 [jax.ShapeDtypeStruct((t, IN_PAD), BF16), jax.ShapeDtypeStruct((t, D_MODEL), F32),
                  jax.ShapeDtypeStruct((t, D_MODEL), F32), jax.ShapeDtypeStruct((N_GROUPS, CHUNK, CHUNK), F32),
                  jax.ShapeDtypeStruct((CHUNK, HEAD_DIM), F32), jax.ShapeDtypeStruct((1, D_MODEL), F32)],
                 [proj, proj, proj, proj, y_b, dmerged, w_s, b_cols, g_v, sel], sem=("arbitrary",),
                 nbytes=40 * _nbytes((CHUNK, D_MODEL), F32), deps=deps)


def _adamw_update(w_ref, g_ref, m_ref, v_ref, d_ref, nm_ref, nv_ref):
    gv = g_ref[...]
    nm = ADAM_B1 * m_ref[...] + (1.0 - ADAM_B1) * gv
    nv = ADAM_B2 * v_ref[...] + (1.0 - ADAM_B2) * (gv * gv)
    m_hat = nm / (1.0 - ADAM_B1 ** ADAM_STEP)
    v_hat = nv / (1.0 - ADAM_B2 ** ADAM_STEP)
    d_ref[...] = -ADAM_LR * (m_hat / (jnp.sqrt(v_hat) + ADAM_EPS) + ADAM_WD * w_ref[...])
    nm_ref[...] = nm
    nv_ref[...] = nv


def _adamw(w, g, m, v, name):
    r, c = w.shape
    tr = _tile(r, 256, unit=8)

    def body(w_ref, g_ref, m_ref, v_ref, d_ref, nm_ref, nv_ref):
        _adamw_update(w_ref, g_ref, m_ref, v_ref, d_ref, nm_ref, nv_ref)

    spec = pl.BlockSpec((tr, c), lambda i: (i, 0))
    shape = jax.ShapeDtypeStruct((r, c), F32)
    return _call(body, name, (r // tr,), [spec] * 4, [spec] * 3, [shape] * 3, [w, g, m, v], sem=("parallel",),
                 nbytes=7 * _nbytes((tr, max(c, 128)), F32))


def _adamw_cols(w, g, m, v, name):
    layers, r, c = w.shape
    tc = 256

    def body(w_ref, g_ref, m_ref, v_ref, d_ref, nm_ref, nv_ref):
        _adamw_update(w_ref, g_ref, m_ref, v_ref, d_ref, nm_ref, nv_ref)

    spec = pl.BlockSpec((None, r, tc), lambda l, j: (l, 0, j))
    shape = jax.ShapeDtypeStruct(w.shape, F32)
    return _call(body, name, (layers, c // tc), [spec] * 4, [spec] * 3, [shape] * 3, [w, g, m, v],
                 sem=("parallel", "parallel"), nbytes=7 * _nbytes((r, tc), F32))


def _pad_rows(a, rows):
    return jnp.pad(a, ((0, rows - a.shape[0]), (0, 0)))


def _pack_rest(w_out, w_gate, w_up, w_down):
    return jnp.concatenate([w_out, w_gate.T, w_up.T, w_down], axis=0).astype(BF16)


def _unpack_in(gathered):
    i = jnp.arange(16)[None, :, None]
    b = jnp.arange(N_DEV)[:, None, None]
    head = jnp.where(i < b, jnp.roll(gathered[:, IN_STRIDE:IN_STRIDE + 16], 1, axis=0), gathered[:, :16])
    nat = jnp.concatenate([head, gathered[:, 16:IN_STRIDE]], axis=1).reshape(N_DEV * IN_STRIDE, D_MODEL)
    qkv = nat[2 * D_MODEL:5 * D_MODEL].reshape(3, N_HEADS, HEAD_DIM, D_MODEL)
    qkv = jnp.transpose(qkv, (1, 0, 2, 3)).reshape(3 * D_MODEL, D_MODEL)
    f = _pad_rows(gathered[N_DEV - 1, IN_STRIDE:IN_STRIDE + N_HEADS], F_WIDTH)
    return jnp.concatenate([nat[:2 * D_MODEL], nat[5 * D_MODEL:7 * D_MODEL], f, qkv], axis=0)


def _unpack_rest(gathered):
    take = lambda lo, n: gathered[:, lo:lo + n, :].reshape(N_DEV * n, D_MODEL)
    return take(R_OUT, SH_OUT), jnp.concatenate([take(R_G, SH_FF), take(R_U, SH_FF)], axis=0), take(R_D, SH_FF)


def _pack_grad_in(dw_in_t):
    qkv = dw_in_t[GATED + F_WIDTH:].reshape(N_HEADS, 3, HEAD_DIM, D_MODEL)
    qkv = jnp.transpose(qkv, (1, 0, 2, 3)).reshape(3 * D_MODEL, D_MODEL)
    return jnp.concatenate([dw_in_t[:2 * D_MODEL], qkv, dw_in_t[2 * D_MODEL:GATED],
                            dw_in_t[GATED:GATED + IN_ROWS - N_DEV * IN_STRIDE]], axis=0)


def _pack_grad_rest(dw_out, dw_gu_t, dw_down):
    split = lambda a, n: a.reshape(N_DEV, n, D_MODEL)
    return jnp.concatenate([split(dw_out, SH_OUT), split(dw_gu_t[:D_FF], SH_FF), split(dw_gu_t[D_FF:], SH_FF),
                            split(dw_down, SH_FF)], axis=1)


def kernel(x, mix_pre_g, w_in, b_forget, sgu_norm_g, w_spatial, b_spatial, w_out, mix_post_g, ffn_pre_g, w_gate, w_up, w_down, ffn_post_g, loss_target, m_mix_pre_g, m_w_in, m_b_forget, m_sgu_norm_g, m_w_spatial, m_b_spatial, m_w_out, m_mix_post_g, m_ffn_pre_g, m_w_gate, m_w_up, m_w_down, m_ffn_post_g, v_mix_pre_g, v_w_in, v_b_forget, v_sgu_norm_g, v_w_spatial, v_b_spatial, v_w_out, v_mix_post_g, v_ffn_pre_g, v_w_gate, v_w_up, v_w_down, v_ffn_post_g):
    depth = w_in.shape[0]
    assert depth == 2
    t = x.shape[1]
    x0 = x.reshape(t, D_MODEL)
    target = loss_target.reshape(t, D_MODEL)
    sel = (jnp.arange(D_MODEL)[:, None] // HEAD_DIM == jnp.arange(HEAD_DIM)[None, :]).astype(BF16)
    vec = lambda a, l: a[l][None, :]
    zero_row = jnp.zeros((1, D_MODEL), F32)

    me = 4 * lax.axis_index("x") + 2 * lax.axis_index("y") + lax.axis_index("c")
    p_in = [lax.dynamic_update_slice(jnp.zeros((P_IN, D_MODEL), BF16), w_in[l].T.astype(BF16), (me, 0))
            for l in range(depth)]
    p_rest = [[w_out[l].astype(BF16), jnp.stack([w_gate[l].T, w_up[l].T]).astype(BF16), w_down[l].astype(BF16)]
              for l in range(depth)]
    w_in_t = [None] * depth
    w_rest = [None] * depth
    w_in_t[0] = _unpack_in(_exchange(("gather", 0), p_in[0], "gather_w_in_0")[0])

    saved = []
    xl = x0
    h = _rms_fwd(xl, vec(mix_pre_g, 0), "rms_in_0")
    dy = loss = None
    for l in range(depth):
        b_pad = jnp.pad(b_forget[l], (0, HEAD_DIM - N_HEADS))[None, :]
        b_cols = b_spatial[l].T
        proj = _matmul(h, w_in_t[l], "nt", F32, f"proj_{l}")
        c_pad = _forget_cumsum(proj, b_pad, f"forget_cumsum_{l}")
        c_row = c_pad[:, :N_HEADS].T[:, None, :]
        riders = [(("gather", 0), p_rest[l][0]), (("gather", 1), p_rest[l][1]), (("gather", 0), p_rest[l][2])]
        riders += [(("gather", 0), p_in[l + 1])] if l + 1 < depth else []
        y_b, e_row, *arrived = _attn_fwd_t(proj, c_pad, c_row, f"attn_fwd_{l}", comm=riders)
        w_rest[l] = (arrived[0].reshape(D_MODEL, D_MODEL), arrived[1].reshape(2 * D_FF, D_MODEL),
                     arrived[2].reshape(D_FF, D_MODEL))
        if l + 1 < depth:
            w_in_t[l + 1] = _unpack_in(arrived[3])
        w_o, w_gu_t, w_d = w_rest[l]
        merged = _mix_fwd(proj, y_b, w_spatial[l], b_cols, vec(sgu_norm_g, l), f"mix_fwd_{l}")
        o = _matmul(merged, w_o, "nn", F32, f"out_proj_{l}")
        x1, h2 = _post_norm_fwd(xl, o, vec(mix_post_g, l), vec(ffn_pre_g, l), f"post_mix_{l}")
        gu = _matmul(h2, w_gu_t, "nt", BF16, f"ffn_gu_{l}")
        act = _swiglu_fwd(gu, f"swiglu_fwd_{l}")
        dn = _matmul(act, w_d, "nn", F32, f"ffn_down_{l}")
        saved.append(dict(x0=xl, h=h, proj=proj, b_pad=b_pad, b_cols=b_cols, c_pad=c_pad, y_b=y_b, e_row=e_row,
                          merged=merged, o=o, x1=x1, h2=h2, gu=gu, act=act, dn=dn))
        if l + 1 < depth:
            xl, h = _post_norm_fwd(x1, dn, vec(ffn_post_g, l), vec(mix_pre_g, l + 1), f"post_ffn_{l}")
        else:
            dy, loss = _post_norm_loss(x1, dn, vec(ffn_post_g, l), target, "loss")

    g_small = [None] * depth
    flights = {}
    late_token = None
    dx = dy
    dd, dg_ffn_post = _norm_bwd(dx, f"bwd_ffn_post_{depth - 1}", post=(saved[-1]["dn"], vec(ffn_post_g, depth - 1)))
    for l in reversed(range(depth)):
        s = saved[l]
        w_o, w_gu_t, w_d = w_rest[l]
        dact = _matmul(dd, w_d, "nt", BF16, f"d_act_{l}")
        dw_d = _matmul(s["act"], dd, "tn", BF16, f"dw_down_{l}")
        dgu = _swiglu_bwd(s["gu"], dact, f"swiglu_bwd_{l}")
        dh2 = _matmul(dgu, w_gu_t, "nn", F32, f"d_h2_{l}")
        dw_gu_t = _matmul(dgu, s["h2"], "tn", BF16, f"dw_gu_{l}")
        dx1, dg_ffn_pre, do, dg_mix_post = _norm_bwd(
            dx, f"bwd_post_mix_{l}", pre=(dh2, s["x1"], vec(ffn_pre_g, l)), post=(s["o"], vec(mix_post_g, l)))
        dmerged = _matmul(do, w_o, "nt", F32, f"d_merged_{l}")
        dw_o = _matmul(s["merged"], do, "tn", BF16, f"dw_out_{l}")
        tokens = []
        for key, dw, base, rows in [("out", dw_o, 0, SH_OUT), ("gate", dw_gu_t, 0, SH_FF), ("up", dw_gu_t, D_FF, SH_FF),
                                    ("down", dw_d, 0, SH_FF)]:
            flights[(key, l)], token = _exchange_start(("windows", base, rows, rows), dw, f"start_grad_{key}_{l}")
            tokens.append(token)
        dproj, dyb, delta_b, dw_s, dbt, dg_v = _mix_bwd(
            s["proj"], s["y_b"], dmerged, w_spatial[l], s["b_cols"], vec(sgu_norm_g, l), sel, f"mix_bwd_{l}",
            deps=tokens)
        db_s = dbt[:, :N_GROUPS].T.reshape(1, D_MODEL)
        tokens = []
        if l == 0:
            g_small[0] = jnp.concatenate([zero_row, dg_v, dg_mix_post, dg_ffn_pre, dg_ffn_post, db_s, zero_row,
                                          zero_row, dw_s.reshape(CHUNK, D_MODEL)], axis=0)
            flights["small"], token = _exchange_start(("spread",),jnp.concatenate([g_small[1], g_small[0]], axis=0),
                                                      "start_small_grads")
            tokens = [token]
        dproj, dc_key, dc_query = _attn_bwd_t(s["proj"], dyb, s["e_row"], delta_b, s["c_pad"], dproj,
                                              f"attn_bwd_{l}", deps=tokens)
        dc = jnp.pad((dc_key + dc_query)[:, 0, :].T, ((0, 0), (0, HEAD_DIM - N_HEADS)))
        dproj, db_f = _forget_bwd_t(s["proj"], s["b_pad"], dc, dproj, f"forget_bwd_{l}")
        db_f_row = jnp.pad(db_f, ((0, 0), (0, D_MODEL - HEAD_DIM)))
        dw_in_t = _matmul(dproj, s["h"], "tn", BF16, f"dw_in_{l}")
        flights[("in", l)], token = _exchange_start(("windows", 0, IN_STRIDE, P_IN), _pack_grad_in(dw_in_t),
                                                    f"start_grad_in_{l}")
        dh = _matmul(dproj, w_in_t[l], "nn", F32, f"d_h_{l}", deps=[token])
        if l > 0:
            dx, dg_mix_pre, dd, dg_ffn_post_below = _norm_bwd(
                dx1, f"bwd_pre_mix_{l}", pre=(dh, s["x0"], vec(mix_pre_g, l)),
                post=(saved[l - 1]["dn"], vec(ffn_post_g, l - 1)))
            g_small[l] = jnp.concatenate([dg_mix_pre, dg_v, dg_mix_post, dg_ffn_pre, dg_ffn_post, db_s, db_f_row,
                                          zero_row, dw_s.reshape(CHUNK, D_MODEL)], axis=0)
            dg_ffn_post = dg_ffn_post_below
        else:
            dx, dg_mix_pre = _norm_bwd(dx1, f"bwd_pre_mix_{l}", pre=(dh, s["x0"], vec(mix_pre_g, l)))
            late_rows = jnp.concatenate([dg_mix_pre, db_f_row] + [zero_row] * 6, axis=0)
            flights["late"], late_token = _exchange_start(("spread",),late_rows, "start_late_small_grads")
    grad_x = dx.reshape(x.shape)
    loss = lax.psum(loss.reshape(()), ("x", "y", "c"))

    landed, after = {}, [grad_x, late_token]
    rest = ["out", "gate", "up", "down"]
    for key in [(n, 1) for n in rest] + [("in", 1)] + [(n, 0) for n in rest] + ["small"]:
        label = key if isinstance(key, str) else f"grad_{key[0]}_{key[1]}"
        landed[key] = _exchange_wait(flights[key], after, f"wait_{label}")
        after = [landed[key]]
    small_sum = _sum_slots(landed["small"], S_ROWS, "sum_small_grads")
    sm = small_sum.reshape(depth, S_ROWS, D_MODEL)[::-1]
    g = {
        "sgu_norm_g": sm[:, 1], "mix_post_g": sm[:, 2],
        "ffn_pre_g": sm[:, 3], "ffn_post_g": sm[:, 4], "b_spatial": sm[:, 5].reshape(depth, N_GROUPS, CHUNK),
        "w_spatial": sm[:, 8:].reshape(depth, N_GROUPS, CHUNK, CHUNK),
    }
    for n in rest:
        g["w_" + n] = jnp.stack([_sum_slots(landed[(n, l)], landed[(n, l)].shape[1], f"sum_grad_{n}_{l}")
                                 for l in range(depth)])

    names = ["mix_pre_g", "w_in", "b_forget", "sgu_norm_g", "w_spatial", "b_spatial", "w_out", "mix_post_g",
             "ffn_pre_g", "w_gate", "w_up", "w_down", "ffn_post_g"]
    ws = dict(mix_pre_g=mix_pre_g, w_in=w_in, b_forget=b_forget, sgu_norm_g=sgu_norm_g, w_spatial=w_spatial,
              b_spatial=b_spatial, w_out=w_out, mix_post_g=mix_post_g, ffn_pre_g=ffn_pre_g, w_gate=w_gate, w_up=w_up,
              w_down=w_down, ffn_post_g=ffn_post_g)
    ms = dict(mix_pre_g=m_mix_pre_g, w_in=m_w_in, b_forget=m_b_forget, sgu_norm_g=m_sgu_norm_g, w_spatial=m_w_spatial,
              b_spatial=m_b_spatial, w_out=m_w_out, mix_post_g=m_mix_post_g, ffn_pre_g=m_ffn_pre_g, w_gate=m_w_gate,
              w_up=m_w_up, w_down=m_w_down, ffn_post_g=m_ffn_post_g)
    vs = dict(mix_pre_g=v_mix_pre_g, w_in=v_w_in, b_forget=v_b_forget, sgu_norm_g=v_sgu_norm_g, w_spatial=v_w_spatial,
              b_spatial=v_b_spatial, w_out=v_w_out, mix_post_g=v_mix_post_g, ffn_pre_g=v_ffn_pre_g, w_gate=v_w_gate,
              w_up=v_w_up, w_down=v_w_down, ffn_post_g=v_ffn_post_g)
    updates = {}
    last = ["w_in", "mix_pre_g", "b_forget"]
    transposed = ["w_in", "w_gate", "w_up"]
    for n in [n for n in names if n not in last] + last:
        if n == last[0]:
            sum_in_1 = _sum_slots(landed[("in", 1)], 304, "sum_grad_in_1")
            done = [u[0] for u in updates.values()] + [sum_in_1]
            sum_in_0 = _sum_slots(_exchange_wait(flights[("in", 0)], done, "wait_grad_in_0"), 304, "sum_grad_in_0")
            late = _sum_slots(_exchange_wait(flights["late"], [sum_in_0], "wait_late"), 8, "sum_late_small_grads")
            g["w_in"] = lax.dynamic_slice(jnp.stack([sum_in_0, sum_in_1]), (0, me, 0), (depth, SH_IN, D_MODEL))
            g["mix_pre_g"] = jnp.stack([late[0], sm[1, 0]])
            g["b_forget"] = jnp.stack([late[1, :N_HEADS], sm[1, 6, :N_HEADS]])
        if n in transposed:
            view = lambda a: jnp.swapaxes(a, 1, 2)
            updates[n] = [view(u) for u in _adamw_cols(view(ws[n]), g[n], view(ms[n]), view(vs[n]), f"adamw_{n}")]
            g[n] = view(g[n])
        else:
            flat = (-1, ws[n].shape[-1])
            updates[n] = [u.reshape(ws[n].shape) for u in
                          _adamw(ws[n].reshape(flat), g[n].reshape(flat), ms[n].reshape(flat), vs[n].reshape(flat),
                                 f"adamw_{n}")]
    deltas = [updates[n][0] for n in names]
    new_m = [updates[n][1] for n in names]
    new_v = [updates[n][2] for n in names]
    grads = [g[n].reshape(ws[n].shape) for n in names]
    return (loss, grad_x, *grads, *deltas, *new_m, *new_v)
```

```python
import math

import jax
import jax.numpy as jnp
from jax import lax
from jax.experimental import pallas as pl
from jax.experimental.pallas import tpu as pltpu

F32 = jnp.float32
BF16 = jnp.bfloat16

N_DEV = 8
D_MODEL = 1024
N_HEADS = 8
HEAD_DIM = 128
CHUNK = 128
N_GROUPS = 8
D_FF = 2816
IN_WIDTH = 7 * D_MODEL + N_HEADS
IN_PAD = 7680
EPS = 1e-6
GATED = 4 * D_MODEL
F_WIDTH = 512
QKV_WIDTH = 3 * HEAD_DIM
CB_F = GATED // HEAD_DIM
CB_QKV = (GATED + F_WIDTH) // HEAD_DIM
assert GATED + F_WIDTH + N_HEADS * QKV_WIDTH == IN_PAD and (GATED + F_WIDTH) % QKV_WIDTH == 0

ADAM_LR, ADAM_B1, ADAM_B2, ADAM_EPS, ADAM_WD, ADAM_STEP = 0.001, 0.9, 0.999, 1e-08, 0.01, 10

SH_IN = IN_WIDTH // N_DEV
SH_OUT = D_MODEL // N_DEV
SH_FF = D_FF // N_DEV
P_IN = 912
IN_STRIDE = 7 * HEAD_DIM
IN_ROWS = IN_STRIDE * (N_DEV - 1) + P_IN
R_OUT, R_G, R_U, R_D = 0, SH_OUT, SH_OUT + SH_FF, SH_OUT + 2 * SH_FF
P_REST = SH_OUT + 3 * SH_FF
S_ROWS = 136

ATTN_BLOCK = 512
ATTN_GROUP = 256
ROW_BLOCK = 512
MM_TM, MM_TN, MM_TK = 1536, 1536, 2048
VMEM_CAP = 56 << 20
NEG = -1e30


def _tile(n, cap, unit=128):
    if n <= cap:
        return n
    best = None
    for t in range(unit, cap + 1, unit):
        if n % t == 0:
            best = t
    assert best is not None, (n, cap)
    return best


def _nbytes(shape, dtype):
    return math.prod(shape) * jnp.dtype(dtype).itemsize


_HBM = pl.BlockSpec(memory_space=pltpu.HBM)
MESH = pl.DeviceIdType.MESH

_N_REMOTE = N_DEV - 1


def _exchange_shapes(kind, x):
    if kind[0] == "gather":
        shape = x.shape[:kind[1]] + (N_DEV,) + x.shape[kind[1]:]
    elif kind[0] == "spread":
        shape = (N_DEV,) + x.shape
    else:
        shape = (N_DEV, kind[3]) + x.shape[1:]
    return [jax.ShapeDtypeStruct(shape, x.dtype)]


def _exchange_sems(kind):
    del kind
    return [pltpu.SemaphoreType.DMA((_N_REMOTE,)), pltpu.SemaphoreType.DMA((_N_REMOTE,)), pltpu.SemaphoreType.DMA]


def _plan(kind, x_ref, outs, sems):
    send_sems, recv_sems, local_sem = sems
    x, y, c = lax.axis_index("x"), lax.axis_index("y"), lax.axis_index("c")

    def remote(src, dst, k, to):
        return pltpu.make_async_remote_copy(src_ref=src, dst_ref=dst, send_sem=send_sems.at[k], recv_sem=recv_sems.at[k],
                                            device_id=to, device_id_type=MESH)

    sibling = (x, y, 1 - c)
    if kind[0] == "gather":
        out, = outs
        slot = lambda px, py, pc: out.at[(slice(None),) * kind[1] + (4 * px + 2 * py + pc,)]
        chips = [(1 - x, y), (x, 1 - y), (1 - x, 1 - y)]
        local = [pltpu.make_async_copy(x_ref, slot(x, y, c), local_sem)]
        first = [remote(x_ref, slot(x, y, c), 0, sibling)]
        first += [remote(x_ref, slot(x, y, c), 1 + k, (*chip, c)) for k, chip in enumerate(chips)]
        relays = [(remote(x_ref, slot(*chip, c), 1 + k, (*chip, c)), remote(slot(*chip, c), slot(*chip, c), 4 + k, sibling))
                  for k, chip in enumerate(chips)]
        arrivals = [remote(x_ref, slot(x, y, 1 - c), 0, sibling)]
        arrivals += [remote(x_ref, slot(*chip, 1 - c), 4 + k, sibling) for k, chip in enumerate(chips)]
        return local, first, relays, arrivals
    out, = outs
    mine = 4 * x + 2 * y + c
    if kind[0] == "spread":
        src = lambda d: x_ref
    else:
        _, base, stride, length = kind
        src = lambda d: x_ref.at[pl.ds(pl.multiple_of(base + stride * d, 16), length)]
    first, arrivals = [], []
    for j in range(1, N_DEV):
        peer = (1 - x if j & 4 else x, 1 - y if j & 2 else y, 1 - c if j & 1 else c)
        theirs = 4 * peer[0] + 2 * peer[1] + peer[2]
        first.append(remote(src(theirs), out.at[mine], j - 1, peer))
        arrivals.append(remote(src(theirs), out.at[theirs], j - 1, peer))
    local = [] if local_sem is None else [pltpu.make_async_copy(src(mine), out.at[mine], local_sem)]
    return local, first, [], arrivals


def _start(plan):
    local, first, _, _ = plan
    for cp in local + first:
        cp.start()


def _finish(plan):
    local, first, relays, arrivals = plan
    for arrival, onward in relays:
        arrival.wait_recv()
        onward.start()
    for cp in arrivals:
        cp.wait_recv()
    for cp in first + [onward for _, onward in relays]:
        cp.wait_send()
    for cp in local:
        cp.wait()


def _direct_copies(kind, x_ref, land_ref, send_sems, recv_sems):
    _, first, _, arrivals = _plan(kind, x_ref, [land_ref], (send_sems, recv_sems, None))
    return first, arrivals


_SEM = pl.BlockSpec(memory_space=pltpu.SEMAPHORE)
_DATAFLOW = pltpu.SideEffectType.DATAFLOW_SIDE_EFFECTING


def _exchange_start(kind, x, name):
    me = 4 * lax.axis_index("x") + 2 * lax.axis_index("y") + lax.axis_index("c")
    own = x if kind[0] == "spread" else lax.dynamic_slice_in_dim(x, kind[1] + kind[2] * me, kind[3], 0)
    shape = _exchange_shapes(kind, x)[0].shape
    land = lax.dynamic_update_slice_in_dim(lax.empty(shape, x.dtype), own[None], me, 0)

    def body(x_ref, land_ref, send_sems, recv_sems, x_thru, land_thru, token):
        del x_thru, land_thru
        for cp in _direct_copies(kind, x_ref, land_ref, send_sems, recv_sems)[0]:
            cp.start()
        token[...] = jnp.zeros_like(token)

    *handle, token = pl.pallas_call(
        body, name=name,
        out_shape=(pltpu.SemaphoreType.DMA((_N_REMOTE,)), pltpu.SemaphoreType.DMA((_N_REMOTE,)),
                   pltpu.HBM(x.shape, x.dtype), pltpu.HBM(land.shape, land.dtype), jax.ShapeDtypeStruct((8, 128), F32)),
        in_specs=(_HBM, _HBM), out_specs=(_SEM, _SEM, _HBM, _HBM, pl.BlockSpec(memory_space=pltpu.VMEM)),
        input_output_aliases={0: 2, 1: 3}, compiler_params=pltpu.CompilerParams(has_side_effects=_DATAFLOW),
    )(pltpu.with_memory_space_constraint(x, pltpu.HBM), pltpu.with_memory_space_constraint(land, pltpu.HBM))
    return (kind, *handle), token


def _exchange_wait(handle, after, name):
    kind, send_sems, recv_sems, x_thru, land_thru = handle

    def body(x_ref, land_ref, send_sems, recv_sems, *rest):
        first, arrivals = _direct_copies(kind, x_ref, land_ref, send_sems, recv_sems)
        for cp in first:
            cp.wait_send()
        for cp in arrivals:
            cp.wait_recv()

    return pl.pallas_call(
        body, name=name,
        out_shape=(pltpu.HBM(x_thru.shape, x_thru.dtype), pltpu.HBM(land_thru.shape, land_thru.dtype)),
        in_specs=(_HBM, _HBM, _SEM, _SEM) + (pl.BlockSpec(memory_space=pl.ANY),) * len(after), out_specs=(_HBM, _HBM),
        input_output_aliases={0: 0, 1: 1}, compiler_params=pltpu.CompilerParams(has_side_effects=_DATAFLOW),
    )(x_thru, land_thru, send_sems, recv_sems, *after)[1]


def _exchange(kind, x, name):
    n_out = len(_exchange_shapes(kind, x))

    def body(x_ref, *refs):
        plan = _plan(kind, x_ref, refs[:n_out], refs[n_out:])
        _start(plan)
        _finish(plan)

    return pl.pallas_call(body, name=name, out_shape=_exchange_shapes(kind, x), in_specs=[_HBM],
                          out_specs=[_HBM] * n_out, scratch_shapes=_exchange_sems(kind))(x)


def _call(body, name, grid, in_specs, out_specs, out_shape, args, scratch=(), sem=None, nbytes=0, aliases=None,
          comm=(), deps=()):
    in_specs, out_specs, out_shape, args, scratch = (list(in_specs), list(out_specs), list(out_shape), list(args),
                                                     list(scratch))
    if deps:
        n_real, n_deps, unordered = len(args), len(deps), body

        def body(*refs):
            unordered(*refs[:n_real], *refs[n_real + n_deps:])

        in_specs += [pl.BlockSpec(memory_space=pl.ANY)] * n_deps
        args += list(deps)
    if comm:
        n_in, n_out, n_scr, n_ops = len(args), len(out_shape), len(scratch), len(comm)
        kinds = [kind for kind, _ in comm]
        shapes = [_exchange_shapes(kind, x) for kind, x in comm]
        inner = body

        def body(*refs):
            ins, refs = refs[:n_in], refs[n_in:]
            cins, refs = refs[:n_ops], refs[n_ops:]
            outs, refs = refs[:n_out], refs[n_out:]
            couts = []
            for sh in shapes:
                couts.append(refs[:len(sh)])
                refs = refs[len(sh):]
            own_scratch, sems = refs[:n_scr], refs[n_scr:]
            first = last = None
            for axis, size in enumerate(grid):
                at_start, at_end = pl.program_id(axis) == 0, pl.program_id(axis) == size - 1
                first = at_start if first is None else first & at_start
                last = at_end if last is None else last & at_end
            plans = [_plan(kinds[o], cins[o], couts[o], sems[3 * o:3 * o + 3]) for o in range(n_ops)]

            @pl.when(first)
            def _():
                for p in plans:
                    _start(p)

            inner(*ins, *outs, *own_scratch)

            @pl.when(last)
            def _():
                for p in plans:
                    _finish(p)

        in_specs += [_HBM] * n_ops
        args += [x for _, x in comm]
        for kind, sh in zip(kinds, shapes):
            out_shape += sh
            out_specs += [_HBM] * len(sh)
            scratch += _exchange_sems(kind)
        sem = ("arbitrary",) * len(grid)
    limit = int(min(max(2 * nbytes + (8 << 20), 32 << 20), VMEM_CAP))
    return pl.pallas_call(
        body, name=name, grid=grid, in_specs=in_specs, out_specs=out_specs, out_shape=out_shape,
        scratch_shapes=scratch, input_output_aliases=aliases or {},
        compiler_params=pltpu.CompilerParams(dimension_semantics=sem, vmem_limit_bytes=limit),
    )(*args)


def _sum_slots(x, tr, name):
    n, r, c = x.shape

    def body(x_ref, o_ref):
        acc = x_ref[0].astype(F32)
        for d in range(1, n):
            acc = acc + x_ref[d].astype(F32)
        o_ref[...] = acc

    return _call(body, name, (r // tr,), [pl.BlockSpec((n, tr, c), lambda i: (0, i, 0))],
                 [pl.BlockSpec((tr, c), lambda i: (i, 0))], [jax.ShapeDtypeStruct((r, c), F32)], [x],
                 sem=("parallel",), nbytes=_nbytes((n, tr, c), x.dtype) + _nbytes((tr, c), F32))[0]


_DIMS = {"nn": (((1,), (0,)), ((), ())), "nt": (((1,), (1,)), ((), ())), "tn": (((0,), (0,)), ((), ()))}


def _matmul(a, b, mode, out_dtype, name, comm=(), deps=()):
    if mode == "nn":
        (m, k), (k2, n) = a.shape, b.shape
    elif mode == "nt":
        (m, k), (n, k2) = a.shape, b.shape
    else:
        (k, m), (k2, n) = a.shape, b.shape
    assert k == k2, (a.shape, b.shape, mode)
    tm, tn, tk = _tile(m, MM_TM), _tile(n, MM_TN), _tile(k, MM_TK)
    nk = k // tk
    dims = _DIMS[mode]
    a_spec = {"nn": pl.BlockSpec((tm, tk), lambda i, j, kk: (i, kk)),
              "nt": pl.BlockSpec((tm, tk), lambda i, j, kk: (i, kk)),
              "tn": pl.BlockSpec((tk, tm), lambda i, j, kk: (kk, i))}[mode]
    b_spec = {"nn": pl.BlockSpec((tk, tn), lambda i, j, kk: (kk, j)),
              "nt": pl.BlockSpec((tn, tk), lambda i, j, kk: (j, kk)),
              "tn": pl.BlockSpec((tk, tn), lambda i, j, kk: (kk, j))}[mode]

    def partial_product(a_ref, b_ref):
        return lax.dot_general(a_ref[...].astype(BF16), b_ref[...].astype(BF16), dims, preferred_element_type=F32)

    if nk == 1:
        def body(a_ref, b_ref, o_ref):
            o_ref[...] = partial_product(a_ref, b_ref).astype(o_ref.dtype)
        scratch = []
    else:
        def body(a_ref, b_ref, o_ref, acc_ref):
            kk = pl.program_id(2)

            @pl.when(kk == 0)
            def _():
                acc_ref[...] = jnp.zeros_like(acc_ref)

            acc_ref[...] += partial_product(a_ref, b_ref)

            @pl.when(kk == nk - 1)
            def _():
                o_ref[...] = acc_ref[...].astype(o_ref.dtype)
        scratch = [pltpu.VMEM((tm, tn), F32)]

    nbytes = (_nbytes((tm, tk), a.dtype) + _nbytes((tk, tn), b.dtype) + _nbytes((tm, tn), out_dtype)
              + _nbytes((tm, tn), F32))
    res = _call(body, name, (m // tm, n // tn, nk), [a_spec, b_spec],
                [pl.BlockSpec((tm, tn), lambda i, j, kk: (i, j))], [jax.ShapeDtypeStruct((m, n), out_dtype)], [a, b],
                scratch=scratch, sem=("parallel", "parallel", "arbitrary"), nbytes=nbytes, comm=comm, deps=deps)
    return res if comm else res[0]


def _rms(x):
    return lax.rsqrt(jnp.mean(x * x, axis=-1, keepdims=True) + EPS)


def _rms_bwd(dz, a, g):
    r = _rms(a)
    dzg = dz * g
    da = r * dzg - a * (r * r * r) * jnp.mean(dzg * a, axis=-1, keepdims=True)
    return da, dz * (a * r)


def _row_spec(tr, width):
    return pl.BlockSpec((tr, width), lambda i: (i, 0))


def _vec_spec(width):
    return pl.BlockSpec((1, width), lambda i: (0, 0))


def _rms_fwd(x, g, name):
    t, d = x.shape
    tr = min(ROW_BLOCK, t)

    def body(x_ref, g_ref, h_ref):
        xv = x_ref[...]
        h_ref[...] = ((xv * _rms(xv)) * g_ref[...]).astype(BF16)

    return _call(body, name, (t // tr,), [_row_spec(tr, d), _vec_spec(d)], [_row_spec(tr, d)],
                 [jax.ShapeDtypeStruct((t, d), BF16)], [x, g], sem=("parallel",), nbytes=3 * _nbytes((tr, d), F32))[0]


def _post_norm_fwd(x, o, g_post, g_next, name):
    t, d = x.shape
    tr = min(ROW_BLOCK, t)

    def body(x_ref, o_ref, gp_ref, gn_ref, xn_ref, h_ref):
        ov = o_ref[...]
        xn = x_ref[...] + (ov * _rms(ov)) * gp_ref[...]
        xn_ref[...] = xn
        h_ref[...] = ((xn * _rms(xn)) * gn_ref[...]).astype(BF16)

    return _call(body, name, (t // tr,), [_row_spec(tr, d), _row_spec(tr, d), _vec_spec(d), _vec_spec(d)],
                 [_row_spec(tr, d), _row_spec(tr, d)],
                 [jax.ShapeDtypeStruct((t, d), F32), jax.ShapeDtypeStruct((t, d), BF16)], [x, o, g_post, g_next],
                 sem=("parallel",), nbytes=5 * _nbytes((tr, d), F32))


def _post_norm_loss(x, o, g_post, target, name):
    t, d = x.shape
    tr = min(ROW_BLOCK, t)

    def body(x_ref, o_ref, gp_ref, t_ref, dy_ref, loss_ref):
        ov = o_ref[...]
        err = x_ref[...] + (ov * _rms(ov)) * gp_ref[...] - t_ref[...]
        dy_ref[...] = err / d
        part = 0.5 * jnp.sum(jnp.mean(err * err, axis=-1, keepdims=True), axis=0, keepdims=True)

        @pl.when(pl.program_id(0) == 0)
        def _():
            loss_ref[...] = jnp.zeros_like(loss_ref)

        loss_ref[...] += part

    return _call(body, name, (t // tr,), [_row_spec(tr, d), _row_spec(tr, d), _vec_spec(d), _row_spec(tr, d)],
                 [_row_spec(tr, d), pl.BlockSpec((1, 1), lambda i: (0, 0))],
                 [jax.ShapeDtypeStruct((t, d), F32), jax.ShapeDtypeStruct((1, 1), F32)], [x, o, g_post, target],
                 sem=("arbitrary",), nbytes=5 * _nbytes((tr, d), F32))


def _norm_bwd(dskip, name, pre=None, post=None):
    t, d = dskip.shape
    tr = min(ROW_BLOCK, t)
    n_in = 1 + (3 if pre else 0) + (2 if post else 0)

    def body(*refs):
        ins, outs = list(refs[:n_in]), list(refs[n_in:])
        first = pl.program_id(0) == 0
        dx = ins.pop(0)[...]
        if pre:
            dh_ref, xin_ref, gpre_ref = ins.pop(0), ins.pop(0), ins.pop(0)
            dxin, dg_rows = _rms_bwd(dh_ref[...].astype(F32), xin_ref[...], gpre_ref[...])
            dx = dx + dxin
            dx_ref, dgpre_ref = outs.pop(0), outs.pop(0)
            dx_ref[...] = dx

            @pl.when(first)
            def _():
                dgpre_ref[...] = jnp.zeros_like(dgpre_ref)

            dgpre_ref[...] += jnp.sum(dg_rows, axis=0, keepdims=True)
        if post:
            a_ref, gpost_ref = ins.pop(0), ins.pop(0)
            da, dg_rows = _rms_bwd(dx, a_ref[...], gpost_ref[...])
            da_ref, dgpost_ref = outs.pop(0), outs.pop(0)
            da_ref[...] = da.astype(BF16)

            @pl.when(first)
            def _():
                dgpost_ref[...] = jnp.zeros_like(dgpost_ref)

            dgpost_ref[...] += jnp.sum(dg_rows, axis=0, keepdims=True)

    args, in_specs, out_shape, out_specs = [dskip], [_row_spec(tr, d)], [], []
    if pre:
        args += list(pre)
        in_specs += [_row_spec(tr, d), _row_spec(tr, d), _vec_spec(d)]
        out_shape += [jax.ShapeDtypeStruct((t, d), F32), jax.ShapeDtypeStruct((1, d), F32)]
        out_specs += [_row_spec(tr, d), _vec_spec(d)]
    if post:
        args += list(post)
        in_specs += [_row_spec(tr, d), _vec_spec(d)]
        out_shape += [jax.ShapeDtypeStruct((t, d), BF16), jax.ShapeDtypeStruct((1, d), F32)]
        out_specs += [_row_spec(tr, d), _vec_spec(d)]
    return _call(body, name, (t // tr,), in_specs, out_specs, out_shape, args, sem=("arbitrary",),
                 nbytes=7 * _nbytes((tr, d), F32))


def _sigmoid(x):
    return 1.0 / (1.0 + jnp.exp(-x))


def _swiglu_fwd(gu, name):
    t = gu.shape[0]
    tr = min(ROW_BLOCK, t)

    def body(gu_ref, a_ref):
        gate, up = gu_ref[:, :D_FF].astype(F32), gu_ref[:, D_FF:].astype(F32)
        a_ref[...] = ((gate * _sigmoid(gate)) * up).astype(BF16)

    return _call(body, name, (t // tr,), [_row_spec(tr, 2 * D_FF)], [_row_spec(tr, D_FF)],
                 [jax.ShapeDtypeStruct((t, D_FF), BF16)], [gu], sem=("parallel",),
                 nbytes=2 * _nbytes((tr, 2 * D_FF), F32))[0]


def _swiglu_bwd(gu, dact, name):
    t = gu.shape[0]
    tr = min(ROW_BLOCK, t)

    def body(gu_ref, da_ref, o_ref):
        gate, up = gu_ref[:, :D_FF].astype(F32), gu_ref[:, D_FF:].astype(F32)
        da = da_ref[...].astype(F32)
        s = _sigmoid(gate)
        o_ref[:, :D_FF] = (da * up * (s * (1.0 + gate * (1.0 - s)))).astype(BF16)
        o_ref[:, D_FF:] = (da * (gate * s)).astype(BF16)

    return _call(body, name, (t // tr,), [_row_spec(tr, 2 * D_FF), _row_spec(tr, D_FF)], [_row_spec(tr, 2 * D_FF)],
                 [jax.ShapeDtypeStruct((t, 2 * D_FF), BF16)], [gu, dact], sem=("parallel",),
                 nbytes=3 * _nbytes((tr, 2 * D_FF), F32))[0]


def _split3(x):
    hi = x.astype(BF16)
    r1 = x - hi.astype(F32)
    mid = r1.astype(BF16)
    lo = (r1 - mid.astype(F32)).astype(BF16)
    return hi, mid, lo


def _dot_exact(mat01, x):
    hi, mid, lo = _split3(x)
    out = jnp.dot(mat01, hi, preferred_element_type=F32)
    out = out + jnp.dot(mat01, mid, preferred_element_type=F32)
    return out + jnp.dot(mat01, lo, preferred_element_type=F32)


def _dot_exact_rhs(x, mat01):
    hi, mid, lo = _split3(x)
    out = jnp.dot(hi, mat01, preferred_element_type=F32)
    out = out + jnp.dot(mid, mat01, preferred_element_type=F32)
    return out + jnp.dot(lo, mat01, preferred_element_type=F32)


def _tri(lower):
    r = lax.broadcasted_iota(jnp.int32, (CHUNK, CHUNK), 0)
    c = lax.broadcasted_iota(jnp.int32, (CHUNK, CHUNK), 1)
    return jnp.where(r >= c if lower else r <= c, 1.0, 0.0).astype(BF16)


def _log_sigmoid(z):
    return jnp.minimum(z, 0.0) - jnp.log(1.0 + jnp.exp(-jnp.abs(z)))


def _forget_cumsum(proj, b_pad, name):
    t = proj.shape[0]
    nb = t // CHUNK

    def body(f_ref, b_ref, c_ref):
        tri = _tri(True)
        b = b_ref[...]

        def blk(i, carry):
            rows = pl.ds(pl.multiple_of(i * CHUNK, CHUNK), CHUNK)
            cs = _dot_exact(tri, _log_sigmoid(f_ref[rows, :] + b)) + carry
            c_ref[rows, :] = cs
            return cs[CHUNK - 1:CHUNK, :]

        lax.fori_loop(0, nb, blk, jnp.zeros((1, HEAD_DIM), F32))

    return _call(body, name, (1,),
                 [pl.BlockSpec((t, HEAD_DIM), lambda i: (0, CB_F)), pl.BlockSpec((1, HEAD_DIM), lambda i: (0, 0))],
                 [pl.BlockSpec((t, HEAD_DIM), lambda i: (0, 0))], [jax.ShapeDtypeStruct((t, HEAD_DIM), F32)],
                 [proj, b_pad], sem=("arbitrary",), nbytes=2 * _nbytes((t, HEAD_DIM), F32))[0]


def _forget_bwd(proj, b_pad, dc_key, dc_query, dproj, name):
    t = proj.shape[0]
    tr = min(ROW_BLOCK, t)
    nb = t // tr
    rev = lambda i: nb - 1 - i

    def body(f_ref, b_ref, dck_ref, dcq_ref, dproj_in, df_ref, db_ref, run_ref):
        del dproj_in

        @pl.when(pl.program_id(0) == 0)
        def _():
            run_ref[...] = jnp.zeros_like(run_ref)
            db_ref[...] = jnp.zeros_like(db_ref)

        tri = _tri(False)
        b = b_ref[...]
        lane = lax.broadcasted_iota(jnp.int32, (CHUNK, HEAD_DIM), 1)
        df_ref[...] = jnp.zeros_like(df_ref)
        for n in reversed(range(tr // CHUNK)):
            rows = slice(n * CHUNK, (n + 1) * CHUNK)
            dc = dck_ref[rows, :]
            for h in range(N_HEADS):
                dc = dc + jnp.where(lane == h, dcq_ref[rows, h * HEAD_DIM:(h + 1) * HEAD_DIM], 0.0)
            dlogf = _dot_exact(tri, dc) + run_ref[...]
            run_ref[...] = dlogf[0:1, :]
            z = f_ref[rows, :] + b
            e = jnp.exp(-jnp.abs(z))
            sig_neg = jnp.where(z >= 0.0, e, 1.0) / (1.0 + e)
            df = jnp.where(lane < N_HEADS, dlogf * sig_neg, 0.0)
            df_ref[rows, :HEAD_DIM] = df.astype(BF16)
            db_ref[...] += jnp.sum(df, axis=0, keepdims=True)

    return _call(body, name, (nb,),
                 [pl.BlockSpec((tr, HEAD_DIM), lambda i: (rev(i), CB_F)), pl.BlockSpec((1, HEAD_DIM), lambda i: (0, 0)),
                  pl.BlockSpec((tr, HEAD_DIM), lambda i: (rev(i), 0)), pl.BlockSpec((tr, D_MODEL), lambda i: (rev(i), 0)),
                  pl.BlockSpec(memory_space=pl.ANY)],
                 [pl.BlockSpec((tr, F_WIDTH), lambda i: (rev(i), GATED // F_WIDTH)),
                  pl.BlockSpec((1, HEAD_DIM), lambda i: (0, 0))],
                 [jax.ShapeDtypeStruct(dproj.shape, dproj.dtype), jax.ShapeDtypeStruct((1, HEAD_DIM), F32)],
                 [proj, b_pad, dc_key, dc_query, dproj], scratch=[pltpu.VMEM((1, HEAD_DIM), F32)],
                 sem=("arbitrary",), nbytes=4 * _nbytes((tr, D_MODEL), F32), aliases={4: 0})


def _causal(block):
    r = lax.broadcasted_iota(jnp.int32, (block, block), 0)
    c = lax.broadcasted_iota(jnp.int32, (block, block), 1)
    return c <= r


def _lanes(x, width):
    return jnp.concatenate([x] * (width // HEAD_DIM), axis=1)


def _attn_fwd(proj, c_pad, c_row, name, comm=()):
    t = proj.shape[0]
    blk = min(ATTN_BLOCK, t)
    nq = t // blk
    scale = HEAD_DIM ** -0.5

    def body(q_ref, k_ref, v_ref, cp_ref, cr_ref, y_ref, e_ref):
        h, i = pl.program_id(0), pl.program_id(1)
        q = (q_ref[...] * scale).astype(BF16)
        lane = lax.broadcasted_iota(jnp.int32, (blk, HEAD_DIM), 1)
        ci = jnp.sum(jnp.where(lane == h, cp_ref[...], 0.0), axis=1, keepdims=True)

        def step(j, carry, diagonal):
            m, l, acc = carry
            off = pl.multiple_of(j * blk, blk)
            k = k_ref[pl.ds(off, blk), :].astype(BF16)
            v = v_ref[pl.ds(off, blk), :].astype(BF16)
            s = lax.dot_general(q, k, _DIMS["nt"], preferred_element_type=F32)
            s = s + ci - cr_ref[:, pl.ds(off, blk)]
            if diagonal:
                s = jnp.where(_causal(blk), s, NEG)
            m_new = jnp.maximum(m, jnp.max(s, axis=1, keepdims=True))
            alpha = jnp.exp(m - m_new)
            p = jnp.exp(s - m_new)
            l = alpha * l + jnp.sum(p, axis=1, keepdims=True)
            acc = alpha * acc + jnp.dot(p.astype(BF16), v, preferred_element_type=F32)
            return m_new, l, acc

        init = (jnp.full((blk, 1), NEG, F32), jnp.zeros((blk, 1), F32), jnp.zeros((blk, HEAD_DIM), F32))
        carry = lax.fori_loop(0, i, lambda j, c: step(j, c, False), init)
        m, l, acc = step(i, carry, True)
        y_ref[...] = acc / l
        e_ref[...] = jnp.broadcast_to(ci - (m + jnp.log(l)), (blk, HEAD_DIM))

    nbytes = 2 * _nbytes((t, HEAD_DIM), F32) + 4 * _nbytes((blk, HEAD_DIM), F32) + 4 * _nbytes((blk, blk), F32)
    head_block = pl.BlockSpec((blk, HEAD_DIM), lambda h, i: (i, h))
    return _call(body, name, (N_HEADS, nq),
                 [pl.BlockSpec((blk, HEAD_DIM), lambda h, i: (i, CB_QKV + 3 * h)),
                  pl.BlockSpec((t, HEAD_DIM), lambda h, i: (0, CB_QKV + 3 * h + 1)),
                  pl.BlockSpec((t, HEAD_DIM), lambda h, i: (0, CB_QKV + 3 * h + 2)),
                  pl.BlockSpec((blk, HEAD_DIM), lambda h, i: (i, 0)),
                  pl.BlockSpec((None, 1, t), lambda h, i: (h, 0, 0))],
                 [head_block, head_block],
                 [jax.ShapeDtypeStruct((t, D_MODEL), F32), jax.ShapeDtypeStruct((t, D_MODEL), F32)],
                 [proj, proj, proj, c_pad, c_row], sem=("parallel", "arbitrary"), nbytes=nbytes, comm=comm)


def _attn_bwd(proj, dy, e_b, delta_b, c_row, dproj, name, comm=()):
    t = proj.shape[0]
    blk = min(ATTN_BLOCK, t)
    nb = t // blk
    scale = HEAD_DIM ** -0.5

    def body(q_ref, k_ref, v_ref, dy_ref, e_ref, dl_ref, cr_ref, dproj_in, dqkv_ref, dc_ref, dr_ref, dq_acc):
        del dproj_in
        j = pl.program_id(1)

        @pl.when(j == 0)
        def _():
            dq_acc[...] = jnp.zeros_like(dq_acc)
            dr_ref[...] = jnp.zeros_like(dr_ref)

        k = k_ref[...].astype(BF16)
        v = v_ref[...].astype(BF16)
        crow = cr_ref[...]

        def step(i, carry, diagonal):
            dk, dv, dcs = carry
            rows = pl.ds(pl.multiple_of(i * blk, blk), blk)
            q = (q_ref[rows, :] * scale).astype(BF16)
            dyi = dy_ref[rows, :].astype(BF16)
            s = lax.dot_general(q, k, _DIMS["nt"], preferred_element_type=F32) + _lanes(e_ref[rows, :], blk) - crow
            if diagonal:
                s = jnp.where(_causal(blk), s, NEG)
            p = jnp.exp(s)
            dv = dv + lax.dot_general(p.astype(BF16), dyi, _DIMS["tn"], preferred_element_type=F32)
            dp = lax.dot_general(dyi, v, _DIMS["nt"], preferred_element_type=F32)
            ds = p * (dp - _lanes(dl_ref[rows, :], blk))
            dsb = ds.astype(BF16)
            dk = dk + lax.dot_general(dsb, q, _DIMS["tn"], preferred_element_type=F32)
            dq_acc[rows, :] += jnp.dot(dsb, k, preferred_element_type=F32) * scale
            dr_ref[rows, :] += jnp.sum(ds, axis=1, keepdims=True)
            return dk, dv, dcs + jnp.sum(ds, axis=0, keepdims=True)

        zeros = jnp.zeros((blk, HEAD_DIM), F32)
        carry = step(j, (zeros, zeros, jnp.zeros((1, blk), F32)), True)
        dk, dv, dcs = lax.fori_loop(j + 1, nb, lambda i, c: step(i, c, False), carry)
        mine = pl.ds(pl.multiple_of(j * blk, blk), blk)
        dqkv_ref[:, :HEAD_DIM] = dq_acc[mine, :].astype(BF16)
        dqkv_ref[:, HEAD_DIM:2 * HEAD_DIM] = dk.astype(BF16)
        dqkv_ref[:, 2 * HEAD_DIM:] = dv.astype(BF16)
        dc_ref[...] = -dcs

    full = lambda cb: pl.BlockSpec((t, HEAD_DIM), lambda h, j: (0, cb(h)))
    head = lambda h: h
    nbytes = (6 * _nbytes((t, HEAD_DIM), F32) + 4 * _nbytes((blk, HEAD_DIM), F32) + 5 * _nbytes((blk, blk), F32))
    return _call(body, name, (N_HEADS, nb),
                 [full(lambda h: CB_QKV + 3 * h),
                  pl.BlockSpec((blk, HEAD_DIM), lambda h, j: (j, CB_QKV + 3 * h + 1)),
                  pl.BlockSpec((blk, HEAD_DIM), lambda h, j: (j, CB_QKV + 3 * h + 2)),
                  full(head), full(head), full(head),
                  pl.BlockSpec((None, 1, blk), lambda h, j: (h, 0, j)),
                  pl.BlockSpec(memory_space=pl.ANY)],
                 [pl.BlockSpec((blk, QKV_WIDTH), lambda h, j: (j, (GATED + F_WIDTH) // QKV_WIDTH + h)),
                  pl.BlockSpec((None, 1, blk), lambda h, j: (h, 0, j)),
                  full(head)],
                 [jax.ShapeDtypeStruct(dproj.shape, dproj.dtype), jax.ShapeDtypeStruct((N_HEADS, 1, t), F32),
                  jax.ShapeDtypeStruct((t, D_MODEL), F32)],
                 [proj, proj, proj, dy, e_b, delta_b, c_row, dproj], scratch=[pltpu.VMEM((t, HEAD_DIM), F32)],
                 sem=("parallel", "arbitrary"), nbytes=nbytes, aliases={7: 0}, comm=comm)


def _chunks(t, blk):
    return [slice(r * blk, (r + 1) * blk) for r in range(t // blk)]


def _lane_of_head(x, h):
    lane = lax.broadcasted_iota(jnp.int32, x.shape, 1)
    return jnp.broadcast_to(jnp.sum(jnp.where(lane == h, x, 0.0), axis=1, keepdims=True), x.shape)


LOG2E = 1.4426950408889634
LN2 = 0.6931471805599453


def _with_bias(x, c, key_side):
    hi, mid, lo = _split3(c * LOG2E)
    lane = lax.broadcasted_iota(jnp.int32, c.shape, 1)
    one, zero = jnp.ones_like(hi), jnp.zeros_like(hi)
    if key_side:
        extra = jnp.where(lane < 3, one, jnp.where(lane == 3, -hi, jnp.where(lane == 4, -mid,
                                                                            jnp.where(lane == 5, -lo, zero))))
        return jnp.concatenate([x.astype(BF16), extra], axis=1)
    extra = jnp.where(lane == 0, hi, jnp.where(lane == 1, mid, jnp.where(lane == 2, lo,
                                                                        jnp.where(lane < 6, one, zero))))
    return jnp.concatenate([(x * LOG2E).astype(BF16), extra], axis=1)


def _causal_t(block):
    r = lax.broadcasted_iota(jnp.int32, (block, block), 0)
    c = lax.broadcasted_iota(jnp.int32, (block, block), 1)
    return r <= c


def _attn_fwd_t(proj, c_pad, c_row, name, comm=()):
    t = proj.shape[0]
    blk = min(ATTN_BLOCK, t)
    sub = min(ATTN_GROUP, blk)
    nq = t // blk
    scale = HEAD_DIM ** -0.5

    def body(q_ref, k_ref, v_ref, cp_ref, cr_ref, y_ref, e_ref, kb_s, vt_s):
        h, i = pl.program_id(0), pl.program_id(1)

        @pl.when(i == 0)
        def _():
            for rows in _chunks(t, blk):
                kb_s[rows, :] = _with_bias(k_ref[rows, :], _lane_of_head(cp_ref[rows, :], h), key_side=True)
                vt_s[:, rows] = v_ref[rows, :].T.astype(BF16)

        mine = pl.ds(pl.multiple_of(i * blk, blk), blk)
        q = _with_bias(q_ref[...] * scale, _lane_of_head(cp_ref[mine, :], h), key_side=False)

        groups = [slice(g * sub, (g + 1) * sub) for g in range(blk // sub)]

        def step(j, carry, diagonal):
            rows = pl.ds(pl.multiple_of(j * blk, blk), blk)
            kj, vtj = kb_s[rows, :], vt_s[:, rows]
            scores = [lax.dot_general(kj, q[g, :], _DIMS["nt"], preferred_element_type=F32) for g in groups]
            out = []
            for g, s, (m, l, acc) in zip(groups, scores, carry):
                if diagonal:
                    s = jnp.where(_causal_t(blk)[:, g], s, NEG)
                m_new = jnp.maximum(m, jnp.max(s, axis=0, keepdims=True))
                alpha = jnp.exp2(m - m_new)
                p = jnp.exp2(s - m_new)
                l = alpha * l + jnp.sum(p, axis=0, keepdims=True)
                acc = alpha * acc + jnp.dot(vtj, p.astype(BF16), preferred_element_type=F32)
                out.append((m_new, l, acc))
            return tuple(out)

        init = tuple((jnp.full((1, sub), NEG, F32), jnp.zeros((1, sub), F32), jnp.zeros((HEAD_DIM, sub), F32))
                     for _ in groups)
        carry = lax.fori_loop(0, i, lambda j, c: step(j, c, False), init)
        ci = cr_ref[:, mine]
        for g, (m, l, acc) in zip(groups, step(i, carry, True)):
            y_ref[g, :] = (acc / l).T
            e_ref[:, g] = ci[:, g] - (m * LN2 + jnp.log(l))

    nbytes = 3 * _nbytes((t, HEAD_DIM), F32) + 4 * _nbytes((blk, HEAD_DIM), F32) + 4 * _nbytes((blk, blk), F32)
    return _call(body, name, (N_HEADS, nq),
                 [pl.BlockSpec((blk, HEAD_DIM), lambda h, i: (i, CB_QKV + 3 * h)),
                  pl.BlockSpec((t, HEAD_DIM), lambda h, i: (0, CB_QKV + 3 * h + 1)),
                  pl.BlockSpec((t, HEAD_DIM), lambda h, i: (0, CB_QKV + 3 * h + 2)),
                  pl.BlockSpec((t, HEAD_DIM), lambda h, i: (0, 0)),
                  pl.BlockSpec((None, 1, t), lambda h, i: (h, 0, 0))],
                 [pl.BlockSpec((blk, HEAD_DIM), lambda h, i: (i, h)),
                  pl.BlockSpec((None, 1, blk), lambda h, i: (h, 0, i))],
                 [jax.ShapeDtypeStruct((t, D_MODEL), F32), jax.ShapeDtypeStruct((N_HEADS, 1, t), F32)],
                 [proj, proj, proj, c_pad, c_row],
                 scratch=[pltpu.VMEM((t, 2 * HEAD_DIM), BF16), pltpu.VMEM((HEAD_DIM, t), BF16)],
                 sem=("parallel", "arbitrary"), nbytes=nbytes, comm=comm)


def _attn_bwd_t(proj, dy, e_row, delta_b, c_pad, dproj, name, deps=()):
    t = proj.shape[0]
    blk = min(ATTN_BLOCK, t)
    nb = t // blk
    scale = HEAD_DIM ** -0.5

    def body(q_ref, k_ref, v_ref, dy_ref, e_ref, dl_ref, cp_ref, dproj_in, dqkv_ref, dck_ref, dcq_ref,
             qb_s, qt_s, dyb_s, dyt_s, dl_s, dqt_acc):
        del dproj_in
        h, j = pl.program_id(0), pl.program_id(1)

        @pl.when(j == 0)
        def _():
            for rows in _chunks(t, blk):
                qs = q_ref[rows, :] * scale
                e_col = jnp.broadcast_to(e_ref[:, rows], (HEAD_DIM, blk)).T
                qb_s[rows, :] = _with_bias(qs, e_col, key_side=False)
                qt_s[:, rows] = qs.T.astype(BF16)
                dyr = dy_ref[rows, :]
                dyb_s[rows, :] = dyr.astype(BF16)
                dyt_s[:, rows] = dyr.T.astype(BF16)
                dl_s[:, rows] = dl_ref[rows, :].T[0:1, :]
            dqt_acc[...] = jnp.zeros_like(dqt_acc)
            dcq_ref[...] = jnp.zeros_like(dcq_ref)

        kf = k_ref[...]
        kb, kt = _with_bias(kf, _lane_of_head(cp_ref[...], h), key_side=True), kf.T.astype(BF16)
        vb = v_ref[...].astype(BF16)

        def step(i, carry, diagonal):
            dkt, dvt, dkey = carry
            cols = pl.ds(pl.multiple_of(i * blk, blk), blk)
            s = lax.dot_general(kb, qb_s[cols, :], _DIMS["nt"], preferred_element_type=F32)
            if diagonal:
                s = jnp.where(_causal_t(blk), s, NEG)
            p = jnp.exp2(s)
            dp = lax.dot_general(vb, dyb_s[cols, :], _DIMS["nt"], preferred_element_type=F32)
            ds = p * (dp - dl_s[:, cols])
            pb, dsb = p.astype(BF16), ds.astype(BF16)
            dvt = dvt + lax.dot_general(dyt_s[:, cols], pb, _DIMS["nt"], preferred_element_type=F32)
            dkt = dkt + lax.dot_general(qt_s[:, cols], dsb, _DIMS["nt"], preferred_element_type=F32)
            dqt_acc[:, cols] += jnp.dot(kt, dsb, preferred_element_type=F32) * scale
            dcq_ref[:, cols] += jnp.sum(ds, axis=0, keepdims=True)
            for k in range(blk // HEAD_DIM):
                dkey = dkey + ds[:, k * HEAD_DIM:(k + 1) * HEAD_DIM]
            return dkt, dvt, dkey

        zeros = jnp.zeros((HEAD_DIM, blk), F32)
        carry = step(j, (zeros, zeros, jnp.zeros((blk, HEAD_DIM), F32)), True)
        dkt, dvt, dkey = lax.fori_loop(j + 1, nb, lambda i, c: step(i, c, False), carry)
        mine = pl.ds(pl.multiple_of(j * blk, blk), blk)
        dqkv_ref[:, :HEAD_DIM] = dqt_acc[:, mine].T.astype(BF16)
        dqkv_ref[:, HEAD_DIM:2 * HEAD_DIM] = dkt.T.astype(BF16)
        dqkv_ref[:, 2 * HEAD_DIM:] = dvt.T.astype(BF16)
        dck_ref[...] = -jnp.sum(dkey.T, axis=0, keepdims=True)

    full = lambda cb: pl.BlockSpec((t, HEAD_DIM), lambda h, j: (0, cb(h)))
    head = lambda h: h
    row = pl.BlockSpec((None, 1, t), lambda h, j: (h, 0, 0))
    nbytes = (8 * _nbytes((t, HEAD_DIM), F32) + 4 * _nbytes((blk, HEAD_DIM), F32) + 6 * _nbytes((blk, blk), F32))
    return _call(body, name, (N_HEADS, nb),
                 [full(lambda h: CB_QKV + 3 * h),
                  pl.BlockSpec((blk, HEAD_DIM), lambda h, j: (j, CB_QKV + 3 * h + 1)),
                  pl.BlockSpec((blk, HEAD_DIM), lambda h, j: (j, CB_QKV + 3 * h + 2)),
                  full(head), row, full(head),
                  pl.BlockSpec((blk, HEAD_DIM), lambda h, j: (j, 0)),
                  pl.BlockSpec(memory_space=pl.ANY)],
                 [pl.BlockSpec((blk, QKV_WIDTH), lambda h, j: (j, (GATED + F_WIDTH) // QKV_WIDTH + h)),
                  pl.BlockSpec((None, 1, blk), lambda h, j: (h, 0, j)),
                  row],
                 [jax.ShapeDtypeStruct(dproj.shape, dproj.dtype), jax.ShapeDtypeStruct((N_HEADS, 1, t), F32),
                  jax.ShapeDtypeStruct((N_HEADS, 1, t), F32)],
                 [proj, proj, proj, dy, e_row, delta_b, c_pad, dproj],
                 scratch=[pltpu.VMEM((t, 2 * HEAD_DIM), BF16), pltpu.VMEM((HEAD_DIM, t), BF16),
                          pltpu.VMEM((t, HEAD_DIM), BF16), pltpu.VMEM((HEAD_DIM, t), BF16),
                          pltpu.VMEM((1, t), F32), pltpu.VMEM((HEAD_DIM, t), F32)],
                 sem=("parallel", "arbitrary"), nbytes=nbytes, aliases={7: 0}, deps=deps)


def _forget_bwd_t(proj, b_pad, dc, dproj, name):
    t = proj.shape[0]
    tr = min(ROW_BLOCK, t)
    nb = t // tr
    rev = lambda i: nb - 1 - i

    def body(f_ref, b_ref, dc_ref, dproj_in, df_ref, db_ref, run_ref):
        del dproj_in

        @pl.when(pl.program_id(0) == 0)
        def _():
            run_ref[...] = jnp.zeros_like(run_ref)
            db_ref[...] = jnp.zeros_like(db_ref)

        tri = _tri(False)
        b = b_ref[...]
        lane = lax.broadcasted_iota(jnp.int32, (CHUNK, HEAD_DIM), 1)
        df_ref[...] = jnp.zeros_like(df_ref)
        for n in reversed(range(tr // CHUNK)):
            rows = slice(n * CHUNK, (n + 1) * CHUNK)
            dlogf = _dot_exact(tri, dc_ref[rows, :]) + run_ref[...]
            run_ref[...] = dlogf[0:1, :]
            z = f_ref[rows, :] + b
            e = jnp.exp(-jnp.abs(z))
            sig_neg = jnp.where(z >= 0.0, e, 1.0) / (1.0 + e)
            df = jnp.where(lane < N_HEADS, dlogf * sig_neg, 0.0)
            df_ref[rows, :HEAD_DIM] = df.astype(BF16)
            db_ref[...] += jnp.sum(df, axis=0, keepdims=True)

    return _call(body, name, (nb,),
                 [pl.BlockSpec((tr, HEAD_DIM), lambda i: (rev(i), CB_F)), pl.BlockSpec((1, HEAD_DIM), lambda i: (0, 0)),
                  pl.BlockSpec((tr, HEAD_DIM), lambda i: (rev(i), 0)), pl.BlockSpec(memory_space=pl.ANY)],
                 [pl.BlockSpec((tr, F_WIDTH), lambda i: (rev(i), GATED // F_WIDTH)),
                  pl.BlockSpec((1, HEAD_DIM), lambda i: (0, 0))],
                 [jax.ShapeDtypeStruct(dproj.shape, dproj.dtype), jax.ShapeDtypeStruct((1, HEAD_DIM), F32)],
                 [proj, b_pad, dc, dproj], scratch=[pltpu.VMEM((1, HEAD_DIM), F32)],
                 sem=("arbitrary",), nbytes=4 * _nbytes((tr, HEAD_DIM), F32), aliases={3: 0})


_GELU_K = math.sqrt(2.0 / math.pi)
_GELU_C = 0.044715


def _gelu(x):
    t = jnp.tanh(_GELU_K * (x + _GELU_C * (x * x * x)))
    return 0.5 * x * (1.0 + t), t


def _gelu_grad(x, t):
    return 0.5 * (1.0 + t) + 0.5 * x * (1.0 - t * t) * (_GELU_K * (1.0 + 3.0 * _GELU_C * (x * x)))


def _layernorm_stats(a):
    mu = jnp.mean(a, axis=-1, keepdims=True)
    xc = a - mu
    r = lax.rsqrt(jnp.mean(xc * xc, axis=-1, keepdims=True) + EPS)
    return xc * r, r


def _group(g):
    return slice(g * CHUNK, (g + 1) * CHUNK)


def _gated_cols(k):
    return pl.BlockSpec((CHUNK, D_MODEL), lambda i: (i, k))


def _fixed(shape):
    return pl.BlockSpec(shape, lambda i: (0,) * len(shape))


def _mix_fwd(proj, y_b, w_s, b_cols, g_v, name):
    t = proj.shape[0]

    def body(u_ref, vs_ref, ga_ref, gb_ref, yb_ref, w_ref, b_ref, gv_ref, o_ref):
        a_u, _ = _gelu(u_ref[...])
        a_v, _ = _gelu(vs_ref[...])
        xhat, _ = _layernorm_stats(a_v)
        vn = (xhat * gv_ref[...]).astype(BF16)
        sa, sb = _sigmoid(ga_ref[...]), _sigmoid(gb_ref[...])
        yb = yb_ref[...]
        mask = _causal(CHUNK)
        for g in range(N_GROUPS):
            cols = _group(g)
            w = jnp.where(mask, w_ref[g], 0.0).astype(BF16)
            mixed = jnp.dot(w, vn[:, cols], preferred_element_type=F32) + b_ref[:, g:g + 1]
            o_ref[:, cols] = (sa[:, cols] * (a_u[:, cols] * mixed) + sb[:, cols] * yb[:, cols]).astype(BF16)

    return _call(body, name, (t // CHUNK,),
                 [_gated_cols(0), _gated_cols(1), _gated_cols(2), _gated_cols(3), _row_spec(CHUNK, D_MODEL),
                  _fixed((N_GROUPS, CHUNK, CHUNK)), _fixed((CHUNK, N_GROUPS)), _fixed((1, D_MODEL))],
                 [_row_spec(CHUNK, D_MODEL)], [jax.ShapeDtypeStruct((t, D_MODEL), BF16)],
                 [proj, proj, proj, proj, y_b, w_s, b_cols, g_v], sem=("parallel",),
                 nbytes=16 * _nbytes((CHUNK, D_MODEL), F32))[0]


def _mix_bwd(proj, y_b, dmerged, w_s, b_cols, g_v, sel, name, deps=()):
    t = proj.shape[0]

    def body(u_ref, vs_ref, ga_ref, gb_ref, yb_ref, dm_ref, w_ref, b_ref, gv_ref, sel_ref,
             dg_ref, dyb_ref, delta_ref, dw_ref, dbt_ref, dgv_ref):
        @pl.when(pl.program_id(0) == 0)
        def _():
            dw_ref[...] = jnp.zeros_like(dw_ref)
            dbt_ref[...] = jnp.zeros_like(dbt_ref)
            dgv_ref[...] = jnp.zeros_like(dgv_ref)

        u, vs = u_ref[...], vs_ref[...]
        a_u, t_u = _gelu(u)
        a_v, t_v = _gelu(vs)
        xhat, r = _layernorm_stats(a_v)
        gv = gv_ref[...]
        vn = (xhat * gv).astype(BF16)
        sa, sb = _sigmoid(ga_ref[...]), _sigmoid(gb_ref[...])
        yb, dm = yb_ref[...], dm_ref[...].astype(F32)
        dyb = dm * sb
        dyb_ref[...] = dyb
        dyb_yb = dyb * yb
        dg_ref[:, 3 * D_MODEL:] = (dm * yb * (sb * (1.0 - sb))).astype(BF16)
        dya = dm * sa
        mask = _causal(CHUNK)
        dmixed_parts, dvn_parts = [], []
        for g in range(N_GROUPS):
            cols = _group(g)
            delta_ref[:, cols] = jnp.broadcast_to(jnp.sum(dyb_yb[:, cols], axis=1, keepdims=True), (CHUNK, CHUNK))
            w = jnp.where(mask, w_ref[g], 0.0).astype(BF16)
            mixed = jnp.dot(w, vn[:, cols], preferred_element_type=F32) + b_ref[:, g:g + 1]
            y_a = a_u[:, cols] * mixed
            dg_ref[:, 2 * D_MODEL + g * CHUNK:2 * D_MODEL + (g + 1) * CHUNK] = (
                dm[:, cols] * y_a * (sa[:, cols] * (1.0 - sa[:, cols]))).astype(BF16)
            dg_ref[:, cols] = (dya[:, cols] * mixed * _gelu_grad(u[:, cols], t_u[:, cols])).astype(BF16)
            dmixed = dya[:, cols] * a_u[:, cols]
            dmb = dmixed.astype(BF16)
            dw = lax.dot_general(dmb, vn[:, cols], _DIMS["nt"], preferred_element_type=F32)
            dw_ref[g] += jnp.where(mask, dw, 0.0)
            dvn_parts.append(lax.dot_general(w, dmb, _DIMS["tn"], preferred_element_type=F32))
            dmixed_parts.append(dmixed)
        dmixed_all = jnp.concatenate(dmixed_parts, axis=1)
        dvn = jnp.concatenate(dvn_parts, axis=1)
        dbt_ref[...] += _dot_exact_rhs(dmixed_all, sel_ref[...])
        dgv_ref[...] += jnp.sum(dvn * xhat, axis=0, keepdims=True)
        dxh = dvn * gv
        da_v = r * (dxh - jnp.mean(dxh, axis=-1, keepdims=True)
                    - xhat * jnp.mean(dxh * xhat, axis=-1, keepdims=True))
        dg_ref[:, D_MODEL:2 * D_MODEL] = (da_v * _gelu_grad(vs, t_v)).astype(BF16)

    row = lambda width: _row_spec(CHUNK, width)
    return _call(body, name, (t // CHUNK,),
                 [_gated_cols(0), _gated_cols(1), _gated_cols(2), _gated_cols(3), row(D_MODEL), row(D_MODEL),
                  _fixed((N_GROUPS, CHUNK, CHUNK)), _fixed((CHUNK, N_GROUPS)), _fixed((1, D_MODEL)),
                  _fixed((D_MODEL, HEAD_DIM))],
                 [row(GATED), row(D_MODEL), row(D_MODEL),
                  _fixed((N_GROUPS, CHUNK, CHUNK)), _fixed((CHUNK, HEAD_DIM)), _fixed((1, D_MODEL))],
                 [jax.ShapeDtypeStruct((t, IN_PAD), BF16), jax.ShapeDtypeStruct((t, D_MODEL), F32),
                  jax.ShapeDtypeStruct((t, D_MODEL), F32), jax.ShapeDtypeStruct((N_GROUPS, CHUNK, CHUNK), F32),
                  jax.ShapeDtypeStruct((CHUNK, HEAD_DIM), F32), jax.ShapeDtypeStruct((1, D_MODEL), F32)],
                 [proj, proj, proj, proj, y_b, dmerged, w_s, b_cols, g_v, sel], sem=("arbitrary",),
                 nbytes=40 * _nbytes((CHUNK, D_MODEL), F32), deps=deps)


def _adamw_update(w_ref, g_ref, m_ref, v_ref, d_ref, nm_ref, nv_ref):
    gv = g_ref[...]
    nm = ADAM_B1 * m_ref[...] + (1.0 - ADAM_B1) * gv
    nv = ADAM_B2 * v_ref[...] + (1.0 - ADAM_B2) * (gv * gv)
    m_hat = nm / (1.0 - ADAM_B1 ** ADAM_STEP)
    v_hat = nv / (1.0 - ADAM_B2 ** ADAM_STEP)
    d_ref[...] = -ADAM_LR * (m_hat / (jnp.sqrt(v_hat) + ADAM_EPS) + ADAM_WD * w_ref[...])
    nm_ref[...] = nm
    nv_ref[...] = nv


def _adamw(w, g, m, v, name):
    r, c = w.shape
    tr = _tile(r, 256, unit=8)

    def body(w_ref, g_ref, m_ref, v_ref, d_ref, nm_ref, nv_ref):
        _adamw_update(w_ref, g_ref, m_ref, v_ref, d_ref, nm_ref, nv_ref)

    spec = pl.BlockSpec((tr, c), lambda i: (i, 0))
    shape = jax.ShapeDtypeStruct((r, c), F32)
    return _call(body, name, (r // tr,), [spec] * 4, [spec] * 3, [shape] * 3, [w, g, m, v], sem=("parallel",),
                 nbytes=7 * _nbytes((tr, max(c, 128)), F32))


def _adamw_cols(w, g, m, v, name):
    layers, r, c = w.shape
    tc = 256

    def body(w_ref, g_ref, m_ref, v_ref, d_ref, nm_ref, nv_ref):
        _adamw_update(w_ref, g_ref, m_ref, v_ref, d_ref, nm_ref, nv_ref)

    spec = pl.BlockSpec((None, r, tc), lambda l, j: (l, 0, j))
    shape = jax.ShapeDtypeStruct(w.shape, F32)
    return _call(body, name, (layers, c // tc), [spec] * 4, [spec] * 3, [shape] * 3, [w, g, m, v],
                 sem=("parallel", "parallel"), nbytes=7 * _nbytes((r, tc), F32))


def _pad_rows(a, rows):
    return jnp.pad(a, ((0, rows - a.shape[0]), (0, 0)))


def _pack_rest(w_out, w_gate, w_up, w_down):
    return jnp.concatenate([w_out, w_gate.T, w_up.T, w_down], axis=0).astype(BF16)


def _unpack_in(gathered):
    i = jnp.arange(16)[None, :, None]
    b = jnp.arange(N_DEV)[:, None, None]
    head = jnp.where(i < b, jnp.roll(gathered[:, IN_STRIDE:IN_STRIDE + 16], 1, axis=0), gathered[:, :16])
    nat = jnp.concatenate([head, gathered[:, 16:IN_STRIDE]], axis=1).reshape(N_DEV * IN_STRIDE, D_MODEL)
    qkv = nat[2 * D_MODEL:5 * D_MODEL].reshape(3, N_HEADS, HEAD_DIM, D_MODEL)
    qkv = jnp.transpose(qkv, (1, 0, 2, 3)).reshape(3 * D_MODEL, D_MODEL)
    f = _pad_rows(gathered[N_DEV - 1, IN_STRIDE:IN_STRIDE + N_HEADS], F_WIDTH)
    return jnp.concatenate([nat[:2 * D_MODEL], nat[5 * D_MODEL:7 * D_MODEL], f, qkv], axis=0)


def _unpack_rest(gathered):
    take = lambda lo, n: gathered[:, lo:lo + n, :].reshape(N_DEV * n, D_MODEL)
    return take(R_OUT, SH_OUT), jnp.concatenate([take(R_G, SH_FF), take(R_U, SH_FF)], axis=0), take(R_D, SH_FF)


def _pack_grad_in(dw_in_t):
    qkv = dw_in_t[GATED + F_WIDTH:].reshape(N_HEADS, 3, HEAD_DIM, D_MODEL)
    qkv = jnp.transpose(qkv, (1, 0, 2, 3)).reshape(3 * D_MODEL, D_MODEL)
    return jnp.concatenate([dw_in_t[:2 * D_MODEL], qkv, dw_in_t[2 * D_MODEL:GATED],
                            dw_in_t[GATED:GATED + IN_ROWS - N_DEV * IN_STRIDE]], axis=0)


def _pack_grad_rest(dw_out, dw_gu_t, dw_down):
    split = lambda a, n: a.reshape(N_DEV, n, D_MODEL)
    return jnp.concatenate([split(dw_out, SH_OUT), split(dw_gu_t[:D_FF], SH_FF), split(dw_gu_t[D_FF:], SH_FF),
                            split(dw_down, SH_FF)], axis=1)


def kernel(x, mix_pre_g, w_in, b_forget, sgu_norm_g, w_spatial, b_spatial, w_out, mix_post_g, ffn_pre_g, w_gate, w_up, w_down, ffn_post_g, loss_target, m_mix_pre_g, m_w_in, m_b_forget, m_sgu_norm_g, m_w_spatial, m_b_spatial, m_w_out, m_mix_post_g, m_ffn_pre_g, m_w_gate, m_w_up, m_w_down, m_ffn_post_g, v_mix_pre_g, v_w_in, v_b_forget, v_sgu_norm_g, v_w_spatial, v_b_spatial, v_w_out, v_mix_post_g, v_ffn_pre_g, v_w_gate, v_w_up, v_w_down, v_ffn_post_g):
    depth = w_in.shape[0]
    assert depth == 2
    t = x.shape[1]
    x0 = x.reshape(t, D_MODEL)
    target = loss_target.reshape(t, D_MODEL)
    sel = (jnp.arange(D_MODEL)[:, None] // HEAD_DIM == jnp.arange(HEAD_DIM)[None, :]).astype(BF16)
    vec = lambda a, l: a[l][None, :]
    zero_row = jnp.zeros((1, D_MODEL), F32)

    me = 4 * lax.axis_index("x") + 2 * lax.axis_index("y") + lax.axis_index("c")
    p_in = [lax.dynamic_update_slice(jnp.zeros((P_IN, D_MODEL), BF16), w_in[l].T.astype(BF16), (me, 0))
            for l in range(depth)]
    p_rest = [[w_out[l].astype(BF16), jnp.stack([w_gate[l].T, w_up[l].T]).astype(BF16), w_down[l].astype(BF16)]
              for l in range(depth)]
    w_in_t = [None] * depth
    w_rest = [None] * depth
    w_in_t[0] = _unpack_in(_exchange(("gather", 0), p_in[0], "gather_w_in_0")[0])

    saved = []
    xl = x0
    h = _rms_fwd(xl, vec(mix_pre_g, 0), "rms_in_0")
    dy = loss = None
    for l in range(depth):
        b_pad = jnp.pad(b_forget[l], (0, HEAD_DIM - N_HEADS))[None, :]
        b_cols = b_spatial[l].T
        proj = _matmul(h, w_in_t[l], "nt", F32, f"proj_{l}")
        c_pad = _forget_cumsum(proj, b_pad, f"forget_cumsum_{l}")
        c_row = c_pad[:, :N_HEADS].T[:, None, :]
        riders = [(("gather", 0), p_rest[l][0]), (("gather", 1), p_rest[l][1]), (("gather", 0), p_rest[l][2])]
        riders += [(("gather", 0), p_in[l + 1])] if l + 1 < depth else []
        y_b, e_row, *arrived = _attn_fwd_t(proj, c_pad, c_row, f"attn_fwd_{l}", comm=riders)
        w_rest[l] = (arrived[0].reshape(D_MODEL, D_MODEL), arrived[1].reshape(2 * D_FF, D_MODEL),
                     arrived[2].reshape(D_FF, D_MODEL))
        if l + 1 < depth:
            w_in_t[l + 1] = _unpack_in(arrived[3])
        w_o, w_gu_t, w_d = w_rest[l]
        merged = _mix_fwd(proj, y_b, w_spatial[l], b_cols, vec(sgu_norm_g, l), f"mix_fwd_{l}")
        o = _matmul(merged, w_o, "nn", F32, f"out_proj_{l}")
        x1, h2 = _post_norm_fwd(xl, o, vec(mix_post_g, l), vec(ffn_pre_g, l), f"post_mix_{l}")
        gu = _matmul(h2, w_gu_t, "nt", BF16, f"ffn_gu_{l}")
        act = _swiglu_fwd(gu, f"swiglu_fwd_{l}")
        dn = _matmul(act, w_d, "nn", F32, f"ffn_down_{l}")
        saved.append(dict(x0=xl, h=h, proj=proj, b_pad=b_pad, b_cols=b_cols, c_pad=c_pad, y_b=y_b, e_row=e_row,
                          merged=merged, o=o, x1=x1, h2=h2, gu=gu, act=act, dn=dn))
        if l + 1 < depth:
            xl, h = _post_norm_fwd(x1, dn, vec(ffn_post_g, l), vec(mix_pre_g, l + 1), f"post_ffn_{l}")
        else:
            dy, loss = _post_norm_loss(x1, dn, vec(ffn_post_g, l), target, "loss")

    g_small = [None] * depth
    flights = {}
    late_token = None
    dx = dy
    dd, dg_ffn_post = _norm_bwd(dx, f"bwd_ffn_post_{depth - 1}", post=(saved[-1]["dn"], vec(ffn_post_g, depth - 1)))
    for l in reversed(range(depth)):
        s = saved[l]
        w_o, w_gu_t, w_d = w_rest[l]
        dact = _matmul(dd, w_d, "nt", BF16, f"d_act_{l}")
        dw_d = _matmul(s["act"], dd, "tn", BF16, f"dw_down_{l}")
        dgu = _swiglu_bwd(s["gu"], dact, f"swiglu_bwd_{l}")
        dh2 = _matmul(dgu, w_gu_t, "nn", F32, f"d_h2_{l}")
        dw_gu_t = _matmul(dgu, s["h2"], "tn", BF16, f"dw_gu_{l}")
        dx1, dg_ffn_pre, do, dg_mix_post = _norm_bwd(
            dx, f"bwd_post_mix_{l}", pre=(dh2, s["x1"], vec(ffn_pre_g, l)), post=(s["o"], vec(mix_post_g, l)))
        dmerged = _matmul(do, w_o, "nt", F32, f"d_merged_{l}")
        dw_o = _matmul(s["merged"], do, "tn", BF16, f"dw_out_{l}")
        tokens = []
        for key, dw, base, rows in [("out", dw_o, 0, SH_OUT), ("gate", dw_gu_t, 0, SH_FF), ("up", dw_gu_t, D_FF, SH_FF),
                                    ("down", dw_d, 0, SH_FF)]:
            flights[(key, l)], token = _exchange_start(("windows", base, rows, rows), dw, f"start_grad_{key}_{l}")
            tokens.append(token)
        dproj, dyb, delta_b, dw_s, dbt, dg_v = _mix_bwd(
            s["proj"], s["y_b"], dmerged, w_spatial[l], s["b_cols"], vec(sgu_norm_g, l), sel, f"mix_bwd_{l}",
            deps=tokens)
        db_s = dbt[:, :N_GROUPS].T.reshape(1, D_MODEL)
        tokens = []
        if l == 0:
            g_small[0] = jnp.concatenate([zero_row, dg_v, dg_mix_post, dg_ffn_pre, dg_ffn_post, db_s, zero_row,
                                          zero_row, dw_s.reshape(CHUNK, D_MODEL)], axis=0)
            flights["small"], token = _exchange_start(("spread",),jnp.concatenate([g_small[1], g_small[0]], axis=0),
                                                      "start_small_grads")
            tokens = [token]
        dproj, dc_key, dc_query = _attn_bwd_t(s["proj"], dyb, s["e_row"], delta_b, s["c_pad"], dproj,
                                              f"attn_bwd_{l}", deps=tokens)
        dc = jnp.pad((dc_key + dc_query)[:, 0, :].T, ((0, 0), (0, HEAD_DIM - N_HEADS)))
        dproj, db_f = _forget_bwd_t(s["proj"], s["b_pad"], dc, dproj, f"forget_bwd_{l}")
        db_f_row = jnp.pad(db_f, ((0, 0), (0, D_MODEL - HEAD_DIM)))
        dw_in_t = _matmul(dproj, s["h"], "tn", BF16, f"dw_in_{l}")
        flights[("in", l)], token = _exchange_start(("windows", 0, IN_STRIDE, P_IN), _pack_grad_in(dw_in_t),
                                                    f"start_grad_in_{l}")
        dh = _matmul(dproj, w_in_t[l], "nn", F32, f"d_h_{l}", deps=[token])
        if l > 0:
            dx, dg_mix_pre, dd, dg_ffn_post_below = _norm_bwd(
                dx1, f"bwd_pre_mix_{l}", pre=(dh, s["x0"], vec(mix_pre_g, l)),
                post=(saved[l - 1]["dn"], vec(ffn_post_g, l - 1)))
            g_small[l] = jnp.concatenate([dg_mix_pre, dg_v, dg_mix_post, dg_ffn_pre, dg_ffn_post, db_s, db_f_row,
                                          zero_row, dw_s.reshape(CHUNK, D_MODEL)], axis=0)
            dg_ffn_post = dg_ffn_post_below
        else:
            dx, dg_mix_pre = _norm_bwd(dx1, f"bwd_pre_mix_{l}", pre=(dh, s["x0"], vec(mix_pre_g, l)))
            late_rows = jnp.concatenate([dg_mix_pre, db_f_row] + [zero_row] * 6, axis=0)
            flights["late"], late_token = _exchange_start(("spread",),late_rows, "start_late_small_grads")
    grad_x = dx.reshape(x.shape)
    loss = lax.psum(loss.reshape(()), ("x", "y", "c"))

    landed, after = {}, [grad_x, late_token]
    rest = ["out", "gate", "up", "down"]
    for key in [(n, 1) for n in rest] + [("in", 1)] + [(n, 0) for n in rest] + ["small"]:
        label = key if isinstance(key, str) else f"grad_{key[0]}_{key[1]}"
        landed[key] = _exchange_wait(flights[key], after, f"wait_{label}")
        after = [landed[key]]
    small_sum = _sum_slots(landed["small"], S_ROWS, "sum_small_grads")
    sm = small_sum.reshape(depth, S_ROWS, D_MODEL)[::-1]
    g = {
        "sgu_norm_g": sm[:, 1], "mix_post_g": sm[:, 2],
        "ffn_pre_g": sm[:, 3], "ffn_post_g": sm[:, 4], "b_spatial": sm[:, 5].reshape(depth, N_GROUPS, CHUNK),
        "w_spatial": sm[:, 8:].reshape(depth, N_GROUPS, CHUNK, CHUNK),
    }
    for n in rest:
        g["w_" + n] = jnp.stack([_sum_slots(landed[(n, l)], landed[(n, l)].shape[1], f"sum_grad_{n}_{l}")
                                 for l in range(depth)])

    names = ["mix_pre_g", "w_in", "b_forget", "sgu_norm_g", "w_spatial", "b_spatial", "w_out", "mix_post_g",
             "ffn_pre_g", "w_gate", "w_up", "w_down", "ffn_post_g"]
    ws = dict(mix_pre_g=mix_pre_g, w_in=w_in, b_forget=b_forget, sgu_norm_g=sgu_norm_g, w_spatial=w_spatial,
              b_spatial=b_spatial, w_out=w_out, mix_post_g=mix_post_g, ffn_pre_g=ffn_pre_g, w_gate=w_gate, w_up=w_up,
              w_down=w_down, ffn_post_g=ffn_post_g)
    ms = dict(mix_pre_g=m_mix_pre_g, w_in=m_w_in, b_forget=m_b_forget, sgu_norm_g=m_sgu_norm_g, w_spatial=m_w_spatial,
              b_spatial=m_b_spatial, w_out=m_w_out, mix_post_g=m_mix_post_g, ffn_pre_g=m_ffn_pre_g, w_gate=m_w_gate,
              w_up=m_w_up, w_down=m_w_down, ffn_post_g=m_ffn_post_g)
    vs = dict(mix_pre_g=v_mix_pre_g, w_in=v_w_in, b_forget=v_b_forget, sgu_norm_g=v_sgu_norm_g, w_spatial=v_w_spatial,
              b_spatial=v_b_spatial, w_out=v_w_out, mix_post_g=v_mix_post_g, ffn_pre_g=v_ffn_pre_g, w_gate=v_w_gate,
              w_up=v_w_up, w_down=v_w_down, ffn_post_g=v_ffn_post_g)
    updates = {}
    last = ["w_in", "mix_pre_g", "b_forget"]
    transposed = ["w_in", "w_gate", "w_up"]
    for n in [n for n in names if n not in last] + last:
        if n == last[0]:
            sum_in_1 = _sum_slots(landed[("in", 1)], 304, "sum_grad_in_1")
            done = [u[0] for u in updates.values()] + [sum_in_1]
            sum_in_0 = _sum_slots(_exchange_wait(flights[("in", 0)], done, "wait_grad_in_0"), 304, "sum_grad_in_0")
            late = _sum_slots(_exchange_wait(flights["late"], [sum_in_0], "wait_late"), 8, "sum_late_small_grads")
            g["w_in"] = lax.dynamic_slice(jnp.stack([sum_in_0, sum_in_1]), (0, me, 0), (depth, SH_IN, D_MODEL))
            g["mix_pre_g"] = jnp.stack([late[0], sm[1, 0]])
            g["b_forget"] = jnp.stack([late[1, :N_HEADS], sm[1, 6, :N_HEADS]])
        if n in transposed:
            view = lambda a: jnp.swapaxes(a, 1, 2)
            updates[n] = [view(u) for u in _adamw_cols(view(ws[n]), g[n], view(ms[n]), view(vs[n]), f"adamw_{n}")]
            g[n] = view(g[n])
        else:
            flat = (-1, ws[n].shape[-1])
            updates[n] = [u.reshape(ws[n].shape) for u in
                          _adamw(ws[n].reshape(flat), g[n].reshape(flat), ms[n].reshape(flat), vs[n].reshape(flat),
                                 f"adamw_{n}")]
    deltas = [updates[n][0] for n in names]
    new_m = [updates[n][1] for n in names]
    new_v = [updates[n][2] for n in names]
    grads = [g[n].reshape(ws[n].shape) for n in names]
    return (loss, grad_x, *grads, *deltas, *new_m, *new_v)
```

```python
import math

import jax
import jax.numpy as jnp
from jax import lax
from jax.experimental import pallas as pl
from jax.experimental.pallas import tpu as pltpu

F32 = jnp.float32
BF16 = jnp.bfloat16

N_DEV = 8
D_MODEL = 1024
N_HEADS = 8
HEAD_DIM = 128
CHUNK = 128
N_GROUPS = 8
D_FF = 2816
IN_WIDTH = 7 * D_MODEL + N_HEADS
IN_PAD = 7680
EPS = 1e-6
GATED = 4 * D_MODEL
F_WIDTH = 512
QKV_WIDTH = 3 * HEAD_DIM
CB_F = GATED // HEAD_DIM
CB_QKV = (GATED + F_WIDTH) // HEAD_DIM
assert GATED + F_WIDTH + N_HEADS * QKV_WIDTH == IN_PAD and (GATED + F_WIDTH) % QKV_WIDTH == 0

ADAM_LR, ADAM_B1, ADAM_B2, ADAM_EPS, ADAM_WD, ADAM_STEP = 0.001, 0.9, 0.999, 1e-08, 0.01, 10

SH_IN = IN_WIDTH // N_DEV
SH_OUT = D_MODEL // N_DEV
SH_FF = D_FF // N_DEV
P_IN = 912
IN_STRIDE = 7 * HEAD_DIM
IN_ROWS = IN_STRIDE * (N_DEV - 1) + P_IN
S_ROWS = 136

ATTN_BLOCK = 512
ROW_BLOCK = 512
MM_TM, MM_TN, MM_TK = 1536, 1536, 2048
VMEM_CAP = 56 << 20
NEG = -1e30


def _tile(n, cap, unit=128):
    if n <= cap:
        return n
    best = None
    for t in range(unit, cap + 1, unit):
        if n % t == 0:
            best = t
    assert best is not None, (n, cap)
    return best


def _nbytes(shape, dtype):
    return math.prod(shape) * jnp.dtype(dtype).itemsize


_HBM = pl.BlockSpec(memory_space=pltpu.HBM)
MESH = pl.DeviceIdType.MESH

_N_REMOTE = N_DEV - 1


def _exchange_shapes(kind, x):
    if kind[0] == "gather":
        shape = x.shape[:kind[1]] + (N_DEV,) + x.shape[kind[1]:]
    elif kind[0] == "spread":
        shape = (N_DEV,) + x.shape
    else:
        shape = (N_DEV, kind[3]) + x.shape[1:]
    return [jax.ShapeDtypeStruct(shape, x.dtype)]


def _exchange_sems(kind):
    del kind
    return [pltpu.SemaphoreType.DMA((_N_REMOTE,)), pltpu.SemaphoreType.DMA((_N_REMOTE,)), pltpu.SemaphoreType.DMA]


def _plan(kind, x_ref, outs, sems):
    send_sems, recv_sems, local_sem = sems
    x, y, c = lax.axis_index("x"), lax.axis_index("y"), lax.axis_index("c")

    def remote(src, dst, k, to):
        return pltpu.make_async_remote_copy(src_ref=src, dst_ref=dst, send_sem=send_sems.at[k], recv_sem=recv_sems.at[k],
                                            device_id=to, device_id_type=MESH)

    sibling = (x, y, 1 - c)
    if kind[0] == "gather":
        out, = outs
        slot = lambda px, py, pc: out.at[(slice(None),) * kind[1] + (4 * px + 2 * py + pc,)]
        chips = [(1 - x, y), (x, 1 - y), (1 - x, 1 - y)]
        local = [pltpu.make_async_copy(x_ref, slot(x, y, c), local_sem)]
        first = [remote(x_ref, slot(x, y, c), 0, sibling)]
        first += [remote(x_ref, slot(x, y, c), 1 + k, (*chip, c)) for k, chip in enumerate(chips)]
        relays = [(remote(x_ref, slot(*chip, c), 1 + k, (*chip, c)), remote(slot(*chip, c), slot(*chip, c), 4 + k, sibling))
                  for k, chip in enumerate(chips)]
        arrivals = [remote(x_ref, slot(x, y, 1 - c), 0, sibling)]
        arrivals += [remote(x_ref, slot(*chip, 1 - c), 4 + k, sibling) for k, chip in enumerate(chips)]
        return local, first, relays, arrivals
    out, = outs
    mine = 4 * x + 2 * y + c
    if kind[0] == "spread":
        src = lambda d: x_ref
    else:
        _, base, stride, length = kind
        src = lambda d: x_ref.at[pl.ds(pl.multiple_of(base + stride * d, 16), length)]
    first, arrivals = [], []
    for j in range(1, N_DEV):
        peer = (1 - x if j & 4 else x, 1 - y if j & 2 else y, 1 - c if j & 1 else c)
        theirs = 4 * peer[0] + 2 * peer[1] + peer[2]
        first.append(remote(src(theirs), out.at[mine], j - 1, peer))
        arrivals.append(remote(src(theirs), out.at[theirs], j - 1, peer))
    local = [] if local_sem is None else [pltpu.make_async_copy(src(mine), out.at[mine], local_sem)]
    return local, first, [], arrivals


def _start(plan):
    local, first, _, _ = plan
    for cp in local + first:
        cp.start()


def _finish(plan):
    local, first, relays, arrivals = plan
    for arrival, onward in relays:
        arrival.wait_recv()
        onward.start()
    for cp in arrivals:
        cp.wait_recv()
    for cp in first + [onward for _, onward in relays]:
        cp.wait_send()
    for cp in local:
        cp.wait()


def _direct_copies(kind, x_ref, land_ref, send_sems, recv_sems):
    _, first, _, arrivals = _plan(kind, x_ref, [land_ref], (send_sems, recv_sems, None))
    return first, arrivals


_SEM = pl.BlockSpec(memory_space=pltpu.SEMAPHORE)
_DATAFLOW = pltpu.SideEffectType.DATAFLOW_SIDE_EFFECTING


def _exchange_start(kind, x, name):
    me = 4 * lax.axis_index("x") + 2 * lax.axis_index("y") + lax.axis_index("c")
    own = x if kind[0] == "spread" else lax.dynamic_slice_in_dim(x, kind[1] + kind[2] * me, kind[3], 0)
    shape = _exchange_shapes(kind, x)[0].shape
    land = lax.dynamic_update_slice_in_dim(lax.empty(shape, x.dtype), own[None], me, 0)

    def body(x_ref, land_ref, send_sems, recv_sems, x_thru, land_thru, token):
        del x_thru, land_thru
        for cp in _direct_copies(kind, x_ref, land_ref, send_sems, recv_sems)[0]:
            cp.start()
        token[...] = jnp.zeros_like(token)

    *handle, token = pl.pallas_call(
        body, name=name,
        out_shape=(pltpu.SemaphoreType.DMA((_N_REMOTE,)), pltpu.SemaphoreType.DMA((_N_REMOTE,)),
                   pltpu.HBM(x.shape, x.dtype), pltpu.HBM(land.shape, land.dtype), jax.ShapeDtypeStruct((8, 128), F32)),
        in_specs=(_HBM, _HBM), out_specs=(_SEM, _SEM, _HBM, _HBM, pl.BlockSpec(memory_space=pltpu.VMEM)),
        input_output_aliases={0: 2, 1: 3}, compiler_params=pltpu.CompilerParams(has_side_effects=_DATAFLOW),
    )(pltpu.with_memory_space_constraint(x, pltpu.HBM), pltpu.with_memory_space_constraint(land, pltpu.HBM))
    return (kind, *handle), token


def _exchange_wait(handle, after, name):
    kind, send_sems, recv_sems, x_thru, land_thru = handle

    def body(x_ref, land_ref, send_sems, recv_sems, *rest):
        first, arrivals = _direct_copies(kind, x_ref, land_ref, send_sems, recv_sems)
        for cp in first:
            cp.wait_send()
        for cp in arrivals:
            cp.wait_recv()

    return pl.pallas_call(
        body, name=name,
        out_shape=(pltpu.HBM(x_thru.shape, x_thru.dtype), pltpu.HBM(land_thru.shape, land_thru.dtype)),
        in_specs=(_HBM, _HBM, _SEM, _SEM) + (pl.BlockSpec(memory_space=pl.ANY),) * len(after), out_specs=(_HBM, _HBM),
        input_output_aliases={0: 0, 1: 1}, compiler_params=pltpu.CompilerParams(has_side_effects=_DATAFLOW),
    )(x_thru, land_thru, send_sems, recv_sems, *after)[1]


def _exchange(kind, x, name):
    n_out = len(_exchange_shapes(kind, x))

    def body(x_ref, *refs):
        plan = _plan(kind, x_ref, refs[:n_out], refs[n_out:])
        _start(plan)
        _finish(plan)

    return pl.pallas_call(body, name=name, out_shape=_exchange_shapes(kind, x), in_specs=[_HBM],
                          out_specs=[_HBM] * n_out, scratch_shapes=_exchange_sems(kind))(x)


def _call(body, name, grid, in_specs, out_specs, out_shape, args, scratch=(), sem=None, nbytes=0, aliases=None,
          comm=(), deps=()):
    in_specs, out_specs, out_shape, args, scratch = (list(in_specs), list(out_specs), list(out_shape), list(args),
                                                     list(scratch))
    if deps:
        n_real, n_deps, unordered = len(args), len(deps), body

        def body(*refs):
            unordered(*refs[:n_real], *refs[n_real + n_deps:])

        in_specs += [pl.BlockSpec(memory_space=pl.ANY)] * n_deps
        args += list(deps)
    if comm:
        n_in, n_out, n_scr, n_ops = len(args), len(out_shape), len(scratch), len(comm)
        kinds = [kind for kind, _ in comm]
        shapes = [_exchange_shapes(kind, x) for kind, x in comm]
        inner = body

        def body(*refs):
            ins, refs = refs[:n_in], refs[n_in:]
            cins, refs = refs[:n_ops], refs[n_ops:]
            outs, refs = refs[:n_out], refs[n_out:]
            couts = []
            for sh in shapes:
                couts.append(refs[:len(sh)])
                refs = refs[len(sh):]
            own_scratch, sems = refs[:n_scr], refs[n_scr:]
            first = last = None
            for axis, size in enumerate(grid):
                at_start, at_end = pl.program_id(axis) == 0, pl.program_id(axis) == size - 1
                first = at_start if first is None else first & at_start
                last = at_end if last is None else last & at_end
            plans = [_plan(kinds[o], cins[o], couts[o], sems[3 * o:3 * o + 3]) for o in range(n_ops)]

            @pl.when(first)
            def _():
                for p in plans:
                    _start(p)

            inner(*ins, *outs, *own_scratch)

            @pl.when(last)
            def _():
                for p in plans:
                    _finish(p)

        in_specs += [_HBM] * n_ops
        args += [x for _, x in comm]
        for kind, sh in zip(kinds, shapes):
            out_shape += sh
            out_specs += [_HBM] * len(sh)
            scratch += _exchange_sems(kind)
        sem = ("arbitrary",) * len(grid)
    limit = int(min(max(2 * nbytes + (8 << 20), 32 << 20), VMEM_CAP))
    return pl.pallas_call(
        body, name=name, grid=grid, in_specs=in_specs, out_specs=out_specs, out_shape=out_shape,
        scratch_shapes=scratch, input_output_aliases=aliases or {},
        compiler_params=pltpu.CompilerParams(dimension_semantics=sem, vmem_limit_bytes=limit),
    )(*args)


def _sum_slots(x, tr, name):
    n, r, c = x.shape

    def body(x_ref, o_ref):
        acc = x_ref[0].astype(F32)
        for d in range(1, n):
            acc = acc + x_ref[d].astype(F32)
        o_ref[...] = acc

    return _call(body, name, (r // tr,), [pl.BlockSpec((n, tr, c), lambda i: (0, i, 0))],
                 [pl.BlockSpec((tr, c), lambda i: (i, 0))], [jax.ShapeDtypeStruct((r, c), F32)], [x],
                 sem=("parallel",), nbytes=_nbytes((n, tr, c), x.dtype) + _nbytes((tr, c), F32))[0]


_DIMS = {"nn": (((1,), (0,)), ((), ())), "nt": (((1,), (1,)), ((), ())), "tn": (((0,), (0,)), ((), ()))}


def _matmul(a, b, mode, out_dtype, name, comm=(), deps=()):
    if mode == "nn":
        (m, k), (k2, n) = a.shape, b.shape
    elif mode == "nt":
        (m, k), (n, k2) = a.shape, b.shape
    else:
        (k, m), (k2, n) = a.shape, b.shape
    assert k == k2, (a.shape, b.shape, mode)
    tm, tn, tk = _tile(m, MM_TM), _tile(n, MM_TN), _tile(k, MM_TK)
    nk = k // tk
    dims = _DIMS[mode]
    a_spec = {"nn": pl.BlockSpec((tm, tk), lambda i, j, kk: (i, kk)),
              "nt": pl.BlockSpec((tm, tk), lambda i, j, kk: (i, kk)),
              "tn": pl.BlockSpec((tk, tm), lambda i, j, kk: (kk, i))}[mode]
    b_spec = {"nn": pl.BlockSpec((tk, tn), lambda i, j, kk: (kk, j)),
              "nt": pl.BlockSpec((tn, tk), lambda i, j, kk: (j, kk)),
              "tn": pl.BlockSpec((tk, tn), lambda i, j, kk: (kk, j))}[mode]

    def partial_product(a_ref, b_ref):
        return lax.dot_general(a_ref[...].astype(BF16), b_ref[...].astype(BF16), dims, preferred_element_type=F32)

    if nk == 1:
        def body(a_ref, b_ref, o_ref):
            o_ref[...] = partial_product(a_ref, b_ref).astype(o_ref.dtype)
        scratch = []
    else:
        def body(a_ref, b_ref, o_ref, acc_ref):
            kk = pl.program_id(2)

            @pl.when(kk == 0)
            def _():
                acc_ref[...] = jnp.zeros_like(acc_ref)

            acc_ref[...] += partial_product(a_ref, b_ref)

            @pl.when(kk == nk - 1)
            def _():
                o_ref[...] = acc_ref[...].astype(o_ref.dtype)
        scratch = [pltpu.VMEM((tm, tn), F32)]

    nbytes = (_nbytes((tm, tk), a.dtype) + _nbytes((tk, tn), b.dtype) + _nbytes((tm, tn), out_dtype)
              + _nbytes((tm, tn), F32))
    res = _call(body, name, (m // tm, n // tn, nk), [a_spec, b_spec],
                [pl.BlockSpec((tm, tn), lambda i, j, kk: (i, j))], [jax.ShapeDtypeStruct((m, n), out_dtype)], [a, b],
                scratch=scratch, sem=("parallel", "parallel", "arbitrary"), nbytes=nbytes, comm=comm, deps=deps)
    return res if comm else res[0]


def _rms(x):
    return lax.rsqrt(jnp.mean(x * x, axis=-1, keepdims=True) + EPS)


def _rms_bwd(dz, a, g):
    r = _rms(a)
    dzg = dz * g
    da = r * dzg - a * (r * r * r) * jnp.mean(dzg * a, axis=-1, keepdims=True)
    return da, dz * (a * r)


def _row_spec(tr, width):
    return pl.BlockSpec((tr, width), lambda i: (i, 0))


def _vec_spec(width):
    return pl.BlockSpec((1, width), lambda i: (0, 0))


def _rms_fwd(x, g, name):
    t, d = x.shape
    tr = min(ROW_BLOCK, t)

    def body(x_ref, g_ref, h_ref):
        xv = x_ref[...]
        h_ref[...] = ((xv * _rms(xv)) * g_ref[...]).astype(BF16)

    return _call(body, name, (t // tr,), [_row_spec(tr, d), _vec_spec(d)], [_row_spec(tr, d)],
                 [jax.ShapeDtypeStruct((t, d), BF16)], [x, g], sem=("parallel",), nbytes=3 * _nbytes((tr, d), F32))[0]


def _post_norm_fwd(x, o, g_post, g_next, name):
    t, d = x.shape
    tr = min(ROW_BLOCK, t)

    def body(x_ref, o_ref, gp_ref, gn_ref, xn_ref, h_ref):
        ov = o_ref[...]
        xn = x_ref[...] + (ov * _rms(ov)) * gp_ref[...]
        xn_ref[...] = xn
        h_ref[...] = ((xn * _rms(xn)) * gn_ref[...]).astype(BF16)

    return _call(body, name, (t // tr,), [_row_spec(tr, d), _row_spec(tr, d), _vec_spec(d), _vec_spec(d)],
                 [_row_spec(tr, d), _row_spec(tr, d)],
                 [jax.ShapeDtypeStruct((t, d), F32), jax.ShapeDtypeStruct((t, d), BF16)], [x, o, g_post, g_next],
                 sem=("parallel",), nbytes=5 * _nbytes((tr, d), F32))


def _post_norm_loss(x, o, g_post, target, name):
    t, d = x.shape
    tr = min(ROW_BLOCK, t)

    def body(x_ref, o_ref, gp_ref, t_ref, dy_ref, loss_ref):
        ov = o_ref[...]
        err = x_ref[...] + (ov * _rms(ov)) * gp_ref[...] - t_ref[...]
        dy_ref[...] = err / d
        part = 0.5 * jnp.sum(jnp.mean(err * err, axis=-1, keepdims=True), axis=0, keepdims=True)

        @pl.when(pl.program_id(0) == 0)
        def _():
            loss_ref[...] = jnp.zeros_like(loss_ref)

        loss_ref[...] += part

    return _call(body, name, (t // tr,), [_row_spec(tr, d), _row_spec(tr, d), _vec_spec(d), _row_spec(tr, d)],
                 [_row_spec(tr, d), pl.BlockSpec((1, 1), lambda i: (0, 0))],
                 [jax.ShapeDtypeStruct((t, d), F32), jax.ShapeDtypeStruct((1, 1), F32)], [x, o, g_post, target],
                 sem=("arbitrary",), nbytes=5 * _nbytes((tr, d), F32))


def _norm_bwd(dskip, name, pre=None, post=None):
    t, d = dskip.shape
    tr = min(ROW_BLOCK, t)
    n_in = 1 + (3 if pre else 0) + (2 if post else 0)

    def body(*refs):
        ins, outs = list(refs[:n_in]), list(refs[n_in:])
        first = pl.program_id(0) == 0
        dx = ins.pop(0)[...]
        if pre:
            dh_ref, xin_ref, gpre_ref = ins.pop(0), ins.pop(0), ins.pop(0)
            dxin, dg_rows = _rms_bwd(dh_ref[...].astype(F32), xin_ref[...], gpre_ref[...])
            dx = dx + dxin
            dx_ref, dgpre_ref = outs.pop(0), outs.pop(0)
            dx_ref[...] = dx

            @pl.when(first)
            def _():
                dgpre_ref[...] = jnp.zeros_like(dgpre_ref)

            dgpre_ref[...] += jnp.sum(dg_rows, axis=0, keepdims=True)
        if post:
            a_ref, gpost_ref = ins.pop(0), ins.pop(0)
            da, dg_rows = _rms_bwd(dx, a_ref[...], gpost_ref[...])
            da_ref, dgpost_ref = outs.pop(0), outs.pop(0)
            da_ref[...] = da.astype(BF16)

            @pl.when(first)
            def _():
                dgpost_ref[...] = jnp.zeros_like(dgpost_ref)

            dgpost_ref[...] += jnp.sum(dg_rows, axis=0, keepdims=True)

    args, in_specs, out_shape, out_specs = [dskip], [_row_spec(tr, d)], [], []
    if pre:
        args += list(pre)
        in_specs += [_row_spec(tr, d), _row_spec(tr, d), _vec_spec(d)]
        out_shape += [jax.ShapeDtypeStruct((t, d), F32), jax.ShapeDtypeStruct((1, d), F32)]
        out_specs += [_row_spec(tr, d), _vec_spec(d)]
    if post:
        args += list(post)
        in_specs += [_row_spec(tr, d), _vec_spec(d)]
        out_shape += [jax.ShapeDtypeStruct((t, d), BF16), jax.ShapeDtypeStruct((1, d), F32)]
        out_specs += [_row_spec(tr, d), _vec_spec(d)]
    return _call(body, name, (t // tr,), in_specs, out_specs, out_shape, args, sem=("arbitrary",),
                 nbytes=7 * _nbytes((tr, d), F32))


def _sigmoid(x):
    return 1.0 / (1.0 + jnp.exp(-x))


def _swiglu_fwd(gu, name):
    t = gu.shape[0]
    tr = min(ROW_BLOCK, t)

    def body(gu_ref, a_ref):
        gate, up = gu_ref[:, :D_FF].astype(F32), gu_ref[:, D_FF:].astype(F32)
        a_ref[...] = ((gate * _sigmoid(gate)) * up).astype(BF16)

    return _call(body, name, (t // tr,), [_row_spec(tr, 2 * D_FF)], [_row_spec(tr, D_FF)],
                 [jax.ShapeDtypeStruct((t, D_FF), BF16)], [gu], sem=("parallel",),
                 nbytes=2 * _nbytes((tr, 2 * D_FF), F32))[0]


def _swiglu_bwd(gu, dact, name):
    t = gu.shape[0]
    tr = min(ROW_BLOCK, t)

    def body(gu_ref, da_ref, o_ref):
        gate, up = gu_ref[:, :D_FF].astype(F32), gu_ref[:, D_FF:].astype(F32)
        da = da_ref[...].astype(F32)
        s = _sigmoid(gate)
        o_ref[:, :D_FF] = (da * up * (s * (1.0 + gate * (1.0 - s)))).astype(BF16)
        o_ref[:, D_FF:] = (da * (gate * s)).astype(BF16)

    return _call(body, name, (t // tr,), [_row_spec(tr, 2 * D_FF), _row_spec(tr, D_FF)], [_row_spec(tr, 2 * D_FF)],
                 [jax.ShapeDtypeStruct((t, 2 * D_FF), BF16)], [gu, dact], sem=("parallel",),
                 nbytes=3 * _nbytes((tr, 2 * D_FF), F32))[0]


def _split3(x):
    hi = x.astype(BF16)
    r1 = x - hi.astype(F32)
    mid = r1.astype(BF16)
    lo = (r1 - mid.astype(F32)).astype(BF16)
    return hi, mid, lo


def _dot_exact(mat01, x):
    hi, mid, lo = _split3(x)
    out = jnp.dot(mat01, hi, preferred_element_type=F32)
    out = out + jnp.dot(mat01, mid, preferred_element_type=F32)
    return out + jnp.dot(mat01, lo, preferred_element_type=F32)


def _dot_exact_rhs(x, mat01):
    hi, mid, lo = _split3(x)
    out = jnp.dot(hi, mat01, preferred_element_type=F32)
    out = out + jnp.dot(mid, mat01, preferred_element_type=F32)
    return out + jnp.dot(lo, mat01, preferred_element_type=F32)


def _tri(lower):
    r = lax.broadcasted_iota(jnp.int32, (CHUNK, CHUNK), 0)
    c = lax.broadcasted_iota(jnp.int32, (CHUNK, CHUNK), 1)
    return jnp.where(r >= c if lower else r <= c, 1.0, 0.0).astype(BF16)


def _log_sigmoid(z):
    return jnp.minimum(z, 0.0) - jnp.log(1.0 + jnp.exp(-jnp.abs(z)))


def _forget_cumsum(proj, b_pad, name):
    t = proj.shape[0]
    nb = t // CHUNK

    def body(f_ref, b_ref, c_ref):
        tri = _tri(True)
        b = b_ref[...]

        def blk(i, carry):
            rows = pl.ds(pl.multiple_of(i * CHUNK, CHUNK), CHUNK)
            cs = _dot_exact(tri, _log_sigmoid(f_ref[rows, :] + b)) + carry
            c_ref[rows, :] = cs
            return cs[CHUNK - 1:CHUNK, :]

        lax.fori_loop(0, nb, blk, jnp.zeros((1, HEAD_DIM), F32))

    return _call(body, name, (1,),
                 [pl.BlockSpec((t, HEAD_DIM), lambda i: (0, CB_F)), pl.BlockSpec((1, HEAD_DIM), lambda i: (0, 0))],
                 [pl.BlockSpec((t, HEAD_DIM), lambda i: (0, 0))], [jax.ShapeDtypeStruct((t, HEAD_DIM), F32)],
                 [proj, b_pad], sem=("arbitrary",), nbytes=2 * _nbytes((t, HEAD_DIM), F32))[0]


def _causal(block):
    r = lax.broadcasted_iota(jnp.int32, (block, block), 0)
    c = lax.broadcasted_iota(jnp.int32, (block, block), 1)
    return c <= r


def _lanes(x, width):
    return jnp.concatenate([x] * (width // HEAD_DIM), axis=1)


def _chunks(t, blk):
    return [slice(r * blk, (r + 1) * blk) for r in range(t // blk)]


def _lane_of_head(x, h):
    lane = lax.broadcasted_iota(jnp.int32, x.shape, 1)
    return jnp.broadcast_to(jnp.sum(jnp.where(lane == h, x, 0.0), axis=1, keepdims=True), x.shape)


def _causal_t(block):
    r = lax.broadcasted_iota(jnp.int32, (block, block), 0)
    c = lax.broadcasted_iota(jnp.int32, (block, block), 1)
    return r <= c


def _attn_fwd_t(proj, c_pad, c_row, name, comm=()):
    t = proj.shape[0]
    blk = min(ATTN_BLOCK, t)
    nq = t // blk
    scale = HEAD_DIM ** -0.5

    def body(q_ref, k_ref, v_ref, cp_ref, cr_ref, y_ref, e_ref, kb_s, vt_s, cb_s):
        h, i = pl.program_id(0), pl.program_id(1)

        @pl.when(i == 0)
        def _():
            for rows in _chunks(t, blk):
                kb_s[rows, :] = k_ref[rows, :].astype(BF16)
                vt_s[:, rows] = v_ref[rows, :].T.astype(BF16)
                cb_s[rows, :] = _lane_of_head(cp_ref[rows, :], h)

        q = (q_ref[...] * scale).astype(BF16)
        ci = cr_ref[:, pl.ds(pl.multiple_of(i * blk, blk), blk)]

        def step(j, carry, diagonal):
            m, l, acc = carry
            rows = pl.ds(pl.multiple_of(j * blk, blk), blk)
            s = lax.dot_general(kb_s[rows, :], q, _DIMS["nt"], preferred_element_type=F32)
            s = s + ci - _lanes(cb_s[rows, :], blk)
            if diagonal:
                s = jnp.where(_causal_t(blk), s, NEG)
            m_new = jnp.maximum(m, jnp.max(s, axis=0, keepdims=True))
            alpha = jnp.exp(m - m_new)
            p = jnp.exp(s - m_new)
            l = alpha * l + jnp.sum(p, axis=0, keepdims=True)
            acc = alpha * acc + jnp.dot(vt_s[:, rows], p.astype(BF16), preferred_element_type=F32)
            return m_new, l, acc

        init = (jnp.full((1, blk), NEG, F32), jnp.zeros((1, blk), F32), jnp.zeros((HEAD_DIM, blk), F32))
        carry = lax.fori_loop(0, i, lambda j, c: step(j, c, False), init)
        m, l, acc = step(i, carry, True)
        y_ref[...] = (acc / l).T
        e_ref[...] = ci - (m + jnp.log(l))

    nbytes = 3 * _nbytes((t, HEAD_DIM), F32) + 4 * _nbytes((blk, HEAD_DIM), F32) + 4 * _nbytes((blk, blk), F32)
    return _call(body, name, (N_HEADS, nq),
                 [pl.BlockSpec((blk, HEAD_DIM), lambda h, i: (i, CB_QKV + 3 * h)),
                  pl.BlockSpec((t, HEAD_DIM), lambda h, i: (0, CB_QKV + 3 * h + 1)),
                  pl.BlockSpec((t, HEAD_DIM), lambda h, i: (0, CB_QKV + 3 * h + 2)),
                  pl.BlockSpec((t, HEAD_DIM), lambda h, i: (0, 0)),
                  pl.BlockSpec((None, 1, t), lambda h, i: (h, 0, 0))],
                 [pl.BlockSpec((blk, HEAD_DIM), lambda h, i: (i, h)),
                  pl.BlockSpec((None, 1, blk), lambda h, i: (h, 0, i))],
                 [jax.ShapeDtypeStruct((t, D_MODEL), F32), jax.ShapeDtypeStruct((N_HEADS, 1, t), F32)],
                 [proj, proj, proj, c_pad, c_row],
                 scratch=[pltpu.VMEM((t, HEAD_DIM), BF16), pltpu.VMEM((HEAD_DIM, t), BF16),
                          pltpu.VMEM((t, HEAD_DIM), F32)],
                 sem=("parallel", "arbitrary"), nbytes=nbytes, comm=comm)


def _attn_bwd_t(proj, dy, e_row, delta_b, c_pad, dproj, name, deps=()):
    t = proj.shape[0]
    blk = min(ATTN_BLOCK, t)
    nb = t // blk
    scale = HEAD_DIM ** -0.5

    def body(q_ref, k_ref, v_ref, dy_ref, e_ref, dl_ref, cp_ref, dproj_in, dqkv_ref, dck_ref, dcq_ref,
             qb_s, qt_s, dyb_s, dyt_s, dl_s, dqt_acc):
        del dproj_in
        h, j = pl.program_id(0), pl.program_id(1)

        @pl.when(j == 0)
        def _():
            for rows in _chunks(t, blk):
                qs = q_ref[rows, :] * scale
                qb_s[rows, :] = qs.astype(BF16)
                qt_s[:, rows] = qs.T.astype(BF16)
                dyr = dy_ref[rows, :]
                dyb_s[rows, :] = dyr.astype(BF16)
                dyt_s[:, rows] = dyr.T.astype(BF16)
                dl_s[:, rows] = dl_ref[rows, :].T[0:1, :]
            dqt_acc[...] = jnp.zeros_like(dqt_acc)
            dcq_ref[...] = jnp.zeros_like(dcq_ref)

        kf = k_ref[...]
        kb, kt = kf.astype(BF16), kf.T.astype(BF16)
        vb = v_ref[...].astype(BF16)
        cj = _lanes(_lane_of_head(cp_ref[...], h), blk)

        def step(i, carry, diagonal):
            dkt, dvt, dkey = carry
            cols = pl.ds(pl.multiple_of(i * blk, blk), blk)
            s = lax.dot_general(kb, qb_s[cols, :], _DIMS["nt"], preferred_element_type=F32) + e_ref[:, cols] - cj
            if diagonal:
                s = jnp.where(_causal_t(blk), s, NEG)
            p = jnp.exp(s)
            dp = lax.dot_general(vb, dyb_s[cols, :], _DIMS["nt"], preferred_element_type=F32)
            ds = p * (dp - dl_s[:, cols])
            pb, dsb = p.astype(BF16), ds.astype(BF16)
            dvt = dvt + lax.dot_general(dyt_s[:, cols], pb, _DIMS["nt"], preferred_element_type=F32)
            dkt = dkt + lax.dot_general(qt_s[:, cols], dsb, _DIMS["nt"], preferred_element_type=F32)
            dqt_acc[:, cols] += jnp.dot(kt, dsb, preferred_element_type=F32) * scale
            dcq_ref[:, cols] += jnp.sum(ds, axis=0, keepdims=True)
            for k in range(blk // HEAD_DIM):
                dkey = dkey + ds[:, k * HEAD_DIM:(k + 1) * HEAD_DIM]
            return dkt, dvt, dkey

        zeros = jnp.zeros((HEAD_DIM, blk), F32)
        carry = step(j, (zeros, zeros, jnp.zeros((blk, HEAD_DIM), F32)), True)
        dkt, dvt, dkey = lax.fori_loop(j + 1, nb, lambda i, c: step(i, c, False), carry)
        mine = pl.ds(pl.multiple_of(j * blk, blk), blk)
        dqkv_ref[:, :HEAD_DIM] = dqt_acc[:, mine].T.astype(BF16)
        dqkv_ref[:, HEAD_DIM:2 * HEAD_DIM] = dkt.T.astype(BF16)
        dqkv_ref[:, 2 * HEAD_DIM:] = dvt.T.astype(BF16)
        dck_ref[...] = -jnp.sum(dkey.T, axis=0, keepdims=True)

    full = lambda cb: pl.BlockSpec((t, HEAD_DIM), lambda h, j: (0, cb(h)))
    head = lambda h: h
    row = pl.BlockSpec((None, 1, t), lambda h, j: (h, 0, 0))
    nbytes = (8 * _nbytes((t, HEAD_DIM), F32) + 4 * _nbytes((blk, HEAD_DIM), F32) + 6 * _nbytes((blk, blk), F32))
    return _call(body, name, (N_HEADS, nb),
                 [full(lambda h: CB_QKV + 3 * h),
                  pl.BlockSpec((blk, HEAD_DIM), lambda h, j: (j, CB_QKV + 3 * h + 1)),
                  pl.BlockSpec((blk, HEAD_DIM), lambda h, j: (j, CB_QKV + 3 * h + 2)),
                  full(head), row, full(head),
                  pl.BlockSpec((blk, HEAD_DIM), lambda h, j: (j, 0)),
                  pl.BlockSpec(memory_space=pl.ANY)],
                 [pl.BlockSpec((blk, QKV_WIDTH), lambda h, j: (j, (GATED + F_WIDTH) // QKV_WIDTH + h)),
                  pl.BlockSpec((None, 1, blk), lambda h, j: (h, 0, j)),
                  row],
                 [jax.ShapeDtypeStruct(dproj.shape, dproj.dtype), jax.ShapeDtypeStruct((N_HEADS, 1, t), F32),
                  jax.ShapeDtypeStruct((N_HEADS, 1, t), F32)],
                 [proj, proj, proj, dy, e_row, delta_b, c_pad, dproj],
                 scratch=[pltpu.VMEM((t, HEAD_DIM), BF16), pltpu.VMEM((HEAD_DIM, t), BF16),
                          pltpu.VMEM((t, HEAD_DIM), BF16), pltpu.VMEM((HEAD_DIM, t), BF16),
                          pltpu.VMEM((1, t), F32), pltpu.VMEM((HEAD_DIM, t), F32)],
                 sem=("parallel", "arbitrary"), nbytes=nbytes, aliases={7: 0}, deps=deps)


def _forget_bwd_t(proj, b_pad, dc, dproj, name):
    t = proj.shape[0]
    tr = min(ROW_BLOCK, t)
    nb = t // tr
    rev = lambda i: nb - 1 - i

    def body(f_ref, b_ref, dc_ref, dproj_in, df_ref, db_ref, run_ref):
        del dproj_in

        @pl.when(pl.program_id(0) == 0)
        def _():
            run_ref[...] = jnp.zeros_like(run_ref)
            db_ref[...] = jnp.zeros_like(db_ref)

        tri = _tri(False)
        b = b_ref[...]
        lane = lax.broadcasted_iota(jnp.int32, (CHUNK, HEAD_DIM), 1)
        df_ref[...] = jnp.zeros_like(df_ref)
        for n in reversed(range(tr // CHUNK)):
            rows = slice(n * CHUNK, (n + 1) * CHUNK)
            dlogf = _dot_exact(tri, dc_ref[rows, :]) + run_ref[...]
            run_ref[...] = dlogf[0:1, :]
            z = f_ref[rows, :] + b
            e = jnp.exp(-jnp.abs(z))
            sig_neg = jnp.where(z >= 0.0, e, 1.0) / (1.0 + e)
            df = jnp.where(lane < N_HEADS, dlogf * sig_neg, 0.0)
            df_ref[rows, :HEAD_DIM] = df.astype(BF16)
            db_ref[...] += jnp.sum(df, axis=0, keepdims=True)

    return _call(body, name, (nb,),
                 [pl.BlockSpec((tr, HEAD_DIM), lambda i: (rev(i), CB_F)), pl.BlockSpec((1, HEAD_DIM), lambda i: (0, 0)),
                  pl.BlockSpec((tr, HEAD_DIM), lambda i: (rev(i), 0)), pl.BlockSpec(memory_space=pl.ANY)],
                 [pl.BlockSpec((tr, F_WIDTH), lambda i: (rev(i), GATED // F_WIDTH)),
                  pl.BlockSpec((1, HEAD_DIM), lambda i: (0, 0))],
                 [jax.ShapeDtypeStruct(dproj.shape, dproj.dtype), jax.ShapeDtypeStruct((1, HEAD_DIM), F32)],
                 [proj, b_pad, dc, dproj], scratch=[pltpu.VMEM((1, HEAD_DIM), F32)],
                 sem=("arbitrary",), nbytes=4 * _nbytes((tr, HEAD_DIM), F32), aliases={3: 0})


_GELU_K = math.sqrt(2.0 / math.pi)
_GELU_C = 0.044715


def _gelu(x):
    t = jnp.tanh(_GELU_K * (x + _GELU_C * (x * x * x)))
    return 0.5 * x * (1.0 + t), t


def _gelu_grad(x, t):
    return 0.5 * (1.0 + t) + 0.5 * x * (1.0 - t * t) * (_GELU_K * (1.0 + 3.0 * _GELU_C * (x * x)))


def _layernorm_stats(a):
    mu = jnp.mean(a, axis=-1, keepdims=True)
    xc = a - mu
    r = lax.rsqrt(jnp.mean(xc * xc, axis=-1, keepdims=True) + EPS)
    return xc * r, r


def _group(g):
    return slice(g * CHUNK, (g + 1) * CHUNK)


def _gated_cols(k):
    return pl.BlockSpec((CHUNK, D_MODEL), lambda i: (i, k))


def _fixed(shape):
    return pl.BlockSpec(shape, lambda i: (0,) * len(shape))


def _mix_fwd(proj, y_b, w_s, b_cols, g_v, name):
    t = proj.shape[0]

    def body(u_ref, vs_ref, ga_ref, gb_ref, yb_ref, w_ref, b_ref, gv_ref, o_ref):
        a_u, _ = _gelu(u_ref[...])
        a_v, _ = _gelu(vs_ref[...])
        xhat, _ = _layernorm_stats(a_v)
        vn = (xhat * gv_ref[...]).astype(BF16)
        sa, sb = _sigmoid(ga_ref[...]), _sigmoid(gb_ref[...])
        yb = yb_ref[...]
        mask = _causal(CHUNK)
        for g in range(N_GROUPS):
            cols = _group(g)
            w = jnp.where(mask, w_ref[g], 0.0).astype(BF16)
            mixed = jnp.dot(w, vn[:, cols], preferred_element_type=F32) + b_ref[:, g:g + 1]
            o_ref[:, cols] = (sa[:, cols] * (a_u[:, cols] * mixed) + sb[:, cols] * yb[:, cols]).astype(BF16)

    return _call(body, name, (t // CHUNK,),
                 [_gated_cols(0), _gated_cols(1), _gated_cols(2), _gated_cols(3), _row_spec(CHUNK, D_MODEL),
                  _fixed((N_GROUPS, CHUNK, CHUNK)), _fixed((CHUNK, N_GROUPS)), _fixed((1, D_MODEL))],
                 [_row_spec(CHUNK, D_MODEL)], [jax.ShapeDtypeStruct((t, D_MODEL), BF16)],
                 [proj, proj, proj, proj, y_b, w_s, b_cols, g_v], sem=("parallel",),
                 nbytes=16 * _nbytes((CHUNK, D_MODEL), F32))[0]


def _mix_bwd(proj, y_b, dmerged, w_s, b_cols, g_v, sel, name, deps=()):
    t = proj.shape[0]

    def body(u_ref, vs_ref, ga_ref, gb_ref, yb_ref, dm_ref, w_ref, b_ref, gv_ref, sel_ref,
             dg_ref, dyb_ref, delta_ref, dw_ref, dbt_ref, dgv_ref):
        @pl.when(pl.program_id(0) == 0)
        def _():
            dw_ref[...] = jnp.zeros_like(dw_ref)
            dbt_ref[...] = jnp.zeros_like(dbt_ref)
            dgv_ref[...] = jnp.zeros_like(dgv_ref)

        u, vs = u_ref[...], vs_ref[...]
        a_u, t_u = _gelu(u)
        a_v, t_v = _gelu(vs)
        xhat, r = _layernorm_stats(a_v)
        gv = gv_ref[...]
        vn = (xhat * gv).astype(BF16)
        sa, sb = _sigmoid(ga_ref[...]), _sigmoid(gb_ref[...])
        yb, dm = yb_ref[...], dm_ref[...].astype(F32)
        dyb = dm * sb
        dyb_ref[...] = dyb
        dyb_yb = dyb * yb
        dg_ref[:, 3 * D_MODEL:] = (dm * yb * (sb * (1.0 - sb))).astype(BF16)
        dya = dm * sa
        mask = _causal(CHUNK)
        dmixed_parts, dvn_parts = [], []
        for g in range(N_GROUPS):
            cols = _group(g)
            delta_ref[:, cols] = jnp.broadcast_to(jnp.sum(dyb_yb[:, cols], axis=1, keepdims=True), (CHUNK, CHUNK))
            w = jnp.where(mask, w_ref[g], 0.0).astype(BF16)
            mixed = jnp.dot(w, vn[:, cols], preferred_element_type=F32) + b_ref[:, g:g + 1]
            y_a = a_u[:, cols] * mixed
            dg_ref[:, 2 * D_MODEL + g * CHUNK:2 * D_MODEL + (g + 1) * CHUNK] = (
                dm[:, cols] * y_a * (sa[:, cols] * (1.0 - sa[:, cols]))).astype(BF16)
            dg_ref[:, cols] = (dya[:, cols] * mixed * _gelu_grad(u[:, cols], t_u[:, cols])).astype(BF16)
            dmixed = dya[:, cols] * a_u[:, cols]
            dmb = dmixed.astype(BF16)
            dw = lax.dot_general(dmb, vn[:, cols], _DIMS["nt"], preferred_element_type=F32)
            dw_ref[g] += jnp.where(mask, dw, 0.0)
            dvn_parts.append(lax.dot_general(w, dmb, _DIMS["tn"], preferred_element_type=F32))
            dmixed_parts.append(dmixed)
        dmixed_all = jnp.concatenate(dmixed_parts, axis=1)
        dvn = jnp.concatenate(dvn_parts, axis=1)
        dbt_ref[...] += _dot_exact_rhs(dmixed_all, sel_ref[...])
        dgv_ref[...] += jnp.sum(dvn * xhat, axis=0, keepdims=True)
        dxh = dvn * gv
        da_v = r * (dxh - jnp.mean(dxh, axis=-1, keepdims=True)
                    - xhat * jnp.mean(dxh * xhat, axis=-1, keepdims=True))
        dg_ref[:, D_MODEL:2 * D_MODEL] = (da_v * _gelu_grad(vs, t_v)).astype(BF16)

    row = lambda width: _row_spec(CHUNK, width)
    return _call(body, name, (t // CHUNK,),
                 [_gated_cols(0), _gated_cols(1), _gated_cols(2), _gated_cols(3), row(D_MODEL), row(D_MODEL),
                  _fixed((N_GROUPS, CHUNK, CHUNK)), _fixed((CHUNK, N_GROUPS)), _fixed((1, D_MODEL)),
                  _fixed((D_MODEL, HEAD_DIM))],
                 [row(GATED), row(D_MODEL), row(D_MODEL),
                  _fixed((N_GROUPS, CHUNK, CHUNK)), _fixed((CHUNK, HEAD_DIM)), _fixed((1, D_MODEL))],
                 [jax.ShapeDtypeStruct((t, IN_PAD), BF16), jax.ShapeDtypeStruct((t, D_MODEL), F32),
                  jax.ShapeDtypeStruct((t, D_MODEL), F32), jax.ShapeDtypeStruct((N_GROUPS, CHUNK, CHUNK), F32),
                  jax.ShapeDtypeStruct((CHUNK, HEAD_DIM), F32), jax.ShapeDtypeStruct((1, D_MODEL), F32)],
                 [proj, proj, proj, proj, y_b, dmerged, w_s, b_cols, g_v, sel], sem=("arbitrary",),
                 nbytes=40 * _nbytes((CHUNK, D_MODEL), F32), deps=deps)


def _adamw_update(w_ref, g_ref, m_ref, v_ref, d_ref, nm_ref, nv_ref):
    gv = g_ref[...]
    nm = ADAM_B1 * m_ref[...] + (1.0 - ADAM_B1) * gv
    nv = ADAM_B2 * v_ref[...] + (1.0 - ADAM_B2) * (gv * gv)
    m_hat = nm / (1.0 - ADAM_B1 ** ADAM_STEP)
    v_hat = nv / (1.0 - ADAM_B2 ** ADAM_STEP)
    d_ref[...] = -ADAM_LR * (m_hat / (jnp.sqrt(v_hat) + ADAM_EPS) + ADAM_WD * w_ref[...])
    nm_ref[...] = nm
    nv_ref[...] = nv


def _adamw(w, g, m, v, name):
    r, c = w.shape
    tr = _tile(r, 256, unit=8)

    def body(w_ref, g_ref, m_ref, v_ref, d_ref, nm_ref, nv_ref):
        _adamw_update(w_ref, g_ref, m_ref, v_ref, d_ref, nm_ref, nv_ref)

    spec = pl.BlockSpec((tr, c), lambda i: (i, 0))
    shape = jax.ShapeDtypeStruct((r, c), F32)
    return _call(body, name, (r // tr,), [spec] * 4, [spec] * 3, [shape] * 3, [w, g, m, v], sem=("parallel",),
                 nbytes=7 * _nbytes((tr, max(c, 128)), F32))


def _adamw_cols(w, g, m, v, name):
    layers, r, c = w.shape
    tc = 256

    def body(w_ref, g_ref, m_ref, v_ref, d_ref, nm_ref, nv_ref):
        _adamw_update(w_ref, g_ref, m_ref, v_ref, d_ref, nm_ref, nv_ref)

    spec = pl.BlockSpec((None, r, tc), lambda l, j: (l, 0, j))
    shape = jax.ShapeDtypeStruct(w.shape, F32)
    return _call(body, name, (layers, c // tc), [spec] * 4, [spec] * 3, [shape] * 3, [w, g, m, v],
                 sem=("parallel", "parallel"), nbytes=7 * _nbytes((r, tc), F32))


def _pad_rows(a, rows):
    return jnp.pad(a, ((0, rows - a.shape[0]), (0, 0)))


def _unpack_in(gathered):
    i = jnp.arange(16)[None, :, None]
    b = jnp.arange(N_DEV)[:, None, None]
    head = jnp.where(i < b, jnp.roll(gathered[:, IN_STRIDE:IN_STRIDE + 16], 1, axis=0), gathered[:, :16])
    nat = jnp.concatenate([head, gathered[:, 16:IN_STRIDE]], axis=1).reshape(N_DEV * IN_STRIDE, D_MODEL)
    qkv = nat[2 * D_MODEL:5 * D_MODEL].reshape(3, N_HEADS, HEAD_DIM, D_MODEL)
    qkv = jnp.transpose(qkv, (1, 0, 2, 3)).reshape(3 * D_MODEL, D_MODEL)
    f = _pad_rows(gathered[N_DEV - 1, IN_STRIDE:IN_STRIDE + N_HEADS], F_WIDTH)
    return jnp.concatenate([nat[:2 * D_MODEL], nat[5 * D_MODEL:7 * D_MODEL], f, qkv], axis=0)


def _pack_grad_in(dw_in_t):
    qkv = dw_in_t[GATED + F_WIDTH:].reshape(N_HEADS, 3, HEAD_DIM, D_MODEL)
    qkv = jnp.transpose(qkv, (1, 0, 2, 3)).reshape(3 * D_MODEL, D_MODEL)
    return jnp.concatenate([dw_in_t[:2 * D_MODEL], qkv, dw_in_t[2 * D_MODEL:GATED],
                            dw_in_t[GATED:GATED + IN_ROWS - N_DEV * IN_STRIDE]], axis=0)


def kernel(x, mix_pre_g, w_in, b_forget, sgu_norm_g, w_spatial, b_spatial, w_out, mix_post_g, ffn_pre_g, w_gate, w_up, w_down, ffn_post_g, loss_target, m_mix_pre_g, m_w_in, m_b_forget, m_sgu_norm_g, m_w_spatial, m_b_spatial, m_w_out, m_mix_post_g, m_ffn_pre_g, m_w_gate, m_w_up, m_w_down, m_ffn_post_g, v_mix_pre_g, v_w_in, v_b_forget, v_sgu_norm_g, v_w_spatial, v_b_spatial, v_w_out, v_mix_post_g, v_ffn_pre_g, v_w_gate, v_w_up, v_w_down, v_ffn_post_g):
    depth = w_in.shape[0]
    assert depth == 2
    t = x.shape[1]
    x0 = x.reshape(t, D_MODEL)
    target = loss_target.reshape(t, D_MODEL)
    sel = (jnp.arange(D_MODEL)[:, None] // HEAD_DIM == jnp.arange(HEAD_DIM)[None, :]).astype(BF16)
    vec = lambda a, l: a[l][None, :]
    zero_row = jnp.zeros((1, D_MODEL), F32)

    me = 4 * lax.axis_index("x") + 2 * lax.axis_index("y") + lax.axis_index("c")
    p_in = [lax.dynamic_update_slice(jnp.zeros((P_IN, D_MODEL), BF16), w_in[l].T.astype(BF16), (me, 0))
            for l in range(depth)]
    p_rest = [[w_out[l].astype(BF16), jnp.stack([w_gate[l].T, w_up[l].T]).astype(BF16), w_down[l].astype(BF16)]
              for l in range(depth)]
    w_in_t = [None] * depth
    w_rest = [None] * depth
    w_in_t[0] = _unpack_in(_exchange(("gather", 0), p_in[0], "gather_w_in_0")[0])

    saved = []
    xl = x0
    h = _rms_fwd(xl, vec(mix_pre_g, 0), "rms_in_0")
    dy = loss = None
    for l in range(depth):
        b_pad = jnp.pad(b_forget[l], (0, HEAD_DIM - N_HEADS))[None, :]
        b_cols = b_spatial[l].T
        proj = _matmul(h, w_in_t[l], "nt", F32, f"proj_{l}")
        c_pad = _forget_cumsum(proj, b_pad, f"forget_cumsum_{l}")
        c_row = c_pad[:, :N_HEADS].T[:, None, :]
        riders = [(("gather", 0), p_rest[l][0]), (("gather", 1), p_rest[l][1]), (("gather", 0), p_rest[l][2])]
        riders += [(("gather", 0), p_in[l + 1])] if l + 1 < depth else []
        y_b, e_row, *arrived = _attn_fwd_t(proj, c_pad, c_row, f"attn_fwd_{l}", comm=riders)
        w_rest[l] = (arrived[0].reshape(D_MODEL, D_MODEL), arrived[1].reshape(2 * D_FF, D_MODEL),
                     arrived[2].reshape(D_FF, D_MODEL))
        if l + 1 < depth:
            w_in_t[l + 1] = _unpack_in(arrived[3])
        w_o, w_gu_t, w_d = w_rest[l]
        merged = _mix_fwd(proj, y_b, w_spatial[l], b_cols, vec(sgu_norm_g, l), f"mix_fwd_{l}")
        o = _matmul(merged, w_o, "nn", F32, f"out_proj_{l}")
        x1, h2 = _post_norm_fwd(xl, o, vec(mix_post_g, l), vec(ffn_pre_g, l), f"post_mix_{l}")
        gu = _matmul(h2, w_gu_t, "nt", BF16, f"ffn_gu_{l}")
        act = _swiglu_fwd(gu, f"swiglu_fwd_{l}")
        dn = _matmul(act, w_d, "nn", F32, f"ffn_down_{l}")
        saved.append(dict(x0=xl, h=h, proj=proj, b_pad=b_pad, b_cols=b_cols, c_pad=c_pad, y_b=y_b, e_row=e_row,
                          merged=merged, o=o, x1=x1, h2=h2, gu=gu, act=act, dn=dn))
        if l + 1 < depth:
            xl, h = _post_norm_fwd(x1, dn, vec(ffn_post_g, l), vec(mix_pre_g, l + 1), f"post_ffn_{l}")
        else:
            dy, loss = _post_norm_loss(x1, dn, vec(ffn_post_g, l), target, "loss")

    g_small = [None] * depth
    flights = {}
    late_token = None
    dx = dy
    dd, dg_ffn_post = _norm_bwd(dx, f"bwd_ffn_post_{depth - 1}", post=(saved[-1]["dn"], vec(ffn_post_g, depth - 1)))
    for l in reversed(range(depth)):
        s = saved[l]
        w_o, w_gu_t, w_d = w_rest[l]
        dact = _matmul(dd, w_d, "nt", BF16, f"d_act_{l}")
        dw_d = _matmul(s["act"], dd, "tn", BF16, f"dw_down_{l}")
        dgu = _swiglu_bwd(s["gu"], dact, f"swiglu_bwd_{l}")
        dh2 = _matmul(dgu, w_gu_t, "nn", F32, f"d_h2_{l}")
        dw_gu_t = _matmul(dgu, s["h2"], "tn", BF16, f"dw_gu_{l}")
        dx1, dg_ffn_pre, do, dg_mix_post = _norm_bwd(
            dx, f"bwd_post_mix_{l}", pre=(dh2, s["x1"], vec(ffn_pre_g, l)), post=(s["o"], vec(mix_post_g, l)))
        dmerged = _matmul(do, w_o, "nt", F32, f"d_merged_{l}")
        dw_o = _matmul(s["merged"], do, "tn", BF16, f"dw_out_{l}")
        tokens = []
        for key, dw, base, rows in [("out", dw_o, 0, SH_OUT), ("gate", dw_gu_t, 0, SH_FF), ("up", dw_gu_t, D_FF, SH_FF),
                                    ("down", dw_d, 0, SH_FF)]:
            flights[(key, l)], token = _exchange_start(("windows", base, rows, rows), dw, f"start_grad_{key}_{l}")
            tokens.append(token)
        dproj, dyb, delta_b, dw_s, dbt, dg_v = _mix_bwd(
            s["proj"], s["y_b"], dmerged, w_spatial[l], s["b_cols"], vec(sgu_norm_g, l), sel, f"mix_bwd_{l}",
            deps=tokens)
        db_s = dbt[:, :N_GROUPS].T.reshape(1, D_MODEL)
        tokens = []
        if l == 0:
            g_small[0] = jnp.concatenate([zero_row, dg_v, dg_mix_post, dg_ffn_pre, dg_ffn_post, db_s, zero_row,
                                          zero_row, dw_s.reshape(CHUNK, D_MODEL)], axis=0)
            flights["small"], token = _exchange_start(("spread",),jnp.concatenate([g_small[1], g_small[0]], axis=0),
                                                      "start_small_grads")
            tokens = [token]
        dproj, dc_key, dc_query = _attn_bwd_t(s["proj"], dyb, s["e_row"], delta_b, s["c_pad"], dproj,
                                              f"attn_bwd_{l}", deps=tokens)
        dc = jnp.pad((dc_key + dc_query)[:, 0, :].T, ((0, 0), (0, HEAD_DIM - N_HEADS)))
        dproj, db_f = _forget_bwd_t(s["proj"], s["b_pad"], dc, dproj, f"forget_bwd_{l}")
        db_f_row = jnp.pad(db_f, ((0, 0), (0, D_MODEL - HEAD_DIM)))
        dw_in_t = _matmul(dproj, s["h"], "tn", BF16, f"dw_in_{l}")
        flights[("in", l)], token = _exchange_start(("windows", 0, IN_STRIDE, P_IN), _pack_grad_in(dw_in_t),
                                                    f"start_grad_in_{l}")
        dh = _matmul(dproj, w_in_t[l], "nn", F32, f"d_h_{l}", deps=[token])
        if l > 0:
            dx, dg_mix_pre, dd, dg_ffn_post_below = _norm_bwd(
                dx1, f"bwd_pre_mix_{l}", pre=(dh, s["x0"], vec(mix_pre_g, l)),
                post=(saved[l - 1]["dn"], vec(ffn_post_g, l - 1)))
            loss_row = jnp.pad(loss, ((0, 0), (0, D_MODEL - 1)))
            g_small[l] = jnp.concatenate([dg_mix_pre, dg_v, dg_mix_post, dg_ffn_pre, dg_ffn_post, db_s, db_f_row,
                                          loss_row, dw_s.reshape(CHUNK, D_MODEL)], axis=0)
            dg_ffn_post = dg_ffn_post_below
        else:
            dx, dg_mix_pre = _norm_bwd(dx1, f"bwd_pre_mix_{l}", pre=(dh, s["x0"], vec(mix_pre_g, l)))
            late_rows = jnp.concatenate([dg_mix_pre, db_f_row] + [zero_row] * 6, axis=0)
            flights["late"], late_token = _exchange_start(("spread",),late_rows, "start_late_small_grads")
    grad_x = dx.reshape(x.shape)

    landed, after = {}, [grad_x, late_token]
    rest = ["out", "gate", "up", "down"]
    for key in [(n, 1) for n in rest] + [("in", 1)] + [(n, 0) for n in rest] + ["small"]:
        label = key if isinstance(key, str) else f"grad_{key[0]}_{key[1]}"
        landed[key] = _exchange_wait(flights[key], after, f"wait_{label}")
        after = [landed[key]]
    small_sum = _sum_slots(landed["small"], S_ROWS, "sum_small_grads")
    sm = small_sum.reshape(depth, S_ROWS, D_MODEL)[::-1]
    loss = sm[depth - 1, 7, 0]
    g = {
        "sgu_norm_g": sm[:, 1], "mix_post_g": sm[:, 2],
        "ffn_pre_g": sm[:, 3], "ffn_post_g": sm[:, 4], "b_spatial": sm[:, 5].reshape(depth, N_GROUPS, CHUNK),
        "w_spatial": sm[:, 8:].reshape(depth, N_GROUPS, CHUNK, CHUNK),
    }
    for n in rest:
        g["w_" + n] = jnp.stack([_sum_slots(landed[(n, l)], landed[(n, l)].shape[1], f"sum_grad_{n}_{l}")
                                 for l in range(depth)])

    names = ["mix_pre_g", "w_in", "b_forget", "sgu_norm_g", "w_spatial", "b_spatial", "w_out", "mix_post_g",
             "ffn_pre_g", "w_gate", "w_up", "w_down", "ffn_post_g"]
    ws = dict(mix_pre_g=mix_pre_g, w_in=w_in, b_forget=b_forget, sgu_norm_g=sgu_norm_g, w_spatial=w_spatial,
              b_spatial=b_spatial, w_out=w_out, mix_post_g=mix_post_g, ffn_pre_g=ffn_pre_g, w_gate=w_gate, w_up=w_up,
              w_down=w_down, ffn_post_g=ffn_post_g)
    ms = dict(mix_pre_g=m_mix_pre_g, w_in=m_w_in, b_forget=m_b_forget, sgu_norm_g=m_sgu_norm_g, w_spatial=m_w_spatial,
              b_spatial=m_b_spatial, w_out=m_w_out, mix_post_g=m_mix_post_g, ffn_pre_g=m_ffn_pre_g, w_gate=m_w_gate,
              w_up=m_w_up, w_down=m_w_down, ffn_post_g=m_ffn_post_g)
    vs = dict(mix_pre_g=v_mix_pre_g, w_in=v_w_in, b_forget=v_b_forget, sgu_norm_g=v_sgu_norm_g, w_spatial=v_w_spatial,
              b_spatial=v_b_spatial, w_out=v_w_out, mix_post_g=v_mix_post_g, ffn_pre_g=v_ffn_pre_g, w_gate=v_w_gate,
              w_up=v_w_up, w_down=v_w_down, ffn_post_g=v_ffn_post_g)
    updates = {}
    last = ["w_in", "mix_pre_g", "b_forget"]
    transposed = ["w_in", "w_gate", "w_up"]
    for n in [n for n in names if n not in last] + last:
        if n == last[0]:
            sum_in_1 = _sum_slots(landed[("in", 1)], 304, "sum_grad_in_1")
            done = [u[0] for u in updates.values()] + [sum_in_1]
            sum_in_0 = _sum_slots(_exchange_wait(flights[("in", 0)], done, "wait_grad_in_0"), 304, "sum_grad_in_0")
            late = _sum_slots(_exchange_wait(flights["late"], [sum_in_0], "wait_late"), 8, "sum_late_small_grads")
            g["w_in"] = lax.dynamic_slice(jnp.stack([sum_in_0, sum_in_1]), (0, me, 0), (depth, SH_IN, D_MODEL))
            g["mix_pre_g"] = jnp.stack([late[0], sm[1, 0]])
            g["b_forget"] = jnp.stack([late[1, :N_HEADS], sm[1, 6, :N_HEADS]])
        if n in transposed:
            view = lambda a: jnp.swapaxes(a, 1, 2)
            updates[n] = [view(u) for u in _adamw_cols(view(ws[n]), g[n], view(ms[n]), view(vs[n]), f"adamw_{n}")]
            g[n] = view(g[n])
        else:
            flat = (-1, ws[n].shape[-1])
            updates[n] = [u.reshape(ws[n].shape) for u in
                          _adamw(ws[n].reshape(flat), g[n].reshape(flat), ms[n].reshape(flat), vs[n].reshape(flat),
                                 f"adamw_{n}")]
    deltas = [updates[n][0] for n in names]
    new_m = [updates[n][1] for n in names]
    new_v = [updates[n][2] for n in names]
    grads = [g[n].reshape(ws[n].shape) for n in names]
    return (loss, grad_x, *grads, *deltas, *new_m, *new_v)
```

```python
import math

import jax
import jax.numpy as jnp
from jax import lax
from jax.experimental import pallas as pl
from jax.experimental.pallas import tpu as pltpu

F32 = jnp.float32
BF16 = jnp.bfloat16

N_DEV = 8
D_MODEL = 1024
N_HEADS = 8
HEAD_DIM = 128
CHUNK = 128
N_GROUPS = 8
D_FF = 2816
IN_WIDTH = 7 * D_MODEL + N_HEADS
IN_PAD = 7680
EPS = 1e-6
GATED = 4 * D_MODEL
F_WIDTH = 512
QKV_WIDTH = 3 * HEAD_DIM
CB_F = GATED // HEAD_DIM
CB_QKV = (GATED + F_WIDTH) // HEAD_DIM
assert GATED + F_WIDTH + N_HEADS * QKV_WIDTH == IN_PAD and (GATED + F_WIDTH) % QKV_WIDTH == 0

ADAM_LR, ADAM_B1, ADAM_B2, ADAM_EPS, ADAM_WD, ADAM_STEP = 0.001, 0.9, 0.999, 1e-08, 0.01, 10

SH_IN = IN_WIDTH // N_DEV
SH_OUT = D_MODEL // N_DEV
SH_FF = D_FF // N_DEV
P_IN = 912
IN_STRIDE = 7 * HEAD_DIM
IN_ROWS = IN_STRIDE * (N_DEV - 1) + P_IN
S_ROWS = 136

ATTN_BLOCK = 1024
ROW_BLOCK = 512
MM_TM, MM_TN, MM_TK = 1536, 1536, 2048
VMEM_CAP = 56 << 20
NEG = -1e30


def _tile(n, cap, unit=128):
    if n <= cap:
        return n
    best = None
    for t in range(unit, cap + 1, unit):
        if n % t == 0:
            best = t
    assert best is not None, (n, cap)
    return best


def _nbytes(shape, dtype):
    return math.prod(shape) * jnp.dtype(dtype).itemsize


_HBM = pl.BlockSpec(memory_space=pltpu.HBM)
MESH = pl.DeviceIdType.MESH

_N_REMOTE = N_DEV - 1


def _exchange_shapes(kind, x):
    if kind[0] == "gather":
        shape = x.shape[:kind[1]] + (N_DEV,) + x.shape[kind[1]:]
    elif kind[0] == "spread":
        shape = (N_DEV,) + x.shape
    else:
        shape = (N_DEV, kind[3]) + x.shape[1:]
    return [jax.ShapeDtypeStruct(shape, x.dtype)]


def _exchange_sems(kind):
    del kind
    return [pltpu.SemaphoreType.DMA((_N_REMOTE,)), pltpu.SemaphoreType.DMA((_N_REMOTE,)), pltpu.SemaphoreType.DMA]


def _plan(kind, x_ref, outs, sems):
    send_sems, recv_sems, local_sem = sems
    x, y, c = lax.axis_index("x"), lax.axis_index("y"), lax.axis_index("c")

    def remote(src, dst, k, to):
        return pltpu.make_async_remote_copy(src_ref=src, dst_ref=dst, send_sem=send_sems.at[k], recv_sem=recv_sems.at[k],
                                            device_id=to, device_id_type=MESH)

    sibling = (x, y, 1 - c)
    if kind[0] == "gather":
        out, = outs
        slot = lambda px, py, pc: out.at[(slice(None),) * kind[1] + (4 * px + 2 * py + pc,)]
        chips = [(1 - x, y), (x, 1 - y), (1 - x, 1 - y)]
        local = [pltpu.make_async_copy(x_ref, slot(x, y, c), local_sem)]
        first = [remote(x_ref, slot(x, y, c), 0, sibling)]
        first += [remote(x_ref, slot(x, y, c), 1 + k, (*chip, c)) for k, chip in enumerate(chips)]
        relays = [(remote(x_ref, slot(*chip, c), 1 + k, (*chip, c)), remote(slot(*chip, c), slot(*chip, c), 4 + k, sibling))
                  for k, chip in enumerate(chips)]
        arrivals = [remote(x_ref, slot(x, y, 1 - c), 0, sibling)]
        arrivals += [remote(x_ref, slot(*chip, 1 - c), 4 + k, sibling) for k, chip in enumerate(chips)]
        return local, first, relays, arrivals
    out, = outs
    mine = 4 * x + 2 * y + c
    if kind[0] == "spread":
        src = lambda d: x_ref
    else:
        _, base, stride, length = kind
        src = lambda d: x_ref.at[pl.ds(pl.multiple_of(base + stride * d, 16), length)]
    first, arrivals = [], []
    for j in range(1, N_DEV):
        peer = (1 - x if j & 4 else x, 1 - y if j & 2 else y, 1 - c if j & 1 else c)
        theirs = 4 * peer[0] + 2 * peer[1] + peer[2]
        first.append(remote(src(theirs), out.at[mine], j - 1, peer))
        arrivals.append(remote(src(theirs), out.at[theirs], j - 1, peer))
    local = [] if local_sem is None else [pltpu.make_async_copy(src(mine), out.at[mine], local_sem)]
    return local, first, [], arrivals


def _start(plan):
    local, first, _, _ = plan
    for cp in local + first:
        cp.start()


def _finish(plan):
    local, first, relays, arrivals = plan
    for arrival, onward in relays:
        arrival.wait_recv()
        onward.start()
    for cp in arrivals:
        cp.wait_recv()
    for cp in first + [onward for _, onward in relays]:
        cp.wait_send()
    for cp in local:
        cp.wait()


def _direct_copies(kind, x_ref, land_ref, send_sems, recv_sems):
    _, first, _, arrivals = _plan(kind, x_ref, [land_ref], (send_sems, recv_sems, None))
    return first, arrivals


_SEM = pl.BlockSpec(memory_space=pltpu.SEMAPHORE)
_DATAFLOW = pltpu.SideEffectType.DATAFLOW_SIDE_EFFECTING


def _exchange_start(kind, x, name):
    me = 4 * lax.axis_index("x") + 2 * lax.axis_index("y") + lax.axis_index("c")
    own = x if kind[0] == "spread" else lax.dynamic_slice_in_dim(x, kind[1] + kind[2] * me, kind[3], 0)
    shape = _exchange_shapes(kind, x)[0].shape
    land = lax.dynamic_update_slice_in_dim(lax.empty(shape, x.dtype), own[None], me, 0)

    def body(x_ref, land_ref, send_sems, recv_sems, x_thru, land_thru, token):
        del x_thru, land_thru
        for cp in _direct_copies(kind, x_ref, land_ref, send_sems, recv_sems)[0]:
            cp.start()
        token[...] = jnp.zeros_like(token)

    *handle, token = pl.pallas_call(
        body, name=name,
        out_shape=(pltpu.SemaphoreType.DMA((_N_REMOTE,)), pltpu.SemaphoreType.DMA((_N_REMOTE,)),
                   pltpu.HBM(x.shape, x.dtype), pltpu.HBM(land.shape, land.dtype), jax.ShapeDtypeStruct((8, 128), F32)),
        in_specs=(_HBM, _HBM), out_specs=(_SEM, _SEM, _HBM, _HBM, pl.BlockSpec(memory_space=pltpu.VMEM)),
        input_output_aliases={0: 2, 1: 3}, compiler_params=pltpu.CompilerParams(has_side_effects=_DATAFLOW),
    )(pltpu.with_memory_space_constraint(x, pltpu.HBM), pltpu.with_memory_space_constraint(land, pltpu.HBM))
    return (kind, *handle), token


def _exchange_wait(handle, after, name):
    kind, send_sems, recv_sems, x_thru, land_thru = handle

    def body(x_ref, land_ref, send_sems, recv_sems, *rest):
        first, arrivals = _direct_copies(kind, x_ref, land_ref, send_sems, recv_sems)
        for cp in first:
            cp.wait_send()
        for cp in arrivals:
            cp.wait_recv()

    return pl.pallas_call(
        body, name=name,
        out_shape=(pltpu.HBM(x_thru.shape, x_thru.dtype), pltpu.HBM(land_thru.shape, land_thru.dtype)),
        in_specs=(_HBM, _HBM, _SEM, _SEM) + (pl.BlockSpec(memory_space=pl.ANY),) * len(after), out_specs=(_HBM, _HBM),
        input_output_aliases={0: 0, 1: 1}, compiler_params=pltpu.CompilerParams(has_side_effects=_DATAFLOW),
    )(x_thru, land_thru, send_sems, recv_sems, *after)[1]


def _exchange(kind, x, name):
    n_out = len(_exchange_shapes(kind, x))

    def body(x_ref, *refs):
        plan = _plan(kind, x_ref, refs[:n_out], refs[n_out:])
        _start(plan)
        _finish(plan)

    return pl.pallas_call(body, name=name, out_shape=_exchange_shapes(kind, x), in_specs=[_HBM],
                          out_specs=[_HBM] * n_out, scratch_shapes=_exchange_sems(kind))(x)


def _call(body, name, grid, in_specs, out_specs, out_shape, args, scratch=(), sem=None, nbytes=0, aliases=None,
          comm=(), deps=()):
    in_specs, out_specs, out_shape, args, scratch = (list(in_specs), list(out_specs), list(out_shape), list(args),
                                                     list(scratch))
    if deps:
        n_real, n_deps, unordered = len(args), len(deps), body

        def body(*refs):
            unordered(*refs[:n_real], *refs[n_real + n_deps:])

        in_specs += [pl.BlockSpec(memory_space=pl.ANY)] * n_deps
        args += list(deps)
    if comm:
        n_in, n_out, n_scr, n_ops = len(args), len(out_shape), len(scratch), len(comm)
        kinds = [kind for kind, _ in comm]
        shapes = [_exchange_shapes(kind, x) for kind, x in comm]
        inner = body

        def body(*refs):
            ins, refs = refs[:n_in], refs[n_in:]
            cins, refs = refs[:n_ops], refs[n_ops:]
            outs, refs = refs[:n_out], refs[n_out:]
            couts = []
            for sh in shapes:
                couts.append(refs[:len(sh)])
                refs = refs[len(sh):]
            own_scratch, sems = refs[:n_scr], refs[n_scr:]
            first = last = None
            for axis, size in enumerate(grid):
                at_start, at_end = pl.program_id(axis) == 0, pl.program_id(axis) == size - 1
                first = at_start if first is None else first & at_start
                last = at_end if last is None else last & at_end
            plans = [_plan(kinds[o], cins[o], couts[o], sems[3 * o:3 * o + 3]) for o in range(n_ops)]

            @pl.when(first)
            def _():
                for p in plans:
                    _start(p)

            inner(*ins, *outs, *own_scratch)

            @pl.when(last)
            def _():
                for p in plans:
                    _finish(p)

        in_specs += [_HBM] * n_ops
        args += [x for _, x in comm]
        for kind, sh in zip(kinds, shapes):
            out_shape += sh
            out_specs += [_HBM] * len(sh)
            scratch += _exchange_sems(kind)
        sem = ("arbitrary",) * len(grid)
    limit = int(min(max(2 * nbytes + (8 << 20), 32 << 20), VMEM_CAP))
    return pl.pallas_call(
        body, name=name, grid=grid, in_specs=in_specs, out_specs=out_specs, out_shape=out_shape,
        scratch_shapes=scratch, input_output_aliases=aliases or {},
        compiler_params=pltpu.CompilerParams(dimension_semantics=sem, vmem_limit_bytes=limit),
    )(*args)


def _sum_slots(x, tr, name):
    n, r, c = x.shape

    def body(x_ref, o_ref):
        acc = x_ref[0].astype(F32)
        for d in range(1, n):
            acc = acc + x_ref[d].astype(F32)
        o_ref[...] = acc

    return _call(body, name, (r // tr,), [pl.BlockSpec((n, tr, c), lambda i: (0, i, 0))],
                 [pl.BlockSpec((tr, c), lambda i: (i, 0))], [jax.ShapeDtypeStruct((r, c), F32)], [x],
                 sem=("parallel",), nbytes=_nbytes((n, tr, c), x.dtype) + _nbytes((tr, c), F32))[0]


_DIMS = {"nn": (((1,), (0,)), ((), ())), "nt": (((1,), (1,)), ((), ())), "tn": (((0,), (0,)), ((), ()))}


def _matmul(a, b, mode, out_dtype, name, comm=(), deps=()):
    if mode == "nn":
        (m, k), (k2, n) = a.shape, b.shape
    elif mode == "nt":
        (m, k), (n, k2) = a.shape, b.shape
    else:
        (k, m), (k2, n) = a.shape, b.shape
    assert k == k2, (a.shape, b.shape, mode)
    tm, tn, tk = _tile(m, MM_TM), _tile(n, MM_TN), _tile(k, MM_TK)
    nk = k // tk
    dims = _DIMS[mode]
    a_spec = {"nn": pl.BlockSpec((tm, tk), lambda i, j, kk: (i, kk)),
              "nt": pl.BlockSpec((tm, tk), lambda i, j, kk: (i, kk)),
              "tn": pl.BlockSpec((tk, tm), lambda i, j, kk: (kk, i))}[mode]
    b_spec = {"nn": pl.BlockSpec((tk, tn), lambda i, j, kk: (kk, j)),
              "nt": pl.BlockSpec((tn, tk), lambda i, j, kk: (j, kk)),
              "tn": pl.BlockSpec((tk, tn), lambda i, j, kk: (kk, j))}[mode]

    def partial_product(a_ref, b_ref):
        return lax.dot_general(a_ref[...].astype(BF16), b_ref[...].astype(BF16), dims, preferred_element_type=F32)

    if nk == 1:
        def body(a_ref, b_ref, o_ref):
            o_ref[...] = partial_product(a_ref, b_ref).astype(o_ref.dtype)
        scratch = []
    else:
        def body(a_ref, b_ref, o_ref, acc_ref):
            kk = pl.program_id(2)

            @pl.when(kk == 0)
            def _():
                acc_ref[...] = jnp.zeros_like(acc_ref)

            acc_ref[...] += partial_product(a_ref, b_ref)

            @pl.when(kk == nk - 1)
            def _():
                o_ref[...] = acc_ref[...].astype(o_ref.dtype)
        scratch = [pltpu.VMEM((tm, tn), F32)]

    nbytes = (_nbytes((tm, tk), a.dtype) + _nbytes((tk, tn), b.dtype) + _nbytes((tm, tn), out_dtype)
              + _nbytes((tm, tn), F32))
    res = _call(body, name, (m // tm, n // tn, nk), [a_spec, b_spec],
                [pl.BlockSpec((tm, tn), lambda i, j, kk: (i, j))], [jax.ShapeDtypeStruct((m, n), out_dtype)], [a, b],
                scratch=scratch, sem=("parallel", "parallel", "arbitrary"), nbytes=nbytes, comm=comm, deps=deps)
    return res if comm else res[0]


def _rms(x):
    return lax.rsqrt(jnp.mean(x * x, axis=-1, keepdims=True) + EPS)


def _rms_bwd(dz, a, g):
    r = _rms(a)
    dzg = dz * g
    da = r * dzg - a * (r * r * r) * jnp.mean(dzg * a, axis=-1, keepdims=True)
    return da, dz * (a * r)


def _row_spec(tr, width):
    return pl.BlockSpec((tr, width), lambda i: (i, 0))


def _vec_spec(width):
    return pl.BlockSpec((1, width), lambda i: (0, 0))


def _rms_fwd(x, g, name):
    t, d = x.shape
    tr = min(ROW_BLOCK, t)

    def body(x_ref, g_ref, h_ref):
        xv = x_ref[...]
        h_ref[...] = ((xv * _rms(xv)) * g_ref[...]).astype(BF16)

    return _call(body, name, (t // tr,), [_row_spec(tr, d), _vec_spec(d)], [_row_spec(tr, d)],
                 [jax.ShapeDtypeStruct((t, d), BF16)], [x, g], sem=("parallel",), nbytes=3 * _nbytes((tr, d), F32))[0]


def _post_norm_fwd(x, o, g_post, g_next, name):
    t, d = x.shape
    tr = min(ROW_BLOCK, t)

    def body(x_ref, o_ref, gp_ref, gn_ref, xn_ref, h_ref):
        ov = o_ref[...]
        xn = x_ref[...] + (ov * _rms(ov)) * gp_ref[...]
        xn_ref[...] = xn
        h_ref[...] = ((xn * _rms(xn)) * gn_ref[...]).astype(BF16)

    return _call(body, name, (t // tr,), [_row_spec(tr, d), _row_spec(tr, d), _vec_spec(d), _vec_spec(d)],
                 [_row_spec(tr, d), _row_spec(tr, d)],
                 [jax.ShapeDtypeStruct((t, d), F32), jax.ShapeDtypeStruct((t, d), BF16)], [x, o, g_post, g_next],
                 sem=("parallel",), nbytes=5 * _nbytes((tr, d), F32))


def _post_norm_loss(x, o, g_post, target, name):
    t, d = x.shape
    tr = min(ROW_BLOCK, t)

    def body(x_ref, o_ref, gp_ref, t_ref, dy_ref, loss_ref):
        ov = o_ref[...]
        err = x_ref[...] + (ov * _rms(ov)) * gp_ref[...] - t_ref[...]
        dy_ref[...] = err / d
        part = 0.5 * jnp.sum(jnp.mean(err * err, axis=-1, keepdims=True), axis=0, keepdims=True)

        @pl.when(pl.program_id(0) == 0)
        def _():
            loss_ref[...] = jnp.zeros_like(loss_ref)

        loss_ref[...] += part

    return _call(body, name, (t // tr,), [_row_spec(tr, d), _row_spec(tr, d), _vec_spec(d), _row_spec(tr, d)],
                 [_row_spec(tr, d), pl.BlockSpec((1, 1), lambda i: (0, 0))],
                 [jax.ShapeDtypeStruct((t, d), F32), jax.ShapeDtypeStruct((1, 1), F32)], [x, o, g_post, target],
                 sem=("arbitrary",), nbytes=5 * _nbytes((tr, d), F32))


def _norm_bwd(dskip, name, pre=None, post=None):
    t, d = dskip.shape
    tr = min(ROW_BLOCK, t)
    n_in = 1 + (3 if pre else 0) + (2 if post else 0)

    def body(*refs):
        ins, outs = list(refs[:n_in]), list(refs[n_in:])
        first = pl.program_id(0) == 0
        dx = ins.pop(0)[...]
        if pre:
            dh_ref, xin_ref, gpre_ref = ins.pop(0), ins.pop(0), ins.pop(0)
            dxin, dg_rows = _rms_bwd(dh_ref[...].astype(F32), xin_ref[...], gpre_ref[...])
            dx = dx + dxin
            dx_ref, dgpre_ref = outs.pop(0), outs.pop(0)
            dx_ref[...] = dx

            @pl.when(first)
            def _():
                dgpre_ref[...] = jnp.zeros_like(dgpre_ref)

            dgpre_ref[...] += jnp.sum(dg_rows, axis=0, keepdims=True)
        if post:
            a_ref, gpost_ref = ins.pop(0), ins.pop(0)
            da, dg_rows = _rms_bwd(dx, a_ref[...], gpost_ref[...])
            da_ref, dgpost_ref = outs.pop(0), outs.pop(0)
            da_ref[...] = da.astype(BF16)

            @pl.when(first)
            def _():
                dgpost_ref[...] = jnp.zeros_like(dgpost_ref)

            dgpost_ref[...] += jnp.sum(dg_rows, axis=0, keepdims=True)

    args, in_specs, out_shape, out_specs = [dskip], [_row_spec(tr, d)], [], []
    if pre:
        args += list(pre)
        in_specs += [_row_spec(tr, d), _row_spec(tr, d), _vec_spec(d)]
        out_shape += [jax.ShapeDtypeStruct((t, d), F32), jax.ShapeDtypeStruct((1, d), F32)]
        out_specs += [_row_spec(tr, d), _vec_spec(d)]
    if post:
        args += list(post)
        in_specs += [_row_spec(tr, d), _vec_spec(d)]
        out_shape += [jax.ShapeDtypeStruct((t, d), BF16), jax.ShapeDtypeStruct((1, d), F32)]
        out_specs += [_row_spec(tr, d), _vec_spec(d)]
    return _call(body, name, (t // tr,), in_specs, out_specs, out_shape, args, sem=("arbitrary",),
                 nbytes=7 * _nbytes((tr, d), F32))


def _sigmoid(x):
    return 1.0 / (1.0 + jnp.exp(-x))


def _swiglu_fwd(gu, name):
    t = gu.shape[0]
    tr = min(ROW_BLOCK, t)

    def body(gu_ref, a_ref):
        gate, up = gu_ref[:, :D_FF].astype(F32), gu_ref[:, D_FF:].astype(F32)
        a_ref[...] = ((gate * _sigmoid(gate)) * up).astype(BF16)

    return _call(body, name, (t // tr,), [_row_spec(tr, 2 * D_FF)], [_row_spec(tr, D_FF)],
                 [jax.ShapeDtypeStruct((t, D_FF), BF16)], [gu], sem=("parallel",),
                 nbytes=2 * _nbytes((tr, 2 * D_FF), F32))[0]


def _swiglu_bwd(gu, dact, name):
    t = gu.shape[0]
    tr = min(ROW_BLOCK, t)

    def body(gu_ref, da_ref, o_ref):
        gate, up = gu_ref[:, :D_FF].astype(F32), gu_ref[:, D_FF:].astype(F32)
        da = da_ref[...].astype(F32)
        s = _sigmoid(gate)
        o_ref[:, :D_FF] = (da * up * (s * (1.0 + gate * (1.0 - s)))).astype(BF16)
        o_ref[:, D_FF:] = (da * (gate * s)).astype(BF16)

    return _call(body, name, (t // tr,), [_row_spec(tr, 2 * D_FF), _row_spec(tr, D_FF)], [_row_spec(tr, 2 * D_FF)],
                 [jax.ShapeDtypeStruct((t, 2 * D_FF), BF16)], [gu, dact], sem=("parallel",),
                 nbytes=3 * _nbytes((tr, 2 * D_FF), F32))[0]


def _split3(x):
    hi = x.astype(BF16)
    r1 = x - hi.astype(F32)
    mid = r1.astype(BF16)
    lo = (r1 - mid.astype(F32)).astype(BF16)
    return hi, mid, lo


def _dot_exact(mat01, x):
    hi, mid, lo = _split3(x)
    out = jnp.dot(mat01, hi, preferred_element_type=F32)
    out = out + jnp.dot(mat01, mid, preferred_element_type=F32)
    return out + jnp.dot(mat01, lo, preferred_element_type=F32)


def _dot_exact_rhs(x, mat01):
    hi, mid, lo = _split3(x)
    out = jnp.dot(hi, mat01, preferred_element_type=F32)
    out = out + jnp.dot(mid, mat01, preferred_element_type=F32)
    return out + jnp.dot(lo, mat01, preferred_element_type=F32)


def _tri(lower):
    r = lax.broadcasted_iota(jnp.int32, (CHUNK, CHUNK), 0)
    c = lax.broadcasted_iota(jnp.int32, (CHUNK, CHUNK), 1)
    return jnp.where(r >= c if lower else r <= c, 1.0, 0.0).astype(BF16)


def _log_sigmoid(z):
    return jnp.minimum(z, 0.0) - jnp.log(1.0 + jnp.exp(-jnp.abs(z)))


def _forget_cumsum(proj, b_pad, name):
    t = proj.shape[0]
    nb = t // CHUNK

    def body(f_ref, b_ref, c_ref):
        tri = _tri(True)
        b = b_ref[...]

        def blk(i, carry):
            rows = pl.ds(pl.multiple_of(i * CHUNK, CHUNK), CHUNK)
            cs = _dot_exact(tri, _log_sigmoid(f_ref[rows, :] + b)) + carry
            c_ref[rows, :] = cs
            return cs[CHUNK - 1:CHUNK, :]

        lax.fori_loop(0, nb, blk, jnp.zeros((1, HEAD_DIM), F32))

    return _call(body, name, (1,),
                 [pl.BlockSpec((t, HEAD_DIM), lambda i: (0, CB_F)), pl.BlockSpec((1, HEAD_DIM), lambda i: (0, 0))],
                 [pl.BlockSpec((t, HEAD_DIM), lambda i: (0, 0))], [jax.ShapeDtypeStruct((t, HEAD_DIM), F32)],
                 [proj, b_pad], sem=("arbitrary",), nbytes=2 * _nbytes((t, HEAD_DIM), F32))[0]


def _causal(block):
    r = lax.broadcasted_iota(jnp.int32, (block, block), 0)
    c = lax.broadcasted_iota(jnp.int32, (block, block), 1)
    return c <= r


def _lanes(x, width):
    return jnp.concatenate([x] * (width // HEAD_DIM), axis=1)


def _chunks(t, blk):
    return [slice(r * blk, (r + 1) * blk) for r in range(t // blk)]


def _lane_of_head(x, h):
    lane = lax.broadcasted_iota(jnp.int32, x.shape, 1)
    return jnp.broadcast_to(jnp.sum(jnp.where(lane == h, x, 0.0), axis=1, keepdims=True), x.shape)


def _causal_t(block):
    r = lax.broadcasted_iota(jnp.int32, (block, block), 0)
    c = lax.broadcasted_iota(jnp.int32, (block, block), 1)
    return r <= c


def _attn_fwd_t(proj, c_pad, c_row, name, comm=()):
    t = proj.shape[0]
    blk = min(ATTN_BLOCK, t)
    nq = t // blk
    scale = HEAD_DIM ** -0.5

    def body(q_ref, k_ref, v_ref, cp_ref, cr_ref, y_ref, e_ref, kb_s, vt_s, cb_s):
        h, i = pl.program_id(0), pl.program_id(1)

        @pl.when(i == 0)
        def _():
            for rows in _chunks(t, blk):
                kb_s[rows, :] = k_ref[rows, :].astype(BF16)
                vt_s[:, rows] = v_ref[rows, :].T.astype(BF16)
                cb_s[rows, :] = _lane_of_head(cp_ref[rows, :], h)

        q = (q_ref[...] * scale).astype(BF16)
        ci = cr_ref[:, pl.ds(pl.multiple_of(i * blk, blk), blk)]

        def step(j, carry, diagonal):
            m, l, acc = carry
            rows = pl.ds(pl.multiple_of(j * blk, blk), blk)
            s = lax.dot_general(kb_s[rows, :], q, _DIMS["nt"], preferred_element_type=F32)
            s = s + ci - _lanes(cb_s[rows, :], blk)
            if diagonal:
                s = jnp.where(_causal_t(blk), s, NEG)
            m_new = jnp.maximum(m, jnp.max(s, axis=0, keepdims=True))
            alpha = jnp.exp(m - m_new)
            p = jnp.exp(s - m_new)
            l = alpha * l + jnp.sum(p, axis=0, keepdims=True)
            acc = alpha * acc + jnp.dot(vt_s[:, rows], p.astype(BF16), preferred_element_type=F32)
            return m_new, l, acc

        init = (jnp.full((1, blk), NEG, F32), jnp.zeros((1, blk), F32), jnp.zeros((HEAD_DIM, blk), F32))
        carry = lax.fori_loop(0, i, lambda j, c: step(j, c, False), init)
        m, l, acc = step(i, carry, True)
        y_ref[...] = (acc / l).T
        e_ref[...] = ci - (m + jnp.log(l))

    nbytes = 3 * _nbytes((t, HEAD_DIM), F32) + 4 * _nbytes((blk, HEAD_DIM), F32) + 4 * _nbytes((blk, blk), F32)
    return _call(body, name, (N_HEADS, nq),
                 [pl.BlockSpec((blk, HEAD_DIM), lambda h, i: (i, CB_QKV + 3 * h)),
                  pl.BlockSpec((t, HEAD_DIM), lambda h, i: (0, CB_QKV + 3 * h + 1)),
                  pl.BlockSpec((t, HEAD_DIM), lambda h, i: (0, CB_QKV + 3 * h + 2)),
                  pl.BlockSpec((t, HEAD_DIM), lambda h, i: (0, 0)),
                  pl.BlockSpec((None, 1, t), lambda h, i: (h, 0, 0))],
                 [pl.BlockSpec((blk, HEAD_DIM), lambda h, i: (i, h)),
                  pl.BlockSpec((None, 1, blk), lambda h, i: (h, 0, i))],
                 [jax.ShapeDtypeStruct((t, D_MODEL), F32), jax.ShapeDtypeStruct((N_HEADS, 1, t), F32)],
                 [proj, proj, proj, c_pad, c_row],
                 scratch=[pltpu.VMEM((t, HEAD_DIM), BF16), pltpu.VMEM((HEAD_DIM, t), BF16),
                          pltpu.VMEM((t, HEAD_DIM), F32)],
                 sem=("parallel", "arbitrary"), nbytes=nbytes, comm=comm)


def _attn_bwd_t(proj, dy, e_row, delta_b, c_pad, dproj, name, deps=()):
    t = proj.shape[0]
    blk = min(ATTN_BLOCK, t)
    nb = t // blk
    scale = HEAD_DIM ** -0.5

    def body(q_ref, k_ref, v_ref, dy_ref, e_ref, dl_ref, cp_ref, dproj_in, dqkv_ref, dck_ref, dcq_ref,
             qb_s, qt_s, dyb_s, dyt_s, dl_s, dqt_acc):
        del dproj_in
        h, j = pl.program_id(0), pl.program_id(1)

        @pl.when(j == 0)
        def _():
            for rows in _chunks(t, blk):
                qs = q_ref[rows, :] * scale
                qb_s[rows, :] = qs.astype(BF16)
                qt_s[:, rows] = qs.T.astype(BF16)
                dyr = dy_ref[rows, :]
                dyb_s[rows, :] = dyr.astype(BF16)
                dyt_s[:, rows] = dyr.T.astype(BF16)
                dl_s[:, rows] = dl_ref[rows, :].T[0:1, :]
            dqt_acc[...] = jnp.zeros_like(dqt_acc)
            dcq_ref[...] = jnp.zeros_like(dcq_ref)

        kf = k_ref[...]
        kb, kt = kf.astype(BF16), kf.T.astype(BF16)
        vb = v_ref[...].astype(BF16)
        cj = _lanes(_lane_of_head(cp_ref[...], h), blk)

        def step(i, carry, diagonal):
            dkt, dvt, dkey = carry
            cols = pl.ds(pl.multiple_of(i * blk, blk), blk)
            s = lax.dot_general(kb, qb_s[cols, :], _DIMS["nt"], preferred_element_type=F32) + e_ref[:, cols] - cj
            if diagonal:
                s = jnp.where(_causal_t(blk), s, NEG)
            p = jnp.exp(s)
            dp = lax.dot_general(vb, dyb_s[cols, :], _DIMS["nt"], preferred_element_type=F32)
            ds = p * (dp - dl_s[:, cols])
            pb, dsb = p.astype(BF16), ds.astype(BF16)
            dvt = dvt + lax.dot_general(dyt_s[:, cols], pb, _DIMS["nt"], preferred_element_type=F32)
            dkt = dkt + lax.dot_general(qt_s[:, cols], dsb, _DIMS["nt"], preferred_element_type=F32)
            dqt_acc[:, cols] += jnp.dot(kt, dsb, preferred_element_type=F32) * scale
            dcq_ref[:, cols] += jnp.sum(ds, axis=0, keepdims=True)
            for k in range(blk // HEAD_DIM):
                dkey = dkey + ds[:, k * HEAD_DIM:(k + 1) * HEAD_DIM]
            return dkt, dvt, dkey

        zeros = jnp.zeros((HEAD_DIM, blk), F32)
        carry = step(j, (zeros, zeros, jnp.zeros((blk, HEAD_DIM), F32)), True)
        dkt, dvt, dkey = lax.fori_loop(j + 1, nb, lambda i, c: step(i, c, False), carry)
        mine = pl.ds(pl.multiple_of(j * blk, blk), blk)
        dqkv_ref[:, :HEAD_DIM] = dqt_acc[:, mine].T.astype(BF16)
        dqkv_ref[:, HEAD_DIM:2 * HEAD_DIM] = dkt.T.astype(BF16)
        dqkv_ref[:, 2 * HEAD_DIM:] = dvt.T.astype(BF16)
        dck_ref[...] = -jnp.sum(dkey.T, axis=0, keepdims=True)

    full = lambda cb: pl.BlockSpec((t, HEAD_DIM), lambda h, j: (0, cb(h)))
    head = lambda h: h
    row = pl.BlockSpec((None, 1, t), lambda h, j: (h, 0, 0))
    nbytes = (8 * _nbytes((t, HEAD_DIM), F32) + 4 * _nbytes((blk, HEAD_DIM), F32) + 6 * _nbytes((blk, blk), F32))
    return _call(body, name, (N_HEADS, nb),
                 [full(lambda h: CB_QKV + 3 * h),
                  pl.BlockSpec((blk, HEAD_DIM), lambda h, j: (j, CB_QKV + 3 * h + 1)),
                  pl.BlockSpec((blk, HEAD_DIM), lambda h, j: (j, CB_QKV + 3 * h + 2)),
                  full(head), row, full(head),
                  pl.BlockSpec((blk, HEAD_DIM), lambda h, j: (j, 0)),
                  pl.BlockSpec(memory_space=pl.ANY)],
                 [pl.BlockSpec((blk, QKV_WIDTH), lambda h, j: (j, (GATED + F_WIDTH) // QKV_WIDTH + h)),
                  pl.BlockSpec((None, 1, blk), lambda h, j: (h, 0, j)),
                  row],
                 [jax.ShapeDtypeStruct(dproj.shape, dproj.dtype), jax.ShapeDtypeStruct((N_HEADS, 1, t), F32),
                  jax.ShapeDtypeStruct((N_HEADS, 1, t), F32)],
                 [proj, proj, proj, dy, e_row, delta_b, c_pad, dproj],
                 scratch=[pltpu.VMEM((t, HEAD_DIM), BF16), pltpu.VMEM((HEAD_DIM, t), BF16),
                          pltpu.VMEM((t, HEAD_DIM), BF16), pltpu.VMEM((HEAD_DIM, t), BF16),
                          pltpu.VMEM((1, t), F32), pltpu.VMEM((HEAD_DIM, t), F32)],
                 sem=("parallel", "arbitrary"), nbytes=nbytes, aliases={7: 0}, deps=deps)


def _forget_bwd_t(proj, b_pad, dc, dproj, name):
    t = proj.shape[0]
    tr = min(ROW_BLOCK, t)
    nb = t // tr
    rev = lambda i: nb - 1 - i

    def body(f_ref, b_ref, dc_ref, dproj_in, df_ref, db_ref, run_ref):
        del dproj_in

        @pl.when(pl.program_id(0) == 0)
        def _():
            run_ref[...] = jnp.zeros_like(run_ref)
            db_ref[...] = jnp.zeros_like(db_ref)

        tri = _tri(False)
        b = b_ref[...]
        lane = lax.broadcasted_iota(jnp.int32, (CHUNK, HEAD_DIM), 1)
        df_ref[...] = jnp.zeros_like(df_ref)
        for n in reversed(range(tr // CHUNK)):
            rows = slice(n * CHUNK, (n + 1) * CHUNK)
            dlogf = _dot_exact(tri, dc_ref[rows, :]) + run_ref[...]
            run_ref[...] = dlogf[0:1, :]
            z = f_ref[rows, :] + b
            e = jnp.exp(-jnp.abs(z))
            sig_neg = jnp.where(z >= 0.0, e, 1.0) / (1.0 + e)
            df = jnp.where(lane < N_HEADS, dlogf * sig_neg, 0.0)
            df_ref[rows, :HEAD_DIM] = df.astype(BF16)
            db_ref[...] += jnp.sum(df, axis=0, keepdims=True)

    return _call(body, name, (nb,),
                 [pl.BlockSpec((tr, HEAD_DIM), lambda i: (rev(i), CB_F)), pl.BlockSpec((1, HEAD_DIM), lambda i: (0, 0)),
                  pl.BlockSpec((tr, HEAD_DIM), lambda i: (rev(i), 0)), pl.BlockSpec(memory_space=pl.ANY)],
                 [pl.BlockSpec((tr, F_WIDTH), lambda i: (rev(i), GATED // F_WIDTH)),
                  pl.BlockSpec((1, HEAD_DIM), lambda i: (0, 0))],
                 [jax.ShapeDtypeStruct(dproj.shape, dproj.dtype), jax.ShapeDtypeStruct((1, HEAD_DIM), F32)],
                 [proj, b_pad, dc, dproj], scratch=[pltpu.VMEM((1, HEAD_DIM), F32)],
                 sem=("arbitrary",), nbytes=4 * _nbytes((tr, HEAD_DIM), F32), aliases={3: 0})


_GELU_K = math.sqrt(2.0 / math.pi)
_GELU_C = 0.044715


def _gelu(x):
    t = jnp.tanh(_GELU_K * (x + _GELU_C * (x * x * x)))
    return 0.5 * x * (1.0 + t), t


def _gelu_grad(x, t):
    return 0.5 * (1.0 + t) + 0.5 * x * (1.0 - t * t) * (_GELU_K * (1.0 + 3.0 * _GELU_C * (x * x)))


def _layernorm_stats(a):
    mu = jnp.mean(a, axis=-1, keepdims=True)
    xc = a - mu
    r = lax.rsqrt(jnp.mean(xc * xc, axis=-1, keepdims=True) + EPS)
    return xc * r, r


def _group(g):
    return slice(g * CHUNK, (g + 1) * CHUNK)


def _gated_cols(k):
    return pl.BlockSpec((CHUNK, D_MODEL), lambda i: (i, k))


def _fixed(shape):
    return pl.BlockSpec(shape, lambda i: (0,) * len(shape))


def _mix_fwd(proj, y_b, w_s, b_cols, g_v, name):
    t = proj.shape[0]

    def body(u_ref, vs_ref, ga_ref, gb_ref, yb_ref, w_ref, b_ref, gv_ref, o_ref):
        a_u, _ = _gelu(u_ref[...])
        a_v, _ = _gelu(vs_ref[...])
        xhat, _ = _layernorm_stats(a_v)
        vn = (xhat * gv_ref[...]).astype(BF16)
        sa, sb = _sigmoid(ga_ref[...]), _sigmoid(gb_ref[...])
        yb = yb_ref[...]
        mask = _causal(CHUNK)
        for g in range(N_GROUPS):
            cols = _group(g)
            w = jnp.where(mask, w_ref[g], 0.0).astype(BF16)
            mixed = jnp.dot(w, vn[:, cols], preferred_element_type=F32) + b_ref[:, g:g + 1]
            o_ref[:, cols] = (sa[:, cols] * (a_u[:, cols] * mixed) + sb[:, cols] * yb[:, cols]).astype(BF16)

    return _call(body, name, (t // CHUNK,),
                 [_gated_cols(0), _gated_cols(1), _gated_cols(2), _gated_cols(3), _row_spec(CHUNK, D_MODEL),
                  _fixed((N_GROUPS, CHUNK, CHUNK)), _fixed((CHUNK, N_GROUPS)), _fixed((1, D_MODEL))],
                 [_row_spec(CHUNK, D_MODEL)], [jax.ShapeDtypeStruct((t, D_MODEL), BF16)],
                 [proj, proj, proj, proj, y_b, w_s, b_cols, g_v], sem=("parallel",),
                 nbytes=16 * _nbytes((CHUNK, D_MODEL), F32))[0]


def _mix_bwd(proj, y_b, dmerged, w_s, b_cols, g_v, sel, name, deps=()):
    t = proj.shape[0]

    def body(u_ref, vs_ref, ga_ref, gb_ref, yb_ref, dm_ref, w_ref, b_ref, gv_ref, sel_ref,
             dg_ref, dyb_ref, delta_ref, dw_ref, dbt_ref, dgv_ref):
        @pl.when(pl.program_id(0) == 0)
        def _():
            dw_ref[...] = jnp.zeros_like(dw_ref)
            dbt_ref[...] = jnp.zeros_like(dbt_ref)
            dgv_ref[...] = jnp.zeros_like(dgv_ref)

        u, vs = u_ref[...], vs_ref[...]
        a_u, t_u = _gelu(u)
        a_v, t_v = _gelu(vs)
        xhat, r = _layernorm_stats(a_v)
        gv = gv_ref[...]
        vn = (xhat * gv).astype(BF16)
        sa, sb = _sigmoid(ga_ref[...]), _sigmoid(gb_ref[...])
        yb, dm = yb_ref[...], dm_ref[...].astype(F32)
        dyb = dm * sb
        dyb_ref[...] = dyb
        dyb_yb = dyb * yb
        dg_ref[:, 3 * D_MODEL:] = (dm * yb * (sb * (1.0 - sb))).astype(BF16)
        dya = dm * sa
        mask = _causal(CHUNK)
        dmixed_parts, dvn_parts = [], []
        for g in range(N_GROUPS):
            cols = _group(g)
            delta_ref[:, cols] = jnp.broadcast_to(jnp.sum(dyb_yb[:, cols], axis=1, keepdims=True), (CHUNK, CHUNK))
            w = jnp.where(mask, w_ref[g], 0.0).astype(BF16)
            mixed = jnp.dot(w, vn[:, cols], preferred_element_type=F32) + b_ref[:, g:g + 1]
            y_a = a_u[:, cols] * mixed
            dg_ref[:, 2 * D_MODEL + g * CHUNK:2 * D_MODEL + (g + 1) * CHUNK] = (
                dm[:, cols] * y_a * (sa[:, cols] * (1.0 - sa[:, cols]))).astype(BF16)
            dg_ref[:, cols] = (dya[:, cols] * mixed * _gelu_grad(u[:, cols], t_u[:, cols])).astype(BF16)
            dmixed = dya[:, cols] * a_u[:, cols]
            dmb = dmixed.astype(BF16)
            dw = lax.dot_general(dmb, vn[:, cols], _DIMS["nt"], preferred_element_type=F32)
            dw_ref[g] += jnp.where(mask, dw, 0.0)
            dvn_parts.append(lax.dot_general(w, dmb, _DIMS["tn"], preferred_element_type=F32))
            dmixed_parts.append(dmixed)
        dmixed_all = jnp.concatenate(dmixed_parts, axis=1)
        dvn = jnp.concatenate(dvn_parts, axis=1)
        dbt_ref[...] += _dot_exact_rhs(dmixed_all, sel_ref[...])
        dgv_ref[...] += jnp.sum(dvn * xhat, axis=0, keepdims=True)
        dxh = dvn * gv
        da_v = r * (dxh - jnp.mean(dxh, axis=-1, keepdims=True)
                    - xhat * jnp.mean(dxh * xhat, axis=-1, keepdims=True))
        dg_ref[:, D_MODEL:2 * D_MODEL] = (da_v * _gelu_grad(vs, t_v)).astype(BF16)

    row = lambda width: _row_spec(CHUNK, width)
    return _call(body, name, (t // CHUNK,),
                 [_gated_cols(0), _gated_cols(1), _gated_cols(2), _gated_cols(3), row(D_MODEL), row(D_MODEL),
                  _fixed((N_GROUPS, CHUNK, CHUNK)), _fixed((CHUNK, N_GROUPS)), _fixed((1, D_MODEL)),
                  _fixed((D_MODEL, HEAD_DIM))],
                 [row(GATED), row(D_MODEL), row(D_MODEL),
                  _fixed((N_GROUPS, CHUNK, CHUNK)), _fixed((CHUNK, HEAD_DIM)), _fixed((1, D_MODEL))],
                 [jax.ShapeDtypeStruct((t, IN_PAD), BF16), jax.ShapeDtypeStruct((t, D_MODEL), F32),
                  jax.ShapeDtypeStruct((t, D_MODEL), F32), jax.ShapeDtypeStruct((N_GROUPS, CHUNK, CHUNK), F32),
                  jax.ShapeDtypeStruct((CHUNK, HEAD_DIM), F32), jax.ShapeDtypeStruct((1, D_MODEL), F32)],
                 [proj, proj, proj, proj, y_b, dmerged, w_s, b_cols, g_v, sel], sem=("arbitrary",),
                 nbytes=40 * _nbytes((CHUNK, D_MODEL), F32), deps=deps)


def _adamw_update(w_ref, g_ref, m_ref, v_ref, d_ref, nm_ref, nv_ref):
    gv = g_ref[...]
    nm = ADAM_B1 * m_ref[...] + (1.0 - ADAM_B1) * gv
    nv = ADAM_B2 * v_ref[...] + (1.0 - ADAM_B2) * (gv * gv)
    m_hat = nm / (1.0 - ADAM_B1 ** ADAM_STEP)
    v_hat = nv / (1.0 - ADAM_B2 ** ADAM_STEP)
    d_ref[...] = -ADAM_LR * (m_hat / (jnp.sqrt(v_hat) + ADAM_EPS) + ADAM_WD * w_ref[...])
    nm_ref[...] = nm
    nv_ref[...] = nv


def _adamw(w, g, m, v, name):
    r, c = w.shape
    tr = _tile(r, 256, unit=8)

    def body(w_ref, g_ref, m_ref, v_ref, d_ref, nm_ref, nv_ref):
        _adamw_update(w_ref, g_ref, m_ref, v_ref, d_ref, nm_ref, nv_ref)

    spec = pl.BlockSpec((tr, c), lambda i: (i, 0))
    shape = jax.ShapeDtypeStruct((r, c), F32)
    return _call(body, name, (r // tr,), [spec] * 4, [spec] * 3, [shape] * 3, [w, g, m, v], sem=("parallel",),
                 nbytes=7 * _nbytes((tr, max(c, 128)), F32))


def _adamw_cols(w, g, m, v, name):
    layers, r, c = w.shape
    tc = 256

    def body(w_ref, g_ref, m_ref, v_ref, d_ref, nm_ref, nv_ref):
        _adamw_update(w_ref, g_ref, m_ref, v_ref, d_ref, nm_ref, nv_ref)

    spec = pl.BlockSpec((None, r, tc), lambda l, j: (l, 0, j))
    shape = jax.ShapeDtypeStruct(w.shape, F32)
    return _call(body, name, (layers, c // tc), [spec] * 4, [spec] * 3, [shape] * 3, [w, g, m, v],
                 sem=("parallel", "parallel"), nbytes=7 * _nbytes((r, tc), F32))


def _pad_rows(a, rows):
    return jnp.pad(a, ((0, rows - a.shape[0]), (0, 0)))


def _unpack_in(gathered):
    i = jnp.arange(16)[None, :, None]
    b = jnp.arange(N_DEV)[:, None, None]
    head = jnp.where(i < b, jnp.roll(gathered[:, IN_STRIDE:IN_STRIDE + 16], 1, axis=0), gathered[:, :16])
    nat = jnp.concatenate([head, gathered[:, 16:IN_STRIDE]], axis=1).reshape(N_DEV * IN_STRIDE, D_MODEL)
    qkv = nat[2 * D_MODEL:5 * D_MODEL].reshape(3, N_HEADS, HEAD_DIM, D_MODEL)
    qkv = jnp.transpose(qkv, (1, 0, 2, 3)).reshape(3 * D_MODEL, D_MODEL)
    f = _pad_rows(gathered[N_DEV - 1, IN_STRIDE:IN_STRIDE + N_HEADS], F_WIDTH)
    return jnp.concatenate([nat[:2 * D_MODEL], nat[5 * D_MODEL:7 * D_MODEL], f, qkv], axis=0)


def _pack_grad_in(dw_in_t):
    qkv = dw_in_t[GATED + F_WIDTH:].reshape(N_HEADS, 3, HEAD_DIM, D_MODEL)
    qkv = jnp.transpose(qkv, (1, 0, 2, 3)).reshape(3 * D_MODEL, D_MODEL)
    return jnp.concatenate([dw_in_t[:2 * D_MODEL], qkv, dw_in_t[2 * D_MODEL:GATED],
                            dw_in_t[GATED:GATED + IN_ROWS - N_DEV * IN_STRIDE]], axis=0)


def kernel(x, mix_pre_g, w_in, b_forget, sgu_norm_g, w_spatial, b_spatial, w_out, mix_post_g, ffn_pre_g, w_gate, w_up, w_down, ffn_post_g, loss_target, m_mix_pre_g, m_w_in, m_b_forget, m_sgu_norm_g, m_w_spatial, m_b_spatial, m_w_out, m_mix_post_g, m_ffn_pre_g, m_w_gate, m_w_up, m_w_down, m_ffn_post_g, v_mix_pre_g, v_w_in, v_b_forget, v_sgu_norm_g, v_w_spatial, v_b_spatial, v_w_out, v_mix_post_g, v_ffn_pre_g, v_w_gate, v_w_up, v_w_down, v_ffn_post_g):
    depth = w_in.shape[0]
    assert depth == 2
    t = x.shape[1]
    x0 = x.reshape(t, D_MODEL)
    target = loss_target.reshape(t, D_MODEL)
    sel = (jnp.arange(D_MODEL)[:, None] // HEAD_DIM == jnp.arange(HEAD_DIM)[None, :]).astype(BF16)
    vec = lambda a, l: a[l][None, :]
    zero_row = jnp.zeros((1, D_MODEL), F32)

    me = 4 * lax.axis_index("x") + 2 * lax.axis_index("y") + lax.axis_index("c")
    p_in = [lax.dynamic_update_slice(jnp.zeros((P_IN, D_MODEL), BF16), w_in[l].T.astype(BF16), (me, 0))
            for l in range(depth)]
    p_rest = [[w_out[l].astype(BF16), jnp.stack([w_gate[l].T, w_up[l].T]).astype(BF16), w_down[l].astype(BF16)]
              for l in range(depth)]
    w_in_t = [None] * depth
    w_rest = [None] * depth
    w_in_t[0] = _unpack_in(_exchange(("gather", 0), p_in[0], "gather_w_in_0")[0])

    saved = []
    xl = x0
    h = _rms_fwd(xl, vec(mix_pre_g, 0), "rms_in_0")
    dy = loss = None
    for l in range(depth):
        b_pad = jnp.pad(b_forget[l], (0, HEAD_DIM - N_HEADS))[None, :]
        b_cols = b_spatial[l].T
        proj = _matmul(h, w_in_t[l], "nt", F32, f"proj_{l}")
        c_pad = _forget_cumsum(proj, b_pad, f"forget_cumsum_{l}")
        c_row = c_pad[:, :N_HEADS].T[:, None, :]
        riders = [(("gather", 0), p_rest[l][0]), (("gather", 1), p_rest[l][1]), (("gather", 0), p_rest[l][2])]
        riders += [(("gather", 0), p_in[l + 1])] if l + 1 < depth else []
        y_b, e_row, *arrived = _attn_fwd_t(proj, c_pad, c_row, f"attn_fwd_{l}", comm=riders)
        w_rest[l] = (arrived[0].reshape(D_MODEL, D_MODEL), arrived[1].reshape(2 * D_FF, D_MODEL),
                     arrived[2].reshape(D_FF, D_MODEL))
        if l + 1 < depth:
            w_in_t[l + 1] = _unpack_in(arrived[3])
        w_o, w_gu_t, w_d = w_rest[l]
        merged = _mix_fwd(proj, y_b, w_spatial[l], b_cols, vec(sgu_norm_g, l), f"mix_fwd_{l}")
        o = _matmul(merged, w_o, "nn", F32, f"out_proj_{l}")
        x1, h2 = _post_norm_fwd(xl, o, vec(mix_post_g, l), vec(ffn_pre_g, l), f"post_mix_{l}")
        gu = _matmul(h2, w_gu_t, "nt", BF16, f"ffn_gu_{l}")
        act = _swiglu_fwd(gu, f"swiglu_fwd_{l}")
        dn = _matmul(act, w_d, "nn", F32, f"ffn_down_{l}")
        saved.append(dict(x0=xl, h=h, proj=proj, b_pad=b_pad, b_cols=b_cols, c_pad=c_pad, y_b=y_b, e_row=e_row,
                          merged=merged, o=o, x1=x1, h2=h2, gu=gu, act=act, dn=dn))
        if l + 1 < depth:
            xl, h = _post_norm_fwd(x1, dn, vec(ffn_post_g, l), vec(mix_pre_g, l + 1), f"post_ffn_{l}")
        else:
            dy, loss = _post_norm_loss(x1, dn, vec(ffn_post_g, l), target, "loss")

    g_small = [None] * depth
    flights = {}
    late_token = None
    dx = dy
    dd, dg_ffn_post = _norm_bwd(dx, f"bwd_ffn_post_{depth - 1}", post=(saved[-1]["dn"], vec(ffn_post_g, depth - 1)))
    for l in reversed(range(depth)):
        s = saved[l]
        w_o, w_gu_t, w_d = w_rest[l]
        dact = _matmul(dd, w_d, "nt", BF16, f"d_act_{l}")
        dw_d = _matmul(s["act"], dd, "tn", BF16, f"dw_down_{l}")
        dgu = _swiglu_bwd(s["gu"], dact, f"swiglu_bwd_{l}")
        dh2 = _matmul(dgu, w_gu_t, "nn", F32, f"d_h2_{l}")
        dw_gu_t = _matmul(dgu, s["h2"], "tn", BF16, f"dw_gu_{l}")
        dx1, dg_ffn_pre, do, dg_mix_post = _norm_bwd(
            dx, f"bwd_post_mix_{l}", pre=(dh2, s["x1"], vec(ffn_pre_g, l)), post=(s["o"], vec(mix_post_g, l)))
        dmerged = _matmul(do, w_o, "nt", F32, f"d_merged_{l}")
        dw_o = _matmul(s["merged"], do, "tn", BF16, f"dw_out_{l}")
        tokens = []
        for key, dw, base, rows in [("out", dw_o, 0, SH_OUT), ("gate", dw_gu_t, 0, SH_FF), ("up", dw_gu_t, D_FF, SH_FF),
                                    ("down", dw_d, 0, SH_FF)]:
            flights[(key, l)], token = _exchange_start(("windows", base, rows, rows), dw, f"start_grad_{key}_{l}")
            tokens.append(token)
        dproj, dyb, delta_b, dw_s, dbt, dg_v = _mix_bwd(
            s["proj"], s["y_b"], dmerged, w_spatial[l], s["b_cols"], vec(sgu_norm_g, l), sel, f"mix_bwd_{l}",
            deps=tokens)
        db_s = dbt[:, :N_GROUPS].T.reshape(1, D_MODEL)
        tokens = []
        if l == 0:
            g_small[0] = jnp.concatenate([zero_row, dg_v, dg_mix_post, dg_ffn_pre, dg_ffn_post, db_s, zero_row,
                                          zero_row, dw_s.reshape(CHUNK, D_MODEL)], axis=0)
            flights["small"], token = _exchange_start(("spread",),jnp.concatenate([g_small[1], g_small[0]], axis=0),
                                                      "start_small_grads")
            tokens = [token]
        dproj, dc_key, dc_query = _attn_bwd_t(s["proj"], dyb, s["e_row"], delta_b, s["c_pad"], dproj,
                                              f"attn_bwd_{l}", deps=tokens)
        dc = jnp.pad((dc_key + dc_query)[:, 0, :].T, ((0, 0), (0, HEAD_DIM - N_HEADS)))
        dproj, db_f = _forget_bwd_t(s["proj"], s["b_pad"], dc, dproj, f"forget_bwd_{l}")
        db_f_row = jnp.pad(db_f, ((0, 0), (0, D_MODEL - HEAD_DIM)))
        dw_in_t = _matmul(dproj, s["h"], "tn", BF16, f"dw_in_{l}")
        flights[("in", l)], token = _exchange_start(("windows", 0, IN_STRIDE, P_IN), _pack_grad_in(dw_in_t),
                                                    f"start_grad_in_{l}")
        dh = _matmul(dproj, w_in_t[l], "nn", F32, f"d_h_{l}", deps=[token])
        if l > 0:
            dx, dg_mix_pre, dd, dg_ffn_post_below = _norm_bwd(
                dx1, f"bwd_pre_mix_{l}", pre=(dh, s["x0"], vec(mix_pre_g, l)),
                post=(saved[l - 1]["dn"], vec(ffn_post_g, l - 1)))
            loss_row = jnp.pad(loss, ((0, 0), (0, D_MODEL - 1)))
            g_small[l] = jnp.concatenate([dg_mix_pre, dg_v, dg_mix_post, dg_ffn_pre, dg_ffn_post, db_s, db_f_row,
                                          loss_row, dw_s.reshape(CHUNK, D_MODEL)], axis=0)
            dg_ffn_post = dg_ffn_post_below
        else:
            dx, dg_mix_pre = _norm_bwd(dx1, f"bwd_pre_mix_{l}", pre=(dh, s["x0"], vec(mix_pre_g, l)))
            late_rows = jnp.concatenate([dg_mix_pre, db_f_row] + [zero_row] * 6, axis=0)
            flights["late"], late_token = _exchange_start(("spread",),late_rows, "start_late_small_grads")
    grad_x = dx.reshape(x.shape)

    landed, after = {}, [grad_x, late_token]
    rest = ["out", "gate", "up", "down"]
    for key in [(n, 1) for n in rest] + [("in", 1)] + [(n, 0) for n in rest] + ["small"]:
        label = key if isinstance(key, str) else f"grad_{key[0]}_{key[1]}"
        landed[key] = _exchange_wait(flights[key], after, f"wait_{label}")
        after = [landed[key]]
    small_sum = _sum_slots(landed["small"], S_ROWS, "sum_small_grads")
    sm = small_sum.reshape(depth, S_ROWS, D_MODEL)[::-1]
    loss = sm[depth - 1, 7, 0]
    g = {
        "sgu_norm_g": sm[:, 1], "mix_post_g": sm[:, 2],
        "ffn_pre_g": sm[:, 3], "ffn_post_g": sm[:, 4], "b_spatial": sm[:, 5].reshape(depth, N_GROUPS, CHUNK),
        "w_spatial": sm[:, 8:].reshape(depth, N_GROUPS, CHUNK, CHUNK),
    }
    for n in rest:
        g["w_" + n] = jnp.stack([_sum_slots(landed[(n, l)], landed[(n, l)].shape[1], f"sum_grad_{n}_{l}")
                                 for l in range(depth)])

    names = ["mix_pre_g", "w_in", "b_forget", "sgu_norm_g", "w_spatial", "b_spatial", "w_out", "mix_post_g",
             "ffn_pre_g", "w_gate", "w_up", "w_down", "ffn_post_g"]
    ws = dict(mix_pre_g=mix_pre_g, w_in=w_in, b_forget=b_forget, sgu_norm_g=sgu_norm_g, w_spatial=w_spatial,
              b_spatial=b_spatial, w_out=w_out, mix_post_g=mix_post_g, ffn_pre_g=ffn_pre_g, w_gate=w_gate, w_up=w_up,
              w_down=w_down, ffn_post_g=ffn_post_g)
    ms = dict(mix_pre_g=m_mix_pre_g, w_in=m_w_in, b_forget=m_b_forget, sgu_norm_g=m_sgu_norm_g, w_spatial=m_w_spatial,
              b_spatial=m_b_spatial, w_out=m_w_out, mix_post_g=m_mix_post_g, ffn_pre_g=m_ffn_pre_g, w_gate=m_w_gate,
              w_up=m_w_up, w_down=m_w_down, ffn_post_g=m_ffn_post_g)
    vs = dict(mix_pre_g=v_mix_pre_g, w_in=v_w_in, b_forget=v_b_forget, sgu_norm_g=v_sgu_norm_g, w_spatial=v_w_spatial,
              b_spatial=v_b_spatial, w_out=v_w_out, mix_post_g=v_mix_post_g, ffn_pre_g=v_ffn_pre_g, w_gate=v_w_gate,
              w_up=v_w_up, w_down=v_w_down, ffn_post_g=v_ffn_post_g)
    updates = {}
    last = ["w_in", "mix_pre_g", "b_forget"]
    transposed = ["w_in", "w_gate", "w_up"]
    for n in [n for n in names if n not in last] + last:
        if n == last[0]:
            sum_in_1 = _sum_slots(landed[("in", 1)], 304, "sum_grad_in_1")
            done = [u[0] for u in updates.values()] + [sum_in_1]
            sum_in_0 = _sum_slots(_exchange_wait(flights[("in", 0)], done, "wait_grad_in_0"), 304, "sum_grad_in_0")
            late = _sum_slots(_exchange_wait(flights["late"], [sum_in_0], "wait_late"), 8, "sum_late_small_grads")
            g["w_in"] = lax.dynamic_slice(jnp.stack([sum_in_0, sum_in_1]), (0, me, 0), (depth, SH_IN, D_MODEL))
            g["mix_pre_g"] = jnp.stack([late[0], sm[1, 0]])
            g["b_forget"] = jnp.stack([late[1, :N_HEADS], sm[1, 6, :N_HEADS]])
        if n in transposed:
            view = lambda a: jnp.swapaxes(a, 1, 2)
            updates[n] = [view(u) for u in _adamw_cols(view(ws[n]), g[n], view(ms[n]), view(vs[n]), f"adamw_{n}")]
            g[n] = view(g[n])
        else:
            flat = (-1, ws[n].shape[-1])
            updates[n] = [u.reshape(ws[n].shape) for u in
                          _adamw(ws[n].reshape(flat), g[n].reshape(flat), ms[n].reshape(flat), vs[n].reshape(flat),
                                 f"adamw_{n}")]
    deltas = [updates[n][0] for n in names]
    new_m = [updates[n][1] for n in names]
    new_v = [updates[n][2] for n in names]
    grads = [g[n].reshape(ws[n].shape) for n in names]
    return (loss, grad_x, *grads, *deltas, *new_m, *new_v)
```

```python
import math

import jax
import jax.numpy as jnp
from jax import lax
from jax.experimental import pallas as pl
from jax.experimental.pallas import tpu as pltpu

F32 = jnp.float32
BF16 = jnp.bfloat16

N_DEV = 8
D_MODEL = 1024
N_HEADS = 8
HEAD_DIM = 128
CHUNK = 128
N_GROUPS = 8
D_FF = 2816
IN_WIDTH = 7 * D_MODEL + N_HEADS
IN_PAD = 7680
EPS = 1e-6
GATED = 4 * D_MODEL
F_WIDTH = 512
QKV_WIDTH = 3 * HEAD_DIM
CB_F = GATED // HEAD_DIM
CB_QKV = (GATED + F_WIDTH) // HEAD_DIM
assert GATED + F_WIDTH + N_HEADS * QKV_WIDTH == IN_PAD and (GATED + F_WIDTH) % QKV_WIDTH == 0

ADAM_LR, ADAM_B1, ADAM_B2, ADAM_EPS, ADAM_WD, ADAM_STEP = 0.001, 0.9, 0.999, 1e-08, 0.01, 10

SH_IN = IN_WIDTH // N_DEV
SH_OUT = D_MODEL // N_DEV
SH_FF = D_FF // N_DEV
P_IN = 912
IN_STRIDE = 7 * HEAD_DIM
IN_ROWS = IN_STRIDE * (N_DEV - 1) + P_IN
S_ROWS = 136

ATTN_BLOCK = 1024
ROW_BLOCK = 512
MM_TM, MM_TN, MM_TK = 2048, 1536, 2048
VMEM_CAP = 56 << 20
NEG = -1e30


def _tile(n, cap, unit=128):
    if n <= cap:
        return n
    best = None
    for t in range(unit, cap + 1, unit):
        if n % t == 0:
            best = t
    assert best is not None, (n, cap)
    return best


def _nbytes(shape, dtype):
    return math.prod(shape) * jnp.dtype(dtype).itemsize


_HBM = pl.BlockSpec(memory_space=pltpu.HBM)
MESH = pl.DeviceIdType.MESH

_N_REMOTE = N_DEV - 1


def _exchange_shapes(kind, x):
    if kind[0] == "gather":
        shape = x.shape[:kind[1]] + (N_DEV,) + x.shape[kind[1]:]
    elif kind[0] == "spread":
        shape = (N_DEV,) + x.shape
    else:
        shape = (N_DEV, kind[3]) + x.shape[1:]
    return [jax.ShapeDtypeStruct(shape, x.dtype)]


def _exchange_sems(kind):
    del kind
    return [pltpu.SemaphoreType.DMA((_N_REMOTE,)), pltpu.SemaphoreType.DMA((_N_REMOTE,)), pltpu.SemaphoreType.DMA]


def _plan(kind, x_ref, outs, sems):
    send_sems, recv_sems, local_sem = sems
    x, y, c = lax.axis_index("x"), lax.axis_index("y"), lax.axis_index("c")

    def remote(src, dst, k, to):
        return pltpu.make_async_remote_copy(src_ref=src, dst_ref=dst, send_sem=send_sems.at[k], recv_sem=recv_sems.at[k],
                                            device_id=to, device_id_type=MESH)

    sibling = (x, y, 1 - c)
    if kind[0] == "gather":
        out, = outs
        slot = lambda px, py, pc: out.at[(slice(None),) * kind[1] + (4 * px + 2 * py + pc,)]
        chips = [(1 - x, y), (x, 1 - y), (1 - x, 1 - y)]
        local = [pltpu.make_async_copy(x_ref, slot(x, y, c), local_sem)]
        first = [remote(x_ref, slot(x, y, c), 0, sibling)]
        first += [remote(x_ref, slot(x, y, c), 1 + k, (*chip, c)) for k, chip in enumerate(chips)]
        relays = [(remote(x_ref, slot(*chip, c), 1 + k, (*chip, c)), remote(slot(*chip, c), slot(*chip, c), 4 + k, sibling))
                  for k, chip in enumerate(chips)]
        arrivals = [remote(x_ref, slot(x, y, 1 - c), 0, sibling)]
        arrivals += [remote(x_ref, slot(*chip, 1 - c), 4 + k, sibling) for k, chip in enumerate(chips)]
        return local, first, relays, arrivals
    out, = outs
    mine = 4 * x + 2 * y + c
    if kind[0] == "spread":
        src = lambda d: x_ref
    else:
        _, base, stride, length = kind
        src = lambda d: x_ref.at[pl.ds(pl.multiple_of(base + stride * d, 16), length)]
    first, arrivals = [], []
    for j in range(1, N_DEV):
        peer = (1 - x if j & 4 else x, 1 - y if j & 2 else y, 1 - c if j & 1 else c)
        theirs = 4 * peer[0] + 2 * peer[1] + peer[2]
        first.append(remote(src(theirs), out.at[mine], j - 1, peer))
        arrivals.append(remote(src(theirs), out.at[theirs], j - 1, peer))
    local = [] if local_sem is None else [pltpu.make_async_copy(src(mine), out.at[mine], local_sem)]
    return local, first, [], arrivals


def _start(plan):
    local, first, _, _ = plan
    for cp in local + first:
        cp.start()


def _finish(plan):
    local, first, relays, arrivals = plan
    for arrival, onward in relays:
        arrival.wait_recv()
        onward.start()
    for cp in arrivals:
        cp.wait_recv()
    for cp in first + [onward for _, onward in relays]:
        cp.wait_send()
    for cp in local:
        cp.wait()


def _direct_copies(kind, x_ref, land_ref, send_sems, recv_sems):
    _, first, _, arrivals = _plan(kind, x_ref, [land_ref], (send_sems, recv_sems, None))
    return first, arrivals


_SEM = pl.BlockSpec(memory_space=pltpu.SEMAPHORE)
_DATAFLOW = pltpu.SideEffectType.DATAFLOW_SIDE_EFFECTING


def _exchange_start(kind, x, name):
    me = 4 * lax.axis_index("x") + 2 * lax.axis_index("y") + lax.axis_index("c")
    own = x if kind[0] == "spread" else lax.dynamic_slice_in_dim(x, kind[1] + kind[2] * me, kind[3], 0)
    shape = _exchange_shapes(kind, x)[0].shape
    land = lax.dynamic_update_slice_in_dim(lax.empty(shape, x.dtype), own[None], me, 0)

    def body(x_ref, land_ref, send_sems, recv_sems, x_thru, land_thru, token):
        del x_thru, land_thru
        for cp in _direct_copies(kind, x_ref, land_ref, send_sems, recv_sems)[0]:
            cp.start()
        token[...] = jnp.zeros_like(token)

    *handle, token = pl.pallas_call(
        body, name=name,
        out_shape=(pltpu.SemaphoreType.DMA((_N_REMOTE,)), pltpu.SemaphoreType.DMA((_N_REMOTE,)),
                   pltpu.HBM(x.shape, x.dtype), pltpu.HBM(land.shape, land.dtype), jax.ShapeDtypeStruct((8, 128), F32)),
        in_specs=(_HBM, _HBM), out_specs=(_SEM, _SEM, _HBM, _HBM, pl.BlockSpec(memory_space=pltpu.VMEM)),
        input_output_aliases={0: 2, 1: 3}, compiler_params=pltpu.CompilerParams(has_side_effects=_DATAFLOW),
    )(pltpu.with_memory_space_constraint(x, pltpu.HBM), pltpu.with_memory_space_constraint(land, pltpu.HBM))
    return (kind, *handle), token


def _exchange_wait(handle, after, name):
    kind, send_sems, recv_sems, x_thru, land_thru = handle

    def body(x_ref, land_ref, send_sems, recv_sems, *rest):
        first, arrivals = _direct_copies(kind, x_ref, land_ref, send_sems, recv_sems)
        for cp in first:
            cp.wait_send()
        for cp in arrivals:
            cp.wait_recv()

    return pl.pallas_call(
        body, name=name,
        out_shape=(pltpu.HBM(x_thru.shape, x_thru.dtype), pltpu.HBM(land_thru.shape, land_thru.dtype)),
        in_specs=(_HBM, _HBM, _SEM, _SEM) + (pl.BlockSpec(memory_space=pl.ANY),) * len(after), out_specs=(_HBM, _HBM),
        input_output_aliases={0: 0, 1: 1}, compiler_params=pltpu.CompilerParams(has_side_effects=_DATAFLOW),
    )(x_thru, land_thru, send_sems, recv_sems, *after)[1]


def _exchange(kind, x, name):
    n_out = len(_exchange_shapes(kind, x))

    def body(x_ref, *refs):
        plan = _plan(kind, x_ref, refs[:n_out], refs[n_out:])
        _start(plan)
        _finish(plan)

    return pl.pallas_call(body, name=name, out_shape=_exchange_shapes(kind, x), in_specs=[_HBM],
                          out_specs=[_HBM] * n_out, scratch_shapes=_exchange_sems(kind))(x)


def _call(body, name, grid, in_specs, out_specs, out_shape, args, scratch=(), sem=None, nbytes=0, aliases=None,
          comm=(), deps=()):
    in_specs, out_specs, out_shape, args, scratch = (list(in_specs), list(out_specs), list(out_shape), list(args),
                                                     list(scratch))
    if deps:
        n_real, n_deps, unordered = len(args), len(deps), body

        def body(*refs):
            unordered(*refs[:n_real], *refs[n_real + n_deps:])

        in_specs += [pl.BlockSpec(memory_space=pl.ANY)] * n_deps
        args += list(deps)
    if comm:
        n_in, n_out, n_scr, n_ops = len(args), len(out_shape), len(scratch), len(comm)
        kinds = [kind for kind, _ in comm]
        shapes = [_exchange_shapes(kind, x) for kind, x in comm]
        inner = body

        def body(*refs):
            ins, refs = refs[:n_in], refs[n_in:]
            cins, refs = refs[:n_ops], refs[n_ops:]
            outs, refs = refs[:n_out], refs[n_out:]
            couts = []
            for sh in shapes:
                couts.append(refs[:len(sh)])
                refs = refs[len(sh):]
            own_scratch, sems = refs[:n_scr], refs[n_scr:]
            first = last = None
            for axis, size in enumerate(grid):
                at_start, at_end = pl.program_id(axis) == 0, pl.program_id(axis) == size - 1
                first = at_start if first is None else first & at_start
                last = at_end if last is None else last & at_end
            plans = [_plan(kinds[o], cins[o], couts[o], sems[3 * o:3 * o + 3]) for o in range(n_ops)]

            @pl.when(first)
            def _():
                for p in plans:
                    _start(p)

            inner(*ins, *outs, *own_scratch)

            @pl.when(last)
            def _():
                for p in plans:
                    _finish(p)

        in_specs += [_HBM] * n_ops
        args += [x for _, x in comm]
        for kind, sh in zip(kinds, shapes):
            out_shape += sh
            out_specs += [_HBM] * len(sh)
            scratch += _exchange_sems(kind)
        sem = ("arbitrary",) * len(grid)
    limit = int(min(max(2 * nbytes + (8 << 20), 32 << 20), VMEM_CAP))
    return pl.pallas_call(
        body, name=name, grid=grid, in_specs=in_specs, out_specs=out_specs, out_shape=out_shape,
        scratch_shapes=scratch, input_output_aliases=aliases or {},
        compiler_params=pltpu.CompilerParams(dimension_semantics=sem, vmem_limit_bytes=limit),
    )(*args)


def _sum_slots(x, tr, name):
    n, r, c = x.shape

    def body(x_ref, o_ref):
        acc = x_ref[0].astype(F32)
        for d in range(1, n):
            acc = acc + x_ref[d].astype(F32)
        o_ref[...] = acc

    return _call(body, name, (r // tr,), [pl.BlockSpec((n, tr, c), lambda i: (0, i, 0))],
                 [pl.BlockSpec((tr, c), lambda i: (i, 0))], [jax.ShapeDtypeStruct((r, c), F32)], [x],
                 sem=("parallel",), nbytes=_nbytes((n, tr, c), x.dtype) + _nbytes((tr, c), F32))[0]


_DIMS = {"nn": (((1,), (0,)), ((), ())), "nt": (((1,), (1,)), ((), ())), "tn": (((0,), (0,)), ((), ()))}


def _matmul(a, b, mode, out_dtype, name, comm=(), deps=()):
    if mode == "nn":
        (m, k), (k2, n) = a.shape, b.shape
    elif mode == "nt":
        (m, k), (n, k2) = a.shape, b.shape
    else:
        (k, m), (k2, n) = a.shape, b.shape
    assert k == k2, (a.shape, b.shape, mode)
    tm, tn, tk = _tile(m, MM_TM), _tile(n, MM_TN), _tile(k, MM_TK)
    nk = k // tk
    dims = _DIMS[mode]
    a_spec = {"nn": pl.BlockSpec((tm, tk), lambda i, j, kk: (i, kk)),
              "nt": pl.BlockSpec((tm, tk), lambda i, j, kk: (i, kk)),
              "tn": pl.BlockSpec((tk, tm), lambda i, j, kk: (kk, i))}[mode]
    b_spec = {"nn": pl.BlockSpec((tk, tn), lambda i, j, kk: (kk, j)),
              "nt": pl.BlockSpec((tn, tk), lambda i, j, kk: (j, kk)),
              "tn": pl.BlockSpec((tk, tn), lambda i, j, kk: (kk, j))}[mode]

    def partial_product(a_ref, b_ref):
        return lax.dot_general(a_ref[...].astype(BF16), b_ref[...].astype(BF16), dims, preferred_element_type=F32)

    if nk == 1:
        def body(a_ref, b_ref, o_ref):
            o_ref[...] = partial_product(a_ref, b_ref).astype(o_ref.dtype)
        scratch = []
    else:
        def body(a_ref, b_ref, o_ref, acc_ref):
            kk = pl.program_id(2)

            @pl.when(kk == 0)
            def _():
                acc_ref[...] = jnp.zeros_like(acc_ref)

            acc_ref[...] += partial_product(a_ref, b_ref)

            @pl.when(kk == nk - 1)
            def _():
                o_ref[...] = acc_ref[...].astype(o_ref.dtype)
        scratch = [pltpu.VMEM((tm, tn), F32)]

    nbytes = (_nbytes((tm, tk), a.dtype) + _nbytes((tk, tn), b.dtype) + _nbytes((tm, tn), out_dtype)
              + _nbytes((tm, tn), F32))
    res = _call(body, name, (m // tm, n // tn, nk), [a_spec, b_spec],
                [pl.BlockSpec((tm, tn), lambda i, j, kk: (i, j))], [jax.ShapeDtypeStruct((m, n), out_dtype)], [a, b],
                scratch=scratch, sem=("parallel", "parallel", "arbitrary"), nbytes=nbytes, comm=comm, deps=deps)
    return res if comm else res[0]


def _rms(x):
    return lax.rsqrt(jnp.mean(x * x, axis=-1, keepdims=True) + EPS)


def _rms_bwd(dz, a, g):
    r = _rms(a)
    dzg = dz * g
    da = r * dzg - a * (r * r * r) * jnp.mean(dzg * a, axis=-1, keepdims=True)
    return da, dz * (a * r)


def _row_spec(tr, width):
    return pl.BlockSpec((tr, width), lambda i: (i, 0))


def _vec_spec(width):
    return pl.BlockSpec((1, width), lambda i: (0, 0))


def _rms_fwd(x, g, name):
    t, d = x.shape
    tr = min(ROW_BLOCK, t)

    def body(x_ref, g_ref, h_ref):
        xv = x_ref[...]
        h_ref[...] = ((xv * _rms(xv)) * g_ref[...]).astype(BF16)

    return _call(body, name, (t // tr,), [_row_spec(tr, d), _vec_spec(d)], [_row_spec(tr, d)],
                 [jax.ShapeDtypeStruct((t, d), BF16)], [x, g], sem=("parallel",), nbytes=3 * _nbytes((tr, d), F32))[0]


def _post_norm_fwd(x, o, g_post, g_next, name):
    t, d = x.shape
    tr = min(ROW_BLOCK, t)

    def body(x_ref, o_ref, gp_ref, gn_ref, xn_ref, h_ref):
        ov = o_ref[...]
        xn = x_ref[...] + (ov * _rms(ov)) * gp_ref[...]
        xn_ref[...] = xn
        h_ref[...] = ((xn * _rms(xn)) * gn_ref[...]).astype(BF16)

    return _call(body, name, (t // tr,), [_row_spec(tr, d), _row_spec(tr, d), _vec_spec(d), _vec_spec(d)],
                 [_row_spec(tr, d), _row_spec(tr, d)],
                 [jax.ShapeDtypeStruct((t, d), F32), jax.ShapeDtypeStruct((t, d), BF16)], [x, o, g_post, g_next],
                 sem=("parallel",), nbytes=5 * _nbytes((tr, d), F32))


def _post_norm_loss(x, o, g_post, target, name):
    t, d = x.shape
    tr = min(ROW_BLOCK, t)

    def body(x_ref, o_ref, gp_ref, t_ref, dy_ref, loss_ref):
        ov = o_ref[...]
        err = x_ref[...] + (ov * _rms(ov)) * gp_ref[...] - t_ref[...]
        dy_ref[...] = err / d
        part = 0.5 * jnp.sum(jnp.mean(err * err, axis=-1, keepdims=True), axis=0, keepdims=True)

        @pl.when(pl.program_id(0) == 0)
        def _():
            loss_ref[...] = jnp.zeros_like(loss_ref)

        loss_ref[...] += part

    return _call(body, name, (t // tr,), [_row_spec(tr, d), _row_spec(tr, d), _vec_spec(d), _row_spec(tr, d)],
                 [_row_spec(tr, d), pl.BlockSpec((1, 1), lambda i: (0, 0))],
                 [jax.ShapeDtypeStruct((t, d), F32), jax.ShapeDtypeStruct((1, 1), F32)], [x, o, g_post, target],
                 sem=("arbitrary",), nbytes=5 * _nbytes((tr, d), F32))


def _norm_bwd(dskip, name, pre=None, post=None):
    t, d = dskip.shape
    tr = min(ROW_BLOCK, t)
    n_in = 1 + (3 if pre else 0) + (2 if post else 0)

    def body(*refs):
        ins, outs = list(refs[:n_in]), list(refs[n_in:])
        first = pl.program_id(0) == 0
        dx = ins.pop(0)[...]
        if pre:
            dh_ref, xin_ref, gpre_ref = ins.pop(0), ins.pop(0), ins.pop(0)
            dxin, dg_rows = _rms_bwd(dh_ref[...].astype(F32), xin_ref[...], gpre_ref[...])
            dx = dx + dxin
            dx_ref, dgpre_ref = outs.pop(0), outs.pop(0)
            dx_ref[...] = dx

            @pl.when(first)
            def _():
                dgpre_ref[...] = jnp.zeros_like(dgpre_ref)

            dgpre_ref[...] += jnp.sum(dg_rows, axis=0, keepdims=True)
        if post:
            a_ref, gpost_ref = ins.pop(0), ins.pop(0)
            da, dg_rows = _rms_bwd(dx, a_ref[...], gpost_ref[...])
            da_ref, dgpost_ref = outs.pop(0), outs.pop(0)
            da_ref[...] = da.astype(BF16)

            @pl.when(first)
            def _():
                dgpost_ref[...] = jnp.zeros_like(dgpost_ref)

            dgpost_ref[...] += jnp.sum(dg_rows, axis=0, keepdims=True)

    args, in_specs, out_shape, out_specs = [dskip], [_row_spec(tr, d)], [], []
    if pre:
        args += list(pre)
        in_specs += [_row_spec(tr, d), _row_spec(tr, d), _vec_spec(d)]
        out_shape += [jax.ShapeDtypeStruct((t, d), F32), jax.ShapeDtypeStruct((1, d), F32)]
        out_specs += [_row_spec(tr, d), _vec_spec(d)]
    if post:
        args += list(post)
        in_specs += [_row_spec(tr, d), _vec_spec(d)]
        out_shape += [jax.ShapeDtypeStruct((t, d), BF16), jax.ShapeDtypeStruct((1, d), F32)]
        out_specs += [_row_spec(tr, d), _vec_spec(d)]
    return _call(body, name, (t // tr,), in_specs, out_specs, out_shape, args, sem=("arbitrary",),
                 nbytes=7 * _nbytes((tr, d), F32))


def _sigmoid(x):
    return 1.0 / (1.0 + jnp.exp(-x))


def _swiglu_fwd(gu, name):
    t = gu.shape[0]
    tr = min(ROW_BLOCK, t)

    def body(gu_ref, a_ref):
        gate, up = gu_ref[:, :D_FF].astype(F32), gu_ref[:, D_FF:].astype(F32)
        a_ref[...] = ((gate * _sigmoid(gate)) * up).astype(BF16)

    return _call(body, name, (t // tr,), [_row_spec(tr, 2 * D_FF)], [_row_spec(tr, D_FF)],
                 [jax.ShapeDtypeStruct((t, D_FF), BF16)], [gu], sem=("parallel",),
                 nbytes=2 * _nbytes((tr, 2 * D_FF), F32))[0]


def _swiglu_bwd(gu, dact, name):
    t = gu.shape[0]
    tr = min(ROW_BLOCK, t)

    def body(gu_ref, da_ref, o_ref):
        gate, up = gu_ref[:, :D_FF].astype(F32), gu_ref[:, D_FF:].astype(F32)
        da = da_ref[...].astype(F32)
        s = _sigmoid(gate)
        o_ref[:, :D_FF] = (da * up * (s * (1.0 + gate * (1.0 - s)))).astype(BF16)
        o_ref[:, D_FF:] = (da * (gate * s)).astype(BF16)

    return _call(body, name, (t // tr,), [_row_spec(tr, 2 * D_FF), _row_spec(tr, D_FF)], [_row_spec(tr, 2 * D_FF)],
                 [jax.ShapeDtypeStruct((t, 2 * D_FF), BF16)], [gu, dact], sem=("parallel",),
                 nbytes=3 * _nbytes((tr, 2 * D_FF), F32))[0]


def _split3(x):
    hi = x.astype(BF16)
    r1 = x - hi.astype(F32)
    mid = r1.astype(BF16)
    lo = (r1 - mid.astype(F32)).astype(BF16)
    return hi, mid, lo


def _dot_exact(mat01, x):
    hi, mid, lo = _split3(x)
    out = jnp.dot(mat01, hi, preferred_element_type=F32)
    out = out + jnp.dot(mat01, mid, preferred_element_type=F32)
    return out + jnp.dot(mat01, lo, preferred_element_type=F32)


def _dot_exact_rhs(x, mat01):
    hi, mid, lo = _split3(x)
    out = jnp.dot(hi, mat01, preferred_element_type=F32)
    out = out + jnp.dot(mid, mat01, preferred_element_type=F32)
    return out + jnp.dot(lo, mat01, preferred_element_type=F32)


def _tri(lower):
    r = lax.broadcasted_iota(jnp.int32, (CHUNK, CHUNK), 0)
    c = lax.broadcasted_iota(jnp.int32, (CHUNK, CHUNK), 1)
    return jnp.where(r >= c if lower else r <= c, 1.0, 0.0).astype(BF16)


def _log_sigmoid(z):
    return jnp.minimum(z, 0.0) - jnp.log(1.0 + jnp.exp(-jnp.abs(z)))


def _forget_cumsum(proj, b_pad, name):
    t = proj.shape[0]
    nb = t // CHUNK

    def body(f_ref, b_ref, c_ref):
        tri = _tri(True)
        b = b_ref[...]

        def blk(i, carry):
            rows = pl.ds(pl.multiple_of(i * CHUNK, CHUNK), CHUNK)
            cs = _dot_exact(tri, _log_sigmoid(f_ref[rows, :] + b)) + carry
            c_ref[rows, :] = cs
            return cs[CHUNK - 1:CHUNK, :]

        lax.fori_loop(0, nb, blk, jnp.zeros((1, HEAD_DIM), F32))

    return _call(body, name, (1,),
                 [pl.BlockSpec((t, HEAD_DIM), lambda i: (0, CB_F)), pl.BlockSpec((1, HEAD_DIM), lambda i: (0, 0))],
                 [pl.BlockSpec((t, HEAD_DIM), lambda i: (0, 0))], [jax.ShapeDtypeStruct((t, HEAD_DIM), F32)],
                 [proj, b_pad], sem=("arbitrary",), nbytes=2 * _nbytes((t, HEAD_DIM), F32))[0]


def _causal(block):
    r = lax.broadcasted_iota(jnp.int32, (block, block), 0)
    c = lax.broadcasted_iota(jnp.int32, (block, block), 1)
    return c <= r


def _lanes(x, width):
    return jnp.concatenate([x] * (width // HEAD_DIM), axis=1)


def _chunks(t, blk):
    return [slice(r * blk, (r + 1) * blk) for r in range(t // blk)]


def _lane_of_head(x, h):
    lane = lax.broadcasted_iota(jnp.int32, x.shape, 1)
    return jnp.broadcast_to(jnp.sum(jnp.where(lane == h, x, 0.0), axis=1, keepdims=True), x.shape)


def _causal_t(block):
    r = lax.broadcasted_iota(jnp.int32, (block, block), 0)
    c = lax.broadcasted_iota(jnp.int32, (block, block), 1)
    return r <= c


def _attn_fwd_t(proj, c_pad, c_row, name, comm=()):
    t = proj.shape[0]
    blk = min(ATTN_BLOCK, t)
    nq = t // blk
    scale = HEAD_DIM ** -0.5

    def body(q_ref, k_ref, v_ref, cp_ref, cr_ref, y_ref, e_ref, kb_s, vt_s, cb_s):
        h, i = pl.program_id(0), pl.program_id(1)

        @pl.when(i == 0)
        def _():
            for rows in _chunks(t, blk):
                kb_s[rows, :] = k_ref[rows, :].astype(BF16)
                vt_s[:, rows] = v_ref[rows, :].T.astype(BF16)
                cb_s[rows, :] = _lane_of_head(cp_ref[rows, :], h)

        q = (q_ref[...] * scale).astype(BF16)
        ci = cr_ref[:, pl.ds(pl.multiple_of(i * blk, blk), blk)]

        def step(j, carry, diagonal):
            m, l, acc = carry
            rows = pl.ds(pl.multiple_of(j * blk, blk), blk)
            s = lax.dot_general(kb_s[rows, :], q, _DIMS["nt"], preferred_element_type=F32)
            s = s + ci - _lanes(cb_s[rows, :], blk)
            if diagonal:
                s = jnp.where(_causal_t(blk), s, NEG)
            m_new = jnp.maximum(m, jnp.max(s, axis=0, keepdims=True))
            alpha = jnp.exp(m - m_new)
            p = jnp.exp(s - m_new)
            l = alpha * l + jnp.sum(p, axis=0, keepdims=True)
            acc = alpha * acc + jnp.dot(vt_s[:, rows], p.astype(BF16), preferred_element_type=F32)
            return m_new, l, acc

        init = (jnp.full((1, blk), NEG, F32), jnp.zeros((1, blk), F32), jnp.zeros((HEAD_DIM, blk), F32))
        carry = lax.fori_loop(0, i, lambda j, c: step(j, c, False), init)
        m, l, acc = step(i, carry, True)
        y_ref[...] = (acc / l).T
        e_ref[...] = ci - (m + jnp.log(l))

    nbytes = 3 * _nbytes((t, HEAD_DIM), F32) + 4 * _nbytes((blk, HEAD_DIM), F32) + 4 * _nbytes((blk, blk), F32)
    return _call(body, name, (N_HEADS, nq),
                 [pl.BlockSpec((blk, HEAD_DIM), lambda h, i: (i, CB_QKV + 3 * h)),
                  pl.BlockSpec((t, HEAD_DIM), lambda h, i: (0, CB_QKV + 3 * h + 1)),
                  pl.BlockSpec((t, HEAD_DIM), lambda h, i: (0, CB_QKV + 3 * h + 2)),
                  pl.BlockSpec((t, HEAD_DIM), lambda h, i: (0, 0)),
                  pl.BlockSpec((None, 1, t), lambda h, i: (h, 0, 0))],
                 [pl.BlockSpec((blk, HEAD_DIM), lambda h, i: (i, h)),
                  pl.BlockSpec((None, 1, blk), lambda h, i: (h, 0, i))],
                 [jax.ShapeDtypeStruct((t, D_MODEL), F32), jax.ShapeDtypeStruct((N_HEADS, 1, t), F32)],
                 [proj, proj, proj, c_pad, c_row],
                 scratch=[pltpu.VMEM((t, HEAD_DIM), BF16), pltpu.VMEM((HEAD_DIM, t), BF16),
                          pltpu.VMEM((t, HEAD_DIM), F32)],
                 sem=("parallel", "arbitrary"), nbytes=nbytes, comm=comm)


def _attn_bwd_t(proj, dy, e_row, delta_b, c_pad, dproj, name, deps=()):
    t = proj.shape[0]
    blk = min(ATTN_BLOCK, t)
    nb = t // blk
    scale = HEAD_DIM ** -0.5

    def body(q_ref, k_ref, v_ref, dy_ref, e_ref, dl_ref, cp_ref, dproj_in, dqkv_ref, dck_ref, dcq_ref,
             qb_s, qt_s, dyb_s, dyt_s, dl_s, dqt_acc):
        del dproj_in
        h, j = pl.program_id(0), pl.program_id(1)

        @pl.when(j == 0)
        def _():
            for rows in _chunks(t, blk):
                qs = q_ref[rows, :] * scale
                qb_s[rows, :] = qs.astype(BF16)
                qt_s[:, rows] = qs.T.astype(BF16)
                dyr = dy_ref[rows, :]
                dyb_s[rows, :] = dyr.astype(BF16)
                dyt_s[:, rows] = dyr.T.astype(BF16)
                dl_s[:, rows] = dl_ref[rows, :].T[0:1, :]
            dqt_acc[...] = jnp.zeros_like(dqt_acc)
            dcq_ref[...] = jnp.zeros_like(dcq_ref)

        kf = k_ref[...]
        kb, kt = kf.astype(BF16), kf.T.astype(BF16)
        vb = v_ref[...].astype(BF16)
        cj = _lanes(_lane_of_head(cp_ref[...], h), blk)

        def step(i, carry, diagonal):
            dkt, dvt, dkey = carry
            cols = pl.ds(pl.multiple_of(i * blk, blk), blk)
            s = lax.dot_general(kb, qb_s[cols, :], _DIMS["nt"], preferred_element_type=F32) + e_ref[:, cols] - cj
            if diagonal:
                s = jnp.where(_causal_t(blk), s, NEG)
            p = jnp.exp(s)
            dp = lax.dot_general(vb, dyb_s[cols, :], _DIMS["nt"], preferred_element_type=F32)
            ds = p * (dp - dl_s[:, cols])
            pb, dsb = p.astype(BF16), ds.astype(BF16)
            dvt = dvt + lax.dot_general(dyt_s[:, cols], pb, _DIMS["nt"], preferred_element_type=F32)
            dkt = dkt + lax.dot_general(qt_s[:, cols], dsb, _DIMS["nt"], preferred_element_type=F32)
            dqt_acc[:, cols] += jnp.dot(kt, dsb, preferred_element_type=F32) * scale
            dcq_ref[:, cols] += jnp.sum(ds, axis=0, keepdims=True)
            for k in range(blk // HEAD_DIM):
                dkey = dkey + ds[:, k * HEAD_DIM:(k + 1) * HEAD_DIM]
            return dkt, dvt, dkey

        zeros = jnp.zeros((HEAD_DIM, blk), F32)
        carry = step(j, (zeros, zeros, jnp.zeros((blk, HEAD_DIM), F32)), True)
        dkt, dvt, dkey = lax.fori_loop(j + 1, nb, lambda i, c: step(i, c, False), carry)
        mine = pl.ds(pl.multiple_of(j * blk, blk), blk)
        dqkv_ref[:, :HEAD_DIM] = dqt_acc[:, mine].T.astype(BF16)
        dqkv_ref[:, HEAD_DIM:2 * HEAD_DIM] = dkt.T.astype(BF16)
        dqkv_ref[:, 2 * HEAD_DIM:] = dvt.T.astype(BF16)
        dck_ref[...] = -jnp.sum(dkey.T, axis=0, keepdims=True)

    full = lambda cb: pl.BlockSpec((t, HEAD_DIM), lambda h, j: (0, cb(h)))
    head = lambda h: h
    row = pl.BlockSpec((None, 1, t), lambda h, j: (h, 0, 0))
    nbytes = (8 * _nbytes((t, HEAD_DIM), F32) + 4 * _nbytes((blk, HEAD_DIM), F32) + 6 * _nbytes((blk, blk), F32))
    return _call(body, name, (N_HEADS, nb),
                 [full(lambda h: CB_QKV + 3 * h),
                  pl.BlockSpec((blk, HEAD_DIM), lambda h, j: (j, CB_QKV + 3 * h + 1)),
                  pl.BlockSpec((blk, HEAD_DIM), lambda h, j: (j, CB_QKV + 3 * h + 2)),
                  full(head), row, full(head),
                  pl.BlockSpec((blk, HEAD_DIM), lambda h, j: (j, 0)),
                  pl.BlockSpec(memory_space=pl.ANY)],
                 [pl.BlockSpec((blk, QKV_WIDTH), lambda h, j: (j, (GATED + F_WIDTH) // QKV_WIDTH + h)),
                  pl.BlockSpec((None, 1, blk), lambda h, j: (h, 0, j)),
                  row],
                 [jax.ShapeDtypeStruct(dproj.shape, dproj.dtype), jax.ShapeDtypeStruct((N_HEADS, 1, t), F32),
                  jax.ShapeDtypeStruct((N_HEADS, 1, t), F32)],
                 [proj, proj, proj, dy, e_row, delta_b, c_pad, dproj],
                 scratch=[pltpu.VMEM((t, HEAD_DIM), BF16), pltpu.VMEM((HEAD_DIM, t), BF16),
                          pltpu.VMEM((t, HEAD_DIM), BF16), pltpu.VMEM((HEAD_DIM, t), BF16),
                          pltpu.VMEM((1, t), F32), pltpu.VMEM((HEAD_DIM, t), F32)],
                 sem=("parallel", "arbitrary"), nbytes=nbytes, aliases={7: 0}, deps=deps)


def _forget_bwd_t(proj, b_pad, dc, dproj, name):
    t = proj.shape[0]
    tr = min(ROW_BLOCK, t)
    nb = t // tr
    rev = lambda i: nb - 1 - i

    def body(f_ref, b_ref, dc_ref, dproj_in, df_ref, db_ref, run_ref):
        del dproj_in

        @pl.when(pl.program_id(0) == 0)
        def _():
            run_ref[...] = jnp.zeros_like(run_ref)
            db_ref[...] = jnp.zeros_like(db_ref)

        tri = _tri(False)
        b = b_ref[...]
        lane = lax.broadcasted_iota(jnp.int32, (CHUNK, HEAD_DIM), 1)
        df_ref[...] = jnp.zeros_like(df_ref)
        for n in reversed(range(tr // CHUNK)):
            rows = slice(n * CHUNK, (n + 1) * CHUNK)
            dlogf = _dot_exact(tri, dc_ref[rows, :]) + run_ref[...]
            run_ref[...] = dlogf[0:1, :]
            z = f_ref[rows, :] + b
            e = jnp.exp(-jnp.abs(z))
            sig_neg = jnp.where(z >= 0.0, e, 1.0) / (1.0 + e)
            df = jnp.where(lane < N_HEADS, dlogf * sig_neg, 0.0)
            df_ref[rows, :HEAD_DIM] = df.astype(BF16)
            db_ref[...] += jnp.sum(df, axis=0, keepdims=True)

    return _call(body, name, (nb,),
                 [pl.BlockSpec((tr, HEAD_DIM), lambda i: (rev(i), CB_F)), pl.BlockSpec((1, HEAD_DIM), lambda i: (0, 0)),
                  pl.BlockSpec((tr, HEAD_DIM), lambda i: (rev(i), 0)), pl.BlockSpec(memory_space=pl.ANY)],
                 [pl.BlockSpec((tr, F_WIDTH), lambda i: (rev(i), GATED // F_WIDTH)),
                  pl.BlockSpec((1, HEAD_DIM), lambda i: (0, 0))],
                 [jax.ShapeDtypeStruct(dproj.shape, dproj.dtype), jax.ShapeDtypeStruct((1, HEAD_DIM), F32)],
                 [proj, b_pad, dc, dproj], scratch=[pltpu.VMEM((1, HEAD_DIM), F32)],
                 sem=("arbitrary",), nbytes=4 * _nbytes((tr, HEAD_DIM), F32), aliases={3: 0})


_GELU_K = math.sqrt(2.0 / math.pi)
_GELU_C = 0.044715


def _gelu(x):
    t = jnp.tanh(_GELU_K * (x + _GELU_C * (x * x * x)))
    return 0.5 * x * (1.0 + t), t


def _gelu_grad(x, t):
    return 0.5 * (1.0 + t) + 0.5 * x * (1.0 - t * t) * (_GELU_K * (1.0 + 3.0 * _GELU_C * (x * x)))


def _layernorm_stats(a):
    mu = jnp.mean(a, axis=-1, keepdims=True)
    xc = a - mu
    r = lax.rsqrt(jnp.mean(xc * xc, axis=-1, keepdims=True) + EPS)
    return xc * r, r


def _group(g):
    return slice(g * CHUNK, (g + 1) * CHUNK)


def _gated_cols(k):
    return pl.BlockSpec((CHUNK, D_MODEL), lambda i: (i, k))


def _fixed(shape):
    return pl.BlockSpec(shape, lambda i: (0,) * len(shape))


def _mix_fwd(proj, y_b, w_s, b_cols, g_v, name):
    t = proj.shape[0]

    def body(u_ref, vs_ref, ga_ref, gb_ref, yb_ref, w_ref, b_ref, gv_ref, o_ref):
        a_u, _ = _gelu(u_ref[...])
        a_v, _ = _gelu(vs_ref[...])
        xhat, _ = _layernorm_stats(a_v)
        vn = (xhat * gv_ref[...]).astype(BF16)
        sa, sb = _sigmoid(ga_ref[...]), _sigmoid(gb_ref[...])
        yb = yb_ref[...]
        mask = _causal(CHUNK)
        for g in range(N_GROUPS):
            cols = _group(g)
            w = jnp.where(mask, w_ref[g], 0.0).astype(BF16)
            mixed = jnp.dot(w, vn[:, cols], preferred_element_type=F32) + b_ref[:, g:g + 1]
            o_ref[:, cols] = (sa[:, cols] * (a_u[:, cols] * mixed) + sb[:, cols] * yb[:, cols]).astype(BF16)

    return _call(body, name, (t // CHUNK,),
                 [_gated_cols(0), _gated_cols(1), _gated_cols(2), _gated_cols(3), _row_spec(CHUNK, D_MODEL),
                  _fixed((N_GROUPS, CHUNK, CHUNK)), _fixed((CHUNK, N_GROUPS)), _fixed((1, D_MODEL))],
                 [_row_spec(CHUNK, D_MODEL)], [jax.ShapeDtypeStruct((t, D_MODEL), BF16)],
                 [proj, proj, proj, proj, y_b, w_s, b_cols, g_v], sem=("parallel",),
                 nbytes=16 * _nbytes((CHUNK, D_MODEL), F32))[0]


def _mix_bwd(proj, y_b, dmerged, w_s, b_cols, g_v, sel, name, deps=()):
    t = proj.shape[0]

    def body(u_ref, vs_ref, ga_ref, gb_ref, yb_ref, dm_ref, w_ref, b_ref, gv_ref, sel_ref,
             dg_ref, dyb_ref, delta_ref, dw_ref, dbt_ref, dgv_ref):
        @pl.when(pl.program_id(0) == 0)
        def _():
            dw_ref[...] = jnp.zeros_like(dw_ref)
            dbt_ref[...] = jnp.zeros_like(dbt_ref)
            dgv_ref[...] = jnp.zeros_like(dgv_ref)

        u, vs = u_ref[...], vs_ref[...]
        a_u, t_u = _gelu(u)
        a_v, t_v = _gelu(vs)
        xhat, r = _layernorm_stats(a_v)
        gv = gv_ref[...]
        vn = (xhat * gv).astype(BF16)
        sa, sb = _sigmoid(ga_ref[...]), _sigmoid(gb_ref[...])
        yb, dm = yb_ref[...], dm_ref[...].astype(F32)
        dyb = dm * sb
        dyb_ref[...] = dyb
        dyb_yb = dyb * yb
        dg_ref[:, 3 * D_MODEL:] = (dm * yb * (sb * (1.0 - sb))).astype(BF16)
        dya = dm * sa
        mask = _causal(CHUNK)
        dmixed_parts, dvn_parts = [], []
        for g in range(N_GROUPS):
            cols = _group(g)
            delta_ref[:, cols] = jnp.broadcast_to(jnp.sum(dyb_yb[:, cols], axis=1, keepdims=True), (CHUNK, CHUNK))
            w = jnp.where(mask, w_ref[g], 0.0).astype(BF16)
            mixed = jnp.dot(w, vn[:, cols], preferred_element_type=F32) + b_ref[:, g:g + 1]
            y_a = a_u[:, cols] * mixed
            dg_ref[:, 2 * D_MODEL + g * CHUNK:2 * D_MODEL + (g + 1) * CHUNK] = (
                dm[:, cols] * y_a * (sa[:, cols] * (1.0 - sa[:, cols]))).astype(BF16)
            dg_ref[:, cols] = (dya[:, cols] * mixed * _gelu_grad(u[:, cols], t_u[:, cols])).astype(BF16)
            dmixed = dya[:, cols] * a_u[:, cols]
            dmb = dmixed.astype(BF16)
            dw = lax.dot_general(dmb, vn[:, cols], _DIMS["nt"], preferred_element_type=F32)
            dw_ref[g] += jnp.where(mask, dw, 0.0)
            dvn_parts.append(lax.dot_general(w, dmb, _DIMS["tn"], preferred_element_type=F32))
            dmixed_parts.append(dmixed)
        dmixed_all = jnp.concatenate(dmixed_parts, axis=1)
        dvn = jnp.concatenate(dvn_parts, axis=1)
        dbt_ref[...] += _dot_exact_rhs(dmixed_all, sel_ref[...])
        dgv_ref[...] += jnp.sum(dvn * xhat, axis=0, keepdims=True)
        dxh = dvn * gv
        da_v = r * (dxh - jnp.mean(dxh, axis=-1, keepdims=True)
                    - xhat * jnp.mean(dxh * xhat, axis=-1, keepdims=True))
        dg_ref[:, D_MODEL:2 * D_MODEL] = (da_v * _gelu_grad(vs, t_v)).astype(BF16)

    row = lambda width: _row_spec(CHUNK, width)
    return _call(body, name, (t // CHUNK,),
                 [_gated_cols(0), _gated_cols(1), _gated_cols(2), _gated_cols(3), row(D_MODEL), row(D_MODEL),
                  _fixed((N_GROUPS, CHUNK, CHUNK)), _fixed((CHUNK, N_GROUPS)), _fixed((1, D_MODEL)),
                  _fixed((D_MODEL, HEAD_DIM))],
                 [row(GATED), row(D_MODEL), row(D_MODEL),
                  _fixed((N_GROUPS, CHUNK, CHUNK)), _fixed((CHUNK, HEAD_DIM)), _fixed((1, D_MODEL))],
                 [jax.ShapeDtypeStruct((t, IN_PAD), BF16), jax.ShapeDtypeStruct((t, D_MODEL), F32),
                  jax.ShapeDtypeStruct((t, D_MODEL), F32), jax.ShapeDtypeStruct((N_GROUPS, CHUNK, CHUNK), F32),
                  jax.ShapeDtypeStruct((CHUNK, HEAD_DIM), F32), jax.ShapeDtypeStruct((1, D_MODEL), F32)],
                 [proj, proj, proj, proj, y_b, dmerged, w_s, b_cols, g_v, sel], sem=("arbitrary",),
                 nbytes=40 * _nbytes((CHUNK, D_MODEL), F32), deps=deps)


def _adamw_update(w_ref, g_ref, m_ref, v_ref, d_ref, nm_ref, nv_ref):
    gv = g_ref[...]
    nm = ADAM_B1 * m_ref[...] + (1.0 - ADAM_B1) * gv
    nv = ADAM_B2 * v_ref[...] + (1.0 - ADAM_B2) * (gv * gv)
    m_hat = nm / (1.0 - ADAM_B1 ** ADAM_STEP)
    v_hat = nv / (1.0 - ADAM_B2 ** ADAM_STEP)
    d_ref[...] = -ADAM_LR * (m_hat / (jnp.sqrt(v_hat) + ADAM_EPS) + ADAM_WD * w_ref[...])
    nm_ref[...] = nm
    nv_ref[...] = nv


def _adamw(w, g, m, v, name):
    r, c = w.shape
    tr = _tile(r, 256, unit=8)

    def body(w_ref, g_ref, m_ref, v_ref, d_ref, nm_ref, nv_ref):
        _adamw_update(w_ref, g_ref, m_ref, v_ref, d_ref, nm_ref, nv_ref)

    spec = pl.BlockSpec((tr, c), lambda i: (i, 0))
    shape = jax.ShapeDtypeStruct((r, c), F32)
    return _call(body, name, (r // tr,), [spec] * 4, [spec] * 3, [shape] * 3, [w, g, m, v], sem=("parallel",),
                 nbytes=7 * _nbytes((tr, max(c, 128)), F32))


def _adamw_cols(w, g, m, v, name):
    layers, r, c = w.shape
    tc = 256

    def body(w_ref, g_ref, m_ref, v_ref, d_ref, nm_ref, nv_ref):
        _adamw_update(w_ref, g_ref, m_ref, v_ref, d_ref, nm_ref, nv_ref)

    spec = pl.BlockSpec((None, r, tc), lambda l, j: (l, 0, j))
    shape = jax.ShapeDtypeStruct(w.shape, F32)
    return _call(body, name, (layers, c // tc), [spec] * 4, [spec] * 3, [shape] * 3, [w, g, m, v],
                 sem=("parallel", "parallel"), nbytes=7 * _nbytes((r, tc), F32))


def _pad_rows(a, rows):
    return jnp.pad(a, ((0, rows - a.shape[0]), (0, 0)))


def _unpack_in(gathered):
    i = jnp.arange(16)[None, :, None]
    b = jnp.arange(N_DEV)[:, None, None]
    head = jnp.where(i < b, jnp.roll(gathered[:, IN_STRIDE:IN_STRIDE + 16], 1, axis=0), gathered[:, :16])
    nat = jnp.concatenate([head, gathered[:, 16:IN_STRIDE]], axis=1).reshape(N_DEV * IN_STRIDE, D_MODEL)
    qkv = nat[2 * D_MODEL:5 * D_MODEL].reshape(3, N_HEADS, HEAD_DIM, D_MODEL)
    qkv = jnp.transpose(qkv, (1, 0, 2, 3)).reshape(3 * D_MODEL, D_MODEL)
    f = _pad_rows(gathered[N_DEV - 1, IN_STRIDE:IN_STRIDE + N_HEADS], F_WIDTH)
    return jnp.concatenate([nat[:2 * D_MODEL], nat[5 * D_MODEL:7 * D_MODEL], f, qkv], axis=0)


def _pack_grad_in(dw_in_t):
    qkv = dw_in_t[GATED + F_WIDTH:].reshape(N_HEADS, 3, HEAD_DIM, D_MODEL)
    qkv = jnp.transpose(qkv, (1, 0, 2, 3)).reshape(3 * D_MODEL, D_MODEL)
    return jnp.concatenate([dw_in_t[:2 * D_MODEL], qkv, dw_in_t[2 * D_MODEL:GATED],
                            dw_in_t[GATED:GATED + IN_ROWS - N_DEV * IN_STRIDE]], axis=0)


def kernel(x, mix_pre_g, w_in, b_forget, sgu_norm_g, w_spatial, b_spatial, w_out, mix_post_g, ffn_pre_g, w_gate, w_up, w_down, ffn_post_g, loss_target, m_mix_pre_g, m_w_in, m_b_forget, m_sgu_norm_g, m_w_spatial, m_b_spatial, m_w_out, m_mix_post_g, m_ffn_pre_g, m_w_gate, m_w_up, m_w_down, m_ffn_post_g, v_mix_pre_g, v_w_in, v_b_forget, v_sgu_norm_g, v_w_spatial, v_b_spatial, v_w_out, v_mix_post_g, v_ffn_pre_g, v_w_gate, v_w_up, v_w_down, v_ffn_post_g):
    depth = w_in.shape[0]
    assert depth == 2
    t = x.shape[1]
    x0 = x.reshape(t, D_MODEL)
    target = loss_target.reshape(t, D_MODEL)
    sel = (jnp.arange(D_MODEL)[:, None] // HEAD_DIM == jnp.arange(HEAD_DIM)[None, :]).astype(BF16)
    vec = lambda a, l: a[l][None, :]
    zero_row = jnp.zeros((1, D_MODEL), F32)

    me = 4 * lax.axis_index("x") + 2 * lax.axis_index("y") + lax.axis_index("c")
    p_in = [lax.dynamic_update_slice(jnp.zeros((P_IN, D_MODEL), BF16), w_in[l].T.astype(BF16), (me, 0))
            for l in range(depth)]
    p_rest = [[w_out[l].astype(BF16), jnp.stack([w_gate[l].T, w_up[l].T]).astype(BF16), w_down[l].astype(BF16)]
              for l in range(depth)]
    w_in_t = [None] * depth
    w_rest = [None] * depth
    w_in_t[0] = _unpack_in(_exchange(("gather", 0), p_in[0], "gather_w_in_0")[0])

    saved = []
    xl = x0
    h = _rms_fwd(xl, vec(mix_pre_g, 0), "rms_in_0")
    dy = loss = None
    for l in range(depth):
        b_pad = jnp.pad(b_forget[l], (0, HEAD_DIM - N_HEADS))[None, :]
        b_cols = b_spatial[l].T
        proj = _matmul(h, w_in_t[l], "nt", F32, f"proj_{l}")
        c_pad = _forget_cumsum(proj, b_pad, f"forget_cumsum_{l}")
        c_row = c_pad[:, :N_HEADS].T[:, None, :]
        riders = [(("gather", 0), p_rest[l][0]), (("gather", 1), p_rest[l][1]), (("gather", 0), p_rest[l][2])]
        riders += [(("gather", 0), p_in[l + 1])] if l + 1 < depth else []
        y_b, e_row, *arrived = _attn_fwd_t(proj, c_pad, c_row, f"attn_fwd_{l}", comm=riders)
        w_rest[l] = (arrived[0].reshape(D_MODEL, D_MODEL), arrived[1].reshape(2 * D_FF, D_MODEL),
                     arrived[2].reshape(D_FF, D_MODEL))
        if l + 1 < depth:
            w_in_t[l + 1] = _unpack_in(arrived[3])
        w_o, w_gu_t, w_d = w_rest[l]
        merged = _mix_fwd(proj, y_b, w_spatial[l], b_cols, vec(sgu_norm_g, l), f"mix_fwd_{l}")
        o = _matmul(merged, w_o, "nn", F32, f"out_proj_{l}")
        x1, h2 = _post_norm_fwd(xl, o, vec(mix_post_g, l), vec(ffn_pre_g, l), f"post_mix_{l}")
        gu = _matmul(h2, w_gu_t, "nt", BF16, f"ffn_gu_{l}")
        act = _swiglu_fwd(gu, f"swiglu_fwd_{l}")
        dn = _matmul(act, w_d, "nn", F32, f"ffn_down_{l}")
        saved.append(dict(x0=xl, h=h, proj=proj, b_pad=b_pad, b_cols=b_cols, c_pad=c_pad, y_b=y_b, e_row=e_row,
                          merged=merged, o=o, x1=x1, h2=h2, gu=gu, act=act, dn=dn))
        if l + 1 < depth:
            xl, h = _post_norm_fwd(x1, dn, vec(ffn_post_g, l), vec(mix_pre_g, l + 1), f"post_ffn_{l}")
        else:
            dy, loss = _post_norm_loss(x1, dn, vec(ffn_post_g, l), target, "loss")

    g_small = [None] * depth
    flights = {}
    late_token = None
    dx = dy
    dd, dg_ffn_post = _norm_bwd(dx, f"bwd_ffn_post_{depth - 1}", post=(saved[-1]["dn"], vec(ffn_post_g, depth - 1)))
    for l in reversed(range(depth)):
        s = saved[l]
        w_o, w_gu_t, w_d = w_rest[l]
        dact = _matmul(dd, w_d, "nt", BF16, f"d_act_{l}")
        dw_d = _matmul(s["act"], dd, "tn", BF16, f"dw_down_{l}")
        dgu = _swiglu_bwd(s["gu"], dact, f"swiglu_bwd_{l}")
        dh2 = _matmul(dgu, w_gu_t, "nn", F32, f"d_h2_{l}")
        dw_gu_t = _matmul(dgu, s["h2"], "tn", BF16, f"dw_gu_{l}")
        dx1, dg_ffn_pre, do, dg_mix_post = _norm_bwd(
            dx, f"bwd_post_mix_{l}", pre=(dh2, s["x1"], vec(ffn_pre_g, l)), post=(s["o"], vec(mix_post_g, l)))
        dmerged = _matmul(do, w_o, "nt", F32, f"d_merged_{l}")
        dw_o = _matmul(s["merged"], do, "tn", BF16, f"dw_out_{l}")
        tokens = []
        for key, dw, base, rows in [("out", dw_o, 0, SH_OUT), ("gate", dw_gu_t, 0, SH_FF), ("up", dw_gu_t, D_FF, SH_FF),
                                    ("down", dw_d, 0, SH_FF)]:
            flights[(key, l)], token = _exchange_start(("windows", base, rows, rows), dw, f"start_grad_{key}_{l}")
            tokens.append(token)
        dproj, dyb, delta_b, dw_s, dbt, dg_v = _mix_bwd(
            s["proj"], s["y_b"], dmerged, w_spatial[l], s["b_cols"], vec(sgu_norm_g, l), sel, f"mix_bwd_{l}",
            deps=tokens)
        db_s = dbt[:, :N_GROUPS].T.reshape(1, D_MODEL)
        tokens = []
        if l == 0:
            g_small[0] = jnp.concatenate([zero_row, dg_v, dg_mix_post, dg_ffn_pre, dg_ffn_post, db_s, zero_row,
                                          zero_row, dw_s.reshape(CHUNK, D_MODEL)], axis=0)
            flights["small"], token = _exchange_start(("spread",),jnp.concatenate([g_small[1], g_small[0]], axis=0),
                                                      "start_small_grads")
            tokens = [token]
        dproj, dc_key, dc_query = _attn_bwd_t(s["proj"], dyb, s["e_row"], delta_b, s["c_pad"], dproj,
                                              f"attn_bwd_{l}", deps=tokens)
        dc = jnp.pad((dc_key + dc_query)[:, 0, :].T, ((0, 0), (0, HEAD_DIM - N_HEADS)))
        dproj, db_f = _forget_bwd_t(s["proj"], s["b_pad"], dc, dproj, f"forget_bwd_{l}")
        db_f_row = jnp.pad(db_f, ((0, 0), (0, D_MODEL - HEAD_DIM)))
        dw_in_t = _matmul(dproj, s["h"], "tn", BF16, f"dw_in_{l}")
        flights[("in", l)], token = _exchange_start(("windows", 0, IN_STRIDE, P_IN), _pack_grad_in(dw_in_t),
                                                    f"start_grad_in_{l}")
        dh = _matmul(dproj, w_in_t[l], "nn", F32, f"d_h_{l}", deps=[token])
        if l > 0:
            dx, dg_mix_pre, dd, dg_ffn_post_below = _norm_bwd(
                dx1, f"bwd_pre_mix_{l}", pre=(dh, s["x0"], vec(mix_pre_g, l)),
                post=(saved[l - 1]["dn"], vec(ffn_post_g, l - 1)))
            loss_row = jnp.pad(loss, ((0, 0), (0, D_MODEL - 1)))
            g_small[l] = jnp.concatenate([dg_mix_pre, dg_v, dg_mix_post, dg_ffn_pre, dg_ffn_post, db_s, db_f_row,
                                          loss_row, dw_s.reshape(CHUNK, D_MODEL)], axis=0)
            dg_ffn_post = dg_ffn_post_below
        else:
            dx, dg_mix_pre = _norm_bwd(dx1, f"bwd_pre_mix_{l}", pre=(dh, s["x0"], vec(mix_pre_g, l)))
            late_rows = jnp.concatenate([dg_mix_pre, db_f_row] + [zero_row] * 6, axis=0)
            flights["late"], late_token = _exchange_start(("spread",),late_rows, "start_late_small_grads")
    grad_x = dx.reshape(x.shape)

    landed, after = {}, [grad_x, late_token]
    rest = ["out", "gate", "up", "down"]
    for key in [(n, 1) for n in rest] + [("in", 1)] + [(n, 0) for n in rest] + ["small"]:
        label = key if isinstance(key, str) else f"grad_{key[0]}_{key[1]}"
        landed[key] = _exchange_wait(flights[key], after, f"wait_{label}")
        after = [landed[key]]
    small_sum = _sum_slots(landed["small"], S_ROWS, "sum_small_grads")
    sm = small_sum.reshape(depth, S_ROWS, D_MODEL)[::-1]
    loss = sm[depth - 1, 7, 0]
    g = {
        "sgu_norm_g": sm[:, 1], "mix_post_g": sm[:, 2],
        "ffn_pre_g": sm[:, 3], "ffn_post_g": sm[:, 4], "b_spatial": sm[:, 5].reshape(depth, N_GROUPS, CHUNK),
        "w_spatial": sm[:, 8:].reshape(depth, N_GROUPS, CHUNK, CHUNK),
    }
    for n in rest:
        g["w_" + n] = jnp.stack([_sum_slots(landed[(n, l)], landed[(n, l)].shape[1], f"sum_grad_{n}_{l}")
                                 for l in range(depth)])

    names = ["mix_pre_g", "w_in", "b_forget", "sgu_norm_g", "w_spatial", "b_spatial", "w_out", "mix_post_g",
             "ffn_pre_g", "w_gate", "w_up", "w_down", "ffn_post_g"]
    ws = dict(mix_pre_g=mix_pre_g, w_in=w_in, b_forget=b_forget, sgu_norm_g=sgu_norm_g, w_spatial=w_spatial,
              b_spatial=b_spatial, w_out=w_out, mix_post_g=mix_post_g, ffn_pre_g=ffn_pre_g, w_gate=w_gate, w_up=w_up,
              w_down=w_down, ffn_post_g=ffn_post_g)
    ms = dict(mix_pre_g=m_mix_pre_g, w_in=m_w_in, b_forget=m_b_forget, sgu_norm_g=m_sgu_norm_g, w_spatial=m_w_spatial,
              b_spatial=m_b_spatial, w_out=m_w_out, mix_post_g=m_mix_post_g, ffn_pre_g=m_ffn_pre_g, w_gate=m_w_gate,
              w_up=m_w_up, w_down=m_w_down, ffn_post_g=m_ffn_post_g)
    vs = dict(mix_pre_g=v_mix_pre_g, w_in=v_w_in, b_forget=v_b_forget, sgu_norm_g=v_sgu_norm_g, w_spatial=v_w_spatial,
              b_spatial=v_b_spatial, w_out=v_w_out, mix_post_g=v_mix_post_g, ffn_pre_g=v_ffn_pre_g, w_gate=v_w_gate,
              w_up=v_w_up, w_down=v_w_down, ffn_post_g=v_ffn_post_g)
    updates = {}
    last = ["w_in", "mix_pre_g", "b_forget"]
    transposed = ["w_in", "w_gate", "w_up"]
    for n in [n for n in names if n not in last] + last:
        if n == last[0]:
            sum_in_1 = _sum_slots(landed[("in", 1)], 304, "sum_grad_in_1")
            done = [u[0] for u in updates.values()] + [sum_in_1]
            sum_in_0 = _sum_slots(_exchange_wait(flights[("in", 0)], done, "wait_grad_in_0"), 304, "sum_grad_in_0")
            late = _sum_slots(_exchange_wait(flights["late"], [sum_in_0], "wait_late"), 8, "sum_late_small_grads")
            g["w_in"] = lax.dynamic_slice(jnp.stack([sum_in_0, sum_in_1]), (0, me, 0), (depth, SH_IN, D_MODEL))
            g["mix_pre_g"] = jnp.stack([late[0], sm[1, 0]])
            g["b_forget"] = jnp.stack([late[1, :N_HEADS], sm[1, 6, :N_HEADS]])
        if n in transposed:
            view = lambda a: jnp.swapaxes(a, 1, 2)
            updates[n] = [view(u) for u in _adamw_cols(view(ws[n]), g[n], view(ms[n]), view(vs[n]), f"adamw_{n}")]
            g[n] = view(g[n])
        else:
            flat = (-1, ws[n].shape[-1])
            updates[n] = [u.reshape(ws[n].shape) for u in
                          _adamw(ws[n].reshape(flat), g[n].reshape(flat), ms[n].reshape(flat), vs[n].reshape(flat),
                                 f"adamw_{n}")]
    deltas = [updates[n][0] for n in names]
    new_m = [updates[n][1] for n in names]
    new_v = [updates[n][2] for n in names]
    grads = [g[n].reshape(ws[n].shape) for n in names]
    return (loss, grad_x, *grads, *deltas, *new_m, *new_v)
```

```python
import math

import jax
import jax.numpy as jnp
from jax import lax
from jax.experimental import pallas as pl
from jax.experimental.pallas import tpu as pltpu

F32 = jnp.float32
BF16 = jnp.bfloat16

N_DEV = 8
D_MODEL = 1024
N_HEADS = 8
HEAD_DIM = 128
CHUNK = 128
N_GROUPS = 8
D_FF = 2816
IN_WIDTH = 7 * D_MODEL + N_HEADS
IN_PAD = 7680
EPS = 1e-6
GATED = 4 * D_MODEL
F_WIDTH = 512
QKV_WIDTH = 3 * HEAD_DIM
CB_F = GATED // HEAD_DIM
CB_QKV = (GATED + F_WIDTH) // HEAD_DIM
assert GATED + F_WIDTH + N_HEADS * QKV_WIDTH == IN_PAD and (GATED + F_WIDTH) % QKV_WIDTH == 0

ADAM_LR, ADAM_B1, ADAM_B2, ADAM_EPS, ADAM_WD, ADAM_STEP = 0.001, 0.9, 0.999, 1e-08, 0.01, 10

SH_IN = IN_WIDTH // N_DEV
SH_OUT = D_MODEL // N_DEV
SH_FF = D_FF // N_DEV
P_IN = 912
IN_STRIDE = 7 * HEAD_DIM
IN_ROWS = IN_STRIDE * (N_DEV - 1) + P_IN
S_ROWS = 136

ATTN_BLOCK = 1024
ROW_BLOCK = 512
MIX_ROWS = 512
MIX_BWD_ROWS = 256
MM_TM, MM_TN, MM_TK = 1536, 1536, 2048
VMEM_CAP = 56 << 20
NEG = -1e30


def _tile(n, cap, unit=128):
    if n <= cap:
        return n
    best = None
    for t in range(unit, cap + 1, unit):
        if n % t == 0:
            best = t
    assert best is not None, (n, cap)
    return best


def _nbytes(shape, dtype):
    return math.prod(shape) * jnp.dtype(dtype).itemsize


_HBM = pl.BlockSpec(memory_space=pltpu.HBM)
MESH = pl.DeviceIdType.MESH

_N_REMOTE = N_DEV - 1


def _exchange_shapes(kind, x):
    if kind[0] == "gather":
        shape = x.shape[:kind[1]] + (N_DEV,) + x.shape[kind[1]:]
    elif kind[0] == "spread":
        shape = (N_DEV,) + x.shape
    else:
        shape = (N_DEV, kind[3]) + x.shape[1:]
    return [jax.ShapeDtypeStruct(shape, x.dtype)]


def _exchange_sems(kind):
    del kind
    return [pltpu.SemaphoreType.DMA((_N_REMOTE,)), pltpu.SemaphoreType.DMA((_N_REMOTE,)), pltpu.SemaphoreType.DMA]


def _plan(kind, x_ref, outs, sems):
    send_sems, recv_sems, local_sem = sems
    x, y, c = lax.axis_index("x"), lax.axis_index("y"), lax.axis_index("c")

    def remote(src, dst, k, to):
        return pltpu.make_async_remote_copy(src_ref=src, dst_ref=dst, send_sem=send_sems.at[k], recv_sem=recv_sems.at[k],
                                            device_id=to, device_id_type=MESH)

    sibling = (x, y, 1 - c)
    if kind[0] == "gather":
        out, = outs
        slot = lambda px, py, pc: out.at[(slice(None),) * kind[1] + (4 * px + 2 * py + pc,)]
        chips = [(1 - x, y), (x, 1 - y), (1 - x, 1 - y)]
        local = [pltpu.make_async_copy(x_ref, slot(x, y, c), local_sem)]
        first = [remote(x_ref, slot(x, y, c), 0, sibling)]
        first += [remote(x_ref, slot(x, y, c), 1 + k, (*chip, c)) for k, chip in enumerate(chips)]
        relays = [(remote(x_ref, slot(*chip, c), 1 + k, (*chip, c)), remote(slot(*chip, c), slot(*chip, c), 4 + k, sibling))
                  for k, chip in enumerate(chips)]
        arrivals = [remote(x_ref, slot(x, y, 1 - c), 0, sibling)]
        arrivals += [remote(x_ref, slot(*chip, 1 - c), 4 + k, sibling) for k, chip in enumerate(chips)]
        return local, first, relays, arrivals
    out, = outs
    mine = 4 * x + 2 * y + c
    if kind[0] == "spread":
        src = lambda d: x_ref
    else:
        _, base, stride, length = kind
        src = lambda d: x_ref.at[pl.ds(pl.multiple_of(base + stride * d, 16), length)]
    first, arrivals = [], []
    for j in range(1, N_DEV):
        peer = (1 - x if j & 4 else x, 1 - y if j & 2 else y, 1 - c if j & 1 else c)
        theirs = 4 * peer[0] + 2 * peer[1] + peer[2]
        first.append(remote(src(theirs), out.at[mine], j - 1, peer))
        arrivals.append(remote(src(theirs), out.at[theirs], j - 1, peer))
    local = [] if local_sem is None else [pltpu.make_async_copy(src(mine), out.at[mine], local_sem)]
    return local, first, [], arrivals


def _start(plan):
    local, first, _, _ = plan
    for cp in local + first:
        cp.start()


def _finish(plan):
    local, first, relays, arrivals = plan
    for arrival, onward in relays:
        arrival.wait_recv()
        onward.start()
    for cp in arrivals:
        cp.wait_recv()
    for cp in first + [onward for _, onward in relays]:
        cp.wait_send()
    for cp in local:
        cp.wait()


def _direct_copies(kind, x_ref, land_ref, send_sems, recv_sems):
    _, first, _, arrivals = _plan(kind, x_ref, [land_ref], (send_sems, recv_sems, None))
    return first, arrivals


_SEM = pl.BlockSpec(memory_space=pltpu.SEMAPHORE)
_DATAFLOW = pltpu.SideEffectType.DATAFLOW_SIDE_EFFECTING


def _exchange_start(kind, x, name):
    me = 4 * lax.axis_index("x") + 2 * lax.axis_index("y") + lax.axis_index("c")
    own = x if kind[0] == "spread" else lax.dynamic_slice_in_dim(x, kind[1] + kind[2] * me, kind[3], 0)
    shape = _exchange_shapes(kind, x)[0].shape
    land = lax.dynamic_update_slice_in_dim(lax.empty(shape, x.dtype), own[None], me, 0)

    def body(x_ref, land_ref, send_sems, recv_sems, x_thru, land_thru, token):
        del x_thru, land_thru
        for cp in _direct_copies(kind, x_ref, land_ref, send_sems, recv_sems)[0]:
            cp.start()
        token[...] = jnp.zeros_like(token)

    *handle, token = pl.pallas_call(
        body, name=name,
        out_shape=(pltpu.SemaphoreType.DMA((_N_REMOTE,)), pltpu.SemaphoreType.DMA((_N_REMOTE,)),
                   pltpu.HBM(x.shape, x.dtype), pltpu.HBM(land.shape, land.dtype), jax.ShapeDtypeStruct((8, 128), F32)),
        in_specs=(_HBM, _HBM), out_specs=(_SEM, _SEM, _HBM, _HBM, pl.BlockSpec(memory_space=pltpu.VMEM)),
        input_output_aliases={0: 2, 1: 3}, compiler_params=pltpu.CompilerParams(has_side_effects=_DATAFLOW),
    )(pltpu.with_memory_space_constraint(x, pltpu.HBM), pltpu.with_memory_space_constraint(land, pltpu.HBM))
    return (kind, *handle), token


def _exchange_wait(handle, after, name):
    kind, send_sems, recv_sems, x_thru, land_thru = handle

    def body(x_ref, land_ref, send_sems, recv_sems, *rest):
        first, arrivals = _direct_copies(kind, x_ref, land_ref, send_sems, recv_sems)
        for cp in first:
            cp.wait_send()
        for cp in arrivals:
            cp.wait_recv()

    return pl.pallas_call(
        body, name=name,
        out_shape=(pltpu.HBM(x_thru.shape, x_thru.dtype), pltpu.HBM(land_thru.shape, land_thru.dtype)),
        in_specs=(_HBM, _HBM, _SEM, _SEM) + (pl.BlockSpec(memory_space=pl.ANY),) * len(after), out_specs=(_HBM, _HBM),
        input_output_aliases={0: 0, 1: 1}, compiler_params=pltpu.CompilerParams(has_side_effects=_DATAFLOW),
    )(x_thru, land_thru, send_sems, recv_sems, *after)[1]


def _exchange(kind, x, name):
    n_out = len(_exchange_shapes(kind, x))

    def body(x_ref, *refs):
        plan = _plan(kind, x_ref, refs[:n_out], refs[n_out:])
        _start(plan)
        _finish(plan)

    return pl.pallas_call(body, name=name, out_shape=_exchange_shapes(kind, x), in_specs=[_HBM],
                          out_specs=[_HBM] * n_out, scratch_shapes=_exchange_sems(kind))(x)


def _call(body, name, grid, in_specs, out_specs, out_shape, args, scratch=(), sem=None, nbytes=0, aliases=None,
          comm=(), deps=()):
    in_specs, out_specs, out_shape, args, scratch = (list(in_specs), list(out_specs), list(out_shape), list(args),
                                                     list(scratch))
    if deps:
        n_real, n_deps, unordered = len(args), len(deps), body

        def body(*refs):
            unordered(*refs[:n_real], *refs[n_real + n_deps:])

        in_specs += [pl.BlockSpec(memory_space=pl.ANY)] * n_deps
        args += list(deps)
    if comm:
        n_in, n_out, n_scr, n_ops = len(args), len(out_shape), len(scratch), len(comm)
        kinds = [kind for kind, _ in comm]
        shapes = [_exchange_shapes(kind, x) for kind, x in comm]
        inner = body

        def body(*refs):
            ins, refs = refs[:n_in], refs[n_in:]
            cins, refs = refs[:n_ops], refs[n_ops:]
            outs, refs = refs[:n_out], refs[n_out:]
            couts = []
            for sh in shapes:
                couts.append(refs[:len(sh)])
                refs = refs[len(sh):]
            own_scratch, sems = refs[:n_scr], refs[n_scr:]
            first = last = None
            for axis, size in enumerate(grid):
                at_start, at_end = pl.program_id(axis) == 0, pl.program_id(axis) == size - 1
                first = at_start if first is None else first & at_start
                last = at_end if last is None else last & at_end
            plans = [_plan(kinds[o], cins[o], couts[o], sems[3 * o:3 * o + 3]) for o in range(n_ops)]

            @pl.when(first)
            def _():
                for p in plans:
                    _start(p)

            inner(*ins, *outs, *own_scratch)

            @pl.when(last)
            def _():
                for p in plans:
                    _finish(p)

        in_specs += [_HBM] * n_ops
        args += [x for _, x in comm]
        for kind, sh in zip(kinds, shapes):
            out_shape += sh
            out_specs += [_HBM] * len(sh)
            scratch += _exchange_sems(kind)
        sem = ("arbitrary",) * len(grid)
    limit = int(min(max(2 * nbytes + (8 << 20), 32 << 20), VMEM_CAP))
    return pl.pallas_call(
        body, name=name, grid=grid, in_specs=in_specs, out_specs=out_specs, out_shape=out_shape,
        scratch_shapes=scratch, input_output_aliases=aliases or {},
        compiler_params=pltpu.CompilerParams(dimension_semantics=sem, vmem_limit_bytes=limit),
    )(*args)


def _sum_slots(x, tr, name):
    n, r, c = x.shape

    def body(x_ref, o_ref):
        acc = x_ref[0].astype(F32)
        for d in range(1, n):
            acc = acc + x_ref[d].astype(F32)
        o_ref[...] = acc

    return _call(body, name, (r // tr,), [pl.BlockSpec((n, tr, c), lambda i: (0, i, 0))],
                 [pl.BlockSpec((tr, c), lambda i: (i, 0))], [jax.ShapeDtypeStruct((r, c), F32)], [x],
                 sem=("parallel",), nbytes=_nbytes((n, tr, c), x.dtype) + _nbytes((tr, c), F32))[0]


_DIMS = {"nn": (((1,), (0,)), ((), ())), "nt": (((1,), (1,)), ((), ())), "tn": (((0,), (0,)), ((), ()))}


def _matmul(a, b, mode, out_dtype, name, comm=(), deps=()):
    if mode == "nn":
        (m, k), (k2, n) = a.shape, b.shape
    elif mode == "nt":
        (m, k), (n, k2) = a.shape, b.shape
    else:
        (k, m), (k2, n) = a.shape, b.shape
    assert k == k2, (a.shape, b.shape, mode)
    tm, tn, tk = _tile(m, MM_TM), _tile(n, MM_TN), _tile(k, MM_TK)
    nk = k // tk
    dims = _DIMS[mode]
    a_spec = {"nn": pl.BlockSpec((tm, tk), lambda i, j, kk: (i, kk)),
              "nt": pl.BlockSpec((tm, tk), lambda i, j, kk: (i, kk)),
              "tn": pl.BlockSpec((tk, tm), lambda i, j, kk: (kk, i))}[mode]
    b_spec = {"nn": pl.BlockSpec((tk, tn), lambda i, j, kk: (kk, j)),
              "nt": pl.BlockSpec((tn, tk), lambda i, j, kk: (j, kk)),
              "tn": pl.BlockSpec((tk, tn), lambda i, j, kk: (kk, j))}[mode]

    def partial_product(a_ref, b_ref):
        return lax.dot_general(a_ref[...].astype(BF16), b_ref[...].astype(BF16), dims, preferred_element_type=F32)

    if nk == 1:
        def body(a_ref, b_ref, o_ref):
            o_ref[...] = partial_product(a_ref, b_ref).astype(o_ref.dtype)
        scratch = []
    else:
        def body(a_ref, b_ref, o_ref, acc_ref):
            kk = pl.program_id(2)

            @pl.when(kk == 0)
            def _():
                acc_ref[...] = jnp.zeros_like(acc_ref)

            acc_ref[...] += partial_product(a_ref, b_ref)

            @pl.when(kk == nk - 1)
            def _():
                o_ref[...] = acc_ref[...].astype(o_ref.dtype)
        scratch = [pltpu.VMEM((tm, tn), F32)]

    nbytes = (_nbytes((tm, tk), a.dtype) + _nbytes((tk, tn), b.dtype) + _nbytes((tm, tn), out_dtype)
              + _nbytes((tm, tn), F32))
    res = _call(body, name, (m // tm, n // tn, nk), [a_spec, b_spec],
                [pl.BlockSpec((tm, tn), lambda i, j, kk: (i, j))], [jax.ShapeDtypeStruct((m, n), out_dtype)], [a, b],
                scratch=scratch, sem=("parallel", "parallel", "arbitrary"), nbytes=nbytes, comm=comm, deps=deps)
    return res if comm else res[0]


def _rms(x):
    return lax.rsqrt(jnp.mean(x * x, axis=-1, keepdims=True) + EPS)


def _rms_bwd(dz, a, g):
    r = _rms(a)
    dzg = dz * g
    da = r * dzg - a * (r * r * r) * jnp.mean(dzg * a, axis=-1, keepdims=True)
    return da, dz * (a * r)


def _row_spec(tr, width):
    return pl.BlockSpec((tr, width), lambda i: (i, 0))


def _vec_spec(width):
    return pl.BlockSpec((1, width), lambda i: (0, 0))


def _rms_fwd(x, g, name):
    t, d = x.shape
    tr = min(ROW_BLOCK, t)

    def body(x_ref, g_ref, h_ref):
        xv = x_ref[...]
        h_ref[...] = ((xv * _rms(xv)) * g_ref[...]).astype(BF16)

    return _call(body, name, (t // tr,), [_row_spec(tr, d), _vec_spec(d)], [_row_spec(tr, d)],
                 [jax.ShapeDtypeStruct((t, d), BF16)], [x, g], sem=("parallel",), nbytes=3 * _nbytes((tr, d), F32))[0]


def _post_norm_fwd(x, o, g_post, g_next, name):
    t, d = x.shape
    tr = min(ROW_BLOCK, t)

    def body(x_ref, o_ref, gp_ref, gn_ref, xn_ref, h_ref):
        ov = o_ref[...]
        xn = x_ref[...] + (ov * _rms(ov)) * gp_ref[...]
        xn_ref[...] = xn
        h_ref[...] = ((xn * _rms(xn)) * gn_ref[...]).astype(BF16)

    return _call(body, name, (t // tr,), [_row_spec(tr, d), _row_spec(tr, d), _vec_spec(d), _vec_spec(d)],
                 [_row_spec(tr, d), _row_spec(tr, d)],
                 [jax.ShapeDtypeStruct((t, d), F32), jax.ShapeDtypeStruct((t, d), BF16)], [x, o, g_post, g_next],
                 sem=("parallel",), nbytes=5 * _nbytes((tr, d), F32))


def _post_norm_loss(x, o, g_post, target, name):
    t, d = x.shape
    tr = min(ROW_BLOCK, t)

    def body(x_ref, o_ref, gp_ref, t_ref, dy_ref, loss_ref):
        ov = o_ref[...]
        err = x_ref[...] + (ov * _rms(ov)) * gp_ref[...] - t_ref[...]
        dy_ref[...] = err / d
        part = 0.5 * jnp.sum(jnp.mean(err * err, axis=-1, keepdims=True), axis=0, keepdims=True)

        @pl.when(pl.program_id(0) == 0)
        def _():
            loss_ref[...] = jnp.zeros_like(loss_ref)

        loss_ref[...] += part

    return _call(body, name, (t // tr,), [_row_spec(tr, d), _row_spec(tr, d), _vec_spec(d), _row_spec(tr, d)],
                 [_row_spec(tr, d), pl.BlockSpec((1, 1), lambda i: (0, 0))],
                 [jax.ShapeDtypeStruct((t, d), F32), jax.ShapeDtypeStruct((1, 1), F32)], [x, o, g_post, target],
                 sem=("arbitrary",), nbytes=5 * _nbytes((tr, d), F32))


def _norm_bwd(dskip, name, pre=None, post=None):
    t, d = dskip.shape
    tr = min(ROW_BLOCK, t)
    n_in = 1 + (3 if pre else 0) + (2 if post else 0)

    def body(*refs):
        ins, outs = list(refs[:n_in]), list(refs[n_in:])
        first = pl.program_id(0) == 0
        dx = ins.pop(0)[...]
        if pre:
            dh_ref, xin_ref, gpre_ref = ins.pop(0), ins.pop(0), ins.pop(0)
            dxin, dg_rows = _rms_bwd(dh_ref[...].astype(F32), xin_ref[...], gpre_ref[...])
            dx = dx + dxin
            dx_ref, dgpre_ref = outs.pop(0), outs.pop(0)
            dx_ref[...] = dx

            @pl.when(first)
            def _():
                dgpre_ref[...] = jnp.zeros_like(dgpre_ref)

            dgpre_ref[...] += jnp.sum(dg_rows, axis=0, keepdims=True)
        if post:
            a_ref, gpost_ref = ins.pop(0), ins.pop(0)
            da, dg_rows = _rms_bwd(dx, a_ref[...], gpost_ref[...])
            da_ref, dgpost_ref = outs.pop(0), outs.pop(0)
            da_ref[...] = da.astype(BF16)

            @pl.when(first)
            def _():
                dgpost_ref[...] = jnp.zeros_like(dgpost_ref)

            dgpost_ref[...] += jnp.sum(dg_rows, axis=0, keepdims=True)

    args, in_specs, out_shape, out_specs = [dskip], [_row_spec(tr, d)], [], []
    if pre:
        args += list(pre)
        in_specs += [_row_spec(tr, d), _row_spec(tr, d), _vec_spec(d)]
        out_shape += [jax.ShapeDtypeStruct((t, d), F32), jax.ShapeDtypeStruct((1, d), F32)]
        out_specs += [_row_spec(tr, d), _vec_spec(d)]
    if post:
        args += list(post)
        in_specs += [_row_spec(tr, d), _vec_spec(d)]
        out_shape += [jax.ShapeDtypeStruct((t, d), BF16), jax.ShapeDtypeStruct((1, d), F32)]
        out_specs += [_row_spec(tr, d), _vec_spec(d)]
    return _call(body, name, (t // tr,), in_specs, out_specs, out_shape, args, sem=("arbitrary",),
                 nbytes=7 * _nbytes((tr, d), F32))


def _sigmoid(x):
    return 1.0 / (1.0 + jnp.exp(-x))


def _swiglu_fwd(gu, name):
    t = gu.shape[0]
    tr = min(ROW_BLOCK, t)

    def body(gu_ref, a_ref):
        gate, up = gu_ref[:, :D_FF].astype(F32), gu_ref[:, D_FF:].astype(F32)
        a_ref[...] = ((gate * _sigmoid(gate)) * up).astype(BF16)

    return _call(body, name, (t // tr,), [_row_spec(tr, 2 * D_FF)], [_row_spec(tr, D_FF)],
                 [jax.ShapeDtypeStruct((t, D_FF), BF16)], [gu], sem=("parallel",),
                 nbytes=2 * _nbytes((tr, 2 * D_FF), F32))[0]


def _swiglu_bwd(gu, dact, name):
    t = gu.shape[0]
    tr = min(ROW_BLOCK, t)

    def body(gu_ref, da_ref, o_ref):
        gate, up = gu_ref[:, :D_FF].astype(F32), gu_ref[:, D_FF:].astype(F32)
        da = da_ref[...].astype(F32)
        s = _sigmoid(gate)
        o_ref[:, :D_FF] = (da * up * (s * (1.0 + gate * (1.0 - s)))).astype(BF16)
        o_ref[:, D_FF:] = (da * (gate * s)).astype(BF16)

    return _call(body, name, (t // tr,), [_row_spec(tr, 2 * D_FF), _row_spec(tr, D_FF)], [_row_spec(tr, 2 * D_FF)],
                 [jax.ShapeDtypeStruct((t, 2 * D_FF), BF16)], [gu, dact], sem=("parallel",),
                 nbytes=3 * _nbytes((tr, 2 * D_FF), F32))[0]


def _split3(x):
    hi = x.astype(BF16)
    r1 = x - hi.astype(F32)
    mid = r1.astype(BF16)
    lo = (r1 - mid.astype(F32)).astype(BF16)
    return hi, mid, lo


def _dot_exact(mat01, x):
    hi, mid, lo = _split3(x)
    out = jnp.dot(mat01, hi, preferred_element_type=F32)
    out = out + jnp.dot(mat01, mid, preferred_element_type=F32)
    return out + jnp.dot(mat01, lo, preferred_element_type=F32)


def _dot_exact_rhs(x, mat01):
    hi, mid, lo = _split3(x)
    out = jnp.dot(hi, mat01, preferred_element_type=F32)
    out = out + jnp.dot(mid, mat01, preferred_element_type=F32)
    return out + jnp.dot(lo, mat01, preferred_element_type=F32)


def _tri(lower):
    r = lax.broadcasted_iota(jnp.int32, (CHUNK, CHUNK), 0)
    c = lax.broadcasted_iota(jnp.int32, (CHUNK, CHUNK), 1)
    return jnp.where(r >= c if lower else r <= c, 1.0, 0.0).astype(BF16)


def _log_sigmoid(z):
    return jnp.minimum(z, 0.0) - jnp.log(1.0 + jnp.exp(-jnp.abs(z)))


def _forget_cumsum(proj, b_pad, name):
    t = proj.shape[0]
    nb = t // CHUNK

    def body(f_ref, b_ref, c_ref):
        tri = _tri(True)
        b = b_ref[...]

        def blk(i, carry):
            rows = pl.ds(pl.multiple_of(i * CHUNK, CHUNK), CHUNK)
            cs = _dot_exact(tri, _log_sigmoid(f_ref[rows, :] + b)) + carry
            c_ref[rows, :] = cs
            return cs[CHUNK - 1:CHUNK, :]

        lax.fori_loop(0, nb, blk, jnp.zeros((1, HEAD_DIM), F32))

    return _call(body, name, (1,),
                 [pl.BlockSpec((t, HEAD_DIM), lambda i: (0, CB_F)), pl.BlockSpec((1, HEAD_DIM), lambda i: (0, 0))],
                 [pl.BlockSpec((t, HEAD_DIM), lambda i: (0, 0))], [jax.ShapeDtypeStruct((t, HEAD_DIM), F32)],
                 [proj, b_pad], sem=("arbitrary",), nbytes=2 * _nbytes((t, HEAD_DIM), F32))[0]


def _causal(block):
    r = lax.broadcasted_iota(jnp.int32, (block, block), 0)
    c = lax.broadcasted_iota(jnp.int32, (block, block), 1)
    return c <= r


def _lanes(x, width):
    return jnp.concatenate([x] * (width // HEAD_DIM), axis=1)


def _chunks(t, blk):
    return [slice(r * blk, (r + 1) * blk) for r in range(t // blk)]


def _lane_of_head(x, h):
    lane = lax.broadcasted_iota(jnp.int32, x.shape, 1)
    return jnp.broadcast_to(jnp.sum(jnp.where(lane == h, x, 0.0), axis=1, keepdims=True), x.shape)


def _causal_t(block):
    r = lax.broadcasted_iota(jnp.int32, (block, block), 0)
    c = lax.broadcasted_iota(jnp.int32, (block, block), 1)
    return r <= c


def _attn_fwd_t(proj, c_pad, c_row, name, comm=()):
    t = proj.shape[0]
    blk = min(ATTN_BLOCK, t)
    nq = t // blk
    scale = HEAD_DIM ** -0.5

    def body(q_ref, k_ref, v_ref, cp_ref, cr_ref, y_ref, e_ref, kb_s, vt_s, cb_s):
        h, i = pl.program_id(0), pl.program_id(1)

        @pl.when(i == 0)
        def _():
            for rows in _chunks(t, blk):
                kb_s[rows, :] = k_ref[rows, :].astype(BF16)
                vt_s[:, rows] = v_ref[rows, :].T.astype(BF16)
                cb_s[rows, :] = _lane_of_head(cp_ref[rows, :], h)

        q = (q_ref[...] * scale).astype(BF16)
        ci = cr_ref[:, pl.ds(pl.multiple_of(i * blk, blk), blk)]

        def step(j, carry, diagonal):
            m, l, acc = carry
            rows = pl.ds(pl.multiple_of(j * blk, blk), blk)
            s = lax.dot_general(kb_s[rows, :], q, _DIMS["nt"], preferred_element_type=F32)
            s = s + ci - _lanes(cb_s[rows, :], blk)
            if diagonal:
                s = jnp.where(_causal_t(blk), s, NEG)
            m_new = jnp.maximum(m, jnp.max(s, axis=0, keepdims=True))
            alpha = jnp.exp(m - m_new)
            p = jnp.exp(s - m_new)
            l = alpha * l + jnp.sum(p, axis=0, keepdims=True)
            acc = alpha * acc + jnp.dot(vt_s[:, rows], p.astype(BF16), preferred_element_type=F32)
            return m_new, l, acc

        init = (jnp.full((1, blk), NEG, F32), jnp.zeros((1, blk), F32), jnp.zeros((HEAD_DIM, blk), F32))
        carry = lax.fori_loop(0, i, lambda j, c: step(j, c, False), init)
        m, l, acc = step(i, carry, True)
        y_ref[...] = (acc / l).T
        e_ref[...] = ci - (m + jnp.log(l))

    nbytes = 3 * _nbytes((t, HEAD_DIM), F32) + 4 * _nbytes((blk, HEAD_DIM), F32) + 4 * _nbytes((blk, blk), F32)
    return _call(body, name, (N_HEADS, nq),
                 [pl.BlockSpec((blk, HEAD_DIM), lambda h, i: (i, CB_QKV + 3 * h)),
                  pl.BlockSpec((t, HEAD_DIM), lambda h, i: (0, CB_QKV + 3 * h + 1)),
                  pl.BlockSpec((t, HEAD_DIM), lambda h, i: (0, CB_QKV + 3 * h + 2)),
                  pl.BlockSpec((t, HEAD_DIM), lambda h, i: (0, 0)),
                  pl.BlockSpec((None, 1, t), lambda h, i: (h, 0, 0))],
                 [pl.BlockSpec((blk, HEAD_DIM), lambda h, i: (i, h)),
                  pl.BlockSpec((None, 1, blk), lambda h, i: (h, 0, i))],
                 [jax.ShapeDtypeStruct((t, D_MODEL), F32), jax.ShapeDtypeStruct((N_HEADS, 1, t), F32)],
                 [proj, proj, proj, c_pad, c_row],
                 scratch=[pltpu.VMEM((t, HEAD_DIM), BF16), pltpu.VMEM((HEAD_DIM, t), BF16),
                          pltpu.VMEM((t, HEAD_DIM), F32)],
                 sem=("parallel", "arbitrary"), nbytes=nbytes, comm=comm)


def _attn_bwd_t(proj, dy, e_row, delta_b, c_pad, dproj, name, deps=()):
    t = proj.shape[0]
    blk = min(ATTN_BLOCK, t)
    nb = t // blk
    scale = HEAD_DIM ** -0.5

    def body(q_ref, k_ref, v_ref, dy_ref, e_ref, dl_ref, cp_ref, dproj_in, dqkv_ref, dck_ref, dcq_ref,
             qb_s, qt_s, dyb_s, dyt_s, dl_s, dqt_acc):
        del dproj_in
        h, j = pl.program_id(0), pl.program_id(1)

        @pl.when(j == 0)
        def _():
            for rows in _chunks(t, blk):
                qs = q_ref[rows, :] * scale
                qb_s[rows, :] = qs.astype(BF16)
                qt_s[:, rows] = qs.T.astype(BF16)
                dyr = dy_ref[rows, :]
                dyb_s[rows, :] = dyr.astype(BF16)
                dyt_s[:, rows] = dyr.T.astype(BF16)
                dl_s[:, rows] = dl_ref[rows, :].T[0:1, :]
            dqt_acc[...] = jnp.zeros_like(dqt_acc)
            dcq_ref[...] = jnp.zeros_like(dcq_ref)

        kf = k_ref[...]
        kb, kt = kf.astype(BF16), kf.T.astype(BF16)
        vb = v_ref[...].astype(BF16)
        cj = _lanes(_lane_of_head(cp_ref[...], h), blk)

        def step(i, carry, diagonal):
            dkt, dvt, dkey = carry
            cols = pl.ds(pl.multiple_of(i * blk, blk), blk)
            s = lax.dot_general(kb, qb_s[cols, :], _DIMS["nt"], preferred_element_type=F32) + e_ref[:, cols] - cj
            if diagonal:
                s = jnp.where(_causal_t(blk), s, NEG)
            p = jnp.exp(s)
            dp = lax.dot_general(vb, dyb_s[cols, :], _DIMS["nt"], preferred_element_type=F32)
            ds = p * (dp - dl_s[:, cols])
            pb, dsb = p.astype(BF16), ds.astype(BF16)
            dvt = dvt + lax.dot_general(dyt_s[:, cols], pb, _DIMS["nt"], preferred_element_type=F32)
            dkt = dkt + lax.dot_general(qt_s[:, cols], dsb, _DIMS["nt"], preferred_element_type=F32)
            dqt_acc[:, cols] += jnp.dot(kt, dsb, preferred_element_type=F32) * scale
            dcq_ref[:, cols] += jnp.sum(ds, axis=0, keepdims=True)
            for k in range(blk // HEAD_DIM):
                dkey = dkey + ds[:, k * HEAD_DIM:(k + 1) * HEAD_DIM]
            return dkt, dvt, dkey

        zeros = jnp.zeros((HEAD_DIM, blk), F32)
        carry = step(j, (zeros, zeros, jnp.zeros((blk, HEAD_DIM), F32)), True)
        dkt, dvt, dkey = lax.fori_loop(j + 1, nb, lambda i, c: step(i, c, False), carry)
        mine = pl.ds(pl.multiple_of(j * blk, blk), blk)
        dqkv_ref[:, :HEAD_DIM] = dqt_acc[:, mine].T.astype(BF16)
        dqkv_ref[:, HEAD_DIM:2 * HEAD_DIM] = dkt.T.astype(BF16)
        dqkv_ref[:, 2 * HEAD_DIM:] = dvt.T.astype(BF16)
        dck_ref[...] = -jnp.sum(dkey.T, axis=0, keepdims=True)

    full = lambda cb: pl.BlockSpec((t, HEAD_DIM), lambda h, j: (0, cb(h)))
    head = lambda h: h
    row = pl.BlockSpec((None, 1, t), lambda h, j: (h, 0, 0))
    nbytes = (8 * _nbytes((t, HEAD_DIM), F32) + 4 * _nbytes((blk, HEAD_DIM), F32) + 6 * _nbytes((blk, blk), F32))
    return _call(body, name, (N_HEADS, nb),
                 [full(lambda h: CB_QKV + 3 * h),
                  pl.BlockSpec((blk, HEAD_DIM), lambda h, j: (j, CB_QKV + 3 * h + 1)),
                  pl.BlockSpec((blk, HEAD_DIM), lambda h, j: (j, CB_QKV + 3 * h + 2)),
                  full(head), row, full(head),
                  pl.BlockSpec((blk, HEAD_DIM), lambda h, j: (j, 0)),
                  pl.BlockSpec(memory_space=pl.ANY)],
                 [pl.BlockSpec((blk, QKV_WIDTH), lambda h, j: (j, (GATED + F_WIDTH) // QKV_WIDTH + h)),
                  pl.BlockSpec((None, 1, blk), lambda h, j: (h, 0, j)),
                  row],
                 [jax.ShapeDtypeStruct(dproj.shape, dproj.dtype), jax.ShapeDtypeStruct((N_HEADS, 1, t), F32),
                  jax.ShapeDtypeStruct((N_HEADS, 1, t), F32)],
                 [proj, proj, proj, dy, e_row, delta_b, c_pad, dproj],
                 scratch=[pltpu.VMEM((t, HEAD_DIM), BF16), pltpu.VMEM((HEAD_DIM, t), BF16),
                          pltpu.VMEM((t, HEAD_DIM), BF16), pltpu.VMEM((HEAD_DIM, t), BF16),
                          pltpu.VMEM((1, t), F32), pltpu.VMEM((HEAD_DIM, t), F32)],
                 sem=("parallel", "arbitrary"), nbytes=nbytes, aliases={7: 0}, deps=deps)


def _forget_bwd_t(proj, b_pad, dc, dproj, name):
    t = proj.shape[0]
    tr = min(ROW_BLOCK, t)
    nb = t // tr
    rev = lambda i: nb - 1 - i

    def body(f_ref, b_ref, dc_ref, dproj_in, df_ref, db_ref, run_ref):
        del dproj_in

        @pl.when(pl.program_id(0) == 0)
        def _():
            run_ref[...] = jnp.zeros_like(run_ref)
            db_ref[...] = jnp.zeros_like(db_ref)

        tri = _tri(False)
        b = b_ref[...]
        lane = lax.broadcasted_iota(jnp.int32, (CHUNK, HEAD_DIM), 1)
        df_ref[...] = jnp.zeros_like(df_ref)
        for n in reversed(range(tr // CHUNK)):
            rows = slice(n * CHUNK, (n + 1) * CHUNK)
            dlogf = _dot_exact(tri, dc_ref[rows, :]) + run_ref[...]
            run_ref[...] = dlogf[0:1, :]
            z = f_ref[rows, :] + b
            e = jnp.exp(-jnp.abs(z))
            sig_neg = jnp.where(z >= 0.0, e, 1.0) / (1.0 + e)
            df = jnp.where(lane < N_HEADS, dlogf * sig_neg, 0.0)
            df_ref[rows, :HEAD_DIM] = df.astype(BF16)
            db_ref[...] += jnp.sum(df, axis=0, keepdims=True)

    return _call(body, name, (nb,),
                 [pl.BlockSpec((tr, HEAD_DIM), lambda i: (rev(i), CB_F)), pl.BlockSpec((1, HEAD_DIM), lambda i: (0, 0)),
                  pl.BlockSpec((tr, HEAD_DIM), lambda i: (rev(i), 0)), pl.BlockSpec(memory_space=pl.ANY)],
                 [pl.BlockSpec((tr, F_WIDTH), lambda i: (rev(i), GATED // F_WIDTH)),
                  pl.BlockSpec((1, HEAD_DIM), lambda i: (0, 0))],
                 [jax.ShapeDtypeStruct(dproj.shape, dproj.dtype), jax.ShapeDtypeStruct((1, HEAD_DIM), F32)],
                 [proj, b_pad, dc, dproj], scratch=[pltpu.VMEM((1, HEAD_DIM), F32)],
                 sem=("arbitrary",), nbytes=4 * _nbytes((tr, HEAD_DIM), F32), aliases={3: 0})


_GELU_K = math.sqrt(2.0 / math.pi)
_GELU_C = 0.044715


def _gelu(x):
    t = jnp.tanh(_GELU_K * (x + _GELU_C * (x * x * x)))
    return 0.5 * x * (1.0 + t), t


def _gelu_grad(x, t):
    return 0.5 * (1.0 + t) + 0.5 * x * (1.0 - t * t) * (_GELU_K * (1.0 + 3.0 * _GELU_C * (x * x)))


def _layernorm_stats(a):
    mu = jnp.mean(a, axis=-1, keepdims=True)
    xc = a - mu
    r = lax.rsqrt(jnp.mean(xc * xc, axis=-1, keepdims=True) + EPS)
    return xc * r, r


def _group(g):
    return slice(g * CHUNK, (g + 1) * CHUNK)


def _fixed(shape):
    return pl.BlockSpec(shape, lambda i: (0,) * len(shape))


def _mix_fwd(proj, y_b, w_s, b_cols, g_v, name):
    t = proj.shape[0]
    step_rows = min(MIX_ROWS, t)

    def body(u_ref, vs_ref, ga_ref, gb_ref, yb_ref, w_ref, b_ref, gv_ref, o_ref):
        mask = _causal(CHUNK)
        w = [jnp.where(mask, w_ref[g], 0.0).astype(BF16) for g in range(N_GROUPS)]
        for rows in _chunks(step_rows, CHUNK):
            a_u, _ = _gelu(u_ref[rows, :])
            a_v, _ = _gelu(vs_ref[rows, :])
            xhat, _ = _layernorm_stats(a_v)
            vn = (xhat * gv_ref[...]).astype(BF16)
            sa, sb = _sigmoid(ga_ref[rows, :]), _sigmoid(gb_ref[rows, :])
            yb = yb_ref[rows, :]
            for g in range(N_GROUPS):
                cols = _group(g)
                mixed = jnp.dot(w[g], vn[:, cols], preferred_element_type=F32) + b_ref[:, g:g + 1]
                o_ref[rows, cols] = (sa[:, cols] * (a_u[:, cols] * mixed) + sb[:, cols] * yb[:, cols]).astype(BF16)

    gated = lambda k: pl.BlockSpec((step_rows, D_MODEL), lambda i: (i, k))
    return _call(body, name, (t // step_rows,),
                 [gated(0), gated(1), gated(2), gated(3), _row_spec(step_rows, D_MODEL),
                  _fixed((N_GROUPS, CHUNK, CHUNK)), _fixed((CHUNK, N_GROUPS)), _fixed((1, D_MODEL))],
                 [_row_spec(step_rows, D_MODEL)], [jax.ShapeDtypeStruct((t, D_MODEL), BF16)],
                 [proj, proj, proj, proj, y_b, w_s, b_cols, g_v], sem=("parallel",),
                 nbytes=16 * _nbytes((step_rows, D_MODEL), F32))[0]


def _mix_bwd(proj, y_b, dmerged, w_s, b_cols, g_v, sel, name, deps=()):
    t = proj.shape[0]
    step_rows = min(MIX_BWD_ROWS, t)

    def body(u_ref, vs_ref, ga_ref, gb_ref, yb_ref, dm_ref, w_ref, b_ref, gv_ref, sel_ref,
             dg_ref, dyb_ref, delta_ref, dw_ref, dbt_ref, dgv_ref):
        @pl.when(pl.program_id(0) == 0)
        def _():
            dw_ref[...] = jnp.zeros_like(dw_ref)
            dbt_ref[...] = jnp.zeros_like(dbt_ref)
            dgv_ref[...] = jnp.zeros_like(dgv_ref)

        gv = gv_ref[...]
        mask = _causal(CHUNK)
        w = [jnp.where(mask, w_ref[g], 0.0).astype(BF16) for g in range(N_GROUPS)]
        for rows in _chunks(step_rows, CHUNK):
            u, vs = u_ref[rows, :], vs_ref[rows, :]
            a_u, t_u = _gelu(u)
            a_v, t_v = _gelu(vs)
            xhat, r = _layernorm_stats(a_v)
            vn = (xhat * gv).astype(BF16)
            sa, sb = _sigmoid(ga_ref[rows, :]), _sigmoid(gb_ref[rows, :])
            yb, dm = yb_ref[rows, :], dm_ref[rows, :].astype(F32)
            dyb = dm * sb
            dyb_ref[rows, :] = dyb
            dyb_yb = dyb * yb
            dg_ref[rows, 3 * D_MODEL:] = (dm * yb * (sb * (1.0 - sb))).astype(BF16)
            dya = dm * sa
            dmixed_parts, dvn_parts = [], []
            for g in range(N_GROUPS):
                cols = _group(g)
                delta_ref[rows, cols] = jnp.broadcast_to(jnp.sum(dyb_yb[:, cols], axis=1, keepdims=True),
                                                         (CHUNK, CHUNK))
                mixed = jnp.dot(w[g], vn[:, cols], preferred_element_type=F32) + b_ref[:, g:g + 1]
                y_a = a_u[:, cols] * mixed
                dg_ref[rows, 2 * D_MODEL + g * CHUNK:2 * D_MODEL + (g + 1) * CHUNK] = (
                    dm[:, cols] * y_a * (sa[:, cols] * (1.0 - sa[:, cols]))).astype(BF16)
                dg_ref[rows, cols] = (dya[:, cols] * mixed * _gelu_grad(u[:, cols], t_u[:, cols])).astype(BF16)
                dmixed = dya[:, cols] * a_u[:, cols]
                dmb = dmixed.astype(BF16)
                dw = lax.dot_general(dmb, vn[:, cols], _DIMS["nt"], preferred_element_type=F32)
                dw_ref[g] += jnp.where(mask, dw, 0.0)
                dvn_parts.append(lax.dot_general(w[g], dmb, _DIMS["tn"], preferred_element_type=F32))
                dmixed_parts.append(dmixed)
            dmixed_all = jnp.concatenate(dmixed_parts, axis=1)
            dvn = jnp.concatenate(dvn_parts, axis=1)
            dbt_ref[...] += _dot_exact_rhs(dmixed_all, sel_ref[...])
            dgv_ref[...] += jnp.sum(dvn * xhat, axis=0, keepdims=True)
            dxh = dvn * gv
            da_v = r * (dxh - jnp.mean(dxh, axis=-1, keepdims=True)
                        - xhat * jnp.mean(dxh * xhat, axis=-1, keepdims=True))
            dg_ref[rows, D_MODEL:2 * D_MODEL] = (da_v * _gelu_grad(vs, t_v)).astype(BF16)

    row = lambda width: _row_spec(step_rows, width)
    gated = lambda k: pl.BlockSpec((step_rows, D_MODEL), lambda i: (i, k))
    return _call(body, name, (t // step_rows,),
                 [gated(0), gated(1), gated(2), gated(3), row(D_MODEL), row(D_MODEL),
                  _fixed((N_GROUPS, CHUNK, CHUNK)), _fixed((CHUNK, N_GROUPS)), _fixed((1, D_MODEL)),
                  _fixed((D_MODEL, HEAD_DIM))],
                 [row(GATED), row(D_MODEL), row(D_MODEL),
                  _fixed((N_GROUPS, CHUNK, CHUNK)), _fixed((CHUNK, HEAD_DIM)), _fixed((1, D_MODEL))],
                 [jax.ShapeDtypeStruct((t, IN_PAD), BF16), jax.ShapeDtypeStruct((t, D_MODEL), F32),
                  jax.ShapeDtypeStruct((t, D_MODEL), F32), jax.ShapeDtypeStruct((N_GROUPS, CHUNK, CHUNK), F32),
                  jax.ShapeDtypeStruct((CHUNK, HEAD_DIM), F32), jax.ShapeDtypeStruct((1, D_MODEL), F32)],
                 [proj, proj, proj, proj, y_b, dmerged, w_s, b_cols, g_v, sel], sem=("arbitrary",),
                 nbytes=24 * _nbytes((step_rows, D_MODEL), F32), deps=deps)


def _adamw_update(w_ref, g_ref, m_ref, v_ref, d_ref, nm_ref, nv_ref):
    gv = g_ref[...]
    nm = ADAM_B1 * m_ref[...] + (1.0 - ADAM_B1) * gv
    nv = ADAM_B2 * v_ref[...] + (1.0 - ADAM_B2) * (gv * gv)
    m_hat = nm / (1.0 - ADAM_B1 ** ADAM_STEP)
    v_hat = nv / (1.0 - ADAM_B2 ** ADAM_STEP)
    d_ref[...] = -ADAM_LR * (m_hat / (jnp.sqrt(v_hat) + ADAM_EPS) + ADAM_WD * w_ref[...])
    nm_ref[...] = nm
    nv_ref[...] = nv


def _adamw(w, g, m, v, name):
    r, c = w.shape
    tr = _tile(r, 256, unit=8)

    def body(w_ref, g_ref, m_ref, v_ref, d_ref, nm_ref, nv_ref):
        _adamw_update(w_ref, g_ref, m_ref, v_ref, d_ref, nm_ref, nv_ref)

    spec = pl.BlockSpec((tr, c), lambda i: (i, 0))
    shape = jax.ShapeDtypeStruct((r, c), F32)
    return _call(body, name, (r // tr,), [spec] * 4, [spec] * 3, [shape] * 3, [w, g, m, v], sem=("parallel",),
                 nbytes=7 * _nbytes((tr, max(c, 128)), F32))


def _adamw_cols(w, g, m, v, name):
    layers, r, c = w.shape
    tc = 256

    def body(w_ref, g_ref, m_ref, v_ref, d_ref, nm_ref, nv_ref):
        _adamw_update(w_ref, g_ref, m_ref, v_ref, d_ref, nm_ref, nv_ref)

    spec = pl.BlockSpec((None, r, tc), lambda l, j: (l, 0, j))
    shape = jax.ShapeDtypeStruct(w.shape, F32)
    return _call(body, name, (layers, c // tc), [spec] * 4, [spec] * 3, [shape] * 3, [w, g, m, v],
                 sem=("parallel", "parallel"), nbytes=7 * _nbytes((r, tc), F32))


def _pad_rows(a, rows):
    return jnp.pad(a, ((0, rows - a.shape[0]), (0, 0)))


def _unpack_in(gathered):
    i = jnp.arange(16)[None, :, None]
    b = jnp.arange(N_DEV)[:, None, None]
    head = jnp.where(i < b, jnp.roll(gathered[:, IN_STRIDE:IN_STRIDE + 16], 1, axis=0), gathered[:, :16])
    nat = jnp.concatenate([head, gathered[:, 16:IN_STRIDE]], axis=1).reshape(N_DEV * IN_STRIDE, D_MODEL)
    qkv = nat[2 * D_MODEL:5 * D_MODEL].reshape(3, N_HEADS, HEAD_DIM, D_MODEL)
    qkv = jnp.transpose(qkv, (1, 0, 2, 3)).reshape(3 * D_MODEL, D_MODEL)
    f = _pad_rows(gathered[N_DEV - 1, IN_STRIDE:IN_STRIDE + N_HEADS], F_WIDTH)
    return jnp.concatenate([nat[:2 * D_MODEL], nat[5 * D_MODEL:7 * D_MODEL], f, qkv], axis=0)


def _pack_grad_in(dw_in_t):
    qkv = dw_in_t[GATED + F_WIDTH:].reshape(N_HEADS, 3, HEAD_DIM, D_MODEL)
    qkv = jnp.transpose(qkv, (1, 0, 2, 3)).reshape(3 * D_MODEL, D_MODEL)
    return jnp.concatenate([dw_in_t[:2 * D_MODEL], qkv, dw_in_t[2 * D_MODEL:GATED],
                            dw_in_t[GATED:GATED + IN_ROWS - N_DEV * IN_STRIDE]], axis=0)


def kernel(x, mix_pre_g, w_in, b_forget, sgu_norm_g, w_spatial, b_spatial, w_out, mix_post_g, ffn_pre_g, w_gate, w_up, w_down, ffn_post_g, loss_target, m_mix_pre_g, m_w_in, m_b_forget, m_sgu_norm_g, m_w_spatial, m_b_spatial, m_w_out, m_mix_post_g, m_ffn_pre_g, m_w_gate, m_w_up, m_w_down, m_ffn_post_g, v_mix_pre_g, v_w_in, v_b_forget, v_sgu_norm_g, v_w_spatial, v_b_spatial, v_w_out, v_mix_post_g, v_ffn_pre_g, v_w_gate, v_w_up, v_w_down, v_ffn_post_g):
    depth = w_in.shape[0]
    assert depth == 2
    t = x.shape[1]
    x0 = x.reshape(t, D_MODEL)
    target = loss_target.reshape(t, D_MODEL)
    sel = (jnp.arange(D_MODEL)[:, None] // HEAD_DIM == jnp.arange(HEAD_DIM)[None, :]).astype(BF16)
    vec = lambda a, l: a[l][None, :]
    zero_row = jnp.zeros((1, D_MODEL), F32)

    me = 4 * lax.axis_index("x") + 2 * lax.axis_index("y") + lax.axis_index("c")
    p_in = [lax.dynamic_update_slice(jnp.zeros((P_IN, D_MODEL), BF16), w_in[l].T.astype(BF16), (me, 0))
            for l in range(depth)]
    p_rest = [[w_out[l].astype(BF16), jnp.stack([w_gate[l].T, w_up[l].T]).astype(BF16), w_down[l].astype(BF16)]
              for l in range(depth)]
    w_in_t = [None] * depth
    w_rest = [None] * depth
    w_in_t[0] = _unpack_in(_exchange(("gather", 0), p_in[0], "gather_w_in_0")[0])

    saved = []
    xl = x0
    h = _rms_fwd(xl, vec(mix_pre_g, 0), "rms_in_0")
    dy = loss = None
    for l in range(depth):
        b_pad = jnp.pad(b_forget[l], (0, HEAD_DIM - N_HEADS))[None, :]
        b_cols = b_spatial[l].T
        proj = _matmul(h, w_in_t[l], "nt", F32, f"proj_{l}")
        c_pad = _forget_cumsum(proj, b_pad, f"forget_cumsum_{l}")
        c_row = c_pad[:, :N_HEADS].T[:, None, :]
        riders = [(("gather", 0), p_rest[l][0]), (("gather", 1), p_rest[l][1]), (("gather", 0), p_rest[l][2])]
        riders += [(("gather", 0), p_in[l + 1])] if l + 1 < depth else []
        y_b, e_row, *arrived = _attn_fwd_t(proj, c_pad, c_row, f"attn_fwd_{l}", comm=riders)
        w_rest[l] = (arrived[0].reshape(D_MODEL, D_MODEL), arrived[1].reshape(2 * D_FF, D_MODEL),
                     arrived[2].reshape(D_FF, D_MODEL))
        if l + 1 < depth:
            w_in_t[l + 1] = _unpack_in(arrived[3])
        w_o, w_gu_t, w_d = w_rest[l]
        merged = _mix_fwd(proj, y_b, w_spatial[l], b_cols, vec(sgu_norm_g, l), f"mix_fwd_{l}")
        o = _matmul(merged, w_o, "nn", F32, f"out_proj_{l}")
        x1, h2 = _post_norm_fwd(xl, o, vec(mix_post_g, l), vec(ffn_pre_g, l), f"post_mix_{l}")
        gu = _matmul(h2, w_gu_t, "nt", BF16, f"ffn_gu_{l}")
        act = _swiglu_fwd(gu, f"swiglu_fwd_{l}")
        dn = _matmul(act, w_d, "nn", F32, f"ffn_down_{l}")
        saved.append(dict(x0=xl, h=h, proj=proj, b_pad=b_pad, b_cols=b_cols, c_pad=c_pad, y_b=y_b, e_row=e_row,
                          merged=merged, o=o, x1=x1, h2=h2, gu=gu, act=act, dn=dn))
        if l + 1 < depth:
            xl, h = _post_norm_fwd(x1, dn, vec(ffn_post_g, l), vec(mix_pre_g, l + 1), f"post_ffn_{l}")
        else:
            dy, loss = _post_norm_loss(x1, dn, vec(ffn_post_g, l), target, "loss")

    g_small = [None] * depth
    flights = {}
    late_token = None
    dx = dy
    dd, dg_ffn_post = _norm_bwd(dx, f"bwd_ffn_post_{depth - 1}", post=(saved[-1]["dn"], vec(ffn_post_g, depth - 1)))
    for l in reversed(range(depth)):
        s = saved[l]
        w_o, w_gu_t, w_d = w_rest[l]
        dact = _matmul(dd, w_d, "nt", BF16, f"d_act_{l}")
        dw_d = _matmul(s["act"], dd, "tn", BF16, f"dw_down_{l}")
        dgu = _swiglu_bwd(s["gu"], dact, f"swiglu_bwd_{l}")
        dh2 = _matmul(dgu, w_gu_t, "nn", F32, f"d_h2_{l}")
        dw_gu_t = _matmul(dgu, s["h2"], "tn", BF16, f"dw_gu_{l}")
        dx1, dg_ffn_pre, do, dg_mix_post = _norm_bwd(
            dx, f"bwd_post_mix_{l}", pre=(dh2, s["x1"], vec(ffn_pre_g, l)), post=(s["o"], vec(mix_post_g, l)))
        dmerged = _matmul(do, w_o, "nt", F32, f"d_merged_{l}")
        dw_o = _matmul(s["merged"], do, "tn", BF16, f"dw_out_{l}")
        tokens = []
        for key, dw, base, rows in [("out", dw_o, 0, SH_OUT), ("gate", dw_gu_t, 0, SH_FF), ("up", dw_gu_t, D_FF, SH_FF),
                                    ("down", dw_d, 0, SH_FF)]:
            flights[(key, l)], token = _exchange_start(("windows", base, rows, rows), dw, f"start_grad_{key}_{l}")
            tokens.append(token)
        dproj, dyb, delta_b, dw_s, dbt, dg_v = _mix_bwd(
            s["proj"], s["y_b"], dmerged, w_spatial[l], s["b_cols"], vec(sgu_norm_g, l), sel, f"mix_bwd_{l}",
            deps=tokens)
        db_s = dbt[:, :N_GROUPS].T.reshape(1, D_MODEL)
        tokens = []
        if l == 0:
            g_small[0] = jnp.concatenate([zero_row, dg_v, dg_mix_post, dg_ffn_pre, dg_ffn_post, db_s, zero_row,
                                          zero_row, dw_s.reshape(CHUNK, D_MODEL)], axis=0)
            flights["small"], token = _exchange_start(("spread",),jnp.concatenate([g_small[1], g_small[0]], axis=0),
                                                      "start_small_grads")
            tokens = [token]
        dproj, dc_key, dc_query = _attn_bwd_t(s["proj"], dyb, s["e_row"], delta_b, s["c_pad"], dproj,
                                              f"attn_bwd_{l}", deps=tokens)
        dc = jnp.pad((dc_key + dc_query)[:, 0, :].T, ((0, 0), (0, HEAD_DIM - N_HEADS)))
        dproj, db_f = _forget_bwd_t(s["proj"], s["b_pad"], dc, dproj, f"forget_bwd_{l}")
        db_f_row = jnp.pad(db_f, ((0, 0), (0, D_MODEL - HEAD_DIM)))
        dw_in_t = _matmul(dproj, s["h"], "tn", BF16, f"dw_in_{l}")
        flights[("in", l)], token = _exchange_start(("windows", 0, IN_STRIDE, P_IN), _pack_grad_in(dw_in_t),
                                                    f"start_grad_in_{l}")
        dh = _matmul(dproj, w_in_t[l], "nn", F32, f"d_h_{l}", deps=[token])
        if l > 0:
            dx, dg_mix_pre, dd, dg_ffn_post_below = _norm_bwd(
                dx1, f"bwd_pre_mix_{l}", pre=(dh, s["x0"], vec(mix_pre_g, l)),
                post=(saved[l - 1]["dn"], vec(ffn_post_g, l - 1)))
            loss_row = jnp.pad(loss, ((0, 0), (0, D_MODEL - 1)))
            g_small[l] = jnp.concatenate([dg_mix_pre, dg_v, dg_mix_post, dg_ffn_pre, dg_ffn_post, db_s, db_f_row,
                                          loss_row, dw_s.reshape(CHUNK, D_MODEL)], axis=0)
            dg_ffn_post = dg_ffn_post_below
        else:
            dx, dg_mix_pre = _norm_bwd(dx1, f"bwd_pre_mix_{l}", pre=(dh, s["x0"], vec(mix_pre_g, l)))
            late_rows = jnp.concatenate([dg_mix_pre, db_f_row] + [zero_row] * 6, axis=0)
            flights["late"], late_token = _exchange_start(("spread",),late_rows, "start_late_small_grads")
    grad_x = dx.reshape(x.shape)

    landed, after = {}, [grad_x, late_token]
    rest = ["out", "gate", "up", "down"]
    for key in [(n, 1) for n in rest] + [("in", 1)] + [(n, 0) for n in rest] + ["small"]:
        label = key if isinstance(key, str) else f"grad_{key[0]}_{key[1]}"
        landed[key] = _exchange_wait(flights[key], after, f"wait_{label}")
        after = [landed[key]]
    small_sum = _sum_slots(landed["small"], S_ROWS, "sum_small_grads")
    sm = small_sum.reshape(depth, S_ROWS, D_MODEL)[::-1]
    loss = sm[depth - 1, 7, 0]
    g = {
        "sgu_norm_g": sm[:, 1], "mix_post_g": sm[:, 2],
        "ffn_pre_g": sm[:, 3], "ffn_post_g": sm[:, 4], "b_spatial": sm[:, 5].reshape(depth, N_GROUPS, CHUNK),
        "w_spatial": sm[:, 8:].reshape(depth, N_GROUPS, CHUNK, CHUNK),
    }
    for n in rest:
        g["w_" + n] = jnp.stack([_sum_slots(landed[(n, l)], landed[(n, l)].shape[1], f"sum_grad_{n}_{l}")
                                 for l in range(depth)])

    names = ["mix_pre_g", "w_in", "b_forget", "sgu_norm_g", "w_spatial", "b_spatial", "w_out", "mix_post_g",
             "ffn_pre_g", "w_gate", "w_up", "w_down", "ffn_post_g"]
    ws = dict(mix_pre_g=mix_pre_g, w_in=w_in, b_forget=b_forget, sgu_norm_g=sgu_norm_g, w_spatial=w_spatial,
              b_spatial=b_spatial, w_out=w_out, mix_post_g=mix_post_g, ffn_pre_g=ffn_pre_g, w_gate=w_gate, w_up=w_up,
              w_down=w_down, ffn_post_g=ffn_post_g)
    ms = dict(mix_pre_g=m_mix_pre_g, w_in=m_w_in, b_forget=m_b_forget, sgu_norm_g=m_sgu_norm_g, w_spatial=m_w_spatial,
              b_spatial=m_b_spatial, w_out=m_w_out, mix_post_g=m_mix_post_g, ffn_pre_g=m_ffn_pre_g, w_gate=m_w_gate,
              w_up=m_w_up, w_down=m_w_down, ffn_post_g=m_ffn_post_g)
    vs = dict(mix_pre_g=v_mix_pre_g, w_in=v_w_in, b_forget=v_b_forget, sgu_norm_g=v_sgu_norm_g, w_spatial=v_w_spatial,
              b_spatial=v_b_spatial, w_out=v_w_out, mix_post_g=v_mix_post_g, ffn_pre_g=v_ffn_pre_g, w_gate=v_w_gate,
              w_up=v_w_up, w_down=v_w_down, ffn_post_g=v_ffn_post_g)
    updates = {}
    last = ["w_in", "mix_pre_g", "b_forget"]
    transposed = ["w_in", "w_gate", "w_up"]
    for n in [n for n in names if n not in last] + last:
        if n == last[0]:
            sum_in_1 = _sum_slots(landed[("in", 1)], 304, "sum_grad_in_1")
            done = [u[0] for u in updates.values()] + [sum_in_1]
            sum_in_0 = _sum_slots(_exchange_wait(flights[("in", 0)], done, "wait_grad_in_0"), 304, "sum_grad_in_0")
            late = _sum_slots(_exchange_wait(flights["late"], [sum_in_0], "wait_late"), 8, "sum_late_small_grads")
            g["w_in"] = lax.dynamic_slice(jnp.stack([sum_in_0, sum_in_1]), (0, me, 0), (depth, SH_IN, D_MODEL))
            g["mix_pre_g"] = jnp.stack([late[0], sm[1, 0]])
            g["b_forget"] = jnp.stack([late[1, :N_HEADS], sm[1, 6, :N_HEADS]])
        if n in transposed:
            view = lambda a: jnp.swapaxes(a, 1, 2)
            updates[n] = [view(u) for u in _adamw_cols(view(ws[n]), g[n], view(ms[n]), view(vs[n]), f"adamw_{n}")]
            g[n] = view(g[n])
        else:
            flat = (-1, ws[n].shape[-1])
            updates[n] = [u.reshape(ws[n].shape) for u in
                          _adamw(ws[n].reshape(flat), g[n].reshape(flat), ms[n].reshape(flat), vs[n].reshape(flat),
                                 f"adamw_{n}")]
    deltas = [updates[n][0] for n in names]
    new_m = [updates[n][1] for n in names]
    new_v = [updates[n][2] for n in names]
    grads = [g[n].reshape(ws[n].shape) for n in names]
    return (loss, grad_x, *grads, *deltas, *new_m, *new_v)
```

```python
import math

import jax
import jax.numpy as jnp
from jax import lax
from jax.experimental import pallas as pl
from jax.experimental.pallas import tpu as pltpu

F32 = jnp.float32
BF16 = jnp.bfloat16

N_DEV = 8
D_MODEL = 1024
N_HEADS = 8
HEAD_DIM = 128
CHUNK = 128
N_GROUPS = 8
D_FF = 2816
IN_WIDTH = 7 * D_MODEL + N_HEADS
IN_PAD = 7680
EPS = 1e-6
GATED = 4 * D_MODEL
F_WIDTH = 512
QKV_WIDTH = 3 * HEAD_DIM
CB_F = GATED // HEAD_DIM
CB_QKV = (GATED + F_WIDTH) // HEAD_DIM
assert GATED + F_WIDTH + N_HEADS * QKV_WIDTH == IN_PAD and (GATED + F_WIDTH) % QKV_WIDTH == 0

ADAM_LR, ADAM_B1, ADAM_B2, ADAM_EPS, ADAM_WD, ADAM_STEP = 0.001, 0.9, 0.999, 1e-08, 0.01, 10

SH_IN = IN_WIDTH // N_DEV
SH_OUT = D_MODEL // N_DEV
SH_FF = D_FF // N_DEV
P_IN = 912
IN_STRIDE = 7 * HEAD_DIM
IN_ROWS = IN_STRIDE * (N_DEV - 1) + P_IN
S_ROWS = 136

ATTN_BLOCK = 1024
ROW_BLOCK = 512
MIX_ROWS = 512
MIX_BWD_ROWS = 256
MM_TM, MM_TN, MM_TK = 1536, 1536, 2048
VMEM_CAP = 56 << 20
NEG = -1e30


def _tile(n, cap, unit=128):
    if n <= cap:
        return n
    best = None
    for t in range(unit, cap + 1, unit):
        if n % t == 0:
            best = t
    assert best is not None, (n, cap)
    return best


def _nbytes(shape, dtype):
    return math.prod(shape) * jnp.dtype(dtype).itemsize


_HBM = pl.BlockSpec(memory_space=pltpu.HBM)
MESH = pl.DeviceIdType.MESH

_N_REMOTE = N_DEV - 1


def _exchange_shapes(kind, x):
    if kind[0] == "gather":
        shape = x.shape[:kind[1]] + (N_DEV,) + x.shape[kind[1]:]
    elif kind[0] == "spread":
        shape = (N_DEV,) + x.shape
    else:
        shape = (N_DEV, kind[3]) + x.shape[1:]
    return [jax.ShapeDtypeStruct(shape, x.dtype)]


def _exchange_sems(kind):
    del kind
    return [pltpu.SemaphoreType.DMA((_N_REMOTE,)), pltpu.SemaphoreType.DMA((_N_REMOTE,)), pltpu.SemaphoreType.DMA]


def _plan(kind, x_ref, outs, sems):
    send_sems, recv_sems, local_sem = sems
    x, y, c = lax.axis_index("x"), lax.axis_index("y"), lax.axis_index("c")

    def remote(src, dst, k, to):
        return pltpu.make_async_remote_copy(src_ref=src, dst_ref=dst, send_sem=send_sems.at[k], recv_sem=recv_sems.at[k],
                                            device_id=to, device_id_type=MESH)

    sibling = (x, y, 1 - c)
    if kind[0] == "gather":
        out, = outs
        slot = lambda px, py, pc: out.at[(slice(None),) * kind[1] + (4 * px + 2 * py + pc,)]
        chips = [(1 - x, y), (x, 1 - y), (1 - x, 1 - y)]
        local = [pltpu.make_async_copy(x_ref, slot(x, y, c), local_sem)]
        first = [remote(x_ref, slot(x, y, c), 0, sibling)]
        first += [remote(x_ref, slot(x, y, c), 1 + k, (*chip, c)) for k, chip in enumerate(chips)]
        relays = [(remote(x_ref, slot(*chip, c), 1 + k, (*chip, c)), remote(slot(*chip, c), slot(*chip, c), 4 + k, sibling))
                  for k, chip in enumerate(chips)]
        arrivals = [remote(x_ref, slot(x, y, 1 - c), 0, sibling)]
        arrivals += [remote(x_ref, slot(*chip, 1 - c), 4 + k, sibling) for k, chip in enumerate(chips)]
        return local, first, relays, arrivals
    out, = outs
    mine = 4 * x + 2 * y + c
    if kind[0] == "spread":
        src = lambda d: x_ref
    else:
        _, base, stride, length = kind
        src = lambda d: x_ref.at[pl.ds(pl.multiple_of(base + stride * d, 16), length)]
    first, arrivals = [], []
    for j in range(1, N_DEV):
        peer = (1 - x if j & 4 else x, 1 - y if j & 2 else y, 1 - c if j & 1 else c)
        theirs = 4 * peer[0] + 2 * peer[1] + peer[2]
        first.append(remote(src(theirs), out.at[mine], j - 1, peer))
        arrivals.append(remote(src(theirs), out.at[theirs], j - 1, peer))
    local = [] if local_sem is None else [pltpu.make_async_copy(src(mine), out.at[mine], local_sem)]
    return local, first, [], arrivals


def _start(plan):
    local, first, _, _ = plan
    for cp in local + first:
        cp.start()


def _finish(plan):
    local, first, relays, arrivals = plan
    for arrival, onward in relays:
        arrival.wait_recv()
        onward.start()
    for cp in arrivals:
        cp.wait_recv()
    for cp in first + [onward for _, onward in relays]:
        cp.wait_send()
    for cp in local:
        cp.wait()


def _direct_copies(kind, x_ref, land_ref, send_sems, recv_sems):
    _, first, _, arrivals = _plan(kind, x_ref, [land_ref], (send_sems, recv_sems, None))
    return first, arrivals


_SEM = pl.BlockSpec(memory_space=pltpu.SEMAPHORE)
_DATAFLOW = pltpu.SideEffectType.DATAFLOW_SIDE_EFFECTING


def _exchange_start(kind, x, name):
    me = 4 * lax.axis_index("x") + 2 * lax.axis_index("y") + lax.axis_index("c")
    own = x if kind[0] == "spread" else lax.dynamic_slice_in_dim(x, kind[1] + kind[2] * me, kind[3], 0)
    shape = _exchange_shapes(kind, x)[0].shape
    land = lax.dynamic_update_slice_in_dim(lax.empty(shape, x.dtype), own[None], me, 0)

    def body(x_ref, land_ref, send_sems, recv_sems, x_thru, land_thru, token):
        del x_thru, land_thru
        for cp in _direct_copies(kind, x_ref, land_ref, send_sems, recv_sems)[0]:
            cp.start()
        token[...] = jnp.zeros_like(token)

    *handle, token = pl.pallas_call(
        body, name=name,
        out_shape=(pltpu.SemaphoreType.DMA((_N_REMOTE,)), pltpu.SemaphoreType.DMA((_N_REMOTE,)),
                   pltpu.HBM(x.shape, x.dtype), pltpu.HBM(land.shape, land.dtype), jax.ShapeDtypeStruct((8, 128), F32)),
        in_specs=(_HBM, _HBM), out_specs=(_SEM, _SEM, _HBM, _HBM, pl.BlockSpec(memory_space=pltpu.VMEM)),
        input_output_aliases={0: 2, 1: 3}, compiler_params=pltpu.CompilerParams(has_side_effects=_DATAFLOW),
    )(pltpu.with_memory_space_constraint(x, pltpu.HBM), pltpu.with_memory_space_constraint(land, pltpu.HBM))
    return (kind, *handle), token


def _exchange_wait(handle, after, name):
    kind, send_sems, recv_sems, x_thru, land_thru = handle

    def body(x_ref, land_ref, send_sems, recv_sems, *rest):
        first, arrivals = _direct_copies(kind, x_ref, land_ref, send_sems, recv_sems)
        for cp in first:
            cp.wait_send()
        for cp in arrivals:
            cp.wait_recv()

    return pl.pallas_call(
        body, name=name,
        out_shape=(pltpu.HBM(x_thru.shape, x_thru.dtype), pltpu.HBM(land_thru.shape, land_thru.dtype)),
        in_specs=(_HBM, _HBM, _SEM, _SEM) + (pl.BlockSpec(memory_space=pl.ANY),) * len(after), out_specs=(_HBM, _HBM),
        input_output_aliases={0: 0, 1: 1}, compiler_params=pltpu.CompilerParams(has_side_effects=_DATAFLOW),
    )(x_thru, land_thru, send_sems, recv_sems, *after)[1]


def _exchange(kind, x, name):
    n_out = len(_exchange_shapes(kind, x))

    def body(x_ref, *refs):
        plan = _plan(kind, x_ref, refs[:n_out], refs[n_out:])
        _start(plan)
        _finish(plan)

    return pl.pallas_call(body, name=name, out_shape=_exchange_shapes(kind, x), in_specs=[_HBM],
                          out_specs=[_HBM] * n_out, scratch_shapes=_exchange_sems(kind))(x)


def _call(body, name, grid, in_specs, out_specs, out_shape, args, scratch=(), sem=None, nbytes=0, aliases=None,
          comm=(), deps=()):
    in_specs, out_specs, out_shape, args, scratch = (list(in_specs), list(out_specs), list(out_shape), list(args),
                                                     list(scratch))
    if deps:
        n_real, n_deps, unordered = len(args), len(deps), body

        def body(*refs):
            unordered(*refs[:n_real], *refs[n_real + n_deps:])

        in_specs += [pl.BlockSpec(memory_space=pl.ANY)] * n_deps
        args += list(deps)
    if comm:
        n_in, n_out, n_scr, n_ops = len(args), len(out_shape), len(scratch), len(comm)
        kinds = [kind for kind, _ in comm]
        shapes = [_exchange_shapes(kind, x) for kind, x in comm]
        inner = body

        def body(*refs):
            ins, refs = refs[:n_in], refs[n_in:]
            cins, refs = refs[:n_ops], refs[n_ops:]
            outs, refs = refs[:n_out], refs[n_out:]
            couts = []
            for sh in shapes:
                couts.append(refs[:len(sh)])
                refs = refs[len(sh):]
            own_scratch, sems = refs[:n_scr], refs[n_scr:]
            first = last = None
            for axis, size in enumerate(grid):
                at_start, at_end = pl.program_id(axis) == 0, pl.program_id(axis) == size - 1
                first = at_start if first is None else first & at_start
                last = at_end if last is None else last & at_end
            plans = [_plan(kinds[o], cins[o], couts[o], sems[3 * o:3 * o + 3]) for o in range(n_ops)]

            @pl.when(first)
            def _():
                for p in plans:
                    _start(p)

            inner(*ins, *outs, *own_scratch)

            @pl.when(last)
            def _():
                for p in plans:
                    _finish(p)

        in_specs += [_HBM] * n_ops
        args += [x for _, x in comm]
        for kind, sh in zip(kinds, shapes):
            out_shape += sh
            out_specs += [_HBM] * len(sh)
            scratch += _exchange_sems(kind)
        sem = ("arbitrary",) * len(grid)
    limit = int(min(max(2 * nbytes + (8 << 20), 32 << 20), VMEM_CAP))
    return pl.pallas_call(
        body, name=name, grid=grid, in_specs=in_specs, out_specs=out_specs, out_shape=out_shape,
        scratch_shapes=scratch, input_output_aliases=aliases or {},
        compiler_params=pltpu.CompilerParams(dimension_semantics=sem, vmem_limit_bytes=limit),
    )(*[pltpu.with_memory_space_constraint(a, pltpu.HBM) for a in args])


def _sum_slots(x, tr, name):
    n, r, c = x.shape

    def body(x_ref, o_ref):
        acc = x_ref[0].astype(F32)
        for d in range(1, n):
            acc = acc + x_ref[d].astype(F32)
        o_ref[...] = acc

    return _call(body, name, (r // tr,), [pl.BlockSpec((n, tr, c), lambda i: (0, i, 0))],
                 [pl.BlockSpec((tr, c), lambda i: (i, 0))], [jax.ShapeDtypeStruct((r, c), F32)], [x],
                 sem=("parallel",), nbytes=_nbytes((n, tr, c), x.dtype) + _nbytes((tr, c), F32))[0]


_DIMS = {"nn": (((1,), (0,)), ((), ())), "nt": (((1,), (1,)), ((), ())), "tn": (((0,), (0,)), ((), ()))}


def _matmul(a, b, mode, out_dtype, name, comm=(), deps=()):
    if mode == "nn":
        (m, k), (k2, n) = a.shape, b.shape
    elif mode == "nt":
        (m, k), (n, k2) = a.shape, b.shape
    else:
        (k, m), (k2, n) = a.shape, b.shape
    assert k == k2, (a.shape, b.shape, mode)
    tm, tn, tk = _tile(m, MM_TM), _tile(n, MM_TN), _tile(k, MM_TK)
    nk = k // tk
    dims = _DIMS[mode]
    a_spec = {"nn": pl.BlockSpec((tm, tk), lambda i, j, kk: (i, kk)),
              "nt": pl.BlockSpec((tm, tk), lambda i, j, kk: (i, kk)),
              "tn": pl.BlockSpec((tk, tm), lambda i, j, kk: (kk, i))}[mode]
    b_spec = {"nn": pl.BlockSpec((tk, tn), lambda i, j, kk: (kk, j)),
              "nt": pl.BlockSpec((tn, tk), lambda i, j, kk: (j, kk)),
              "tn": pl.BlockSpec((tk, tn), lambda i, j, kk: (kk, j))}[mode]

    def partial_product(a_ref, b_ref):
        return lax.dot_general(a_ref[...].astype(BF16), b_ref[...].astype(BF16), dims, preferred_element_type=F32)

    if nk == 1:
        def body(a_ref, b_ref, o_ref):
            o_ref[...] = partial_product(a_ref, b_ref).astype(o_ref.dtype)
        scratch = []
    else:
        def body(a_ref, b_ref, o_ref, acc_ref):
            kk = pl.program_id(2)

            @pl.when(kk == 0)
            def _():
                acc_ref[...] = jnp.zeros_like(acc_ref)

            acc_ref[...] += partial_product(a_ref, b_ref)

            @pl.when(kk == nk - 1)
            def _():
                o_ref[...] = acc_ref[...].astype(o_ref.dtype)
        scratch = [pltpu.VMEM((tm, tn), F32)]

    nbytes = (_nbytes((tm, tk), a.dtype) + _nbytes((tk, tn), b.dtype) + _nbytes((tm, tn), out_dtype)
              + _nbytes((tm, tn), F32))
    res = _call(body, name, (m // tm, n // tn, nk), [a_spec, b_spec],
                [pl.BlockSpec((tm, tn), lambda i, j, kk: (i, j))], [jax.ShapeDtypeStruct((m, n), out_dtype)], [a, b],
                scratch=scratch, sem=("parallel", "parallel", "arbitrary"), nbytes=nbytes, comm=comm, deps=deps)
    return res if comm else res[0]


def _rms(x):
    return lax.rsqrt(jnp.mean(x * x, axis=-1, keepdims=True) + EPS)


def _rms_bwd(dz, a, g):
    r = _rms(a)
    dzg = dz * g
    da = r * dzg - a * (r * r * r) * jnp.mean(dzg * a, axis=-1, keepdims=True)
    return da, dz * (a * r)


def _row_spec(tr, width):
    return pl.BlockSpec((tr, width), lambda i: (i, 0))


def _vec_spec(width):
    return pl.BlockSpec((1, width), lambda i: (0, 0))


def _rms_fwd(x, g, name):
    t, d = x.shape
    tr = min(ROW_BLOCK, t)

    def body(x_ref, g_ref, h_ref):
        xv = x_ref[...]
        h_ref[...] = ((xv * _rms(xv)) * g_ref[...]).astype(BF16)

    return _call(body, name, (t // tr,), [_row_spec(tr, d), _vec_spec(d)], [_row_spec(tr, d)],
                 [jax.ShapeDtypeStruct((t, d), BF16)], [x, g], sem=("parallel",), nbytes=3 * _nbytes((tr, d), F32))[0]


def _post_norm_fwd(x, o, g_post, g_next, name):
    t, d = x.shape
    tr = min(ROW_BLOCK, t)

    def body(x_ref, o_ref, gp_ref, gn_ref, xn_ref, h_ref):
        ov = o_ref[...]
        xn = x_ref[...] + (ov * _rms(ov)) * gp_ref[...]
        xn_ref[...] = xn
        h_ref[...] = ((xn * _rms(xn)) * gn_ref[...]).astype(BF16)

    return _call(body, name, (t // tr,), [_row_spec(tr, d), _row_spec(tr, d), _vec_spec(d), _vec_spec(d)],
                 [_row_spec(tr, d), _row_spec(tr, d)],
                 [jax.ShapeDtypeStruct((t, d), F32), jax.ShapeDtypeStruct((t, d), BF16)], [x, o, g_post, g_next],
                 sem=("parallel",), nbytes=5 * _nbytes((tr, d), F32))


def _post_norm_loss(x, o, g_post, target, name):
    t, d = x.shape
    tr = min(ROW_BLOCK, t)

    def body(x_ref, o_ref, gp_ref, t_ref, dy_ref, loss_ref):
        ov = o_ref[...]
        err = x_ref[...] + (ov * _rms(ov)) * gp_ref[...] - t_ref[...]
        dy_ref[...] = err / d
        part = 0.5 * jnp.sum(jnp.mean(err * err, axis=-1, keepdims=True), axis=0, keepdims=True)

        @pl.when(pl.program_id(0) == 0)
        def _():
            loss_ref[...] = jnp.zeros_like(loss_ref)

        loss_ref[...] += part

    return _call(body, name, (t // tr,), [_row_spec(tr, d), _row_spec(tr, d), _vec_spec(d), _row_spec(tr, d)],
                 [_row_spec(tr, d), pl.BlockSpec((1, 1), lambda i: (0, 0))],
                 [jax.ShapeDtypeStruct((t, d), F32), jax.ShapeDtypeStruct((1, 1), F32)], [x, o, g_post, target],
                 sem=("arbitrary",), nbytes=5 * _nbytes((tr, d), F32))


def _norm_bwd(dskip, name, pre=None, post=None):
    t, d = dskip.shape
    tr = min(ROW_BLOCK, t)
    n_in = 1 + (3 if pre else 0) + (2 if post else 0)

    def body(*refs):
        ins, outs = list(refs[:n_in]), list(refs[n_in:])
        first = pl.program_id(0) == 0
        dx = ins.pop(0)[...]
        if pre:
            dh_ref, xin_ref, gpre_ref = ins.pop(0), ins.pop(0), ins.pop(0)
            dxin, dg_rows = _rms_bwd(dh_ref[...].astype(F32), xin_ref[...], gpre_ref[...])
            dx = dx + dxin
            dx_ref, dgpre_ref = outs.pop(0), outs.pop(0)
            dx_ref[...] = dx

            @pl.when(first)
            def _():
                dgpre_ref[...] = jnp.zeros_like(dgpre_ref)

            dgpre_ref[...] += jnp.sum(dg_rows, axis=0, keepdims=True)
        if post:
            a_ref, gpost_ref = ins.pop(0), ins.pop(0)
            da, dg_rows = _rms_bwd(dx, a_ref[...], gpost_ref[...])
            da_ref, dgpost_ref = outs.pop(0), outs.pop(0)
            da_ref[...] = da.astype(BF16)

            @pl.when(first)
            def _():
                dgpost_ref[...] = jnp.zeros_like(dgpost_ref)

            dgpost_ref[...] += jnp.sum(dg_rows, axis=0, keepdims=True)

    args, in_specs, out_shape, out_specs = [dskip], [_row_spec(tr, d)], [], []
    if pre:
        args += list(pre)
        in_specs += [_row_spec(tr, d), _row_spec(tr, d), _vec_spec(d)]
        out_shape += [jax.ShapeDtypeStruct((t, d), F32), jax.ShapeDtypeStruct((1, d), F32)]
        out_specs += [_row_spec(tr, d), _vec_spec(d)]
    if post:
        args += list(post)
        in_specs += [_row_spec(tr, d), _vec_spec(d)]
        out_shape += [jax.ShapeDtypeStruct((t, d), BF16), jax.ShapeDtypeStruct((1, d), F32)]
        out_specs += [_row_spec(tr, d), _vec_spec(d)]
    return _call(body, name, (t // tr,), in_specs, out_specs, out_shape, args, sem=("arbitrary",),
                 nbytes=7 * _nbytes((tr, d), F32))


def _sigmoid(x):
    return 1.0 / (1.0 + jnp.exp(-x))


def _swiglu_fwd(gu, name):
    t = gu.shape[0]
    tr = min(ROW_BLOCK, t)

    def body(gu_ref, a_ref):
        gate, up = gu_ref[:, :D_FF].astype(F32), gu_ref[:, D_FF:].astype(F32)
        a_ref[...] = ((gate * _sigmoid(gate)) * up).astype(BF16)

    return _call(body, name, (t // tr,), [_row_spec(tr, 2 * D_FF)], [_row_spec(tr, D_FF)],
                 [jax.ShapeDtypeStruct((t, D_FF), BF16)], [gu], sem=("parallel",),
                 nbytes=2 * _nbytes((tr, 2 * D_FF), F32))[0]


def _swiglu_bwd(gu, dact, name):
    t = gu.shape[0]
    tr = min(ROW_BLOCK, t)

    def body(gu_ref, da_ref, o_ref):
        gate, up = gu_ref[:, :D_FF].astype(F32), gu_ref[:, D_FF:].astype(F32)
        da = da_ref[...].astype(F32)
        s = _sigmoid(gate)
        o_ref[:, :D_FF] = (da * up * (s * (1.0 + gate * (1.0 - s)))).astype(BF16)
        o_ref[:, D_FF:] = (da * (gate * s)).astype(BF16)

    return _call(body, name, (t // tr,), [_row_spec(tr, 2 * D_FF), _row_spec(tr, D_FF)], [_row_spec(tr, 2 * D_FF)],
                 [jax.ShapeDtypeStruct((t, 2 * D_FF), BF16)], [gu, dact], sem=("parallel",),
                 nbytes=3 * _nbytes((tr, 2 * D_FF), F32))[0]


def _split3(x):
    hi = x.astype(BF16)
    r1 = x - hi.astype(F32)
    mid = r1.astype(BF16)
    lo = (r1 - mid.astype(F32)).astype(BF16)
    return hi, mid, lo


def _dot_exact(mat01, x):
    hi, mid, lo = _split3(x)
    out = jnp.dot(mat01, hi, preferred_element_type=F32)
    out = out + jnp.dot(mat01, mid, preferred_element_type=F32)
    return out + jnp.dot(mat01, lo, preferred_element_type=F32)


def _dot_exact_rhs(x, mat01):
    hi, mid, lo = _split3(x)
    out = jnp.dot(hi, mat01, preferred_element_type=F32)
    out = out + jnp.dot(mid, mat01, preferred_element_type=F32)
    return out + jnp.dot(lo, mat01, preferred_element_type=F32)


def _tri(lower):
    r = lax.broadcasted_iota(jnp.int32, (CHUNK, CHUNK), 0)
    c = lax.broadcasted_iota(jnp.int32, (CHUNK, CHUNK), 1)
    return jnp.where(r >= c if lower else r <= c, 1.0, 0.0).astype(BF16)


def _log_sigmoid(z):
    return jnp.minimum(z, 0.0) - jnp.log(1.0 + jnp.exp(-jnp.abs(z)))


def _forget_cumsum(proj, b_pad, name):
    t = proj.shape[0]
    nb = t // CHUNK

    def body(f_ref, b_ref, c_ref):
        tri = _tri(True)
        b = b_ref[...]

        def blk(i, carry):
            rows = pl.ds(pl.multiple_of(i * CHUNK, CHUNK), CHUNK)
            cs = _dot_exact(tri, _log_sigmoid(f_ref[rows, :] + b)) + carry
            c_ref[rows, :] = cs
            return cs[CHUNK - 1:CHUNK, :]

        lax.fori_loop(0, nb, blk, jnp.zeros((1, HEAD_DIM), F32))

    return _call(body, name, (1,),
                 [pl.BlockSpec((t, HEAD_DIM), lambda i: (0, CB_F)), pl.BlockSpec((1, HEAD_DIM), lambda i: (0, 0))],
                 [pl.BlockSpec((t, HEAD_DIM), lambda i: (0, 0))], [jax.ShapeDtypeStruct((t, HEAD_DIM), F32)],
                 [proj, b_pad], sem=("arbitrary",), nbytes=2 * _nbytes((t, HEAD_DIM), F32))[0]


def _causal(block):
    r = lax.broadcasted_iota(jnp.int32, (block, block), 0)
    c = lax.broadcasted_iota(jnp.int32, (block, block), 1)
    return c <= r


def _lanes(x, width):
    return jnp.concatenate([x] * (width // HEAD_DIM), axis=1)


def _chunks(t, blk):
    return [slice(r * blk, (r + 1) * blk) for r in range(t // blk)]


def _lane_of_head(x, h):
    lane = lax.broadcasted_iota(jnp.int32, x.shape, 1)
    return jnp.broadcast_to(jnp.sum(jnp.where(lane == h, x, 0.0), axis=1, keepdims=True), x.shape)


def _causal_t(block):
    r = lax.broadcasted_iota(jnp.int32, (block, block), 0)
    c = lax.broadcasted_iota(jnp.int32, (block, block), 1)
    return r <= c


def _attn_fwd_t(proj, c_pad, c_row, name, comm=()):
    t = proj.shape[0]
    blk = min(ATTN_BLOCK, t)
    nq = t // blk
    scale = HEAD_DIM ** -0.5

    def body(q_ref, k_ref, v_ref, cp_ref, cr_ref, y_ref, e_ref, kb_s, vt_s, cb_s):
        h, i = pl.program_id(0), pl.program_id(1)

        @pl.when(i == 0)
        def _():
            for rows in _chunks(t, blk):
                kb_s[rows, :] = k_ref[rows, :].astype(BF16)
                vt_s[:, rows] = v_ref[rows, :].T.astype(BF16)
                cb_s[rows, :] = _lane_of_head(cp_ref[rows, :], h)

        q = (q_ref[...] * scale).astype(BF16)
        ci = cr_ref[:, pl.ds(pl.multiple_of(i * blk, blk), blk)]

        def step(j, carry, diagonal):
            m, l, acc = carry
            rows = pl.ds(pl.multiple_of(j * blk, blk), blk)
            s = lax.dot_general(kb_s[rows, :], q, _DIMS["nt"], preferred_element_type=F32)
            s = s + ci - _lanes(cb_s[rows, :], blk)
            if diagonal:
                s = jnp.where(_causal_t(blk), s, NEG)
            m_new = jnp.maximum(m, jnp.max(s, axis=0, keepdims=True))
            alpha = jnp.exp(m - m_new)
            p = jnp.exp(s - m_new)
            l = alpha * l + jnp.sum(p, axis=0, keepdims=True)
            acc = alpha * acc + jnp.dot(vt_s[:, rows], p.astype(BF16), preferred_element_type=F32)
            return m_new, l, acc

        init = (jnp.full((1, blk), NEG, F32), jnp.zeros((1, blk), F32), jnp.zeros((HEAD_DIM, blk), F32))
        carry = lax.fori_loop(0, i, lambda j, c: step(j, c, False), init)
        m, l, acc = step(i, carry, True)
        y_ref[...] = (acc / l).T
        e_ref[...] = ci - (m + jnp.log(l))

    nbytes = 3 * _nbytes((t, HEAD_DIM), F32) + 4 * _nbytes((blk, HEAD_DIM), F32) + 4 * _nbytes((blk, blk), F32)
    return _call(body, name, (N_HEADS, nq),
                 [pl.BlockSpec((blk, HEAD_DIM), lambda h, i: (i, CB_QKV + 3 * h)),
                  pl.BlockSpec((t, HEAD_DIM), lambda h, i: (0, CB_QKV + 3 * h + 1)),
                  pl.BlockSpec((t, HEAD_DIM), lambda h, i: (0, CB_QKV + 3 * h + 2)),
                  pl.BlockSpec((t, HEAD_DIM), lambda h, i: (0, 0)),
                  pl.BlockSpec((None, 1, t), lambda h, i: (h, 0, 0))],
                 [pl.BlockSpec((blk, HEAD_DIM), lambda h, i: (i, h)),
                  pl.BlockSpec((None, 1, blk), lambda h, i: (h, 0, i))],
                 [jax.ShapeDtypeStruct((t, D_MODEL), F32), jax.ShapeDtypeStruct((N_HEADS, 1, t), F32)],
                 [proj, proj, proj, c_pad, c_row],
                 scratch=[pltpu.VMEM((t, HEAD_DIM), BF16), pltpu.VMEM((HEAD_DIM, t), BF16),
                          pltpu.VMEM((t, HEAD_DIM), F32)],
                 sem=("parallel", "arbitrary"), nbytes=nbytes, comm=comm)


def _attn_bwd_t(proj, dy, e_row, delta_b, c_pad, dproj, name, deps=()):
    t = proj.shape[0]
    blk = min(ATTN_BLOCK, t)
    nb = t // blk
    scale = HEAD_DIM ** -0.5

    def body(q_ref, k_ref, v_ref, dy_ref, e_ref, dl_ref, cp_ref, dproj_in, dqkv_ref, dck_ref, dcq_ref,
             qb_s, qt_s, dyb_s, dyt_s, dl_s, dqt_acc):
        del dproj_in
        h, j = pl.program_id(0), pl.program_id(1)

        @pl.when(j == 0)
        def _():
            for rows in _chunks(t, blk):
                qs = q_ref[rows, :] * scale
                qb_s[rows, :] = qs.astype(BF16)
                qt_s[:, rows] = qs.T.astype(BF16)
                dyr = dy_ref[rows, :]
                dyb_s[rows, :] = dyr.astype(BF16)
                dyt_s[:, rows] = dyr.T.astype(BF16)
                dl_s[:, rows] = dl_ref[rows, :].T[0:1, :]
            dqt_acc[...] = jnp.zeros_like(dqt_acc)
            dcq_ref[...] = jnp.zeros_like(dcq_ref)

        kf = k_ref[...]
        kb, kt = kf.astype(BF16), kf.T.astype(BF16)
        vb = v_ref[...].astype(BF16)
        cj = _lanes(_lane_of_head(cp_ref[...], h), blk)

        def step(i, carry, diagonal):
            dkt, dvt, dkey = carry
            cols = pl.ds(pl.multiple_of(i * blk, blk), blk)
            s = lax.dot_general(kb, qb_s[cols, :], _DIMS["nt"], preferred_element_type=F32) + e_ref[:, cols] - cj
            if diagonal:
                s = jnp.where(_causal_t(blk), s, NEG)
            p = jnp.exp(s)
            dp = lax.dot_general(vb, dyb_s[cols, :], _DIMS["nt"], preferred_element_type=F32)
            ds = p * (dp - dl_s[:, cols])
            pb, dsb = p.astype(BF16), ds.astype(BF16)
            dvt = dvt + lax.dot_general(dyt_s[:, cols], pb, _DIMS["nt"], preferred_element_type=F32)
            dkt = dkt + lax.dot_general(qt_s[:, cols], dsb, _DIMS["nt"], preferred_element_type=F32)
            dqt_acc[:, cols] += jnp.dot(kt, dsb, preferred_element_type=F32) * scale
            dcq_ref[:, cols] += jnp.sum(ds, axis=0, keepdims=True)
            for k in range(blk // HEAD_DIM):
                dkey = dkey + ds[:, k * HEAD_DIM:(k + 1) * HEAD_DIM]
            return dkt, dvt, dkey

        zeros = jnp.zeros((HEAD_DIM, blk), F32)
        carry = step(j, (zeros, zeros, jnp.zeros((blk, HEAD_DIM), F32)), True)
        dkt, dvt, dkey = lax.fori_loop(j + 1, nb, lambda i, c: step(i, c, False), carry)
        mine = pl.ds(pl.multiple_of(j * blk, blk), blk)
        dqkv_ref[:, :HEAD_DIM] = dqt_acc[:, mine].T.astype(BF16)
        dqkv_ref[:, HEAD_DIM:2 * HEAD_DIM] = dkt.T.astype(BF16)
        dqkv_ref[:, 2 * HEAD_DIM:] = dvt.T.astype(BF16)
        dck_ref[...] = -jnp.sum(dkey.T, axis=0, keepdims=True)

    full = lambda cb: pl.BlockSpec((t, HEAD_DIM), lambda h, j: (0, cb(h)))
    head = lambda h: h
    row = pl.BlockSpec((None, 1, t), lambda h, j: (h, 0, 0))
    nbytes = (8 * _nbytes((t, HEAD_DIM), F32) + 4 * _nbytes((blk, HEAD_DIM), F32) + 6 * _nbytes((blk, blk), F32))
    return _call(body, name, (N_HEADS, nb),
                 [full(lambda h: CB_QKV + 3 * h),
                  pl.BlockSpec((blk, HEAD_DIM), lambda h, j: (j, CB_QKV + 3 * h + 1)),
                  pl.BlockSpec((blk, HEAD_DIM), lambda h, j: (j, CB_QKV + 3 * h + 2)),
                  full(head), row, full(head),
                  pl.BlockSpec((blk, HEAD_DIM), lambda h, j: (j, 0)),
                  pl.BlockSpec(memory_space=pl.ANY)],
                 [pl.BlockSpec((blk, QKV_WIDTH), lambda h, j: (j, (GATED + F_WIDTH) // QKV_WIDTH + h)),
                  pl.BlockSpec((None, 1, blk), lambda h, j: (h, 0, j)),
                  row],
                 [jax.ShapeDtypeStruct(dproj.shape, dproj.dtype), jax.ShapeDtypeStruct((N_HEADS, 1, t), F32),
                  jax.ShapeDtypeStruct((N_HEADS, 1, t), F32)],
                 [proj, proj, proj, dy, e_row, delta_b, c_pad, dproj],
                 scratch=[pltpu.VMEM((t, HEAD_DIM), BF16), pltpu.VMEM((HEAD_DIM, t), BF16),
                          pltpu.VMEM((t, HEAD_DIM), BF16), pltpu.VMEM((HEAD_DIM, t), BF16),
                          pltpu.VMEM((1, t), F32), pltpu.VMEM((HEAD_DIM, t), F32)],
                 sem=("parallel", "arbitrary"), nbytes=nbytes, aliases={7: 0}, deps=deps)


def _forget_bwd_t(proj, b_pad, dc, dproj, name):
    t = proj.shape[0]
    tr = min(ROW_BLOCK, t)
    nb = t // tr
    rev = lambda i: nb - 1 - i

    def body(f_ref, b_ref, dc_ref, dproj_in, df_ref, db_ref, run_ref):
        del dproj_in

        @pl.when(pl.program_id(0) == 0)
        def _():
            run_ref[...] = jnp.zeros_like(run_ref)
            db_ref[...] = jnp.zeros_like(db_ref)

        tri = _tri(False)
        b = b_ref[...]
        lane = lax.broadcasted_iota(jnp.int32, (CHUNK, HEAD_DIM), 1)
        df_ref[...] = jnp.zeros_like(df_ref)
        for n in reversed(range(tr // CHUNK)):
            rows = slice(n * CHUNK, (n + 1) * CHUNK)
            dlogf = _dot_exact(tri, dc_ref[rows, :]) + run_ref[...]
            run_ref[...] = dlogf[0:1, :]
            z = f_ref[rows, :] + b
            e = jnp.exp(-jnp.abs(z))
            sig_neg = jnp.where(z >= 0.0, e, 1.0) / (1.0 + e)
            df = jnp.where(lane < N_HEADS, dlogf * sig_neg, 0.0)
            df_ref[rows, :HEAD_DIM] = df.astype(BF16)
            db_ref[...] += jnp.sum(df, axis=0, keepdims=True)

    return _call(body, name, (nb,),
                 [pl.BlockSpec((tr, HEAD_DIM), lambda i: (rev(i), CB_F)), pl.BlockSpec((1, HEAD_DIM), lambda i: (0, 0)),
                  pl.BlockSpec((tr, HEAD_DIM), lambda i: (rev(i), 0)), pl.BlockSpec(memory_space=pl.ANY)],
                 [pl.BlockSpec((tr, F_WIDTH), lambda i: (rev(i), GATED // F_WIDTH)),
                  pl.BlockSpec((1, HEAD_DIM), lambda i: (0, 0))],
                 [jax.ShapeDtypeStruct(dproj.shape, dproj.dtype), jax.ShapeDtypeStruct((1, HEAD_DIM), F32)],
                 [proj, b_pad, dc, dproj], scratch=[pltpu.VMEM((1, HEAD_DIM), F32)],
                 sem=("arbitrary",), nbytes=4 * _nbytes((tr, HEAD_DIM), F32), aliases={3: 0})


_GELU_K = math.sqrt(2.0 / math.pi)
_GELU_C = 0.044715


def _gelu(x):
    t = jnp.tanh(_GELU_K * (x + _GELU_C * (x * x * x)))
    return 0.5 * x * (1.0 + t), t


def _gelu_grad(x, t):
    return 0.5 * (1.0 + t) + 0.5 * x * (1.0 - t * t) * (_GELU_K * (1.0 + 3.0 * _GELU_C * (x * x)))


def _layernorm_stats(a):
    mu = jnp.mean(a, axis=-1, keepdims=True)
    xc = a - mu
    r = lax.rsqrt(jnp.mean(xc * xc, axis=-1, keepdims=True) + EPS)
    return xc * r, r


def _group(g):
    return slice(g * CHUNK, (g + 1) * CHUNK)


def _fixed(shape):
    return pl.BlockSpec(shape, lambda i: (0,) * len(shape))


def _mix_fwd(proj, y_b, w_s, b_cols, g_v, name):
    t = proj.shape[0]
    step_rows = min(MIX_ROWS, t)

    def body(u_ref, vs_ref, ga_ref, gb_ref, yb_ref, w_ref, b_ref, gv_ref, o_ref):
        mask = _causal(CHUNK)
        w = [jnp.where(mask, w_ref[g], 0.0).astype(BF16) for g in range(N_GROUPS)]
        for rows in _chunks(step_rows, CHUNK):
            a_u, _ = _gelu(u_ref[rows, :])
            a_v, _ = _gelu(vs_ref[rows, :])
            xhat, _ = _layernorm_stats(a_v)
            vn = (xhat * gv_ref[...]).astype(BF16)
            sa, sb = _sigmoid(ga_ref[rows, :]), _sigmoid(gb_ref[rows, :])
            yb = yb_ref[rows, :]
            for g in range(N_GROUPS):
                cols = _group(g)
                mixed = jnp.dot(w[g], vn[:, cols], preferred_element_type=F32) + b_ref[:, g:g + 1]
                o_ref[rows, cols] = (sa[:, cols] * (a_u[:, cols] * mixed) + sb[:, cols] * yb[:, cols]).astype(BF16)

    gated = lambda k: pl.BlockSpec((step_rows, D_MODEL), lambda i: (i, k))
    return _call(body, name, (t // step_rows,),
                 [gated(0), gated(1), gated(2), gated(3), _row_spec(step_rows, D_MODEL),
                  _fixed((N_GROUPS, CHUNK, CHUNK)), _fixed((CHUNK, N_GROUPS)), _fixed((1, D_MODEL))],
                 [_row_spec(step_rows, D_MODEL)], [jax.ShapeDtypeStruct((t, D_MODEL), BF16)],
                 [proj, proj, proj, proj, y_b, w_s, b_cols, g_v], sem=("parallel",),
                 nbytes=16 * _nbytes((step_rows, D_MODEL), F32))[0]


def _mix_bwd(proj, y_b, dmerged, w_s, b_cols, g_v, sel, name, deps=()):
    t = proj.shape[0]
    step_rows = min(MIX_BWD_ROWS, t)

    def body(u_ref, vs_ref, ga_ref, gb_ref, yb_ref, dm_ref, w_ref, b_ref, gv_ref, sel_ref,
             dg_ref, dyb_ref, delta_ref, dw_ref, dbt_ref, dgv_ref):
        @pl.when(pl.program_id(0) == 0)
        def _():
            dw_ref[...] = jnp.zeros_like(dw_ref)
            dbt_ref[...] = jnp.zeros_like(dbt_ref)
            dgv_ref[...] = jnp.zeros_like(dgv_ref)

        gv = gv_ref[...]
        mask = _causal(CHUNK)
        w = [jnp.where(mask, w_ref[g], 0.0).astype(BF16) for g in range(N_GROUPS)]
        for rows in _chunks(step_rows, CHUNK):
            u, vs = u_ref[rows, :], vs_ref[rows, :]
            a_u, t_u = _gelu(u)
            a_v, t_v = _gelu(vs)
            xhat, r = _layernorm_stats(a_v)
            vn = (xhat * gv).astype(BF16)
            sa, sb = _sigmoid(ga_ref[rows, :]), _sigmoid(gb_ref[rows, :])
            yb, dm = yb_ref[rows, :], dm_ref[rows, :].astype(F32)
            dyb = dm * sb
            dyb_ref[rows, :] = dyb
            dyb_yb = dyb * yb
            dg_ref[rows, 3 * D_MODEL:] = (dm * yb * (sb * (1.0 - sb))).astype(BF16)
            dya = dm * sa
            dmixed_parts, dvn_parts = [], []
            for g in range(N_GROUPS):
                cols = _group(g)
                delta_ref[rows, cols] = jnp.broadcast_to(jnp.sum(dyb_yb[:, cols], axis=1, keepdims=True),
                                                         (CHUNK, CHUNK))
                mixed = jnp.dot(w[g], vn[:, cols], preferred_element_type=F32) + b_ref[:, g:g + 1]
                y_a = a_u[:, cols] * mixed
                dg_ref[rows, 2 * D_MODEL + g * CHUNK:2 * D_MODEL + (g + 1) * CHUNK] = (
                    dm[:, cols] * y_a * (sa[:, cols] * (1.0 - sa[:, cols]))).astype(BF16)
                dg_ref[rows, cols] = (dya[:, cols] * mixed * _gelu_grad(u[:, cols], t_u[:, cols])).astype(BF16)
                dmixed = dya[:, cols] * a_u[:, cols]
                dmb = dmixed.astype(BF16)
                dw = lax.dot_general(dmb, vn[:, cols], _DIMS["nt"], preferred_element_type=F32)
                dw_ref[g] += jnp.where(mask, dw, 0.0)
                dvn_parts.append(lax.dot_general(w[g], dmb, _DIMS["tn"], preferred_element_type=F32))
                dmixed_parts.append(dmixed)
            dmixed_all = jnp.concatenate(dmixed_parts, axis=1)
            dvn = jnp.concatenate(dvn_parts, axis=1)
            dbt_ref[...] += _dot_exact_rhs(dmixed_all, sel_ref[...])
            dgv_ref[...] += jnp.sum(dvn * xhat, axis=0, keepdims=True)
            dxh = dvn * gv
            da_v = r * (dxh - jnp.mean(dxh, axis=-1, keepdims=True)
                        - xhat * jnp.mean(dxh * xhat, axis=-1, keepdims=True))
            dg_ref[rows, D_MODEL:2 * D_MODEL] = (da_v * _gelu_grad(vs, t_v)).astype(BF16)

    row = lambda width: _row_spec(step_rows, width)
    gated = lambda k: pl.BlockSpec((step_rows, D_MODEL), lambda i: (i, k))
    return _call(body, name, (t // step_rows,),
                 [gated(0), gated(1), gated(2), gated(3), row(D_MODEL), row(D_MODEL),
                  _fixed((N_GROUPS, CHUNK, CHUNK)), _fixed((CHUNK, N_GROUPS)), _fixed((1, D_MODEL)),
                  _fixed((D_MODEL, HEAD_DIM))],
                 [row(GATED), row(D_MODEL), row(D_MODEL),
                  _fixed((N_GROUPS, CHUNK, CHUNK)), _fixed((CHUNK, HEAD_DIM)), _fixed((1, D_MODEL))],
                 [jax.ShapeDtypeStruct((t, IN_PAD), BF16), jax.ShapeDtypeStruct((t, D_MODEL), F32),
                  jax.ShapeDtypeStruct((t, D_MODEL), F32), jax.ShapeDtypeStruct((N_GROUPS, CHUNK, CHUNK), F32),
                  jax.ShapeDtypeStruct((CHUNK, HEAD_DIM), F32), jax.ShapeDtypeStruct((1, D_MODEL), F32)],
                 [proj, proj, proj, proj, y_b, dmerged, w_s, b_cols, g_v, sel], sem=("arbitrary",),
                 nbytes=24 * _nbytes((step_rows, D_MODEL), F32), deps=deps)


def _adamw_update(w_ref, g_ref, m_ref, v_ref, d_ref, nm_ref, nv_ref):
    gv = g_ref[...]
    nm = ADAM_B1 * m_ref[...] + (1.0 - ADAM_B1) * gv
    nv = ADAM_B2 * v_ref[...] + (1.0 - ADAM_B2) * (gv * gv)
    m_hat = nm / (1.0 - ADAM_B1 ** ADAM_STEP)
    v_hat = nv / (1.0 - ADAM_B2 ** ADAM_STEP)
    d_ref[...] = -ADAM_LR * (m_hat / (jnp.sqrt(v_hat) + ADAM_EPS) + ADAM_WD * w_ref[...])
    nm_ref[...] = nm
    nv_ref[...] = nv


def _adamw(w, g, m, v, name):
    r, c = w.shape
    tr = _tile(r, 256, unit=8)

    def body(w_ref, g_ref, m_ref, v_ref, d_ref, nm_ref, nv_ref):
        _adamw_update(w_ref, g_ref, m_ref, v_ref, d_ref, nm_ref, nv_ref)

    spec = pl.BlockSpec((tr, c), lambda i: (i, 0))
    shape = jax.ShapeDtypeStruct((r, c), F32)
    return _call(body, name, (r // tr,), [spec] * 4, [spec] * 3, [shape] * 3, [w, g, m, v], sem=("parallel",),
                 nbytes=7 * _nbytes((tr, max(c, 128)), F32))


def _adamw_cols(w, g, m, v, name):
    layers, r, c = w.shape
    tc = 256

    def body(w_ref, g_ref, m_ref, v_ref, d_ref, nm_ref, nv_ref):
        _adamw_update(w_ref, g_ref, m_ref, v_ref, d_ref, nm_ref, nv_ref)

    spec = pl.BlockSpec((None, r, tc), lambda l, j: (l, 0, j))
    shape = jax.ShapeDtypeStruct(w.shape, F32)
    return _call(body, name, (layers, c // tc), [spec] * 4, [spec] * 3, [shape] * 3, [w, g, m, v],
                 sem=("parallel", "parallel"), nbytes=7 * _nbytes((r, tc), F32))


def _pad_rows(a, rows):
    return jnp.pad(a, ((0, rows - a.shape[0]), (0, 0)))


def _unpack_in(gathered):
    i = jnp.arange(16)[None, :, None]
    b = jnp.arange(N_DEV)[:, None, None]
    head = jnp.where(i < b, jnp.roll(gathered[:, IN_STRIDE:IN_STRIDE + 16], 1, axis=0), gathered[:, :16])
    nat = jnp.concatenate([head, gathered[:, 16:IN_STRIDE]], axis=1).reshape(N_DEV * IN_STRIDE, D_MODEL)
    qkv = nat[2 * D_MODEL:5 * D_MODEL].reshape(3, N_HEADS, HEAD_DIM, D_MODEL)
    qkv = jnp.transpose(qkv, (1, 0, 2, 3)).reshape(3 * D_MODEL, D_MODEL)
    f = _pad_rows(gathered[N_DEV - 1, IN_STRIDE:IN_STRIDE + N_HEADS], F_WIDTH)
    return jnp.concatenate([nat[:2 * D_MODEL], nat[5 * D_MODEL:7 * D_MODEL], f, qkv], axis=0)


def _pack_grad_in(dw_in_t):
    qkv = dw_in_t[GATED + F_WIDTH:].reshape(N_HEADS, 3, HEAD_DIM, D_MODEL)
    qkv = jnp.transpose(qkv, (1, 0, 2, 3)).reshape(3 * D_MODEL, D_MODEL)
    return jnp.concatenate([dw_in_t[:2 * D_MODEL], qkv, dw_in_t[2 * D_MODEL:GATED],
                            dw_in_t[GATED:GATED + IN_ROWS - N_DEV * IN_STRIDE]], axis=0)


def kernel(x, mix_pre_g, w_in, b_forget, sgu_norm_g, w_spatial, b_spatial, w_out, mix_post_g, ffn_pre_g, w_gate, w_up, w_down, ffn_post_g, loss_target, m_mix_pre_g, m_w_in, m_b_forget, m_sgu_norm_g, m_w_spatial, m_b_spatial, m_w_out, m_mix_post_g, m_ffn_pre_g, m_w_gate, m_w_up, m_w_down, m_ffn_post_g, v_mix_pre_g, v_w_in, v_b_forget, v_sgu_norm_g, v_w_spatial, v_b_spatial, v_w_out, v_mix_post_g, v_ffn_pre_g, v_w_gate, v_w_up, v_w_down, v_ffn_post_g):
    depth = w_in.shape[0]
    assert depth == 2
    t = x.shape[1]
    x0 = x.reshape(t, D_MODEL)
    target = loss_target.reshape(t, D_MODEL)
    sel = (jnp.arange(D_MODEL)[:, None] // HEAD_DIM == jnp.arange(HEAD_DIM)[None, :]).astype(BF16)
    vec = lambda a, l: a[l][None, :]
    zero_row = jnp.zeros((1, D_MODEL), F32)

    me = 4 * lax.axis_index("x") + 2 * lax.axis_index("y") + lax.axis_index("c")
    p_in = [lax.dynamic_update_slice(jnp.zeros((P_IN, D_MODEL), BF16), w_in[l].T.astype(BF16), (me, 0))
            for l in range(depth)]
    p_rest = [[w_out[l].astype(BF16), jnp.stack([w_gate[l].T, w_up[l].T]).astype(BF16), w_down[l].astype(BF16)]
              for l in range(depth)]
    w_in_t = [None] * depth
    w_rest = [None] * depth
    w_in_t[0] = _unpack_in(_exchange(("gather", 0), p_in[0], "gather_w_in_0")[0])

    saved = []
    xl = x0
    h = _rms_fwd(xl, vec(mix_pre_g, 0), "rms_in_0")
    dy = loss = None
    for l in range(depth):
        b_pad = jnp.pad(b_forget[l], (0, HEAD_DIM - N_HEADS))[None, :]
        b_cols = b_spatial[l].T
        proj = _matmul(h, w_in_t[l], "nt", F32, f"proj_{l}")
        c_pad = _forget_cumsum(proj, b_pad, f"forget_cumsum_{l}")
        c_row = c_pad[:, :N_HEADS].T[:, None, :]
        riders = [(("gather", 0), p_rest[l][0]), (("gather", 1), p_rest[l][1]), (("gather", 0), p_rest[l][2])]
        riders += [(("gather", 0), p_in[l + 1])] if l + 1 < depth else []
        y_b, e_row, *arrived = _attn_fwd_t(proj, c_pad, c_row, f"attn_fwd_{l}", comm=riders)
        w_rest[l] = (arrived[0].reshape(D_MODEL, D_MODEL), arrived[1].reshape(2 * D_FF, D_MODEL),
                     arrived[2].reshape(D_FF, D_MODEL))
        if l + 1 < depth:
            w_in_t[l + 1] = _unpack_in(arrived[3])
        w_o, w_gu_t, w_d = w_rest[l]
        merged = _mix_fwd(proj, y_b, w_spatial[l], b_cols, vec(sgu_norm_g, l), f"mix_fwd_{l}")
        o = _matmul(merged, w_o, "nn", F32, f"out_proj_{l}")
        x1, h2 = _post_norm_fwd(xl, o, vec(mix_post_g, l), vec(ffn_pre_g, l), f"post_mix_{l}")
        gu = _matmul(h2, w_gu_t, "nt", BF16, f"ffn_gu_{l}")
        act = _swiglu_fwd(gu, f"swiglu_fwd_{l}")
        dn = _matmul(act, w_d, "nn", F32, f"ffn_down_{l}")
        saved.append(dict(x0=xl, h=h, proj=proj, b_pad=b_pad, b_cols=b_cols, c_pad=c_pad, y_b=y_b, e_row=e_row,
                          merged=merged, o=o, x1=x1, h2=h2, gu=gu, act=act, dn=dn))
        if l + 1 < depth:
            xl, h = _post_norm_fwd(x1, dn, vec(ffn_post_g, l), vec(mix_pre_g, l + 1), f"post_ffn_{l}")
        else:
            dy, loss = _post_norm_loss(x1, dn, vec(ffn_post_g, l), target, "loss")

    g_small = [None] * depth
    flights = {}
    late_token = None
    dx = dy
    dd, dg_ffn_post = _norm_bwd(dx, f"bwd_ffn_post_{depth - 1}", post=(saved[-1]["dn"], vec(ffn_post_g, depth - 1)))
    for l in reversed(range(depth)):
        s = saved[l]
        w_o, w_gu_t, w_d = w_rest[l]
        dact = _matmul(dd, w_d, "nt", BF16, f"d_act_{l}")
        dw_d = _matmul(s["act"], dd, "tn", BF16, f"dw_down_{l}")
        dgu = _swiglu_bwd(s["gu"], dact, f"swiglu_bwd_{l}")
        dh2 = _matmul(dgu, w_gu_t, "nn", F32, f"d_h2_{l}")
        dw_gu_t = _matmul(dgu, s["h2"], "tn", BF16, f"dw_gu_{l}")
        dx1, dg_ffn_pre, do, dg_mix_post = _norm_bwd(
            dx, f"bwd_post_mix_{l}", pre=(dh2, s["x1"], vec(ffn_pre_g, l)), post=(s["o"], vec(mix_post_g, l)))
        dmerged = _matmul(do, w_o, "nt", F32, f"d_merged_{l}")
        dw_o = _matmul(s["merged"], do, "tn", BF16, f"dw_out_{l}")
        tokens = []
        for key, dw, base, rows in [("out", dw_o, 0, SH_OUT), ("gate", dw_gu_t, 0, SH_FF), ("up", dw_gu_t, D_FF, SH_FF),
                                    ("down", dw_d, 0, SH_FF)]:
            flights[(key, l)], token = _exchange_start(("windows", base, rows, rows), dw, f"start_grad_{key}_{l}")
            tokens.append(token)
        dproj, dyb, delta_b, dw_s, dbt, dg_v = _mix_bwd(
            s["proj"], s["y_b"], dmerged, w_spatial[l], s["b_cols"], vec(sgu_norm_g, l), sel, f"mix_bwd_{l}",
            deps=tokens)
        db_s = dbt[:, :N_GROUPS].T.reshape(1, D_MODEL)
        tokens = []
        if l == 0:
            g_small[0] = jnp.concatenate([zero_row, dg_v, dg_mix_post, dg_ffn_pre, dg_ffn_post, db_s, zero_row,
                                          zero_row, dw_s.reshape(CHUNK, D_MODEL)], axis=0)
            flights["small"], token = _exchange_start(("spread",),jnp.concatenate([g_small[1], g_small[0]], axis=0),
                                                      "start_small_grads")
            tokens = [token]
        dproj, dc_key, dc_query = _attn_bwd_t(s["proj"], dyb, s["e_row"], delta_b, s["c_pad"], dproj,
                                              f"attn_bwd_{l}", deps=tokens)
        dc = jnp.pad((dc_key + dc_query)[:, 0, :].T, ((0, 0), (0, HEAD_DIM - N_HEADS)))
        dproj, db_f = _forget_bwd_t(s["proj"], s["b_pad"], dc, dproj, f"forget_bwd_{l}")
        db_f_row = jnp.pad(db_f, ((0, 0), (0, D_MODEL - HEAD_DIM)))
        dw_in_t = _matmul(dproj, s["h"], "tn", BF16, f"dw_in_{l}")
        flights[("in", l)], token = _exchange_start(("windows", 0, IN_STRIDE, P_IN), _pack_grad_in(dw_in_t),
                                                    f"start_grad_in_{l}")
        dh = _matmul(dproj, w_in_t[l], "nn", F32, f"d_h_{l}", deps=[token])
        if l > 0:
            dx, dg_mix_pre, dd, dg_ffn_post_below = _norm_bwd(
                dx1, f"bwd_pre_mix_{l}", pre=(dh, s["x0"], vec(mix_pre_g, l)),
                post=(saved[l - 1]["dn"], vec(ffn_post_g, l - 1)))
            loss_row = jnp.pad(loss, ((0, 0), (0, D_MODEL - 1)))
            g_small[l] = jnp.concatenate([dg_mix_pre, dg_v, dg_mix_post, dg_ffn_pre, dg_ffn_post, db_s, db_f_row,
                                          loss_row, dw_s.reshape(CHUNK, D_MODEL)], axis=0)
            dg_ffn_post = dg_ffn_post_below
        else:
            dx, dg_mix_pre = _norm_bwd(dx1, f"bwd_pre_mix_{l}", pre=(dh, s["x0"], vec(mix_pre_g, l)))
            late_rows = jnp.concatenate([dg_mix_pre, db_f_row] + [zero_row] * 6, axis=0)
            flights["late"], late_token = _exchange_start(("spread",),late_rows, "start_late_small_grads")
    grad_x = dx.reshape(x.shape)

    landed, after = {}, [grad_x, late_token]
    rest = ["out", "gate", "up", "down"]
    for key in [(n, 1) for n in rest] + [("in", 1)] + [(n, 0) for n in rest] + ["small"]:
        label = key if isinstance(key, str) else f"grad_{key[0]}_{key[1]}"
        landed[key] = _exchange_wait(flights[key], after, f"wait_{label}")
        after = [landed[key]]
    small_sum = _sum_slots(landed["small"], S_ROWS, "sum_small_grads")
    sm = small_sum.reshape(depth, S_ROWS, D_MODEL)[::-1]
    loss = sm[depth - 1, 7, 0]
    g = {
        "sgu_norm_g": sm[:, 1], "mix_post_g": sm[:, 2],
        "ffn_pre_g": sm[:, 3], "ffn_post_g": sm[:, 4], "b_spatial": sm[:, 5].reshape(depth, N_GROUPS, CHUNK),
        "w_spatial": sm[:, 8:].reshape(depth, N_GROUPS, CHUNK, CHUNK),
    }
    for n in rest:
        g["w_" + n] = jnp.stack([_sum_slots(landed[(n, l)], landed[(n, l)].shape[1], f"sum_grad_{n}_{l}")
                                 for l in range(depth)])

    names = ["mix_pre_g", "w_in", "b_forget", "sgu_norm_g", "w_spatial", "b_spatial", "w_out", "mix_post_g",
             "ffn_pre_g", "w_gate", "w_up", "w_down", "ffn_post_g"]
    ws = dict(mix_pre_g=mix_pre_g, w_in=w_in, b_forget=b_forget, sgu_norm_g=sgu_norm_g, w_spatial=w_spatial,
              b_spatial=b_spatial, w_out=w_out, mix_post_g=mix_post_g, ffn_pre_g=ffn_pre_g, w_gate=w_gate, w_up=w_up,
              w_down=w_down, ffn_post_g=ffn_post_g)
    ms = dict(mix_pre_g=m_mix_pre_g, w_in=m_w_in, b_forget=m_b_forget, sgu_norm_g=m_sgu_norm_g, w_spatial=m_w_spatial,
              b_spatial=m_b_spatial, w_out=m_w_out, mix_post_g=m_mix_post_g, ffn_pre_g=m_ffn_pre_g, w_gate=m_w_gate,
              w_up=m_w_up, w_down=m_w_down, ffn_post_g=m_ffn_post_g)
    vs = dict(mix_pre_g=v_mix_pre_g, w_in=v_w_in, b_forget=v_b_forget, sgu_norm_g=v_sgu_norm_g, w_spatial=v_w_spatial,
              b_spatial=v_b_spatial, w_out=v_w_out, mix_post_g=v_mix_post_g, ffn_pre_g=v_ffn_pre_g, w_gate=v_w_gate,
              w_up=v_w_up, w_down=v_w_down, ffn_post_g=v_ffn_post_g)
    updates = {}
    last = ["w_in", "mix_pre_g", "b_forget"]
    transposed = ["w_in", "w_gate", "w_up"]
    for n in [n for n in names if n not in last] + last:
        if n == last[0]:
            sum_in_1 = _sum_slots(landed[("in", 1)], 304, "sum_grad_in_1")
            done = [u[0] for u in updates.values()] + [sum_in_1]
            sum_in_0 = _sum_slots(_exchange_wait(flights[("in", 0)], done, "wait_grad_in_0"), 304, "sum_grad_in_0")
            late = _sum_slots(_exchange_wait(flights["late"], [sum_in_0], "wait_late"), 8, "sum_late_small_grads")
            g["w_in"] = lax.dynamic_slice(jnp.stack([sum_in_0, sum_in_1]), (0, me, 0), (depth, SH_IN, D_MODEL))
            g["mix_pre_g"] = jnp.stack([late[0], sm[1, 0]])
            g["b_forget"] = jnp.stack([late[1, :N_HEADS], sm[1, 6, :N_HEADS]])
        if n in transposed:
            view = lambda a: jnp.swapaxes(a, 1, 2)
            updates[n] = [view(u) for u in _adamw_cols(view(ws[n]), g[n], view(ms[n]), view(vs[n]), f"adamw_{n}")]
            g[n] = view(g[n])
        else:
            flat = (-1, ws[n].shape[-1])
            updates[n] = [u.reshape(ws[n].shape) for u in
                          _adamw(ws[n].reshape(flat), g[n].reshape(flat), ms[n].reshape(flat), vs[n].reshape(flat),
                                 f"adamw_{n}")]
    deltas = [updates[n][0] for n in names]
    new_m = [updates[n][1] for n in names]
    new_v = [updates[n][2] for n in names]
    grads = [g[n].reshape(ws[n].shape) for n in names]
    return (loss, grad_x, *grads, *deltas, *new_m, *new_v)
```

```python
import math

import jax
import jax.numpy as jnp
from jax import lax
from jax.experimental import pallas as pl
from jax.experimental.pallas import tpu as pltpu

F32 = jnp.float32
BF16 = jnp.bfloat16

N_DEV = 8
D_MODEL = 1024
N_HEADS = 8
HEAD_DIM = 128
CHUNK = 128
N_GROUPS = 8
D_FF = 2816
IN_WIDTH = 7 * D_MODEL + N_HEADS
IN_PAD = 7680
EPS = 1e-6
GATED = 4 * D_MODEL
F_WIDTH = 512
QKV_WIDTH = 3 * HEAD_DIM
CB_F = GATED // HEAD_DIM
CB_QKV = (GATED + F_WIDTH) // HEAD_DIM
assert GATED + F_WIDTH + N_HEADS * QKV_WIDTH == IN_PAD and (GATED + F_WIDTH) % QKV_WIDTH == 0

ADAM_LR, ADAM_B1, ADAM_B2, ADAM_EPS, ADAM_WD, ADAM_STEP = 0.001, 0.9, 0.999, 1e-08, 0.01, 10

SH_IN = IN_WIDTH // N_DEV
SH_OUT = D_MODEL // N_DEV
SH_FF = D_FF // N_DEV
P_IN = 912
IN_STRIDE = 7 * HEAD_DIM
IN_ROWS = IN_STRIDE * (N_DEV - 1) + P_IN
S_ROWS = 136

ATTN_BLOCK = 1024
ROW_BLOCK = 512
MIX_ROWS = 512
MIX_BWD_ROWS = 256
MM_TM, MM_TN, MM_TK = 1536, 1536, 2048
VMEM_CAP = 56 << 20
BIG_OPERAND = 1 << 20
NEG = -1e30


def _tile(n, cap, unit=128):
    if n <= cap:
        return n
    best = None
    for t in range(unit, cap + 1, unit):
        if n % t == 0:
            best = t
    assert best is not None, (n, cap)
    return best


def _nbytes(shape, dtype):
    return math.prod(shape) * jnp.dtype(dtype).itemsize


_HBM = pl.BlockSpec(memory_space=pltpu.HBM)
MESH = pl.DeviceIdType.MESH

_N_REMOTE = N_DEV - 1


def _exchange_shapes(kind, x):
    if kind[0] == "gather":
        shape = x.shape[:kind[1]] + (N_DEV,) + x.shape[kind[1]:]
    elif kind[0] == "spread":
        shape = (N_DEV,) + x.shape
    else:
        shape = (N_DEV, kind[3]) + x.shape[1:]
    return [jax.ShapeDtypeStruct(shape, x.dtype)]


def _exchange_sems(kind):
    del kind
    return [pltpu.SemaphoreType.DMA((_N_REMOTE,)), pltpu.SemaphoreType.DMA((_N_REMOTE,)), pltpu.SemaphoreType.DMA]


def _plan(kind, x_ref, outs, sems):
    send_sems, recv_sems, local_sem = sems
    x, y, c = lax.axis_index("x"), lax.axis_index("y"), lax.axis_index("c")

    def remote(src, dst, k, to):
        return pltpu.make_async_remote_copy(src_ref=src, dst_ref=dst, send_sem=send_sems.at[k], recv_sem=recv_sems.at[k],
                                            device_id=to, device_id_type=MESH)

    sibling = (x, y, 1 - c)
    if kind[0] == "gather":
        out, = outs
        slot = lambda px, py, pc: out.at[(slice(None),) * kind[1] + (4 * px + 2 * py + pc,)]
        chips = [(1 - x, y), (x, 1 - y), (1 - x, 1 - y)]
        local = [pltpu.make_async_copy(x_ref, slot(x, y, c), local_sem)]
        first = [remote(x_ref, slot(x, y, c), 0, sibling)]
        first += [remote(x_ref, slot(x, y, c), 1 + k, (*chip, c)) for k, chip in enumerate(chips)]
        relays = [(remote(x_ref, slot(*chip, c), 1 + k, (*chip, c)), remote(slot(*chip, c), slot(*chip, c), 4 + k, sibling))
                  for k, chip in enumerate(chips)]
        arrivals = [remote(x_ref, slot(x, y, 1 - c), 0, sibling)]
        arrivals += [remote(x_ref, slot(*chip, 1 - c), 4 + k, sibling) for k, chip in enumerate(chips)]
        return local, first, relays, arrivals
    out, = outs
    mine = 4 * x + 2 * y + c
    if kind[0] == "spread":
        src = lambda d: x_ref
    else:
        _, base, stride, length = kind
        src = lambda d: x_ref.at[pl.ds(pl.multiple_of(base + stride * d, 16), length)]
    first, arrivals = [], []
    for j in range(1, N_DEV):
        peer = (1 - x if j & 4 else x, 1 - y if j & 2 else y, 1 - c if j & 1 else c)
        theirs = 4 * peer[0] + 2 * peer[1] + peer[2]
        first.append(remote(src(theirs), out.at[mine], j - 1, peer))
        arrivals.append(remote(src(theirs), out.at[theirs], j - 1, peer))
    local = [] if local_sem is None else [pltpu.make_async_copy(src(mine), out.at[mine], local_sem)]
    return local, first, [], arrivals


def _start(plan):
    local, first, _, _ = plan
    for cp in local + first:
        cp.start()


def _finish(plan):
    local, first, relays, arrivals = plan
    for arrival, onward in relays:
        arrival.wait_recv()
        onward.start()
    for cp in arrivals:
        cp.wait_recv()
    for cp in first + [onward for _, onward in relays]:
        cp.wait_send()
    for cp in local:
        cp.wait()


def _direct_copies(kind, x_ref, land_ref, send_sems, recv_sems):
    _, first, _, arrivals = _plan(kind, x_ref, [land_ref], (send_sems, recv_sems, None))
    return first, arrivals


_SEM = pl.BlockSpec(memory_space=pltpu.SEMAPHORE)
_DATAFLOW = pltpu.SideEffectType.DATAFLOW_SIDE_EFFECTING


def _exchange_start(kind, x, name):
    me = 4 * lax.axis_index("x") + 2 * lax.axis_index("y") + lax.axis_index("c")
    own = x if kind[0] == "spread" else lax.dynamic_slice_in_dim(x, kind[1] + kind[2] * me, kind[3], 0)
    shape = _exchange_shapes(kind, x)[0].shape
    land = lax.dynamic_update_slice_in_dim(lax.empty(shape, x.dtype), own[None], me, 0)

    def body(x_ref, land_ref, send_sems, recv_sems, x_thru, land_thru, token):
        del x_thru, land_thru
        for cp in _direct_copies(kind, x_ref, land_ref, send_sems, recv_sems)[0]:
            cp.start()
        token[...] = jnp.zeros_like(token)

    *handle, token = pl.pallas_call(
        body, name=name,
        out_shape=(pltpu.SemaphoreType.DMA((_N_REMOTE,)), pltpu.SemaphoreType.DMA((_N_REMOTE,)),
                   pltpu.HBM(x.shape, x.dtype), pltpu.HBM(land.shape, land.dtype), jax.ShapeDtypeStruct((8, 128), F32)),
        in_specs=(_HBM, _HBM), out_specs=(_SEM, _SEM, _HBM, _HBM, pl.BlockSpec(memory_space=pltpu.VMEM)),
        input_output_aliases={0: 2, 1: 3}, compiler_params=pltpu.CompilerParams(has_side_effects=_DATAFLOW),
    )(pltpu.with_memory_space_constraint(x, pltpu.HBM), pltpu.with_memory_space_constraint(land, pltpu.HBM))
    return (kind, *handle), token


def _exchange_wait(handle, after, name):
    kind, send_sems, recv_sems, x_thru, land_thru = handle

    def body(x_ref, land_ref, send_sems, recv_sems, *rest):
        first, arrivals = _direct_copies(kind, x_ref, land_ref, send_sems, recv_sems)
        for cp in first:
            cp.wait_send()
        for cp in arrivals:
            cp.wait_recv()

    return pl.pallas_call(
        body, name=name,
        out_shape=(pltpu.HBM(x_thru.shape, x_thru.dtype), pltpu.HBM(land_thru.shape, land_thru.dtype)),
        in_specs=(_HBM, _HBM, _SEM, _SEM) + (pl.BlockSpec(memory_space=pl.ANY),) * len(after), out_specs=(_HBM, _HBM),
        input_output_aliases={0: 0, 1: 1}, compiler_params=pltpu.CompilerParams(has_side_effects=_DATAFLOW),
    )(x_thru, land_thru, send_sems, recv_sems, *after)[1]


def _exchange(kind, x, name):
    n_out = len(_exchange_shapes(kind, x))

    def body(x_ref, *refs):
        plan = _plan(kind, x_ref, refs[:n_out], refs[n_out:])
        _start(plan)
        _finish(plan)

    return pl.pallas_call(body, name=name, out_shape=_exchange_shapes(kind, x), in_specs=[_HBM],
                          out_specs=[_HBM] * n_out, scratch_shapes=_exchange_sems(kind))(x)


def _call(body, name, grid, in_specs, out_specs, out_shape, args, scratch=(), sem=None, nbytes=0, aliases=None,
          comm=(), deps=()):
    in_specs, out_specs, out_shape, args, scratch = (list(in_specs), list(out_specs), list(out_shape), list(args),
                                                     list(scratch))
    if deps:
        n_real, n_deps, unordered = len(args), len(deps), body

        def body(*refs):
            unordered(*refs[:n_real], *refs[n_real + n_deps:])

        in_specs += [pl.BlockSpec(memory_space=pl.ANY)] * n_deps
        args += list(deps)
    if comm:
        n_in, n_out, n_scr, n_ops = len(args), len(out_shape), len(scratch), len(comm)
        kinds = [kind for kind, _ in comm]
        shapes = [_exchange_shapes(kind, x) for kind, x in comm]
        inner = body

        def body(*refs):
            ins, refs = refs[:n_in], refs[n_in:]
            cins, refs = refs[:n_ops], refs[n_ops:]
            outs, refs = refs[:n_out], refs[n_out:]
            couts = []
            for sh in shapes:
                couts.append(refs[:len(sh)])
                refs = refs[len(sh):]
            own_scratch, sems = refs[:n_scr], refs[n_scr:]
            first = last = None
            for axis, size in enumerate(grid):
                at_start, at_end = pl.program_id(axis) == 0, pl.program_id(axis) == size - 1
                first = at_start if first is None else first & at_start
                last = at_end if last is None else last & at_end
            plans = [_plan(kinds[o], cins[o], couts[o], sems[3 * o:3 * o + 3]) for o in range(n_ops)]

            @pl.when(first)
            def _():
                for p in plans:
                    _start(p)

            inner(*ins, *outs, *own_scratch)

            @pl.when(last)
            def _():
                for p in plans:
                    _finish(p)

        in_specs += [_HBM] * n_ops
        args += [x for _, x in comm]
        for kind, sh in zip(kinds, shapes):
            out_shape += sh
            out_specs += [_HBM] * len(sh)
            scratch += _exchange_sems(kind)
        sem = ("arbitrary",) * len(grid)
    limit = int(min(max(2 * nbytes + (8 << 20), 32 << 20), VMEM_CAP))
    return pl.pallas_call(
        body, name=name, grid=grid, in_specs=in_specs, out_specs=out_specs, out_shape=out_shape,
        scratch_shapes=scratch, input_output_aliases=aliases or {},
        compiler_params=pltpu.CompilerParams(dimension_semantics=sem, vmem_limit_bytes=limit),
    )(*[pltpu.with_memory_space_constraint(a, pltpu.HBM) if a.size * a.dtype.itemsize >= BIG_OPERAND else a
        for a in args])


def _sum_slots(x, tr, name):
    n, r, c = x.shape

    def body(x_ref, o_ref):
        acc = x_ref[0].astype(F32)
        for d in range(1, n):
            acc = acc + x_ref[d].astype(F32)
        o_ref[...] = acc

    return _call(body, name, (r // tr,), [pl.BlockSpec((n, tr, c), lambda i: (0, i, 0))],
                 [pl.BlockSpec((tr, c), lambda i: (i, 0))], [jax.ShapeDtypeStruct((r, c), F32)], [x],
                 sem=("parallel",), nbytes=_nbytes((n, tr, c), x.dtype) + _nbytes((tr, c), F32))[0]


_DIMS = {"nn": (((1,), (0,)), ((), ())), "nt": (((1,), (1,)), ((), ())), "tn": (((0,), (0,)), ((), ()))}


def _matmul(a, b, mode, out_dtype, name, comm=(), deps=()):
    if mode == "nn":
        (m, k), (k2, n) = a.shape, b.shape
    elif mode == "nt":
        (m, k), (n, k2) = a.shape, b.shape
    else:
        (k, m), (k2, n) = a.shape, b.shape
    assert k == k2, (a.shape, b.shape, mode)
    tm, tn, tk = _tile(m, MM_TM), _tile(n, MM_TN), _tile(k, MM_TK)
    nk = k // tk
    dims = _DIMS[mode]
    a_spec = {"nn": pl.BlockSpec((tm, tk), lambda i, j, kk: (i, kk)),
              "nt": pl.BlockSpec((tm, tk), lambda i, j, kk: (i, kk)),
              "tn": pl.BlockSpec((tk, tm), lambda i, j, kk: (kk, i))}[mode]
    b_spec = {"nn": pl.BlockSpec((tk, tn), lambda i, j, kk: (kk, j)),
              "nt": pl.BlockSpec((tn, tk), lambda i, j, kk: (j, kk)),
              "tn": pl.BlockSpec((tk, tn), lambda i, j, kk: (kk, j))}[mode]

    def partial_product(a_ref, b_ref):
        return lax.dot_general(a_ref[...].astype(BF16), b_ref[...].astype(BF16), dims, preferred_element_type=F32)

    if nk == 1:
        def body(a_ref, b_ref, o_ref):
            o_ref[...] = partial_product(a_ref, b_ref).astype(o_ref.dtype)
        scratch = []
    else:
        def body(a_ref, b_ref, o_ref, acc_ref):
            kk = pl.program_id(2)

            @pl.when(kk == 0)
            def _():
                acc_ref[...] = jnp.zeros_like(acc_ref)

            acc_ref[...] += partial_product(a_ref, b_ref)

            @pl.when(kk == nk - 1)
            def _():
                o_ref[...] = acc_ref[...].astype(o_ref.dtype)
        scratch = [pltpu.VMEM((tm, tn), F32)]

    nbytes = (_nbytes((tm, tk), a.dtype) + _nbytes((tk, tn), b.dtype) + _nbytes((tm, tn), out_dtype)
              + _nbytes((tm, tn), F32))
    res = _call(body, name, (m // tm, n // tn, nk), [a_spec, b_spec],
                [pl.BlockSpec((tm, tn), lambda i, j, kk: (i, j))], [jax.ShapeDtypeStruct((m, n), out_dtype)], [a, b],
                scratch=scratch, sem=("parallel", "parallel", "arbitrary"), nbytes=nbytes, comm=comm, deps=deps)
    return res if comm else res[0]


def _rms(x):
    return lax.rsqrt(jnp.mean(x * x, axis=-1, keepdims=True) + EPS)


def _rms_bwd(dz, a, g):
    r = _rms(a)
    dzg = dz * g
    da = r * dzg - a * (r * r * r) * jnp.mean(dzg * a, axis=-1, keepdims=True)
    return da, dz * (a * r)


def _row_spec(tr, width):
    return pl.BlockSpec((tr, width), lambda i: (i, 0))


def _vec_spec(width):
    return pl.BlockSpec((1, width), lambda i: (0, 0))


def _rms_fwd(x, g, name):
    t, d = x.shape
    tr = min(ROW_BLOCK, t)

    def body(x_ref, g_ref, h_ref):
        xv = x_ref[...]
        h_ref[...] = ((xv * _rms(xv)) * g_ref[...]).astype(BF16)

    return _call(body, name, (t // tr,), [_row_spec(tr, d), _vec_spec(d)], [_row_spec(tr, d)],
                 [jax.ShapeDtypeStruct((t, d), BF16)], [x, g], sem=("parallel",), nbytes=3 * _nbytes((tr, d), F32))[0]


def _post_norm_fwd(x, o, g_post, g_next, name):
    t, d = x.shape
    tr = min(ROW_BLOCK, t)

    def body(x_ref, o_ref, gp_ref, gn_ref, xn_ref, h_ref):
        ov = o_ref[...]
        xn = x_ref[...] + (ov * _rms(ov)) * gp_ref[...]
        xn_ref[...] = xn
        h_ref[...] = ((xn * _rms(xn)) * gn_ref[...]).astype(BF16)

    return _call(body, name, (t // tr,), [_row_spec(tr, d), _row_spec(tr, d), _vec_spec(d), _vec_spec(d)],
                 [_row_spec(tr, d), _row_spec(tr, d)],
                 [jax.ShapeDtypeStruct((t, d), F32), jax.ShapeDtypeStruct((t, d), BF16)], [x, o, g_post, g_next],
                 sem=("parallel",), nbytes=5 * _nbytes((tr, d), F32))


def _post_norm_loss(x, o, g_post, target, name):
    t, d = x.shape
    tr = min(ROW_BLOCK, t)

    def body(x_ref, o_ref, gp_ref, t_ref, dy_ref, loss_ref):
        ov = o_ref[...]
        err = x_ref[...] + (ov * _rms(ov)) * gp_ref[...] - t_ref[...]
        dy_ref[...] = err / d
        part = 0.5 * jnp.sum(jnp.mean(err * err, axis=-1, keepdims=True), axis=0, keepdims=True)

        @pl.when(pl.program_id(0) == 0)
        def _():
            loss_ref[...] = jnp.zeros_like(loss_ref)

        loss_ref[...] += part

    return _call(body, name, (t // tr,), [_row_spec(tr, d), _row_spec(tr, d), _vec_spec(d), _row_spec(tr, d)],
                 [_row_spec(tr, d), pl.BlockSpec((1, 1), lambda i: (0, 0))],
                 [jax.ShapeDtypeStruct((t, d), F32), jax.ShapeDtypeStruct((1, 1), F32)], [x, o, g_post, target],
                 sem=("arbitrary",), nbytes=5 * _nbytes((tr, d), F32))


def _norm_bwd(dskip, name, pre=None, post=None):
    t, d = dskip.shape
    tr = min(ROW_BLOCK, t)
    n_in = 1 + (3 if pre else 0) + (2 if post else 0)

    def body(*refs):
        ins, outs = list(refs[:n_in]), list(refs[n_in:])
        first = pl.program_id(0) == 0
        dx = ins.pop(0)[...]
        if pre:
            dh_ref, xin_ref, gpre_ref = ins.pop(0), ins.pop(0), ins.pop(0)
            dxin, dg_rows = _rms_bwd(dh_ref[...].astype(F32), xin_ref[...], gpre_ref[...])
            dx = dx + dxin
            dx_ref, dgpre_ref = outs.pop(0), outs.pop(0)
            dx_ref[...] = dx

            @pl.when(first)
            def _():
                dgpre_ref[...] = jnp.zeros_like(dgpre_ref)

            dgpre_ref[...] += jnp.sum(dg_rows, axis=0, keepdims=True)
        if post:
            a_ref, gpost_ref = ins.pop(0), ins.pop(0)
            da, dg_rows = _rms_bwd(dx, a_ref[...], gpost_ref[...])
            da_ref, dgpost_ref = outs.pop(0), outs.pop(0)
            da_ref[...] = da.astype(BF16)

            @pl.when(first)
            def _():
                dgpost_ref[...] = jnp.zeros_like(dgpost_ref)

            dgpost_ref[...] += jnp.sum(dg_rows, axis=0, keepdims=True)

    args, in_specs, out_shape, out_specs = [dskip], [_row_spec(tr, d)], [], []
    if pre:
        args += list(pre)
        in_specs += [_row_spec(tr, d), _row_spec(tr, d), _vec_spec(d)]
        out_shape += [jax.ShapeDtypeStruct((t, d), F32), jax.ShapeDtypeStruct((1, d), F32)]
        out_specs += [_row_spec(tr, d), _vec_spec(d)]
    if post:
        args += list(post)
        in_specs += [_row_spec(tr, d), _vec_spec(d)]
        out_shape += [jax.ShapeDtypeStruct((t, d), BF16), jax.ShapeDtypeStruct((1, d), F32)]
        out_specs += [_row_spec(tr, d), _vec_spec(d)]
    return _call(body, name, (t // tr,), in_specs, out_specs, out_shape, args, sem=("arbitrary",),
                 nbytes=7 * _nbytes((tr, d), F32))


def _sigmoid(x):
    return 1.0 / (1.0 + jnp.exp(-x))


def _swiglu_fwd(gu, name):
    t = gu.shape[0]
    tr = min(ROW_BLOCK, t)

    def body(gu_ref, a_ref):
        gate, up = gu_ref[:, :D_FF].astype(F32), gu_ref[:, D_FF:].astype(F32)
        a_ref[...] = ((gate * _sigmoid(gate)) * up).astype(BF16)

    return _call(body, name, (t // tr,), [_row_spec(tr, 2 * D_FF)], [_row_spec(tr, D_FF)],
                 [jax.ShapeDtypeStruct((t, D_FF), BF16)], [gu], sem=("parallel",),
                 nbytes=2 * _nbytes((tr, 2 * D_FF), F32))[0]


def _swiglu_bwd(gu, dact, name):
    t = gu.shape[0]
    tr = min(ROW_BLOCK, t)

    def body(gu_ref, da_ref, o_ref):
        gate, up = gu_ref[:, :D_FF].astype(F32), gu_ref[:, D_FF:].astype(F32)
        da = da_ref[...].astype(F32)
        s = _sigmoid(gate)
        o_ref[:, :D_FF] = (da * up * (s * (1.0 + gate * (1.0 - s)))).astype(BF16)
        o_ref[:, D_FF:] = (da * (gate * s)).astype(BF16)

    return _call(body, name, (t // tr,), [_row_spec(tr, 2 * D_FF), _row_spec(tr, D_FF)], [_row_spec(tr, 2 * D_FF)],
                 [jax.ShapeDtypeStruct((t, 2 * D_FF), BF16)], [gu, dact], sem=("parallel",),
                 nbytes=3 * _nbytes((tr, 2 * D_FF), F32))[0]


def _split3(x):
    hi = x.astype(BF16)
    r1 = x - hi.astype(F32)
    mid = r1.astype(BF16)
    lo = (r1 - mid.astype(F32)).astype(BF16)
    return hi, mid, lo


def _dot_exact(mat01, x):
    hi, mid, lo = _split3(x)
    out = jnp.dot(mat01, hi, preferred_element_type=F32)
    out = out + jnp.dot(mat01, mid, preferred_element_type=F32)
    return out + jnp.dot(mat01, lo, preferred_element_type=F32)


def _dot_exact_rhs(x, mat01):
    hi, mid, lo = _split3(x)
    out = jnp.dot(hi, mat01, preferred_element_type=F32)
    out = out + jnp.dot(mid, mat01, preferred_element_type=F32)
    return out + jnp.dot(lo, mat01, preferred_element_type=F32)


def _tri(lower):
    r = lax.broadcasted_iota(jnp.int32, (CHUNK, CHUNK), 0)
    c = lax.broadcasted_iota(jnp.int32, (CHUNK, CHUNK), 1)
    return jnp.where(r >= c if lower else r <= c, 1.0, 0.0).astype(BF16)


def _log_sigmoid(z):
    return jnp.minimum(z, 0.0) - jnp.log(1.0 + jnp.exp(-jnp.abs(z)))


def _forget_cumsum(proj, b_pad, name):
    t = proj.shape[0]
    nb = t // CHUNK

    def body(f_ref, b_ref, c_ref):
        tri = _tri(True)
        b = b_ref[...]

        def blk(i, carry):
            rows = pl.ds(pl.multiple_of(i * CHUNK, CHUNK), CHUNK)
            cs = _dot_exact(tri, _log_sigmoid(f_ref[rows, :] + b)) + carry
            c_ref[rows, :] = cs
            return cs[CHUNK - 1:CHUNK, :]

        lax.fori_loop(0, nb, blk, jnp.zeros((1, HEAD_DIM), F32))

    return _call(body, name, (1,),
                 [pl.BlockSpec((t, HEAD_DIM), lambda i: (0, CB_F)), pl.BlockSpec((1, HEAD_DIM), lambda i: (0, 0))],
                 [pl.BlockSpec((t, HEAD_DIM), lambda i: (0, 0))], [jax.ShapeDtypeStruct((t, HEAD_DIM), F32)],
                 [proj, b_pad], sem=("arbitrary",), nbytes=2 * _nbytes((t, HEAD_DIM), F32))[0]


def _causal(block):
    r = lax.broadcasted_iota(jnp.int32, (block, block), 0)
    c = lax.broadcasted_iota(jnp.int32, (block, block), 1)
    return c <= r


def _lanes(x, width):
    return jnp.concatenate([x] * (width // HEAD_DIM), axis=1)


def _chunks(t, blk):
    return [slice(r * blk, (r + 1) * blk) for r in range(t // blk)]


def _lane_of_head(x, h):
    lane = lax.broadcasted_iota(jnp.int32, x.shape, 1)
    return jnp.broadcast_to(jnp.sum(jnp.where(lane == h, x, 0.0), axis=1, keepdims=True), x.shape)


def _causal_t(block):
    r = lax.broadcasted_iota(jnp.int32, (block, block), 0)
    c = lax.broadcasted_iota(jnp.int32, (block, block), 1)
    return r <= c


def _attn_fwd_t(proj, c_pad, c_row, name, comm=()):
    t = proj.shape[0]
    blk = min(ATTN_BLOCK, t)
    nq = t // blk
    scale = HEAD_DIM ** -0.5

    def body(q_ref, k_ref, v_ref, cp_ref, cr_ref, y_ref, e_ref, kb_s, vt_s, cb_s):
        h, i = pl.program_id(0), pl.program_id(1)

        @pl.when(i == 0)
        def _():
            for rows in _chunks(t, blk):
                kb_s[rows, :] = k_ref[rows, :].astype(BF16)
                vt_s[:, rows] = v_ref[rows, :].T.astype(BF16)
                cb_s[rows, :] = _lane_of_head(cp_ref[rows, :], h)

        q = (q_ref[...] * scale).astype(BF16)
        ci = cr_ref[:, pl.ds(pl.multiple_of(i * blk, blk), blk)]

        def step(j, carry, diagonal):
            m, l, acc = carry
            rows = pl.ds(pl.multiple_of(j * blk, blk), blk)
            s = lax.dot_general(kb_s[rows, :], q, _DIMS["nt"], preferred_element_type=F32)
            s = s + ci - _lanes(cb_s[rows, :], blk)
            if diagonal:
                s = jnp.where(_causal_t(blk), s, NEG)
            m_new = jnp.maximum(m, jnp.max(s, axis=0, keepdims=True))
            alpha = jnp.exp(m - m_new)
            p = jnp.exp(s - m_new)
            l = alpha * l + jnp.sum(p, axis=0, keepdims=True)
            acc = alpha * acc + jnp.dot(vt_s[:, rows], p.astype(BF16), preferred_element_type=F32)
            return m_new, l, acc

        init = (jnp.full((1, blk), NEG, F32), jnp.zeros((1, blk), F32), jnp.zeros((HEAD_DIM, blk), F32))
        carry = lax.fori_loop(0, i, lambda j, c: step(j, c, False), init)
        m, l, acc = step(i, carry, True)
        y_ref[...] = (acc / l).T
        e_ref[...] = ci - (m + jnp.log(l))

    nbytes = 3 * _nbytes((t, HEAD_DIM), F32) + 4 * _nbytes((blk, HEAD_DIM), F32) + 4 * _nbytes((blk, blk), F32)
    return _call(body, name, (N_HEADS, nq),
                 [pl.BlockSpec((blk, HEAD_DIM), lambda h, i: (i, CB_QKV + 3 * h)),
                  pl.BlockSpec((t, HEAD_DIM), lambda h, i: (0, CB_QKV + 3 * h + 1)),
                  pl.BlockSpec((t, HEAD_DIM), lambda h, i: (0, CB_QKV + 3 * h + 2)),
                  pl.BlockSpec((t, HEAD_DIM), lambda h, i: (0, 0)),
                  pl.BlockSpec((None, 1, t), lambda h, i: (h, 0, 0))],
                 [pl.BlockSpec((blk, HEAD_DIM), lambda h, i: (i, h)),
                  pl.BlockSpec((None, 1, blk), lambda h, i: (h, 0, i))],
                 [jax.ShapeDtypeStruct((t, D_MODEL), F32), jax.ShapeDtypeStruct((N_HEADS, 1, t), F32)],
                 [proj, proj, proj, c_pad, c_row],
                 scratch=[pltpu.VMEM((t, HEAD_DIM), BF16), pltpu.VMEM((HEAD_DIM, t), BF16),
                          pltpu.VMEM((t, HEAD_DIM), F32)],
                 sem=("parallel", "arbitrary"), nbytes=nbytes, comm=comm)


def _attn_bwd_t(proj, dy, e_row, delta_b, c_pad, dproj, name, deps=()):
    t = proj.shape[0]
    blk = min(ATTN_BLOCK, t)
    nb = t // blk
    scale = HEAD_DIM ** -0.5

    def body(q_ref, k_ref, v_ref, dy_ref, e_ref, dl_ref, cp_ref, dproj_in, dqkv_ref, dck_ref, dcq_ref,
             qb_s, qt_s, dyb_s, dyt_s, dl_s, dqt_acc):
        del dproj_in
        h, j = pl.program_id(0), pl.program_id(1)

        @pl.when(j == 0)
        def _():
            for rows in _chunks(t, blk):
                qs = q_ref[rows, :] * scale
                qb_s[rows, :] = qs.astype(BF16)
                qt_s[:, rows] = qs.T.astype(BF16)
                dyr = dy_ref[rows, :]
                dyb_s[rows, :] = dyr.astype(BF16)
                dyt_s[:, rows] = dyr.T.astype(BF16)
                dl_s[:, rows] = dl_ref[rows, :].T[0:1, :]
            dqt_acc[...] = jnp.zeros_like(dqt_acc)
            dcq_ref[...] = jnp.zeros_like(dcq_ref)

        kf = k_ref[...]
        kb, kt = kf.astype(BF16), kf.T.astype(BF16)
        vb = v_ref[...].astype(BF16)
        cj = _lanes(_lane_of_head(cp_ref[...], h), blk)

        def step(i, carry, diagonal):
            dkt, dvt, dkey = carry
            cols = pl.ds(pl.multiple_of(i * blk, blk), blk)
            s = lax.dot_general(kb, qb_s[cols, :], _DIMS["nt"], preferred_element_type=F32) + e_ref[:, cols] - cj
            if diagonal:
                s = jnp.where(_causal_t(blk), s, NEG)
            p = jnp.exp(s)
            dp = lax.dot_general(vb, dyb_s[cols, :], _DIMS["nt"], preferred_element_type=F32)
            ds = p * (dp - dl_s[:, cols])
            pb, dsb = p.astype(BF16), ds.astype(BF16)
            dvt = dvt + lax.dot_general(dyt_s[:, cols], pb, _DIMS["nt"], preferred_element_type=F32)
            dkt = dkt + lax.dot_general(qt_s[:, cols], dsb, _DIMS["nt"], preferred_element_type=F32)
            dqt_acc[:, cols] += jnp.dot(kt, dsb, preferred_element_type=F32) * scale
            dcq_ref[:, cols] += jnp.sum(ds, axis=0, keepdims=True)
            for k in range(blk // HEAD_DIM):
                dkey = dkey + ds[:, k * HEAD_DIM:(k + 1) * HEAD_DIM]
            return dkt, dvt, dkey

        zeros = jnp.zeros((HEAD_DIM, blk), F32)
        carry = step(j, (zeros, zeros, jnp.zeros((blk, HEAD_DIM), F32)), True)
        dkt, dvt, dkey = lax.fori_loop(j + 1, nb, lambda i, c: step(i, c, False), carry)
        mine = pl.ds(pl.multiple_of(j * blk, blk), blk)
        dqkv_ref[:, :HEAD_DIM] = dqt_acc[:, mine].T.astype(BF16)
        dqkv_ref[:, HEAD_DIM:2 * HEAD_DIM] = dkt.T.astype(BF16)
        dqkv_ref[:, 2 * HEAD_DIM:] = dvt.T.astype(BF16)
        dck_ref[...] = -jnp.sum(dkey.T, axis=0, keepdims=True)

    full = lambda cb: pl.BlockSpec((t, HEAD_DIM), lambda h, j: (0, cb(h)))
    head = lambda h: h
    row = pl.BlockSpec((None, 1, t), lambda h, j: (h, 0, 0))
    nbytes = (8 * _nbytes((t, HEAD_DIM), F32) + 4 * _nbytes((blk, HEAD_DIM), F32) + 6 * _nbytes((blk, blk), F32))
    return _call(body, name, (N_HEADS, nb),
                 [full(lambda h: CB_QKV + 3 * h),
                  pl.BlockSpec((blk, HEAD_DIM), lambda h, j: (j, CB_QKV + 3 * h + 1)),
                  pl.BlockSpec((blk, HEAD_DIM), lambda h, j: (j, CB_QKV + 3 * h + 2)),
                  full(head), row, full(head),
                  pl.BlockSpec((blk, HEAD_DIM), lambda h, j: (j, 0)),
                  pl.BlockSpec(memory_space=pl.ANY)],
                 [pl.BlockSpec((blk, QKV_WIDTH), lambda h, j: (j, (GATED + F_WIDTH) // QKV_WIDTH + h)),
                  pl.BlockSpec((None, 1, blk), lambda h, j: (h, 0, j)),
                  row],
                 [jax.ShapeDtypeStruct(dproj.shape, dproj.dtype), jax.ShapeDtypeStruct((N_HEADS, 1, t), F32),
                  jax.ShapeDtypeStruct((N_HEADS, 1, t), F32)],
                 [proj, proj, proj, dy, e_row, delta_b, c_pad, dproj],
                 scratch=[pltpu.VMEM((t, HEAD_DIM), BF16), pltpu.VMEM((HEAD_DIM, t), BF16),
                          pltpu.VMEM((t, HEAD_DIM), BF16), pltpu.VMEM((HEAD_DIM, t), BF16),
                          pltpu.VMEM((1, t), F32), pltpu.VMEM((HEAD_DIM, t), F32)],
                 sem=("parallel", "arbitrary"), nbytes=nbytes, aliases={7: 0}, deps=deps)


def _forget_bwd_t(proj, b_pad, dc, dproj, name):
    t = proj.shape[0]
    tr = min(ROW_BLOCK, t)
    nb = t // tr
    rev = lambda i: nb - 1 - i

    def body(f_ref, b_ref, dc_ref, dproj_in, df_ref, db_ref, run_ref):
        del dproj_in

        @pl.when(pl.program_id(0) == 0)
        def _():
            run_ref[...] = jnp.zeros_like(run_ref)
            db_ref[...] = jnp.zeros_like(db_ref)

        tri = _tri(False)
        b = b_ref[...]
        lane = lax.broadcasted_iota(jnp.int32, (CHUNK, HEAD_DIM), 1)
        df_ref[...] = jnp.zeros_like(df_ref)
        for n in reversed(range(tr // CHUNK)):
            rows = slice(n * CHUNK, (n + 1) * CHUNK)
            dlogf = _dot_exact(tri, dc_ref[rows, :]) + run_ref[...]
            run_ref[...] = dlogf[0:1, :]
            z = f_ref[rows, :] + b
            e = jnp.exp(-jnp.abs(z))
            sig_neg = jnp.where(z >= 0.0, e, 1.0) / (1.0 + e)
            df = jnp.where(lane < N_HEADS, dlogf * sig_neg, 0.0)
            df_ref[rows, :HEAD_DIM] = df.astype(BF16)
            db_ref[...] += jnp.sum(df, axis=0, keepdims=True)

    return _call(body, name, (nb,),
                 [pl.BlockSpec((tr, HEAD_DIM), lambda i: (rev(i), CB_F)), pl.BlockSpec((1, HEAD_DIM), lambda i: (0, 0)),
                  pl.BlockSpec((tr, HEAD_DIM), lambda i: (rev(i), 0)), pl.BlockSpec(memory_space=pl.ANY)],
                 [pl.BlockSpec((tr, F_WIDTH), lambda i: (rev(i), GATED // F_WIDTH)),
                  pl.BlockSpec((1, HEAD_DIM), lambda i: (0, 0))],
                 [jax.ShapeDtypeStruct(dproj.shape, dproj.dtype), jax.ShapeDtypeStruct((1, HEAD_DIM), F32)],
                 [proj, b_pad, dc, dproj], scratch=[pltpu.VMEM((1, HEAD_DIM), F32)],
                 sem=("arbitrary",), nbytes=4 * _nbytes((tr, HEAD_DIM), F32), aliases={3: 0})


_GELU_K = math.sqrt(2.0 / math.pi)
_GELU_C = 0.044715


def _gelu(x):
    t = jnp.tanh(_GELU_K * (x + _GELU_C * (x * x * x)))
    return 0.5 * x * (1.0 + t), t


def _gelu_grad(x, t):
    return 0.5 * (1.0 + t) + 0.5 * x * (1.0 - t * t) * (_GELU_K * (1.0 + 3.0 * _GELU_C * (x * x)))


def _layernorm_stats(a):
    mu = jnp.mean(a, axis=-1, keepdims=True)
    xc = a - mu
    r = lax.rsqrt(jnp.mean(xc * xc, axis=-1, keepdims=True) + EPS)
    return xc * r, r


def _group(g):
    return slice(g * CHUNK, (g + 1) * CHUNK)


def _fixed(shape):
    return pl.BlockSpec(shape, lambda i: (0,) * len(shape))


def _mix_fwd(proj, y_b, w_s, b_cols, g_v, name):
    t = proj.shape[0]
    step_rows = min(MIX_ROWS, t)

    def body(u_ref, vs_ref, ga_ref, gb_ref, yb_ref, w_ref, b_ref, gv_ref, o_ref):
        mask = _causal(CHUNK)
        w = [jnp.where(mask, w_ref[g], 0.0).astype(BF16) for g in range(N_GROUPS)]
        for rows in _chunks(step_rows, CHUNK):
            a_u, _ = _gelu(u_ref[rows, :])
            a_v, _ = _gelu(vs_ref[rows, :])
            xhat, _ = _layernorm_stats(a_v)
            vn = (xhat * gv_ref[...]).astype(BF16)
            sa, sb = _sigmoid(ga_ref[rows, :]), _sigmoid(gb_ref[rows, :])
            yb = yb_ref[rows, :]
            for g in range(N_GROUPS):
                cols = _group(g)
                mixed = jnp.dot(w[g], vn[:, cols], preferred_element_type=F32) + b_ref[:, g:g + 1]
                o_ref[rows, cols] = (sa[:, cols] * (a_u[:, cols] * mixed) + sb[:, cols] * yb[:, cols]).astype(BF16)

    gated = lambda k: pl.BlockSpec((step_rows, D_MODEL), lambda i: (i, k))
    return _call(body, name, (t // step_rows,),
                 [gated(0), gated(1), gated(2), gated(3), _row_spec(step_rows, D_MODEL),
                  _fixed((N_GROUPS, CHUNK, CHUNK)), _fixed((CHUNK, N_GROUPS)), _fixed((1, D_MODEL))],
                 [_row_spec(step_rows, D_MODEL)], [jax.ShapeDtypeStruct((t, D_MODEL), BF16)],
                 [proj, proj, proj, proj, y_b, w_s, b_cols, g_v], sem=("parallel",),
                 nbytes=16 * _nbytes((step_rows, D_MODEL), F32))[0]


def _mix_bwd(proj, y_b, dmerged, w_s, b_cols, g_v, sel, name, deps=()):
    t = proj.shape[0]
    step_rows = min(MIX_BWD_ROWS, t)

    def body(u_ref, vs_ref, ga_ref, gb_ref, yb_ref, dm_ref, w_ref, b_ref, gv_ref, sel_ref,
             dg_ref, dyb_ref, delta_ref, dw_ref, dbt_ref, dgv_ref):
        @pl.when(pl.program_id(0) == 0)
        def _():
            dw_ref[...] = jnp.zeros_like(dw_ref)
            dbt_ref[...] = jnp.zeros_like(dbt_ref)
            dgv_ref[...] = jnp.zeros_like(dgv_ref)

        gv = gv_ref[...]
        mask = _causal(CHUNK)
        w = [jnp.where(mask, w_ref[g], 0.0).astype(BF16) for g in range(N_GROUPS)]
        for rows in _chunks(step_rows, CHUNK):
            u, vs = u_ref[rows, :], vs_ref[rows, :]
            a_u, t_u = _gelu(u)
            a_v, t_v = _gelu(vs)
            xhat, r = _layernorm_stats(a_v)
            vn = (xhat * gv).astype(BF16)
            sa, sb = _sigmoid(ga_ref[rows, :]), _sigmoid(gb_ref[rows, :])
            yb, dm = yb_ref[rows, :], dm_ref[rows, :].astype(F32)
            dyb = dm * sb
            dyb_ref[rows, :] = dyb
            dyb_yb = dyb * yb
            dg_ref[rows, 3 * D_MODEL:] = (dm * yb * (sb * (1.0 - sb))).astype(BF16)
            dya = dm * sa
            dmixed_parts, dvn_parts = [], []
            for g in range(N_GROUPS):
                cols = _group(g)
                delta_ref[rows, cols] = jnp.broadcast_to(jnp.sum(dyb_yb[:, cols], axis=1, keepdims=True),
                                                         (CHUNK, CHUNK))
                mixed = jnp.dot(w[g], vn[:, cols], preferred_element_type=F32) + b_ref[:, g:g + 1]
                y_a = a_u[:, cols] * mixed
                dg_ref[rows, 2 * D_MODEL + g * CHUNK:2 * D_MODEL + (g + 1) * CHUNK] = (
                    dm[:, cols] * y_a * (sa[:, cols] * (1.0 - sa[:, cols]))).astype(BF16)
                dg_ref[rows, cols] = (dya[:, cols] * mixed * _gelu_grad(u[:, cols], t_u[:, cols])).astype(BF16)
                dmixed = dya[:, cols] * a_u[:, cols]
                dmb = dmixed.astype(BF16)
                dw = lax.dot_general(dmb, vn[:, cols], _DIMS["nt"], preferred_element_type=F32)
                dw_ref[g] += jnp.where(mask, dw, 0.0)
                dvn_parts.append(lax.dot_general(w[g], dmb, _DIMS["tn"], preferred_element_type=F32))
                dmixed_parts.append(dmixed)
            dmixed_all = jnp.concatenate(dmixed_parts, axis=1)
            dvn = jnp.concatenate(dvn_parts, axis=1)
            dbt_ref[...] += _dot_exact_rhs(dmixed_all, sel_ref[...])
            dgv_ref[...] += jnp.sum(dvn * xhat, axis=0, keepdims=True)
            dxh = dvn * gv
            da_v = r * (dxh - jnp.mean(dxh, axis=-1, keepdims=True)
                        - xhat * jnp.mean(dxh * xhat, axis=-1, keepdims=True))
            dg_ref[rows, D_MODEL:2 * D_MODEL] = (da_v * _gelu_grad(vs, t_v)).astype(BF16)

    row = lambda width: _row_spec(step_rows, width)
    gated = lambda k: pl.BlockSpec((step_rows, D_MODEL), lambda i: (i, k))
    return _call(body, name, (t // step_rows,),
                 [gated(0), gated(1), gated(2), gated(3), row(D_MODEL), row(D_MODEL),
                  _fixed((N_GROUPS, CHUNK, CHUNK)), _fixed((CHUNK, N_GROUPS)), _fixed((1, D_MODEL)),
                  _fixed((D_MODEL, HEAD_DIM))],
                 [row(GATED), row(D_MODEL), row(D_MODEL),
                  _fixed((N_GROUPS, CHUNK, CHUNK)), _fixed((CHUNK, HEAD_DIM)), _fixed((1, D_MODEL))],
                 [jax.ShapeDtypeStruct((t, IN_PAD), BF16), jax.ShapeDtypeStruct((t, D_MODEL), F32),
                  jax.ShapeDtypeStruct((t, D_MODEL), F32), jax.ShapeDtypeStruct((N_GROUPS, CHUNK, CHUNK), F32),
                  jax.ShapeDtypeStruct((CHUNK, HEAD_DIM), F32), jax.ShapeDtypeStruct((1, D_MODEL), F32)],
                 [proj, proj, proj, proj, y_b, dmerged, w_s, b_cols, g_v, sel], sem=("arbitrary",),
                 nbytes=24 * _nbytes((step_rows, D_MODEL), F32), deps=deps)


def _adamw_update(w_ref, g_ref, m_ref, v_ref, d_ref, nm_ref, nv_ref):
    gv = g_ref[...]
    nm = ADAM_B1 * m_ref[...] + (1.0 - ADAM_B1) * gv
    nv = ADAM_B2 * v_ref[...] + (1.0 - ADAM_B2) * (gv * gv)
    m_hat = nm / (1.0 - ADAM_B1 ** ADAM_STEP)
    v_hat = nv / (1.0 - ADAM_B2 ** ADAM_STEP)
    d_ref[...] = -ADAM_LR * (m_hat / (jnp.sqrt(v_hat) + ADAM_EPS) + ADAM_WD * w_ref[...])
    nm_ref[...] = nm
    nv_ref[...] = nv


def _adamw(w, g, m, v, name):
    r, c = w.shape
    tr = _tile(r, 256, unit=8)

    def body(w_ref, g_ref, m_ref, v_ref, d_ref, nm_ref, nv_ref):
        _adamw_update(w_ref, g_ref, m_ref, v_ref, d_ref, nm_ref, nv_ref)

    spec = pl.BlockSpec((tr, c), lambda i: (i, 0))
    shape = jax.ShapeDtypeStruct((r, c), F32)
    return _call(body, name, (r // tr,), [spec] * 4, [spec] * 3, [shape] * 3, [w, g, m, v], sem=("parallel",),
                 nbytes=7 * _nbytes((tr, max(c, 128)), F32))


def _adamw_cols(w, g, m, v, name):
    layers, r, c = w.shape
    tc = 256

    def body(w_ref, g_ref, m_ref, v_ref, d_ref, nm_ref, nv_ref):
        _adamw_update(w_ref, g_ref, m_ref, v_ref, d_ref, nm_ref, nv_ref)

    spec = pl.BlockSpec((None, r, tc), lambda l, j: (l, 0, j))
    shape = jax.ShapeDtypeStruct(w.shape, F32)
    return _call(body, name, (layers, c // tc), [spec] * 4, [spec] * 3, [shape] * 3, [w, g, m, v],
                 sem=("parallel", "parallel"), nbytes=7 * _nbytes((r, tc), F32))


def _pad_rows(a, rows):
    return jnp.pad(a, ((0, rows - a.shape[0]), (0, 0)))


def _unpack_in(gathered):
    i = jnp.arange(16)[None, :, None]
    b = jnp.arange(N_DEV)[:, None, None]
    head = jnp.where(i < b, jnp.roll(gathered[:, IN_STRIDE:IN_STRIDE + 16], 1, axis=0), gathered[:, :16])
    nat = jnp.concatenate([head, gathered[:, 16:IN_STRIDE]], axis=1).reshape(N_DEV * IN_STRIDE, D_MODEL)
    qkv = nat[2 * D_MODEL:5 * D_MODEL].reshape(3, N_HEADS, HEAD_DIM, D_MODEL)
    qkv = jnp.transpose(qkv, (1, 0, 2, 3)).reshape(3 * D_MODEL, D_MODEL)
    f = _pad_rows(gathered[N_DEV - 1, IN_STRIDE:IN_STRIDE + N_HEADS], F_WIDTH)
    return jnp.concatenate([nat[:2 * D_MODEL], nat[5 * D_MODEL:7 * D_MODEL], f, qkv], axis=0)


def _pack_grad_in(dw_in_t):
    qkv = dw_in_t[GATED + F_WIDTH:].reshape(N_HEADS, 3, HEAD_DIM, D_MODEL)
    qkv = jnp.transpose(qkv, (1, 0, 2, 3)).reshape(3 * D_MODEL, D_MODEL)
    return jnp.concatenate([dw_in_t[:2 * D_MODEL], qkv, dw_in_t[2 * D_MODEL:GATED],
                            dw_in_t[GATED:GATED + IN_ROWS - N_DEV * IN_STRIDE]], axis=0)


def kernel(x, mix_pre_g, w_in, b_forget, sgu_norm_g, w_spatial, b_spatial, w_out, mix_post_g, ffn_pre_g, w_gate, w_up, w_down, ffn_post_g, loss_target, m_mix_pre_g, m_w_in, m_b_forget, m_sgu_norm_g, m_w_spatial, m_b_spatial, m_w_out, m_mix_post_g, m_ffn_pre_g, m_w_gate, m_w_up, m_w_down, m_ffn_post_g, v_mix_pre_g, v_w_in, v_b_forget, v_sgu_norm_g, v_w_spatial, v_b_spatial, v_w_out, v_mix_post_g, v_ffn_pre_g, v_w_gate, v_w_up, v_w_down, v_ffn_post_g):
    depth = w_in.shape[0]
    assert depth == 2
    t = x.shape[1]
    x0 = x.reshape(t, D_MODEL)
    target = loss_target.reshape(t, D_MODEL)
    sel = (jnp.arange(D_MODEL)[:, None] // HEAD_DIM == jnp.arange(HEAD_DIM)[None, :]).astype(BF16)
    vec = lambda a, l: a[l][None, :]
    zero_row = jnp.zeros((1, D_MODEL), F32)

    me = 4 * lax.axis_index("x") + 2 * lax.axis_index("y") + lax.axis_index("c")
    p_in = [lax.dynamic_update_slice(jnp.zeros((P_IN, D_MODEL), BF16), w_in[l].T.astype(BF16), (me, 0))
            for l in range(depth)]
    p_rest = [[w_out[l].astype(BF16), jnp.stack([w_gate[l].T, w_up[l].T]).astype(BF16), w_down[l].astype(BF16)]
              for l in range(depth)]
    w_in_t = [None] * depth
    w_rest = [None] * depth
    w_in_t[0] = _unpack_in(_exchange(("gather", 0), p_in[0], "gather_w_in_0")[0])

    saved = []
    xl = x0
    h = _rms_fwd(xl, vec(mix_pre_g, 0), "rms_in_0")
    dy = loss = None
    for l in range(depth):
        b_pad = jnp.pad(b_forget[l], (0, HEAD_DIM - N_HEADS))[None, :]
        b_cols = b_spatial[l].T
        proj = _matmul(h, w_in_t[l], "nt", F32, f"proj_{l}")
        c_pad = _forget_cumsum(proj, b_pad, f"forget_cumsum_{l}")
        c_row = c_pad[:, :N_HEADS].T[:, None, :]
        riders = [(("gather", 0), p_rest[l][0]), (("gather", 1), p_rest[l][1]), (("gather", 0), p_rest[l][2])]
        riders += [(("gather", 0), p_in[l + 1])] if l + 1 < depth else []
        y_b, e_row, *arrived = _attn_fwd_t(proj, c_pad, c_row, f"attn_fwd_{l}", comm=riders)
        w_rest[l] = (arrived[0].reshape(D_MODEL, D_MODEL), arrived[1].reshape(2 * D_FF, D_MODEL),
                     arrived[2].reshape(D_FF, D_MODEL))
        if l + 1 < depth:
            w_in_t[l + 1] = _unpack_in(arrived[3])
        w_o, w_gu_t, w_d = w_rest[l]
        merged = _mix_fwd(proj, y_b, w_spatial[l], b_cols, vec(sgu_norm_g, l), f"mix_fwd_{l}")
        o = _matmul(merged, w_o, "nn", F32, f"out_proj_{l}")
        x1, h2 = _post_norm_fwd(xl, o, vec(mix_post_g, l), vec(ffn_pre_g, l), f"post_mix_{l}")
        gu = _matmul(h2, w_gu_t, "nt", BF16, f"ffn_gu_{l}")
        act = _swiglu_fwd(gu, f"swiglu_fwd_{l}")
        dn = _matmul(act, w_d, "nn", F32, f"ffn_down_{l}")
        saved.append(dict(x0=xl, h=h, proj=proj, b_pad=b_pad, b_cols=b_cols, c_pad=c_pad, y_b=y_b, e_row=e_row,
                          merged=merged, o=o, x1=x1, h2=h2, gu=gu, act=act, dn=dn))
        if l + 1 < depth:
            xl, h = _post_norm_fwd(x1, dn, vec(ffn_post_g, l), vec(mix_pre_g, l + 1), f"post_ffn_{l}")
        else:
            dy, loss = _post_norm_loss(x1, dn, vec(ffn_post_g, l), target, "loss")

    g_small = [None] * depth
    flights = {}
    late_token = None
    dx = dy
    dd, dg_ffn_post = _norm_bwd(dx, f"bwd_ffn_post_{depth - 1}", post=(saved[-1]["dn"], vec(ffn_post_g, depth - 1)))
    for l in reversed(range(depth)):
        s = saved[l]
        w_o, w_gu_t, w_d = w_rest[l]
        dact = _matmul(dd, w_d, "nt", BF16, f"d_act_{l}")
        dw_d = _matmul(s["act"], dd, "tn", BF16, f"dw_down_{l}")
        dgu = _swiglu_bwd(s["gu"], dact, f"swiglu_bwd_{l}")
        dh2 = _matmul(dgu, w_gu_t, "nn", F32, f"d_h2_{l}")
        dw_gu_t = _matmul(dgu, s["h2"], "tn", BF16, f"dw_gu_{l}")
        dx1, dg_ffn_pre, do, dg_mix_post = _norm_bwd(
            dx, f"bwd_post_mix_{l}", pre=(dh2, s["x1"], vec(ffn_pre_g, l)), post=(s["o"], vec(mix_post_g, l)))
        dmerged = _matmul(do, w_o, "nt", F32, f"d_merged_{l}")
        dw_o = _matmul(s["merged"], do, "tn", BF16, f"dw_out_{l}")
        tokens = []
        for key, dw, base, rows in [("out", dw_o, 0, SH_OUT), ("gate", dw_gu_t, 0, SH_FF), ("up", dw_gu_t, D_FF, SH_FF),
                                    ("down", dw_d, 0, SH_FF)]:
            flights[(key, l)], token = _exchange_start(("windows", base, rows, rows), dw, f"start_grad_{key}_{l}")
            tokens.append(token)
        dproj, dyb, delta_b, dw_s, dbt, dg_v = _mix_bwd(
            s["proj"], s["y_b"], dmerged, w_spatial[l], s["b_cols"], vec(sgu_norm_g, l), sel, f"mix_bwd_{l}",
            deps=tokens)
        db_s = dbt[:, :N_GROUPS].T.reshape(1, D_MODEL)
        tokens = []
        if l == 0:
            g_small[0] = jnp.concatenate([zero_row, dg_v, dg_mix_post, dg_ffn_pre, dg_ffn_post, db_s, zero_row,
                                          zero_row, dw_s.reshape(CHUNK, D_MODEL)], axis=0)
            flights["small"], token = _exchange_start(("spread",),jnp.concatenate([g_small[1], g_small[0]], axis=0),
                                                      "start_small_grads")
            tokens = [token]
        dproj, dc_key, dc_query = _attn_bwd_t(s["proj"], dyb, s["e_row"], delta_b, s["c_pad"], dproj,
                                              f"attn_bwd_{l}", deps=tokens)
        dc = jnp.pad((dc_key + dc_query)[:, 0, :].T, ((0, 0), (0, HEAD_DIM - N_HEADS)))
        dproj, db_f = _forget_bwd_t(s["proj"], s["b_pad"], dc, dproj, f"forget_bwd_{l}")
        db_f_row = jnp.pad(db_f, ((0, 0), (0, D_MODEL - HEAD_DIM)))
        dw_in_t = _matmul(dproj, s["h"], "tn", BF16, f"dw_in_{l}")
        flights[("in", l)], token = _exchange_start(("windows", 0, IN_STRIDE, P_IN), _pack_grad_in(dw_in_t),
                                                    f"start_grad_in_{l}")
        dh = _matmul(dproj, w_in_t[l], "nn", F32, f"d_h_{l}", deps=[token])
        if l > 0:
            dx, dg_mix_pre, dd, dg_ffn_post_below = _norm_bwd(
                dx1, f"bwd_pre_mix_{l}", pre=(dh, s["x0"], vec(mix_pre_g, l)),
                post=(saved[l - 1]["dn"], vec(ffn_post_g, l - 1)))
            loss_row = jnp.pad(loss, ((0, 0), (0, D_MODEL - 1)))
            g_small[l] = jnp.concatenate([dg_mix_pre, dg_v, dg_mix_post, dg_ffn_pre, dg_ffn_post, db_s, db_f_row,
                                          loss_row, dw_s.reshape(CHUNK, D_MODEL)], axis=0)
            dg_ffn_post = dg_ffn_post_below
        else:
            dx, dg_mix_pre = _norm_bwd(dx1, f"bwd_pre_mix_{l}", pre=(dh, s["x0"], vec(mix_pre_g, l)))
            late_rows = jnp.concatenate([dg_mix_pre, db_f_row] + [zero_row] * 6, axis=0)
            flights["late"], late_token = _exchange_start(("spread",),late_rows, "start_late_small_grads")
    grad_x = dx.reshape(x.shape)

    landed, after = {}, [grad_x, late_token]
    rest = ["out", "gate", "up", "down"]
    for key in [(n, 1) for n in rest] + [("in", 1)] + [(n, 0) for n in rest] + ["small"]:
        label = key if isinstance(key, str) else f"grad_{key[0]}_{key[1]}"
        landed[key] = _exchange_wait(flights[key], after, f"wait_{label}")
        after = [landed[key]]
    small_sum = _sum_slots(landed["small"], S_ROWS, "sum_small_grads")
    sm = small_sum.reshape(depth, S_ROWS, D_MODEL)[::-1]
    loss = sm[depth - 1, 7, 0]
    g = {
        "sgu_norm_g": sm[:, 1], "mix_post_g": sm[:, 2],
        "ffn_pre_g": sm[:, 3], "ffn_post_g": sm[:, 4], "b_spatial": sm[:, 5].reshape(depth, N_GROUPS, CHUNK),
        "w_spatial": sm[:, 8:].reshape(depth, N_GROUPS, CHUNK, CHUNK),
    }
    for n in rest:
        g["w_" + n] = jnp.stack([_sum_slots(landed[(n, l)], landed[(n, l)].shape[1], f"sum_grad_{n}_{l}")
                                 for l in range(depth)])

    names = ["mix_pre_g", "w_in", "b_forget", "sgu_norm_g", "w_spatial", "b_spatial", "w_out", "mix_post_g",
             "ffn_pre_g", "w_gate", "w_up", "w_down", "ffn_post_g"]
    ws = dict(mix_pre_g=mix_pre_g, w_in=w_in, b_forget=b_forget, sgu_norm_g=sgu_norm_g, w_spatial=w_spatial,
              b_spatial=b_spatial, w_out=w_out, mix_post_g=mix_post_g, ffn_pre_g=ffn_pre_g, w_gate=w_gate, w_up=w_up,
              w_down=w_down, ffn_post_g=ffn_post_g)
    ms = dict(mix_pre_g=m_mix_pre_g, w_in=m_w_in, b_forget=m_b_forget, sgu_norm_g=m_sgu_norm_g, w_spatial=m_w_spatial,
              b_spatial=m_b_spatial, w_out=m_w_out, mix_post_g=m_mix_post_g, ffn_pre_g=m_ffn_pre_g, w_gate=m_w_gate,
              w_up=m_w_up, w_down=m_w_down, ffn_post_g=m_ffn_post_g)
    vs = dict(mix_pre_g=v_mix_pre_g, w_in=v_w_in, b_forget=v_b_forget, sgu_norm_g=v_sgu_norm_g, w_spatial=v_w_spatial,
              b_spatial=v_b_spatial, w_out=v_w_out, mix_post_g=v_mix_post_g, ffn_pre_g=v_ffn_pre_g, w_gate=v_w_gate,
              w_up=v_w_up, w_down=v_w_down, ffn_post_g=v_ffn_post_g)
    updates = {}
    last = ["w_in", "mix_pre_g", "b_forget"]
    transposed = ["w_in", "w_gate", "w_up"]
    for n in [n for n in names if n not in last] + last:
        if n == last[0]:
            sum_in_1 = _sum_slots(landed[("in", 1)], 304, "sum_grad_in_1")
            done = [u[0] for u in updates.values()] + [sum_in_1]
            sum_in_0 = _sum_slots(_exchange_wait(flights[("in", 0)], done, "wait_grad_in_0"), 304, "sum_grad_in_0")
            late = _sum_slots(_exchange_wait(flights["late"], [sum_in_0], "wait_late"), 8, "sum_late_small_grads")
            g["w_in"] = lax.dynamic_slice(jnp.stack([sum_in_0, sum_in_1]), (0, me, 0), (depth, SH_IN, D_MODEL))
            g["mix_pre_g"] = jnp.stack([late[0], sm[1, 0]])
            g["b_forget"] = jnp.stack([late[1, :N_HEADS], sm[1, 6, :N_HEADS]])
        if n in transposed:
            view = lambda a: jnp.swapaxes(a, 1, 2)
            updates[n] = [view(u) for u in _adamw_cols(view(ws[n]), g[n], view(ms[n]), view(vs[n]), f"adamw_{n}")]
            g[n] = view(g[n])
        else:
            flat = (-1, ws[n].shape[-1])
            updates[n] = [u.reshape(ws[n].shape) for u in
                          _adamw(ws[n].reshape(flat), g[n].reshape(flat), ms[n].reshape(flat), vs[n].reshape(flat),
                                 f"adamw_{n}")]
    deltas = [updates[n][0] for n in names]
    new_m = [updates[n][1] for n in names]
    new_v = [updates[n][2] for n in names]
    grads = [g[n].reshape(ws[n].shape) for n in names]
    return (loss, grad_x, *grads, *deltas, *new_m, *new_v)
```

```python
import math

import jax
import jax.numpy as jnp
from jax import lax
from jax.experimental import pallas as pl
from jax.experimental.pallas import tpu as pltpu

F32 = jnp.float32
BF16 = jnp.bfloat16

N_DEV = 8
D_MODEL = 1024
N_HEADS = 8
HEAD_DIM = 128
CHUNK = 128
N_GROUPS = 8
D_FF = 2816
IN_WIDTH = 7 * D_MODEL + N_HEADS
IN_PAD = 7680
EPS = 1e-6
GATED = 4 * D_MODEL
F_WIDTH = 512
QKV_WIDTH = 3 * HEAD_DIM
CB_F = GATED // HEAD_DIM
CB_QKV = (GATED + F_WIDTH) // HEAD_DIM
assert GATED + F_WIDTH + N_HEADS * QKV_WIDTH == IN_PAD and (GATED + F_WIDTH) % QKV_WIDTH == 0

ADAM_LR, ADAM_B1, ADAM_B2, ADAM_EPS, ADAM_WD, ADAM_STEP = 0.001, 0.9, 0.999, 1e-08, 0.01, 10

SH_IN = IN_WIDTH // N_DEV
SH_OUT = D_MODEL // N_DEV
SH_FF = D_FF // N_DEV
P_IN = 912
IN_STRIDE = 7 * HEAD_DIM
IN_ROWS = IN_STRIDE * (N_DEV - 1) + P_IN
S_ROWS = 136

ATTN_BLOCK = 1024
ROW_BLOCK = 512
MIX_ROWS = 512
MIX_BWD_ROWS = 256
MM_TM, MM_TN, MM_TK = 1536, 1536, 2048
VMEM_CAP = 56 << 20
NEG = -1e30


def _tile(n, cap, unit=128):
    if n <= cap:
        return n
    best = None
    for t in range(unit, cap + 1, unit):
        if n % t == 0:
            best = t
    assert best is not None, (n, cap)
    return best


def _nbytes(shape, dtype):
    return math.prod(shape) * jnp.dtype(dtype).itemsize


_HBM = pl.BlockSpec(memory_space=pltpu.HBM)
MESH = pl.DeviceIdType.MESH

_N_REMOTE = N_DEV - 1


def _exchange_shapes(kind, x):
    if kind[0] == "gather":
        shape = x.shape[:kind[1]] + (N_DEV,) + x.shape[kind[1]:]
    elif kind[0] == "spread":
        shape = (N_DEV,) + x.shape
    else:
        shape = (N_DEV, kind[3]) + x.shape[1:]
    return [jax.ShapeDtypeStruct(shape, x.dtype)]


def _exchange_sems(kind):
    del kind
    return [pltpu.SemaphoreType.DMA((_N_REMOTE,)), pltpu.SemaphoreType.DMA((_N_REMOTE,)), pltpu.SemaphoreType.DMA]


def _plan(kind, x_ref, outs, sems):
    send_sems, recv_sems, local_sem = sems
    x, y, c = lax.axis_index("x"), lax.axis_index("y"), lax.axis_index("c")

    def remote(src, dst, k, to):
        return pltpu.make_async_remote_copy(src_ref=src, dst_ref=dst, send_sem=send_sems.at[k], recv_sem=recv_sems.at[k],
                                            device_id=to, device_id_type=MESH)

    sibling = (x, y, 1 - c)
    if kind[0] == "gather":
        out, = outs
        slot = lambda px, py, pc: out.at[(slice(None),) * kind[1] + (4 * px + 2 * py + pc,)]
        chips = [(1 - x, y), (x, 1 - y), (1 - x, 1 - y)]
        local = [pltpu.make_async_copy(x_ref, slot(x, y, c), local_sem)]
        first = [remote(x_ref, slot(x, y, c), 0, sibling)]
        first += [remote(x_ref, slot(x, y, c), 1 + k, (*chip, c)) for k, chip in enumerate(chips)]
        relays = [(remote(x_ref, slot(*chip, c), 1 + k, (*chip, c)), remote(slot(*chip, c), slot(*chip, c), 4 + k, sibling))
                  for k, chip in enumerate(chips)]
        arrivals = [remote(x_ref, slot(x, y, 1 - c), 0, sibling)]
        arrivals += [remote(x_ref, slot(*chip, 1 - c), 4 + k, sibling) for k, chip in enumerate(chips)]
        return local, first, relays, arrivals
    out, = outs
    mine = 4 * x + 2 * y + c
    if kind[0] == "spread":
        src = lambda d: x_ref
    else:
        _, base, stride, length = kind
        src = lambda d: x_ref.at[pl.ds(pl.multiple_of(base + stride * d, 16), length)]
    first, arrivals = [], []
    for j in range(1, N_DEV):
        peer = (1 - x if j & 4 else x, 1 - y if j & 2 else y, 1 - c if j & 1 else c)
        theirs = 4 * peer[0] + 2 * peer[1] + peer[2]
        first.append(remote(src(theirs), out.at[mine], j - 1, peer))
        arrivals.append(remote(src(theirs), out.at[theirs], j - 1, peer))
    local = [] if local_sem is None else [pltpu.make_async_copy(src(mine), out.at[mine], local_sem)]
    return local, first, [], arrivals


def _start(plan):
    local, first, _, _ = plan
    for cp in local + first:
        cp.start()


def _finish(plan):
    local, first, relays, arrivals = plan
    for arrival, onward in relays:
        arrival.wait_recv()
        onward.start()
    for cp in arrivals:
        cp.wait_recv()
    for cp in first + [onward for _, onward in relays]:
        cp.wait_send()
    for cp in local:
        cp.wait()


def _direct_copies(kind, x_ref, land_ref, send_sems, recv_sems):
    _, first, _, arrivals = _plan(kind, x_ref, [land_ref], (send_sems, recv_sems, None))
    return first, arrivals


_SEM = pl.BlockSpec(memory_space=pltpu.SEMAPHORE)
_DATAFLOW = pltpu.SideEffectType.DATAFLOW_SIDE_EFFECTING


def _exchange_start(kind, x, name):
    me = 4 * lax.axis_index("x") + 2 * lax.axis_index("y") + lax.axis_index("c")
    own = x if kind[0] == "spread" else lax.dynamic_slice_in_dim(x, kind[1] + kind[2] * me, kind[3], 0)
    shape = _exchange_shapes(kind, x)[0].shape
    land = lax.dynamic_update_slice_in_dim(lax.empty(shape, x.dtype), own[None], me, 0)

    def body(x_ref, land_ref, send_sems, recv_sems, x_thru, land_thru, token):
        del x_thru, land_thru
        for cp in _direct_copies(kind, x_ref, land_ref, send_sems, recv_sems)[0]:
            cp.start()
        token[...] = jnp.zeros_like(token)

    *handle, token = pl.pallas_call(
        body, name=name,
        out_shape=(pltpu.SemaphoreType.DMA((_N_REMOTE,)), pltpu.SemaphoreType.DMA((_N_REMOTE,)),
                   pltpu.HBM(x.shape, x.dtype), pltpu.HBM(land.shape, land.dtype), jax.ShapeDtypeStruct((8, 128), F32)),
        in_specs=(_HBM, _HBM), out_specs=(_SEM, _SEM, _HBM, _HBM, pl.BlockSpec(memory_space=pltpu.VMEM)),
        input_output_aliases={0: 2, 1: 3}, compiler_params=pltpu.CompilerParams(has_side_effects=_DATAFLOW),
    )(pltpu.with_memory_space_constraint(x, pltpu.HBM), pltpu.with_memory_space_constraint(land, pltpu.HBM))
    return (kind, *handle), token


def _exchange_wait(handle, after, name):
    kind, send_sems, recv_sems, x_thru, land_thru = handle

    def body(x_ref, land_ref, send_sems, recv_sems, *rest):
        first, arrivals = _direct_copies(kind, x_ref, land_ref, send_sems, recv_sems)
        for cp in first:
            cp.wait_send()
        for cp in arrivals:
            cp.wait_recv()

    return pl.pallas_call(
        body, name=name,
        out_shape=(pltpu.HBM(x_thru.shape, x_thru.dtype), pltpu.HBM(land_thru.shape, land_thru.dtype)),
        in_specs=(_HBM, _HBM, _SEM, _SEM) + (pl.BlockSpec(memory_space=pl.ANY),) * len(after), out_specs=(_HBM, _HBM),
        input_output_aliases={0: 0, 1: 1}, compiler_params=pltpu.CompilerParams(has_side_effects=_DATAFLOW),
    )(x_thru, land_thru, send_sems, recv_sems, *after)[1]


def _exchange(kind, x, name):
    n_out = len(_exchange_shapes(kind, x))

    def body(x_ref, *refs):
        plan = _plan(kind, x_ref, refs[:n_out], refs[n_out:])
        _start(plan)
        _finish(plan)

    return pl.pallas_call(body, name=name, out_shape=_exchange_shapes(kind, x), in_specs=[_HBM],
                          out_specs=[_HBM] * n_out, scratch_shapes=_exchange_sems(kind))(x)


def _call(body, name, grid, in_specs, out_specs, out_shape, args, scratch=(), sem=None, nbytes=0, aliases=None,
          comm=(), deps=()):
    in_specs, out_specs, out_shape, args, scratch = (list(in_specs), list(out_specs), list(out_shape), list(args),
                                                     list(scratch))
    if deps:
        n_real, n_deps, unordered = len(args), len(deps), body

        def body(*refs):
            unordered(*refs[:n_real], *refs[n_real + n_deps:])

        in_specs += [pl.BlockSpec(memory_space=pl.ANY)] * n_deps
        args += list(deps)
    if comm:
        n_in, n_out, n_scr, n_ops = len(args), len(out_shape), len(scratch), len(comm)
        kinds = [kind for kind, _ in comm]
        shapes = [_exchange_shapes(kind, x) for kind, x in comm]
        inner = body

        def body(*refs):
            ins, refs = refs[:n_in], refs[n_in:]
            cins, refs = refs[:n_ops], refs[n_ops:]
            outs, refs = refs[:n_out], refs[n_out:]
            couts = []
            for sh in shapes:
                couts.append(refs[:len(sh)])
                refs = refs[len(sh):]
            own_scratch, sems = refs[:n_scr], refs[n_scr:]
            first = last = None
            for axis, size in enumerate(grid):
                at_start, at_end = pl.program_id(axis) == 0, pl.program_id(axis) == size - 1
                first = at_start if first is None else first & at_start
                last = at_end if last is None else last & at_end
            plans = [_plan(kinds[o], cins[o], couts[o], sems[3 * o:3 * o + 3]) for o in range(n_ops)]

            @pl.when(first)
            def _():
                for p in plans:
                    _start(p)

            inner(*ins, *outs, *own_scratch)

            @pl.when(last)
            def _():
                for p in plans:
                    _finish(p)

        in_specs += [_HBM] * n_ops
        args += [x for _, x in comm]
        for kind, sh in zip(kinds, shapes):
            out_shape += sh
            out_specs += [_HBM] * len(sh)
            scratch += _exchange_sems(kind)
        sem = ("arbitrary",) * len(grid)
    limit = int(min(max(2 * nbytes + (8 << 20), 32 << 20), VMEM_CAP))
    return pl.pallas_call(
        body, name=name, grid=grid, in_specs=in_specs, out_specs=out_specs, out_shape=out_shape,
        scratch_shapes=scratch, input_output_aliases=aliases or {},
        compiler_params=pltpu.CompilerParams(dimension_semantics=sem, vmem_limit_bytes=limit),
    )(*args)


def _sum_slots(x, tr, name):
    n, r, c = x.shape

    def body(x_ref, o_ref):
        acc = x_ref[0].astype(F32)
        for d in range(1, n):
            acc = acc + x_ref[d].astype(F32)
        o_ref[...] = acc

    return _call(body, name, (r // tr,), [pl.BlockSpec((n, tr, c), lambda i: (0, i, 0))],
                 [pl.BlockSpec((tr, c), lambda i: (i, 0))], [jax.ShapeDtypeStruct((r, c), F32)], [x],
                 sem=("parallel",), nbytes=_nbytes((n, tr, c), x.dtype) + _nbytes((tr, c), F32))[0]


_DIMS = {"nn": (((1,), (0,)), ((), ())), "nt": (((1,), (1,)), ((), ())), "tn": (((0,), (0,)), ((), ()))}


def _matmul(a, b, mode, out_dtype, name, comm=(), deps=()):
    if mode == "nn":
        (m, k), (k2, n) = a.shape, b.shape
    elif mode == "nt":
        (m, k), (n, k2) = a.shape, b.shape
    else:
        (k, m), (k2, n) = a.shape, b.shape
    assert k == k2, (a.shape, b.shape, mode)
    tm, tn, tk = _tile(m, MM_TM), _tile(n, MM_TN), _tile(k, MM_TK)
    nk = k // tk
    dims = _DIMS[mode]
    a_spec = {"nn": pl.BlockSpec((tm, tk), lambda i, j, kk: (i, kk)),
              "nt": pl.BlockSpec((tm, tk), lambda i, j, kk: (i, kk)),
              "tn": pl.BlockSpec((tk, tm), lambda i, j, kk: (kk, i))}[mode]
    b_spec = {"nn": pl.BlockSpec((tk, tn), lambda i, j, kk: (kk, j)),
              "nt": pl.BlockSpec((tn, tk), lambda i, j, kk: (j, kk)),
              "tn": pl.BlockSpec((tk, tn), lambda i, j, kk: (kk, j))}[mode]

    def partial_product(a_ref, b_ref):
        return lax.dot_general(a_ref[...].astype(BF16), b_ref[...].astype(BF16), dims, preferred_element_type=F32)

    if nk == 1:
        def body(a_ref, b_ref, o_ref):
            o_ref[...] = partial_product(a_ref, b_ref).astype(o_ref.dtype)
        scratch = []
    else:
        def body(a_ref, b_ref, o_ref, acc_ref):
            kk = pl.program_id(2)

            @pl.when(kk == 0)
            def _():
                acc_ref[...] = jnp.zeros_like(acc_ref)

            acc_ref[...] += partial_product(a_ref, b_ref)

            @pl.when(kk == nk - 1)
            def _():
                o_ref[...] = acc_ref[...].astype(o_ref.dtype)
        scratch = [pltpu.VMEM((tm, tn), F32)]

    nbytes = (_nbytes((tm, tk), a.dtype) + _nbytes((tk, tn), b.dtype) + _nbytes((tm, tn), out_dtype)
              + _nbytes((tm, tn), F32))
    res = _call(body, name, (m // tm, n // tn, nk), [a_spec, b_spec],
                [pl.BlockSpec((tm, tn), lambda i, j, kk: (i, j))], [jax.ShapeDtypeStruct((m, n), out_dtype)], [a, b],
                scratch=scratch, sem=("parallel", "parallel", "arbitrary"), nbytes=nbytes, comm=comm, deps=deps)
    return res if comm else res[0]


def _matmul_post_norm(a, b, x, g_post, name, g_next=None, target=None):
    (m, k), (k2, d) = a.shape, b.shape
    assert k == k2 and d == D_MODEL
    tm, tk = _tile(m, ROW_BLOCK), _tile(k, MM_TK)
    nk = k // tk
    with_loss = target is not None

    def body(a_ref, b_ref, x_ref, gp_ref, last_ref, o_ref, first_ref, second_ref, acc_ref):
        i, kk = pl.program_id(0), pl.program_id(1)

        @pl.when(kk == 0)
        def _():
            acc_ref[...] = jnp.zeros_like(acc_ref)

        acc_ref[...] += jnp.dot(a_ref[...], b_ref[...], preferred_element_type=F32)

        @pl.when(kk == nk - 1)
        def _():
            ov = acc_ref[...]
            o_ref[...] = ov
            xn = x_ref[...] + (ov * _rms(ov)) * gp_ref[...]
            if with_loss:
                err = xn - last_ref[...]
                first_ref[...] = err / d

                @pl.when(i == 0)
                def _():
                    second_ref[...] = jnp.zeros_like(second_ref)

                second_ref[...] += 0.5 * jnp.sum(jnp.mean(err * err, axis=-1, keepdims=True), axis=0, keepdims=True)
            else:
                first_ref[...] = xn
                second_ref[...] = ((xn * _rms(xn)) * last_ref[...]).astype(BF16)

    rows = pl.BlockSpec((tm, d), lambda i, kk: (i, 0))
    vec = pl.BlockSpec((1, d), lambda i, kk: (0, 0))
    second = ((pl.BlockSpec((1, 1), lambda i, kk: (0, 0)), jax.ShapeDtypeStruct((1, 1), F32)) if with_loss
              else (rows, jax.ShapeDtypeStruct((m, d), BF16)))
    nbytes = _nbytes((tm, tk), BF16) + _nbytes((tk, d), BF16) + 6 * _nbytes((tm, d), F32)
    return _call(body, name, (m // tm, nk),
                 [pl.BlockSpec((tm, tk), lambda i, kk: (i, kk)), pl.BlockSpec((tk, d), lambda i, kk: (kk, 0)), rows, vec,
                  rows if with_loss else vec],
                 [rows, rows, second[0]],
                 [jax.ShapeDtypeStruct((m, d), F32), jax.ShapeDtypeStruct((m, d), F32), second[1]],
                 [a, b, x, g_post, target if with_loss else g_next],
                 scratch=[pltpu.VMEM((tm, d), F32)], sem=("arbitrary", "arbitrary"), nbytes=nbytes)


def _rms(x):
    return lax.rsqrt(jnp.mean(x * x, axis=-1, keepdims=True) + EPS)


def _rms_bwd(dz, a, g):
    r = _rms(a)
    dzg = dz * g
    da = r * dzg - a * (r * r * r) * jnp.mean(dzg * a, axis=-1, keepdims=True)
    return da, dz * (a * r)


def _row_spec(tr, width):
    return pl.BlockSpec((tr, width), lambda i: (i, 0))


def _vec_spec(width):
    return pl.BlockSpec((1, width), lambda i: (0, 0))


def _rms_fwd(x, g, name):
    t, d = x.shape
    tr = min(ROW_BLOCK, t)

    def body(x_ref, g_ref, h_ref):
        xv = x_ref[...]
        h_ref[...] = ((xv * _rms(xv)) * g_ref[...]).astype(BF16)

    return _call(body, name, (t // tr,), [_row_spec(tr, d), _vec_spec(d)], [_row_spec(tr, d)],
                 [jax.ShapeDtypeStruct((t, d), BF16)], [x, g], sem=("parallel",), nbytes=3 * _nbytes((tr, d), F32))[0]


def _norm_bwd(dskip, name, pre=None, post=None):
    t, d = dskip.shape
    tr = min(ROW_BLOCK, t)
    n_in = 1 + (3 if pre else 0) + (2 if post else 0)

    def body(*refs):
        ins, outs = list(refs[:n_in]), list(refs[n_in:])
        first = pl.program_id(0) == 0
        dx = ins.pop(0)[...]
        if pre:
            dh_ref, xin_ref, gpre_ref = ins.pop(0), ins.pop(0), ins.pop(0)
            dxin, dg_rows = _rms_bwd(dh_ref[...].astype(F32), xin_ref[...], gpre_ref[...])
            dx = dx + dxin
            dx_ref, dgpre_ref = outs.pop(0), outs.pop(0)
            dx_ref[...] = dx

            @pl.when(first)
            def _():
                dgpre_ref[...] = jnp.zeros_like(dgpre_ref)

            dgpre_ref[...] += jnp.sum(dg_rows, axis=0, keepdims=True)
        if post:
            a_ref, gpost_ref = ins.pop(0), ins.pop(0)
            da, dg_rows = _rms_bwd(dx, a_ref[...], gpost_ref[...])
            da_ref, dgpost_ref = outs.pop(0), outs.pop(0)
            da_ref[...] = da.astype(BF16)

            @pl.when(first)
            def _():
                dgpost_ref[...] = jnp.zeros_like(dgpost_ref)

            dgpost_ref[...] += jnp.sum(dg_rows, axis=0, keepdims=True)

    args, in_specs, out_shape, out_specs = [dskip], [_row_spec(tr, d)], [], []
    if pre:
        args += list(pre)
        in_specs += [_row_spec(tr, d), _row_spec(tr, d), _vec_spec(d)]
        out_shape += [jax.ShapeDtypeStruct((t, d), F32), jax.ShapeDtypeStruct((1, d), F32)]
        out_specs += [_row_spec(tr, d), _vec_spec(d)]
    if post:
        args += list(post)
        in_specs += [_row_spec(tr, d), _vec_spec(d)]
        out_shape += [jax.ShapeDtypeStruct((t, d), BF16), jax.ShapeDtypeStruct((1, d), F32)]
        out_specs += [_row_spec(tr, d), _vec_spec(d)]
    return _call(body, name, (t // tr,), in_specs, out_specs, out_shape, args, sem=("arbitrary",),
                 nbytes=7 * _nbytes((tr, d), F32))


def _sigmoid(x):
    return 1.0 / (1.0 + jnp.exp(-x))


def _swiglu_fwd(gu, name):
    t = gu.shape[0]
    tr = min(ROW_BLOCK, t)

    def body(gu_ref, a_ref):
        gate, up = gu_ref[:, :D_FF].astype(F32), gu_ref[:, D_FF:].astype(F32)
        a_ref[...] = ((gate * _sigmoid(gate)) * up).astype(BF16)

    return _call(body, name, (t // tr,), [_row_spec(tr, 2 * D_FF)], [_row_spec(tr, D_FF)],
                 [jax.ShapeDtypeStruct((t, D_FF), BF16)], [gu], sem=("parallel",),
                 nbytes=2 * _nbytes((tr, 2 * D_FF), F32))[0]


def _swiglu_bwd(gu, dact, name):
    t = gu.shape[0]
    tr = min(ROW_BLOCK, t)

    def body(gu_ref, da_ref, o_ref):
        gate, up = gu_ref[:, :D_FF].astype(F32), gu_ref[:, D_FF:].astype(F32)
        da = da_ref[...].astype(F32)
        s = _sigmoid(gate)
        o_ref[:, :D_FF] = (da * up * (s * (1.0 + gate * (1.0 - s)))).astype(BF16)
        o_ref[:, D_FF:] = (da * (gate * s)).astype(BF16)

    return _call(body, name, (t // tr,), [_row_spec(tr, 2 * D_FF), _row_spec(tr, D_FF)], [_row_spec(tr, 2 * D_FF)],
                 [jax.ShapeDtypeStruct((t, 2 * D_FF), BF16)], [gu, dact], sem=("parallel",),
                 nbytes=3 * _nbytes((tr, 2 * D_FF), F32))[0]


def _split3(x):
    hi = x.astype(BF16)
    r1 = x - hi.astype(F32)
    mid = r1.astype(BF16)
    lo = (r1 - mid.astype(F32)).astype(BF16)
    return hi, mid, lo


def _dot_exact(mat01, x):
    hi, mid, lo = _split3(x)
    out = jnp.dot(mat01, hi, preferred_element_type=F32)
    out = out + jnp.dot(mat01, mid, preferred_element_type=F32)
    return out + jnp.dot(mat01, lo, preferred_element_type=F32)


def _dot_exact_rhs(x, mat01):
    hi, mid, lo = _split3(x)
    out = jnp.dot(hi, mat01, preferred_element_type=F32)
    out = out + jnp.dot(mid, mat01, preferred_element_type=F32)
    return out + jnp.dot(lo, mat01, preferred_element_type=F32)


def _tri(lower):
    r = lax.broadcasted_iota(jnp.int32, (CHUNK, CHUNK), 0)
    c = lax.broadcasted_iota(jnp.int32, (CHUNK, CHUNK), 1)
    return jnp.where(r >= c if lower else r <= c, 1.0, 0.0).astype(BF16)


def _log_sigmoid(z):
    return jnp.minimum(z, 0.0) - jnp.log(1.0 + jnp.exp(-jnp.abs(z)))


def _forget_cumsum(proj, b_pad, name):
    t = proj.shape[0]
    nb = t // CHUNK

    def body(f_ref, b_ref, c_ref):
        tri = _tri(True)
        b = b_ref[...]

        def blk(i, carry):
            rows = pl.ds(pl.multiple_of(i * CHUNK, CHUNK), CHUNK)
            cs = _dot_exact(tri, _log_sigmoid(f_ref[rows, :] + b)) + carry
            c_ref[rows, :] = cs
            return cs[CHUNK - 1:CHUNK, :]

        lax.fori_loop(0, nb, blk, jnp.zeros((1, HEAD_DIM), F32))

    return _call(body, name, (1,),
                 [pl.BlockSpec((t, HEAD_DIM), lambda i: (0, CB_F)), pl.BlockSpec((1, HEAD_DIM), lambda i: (0, 0))],
                 [pl.BlockSpec((t, HEAD_DIM), lambda i: (0, 0))], [jax.ShapeDtypeStruct((t, HEAD_DIM), F32)],
                 [proj, b_pad], sem=("arbitrary",), nbytes=2 * _nbytes((t, HEAD_DIM), F32))[0]


def _causal(block):
    r = lax.broadcasted_iota(jnp.int32, (block, block), 0)
    c = lax.broadcasted_iota(jnp.int32, (block, block), 1)
    return c <= r


def _lanes(x, width):
    return jnp.concatenate([x] * (width // HEAD_DIM), axis=1)


def _chunks(t, blk):
    return [slice(r * blk, (r + 1) * blk) for r in range(t // blk)]


def _lane_of_head(x, h):
    lane = lax.broadcasted_iota(jnp.int32, x.shape, 1)
    return jnp.broadcast_to(jnp.sum(jnp.where(lane == h, x, 0.0), axis=1, keepdims=True), x.shape)


def _causal_t(block):
    r = lax.broadcasted_iota(jnp.int32, (block, block), 0)
    c = lax.broadcasted_iota(jnp.int32, (block, block), 1)
    return r <= c


def _attn_fwd_t(proj, c_pad, c_row, name, comm=()):
    t = proj.shape[0]
    blk = min(ATTN_BLOCK, t)
    nq = t // blk
    scale = HEAD_DIM ** -0.5

    def body(q_ref, k_ref, v_ref, cp_ref, cr_ref, y_ref, e_ref, kb_s, vt_s, cb_s):
        h, i = pl.program_id(0), pl.program_id(1)

        @pl.when(i == 0)
        def _():
            for rows in _chunks(t, blk):
                kb_s[rows, :] = k_ref[rows, :].astype(BF16)
                vt_s[:, rows] = v_ref[rows, :].T.astype(BF16)
                cb_s[rows, :] = _lane_of_head(cp_ref[rows, :], h)

        q = (q_ref[...] * scale).astype(BF16)
        ci = cr_ref[:, pl.ds(pl.multiple_of(i * blk, blk), blk)]

        def step(j, carry, diagonal):
            m, l, acc = carry
            rows = pl.ds(pl.multiple_of(j * blk, blk), blk)
            s = lax.dot_general(kb_s[rows, :], q, _DIMS["nt"], preferred_element_type=F32)
            s = s + ci - _lanes(cb_s[rows, :], blk)
            if diagonal:
                s = jnp.where(_causal_t(blk), s, NEG)
            m_new = jnp.maximum(m, jnp.max(s, axis=0, keepdims=True))
            alpha = jnp.exp(m - m_new)
            p = jnp.exp(s - m_new)
            l = alpha * l + jnp.sum(p, axis=0, keepdims=True)
            acc = alpha * acc + jnp.dot(vt_s[:, rows], p.astype(BF16), preferred_element_type=F32)
            return m_new, l, acc

        init = (jnp.full((1, blk), NEG, F32), jnp.zeros((1, blk), F32), jnp.zeros((HEAD_DIM, blk), F32))
        carry = lax.fori_loop(0, i, lambda j, c: step(j, c, False), init)
        m, l, acc = step(i, carry, True)
        y_ref[...] = (acc / l).T
        e_ref[...] = ci - (m + jnp.log(l))

    nbytes = 3 * _nbytes((t, HEAD_DIM), F32) + 4 * _nbytes((blk, HEAD_DIM), F32) + 4 * _nbytes((blk, blk), F32)
    return _call(body, name, (N_HEADS, nq),
                 [pl.BlockSpec((blk, HEAD_DIM), lambda h, i: (i, CB_QKV + 3 * h)),
                  pl.BlockSpec((t, HEAD_DIM), lambda h, i: (0, CB_QKV + 3 * h + 1)),
                  pl.BlockSpec((t, HEAD_DIM), lambda h, i: (0, CB_QKV + 3 * h + 2)),
                  pl.BlockSpec((t, HEAD_DIM), lambda h, i: (0, 0)),
                  pl.BlockSpec((None, 1, t), lambda h, i: (h, 0, 0))],
                 [pl.BlockSpec((blk, HEAD_DIM), lambda h, i: (i, h)),
                  pl.BlockSpec((None, 1, blk), lambda h, i: (h, 0, i))],
                 [jax.ShapeDtypeStruct((t, D_MODEL), F32), jax.ShapeDtypeStruct((N_HEADS, 1, t), F32)],
                 [proj, proj, proj, c_pad, c_row],
                 scratch=[pltpu.VMEM((t, HEAD_DIM), BF16), pltpu.VMEM((HEAD_DIM, t), BF16),
                          pltpu.VMEM((t, HEAD_DIM), F32)],
                 sem=("parallel", "arbitrary"), nbytes=nbytes, comm=comm)


def _attn_bwd_t(proj, dy, e_row, delta_b, c_pad, dproj, name, deps=()):
    t = proj.shape[0]
    blk = min(ATTN_BLOCK, t)
    nb = t // blk
    scale = HEAD_DIM ** -0.5

    def body(q_ref, k_ref, v_ref, dy_ref, e_ref, dl_ref, cp_ref, dproj_in, dqkv_ref, dck_ref, dcq_ref,
             qb_s, qt_s, dyb_s, dyt_s, dl_s, dqt_acc):
        del dproj_in
        h, j = pl.program_id(0), pl.program_id(1)

        @pl.when(j == 0)
        def _():
            for rows in _chunks(t, blk):
                qs = q_ref[rows, :] * scale
                qb_s[rows, :] = qs.astype(BF16)
                qt_s[:, rows] = qs.T.astype(BF16)
                dyr = dy_ref[rows, :]
                dyb_s[rows, :] = dyr.astype(BF16)
                dyt_s[:, rows] = dyr.T.astype(BF16)
                dl_s[:, rows] = dl_ref[rows, :].T[0:1, :]
            dqt_acc[...] = jnp.zeros_like(dqt_acc)
            dcq_ref[...] = jnp.zeros_like(dcq_ref)

        kf = k_ref[...]
        kb, kt = kf.astype(BF16), kf.T.astype(BF16)
        vb = v_ref[...].astype(BF16)
        cj = _lanes(_lane_of_head(cp_ref[...], h), blk)

        def step(i, carry, diagonal):
            dkt, dvt, dkey = carry
            cols = pl.ds(pl.multiple_of(i * blk, blk), blk)
            s = lax.dot_general(kb, qb_s[cols, :], _DIMS["nt"], preferred_element_type=F32) + e_ref[:, cols] - cj
            if diagonal:
                s = jnp.where(_causal_t(blk), s, NEG)
            p = jnp.exp(s)
            dp = lax.dot_general(vb, dyb_s[cols, :], _DIMS["nt"], preferred_element_type=F32)
            ds = p * (dp - dl_s[:, cols])
            pb, dsb = p.astype(BF16), ds.astype(BF16)
            dvt = dvt + lax.dot_general(dyt_s[:, cols], pb, _DIMS["nt"], preferred_element_type=F32)
            dkt = dkt + lax.dot_general(qt_s[:, cols], dsb, _DIMS["nt"], preferred_element_type=F32)
            dqt_acc[:, cols] += jnp.dot(kt, dsb, preferred_element_type=F32) * scale
            dcq_ref[:, cols] += jnp.sum(ds, axis=0, keepdims=True)
            for k in range(blk // HEAD_DIM):
                dkey = dkey + ds[:, k * HEAD_DIM:(k + 1) * HEAD_DIM]
            return dkt, dvt, dkey

        zeros = jnp.zeros((HEAD_DIM, blk), F32)
        carry = step(j, (zeros, zeros, jnp.zeros((blk, HEAD_DIM), F32)), True)
        dkt, dvt, dkey = lax.fori_loop(j + 1, nb, lambda i, c: step(i, c, False), carry)
        mine = pl.ds(pl.multiple_of(j * blk, blk), blk)
        dqkv_ref[:, :HEAD_DIM] = dqt_acc[:, mine].T.astype(BF16)
        dqkv_ref[:, HEAD_DIM:2 * HEAD_DIM] = dkt.T.astype(BF16)
        dqkv_ref[:, 2 * HEAD_DIM:] = dvt.T.astype(BF16)
        dck_ref[...] = -jnp.sum(dkey.T, axis=0, keepdims=True)

    full = lambda cb: pl.BlockSpec((t, HEAD_DIM), lambda h, j: (0, cb(h)))
    head = lambda h: h
    row = pl.BlockSpec((None, 1, t), lambda h, j: (h, 0, 0))
    nbytes = (8 * _nbytes((t, HEAD_DIM), F32) + 4 * _nbytes((blk, HEAD_DIM), F32) + 6 * _nbytes((blk, blk), F32))
    return _call(body, name, (N_HEADS, nb),
                 [full(lambda h: CB_QKV + 3 * h),
                  pl.BlockSpec((blk, HEAD_DIM), lambda h, j: (j, CB_QKV + 3 * h + 1)),
                  pl.BlockSpec((blk, HEAD_DIM), lambda h, j: (j, CB_QKV + 3 * h + 2)),
                  full(head), row, full(head),
                  pl.BlockSpec((blk, HEAD_DIM), lambda h, j: (j, 0)),
                  pl.BlockSpec(memory_space=pl.ANY)],
                 [pl.BlockSpec((blk, QKV_WIDTH), lambda h, j: (j, (GATED + F_WIDTH) // QKV_WIDTH + h)),
                  pl.BlockSpec((None, 1, blk), lambda h, j: (h, 0, j)),
                  row],
                 [jax.ShapeDtypeStruct(dproj.shape, dproj.dtype), jax.ShapeDtypeStruct((N_HEADS, 1, t), F32),
                  jax.ShapeDtypeStruct((N_HEADS, 1, t), F32)],
                 [proj, proj, proj, dy, e_row, delta_b, c_pad, dproj],
                 scratch=[pltpu.VMEM((t, HEAD_DIM), BF16), pltpu.VMEM((HEAD_DIM, t), BF16),
                          pltpu.VMEM((t, HEAD_DIM), BF16), pltpu.VMEM((HEAD_DIM, t), BF16),
                          pltpu.VMEM((1, t), F32), pltpu.VMEM((HEAD_DIM, t), F32)],
                 sem=("parallel", "arbitrary"), nbytes=nbytes, aliases={7: 0}, deps=deps)


def _forget_bwd_t(proj, b_pad, dc, dproj, name):
    t = proj.shape[0]
    tr = min(ROW_BLOCK, t)
    nb = t // tr
    rev = lambda i: nb - 1 - i

    def body(f_ref, b_ref, dc_ref, dproj_in, df_ref, db_ref, run_ref):
        del dproj_in

        @pl.when(pl.program_id(0) == 0)
        def _():
            run_ref[...] = jnp.zeros_like(run_ref)
            db_ref[...] = jnp.zeros_like(db_ref)

        tri = _tri(False)
        b = b_ref[...]
        lane = lax.broadcasted_iota(jnp.int32, (CHUNK, HEAD_DIM), 1)
        df_ref[...] = jnp.zeros_like(df_ref)
        for n in reversed(range(tr // CHUNK)):
            rows = slice(n * CHUNK, (n + 1) * CHUNK)
            dlogf = _dot_exact(tri, dc_ref[rows, :]) + run_ref[...]
            run_ref[...] = dlogf[0:1, :]
            z = f_ref[rows, :] + b
            e = jnp.exp(-jnp.abs(z))
            sig_neg = jnp.where(z >= 0.0, e, 1.0) / (1.0 + e)
            df = jnp.where(lane < N_HEADS, dlogf * sig_neg, 0.0)
            df_ref[rows, :HEAD_DIM] = df.astype(BF16)
            db_ref[...] += jnp.sum(df, axis=0, keepdims=True)

    return _call(body, name, (nb,),
                 [pl.BlockSpec((tr, HEAD_DIM), lambda i: (rev(i), CB_F)), pl.BlockSpec((1, HEAD_DIM), lambda i: (0, 0)),
                  pl.BlockSpec((tr, HEAD_DIM), lambda i: (rev(i), 0)), pl.BlockSpec(memory_space=pl.ANY)],
                 [pl.BlockSpec((tr, F_WIDTH), lambda i: (rev(i), GATED // F_WIDTH)),
                  pl.BlockSpec((1, HEAD_DIM), lambda i: (0, 0))],
                 [jax.ShapeDtypeStruct(dproj.shape, dproj.dtype), jax.ShapeDtypeStruct((1, HEAD_DIM), F32)],
                 [proj, b_pad, dc, dproj], scratch=[pltpu.VMEM((1, HEAD_DIM), F32)],
                 sem=("arbitrary",), nbytes=4 * _nbytes((tr, HEAD_DIM), F32), aliases={3: 0})


_GELU_K = math.sqrt(2.0 / math.pi)
_GELU_C = 0.044715


def _gelu(x):
    t = jnp.tanh(_GELU_K * (x + _GELU_C * (x * x * x)))
    return 0.5 * x * (1.0 + t), t


def _gelu_grad(x, t):
    return 0.5 * (1.0 + t) + 0.5 * x * (1.0 - t * t) * (_GELU_K * (1.0 + 3.0 * _GELU_C * (x * x)))


def _layernorm_stats(a):
    mu = jnp.mean(a, axis=-1, keepdims=True)
    xc = a - mu
    r = lax.rsqrt(jnp.mean(xc * xc, axis=-1, keepdims=True) + EPS)
    return xc * r, r


def _group(g):
    return slice(g * CHUNK, (g + 1) * CHUNK)


def _fixed(shape):
    return pl.BlockSpec(shape, lambda i: (0,) * len(shape))


def _mix_fwd(proj, y_b, w_s, b_cols, g_v, name):
    t = proj.shape[0]
    step_rows = min(MIX_ROWS, t)

    def body(u_ref, vs_ref, ga_ref, gb_ref, yb_ref, w_ref, b_ref, gv_ref, o_ref):
        mask = _causal(CHUNK)
        w = [jnp.where(mask, w_ref[g], 0.0).astype(BF16) for g in range(N_GROUPS)]
        for rows in _chunks(step_rows, CHUNK):
            a_u, _ = _gelu(u_ref[rows, :])
            a_v, _ = _gelu(vs_ref[rows, :])
            xhat, _ = _layernorm_stats(a_v)
            vn = (xhat * gv_ref[...]).astype(BF16)
            sa, sb = _sigmoid(ga_ref[rows, :]), _sigmoid(gb_ref[rows, :])
            yb = yb_ref[rows, :]
            for g in range(N_GROUPS):
                cols = _group(g)
                mixed = jnp.dot(w[g], vn[:, cols], preferred_element_type=F32) + b_ref[:, g:g + 1]
                o_ref[rows, cols] = (sa[:, cols] * (a_u[:, cols] * mixed) + sb[:, cols] * yb[:, cols]).astype(BF16)

    gated = lambda k: pl.BlockSpec((step_rows, D_MODEL), lambda i: (i, k))
    return _call(body, name, (t // step_rows,),
                 [gated(0), gated(1), gated(2), gated(3), _row_spec(step_rows, D_MODEL),
                  _fixed((N_GROUPS, CHUNK, CHUNK)), _fixed((CHUNK, N_GROUPS)), _fixed((1, D_MODEL))],
                 [_row_spec(step_rows, D_MODEL)], [jax.ShapeDtypeStruct((t, D_MODEL), BF16)],
                 [proj, proj, proj, proj, y_b, w_s, b_cols, g_v], sem=("parallel",),
                 nbytes=16 * _nbytes((step_rows, D_MODEL), F32))[0]


def _mix_bwd(proj, y_b, dmerged, w_s, b_cols, g_v, sel, name, deps=()):
    t = proj.shape[0]
    step_rows = min(MIX_BWD_ROWS, t)

    def body(u_ref, vs_ref, ga_ref, gb_ref, yb_ref, dm_ref, w_ref, b_ref, gv_ref, sel_ref,
             dg_ref, dyb_ref, delta_ref, dw_ref, dbt_ref, dgv_ref):
        @pl.when(pl.program_id(0) == 0)
        def _():
            dw_ref[...] = jnp.zeros_like(dw_ref)
            dbt_ref[...] = jnp.zeros_like(dbt_ref)
            dgv_ref[...] = jnp.zeros_like(dgv_ref)

        gv = gv_ref[...]
        mask = _causal(CHUNK)
        w = [jnp.where(mask, w_ref[g], 0.0).astype(BF16) for g in range(N_GROUPS)]
        for rows in _chunks(step_rows, CHUNK):
            u, vs = u_ref[rows, :], vs_ref[rows, :]
            a_u, t_u = _gelu(u)
            a_v, t_v = _gelu(vs)
            xhat, r = _layernorm_stats(a_v)
            vn = (xhat * gv).astype(BF16)
            sa, sb = _sigmoid(ga_ref[rows, :]), _sigmoid(gb_ref[rows, :])
            yb, dm = yb_ref[rows, :], dm_ref[rows, :].astype(F32)
            dyb = dm * sb
            dyb_ref[rows, :] = dyb
            dyb_yb = dyb * yb
            dg_ref[rows, 3 * D_MODEL:] = (dm * yb * (sb * (1.0 - sb))).astype(BF16)
            dya = dm * sa
            dmixed_parts, dvn_parts = [], []
            for g in range(N_GROUPS):
                cols = _group(g)
                delta_ref[rows, cols] = jnp.broadcast_to(jnp.sum(dyb_yb[:, cols], axis=1, keepdims=True),
                                                         (CHUNK, CHUNK))
                mixed = jnp.dot(w[g], vn[:, cols], preferred_element_type=F32) + b_ref[:, g:g + 1]
                y_a = a_u[:, cols] * mixed
                dg_ref[rows, 2 * D_MODEL + g * CHUNK:2 * D_MODEL + (g + 1) * CHUNK] = (
                    dm[:, cols] * y_a * (sa[:, cols] * (1.0 - sa[:, cols]))).astype(BF16)
                dg_ref[rows, cols] = (dya[:, cols] * mixed * _gelu_grad(u[:, cols], t_u[:, cols])).astype(BF16)
                dmixed = dya[:, cols] * a_u[:, cols]
                dmb = dmixed.astype(BF16)
                dw = lax.dot_general(dmb, vn[:, cols], _DIMS["nt"], preferred_element_type=F32)
                dw_ref[g] += jnp.where(mask, dw, 0.0)
                dvn_parts.append(lax.dot_general(w[g], dmb, _DIMS["tn"], preferred_element_type=F32))
                dmixed_parts.append(dmixed)
            dmixed_all = jnp.concatenate(dmixed_parts, axis=1)
            dvn = jnp.concatenate(dvn_parts, axis=1)
            dbt_ref[...] += _dot_exact_rhs(dmixed_all, sel_ref[...])
            dgv_ref[...] += jnp.sum(dvn * xhat, axis=0, keepdims=True)
            dxh = dvn * gv
            da_v = r * (dxh - jnp.mean(dxh, axis=-1, keepdims=True)
                        - xhat * jnp.mean(dxh * xhat, axis=-1, keepdims=True))
            dg_ref[rows, D_MODEL:2 * D_MODEL] = (da_v * _gelu_grad(vs, t_v)).astype(BF16)

    row = lambda width: _row_spec(step_rows, width)
    gated = lambda k: pl.BlockSpec((step_rows, D_MODEL), lambda i: (i, k))
    return _call(body, name, (t // step_rows,),
                 [gated(0), gated(1), gated(2), gated(3), row(D_MODEL), row(D_MODEL),
                  _fixed((N_GROUPS, CHUNK, CHUNK)), _fixed((CHUNK, N_GROUPS)), _fixed((1, D_MODEL)),
                  _fixed((D_MODEL, HEAD_DIM))],
                 [row(GATED), row(D_MODEL), row(D_MODEL),
                  _fixed((N_GROUPS, CHUNK, CHUNK)), _fixed((CHUNK, HEAD_DIM)), _fixed((1, D_MODEL))],
                 [jax.ShapeDtypeStruct((t, IN_PAD), BF16), jax.ShapeDtypeStruct((t, D_MODEL), F32),
                  jax.ShapeDtypeStruct((t, D_MODEL), F32), jax.ShapeDtypeStruct((N_GROUPS, CHUNK, CHUNK), F32),
                  jax.ShapeDtypeStruct((CHUNK, HEAD_DIM), F32), jax.ShapeDtypeStruct((1, D_MODEL), F32)],
                 [proj, proj, proj, proj, y_b, dmerged, w_s, b_cols, g_v, sel], sem=("arbitrary",),
                 nbytes=24 * _nbytes((step_rows, D_MODEL), F32), deps=deps)


def _adamw_update(w_ref, g_ref, m_ref, v_ref, d_ref, nm_ref, nv_ref):
    gv = g_ref[...]
    nm = ADAM_B1 * m_ref[...] + (1.0 - ADAM_B1) * gv
    nv = ADAM_B2 * v_ref[...] + (1.0 - ADAM_B2) * (gv * gv)
    m_hat = nm / (1.0 - ADAM_B1 ** ADAM_STEP)
    v_hat = nv / (1.0 - ADAM_B2 ** ADAM_STEP)
    d_ref[...] = -ADAM_LR * (m_hat / (jnp.sqrt(v_hat) + ADAM_EPS) + ADAM_WD * w_ref[...])
    nm_ref[...] = nm
    nv_ref[...] = nv


def _adamw(w, g, m, v, name):
    r, c = w.shape
    tr = _tile(r, 256, unit=8)

    def body(w_ref, g_ref, m_ref, v_ref, d_ref, nm_ref, nv_ref):
        _adamw_update(w_ref, g_ref, m_ref, v_ref, d_ref, nm_ref, nv_ref)

    spec = pl.BlockSpec((tr, c), lambda i: (i, 0))
    shape = jax.ShapeDtypeStruct((r, c), F32)
    return _call(body, name, (r // tr,), [spec] * 4, [spec] * 3, [shape] * 3, [w, g, m, v], sem=("parallel",),
                 nbytes=7 * _nbytes((tr, max(c, 128)), F32))


def _adamw_cols(w, g, m, v, name):
    layers, r, c = w.shape
    tc = 256

    def body(w_ref, g_ref, m_ref, v_ref, d_ref, nm_ref, nv_ref):
        _adamw_update(w_ref, g_ref, m_ref, v_ref, d_ref, nm_ref, nv_ref)

    spec = pl.BlockSpec((None, r, tc), lambda l, j: (l, 0, j))
    shape = jax.ShapeDtypeStruct(w.shape, F32)
    return _call(body, name, (layers, c // tc), [spec] * 4, [spec] * 3, [shape] * 3, [w, g, m, v],
                 sem=("parallel", "parallel"), nbytes=7 * _nbytes((r, tc), F32))


def _pad_rows(a, rows):
    return jnp.pad(a, ((0, rows - a.shape[0]), (0, 0)))


def _unpack_in(gathered):
    i = jnp.arange(16)[None, :, None]
    b = jnp.arange(N_DEV)[:, None, None]
    head = jnp.where(i < b, jnp.roll(gathered[:, IN_STRIDE:IN_STRIDE + 16], 1, axis=0), gathered[:, :16])
    nat = jnp.concatenate([head, gathered[:, 16:IN_STRIDE]], axis=1).reshape(N_DEV * IN_STRIDE, D_MODEL)
    qkv = nat[2 * D_MODEL:5 * D_MODEL].reshape(3, N_HEADS, HEAD_DIM, D_MODEL)
    qkv = jnp.transpose(qkv, (1, 0, 2, 3)).reshape(3 * D_MODEL, D_MODEL)
    f = _pad_rows(gathered[N_DEV - 1, IN_STRIDE:IN_STRIDE + N_HEADS], F_WIDTH)
    return jnp.concatenate([nat[:2 * D_MODEL], nat[5 * D_MODEL:7 * D_MODEL], f, qkv], axis=0)


def _pack_grad_in(dw_in_t):
    qkv = dw_in_t[GATED + F_WIDTH:].reshape(N_HEADS, 3, HEAD_DIM, D_MODEL)
    qkv = jnp.transpose(qkv, (1, 0, 2, 3)).reshape(3 * D_MODEL, D_MODEL)
    return jnp.concatenate([dw_in_t[:2 * D_MODEL], qkv, dw_in_t[2 * D_MODEL:GATED],
                            dw_in_t[GATED:GATED + IN_ROWS - N_DEV * IN_STRIDE]], axis=0)


def kernel(x, mix_pre_g, w_in, b_forget, sgu_norm_g, w_spatial, b_spatial, w_out, mix_post_g, ffn_pre_g, w_gate, w_up, w_down, ffn_post_g, loss_target, m_mix_pre_g, m_w_in, m_b_forget, m_sgu_norm_g, m_w_spatial, m_b_spatial, m_w_out, m_mix_post_g, m_ffn_pre_g, m_w_gate, m_w_up, m_w_down, m_ffn_post_g, v_mix_pre_g, v_w_in, v_b_forget, v_sgu_norm_g, v_w_spatial, v_b_spatial, v_w_out, v_mix_post_g, v_ffn_pre_g, v_w_gate, v_w_up, v_w_down, v_ffn_post_g):
    depth = w_in.shape[0]
    assert depth == 2
    t = x.shape[1]
    x0 = x.reshape(t, D_MODEL)
    target = loss_target.reshape(t, D_MODEL)
    sel = (jnp.arange(D_MODEL)[:, None] // HEAD_DIM == jnp.arange(HEAD_DIM)[None, :]).astype(BF16)
    vec = lambda a, l: a[l][None, :]
    zero_row = jnp.zeros((1, D_MODEL), F32)

    me = 4 * lax.axis_index("x") + 2 * lax.axis_index("y") + lax.axis_index("c")
    p_in = [lax.dynamic_update_slice(jnp.zeros((P_IN, D_MODEL), BF16), w_in[l].T.astype(BF16), (me, 0))
            for l in range(depth)]
    p_rest = [[w_out[l].astype(BF16), jnp.stack([w_gate[l].T, w_up[l].T]).astype(BF16), w_down[l].astype(BF16)]
              for l in range(depth)]
    w_in_t = [None] * depth
    w_rest = [None] * depth
    w_in_t[0] = _unpack_in(_exchange(("gather", 0), p_in[0], "gather_w_in_0")[0])

    saved = []
    xl = x0
    h = _rms_fwd(xl, vec(mix_pre_g, 0), "rms_in_0")
    dy = loss = None
    for l in range(depth):
        b_pad = jnp.pad(b_forget[l], (0, HEAD_DIM - N_HEADS))[None, :]
        b_cols = b_spatial[l].T
        proj = _matmul(h, w_in_t[l], "nt", F32, f"proj_{l}")
        c_pad = _forget_cumsum(proj, b_pad, f"forget_cumsum_{l}")
        c_row = c_pad[:, :N_HEADS].T[:, None, :]
        riders = [(("gather", 0), p_rest[l][0]), (("gather", 1), p_rest[l][1]), (("gather", 0), p_rest[l][2])]
        riders += [(("gather", 0), p_in[l + 1])] if l + 1 < depth else []
        y_b, e_row, *arrived = _attn_fwd_t(proj, c_pad, c_row, f"attn_fwd_{l}", comm=riders)
        w_rest[l] = (arrived[0].reshape(D_MODEL, D_MODEL), arrived[1].reshape(2 * D_FF, D_MODEL),
                     arrived[2].reshape(D_FF, D_MODEL))
        if l + 1 < depth:
            w_in_t[l + 1] = _unpack_in(arrived[3])
        w_o, w_gu_t, w_d = w_rest[l]
        merged = _mix_fwd(proj, y_b, w_spatial[l], b_cols, vec(sgu_norm_g, l), f"mix_fwd_{l}")
        o, x1, h2 = _matmul_post_norm(merged, w_o, xl, vec(mix_post_g, l), f"out_proj_{l}", g_next=vec(ffn_pre_g, l))
        gu = _matmul(h2, w_gu_t, "nt", BF16, f"ffn_gu_{l}")
        act = _swiglu_fwd(gu, f"swiglu_fwd_{l}")
        if l + 1 < depth:
            dn, x_next, h_next = _matmul_post_norm(act, w_d, x1, vec(ffn_post_g, l), f"ffn_down_{l}",
                                                   g_next=vec(mix_pre_g, l + 1))
        else:
            dn, dy, loss = _matmul_post_norm(act, w_d, x1, vec(ffn_post_g, l), f"ffn_down_{l}", target=target)
        saved.append(dict(x0=xl, h=h, proj=proj, b_pad=b_pad, b_cols=b_cols, c_pad=c_pad, y_b=y_b, e_row=e_row,
                          merged=merged, o=o, x1=x1, h2=h2, gu=gu, act=act, dn=dn))
        if l + 1 < depth:
            xl, h = x_next, h_next

    g_small = [None] * depth
    flights = {}
    late_token = None
    dx = dy
    dd, dg_ffn_post = _norm_bwd(dx, f"bwd_ffn_post_{depth - 1}", post=(saved[-1]["dn"], vec(ffn_post_g, depth - 1)))
    for l in reversed(range(depth)):
        s = saved[l]
        w_o, w_gu_t, w_d = w_rest[l]
        dact = _matmul(dd, w_d, "nt", BF16, f"d_act_{l}")
        dw_d = _matmul(s["act"], dd, "tn", BF16, f"dw_down_{l}")
        dgu = _swiglu_bwd(s["gu"], dact, f"swiglu_bwd_{l}")
        dh2 = _matmul(dgu, w_gu_t, "nn", F32, f"d_h2_{l}")
        dw_gu_t = _matmul(dgu, s["h2"], "tn", BF16, f"dw_gu_{l}")
        dx1, dg_ffn_pre, do, dg_mix_post = _norm_bwd(
            dx, f"bwd_post_mix_{l}", pre=(dh2, s["x1"], vec(ffn_pre_g, l)), post=(s["o"], vec(mix_post_g, l)))
        dmerged = _matmul(do, w_o, "nt", F32, f"d_merged_{l}")
        dw_o = _matmul(s["merged"], do, "tn", BF16, f"dw_out_{l}")
        tokens = []
        for key, dw, base, rows in [("out", dw_o, 0, SH_OUT), ("gate", dw_gu_t, 0, SH_FF), ("up", dw_gu_t, D_FF, SH_FF),
                                    ("down", dw_d, 0, SH_FF)]:
            flights[(key, l)], token = _exchange_start(("windows", base, rows, rows), dw, f"start_grad_{key}_{l}")
            tokens.append(token)
        dproj, dyb, delta_b, dw_s, dbt, dg_v = _mix_bwd(
            s["proj"], s["y_b"], dmerged, w_spatial[l], s["b_cols"], vec(sgu_norm_g, l), sel, f"mix_bwd_{l}",
            deps=tokens)
        db_s = dbt[:, :N_GROUPS].T.reshape(1, D_MODEL)
        tokens = []
        if l == 0:
            g_small[0] = jnp.concatenate([zero_row, dg_v, dg_mix_post, dg_ffn_pre, dg_ffn_post, db_s, zero_row,
                                          zero_row, dw_s.reshape(CHUNK, D_MODEL)], axis=0)
            flights["small"], token = _exchange_start(("spread",),jnp.concatenate([g_small[1], g_small[0]], axis=0),
                                                      "start_small_grads")
            tokens = [token]
        dproj, dc_key, dc_query = _attn_bwd_t(s["proj"], dyb, s["e_row"], delta_b, s["c_pad"], dproj,
                                              f"attn_bwd_{l}", deps=tokens)
        dc = jnp.pad((dc_key + dc_query)[:, 0, :].T, ((0, 0), (0, HEAD_DIM - N_HEADS)))
        dproj, db_f = _forget_bwd_t(s["proj"], s["b_pad"], dc, dproj, f"forget_bwd_{l}")
        db_f_row = jnp.pad(db_f, ((0, 0), (0, D_MODEL - HEAD_DIM)))
        dw_in_t = _matmul(dproj, s["h"], "tn", BF16, f"dw_in_{l}")
        flights[("in", l)], token = _exchange_start(("windows", 0, IN_STRIDE, P_IN), _pack_grad_in(dw_in_t),
                                                    f"start_grad_in_{l}")
        dh = _matmul(dproj, w_in_t[l], "nn", F32, f"d_h_{l}", deps=[token])
        if l > 0:
            dx, dg_mix_pre, dd, dg_ffn_post_below = _norm_bwd(
                dx1, f"bwd_pre_mix_{l}", pre=(dh, s["x0"], vec(mix_pre_g, l)),
                post=(saved[l - 1]["dn"], vec(ffn_post_g, l - 1)))
            loss_row = jnp.pad(loss, ((0, 0), (0, D_MODEL - 1)))
            g_small[l] = jnp.concatenate([dg_mix_pre, dg_v, dg_mix_post, dg_ffn_pre, dg_ffn_post, db_s, db_f_row,
                                          loss_row, dw_s.reshape(CHUNK, D_MODEL)], axis=0)
            dg_ffn_post = dg_ffn_post_below
        else:
            dx, dg_mix_pre = _norm_bwd(dx1, f"bwd_pre_mix_{l}", pre=(dh, s["x0"], vec(mix_pre_g, l)))
            late_rows = jnp.concatenate([dg_mix_pre, db_f_row] + [zero_row] * 6, axis=0)
            flights["late"], late_token = _exchange_start(("spread",),late_rows, "start_late_small_grads")
    grad_x = dx.reshape(x.shape)

    landed, after = {}, [grad_x, late_token]
    rest = ["out", "gate", "up", "down"]
    for key in [(n, 1) for n in rest] + [("in", 1)] + [(n, 0) for n in rest] + ["small"]:
        label = key if isinstance(key, str) else f"grad_{key[0]}_{key[1]}"
        landed[key] = _exchange_wait(flights[key], after, f"wait_{label}")
        after = [landed[key]]
    small_sum = _sum_slots(landed["small"], S_ROWS, "sum_small_grads")
    sm = small_sum.reshape(depth, S_ROWS, D_MODEL)[::-1]
    loss = sm[depth - 1, 7, 0]
    g = {
        "sgu_norm_g": sm[:, 1], "mix_post_g": sm[:, 2],
        "ffn_pre_g": sm[:, 3], "ffn_post_g": sm[:, 4], "b_spatial": sm[:, 5].reshape(depth, N_GROUPS, CHUNK),
        "w_spatial": sm[:, 8:].reshape(depth, N_GROUPS, CHUNK, CHUNK),
    }
    for n in rest:
        g["w_" + n] = jnp.stack([_sum_slots(landed[(n, l)], landed[(n, l)].shape[1], f"sum_grad_{n}_{l}")
                                 for l in range(depth)])

    names = ["mix_pre_g", "w_in", "b_forget", "sgu_norm_g", "w_spatial", "b_spatial", "w_out", "mix_post_g",
             "ffn_pre_g", "w_gate", "w_up", "w_down", "ffn_post_g"]
    ws = dict(mix_pre_g=mix_pre_g, w_in=w_in, b_forget=b_forget, sgu_norm_g=sgu_norm_g, w_spatial=w_spatial,
              b_spatial=b_spatial, w_out=w_out, mix_post_g=mix_post_g, ffn_pre_g=ffn_pre_g, w_gate=w_gate, w_up=w_up,
              w_down=w_down, ffn_post_g=ffn_post_g)
    ms = dict(mix_pre_g=m_mix_pre_g, w_in=m_w_in, b_forget=m_b_forget, sgu_norm_g=m_sgu_norm_g, w_spatial=m_w_spatial,
              b_spatial=m_b_spatial, w_out=m_w_out, mix_post_g=m_mix_post_g, ffn_pre_g=m_ffn_pre_g, w_gate=m_w_gate,
              w_up=m_w_up, w_down=m_w_down, ffn_post_g=m_ffn_post_g)
    vs = dict(mix_pre_g=v_mix_pre_g, w_in=v_w_in, b_forget=v_b_forget, sgu_norm_g=v_sgu_norm_g, w_spatial=v_w_spatial,
              b_spatial=v_b_spatial, w_out=v_w_out, mix_post_g=v_mix_post_g, ffn_pre_g=v_ffn_pre_g, w_gate=v_w_gate,
              w_up=v_w_up, w_down=v_w_down, ffn_post_g=v_ffn_post_g)
    updates = {}
    last = ["w_in", "mix_pre_g", "b_forget"]
    transposed = ["w_in", "w_gate", "w_up"]
    for n in [n for n in names if n not in last] + last:
        if n == last[0]:
            sum_in_1 = _sum_slots(landed[("in", 1)], 304, "sum_grad_in_1")
            done = [u[0] for u in updates.values()] + [sum_in_1]
            sum_in_0 = _sum_slots(_exchange_wait(flights[("in", 0)], done, "wait_grad_in_0"), 304, "sum_grad_in_0")
            late = _sum_slots(_exchange_wait(flights["late"], [sum_in_0], "wait_late"), 8, "sum_late_small_grads")
            g["w_in"] = lax.dynamic_slice(jnp.stack([sum_in_0, sum_in_1]), (0, me, 0), (depth, SH_IN, D_MODEL))
            g["mix_pre_g"] = jnp.stack([late[0], sm[1, 0]])
            g["b_forget"] = jnp.stack([late[1, :N_HEADS], sm[1, 6, :N_HEADS]])
        if n in transposed:
            view = lambda a: jnp.swapaxes(a, 1, 2)
            updates[n] = [view(u) for u in _adamw_cols(view(ws[n]), g[n], view(ms[n]), view(vs[n]), f"adamw_{n}")]
            g[n] = view(g[n])
        else:
            flat = (-1, ws[n].shape[-1])
            updates[n] = [u.reshape(ws[n].shape) for u in
                          _adamw(ws[n].reshape(flat), g[n].reshape(flat), ms[n].reshape(flat), vs[n].reshape(flat),
                                 f"adamw_{n}")]
    deltas = [updates[n][0] for n in names]
    new_m = [updates[n][1] for n in names]
    new_v = [updates[n][2] for n in names]
    grads = [g[n].reshape(ws[n].shape) for n in names]
    return (loss, grad_x, *grads, *deltas, *new_m, *new_v)
```

```python
import math

import jax
import jax.numpy as jnp
from jax import lax
from jax.experimental import pallas as pl
from jax.experimental.pallas import tpu as pltpu

F32 = jnp.float32
BF16 = jnp.bfloat16

N_DEV = 8
D_MODEL = 1024
N_HEADS = 8
HEAD_DIM = 128
CHUNK = 128
N_GROUPS = 8
D_FF = 2816
IN_WIDTH = 7 * D_MODEL + N_HEADS
IN_PAD = 7680
EPS = 1e-6
GATED = 4 * D_MODEL
F_WIDTH = 512
QKV_WIDTH = 3 * HEAD_DIM
CB_F = GATED // HEAD_DIM
CB_QKV = (GATED + F_WIDTH) // HEAD_DIM
assert GATED + F_WIDTH + N_HEADS * QKV_WIDTH == IN_PAD and (GATED + F_WIDTH) % QKV_WIDTH == 0

ADAM_LR, ADAM_B1, ADAM_B2, ADAM_EPS, ADAM_WD, ADAM_STEP = 0.001, 0.9, 0.999, 1e-08, 0.01, 10

SH_IN = IN_WIDTH // N_DEV
SH_OUT = D_MODEL // N_DEV
SH_FF = D_FF // N_DEV
P_IN = 912
IN_STRIDE = 7 * HEAD_DIM
IN_ROWS = IN_STRIDE * (N_DEV - 1) + P_IN
S_ROWS = 136

ATTN_BLOCK = 1024
ROW_BLOCK = 512
MIX_ROWS = 512
MIX_BWD_ROWS = 256
MM_TM, MM_TN, MM_TK = 1536, 1536, 2048
VMEM_CAP = 56 << 20
NEG = -1e30


def _tile(n, cap, unit=128):
    if n <= cap:
        return n
    best = None
    for t in range(unit, cap + 1, unit):
        if n % t == 0:
            best = t
    assert best is not None, (n, cap)
    return best


def _nbytes(shape, dtype):
    return math.prod(shape) * jnp.dtype(dtype).itemsize


_HBM = pl.BlockSpec(memory_space=pltpu.HBM)
MESH = pl.DeviceIdType.MESH

_N_REMOTE = N_DEV - 1


def _exchange_shapes(kind, x):
    if kind[0] == "gather":
        shape = x.shape[:kind[1]] + (N_DEV,) + x.shape[kind[1]:]
    elif kind[0] == "spread":
        shape = (N_DEV,) + x.shape
    else:
        shape = (N_DEV, kind[3]) + x.shape[1:]
    return [jax.ShapeDtypeStruct(shape, x.dtype)]


def _exchange_sems(kind):
    del kind
    return [pltpu.SemaphoreType.DMA((_N_REMOTE,)), pltpu.SemaphoreType.DMA((_N_REMOTE,)), pltpu.SemaphoreType.DMA]


def _plan(kind, x_ref, outs, sems):
    send_sems, recv_sems, local_sem = sems
    x, y, c = lax.axis_index("x"), lax.axis_index("y"), lax.axis_index("c")

    def remote(src, dst, k, to):
        return pltpu.make_async_remote_copy(src_ref=src, dst_ref=dst, send_sem=send_sems.at[k], recv_sem=recv_sems.at[k],
                                            device_id=to, device_id_type=MESH)

    sibling = (x, y, 1 - c)
    if kind[0] == "gather":
        out, = outs
        slot = lambda px, py, pc: out.at[(slice(None),) * kind[1] + (4 * px + 2 * py + pc,)]
        chips = [(1 - x, y), (x, 1 - y), (1 - x, 1 - y)]
        local = [pltpu.make_async_copy(x_ref, slot(x, y, c), local_sem)]
        first = [remote(x_ref, slot(x, y, c), 0, sibling)]
        first += [remote(x_ref, slot(x, y, c), 1 + k, (*chip, c)) for k, chip in enumerate(chips)]
        relays = [(remote(x_ref, slot(*chip, c), 1 + k, (*chip, c)), remote(slot(*chip, c), slot(*chip, c), 4 + k, sibling))
                  for k, chip in enumerate(chips)]
        arrivals = [remote(x_ref, slot(x, y, 1 - c), 0, sibling)]
        arrivals += [remote(x_ref, slot(*chip, 1 - c), 4 + k, sibling) for k, chip in enumerate(chips)]
        return local, first, relays, arrivals
    out, = outs
    mine = 4 * x + 2 * y + c
    if kind[0] == "spread":
        src = lambda d: x_ref
    else:
        _, base, stride, length = kind
        src = lambda d: x_ref.at[pl.ds(pl.multiple_of(base + stride * d, 16), length)]
    first, arrivals = [], []
    for j in range(1, N_DEV):
        peer = (1 - x if j & 4 else x, 1 - y if j & 2 else y, 1 - c if j & 1 else c)
        theirs = 4 * peer[0] + 2 * peer[1] + peer[2]
        first.append(remote(src(theirs), out.at[mine], j - 1, peer))
        arrivals.append(remote(src(theirs), out.at[theirs], j - 1, peer))
    local = [] if local_sem is None else [pltpu.make_async_copy(src(mine), out.at[mine], local_sem)]
    return local, first, [], arrivals


def _start(plan):
    local, first, _, _ = plan
    for cp in local + first:
        cp.start()


def _finish(plan):
    local, first, relays, arrivals = plan
    for arrival, onward in relays:
        arrival.wait_recv()
        onward.start()
    for cp in arrivals:
        cp.wait_recv()
    for cp in first + [onward for _, onward in relays]:
        cp.wait_send()
    for cp in local:
        cp.wait()


def _direct_copies(kind, x_ref, land_ref, send_sems, recv_sems):
    _, first, _, arrivals = _plan(kind, x_ref, [land_ref], (send_sems, recv_sems, None))
    return first, arrivals


_SEM = pl.BlockSpec(memory_space=pltpu.SEMAPHORE)
_DATAFLOW = pltpu.SideEffectType.DATAFLOW_SIDE_EFFECTING


def _exchange_start(kind, x, name):
    me = 4 * lax.axis_index("x") + 2 * lax.axis_index("y") + lax.axis_index("c")
    own = x if kind[0] == "spread" else lax.dynamic_slice_in_dim(x, kind[1] + kind[2] * me, kind[3], 0)
    shape = _exchange_shapes(kind, x)[0].shape
    land = lax.dynamic_update_slice_in_dim(lax.empty(shape, x.dtype), own[None], me, 0)

    def body(x_ref, land_ref, send_sems, recv_sems, x_thru, land_thru, token):
        del x_thru, land_thru
        for cp in _direct_copies(kind, x_ref, land_ref, send_sems, recv_sems)[0]:
            cp.start()
        token[...] = jnp.zeros_like(token)

    *handle, token = pl.pallas_call(
        body, name=name,
        out_shape=(pltpu.SemaphoreType.DMA((_N_REMOTE,)), pltpu.SemaphoreType.DMA((_N_REMOTE,)),
                   pltpu.HBM(x.shape, x.dtype), pltpu.HBM(land.shape, land.dtype), jax.ShapeDtypeStruct((8, 128), F32)),
        in_specs=(_HBM, _HBM), out_specs=(_SEM, _SEM, _HBM, _HBM, pl.BlockSpec(memory_space=pltpu.VMEM)),
        input_output_aliases={0: 2, 1: 3}, compiler_params=pltpu.CompilerParams(has_side_effects=_DATAFLOW),
    )(pltpu.with_memory_space_constraint(x, pltpu.HBM), pltpu.with_memory_space_constraint(land, pltpu.HBM))
    return (kind, *handle), token


def _exchange_wait(handle, after, name):
    kind, send_sems, recv_sems, x_thru, land_thru = handle

    def body(x_ref, land_ref, send_sems, recv_sems, *rest):
        first, arrivals = _direct_copies(kind, x_ref, land_ref, send_sems, recv_sems)
        for cp in first:
            cp.wait_send()
        for cp in arrivals:
            cp.wait_recv()

    return pl.pallas_call(
        body, name=name,
        out_shape=(pltpu.HBM(x_thru.shape, x_thru.dtype), pltpu.HBM(land_thru.shape, land_thru.dtype)),
        in_specs=(_HBM, _HBM, _SEM, _SEM) + (pl.BlockSpec(memory_space=pl.ANY),) * len(after), out_specs=(_HBM, _HBM),
        input_output_aliases={0: 0, 1: 1}, compiler_params=pltpu.CompilerParams(has_side_effects=_DATAFLOW),
    )(x_thru, land_thru, send_sems, recv_sems, *after)[1]


def _exchange(kind, x, name):
    n_out = len(_exchange_shapes(kind, x))

    def body(x_ref, *refs):
        plan = _plan(kind, x_ref, refs[:n_out], refs[n_out:])
        _start(plan)
        _finish(plan)

    return pl.pallas_call(body, name=name, out_shape=_exchange_shapes(kind, x), in_specs=[_HBM],
                          out_specs=[_HBM] * n_out, scratch_shapes=_exchange_sems(kind))(x)


def _call(body, name, grid, in_specs, out_specs, out_shape, args, scratch=(), sem=None, nbytes=0, aliases=None,
          comm=(), deps=()):
    in_specs, out_specs, out_shape, args, scratch = (list(in_specs), list(out_specs), list(out_shape), list(args),
                                                     list(scratch))
    if deps:
        n_real, n_deps, unordered = len(args), len(deps), body

        def body(*refs):
            unordered(*refs[:n_real], *refs[n_real + n_deps:])

        in_specs += [pl.BlockSpec(memory_space=pl.ANY)] * n_deps
        args += list(deps)
    if comm:
        n_in, n_out, n_scr, n_ops = len(args), len(out_shape), len(scratch), len(comm)
        kinds = [kind for kind, _ in comm]
        shapes = [_exchange_shapes(kind, x) for kind, x in comm]
        inner = body

        def body(*refs):
            ins, refs = refs[:n_in], refs[n_in:]
            cins, refs = refs[:n_ops], refs[n_ops:]
            outs, refs = refs[:n_out], refs[n_out:]
            couts = []
            for sh in shapes:
                couts.append(refs[:len(sh)])
                refs = refs[len(sh):]
            own_scratch, sems = refs[:n_scr], refs[n_scr:]
            first = last = None
            for axis, size in enumerate(grid):
                at_start, at_end = pl.program_id(axis) == 0, pl.program_id(axis) == size - 1
                first = at_start if first is None else first & at_start
                last = at_end if last is None else last & at_end
            plans = [_plan(kinds[o], cins[o], couts[o], sems[3 * o:3 * o + 3]) for o in range(n_ops)]

            @pl.when(first)
            def _():
                for p in plans:
                    _start(p)

            inner(*ins, *outs, *own_scratch)

            @pl.when(last)
            def _():
                for p in plans:
                    _finish(p)

        in_specs += [_HBM] * n_ops
        args += [x for _, x in comm]
        for kind, sh in zip(kinds, shapes):
            out_shape += sh
            out_specs += [_HBM] * len(sh)
            scratch += _exchange_sems(kind)
        sem = ("arbitrary",) * len(grid)
    limit = int(min(max(2 * nbytes + (8 << 20), 32 << 20), VMEM_CAP))
    return pl.pallas_call(
        body, name=name, grid=grid, in_specs=in_specs, out_specs=out_specs, out_shape=out_shape,
        scratch_shapes=scratch, input_output_aliases=aliases or {},
        compiler_params=pltpu.CompilerParams(dimension_semantics=sem, vmem_limit_bytes=limit),
    )(*args)


def _sum_slots(x, tr, name):
    n, r, c = x.shape

    def body(x_ref, o_ref):
        acc = x_ref[0].astype(F32)
        for d in range(1, n):
            acc = acc + x_ref[d].astype(F32)
        o_ref[...] = acc

    return _call(body, name, (r // tr,), [pl.BlockSpec((n, tr, c), lambda i: (0, i, 0))],
                 [pl.BlockSpec((tr, c), lambda i: (i, 0))], [jax.ShapeDtypeStruct((r, c), F32)], [x],
                 sem=("parallel",), nbytes=_nbytes((n, tr, c), x.dtype) + _nbytes((tr, c), F32))[0]


_DIMS = {"nn": (((1,), (0,)), ((), ())), "nt": (((1,), (1,)), ((), ())), "tn": (((0,), (0,)), ((), ()))}


def _matmul(a, b, mode, out_dtype, name, comm=(), deps=()):
    if mode == "nn":
        (m, k), (k2, n) = a.shape, b.shape
    elif mode == "nt":
        (m, k), (n, k2) = a.shape, b.shape
    else:
        (k, m), (k2, n) = a.shape, b.shape
    assert k == k2, (a.shape, b.shape, mode)
    tm, tn, tk = _tile(m, MM_TM), _tile(n, MM_TN), _tile(k, MM_TK)
    nk = k // tk
    dims = _DIMS[mode]
    a_spec = {"nn": pl.BlockSpec((tm, tk), lambda i, j, kk: (i, kk)),
              "nt": pl.BlockSpec((tm, tk), lambda i, j, kk: (i, kk)),
              "tn": pl.BlockSpec((tk, tm), lambda i, j, kk: (kk, i))}[mode]
    b_spec = {"nn": pl.BlockSpec((tk, tn), lambda i, j, kk: (kk, j)),
              "nt": pl.BlockSpec((tn, tk), lambda i, j, kk: (j, kk)),
              "tn": pl.BlockSpec((tk, tn), lambda i, j, kk: (kk, j))}[mode]

    def partial_product(a_ref, b_ref):
        return lax.dot_general(a_ref[...].astype(BF16), b_ref[...].astype(BF16), dims, preferred_element_type=F32)

    if nk == 1:
        def body(a_ref, b_ref, o_ref):
            o_ref[...] = partial_product(a_ref, b_ref).astype(o_ref.dtype)
        scratch = []
    else:
        def body(a_ref, b_ref, o_ref, acc_ref):
            kk = pl.program_id(2)

            @pl.when(kk == 0)
            def _():
                acc_ref[...] = jnp.zeros_like(acc_ref)

            acc_ref[...] += partial_product(a_ref, b_ref)

            @pl.when(kk == nk - 1)
            def _():
                o_ref[...] = acc_ref[...].astype(o_ref.dtype)
        scratch = [pltpu.VMEM((tm, tn), F32)]

    nbytes = (_nbytes((tm, tk), a.dtype) + _nbytes((tk, tn), b.dtype) + _nbytes((tm, tn), out_dtype)
              + _nbytes((tm, tn), F32))
    res = _call(body, name, (m // tm, n // tn, nk), [a_spec, b_spec],
                [pl.BlockSpec((tm, tn), lambda i, j, kk: (i, j))], [jax.ShapeDtypeStruct((m, n), out_dtype)], [a, b],
                scratch=scratch, sem=("parallel", "parallel", "arbitrary"), nbytes=nbytes, comm=comm, deps=deps)
    return res if comm else res[0]


def _matmul_post_norm(a, b, x, g_post, name, g_next=None, target=None):
    (m, k), (k2, d) = a.shape, b.shape
    assert k == k2 and d == D_MODEL
    tm, tk = _tile(m, ROW_BLOCK), _tile(k, MM_TK)
    nk = k // tk
    with_loss = target is not None

    def body(a_ref, b_ref, x_ref, gp_ref, last_ref, o_ref, first_ref, second_ref, acc_ref):
        i, kk = pl.program_id(0), pl.program_id(1)

        @pl.when(kk == 0)
        def _():
            acc_ref[...] = jnp.zeros_like(acc_ref)

        acc_ref[...] += jnp.dot(a_ref[...], b_ref[...], preferred_element_type=F32)

        @pl.when(kk == nk - 1)
        def _():
            ov = acc_ref[...]
            o_ref[...] = ov
            xn = x_ref[...] + (ov * _rms(ov)) * gp_ref[...]
            if with_loss:
                err = xn - last_ref[...]
                first_ref[...] = err / d

                @pl.when(i == 0)
                def _():
                    second_ref[...] = jnp.zeros_like(second_ref)

                second_ref[...] += 0.5 * jnp.sum(jnp.mean(err * err, axis=-1, keepdims=True), axis=0, keepdims=True)
            else:
                first_ref[...] = xn
                second_ref[...] = ((xn * _rms(xn)) * last_ref[...]).astype(BF16)

    rows = pl.BlockSpec((tm, d), lambda i, kk: (i, 0))
    vec = pl.BlockSpec((1, d), lambda i, kk: (0, 0))
    second = ((pl.BlockSpec((1, 1), lambda i, kk: (0, 0)), jax.ShapeDtypeStruct((1, 1), F32)) if with_loss
              else (rows, jax.ShapeDtypeStruct((m, d), BF16)))
    nbytes = _nbytes((tm, tk), BF16) + _nbytes((tk, d), BF16) + 6 * _nbytes((tm, d), F32)
    return _call(body, name, (m // tm, nk),
                 [pl.BlockSpec((tm, tk), lambda i, kk: (i, kk)), pl.BlockSpec((tk, d), lambda i, kk: (kk, 0)), rows, vec,
                  rows if with_loss else vec],
                 [rows, rows, second[0]],
                 [jax.ShapeDtypeStruct((m, d), F32), jax.ShapeDtypeStruct((m, d), F32), second[1]],
                 [a, b, x, g_post, target if with_loss else g_next],
                 scratch=[pltpu.VMEM((tm, d), F32)], sem=("arbitrary", "arbitrary"), nbytes=nbytes)


def _rms(x):
    return lax.rsqrt(jnp.mean(x * x, axis=-1, keepdims=True) + EPS)


def _rms_bwd(dz, a, g):
    r = _rms(a)
    dzg = dz * g
    da = r * dzg - a * (r * r * r) * jnp.mean(dzg * a, axis=-1, keepdims=True)
    return da, dz * (a * r)


def _row_spec(tr, width):
    return pl.BlockSpec((tr, width), lambda i: (i, 0))


def _vec_spec(width):
    return pl.BlockSpec((1, width), lambda i: (0, 0))


def _rms_fwd(x, g, name):
    t, d = x.shape
    tr = min(ROW_BLOCK, t)

    def body(x_ref, g_ref, h_ref):
        xv = x_ref[...]
        h_ref[...] = ((xv * _rms(xv)) * g_ref[...]).astype(BF16)

    return _call(body, name, (t // tr,), [_row_spec(tr, d), _vec_spec(d)], [_row_spec(tr, d)],
                 [jax.ShapeDtypeStruct((t, d), BF16)], [x, g], sem=("parallel",), nbytes=3 * _nbytes((tr, d), F32))[0]


def _norm_bwd(dskip, name, pre=None, post=None):
    t, d = dskip.shape
    tr = min(ROW_BLOCK, t)
    n_in = 1 + (3 if pre else 0) + (2 if post else 0)

    def body(*refs):
        ins, outs = list(refs[:n_in]), list(refs[n_in:])
        first = pl.program_id(0) == 0
        dx = ins.pop(0)[...]
        if pre:
            dh_ref, xin_ref, gpre_ref = ins.pop(0), ins.pop(0), ins.pop(0)
            dxin, dg_rows = _rms_bwd(dh_ref[...].astype(F32), xin_ref[...], gpre_ref[...])
            dx = dx + dxin
            dx_ref, dgpre_ref = outs.pop(0), outs.pop(0)
            dx_ref[...] = dx

            @pl.when(first)
            def _():
                dgpre_ref[...] = jnp.zeros_like(dgpre_ref)

            dgpre_ref[...] += jnp.sum(dg_rows, axis=0, keepdims=True)
        if post:
            a_ref, gpost_ref = ins.pop(0), ins.pop(0)
            da, dg_rows = _rms_bwd(dx, a_ref[...], gpost_ref[...])
            da_ref, dgpost_ref = outs.pop(0), outs.pop(0)
            da_ref[...] = da.astype(BF16)

            @pl.when(first)
            def _():
                dgpost_ref[...] = jnp.zeros_like(dgpost_ref)

            dgpost_ref[...] += jnp.sum(dg_rows, axis=0, keepdims=True)

    args, in_specs, out_shape, out_specs = [dskip], [_row_spec(tr, d)], [], []
    if pre:
        args += list(pre)
        in_specs += [_row_spec(tr, d), _row_spec(tr, d), _vec_spec(d)]
        out_shape += [jax.ShapeDtypeStruct((t, d), F32), jax.ShapeDtypeStruct((1, d), F32)]
        out_specs += [_row_spec(tr, d), _vec_spec(d)]
    if post:
        args += list(post)
        in_specs += [_row_spec(tr, d), _vec_spec(d)]
        out_shape += [jax.ShapeDtypeStruct((t, d), BF16), jax.ShapeDtypeStruct((1, d), F32)]
        out_specs += [_row_spec(tr, d), _vec_spec(d)]
    return _call(body, name, (t // tr,), in_specs, out_specs, out_shape, args, sem=("arbitrary",),
                 nbytes=7 * _nbytes((tr, d), F32))


def _sigmoid(x):
    return 1.0 / (1.0 + jnp.exp(-x))


def _ffn_in(h, w_gu_t, name):
    t, d = h.shape
    tm, tn = _tile(t, 1024), _tile(D_FF, MM_TN)
    n_tiles = D_FF // tn

    def body(h_ref, wg_ref, wu_ref, g_ref, u_ref, a_ref):
        hv = h_ref[...]
        gate = lax.dot_general(hv, wg_ref[...], _DIMS["nt"], preferred_element_type=F32).astype(BF16)
        up = lax.dot_general(hv, wu_ref[...], _DIMS["nt"], preferred_element_type=F32).astype(BF16)
        g_ref[...] = gate
        u_ref[...] = up
        gate, up = gate.astype(F32), up.astype(F32)
        a_ref[...] = ((gate * _sigmoid(gate)) * up).astype(BF16)

    tile = pl.BlockSpec((tm, tn), lambda i, j: (i, j))
    shape = jax.ShapeDtypeStruct((t, D_FF), BF16)
    return _call(body, name, (t // tm, n_tiles),
                 [pl.BlockSpec((tm, d), lambda i, j: (i, 0)), pl.BlockSpec((tn, d), lambda i, j: (j, 0)),
                  pl.BlockSpec((tn, d), lambda i, j: (j + n_tiles, 0))],
                 [tile, tile, tile], [shape, shape, shape], [h, w_gu_t, w_gu_t], sem=("parallel", "parallel"),
                 nbytes=_nbytes((tm, d), BF16) + 2 * _nbytes((tn, d), BF16) + 5 * _nbytes((tm, tn), F32))


def _swiglu_bwd(gate, up, dact, name):
    t = gate.shape[0]
    tr = min(ROW_BLOCK, t)

    def body(g_ref, u_ref, da_ref, o_ref):
        gate, up = g_ref[...].astype(F32), u_ref[...].astype(F32)
        da = da_ref[...].astype(F32)
        s = _sigmoid(gate)
        o_ref[:, :D_FF] = (da * up * (s * (1.0 + gate * (1.0 - s)))).astype(BF16)
        o_ref[:, D_FF:] = (da * (gate * s)).astype(BF16)

    half = _row_spec(tr, D_FF)
    return _call(body, name, (t // tr,), [half, half, half], [_row_spec(tr, 2 * D_FF)],
                 [jax.ShapeDtypeStruct((t, 2 * D_FF), BF16)], [gate, up, dact], sem=("parallel",),
                 nbytes=3 * _nbytes((tr, 2 * D_FF), F32))[0]


def _split3(x):
    hi = x.astype(BF16)
    r1 = x - hi.astype(F32)
    mid = r1.astype(BF16)
    lo = (r1 - mid.astype(F32)).astype(BF16)
    return hi, mid, lo


def _dot_exact(mat01, x):
    hi, mid, lo = _split3(x)
    out = jnp.dot(mat01, hi, preferred_element_type=F32)
    out = out + jnp.dot(mat01, mid, preferred_element_type=F32)
    return out + jnp.dot(mat01, lo, preferred_element_type=F32)


def _dot_exact_rhs(x, mat01):
    hi, mid, lo = _split3(x)
    out = jnp.dot(hi, mat01, preferred_element_type=F32)
    out = out + jnp.dot(mid, mat01, preferred_element_type=F32)
    return out + jnp.dot(lo, mat01, preferred_element_type=F32)


def _tri(lower):
    r = lax.broadcasted_iota(jnp.int32, (CHUNK, CHUNK), 0)
    c = lax.broadcasted_iota(jnp.int32, (CHUNK, CHUNK), 1)
    return jnp.where(r >= c if lower else r <= c, 1.0, 0.0).astype(BF16)


def _log_sigmoid(z):
    return jnp.minimum(z, 0.0) - jnp.log(1.0 + jnp.exp(-jnp.abs(z)))


def _forget_cumsum(proj, b_pad, name):
    t = proj.shape[0]
    nb = t // CHUNK

    def body(f_ref, b_ref, c_ref):
        tri = _tri(True)
        b = b_ref[...]

        def blk(i, carry):
            rows = pl.ds(pl.multiple_of(i * CHUNK, CHUNK), CHUNK)
            cs = _dot_exact(tri, _log_sigmoid(f_ref[rows, :] + b)) + carry
            c_ref[rows, :] = cs
            return cs[CHUNK - 1:CHUNK, :]

        lax.fori_loop(0, nb, blk, jnp.zeros((1, HEAD_DIM), F32))

    return _call(body, name, (1,),
                 [pl.BlockSpec((t, HEAD_DIM), lambda i: (0, CB_F)), pl.BlockSpec((1, HEAD_DIM), lambda i: (0, 0))],
                 [pl.BlockSpec((t, HEAD_DIM), lambda i: (0, 0))], [jax.ShapeDtypeStruct((t, HEAD_DIM), F32)],
                 [proj, b_pad], sem=("arbitrary",), nbytes=2 * _nbytes((t, HEAD_DIM), F32))[0]


def _causal(block):
    r = lax.broadcasted_iota(jnp.int32, (block, block), 0)
    c = lax.broadcasted_iota(jnp.int32, (block, block), 1)
    return c <= r


def _lanes(x, width):
    return jnp.concatenate([x] * (width // HEAD_DIM), axis=1)


def _chunks(t, blk):
    return [slice(r * blk, (r + 1) * blk) for r in range(t // blk)]


def _lane_of_head(x, h):
    lane = lax.broadcasted_iota(jnp.int32, x.shape, 1)
    return jnp.broadcast_to(jnp.sum(jnp.where(lane == h, x, 0.0), axis=1, keepdims=True), x.shape)


def _causal_t(block):
    r = lax.broadcasted_iota(jnp.int32, (block, block), 0)
    c = lax.broadcasted_iota(jnp.int32, (block, block), 1)
    return r <= c


def _attn_fwd_t(proj, c_pad, c_row, name, comm=()):
    t = proj.shape[0]
    blk = min(ATTN_BLOCK, t)
    nq = t // blk
    scale = HEAD_DIM ** -0.5

    def body(q_ref, k_ref, v_ref, cp_ref, cr_ref, y_ref, e_ref, kb_s, vt_s, cb_s):
        h, i = pl.program_id(0), pl.program_id(1)

        @pl.when(i == 0)
        def _():
            for rows in _chunks(t, blk):
                kb_s[rows, :] = k_ref[rows, :].astype(BF16)
                vt_s[:, rows] = v_ref[rows, :].T.astype(BF16)
                cb_s[rows, :] = _lane_of_head(cp_ref[rows, :], h)

        q = (q_ref[...] * scale).astype(BF16)
        ci = cr_ref[:, pl.ds(pl.multiple_of(i * blk, blk), blk)]

        def step(j, carry, diagonal):
            m, l, acc = carry
            rows = pl.ds(pl.multiple_of(j * blk, blk), blk)
            s = lax.dot_general(kb_s[rows, :], q, _DIMS["nt"], preferred_element_type=F32)
            s = s + ci - _lanes(cb_s[rows, :], blk)
            if diagonal:
                s = jnp.where(_causal_t(blk), s, NEG)
            m_new = jnp.maximum(m, jnp.max(s, axis=0, keepdims=True))
            alpha = jnp.exp(m - m_new)
            p = jnp.exp(s - m_new)
            l = alpha * l + jnp.sum(p, axis=0, keepdims=True)
            acc = alpha * acc + jnp.dot(vt_s[:, rows], p.astype(BF16), preferred_element_type=F32)
            return m_new, l, acc

        init = (jnp.full((1, blk), NEG, F32), jnp.zeros((1, blk), F32), jnp.zeros((HEAD_DIM, blk), F32))
        carry = lax.fori_loop(0, i, lambda j, c: step(j, c, False), init)
        m, l, acc = step(i, carry, True)
        y_ref[...] = (acc / l).T
        e_ref[...] = ci - (m + jnp.log(l))

    nbytes = 3 * _nbytes((t, HEAD_DIM), F32) + 4 * _nbytes((blk, HEAD_DIM), F32) + 4 * _nbytes((blk, blk), F32)
    return _call(body, name, (N_HEADS, nq),
                 [pl.BlockSpec((blk, HEAD_DIM), lambda h, i: (i, CB_QKV + 3 * h)),
                  pl.BlockSpec((t, HEAD_DIM), lambda h, i: (0, CB_QKV + 3 * h + 1)),
                  pl.BlockSpec((t, HEAD_DIM), lambda h, i: (0, CB_QKV + 3 * h + 2)),
                  pl.BlockSpec((t, HEAD_DIM), lambda h, i: (0, 0)),
                  pl.BlockSpec((None, 1, t), lambda h, i: (h, 0, 0))],
                 [pl.BlockSpec((blk, HEAD_DIM), lambda h, i: (i, h)),
                  pl.BlockSpec((None, 1, blk), lambda h, i: (h, 0, i))],
                 [jax.ShapeDtypeStruct((t, D_MODEL), F32), jax.ShapeDtypeStruct((N_HEADS, 1, t), F32)],
                 [proj, proj, proj, c_pad, c_row],
                 scratch=[pltpu.VMEM((t, HEAD_DIM), BF16), pltpu.VMEM((HEAD_DIM, t), BF16),
                          pltpu.VMEM((t, HEAD_DIM), F32)],
                 sem=("parallel", "arbitrary"), nbytes=nbytes, comm=comm)


def _attn_bwd_t(proj, dy, e_row, delta_b, c_pad, dproj, name, deps=()):
    t = proj.shape[0]
    blk = min(ATTN_BLOCK, t)
    nb = t // blk
    scale = HEAD_DIM ** -0.5

    def body(q_ref, k_ref, v_ref, dy_ref, e_ref, dl_ref, cp_ref, dproj_in, dqkv_ref, dck_ref, dcq_ref,
             qb_s, qt_s, dyb_s, dyt_s, dl_s, dqt_acc):
        del dproj_in
        h, j = pl.program_id(0), pl.program_id(1)

        @pl.when(j == 0)
        def _():
            for rows in _chunks(t, blk):
                qs = q_ref[rows, :] * scale
                qb_s[rows, :] = qs.astype(BF16)
                qt_s[:, rows] = qs.T.astype(BF16)
                dyr = dy_ref[rows, :]
                dyb_s[rows, :] = dyr.astype(BF16)
                dyt_s[:, rows] = dyr.T.astype(BF16)
                dl_s[:, rows] = dl_ref[rows, :].T[0:1, :]
            dqt_acc[...] = jnp.zeros_like(dqt_acc)
            dcq_ref[...] = jnp.zeros_like(dcq_ref)

        kf = k_ref[...]
        kb, kt = kf.astype(BF16), kf.T.astype(BF16)
        vb = v_ref[...].astype(BF16)
        cj = _lanes(_lane_of_head(cp_ref[...], h), blk)

        def step(i, carry, diagonal):
            dkt, dvt, dkey = carry
            cols = pl.ds(pl.multiple_of(i * blk, blk), blk)
            s = lax.dot_general(kb, qb_s[cols, :], _DIMS["nt"], preferred_element_type=F32) + e_ref[:, cols] - cj
            if diagonal:
                s = jnp.where(_causal_t(blk), s, NEG)
            p = jnp.exp(s)
            dp = lax.dot_general(vb, dyb_s[cols, :], _DIMS["nt"], preferred_element_type=F32)
            ds = p * (dp - dl_s[:, cols])
            pb, dsb = p.astype(BF16), ds.astype(BF16)
            dvt = dvt + lax.dot_general(dyt_s[:, cols], pb, _DIMS["nt"], preferred_element_type=F32)
            dkt = dkt + lax.dot_general(qt_s[:, cols], dsb, _DIMS["nt"], preferred_element_type=F32)
            dqt_acc[:, cols] += jnp.dot(kt, dsb, preferred_element_type=F32) * scale
            dcq_ref[:, cols] += jnp.sum(ds, axis=0, keepdims=True)
            for k in range(blk // HEAD_DIM):
                dkey = dkey + ds[:, k * HEAD_DIM:(k + 1) * HEAD_DIM]
            return dkt, dvt, dkey

        zeros = jnp.zeros((HEAD_DIM, blk), F32)
        carry = step(j, (zeros, zeros, jnp.zeros((blk, HEAD_DIM), F32)), True)
        dkt, dvt, dkey = lax.fori_loop(j + 1, nb, lambda i, c: step(i, c, False), carry)
        mine = pl.ds(pl.multiple_of(j * blk, blk), blk)
        dqkv_ref[:, :HEAD_DIM] = dqt_acc[:, mine].T.astype(BF16)
        dqkv_ref[:, HEAD_DIM:2 * HEAD_DIM] = dkt.T.astype(BF16)
        dqkv_ref[:, 2 * HEAD_DIM:] = dvt.T.astype(BF16)
        dck_ref[...] = -jnp.sum(dkey.T, axis=0, keepdims=True)

    full = lambda cb: pl.BlockSpec((t, HEAD_DIM), lambda h, j: (0, cb(h)))
    head = lambda h: h
    row = pl.BlockSpec((None, 1, t), lambda h, j: (h, 0, 0))
    nbytes = (8 * _nbytes((t, HEAD_DIM), F32) + 4 * _nbytes((blk, HEAD_DIM), F32) + 6 * _nbytes((blk, blk), F32))
    return _call(body, name, (N_HEADS, nb),
                 [full(lambda h: CB_QKV + 3 * h),
                  pl.BlockSpec((blk, HEAD_DIM), lambda h, j: (j, CB_QKV + 3 * h + 1)),
                  pl.BlockSpec((blk, HEAD_DIM), lambda h, j: (j, CB_QKV + 3 * h + 2)),
                  full(head), row, full(head),
                  pl.BlockSpec((blk, HEAD_DIM), lambda h, j: (j, 0)),
                  pl.BlockSpec(memory_space=pl.ANY)],
                 [pl.BlockSpec((blk, QKV_WIDTH), lambda h, j: (j, (GATED + F_WIDTH) // QKV_WIDTH + h)),
                  pl.BlockSpec((None, 1, blk), lambda h, j: (h, 0, j)),
                  row],
                 [jax.ShapeDtypeStruct(dproj.shape, dproj.dtype), jax.ShapeDtypeStruct((N_HEADS, 1, t), F32),
                  jax.ShapeDtypeStruct((N_HEADS, 1, t), F32)],
                 [proj, proj, proj, dy, e_row, delta_b, c_pad, dproj],
                 scratch=[pltpu.VMEM((t, HEAD_DIM), BF16), pltpu.VMEM((HEAD_DIM, t), BF16),
                          pltpu.VMEM((t, HEAD_DIM), BF16), pltpu.VMEM((HEAD_DIM, t), BF16),
                          pltpu.VMEM((1, t), F32), pltpu.VMEM((HEAD_DIM, t), F32)],
                 sem=("parallel", "arbitrary"), nbytes=nbytes, aliases={7: 0}, deps=deps)


def _forget_bwd_t(proj, b_pad, dc, dproj, name):
    t = proj.shape[0]
    tr = min(ROW_BLOCK, t)
    nb = t // tr
    rev = lambda i: nb - 1 - i

    def body(f_ref, b_ref, dc_ref, dproj_in, df_ref, db_ref, run_ref):
        del dproj_in

        @pl.when(pl.program_id(0) == 0)
        def _():
            run_ref[...] = jnp.zeros_like(run_ref)
            db_ref[...] = jnp.zeros_like(db_ref)

        tri = _tri(False)
        b = b_ref[...]
        lane = lax.broadcasted_iota(jnp.int32, (CHUNK, HEAD_DIM), 1)
        df_ref[...] = jnp.zeros_like(df_ref)
        for n in reversed(range(tr // CHUNK)):
            rows = slice(n * CHUNK, (n + 1) * CHUNK)
            dlogf = _dot_exact(tri, dc_ref[rows, :]) + run_ref[...]
            run_ref[...] = dlogf[0:1, :]
            z = f_ref[rows, :] + b
            e = jnp.exp(-jnp.abs(z))
            sig_neg = jnp.where(z >= 0.0, e, 1.0) / (1.0 + e)
            df = jnp.where(lane < N_HEADS, dlogf * sig_neg, 0.0)
            df_ref[rows, :HEAD_DIM] = df.astype(BF16)
            db_ref[...] += jnp.sum(df, axis=0, keepdims=True)

    return _call(body, name, (nb,),
                 [pl.BlockSpec((tr, HEAD_DIM), lambda i: (rev(i), CB_F)), pl.BlockSpec((1, HEAD_DIM), lambda i: (0, 0)),
                  pl.BlockSpec((tr, HEAD_DIM), lambda i: (rev(i), 0)), pl.BlockSpec(memory_space=pl.ANY)],
                 [pl.BlockSpec((tr, F_WIDTH), lambda i: (rev(i), GATED // F_WIDTH)),
                  pl.BlockSpec((1, HEAD_DIM), lambda i: (0, 0))],
                 [jax.ShapeDtypeStruct(dproj.shape, dproj.dtype), jax.ShapeDtypeStruct((1, HEAD_DIM), F32)],
                 [proj, b_pad, dc, dproj], scratch=[pltpu.VMEM((1, HEAD_DIM), F32)],
                 sem=("arbitrary",), nbytes=4 * _nbytes((tr, HEAD_DIM), F32), aliases={3: 0})


_GELU_K = math.sqrt(2.0 / math.pi)
_GELU_C = 0.044715


def _gelu(x):
    t = jnp.tanh(_GELU_K * (x + _GELU_C * (x * x * x)))
    return 0.5 * x * (1.0 + t), t


def _gelu_grad(x, t):
    return 0.5 * (1.0 + t) + 0.5 * x * (1.0 - t * t) * (_GELU_K * (1.0 + 3.0 * _GELU_C * (x * x)))


def _layernorm_stats(a):
    mu = jnp.mean(a, axis=-1, keepdims=True)
    xc = a - mu
    r = lax.rsqrt(jnp.mean(xc * xc, axis=-1, keepdims=True) + EPS)
    return xc * r, r


def _group(g):
    return slice(g * CHUNK, (g + 1) * CHUNK)


def _fixed(shape):
    return pl.BlockSpec(shape, lambda i: (0,) * len(shape))


def _mix_fwd(proj, y_b, w_s, b_cols, g_v, name):
    t = proj.shape[0]
    step_rows = min(MIX_ROWS, t)

    def body(u_ref, vs_ref, ga_ref, gb_ref, yb_ref, w_ref, b_ref, gv_ref, o_ref):
        mask = _causal(CHUNK)
        w = [jnp.where(mask, w_ref[g], 0.0).astype(BF16) for g in range(N_GROUPS)]
        for rows in _chunks(step_rows, CHUNK):
            a_u, _ = _gelu(u_ref[rows, :])
            a_v, _ = _gelu(vs_ref[rows, :])
            xhat, _ = _layernorm_stats(a_v)
            vn = (xhat * gv_ref[...]).astype(BF16)
            sa, sb = _sigmoid(ga_ref[rows, :]), _sigmoid(gb_ref[rows, :])
            yb = yb_ref[rows, :]
            for g in range(N_GROUPS):
                cols = _group(g)
                mixed = jnp.dot(w[g], vn[:, cols], preferred_element_type=F32) + b_ref[:, g:g + 1]
                o_ref[rows, cols] = (sa[:, cols] * (a_u[:, cols] * mixed) + sb[:, cols] * yb[:, cols]).astype(BF16)

    gated = lambda k: pl.BlockSpec((step_rows, D_MODEL), lambda i: (i, k))
    return _call(body, name, (t // step_rows,),
                 [gated(0), gated(1), gated(2), gated(3), _row_spec(step_rows, D_MODEL),
                  _fixed((N_GROUPS, CHUNK, CHUNK)), _fixed((CHUNK, N_GROUPS)), _fixed((1, D_MODEL))],
                 [_row_spec(step_rows, D_MODEL)], [jax.ShapeDtypeStruct((t, D_MODEL), BF16)],
                 [proj, proj, proj, proj, y_b, w_s, b_cols, g_v], sem=("parallel",),
                 nbytes=16 * _nbytes((step_rows, D_MODEL), F32))[0]


def _mix_bwd(proj, y_b, dmerged, w_s, b_cols, g_v, sel, name, deps=()):
    t = proj.shape[0]
    step_rows = min(MIX_BWD_ROWS, t)

    def body(u_ref, vs_ref, ga_ref, gb_ref, yb_ref, dm_ref, w_ref, b_ref, gv_ref, sel_ref,
             dg_ref, dyb_ref, delta_ref, dw_ref, dbt_ref, dgv_ref):
        @pl.when(pl.program_id(0) == 0)
        def _():
            dw_ref[...] = jnp.zeros_like(dw_ref)
            dbt_ref[...] = jnp.zeros_like(dbt_ref)
            dgv_ref[...] = jnp.zeros_like(dgv_ref)

        gv = gv_ref[...]
        mask = _causal(CHUNK)
        w = [jnp.where(mask, w_ref[g], 0.0).astype(BF16) for g in range(N_GROUPS)]
        for rows in _chunks(step_rows, CHUNK):
            u, vs = u_ref[rows, :], vs_ref[rows, :]
            a_u, t_u = _gelu(u)
            a_v, t_v = _gelu(vs)
            xhat, r = _layernorm_stats(a_v)
            vn = (xhat * gv).astype(BF16)
            sa, sb = _sigmoid(ga_ref[rows, :]), _sigmoid(gb_ref[rows, :])
            yb, dm = yb_ref[rows, :], dm_ref[rows, :].astype(F32)
            dyb = dm * sb
            dyb_ref[rows, :] = dyb
            dyb_yb = dyb * yb
            dg_ref[rows, 3 * D_MODEL:] = (dm * yb * (sb * (1.0 - sb))).astype(BF16)
            dya = dm * sa
            dmixed_parts, dvn_parts = [], []
            for g in range(N_GROUPS):
                cols = _group(g)
                delta_ref[rows, cols] = jnp.broadcast_to(jnp.sum(dyb_yb[:, cols], axis=1, keepdims=True),
                                                         (CHUNK, CHUNK))
                mixed = jnp.dot(w[g], vn[:, cols], preferred_element_type=F32) + b_ref[:, g:g + 1]
                y_a = a_u[:, cols] * mixed
                dg_ref[rows, 2 * D_MODEL + g * CHUNK:2 * D_MODEL + (g + 1) * CHUNK] = (
                    dm[:, cols] * y_a * (sa[:, cols] * (1.0 - sa[:, cols]))).astype(BF16)
                dg_ref[rows, cols] = (dya[:, cols] * mixed * _gelu_grad(u[:, cols], t_u[:, cols])).astype(BF16)
                dmixed = dya[:, cols] * a_u[:, cols]
                dmb = dmixed.astype(BF16)
                dw = lax.dot_general(dmb, vn[:, cols], _DIMS["nt"], preferred_element_type=F32)
                dw_ref[g] += jnp.where(mask, dw, 0.0)
                dvn_parts.append(lax.dot_general(w[g], dmb, _DIMS["tn"], preferred_element_type=F32))
                dmixed_parts.append(dmixed)
            dmixed_all = jnp.concatenate(dmixed_parts, axis=1)
            dvn = jnp.concatenate(dvn_parts, axis=1)
            dbt_ref[...] += _dot_exact_rhs(dmixed_all, sel_ref[...])
            dgv_ref[...] += jnp.sum(dvn * xhat, axis=0, keepdims=True)
            dxh = dvn * gv
            da_v = r * (dxh - jnp.mean(dxh, axis=-1, keepdims=True)
                        - xhat * jnp.mean(dxh * xhat, axis=-1, keepdims=True))
            dg_ref[rows, D_MODEL:2 * D_MODEL] = (da_v * _gelu_grad(vs, t_v)).astype(BF16)

    row = lambda width: _row_spec(step_rows, width)
    gated = lambda k: pl.BlockSpec((step_rows, D_MODEL), lambda i: (i, k))
    return _call(body, name, (t // step_rows,),
                 [gated(0), gated(1), gated(2), gated(3), row(D_MODEL), row(D_MODEL),
                  _fixed((N_GROUPS, CHUNK, CHUNK)), _fixed((CHUNK, N_GROUPS)), _fixed((1, D_MODEL)),
                  _fixed((D_MODEL, HEAD_DIM))],
                 [row(GATED), row(D_MODEL), row(D_MODEL),
                  _fixed((N_GROUPS, CHUNK, CHUNK)), _fixed((CHUNK, HEAD_DIM)), _fixed((1, D_MODEL))],
                 [jax.ShapeDtypeStruct((t, IN_PAD), BF16), jax.ShapeDtypeStruct((t, D_MODEL), F32),
                  jax.ShapeDtypeStruct((t, D_MODEL), F32), jax.ShapeDtypeStruct((N_GROUPS, CHUNK, CHUNK), F32),
                  jax.ShapeDtypeStruct((CHUNK, HEAD_DIM), F32), jax.ShapeDtypeStruct((1, D_MODEL), F32)],
                 [proj, proj, proj, proj, y_b, dmerged, w_s, b_cols, g_v, sel], sem=("arbitrary",),
                 nbytes=24 * _nbytes((step_rows, D_MODEL), F32), deps=deps)


def _adamw_update(w_ref, g_ref, m_ref, v_ref, d_ref, nm_ref, nv_ref):
    gv = g_ref[...]
    nm = ADAM_B1 * m_ref[...] + (1.0 - ADAM_B1) * gv
    nv = ADAM_B2 * v_ref[...] + (1.0 - ADAM_B2) * (gv * gv)
    m_hat = nm / (1.0 - ADAM_B1 ** ADAM_STEP)
    v_hat = nv / (1.0 - ADAM_B2 ** ADAM_STEP)
    d_ref[...] = -ADAM_LR * (m_hat / (jnp.sqrt(v_hat) + ADAM_EPS) + ADAM_WD * w_ref[...])
    nm_ref[...] = nm
    nv_ref[...] = nv


def _adamw(w, g, m, v, name):
    r, c = w.shape
    tr = _tile(r, 256, unit=8)

    def body(w_ref, g_ref, m_ref, v_ref, d_ref, nm_ref, nv_ref):
        _adamw_update(w_ref, g_ref, m_ref, v_ref, d_ref, nm_ref, nv_ref)

    spec = pl.BlockSpec((tr, c), lambda i: (i, 0))
    shape = jax.ShapeDtypeStruct((r, c), F32)
    return _call(body, name, (r // tr,), [spec] * 4, [spec] * 3, [shape] * 3, [w, g, m, v], sem=("parallel",),
                 nbytes=7 * _nbytes((tr, max(c, 128)), F32))


def _adamw_cols(w, g, m, v, name):
    layers, r, c = w.shape
    tc = 256

    def body(w_ref, g_ref, m_ref, v_ref, d_ref, nm_ref, nv_ref):
        _adamw_update(w_ref, g_ref, m_ref, v_ref, d_ref, nm_ref, nv_ref)

    spec = pl.BlockSpec((None, r, tc), lambda l, j: (l, 0, j))
    shape = jax.ShapeDtypeStruct(w.shape, F32)
    return _call(body, name, (layers, c // tc), [spec] * 4, [spec] * 3, [shape] * 3, [w, g, m, v],
                 sem=("parallel", "parallel"), nbytes=7 * _nbytes((r, tc), F32))


def _pad_rows(a, rows):
    return jnp.pad(a, ((0, rows - a.shape[0]), (0, 0)))


def _unpack_in(gathered):
    i = jnp.arange(16)[None, :, None]
    b = jnp.arange(N_DEV)[:, None, None]
    head = jnp.where(i < b, jnp.roll(gathered[:, IN_STRIDE:IN_STRIDE + 16], 1, axis=0), gathered[:, :16])
    nat = jnp.concatenate([head, gathered[:, 16:IN_STRIDE]], axis=1).reshape(N_DEV * IN_STRIDE, D_MODEL)
    qkv = nat[2 * D_MODEL:5 * D_MODEL].reshape(3, N_HEADS, HEAD_DIM, D_MODEL)
    qkv = jnp.transpose(qkv, (1, 0, 2, 3)).reshape(3 * D_MODEL, D_MODEL)
    f = _pad_rows(gathered[N_DEV - 1, IN_STRIDE:IN_STRIDE + N_HEADS], F_WIDTH)
    return jnp.concatenate([nat[:2 * D_MODEL], nat[5 * D_MODEL:7 * D_MODEL], f, qkv], axis=0)


def _pack_grad_in(dw_in_t):
    qkv = dw_in_t[GATED + F_WIDTH:].reshape(N_HEADS, 3, HEAD_DIM, D_MODEL)
    qkv = jnp.transpose(qkv, (1, 0, 2, 3)).reshape(3 * D_MODEL, D_MODEL)
    return jnp.concatenate([dw_in_t[:2 * D_MODEL], qkv, dw_in_t[2 * D_MODEL:GATED],
                            dw_in_t[GATED:GATED + IN_ROWS - N_DEV * IN_STRIDE]], axis=0)


def kernel(x, mix_pre_g, w_in, b_forget, sgu_norm_g, w_spatial, b_spatial, w_out, mix_post_g, ffn_pre_g, w_gate, w_up, w_down, ffn_post_g, loss_target, m_mix_pre_g, m_w_in, m_b_forget, m_sgu_norm_g, m_w_spatial, m_b_spatial, m_w_out, m_mix_post_g, m_ffn_pre_g, m_w_gate, m_w_up, m_w_down, m_ffn_post_g, v_mix_pre_g, v_w_in, v_b_forget, v_sgu_norm_g, v_w_spatial, v_b_spatial, v_w_out, v_mix_post_g, v_ffn_pre_g, v_w_gate, v_w_up, v_w_down, v_ffn_post_g):
    depth = w_in.shape[0]
    assert depth == 2
    t = x.shape[1]
    x0 = x.reshape(t, D_MODEL)
    target = loss_target.reshape(t, D_MODEL)
    sel = (jnp.arange(D_MODEL)[:, None] // HEAD_DIM == jnp.arange(HEAD_DIM)[None, :]).astype(BF16)
    vec = lambda a, l: a[l][None, :]
    zero_row = jnp.zeros((1, D_MODEL), F32)

    me = 4 * lax.axis_index("x") + 2 * lax.axis_index("y") + lax.axis_index("c")
    p_in = [lax.dynamic_update_slice(jnp.zeros((P_IN, D_MODEL), BF16), w_in[l].T.astype(BF16), (me, 0))
            for l in range(depth)]
    p_rest = [[w_out[l].astype(BF16), jnp.stack([w_gate[l].T, w_up[l].T]).astype(BF16), w_down[l].astype(BF16)]
              for l in range(depth)]
    w_in_t = [None] * depth
    w_rest = [None] * depth
    w_in_t[0] = _unpack_in(_exchange(("gather", 0), p_in[0], "gather_w_in_0")[0])

    saved = []
    xl = x0
    h = _rms_fwd(xl, vec(mix_pre_g, 0), "rms_in_0")
    dy = loss = None
    for l in range(depth):
        b_pad = jnp.pad(b_forget[l], (0, HEAD_DIM - N_HEADS))[None, :]
        b_cols = b_spatial[l].T
        proj = _matmul(h, w_in_t[l], "nt", F32, f"proj_{l}")
        c_pad = _forget_cumsum(proj, b_pad, f"forget_cumsum_{l}")
        c_row = c_pad[:, :N_HEADS].T[:, None, :]
        riders = [(("gather", 0), p_rest[l][0]), (("gather", 1), p_rest[l][1]), (("gather", 0), p_rest[l][2])]
        riders += [(("gather", 0), p_in[l + 1])] if l + 1 < depth else []
        y_b, e_row, *arrived = _attn_fwd_t(proj, c_pad, c_row, f"attn_fwd_{l}", comm=riders)
        w_rest[l] = (arrived[0].reshape(D_MODEL, D_MODEL), arrived[1].reshape(2 * D_FF, D_MODEL),
                     arrived[2].reshape(D_FF, D_MODEL))
        if l + 1 < depth:
            w_in_t[l + 1] = _unpack_in(arrived[3])
        w_o, w_gu_t, w_d = w_rest[l]
        merged = _mix_fwd(proj, y_b, w_spatial[l], b_cols, vec(sgu_norm_g, l), f"mix_fwd_{l}")
        o, x1, h2 = _matmul_post_norm(merged, w_o, xl, vec(mix_post_g, l), f"out_proj_{l}", g_next=vec(ffn_pre_g, l))
        gate, up, act = _ffn_in(h2, w_gu_t, f"ffn_gu_{l}")
        if l + 1 < depth:
            dn, x_next, h_next = _matmul_post_norm(act, w_d, x1, vec(ffn_post_g, l), f"ffn_down_{l}",
                                                   g_next=vec(mix_pre_g, l + 1))
        else:
            dn, dy, loss = _matmul_post_norm(act, w_d, x1, vec(ffn_post_g, l), f"ffn_down_{l}", target=target)
        saved.append(dict(x0=xl, h=h, proj=proj, b_pad=b_pad, b_cols=b_cols, c_pad=c_pad, y_b=y_b, e_row=e_row,
                          merged=merged, o=o, x1=x1, h2=h2, gate=gate, up=up, act=act, dn=dn))
        if l + 1 < depth:
            xl, h = x_next, h_next

    g_small = [None] * depth
    flights = {}
    late_token = None
    dx = dy
    dd, dg_ffn_post = _norm_bwd(dx, f"bwd_ffn_post_{depth - 1}", post=(saved[-1]["dn"], vec(ffn_post_g, depth - 1)))
    for l in reversed(range(depth)):
        s = saved[l]
        w_o, w_gu_t, w_d = w_rest[l]
        dact = _matmul(dd, w_d, "nt", BF16, f"d_act_{l}")
        dw_d = _matmul(s["act"], dd, "tn", BF16, f"dw_down_{l}")
        dgu = _swiglu_bwd(s["gate"], s["up"], dact, f"swiglu_bwd_{l}")
        dh2 = _matmul(dgu, w_gu_t, "nn", F32, f"d_h2_{l}")
        dw_gu_t = _matmul(dgu, s["h2"], "tn", BF16, f"dw_gu_{l}")
        dx1, dg_ffn_pre, do, dg_mix_post = _norm_bwd(
            dx, f"bwd_post_mix_{l}", pre=(dh2, s["x1"], vec(ffn_pre_g, l)), post=(s["o"], vec(mix_post_g, l)))
        dmerged = _matmul(do, w_o, "nt", F32, f"d_merged_{l}")
        dw_o = _matmul(s["merged"], do, "tn", BF16, f"dw_out_{l}")
        tokens = []
        for key, dw, base, rows in [("out", dw_o, 0, SH_OUT), ("gate", dw_gu_t, 0, SH_FF), ("up", dw_gu_t, D_FF, SH_FF),
                                    ("down", dw_d, 0, SH_FF)]:
            flights[(key, l)], token = _exchange_start(("windows", base, rows, rows), dw, f"start_grad_{key}_{l}")
            tokens.append(token)
        dproj, dyb, delta_b, dw_s, dbt, dg_v = _mix_bwd(
            s["proj"], s["y_b"], dmerged, w_spatial[l], s["b_cols"], vec(sgu_norm_g, l), sel, f"mix_bwd_{l}",
            deps=tokens)
        db_s = dbt[:, :N_GROUPS].T.reshape(1, D_MODEL)
        tokens = []
        if l == 0:
            g_small[0] = jnp.concatenate([zero_row, dg_v, dg_mix_post, dg_ffn_pre, dg_ffn_post, db_s, zero_row,
                                          zero_row, dw_s.reshape(CHUNK, D_MODEL)], axis=0)
            flights["small"], token = _exchange_start(("spread",),jnp.concatenate([g_small[1], g_small[0]], axis=0),
                                                      "start_small_grads")
            tokens = [token]
        dproj, dc_key, dc_query = _attn_bwd_t(s["proj"], dyb, s["e_row"], delta_b, s["c_pad"], dproj,
                                              f"attn_bwd_{l}", deps=tokens)
        dc = jnp.pad((dc_key + dc_query)[:, 0, :].T, ((0, 0), (0, HEAD_DIM - N_HEADS)))
        dproj, db_f = _forget_bwd_t(s["proj"], s["b_pad"], dc, dproj, f"forget_bwd_{l}")
        db_f_row = jnp.pad(db_f, ((0, 0), (0, D_MODEL - HEAD_DIM)))
        dw_in_t = _matmul(dproj, s["h"], "tn", BF16, f"dw_in_{l}")
        flights[("in", l)], token = _exchange_start(("windows", 0, IN_STRIDE, P_IN), _pack_grad_in(dw_in_t),
                                                    f"start_grad_in_{l}")
        dh = _matmul(dproj, w_in_t[l], "nn", F32, f"d_h_{l}", deps=[token])
        if l > 0:
            dx, dg_mix_pre, dd, dg_ffn_post_below = _norm_bwd(
                dx1, f"bwd_pre_mix_{l}", pre=(dh, s["x0"], vec(mix_pre_g, l)),
                post=(saved[l - 1]["dn"], vec(ffn_post_g, l - 1)))
            loss_row = jnp.pad(loss, ((0, 0), (0, D_MODEL - 1)))
            g_small[l] = jnp.concatenate([dg_mix_pre, dg_v, dg_mix_post, dg_ffn_pre, dg_ffn_post, db_s, db_f_row,
                                          loss_row, dw_s.reshape(CHUNK, D_MODEL)], axis=0)
            dg_ffn_post = dg_ffn_post_below
        else:
            dx, dg_mix_pre = _norm_bwd(dx1, f"bwd_pre_mix_{l}", pre=(dh, s["x0"], vec(mix_pre_g, l)))
            late_rows = jnp.concatenate([dg_mix_pre, db_f_row] + [zero_row] * 6, axis=0)
            flights["late"], late_token = _exchange_start(("spread",),late_rows, "start_late_small_grads")
    grad_x = dx.reshape(x.shape)

    landed, after = {}, [grad_x, late_token]
    rest = ["out", "gate", "up", "down"]
    for key in [(n, 1) for n in rest] + [("in", 1)] + [(n, 0) for n in rest] + ["small"]:
        label = key if isinstance(key, str) else f"grad_{key[0]}_{key[1]}"
        landed[key] = _exchange_wait(flights[key], after, f"wait_{label}")
        after = [landed[key]]
    small_sum = _sum_slots(landed["small"], S_ROWS, "sum_small_grads")
    sm = small_sum.reshape(depth, S_ROWS, D_MODEL)[::-1]
    loss = sm[depth - 1, 7, 0]
    g = {
        "sgu_norm_g": sm[:, 1], "mix_post_g": sm[:, 2],
        "ffn_pre_g": sm[:, 3], "ffn_post_g": sm[:, 4], "b_spatial": sm[:, 5].reshape(depth, N_GROUPS, CHUNK),
        "w_spatial": sm[:, 8:].reshape(depth, N_GROUPS, CHUNK, CHUNK),
    }
    for n in rest:
        g["w_" + n] = jnp.stack([_sum_slots(landed[(n, l)], landed[(n, l)].shape[1], f"sum_grad_{n}_{l}")
                                 for l in range(depth)])

    names = ["mix_pre_g", "w_in", "b_forget", "sgu_norm_g", "w_spatial", "b_spatial", "w_out", "mix_post_g",
             "ffn_pre_g", "w_gate", "w_up", "w_down", "ffn_post_g"]
    ws = dict(mix_pre_g=mix_pre_g, w_in=w_in, b_forget=b_forget, sgu_norm_g=sgu_norm_g, w_spatial=w_spatial,
              b_spatial=b_spatial, w_out=w_out, mix_post_g=mix_post_g, ffn_pre_g=ffn_pre_g, w_gate=w_gate, w_up=w_up,
              w_down=w_down, ffn_post_g=ffn_post_g)
    ms = dict(mix_pre_g=m_mix_pre_g, w_in=m_w_in, b_forget=m_b_forget, sgu_norm_g=m_sgu_norm_g, w_spatial=m_w_spatial,
              b_spatial=m_b_spatial, w_out=m_w_out, mix_post_g=m_mix_post_g, ffn_pre_g=m_ffn_pre_g, w_gate=m_w_gate,
              w_up=m_w_up, w_down=m_w_down, ffn_post_g=m_ffn_post_g)
    vs = dict(mix_pre_g=v_mix_pre_g, w_in=v_w_in, b_forget=v_b_forget, sgu_norm_g=v_sgu_norm_g, w_spatial=v_w_spatial,
              b_spatial=v_b_spatial, w_out=v_w_out, mix_post_g=v_mix_post_g, ffn_pre_g=v_ffn_pre_g, w_gate=v_w_gate,
              w_up=v_w_up, w_down=v_w_down, ffn_post_g=v_ffn_post_g)
    updates = {}
    last = ["w_in", "mix_pre_g", "b_forget"]
    transposed = ["w_in", "w_gate", "w_up"]
    for n in [n for n in names if n not in last] + last:
        if n == last[0]:
            sum_in_1 = _sum_slots(landed[("in", 1)], 304, "sum_grad_in_1")
            done = [u[0] for u in updates.values()] + [sum_in_1]
            sum_in_0 = _sum_slots(_exchange_wait(flights[("in", 0)], done, "wait_grad_in_0"), 304, "sum_grad_in_0")
            late = _sum_slots(_exchange_wait(flights["late"], [sum_in_0], "wait_late"), 8, "sum_late_small_grads")
            g["w_in"] = lax.dynamic_slice(jnp.stack([sum_in_0, sum_in_1]), (0, me, 0), (depth, SH_IN, D_MODEL))
            g["mix_pre_g"] = jnp.stack([late[0], sm[1, 0]])
            g["b_forget"] = jnp.stack([late[1, :N_HEADS], sm[1, 6, :N_HEADS]])
        if n in transposed:
            view = lambda a: jnp.swapaxes(a, 1, 2)
            updates[n] = [view(u) for u in _adamw_cols(view(ws[n]), g[n], view(ms[n]), view(vs[n]), f"adamw_{n}")]
            g[n] = view(g[n])
        else:
            flat = (-1, ws[n].shape[-1])
            updates[n] = [u.reshape(ws[n].shape) for u in
                          _adamw(ws[n].reshape(flat), g[n].reshape(flat), ms[n].reshape(flat), vs[n].reshape(flat),
                                 f"adamw_{n}")]
    deltas = [updates[n][0] for n in names]
    new_m = [updates[n][1] for n in names]
    new_v = [updates[n][2] for n in names]
    grads = [g[n].reshape(ws[n].shape) for n in names]
    return (loss, grad_x, *grads, *deltas, *new_m, *new_v)
```

```python
import math

import jax
import jax.numpy as jnp
from jax import lax
from jax.experimental import pallas as pl
from jax.experimental.pallas import tpu as pltpu

F32 = jnp.float32
BF16 = jnp.bfloat16

N_DEV = 8
D_MODEL = 1024
N_HEADS = 8
HEAD_DIM = 128
CHUNK = 128
N_GROUPS = 8
D_FF = 2816
IN_WIDTH = 7 * D_MODEL + N_HEADS
IN_PAD = 7680
EPS = 1e-6
GATED = 4 * D_MODEL
F_WIDTH = 512
QKV_WIDTH = 3 * HEAD_DIM
CB_F = GATED // HEAD_DIM
CB_QKV = (GATED + F_WIDTH) // HEAD_DIM
assert GATED + F_WIDTH + N_HEADS * QKV_WIDTH == IN_PAD and (GATED + F_WIDTH) % QKV_WIDTH == 0

ADAM_LR, ADAM_B1, ADAM_B2, ADAM_EPS, ADAM_WD, ADAM_STEP = 0.001, 0.9, 0.999, 1e-08, 0.01, 10

SH_IN = IN_WIDTH // N_DEV
SH_OUT = D_MODEL // N_DEV
SH_FF = D_FF // N_DEV
P_IN = 912
IN_STRIDE = 7 * HEAD_DIM
IN_ROWS = IN_STRIDE * (N_DEV - 1) + P_IN
S_ROWS = 136

ATTN_BLOCK = 1024
ROW_BLOCK = 512
MIX_ROWS = 512
MIX_BWD_ROWS = 256
MM_TM, MM_TN, MM_TK = 1536, 1536, 2048
VMEM_CAP = 56 << 20
NEG = -1e30


def _tile(n, cap, unit=128):
    if n <= cap:
        return n
    best = None
    for t in range(unit, cap + 1, unit):
        if n % t == 0:
            best = t
    assert best is not None, (n, cap)
    return best


def _nbytes(shape, dtype):
    return math.prod(shape) * jnp.dtype(dtype).itemsize


_HBM = pl.BlockSpec(memory_space=pltpu.HBM)
MESH = pl.DeviceIdType.MESH

_N_REMOTE = N_DEV - 1


def _exchange_shapes(kind, x):
    if kind[0] == "gather":
        shape = x.shape[:kind[1]] + (N_DEV,) + x.shape[kind[1]:]
    elif kind[0] == "spread":
        shape = (N_DEV,) + x.shape
    else:
        shape = (N_DEV, kind[3]) + x.shape[1:]
    return [jax.ShapeDtypeStruct(shape, x.dtype)]


def _exchange_sems(kind):
    del kind
    return [pltpu.SemaphoreType.DMA((_N_REMOTE,)), pltpu.SemaphoreType.DMA((_N_REMOTE,)), pltpu.SemaphoreType.DMA]


def _plan(kind, x_ref, outs, sems):
    send_sems, recv_sems, local_sem = sems
    x, y, c = lax.axis_index("x"), lax.axis_index("y"), lax.axis_index("c")

    def remote(src, dst, k, to):
        return pltpu.make_async_remote_copy(src_ref=src, dst_ref=dst, send_sem=send_sems.at[k], recv_sem=recv_sems.at[k],
                                            device_id=to, device_id_type=MESH)

    sibling = (x, y, 1 - c)
    if kind[0] == "gather":
        out, = outs
        slot = lambda px, py, pc: out.at[(slice(None),) * kind[1] + (4 * px + 2 * py + pc,)]
        chips = [(1 - x, y), (x, 1 - y), (1 - x, 1 - y)]
        local = [pltpu.make_async_copy(x_ref, slot(x, y, c), local_sem)]
        first = [remote(x_ref, slot(x, y, c), 0, sibling)]
        first += [remote(x_ref, slot(x, y, c), 1 + k, (*chip, c)) for k, chip in enumerate(chips)]
        relays = [(remote(x_ref, slot(*chip, c), 1 + k, (*chip, c)), remote(slot(*chip, c), slot(*chip, c), 4 + k, sibling))
                  for k, chip in enumerate(chips)]
        arrivals = [remote(x_ref, slot(x, y, 1 - c), 0, sibling)]
        arrivals += [remote(x_ref, slot(*chip, 1 - c), 4 + k, sibling) for k, chip in enumerate(chips)]
        return local, first, relays, arrivals
    out, = outs
    mine = 4 * x + 2 * y + c
    if kind[0] == "spread":
        src = lambda d: x_ref
    else:
        _, base, stride, length = kind
        src = lambda d: x_ref.at[pl.ds(pl.multiple_of(base + stride * d, 16), length)]
    first, arrivals = [], []
    for j in range(1, N_DEV):
        peer = (1 - x if j & 4 else x, 1 - y if j & 2 else y, 1 - c if j & 1 else c)
        theirs = 4 * peer[0] + 2 * peer[1] + peer[2]
        first.append(remote(src(theirs), out.at[mine], j - 1, peer))
        arrivals.append(remote(src(theirs), out.at[theirs], j - 1, peer))
    local = [] if local_sem is None else [pltpu.make_async_copy(src(mine), out.at[mine], local_sem)]
    return local, first, [], arrivals


def _start(plan):
    local, first, _, _ = plan
    for cp in local + first:
        cp.start()


def _finish(plan):
    local, first, relays, arrivals = plan
    for arrival, onward in relays:
        arrival.wait_recv()
        onward.start()
    for cp in arrivals:
        cp.wait_recv()
    for cp in first + [onward for _, onward in relays]:
        cp.wait_send()
    for cp in local:
        cp.wait()


def _direct_copies(kind, x_ref, land_ref, send_sems, recv_sems):
    _, first, _, arrivals = _plan(kind, x_ref, [land_ref], (send_sems, recv_sems, None))
    return first, arrivals


_SEM = pl.BlockSpec(memory_space=pltpu.SEMAPHORE)
_DATAFLOW = pltpu.SideEffectType.DATAFLOW_SIDE_EFFECTING


def _exchange_start(kind, x, name):
    me = 4 * lax.axis_index("x") + 2 * lax.axis_index("y") + lax.axis_index("c")
    own = x if kind[0] == "spread" else lax.dynamic_slice_in_dim(x, kind[1] + kind[2] * me, kind[3], 0)
    shape = _exchange_shapes(kind, x)[0].shape
    land = lax.dynamic_update_slice_in_dim(lax.empty(shape, x.dtype), own[None], me, 0)

    def body(x_ref, land_ref, send_sems, recv_sems, x_thru, land_thru, token):
        del x_thru, land_thru
        for cp in _direct_copies(kind, x_ref, land_ref, send_sems, recv_sems)[0]:
            cp.start()
        token[...] = jnp.zeros_like(token)

    *handle, token = pl.pallas_call(
        body, name=name,
        out_shape=(pltpu.SemaphoreType.DMA((_N_REMOTE,)), pltpu.SemaphoreType.DMA((_N_REMOTE,)),
                   pltpu.HBM(x.shape, x.dtype), pltpu.HBM(land.shape, land.dtype), jax.ShapeDtypeStruct((8, 128), F32)),
        in_specs=(_HBM, _HBM), out_specs=(_SEM, _SEM, _HBM, _HBM, pl.BlockSpec(memory_space=pltpu.VMEM)),
        input_output_aliases={0: 2, 1: 3}, compiler_params=pltpu.CompilerParams(has_side_effects=_DATAFLOW),
    )(pltpu.with_memory_space_constraint(x, pltpu.HBM), pltpu.with_memory_space_constraint(land, pltpu.HBM))
    return (kind, *handle), token


def _exchange_wait(handle, after, name):
    kind, send_sems, recv_sems, x_thru, land_thru = handle

    def body(x_ref, land_ref, send_sems, recv_sems, *rest):
        first, arrivals = _direct_copies(kind, x_ref, land_ref, send_sems, recv_sems)
        for cp in first:
            cp.wait_send()
        for cp in arrivals:
            cp.wait_recv()

    return pl.pallas_call(
        body, name=name,
        out_shape=(pltpu.HBM(x_thru.shape, x_thru.dtype), pltpu.HBM(land_thru.shape, land_thru.dtype)),
        in_specs=(_HBM, _HBM, _SEM, _SEM) + (pl.BlockSpec(memory_space=pl.ANY),) * len(after), out_specs=(_HBM, _HBM),
        input_output_aliases={0: 0, 1: 1}, compiler_params=pltpu.CompilerParams(has_side_effects=_DATAFLOW),
    )(x_thru, land_thru, send_sems, recv_sems, *after)[1]


def _exchange(kind, x, name):
    n_out = len(_exchange_shapes(kind, x))

    def body(x_ref, *refs):
        plan = _plan(kind, x_ref, refs[:n_out], refs[n_out:])
        _start(plan)
        _finish(plan)

    return pl.pallas_call(body, name=name, out_shape=_exchange_shapes(kind, x), in_specs=[_HBM],
                          out_specs=[_HBM] * n_out, scratch_shapes=_exchange_sems(kind))(x)


def _call(body, name, grid, in_specs, out_specs, out_shape, args, scratch=(), sem=None, nbytes=0, aliases=None,
          comm=(), deps=()):
    in_specs, out_specs, out_shape, args, scratch = (list(in_specs), list(out_specs), list(out_shape), list(args),
                                                     list(scratch))
    if deps:
        n_real, n_deps, unordered = len(args), len(deps), body

        def body(*refs):
            unordered(*refs[:n_real], *refs[n_real + n_deps:])

        in_specs += [pl.BlockSpec(memory_space=pl.ANY)] * n_deps
        args += list(deps)
    if comm:
        n_in, n_out, n_scr, n_ops = len(args), len(out_shape), len(scratch), len(comm)
        kinds = [kind for kind, _ in comm]
        shapes = [_exchange_shapes(kind, x) for kind, x in comm]
        inner = body

        def body(*refs):
            ins, refs = refs[:n_in], refs[n_in:]
            cins, refs = refs[:n_ops], refs[n_ops:]
            outs, refs = refs[:n_out], refs[n_out:]
            couts = []
            for sh in shapes:
                couts.append(refs[:len(sh)])
                refs = refs[len(sh):]
            own_scratch, sems = refs[:n_scr], refs[n_scr:]
            first = last = None
            for axis, size in enumerate(grid):
                at_start, at_end = pl.program_id(axis) == 0, pl.program_id(axis) == size - 1
                first = at_start if first is None else first & at_start
                last = at_end if last is None else last & at_end
            plans = [_plan(kinds[o], cins[o], couts[o], sems[3 * o:3 * o + 3]) for o in range(n_ops)]

            @pl.when(first)
            def _():
                for p in plans:
                    _start(p)

            inner(*ins, *outs, *own_scratch)

            @pl.when(last)
            def _():
                for p in plans:
                    _finish(p)

        in_specs += [_HBM] * n_ops
        args += [x for _, x in comm]
        for kind, sh in zip(kinds, shapes):
            out_shape += sh
            out_specs += [_HBM] * len(sh)
            scratch += _exchange_sems(kind)
        sem = ("arbitrary",) * len(grid)
    limit = int(min(max(2 * nbytes + (8 << 20), 32 << 20), VMEM_CAP))
    return pl.pallas_call(
        body, name=name, grid=grid, in_specs=in_specs, out_specs=out_specs, out_shape=out_shape,
        scratch_shapes=scratch, input_output_aliases=aliases or {},
        compiler_params=pltpu.CompilerParams(dimension_semantics=sem, vmem_limit_bytes=limit),
    )(*args)


def _sum_slots(x, tr, name):
    n, r, c = x.shape

    def body(x_ref, o_ref):
        acc = x_ref[0].astype(F32)
        for d in range(1, n):
            acc = acc + x_ref[d].astype(F32)
        o_ref[...] = acc

    return _call(body, name, (r // tr,), [pl.BlockSpec((n, tr, c), lambda i: (0, i, 0))],
                 [pl.BlockSpec((tr, c), lambda i: (i, 0))], [jax.ShapeDtypeStruct((r, c), F32)], [x],
                 sem=("parallel",), nbytes=_nbytes((n, tr, c), x.dtype) + _nbytes((tr, c), F32))[0]


_DIMS = {"nn": (((1,), (0,)), ((), ())), "nt": (((1,), (1,)), ((), ())), "tn": (((0,), (0,)), ((), ()))}


def _matmul(a, b, mode, out_dtype, name, comm=(), deps=(), halves=False):
    a_shape = (a.shape[1], 2 * a.shape[2]) if halves else a.shape
    if mode == "nn":
        (m, k), (k2, n) = a_shape, b.shape
    elif mode == "nt":
        (m, k), (n, k2) = a_shape, b.shape
    else:
        (k, m), (k2, n) = a_shape, b.shape
    assert k == k2, (a.shape, b.shape, mode)
    tm, tn, tk = _tile(m, MM_TM), _tile(n, MM_TN), _tile(k, MM_TK)
    if halves and mode == "nn":
        tk = _tile(k // 2, MM_TK)
    if halves and mode == "tn":
        tm = _tile(m // 2, MM_TM)
    nk = k // tk
    dims = _DIMS[mode]
    if halves:
        per_half = (k // 2) // tk if mode == "nn" else (m // 2) // tm
        a_spec = {"nn": pl.BlockSpec((None, tm, tk), lambda i, j, kk: (kk // per_half, i, kk % per_half)),
                  "tn": pl.BlockSpec((None, tk, tm), lambda i, j, kk: (i // per_half, kk, i % per_half))}[mode]
    else:
        a_spec = {"nn": pl.BlockSpec((tm, tk), lambda i, j, kk: (i, kk)),
                  "nt": pl.BlockSpec((tm, tk), lambda i, j, kk: (i, kk)),
                  "tn": pl.BlockSpec((tk, tm), lambda i, j, kk: (kk, i))}[mode]
    b_spec = {"nn": pl.BlockSpec((tk, tn), lambda i, j, kk: (kk, j)),
              "nt": pl.BlockSpec((tn, tk), lambda i, j, kk: (j, kk)),
              "tn": pl.BlockSpec((tk, tn), lambda i, j, kk: (kk, j))}[mode]

    def partial_product(a_ref, b_ref):
        return lax.dot_general(a_ref[...].astype(BF16), b_ref[...].astype(BF16), dims, preferred_element_type=F32)

    if nk == 1:
        def body(a_ref, b_ref, o_ref):
            o_ref[...] = partial_product(a_ref, b_ref).astype(o_ref.dtype)
        scratch = []
    else:
        def body(a_ref, b_ref, o_ref, acc_ref):
            kk = pl.program_id(2)

            @pl.when(kk == 0)
            def _():
                acc_ref[...] = jnp.zeros_like(acc_ref)

            acc_ref[...] += partial_product(a_ref, b_ref)

            @pl.when(kk == nk - 1)
            def _():
                o_ref[...] = acc_ref[...].astype(o_ref.dtype)
        scratch = [pltpu.VMEM((tm, tn), F32)]

    nbytes = (_nbytes((tm, tk), a.dtype) + _nbytes((tk, tn), b.dtype) + _nbytes((tm, tn), out_dtype)
              + _nbytes((tm, tn), F32))
    res = _call(body, name, (m // tm, n // tn, nk), [a_spec, b_spec],
                [pl.BlockSpec((tm, tn), lambda i, j, kk: (i, j))], [jax.ShapeDtypeStruct((m, n), out_dtype)], [a, b],
                scratch=scratch, sem=("parallel", "parallel", "arbitrary"), nbytes=nbytes, comm=comm, deps=deps)
    return res if comm else res[0]


def _matmul_post_norm(a, b, x, g_post, name, g_next=None, target=None):
    (m, k), (k2, d) = a.shape, b.shape
    assert k == k2 and d == D_MODEL
    tm, tk = _tile(m, ROW_BLOCK), _tile(k, MM_TK)
    nk = k // tk
    with_loss = target is not None

    def body(a_ref, b_ref, x_ref, gp_ref, last_ref, o_ref, first_ref, second_ref, acc_ref):
        i, kk = pl.program_id(0), pl.program_id(1)

        @pl.when(kk == 0)
        def _():
            acc_ref[...] = jnp.zeros_like(acc_ref)

        acc_ref[...] += jnp.dot(a_ref[...], b_ref[...], preferred_element_type=F32)

        @pl.when(kk == nk - 1)
        def _():
            ov = acc_ref[...]
            o_ref[...] = ov
            xn = x_ref[...] + (ov * _rms(ov)) * gp_ref[...]
            if with_loss:
                err = xn - last_ref[...]
                first_ref[...] = err / d

                @pl.when(i == 0)
                def _():
                    second_ref[...] = jnp.zeros_like(second_ref)

                second_ref[...] += 0.5 * jnp.sum(jnp.mean(err * err, axis=-1, keepdims=True), axis=0, keepdims=True)
            else:
                first_ref[...] = xn
                second_ref[...] = ((xn * _rms(xn)) * last_ref[...]).astype(BF16)

    rows = pl.BlockSpec((tm, d), lambda i, kk: (i, 0))
    vec = pl.BlockSpec((1, d), lambda i, kk: (0, 0))
    second = ((pl.BlockSpec((1, 1), lambda i, kk: (0, 0)), jax.ShapeDtypeStruct((1, 1), F32)) if with_loss
              else (rows, jax.ShapeDtypeStruct((m, d), BF16)))
    nbytes = _nbytes((tm, tk), BF16) + _nbytes((tk, d), BF16) + 6 * _nbytes((tm, d), F32)
    return _call(body, name, (m // tm, nk),
                 [pl.BlockSpec((tm, tk), lambda i, kk: (i, kk)), pl.BlockSpec((tk, d), lambda i, kk: (kk, 0)), rows, vec,
                  rows if with_loss else vec],
                 [rows, rows, second[0]],
                 [jax.ShapeDtypeStruct((m, d), F32), jax.ShapeDtypeStruct((m, d), F32), second[1]],
                 [a, b, x, g_post, target if with_loss else g_next],
                 scratch=[pltpu.VMEM((tm, d), F32)], sem=("arbitrary", "arbitrary"), nbytes=nbytes)


def _rms(x):
    return lax.rsqrt(jnp.mean(x * x, axis=-1, keepdims=True) + EPS)


def _rms_bwd(dz, a, g):
    r = _rms(a)
    dzg = dz * g
    da = r * dzg - a * (r * r * r) * jnp.mean(dzg * a, axis=-1, keepdims=True)
    return da, dz * (a * r)


def _row_spec(tr, width):
    return pl.BlockSpec((tr, width), lambda i: (i, 0))


def _vec_spec(width):
    return pl.BlockSpec((1, width), lambda i: (0, 0))


def _rms_fwd(x, g, name):
    t, d = x.shape
    tr = min(ROW_BLOCK, t)

    def body(x_ref, g_ref, h_ref):
        xv = x_ref[...]
        h_ref[...] = ((xv * _rms(xv)) * g_ref[...]).astype(BF16)

    return _call(body, name, (t // tr,), [_row_spec(tr, d), _vec_spec(d)], [_row_spec(tr, d)],
                 [jax.ShapeDtypeStruct((t, d), BF16)], [x, g], sem=("parallel",), nbytes=3 * _nbytes((tr, d), F32))[0]


def _norm_bwd(dskip, name, pre=None, post=None):
    t, d = dskip.shape
    tr = min(ROW_BLOCK, t)
    n_in = 1 + (3 if pre else 0) + (2 if post else 0)

    def body(*refs):
        ins, outs = list(refs[:n_in]), list(refs[n_in:])
        first = pl.program_id(0) == 0
        dx = ins.pop(0)[...]
        if pre:
            dh_ref, xin_ref, gpre_ref = ins.pop(0), ins.pop(0), ins.pop(0)
            dxin, dg_rows = _rms_bwd(dh_ref[...].astype(F32), xin_ref[...], gpre_ref[...])
            dx = dx + dxin
            dx_ref, dgpre_ref = outs.pop(0), outs.pop(0)
            dx_ref[...] = dx

            @pl.when(first)
            def _():
                dgpre_ref[...] = jnp.zeros_like(dgpre_ref)

            dgpre_ref[...] += jnp.sum(dg_rows, axis=0, keepdims=True)
        if post:
            a_ref, gpost_ref = ins.pop(0), ins.pop(0)
            da, dg_rows = _rms_bwd(dx, a_ref[...], gpost_ref[...])
            da_ref, dgpost_ref = outs.pop(0), outs.pop(0)
            da_ref[...] = da.astype(BF16)

            @pl.when(first)
            def _():
                dgpost_ref[...] = jnp.zeros_like(dgpost_ref)

            dgpost_ref[...] += jnp.sum(dg_rows, axis=0, keepdims=True)

    args, in_specs, out_shape, out_specs = [dskip], [_row_spec(tr, d)], [], []
    if pre:
        args += list(pre)
        in_specs += [_row_spec(tr, d), _row_spec(tr, d), _vec_spec(d)]
        out_shape += [jax.ShapeDtypeStruct((t, d), F32), jax.ShapeDtypeStruct((1, d), F32)]
        out_specs += [_row_spec(tr, d), _vec_spec(d)]
    if post:
        args += list(post)
        in_specs += [_row_spec(tr, d), _vec_spec(d)]
        out_shape += [jax.ShapeDtypeStruct((t, d), BF16), jax.ShapeDtypeStruct((1, d), F32)]
        out_specs += [_row_spec(tr, d), _vec_spec(d)]
    return _call(body, name, (t // tr,), in_specs, out_specs, out_shape, args, sem=("arbitrary",),
                 nbytes=7 * _nbytes((tr, d), F32))


def _sigmoid(x):
    return 1.0 / (1.0 + jnp.exp(-x))


def _ffn_in(h, w_gu_t, name):
    t, d = h.shape
    tm, tn = _tile(t, 1024), _tile(D_FF, MM_TN)
    n_tiles = D_FF // tn

    def body(h_ref, wg_ref, wu_ref, g_ref, u_ref, a_ref):
        hv = h_ref[...]
        gate = lax.dot_general(hv, wg_ref[...], _DIMS["nt"], preferred_element_type=F32).astype(BF16)
        up = lax.dot_general(hv, wu_ref[...], _DIMS["nt"], preferred_element_type=F32).astype(BF16)
        g_ref[...] = gate
        u_ref[...] = up
        gate, up = gate.astype(F32), up.astype(F32)
        a_ref[...] = ((gate * _sigmoid(gate)) * up).astype(BF16)

    tile = pl.BlockSpec((tm, tn), lambda i, j: (i, j))
    shape = jax.ShapeDtypeStruct((t, D_FF), BF16)
    return _call(body, name, (t // tm, n_tiles),
                 [pl.BlockSpec((tm, d), lambda i, j: (i, 0)), pl.BlockSpec((tn, d), lambda i, j: (j, 0)),
                  pl.BlockSpec((tn, d), lambda i, j: (j + n_tiles, 0))],
                 [tile, tile, tile], [shape, shape, shape], [h, w_gu_t, w_gu_t], sem=("parallel", "parallel"),
                 nbytes=_nbytes((tm, d), BF16) + 2 * _nbytes((tn, d), BF16) + 5 * _nbytes((tm, tn), F32))


def _ffn_in_bwd(dd, w_d, gate, up, name):
    t, d = dd.shape
    tm, tn = _tile(t, 1024), _tile(D_FF, MM_TN)

    def body(dd_ref, w_ref, g_ref, u_ref, o_ref):
        da = lax.dot_general(dd_ref[...], w_ref[...], _DIMS["nt"], preferred_element_type=F32)
        gate, up = g_ref[...].astype(F32), u_ref[...].astype(F32)
        s = _sigmoid(gate)
        o_ref[0] = (da * up * (s * (1.0 + gate * (1.0 - s)))).astype(BF16)
        o_ref[1] = (da * (gate * s)).astype(BF16)

    tile = pl.BlockSpec((tm, tn), lambda i, j: (i, j))
    return _call(body, name, (t // tm, D_FF // tn),
                 [pl.BlockSpec((tm, d), lambda i, j: (i, 0)), pl.BlockSpec((tn, d), lambda i, j: (j, 0)), tile, tile],
                 [pl.BlockSpec((2, tm, tn), lambda i, j: (0, i, j))], [jax.ShapeDtypeStruct((2, t, D_FF), BF16)],
                 [dd, w_d, gate, up], sem=("parallel", "parallel"),
                 nbytes=_nbytes((tm, d), BF16) + _nbytes((tn, d), BF16) + 6 * _nbytes((tm, tn), F32))[0]


def _split3(x):
    hi = x.astype(BF16)
    r1 = x - hi.astype(F32)
    mid = r1.astype(BF16)
    lo = (r1 - mid.astype(F32)).astype(BF16)
    return hi, mid, lo


def _dot_exact(mat01, x):
    hi, mid, lo = _split3(x)
    out = jnp.dot(mat01, hi, preferred_element_type=F32)
    out = out + jnp.dot(mat01, mid, preferred_element_type=F32)
    return out + jnp.dot(mat01, lo, preferred_element_type=F32)


def _dot_exact_rhs(x, mat01):
    hi, mid, lo = _split3(x)
    out = jnp.dot(hi, mat01, preferred_element_type=F32)
    out = out + jnp.dot(mid, mat01, preferred_element_type=F32)
    return out + jnp.dot(lo, mat01, preferred_element_type=F32)


def _tri(lower):
    r = lax.broadcasted_iota(jnp.int32, (CHUNK, CHUNK), 0)
    c = lax.broadcasted_iota(jnp.int32, (CHUNK, CHUNK), 1)
    return jnp.where(r >= c if lower else r <= c, 1.0, 0.0).astype(BF16)


def _log_sigmoid(z):
    return jnp.minimum(z, 0.0) - jnp.log(1.0 + jnp.exp(-jnp.abs(z)))


def _forget_cumsum(proj, b_pad, name):
    t = proj.shape[0]
    nb = t // CHUNK

    def body(f_ref, b_ref, c_ref):
        tri = _tri(True)
        b = b_ref[...]

        def blk(i, carry):
            rows = pl.ds(pl.multiple_of(i * CHUNK, CHUNK), CHUNK)
            cs = _dot_exact(tri, _log_sigmoid(f_ref[rows, :] + b)) + carry
            c_ref[rows, :] = cs
            return cs[CHUNK - 1:CHUNK, :]

        lax.fori_loop(0, nb, blk, jnp.zeros((1, HEAD_DIM), F32))

    return _call(body, name, (1,),
                 [pl.BlockSpec((t, HEAD_DIM), lambda i: (0, CB_F)), pl.BlockSpec((1, HEAD_DIM), lambda i: (0, 0))],
                 [pl.BlockSpec((t, HEAD_DIM), lambda i: (0, 0))], [jax.ShapeDtypeStruct((t, HEAD_DIM), F32)],
                 [proj, b_pad], sem=("arbitrary",), nbytes=2 * _nbytes((t, HEAD_DIM), F32))[0]


def _causal(block):
    r = lax.broadcasted_iota(jnp.int32, (block, block), 0)
    c = lax.broadcasted_iota(jnp.int32, (block, block), 1)
    return c <= r


def _lanes(x, width):
    return jnp.concatenate([x] * (width // HEAD_DIM), axis=1)


def _chunks(t, blk):
    return [slice(r * blk, (r + 1) * blk) for r in range(t // blk)]


def _lane_of_head(x, h):
    lane = lax.broadcasted_iota(jnp.int32, x.shape, 1)
    return jnp.broadcast_to(jnp.sum(jnp.where(lane == h, x, 0.0), axis=1, keepdims=True), x.shape)


def _causal_t(block):
    r = lax.broadcasted_iota(jnp.int32, (block, block), 0)
    c = lax.broadcasted_iota(jnp.int32, (block, block), 1)
    return r <= c


def _attn_fwd_t(proj, c_pad, c_row, name, comm=()):
    t = proj.shape[0]
    blk = min(ATTN_BLOCK, t)
    nq = t // blk
    scale = HEAD_DIM ** -0.5

    def body(q_ref, k_ref, v_ref, cp_ref, cr_ref, y_ref, e_ref, kb_s, vt_s, cb_s):
        h, i = pl.program_id(0), pl.program_id(1)

        @pl.when(i == 0)
        def _():
            for rows in _chunks(t, blk):
                kb_s[rows, :] = k_ref[rows, :].astype(BF16)
                vt_s[:, rows] = v_ref[rows, :].T.astype(BF16)
                cb_s[rows, :] = _lane_of_head(cp_ref[rows, :], h)

        q = (q_ref[...] * scale).astype(BF16)
        ci = cr_ref[:, pl.ds(pl.multiple_of(i * blk, blk), blk)]

        def step(j, carry, diagonal):
            m, l, acc = carry
            rows = pl.ds(pl.multiple_of(j * blk, blk), blk)
            s = lax.dot_general(kb_s[rows, :], q, _DIMS["nt"], preferred_element_type=F32)
            s = s + ci - _lanes(cb_s[rows, :], blk)
            if diagonal:
                s = jnp.where(_causal_t(blk), s, NEG)
            m_new = jnp.maximum(m, jnp.max(s, axis=0, keepdims=True))
            alpha = jnp.exp(m - m_new)
            p = jnp.exp(s - m_new)
            l = alpha * l + jnp.sum(p, axis=0, keepdims=True)
            acc = alpha * acc + jnp.dot(vt_s[:, rows], p.astype(BF16), preferred_element_type=F32)
            return m_new, l, acc

        init = (jnp.full((1, blk), NEG, F32), jnp.zeros((1, blk), F32), jnp.zeros((HEAD_DIM, blk), F32))
        carry = lax.fori_loop(0, i, lambda j, c: step(j, c, False), init)
        m, l, acc = step(i, carry, True)
        y_ref[...] = (acc / l).T
        e_ref[...] = ci - (m + jnp.log(l))

    nbytes = 3 * _nbytes((t, HEAD_DIM), F32) + 4 * _nbytes((blk, HEAD_DIM), F32) + 4 * _nbytes((blk, blk), F32)
    return _call(body, name, (N_HEADS, nq),
                 [pl.BlockSpec((blk, HEAD_DIM), lambda h, i: (i, CB_QKV + 3 * h)),
                  pl.BlockSpec((t, HEAD_DIM), lambda h, i: (0, CB_QKV + 3 * h + 1)),
                  pl.BlockSpec((t, HEAD_DIM), lambda h, i: (0, CB_QKV + 3 * h + 2)),
                  pl.BlockSpec((t, HEAD_DIM), lambda h, i: (0, 0)),
                  pl.BlockSpec((None, 1, t), lambda h, i: (h, 0, 0))],
                 [pl.BlockSpec((blk, HEAD_DIM), lambda h, i: (i, h)),
                  pl.BlockSpec((None, 1, blk), lambda h, i: (h, 0, i))],
                 [jax.ShapeDtypeStruct((t, D_MODEL), F32), jax.ShapeDtypeStruct((N_HEADS, 1, t), F32)],
                 [proj, proj, proj, c_pad, c_row],
                 scratch=[pltpu.VMEM((t, HEAD_DIM), BF16), pltpu.VMEM((HEAD_DIM, t), BF16),
                          pltpu.VMEM((t, HEAD_DIM), F32)],
                 sem=("parallel", "arbitrary"), nbytes=nbytes, comm=comm)


def _attn_bwd_t(proj, dy, e_row, delta_b, c_pad, dproj, name, deps=()):
    t = proj.shape[0]
    blk = min(ATTN_BLOCK, t)
    nb = t // blk
    scale = HEAD_DIM ** -0.5

    def body(q_ref, k_ref, v_ref, dy_ref, e_ref, dl_ref, cp_ref, dproj_in, dqkv_ref, dck_ref, dcq_ref,
             qb_s, qt_s, dyb_s, dyt_s, dl_s, dqt_acc):
        del dproj_in
        h, j = pl.program_id(0), pl.program_id(1)

        @pl.when(j == 0)
        def _():
            for rows in _chunks(t, blk):
                qs = q_ref[rows, :] * scale
                qb_s[rows, :] = qs.astype(BF16)
                qt_s[:, rows] = qs.T.astype(BF16)
                dyr = dy_ref[rows, :]
                dyb_s[rows, :] = dyr.astype(BF16)
                dyt_s[:, rows] = dyr.T.astype(BF16)
                dl_s[:, rows] = dl_ref[rows, :].T[0:1, :]
            dqt_acc[...] = jnp.zeros_like(dqt_acc)
            dcq_ref[...] = jnp.zeros_like(dcq_ref)

        kf = k_ref[...]
        kb, kt = kf.astype(BF16), kf.T.astype(BF16)
        vb = v_ref[...].astype(BF16)
        cj = _lanes(_lane_of_head(cp_ref[...], h), blk)

        def step(i, carry, diagonal):
            dkt, dvt, dkey = carry
            cols = pl.ds(pl.multiple_of(i * blk, blk), blk)
            s = lax.dot_general(kb, qb_s[cols, :], _DIMS["nt"], preferred_element_type=F32) + e_ref[:, cols] - cj
            if diagonal:
                s = jnp.where(_causal_t(blk), s, NEG)
            p = jnp.exp(s)
            dp = lax.dot_general(vb, dyb_s[cols, :], _DIMS["nt"], preferred_element_type=F32)
            ds = p * (dp - dl_s[:, cols])
            pb, dsb = p.astype(BF16), ds.astype(BF16)
            dvt = dvt + lax.dot_general(dyt_s[:, cols], pb, _DIMS["nt"], preferred_element_type=F32)
            dkt = dkt + lax.dot_general(qt_s[:, cols], dsb, _DIMS["nt"], preferred_element_type=F32)
            dqt_acc[:, cols] += jnp.dot(kt, dsb, preferred_element_type=F32) * scale
            dcq_ref[:, cols] += jnp.sum(ds, axis=0, keepdims=True)
            for k in range(blk // HEAD_DIM):
                dkey = dkey + ds[:, k * HEAD_DIM:(k + 1) * HEAD_DIM]
            return dkt, dvt, dkey

        zeros = jnp.zeros((HEAD_DIM, blk), F32)
        carry = step(j, (zeros, zeros, jnp.zeros((blk, HEAD_DIM), F32)), True)
        dkt, dvt, dkey = lax.fori_loop(j + 1, nb, lambda i, c: step(i, c, False), carry)
        mine = pl.ds(pl.multiple_of(j * blk, blk), blk)
        dqkv_ref[:, :HEAD_DIM] = dqt_acc[:, mine].T.astype(BF16)
        dqkv_ref[:, HEAD_DIM:2 * HEAD_DIM] = dkt.T.astype(BF16)
        dqkv_ref[:, 2 * HEAD_DIM:] = dvt.T.astype(BF16)
        dck_ref[...] = -jnp.sum(dkey.T, axis=0, keepdims=True)

    full = lambda cb: pl.BlockSpec((t, HEAD_DIM), lambda h, j: (0, cb(h)))
    head = lambda h: h
    row = pl.BlockSpec((None, 1, t), lambda h, j: (h, 0, 0))
    nbytes = (8 * _nbytes((t, HEAD_DIM), F32) + 4 * _nbytes((blk, HEAD_DIM), F32) + 6 * _nbytes((blk, blk), F32))
    return _call(body, name, (N_HEADS, nb),
                 [full(lambda h: CB_QKV + 3 * h),
                  pl.BlockSpec((blk, HEAD_DIM), lambda h, j: (j, CB_QKV + 3 * h + 1)),
                  pl.BlockSpec((blk, HEAD_DIM), lambda h, j: (j, CB_QKV + 3 * h + 2)),
                  full(head), row, full(head),
                  pl.BlockSpec((blk, HEAD_DIM), lambda h, j: (j, 0)),
                  pl.BlockSpec(memory_space=pl.ANY)],
                 [pl.BlockSpec((blk, QKV_WIDTH), lambda h, j: (j, (GATED + F_WIDTH) // QKV_WIDTH + h)),
                  pl.BlockSpec((None, 1, blk), lambda h, j: (h, 0, j)),
                  row],
                 [jax.ShapeDtypeStruct(dproj.shape, dproj.dtype), jax.ShapeDtypeStruct((N_HEADS, 1, t), F32),
                  jax.ShapeDtypeStruct((N_HEADS, 1, t), F32)],
                 [proj, proj, proj, dy, e_row, delta_b, c_pad, dproj],
                 scratch=[pltpu.VMEM((t, HEAD_DIM), BF16), pltpu.VMEM((HEAD_DIM, t), BF16),
                          pltpu.VMEM((t, HEAD_DIM), BF16), pltpu.VMEM((HEAD_DIM, t), BF16),
                          pltpu.VMEM((1, t), F32), pltpu.VMEM((HEAD_DIM, t), F32)],
                 sem=("parallel", "arbitrary"), nbytes=nbytes, aliases={7: 0}, deps=deps)


def _forget_bwd_t(proj, b_pad, dc, dproj, name):
    t = proj.shape[0]
    tr = min(ROW_BLOCK, t)
    nb = t // tr
    rev = lambda i: nb - 1 - i

    def body(f_ref, b_ref, dc_ref, dproj_in, df_ref, db_ref, run_ref):
        del dproj_in

        @pl.when(pl.program_id(0) == 0)
        def _():
            run_ref[...] = jnp.zeros_like(run_ref)
            db_ref[...] = jnp.zeros_like(db_ref)

        tri = _tri(False)
        b = b_ref[...]
        lane = lax.broadcasted_iota(jnp.int32, (CHUNK, HEAD_DIM), 1)
        df_ref[...] = jnp.zeros_like(df_ref)
        for n in reversed(range(tr // CHUNK)):
            rows = slice(n * CHUNK, (n + 1) * CHUNK)
            dlogf = _dot_exact(tri, dc_ref[rows, :]) + run_ref[...]
            run_ref[...] = dlogf[0:1, :]
            z = f_ref[rows, :] + b
            e = jnp.exp(-jnp.abs(z))
            sig_neg = jnp.where(z >= 0.0, e, 1.0) / (1.0 + e)
            df = jnp.where(lane < N_HEADS, dlogf * sig_neg, 0.0)
            df_ref[rows, :HEAD_DIM] = df.astype(BF16)
            db_ref[...] += jnp.sum(df, axis=0, keepdims=True)

    return _call(body, name, (nb,),
                 [pl.BlockSpec((tr, HEAD_DIM), lambda i: (rev(i), CB_F)), pl.BlockSpec((1, HEAD_DIM), lambda i: (0, 0)),
                  pl.BlockSpec((tr, HEAD_DIM), lambda i: (rev(i), 0)), pl.BlockSpec(memory_space=pl.ANY)],
                 [pl.BlockSpec((tr, F_WIDTH), lambda i: (rev(i), GATED // F_WIDTH)),
                  pl.BlockSpec((1, HEAD_DIM), lambda i: (0, 0))],
                 [jax.ShapeDtypeStruct(dproj.shape, dproj.dtype), jax.ShapeDtypeStruct((1, HEAD_DIM), F32)],
                 [proj, b_pad, dc, dproj], scratch=[pltpu.VMEM((1, HEAD_DIM), F32)],
                 sem=("arbitrary",), nbytes=4 * _nbytes((tr, HEAD_DIM), F32), aliases={3: 0})


_GELU_K = math.sqrt(2.0 / math.pi)
_GELU_C = 0.044715


def _gelu(x):
    t = jnp.tanh(_GELU_K * (x + _GELU_C * (x * x * x)))
    return 0.5 * x * (1.0 + t), t


def _gelu_grad(x, t):
    return 0.5 * (1.0 + t) + 0.5 * x * (1.0 - t * t) * (_GELU_K * (1.0 + 3.0 * _GELU_C * (x * x)))


def _layernorm_stats(a):
    mu = jnp.mean(a, axis=-1, keepdims=True)
    xc = a - mu
    r = lax.rsqrt(jnp.mean(xc * xc, axis=-1, keepdims=True) + EPS)
    return xc * r, r


def _group(g):
    return slice(g * CHUNK, (g + 1) * CHUNK)


def _fixed(shape):
    return pl.BlockSpec(shape, lambda i: (0,) * len(shape))


def _mix_fwd(proj, y_b, w_s, b_cols, g_v, name):
    t = proj.shape[0]
    step_rows = min(MIX_ROWS, t)

    def body(u_ref, vs_ref, ga_ref, gb_ref, yb_ref, w_ref, b_ref, gv_ref, o_ref):
        mask = _causal(CHUNK)
        w = [jnp.where(mask, w_ref[g], 0.0).astype(BF16) for g in range(N_GROUPS)]
        for rows in _chunks(step_rows, CHUNK):
            a_u, _ = _gelu(u_ref[rows, :])
            a_v, _ = _gelu(vs_ref[rows, :])
            xhat, _ = _layernorm_stats(a_v)
            vn = (xhat * gv_ref[...]).astype(BF16)
            sa, sb = _sigmoid(ga_ref[rows, :]), _sigmoid(gb_ref[rows, :])
            yb = yb_ref[rows, :]
            for g in range(N_GROUPS):
                cols = _group(g)
                mixed = jnp.dot(w[g], vn[:, cols], preferred_element_type=F32) + b_ref[:, g:g + 1]
                o_ref[rows, cols] = (sa[:, cols] * (a_u[:, cols] * mixed) + sb[:, cols] * yb[:, cols]).astype(BF16)

    gated = lambda k: pl.BlockSpec((step_rows, D_MODEL), lambda i: (i, k))
    return _call(body, name, (t // step_rows,),
                 [gated(0), gated(1), gated(2), gated(3), _row_spec(step_rows, D_MODEL),
                  _fixed((N_GROUPS, CHUNK, CHUNK)), _fixed((CHUNK, N_GROUPS)), _fixed((1, D_MODEL))],
                 [_row_spec(step_rows, D_MODEL)], [jax.ShapeDtypeStruct((t, D_MODEL), BF16)],
                 [proj, proj, proj, proj, y_b, w_s, b_cols, g_v], sem=("parallel",),
                 nbytes=16 * _nbytes((step_rows, D_MODEL), F32))[0]


def _mix_bwd(proj, y_b, dmerged, w_s, b_cols, g_v, sel, name, deps=()):
    t = proj.shape[0]
    step_rows = min(MIX_BWD_ROWS, t)

    def body(u_ref, vs_ref, ga_ref, gb_ref, yb_ref, dm_ref, w_ref, b_ref, gv_ref, sel_ref,
             dg_ref, dyb_ref, delta_ref, dw_ref, dbt_ref, dgv_ref):
        @pl.when(pl.program_id(0) == 0)
        def _():
            dw_ref[...] = jnp.zeros_like(dw_ref)
            dbt_ref[...] = jnp.zeros_like(dbt_ref)
            dgv_ref[...] = jnp.zeros_like(dgv_ref)

        gv = gv_ref[...]
        mask = _causal(CHUNK)
        w = [jnp.where(mask, w_ref[g], 0.0).astype(BF16) for g in range(N_GROUPS)]
        for rows in _chunks(step_rows, CHUNK):
            u, vs = u_ref[rows, :], vs_ref[rows, :]
            a_u, t_u = _gelu(u)
            a_v, t_v = _gelu(vs)
            xhat, r = _layernorm_stats(a_v)
            vn = (xhat * gv).astype(BF16)
            sa, sb = _sigmoid(ga_ref[rows, :]), _sigmoid(gb_ref[rows, :])
            yb, dm = yb_ref[rows, :], dm_ref[rows, :].astype(F32)
            dyb = dm * sb
            dyb_ref[rows, :] = dyb
            dyb_yb = dyb * yb
            dg_ref[rows, 3 * D_MODEL:] = (dm * yb * (sb * (1.0 - sb))).astype(BF16)
            dya = dm * sa
            dmixed_parts, dvn_parts = [], []
            for g in range(N_GROUPS):
                cols = _group(g)
                delta_ref[rows, cols] = jnp.broadcast_to(jnp.sum(dyb_yb[:, cols], axis=1, keepdims=True),
                                                         (CHUNK, CHUNK))
                mixed = jnp.dot(w[g], vn[:, cols], preferred_element_type=F32) + b_ref[:, g:g + 1]
                y_a = a_u[:, cols] * mixed
                dg_ref[rows, 2 * D_MODEL + g * CHUNK:2 * D_MODEL + (g + 1) * CHUNK] = (
                    dm[:, cols] * y_a * (sa[:, cols] * (1.0 - sa[:, cols]))).astype(BF16)
                dg_ref[rows, cols] = (dya[:, cols] * mixed * _gelu_grad(u[:, cols], t_u[:, cols])).astype(BF16)
                dmixed = dya[:, cols] * a_u[:, cols]
                dmb = dmixed.astype(BF16)
                dw = lax.dot_general(dmb, vn[:, cols], _DIMS["nt"], preferred_element_type=F32)
                dw_ref[g] += jnp.where(mask, dw, 0.0)
                dvn_parts.append(lax.dot_general(w[g], dmb, _DIMS["tn"], preferred_element_type=F32))
                dmixed_parts.append(dmixed)
            dmixed_all = jnp.concatenate(dmixed_parts, axis=1)
            dvn = jnp.concatenate(dvn_parts, axis=1)
            dbt_ref[...] += _dot_exact_rhs(dmixed_all, sel_ref[...])
            dgv_ref[...] += jnp.sum(dvn * xhat, axis=0, keepdims=True)
            dxh = dvn * gv
            da_v = r * (dxh - jnp.mean(dxh, axis=-1, keepdims=True)
                        - xhat * jnp.mean(dxh * xhat, axis=-1, keepdims=True))
            dg_ref[rows, D_MODEL:2 * D_MODEL] = (da_v * _gelu_grad(vs, t_v)).astype(BF16)

    row = lambda width: _row_spec(step_rows, width)
    gated = lambda k: pl.BlockSpec((step_rows, D_MODEL), lambda i: (i, k))
    return _call(body, name, (t // step_rows,),
                 [gated(0), gated(1), gated(2), gated(3), row(D_MODEL), row(D_MODEL),
                  _fixed((N_GROUPS, CHUNK, CHUNK)), _fixed((CHUNK, N_GROUPS)), _fixed((1, D_MODEL)),
                  _fixed((D_MODEL, HEAD_DIM))],
                 [row(GATED), row(D_MODEL), row(D_MODEL),
                  _fixed((N_GROUPS, CHUNK, CHUNK)), _fixed((CHUNK, HEAD_DIM)), _fixed((1, D_MODEL))],
                 [jax.ShapeDtypeStruct((t, IN_PAD), BF16), jax.ShapeDtypeStruct((t, D_MODEL), F32),
                  jax.ShapeDtypeStruct((t, D_MODEL), F32), jax.ShapeDtypeStruct((N_GROUPS, CHUNK, CHUNK), F32),
                  jax.ShapeDtypeStruct((CHUNK, HEAD_DIM), F32), jax.ShapeDtypeStruct((1, D_MODEL), F32)],
                 [proj, proj, proj, proj, y_b, dmerged, w_s, b_cols, g_v, sel], sem=("arbitrary",),
                 nbytes=24 * _nbytes((step_rows, D_MODEL), F32), deps=deps)


def _adamw_update(w_ref, g_ref, m_ref, v_ref, d_ref, nm_ref, nv_ref):
    gv = g_ref[...]
    nm = ADAM_B1 * m_ref[...] + (1.0 - ADAM_B1) * gv
    nv = ADAM_B2 * v_ref[...] + (1.0 - ADAM_B2) * (gv * gv)
    m_hat = nm / (1.0 - ADAM_B1 ** ADAM_STEP)
    v_hat = nv / (1.0 - ADAM_B2 ** ADAM_STEP)
    d_ref[...] = -ADAM_LR * (m_hat / (jnp.sqrt(v_hat) + ADAM_EPS) + ADAM_WD * w_ref[...])
    nm_ref[...] = nm
    nv_ref[...] = nv


def _adamw(w, g, m, v, name):
    r, c = w.shape
    tr = _tile(r, 256, unit=8)

    def body(w_ref, g_ref, m_ref, v_ref, d_ref, nm_ref, nv_ref):
        _adamw_update(w_ref, g_ref, m_ref, v_ref, d_ref, nm_ref, nv_ref)

    spec = pl.BlockSpec((tr, c), lambda i: (i, 0))
    shape = jax.ShapeDtypeStruct((r, c), F32)
    return _call(body, name, (r // tr,), [spec] * 4, [spec] * 3, [shape] * 3, [w, g, m, v], sem=("parallel",),
                 nbytes=7 * _nbytes((tr, max(c, 128)), F32))


def _adamw_cols(w, g, m, v, name):
    layers, r, c = w.shape
    tc = 256

    def body(w_ref, g_ref, m_ref, v_ref, d_ref, nm_ref, nv_ref):
        _adamw_update(w_ref, g_ref, m_ref, v_ref, d_ref, nm_ref, nv_ref)

    spec = pl.BlockSpec((None, r, tc), lambda l, j: (l, 0, j))
    shape = jax.ShapeDtypeStruct(w.shape, F32)
    return _call(body, name, (layers, c // tc), [spec] * 4, [spec] * 3, [shape] * 3, [w, g, m, v],
                 sem=("parallel", "parallel"), nbytes=7 * _nbytes((r, tc), F32))


def _pad_rows(a, rows):
    return jnp.pad(a, ((0, rows - a.shape[0]), (0, 0)))


def _unpack_in(gathered):
    i = jnp.arange(16)[None, :, None]
    b = jnp.arange(N_DEV)[:, None, None]
    head = jnp.where(i < b, jnp.roll(gathered[:, IN_STRIDE:IN_STRIDE + 16], 1, axis=0), gathered[:, :16])
    nat = jnp.concatenate([head, gathered[:, 16:IN_STRIDE]], axis=1).reshape(N_DEV * IN_STRIDE, D_MODEL)
    qkv = nat[2 * D_MODEL:5 * D_MODEL].reshape(3, N_HEADS, HEAD_DIM, D_MODEL)
    qkv = jnp.transpose(qkv, (1, 0, 2, 3)).reshape(3 * D_MODEL, D_MODEL)
    f = _pad_rows(gathered[N_DEV - 1, IN_STRIDE:IN_STRIDE + N_HEADS], F_WIDTH)
    return jnp.concatenate([nat[:2 * D_MODEL], nat[5 * D_MODEL:7 * D_MODEL], f, qkv], axis=0)


def _pack_grad_in(dw_in_t):
    qkv = dw_in_t[GATED + F_WIDTH:].reshape(N_HEADS, 3, HEAD_DIM, D_MODEL)
    qkv = jnp.transpose(qkv, (1, 0, 2, 3)).reshape(3 * D_MODEL, D_MODEL)
    return jnp.concatenate([dw_in_t[:2 * D_MODEL], qkv, dw_in_t[2 * D_MODEL:GATED],
                            dw_in_t[GATED:GATED + IN_ROWS - N_DEV * IN_STRIDE]], axis=0)


def kernel(x, mix_pre_g, w_in, b_forget, sgu_norm_g, w_spatial, b_spatial, w_out, mix_post_g, ffn_pre_g, w_gate, w_up, w_down, ffn_post_g, loss_target, m_mix_pre_g, m_w_in, m_b_forget, m_sgu_norm_g, m_w_spatial, m_b_spatial, m_w_out, m_mix_post_g, m_ffn_pre_g, m_w_gate, m_w_up, m_w_down, m_ffn_post_g, v_mix_pre_g, v_w_in, v_b_forget, v_sgu_norm_g, v_w_spatial, v_b_spatial, v_w_out, v_mix_post_g, v_ffn_pre_g, v_w_gate, v_w_up, v_w_down, v_ffn_post_g):
    depth = w_in.shape[0]
    assert depth == 2
    t = x.shape[1]
    x0 = x.reshape(t, D_MODEL)
    target = loss_target.reshape(t, D_MODEL)
    sel = (jnp.arange(D_MODEL)[:, None] // HEAD_DIM == jnp.arange(HEAD_DIM)[None, :]).astype(BF16)
    vec = lambda a, l: a[l][None, :]
    zero_row = jnp.zeros((1, D_MODEL), F32)

    me = 4 * lax.axis_index("x") + 2 * lax.axis_index("y") + lax.axis_index("c")
    p_in = [lax.dynamic_update_slice(jnp.zeros((P_IN, D_MODEL), BF16), w_in[l].T.astype(BF16), (me, 0))
            for l in range(depth)]
    p_rest = [[w_out[l].astype(BF16), jnp.stack([w_gate[l].T, w_up[l].T]).astype(BF16), w_down[l].astype(BF16)]
              for l in range(depth)]
    w_in_t = [None] * depth
    w_rest = [None] * depth
    w_in_t[0] = _unpack_in(_exchange(("gather", 0), p_in[0], "gather_w_in_0")[0])

    saved = []
    xl = x0
    h = _rms_fwd(xl, vec(mix_pre_g, 0), "rms_in_0")
    dy = loss = None
    for l in range(depth):
        b_pad = jnp.pad(b_forget[l], (0, HEAD_DIM - N_HEADS))[None, :]
        b_cols = b_spatial[l].T
        proj = _matmul(h, w_in_t[l], "nt", F32, f"proj_{l}")
        c_pad = _forget_cumsum(proj, b_pad, f"forget_cumsum_{l}")
        c_row = c_pad[:, :N_HEADS].T[:, None, :]
        riders = [(("gather", 0), p_rest[l][0]), (("gather", 1), p_rest[l][1]), (("gather", 0), p_rest[l][2])]
        riders += [(("gather", 0), p_in[l + 1])] if l + 1 < depth else []
        y_b, e_row, *arrived = _attn_fwd_t(proj, c_pad, c_row, f"attn_fwd_{l}", comm=riders)
        w_rest[l] = (arrived[0].reshape(D_MODEL, D_MODEL), arrived[1].reshape(2 * D_FF, D_MODEL),
                     arrived[2].reshape(D_FF, D_MODEL))
        if l + 1 < depth:
            w_in_t[l + 1] = _unpack_in(arrived[3])
        w_o, w_gu_t, w_d = w_rest[l]
        merged = _mix_fwd(proj, y_b, w_spatial[l], b_cols, vec(sgu_norm_g, l), f"mix_fwd_{l}")
        o, x1, h2 = _matmul_post_norm(merged, w_o, xl, vec(mix_post_g, l), f"out_proj_{l}", g_next=vec(ffn_pre_g, l))
        gate, up, act = _ffn_in(h2, w_gu_t, f"ffn_gu_{l}")
        if l + 1 < depth:
            dn, x_next, h_next = _matmul_post_norm(act, w_d, x1, vec(ffn_post_g, l), f"ffn_down_{l}",
                                                   g_next=vec(mix_pre_g, l + 1))
        else:
            dn, dy, loss = _matmul_post_norm(act, w_d, x1, vec(ffn_post_g, l), f"ffn_down_{l}", target=target)
        saved.append(dict(x0=xl, h=h, proj=proj, b_pad=b_pad, b_cols=b_cols, c_pad=c_pad, y_b=y_b, e_row=e_row,
                          merged=merged, o=o, x1=x1, h2=h2, gate=gate, up=up, act=act, dn=dn))
        if l + 1 < depth:
            xl, h = x_next, h_next

    g_small = [None] * depth
    flights = {}
    late_token = None
    dx = dy
    dd, dg_ffn_post = _norm_bwd(dx, f"bwd_ffn_post_{depth - 1}", post=(saved[-1]["dn"], vec(ffn_post_g, depth - 1)))
    for l in reversed(range(depth)):
        s = saved[l]
        w_o, w_gu_t, w_d = w_rest[l]
        dgu = _ffn_in_bwd(dd, w_d, s["gate"], s["up"], f"d_act_{l}")
        dw_d = _matmul(s["act"], dd, "tn", BF16, f"dw_down_{l}")
        dh2 = _matmul(dgu, w_gu_t, "nn", F32, f"d_h2_{l}", halves=True)
        dw_gu_t = _matmul(dgu, s["h2"], "tn", BF16, f"dw_gu_{l}", halves=True)
        dx1, dg_ffn_pre, do, dg_mix_post = _norm_bwd(
            dx, f"bwd_post_mix_{l}", pre=(dh2, s["x1"], vec(ffn_pre_g, l)), post=(s["o"], vec(mix_post_g, l)))
        dmerged = _matmul(do, w_o, "nt", F32, f"d_merged_{l}")
        dw_o = _matmul(s["merged"], do, "tn", BF16, f"dw_out_{l}")
        tokens = []
        for key, dw, base, rows in [("out", dw_o, 0, SH_OUT), ("gate", dw_gu_t, 0, SH_FF), ("up", dw_gu_t, D_FF, SH_FF),
                                    ("down", dw_d, 0, SH_FF)]:
            flights[(key, l)], token = _exchange_start(("windows", base, rows, rows), dw, f"start_grad_{key}_{l}")
            tokens.append(token)
        dproj, dyb, delta_b, dw_s, dbt, dg_v = _mix_bwd(
            s["proj"], s["y_b"], dmerged, w_spatial[l], s["b_cols"], vec(sgu_norm_g, l), sel, f"mix_bwd_{l}",
            deps=tokens)
        db_s = dbt[:, :N_GROUPS].T.reshape(1, D_MODEL)
        tokens = []
        if l == 0:
            g_small[0] = jnp.concatenate([zero_row, dg_v, dg_mix_post, dg_ffn_pre, dg_ffn_post, db_s, zero_row,
                                          zero_row, dw_s.reshape(CHUNK, D_MODEL)], axis=0)
            flights["small"], token = _exchange_start(("spread",),jnp.concatenate([g_small[1], g_small[0]], axis=0),
                                                      "start_small_grads")
            tokens = [token]
        dproj, dc_key, dc_query = _attn_bwd_t(s["proj"], dyb, s["e_row"], delta_b, s["c_pad"], dproj,
                                              f"attn_bwd_{l}", deps=tokens)
        dc = jnp.pad((dc_key + dc_query)[:, 0, :].T, ((0, 0), (0, HEAD_DIM - N_HEADS)))
        dproj, db_f = _forget_bwd_t(s["proj"], s["b_pad"], dc, dproj, f"forget_bwd_{l}")
        db_f_row = jnp.pad(db_f, ((0, 0), (0, D_MODEL - HEAD_DIM)))
        dw_in_t = _matmul(dproj, s["h"], "tn", BF16, f"dw_in_{l}")
        flights[("in", l)], token = _exchange_start(("windows", 0, IN_STRIDE, P_IN), _pack_grad_in(dw_in_t),
                                                    f"start_grad_in_{l}")
        dh = _matmul(dproj, w_in_t[l], "nn", F32, f"d_h_{l}", deps=[token])
        if l > 0:
            dx, dg_mix_pre, dd, dg_ffn_post_below = _norm_bwd(
                dx1, f"bwd_pre_mix_{l}", pre=(dh, s["x0"], vec(mix_pre_g, l)),
                post=(saved[l - 1]["dn"], vec(ffn_post_g, l - 1)))
            loss_row = jnp.pad(loss, ((0, 0), (0, D_MODEL - 1)))
            g_small[l] = jnp.concatenate([dg_mix_pre, dg_v, dg_mix_post, dg_ffn_pre, dg_ffn_post, db_s, db_f_row,
                                          loss_row, dw_s.reshape(CHUNK, D_MODEL)], axis=0)
            dg_ffn_post = dg_ffn_post_below
        else:
            dx, dg_mix_pre = _norm_bwd(dx1, f"bwd_pre_mix_{l}", pre=(dh, s["x0"], vec(mix_pre_g, l)))
            late_rows = jnp.concatenate([dg_mix_pre, db_f_row] + [zero_row] * 6, axis=0)
            flights["late"], late_token = _exchange_start(("spread",),late_rows, "start_late_small_grads")
    grad_x = dx.reshape(x.shape)

    landed, after = {}, [grad_x, late_token]
    rest = ["out", "gate", "up", "down"]
    for key in [(n, 1) for n in rest] + [("in", 1)] + [(n, 0) for n in rest] + ["small"]:
        label = key if isinstance(key, str) else f"grad_{key[0]}_{key[1]}"
        landed[key] = _exchange_wait(flights[key], after, f"wait_{label}")
        after = [landed[key]]
    small_sum = _sum_slots(landed["small"], S_ROWS, "sum_small_grads")
    sm = small_sum.reshape(depth, S_ROWS, D_MODEL)[::-1]
    loss = sm[depth - 1, 7, 0]
    g = {
        "sgu_norm_g": sm[:, 1], "mix_post_g": sm[:, 2],
        "ffn_pre_g": sm[:, 3], "ffn_post_g": sm[:, 4], "b_spatial": sm[:, 5].reshape(depth, N_GROUPS, CHUNK),
        "w_spatial": sm[:, 8:].reshape(depth, N_GROUPS, CHUNK, CHUNK),
    }
    for n in rest:
        g["w_" + n] = jnp.stack([_sum_slots(landed[(n, l)], landed[(n, l)].shape[1], f"sum_grad_{n}_{l}")
                                 for l in range(depth)])

    names = ["mix_pre_g", "w_in", "b_forget", "sgu_norm_g", "w_spatial", "b_spatial", "w_out", "mix_post_g",
             "ffn_pre_g", "w_gate", "w_up", "w_down", "ffn_post_g"]
    ws = dict(mix_pre_g=mix_pre_g, w_in=w_in, b_forget=b_forget, sgu_norm_g=sgu_norm_g, w_spatial=w_spatial,
              b_spatial=b_spatial, w_out=w_out, mix_post_g=mix_post_g, ffn_pre_g=ffn_pre_g, w_gate=w_gate, w_up=w_up,
              w_down=w_down, ffn_post_g=ffn_post_g)
    ms = dict(mix_pre_g=m_mix_pre_g, w_in=m_w_in, b_forget=m_b_forget, sgu_norm_g=m_sgu_norm_g, w_spatial=m_w_spatial,
              b_spatial=m_b_spatial, w_out=m_w_out, mix_post_g=m_mix_post_g, ffn_pre_g=m_ffn_pre_g, w_gate=m_w_gate,
              w_up=m_w_up, w_down=m_w_down, ffn_post_g=m_ffn_post_g)
    vs = dict(mix_pre_g=v_mix_pre_g, w_in=v_w_in, b_forget=v_b_forget, sgu_norm_g=v_sgu_norm_g, w_spatial=v_w_spatial,
              b_spatial=v_b_spatial, w_out=v_w_out, mix_post_g=v_mix_post_g, ffn_pre_g=v_ffn_pre_g, w_gate=v_w_gate,
              w_up=v_w_up, w_down=v_w_down, ffn_post_g=v_ffn_post_g)
    updates = {}
    last = ["w_in", "mix_pre_g", "b_forget"]
    transposed = ["w_in", "w_gate", "w_up"]
    for n in [n for n in names if n not in last] + last:
        if n == last[0]:
            sum_in_1 = _sum_slots(landed[("in", 1)], 304, "sum_grad_in_1")
            done = [u[0] for u in updates.values()] + [sum_in_1]
            sum_in_0 = _sum_slots(_exchange_wait(flights[("in", 0)], done, "wait_grad_in_0"), 304, "sum_grad_in_0")
            late = _sum_slots(_exchange_wait(flights["late"], [sum_in_0], "wait_late"), 8, "sum_late_small_grads")
            g["w_in"] = lax.dynamic_slice(jnp.stack([sum_in_0, sum_in_1]), (0, me, 0), (depth, SH_IN, D_MODEL))
            g["mix_pre_g"] = jnp.stack([late[0], sm[1, 0]])
            g["b_forget"] = jnp.stack([late[1, :N_HEADS], sm[1, 6, :N_HEADS]])
        if n in transposed:
            view = lambda a: jnp.swapaxes(a, 1, 2)
            updates[n] = [view(u) for u in _adamw_cols(view(ws[n]), g[n], view(ms[n]), view(vs[n]), f"adamw_{n}")]
            g[n] = view(g[n])
        else:
            flat = (-1, ws[n].shape[-1])
            updates[n] = [u.reshape(ws[n].shape) for u in
                          _adamw(ws[n].reshape(flat), g[n].reshape(flat), ms[n].reshape(flat), vs[n].reshape(flat),
                                 f"adamw_{n}")]
    deltas = [updates[n][0] for n in names]
    new_m = [updates[n][1] for n in names]
    new_v = [updates[n][2] for n in names]
    grads = [g[n].reshape(ws[n].shape) for n in names]
    return (loss, grad_x, *grads, *deltas, *new_m, *new_v)
```

```python
import math

import jax
import jax.numpy as jnp
from jax import lax
from jax.experimental import pallas as pl
from jax.experimental.pallas import tpu as pltpu

F32 = jnp.float32
BF16 = jnp.bfloat16

N_DEV = 8
D_MODEL = 1024
N_HEADS = 8
HEAD_DIM = 128
CHUNK = 128
N_GROUPS = 8
D_FF = 2816
IN_WIDTH = 7 * D_MODEL + N_HEADS
IN_PAD = 7680
EPS = 1e-6
GATED = 4 * D_MODEL
F_WIDTH = 512
QKV_WIDTH = 3 * HEAD_DIM
CB_F = GATED // HEAD_DIM
CB_QKV = (GATED + F_WIDTH) // HEAD_DIM
assert GATED + F_WIDTH + N_HEADS * QKV_WIDTH == IN_PAD and (GATED + F_WIDTH) % QKV_WIDTH == 0

ADAM_LR, ADAM_B1, ADAM_B2, ADAM_EPS, ADAM_WD, ADAM_STEP = 0.001, 0.9, 0.999, 1e-08, 0.01, 10

SH_IN = IN_WIDTH // N_DEV
SH_OUT = D_MODEL // N_DEV
SH_FF = D_FF // N_DEV
P_IN = 912
IN_STRIDE = 7 * HEAD_DIM
IN_ROWS = IN_STRIDE * (N_DEV - 1) + P_IN
S_ROWS = 136

ATTN_BLOCK = 1024
ROW_BLOCK = 512
MIX_ROWS = 512
MIX_BWD_ROWS = 256
MM_TM, MM_TN, MM_TK = 1536, 1536, 2048
VMEM_CAP = 56 << 20
NEG = -1e30


def _tile(n, cap, unit=128):
    if n <= cap:
        return n
    best = None
    for t in range(unit, cap + 1, unit):
        if n % t == 0:
            best = t
    assert best is not None, (n, cap)
    return best


def _nbytes(shape, dtype):
    return math.prod(shape) * jnp.dtype(dtype).itemsize


_HBM = pl.BlockSpec(memory_space=pltpu.HBM)
MESH = pl.DeviceIdType.MESH

_N_REMOTE = N_DEV - 1


def _exchange_shapes(kind, x):
    if kind[0] == "gather":
        shape = x.shape[:kind[1]] + (N_DEV,) + x.shape[kind[1]:]
    elif kind[0] == "spread":
        shape = (N_DEV,) + x.shape
    else:
        shape = (N_DEV, kind[3]) + x.shape[1:]
    return [jax.ShapeDtypeStruct(shape, x.dtype)]


def _exchange_sems(kind):
    del kind
    return [pltpu.SemaphoreType.DMA((_N_REMOTE,)), pltpu.SemaphoreType.DMA((_N_REMOTE,)), pltpu.SemaphoreType.DMA]


def _plan(kind, x_ref, outs, sems):
    send_sems, recv_sems, local_sem = sems
    x, y, c = lax.axis_index("x"), lax.axis_index("y"), lax.axis_index("c")

    def remote(src, dst, k, to):
        return pltpu.make_async_remote_copy(src_ref=src, dst_ref=dst, send_sem=send_sems.at[k], recv_sem=recv_sems.at[k],
                                            device_id=to, device_id_type=MESH)

    sibling = (x, y, 1 - c)
    if kind[0] == "gather":
        out, = outs
        slot = lambda px, py, pc: out.at[(slice(None),) * kind[1] + (4 * px + 2 * py + pc,)]
        chips = [(1 - x, y), (x, 1 - y), (1 - x, 1 - y)]
        local = [pltpu.make_async_copy(x_ref, slot(x, y, c), local_sem)]
        first = [remote(x_ref, slot(x, y, c), 0, sibling)]
        first += [remote(x_ref, slot(x, y, c), 1 + k, (*chip, c)) for k, chip in enumerate(chips)]
        relays = [(remote(x_ref, slot(*chip, c), 1 + k, (*chip, c)), remote(slot(*chip, c), slot(*chip, c), 4 + k, sibling))
                  for k, chip in enumerate(chips)]
        arrivals = [remote(x_ref, slot(x, y, 1 - c), 0, sibling)]
        arrivals += [remote(x_ref, slot(*chip, 1 - c), 4 + k, sibling) for k, chip in enumerate(chips)]
        return local, first, relays, arrivals
    out, = outs
    mine = 4 * x + 2 * y + c
    if kind[0] == "spread":
        src = lambda d: x_ref
    else:
        _, base, stride, length = kind
        src = lambda d: x_ref.at[pl.ds(pl.multiple_of(base + stride * d, 16), length)]
    first, arrivals = [], []
    for j in range(1, N_DEV):
        peer = (1 - x if j & 4 else x, 1 - y if j & 2 else y, 1 - c if j & 1 else c)
        theirs = 4 * peer[0] + 2 * peer[1] + peer[2]
        first.append(remote(src(theirs), out.at[mine], j - 1, peer))
        arrivals.append(remote(src(theirs), out.at[theirs], j - 1, peer))
    local = [] if local_sem is None else [pltpu.make_async_copy(src(mine), out.at[mine], local_sem)]
    return local, first, [], arrivals


def _start(plan):
    local, first, _, _ = plan
    for cp in local + first:
        cp.start()


def _finish(plan):
    local, first, relays, arrivals = plan
    for arrival, onward in relays:
        arrival.wait_recv()
        onward.start()
    for cp in arrivals:
        cp.wait_recv()
    for cp in first + [onward for _, onward in relays]:
        cp.wait_send()
    for cp in local:
        cp.wait()


def _direct_copies(kind, x_ref, land_ref, send_sems, recv_sems):
    _, first, _, arrivals = _plan(kind, x_ref, [land_ref], (send_sems, recv_sems, None))
    return first, arrivals


_SEM = pl.BlockSpec(memory_space=pltpu.SEMAPHORE)
_DATAFLOW = pltpu.SideEffectType.DATAFLOW_SIDE_EFFECTING


def _exchange_start(kind, x, name):
    me = 4 * lax.axis_index("x") + 2 * lax.axis_index("y") + lax.axis_index("c")
    own = x if kind[0] == "spread" else lax.dynamic_slice_in_dim(x, kind[1] + kind[2] * me, kind[3], 0)
    shape = _exchange_shapes(kind, x)[0].shape
    land = lax.dynamic_update_slice_in_dim(lax.empty(shape, x.dtype), own[None], me, 0)

    def body(x_ref, land_ref, send_sems, recv_sems, x_thru, land_thru, token):
        del x_thru, land_thru
        for cp in _direct_copies(kind, x_ref, land_ref, send_sems, recv_sems)[0]:
            cp.start()
        token[...] = jnp.zeros_like(token)

    *handle, token = pl.pallas_call(
        body, name=name,
        out_shape=(pltpu.SemaphoreType.DMA((_N_REMOTE,)), pltpu.SemaphoreType.DMA((_N_REMOTE,)),
                   pltpu.HBM(x.shape, x.dtype), pltpu.HBM(land.shape, land.dtype), jax.ShapeDtypeStruct((8, 128), F32)),
        in_specs=(_HBM, _HBM), out_specs=(_SEM, _SEM, _HBM, _HBM, pl.BlockSpec(memory_space=pltpu.VMEM)),
        input_output_aliases={0: 2, 1: 3}, compiler_params=pltpu.CompilerParams(has_side_effects=_DATAFLOW),
    )(pltpu.with_memory_space_constraint(x, pltpu.HBM), pltpu.with_memory_space_constraint(land, pltpu.HBM))
    return (kind, *handle), token


def _exchange_wait(handle, after, name):
    kind, send_sems, recv_sems, x_thru, land_thru = handle

    def body(x_ref, land_ref, send_sems, recv_sems, *rest):
        first, arrivals = _direct_copies(kind, x_ref, land_ref, send_sems, recv_sems)
        for cp in first:
            cp.wait_send()
        for cp in arrivals:
            cp.wait_recv()

    return pl.pallas_call(
        body, name=name,
        out_shape=(pltpu.HBM(x_thru.shape, x_thru.dtype), pltpu.HBM(land_thru.shape, land_thru.dtype)),
        in_specs=(_HBM, _HBM, _SEM, _SEM) + (pl.BlockSpec(memory_space=pl.ANY),) * len(after), out_specs=(_HBM, _HBM),
        input_output_aliases={0: 0, 1: 1}, compiler_params=pltpu.CompilerParams(has_side_effects=_DATAFLOW),
    )(x_thru, land_thru, send_sems, recv_sems, *after)[1]


def _exchange(kind, x, name):
    n_out = len(_exchange_shapes(kind, x))

    def body(x_ref, *refs):
        plan = _plan(kind, x_ref, refs[:n_out], refs[n_out:])
        _start(plan)
        _finish(plan)

    return pl.pallas_call(body, name=name, out_shape=_exchange_shapes(kind, x), in_specs=[_HBM],
                          out_specs=[_HBM] * n_out, scratch_shapes=_exchange_sems(kind))(x)


def _call(body, name, grid, in_specs, out_specs, out_shape, args, scratch=(), sem=None, nbytes=0, aliases=None,
          comm=(), deps=()):
    in_specs, out_specs, out_shape, args, scratch = (list(in_specs), list(out_specs), list(out_shape), list(args),
                                                     list(scratch))
    if deps:
        n_real, n_deps, unordered = len(args), len(deps), body

        def body(*refs):
            unordered(*refs[:n_real], *refs[n_real + n_deps:])

        in_specs += [pl.BlockSpec(memory_space=pl.ANY)] * n_deps
        args += list(deps)
    if comm:
        n_in, n_out, n_scr, n_ops = len(args), len(out_shape), len(scratch), len(comm)
        kinds = [kind for kind, _ in comm]
        shapes = [_exchange_shapes(kind, x) for kind, x in comm]
        inner = body

        def body(*refs):
            ins, refs = refs[:n_in], refs[n_in:]
            cins, refs = refs[:n_ops], refs[n_ops:]
            outs, refs = refs[:n_out], refs[n_out:]
            couts = []
            for sh in shapes:
                couts.append(refs[:len(sh)])
                refs = refs[len(sh):]
            own_scratch, sems = refs[:n_scr], refs[n_scr:]
            first = last = None
            for axis, size in enumerate(grid):
                at_start, at_end = pl.program_id(axis) == 0, pl.program_id(axis) == size - 1
                first = at_start if first is None else first & at_start
                last = at_end if last is None else last & at_end
            plans = [_plan(kinds[o], cins[o], couts[o], sems[3 * o:3 * o + 3]) for o in range(n_ops)]

            @pl.when(first)
            def _():
                for p in plans:
                    _start(p)

            inner(*ins, *outs, *own_scratch)

            @pl.when(last)
            def _():
                for p in plans:
                    _finish(p)

        in_specs += [_HBM] * n_ops
        args += [x for _, x in comm]
        for kind, sh in zip(kinds, shapes):
            out_shape += sh
            out_specs += [_HBM] * len(sh)
            scratch += _exchange_sems(kind)
        sem = ("arbitrary",) * len(grid)
    limit = int(min(max(2 * nbytes + (8 << 20), 32 << 20), VMEM_CAP))
    return pl.pallas_call(
        body, name=name, grid=grid, in_specs=in_specs, out_specs=out_specs, out_shape=out_shape,
        scratch_shapes=scratch, input_output_aliases=aliases or {},
        compiler_params=pltpu.CompilerParams(dimension_semantics=sem, vmem_limit_bytes=limit),
    )(*args)


def _sum_slots(x, tr, name):
    n, r, c = x.shape

    def body(x_ref, o_ref):
        acc = x_ref[0].astype(F32)
        for d in range(1, n):
            acc = acc + x_ref[d].astype(F32)
        o_ref[...] = acc

    return _call(body, name, (r // tr,), [pl.BlockSpec((n, tr, c), lambda i: (0, i, 0))],
                 [pl.BlockSpec((tr, c), lambda i: (i, 0))], [jax.ShapeDtypeStruct((r, c), F32)], [x],
                 sem=("parallel",), nbytes=_nbytes((n, tr, c), x.dtype) + _nbytes((tr, c), F32))[0]


_DIMS = {"nn": (((1,), (0,)), ((), ())), "nt": (((1,), (1,)), ((), ())), "tn": (((0,), (0,)), ((), ()))}


def _matmul(a, b, mode, out_dtype, name, comm=(), deps=(), halves=False):
    a_shape = (a.shape[1], 2 * a.shape[2]) if halves else a.shape
    if mode == "nn":
        (m, k), (k2, n) = a_shape, b.shape
    elif mode == "nt":
        (m, k), (n, k2) = a_shape, b.shape
    else:
        (k, m), (k2, n) = a_shape, b.shape
    assert k == k2, (a.shape, b.shape, mode)
    tm, tn, tk = _tile(m, MM_TM), _tile(n, MM_TN), _tile(k, MM_TK)
    if halves and mode == "nn":
        tk = _tile(k // 2, MM_TK)
    if halves and mode == "tn":
        tm = _tile(m // 2, MM_TM)
    nk = k // tk
    dims = _DIMS[mode]
    if halves:
        per_half = (k // 2) // tk if mode == "nn" else (m // 2) // tm
        a_spec = {"nn": pl.BlockSpec((None, tm, tk), lambda i, j, kk: (kk // per_half, i, kk % per_half)),
                  "tn": pl.BlockSpec((None, tk, tm), lambda i, j, kk: (i // per_half, kk, i % per_half))}[mode]
    else:
        a_spec = {"nn": pl.BlockSpec((tm, tk), lambda i, j, kk: (i, kk)),
                  "nt": pl.BlockSpec((tm, tk), lambda i, j, kk: (i, kk)),
                  "tn": pl.BlockSpec((tk, tm), lambda i, j, kk: (kk, i))}[mode]
    b_spec = {"nn": pl.BlockSpec((tk, tn), lambda i, j, kk: (kk, j)),
              "nt": pl.BlockSpec((tn, tk), lambda i, j, kk: (j, kk)),
              "tn": pl.BlockSpec((tk, tn), lambda i, j, kk: (kk, j))}[mode]

    def partial_product(a_ref, b_ref):
        return lax.dot_general(a_ref[...].astype(BF16), b_ref[...].astype(BF16), dims, preferred_element_type=F32)

    if nk == 1:
        def body(a_ref, b_ref, o_ref):
            o_ref[...] = partial_product(a_ref, b_ref).astype(o_ref.dtype)
        scratch = []
    else:
        def body(a_ref, b_ref, o_ref, acc_ref):
            kk = pl.program_id(2)

            @pl.when(kk == 0)
            def _():
                acc_ref[...] = jnp.zeros_like(acc_ref)

            acc_ref[...] += partial_product(a_ref, b_ref)

            @pl.when(kk == nk - 1)
            def _():
                o_ref[...] = acc_ref[...].astype(o_ref.dtype)
        scratch = [pltpu.VMEM((tm, tn), F32)]

    nbytes = (_nbytes((tm, tk), a.dtype) + _nbytes((tk, tn), b.dtype) + _nbytes((tm, tn), out_dtype)
              + _nbytes((tm, tn), F32))
    res = _call(body, name, (m // tm, n // tn, nk), [a_spec, b_spec],
                [pl.BlockSpec((tm, tn), lambda i, j, kk: (i, j))], [jax.ShapeDtypeStruct((m, n), out_dtype)], [a, b],
                scratch=scratch, sem=("parallel", "parallel", "arbitrary"), nbytes=nbytes, comm=comm, deps=deps)
    return res if comm else res[0]


def _matmul_post_norm(a, b, x, g_post, name, g_next=None, target=None):
    (m, k), (k2, d) = a.shape, b.shape
    assert k == k2 and d == D_MODEL
    tm, tk = _tile(m, ROW_BLOCK), _tile(k, MM_TK)
    nk = k // tk
    with_loss = target is not None

    def body(a_ref, b_ref, x_ref, gp_ref, last_ref, o_ref, first_ref, second_ref, acc_ref):
        i, kk = pl.program_id(0), pl.program_id(1)

        @pl.when(kk == 0)
        def _():
            acc_ref[...] = jnp.zeros_like(acc_ref)

        acc_ref[...] += jnp.dot(a_ref[...], b_ref[...], preferred_element_type=F32)

        @pl.when(kk == nk - 1)
        def _():
            ov = acc_ref[...]
            o_ref[...] = ov
            xn = x_ref[...] + (ov * _rms(ov)) * gp_ref[...]
            if with_loss:
                err = xn - last_ref[...]
                first_ref[...] = err / d

                @pl.when(i == 0)
                def _():
                    second_ref[...] = jnp.zeros_like(second_ref)

                second_ref[...] += 0.5 * jnp.sum(jnp.mean(err * err, axis=-1, keepdims=True), axis=0, keepdims=True)
            else:
                first_ref[...] = xn
                second_ref[...] = ((xn * _rms(xn)) * last_ref[...]).astype(BF16)

    rows = pl.BlockSpec((tm, d), lambda i, kk: (i, 0))
    vec = pl.BlockSpec((1, d), lambda i, kk: (0, 0))
    second = ((pl.BlockSpec((1, 1), lambda i, kk: (0, 0)), jax.ShapeDtypeStruct((1, 1), F32)) if with_loss
              else (rows, jax.ShapeDtypeStruct((m, d), BF16)))
    nbytes = _nbytes((tm, tk), BF16) + _nbytes((tk, d), BF16) + 6 * _nbytes((tm, d), F32)
    return _call(body, name, (m // tm, nk),
                 [pl.BlockSpec((tm, tk), lambda i, kk: (i, kk)), pl.BlockSpec((tk, d), lambda i, kk: (kk, 0)), rows, vec,
                  rows if with_loss else vec],
                 [rows, rows, second[0]],
                 [jax.ShapeDtypeStruct((m, d), F32), jax.ShapeDtypeStruct((m, d), F32), second[1]],
                 [a, b, x, g_post, target if with_loss else g_next],
                 scratch=[pltpu.VMEM((tm, d), F32)], sem=("arbitrary", "arbitrary"), nbytes=nbytes)


def _matmul_norm_bwd(a, b, dskip, xin, g_pre, name, post=None, halves=False, deps=()):
    m, k = (a.shape[1], 2 * a.shape[2]) if halves else a.shape
    (k2, d) = b.shape
    assert k == k2 and d == D_MODEL
    tm = _tile(m, ROW_BLOCK)
    tk = _tile(k // 2 if halves else k, MM_TK)
    nk = k // tk
    n_in = 5 + (2 if post else 0)

    def body(*refs):
        (a_ref, b_ref, dskip_ref, xin_ref, gpre_ref), rest = refs[:5], list(refs[5:n_in])
        outs, acc_ref = list(refs[n_in:-1]), refs[-1]
        i, kk = pl.program_id(0), pl.program_id(1)

        @pl.when(kk == 0)
        def _():
            acc_ref[...] = jnp.zeros_like(acc_ref)

        acc_ref[...] += jnp.dot(a_ref[...], b_ref[...], preferred_element_type=F32)

        @pl.when(kk == nk - 1)
        def _():
            dxin, dg_rows = _rms_bwd(acc_ref[...], xin_ref[...], gpre_ref[...])
            dx = dskip_ref[...] + dxin
            dx_ref, dgpre_ref = outs[0], outs[1]
            dx_ref[...] = dx

            @pl.when(i == 0)
            def _():
                for ref in outs[1::2]:
                    ref[...] = jnp.zeros_like(ref)

            dgpre_ref[...] += jnp.sum(dg_rows, axis=0, keepdims=True)
            if post:
                da, dg_rows = _rms_bwd(dx, rest[0][...], rest[1][...])
                outs[2][...] = da.astype(BF16)
                outs[3][...] += jnp.sum(dg_rows, axis=0, keepdims=True)

    if halves:
        per_half = (k // 2) // tk
        a_spec = pl.BlockSpec((None, tm, tk), lambda i, kk: (kk // per_half, i, kk % per_half))
    else:
        a_spec = pl.BlockSpec((tm, tk), lambda i, kk: (i, kk))
    rows = pl.BlockSpec((tm, d), lambda i, kk: (i, 0))
    vec = pl.BlockSpec((1, d), lambda i, kk: (0, 0))
    args, in_specs = [a, b, dskip, xin, g_pre], [a_spec, pl.BlockSpec((tk, d), lambda i, kk: (kk, 0)), rows, rows, vec]
    out_specs, out_shape = [rows, vec], [jax.ShapeDtypeStruct((m, d), F32), jax.ShapeDtypeStruct((1, d), F32)]
    if post:
        args += list(post)
        in_specs += [rows, vec]
        out_specs += [rows, vec]
        out_shape += [jax.ShapeDtypeStruct((m, d), BF16), jax.ShapeDtypeStruct((1, d), F32)]
    nbytes = _nbytes((tm, tk), BF16) + _nbytes((tk, d), BF16) + 7 * _nbytes((tm, d), F32)
    return _call(body, name, (m // tm, nk), in_specs, out_specs, out_shape, args, scratch=[pltpu.VMEM((tm, d), F32)],
                 sem=("arbitrary", "arbitrary"), nbytes=nbytes, deps=deps)


def _rms(x):
    return lax.rsqrt(jnp.mean(x * x, axis=-1, keepdims=True) + EPS)


def _rms_bwd(dz, a, g):
    r = _rms(a)
    dzg = dz * g
    da = r * dzg - a * (r * r * r) * jnp.mean(dzg * a, axis=-1, keepdims=True)
    return da, dz * (a * r)


def _row_spec(tr, width):
    return pl.BlockSpec((tr, width), lambda i: (i, 0))


def _vec_spec(width):
    return pl.BlockSpec((1, width), lambda i: (0, 0))


def _rms_fwd(x, g, name):
    t, d = x.shape
    tr = min(ROW_BLOCK, t)

    def body(x_ref, g_ref, h_ref):
        xv = x_ref[...]
        h_ref[...] = ((xv * _rms(xv)) * g_ref[...]).astype(BF16)

    return _call(body, name, (t // tr,), [_row_spec(tr, d), _vec_spec(d)], [_row_spec(tr, d)],
                 [jax.ShapeDtypeStruct((t, d), BF16)], [x, g], sem=("parallel",), nbytes=3 * _nbytes((tr, d), F32))[0]


def _norm_bwd(dskip, name, pre=None, post=None):
    t, d = dskip.shape
    tr = min(ROW_BLOCK, t)
    n_in = 1 + (3 if pre else 0) + (2 if post else 0)

    def body(*refs):
        ins, outs = list(refs[:n_in]), list(refs[n_in:])
        first = pl.program_id(0) == 0
        dx = ins.pop(0)[...]
        if pre:
            dh_ref, xin_ref, gpre_ref = ins.pop(0), ins.pop(0), ins.pop(0)
            dxin, dg_rows = _rms_bwd(dh_ref[...].astype(F32), xin_ref[...], gpre_ref[...])
            dx = dx + dxin
            dx_ref, dgpre_ref = outs.pop(0), outs.pop(0)
            dx_ref[...] = dx

            @pl.when(first)
            def _():
                dgpre_ref[...] = jnp.zeros_like(dgpre_ref)

            dgpre_ref[...] += jnp.sum(dg_rows, axis=0, keepdims=True)
        if post:
            a_ref, gpost_ref = ins.pop(0), ins.pop(0)
            da, dg_rows = _rms_bwd(dx, a_ref[...], gpost_ref[...])
            da_ref, dgpost_ref = outs.pop(0), outs.pop(0)
            da_ref[...] = da.astype(BF16)

            @pl.when(first)
            def _():
                dgpost_ref[...] = jnp.zeros_like(dgpost_ref)

            dgpost_ref[...] += jnp.sum(dg_rows, axis=0, keepdims=True)

    args, in_specs, out_shape, out_specs = [dskip], [_row_spec(tr, d)], [], []
    if pre:
        args += list(pre)
        in_specs += [_row_spec(tr, d), _row_spec(tr, d), _vec_spec(d)]
        out_shape += [jax.ShapeDtypeStruct((t, d), F32), jax.ShapeDtypeStruct((1, d), F32)]
        out_specs += [_row_spec(tr, d), _vec_spec(d)]
    if post:
        args += list(post)
        in_specs += [_row_spec(tr, d), _vec_spec(d)]
        out_shape += [jax.ShapeDtypeStruct((t, d), BF16), jax.ShapeDtypeStruct((1, d), F32)]
        out_specs += [_row_spec(tr, d), _vec_spec(d)]
    return _call(body, name, (t // tr,), in_specs, out_specs, out_shape, args, sem=("arbitrary",),
                 nbytes=7 * _nbytes((tr, d), F32))


def _sigmoid(x):
    return 1.0 / (1.0 + jnp.exp(-x))


def _ffn_in(h, w_gu_t, name):
    t, d = h.shape
    tm, tn = _tile(t, 1024), _tile(D_FF, MM_TN)
    n_tiles = D_FF // tn

    def body(h_ref, wg_ref, wu_ref, g_ref, u_ref, a_ref):
        hv = h_ref[...]
        gate = lax.dot_general(hv, wg_ref[...], _DIMS["nt"], preferred_element_type=F32).astype(BF16)
        up = lax.dot_general(hv, wu_ref[...], _DIMS["nt"], preferred_element_type=F32).astype(BF16)
        g_ref[...] = gate
        u_ref[...] = up
        gate, up = gate.astype(F32), up.astype(F32)
        a_ref[...] = ((gate * _sigmoid(gate)) * up).astype(BF16)

    tile = pl.BlockSpec((tm, tn), lambda i, j: (i, j))
    shape = jax.ShapeDtypeStruct((t, D_FF), BF16)
    return _call(body, name, (t // tm, n_tiles),
                 [pl.BlockSpec((tm, d), lambda i, j: (i, 0)), pl.BlockSpec((tn, d), lambda i, j: (j, 0)),
                  pl.BlockSpec((tn, d), lambda i, j: (j + n_tiles, 0))],
                 [tile, tile, tile], [shape, shape, shape], [h, w_gu_t, w_gu_t], sem=("parallel", "parallel"),
                 nbytes=_nbytes((tm, d), BF16) + 2 * _nbytes((tn, d), BF16) + 5 * _nbytes((tm, tn), F32))


def _ffn_in_bwd(dd, w_d, gate, up, name):
    t, d = dd.shape
    tm, tn = _tile(t, 1024), _tile(D_FF, MM_TN)

    def body(dd_ref, w_ref, g_ref, u_ref, o_ref):
        da = lax.dot_general(dd_ref[...], w_ref[...], _DIMS["nt"], preferred_element_type=F32)
        gate, up = g_ref[...].astype(F32), u_ref[...].astype(F32)
        s = _sigmoid(gate)
        o_ref[0] = (da * up * (s * (1.0 + gate * (1.0 - s)))).astype(BF16)
        o_ref[1] = (da * (gate * s)).astype(BF16)

    tile = pl.BlockSpec((tm, tn), lambda i, j: (i, j))
    return _call(body, name, (t // tm, D_FF // tn),
                 [pl.BlockSpec((tm, d), lambda i, j: (i, 0)), pl.BlockSpec((tn, d), lambda i, j: (j, 0)), tile, tile],
                 [pl.BlockSpec((2, tm, tn), lambda i, j: (0, i, j))], [jax.ShapeDtypeStruct((2, t, D_FF), BF16)],
                 [dd, w_d, gate, up], sem=("parallel", "parallel"),
                 nbytes=_nbytes((tm, d), BF16) + _nbytes((tn, d), BF16) + 6 * _nbytes((tm, tn), F32))[0]


def _split3(x):
    hi = x.astype(BF16)
    r1 = x - hi.astype(F32)
    mid = r1.astype(BF16)
    lo = (r1 - mid.astype(F32)).astype(BF16)
    return hi, mid, lo


def _dot_exact(mat01, x):
    hi, mid, lo = _split3(x)
    out = jnp.dot(mat01, hi, preferred_element_type=F32)
    out = out + jnp.dot(mat01, mid, preferred_element_type=F32)
    return out + jnp.dot(mat01, lo, preferred_element_type=F32)


def _dot_exact_rhs(x, mat01):
    hi, mid, lo = _split3(x)
    out = jnp.dot(hi, mat01, preferred_element_type=F32)
    out = out + jnp.dot(mid, mat01, preferred_element_type=F32)
    return out + jnp.dot(lo, mat01, preferred_element_type=F32)


def _tri(lower):
    r = lax.broadcasted_iota(jnp.int32, (CHUNK, CHUNK), 0)
    c = lax.broadcasted_iota(jnp.int32, (CHUNK, CHUNK), 1)
    return jnp.where(r >= c if lower else r <= c, 1.0, 0.0).astype(BF16)


def _log_sigmoid(z):
    return jnp.minimum(z, 0.0) - jnp.log(1.0 + jnp.exp(-jnp.abs(z)))


def _forget_cumsum(proj, b_pad, name):
    t = proj.shape[0]
    nb = t // CHUNK

    def body(f_ref, b_ref, c_ref):
        tri = _tri(True)
        b = b_ref[...]

        def blk(i, carry):
            rows = pl.ds(pl.multiple_of(i * CHUNK, CHUNK), CHUNK)
            cs = _dot_exact(tri, _log_sigmoid(f_ref[rows, :] + b)) + carry
            c_ref[rows, :] = cs
            return cs[CHUNK - 1:CHUNK, :]

        lax.fori_loop(0, nb, blk, jnp.zeros((1, HEAD_DIM), F32))

    return _call(body, name, (1,),
                 [pl.BlockSpec((t, HEAD_DIM), lambda i: (0, CB_F)), pl.BlockSpec((1, HEAD_DIM), lambda i: (0, 0))],
                 [pl.BlockSpec((t, HEAD_DIM), lambda i: (0, 0))], [jax.ShapeDtypeStruct((t, HEAD_DIM), F32)],
                 [proj, b_pad], sem=("arbitrary",), nbytes=2 * _nbytes((t, HEAD_DIM), F32))[0]


def _causal(block):
    r = lax.broadcasted_iota(jnp.int32, (block, block), 0)
    c = lax.broadcasted_iota(jnp.int32, (block, block), 1)
    return c <= r


def _lanes(x, width):
    return jnp.concatenate([x] * (width // HEAD_DIM), axis=1)


def _chunks(t, blk):
    return [slice(r * blk, (r + 1) * blk) for r in range(t // blk)]


def _lane_of_head(x, h):
    lane = lax.broadcasted_iota(jnp.int32, x.shape, 1)
    return jnp.broadcast_to(jnp.sum(jnp.where(lane == h, x, 0.0), axis=1, keepdims=True), x.shape)


def _causal_t(block):
    r = lax.broadcasted_iota(jnp.int32, (block, block), 0)
    c = lax.broadcasted_iota(jnp.int32, (block, block), 1)
    return r <= c


def _attn_fwd_t(proj, c_pad, c_row, name, comm=()):
    t = proj.shape[0]
    blk = min(ATTN_BLOCK, t)
    nq = t // blk
    scale = HEAD_DIM ** -0.5

    def body(q_ref, k_ref, v_ref, cp_ref, cr_ref, y_ref, e_ref, kb_s, vt_s, cb_s):
        h, i = pl.program_id(0), pl.program_id(1)

        @pl.when(i == 0)
        def _():
            for rows in _chunks(t, blk):
                kb_s[rows, :] = k_ref[rows, :].astype(BF16)
                vt_s[:, rows] = v_ref[rows, :].T.astype(BF16)
                cb_s[rows, :] = _lane_of_head(cp_ref[rows, :], h)

        q = (q_ref[...] * scale).astype(BF16)
        ci = cr_ref[:, pl.ds(pl.multiple_of(i * blk, blk), blk)]

        def step(j, carry, diagonal):
            m, l, acc = carry
            rows = pl.ds(pl.multiple_of(j * blk, blk), blk)
            s = lax.dot_general(kb_s[rows, :], q, _DIMS["nt"], preferred_element_type=F32)
            s = s + ci - _lanes(cb_s[rows, :], blk)
            if diagonal:
                s = jnp.where(_causal_t(blk), s, NEG)
            m_new = jnp.maximum(m, jnp.max(s, axis=0, keepdims=True))
            alpha = jnp.exp(m - m_new)
            p = jnp.exp(s - m_new)
            l = alpha * l + jnp.sum(p, axis=0, keepdims=True)
            acc = alpha * acc + jnp.dot(vt_s[:, rows], p.astype(BF16), preferred_element_type=F32)
            return m_new, l, acc

        init = (jnp.full((1, blk), NEG, F32), jnp.zeros((1, blk), F32), jnp.zeros((HEAD_DIM, blk), F32))
        carry = lax.fori_loop(0, i, lambda j, c: step(j, c, False), init)
        m, l, acc = step(i, carry, True)
        y_ref[...] = (acc / l).T
        e_ref[...] = ci - (m + jnp.log(l))

    nbytes = 3 * _nbytes((t, HEAD_DIM), F32) + 4 * _nbytes((blk, HEAD_DIM), F32) + 4 * _nbytes((blk, blk), F32)
    return _call(body, name, (N_HEADS, nq),
                 [pl.BlockSpec((blk, HEAD_DIM), lambda h, i: (i, CB_QKV + 3 * h)),
                  pl.BlockSpec((t, HEAD_DIM), lambda h, i: (0, CB_QKV + 3 * h + 1)),
                  pl.BlockSpec((t, HEAD_DIM), lambda h, i: (0, CB_QKV + 3 * h + 2)),
                  pl.BlockSpec((t, HEAD_DIM), lambda h, i: (0, 0)),
                  pl.BlockSpec((None, 1, t), lambda h, i: (h, 0, 0))],
                 [pl.BlockSpec((blk, HEAD_DIM), lambda h, i: (i, h)),
                  pl.BlockSpec((None, 1, blk), lambda h, i: (h, 0, i))],
                 [jax.ShapeDtypeStruct((t, D_MODEL), F32), jax.ShapeDtypeStruct((N_HEADS, 1, t), F32)],
                 [proj, proj, proj, c_pad, c_row],
                 scratch=[pltpu.VMEM((t, HEAD_DIM), BF16), pltpu.VMEM((HEAD_DIM, t), BF16),
                          pltpu.VMEM((t, HEAD_DIM), F32)],
                 sem=("parallel", "arbitrary"), nbytes=nbytes, comm=comm)


def _attn_bwd_t(proj, dy, e_row, delta_b, c_pad, dproj, name, deps=()):
    t = proj.shape[0]
    blk = min(ATTN_BLOCK, t)
    nb = t // blk
    scale = HEAD_DIM ** -0.5

    def body(q_ref, k_ref, v_ref, dy_ref, e_ref, dl_ref, cp_ref, dproj_in, dqkv_ref, dck_ref, dcq_ref,
             qb_s, qt_s, dyb_s, dyt_s, dl_s, dqt_acc):
        del dproj_in
        h, j = pl.program_id(0), pl.program_id(1)

        @pl.when(j == 0)
        def _():
            for rows in _chunks(t, blk):
                qs = q_ref[rows, :] * scale
                qb_s[rows, :] = qs.astype(BF16)
                qt_s[:, rows] = qs.T.astype(BF16)
                dyr = dy_ref[rows, :]
                dyb_s[rows, :] = dyr.astype(BF16)
                dyt_s[:, rows] = dyr.T.astype(BF16)
                dl_s[:, rows] = dl_ref[rows, :].T[0:1, :]
            dqt_acc[...] = jnp.zeros_like(dqt_acc)
            dcq_ref[...] = jnp.zeros_like(dcq_ref)

        kf = k_ref[...]
        kb, kt = kf.astype(BF16), kf.T.astype(BF16)
        vb = v_ref[...].astype(BF16)
        cj = _lanes(_lane_of_head(cp_ref[...], h), blk)

        def step(i, carry, diagonal):
            dkt, dvt, dkey = carry
            cols = pl.ds(pl.multiple_of(i * blk, blk), blk)
            s = lax.dot_general(kb, qb_s[cols, :], _DIMS["nt"], preferred_element_type=F32) + e_ref[:, cols] - cj
            if diagonal:
                s = jnp.where(_causal_t(blk), s, NEG)
            p = jnp.exp(s)
            dp = lax.dot_general(vb, dyb_s[cols, :], _DIMS["nt"], preferred_element_type=F32)
            ds = p * (dp - dl_s[:, cols])
            pb, dsb = p.astype(BF16), ds.astype(BF16)
            dvt = dvt + lax.dot_general(dyt_s[:, cols], pb, _DIMS["nt"], preferred_element_type=F32)
            dkt = dkt + lax.dot_general(qt_s[:, cols], dsb, _DIMS["nt"], preferred_element_type=F32)
            dqt_acc[:, cols] += jnp.dot(kt, dsb, preferred_element_type=F32) * scale
            dcq_ref[:, cols] += jnp.sum(ds, axis=0, keepdims=True)
            for k in range(blk // HEAD_DIM):
                dkey = dkey + ds[:, k * HEAD_DIM:(k + 1) * HEAD_DIM]
            return dkt, dvt, dkey

        zeros = jnp.zeros((HEAD_DIM, blk), F32)
        carry = step(j, (zeros, zeros, jnp.zeros((blk, HEAD_DIM), F32)), True)
        dkt, dvt, dkey = lax.fori_loop(j + 1, nb, lambda i, c: step(i, c, False), carry)
        mine = pl.ds(pl.multiple_of(j * blk, blk), blk)
        dqkv_ref[:, :HEAD_DIM] = dqt_acc[:, mine].T.astype(BF16)
        dqkv_ref[:, HEAD_DIM:2 * HEAD_DIM] = dkt.T.astype(BF16)
        dqkv_ref[:, 2 * HEAD_DIM:] = dvt.T.astype(BF16)
        dck_ref[...] = -jnp.sum(dkey.T, axis=0, keepdims=True)

    full = lambda cb: pl.BlockSpec((t, HEAD_DIM), lambda h, j: (0, cb(h)))
    head = lambda h: h
    row = pl.BlockSpec((None, 1, t), lambda h, j: (h, 0, 0))
    nbytes = (8 * _nbytes((t, HEAD_DIM), F32) + 4 * _nbytes((blk, HEAD_DIM), F32) + 6 * _nbytes((blk, blk), F32))
    return _call(body, name, (N_HEADS, nb),
                 [full(lambda h: CB_QKV + 3 * h),
                  pl.BlockSpec((blk, HEAD_DIM), lambda h, j: (j, CB_QKV + 3 * h + 1)),
                  pl.BlockSpec((blk, HEAD_DIM), lambda h, j: (j, CB_QKV + 3 * h + 2)),
                  full(head), row, full(head),
                  pl.BlockSpec((blk, HEAD_DIM), lambda h, j: (j, 0)),
                  pl.BlockSpec(memory_space=pl.ANY)],
                 [pl.BlockSpec((blk, QKV_WIDTH), lambda h, j: (j, (GATED + F_WIDTH) // QKV_WIDTH + h)),
                  pl.BlockSpec((None, 1, blk), lambda h, j: (h, 0, j)),
                  row],
                 [jax.ShapeDtypeStruct(dproj.shape, dproj.dtype), jax.ShapeDtypeStruct((N_HEADS, 1, t), F32),
                  jax.ShapeDtypeStruct((N_HEADS, 1, t), F32)],
                 [proj, proj, proj, dy, e_row, delta_b, c_pad, dproj],
                 scratch=[pltpu.VMEM((t, HEAD_DIM), BF16), pltpu.VMEM((HEAD_DIM, t), BF16),
                          pltpu.VMEM((t, HEAD_DIM), BF16), pltpu.VMEM((HEAD_DIM, t), BF16),
                          pltpu.VMEM((1, t), F32), pltpu.VMEM((HEAD_DIM, t), F32)],
                 sem=("parallel", "arbitrary"), nbytes=nbytes, aliases={7: 0}, deps=deps)


def _forget_bwd_t(proj, b_pad, dc, dproj, name):
    t = proj.shape[0]
    tr = min(ROW_BLOCK, t)
    nb = t // tr
    rev = lambda i: nb - 1 - i

    def body(f_ref, b_ref, dc_ref, dproj_in, df_ref, db_ref, run_ref):
        del dproj_in

        @pl.when(pl.program_id(0) == 0)
        def _():
            run_ref[...] = jnp.zeros_like(run_ref)
            db_ref[...] = jnp.zeros_like(db_ref)

        tri = _tri(False)
        b = b_ref[...]
        lane = lax.broadcasted_iota(jnp.int32, (CHUNK, HEAD_DIM), 1)
        df_ref[...] = jnp.zeros_like(df_ref)
        for n in reversed(range(tr // CHUNK)):
            rows = slice(n * CHUNK, (n + 1) * CHUNK)
            dlogf = _dot_exact(tri, dc_ref[rows, :]) + run_ref[...]
            run_ref[...] = dlogf[0:1, :]
            z = f_ref[rows, :] + b
            e = jnp.exp(-jnp.abs(z))
            sig_neg = jnp.where(z >= 0.0, e, 1.0) / (1.0 + e)
            df = jnp.where(lane < N_HEADS, dlogf * sig_neg, 0.0)
            df_ref[rows, :HEAD_DIM] = df.astype(BF16)
            db_ref[...] += jnp.sum(df, axis=0, keepdims=True)

    return _call(body, name, (nb,),
                 [pl.BlockSpec((tr, HEAD_DIM), lambda i: (rev(i), CB_F)), pl.BlockSpec((1, HEAD_DIM), lambda i: (0, 0)),
                  pl.BlockSpec((tr, HEAD_DIM), lambda i: (rev(i), 0)), pl.BlockSpec(memory_space=pl.ANY)],
                 [pl.BlockSpec((tr, F_WIDTH), lambda i: (rev(i), GATED // F_WIDTH)),
                  pl.BlockSpec((1, HEAD_DIM), lambda i: (0, 0))],
                 [jax.ShapeDtypeStruct(dproj.shape, dproj.dtype), jax.ShapeDtypeStruct((1, HEAD_DIM), F32)],
                 [proj, b_pad, dc, dproj], scratch=[pltpu.VMEM((1, HEAD_DIM), F32)],
                 sem=("arbitrary",), nbytes=4 * _nbytes((tr, HEAD_DIM), F32), aliases={3: 0})


_GELU_K = math.sqrt(2.0 / math.pi)
_GELU_C = 0.044715


def _gelu(x):
    t = jnp.tanh(_GELU_K * (x + _GELU_C * (x * x * x)))
    return 0.5 * x * (1.0 + t), t


def _gelu_grad(x, t):
    return 0.5 * (1.0 + t) + 0.5 * x * (1.0 - t * t) * (_GELU_K * (1.0 + 3.0 * _GELU_C * (x * x)))


def _layernorm_stats(a):
    mu = jnp.mean(a, axis=-1, keepdims=True)
    xc = a - mu
    r = lax.rsqrt(jnp.mean(xc * xc, axis=-1, keepdims=True) + EPS)
    return xc * r, r


def _group(g):
    return slice(g * CHUNK, (g + 1) * CHUNK)


def _fixed(shape):
    return pl.BlockSpec(shape, lambda i: (0,) * len(shape))


def _mix_fwd(proj, y_b, w_s, b_cols, g_v, name):
    t = proj.shape[0]
    step_rows = min(MIX_ROWS, t)

    def body(u_ref, vs_ref, ga_ref, gb_ref, yb_ref, w_ref, b_ref, gv_ref, o_ref):
        mask = _causal(CHUNK)
        w = [jnp.where(mask, w_ref[g], 0.0).astype(BF16) for g in range(N_GROUPS)]
        for rows in _chunks(step_rows, CHUNK):
            a_u, _ = _gelu(u_ref[rows, :])
            a_v, _ = _gelu(vs_ref[rows, :])
            xhat, _ = _layernorm_stats(a_v)
            vn = (xhat * gv_ref[...]).astype(BF16)
            sa, sb = _sigmoid(ga_ref[rows, :]), _sigmoid(gb_ref[rows, :])
            yb = yb_ref[rows, :]
            for g in range(N_GROUPS):
                cols = _group(g)
                mixed = jnp.dot(w[g], vn[:, cols], preferred_element_type=F32) + b_ref[:, g:g + 1]
                o_ref[rows, cols] = (sa[:, cols] * (a_u[:, cols] * mixed) + sb[:, cols] * yb[:, cols]).astype(BF16)

    gated = lambda k: pl.BlockSpec((step_rows, D_MODEL), lambda i: (i, k))
    return _call(body, name, (t // step_rows,),
                 [gated(0), gated(1), gated(2), gated(3), _row_spec(step_rows, D_MODEL),
                  _fixed((N_GROUPS, CHUNK, CHUNK)), _fixed((CHUNK, N_GROUPS)), _fixed((1, D_MODEL))],
                 [_row_spec(step_rows, D_MODEL)], [jax.ShapeDtypeStruct((t, D_MODEL), BF16)],
                 [proj, proj, proj, proj, y_b, w_s, b_cols, g_v], sem=("parallel",),
                 nbytes=16 * _nbytes((step_rows, D_MODEL), F32))[0]


def _mix_bwd(proj, y_b, dmerged, w_s, b_cols, g_v, sel, name, deps=()):
    t = proj.shape[0]
    step_rows = min(MIX_BWD_ROWS, t)

    def body(u_ref, vs_ref, ga_ref, gb_ref, yb_ref, dm_ref, w_ref, b_ref, gv_ref, sel_ref,
             dg_ref, dyb_ref, delta_ref, dw_ref, dbt_ref, dgv_ref):
        @pl.when(pl.program_id(0) == 0)
        def _():
            dw_ref[...] = jnp.zeros_like(dw_ref)
            dbt_ref[...] = jnp.zeros_like(dbt_ref)
            dgv_ref[...] = jnp.zeros_like(dgv_ref)

        gv = gv_ref[...]
        mask = _causal(CHUNK)
        w = [jnp.where(mask, w_ref[g], 0.0).astype(BF16) for g in range(N_GROUPS)]
        for rows in _chunks(step_rows, CHUNK):
            u, vs = u_ref[rows, :], vs_ref[rows, :]
            a_u, t_u = _gelu(u)
            a_v, t_v = _gelu(vs)
            xhat, r = _layernorm_stats(a_v)
            vn = (xhat * gv).astype(BF16)
            sa, sb = _sigmoid(ga_ref[rows, :]), _sigmoid(gb_ref[rows, :])
            yb, dm = yb_ref[rows, :], dm_ref[rows, :].astype(F32)
            dyb = dm * sb
            dyb_ref[rows, :] = dyb
            dyb_yb = dyb * yb
            dg_ref[rows, 3 * D_MODEL:] = (dm * yb * (sb * (1.0 - sb))).astype(BF16)
            dya = dm * sa
            dmixed_parts, dvn_parts = [], []
            for g in range(N_GROUPS):
                cols = _group(g)
                delta_ref[rows, cols] = jnp.broadcast_to(jnp.sum(dyb_yb[:, cols], axis=1, keepdims=True),
                                                         (CHUNK, CHUNK))
                mixed = jnp.dot(w[g], vn[:, cols], preferred_element_type=F32) + b_ref[:, g:g + 1]
                y_a = a_u[:, cols] * mixed
                dg_ref[rows, 2 * D_MODEL + g * CHUNK:2 * D_MODEL + (g + 1) * CHUNK] = (
                    dm[:, cols] * y_a * (sa[:, cols] * (1.0 - sa[:, cols]))).astype(BF16)
                dg_ref[rows, cols] = (dya[:, cols] * mixed * _gelu_grad(u[:, cols], t_u[:, cols])).astype(BF16)
                dmixed = dya[:, cols] * a_u[:, cols]
                dmb = dmixed.astype(BF16)
                dw = lax.dot_general(dmb, vn[:, cols], _DIMS["nt"], preferred_element_type=F32)
                dw_ref[g] += jnp.where(mask, dw, 0.0)
                dvn_parts.append(lax.dot_general(w[g], dmb, _DIMS["tn"], preferred_element_type=F32))
                dmixed_parts.append(dmixed)
            dmixed_all = jnp.concatenate(dmixed_parts, axis=1)
            dvn = jnp.concatenate(dvn_parts, axis=1)
            dbt_ref[...] += _dot_exact_rhs(dmixed_all, sel_ref[...])
            dgv_ref[...] += jnp.sum(dvn * xhat, axis=0, keepdims=True)
            dxh = dvn * gv
            da_v = r * (dxh - jnp.mean(dxh, axis=-1, keepdims=True)
                        - xhat * jnp.mean(dxh * xhat, axis=-1, keepdims=True))
            dg_ref[rows, D_MODEL:2 * D_MODEL] = (da_v * _gelu_grad(vs, t_v)).astype(BF16)

    row = lambda width: _row_spec(step_rows, width)
    gated = lambda k: pl.BlockSpec((step_rows, D_MODEL), lambda i: (i, k))
    return _call(body, name, (t // step_rows,),
                 [gated(0), gated(1), gated(2), gated(3), row(D_MODEL), row(D_MODEL),
                  _fixed((N_GROUPS, CHUNK, CHUNK)), _fixed((CHUNK, N_GROUPS)), _fixed((1, D_MODEL)),
                  _fixed((D_MODEL, HEAD_DIM))],
                 [row(GATED), row(D_MODEL), row(D_MODEL),
                  _fixed((N_GROUPS, CHUNK, CHUNK)), _fixed((CHUNK, HEAD_DIM)), _fixed((1, D_MODEL))],
                 [jax.ShapeDtypeStruct((t, IN_PAD), BF16), jax.ShapeDtypeStruct((t, D_MODEL), F32),
                  jax.ShapeDtypeStruct((t, D_MODEL), F32), jax.ShapeDtypeStruct((N_GROUPS, CHUNK, CHUNK), F32),
                  jax.ShapeDtypeStruct((CHUNK, HEAD_DIM), F32), jax.ShapeDtypeStruct((1, D_MODEL), F32)],
                 [proj, proj, proj, proj, y_b, dmerged, w_s, b_cols, g_v, sel], sem=("arbitrary",),
                 nbytes=24 * _nbytes((step_rows, D_MODEL), F32), deps=deps)


def _adamw_update(w_ref, g_ref, m_ref, v_ref, d_ref, nm_ref, nv_ref):
    gv = g_ref[...]
    nm = ADAM_B1 * m_ref[...] + (1.0 - ADAM_B1) * gv
    nv = ADAM_B2 * v_ref[...] + (1.0 - ADAM_B2) * (gv * gv)
    m_hat = nm / (1.0 - ADAM_B1 ** ADAM_STEP)
    v_hat = nv / (1.0 - ADAM_B2 ** ADAM_STEP)
    d_ref[...] = -ADAM_LR * (m_hat / (jnp.sqrt(v_hat) + ADAM_EPS) + ADAM_WD * w_ref[...])
    nm_ref[...] = nm
    nv_ref[...] = nv


def _adamw(w, g, m, v, name):
    r, c = w.shape
    tr = _tile(r, 256, unit=8)

    def body(w_ref, g_ref, m_ref, v_ref, d_ref, nm_ref, nv_ref):
        _adamw_update(w_ref, g_ref, m_ref, v_ref, d_ref, nm_ref, nv_ref)

    spec = pl.BlockSpec((tr, c), lambda i: (i, 0))
    shape = jax.ShapeDtypeStruct((r, c), F32)
    return _call(body, name, (r // tr,), [spec] * 4, [spec] * 3, [shape] * 3, [w, g, m, v], sem=("parallel",),
                 nbytes=7 * _nbytes((tr, max(c, 128)), F32))


def _adamw_cols(w, g, m, v, name):
    layers, r, c = w.shape
    tc = 256

    def body(w_ref, g_ref, m_ref, v_ref, d_ref, nm_ref, nv_ref):
        _adamw_update(w_ref, g_ref, m_ref, v_ref, d_ref, nm_ref, nv_ref)

    spec = pl.BlockSpec((None, r, tc), lambda l, j: (l, 0, j))
    shape = jax.ShapeDtypeStruct(w.shape, F32)
    return _call(body, name, (layers, c // tc), [spec] * 4, [spec] * 3, [shape] * 3, [w, g, m, v],
                 sem=("parallel", "parallel"), nbytes=7 * _nbytes((r, tc), F32))


def _pad_rows(a, rows):
    return jnp.pad(a, ((0, rows - a.shape[0]), (0, 0)))


def _unpack_in(gathered):
    i = jnp.arange(16)[None, :, None]
    b = jnp.arange(N_DEV)[:, None, None]
    head = jnp.where(i < b, jnp.roll(gathered[:, IN_STRIDE:IN_STRIDE + 16], 1, axis=0), gathered[:, :16])
    nat = jnp.concatenate([head, gathered[:, 16:IN_STRIDE]], axis=1).reshape(N_DEV * IN_STRIDE, D_MODEL)
    qkv = nat[2 * D_MODEL:5 * D_MODEL].reshape(3, N_HEADS, HEAD_DIM, D_MODEL)
    qkv = jnp.transpose(qkv, (1, 0, 2, 3)).reshape(3 * D_MODEL, D_MODEL)
    f = _pad_rows(gathered[N_DEV - 1, IN_STRIDE:IN_STRIDE + N_HEADS], F_WIDTH)
    return jnp.concatenate([nat[:2 * D_MODEL], nat[5 * D_MODEL:7 * D_MODEL], f, qkv], axis=0)


def _pack_grad_in(dw_in_t):
    qkv = dw_in_t[GATED + F_WIDTH:].reshape(N_HEADS, 3, HEAD_DIM, D_MODEL)
    qkv = jnp.transpose(qkv, (1, 0, 2, 3)).reshape(3 * D_MODEL, D_MODEL)
    return jnp.concatenate([dw_in_t[:2 * D_MODEL], qkv, dw_in_t[2 * D_MODEL:GATED],
                            dw_in_t[GATED:GATED + IN_ROWS - N_DEV * IN_STRIDE]], axis=0)


def kernel(x, mix_pre_g, w_in, b_forget, sgu_norm_g, w_spatial, b_spatial, w_out, mix_post_g, ffn_pre_g, w_gate, w_up, w_down, ffn_post_g, loss_target, m_mix_pre_g, m_w_in, m_b_forget, m_sgu_norm_g, m_w_spatial, m_b_spatial, m_w_out, m_mix_post_g, m_ffn_pre_g, m_w_gate, m_w_up, m_w_down, m_ffn_post_g, v_mix_pre_g, v_w_in, v_b_forget, v_sgu_norm_g, v_w_spatial, v_b_spatial, v_w_out, v_mix_post_g, v_ffn_pre_g, v_w_gate, v_w_up, v_w_down, v_ffn_post_g):
    depth = w_in.shape[0]
    assert depth == 2
    t = x.shape[1]
    x0 = x.reshape(t, D_MODEL)
    target = loss_target.reshape(t, D_MODEL)
    sel = (jnp.arange(D_MODEL)[:, None] // HEAD_DIM == jnp.arange(HEAD_DIM)[None, :]).astype(BF16)
    vec = lambda a, l: a[l][None, :]
    zero_row = jnp.zeros((1, D_MODEL), F32)

    me = 4 * lax.axis_index("x") + 2 * lax.axis_index("y") + lax.axis_index("c")
    p_in = [lax.dynamic_update_slice(jnp.zeros((P_IN, D_MODEL), BF16), w_in[l].T.astype(BF16), (me, 0))
            for l in range(depth)]
    p_rest = [[w_out[l].astype(BF16), jnp.stack([w_gate[l].T, w_up[l].T]).astype(BF16), w_down[l].astype(BF16)]
              for l in range(depth)]
    w_in_t = [None] * depth
    w_rest = [None] * depth
    w_in_t[0] = _unpack_in(_exchange(("gather", 0), p_in[0], "gather_w_in_0")[0])

    saved = []
    xl = x0
    h = _rms_fwd(xl, vec(mix_pre_g, 0), "rms_in_0")
    dy = loss = None
    for l in range(depth):
        b_pad = jnp.pad(b_forget[l], (0, HEAD_DIM - N_HEADS))[None, :]
        b_cols = b_spatial[l].T
        proj = _matmul(h, w_in_t[l], "nt", F32, f"proj_{l}")
        c_pad = _forget_cumsum(proj, b_pad, f"forget_cumsum_{l}")
        c_row = c_pad[:, :N_HEADS].T[:, None, :]
        riders = [(("gather", 0), p_rest[l][0]), (("gather", 1), p_rest[l][1]), (("gather", 0), p_rest[l][2])]
        riders += [(("gather", 0), p_in[l + 1])] if l + 1 < depth else []
        y_b, e_row, *arrived = _attn_fwd_t(proj, c_pad, c_row, f"attn_fwd_{l}", comm=riders)
        w_rest[l] = (arrived[0].reshape(D_MODEL, D_MODEL), arrived[1].reshape(2 * D_FF, D_MODEL),
                     arrived[2].reshape(D_FF, D_MODEL))
        if l + 1 < depth:
            w_in_t[l + 1] = _unpack_in(arrived[3])
        w_o, w_gu_t, w_d = w_rest[l]
        merged = _mix_fwd(proj, y_b, w_spatial[l], b_cols, vec(sgu_norm_g, l), f"mix_fwd_{l}")
        o, x1, h2 = _matmul_post_norm(merged, w_o, xl, vec(mix_post_g, l), f"out_proj_{l}", g_next=vec(ffn_pre_g, l))
        gate, up, act = _ffn_in(h2, w_gu_t, f"ffn_gu_{l}")
        if l + 1 < depth:
            dn, x_next, h_next = _matmul_post_norm(act, w_d, x1, vec(ffn_post_g, l), f"ffn_down_{l}",
                                                   g_next=vec(mix_pre_g, l + 1))
        else:
            dn, dy, loss = _matmul_post_norm(act, w_d, x1, vec(ffn_post_g, l), f"ffn_down_{l}", target=target)
        saved.append(dict(x0=xl, h=h, proj=proj, b_pad=b_pad, b_cols=b_cols, c_pad=c_pad, y_b=y_b, e_row=e_row,
                          merged=merged, o=o, x1=x1, h2=h2, gate=gate, up=up, act=act, dn=dn))
        if l + 1 < depth:
            xl, h = x_next, h_next

    g_small = [None] * depth
    flights = {}
    late_token = None
    dx = dy
    dd, dg_ffn_post = _norm_bwd(dx, f"bwd_ffn_post_{depth - 1}", post=(saved[-1]["dn"], vec(ffn_post_g, depth - 1)))
    for l in reversed(range(depth)):
        s = saved[l]
        w_o, w_gu_t, w_d = w_rest[l]
        dgu = _ffn_in_bwd(dd, w_d, s["gate"], s["up"], f"d_act_{l}")
        dw_d = _matmul(s["act"], dd, "tn", BF16, f"dw_down_{l}")
        dw_gu_t = _matmul(dgu, s["h2"], "tn", BF16, f"dw_gu_{l}", halves=True)
        dx1, dg_ffn_pre, do, dg_mix_post = _matmul_norm_bwd(
            dgu, w_gu_t, dx, s["x1"], vec(ffn_pre_g, l), f"d_h2_{l}", post=(s["o"], vec(mix_post_g, l)), halves=True)
        dmerged = _matmul(do, w_o, "nt", F32, f"d_merged_{l}")
        dw_o = _matmul(s["merged"], do, "tn", BF16, f"dw_out_{l}")
        tokens = []
        for key, dw, base, rows in [("out", dw_o, 0, SH_OUT), ("gate", dw_gu_t, 0, SH_FF), ("up", dw_gu_t, D_FF, SH_FF),
                                    ("down", dw_d, 0, SH_FF)]:
            flights[(key, l)], token = _exchange_start(("windows", base, rows, rows), dw, f"start_grad_{key}_{l}")
            tokens.append(token)
        dproj, dyb, delta_b, dw_s, dbt, dg_v = _mix_bwd(
            s["proj"], s["y_b"], dmerged, w_spatial[l], s["b_cols"], vec(sgu_norm_g, l), sel, f"mix_bwd_{l}",
            deps=tokens)
        db_s = dbt[:, :N_GROUPS].T.reshape(1, D_MODEL)
        tokens = []
        if l == 0:
            g_small[0] = jnp.concatenate([zero_row, dg_v, dg_mix_post, dg_ffn_pre, dg_ffn_post, db_s, zero_row,
                                          zero_row, dw_s.reshape(CHUNK, D_MODEL)], axis=0)
            flights["small"], token = _exchange_start(("spread",),jnp.concatenate([g_small[1], g_small[0]], axis=0),
                                                      "start_small_grads")
            tokens = [token]
        dproj, dc_key, dc_query = _attn_bwd_t(s["proj"], dyb, s["e_row"], delta_b, s["c_pad"], dproj,
                                              f"attn_bwd_{l}", deps=tokens)
        dc = jnp.pad((dc_key + dc_query)[:, 0, :].T, ((0, 0), (0, HEAD_DIM - N_HEADS)))
        dproj, db_f = _forget_bwd_t(s["proj"], s["b_pad"], dc, dproj, f"forget_bwd_{l}")
        db_f_row = jnp.pad(db_f, ((0, 0), (0, D_MODEL - HEAD_DIM)))
        dw_in_t = _matmul(dproj, s["h"], "tn", BF16, f"dw_in_{l}")
        flights[("in", l)], token = _exchange_start(("windows", 0, IN_STRIDE, P_IN), _pack_grad_in(dw_in_t),
                                                    f"start_grad_in_{l}")
        if l > 0:
            dx, dg_mix_pre, dd, dg_ffn_post_below = _matmul_norm_bwd(
                dproj, w_in_t[l], dx1, s["x0"], vec(mix_pre_g, l), f"d_h_{l}",
                post=(saved[l - 1]["dn"], vec(ffn_post_g, l - 1)), deps=[token])
            loss_row = jnp.pad(loss, ((0, 0), (0, D_MODEL - 1)))
            g_small[l] = jnp.concatenate([dg_mix_pre, dg_v, dg_mix_post, dg_ffn_pre, dg_ffn_post, db_s, db_f_row,
                                          loss_row, dw_s.reshape(CHUNK, D_MODEL)], axis=0)
            dg_ffn_post = dg_ffn_post_below
        else:
            dx, dg_mix_pre = _matmul_norm_bwd(dproj, w_in_t[l], dx1, s["x0"], vec(mix_pre_g, l), f"d_h_{l}",
                                              deps=[token])
            late_rows = jnp.concatenate([dg_mix_pre, db_f_row] + [zero_row] * 6, axis=0)
            flights["late"], late_token = _exchange_start(("spread",),late_rows, "start_late_small_grads")
    grad_x = dx.reshape(x.shape)

    landed, after = {}, [grad_x, late_token]
    rest = ["out", "gate", "up", "down"]
    for key in [(n, 1) for n in rest] + [("in", 1)] + [(n, 0) for n in rest] + ["small"]:
        label = key if isinstance(key, str) else f"grad_{key[0]}_{key[1]}"
        landed[key] = _exchange_wait(flights[key], after, f"wait_{label}")
        after = [landed[key]]
    small_sum = _sum_slots(landed["small"], S_ROWS, "sum_small_grads")
    sm = small_sum.reshape(depth, S_ROWS, D_MODEL)[::-1]
    loss = sm[depth - 1, 7, 0]
    g = {
        "sgu_norm_g": sm[:, 1], "mix_post_g": sm[:, 2],
        "ffn_pre_g": sm[:, 3], "ffn_post_g": sm[:, 4], "b_spatial": sm[:, 5].reshape(depth, N_GROUPS, CHUNK),
        "w_spatial": sm[:, 8:].reshape(depth, N_GROUPS, CHUNK, CHUNK),
    }
    for n in rest:
        g["w_" + n] = jnp.stack([_sum_slots(landed[(n, l)], landed[(n, l)].shape[1], f"sum_grad_{n}_{l}")
                                 for l in range(depth)])

    names = ["mix_pre_g", "w_in", "b_forget", "sgu_norm_g", "w_spatial", "b_spatial", "w_out", "mix_post_g",
             "ffn_pre_g", "w_gate", "w_up", "w_down", "ffn_post_g"]
    ws = dict(mix_pre_g=mix_pre_g, w_in=w_in, b_forget=b_forget, sgu_norm_g=sgu_norm_g, w_spatial=w_spatial,
              b_spatial=b_spatial, w_out=w_out, mix_post_g=mix_post_g, ffn_pre_g=ffn_pre_g, w_gate=w_gate, w_up=w_up,
              w_down=w_down, ffn_post_g=ffn_post_g)
    ms = dict(mix_pre_g=m_mix_pre_g, w_in=m_w_in, b_forget=m_b_forget, sgu_norm_g=m_sgu_norm_g, w_spatial=m_w_spatial,
              b_spatial=m_b_spatial, w_out=m_w_out, mix_post_g=m_mix_post_g, ffn_pre_g=m_ffn_pre_g, w_gate=m_w_gate,
              w_up=m_w_up, w_down=m_w_down, ffn_post_g=m_ffn_post_g)
    vs = dict(mix_pre_g=v_mix_pre_g, w_in=v_w_in, b_forget=v_b_forget, sgu_norm_g=v_sgu_norm_g, w_spatial=v_w_spatial,
              b_spatial=v_b_spatial, w_out=v_w_out, mix_post_g=v_mix_post_g, ffn_pre_g=v_ffn_pre_g, w_gate=v_w_gate,
              w_up=v_w_up, w_down=v_w_down, ffn_post_g=v_ffn_post_g)
    updates = {}
    last = ["w_in", "mix_pre_g", "b_forget"]
    transposed = ["w_in", "w_gate", "w_up"]
    for n in [n for n in names if n not in last] + last:
        if n == last[0]:
            sum_in_1 = _sum_slots(landed[("in", 1)], 304, "sum_grad_in_1")
            done = [u[0] for u in updates.values()] + [sum_in_1]
            sum_in_0 = _sum_slots(_exchange_wait(flights[("in", 0)], done, "wait_grad_in_0"), 304, "sum_grad_in_0")
            late = _sum_slots(_exchange_wait(flights["late"], [sum_in_0], "wait_late"), 8, "sum_late_small_grads")
            g["w_in"] = lax.dynamic_slice(jnp.stack([sum_in_0, sum_in_1]), (0, me, 0), (depth, SH_IN, D_MODEL))
            g["mix_pre_g"] = jnp.stack([late[0], sm[1, 0]])
            g["b_forget"] = jnp.stack([late[1, :N_HEADS], sm[1, 6, :N_HEADS]])
        if n in transposed:
            view = lambda a: jnp.swapaxes(a, 1, 2)
            updates[n] = [view(u) for u in _adamw_cols(view(ws[n]), g[n], view(ms[n]), view(vs[n]), f"adamw_{n}")]
            g[n] = view(g[n])
        else:
            flat = (-1, ws[n].shape[-1])
            updates[n] = [u.reshape(ws[n].shape) for u in
                          _adamw(ws[n].reshape(flat), g[n].reshape(flat), ms[n].reshape(flat), vs[n].reshape(flat),
                                 f"adamw_{n}")]
    deltas = [updates[n][0] for n in names]
    new_m = [updates[n][1] for n in names]
    new_v = [updates[n][2] for n in names]
    grads = [g[n].reshape(ws[n].shape) for n in names]
    return (loss, grad_x, *grads, *deltas, *new_m, *new_v)
```

```python
import math

import jax
import jax.numpy as jnp
from jax import lax
from jax.experimental import pallas as pl
from jax.experimental.pallas import tpu as pltpu

F32 = jnp.float32
BF16 = jnp.bfloat16

N_DEV = 8
D_MODEL = 1024
N_HEADS = 8
HEAD_DIM = 128
CHUNK = 128
N_GROUPS = 8
D_FF = 2816
IN_WIDTH = 7 * D_MODEL + N_HEADS
IN_PAD = 7680
EPS = 1e-6
GATED = 4 * D_MODEL
F_WIDTH = 512
QKV_WIDTH = 3 * HEAD_DIM
CB_F = GATED // HEAD_DIM
CB_QKV = (GATED + F_WIDTH) // HEAD_DIM
assert GATED + F_WIDTH + N_HEADS * QKV_WIDTH == IN_PAD and (GATED + F_WIDTH) % QKV_WIDTH == 0

ADAM_LR, ADAM_B1, ADAM_B2, ADAM_EPS, ADAM_WD, ADAM_STEP = 0.001, 0.9, 0.999, 1e-08, 0.01, 10

SH_IN = IN_WIDTH // N_DEV
SH_OUT = D_MODEL // N_DEV
SH_FF = D_FF // N_DEV
P_IN = 912
IN_STRIDE = 7 * HEAD_DIM
IN_ROWS = IN_STRIDE * (N_DEV - 1) + P_IN
S_ROWS = 136

ATTN_BLOCK = 1024
ROW_BLOCK = 512
MIX_ROWS = 1024
MIX_BWD_ROWS = 256
MM_TM, MM_TN, MM_TK = 1536, 1536, 2048
VMEM_CAP = 56 << 20
NEG = -1e30


def _tile(n, cap, unit=128):
    if n <= cap:
        return n
    best = None
    for t in range(unit, cap + 1, unit):
        if n % t == 0:
            best = t
    assert best is not None, (n, cap)
    return best


def _nbytes(shape, dtype):
    return math.prod(shape) * jnp.dtype(dtype).itemsize


_HBM = pl.BlockSpec(memory_space=pltpu.HBM)
MESH = pl.DeviceIdType.MESH

_N_REMOTE = N_DEV - 1


def _exchange_shapes(kind, x):
    if kind[0] == "gather":
        shape = x.shape[:kind[1]] + (N_DEV,) + x.shape[kind[1]:]
    elif kind[0] == "spread":
        shape = (N_DEV,) + x.shape
    else:
        shape = (N_DEV, kind[3]) + x.shape[1:]
    return [jax.ShapeDtypeStruct(shape, x.dtype)]


def _exchange_sems(kind):
    del kind
    return [pltpu.SemaphoreType.DMA((_N_REMOTE,)), pltpu.SemaphoreType.DMA((_N_REMOTE,)), pltpu.SemaphoreType.DMA]


def _plan(kind, x_ref, outs, sems):
    send_sems, recv_sems, local_sem = sems
    x, y, c = lax.axis_index("x"), lax.axis_index("y"), lax.axis_index("c")

    def remote(src, dst, k, to):
        return pltpu.make_async_remote_copy(src_ref=src, dst_ref=dst, send_sem=send_sems.at[k], recv_sem=recv_sems.at[k],
                                            device_id=to, device_id_type=MESH)

    sibling = (x, y, 1 - c)
    if kind[0] == "gather":
        out, = outs
        slot = lambda px, py, pc: out.at[(slice(None),) * kind[1] + (4 * px + 2 * py + pc,)]
        chips = [(1 - x, y), (x, 1 - y), (1 - x, 1 - y)]
        local = [pltpu.make_async_copy(x_ref, slot(x, y, c), local_sem)]
        first = [remote(x_ref, slot(x, y, c), 0, sibling)]
        first += [remote(x_ref, slot(x, y, c), 1 + k, (*chip, c)) for k, chip in enumerate(chips)]
        relays = [(remote(x_ref, slot(*chip, c), 1 + k, (*chip, c)), remote(slot(*chip, c), slot(*chip, c), 4 + k, sibling))
                  for k, chip in enumerate(chips)]
        arrivals = [remote(x_ref, slot(x, y, 1 - c), 0, sibling)]
        arrivals += [remote(x_ref, slot(*chip, 1 - c), 4 + k, sibling) for k, chip in enumerate(chips)]
        return local, first, relays, arrivals
    out, = outs
    mine = 4 * x + 2 * y + c
    if kind[0] == "spread":
        src = lambda d: x_ref
    else:
        _, base, stride, length = kind
        src = lambda d: x_ref.at[pl.ds(pl.multiple_of(base + stride * d, 16), length)]
    first, arrivals = [], []
    for j in range(1, N_DEV):
        peer = (1 - x if j & 4 else x, 1 - y if j & 2 else y, 1 - c if j & 1 else c)
        theirs = 4 * peer[0] + 2 * peer[1] + peer[2]
        first.append(remote(src(theirs), out.at[mine], j - 1, peer))
        arrivals.append(remote(src(theirs), out.at[theirs], j - 1, peer))
    local = [] if local_sem is None else [pltpu.make_async_copy(src(mine), out.at[mine], local_sem)]
    return local, first, [], arrivals


def _start(plan):
    local, first, _, _ = plan
    for cp in local + first:
        cp.start()


def _finish(plan):
    local, first, relays, arrivals = plan
    for arrival, onward in relays:
        arrival.wait_recv()
        onward.start()
    for cp in arrivals:
        cp.wait_recv()
    for cp in first + [onward for _, onward in relays]:
        cp.wait_send()
    for cp in local:
        cp.wait()


def _direct_copies(kind, x_ref, land_ref, send_sems, recv_sems):
    _, first, _, arrivals = _plan(kind, x_ref, [land_ref], (send_sems, recv_sems, None))
    return first, arrivals


_SEM = pl.BlockSpec(memory_space=pltpu.SEMAPHORE)
_DATAFLOW = pltpu.SideEffectType.DATAFLOW_SIDE_EFFECTING


def _exchange_start(kind, x, name):
    me = 4 * lax.axis_index("x") + 2 * lax.axis_index("y") + lax.axis_index("c")
    own = x if kind[0] == "spread" else lax.dynamic_slice_in_dim(x, kind[1] + kind[2] * me, kind[3], 0)
    shape = _exchange_shapes(kind, x)[0].shape
    land = lax.dynamic_update_slice_in_dim(lax.empty(shape, x.dtype), own[None], me, 0)

    def body(x_ref, land_ref, send_sems, recv_sems, x_thru, land_thru, token):
        del x_thru, land_thru
        for cp in _direct_copies(kind, x_ref, land_ref, send_sems, recv_sems)[0]:
            cp.start()
        token[...] = jnp.zeros_like(token)

    *handle, token = pl.pallas_call(
        body, name=name,
        out_shape=(pltpu.SemaphoreType.DMA((_N_REMOTE,)), pltpu.SemaphoreType.DMA((_N_REMOTE,)),
                   pltpu.HBM(x.shape, x.dtype), pltpu.HBM(land.shape, land.dtype), jax.ShapeDtypeStruct((8, 128), F32)),
        in_specs=(_HBM, _HBM), out_specs=(_SEM, _SEM, _HBM, _HBM, pl.BlockSpec(memory_space=pltpu.VMEM)),
        input_output_aliases={0: 2, 1: 3}, compiler_params=pltpu.CompilerParams(has_side_effects=_DATAFLOW),
    )(pltpu.with_memory_space_constraint(x, pltpu.HBM), pltpu.with_memory_space_constraint(land, pltpu.HBM))
    return (kind, *handle), token


def _exchange_wait(handle, after, name):
    kind, send_sems, recv_sems, x_thru, land_thru = handle

    def body(x_ref, land_ref, send_sems, recv_sems, *rest):
        first, arrivals = _direct_copies(kind, x_ref, land_ref, send_sems, recv_sems)
        for cp in first:
            cp.wait_send()
        for cp in arrivals:
            cp.wait_recv()

    return pl.pallas_call(
        body, name=name,
        out_shape=(pltpu.HBM(x_thru.shape, x_thru.dtype), pltpu.HBM(land_thru.shape, land_thru.dtype)),
        in_specs=(_HBM, _HBM, _SEM, _SEM) + (pl.BlockSpec(memory_space=pl.ANY),) * len(after), out_specs=(_HBM, _HBM),
        input_output_aliases={0: 0, 1: 1}, compiler_params=pltpu.CompilerParams(has_side_effects=_DATAFLOW),
    )(x_thru, land_thru, send_sems, recv_sems, *after)[1]


def _exchange(kind, x, name):
    n_out = len(_exchange_shapes(kind, x))

    def body(x_ref, *refs):
        plan = _plan(kind, x_ref, refs[:n_out], refs[n_out:])
        _start(plan)
        _finish(plan)

    return pl.pallas_call(body, name=name, out_shape=_exchange_shapes(kind, x), in_specs=[_HBM],
                          out_specs=[_HBM] * n_out, scratch_shapes=_exchange_sems(kind))(x)


def _call(body, name, grid, in_specs, out_specs, out_shape, args, scratch=(), sem=None, nbytes=0, aliases=None,
          comm=(), deps=()):
    in_specs, out_specs, out_shape, args, scratch = (list(in_specs), list(out_specs), list(out_shape), list(args),
                                                     list(scratch))
    if deps:
        n_real, n_deps, unordered = len(args), len(deps), body

        def body(*refs):
            unordered(*refs[:n_real], *refs[n_real + n_deps:])

        in_specs += [pl.BlockSpec(memory_space=pl.ANY)] * n_deps
        args += list(deps)
    if comm:
        n_in, n_out, n_scr, n_ops = len(args), len(out_shape), len(scratch), len(comm)
        kinds = [kind for kind, _ in comm]
        shapes = [_exchange_shapes(kind, x) for kind, x in comm]
        inner = body

        def body(*refs):
            ins, refs = refs[:n_in], refs[n_in:]
            cins, refs = refs[:n_ops], refs[n_ops:]
            outs, refs = refs[:n_out], refs[n_out:]
            couts = []
            for sh in shapes:
                couts.append(refs[:len(sh)])
                refs = refs[len(sh):]
            own_scratch, sems = refs[:n_scr], refs[n_scr:]
            first = last = None
            for axis, size in enumerate(grid):
                at_start, at_end = pl.program_id(axis) == 0, pl.program_id(axis) == size - 1
                first = at_start if first is None else first & at_start
                last = at_end if last is None else last & at_end
            plans = [_plan(kinds[o], cins[o], couts[o], sems[3 * o:3 * o + 3]) for o in range(n_ops)]

            @pl.when(first)
            def _():
                for p in plans:
                    _start(p)

            inner(*ins, *outs, *own_scratch)

            @pl.when(last)
            def _():
                for p in plans:
                    _finish(p)

        in_specs += [_HBM] * n_ops
        args += [x for _, x in comm]
        for kind, sh in zip(kinds, shapes):
            out_shape += sh
            out_specs += [_HBM] * len(sh)
            scratch += _exchange_sems(kind)
        sem = ("arbitrary",) * len(grid)
    limit = int(min(max(2 * nbytes + (8 << 20), 32 << 20), VMEM_CAP))
    return pl.pallas_call(
        body, name=name, grid=grid, in_specs=in_specs, out_specs=out_specs, out_shape=out_shape,
        scratch_shapes=scratch, input_output_aliases=aliases or {},
        compiler_params=pltpu.CompilerParams(dimension_semantics=sem, vmem_limit_bytes=limit),
    )(*args)


def _sum_slots(x, tr, name):
    n, r, c = x.shape

    def body(x_ref, o_ref):
        acc = x_ref[0].astype(F32)
        for d in range(1, n):
            acc = acc + x_ref[d].astype(F32)
        o_ref[...] = acc

    return _call(body, name, (r // tr,), [pl.BlockSpec((n, tr, c), lambda i: (0, i, 0))],
                 [pl.BlockSpec((tr, c), lambda i: (i, 0))], [jax.ShapeDtypeStruct((r, c), F32)], [x],
                 sem=("parallel",), nbytes=_nbytes((n, tr, c), x.dtype) + _nbytes((tr, c), F32))[0]


_DIMS = {"nn": (((1,), (0,)), ((), ())), "nt": (((1,), (1,)), ((), ())), "tn": (((0,), (0,)), ((), ()))}


def _matmul(a, b, mode, out_dtype, name, comm=(), deps=(), halves=False):
    a_shape = (a.shape[1], 2 * a.shape[2]) if halves else a.shape
    if mode == "nn":
        (m, k), (k2, n) = a_shape, b.shape
    elif mode == "nt":
        (m, k), (n, k2) = a_shape, b.shape
    else:
        (k, m), (k2, n) = a_shape, b.shape
    assert k == k2, (a.shape, b.shape, mode)
    tm, tn, tk = _tile(m, MM_TM), _tile(n, MM_TN), _tile(k, MM_TK)
    if halves and mode == "nn":
        tk = _tile(k // 2, MM_TK)
    if halves and mode == "tn":
        tm = _tile(m // 2, MM_TM)
    nk = k // tk
    dims = _DIMS[mode]
    if halves:
        per_half = (k // 2) // tk if mode == "nn" else (m // 2) // tm
        a_spec = {"nn": pl.BlockSpec((None, tm, tk), lambda i, j, kk: (kk // per_half, i, kk % per_half)),
                  "tn": pl.BlockSpec((None, tk, tm), lambda i, j, kk: (i // per_half, kk, i % per_half))}[mode]
    else:
        a_spec = {"nn": pl.BlockSpec((tm, tk), lambda i, j, kk: (i, kk)),
                  "nt": pl.BlockSpec((tm, tk), lambda i, j, kk: (i, kk)),
                  "tn": pl.BlockSpec((tk, tm), lambda i, j, kk: (kk, i))}[mode]
    b_spec = {"nn": pl.BlockSpec((tk, tn), lambda i, j, kk: (kk, j)),
              "nt": pl.BlockSpec((tn, tk), lambda i, j, kk: (j, kk)),
              "tn": pl.BlockSpec((tk, tn), lambda i, j, kk: (kk, j))}[mode]

    def partial_product(a_ref, b_ref):
        return lax.dot_general(a_ref[...].astype(BF16), b_ref[...].astype(BF16), dims, preferred_element_type=F32)

    if nk == 1:
        def body(a_ref, b_ref, o_ref):
            o_ref[...] = partial_product(a_ref, b_ref).astype(o_ref.dtype)
        scratch = []
    else:
        def body(a_ref, b_ref, o_ref, acc_ref):
            kk = pl.program_id(2)

            @pl.when(kk == 0)
            def _():
                acc_ref[...] = jnp.zeros_like(acc_ref)

            acc_ref[...] += partial_product(a_ref, b_ref)

            @pl.when(kk == nk - 1)
            def _():
                o_ref[...] = acc_ref[...].astype(o_ref.dtype)
        scratch = [pltpu.VMEM((tm, tn), F32)]

    nbytes = (_nbytes((tm, tk), a.dtype) + _nbytes((tk, tn), b.dtype) + _nbytes((tm, tn), out_dtype)
              + _nbytes((tm, tn), F32))
    res = _call(body, name, (m // tm, n // tn, nk), [a_spec, b_spec],
                [pl.BlockSpec((tm, tn), lambda i, j, kk: (i, j))], [jax.ShapeDtypeStruct((m, n), out_dtype)], [a, b],
                scratch=scratch, sem=("parallel", "parallel", "arbitrary"), nbytes=nbytes, comm=comm, deps=deps)
    return res if comm else res[0]


def _matmul_post_norm(a, b, x, g_post, name, g_next=None, target=None):
    (m, k), (k2, d) = a.shape, b.shape
    assert k == k2 and d == D_MODEL
    with_loss = target is not None
    tm, tk = _tile(m, ROW_BLOCK if with_loss else 2 * ROW_BLOCK), _tile(k, MM_TK)
    nk = k // tk

    def body(a_ref, b_ref, x_ref, gp_ref, last_ref, o_ref, first_ref, second_ref, acc_ref):
        i, kk = pl.program_id(0), pl.program_id(1)

        @pl.when(kk == 0)
        def _():
            acc_ref[...] = jnp.zeros_like(acc_ref)

        acc_ref[...] += jnp.dot(a_ref[...], b_ref[...], preferred_element_type=F32)

        @pl.when(kk == nk - 1)
        def _():
            ov = acc_ref[...]
            o_ref[...] = ov
            xn = x_ref[...] + (ov * _rms(ov)) * gp_ref[...]
            if with_loss:
                err = xn - last_ref[...]
                first_ref[...] = err / d

                @pl.when(i == 0)
                def _():
                    second_ref[...] = jnp.zeros_like(second_ref)

                second_ref[...] += 0.5 * jnp.sum(jnp.mean(err * err, axis=-1, keepdims=True), axis=0, keepdims=True)
            else:
                first_ref[...] = xn
                second_ref[...] = ((xn * _rms(xn)) * last_ref[...]).astype(BF16)

    rows = pl.BlockSpec((tm, d), lambda i, kk: (i, 0))
    vec = pl.BlockSpec((1, d), lambda i, kk: (0, 0))
    second = ((pl.BlockSpec((1, 1), lambda i, kk: (0, 0)), jax.ShapeDtypeStruct((1, 1), F32)) if with_loss
              else (rows, jax.ShapeDtypeStruct((m, d), BF16)))
    nbytes = _nbytes((tm, tk), BF16) + _nbytes((tk, d), BF16) + 6 * _nbytes((tm, d), F32)
    return _call(body, name, (m // tm, nk),
                 [pl.BlockSpec((tm, tk), lambda i, kk: (i, kk)), pl.BlockSpec((tk, d), lambda i, kk: (kk, 0)), rows, vec,
                  rows if with_loss else vec],
                 [rows, rows, second[0]],
                 [jax.ShapeDtypeStruct((m, d), F32), jax.ShapeDtypeStruct((m, d), F32), second[1]],
                 [a, b, x, g_post, target if with_loss else g_next],
                 scratch=[pltpu.VMEM((tm, d), F32)], sem=("arbitrary", "arbitrary"), nbytes=nbytes)


def _matmul_norm_bwd(a, b, dskip, xin, g_pre, name, post=None, halves=False, deps=()):
    m, k = (a.shape[1], 2 * a.shape[2]) if halves else a.shape
    (k2, d) = b.shape
    assert k == k2 and d == D_MODEL
    tm = _tile(m, ROW_BLOCK)
    tk = _tile(k // 2 if halves else k, MM_TK)
    nk = k // tk
    n_in = 5 + (2 if post else 0)

    def body(*refs):
        (a_ref, b_ref, dskip_ref, xin_ref, gpre_ref), rest = refs[:5], list(refs[5:n_in])
        outs, acc_ref = list(refs[n_in:-1]), refs[-1]
        i, kk = pl.program_id(0), pl.program_id(1)

        @pl.when(kk == 0)
        def _():
            acc_ref[...] = jnp.zeros_like(acc_ref)

        acc_ref[...] += jnp.dot(a_ref[...], b_ref[...], preferred_element_type=F32)

        @pl.when(kk == nk - 1)
        def _():
            dxin, dg_rows = _rms_bwd(acc_ref[...], xin_ref[...], gpre_ref[...])
            dx = dskip_ref[...] + dxin
            dx_ref, dgpre_ref = outs[0], outs[1]
            dx_ref[...] = dx

            @pl.when(i == 0)
            def _():
                for ref in outs[1::2]:
                    ref[...] = jnp.zeros_like(ref)

            dgpre_ref[...] += jnp.sum(dg_rows, axis=0, keepdims=True)
            if post:
                da, dg_rows = _rms_bwd(dx, rest[0][...], rest[1][...])
                outs[2][...] = da.astype(BF16)
                outs[3][...] += jnp.sum(dg_rows, axis=0, keepdims=True)

    if halves:
        per_half = (k // 2) // tk
        a_spec = pl.BlockSpec((None, tm, tk), lambda i, kk: (kk // per_half, i, kk % per_half))
    else:
        a_spec = pl.BlockSpec((tm, tk), lambda i, kk: (i, kk))
    rows = pl.BlockSpec((tm, d), lambda i, kk: (i, 0))
    vec = pl.BlockSpec((1, d), lambda i, kk: (0, 0))
    args, in_specs = [a, b, dskip, xin, g_pre], [a_spec, pl.BlockSpec((tk, d), lambda i, kk: (kk, 0)), rows, rows, vec]
    out_specs, out_shape = [rows, vec], [jax.ShapeDtypeStruct((m, d), F32), jax.ShapeDtypeStruct((1, d), F32)]
    if post:
        args += list(post)
        in_specs += [rows, vec]
        out_specs += [rows, vec]
        out_shape += [jax.ShapeDtypeStruct((m, d), BF16), jax.ShapeDtypeStruct((1, d), F32)]
    nbytes = _nbytes((tm, tk), BF16) + _nbytes((tk, d), BF16) + 7 * _nbytes((tm, d), F32)
    return _call(body, name, (m // tm, nk), in_specs, out_specs, out_shape, args, scratch=[pltpu.VMEM((tm, d), F32)],
                 sem=("arbitrary", "arbitrary"), nbytes=nbytes, deps=deps)


def _rms(x):
    return lax.rsqrt(jnp.mean(x * x, axis=-1, keepdims=True) + EPS)


def _rms_bwd(dz, a, g):
    r = _rms(a)
    dzg = dz * g
    da = r * dzg - a * (r * r * r) * jnp.mean(dzg * a, axis=-1, keepdims=True)
    return da, dz * (a * r)


def _row_spec(tr, width):
    return pl.BlockSpec((tr, width), lambda i: (i, 0))


def _vec_spec(width):
    return pl.BlockSpec((1, width), lambda i: (0, 0))


def _rms_fwd(x, g, name):
    t, d = x.shape
    tr = min(ROW_BLOCK, t)

    def body(x_ref, g_ref, h_ref):
        xv = x_ref[...]
        h_ref[...] = ((xv * _rms(xv)) * g_ref[...]).astype(BF16)

    return _call(body, name, (t // tr,), [_row_spec(tr, d), _vec_spec(d)], [_row_spec(tr, d)],
                 [jax.ShapeDtypeStruct((t, d), BF16)], [x, g], sem=("parallel",), nbytes=3 * _nbytes((tr, d), F32))[0]


def _norm_bwd(dskip, name, pre=None, post=None):
    t, d = dskip.shape
    tr = min(ROW_BLOCK, t)
    n_in = 1 + (3 if pre else 0) + (2 if post else 0)

    def body(*refs):
        ins, outs = list(refs[:n_in]), list(refs[n_in:])
        first = pl.program_id(0) == 0
        dx = ins.pop(0)[...]
        if pre:
            dh_ref, xin_ref, gpre_ref = ins.pop(0), ins.pop(0), ins.pop(0)
            dxin, dg_rows = _rms_bwd(dh_ref[...].astype(F32), xin_ref[...], gpre_ref[...])
            dx = dx + dxin
            dx_ref, dgpre_ref = outs.pop(0), outs.pop(0)
            dx_ref[...] = dx

            @pl.when(first)
            def _():
                dgpre_ref[...] = jnp.zeros_like(dgpre_ref)

            dgpre_ref[...] += jnp.sum(dg_rows, axis=0, keepdims=True)
        if post:
            a_ref, gpost_ref = ins.pop(0), ins.pop(0)
            da, dg_rows = _rms_bwd(dx, a_ref[...], gpost_ref[...])
            da_ref, dgpost_ref = outs.pop(0), outs.pop(0)
            da_ref[...] = da.astype(BF16)

            @pl.when(first)
            def _():
                dgpost_ref[...] = jnp.zeros_like(dgpost_ref)

            dgpost_ref[...] += jnp.sum(dg_rows, axis=0, keepdims=True)

    args, in_specs, out_shape, out_specs = [dskip], [_row_spec(tr, d)], [], []
    if pre:
        args += list(pre)
        in_specs += [_row_spec(tr, d), _row_spec(tr, d), _vec_spec(d)]
        out_shape += [jax.ShapeDtypeStruct((t, d), F32), jax.ShapeDtypeStruct((1, d), F32)]
        out_specs += [_row_spec(tr, d), _vec_spec(d)]
    if post:
        args += list(post)
        in_specs += [_row_spec(tr, d), _vec_spec(d)]
        out_shape += [jax.ShapeDtypeStruct((t, d), BF16), jax.ShapeDtypeStruct((1, d), F32)]
        out_specs += [_row_spec(tr, d), _vec_spec(d)]
    return _call(body, name, (t // tr,), in_specs, out_specs, out_shape, args, sem=("arbitrary",),
                 nbytes=7 * _nbytes((tr, d), F32))


def _sigmoid(x):
    return 1.0 / (1.0 + jnp.exp(-x))


def _ffn_in(h, w_gu_t, name):
    t, d = h.shape
    tm, tn = _tile(t, 1024), _tile(D_FF, MM_TN)
    n_tiles = D_FF // tn

    def body(h_ref, wg_ref, wu_ref, g_ref, u_ref, a_ref):
        hv = h_ref[...]
        gate = lax.dot_general(hv, wg_ref[...], _DIMS["nt"], preferred_element_type=F32).astype(BF16)
        up = lax.dot_general(hv, wu_ref[...], _DIMS["nt"], preferred_element_type=F32).astype(BF16)
        g_ref[...] = gate
        u_ref[...] = up
        gate, up = gate.astype(F32), up.astype(F32)
        a_ref[...] = ((gate * _sigmoid(gate)) * up).astype(BF16)

    tile = pl.BlockSpec((tm, tn), lambda i, j: (i, j))
    shape = jax.ShapeDtypeStruct((t, D_FF), BF16)
    return _call(body, name, (t // tm, n_tiles),
                 [pl.BlockSpec((tm, d), lambda i, j: (i, 0)), pl.BlockSpec((tn, d), lambda i, j: (j, 0)),
                  pl.BlockSpec((tn, d), lambda i, j: (j + n_tiles, 0))],
                 [tile, tile, tile], [shape, shape, shape], [h, w_gu_t, w_gu_t], sem=("parallel", "parallel"),
                 nbytes=_nbytes((tm, d), BF16) + 2 * _nbytes((tn, d), BF16) + 5 * _nbytes((tm, tn), F32))


def _ffn_in_bwd(dd, w_d, gate, up, name):
    t, d = dd.shape
    tm, tn = _tile(t, 1024), _tile(D_FF, MM_TN)

    def body(dd_ref, w_ref, g_ref, u_ref, o_ref):
        da = lax.dot_general(dd_ref[...], w_ref[...], _DIMS["nt"], preferred_element_type=F32)
        gate, up = g_ref[...].astype(F32), u_ref[...].astype(F32)
        s = _sigmoid(gate)
        o_ref[0] = (da * up * (s * (1.0 + gate * (1.0 - s)))).astype(BF16)
        o_ref[1] = (da * (gate * s)).astype(BF16)

    tile = pl.BlockSpec((tm, tn), lambda i, j: (i, j))
    return _call(body, name, (t // tm, D_FF // tn),
                 [pl.BlockSpec((tm, d), lambda i, j: (i, 0)), pl.BlockSpec((tn, d), lambda i, j: (j, 0)), tile, tile],
                 [pl.BlockSpec((2, tm, tn), lambda i, j: (0, i, j))], [jax.ShapeDtypeStruct((2, t, D_FF), BF16)],
                 [dd, w_d, gate, up], sem=("parallel", "parallel"),
                 nbytes=_nbytes((tm, d), BF16) + _nbytes((tn, d), BF16) + 6 * _nbytes((tm, tn), F32))[0]


def _split3(x):
    hi = x.astype(BF16)
    r1 = x - hi.astype(F32)
    mid = r1.astype(BF16)
    lo = (r1 - mid.astype(F32)).astype(BF16)
    return hi, mid, lo


def _dot_exact(mat01, x):
    hi, mid, lo = _split3(x)
    out = jnp.dot(mat01, hi, preferred_element_type=F32)
    out = out + jnp.dot(mat01, mid, preferred_element_type=F32)
    return out + jnp.dot(mat01, lo, preferred_element_type=F32)


def _dot_exact_rhs(x, mat01):
    hi, mid, lo = _split3(x)
    out = jnp.dot(hi, mat01, preferred_element_type=F32)
    out = out + jnp.dot(mid, mat01, preferred_element_type=F32)
    return out + jnp.dot(lo, mat01, preferred_element_type=F32)


def _tri(lower):
    r = lax.broadcasted_iota(jnp.int32, (CHUNK, CHUNK), 0)
    c = lax.broadcasted_iota(jnp.int32, (CHUNK, CHUNK), 1)
    return jnp.where(r >= c if lower else r <= c, 1.0, 0.0).astype(BF16)


def _log_sigmoid(z):
    return jnp.minimum(z, 0.0) - jnp.log(1.0 + jnp.exp(-jnp.abs(z)))


def _forget_cumsum(proj, b_pad, name):
    t = proj.shape[0]
    nb = t // CHUNK

    def body(f_ref, b_ref, c_ref):
        tri = _tri(True)
        b = b_ref[...]

        def blk(i, carry):
            rows = pl.ds(pl.multiple_of(i * CHUNK, CHUNK), CHUNK)
            cs = _dot_exact(tri, _log_sigmoid(f_ref[rows, :] + b)) + carry
            c_ref[rows, :] = cs
            return cs[CHUNK - 1:CHUNK, :]

        lax.fori_loop(0, nb, blk, jnp.zeros((1, HEAD_DIM), F32))

    return _call(body, name, (1,),
                 [pl.BlockSpec((t, HEAD_DIM), lambda i: (0, CB_F)), pl.BlockSpec((1, HEAD_DIM), lambda i: (0, 0))],
                 [pl.BlockSpec((t, HEAD_DIM), lambda i: (0, 0))], [jax.ShapeDtypeStruct((t, HEAD_DIM), F32)],
                 [proj, b_pad], sem=("arbitrary",), nbytes=2 * _nbytes((t, HEAD_DIM), F32))[0]


def _causal(block):
    r = lax.broadcasted_iota(jnp.int32, (block, block), 0)
    c = lax.broadcasted_iota(jnp.int32, (block, block), 1)
    return c <= r


def _lanes(x, width):
    return jnp.concatenate([x] * (width // HEAD_DIM), axis=1)


def _chunks(t, blk):
    return [slice(r * blk, (r + 1) * blk) for r in range(t // blk)]


def _lane_of_head(x, h):
    lane = lax.broadcasted_iota(jnp.int32, x.shape, 1)
    return jnp.broadcast_to(jnp.sum(jnp.where(lane == h, x, 0.0), axis=1, keepdims=True), x.shape)


def _causal_t(block):
    r = lax.broadcasted_iota(jnp.int32, (block, block), 0)
    c = lax.broadcasted_iota(jnp.int32, (block, block), 1)
    return r <= c


def _attn_fwd_t(proj, c_pad, c_row, name, comm=()):
    t = proj.shape[0]
    blk = min(ATTN_BLOCK, t)
    nq = t // blk
    scale = HEAD_DIM ** -0.5

    def body(q_ref, k_ref, v_ref, cp_ref, cr_ref, y_ref, e_ref, kb_s, vt_s, cb_s):
        h, i = pl.program_id(0), pl.program_id(1)

        @pl.when(i == 0)
        def _():
            for rows in _chunks(t, blk):
                kb_s[rows, :] = k_ref[rows, :].astype(BF16)
                vt_s[:, rows] = v_ref[rows, :].T.astype(BF16)
                cb_s[rows, :] = _lane_of_head(cp_ref[rows, :], h)

        q = (q_ref[...] * scale).astype(BF16)
        ci = cr_ref[:, pl.ds(pl.multiple_of(i * blk, blk), blk)]

        def step(j, carry, diagonal):
            m, l, acc = carry
            rows = pl.ds(pl.multiple_of(j * blk, blk), blk)
            s = lax.dot_general(kb_s[rows, :], q, _DIMS["nt"], preferred_element_type=F32)
            s = s + ci - _lanes(cb_s[rows, :], blk)
            if diagonal:
                s = jnp.where(_causal_t(blk), s, NEG)
            m_new = jnp.maximum(m, jnp.max(s, axis=0, keepdims=True))
            alpha = jnp.exp(m - m_new)
            p = jnp.exp(s - m_new)
            l = alpha * l + jnp.sum(p, axis=0, keepdims=True)
            acc = alpha * acc + jnp.dot(vt_s[:, rows], p.astype(BF16), preferred_element_type=F32)
            return m_new, l, acc

        init = (jnp.full((1, blk), NEG, F32), jnp.zeros((1, blk), F32), jnp.zeros((HEAD_DIM, blk), F32))
        carry = lax.fori_loop(0, i, lambda j, c: step(j, c, False), init)
        m, l, acc = step(i, carry, True)
        y_ref[...] = (acc / l).T
        e_ref[...] = ci - (m + jnp.log(l))

    nbytes = 3 * _nbytes((t, HEAD_DIM), F32) + 4 * _nbytes((blk, HEAD_DIM), F32) + 4 * _nbytes((blk, blk), F32)
    return _call(body, name, (N_HEADS, nq),
                 [pl.BlockSpec((blk, HEAD_DIM), lambda h, i: (i, CB_QKV + 3 * h)),
                  pl.BlockSpec((t, HEAD_DIM), lambda h, i: (0, CB_QKV + 3 * h + 1)),
                  pl.BlockSpec((t, HEAD_DIM), lambda h, i: (0, CB_QKV + 3 * h + 2)),
                  pl.BlockSpec((t, HEAD_DIM), lambda h, i: (0, 0)),
                  pl.BlockSpec((None, 1, t), lambda h, i: (h, 0, 0))],
                 [pl.BlockSpec((blk, HEAD_DIM), lambda h, i: (i, h)),
                  pl.BlockSpec((None, 1, blk), lambda h, i: (h, 0, i))],
                 [jax.ShapeDtypeStruct((t, D_MODEL), F32), jax.ShapeDtypeStruct((N_HEADS, 1, t), F32)],
                 [proj, proj, proj, c_pad, c_row],
                 scratch=[pltpu.VMEM((t, HEAD_DIM), BF16), pltpu.VMEM((HEAD_DIM, t), BF16),
                          pltpu.VMEM((t, HEAD_DIM), F32)],
                 sem=("parallel", "arbitrary"), nbytes=nbytes, comm=comm)


def _attn_bwd_t(proj, dy, e_row, delta_b, c_pad, dproj, name, deps=()):
    t = proj.shape[0]
    blk = min(ATTN_BLOCK, t)
    nb = t // blk
    scale = HEAD_DIM ** -0.5

    def body(q_ref, k_ref, v_ref, dy_ref, e_ref, dl_ref, cp_ref, dproj_in, dqkv_ref, dck_ref, dcq_ref,
             qb_s, qt_s, dyb_s, dyt_s, dl_s, dqt_acc):
        del dproj_in
        h, j = pl.program_id(0), pl.program_id(1)

        @pl.when(j == 0)
        def _():
            for rows in _chunks(t, blk):
                qs = q_ref[rows, :] * scale
                qb_s[rows, :] = qs.astype(BF16)
                qt_s[:, rows] = qs.T.astype(BF16)
                dyr = dy_ref[rows, :]
                dyb_s[rows, :] = dyr.astype(BF16)
                dyt_s[:, rows] = dyr.T.astype(BF16)
                dl_s[:, rows] = dl_ref[rows, :].T[0:1, :]
            dqt_acc[...] = jnp.zeros_like(dqt_acc)
            dcq_ref[...] = jnp.zeros_like(dcq_ref)

        kf = k_ref[...]
        kb, kt = kf.astype(BF16), kf.T.astype(BF16)
        vb = v_ref[...].astype(BF16)
        cj = _lanes(_lane_of_head(cp_ref[...], h), blk)

        def step(i, carry, diagonal):
            dkt, dvt, dkey = carry
            cols = pl.ds(pl.multiple_of(i * blk, blk), blk)
            s = lax.dot_general(kb, qb_s[cols, :], _DIMS["nt"], preferred_element_type=F32) + e_ref[:, cols] - cj
            if diagonal:
                s = jnp.where(_causal_t(blk), s, NEG)
            p = jnp.exp(s)
            dp = lax.dot_general(vb, dyb_s[cols, :], _DIMS["nt"], preferred_element_type=F32)
            ds = p * (dp - dl_s[:, cols])
            pb, dsb = p.astype(BF16), ds.astype(BF16)
            dvt = dvt + lax.dot_general(dyt_s[:, cols], pb, _DIMS["nt"], preferred_element_type=F32)
            dkt = dkt + lax.dot_general(qt_s[:, cols], dsb, _DIMS["nt"], preferred_element_type=F32)
            dqt_acc[:, cols] += jnp.dot(kt, dsb, preferred_element_type=F32) * scale
            dcq_ref[:, cols] += jnp.sum(ds, axis=0, keepdims=True)
            for k in range(blk // HEAD_DIM):
                dkey = dkey + ds[:, k * HEAD_DIM:(k + 1) * HEAD_DIM]
            return dkt, dvt, dkey

        zeros = jnp.zeros((HEAD_DIM, blk), F32)
        carry = step(j, (zeros, zeros, jnp.zeros((blk, HEAD_DIM), F32)), True)
        dkt, dvt, dkey = lax.fori_loop(j + 1, nb, lambda i, c: step(i, c, False), carry)
        mine = pl.ds(pl.multiple_of(j * blk, blk), blk)
        dqkv_ref[:, :HEAD_DIM] = dqt_acc[:, mine].T.astype(BF16)
        dqkv_ref[:, HEAD_DIM:2 * HEAD_DIM] = dkt.T.astype(BF16)
        dqkv_ref[:, 2 * HEAD_DIM:] = dvt.T.astype(BF16)
        dck_ref[...] = -jnp.sum(dkey.T, axis=0, keepdims=True)

    full = lambda cb: pl.BlockSpec((t, HEAD_DIM), lambda h, j: (0, cb(h)))
    head = lambda h: h
    row = pl.BlockSpec((None, 1, t), lambda h, j: (h, 0, 0))
    nbytes = (8 * _nbytes((t, HEAD_DIM), F32) + 4 * _nbytes((blk, HEAD_DIM), F32) + 6 * _nbytes((blk, blk), F32))
    return _call(body, name, (N_HEADS, nb),
                 [full(lambda h: CB_QKV + 3 * h),
                  pl.BlockSpec((blk, HEAD_DIM), lambda h, j: (j, CB_QKV + 3 * h + 1)),
                  pl.BlockSpec((blk, HEAD_DIM), lambda h, j: (j, CB_QKV + 3 * h + 2)),
                  full(head), row, full(head),
                  pl.BlockSpec((blk, HEAD_DIM), lambda h, j: (j, 0)),
                  pl.BlockSpec(memory_space=pl.ANY)],
                 [pl.BlockSpec((blk, QKV_WIDTH), lambda h, j: (j, (GATED + F_WIDTH) // QKV_WIDTH + h)),
                  pl.BlockSpec((None, 1, blk), lambda h, j: (h, 0, j)),
                  row],
                 [jax.ShapeDtypeStruct(dproj.shape, dproj.dtype), jax.ShapeDtypeStruct((N_HEADS, 1, t), F32),
                  jax.ShapeDtypeStruct((N_HEADS, 1, t), F32)],
                 [proj, proj, proj, dy, e_row, delta_b, c_pad, dproj],
                 scratch=[pltpu.VMEM((t, HEAD_DIM), BF16), pltpu.VMEM((HEAD_DIM, t), BF16),
                          pltpu.VMEM((t, HEAD_DIM), BF16), pltpu.VMEM((HEAD_DIM, t), BF16),
                          pltpu.VMEM((1, t), F32), pltpu.VMEM((HEAD_DIM, t), F32)],
                 sem=("parallel", "arbitrary"), nbytes=nbytes, aliases={7: 0}, deps=deps)


def _forget_bwd_t(proj, b_pad, dc, dproj, name):
    t = proj.shape[0]
    tr = min(ROW_BLOCK, t)
    nb = t // tr
    rev = lambda i: nb - 1 - i

    def body(f_ref, b_ref, dc_ref, dproj_in, df_ref, db_ref, run_ref):
        del dproj_in

        @pl.when(pl.program_id(0) == 0)
        def _():
            run_ref[...] = jnp.zeros_like(run_ref)
            db_ref[...] = jnp.zeros_like(db_ref)

        tri = _tri(False)
        b = b_ref[...]
        lane = lax.broadcasted_iota(jnp.int32, (CHUNK, HEAD_DIM), 1)
        df_ref[...] = jnp.zeros_like(df_ref)
        for n in reversed(range(tr // CHUNK)):
            rows = slice(n * CHUNK, (n + 1) * CHUNK)
            dlogf = _dot_exact(tri, dc_ref[rows, :]) + run_ref[...]
            run_ref[...] = dlogf[0:1, :]
            z = f_ref[rows, :] + b
            e = jnp.exp(-jnp.abs(z))
            sig_neg = jnp.where(z >= 0.0, e, 1.0) / (1.0 + e)
            df = jnp.where(lane < N_HEADS, dlogf * sig_neg, 0.0)
            df_ref[rows, :HEAD_DIM] = df.astype(BF16)
            db_ref[...] += jnp.sum(df, axis=0, keepdims=True)

    return _call(body, name, (nb,),
                 [pl.BlockSpec((tr, HEAD_DIM), lambda i: (rev(i), CB_F)), pl.BlockSpec((1, HEAD_DIM), lambda i: (0, 0)),
                  pl.BlockSpec((tr, HEAD_DIM), lambda i: (rev(i), 0)), pl.BlockSpec(memory_space=pl.ANY)],
                 [pl.BlockSpec((tr, F_WIDTH), lambda i: (rev(i), GATED // F_WIDTH)),
                  pl.BlockSpec((1, HEAD_DIM), lambda i: (0, 0))],
                 [jax.ShapeDtypeStruct(dproj.shape, dproj.dtype), jax.ShapeDtypeStruct((1, HEAD_DIM), F32)],
                 [proj, b_pad, dc, dproj], scratch=[pltpu.VMEM((1, HEAD_DIM), F32)],
                 sem=("arbitrary",), nbytes=4 * _nbytes((tr, HEAD_DIM), F32), aliases={3: 0})


_GELU_K = math.sqrt(2.0 / math.pi)
_GELU_C = 0.044715


def _gelu(x):
    t = jnp.tanh(_GELU_K * (x + _GELU_C * (x * x * x)))
    return 0.5 * x * (1.0 + t), t


def _gelu_grad(x, t):
    return 0.5 * (1.0 + t) + 0.5 * x * (1.0 - t * t) * (_GELU_K * (1.0 + 3.0 * _GELU_C * (x * x)))


def _layernorm_stats(a):
    mu = jnp.mean(a, axis=-1, keepdims=True)
    xc = a - mu
    r = lax.rsqrt(jnp.mean(xc * xc, axis=-1, keepdims=True) + EPS)
    return xc * r, r


def _group(g):
    return slice(g * CHUNK, (g + 1) * CHUNK)


def _fixed(shape):
    return pl.BlockSpec(shape, lambda i: (0,) * len(shape))


def _mix_fwd(proj, y_b, w_s, b_cols, g_v, name):
    t = proj.shape[0]
    step_rows = min(MIX_ROWS, t)

    def body(u_ref, vs_ref, ga_ref, gb_ref, yb_ref, w_ref, b_ref, gv_ref, o_ref):
        mask = _causal(CHUNK)
        w = [jnp.where(mask, w_ref[g], 0.0).astype(BF16) for g in range(N_GROUPS)]
        for rows in _chunks(step_rows, CHUNK):
            a_u, _ = _gelu(u_ref[rows, :])
            a_v, _ = _gelu(vs_ref[rows, :])
            xhat, _ = _layernorm_stats(a_v)
            vn = (xhat * gv_ref[...]).astype(BF16)
            sa, sb = _sigmoid(ga_ref[rows, :]), _sigmoid(gb_ref[rows, :])
            yb = yb_ref[rows, :]
            for g in range(N_GROUPS):
                cols = _group(g)
                mixed = jnp.dot(w[g], vn[:, cols], preferred_element_type=F32) + b_ref[:, g:g + 1]
                o_ref[rows, cols] = (sa[:, cols] * (a_u[:, cols] * mixed) + sb[:, cols] * yb[:, cols]).astype(BF16)

    gated = lambda k: pl.BlockSpec((step_rows, D_MODEL), lambda i: (i, k))
    return _call(body, name, (t // step_rows,),
                 [gated(0), gated(1), gated(2), gated(3), _row_spec(step_rows, D_MODEL),
                  _fixed((N_GROUPS, CHUNK, CHUNK)), _fixed((CHUNK, N_GROUPS)), _fixed((1, D_MODEL))],
                 [_row_spec(step_rows, D_MODEL)], [jax.ShapeDtypeStruct((t, D_MODEL), BF16)],
                 [proj, proj, proj, proj, y_b, w_s, b_cols, g_v], sem=("parallel",),
                 nbytes=16 * _nbytes((step_rows, D_MODEL), F32))[0]


def _mix_bwd(proj, y_b, dmerged, w_s, b_cols, g_v, sel, name, deps=()):
    t = proj.shape[0]
    step_rows = min(MIX_BWD_ROWS, t)

    def body(u_ref, vs_ref, ga_ref, gb_ref, yb_ref, dm_ref, w_ref, b_ref, gv_ref, sel_ref,
             dg_ref, dyb_ref, delta_ref, dw_ref, dbt_ref, dgv_ref):
        @pl.when(pl.program_id(0) == 0)
        def _():
            dw_ref[...] = jnp.zeros_like(dw_ref)
            dbt_ref[...] = jnp.zeros_like(dbt_ref)
            dgv_ref[...] = jnp.zeros_like(dgv_ref)

        gv = gv_ref[...]
        mask = _causal(CHUNK)
        w = [jnp.where(mask, w_ref[g], 0.0).astype(BF16) for g in range(N_GROUPS)]
        for rows in _chunks(step_rows, CHUNK):
            u, vs = u_ref[rows, :], vs_ref[rows, :]
            a_u, t_u = _gelu(u)
            a_v, t_v = _gelu(vs)
            xhat, r = _layernorm_stats(a_v)
            vn = (xhat * gv).astype(BF16)
            sa, sb = _sigmoid(ga_ref[rows, :]), _sigmoid(gb_ref[rows, :])
            yb, dm = yb_ref[rows, :], dm_ref[rows, :].astype(F32)
            dyb = dm * sb
            dyb_ref[rows, :] = dyb
            dyb_yb = dyb * yb
            dg_ref[rows, 3 * D_MODEL:] = (dm * yb * (sb * (1.0 - sb))).astype(BF16)
            dya = dm * sa
            dmixed_parts, dvn_parts = [], []
            for g in range(N_GROUPS):
                cols = _group(g)
                delta_ref[rows, cols] = jnp.broadcast_to(jnp.sum(dyb_yb[:, cols], axis=1, keepdims=True),
                                                         (CHUNK, CHUNK))
                mixed = jnp.dot(w[g], vn[:, cols], preferred_element_type=F32) + b_ref[:, g:g + 1]
                y_a = a_u[:, cols] * mixed
                dg_ref[rows, 2 * D_MODEL + g * CHUNK:2 * D_MODEL + (g + 1) * CHUNK] = (
                    dm[:, cols] * y_a * (sa[:, cols] * (1.0 - sa[:, cols]))).astype(BF16)
                dg_ref[rows, cols] = (dya[:, cols] * mixed * _gelu_grad(u[:, cols], t_u[:, cols])).astype(BF16)
                dmixed = dya[:, cols] * a_u[:, cols]
                dmb = dmixed.astype(BF16)
                dw = lax.dot_general(dmb, vn[:, cols], _DIMS["nt"], preferred_element_type=F32)
                dw_ref[g] += jnp.where(mask, dw, 0.0)
                dvn_parts.append(lax.dot_general(w[g], dmb, _DIMS["tn"], preferred_element_type=F32))
                dmixed_parts.append(dmixed)
            dmixed_all = jnp.concatenate(dmixed_parts, axis=1)
            dvn = jnp.concatenate(dvn_parts, axis=1)
            dbt_ref[...] += _dot_exact_rhs(dmixed_all, sel_ref[...])
            dgv_ref[...] += jnp.sum(dvn * xhat, axis=0, keepdims=True)
            dxh = dvn * gv
            da_v = r * (dxh - jnp.mean(dxh, axis=-1, keepdims=True)
                        - xhat * jnp.mean(dxh * xhat, axis=-1, keepdims=True))
            dg_ref[rows, D_MODEL:2 * D_MODEL] = (da_v * _gelu_grad(vs, t_v)).astype(BF16)

    row = lambda width: _row_spec(step_rows, width)
    gated = lambda k: pl.BlockSpec((step_rows, D_MODEL), lambda i: (i, k))
    return _call(body, name, (t // step_rows,),
                 [gated(0), gated(1), gated(2), gated(3), row(D_MODEL), row(D_MODEL),
                  _fixed((N_GROUPS, CHUNK, CHUNK)), _fixed((CHUNK, N_GROUPS)), _fixed((1, D_MODEL)),
                  _fixed((D_MODEL, HEAD_DIM))],
                 [row(GATED), row(D_MODEL), row(D_MODEL),
                  _fixed((N_GROUPS, CHUNK, CHUNK)), _fixed((CHUNK, HEAD_DIM)), _fixed((1, D_MODEL))],
                 [jax.ShapeDtypeStruct((t, IN_PAD), BF16), jax.ShapeDtypeStruct((t, D_MODEL), F32),
                  jax.ShapeDtypeStruct((t, D_MODEL), F32), jax.ShapeDtypeStruct((N_GROUPS, CHUNK, CHUNK), F32),
                  jax.ShapeDtypeStruct((CHUNK, HEAD_DIM), F32), jax.ShapeDtypeStruct((1, D_MODEL), F32)],
                 [proj, proj, proj, proj, y_b, dmerged, w_s, b_cols, g_v, sel], sem=("arbitrary",),
                 nbytes=24 * _nbytes((step_rows, D_MODEL), F32), deps=deps)


def _adamw_update(w_ref, g_ref, m_ref, v_ref, d_ref, nm_ref, nv_ref):
    gv = g_ref[...]
    nm = ADAM_B1 * m_ref[...] + (1.0 - ADAM_B1) * gv
    nv = ADAM_B2 * v_ref[...] + (1.0 - ADAM_B2) * (gv * gv)
    m_hat = nm / (1.0 - ADAM_B1 ** ADAM_STEP)
    v_hat = nv / (1.0 - ADAM_B2 ** ADAM_STEP)
    d_ref[...] = -ADAM_LR * (m_hat / (jnp.sqrt(v_hat) + ADAM_EPS) + ADAM_WD * w_ref[...])
    nm_ref[...] = nm
    nv_ref[...] = nv


def _adamw(w, g, m, v, name):
    r, c = w.shape
    tr = _tile(r, 256, unit=8)

    def body(w_ref, g_ref, m_ref, v_ref, d_ref, nm_ref, nv_ref):
        _adamw_update(w_ref, g_ref, m_ref, v_ref, d_ref, nm_ref, nv_ref)

    spec = pl.BlockSpec((tr, c), lambda i: (i, 0))
    shape = jax.ShapeDtypeStruct((r, c), F32)
    return _call(body, name, (r // tr,), [spec] * 4, [spec] * 3, [shape] * 3, [w, g, m, v], sem=("parallel",),
                 nbytes=7 * _nbytes((tr, max(c, 128)), F32))


def _adamw_cols(w, g, m, v, name):
    layers, r, c = w.shape
    tc = 256

    def body(w_ref, g_ref, m_ref, v_ref, d_ref, nm_ref, nv_ref):
        _adamw_update(w_ref, g_ref, m_ref, v_ref, d_ref, nm_ref, nv_ref)

    spec = pl.BlockSpec((None, r, tc), lambda l, j: (l, 0, j))
    shape = jax.ShapeDtypeStruct(w.shape, F32)
    return _call(body, name, (layers, c // tc), [spec] * 4, [spec] * 3, [shape] * 3, [w, g, m, v],
                 sem=("parallel", "parallel"), nbytes=7 * _nbytes((r, tc), F32))


def _pad_rows(a, rows):
    return jnp.pad(a, ((0, rows - a.shape[0]), (0, 0)))


def _unpack_in(gathered):
    i = jnp.arange(16)[None, :, None]
    b = jnp.arange(N_DEV)[:, None, None]
    head = jnp.where(i < b, jnp.roll(gathered[:, IN_STRIDE:IN_STRIDE + 16], 1, axis=0), gathered[:, :16])
    nat = jnp.concatenate([head, gathered[:, 16:IN_STRIDE]], axis=1).reshape(N_DEV * IN_STRIDE, D_MODEL)
    qkv = nat[2 * D_MODEL:5 * D_MODEL].reshape(3, N_HEADS, HEAD_DIM, D_MODEL)
    qkv = jnp.transpose(qkv, (1, 0, 2, 3)).reshape(3 * D_MODEL, D_MODEL)
    f = _pad_rows(gathered[N_DEV - 1, IN_STRIDE:IN_STRIDE + N_HEADS], F_WIDTH)
    return jnp.concatenate([nat[:2 * D_MODEL], nat[5 * D_MODEL:7 * D_MODEL], f, qkv], axis=0)


def _pack_grad_in(dw_in_t):
    qkv = dw_in_t[GATED + F_WIDTH:].reshape(N_HEADS, 3, HEAD_DIM, D_MODEL)
    qkv = jnp.transpose(qkv, (1, 0, 2, 3)).reshape(3 * D_MODEL, D_MODEL)
    return jnp.concatenate([dw_in_t[:2 * D_MODEL], qkv, dw_in_t[2 * D_MODEL:GATED],
                            dw_in_t[GATED:GATED + IN_ROWS - N_DEV * IN_STRIDE]], axis=0)


def kernel(x, mix_pre_g, w_in, b_forget, sgu_norm_g, w_spatial, b_spatial, w_out, mix_post_g, ffn_pre_g, w_gate, w_up, w_down, ffn_post_g, loss_target, m_mix_pre_g, m_w_in, m_b_forget, m_sgu_norm_g, m_w_spatial, m_b_spatial, m_w_out, m_mix_post_g, m_ffn_pre_g, m_w_gate, m_w_up, m_w_down, m_ffn_post_g, v_mix_pre_g, v_w_in, v_b_forget, v_sgu_norm_g, v_w_spatial, v_b_spatial, v_w_out, v_mix_post_g, v_ffn_pre_g, v_w_gate, v_w_up, v_w_down, v_ffn_post_g):
    depth = w_in.shape[0]
    assert depth == 2
    t = x.shape[1]
    x0 = x.reshape(t, D_MODEL)
    target = loss_target.reshape(t, D_MODEL)
    sel = (jnp.arange(D_MODEL)[:, None] // HEAD_DIM == jnp.arange(HEAD_DIM)[None, :]).astype(BF16)
    vec = lambda a, l: a[l][None, :]
    zero_row = jnp.zeros((1, D_MODEL), F32)

    me = 4 * lax.axis_index("x") + 2 * lax.axis_index("y") + lax.axis_index("c")
    p_in = [lax.dynamic_update_slice(jnp.zeros((P_IN, D_MODEL), BF16), w_in[l].T.astype(BF16), (me, 0))
            for l in range(depth)]
    p_rest = [[w_out[l].astype(BF16), jnp.stack([w_gate[l].T, w_up[l].T]).astype(BF16), w_down[l].astype(BF16)]
              for l in range(depth)]
    w_in_t = [None] * depth
    w_rest = [None] * depth
    w_in_t[0] = _unpack_in(_exchange(("gather", 0), p_in[0], "gather_w_in_0")[0])

    saved = []
    xl = x0
    h = _rms_fwd(xl, vec(mix_pre_g, 0), "rms_in_0")
    dy = loss = None
    for l in range(depth):
        b_pad = jnp.pad(b_forget[l], (0, HEAD_DIM - N_HEADS))[None, :]
        b_cols = b_spatial[l].T
        proj = _matmul(h, w_in_t[l], "nt", F32, f"proj_{l}")
        c_pad = _forget_cumsum(proj, b_pad, f"forget_cumsum_{l}")
        c_row = c_pad[:, :N_HEADS].T[:, None, :]
        riders = [(("gather", 0), p_rest[l][0]), (("gather", 1), p_rest[l][1]), (("gather", 0), p_rest[l][2])]
        riders += [(("gather", 0), p_in[l + 1])] if l + 1 < depth else []
        y_b, e_row, *arrived = _attn_fwd_t(proj, c_pad, c_row, f"attn_fwd_{l}", comm=riders)
        w_rest[l] = (arrived[0].reshape(D_MODEL, D_MODEL), arrived[1].reshape(2 * D_FF, D_MODEL),
                     arrived[2].reshape(D_FF, D_MODEL))
        if l + 1 < depth:
            w_in_t[l + 1] = _unpack_in(arrived[3])
        w_o, w_gu_t, w_d = w_rest[l]
        merged = _mix_fwd(proj, y_b, w_spatial[l], b_cols, vec(sgu_norm_g, l), f"mix_fwd_{l}")
        o, x1, h2 = _matmul_post_norm(merged, w_o, xl, vec(mix_post_g, l), f"out_proj_{l}", g_next=vec(ffn_pre_g, l))
        gate, up, act = _ffn_in(h2, w_gu_t, f"ffn_gu_{l}")
        if l + 1 < depth:
            dn, x_next, h_next = _matmul_post_norm(act, w_d, x1, vec(ffn_post_g, l), f"ffn_down_{l}",
                                                   g_next=vec(mix_pre_g, l + 1))
        else:
            dn, dy, loss = _matmul_post_norm(act, w_d, x1, vec(ffn_post_g, l), f"ffn_down_{l}", target=target)
        saved.append(dict(x0=xl, h=h, proj=proj, b_pad=b_pad, b_cols=b_cols, c_pad=c_pad, y_b=y_b, e_row=e_row,
                          merged=merged, o=o, x1=x1, h2=h2, gate=gate, up=up, act=act, dn=dn))
        if l + 1 < depth:
            xl, h = x_next, h_next

    g_small = [None] * depth
    flights = {}
    late_token = None
    dx = dy
    dd, dg_ffn_post = _norm_bwd(dx, f"bwd_ffn_post_{depth - 1}", post=(saved[-1]["dn"], vec(ffn_post_g, depth - 1)))
    for l in reversed(range(depth)):
        s = saved[l]
        w_o, w_gu_t, w_d = w_rest[l]
        dgu = _ffn_in_bwd(dd, w_d, s["gate"], s["up"], f"d_act_{l}")
        dw_d = _matmul(s["act"], dd, "tn", BF16, f"dw_down_{l}")
        dw_gu_t = _matmul(dgu, s["h2"], "tn", BF16, f"dw_gu_{l}", halves=True)
        dx1, dg_ffn_pre, do, dg_mix_post = _matmul_norm_bwd(
            dgu, w_gu_t, dx, s["x1"], vec(ffn_pre_g, l), f"d_h2_{l}", post=(s["o"], vec(mix_post_g, l)), halves=True)
        dmerged = _matmul(do, w_o, "nt", F32, f"d_merged_{l}")
        dw_o = _matmul(s["merged"], do, "tn", BF16, f"dw_out_{l}")
        tokens = []
        for key, dw, base, rows in [("out", dw_o, 0, SH_OUT), ("gate", dw_gu_t, 0, SH_FF), ("up", dw_gu_t, D_FF, SH_FF),
                                    ("down", dw_d, 0, SH_FF)]:
            flights[(key, l)], token = _exchange_start(("windows", base, rows, rows), dw, f"start_grad_{key}_{l}")
            tokens.append(token)
        dproj, dyb, delta_b, dw_s, dbt, dg_v = _mix_bwd(
            s["proj"], s["y_b"], dmerged, w_spatial[l], s["b_cols"], vec(sgu_norm_g, l), sel, f"mix_bwd_{l}",
            deps=tokens)
        db_s = dbt[:, :N_GROUPS].T.reshape(1, D_MODEL)
        tokens = []
        if l == 0:
            g_small[0] = jnp.concatenate([zero_row, dg_v, dg_mix_post, dg_ffn_pre, dg_ffn_post, db_s, zero_row,
                                          zero_row, dw_s.reshape(CHUNK, D_MODEL)], axis=0)
            flights["small"], token = _exchange_start(("spread",),jnp.concatenate([g_small[1], g_small[0]], axis=0),
                                                      "start_small_grads")
            tokens = [token]
        dproj, dc_key, dc_query = _attn_bwd_t(s["proj"], dyb, s["e_row"], delta_b, s["c_pad"], dproj,
                                              f"attn_bwd_{l}", deps=tokens)
        dc = jnp.pad((dc_key + dc_query)[:, 0, :].T, ((0, 0), (0, HEAD_DIM - N_HEADS)))
        dproj, db_f = _forget_bwd_t(s["proj"], s["b_pad"], dc, dproj, f"forget_bwd_{l}")
        db_f_row = jnp.pad(db_f, ((0, 0), (0, D_MODEL - HEAD_DIM)))
        dw_in_t = _matmul(dproj, s["h"], "tn", BF16, f"dw_in_{l}")
        flights[("in", l)], token = _exchange_start(("windows", 0, IN_STRIDE, P_IN), _pack_grad_in(dw_in_t),
                                                    f"start_grad_in_{l}")
        if l > 0:
            dx, dg_mix_pre, dd, dg_ffn_post_below = _matmul_norm_bwd(
                dproj, w_in_t[l], dx1, s["x0"], vec(mix_pre_g, l), f"d_h_{l}",
                post=(saved[l - 1]["dn"], vec(ffn_post_g, l - 1)), deps=[token])
            loss_row = jnp.pad(loss, ((0, 0), (0, D_MODEL - 1)))
            g_small[l] = jnp.concatenate([dg_mix_pre, dg_v, dg_mix_post, dg_ffn_pre, dg_ffn_post, db_s, db_f_row,
                                          loss_row, dw_s.reshape(CHUNK, D_MODEL)], axis=0)
            dg_ffn_post = dg_ffn_post_below
        else:
            dx, dg_mix_pre = _matmul_norm_bwd(dproj, w_in_t[l], dx1, s["x0"], vec(mix_pre_g, l), f"d_h_{l}",
                                              deps=[token])
            late_rows = jnp.concatenate([dg_mix_pre, db_f_row] + [zero_row] * 6, axis=0)
            flights["late"], late_token = _exchange_start(("spread",),late_rows, "start_late_small_grads")
    grad_x = dx.reshape(x.shape)

    landed, after = {}, [grad_x, late_token]
    rest = ["out", "gate", "up", "down"]
    for key in [(n, 1) for n in rest] + [("in", 1)] + [(n, 0) for n in rest] + ["small"]:
        label = key if isinstance(key, str) else f"grad_{key[0]}_{key[1]}"
        landed[key] = _exchange_wait(flights[key], after, f"wait_{label}")
        after = [landed[key]]
    small_sum = _sum_slots(landed["small"], S_ROWS, "sum_small_grads")
    sm = small_sum.reshape(depth, S_ROWS, D_MODEL)[::-1]
    loss = sm[depth - 1, 7, 0]
    g = {
        "sgu_norm_g": sm[:, 1], "mix_post_g": sm[:, 2],
        "ffn_pre_g": sm[:, 3], "ffn_post_g": sm[:, 4], "b_spatial": sm[:, 5].reshape(depth, N_GROUPS, CHUNK),
        "w_spatial": sm[:, 8:].reshape(depth, N_GROUPS, CHUNK, CHUNK),
    }
    for n in rest:
        g["w_" + n] = jnp.stack([_sum_slots(landed[(n, l)], landed[(n, l)].shape[1], f"sum_grad_{n}_{l}")
                                 for l in range(depth)])

    names = ["mix_pre_g", "w_in", "b_forget", "sgu_norm_g", "w_spatial", "b_spatial", "w_out", "mix_post_g",
             "ffn_pre_g", "w_gate", "w_up", "w_down", "ffn_post_g"]
    ws = dict(mix_pre_g=mix_pre_g, w_in=w_in, b_forget=b_forget, sgu_norm_g=sgu_norm_g, w_spatial=w_spatial,
              b_spatial=b_spatial, w_out=w_out, mix_post_g=mix_post_g, ffn_pre_g=ffn_pre_g, w_gate=w_gate, w_up=w_up,
              w_down=w_down, ffn_post_g=ffn_post_g)
    ms = dict(mix_pre_g=m_mix_pre_g, w_in=m_w_in, b_forget=m_b_forget, sgu_norm_g=m_sgu_norm_g, w_spatial=m_w_spatial,
              b_spatial=m_b_spatial, w_out=m_w_out, mix_post_g=m_mix_post_g, ffn_pre_g=m_ffn_pre_g, w_gate=m_w_gate,
              w_up=m_w_up, w_down=m_w_down, ffn_post_g=m_ffn_post_g)
    vs = dict(mix_pre_g=v_mix_pre_g, w_in=v_w_in, b_forget=v_b_forget, sgu_norm_g=v_sgu_norm_g, w_spatial=v_w_spatial,
              b_spatial=v_b_spatial, w_out=v_w_out, mix_post_g=v_mix_post_g, ffn_pre_g=v_ffn_pre_g, w_gate=v_w_gate,
              w_up=v_w_up, w_down=v_w_down, ffn_post_g=v_ffn_post_g)
    updates = {}
    last = ["w_in", "mix_pre_g", "b_forget"]
    transposed = ["w_in", "w_gate", "w_up"]
    for n in [n for n in names if n not in last] + last:
        if n == last[0]:
            sum_in_1 = _sum_slots(landed[("in", 1)], 304, "sum_grad_in_1")
            done = [u[0] for u in updates.values()] + [sum_in_1]
            sum_in_0 = _sum_slots(_exchange_wait(flights[("in", 0)], done, "wait_grad_in_0"), 304, "sum_grad_in_0")
            late = _sum_slots(_exchange_wait(flights["late"], [sum_in_0], "wait_late"), 8, "sum_late_small_grads")
            g["w_in"] = lax.dynamic_slice(jnp.stack([sum_in_0, sum_in_1]), (0, me, 0), (depth, SH_IN, D_MODEL))
            g["mix_pre_g"] = jnp.stack([late[0], sm[1, 0]])
            g["b_forget"] = jnp.stack([late[1, :N_HEADS], sm[1, 6, :N_HEADS]])
        if n in transposed:
            view = lambda a: jnp.swapaxes(a, 1, 2)
            updates[n] = [view(u) for u in _adamw_cols(view(ws[n]), g[n], view(ms[n]), view(vs[n]), f"adamw_{n}")]
            g[n] = view(g[n])
        else:
            flat = (-1, ws[n].shape[-1])
            updates[n] = [u.reshape(ws[n].shape) for u in
                          _adamw(ws[n].reshape(flat), g[n].reshape(flat), ms[n].reshape(flat), vs[n].reshape(flat),
                                 f"adamw_{n}")]
    deltas = [updates[n][0] for n in names]
    new_m = [updates[n][1] for n in names]
    new_v = [updates[n][2] for n in names]
    grads = [g[n].reshape(ws[n].shape) for n in names]
    return (loss, grad_x, *grads, *deltas, *new_m, *new_v)
```

```python
import math

import jax
import jax.numpy as jnp
from jax import lax
from jax.experimental import pallas as pl
from jax.experimental.pallas import tpu as pltpu

F32 = jnp.float32
BF16 = jnp.bfloat16

N_DEV = 8
D_MODEL = 1024
N_HEADS = 8
HEAD_DIM = 128
CHUNK = 128
N_GROUPS = 8
D_FF = 2816
IN_WIDTH = 7 * D_MODEL + N_HEADS
IN_PAD = 7680
EPS = 1e-6
GATED = 4 * D_MODEL
F_WIDTH = 512
QKV_WIDTH = 3 * HEAD_DIM
CB_F = GATED // HEAD_DIM
CB_QKV = (GATED + F_WIDTH) // HEAD_DIM
assert GATED + F_WIDTH + N_HEADS * QKV_WIDTH == IN_PAD and (GATED + F_WIDTH) % QKV_WIDTH == 0

ADAM_LR, ADAM_B1, ADAM_B2, ADAM_EPS, ADAM_WD, ADAM_STEP = 0.001, 0.9, 0.999, 1e-08, 0.01, 10

SH_IN = IN_WIDTH // N_DEV
SH_OUT = D_MODEL // N_DEV
SH_FF = D_FF // N_DEV
P_IN = 912
IN_STRIDE = 7 * HEAD_DIM
IN_ROWS = IN_STRIDE * (N_DEV - 1) + P_IN
S_ROWS = 136

ATTN_BLOCK = 1024
ROW_BLOCK = 512
MIX_ROWS = 1024
MIX_BWD_ROWS = 256
MM_TM, MM_TN, MM_TK = 1536, 1536, 2048
VMEM_CAP = 56 << 20
NEG = -1e30


def _tile(n, cap, unit=128):
    if n <= cap:
        return n
    best = None
    for t in range(unit, cap + 1, unit):
        if n % t == 0:
            best = t
    assert best is not None, (n, cap)
    return best


def _nbytes(shape, dtype):
    return math.prod(shape) * jnp.dtype(dtype).itemsize


_HBM = pl.BlockSpec(memory_space=pltpu.HBM)
MESH = pl.DeviceIdType.MESH

_N_REMOTE = N_DEV - 1


def _exchange_shapes(kind, x):
    if kind[0] == "gather":
        shape = x.shape[:kind[1]] + (N_DEV,) + x.shape[kind[1]:]
    elif kind[0] == "spread":
        shape = (N_DEV,) + x.shape
    else:
        shape = (N_DEV, kind[3]) + x.shape[1:]
    return [jax.ShapeDtypeStruct(shape, x.dtype)]


def _exchange_sems(kind):
    del kind
    return [pltpu.SemaphoreType.DMA((_N_REMOTE,)), pltpu.SemaphoreType.DMA((_N_REMOTE,)), pltpu.SemaphoreType.DMA]


def _plan(kind, x_ref, outs, sems):
    send_sems, recv_sems, local_sem = sems
    x, y, c = lax.axis_index("x"), lax.axis_index("y"), lax.axis_index("c")

    def remote(src, dst, k, to):
        return pltpu.make_async_remote_copy(src_ref=src, dst_ref=dst, send_sem=send_sems.at[k], recv_sem=recv_sems.at[k],
                                            device_id=to, device_id_type=MESH)

    sibling = (x, y, 1 - c)
    if kind[0] == "gather":
        out, = outs
        slot = lambda px, py, pc: out.at[(slice(None),) * kind[1] + (4 * px + 2 * py + pc,)]
        chips = [(1 - x, y), (x, 1 - y), (1 - x, 1 - y)]
        local = [pltpu.make_async_copy(x_ref, slot(x, y, c), local_sem)]
        first = [remote(x_ref, slot(x, y, c), 0, sibling)]
        first += [remote(x_ref, slot(x, y, c), 1 + k, (*chip, c)) for k, chip in enumerate(chips)]
        relays = [(remote(x_ref, slot(*chip, c), 1 + k, (*chip, c)), remote(slot(*chip, c), slot(*chip, c), 4 + k, sibling))
                  for k, chip in enumerate(chips)]
        arrivals = [remote(x_ref, slot(x, y, 1 - c), 0, sibling)]
        arrivals += [remote(x_ref, slot(*chip, 1 - c), 4 + k, sibling) for k, chip in enumerate(chips)]
        return local, first, relays, arrivals
    out, = outs
    mine = 4 * x + 2 * y + c
    if kind[0] == "spread":
        src = lambda d: x_ref
    else:
        _, base, stride, length = kind
        src = lambda d: x_ref.at[pl.ds(pl.multiple_of(base + stride * d, 16), length)]
    first, arrivals = [], []
    for j in range(1, N_DEV):
        peer = (1 - x if j & 4 else x, 1 - y if j & 2 else y, 1 - c if j & 1 else c)
        theirs = 4 * peer[0] + 2 * peer[1] + peer[2]
        first.append(remote(src(theirs), out.at[mine], j - 1, peer))
        arrivals.append(remote(src(theirs), out.at[theirs], j - 1, peer))
    local = [] if local_sem is None else [pltpu.make_async_copy(src(mine), out.at[mine], local_sem)]
    return local, first, [], arrivals


def _start(plan):
    local, first, _, _ = plan
    for cp in local + first:
        cp.start()


def _finish(plan):
    local, first, relays, arrivals = plan
    for arrival, onward in relays:
        arrival.wait_recv()
        onward.start()
    for cp in arrivals:
        cp.wait_recv()
    for cp in first + [onward for _, onward in relays]:
        cp.wait_send()
    for cp in local:
        cp.wait()


def _direct_copies(kind, x_ref, land_ref, send_sems, recv_sems):
    _, first, _, arrivals = _plan(kind, x_ref, [land_ref], (send_sems, recv_sems, None))
    return first, arrivals


_SEM = pl.BlockSpec(memory_space=pltpu.SEMAPHORE)
_DATAFLOW = pltpu.SideEffectType.DATAFLOW_SIDE_EFFECTING


def _exchange_start(kind, x, name):
    me = 4 * lax.axis_index("x") + 2 * lax.axis_index("y") + lax.axis_index("c")
    own = x if kind[0] == "spread" else lax.dynamic_slice_in_dim(x, kind[1] + kind[2] * me, kind[3], 0)
    shape = _exchange_shapes(kind, x)[0].shape
    land = lax.dynamic_update_slice_in_dim(lax.empty(shape, x.dtype), own[None], me, 0)

    def body(x_ref, land_ref, send_sems, recv_sems, x_thru, land_thru, token):
        del x_thru, land_thru
        for cp in _direct_copies(kind, x_ref, land_ref, send_sems, recv_sems)[0]:
            cp.start()
        token[...] = jnp.zeros_like(token)

    *handle, token = pl.pallas_call(
        body, name=name,
        out_shape=(pltpu.SemaphoreType.DMA((_N_REMOTE,)), pltpu.SemaphoreType.DMA((_N_REMOTE,)),
                   pltpu.HBM(x.shape, x.dtype), pltpu.HBM(land.shape, land.dtype), jax.ShapeDtypeStruct((8, 128), F32)),
        in_specs=(_HBM, _HBM), out_specs=(_SEM, _SEM, _HBM, _HBM, pl.BlockSpec(memory_space=pltpu.VMEM)),
        input_output_aliases={0: 2, 1: 3}, compiler_params=pltpu.CompilerParams(has_side_effects=_DATAFLOW),
    )(pltpu.with_memory_space_constraint(x, pltpu.HBM), pltpu.with_memory_space_constraint(land, pltpu.HBM))
    return (kind, *handle), token


def _exchange_wait(handle, after, name):
    kind, send_sems, recv_sems, x_thru, land_thru = handle

    def body(x_ref, land_ref, send_sems, recv_sems, *rest):
        first, arrivals = _direct_copies(kind, x_ref, land_ref, send_sems, recv_sems)
        for cp in first:
            cp.wait_send()
        for cp in arrivals:
            cp.wait_recv()

    return pl.pallas_call(
        body, name=name,
        out_shape=(pltpu.HBM(x_thru.shape, x_thru.dtype), pltpu.HBM(land_thru.shape, land_thru.dtype)),
        in_specs=(_HBM, _HBM, _SEM, _SEM) + (pl.BlockSpec(memory_space=pl.ANY),) * len(after), out_specs=(_HBM, _HBM),
        input_output_aliases={0: 0, 1: 1}, compiler_params=pltpu.CompilerParams(has_side_effects=_DATAFLOW),
    )(x_thru, land_thru, send_sems, recv_sems, *after)[1]


def _exchange(kind, x, name):
    n_out = len(_exchange_shapes(kind, x))

    def body(x_ref, *refs):
        plan = _plan(kind, x_ref, refs[:n_out], refs[n_out:])
        _start(plan)
        _finish(plan)

    return pl.pallas_call(body, name=name, out_shape=_exchange_shapes(kind, x), in_specs=[_HBM],
                          out_specs=[_HBM] * n_out, scratch_shapes=_exchange_sems(kind))(x)


def _call(body, name, grid, in_specs, out_specs, out_shape, args, scratch=(), sem=None, nbytes=0, aliases=None,
          comm=(), deps=()):
    in_specs, out_specs, out_shape, args, scratch = (list(in_specs), list(out_specs), list(out_shape), list(args),
                                                     list(scratch))
    if deps:
        n_real, n_deps, unordered = len(args), len(deps), body

        def body(*refs):
            unordered(*refs[:n_real], *refs[n_real + n_deps:])

        in_specs += [pl.BlockSpec(memory_space=pl.ANY)] * n_deps
        args += list(deps)
    if comm:
        n_in, n_out, n_scr, n_ops = len(args), len(out_shape), len(scratch), len(comm)
        kinds = [kind for kind, _ in comm]
        shapes = [_exchange_shapes(kind, x) for kind, x in comm]
        inner = body

        def body(*refs):
            ins, refs = refs[:n_in], refs[n_in:]
            cins, refs = refs[:n_ops], refs[n_ops:]
            outs, refs = refs[:n_out], refs[n_out:]
            couts = []
            for sh in shapes:
                couts.append(refs[:len(sh)])
                refs = refs[len(sh):]
            own_scratch, sems = refs[:n_scr], refs[n_scr:]
            first = last = None
            for axis, size in enumerate(grid):
                at_start, at_end = pl.program_id(axis) == 0, pl.program_id(axis) == size - 1
                first = at_start if first is None else first & at_start
                last = at_end if last is None else last & at_end
            plans = [_plan(kinds[o], cins[o], couts[o], sems[3 * o:3 * o + 3]) for o in range(n_ops)]

            @pl.when(first)
            def _():
                for p in plans:
                    _start(p)

            inner(*ins, *outs, *own_scratch)

            @pl.when(last)
            def _():
                for p in plans:
                    _finish(p)

        in_specs += [_HBM] * n_ops
        args += [x for _, x in comm]
        for kind, sh in zip(kinds, shapes):
            out_shape += sh
            out_specs += [_HBM] * len(sh)
            scratch += _exchange_sems(kind)
        sem = ("arbitrary",) * len(grid)
    limit = int(min(max(2 * nbytes + (8 << 20), 32 << 20), VMEM_CAP))
    return pl.pallas_call(
        body, name=name, grid=grid, in_specs=in_specs, out_specs=out_specs, out_shape=out_shape,
        scratch_shapes=scratch, input_output_aliases=aliases or {},
        compiler_params=pltpu.CompilerParams(dimension_semantics=sem, vmem_limit_bytes=limit),
    )(*args)


def _sum_slots(x, tr, name):
    n, r, c = x.shape

    def body(x_ref, o_ref):
        acc = x_ref[0].astype(F32)
        for d in range(1, n):
            acc = acc + x_ref[d].astype(F32)
        o_ref[...] = acc

    return _call(body, name, (r // tr,), [pl.BlockSpec((n, tr, c), lambda i: (0, i, 0))],
                 [pl.BlockSpec((tr, c), lambda i: (i, 0))], [jax.ShapeDtypeStruct((r, c), F32)], [x],
                 sem=("parallel",), nbytes=_nbytes((n, tr, c), x.dtype) + _nbytes((tr, c), F32))[0]


_DIMS = {"nn": (((1,), (0,)), ((), ())), "nt": (((1,), (1,)), ((), ())), "tn": (((0,), (0,)), ((), ()))}


def _matmul(a, b, mode, out_dtype, name, comm=(), deps=(), halves=False):
    a_shape = (a.shape[1], 2 * a.shape[2]) if halves else a.shape
    if mode == "nn":
        (m, k), (k2, n) = a_shape, b.shape
    elif mode == "nt":
        (m, k), (n, k2) = a_shape, b.shape
    else:
        (k, m), (k2, n) = a_shape, b.shape
    assert k == k2, (a.shape, b.shape, mode)
    tm, tn, tk = _tile(m, MM_TM), _tile(n, MM_TN), _tile(k, MM_TK)
    if halves and mode == "nn":
        tk = _tile(k // 2, MM_TK)
    if halves and mode == "tn":
        tm = _tile(m // 2, MM_TM)
    nk = k // tk
    dims = _DIMS[mode]
    if halves:
        per_half = (k // 2) // tk if mode == "nn" else (m // 2) // tm
        a_spec = {"nn": pl.BlockSpec((None, tm, tk), lambda i, j, kk: (kk // per_half, i, kk % per_half)),
                  "tn": pl.BlockSpec((None, tk, tm), lambda i, j, kk: (i // per_half, kk, i % per_half))}[mode]
    else:
        a_spec = {"nn": pl.BlockSpec((tm, tk), lambda i, j, kk: (i, kk)),
                  "nt": pl.BlockSpec((tm, tk), lambda i, j, kk: (i, kk)),
                  "tn": pl.BlockSpec((tk, tm), lambda i, j, kk: (kk, i))}[mode]
    b_spec = {"nn": pl.BlockSpec((tk, tn), lambda i, j, kk: (kk, j)),
              "nt": pl.BlockSpec((tn, tk), lambda i, j, kk: (j, kk)),
              "tn": pl.BlockSpec((tk, tn), lambda i, j, kk: (kk, j))}[mode]

    def partial_product(a_ref, b_ref):
        return lax.dot_general(a_ref[...].astype(BF16), b_ref[...].astype(BF16), dims, preferred_element_type=F32)

    if nk == 1:
        def body(a_ref, b_ref, o_ref):
            o_ref[...] = partial_product(a_ref, b_ref).astype(o_ref.dtype)
        scratch = []
    else:
        def body(a_ref, b_ref, o_ref, acc_ref):
            kk = pl.program_id(2)

            @pl.when(kk == 0)
            def _():
                acc_ref[...] = jnp.zeros_like(acc_ref)

            acc_ref[...] += partial_product(a_ref, b_ref)

            @pl.when(kk == nk - 1)
            def _():
                o_ref[...] = acc_ref[...].astype(o_ref.dtype)
        scratch = [pltpu.VMEM((tm, tn), F32)]

    nbytes = (_nbytes((tm, tk), a.dtype) + _nbytes((tk, tn), b.dtype) + _nbytes((tm, tn), out_dtype)
              + _nbytes((tm, tn), F32))
    res = _call(body, name, (m // tm, n // tn, nk), [a_spec, b_spec],
                [pl.BlockSpec((tm, tn), lambda i, j, kk: (i, j))], [jax.ShapeDtypeStruct((m, n), out_dtype)], [a, b],
                scratch=scratch, sem=("parallel", "parallel", "arbitrary"), nbytes=nbytes, comm=comm, deps=deps)
    return res if comm else res[0]


def _matmul_post_norm(a, b, x, g_post, name, g_next=None, target=None):
    (m, k), (k2, d) = a.shape, b.shape
    assert k == k2 and d == D_MODEL
    with_loss = target is not None
    tm, tk = _tile(m, ROW_BLOCK if with_loss else 2 * ROW_BLOCK), _tile(k, MM_TK)
    nk = k // tk

    def body(a_ref, b_ref, x_ref, gp_ref, last_ref, o_ref, first_ref, second_ref, *rest):
        acc_ref = rest[-1]
        i, kk = pl.program_id(0), pl.program_id(1)

        @pl.when(kk == 0)
        def _():
            acc_ref[...] = jnp.zeros_like(acc_ref)

        acc_ref[...] += jnp.dot(a_ref[...], b_ref[...], preferred_element_type=F32)

        @pl.when(kk == nk - 1)
        def _():
            ov = acc_ref[...]
            o_ref[...] = ov
            xn = x_ref[...] + (ov * _rms(ov)) * gp_ref[...]
            if with_loss:
                dd_ref, dg_ref = rest[:2]
                err = xn - last_ref[...]
                dy = err / d
                first_ref[...] = dy
                dd, dg_rows = _rms_bwd(dy, ov, gp_ref[...])
                dd_ref[...] = dd.astype(BF16)

                @pl.when(i == 0)
                def _():
                    second_ref[...] = jnp.zeros_like(second_ref)
                    dg_ref[...] = jnp.zeros_like(dg_ref)

                second_ref[...] += 0.5 * jnp.sum(jnp.mean(err * err, axis=-1, keepdims=True), axis=0, keepdims=True)
                dg_ref[...] += jnp.sum(dg_rows, axis=0, keepdims=True)
            else:
                first_ref[...] = xn
                second_ref[...] = ((xn * _rms(xn)) * last_ref[...]).astype(BF16)

    rows = pl.BlockSpec((tm, d), lambda i, kk: (i, 0))
    vec = pl.BlockSpec((1, d), lambda i, kk: (0, 0))
    second = ((pl.BlockSpec((1, 1), lambda i, kk: (0, 0)), jax.ShapeDtypeStruct((1, 1), F32)) if with_loss
              else (rows, jax.ShapeDtypeStruct((m, d), BF16)))
    nbytes = _nbytes((tm, tk), BF16) + _nbytes((tk, d), BF16) + 6 * _nbytes((tm, d), F32)
    extra_specs = [rows, vec] if with_loss else []
    extra_shapes = [jax.ShapeDtypeStruct((m, d), BF16), jax.ShapeDtypeStruct((1, d), F32)] if with_loss else []
    return _call(body, name, (m // tm, nk),
                 [pl.BlockSpec((tm, tk), lambda i, kk: (i, kk)), pl.BlockSpec((tk, d), lambda i, kk: (kk, 0)), rows, vec,
                  rows if with_loss else vec],
                 [rows, rows, second[0]] + extra_specs,
                 [jax.ShapeDtypeStruct((m, d), F32), jax.ShapeDtypeStruct((m, d), F32), second[1]] + extra_shapes,
                 [a, b, x, g_post, target if with_loss else g_next],
                 scratch=[pltpu.VMEM((tm, d), F32)], sem=("arbitrary", "arbitrary"), nbytes=nbytes)


def _matmul_norm_bwd(a, b, dskip, xin, g_pre, name, post=None, halves=False, deps=()):
    m, k = (a.shape[1], 2 * a.shape[2]) if halves else a.shape
    (k2, d) = b.shape
    assert k == k2 and d == D_MODEL
    tm = _tile(m, ROW_BLOCK)
    tk = _tile(k // 2 if halves else k, MM_TK)
    nk = k // tk
    n_in = 5 + (2 if post else 0)

    def body(*refs):
        (a_ref, b_ref, dskip_ref, xin_ref, gpre_ref), rest = refs[:5], list(refs[5:n_in])
        outs, acc_ref = list(refs[n_in:-1]), refs[-1]
        i, kk = pl.program_id(0), pl.program_id(1)

        @pl.when(kk == 0)
        def _():
            acc_ref[...] = jnp.zeros_like(acc_ref)

        acc_ref[...] += jnp.dot(a_ref[...], b_ref[...], preferred_element_type=F32)

        @pl.when(kk == nk - 1)
        def _():
            dxin, dg_rows = _rms_bwd(acc_ref[...], xin_ref[...], gpre_ref[...])
            dx = dskip_ref[...] + dxin
            dx_ref, dgpre_ref = outs[0], outs[1]
            dx_ref[...] = dx

            @pl.when(i == 0)
            def _():
                for ref in outs[1::2]:
                    ref[...] = jnp.zeros_like(ref)

            dgpre_ref[...] += jnp.sum(dg_rows, axis=0, keepdims=True)
            if post:
                da, dg_rows = _rms_bwd(dx, rest[0][...], rest[1][...])
                outs[2][...] = da.astype(BF16)
                outs[3][...] += jnp.sum(dg_rows, axis=0, keepdims=True)

    if halves:
        per_half = (k // 2) // tk
        a_spec = pl.BlockSpec((None, tm, tk), lambda i, kk: (kk // per_half, i, kk % per_half))
    else:
        a_spec = pl.BlockSpec((tm, tk), lambda i, kk: (i, kk))
    rows = pl.BlockSpec((tm, d), lambda i, kk: (i, 0))
    vec = pl.BlockSpec((1, d), lambda i, kk: (0, 0))
    args, in_specs = [a, b, dskip, xin, g_pre], [a_spec, pl.BlockSpec((tk, d), lambda i, kk: (kk, 0)), rows, rows, vec]
    out_specs, out_shape = [rows, vec], [jax.ShapeDtypeStruct((m, d), F32), jax.ShapeDtypeStruct((1, d), F32)]
    if post:
        args += list(post)
        in_specs += [rows, vec]
        out_specs += [rows, vec]
        out_shape += [jax.ShapeDtypeStruct((m, d), BF16), jax.ShapeDtypeStruct((1, d), F32)]
    nbytes = _nbytes((tm, tk), BF16) + _nbytes((tk, d), BF16) + 7 * _nbytes((tm, d), F32)
    return _call(body, name, (m // tm, nk), in_specs, out_specs, out_shape, args, scratch=[pltpu.VMEM((tm, d), F32)],
                 sem=("arbitrary", "arbitrary"), nbytes=nbytes, deps=deps)


def _rms(x):
    return lax.rsqrt(jnp.mean(x * x, axis=-1, keepdims=True) + EPS)


def _rms_bwd(dz, a, g):
    r = _rms(a)
    dzg = dz * g
    da = r * dzg - a * (r * r * r) * jnp.mean(dzg * a, axis=-1, keepdims=True)
    return da, dz * (a * r)


def _row_spec(tr, width):
    return pl.BlockSpec((tr, width), lambda i: (i, 0))


def _vec_spec(width):
    return pl.BlockSpec((1, width), lambda i: (0, 0))


def _rms_fwd(x, g, name):
    t, d = x.shape
    tr = min(ROW_BLOCK, t)

    def body(x_ref, g_ref, h_ref):
        xv = x_ref[...]
        h_ref[...] = ((xv * _rms(xv)) * g_ref[...]).astype(BF16)

    return _call(body, name, (t // tr,), [_row_spec(tr, d), _vec_spec(d)], [_row_spec(tr, d)],
                 [jax.ShapeDtypeStruct((t, d), BF16)], [x, g], sem=("parallel",), nbytes=3 * _nbytes((tr, d), F32))[0]


def _sigmoid(x):
    return 1.0 / (1.0 + jnp.exp(-x))


def _ffn_in(h, w_gu_t, name):
    t, d = h.shape
    tm, tn = _tile(t, 1024), _tile(D_FF, MM_TN)
    n_tiles = D_FF // tn

    def body(h_ref, wg_ref, wu_ref, g_ref, u_ref, a_ref):
        hv = h_ref[...]
        gate = lax.dot_general(hv, wg_ref[...], _DIMS["nt"], preferred_element_type=F32).astype(BF16)
        up = lax.dot_general(hv, wu_ref[...], _DIMS["nt"], preferred_element_type=F32).astype(BF16)
        g_ref[...] = gate
        u_ref[...] = up
        gate, up = gate.astype(F32), up.astype(F32)
        a_ref[...] = ((gate * _sigmoid(gate)) * up).astype(BF16)

    tile = pl.BlockSpec((tm, tn), lambda i, j: (i, j))
    shape = jax.ShapeDtypeStruct((t, D_FF), BF16)
    return _call(body, name, (t // tm, n_tiles),
                 [pl.BlockSpec((tm, d), lambda i, j: (i, 0)), pl.BlockSpec((tn, d), lambda i, j: (j, 0)),
                  pl.BlockSpec((tn, d), lambda i, j: (j + n_tiles, 0))],
                 [tile, tile, tile], [shape, shape, shape], [h, w_gu_t, w_gu_t], sem=("parallel", "parallel"),
                 nbytes=_nbytes((tm, d), BF16) + 2 * _nbytes((tn, d), BF16) + 5 * _nbytes((tm, tn), F32))


def _ffn_in_bwd(dd, w_d, gate, up, name):
    t, d = dd.shape
    tm, tn = _tile(t, 1024), _tile(D_FF, MM_TN)

    def body(dd_ref, w_ref, g_ref, u_ref, o_ref):
        da = lax.dot_general(dd_ref[...], w_ref[...], _DIMS["nt"], preferred_element_type=F32)
        gate, up = g_ref[...].astype(F32), u_ref[...].astype(F32)
        s = _sigmoid(gate)
        o_ref[0] = (da * up * (s * (1.0 + gate * (1.0 - s)))).astype(BF16)
        o_ref[1] = (da * (gate * s)).astype(BF16)

    tile = pl.BlockSpec((tm, tn), lambda i, j: (i, j))
    return _call(body, name, (t // tm, D_FF // tn),
                 [pl.BlockSpec((tm, d), lambda i, j: (i, 0)), pl.BlockSpec((tn, d), lambda i, j: (j, 0)), tile, tile],
                 [pl.BlockSpec((2, tm, tn), lambda i, j: (0, i, j))], [jax.ShapeDtypeStruct((2, t, D_FF), BF16)],
                 [dd, w_d, gate, up], sem=("parallel", "parallel"),
                 nbytes=_nbytes((tm, d), BF16) + _nbytes((tn, d), BF16) + 6 * _nbytes((tm, tn), F32))[0]


def _split3(x):
    hi = x.astype(BF16)
    r1 = x - hi.astype(F32)
    mid = r1.astype(BF16)
    lo = (r1 - mid.astype(F32)).astype(BF16)
    return hi, mid, lo


def _dot_exact(mat01, x):
    hi, mid, lo = _split3(x)
    out = jnp.dot(mat01, hi, preferred_element_type=F32)
    out = out + jnp.dot(mat01, mid, preferred_element_type=F32)
    return out + jnp.dot(mat01, lo, preferred_element_type=F32)


def _dot_exact_rhs(x, mat01):
    hi, mid, lo = _split3(x)
    out = jnp.dot(hi, mat01, preferred_element_type=F32)
    out = out + jnp.dot(mid, mat01, preferred_element_type=F32)
    return out + jnp.dot(lo, mat01, preferred_element_type=F32)


def _tri(lower):
    r = lax.broadcasted_iota(jnp.int32, (CHUNK, CHUNK), 0)
    c = lax.broadcasted_iota(jnp.int32, (CHUNK, CHUNK), 1)
    return jnp.where(r >= c if lower else r <= c, 1.0, 0.0).astype(BF16)


def _log_sigmoid(z):
    return jnp.minimum(z, 0.0) - jnp.log(1.0 + jnp.exp(-jnp.abs(z)))


def _forget_cumsum(proj, b_pad, name):
    t = proj.shape[0]
    nb = t // CHUNK

    def body(f_ref, b_ref, c_ref):
        tri = _tri(True)
        b = b_ref[...]

        def blk(i, carry):
            rows = pl.ds(pl.multiple_of(i * CHUNK, CHUNK), CHUNK)
            cs = _dot_exact(tri, _log_sigmoid(f_ref[rows, :] + b)) + carry
            c_ref[rows, :] = cs
            return cs[CHUNK - 1:CHUNK, :]

        lax.fori_loop(0, nb, blk, jnp.zeros((1, HEAD_DIM), F32))

    return _call(body, name, (1,),
                 [pl.BlockSpec((t, HEAD_DIM), lambda i: (0, CB_F)), pl.BlockSpec((1, HEAD_DIM), lambda i: (0, 0))],
                 [pl.BlockSpec((t, HEAD_DIM), lambda i: (0, 0))], [jax.ShapeDtypeStruct((t, HEAD_DIM), F32)],
                 [proj, b_pad], sem=("arbitrary",), nbytes=2 * _nbytes((t, HEAD_DIM), F32))[0]


def _causal(block):
    r = lax.broadcasted_iota(jnp.int32, (block, block), 0)
    c = lax.broadcasted_iota(jnp.int32, (block, block), 1)
    return c <= r


def _lanes(x, width):
    return jnp.concatenate([x] * (width // HEAD_DIM), axis=1)


def _chunks(t, blk):
    return [slice(r * blk, (r + 1) * blk) for r in range(t // blk)]


def _lane_of_head(x, h):
    lane = lax.broadcasted_iota(jnp.int32, x.shape, 1)
    return jnp.broadcast_to(jnp.sum(jnp.where(lane == h, x, 0.0), axis=1, keepdims=True), x.shape)


def _causal_t(block):
    r = lax.broadcasted_iota(jnp.int32, (block, block), 0)
    c = lax.broadcasted_iota(jnp.int32, (block, block), 1)
    return r <= c


def _attn_fwd_t(proj, c_pad, c_row, name, comm=()):
    t = proj.shape[0]
    blk = min(ATTN_BLOCK, t)
    nq = t // blk
    scale = HEAD_DIM ** -0.5

    def body(q_ref, k_ref, v_ref, cp_ref, cr_ref, y_ref, e_ref, kb_s, vt_s, cb_s):
        h, i = pl.program_id(0), pl.program_id(1)

        @pl.when(i == 0)
        def _():
            for rows in _chunks(t, blk):
                kb_s[rows, :] = k_ref[rows, :].astype(BF16)
                vt_s[:, rows] = v_ref[rows, :].T.astype(BF16)
                cb_s[rows, :] = _lane_of_head(cp_ref[rows, :], h)

        q = (q_ref[...] * scale).astype(BF16)
        ci = cr_ref[:, pl.ds(pl.multiple_of(i * blk, blk), blk)]

        def step(j, carry, diagonal):
            m, l, acc = carry
            rows = pl.ds(pl.multiple_of(j * blk, blk), blk)
            s = lax.dot_general(kb_s[rows, :], q, _DIMS["nt"], preferred_element_type=F32)
            s = s + ci - _lanes(cb_s[rows, :], blk)
            if diagonal:
                s = jnp.where(_causal_t(blk), s, NEG)
            m_new = jnp.maximum(m, jnp.max(s, axis=0, keepdims=True))
            alpha = jnp.exp(m - m_new)
            p = jnp.exp(s - m_new)
            l = alpha * l + jnp.sum(p, axis=0, keepdims=True)
            acc = alpha * acc + jnp.dot(vt_s[:, rows], p.astype(BF16), preferred_element_type=F32)
            return m_new, l, acc

        init = (jnp.full((1, blk), NEG, F32), jnp.zeros((1, blk), F32), jnp.zeros((HEAD_DIM, blk), F32))
        carry = lax.fori_loop(0, i, lambda j, c: step(j, c, False), init)
        m, l, acc = step(i, carry, True)
        y_ref[...] = (acc / l).T
        e_ref[...] = ci - (m + jnp.log(l))

    nbytes = 3 * _nbytes((t, HEAD_DIM), F32) + 4 * _nbytes((blk, HEAD_DIM), F32) + 4 * _nbytes((blk, blk), F32)
    return _call(body, name, (N_HEADS, nq),
                 [pl.BlockSpec((blk, HEAD_DIM), lambda h, i: (i, CB_QKV + 3 * h)),
                  pl.BlockSpec((t, HEAD_DIM), lambda h, i: (0, CB_QKV + 3 * h + 1)),
                  pl.BlockSpec((t, HEAD_DIM), lambda h, i: (0, CB_QKV + 3 * h + 2)),
                  pl.BlockSpec((t, HEAD_DIM), lambda h, i: (0, 0)),
                  pl.BlockSpec((None, 1, t), lambda h, i: (h, 0, 0))],
                 [pl.BlockSpec((blk, HEAD_DIM), lambda h, i: (i, h)),
                  pl.BlockSpec((None, 1, blk), lambda h, i: (h, 0, i))],
                 [jax.ShapeDtypeStruct((t, D_MODEL), F32), jax.ShapeDtypeStruct((N_HEADS, 1, t), F32)],
                 [proj, proj, proj, c_pad, c_row],
                 scratch=[pltpu.VMEM((t, HEAD_DIM), BF16), pltpu.VMEM((HEAD_DIM, t), BF16),
                          pltpu.VMEM((t, HEAD_DIM), F32)],
                 sem=("parallel", "arbitrary"), nbytes=nbytes, comm=comm)


def _attn_bwd_t(proj, dy, e_row, delta_b, c_pad, dproj, name, deps=()):
    t = proj.shape[0]
    blk = min(ATTN_BLOCK, t)
    nb = t // blk
    scale = HEAD_DIM ** -0.5

    def body(q_ref, k_ref, v_ref, dy_ref, e_ref, dl_ref, cp_ref, dproj_in, dqkv_ref, dck_ref, dcq_ref,
             qb_s, qt_s, dyb_s, dyt_s, dl_s, dqt_acc):
        del dproj_in
        h, j = pl.program_id(0), pl.program_id(1)

        @pl.when(j == 0)
        def _():
            for rows in _chunks(t, blk):
                qs = q_ref[rows, :] * scale
                qb_s[rows, :] = qs.astype(BF16)
                qt_s[:, rows] = qs.T.astype(BF16)
                dyr = dy_ref[rows, :]
                dyb_s[rows, :] = dyr.astype(BF16)
                dyt_s[:, rows] = dyr.T.astype(BF16)
                dl_s[:, rows] = dl_ref[rows, :].T[0:1, :]
            dqt_acc[...] = jnp.zeros_like(dqt_acc)
            dcq_ref[...] = jnp.zeros_like(dcq_ref)

        kf = k_ref[...]
        kb, kt = kf.astype(BF16), kf.T.astype(BF16)
        vb = v_ref[...].astype(BF16)
        cj = _lanes(_lane_of_head(cp_ref[...], h), blk)

        def step(i, carry, diagonal):
            dkt, dvt, dkey = carry
            cols = pl.ds(pl.multiple_of(i * blk, blk), blk)
            s = lax.dot_general(kb, qb_s[cols, :], _DIMS["nt"], preferred_element_type=F32) + e_ref[:, cols] - cj
            if diagonal:
                s = jnp.where(_causal_t(blk), s, NEG)
            p = jnp.exp(s)
            dp = lax.dot_general(vb, dyb_s[cols, :], _DIMS["nt"], preferred_element_type=F32)
            ds = p * (dp - dl_s[:, cols])
            pb, dsb = p.astype(BF16), ds.astype(BF16)
            dvt = dvt + lax.dot_general(dyt_s[:, cols], pb, _DIMS["nt"], preferred_element_type=F32)
            dkt = dkt + lax.dot_general(qt_s[:, cols], dsb, _DIMS["nt"], preferred_element_type=F32)
            dqt_acc[:, cols] += jnp.dot(kt, dsb, preferred_element_type=F32) * scale
            dcq_ref[:, cols] += jnp.sum(ds, axis=0, keepdims=True)
            for k in range(blk // HEAD_DIM):
                dkey = dkey + ds[:, k * HEAD_DIM:(k + 1) * HEAD_DIM]
            return dkt, dvt, dkey

        zeros = jnp.zeros((HEAD_DIM, blk), F32)
        carry = step(j, (zeros, zeros, jnp.zeros((blk, HEAD_DIM), F32)), True)
        dkt, dvt, dkey = lax.fori_loop(j + 1, nb, lambda i, c: step(i, c, False), carry)
        mine = pl.ds(pl.multiple_of(j * blk, blk), blk)
        dqkv_ref[:, :HEAD_DIM] = dqt_acc[:, mine].T.astype(BF16)
        dqkv_ref[:, HEAD_DIM:2 * HEAD_DIM] = dkt.T.astype(BF16)
        dqkv_ref[:, 2 * HEAD_DIM:] = dvt.T.astype(BF16)
        dck_ref[...] = -jnp.sum(dkey.T, axis=0, keepdims=True)

    full = lambda cb: pl.BlockSpec((t, HEAD_DIM), lambda h, j: (0, cb(h)))
    head = lambda h: h
    row = pl.BlockSpec((None, 1, t), lambda h, j: (h, 0, 0))
    nbytes = (8 * _nbytes((t, HEAD_DIM), F32) + 4 * _nbytes((blk, HEAD_DIM), F32) + 6 * _nbytes((blk, blk), F32))
    return _call(body, name, (N_HEADS, nb),
                 [full(lambda h: CB_QKV + 3 * h),
                  pl.BlockSpec((blk, HEAD_DIM), lambda h, j: (j, CB_QKV + 3 * h + 1)),
                  pl.BlockSpec((blk, HEAD_DIM), lambda h, j: (j, CB_QKV + 3 * h + 2)),
                  full(head), row, full(head),
                  pl.BlockSpec((blk, HEAD_DIM), lambda h, j: (j, 0)),
                  pl.BlockSpec(memory_space=pl.ANY)],
                 [pl.BlockSpec((blk, QKV_WIDTH), lambda h, j: (j, (GATED + F_WIDTH) // QKV_WIDTH + h)),
                  pl.BlockSpec((None, 1, blk), lambda h, j: (h, 0, j)),
                  row],
                 [jax.ShapeDtypeStruct(dproj.shape, dproj.dtype), jax.ShapeDtypeStruct((N_HEADS, 1, t), F32),
                  jax.ShapeDtypeStruct((N_HEADS, 1, t), F32)],
                 [proj, proj, proj, dy, e_row, delta_b, c_pad, dproj],
                 scratch=[pltpu.VMEM((t, HEAD_DIM), BF16), pltpu.VMEM((HEAD_DIM, t), BF16),
                          pltpu.VMEM((t, HEAD_DIM), BF16), pltpu.VMEM((HEAD_DIM, t), BF16),
                          pltpu.VMEM((1, t), F32), pltpu.VMEM((HEAD_DIM, t), F32)],
                 sem=("parallel", "arbitrary"), nbytes=nbytes, aliases={7: 0}, deps=deps)


def _forget_bwd_t(proj, b_pad, dc, dproj, name):
    t = proj.shape[0]
    tr = min(ROW_BLOCK, t)
    nb = t // tr
    rev = lambda i: nb - 1 - i

    def body(f_ref, b_ref, dc_ref, dproj_in, df_ref, db_ref, run_ref):
        del dproj_in

        @pl.when(pl.program_id(0) == 0)
        def _():
            run_ref[...] = jnp.zeros_like(run_ref)
            db_ref[...] = jnp.zeros_like(db_ref)

        tri = _tri(False)
        b = b_ref[...]
        lane = lax.broadcasted_iota(jnp.int32, (CHUNK, HEAD_DIM), 1)
        df_ref[...] = jnp.zeros_like(df_ref)
        for n in reversed(range(tr // CHUNK)):
            rows = slice(n * CHUNK, (n + 1) * CHUNK)
            dlogf = _dot_exact(tri, dc_ref[rows, :]) + run_ref[...]
            run_ref[...] = dlogf[0:1, :]
            z = f_ref[rows, :] + b
            e = jnp.exp(-jnp.abs(z))
            sig_neg = jnp.where(z >= 0.0, e, 1.0) / (1.0 + e)
            df = jnp.where(lane < N_HEADS, dlogf * sig_neg, 0.0)
            df_ref[rows, :HEAD_DIM] = df.astype(BF16)
            db_ref[...] += jnp.sum(df, axis=0, keepdims=True)

    return _call(body, name, (nb,),
                 [pl.BlockSpec((tr, HEAD_DIM), lambda i: (rev(i), CB_F)), pl.BlockSpec((1, HEAD_DIM), lambda i: (0, 0)),
                  pl.BlockSpec((tr, HEAD_DIM), lambda i: (rev(i), 0)), pl.BlockSpec(memory_space=pl.ANY)],
                 [pl.BlockSpec((tr, F_WIDTH), lambda i: (rev(i), GATED // F_WIDTH)),
                  pl.BlockSpec((1, HEAD_DIM), lambda i: (0, 0))],
                 [jax.ShapeDtypeStruct(dproj.shape, dproj.dtype), jax.ShapeDtypeStruct((1, HEAD_DIM), F32)],
                 [proj, b_pad, dc, dproj], scratch=[pltpu.VMEM((1, HEAD_DIM), F32)],
                 sem=("arbitrary",), nbytes=4 * _nbytes((tr, HEAD_DIM), F32), aliases={3: 0})


_GELU_K = math.sqrt(2.0 / math.pi)
_GELU_C = 0.044715


def _gelu(x):
    t = jnp.tanh(_GELU_K * (x + _GELU_C * (x * x * x)))
    return 0.5 * x * (1.0 + t), t


def _gelu_grad(x, t):
    return 0.5 * (1.0 + t) + 0.5 * x * (1.0 - t * t) * (_GELU_K * (1.0 + 3.0 * _GELU_C * (x * x)))


def _layernorm_stats(a):
    mu = jnp.mean(a, axis=-1, keepdims=True)
    xc = a - mu
    r = lax.rsqrt(jnp.mean(xc * xc, axis=-1, keepdims=True) + EPS)
    return xc * r, r


def _group(g):
    return slice(g * CHUNK, (g + 1) * CHUNK)


def _fixed(shape):
    return pl.BlockSpec(shape, lambda i: (0,) * len(shape))


def _mix_fwd(proj, y_b, w_s, b_cols, g_v, name):
    t = proj.shape[0]
    step_rows = min(MIX_ROWS, t)

    def body(u_ref, vs_ref, ga_ref, gb_ref, yb_ref, w_ref, b_ref, gv_ref, o_ref):
        mask = _causal(CHUNK)
        w = [jnp.where(mask, w_ref[g], 0.0).astype(BF16) for g in range(N_GROUPS)]
        for rows in _chunks(step_rows, CHUNK):
            a_u, _ = _gelu(u_ref[rows, :])
            a_v, _ = _gelu(vs_ref[rows, :])
            xhat, _ = _layernorm_stats(a_v)
            vn = (xhat * gv_ref[...]).astype(BF16)
            sa, sb = _sigmoid(ga_ref[rows, :]), _sigmoid(gb_ref[rows, :])
            yb = yb_ref[rows, :]
            for g in range(N_GROUPS):
                cols = _group(g)
                mixed = jnp.dot(w[g], vn[:, cols], preferred_element_type=F32) + b_ref[:, g:g + 1]
                o_ref[rows, cols] = (sa[:, cols] * (a_u[:, cols] * mixed) + sb[:, cols] * yb[:, cols]).astype(BF16)

    gated = lambda k: pl.BlockSpec((step_rows, D_MODEL), lambda i: (i, k))
    return _call(body, name, (t // step_rows,),
                 [gated(0), gated(1), gated(2), gated(3), _row_spec(step_rows, D_MODEL),
                  _fixed((N_GROUPS, CHUNK, CHUNK)), _fixed((CHUNK, N_GROUPS)), _fixed((1, D_MODEL))],
                 [_row_spec(step_rows, D_MODEL)], [jax.ShapeDtypeStruct((t, D_MODEL), BF16)],
                 [proj, proj, proj, proj, y_b, w_s, b_cols, g_v], sem=("parallel",),
                 nbytes=16 * _nbytes((step_rows, D_MODEL), F32))[0]


def _mix_bwd(proj, y_b, dmerged, w_s, b_cols, g_v, sel, name, deps=()):
    t = proj.shape[0]
    step_rows = min(MIX_BWD_ROWS, t)

    def body(u_ref, vs_ref, ga_ref, gb_ref, yb_ref, dm_ref, w_ref, b_ref, gv_ref, sel_ref,
             dg_ref, dyb_ref, delta_ref, dw_ref, dbt_ref, dgv_ref):
        @pl.when(pl.program_id(0) == 0)
        def _():
            dw_ref[...] = jnp.zeros_like(dw_ref)
            dbt_ref[...] = jnp.zeros_like(dbt_ref)
            dgv_ref[...] = jnp.zeros_like(dgv_ref)

        gv = gv_ref[...]
        mask = _causal(CHUNK)
        w = [jnp.where(mask, w_ref[g], 0.0).astype(BF16) for g in range(N_GROUPS)]
        for rows in _chunks(step_rows, CHUNK):
            u, vs = u_ref[rows, :], vs_ref[rows, :]
            a_u, t_u = _gelu(u)
            a_v, t_v = _gelu(vs)
            xhat, r = _layernorm_stats(a_v)
            vn = (xhat * gv).astype(BF16)
            sa, sb = _sigmoid(ga_ref[rows, :]), _sigmoid(gb_ref[rows, :])
            yb, dm = yb_ref[rows, :], dm_ref[rows, :].astype(F32)
            dyb = dm * sb
            dyb_ref[rows, :] = dyb
            dyb_yb = dyb * yb
            dg_ref[rows, 3 * D_MODEL:] = (dm * yb * (sb * (1.0 - sb))).astype(BF16)
            dya = dm * sa
            dmixed_parts, dvn_parts = [], []
            for g in range(N_GROUPS):
                cols = _group(g)
                delta_ref[rows, cols] = jnp.broadcast_to(jnp.sum(dyb_yb[:, cols], axis=1, keepdims=True),
                                                         (CHUNK, CHUNK))
                mixed = jnp.dot(w[g], vn[:, cols], preferred_element_type=F32) + b_ref[:, g:g + 1]
                y_a = a_u[:, cols] * mixed
                dg_ref[rows, 2 * D_MODEL + g * CHUNK:2 * D_MODEL + (g + 1) * CHUNK] = (
                    dm[:, cols] * y_a * (sa[:, cols] * (1.0 - sa[:, cols]))).astype(BF16)
                dg_ref[rows, cols] = (dya[:, cols] * mixed * _gelu_grad(u[:, cols], t_u[:, cols])).astype(BF16)
                dmixed = dya[:, cols] * a_u[:, cols]
                dmb = dmixed.astype(BF16)
                dw = lax.dot_general(dmb, vn[:, cols], _DIMS["nt"], preferred_element_type=F32)
                dw_ref[g] += jnp.where(mask, dw, 0.0)
                dvn_parts.append(lax.dot_general(w[g], dmb, _DIMS["tn"], preferred_element_type=F32))
                dmixed_parts.append(dmixed)
            dmixed_all = jnp.concatenate(dmixed_parts, axis=1)
            dvn = jnp.concatenate(dvn_parts, axis=1)
            dbt_ref[...] += _dot_exact_rhs(dmixed_all, sel_ref[...])
            dgv_ref[...] += jnp.sum(dvn * xhat, axis=0, keepdims=True)
            dxh = dvn * gv
            da_v = r * (dxh - jnp.mean(dxh, axis=-1, keepdims=True)
                        - xhat * jnp.mean(dxh * xhat, axis=-1, keepdims=True))
            dg_ref[rows, D_MODEL:2 * D_MODEL] = (da_v * _gelu_grad(vs, t_v)).astype(BF16)

    row = lambda width: _row_spec(step_rows, width)
    gated = lambda k: pl.BlockSpec((step_rows, D_MODEL), lambda i: (i, k))
    return _call(body, name, (t // step_rows,),
                 [gated(0), gated(1), gated(2), gated(3), row(D_MODEL), row(D_MODEL),
                  _fixed((N_GROUPS, CHUNK, CHUNK)), _fixed((CHUNK, N_GROUPS)), _fixed((1, D_MODEL)),
                  _fixed((D_MODEL, HEAD_DIM))],
                 [row(GATED), row(D_MODEL), row(D_MODEL),
                  _fixed((N_GROUPS, CHUNK, CHUNK)), _fixed((CHUNK, HEAD_DIM)), _fixed((1, D_MODEL))],
                 [jax.ShapeDtypeStruct((t, IN_PAD), BF16), jax.ShapeDtypeStruct((t, D_MODEL), F32),
                  jax.ShapeDtypeStruct((t, D_MODEL), F32), jax.ShapeDtypeStruct((N_GROUPS, CHUNK, CHUNK), F32),
                  jax.ShapeDtypeStruct((CHUNK, HEAD_DIM), F32), jax.ShapeDtypeStruct((1, D_MODEL), F32)],
                 [proj, proj, proj, proj, y_b, dmerged, w_s, b_cols, g_v, sel], sem=("arbitrary",),
                 nbytes=24 * _nbytes((step_rows, D_MODEL), F32), deps=deps)


def _adamw_update(w_ref, g_ref, m_ref, v_ref, d_ref, nm_ref, nv_ref):
    gv = g_ref[...]
    nm = ADAM_B1 * m_ref[...] + (1.0 - ADAM_B1) * gv
    nv = ADAM_B2 * v_ref[...] + (1.0 - ADAM_B2) * (gv * gv)
    m_hat = nm / (1.0 - ADAM_B1 ** ADAM_STEP)
    v_hat = nv / (1.0 - ADAM_B2 ** ADAM_STEP)
    d_ref[...] = -ADAM_LR * (m_hat / (jnp.sqrt(v_hat) + ADAM_EPS) + ADAM_WD * w_ref[...])
    nm_ref[...] = nm
    nv_ref[...] = nv


def _adamw(w, g, m, v, name):
    r, c = w.shape
    tr = _tile(r, 256, unit=8)

    def body(w_ref, g_ref, m_ref, v_ref, d_ref, nm_ref, nv_ref):
        _adamw_update(w_ref, g_ref, m_ref, v_ref, d_ref, nm_ref, nv_ref)

    spec = pl.BlockSpec((tr, c), lambda i: (i, 0))
    shape = jax.ShapeDtypeStruct((r, c), F32)
    return _call(body, name, (r // tr,), [spec] * 4, [spec] * 3, [shape] * 3, [w, g, m, v], sem=("parallel",),
                 nbytes=7 * _nbytes((tr, max(c, 128)), F32))


def _adamw_cols(w, g, m, v, name):
    layers, r, c = w.shape
    tc = 256

    def body(w_ref, g_ref, m_ref, v_ref, d_ref, nm_ref, nv_ref):
        _adamw_update(w_ref, g_ref, m_ref, v_ref, d_ref, nm_ref, nv_ref)

    spec = pl.BlockSpec((None, r, tc), lambda l, j: (l, 0, j))
    shape = jax.ShapeDtypeStruct(w.shape, F32)
    return _call(body, name, (layers, c // tc), [spec] * 4, [spec] * 3, [shape] * 3, [w, g, m, v],
                 sem=("parallel", "parallel"), nbytes=7 * _nbytes((r, tc), F32))


def _pad_rows(a, rows):
    return jnp.pad(a, ((0, rows - a.shape[0]), (0, 0)))


def _unpack_in(gathered):
    i = jnp.arange(16)[None, :, None]
    b = jnp.arange(N_DEV)[:, None, None]
    head = jnp.where(i < b, jnp.roll(gathered[:, IN_STRIDE:IN_STRIDE + 16], 1, axis=0), gathered[:, :16])
    nat = jnp.concatenate([head, gathered[:, 16:IN_STRIDE]], axis=1).reshape(N_DEV * IN_STRIDE, D_MODEL)
    qkv = nat[2 * D_MODEL:5 * D_MODEL].reshape(3, N_HEADS, HEAD_DIM, D_MODEL)
    qkv = jnp.transpose(qkv, (1, 0, 2, 3)).reshape(3 * D_MODEL, D_MODEL)
    f = _pad_rows(gathered[N_DEV - 1, IN_STRIDE:IN_STRIDE + N_HEADS], F_WIDTH)
    return jnp.concatenate([nat[:2 * D_MODEL], nat[5 * D_MODEL:7 * D_MODEL], f, qkv], axis=0)


def _pack_grad_in(dw_in_t):
    qkv = dw_in_t[GATED + F_WIDTH:].reshape(N_HEADS, 3, HEAD_DIM, D_MODEL)
    qkv = jnp.transpose(qkv, (1, 0, 2, 3)).reshape(3 * D_MODEL, D_MODEL)
    return jnp.concatenate([dw_in_t[:2 * D_MODEL], qkv, dw_in_t[2 * D_MODEL:GATED],
                            dw_in_t[GATED:GATED + IN_ROWS - N_DEV * IN_STRIDE]], axis=0)


def kernel(x, mix_pre_g, w_in, b_forget, sgu_norm_g, w_spatial, b_spatial, w_out, mix_post_g, ffn_pre_g, w_gate, w_up, w_down, ffn_post_g, loss_target, m_mix_pre_g, m_w_in, m_b_forget, m_sgu_norm_g, m_w_spatial, m_b_spatial, m_w_out, m_mix_post_g, m_ffn_pre_g, m_w_gate, m_w_up, m_w_down, m_ffn_post_g, v_mix_pre_g, v_w_in, v_b_forget, v_sgu_norm_g, v_w_spatial, v_b_spatial, v_w_out, v_mix_post_g, v_ffn_pre_g, v_w_gate, v_w_up, v_w_down, v_ffn_post_g):
    depth = w_in.shape[0]
    assert depth == 2
    t = x.shape[1]
    x0 = x.reshape(t, D_MODEL)
    target = loss_target.reshape(t, D_MODEL)
    sel = (jnp.arange(D_MODEL)[:, None] // HEAD_DIM == jnp.arange(HEAD_DIM)[None, :]).astype(BF16)
    vec = lambda a, l: a[l][None, :]
    zero_row = jnp.zeros((1, D_MODEL), F32)

    me = 4 * lax.axis_index("x") + 2 * lax.axis_index("y") + lax.axis_index("c")
    p_in = [lax.dynamic_update_slice(jnp.zeros((P_IN, D_MODEL), BF16), w_in[l].T.astype(BF16), (me, 0))
            for l in range(depth)]
    p_rest = [[w_out[l].astype(BF16), jnp.stack([w_gate[l].T, w_up[l].T]).astype(BF16), w_down[l].astype(BF16)]
              for l in range(depth)]
    w_in_t = [None] * depth
    w_rest = [None] * depth
    w_in_t[0] = _unpack_in(_exchange(("gather", 0), p_in[0], "gather_w_in_0")[0])

    saved = []
    xl = x0
    h = _rms_fwd(xl, vec(mix_pre_g, 0), "rms_in_0")
    dy = loss = None
    for l in range(depth):
        b_pad = jnp.pad(b_forget[l], (0, HEAD_DIM - N_HEADS))[None, :]
        b_cols = b_spatial[l].T
        proj = _matmul(h, w_in_t[l], "nt", F32, f"proj_{l}")
        c_pad = _forget_cumsum(proj, b_pad, f"forget_cumsum_{l}")
        c_row = c_pad[:, :N_HEADS].T[:, None, :]
        riders = [(("gather", 0), p_rest[l][0]), (("gather", 1), p_rest[l][1]), (("gather", 0), p_rest[l][2])]
        riders += [(("gather", 0), p_in[l + 1])] if l + 1 < depth else []
        y_b, e_row, *arrived = _attn_fwd_t(proj, c_pad, c_row, f"attn_fwd_{l}", comm=riders)
        w_rest[l] = (arrived[0].reshape(D_MODEL, D_MODEL), arrived[1].reshape(2 * D_FF, D_MODEL),
                     arrived[2].reshape(D_FF, D_MODEL))
        if l + 1 < depth:
            w_in_t[l + 1] = _unpack_in(arrived[3])
        w_o, w_gu_t, w_d = w_rest[l]
        merged = _mix_fwd(proj, y_b, w_spatial[l], b_cols, vec(sgu_norm_g, l), f"mix_fwd_{l}")
        o, x1, h2 = _matmul_post_norm(merged, w_o, xl, vec(mix_post_g, l), f"out_proj_{l}", g_next=vec(ffn_pre_g, l))
        gate, up, act = _ffn_in(h2, w_gu_t, f"ffn_gu_{l}")
        if l + 1 < depth:
            dn, x_next, h_next = _matmul_post_norm(act, w_d, x1, vec(ffn_post_g, l), f"ffn_down_{l}",
                                                   g_next=vec(mix_pre_g, l + 1))
        else:
            dn, dy, loss, dd, dg_ffn_post = _matmul_post_norm(act, w_d, x1, vec(ffn_post_g, l), f"ffn_down_{l}",
                                                              target=target)
        saved.append(dict(x0=xl, h=h, proj=proj, b_pad=b_pad, b_cols=b_cols, c_pad=c_pad, y_b=y_b, e_row=e_row,
                          merged=merged, o=o, x1=x1, h2=h2, gate=gate, up=up, act=act, dn=dn))
        if l + 1 < depth:
            xl, h = x_next, h_next

    g_small = [None] * depth
    flights = {}
    late_token = None
    dx = dy
    for l in reversed(range(depth)):
        s = saved[l]
        w_o, w_gu_t, w_d = w_rest[l]
        dgu = _ffn_in_bwd(dd, w_d, s["gate"], s["up"], f"d_act_{l}")
        dw_d = _matmul(s["act"], dd, "tn", BF16, f"dw_down_{l}")
        dw_gu_t = _matmul(dgu, s["h2"], "tn", BF16, f"dw_gu_{l}", halves=True)
        dx1, dg_ffn_pre, do, dg_mix_post = _matmul_norm_bwd(
            dgu, w_gu_t, dx, s["x1"], vec(ffn_pre_g, l), f"d_h2_{l}", post=(s["o"], vec(mix_post_g, l)), halves=True)
        dmerged = _matmul(do, w_o, "nt", F32, f"d_merged_{l}")
        dw_o = _matmul(s["merged"], do, "tn", BF16, f"dw_out_{l}")
        tokens = []
        for key, dw, base, rows in [("out", dw_o, 0, SH_OUT), ("gate", dw_gu_t, 0, SH_FF), ("up", dw_gu_t, D_FF, SH_FF),
                                    ("down", dw_d, 0, SH_FF)]:
            flights[(key, l)], token = _exchange_start(("windows", base, rows, rows), dw, f"start_grad_{key}_{l}")
            tokens.append(token)
        dproj, dyb, delta_b, dw_s, dbt, dg_v = _mix_bwd(
            s["proj"], s["y_b"], dmerged, w_spatial[l], s["b_cols"], vec(sgu_norm_g, l), sel, f"mix_bwd_{l}",
            deps=tokens)
        db_s = dbt[:, :N_GROUPS].T.reshape(1, D_MODEL)
        tokens = []
        if l == 0:
            g_small[0] = jnp.concatenate([zero_row, dg_v, dg_mix_post, dg_ffn_pre, dg_ffn_post, db_s, zero_row,
                                          zero_row, dw_s.reshape(CHUNK, D_MODEL)], axis=0)
            flights["small"], token = _exchange_start(("spread",),jnp.concatenate([g_small[1], g_small[0]], axis=0),
                                                      "start_small_grads")
            tokens = [token]
        dproj, dc_key, dc_query = _attn_bwd_t(s["proj"], dyb, s["e_row"], delta_b, s["c_pad"], dproj,
                                              f"attn_bwd_{l}", deps=tokens)
        dc = jnp.pad((dc_key + dc_query)[:, 0, :].T, ((0, 0), (0, HEAD_DIM - N_HEADS)))
        dproj, db_f = _forget_bwd_t(s["proj"], s["b_pad"], dc, dproj, f"forget_bwd_{l}")
        db_f_row = jnp.pad(db_f, ((0, 0), (0, D_MODEL - HEAD_DIM)))
        dw_in_t = _matmul(dproj, s["h"], "tn", BF16, f"dw_in_{l}")
        flights[("in", l)], token = _exchange_start(("windows", 0, IN_STRIDE, P_IN), _pack_grad_in(dw_in_t),
                                                    f"start_grad_in_{l}")
        if l > 0:
            dx, dg_mix_pre, dd, dg_ffn_post_below = _matmul_norm_bwd(
                dproj, w_in_t[l], dx1, s["x0"], vec(mix_pre_g, l), f"d_h_{l}",
                post=(saved[l - 1]["dn"], vec(ffn_post_g, l - 1)), deps=[token])
            loss_row = jnp.pad(loss, ((0, 0), (0, D_MODEL - 1)))
            g_small[l] = jnp.concatenate([dg_mix_pre, dg_v, dg_mix_post, dg_ffn_pre, dg_ffn_post, db_s, db_f_row,
                                          loss_row, dw_s.reshape(CHUNK, D_MODEL)], axis=0)
            dg_ffn_post = dg_ffn_post_below
        else:
            dx, dg_mix_pre = _matmul_norm_bwd(dproj, w_in_t[l], dx1, s["x0"], vec(mix_pre_g, l), f"d_h_{l}",
                                              deps=[token])
            late_rows = jnp.concatenate([dg_mix_pre, db_f_row] + [zero_row] * 6, axis=0)
            flights["late"], late_token = _exchange_start(("spread",),late_rows, "start_late_small_grads")
    grad_x = dx.reshape(x.shape)

    landed, after = {}, [grad_x, late_token]
    rest = ["out", "gate", "up", "down"]
    for key in [(n, 1) for n in rest] + [("in", 1)] + [(n, 0) for n in rest] + ["small"]:
        label = key if isinstance(key, str) else f"grad_{key[0]}_{key[1]}"
        landed[key] = _exchange_wait(flights[key], after, f"wait_{label}")
        after = [landed[key]]
    small_sum = _sum_slots(landed["small"], S_ROWS, "sum_small_grads")
    sm = small_sum.reshape(depth, S_ROWS, D_MODEL)[::-1]
    loss = sm[depth - 1, 7, 0]
    g = {
        "sgu_norm_g": sm[:, 1], "mix_post_g": sm[:, 2],
        "ffn_pre_g": sm[:, 3], "ffn_post_g": sm[:, 4], "b_spatial": sm[:, 5].reshape(depth, N_GROUPS, CHUNK),
        "w_spatial": sm[:, 8:].reshape(depth, N_GROUPS, CHUNK, CHUNK),
    }
    for n in rest:
        g["w_" + n] = jnp.stack([_sum_slots(landed[(n, l)], landed[(n, l)].shape[1], f"sum_grad_{n}_{l}")
                                 for l in range(depth)])

    names = ["mix_pre_g", "w_in", "b_forget", "sgu_norm_g", "w_spatial", "b_spatial", "w_out", "mix_post_g",
             "ffn_pre_g", "w_gate", "w_up", "w_down", "ffn_post_g"]
    ws = dict(mix_pre_g=mix_pre_g, w_in=w_in, b_forget=b_forget, sgu_norm_g=sgu_norm_g, w_spatial=w_spatial,
              b_spatial=b_spatial, w_out=w_out, mix_post_g=mix_post_g, ffn_pre_g=ffn_pre_g, w_gate=w_gate, w_up=w_up,
              w_down=w_down, ffn_post_g=ffn_post_g)
    ms = dict(mix_pre_g=m_mix_pre_g, w_in=m_w_in, b_forget=m_b_forget, sgu_norm_g=m_sgu_norm_g, w_spatial=m_w_spatial,
              b_spatial=m_b_spatial, w_out=m_w_out, mix_post_g=m_mix_post_g, ffn_pre_g=m_ffn_pre_g, w_gate=m_w_gate,
              w_up=m_w_up, w_down=m_w_down, ffn_post_g=m_ffn_post_g)
    vs = dict(mix_pre_g=v_mix_pre_g, w_in=v_w_in, b_forget=v_b_forget, sgu_norm_g=v_sgu_norm_g, w_spatial=v_w_spatial,
              b_spatial=v_b_spatial, w_out=v_w_out, mix_post_g=v_mix_post_g, ffn_pre_g=v_ffn_pre_g, w_gate=v_w_gate,
              w_up=v_w_up, w_down=v_w_down, ffn_post_g=v_ffn_post_g)
    updates = {}
    last = ["w_in", "mix_pre_g", "b_forget"]
    transposed = ["w_in", "w_gate", "w_up"]
    for n in [n for n in names if n not in last] + last:
        if n == last[0]:
            sum_in_1 = _sum_slots(landed[("in", 1)], 304, "sum_grad_in_1")
            done = [u[0] for u in updates.values()] + [sum_in_1]
            sum_in_0 = _sum_slots(_exchange_wait(flights[("in", 0)], done, "wait_grad_in_0"), 304, "sum_grad_in_0")
            late = _sum_slots(_exchange_wait(flights["late"], [sum_in_0], "wait_late"), 8, "sum_late_small_grads")
            g["w_in"] = lax.dynamic_slice(jnp.stack([sum_in_0, sum_in_1]), (0, me, 0), (depth, SH_IN, D_MODEL))
            g["mix_pre_g"] = jnp.stack([late[0], sm[1, 0]])
            g["b_forget"] = jnp.stack([late[1, :N_HEADS], sm[1, 6, :N_HEADS]])
        if n in transposed:
            view = lambda a: jnp.swapaxes(a, 1, 2)
            updates[n] = [view(u) for u in _adamw_cols(view(ws[n]), g[n], view(ms[n]), view(vs[n]), f"adamw_{n}")]
            g[n] = view(g[n])
        else:
            flat = (-1, ws[n].shape[-1])
            updates[n] = [u.reshape(ws[n].shape) for u in
                          _adamw(ws[n].reshape(flat), g[n].reshape(flat), ms[n].reshape(flat), vs[n].reshape(flat),
                                 f"adamw_{n}")]
    deltas = [updates[n][0] for n in names]
    new_m = [updates[n][1] for n in names]
    new_v = [updates[n][2] for n in names]
    grads = [g[n].reshape(ws[n].shape) for n in names]
    return (loss, grad_x, *grads, *deltas, *new_m, *new_v)
```

```python
import math

import jax
import jax.numpy as jnp
from jax import lax
from jax.experimental import pallas as pl
from jax.experimental.pallas import tpu as pltpu

F32 = jnp.float32
BF16 = jnp.bfloat16

N_DEV = 8
D_MODEL = 1024
N_HEADS = 8
HEAD_DIM = 128
CHUNK = 128
N_GROUPS = 8
D_FF = 2816
IN_WIDTH = 7 * D_MODEL + N_HEADS
IN_PAD = 7680
EPS = 1e-6
GATED = 4 * D_MODEL
F_WIDTH = 512
QKV_WIDTH = 3 * HEAD_DIM
CB_F = GATED // HEAD_DIM
CB_QKV = (GATED + F_WIDTH) // HEAD_DIM
assert GATED + F_WIDTH + N_HEADS * QKV_WIDTH == IN_PAD and (GATED + F_WIDTH) % QKV_WIDTH == 0

ADAM_LR, ADAM_B1, ADAM_B2, ADAM_EPS, ADAM_WD, ADAM_STEP = 0.001, 0.9, 0.999, 1e-08, 0.01, 10

SH_IN = IN_WIDTH // N_DEV
SH_OUT = D_MODEL // N_DEV
SH_FF = D_FF // N_DEV
P_IN = 912
IN_STRIDE = 7 * HEAD_DIM
IN_ROWS = IN_STRIDE * (N_DEV - 1) + P_IN
S_ROWS = 136

ATTN_BLOCK = 1024
ROW_BLOCK = 512
MIX_ROWS = 1024
MIX_BWD_ROWS = 256
MM_TM, MM_TN, MM_TK = 1536, 1536, 2048
VMEM_CAP = 56 << 20
NEG = -1e30


def _tile(n, cap, unit=128):
    if n <= cap:
        return n
    best = None
    for t in range(unit, cap + 1, unit):
        if n % t == 0:
            best = t
    assert best is not None, (n, cap)
    return best


def _nbytes(shape, dtype):
    return math.prod(shape) * jnp.dtype(dtype).itemsize


_HBM = pl.BlockSpec(memory_space=pltpu.HBM)
MESH = pl.DeviceIdType.MESH

_N_REMOTE = N_DEV - 1


def _exchange_shapes(kind, x):
    if kind[0] == "gather":
        shape = x.shape[:kind[1]] + (N_DEV,) + x.shape[kind[1]:]
    elif kind[0] == "spread":
        shape = (N_DEV,) + x.shape
    else:
        shape = (N_DEV, kind[3]) + x.shape[1:]
    return [jax.ShapeDtypeStruct(shape, x.dtype)]


def _exchange_sems(kind):
    del kind
    return [pltpu.SemaphoreType.DMA((_N_REMOTE,)), pltpu.SemaphoreType.DMA((_N_REMOTE,)), pltpu.SemaphoreType.DMA]


def _plan(kind, x_ref, outs, sems):
    send_sems, recv_sems, local_sem = sems
    x, y, c = lax.axis_index("x"), lax.axis_index("y"), lax.axis_index("c")

    def remote(src, dst, k, to):
        return pltpu.make_async_remote_copy(src_ref=src, dst_ref=dst, send_sem=send_sems.at[k], recv_sem=recv_sems.at[k],
                                            device_id=to, device_id_type=MESH)

    sibling = (x, y, 1 - c)
    if kind[0] == "gather":
        out, = outs
        slot = lambda px, py, pc: out.at[(slice(None),) * kind[1] + (4 * px + 2 * py + pc,)]
        chips = [(1 - x, y), (x, 1 - y), (1 - x, 1 - y)]
        local = [pltpu.make_async_copy(x_ref, slot(x, y, c), local_sem)]
        first = [remote(x_ref, slot(x, y, c), 0, sibling)]
        first += [remote(x_ref, slot(x, y, c), 1 + k, (*chip, c)) for k, chip in enumerate(chips)]
        relays = [(remote(x_ref, slot(*chip, c), 1 + k, (*chip, c)), remote(slot(*chip, c), slot(*chip, c), 4 + k, sibling))
                  for k, chip in enumerate(chips)]
        arrivals = [remote(x_ref, slot(x, y, 1 - c), 0, sibling)]
        arrivals += [remote(x_ref, slot(*chip, 1 - c), 4 + k, sibling) for k, chip in enumerate(chips)]
        return local, first, relays, arrivals
    out, = outs
    mine = 4 * x + 2 * y + c
    if kind[0] == "spread":
        src = lambda d: x_ref
    else:
        _, base, stride, length = kind
        src = lambda d: x_ref.at[pl.ds(pl.multiple_of(base + stride * d, 16), length)]
    first, arrivals = [], []
    for j in range(1, N_DEV):
        peer = (1 - x if j & 4 else x, 1 - y if j & 2 else y, 1 - c if j & 1 else c)
        theirs = 4 * peer[0] + 2 * peer[1] + peer[2]
        first.append(remote(src(theirs), out.at[mine], j - 1, peer))
        arrivals.append(remote(src(theirs), out.at[theirs], j - 1, peer))
    local = [] if local_sem is None else [pltpu.make_async_copy(src(mine), out.at[mine], local_sem)]
    return local, first, [], arrivals


def _start(plan):
    local, first, _, _ = plan
    for cp in local + first:
        cp.start()


def _finish(plan):
    local, first, relays, arrivals = plan
    for arrival, onward in relays:
        arrival.wait_recv()
        onward.start()
    for cp in arrivals:
        cp.wait_recv()
    for cp in first + [onward for _, onward in relays]:
        cp.wait_send()
    for cp in local:
        cp.wait()


def _direct_copies(kind, x_ref, land_ref, send_sems, recv_sems):
    _, first, _, arrivals = _plan(kind, x_ref, [land_ref], (send_sems, recv_sems, None))
    return first, arrivals


_SEM = pl.BlockSpec(memory_space=pltpu.SEMAPHORE)
_DATAFLOW = pltpu.SideEffectType.DATAFLOW_SIDE_EFFECTING


def _exchange_start(kind, x, name):
    me = 4 * lax.axis_index("x") + 2 * lax.axis_index("y") + lax.axis_index("c")
    own = x if kind[0] == "spread" else lax.dynamic_slice_in_dim(x, kind[1] + kind[2] * me, kind[3], 0)
    shape = _exchange_shapes(kind, x)[0].shape
    land = lax.dynamic_update_slice_in_dim(lax.empty(shape, x.dtype), own[None], me, 0)

    def body(x_ref, land_ref, send_sems, recv_sems, x_thru, land_thru, token):
        del x_thru, land_thru
        for cp in _direct_copies(kind, x_ref, land_ref, send_sems, recv_sems)[0]:
            cp.start()
        token[...] = jnp.zeros_like(token)

    *handle, token = pl.pallas_call(
        body, name=name,
        out_shape=(pltpu.SemaphoreType.DMA((_N_REMOTE,)), pltpu.SemaphoreType.DMA((_N_REMOTE,)),
                   pltpu.HBM(x.shape, x.dtype), pltpu.HBM(land.shape, land.dtype), jax.ShapeDtypeStruct((8, 128), F32)),
        in_specs=(_HBM, _HBM), out_specs=(_SEM, _SEM, _HBM, _HBM, pl.BlockSpec(memory_space=pltpu.VMEM)),
        input_output_aliases={0: 2, 1: 3}, compiler_params=pltpu.CompilerParams(has_side_effects=_DATAFLOW),
    )(pltpu.with_memory_space_constraint(x, pltpu.HBM), pltpu.with_memory_space_constraint(land, pltpu.HBM))
    return (kind, *handle), token


def _exchange_wait(handle, after, name):
    kind, send_sems, recv_sems, x_thru, land_thru = handle

    def body(x_ref, land_ref, send_sems, recv_sems, *rest):
        first, arrivals = _direct_copies(kind, x_ref, land_ref, send_sems, recv_sems)
        for cp in first:
            cp.wait_send()
        for cp in arrivals:
            cp.wait_recv()

    return pl.pallas_call(
        body, name=name,
        out_shape=(pltpu.HBM(x_thru.shape, x_thru.dtype), pltpu.HBM(land_thru.shape, land_thru.dtype)),
        in_specs=(_HBM, _HBM, _SEM, _SEM) + (pl.BlockSpec(memory_space=pl.ANY),) * len(after), out_specs=(_HBM, _HBM),
        input_output_aliases={0: 0, 1: 1}, compiler_params=pltpu.CompilerParams(has_side_effects=_DATAFLOW),
    )(x_thru, land_thru, send_sems, recv_sems, *after)[1]


def _exchange(kind, x, name):
    n_out = len(_exchange_shapes(kind, x))

    def body(x_ref, *refs):
        plan = _plan(kind, x_ref, refs[:n_out], refs[n_out:])
        _start(plan)
        _finish(plan)

    return pl.pallas_call(body, name=name, out_shape=_exchange_shapes(kind, x), in_specs=[_HBM],
                          out_specs=[_HBM] * n_out, scratch_shapes=_exchange_sems(kind))(x)


def _call(body, name, grid, in_specs, out_specs, out_shape, args, scratch=(), sem=None, nbytes=0, aliases=None,
          comm=(), deps=()):
    in_specs, out_specs, out_shape, args, scratch = (list(in_specs), list(out_specs), list(out_shape), list(args),
                                                     list(scratch))
    if deps:
        n_real, n_deps, unordered = len(args), len(deps), body

        def body(*refs):
            unordered(*refs[:n_real], *refs[n_real + n_deps:])

        in_specs += [pl.BlockSpec(memory_space=pl.ANY)] * n_deps
        args += list(deps)
    if comm:
        n_in, n_out, n_scr, n_ops = len(args), len(out_shape), len(scratch), len(comm)
        kinds = [kind for kind, _ in comm]
        shapes = [_exchange_shapes(kind, x) for kind, x in comm]
        inner = body

        def body(*refs):
            ins, refs = refs[:n_in], refs[n_in:]
            cins, refs = refs[:n_ops], refs[n_ops:]
            outs, refs = refs[:n_out], refs[n_out:]
            couts = []
            for sh in shapes:
                couts.append(refs[:len(sh)])
                refs = refs[len(sh):]
            own_scratch, sems = refs[:n_scr], refs[n_scr:]
            first = last = None
            for axis, size in enumerate(grid):
                at_start, at_end = pl.program_id(axis) == 0, pl.program_id(axis) == size - 1
                first = at_start if first is None else first & at_start
                last = at_end if last is None else last & at_end
            plans = [_plan(kinds[o], cins[o], couts[o], sems[3 * o:3 * o + 3]) for o in range(n_ops)]

            @pl.when(first)
            def _():
                for p in plans:
                    _start(p)

            inner(*ins, *outs, *own_scratch)

            @pl.when(last)
            def _():
                for p in plans:
                    _finish(p)

        in_specs += [_HBM] * n_ops
        args += [x for _, x in comm]
        for kind, sh in zip(kinds, shapes):
            out_shape += sh
            out_specs += [_HBM] * len(sh)
            scratch += _exchange_sems(kind)
        sem = ("arbitrary",) * len(grid)
    limit = int(min(max(2 * nbytes + (8 << 20), 32 << 20), VMEM_CAP))
    return pl.pallas_call(
        body, name=name, grid=grid, in_specs=in_specs, out_specs=out_specs, out_shape=out_shape,
        scratch_shapes=scratch, input_output_aliases=aliases or {},
        compiler_params=pltpu.CompilerParams(dimension_semantics=sem, vmem_limit_bytes=limit),
    )(*args)


def _sum_slots(x, tr, name):
    n, r, c = x.shape

    def body(x_ref, o_ref):
        acc = x_ref[0].astype(F32)
        for d in range(1, n):
            acc = acc + x_ref[d].astype(F32)
        o_ref[...] = acc

    return _call(body, name, (r // tr,), [pl.BlockSpec((n, tr, c), lambda i: (0, i, 0))],
                 [pl.BlockSpec((tr, c), lambda i: (i, 0))], [jax.ShapeDtypeStruct((r, c), F32)], [x],
                 sem=("parallel",), nbytes=_nbytes((n, tr, c), x.dtype) + _nbytes((tr, c), F32))[0]


_DIMS = {"nn": (((1,), (0,)), ((), ())), "nt": (((1,), (1,)), ((), ())), "tn": (((0,), (0,)), ((), ()))}


def _matmul(a, b, mode, out_dtype, name, comm=(), deps=(), halves=False):
    a_shape = (a.shape[1], 2 * a.shape[2]) if halves else a.shape
    if mode == "nn":
        (m, k), (k2, n) = a_shape, b.shape
    elif mode == "nt":
        (m, k), (n, k2) = a_shape, b.shape
    else:
        (k, m), (k2, n) = a_shape, b.shape
    assert k == k2, (a.shape, b.shape, mode)
    tm, tn, tk = _tile(m, MM_TM), _tile(n, MM_TN), _tile(k, MM_TK)
    if halves and mode == "nn":
        tk = _tile(k // 2, MM_TK)
    if halves and mode == "tn":
        tm = _tile(m // 2, MM_TM)
    nk = k // tk
    dims = _DIMS[mode]
    if halves:
        per_half = (k // 2) // tk if mode == "nn" else (m // 2) // tm
        a_spec = {"nn": pl.BlockSpec((None, tm, tk), lambda i, j, kk: (kk // per_half, i, kk % per_half)),
                  "tn": pl.BlockSpec((None, tk, tm), lambda i, j, kk: (i // per_half, kk, i % per_half))}[mode]
    else:
        a_spec = {"nn": pl.BlockSpec((tm, tk), lambda i, j, kk: (i, kk)),
                  "nt": pl.BlockSpec((tm, tk), lambda i, j, kk: (i, kk)),
                  "tn": pl.BlockSpec((tk, tm), lambda i, j, kk: (kk, i))}[mode]
    b_spec = {"nn": pl.BlockSpec((tk, tn), lambda i, j, kk: (kk, j)),
              "nt": pl.BlockSpec((tn, tk), lambda i, j, kk: (j, kk)),
              "tn": pl.BlockSpec((tk, tn), lambda i, j, kk: (kk, j))}[mode]

    def partial_product(a_ref, b_ref):
        return lax.dot_general(a_ref[...].astype(BF16), b_ref[...].astype(BF16), dims, preferred_element_type=F32)

    if nk == 1:
        def body(a_ref, b_ref, o_ref):
            o_ref[...] = partial_product(a_ref, b_ref).astype(o_ref.dtype)
        scratch = []
    else:
        def body(a_ref, b_ref, o_ref, acc_ref):
            kk = pl.program_id(2)

            @pl.when(kk == 0)
            def _():
                acc_ref[...] = jnp.zeros_like(acc_ref)

            acc_ref[...] += partial_product(a_ref, b_ref)

            @pl.when(kk == nk - 1)
            def _():
                o_ref[...] = acc_ref[...].astype(o_ref.dtype)
        scratch = [pltpu.VMEM((tm, tn), F32)]

    nbytes = (_nbytes((tm, tk), a.dtype) + _nbytes((tk, tn), b.dtype) + _nbytes((tm, tn), out_dtype)
              + _nbytes((tm, tn), F32))
    res = _call(body, name, (m // tm, n // tn, nk), [a_spec, b_spec],
                [pl.BlockSpec((tm, tn), lambda i, j, kk: (i, j))], [jax.ShapeDtypeStruct((m, n), out_dtype)], [a, b],
                scratch=scratch, sem=("parallel", "parallel", "arbitrary"), nbytes=nbytes, comm=comm, deps=deps)
    return res if comm else res[0]


def _matmul_post_norm(a, b, x, g_post, name, g_next=None, target=None):
    (m, k), (k2, d) = a.shape, b.shape
    assert k == k2 and d == D_MODEL
    with_loss = target is not None
    tm, tk = _tile(m, ROW_BLOCK if with_loss else 2 * ROW_BLOCK), _tile(k, MM_TK)
    nk = k // tk

    def body(a_ref, b_ref, x_ref, gp_ref, last_ref, o_ref, first_ref, second_ref, *rest):
        acc_ref = rest[-1]
        i, kk = pl.program_id(0), pl.program_id(1)

        @pl.when(kk == 0)
        def _():
            acc_ref[...] = jnp.zeros_like(acc_ref)

        acc_ref[...] += jnp.dot(a_ref[...], b_ref[...], preferred_element_type=F32)

        @pl.when(kk == nk - 1)
        def _():
            ov = acc_ref[...]
            o_ref[...] = ov
            xn = x_ref[...] + (ov * _rms(ov)) * gp_ref[...]
            if with_loss:
                dd_ref, dg_ref = rest[:2]
                err = xn - last_ref[...]
                dy = err / d
                first_ref[...] = dy
                dd, dg_rows = _rms_bwd(dy, ov, gp_ref[...])
                dd_ref[...] = dd.astype(BF16)

                @pl.when(i == 0)
                def _():
                    second_ref[...] = jnp.zeros_like(second_ref)
                    dg_ref[...] = jnp.zeros_like(dg_ref)

                second_ref[...] += 0.5 * jnp.sum(jnp.mean(err * err, axis=-1, keepdims=True), axis=0, keepdims=True)
                dg_ref[...] += jnp.sum(dg_rows, axis=0, keepdims=True)
            else:
                first_ref[...] = xn
                second_ref[...] = ((xn * _rms(xn)) * last_ref[...]).astype(BF16)

    rows = pl.BlockSpec((tm, d), lambda i, kk: (i, 0))
    vec = pl.BlockSpec((1, d), lambda i, kk: (0, 0))
    second = ((pl.BlockSpec((1, 1), lambda i, kk: (0, 0)), jax.ShapeDtypeStruct((1, 1), F32)) if with_loss
              else (rows, jax.ShapeDtypeStruct((m, d), BF16)))
    nbytes = _nbytes((tm, tk), BF16) + _nbytes((tk, d), BF16) + 6 * _nbytes((tm, d), F32)
    extra_specs = [rows, vec] if with_loss else []
    extra_shapes = [jax.ShapeDtypeStruct((m, d), BF16), jax.ShapeDtypeStruct((1, d), F32)] if with_loss else []
    return _call(body, name, (m // tm, nk),
                 [pl.BlockSpec((tm, tk), lambda i, kk: (i, kk)), pl.BlockSpec((tk, d), lambda i, kk: (kk, 0)), rows, vec,
                  rows if with_loss else vec],
                 [rows, rows, second[0]] + extra_specs,
                 [jax.ShapeDtypeStruct((m, d), F32), jax.ShapeDtypeStruct((m, d), F32), second[1]] + extra_shapes,
                 [a, b, x, g_post, target if with_loss else g_next],
                 scratch=[pltpu.VMEM((tm, d), F32)], sem=("arbitrary", "arbitrary"), nbytes=nbytes)


def _matmul_norm_bwd(a, b, dskip, xin, g_pre, name, post=None, halves=False, deps=()):
    m, k = (a.shape[1], 2 * a.shape[2]) if halves else a.shape
    (k2, d) = b.shape
    assert k == k2 and d == D_MODEL
    tm = _tile(m, ROW_BLOCK if post else 2 * ROW_BLOCK)
    tk = _tile(k // 2 if halves else k, MM_TK)
    nk = k // tk
    n_in = 5 + (2 if post else 0)

    def body(*refs):
        (a_ref, b_ref, dskip_ref, xin_ref, gpre_ref), rest = refs[:5], list(refs[5:n_in])
        outs, acc_ref = list(refs[n_in:-1]), refs[-1]
        i, kk = pl.program_id(0), pl.program_id(1)

        @pl.when(kk == 0)
        def _():
            acc_ref[...] = jnp.zeros_like(acc_ref)

        acc_ref[...] += jnp.dot(a_ref[...], b_ref[...], preferred_element_type=F32)

        @pl.when(kk == nk - 1)
        def _():
            dxin, dg_rows = _rms_bwd(acc_ref[...], xin_ref[...], gpre_ref[...])
            dx = dskip_ref[...] + dxin
            dx_ref, dgpre_ref = outs[0], outs[1]
            dx_ref[...] = dx

            @pl.when(i == 0)
            def _():
                for ref in outs[1::2]:
                    ref[...] = jnp.zeros_like(ref)

            dgpre_ref[...] += jnp.sum(dg_rows, axis=0, keepdims=True)
            if post:
                da, dg_rows = _rms_bwd(dx, rest[0][...], rest[1][...])
                outs[2][...] = da.astype(BF16)
                outs[3][...] += jnp.sum(dg_rows, axis=0, keepdims=True)

    if halves:
        per_half = (k // 2) // tk
        a_spec = pl.BlockSpec((None, tm, tk), lambda i, kk: (kk // per_half, i, kk % per_half))
    else:
        a_spec = pl.BlockSpec((tm, tk), lambda i, kk: (i, kk))
    rows = pl.BlockSpec((tm, d), lambda i, kk: (i, 0))
    vec = pl.BlockSpec((1, d), lambda i, kk: (0, 0))
    args, in_specs = [a, b, dskip, xin, g_pre], [a_spec, pl.BlockSpec((tk, d), lambda i, kk: (kk, 0)), rows, rows, vec]
    out_specs, out_shape = [rows, vec], [jax.ShapeDtypeStruct((m, d), F32), jax.ShapeDtypeStruct((1, d), F32)]
    if post:
        args += list(post)
        in_specs += [rows, vec]
        out_specs += [rows, vec]
        out_shape += [jax.ShapeDtypeStruct((m, d), BF16), jax.ShapeDtypeStruct((1, d), F32)]
    nbytes = _nbytes((tm, tk), BF16) + _nbytes((tk, d), BF16) + 7 * _nbytes((tm, d), F32)
    return _call(body, name, (m // tm, nk), in_specs, out_specs, out_shape, args, scratch=[pltpu.VMEM((tm, d), F32)],
                 sem=("arbitrary", "arbitrary"), nbytes=nbytes, deps=deps)


def _rms(x):
    return lax.rsqrt(jnp.mean(x * x, axis=-1, keepdims=True) + EPS)


def _rms_bwd(dz, a, g):
    r = _rms(a)
    dzg = dz * g
    da = r * dzg - a * (r * r * r) * jnp.mean(dzg * a, axis=-1, keepdims=True)
    return da, dz * (a * r)


def _row_spec(tr, width):
    return pl.BlockSpec((tr, width), lambda i: (i, 0))


def _vec_spec(width):
    return pl.BlockSpec((1, width), lambda i: (0, 0))


def _rms_fwd(x, g, name):
    t, d = x.shape
    tr = min(ROW_BLOCK, t)

    def body(x_ref, g_ref, h_ref):
        xv = x_ref[...]
        h_ref[...] = ((xv * _rms(xv)) * g_ref[...]).astype(BF16)

    return _call(body, name, (t // tr,), [_row_spec(tr, d), _vec_spec(d)], [_row_spec(tr, d)],
                 [jax.ShapeDtypeStruct((t, d), BF16)], [x, g], sem=("parallel",), nbytes=3 * _nbytes((tr, d), F32))[0]


def _sigmoid(x):
    return 1.0 / (1.0 + jnp.exp(-x))


def _ffn_in(h, w_gu_t, name):
    t, d = h.shape
    tm, tn = _tile(t, 1024), _tile(D_FF, MM_TN)
    n_tiles = D_FF // tn

    def body(h_ref, wg_ref, wu_ref, g_ref, u_ref, a_ref):
        hv = h_ref[...]
        gate = lax.dot_general(hv, wg_ref[...], _DIMS["nt"], preferred_element_type=F32).astype(BF16)
        up = lax.dot_general(hv, wu_ref[...], _DIMS["nt"], preferred_element_type=F32).astype(BF16)
        g_ref[...] = gate
        u_ref[...] = up
        gate, up = gate.astype(F32), up.astype(F32)
        a_ref[...] = ((gate * _sigmoid(gate)) * up).astype(BF16)

    tile = pl.BlockSpec((tm, tn), lambda i, j: (i, j))
    shape = jax.ShapeDtypeStruct((t, D_FF), BF16)
    return _call(body, name, (t // tm, n_tiles),
                 [pl.BlockSpec((tm, d), lambda i, j: (i, 0)), pl.BlockSpec((tn, d), lambda i, j: (j, 0)),
                  pl.BlockSpec((tn, d), lambda i, j: (j + n_tiles, 0))],
                 [tile, tile, tile], [shape, shape, shape], [h, w_gu_t, w_gu_t], sem=("parallel", "parallel"),
                 nbytes=_nbytes((tm, d), BF16) + 2 * _nbytes((tn, d), BF16) + 5 * _nbytes((tm, tn), F32))


def _ffn_in_bwd(dd, w_d, gate, up, name):
    t, d = dd.shape
    tm, tn = _tile(t, 1024), _tile(D_FF, MM_TN)

    def body(dd_ref, w_ref, g_ref, u_ref, o_ref):
        da = lax.dot_general(dd_ref[...], w_ref[...], _DIMS["nt"], preferred_element_type=F32)
        gate, up = g_ref[...].astype(F32), u_ref[...].astype(F32)
        s = _sigmoid(gate)
        o_ref[0] = (da * up * (s * (1.0 + gate * (1.0 - s)))).astype(BF16)
        o_ref[1] = (da * (gate * s)).astype(BF16)

    tile = pl.BlockSpec((tm, tn), lambda i, j: (i, j))
    return _call(body, name, (t // tm, D_FF // tn),
                 [pl.BlockSpec((tm, d), lambda i, j: (i, 0)), pl.BlockSpec((tn, d), lambda i, j: (j, 0)), tile, tile],
                 [pl.BlockSpec((2, tm, tn), lambda i, j: (0, i, j))], [jax.ShapeDtypeStruct((2, t, D_FF), BF16)],
                 [dd, w_d, gate, up], sem=("parallel", "parallel"),
                 nbytes=_nbytes((tm, d), BF16) + _nbytes((tn, d), BF16) + 6 * _nbytes((tm, tn), F32))[0]


def _split3(x):
    hi = x.astype(BF16)
    r1 = x - hi.astype(F32)
    mid = r1.astype(BF16)
    lo = (r1 - mid.astype(F32)).astype(BF16)
    return hi, mid, lo


def _dot_exact(mat01, x):
    hi, mid, lo = _split3(x)
    out = jnp.dot(mat01, hi, preferred_element_type=F32)
    out = out + jnp.dot(mat01, mid, preferred_element_type=F32)
    return out + jnp.dot(mat01, lo, preferred_element_type=F32)


def _dot_exact_rhs(x, mat01):
    hi, mid, lo = _split3(x)
    out = jnp.dot(hi, mat01, preferred_element_type=F32)
    out = out + jnp.dot(mid, mat01, preferred_element_type=F32)
    return out + jnp.dot(lo, mat01, preferred_element_type=F32)


def _tri(lower):
    r = lax.broadcasted_iota(jnp.int32, (CHUNK, CHUNK), 0)
    c = lax.broadcasted_iota(jnp.int32, (CHUNK, CHUNK), 1)
    return jnp.where(r >= c if lower else r <= c, 1.0, 0.0).astype(BF16)


def _log_sigmoid(z):
    return jnp.minimum(z, 0.0) - jnp.log(1.0 + jnp.exp(-jnp.abs(z)))


def _forget_cumsum(proj, b_pad, name):
    t = proj.shape[0]
    nb = t // CHUNK

    def body(f_ref, b_ref, c_ref):
        tri = _tri(True)
        b = b_ref[...]

        def blk(i, carry):
            rows = pl.ds(pl.multiple_of(i * CHUNK, CHUNK), CHUNK)
            cs = _dot_exact(tri, _log_sigmoid(f_ref[rows, :] + b)) + carry
            c_ref[rows, :] = cs
            return cs[CHUNK - 1:CHUNK, :]

        lax.fori_loop(0, nb, blk, jnp.zeros((1, HEAD_DIM), F32))

    return _call(body, name, (1,),
                 [pl.BlockSpec((t, HEAD_DIM), lambda i: (0, CB_F)), pl.BlockSpec((1, HEAD_DIM), lambda i: (0, 0))],
                 [pl.BlockSpec((t, HEAD_DIM), lambda i: (0, 0))], [jax.ShapeDtypeStruct((t, HEAD_DIM), F32)],
                 [proj, b_pad], sem=("arbitrary",), nbytes=2 * _nbytes((t, HEAD_DIM), F32))[0]


def _causal(block):
    r = lax.broadcasted_iota(jnp.int32, (block, block), 0)
    c = lax.broadcasted_iota(jnp.int32, (block, block), 1)
    return c <= r


def _lanes(x, width):
    return jnp.concatenate([x] * (width // HEAD_DIM), axis=1)


def _chunks(t, blk):
    return [slice(r * blk, (r + 1) * blk) for r in range(t // blk)]


def _lane_of_head(x, h):
    lane = lax.broadcasted_iota(jnp.int32, x.shape, 1)
    return jnp.broadcast_to(jnp.sum(jnp.where(lane == h, x, 0.0), axis=1, keepdims=True), x.shape)


def _causal_t(block):
    r = lax.broadcasted_iota(jnp.int32, (block, block), 0)
    c = lax.broadcasted_iota(jnp.int32, (block, block), 1)
    return r <= c


def _attn_fwd_t(proj, c_pad, c_row, name, comm=()):
    t = proj.shape[0]
    blk = min(ATTN_BLOCK, t)
    nq = t // blk
    scale = HEAD_DIM ** -0.5

    def body(q_ref, k_ref, v_ref, cp_ref, cr_ref, y_ref, e_ref, kb_s, vt_s, cb_s):
        h, i = pl.program_id(0), pl.program_id(1)

        @pl.when(i == 0)
        def _():
            for rows in _chunks(t, blk):
                kb_s[rows, :] = k_ref[rows, :].astype(BF16)
                vt_s[:, rows] = v_ref[rows, :].T.astype(BF16)
                cb_s[rows, :] = _lane_of_head(cp_ref[rows, :], h)

        q = (q_ref[...] * scale).astype(BF16)
        ci = cr_ref[:, pl.ds(pl.multiple_of(i * blk, blk), blk)]

        def step(j, carry, diagonal):
            m, l, acc = carry
            rows = pl.ds(pl.multiple_of(j * blk, blk), blk)
            s = lax.dot_general(kb_s[rows, :], q, _DIMS["nt"], preferred_element_type=F32)
            s = s + ci - _lanes(cb_s[rows, :], blk)
            if diagonal:
                s = jnp.where(_causal_t(blk), s, NEG)
            m_new = jnp.maximum(m, jnp.max(s, axis=0, keepdims=True))
            alpha = jnp.exp(m - m_new)
            p = jnp.exp(s - m_new)
            l = alpha * l + jnp.sum(p, axis=0, keepdims=True)
            acc = alpha * acc + jnp.dot(vt_s[:, rows], p.astype(BF16), preferred_element_type=F32)
            return m_new, l, acc

        init = (jnp.full((1, blk), NEG, F32), jnp.zeros((1, blk), F32), jnp.zeros((HEAD_DIM, blk), F32))
        carry = lax.fori_loop(0, i, lambda j, c: step(j, c, False), init)
        m, l, acc = step(i, carry, True)
        y_ref[...] = (acc / l).T
        e_ref[...] = ci - (m + jnp.log(l))

    nbytes = 3 * _nbytes((t, HEAD_DIM), F32) + 4 * _nbytes((blk, HEAD_DIM), F32) + 4 * _nbytes((blk, blk), F32)
    return _call(body, name, (N_HEADS, nq),
                 [pl.BlockSpec((blk, HEAD_DIM), lambda h, i: (i, CB_QKV + 3 * h)),
                  pl.BlockSpec((t, HEAD_DIM), lambda h, i: (0, CB_QKV + 3 * h + 1)),
                  pl.BlockSpec((t, HEAD_DIM), lambda h, i: (0, CB_QKV + 3 * h + 2)),
                  pl.BlockSpec((t, HEAD_DIM), lambda h, i: (0, 0)),
                  pl.BlockSpec((None, 1, t), lambda h, i: (h, 0, 0))],
                 [pl.BlockSpec((blk, HEAD_DIM), lambda h, i: (i, h)),
                  pl.BlockSpec((None, 1, blk), lambda h, i: (h, 0, i))],
                 [jax.ShapeDtypeStruct((t, D_MODEL), F32), jax.ShapeDtypeStruct((N_HEADS, 1, t), F32)],
                 [proj, proj, proj, c_pad, c_row],
                 scratch=[pltpu.VMEM((t, HEAD_DIM), BF16), pltpu.VMEM((HEAD_DIM, t), BF16),
                          pltpu.VMEM((t, HEAD_DIM), F32)],
                 sem=("parallel", "arbitrary"), nbytes=nbytes, comm=comm)


def _attn_bwd_t(proj, dy, e_row, delta_b, c_pad, dproj, name, deps=()):
    t = proj.shape[0]
    blk = min(ATTN_BLOCK, t)
    nb = t // blk
    scale = HEAD_DIM ** -0.5

    def body(q_ref, k_ref, v_ref, dy_ref, e_ref, dl_ref, cp_ref, dproj_in, dqkv_ref, dck_ref, dcq_ref,
             qb_s, qt_s, dyb_s, dyt_s, dl_s, dqt_acc):
        del dproj_in
        h, j = pl.program_id(0), pl.program_id(1)

        @pl.when(j == 0)
        def _():
            for rows in _chunks(t, blk):
                qs = q_ref[rows, :] * scale
                qb_s[rows, :] = qs.astype(BF16)
                qt_s[:, rows] = qs.T.astype(BF16)
                dyr = dy_ref[rows, :]
                dyb_s[rows, :] = dyr.astype(BF16)
                dyt_s[:, rows] = dyr.T.astype(BF16)
                dl_s[:, rows] = dl_ref[rows, :].T[0:1, :]
            dqt_acc[...] = jnp.zeros_like(dqt_acc)
            dcq_ref[...] = jnp.zeros_like(dcq_ref)

        kf = k_ref[...]
        kb, kt = kf.astype(BF16), kf.T.astype(BF16)
        vb = v_ref[...].astype(BF16)
        cj = _lanes(_lane_of_head(cp_ref[...], h), blk)

        def step(i, carry, diagonal):
            dkt, dvt, dkey = carry
            cols = pl.ds(pl.multiple_of(i * blk, blk), blk)
            s = lax.dot_general(kb, qb_s[cols, :], _DIMS["nt"], preferred_element_type=F32) + e_ref[:, cols] - cj
            if diagonal:
                s = jnp.where(_causal_t(blk), s, NEG)
            p = jnp.exp(s)
            dp = lax.dot_general(vb, dyb_s[cols, :], _DIMS["nt"], preferred_element_type=F32)
            ds = p * (dp - dl_s[:, cols])
            pb, dsb = p.astype(BF16), ds.astype(BF16)
            dvt = dvt + lax.dot_general(dyt_s[:, cols], pb, _DIMS["nt"], preferred_element_type=F32)
            dkt = dkt + lax.dot_general(qt_s[:, cols], dsb, _DIMS["nt"], preferred_element_type=F32)
            dqt_acc[:, cols] += jnp.dot(kt, dsb, preferred_element_type=F32) * scale
            dcq_ref[:, cols] += jnp.sum(ds, axis=0, keepdims=True)
            for k in range(blk // HEAD_DIM):
                dkey = dkey + ds[:, k * HEAD_DIM:(k + 1) * HEAD_DIM]
            return dkt, dvt, dkey

        zeros = jnp.zeros((HEAD_DIM, blk), F32)
        carry = step(j, (zeros, zeros, jnp.zeros((blk, HEAD_DIM), F32)), True)
        dkt, dvt, dkey = lax.fori_loop(j + 1, nb, lambda i, c: step(i, c, False), carry)
        mine = pl.ds(pl.multiple_of(j * blk, blk), blk)
        dqkv_ref[:, :HEAD_DIM] = dqt_acc[:, mine].T.astype(BF16)
        dqkv_ref[:, HEAD_DIM:2 * HEAD_DIM] = dkt.T.astype(BF16)
        dqkv_ref[:, 2 * HEAD_DIM:] = dvt.T.astype(BF16)
        dck_ref[...] = -jnp.sum(dkey.T, axis=0, keepdims=True)

    full = lambda cb: pl.BlockSpec((t, HEAD_DIM), lambda h, j: (0, cb(h)))
    head = lambda h: h
    row = pl.BlockSpec((None, 1, t), lambda h, j: (h, 0, 0))
    nbytes = (8 * _nbytes((t, HEAD_DIM), F32) + 4 * _nbytes((blk, HEAD_DIM), F32) + 6 * _nbytes((blk, blk), F32))
    return _call(body, name, (N_HEADS, nb),
                 [full(lambda h: CB_QKV + 3 * h),
                  pl.BlockSpec((blk, HEAD_DIM), lambda h, j: (j, CB_QKV + 3 * h + 1)),
                  pl.BlockSpec((blk, HEAD_DIM), lambda h, j: (j, CB_QKV + 3 * h + 2)),
                  full(head), row, full(head),
                  pl.BlockSpec((blk, HEAD_DIM), lambda h, j: (j, 0)),
                  pl.BlockSpec(memory_space=pl.ANY)],
                 [pl.BlockSpec((blk, QKV_WIDTH), lambda h, j: (j, (GATED + F_WIDTH) // QKV_WIDTH + h)),
                  pl.BlockSpec((None, 1, blk), lambda h, j: (h, 0, j)),
                  row],
                 [jax.ShapeDtypeStruct(dproj.shape, dproj.dtype), jax.ShapeDtypeStruct((N_HEADS, 1, t), F32),
                  jax.ShapeDtypeStruct((N_HEADS, 1, t), F32)],
                 [proj, proj, proj, dy, e_row, delta_b, c_pad, dproj],
                 scratch=[pltpu.VMEM((t, HEAD_DIM), BF16), pltpu.VMEM((HEAD_DIM, t), BF16),
                          pltpu.VMEM((t, HEAD_DIM), BF16), pltpu.VMEM((HEAD_DIM, t), BF16),
                          pltpu.VMEM((1, t), F32), pltpu.VMEM((HEAD_DIM, t), F32)],
                 sem=("parallel", "arbitrary"), nbytes=nbytes, aliases={7: 0}, deps=deps)


def _forget_bwd_t(proj, b_pad, dc, dproj, name):
    t = proj.shape[0]
    tr = min(ROW_BLOCK, t)
    nb = t // tr
    rev = lambda i: nb - 1 - i

    def body(f_ref, b_ref, dc_ref, dproj_in, df_ref, db_ref, run_ref):
        del dproj_in

        @pl.when(pl.program_id(0) == 0)
        def _():
            run_ref[...] = jnp.zeros_like(run_ref)
            db_ref[...] = jnp.zeros_like(db_ref)

        tri = _tri(False)
        b = b_ref[...]
        lane = lax.broadcasted_iota(jnp.int32, (CHUNK, HEAD_DIM), 1)
        df_ref[...] = jnp.zeros_like(df_ref)
        for n in reversed(range(tr // CHUNK)):
            rows = slice(n * CHUNK, (n + 1) * CHUNK)
            dlogf = _dot_exact(tri, dc_ref[rows, :]) + run_ref[...]
            run_ref[...] = dlogf[0:1, :]
            z = f_ref[rows, :] + b
            e = jnp.exp(-jnp.abs(z))
            sig_neg = jnp.where(z >= 0.0, e, 1.0) / (1.0 + e)
            df = jnp.where(lane < N_HEADS, dlogf * sig_neg, 0.0)
            df_ref[rows, :HEAD_DIM] = df.astype(BF16)
            db_ref[...] += jnp.sum(df, axis=0, keepdims=True)

    return _call(body, name, (nb,),
                 [pl.BlockSpec((tr, HEAD_DIM), lambda i: (rev(i), CB_F)), pl.BlockSpec((1, HEAD_DIM), lambda i: (0, 0)),
                  pl.BlockSpec((tr, HEAD_DIM), lambda i: (rev(i), 0)), pl.BlockSpec(memory_space=pl.ANY)],
                 [pl.BlockSpec((tr, F_WIDTH), lambda i: (rev(i), GATED // F_WIDTH)),
                  pl.BlockSpec((1, HEAD_DIM), lambda i: (0, 0))],
                 [jax.ShapeDtypeStruct(dproj.shape, dproj.dtype), jax.ShapeDtypeStruct((1, HEAD_DIM), F32)],
                 [proj, b_pad, dc, dproj], scratch=[pltpu.VMEM((1, HEAD_DIM), F32)],
                 sem=("arbitrary",), nbytes=4 * _nbytes((tr, HEAD_DIM), F32), aliases={3: 0})


_GELU_K = math.sqrt(2.0 / math.pi)
_GELU_C = 0.044715


def _gelu(x):
    t = jnp.tanh(_GELU_K * (x + _GELU_C * (x * x * x)))
    return 0.5 * x * (1.0 + t), t


def _gelu_grad(x, t):
    return 0.5 * (1.0 + t) + 0.5 * x * (1.0 - t * t) * (_GELU_K * (1.0 + 3.0 * _GELU_C * (x * x)))


def _layernorm_stats(a):
    mu = jnp.mean(a, axis=-1, keepdims=True)
    xc = a - mu
    r = lax.rsqrt(jnp.mean(xc * xc, axis=-1, keepdims=True) + EPS)
    return xc * r, r


def _group(g):
    return slice(g * CHUNK, (g + 1) * CHUNK)


def _fixed(shape):
    return pl.BlockSpec(shape, lambda i: (0,) * len(shape))


def _mix_fwd(proj, y_b, w_s, b_cols, g_v, name):
    t = proj.shape[0]
    step_rows = min(MIX_ROWS, t)

    def body(u_ref, vs_ref, ga_ref, gb_ref, yb_ref, w_ref, b_ref, gv_ref, o_ref):
        mask = _causal(CHUNK)
        w = [jnp.where(mask, w_ref[g], 0.0).astype(BF16) for g in range(N_GROUPS)]
        for rows in _chunks(step_rows, CHUNK):
            a_u, _ = _gelu(u_ref[rows, :])
            a_v, _ = _gelu(vs_ref[rows, :])
            xhat, _ = _layernorm_stats(a_v)
            vn = (xhat * gv_ref[...]).astype(BF16)
            sa, sb = _sigmoid(ga_ref[rows, :]), _sigmoid(gb_ref[rows, :])
            yb = yb_ref[rows, :]
            for g in range(N_GROUPS):
                cols = _group(g)
                mixed = jnp.dot(w[g], vn[:, cols], preferred_element_type=F32) + b_ref[:, g:g + 1]
                o_ref[rows, cols] = (sa[:, cols] * (a_u[:, cols] * mixed) + sb[:, cols] * yb[:, cols]).astype(BF16)

    gated = lambda k: pl.BlockSpec((step_rows, D_MODEL), lambda i: (i, k))
    return _call(body, name, (t // step_rows,),
                 [gated(0), gated(1), gated(2), gated(3), _row_spec(step_rows, D_MODEL),
                  _fixed((N_GROUPS, CHUNK, CHUNK)), _fixed((CHUNK, N_GROUPS)), _fixed((1, D_MODEL))],
                 [_row_spec(step_rows, D_MODEL)], [jax.ShapeDtypeStruct((t, D_MODEL), BF16)],
                 [proj, proj, proj, proj, y_b, w_s, b_cols, g_v], sem=("parallel",),
                 nbytes=16 * _nbytes((step_rows, D_MODEL), F32))[0]


def _mix_bwd(proj, y_b, dmerged, w_s, b_cols, g_v, sel, name, deps=()):
    t = proj.shape[0]
    step_rows = min(MIX_BWD_ROWS, t)

    def body(u_ref, vs_ref, ga_ref, gb_ref, yb_ref, dm_ref, w_ref, b_ref, gv_ref, sel_ref,
             dg_ref, dyb_ref, delta_ref, dw_ref, dbt_ref, dgv_ref):
        @pl.when(pl.program_id(0) == 0)
        def _():
            dw_ref[...] = jnp.zeros_like(dw_ref)
            dbt_ref[...] = jnp.zeros_like(dbt_ref)
            dgv_ref[...] = jnp.zeros_like(dgv_ref)

        gv = gv_ref[...]
        mask = _causal(CHUNK)
        w = [jnp.where(mask, w_ref[g], 0.0).astype(BF16) for g in range(N_GROUPS)]
        for rows in _chunks(step_rows, CHUNK):
            u, vs = u_ref[rows, :], vs_ref[rows, :]
            a_u, t_u = _gelu(u)
            a_v, t_v = _gelu(vs)
            xhat, r = _layernorm_stats(a_v)
            vn = (xhat * gv).astype(BF16)
            sa, sb = _sigmoid(ga_ref[rows, :]), _sigmoid(gb_ref[rows, :])
            yb, dm = yb_ref[rows, :], dm_ref[rows, :].astype(F32)
            dyb = dm * sb
            dyb_ref[rows, :] = dyb
            dyb_yb = dyb * yb
            dg_ref[rows, 3 * D_MODEL:] = (dm * yb * (sb * (1.0 - sb))).astype(BF16)
            dya = dm * sa
            dmixed_parts, dvn_parts = [], []
            for g in range(N_GROUPS):
                cols = _group(g)
                delta_ref[rows, cols] = jnp.broadcast_to(jnp.sum(dyb_yb[:, cols], axis=1, keepdims=True),
                                                         (CHUNK, CHUNK))
                mixed = jnp.dot(w[g], vn[:, cols], preferred_element_type=F32) + b_ref[:, g:g + 1]
                y_a = a_u[:, cols] * mixed
                dg_ref[rows, 2 * D_MODEL + g * CHUNK:2 * D_MODEL + (g + 1) * CHUNK] = (
                    dm[:, cols] * y_a * (sa[:, cols] * (1.0 - sa[:, cols]))).astype(BF16)
                dg_ref[rows, cols] = (dya[:, cols] * mixed * _gelu_grad(u[:, cols], t_u[:, cols])).astype(BF16)
                dmixed = dya[:, cols] * a_u[:, cols]
                dmb = dmixed.astype(BF16)
                dw = lax.dot_general(dmb, vn[:, cols], _DIMS["nt"], preferred_element_type=F32)
                dw_ref[g] += jnp.where(mask, dw, 0.0)
                dvn_parts.append(lax.dot_general(w[g], dmb, _DIMS["tn"], preferred_element_type=F32))
                dmixed_parts.append(dmixed)
            dmixed_all = jnp.concatenate(dmixed_parts, axis=1)
            dvn = jnp.concatenate(dvn_parts, axis=1)
            dbt_ref[...] += _dot_exact_rhs(dmixed_all, sel_ref[...])
            dgv_ref[...] += jnp.sum(dvn * xhat, axis=0, keepdims=True)
            dxh = dvn * gv
            da_v = r * (dxh - jnp.mean(dxh, axis=-1, keepdims=True)
                        - xhat * jnp.mean(dxh * xhat, axis=-1, keepdims=True))
            dg_ref[rows, D_MODEL:2 * D_MODEL] = (da_v * _gelu_grad(vs, t_v)).astype(BF16)

    row = lambda width: _row_spec(step_rows, width)
    gated = lambda k: pl.BlockSpec((step_rows, D_MODEL), lambda i: (i, k))
    return _call(body, name, (t // step_rows,),
                 [gated(0), gated(1), gated(2), gated(3), row(D_MODEL), row(D_MODEL),
                  _fixed((N_GROUPS, CHUNK, CHUNK)), _fixed((CHUNK, N_GROUPS)), _fixed((1, D_MODEL)),
                  _fixed((D_MODEL, HEAD_DIM))],
                 [row(GATED), row(D_MODEL), row(D_MODEL),
                  _fixed((N_GROUPS, CHUNK, CHUNK)), _fixed((CHUNK, HEAD_DIM)), _fixed((1, D_MODEL))],
                 [jax.ShapeDtypeStruct((t, IN_PAD), BF16), jax.ShapeDtypeStruct((t, D_MODEL), F32),
                  jax.ShapeDtypeStruct((t, D_MODEL), F32), jax.ShapeDtypeStruct((N_GROUPS, CHUNK, CHUNK), F32),
                  jax.ShapeDtypeStruct((CHUNK, HEAD_DIM), F32), jax.ShapeDtypeStruct((1, D_MODEL), F32)],
                 [proj, proj, proj, proj, y_b, dmerged, w_s, b_cols, g_v, sel], sem=("arbitrary",),
                 nbytes=24 * _nbytes((step_rows, D_MODEL), F32), deps=deps)


def _adamw_update(w_ref, g_ref, m_ref, v_ref, d_ref, nm_ref, nv_ref):
    gv = g_ref[...]
    nm = ADAM_B1 * m_ref[...] + (1.0 - ADAM_B1) * gv
    nv = ADAM_B2 * v_ref[...] + (1.0 - ADAM_B2) * (gv * gv)
    m_hat = nm / (1.0 - ADAM_B1 ** ADAM_STEP)
    v_hat = nv / (1.0 - ADAM_B2 ** ADAM_STEP)
    d_ref[...] = -ADAM_LR * (m_hat / (jnp.sqrt(v_hat) + ADAM_EPS) + ADAM_WD * w_ref[...])
    nm_ref[...] = nm
    nv_ref[...] = nv


def _adamw(w, g, m, v, name):
    r, c = w.shape
    tr = _tile(r, 256, unit=8)

    def body(w_ref, g_ref, m_ref, v_ref, d_ref, nm_ref, nv_ref):
        _adamw_update(w_ref, g_ref, m_ref, v_ref, d_ref, nm_ref, nv_ref)

    spec = pl.BlockSpec((tr, c), lambda i: (i, 0))
    shape = jax.ShapeDtypeStruct((r, c), F32)
    return _call(body, name, (r // tr,), [spec] * 4, [spec] * 3, [shape] * 3, [w, g, m, v], sem=("parallel",),
                 nbytes=7 * _nbytes((tr, max(c, 128)), F32))


def _adamw_cols(w, g, m, v, name):
    layers, r, c = w.shape
    tc = 256

    def body(w_ref, g_ref, m_ref, v_ref, d_ref, nm_ref, nv_ref):
        _adamw_update(w_ref, g_ref, m_ref, v_ref, d_ref, nm_ref, nv_ref)

    spec = pl.BlockSpec((None, r, tc), lambda l, j: (l, 0, j))
    shape = jax.ShapeDtypeStruct(w.shape, F32)
    return _call(body, name, (layers, c // tc), [spec] * 4, [spec] * 3, [shape] * 3, [w, g, m, v],
                 sem=("parallel", "parallel"), nbytes=7 * _nbytes((r, tc), F32))


def _pad_rows(a, rows):
    return jnp.pad(a, ((0, rows - a.shape[0]), (0, 0)))


def _unpack_in(gathered):
    i = jnp.arange(16)[None, :, None]
    b = jnp.arange(N_DEV)[:, None, None]
    head = jnp.where(i < b, jnp.roll(gathered[:, IN_STRIDE:IN_STRIDE + 16], 1, axis=0), gathered[:, :16])
    nat = jnp.concatenate([head, gathered[:, 16:IN_STRIDE]], axis=1).reshape(N_DEV * IN_STRIDE, D_MODEL)
    qkv = nat[2 * D_MODEL:5 * D_MODEL].reshape(3, N_HEADS, HEAD_DIM, D_MODEL)
    qkv = jnp.transpose(qkv, (1, 0, 2, 3)).reshape(3 * D_MODEL, D_MODEL)
    f = _pad_rows(gathered[N_DEV - 1, IN_STRIDE:IN_STRIDE + N_HEADS], F_WIDTH)
    return jnp.concatenate([nat[:2 * D_MODEL], nat[5 * D_MODEL:7 * D_MODEL], f, qkv], axis=0)


def _pack_grad_in(dw_in_t):
    qkv = dw_in_t[GATED + F_WIDTH:].reshape(N_HEADS, 3, HEAD_DIM, D_MODEL)
    qkv = jnp.transpose(qkv, (1, 0, 2, 3)).reshape(3 * D_MODEL, D_MODEL)
    return jnp.concatenate([dw_in_t[:2 * D_MODEL], qkv, dw_in_t[2 * D_MODEL:GATED],
                            dw_in_t[GATED:GATED + IN_ROWS - N_DEV * IN_STRIDE]], axis=0)


def kernel(x, mix_pre_g, w_in, b_forget, sgu_norm_g, w_spatial, b_spatial, w_out, mix_post_g, ffn_pre_g, w_gate, w_up, w_down, ffn_post_g, loss_target, m_mix_pre_g, m_w_in, m_b_forget, m_sgu_norm_g, m_w_spatial, m_b_spatial, m_w_out, m_mix_post_g, m_ffn_pre_g, m_w_gate, m_w_up, m_w_down, m_ffn_post_g, v_mix_pre_g, v_w_in, v_b_forget, v_sgu_norm_g, v_w_spatial, v_b_spatial, v_w_out, v_mix_post_g, v_ffn_pre_g, v_w_gate, v_w_up, v_w_down, v_ffn_post_g):
    depth = w_in.shape[0]
    assert depth == 2
    t = x.shape[1]
    x0 = x.reshape(t, D_MODEL)
    target = loss_target.reshape(t, D_MODEL)
    sel = (jnp.arange(D_MODEL)[:, None] // HEAD_DIM == jnp.arange(HEAD_DIM)[None, :]).astype(BF16)
    vec = lambda a, l: a[l][None, :]
    zero_row = jnp.zeros((1, D_MODEL), F32)

    me = 4 * lax.axis_index("x") + 2 * lax.axis_index("y") + lax.axis_index("c")
    p_in = [lax.dynamic_update_slice(jnp.zeros((P_IN, D_MODEL), BF16), w_in[l].T.astype(BF16), (me, 0))
            for l in range(depth)]
    p_rest = [[w_out[l].astype(BF16), jnp.stack([w_gate[l].T, w_up[l].T]).astype(BF16), w_down[l].astype(BF16)]
              for l in range(depth)]
    w_in_t = [None] * depth
    w_rest = [None] * depth
    w_in_t[0] = _unpack_in(_exchange(("gather", 0), p_in[0], "gather_w_in_0")[0])

    saved = []
    xl = x0
    h = _rms_fwd(xl, vec(mix_pre_g, 0), "rms_in_0")
    dy = loss = None
    for l in range(depth):
        b_pad = jnp.pad(b_forget[l], (0, HEAD_DIM - N_HEADS))[None, :]
        b_cols = b_spatial[l].T
        proj = _matmul(h, w_in_t[l], "nt", F32, f"proj_{l}")
        c_pad = _forget_cumsum(proj, b_pad, f"forget_cumsum_{l}")
        c_row = c_pad[:, :N_HEADS].T[:, None, :]
        riders = [(("gather", 0), p_rest[l][0]), (("gather", 1), p_rest[l][1]), (("gather", 0), p_rest[l][2])]
        riders += [(("gather", 0), p_in[l + 1])] if l + 1 < depth else []
        y_b, e_row, *arrived = _attn_fwd_t(proj, c_pad, c_row, f"attn_fwd_{l}", comm=riders)
        w_rest[l] = (arrived[0].reshape(D_MODEL, D_MODEL), arrived[1].reshape(2 * D_FF, D_MODEL),
                     arrived[2].reshape(D_FF, D_MODEL))
        if l + 1 < depth:
            w_in_t[l + 1] = _unpack_in(arrived[3])
        w_o, w_gu_t, w_d = w_rest[l]
        merged = _mix_fwd(proj, y_b, w_spatial[l], b_cols, vec(sgu_norm_g, l), f"mix_fwd_{l}")
        o, x1, h2 = _matmul_post_norm(merged, w_o, xl, vec(mix_post_g, l), f"out_proj_{l}", g_next=vec(ffn_pre_g, l))
        gate, up, act = _ffn_in(h2, w_gu_t, f"ffn_gu_{l}")
        if l + 1 < depth:
            dn, x_next, h_next = _matmul_post_norm(act, w_d, x1, vec(ffn_post_g, l), f"ffn_down_{l}",
                                                   g_next=vec(mix_pre_g, l + 1))
        else:
            dn, dy, loss, dd, dg_ffn_post = _matmul_post_norm(act, w_d, x1, vec(ffn_post_g, l), f"ffn_down_{l}",
                                                              target=target)
        saved.append(dict(x0=xl, h=h, proj=proj, b_pad=b_pad, b_cols=b_cols, c_pad=c_pad, y_b=y_b, e_row=e_row,
                          merged=merged, o=o, x1=x1, h2=h2, gate=gate, up=up, act=act, dn=dn))
        if l + 1 < depth:
            xl, h = x_next, h_next

    g_small = [None] * depth
    flights = {}
    late_token = None
    dx = dy
    for l in reversed(range(depth)):
        s = saved[l]
        w_o, w_gu_t, w_d = w_rest[l]
        dgu = _ffn_in_bwd(dd, w_d, s["gate"], s["up"], f"d_act_{l}")
        dw_d = _matmul(s["act"], dd, "tn", BF16, f"dw_down_{l}")
        dw_gu_t = _matmul(dgu, s["h2"], "tn", BF16, f"dw_gu_{l}", halves=True)
        dx1, dg_ffn_pre, do, dg_mix_post = _matmul_norm_bwd(
            dgu, w_gu_t, dx, s["x1"], vec(ffn_pre_g, l), f"d_h2_{l}", post=(s["o"], vec(mix_post_g, l)), halves=True)
        dmerged = _matmul(do, w_o, "nt", F32, f"d_merged_{l}")
        dw_o = _matmul(s["merged"], do, "tn", BF16, f"dw_out_{l}")
        tokens = []
        for key, dw, base, rows in [("out", dw_o, 0, SH_OUT), ("gate", dw_gu_t, 0, SH_FF), ("up", dw_gu_t, D_FF, SH_FF),
                                    ("down", dw_d, 0, SH_FF)]:
            flights[(key, l)], token = _exchange_start(("windows", base, rows, rows), dw, f"start_grad_{key}_{l}")
            tokens.append(token)
        dproj, dyb, delta_b, dw_s, dbt, dg_v = _mix_bwd(
            s["proj"], s["y_b"], dmerged, w_spatial[l], s["b_cols"], vec(sgu_norm_g, l), sel, f"mix_bwd_{l}",
            deps=tokens)
        db_s = dbt[:, :N_GROUPS].T.reshape(1, D_MODEL)
        tokens = []
        if l == 0:
            g_small[0] = jnp.concatenate([zero_row, dg_v, dg_mix_post, dg_ffn_pre, dg_ffn_post, db_s, zero_row,
                                          zero_row, dw_s.reshape(CHUNK, D_MODEL)], axis=0)
            flights["small"], token = _exchange_start(("spread",),jnp.concatenate([g_small[1], g_small[0]], axis=0),
                                                      "start_small_grads")
            tokens = [token]
        dproj, dc_key, dc_query = _attn_bwd_t(s["proj"], dyb, s["e_row"], delta_b, s["c_pad"], dproj,
                                              f"attn_bwd_{l}", deps=tokens)
        dc = jnp.pad((dc_key + dc_query)[:, 0, :].T, ((0, 0), (0, HEAD_DIM - N_HEADS)))
        dproj, db_f = _forget_bwd_t(s["proj"], s["b_pad"], dc, dproj, f"forget_bwd_{l}")
        db_f_row = jnp.pad(db_f, ((0, 0), (0, D_MODEL - HEAD_DIM)))
        dw_in_t = _matmul(dproj, s["h"], "tn", BF16, f"dw_in_{l}")
        flights[("in", l)], token = _exchange_start(("windows", 0, IN_STRIDE, P_IN), _pack_grad_in(dw_in_t),
                                                    f"start_grad_in_{l}")
        if l > 0:
            dx, dg_mix_pre, dd, dg_ffn_post_below = _matmul_norm_bwd(
                dproj, w_in_t[l], dx1, s["x0"], vec(mix_pre_g, l), f"d_h_{l}",
                post=(saved[l - 1]["dn"], vec(ffn_post_g, l - 1)), deps=[token])
            loss_row = jnp.pad(loss, ((0, 0), (0, D_MODEL - 1)))
            g_small[l] = jnp.concatenate([dg_mix_pre, dg_v, dg_mix_post, dg_ffn_pre, dg_ffn_post, db_s, db_f_row,
                                          loss_row, dw_s.reshape(CHUNK, D_MODEL)], axis=0)
            dg_ffn_post = dg_ffn_post_below
        else:
            dx, dg_mix_pre = _matmul_norm_bwd(dproj, w_in_t[l], dx1, s["x0"], vec(mix_pre_g, l), f"d_h_{l}",
                                              deps=[token])
            late_rows = jnp.concatenate([dg_mix_pre, db_f_row] + [zero_row] * 6, axis=0)
            flights["late"], late_token = _exchange_start(("spread",),late_rows, "start_late_small_grads")
    grad_x = dx.reshape(x.shape)

    landed, after = {}, [grad_x, late_token]
    rest = ["out", "gate", "up", "down"]
    for key in [(n, 1) for n in rest] + [("in", 1)] + [(n, 0) for n in rest] + ["small"]:
        label = key if isinstance(key, str) else f"grad_{key[0]}_{key[1]}"
        landed[key] = _exchange_wait(flights[key], after, f"wait_{label}")
        after = [landed[key]]
    small_sum = _sum_slots(landed["small"], S_ROWS, "sum_small_grads")
    sm = small_sum.reshape(depth, S_ROWS, D_MODEL)[::-1]
    loss = sm[depth - 1, 7, 0]
    g = {
        "sgu_norm_g": sm[:, 1], "mix_post_g": sm[:, 2],
        "ffn_pre_g": sm[:, 3], "ffn_post_g": sm[:, 4], "b_spatial": sm[:, 5].reshape(depth, N_GROUPS, CHUNK),
        "w_spatial": sm[:, 8:].reshape(depth, N_GROUPS, CHUNK, CHUNK),
    }
    for n in rest:
        g["w_" + n] = jnp.stack([_sum_slots(landed[(n, l)], landed[(n, l)].shape[1], f"sum_grad_{n}_{l}")
                                 for l in range(depth)])

    names = ["mix_pre_g", "w_in", "b_forget", "sgu_norm_g", "w_spatial", "b_spatial", "w_out", "mix_post_g",
             "ffn_pre_g", "w_gate", "w_up", "w_down", "ffn_post_g"]
    ws = dict(mix_pre_g=mix_pre_g, w_in=w_in, b_forget=b_forget, sgu_norm_g=sgu_norm_g, w_spatial=w_spatial,
              b_spatial=b_spatial, w_out=w_out, mix_post_g=mix_post_g, ffn_pre_g=ffn_pre_g, w_gate=w_gate, w_up=w_up,
              w_down=w_down, ffn_post_g=ffn_post_g)
    ms = dict(mix_pre_g=m_mix_pre_g, w_in=m_w_in, b_forget=m_b_forget, sgu_norm_g=m_sgu_norm_g, w_spatial=m_w_spatial,
              b_spatial=m_b_spatial, w_out=m_w_out, mix_post_g=m_mix_post_g, ffn_pre_g=m_ffn_pre_g, w_gate=m_w_gate,
              w_up=m_w_up, w_down=m_w_down, ffn_post_g=m_ffn_post_g)
    vs = dict(mix_pre_g=v_mix_pre_g, w_in=v_w_in, b_forget=v_b_forget, sgu_norm_g=v_sgu_norm_g, w_spatial=v_w_spatial,
              b_spatial=v_b_spatial, w_out=v_w_out, mix_post_g=v_mix_post_g, ffn_pre_g=v_ffn_pre_g, w_gate=v_w_gate,
              w_up=v_w_up, w_down=v_w_down, ffn_post_g=v_ffn_post_g)
    updates = {}
    last = ["w_in", "mix_pre_g", "b_forget"]
    transposed = ["w_in", "w_gate", "w_up"]
    for n in [n for n in names if n not in last] + last:
        if n == last[0]:
            sum_in_1 = _sum_slots(landed[("in", 1)], 304, "sum_grad_in_1")
            done = [u[0] for u in updates.values()] + [sum_in_1]
            sum_in_0 = _sum_slots(_exchange_wait(flights[("in", 0)], done, "wait_grad_in_0"), 304, "sum_grad_in_0")
            late = _sum_slots(_exchange_wait(flights["late"], [sum_in_0], "wait_late"), 8, "sum_late_small_grads")
            g["w_in"] = lax.dynamic_slice(jnp.stack([sum_in_0, sum_in_1]), (0, me, 0), (depth, SH_IN, D_MODEL))
            g["mix_pre_g"] = jnp.stack([late[0], sm[1, 0]])
            g["b_forget"] = jnp.stack([late[1, :N_HEADS], sm[1, 6, :N_HEADS]])
        if n in transposed:
            view = lambda a: jnp.swapaxes(a, 1, 2)
            updates[n] = [view(u) for u in _adamw_cols(view(ws[n]), g[n], view(ms[n]), view(vs[n]), f"adamw_{n}")]
            g[n] = view(g[n])
        else:
            flat = (-1, ws[n].shape[-1])
            updates[n] = [u.reshape(ws[n].shape) for u in
                          _adamw(ws[n].reshape(flat), g[n].reshape(flat), ms[n].reshape(flat), vs[n].reshape(flat),
                                 f"adamw_{n}")]
    deltas = [updates[n][0] for n in names]
    new_m = [updates[n][1] for n in names]
    new_v = [updates[n][2] for n in names]
    grads = [g[n].reshape(ws[n].shape) for n in names]
    return (loss, grad_x, *grads, *deltas, *new_m, *new_v)
```

```python
import math

import jax
import jax.numpy as jnp
from jax import lax
from jax.experimental import pallas as pl
from jax.experimental.pallas import tpu as pltpu

F32 = jnp.float32
BF16 = jnp.bfloat16

N_DEV = 8
D_MODEL = 1024
N_HEADS = 8
HEAD_DIM = 128
CHUNK = 128
N_GROUPS = 8
D_FF = 2816
IN_WIDTH = 7 * D_MODEL + N_HEADS
IN_PAD = 7680
EPS = 1e-6
GATED = 4 * D_MODEL
F_WIDTH = 512
QKV_WIDTH = 3 * HEAD_DIM
CB_F = GATED // HEAD_DIM
CB_QKV = (GATED + F_WIDTH) // HEAD_DIM
assert GATED + F_WIDTH + N_HEADS * QKV_WIDTH == IN_PAD and (GATED + F_WIDTH) % QKV_WIDTH == 0

ADAM_LR, ADAM_B1, ADAM_B2, ADAM_EPS, ADAM_WD, ADAM_STEP = 0.001, 0.9, 0.999, 1e-08, 0.01, 10

SH_IN = IN_WIDTH // N_DEV
SH_OUT = D_MODEL // N_DEV
SH_FF = D_FF // N_DEV
P_IN = 912
IN_STRIDE = 7 * HEAD_DIM
IN_ROWS = IN_STRIDE * (N_DEV - 1) + P_IN
S_ROWS = 136

ATTN_BLOCK = 1024
ROW_BLOCK = 512
MIX_ROWS = 1024
MIX_BWD_ROWS = 256
MM_TM, MM_TN, MM_TK = 1536, 1536, 2048
VMEM_CAP = 56 << 20
NEG = -1e30


def _tile(n, cap, unit=128):
    if n <= cap:
        return n
    best = None
    for t in range(unit, cap + 1, unit):
        if n % t == 0:
            best = t
    assert best is not None, (n, cap)
    return best


def _nbytes(shape, dtype):
    return math.prod(shape) * jnp.dtype(dtype).itemsize


_HBM = pl.BlockSpec(memory_space=pltpu.HBM)
MESH = pl.DeviceIdType.MESH

_N_REMOTE = N_DEV - 1


def _exchange_shapes(kind, x):
    if kind[0] == "gather":
        shape = x.shape[:kind[1]] + (N_DEV,) + x.shape[kind[1]:]
    elif kind[0] == "spread":
        shape = (N_DEV,) + x.shape
    else:
        shape = (N_DEV, kind[3]) + x.shape[1:]
    return [jax.ShapeDtypeStruct(shape, x.dtype)]


def _exchange_sems(kind):
    del kind
    return [pltpu.SemaphoreType.DMA((_N_REMOTE,)), pltpu.SemaphoreType.DMA((_N_REMOTE,)), pltpu.SemaphoreType.DMA]


def _plan(kind, x_ref, outs, sems):
    send_sems, recv_sems, local_sem = sems
    x, y, c = lax.axis_index("x"), lax.axis_index("y"), lax.axis_index("c")

    def remote(src, dst, k, to):
        return pltpu.make_async_remote_copy(src_ref=src, dst_ref=dst, send_sem=send_sems.at[k], recv_sem=recv_sems.at[k],
                                            device_id=to, device_id_type=MESH)

    sibling = (x, y, 1 - c)
    if kind[0] == "gather":
        out, = outs
        slot = lambda px, py, pc: out.at[(slice(None),) * kind[1] + (4 * px + 2 * py + pc,)]
        chips = [(1 - x, y), (x, 1 - y), (1 - x, 1 - y)]
        local = [pltpu.make_async_copy(x_ref, slot(x, y, c), local_sem)]
        first = [remote(x_ref, slot(x, y, c), 0, sibling)]
        first += [remote(x_ref, slot(x, y, c), 1 + k, (*chip, c)) for k, chip in enumerate(chips)]
        relays = [(remote(x_ref, slot(*chip, c), 1 + k, (*chip, c)), remote(slot(*chip, c), slot(*chip, c), 4 + k, sibling))
                  for k, chip in enumerate(chips)]
        arrivals = [remote(x_ref, slot(x, y, 1 - c), 0, sibling)]
        arrivals += [remote(x_ref, slot(*chip, 1 - c), 4 + k, sibling) for k, chip in enumerate(chips)]
        return local, first, relays, arrivals
    out, = outs
    mine = 4 * x + 2 * y + c
    if kind[0] == "spread":
        src = lambda d: x_ref
    else:
        _, base, stride, length = kind
        src = lambda d: x_ref.at[pl.ds(pl.multiple_of(base + stride * d, 16), length)]
    first, arrivals = [], []
    for j in range(1, N_DEV):
        peer = (1 - x if j & 4 else x, 1 - y if j & 2 else y, 1 - c if j & 1 else c)
        theirs = 4 * peer[0] + 2 * peer[1] + peer[2]
        first.append(remote(src(theirs), out.at[mine], j - 1, peer))
        arrivals.append(remote(src(theirs), out.at[theirs], j - 1, peer))
    local = [] if local_sem is None else [pltpu.make_async_copy(src(mine), out.at[mine], local_sem)]
    return local, first, [], arrivals


def _start(plan):
    local, first, _, _ = plan
    for cp in local + first:
        cp.start()


def _finish(plan):
    local, first, relays, arrivals = plan
    for arrival, onward in relays:
        arrival.wait_recv()
        onward.start()
    for cp in arrivals:
        cp.wait_recv()
    for cp in first + [onward for _, onward in relays]:
        cp.wait_send()
    for cp in local:
        cp.wait()


def _direct_copies(kind, x_ref, land_ref, send_sems, recv_sems):
    _, first, _, arrivals = _plan(kind, x_ref, [land_ref], (send_sems, recv_sems, None))
    return first, arrivals


_SEM = pl.BlockSpec(memory_space=pltpu.SEMAPHORE)
_DATAFLOW = pltpu.SideEffectType.DATAFLOW_SIDE_EFFECTING


def _exchange_start(kind, x, name):
    me = 4 * lax.axis_index("x") + 2 * lax.axis_index("y") + lax.axis_index("c")
    own = x if kind[0] == "spread" else lax.dynamic_slice_in_dim(x, kind[1] + kind[2] * me, kind[3], 0)
    shape = _exchange_shapes(kind, x)[0].shape
    land = lax.dynamic_update_slice_in_dim(lax.empty(shape, x.dtype), own[None], me, 0)

    def body(x_ref, land_ref, send_sems, recv_sems, x_thru, land_thru, token):
        del x_thru, land_thru
        for cp in _direct_copies(kind, x_ref, land_ref, send_sems, recv_sems)[0]:
            cp.start()
        token[...] = jnp.zeros_like(token)

    *handle, token = pl.pallas_call(
        body, name=name,
        out_shape=(pltpu.SemaphoreType.DMA((_N_REMOTE,)), pltpu.SemaphoreType.DMA((_N_REMOTE,)),
                   pltpu.HBM(x.shape, x.dtype), pltpu.HBM(land.shape, land.dtype), jax.ShapeDtypeStruct((8, 128), F32)),
        in_specs=(_HBM, _HBM), out_specs=(_SEM, _SEM, _HBM, _HBM, pl.BlockSpec(memory_space=pltpu.VMEM)),
        input_output_aliases={0: 2, 1: 3}, compiler_params=pltpu.CompilerParams(has_side_effects=_DATAFLOW),
    )(pltpu.with_memory_space_constraint(x, pltpu.HBM), pltpu.with_memory_space_constraint(land, pltpu.HBM))
    return (kind, *handle), token


def _exchange_wait(handle, after, name):
    kind, send_sems, recv_sems, x_thru, land_thru = handle

    def body(x_ref, land_ref, send_sems, recv_sems, *rest):
        first, arrivals = _direct_copies(kind, x_ref, land_ref, send_sems, recv_sems)
        for cp in first:
            cp.wait_send()
        for cp in arrivals:
            cp.wait_recv()

    return pl.pallas_call(
        body, name=name,
        out_shape=(pltpu.HBM(x_thru.shape, x_thru.dtype), pltpu.HBM(land_thru.shape, land_thru.dtype)),
        in_specs=(_HBM, _HBM, _SEM, _SEM) + (pl.BlockSpec(memory_space=pl.ANY),) * len(after), out_specs=(_HBM, _HBM),
        input_output_aliases={0: 0, 1: 1}, compiler_params=pltpu.CompilerParams(has_side_effects=_DATAFLOW),
    )(x_thru, land_thru, send_sems, recv_sems, *after)[1]


def _exchange(kind, x, name):
    n_out = len(_exchange_shapes(kind, x))

    def body(x_ref, *refs):
        plan = _plan(kind, x_ref, refs[:n_out], refs[n_out:])
        _start(plan)
        _finish(plan)

    return pl.pallas_call(body, name=name, out_shape=_exchange_shapes(kind, x), in_specs=[_HBM],
                          out_specs=[_HBM] * n_out, scratch_shapes=_exchange_sems(kind))(x)


def _call(body, name, grid, in_specs, out_specs, out_shape, args, scratch=(), sem=None, nbytes=0, aliases=None,
          comm=(), deps=()):
    in_specs, out_specs, out_shape, args, scratch = (list(in_specs), list(out_specs), list(out_shape), list(args),
                                                     list(scratch))
    if deps:
        n_real, n_deps, unordered = len(args), len(deps), body

        def body(*refs):
            unordered(*refs[:n_real], *refs[n_real + n_deps:])

        in_specs += [pl.BlockSpec(memory_space=pl.ANY)] * n_deps
        args += list(deps)
    if comm:
        n_in, n_out, n_scr, n_ops = len(args), len(out_shape), len(scratch), len(comm)
        kinds = [kind for kind, _ in comm]
        shapes = [_exchange_shapes(kind, x) for kind, x in comm]
        inner = body

        def body(*refs):
            ins, refs = refs[:n_in], refs[n_in:]
            cins, refs = refs[:n_ops], refs[n_ops:]
            outs, refs = refs[:n_out], refs[n_out:]
            couts = []
            for sh in shapes:
                couts.append(refs[:len(sh)])
                refs = refs[len(sh):]
            own_scratch, sems = refs[:n_scr], refs[n_scr:]
            first = last = None
            for axis, size in enumerate(grid):
                at_start, at_end = pl.program_id(axis) == 0, pl.program_id(axis) == size - 1
                first = at_start if first is None else first & at_start
                last = at_end if last is None else last & at_end
            plans = [_plan(kinds[o], cins[o], couts[o], sems[3 * o:3 * o + 3]) for o in range(n_ops)]

            @pl.when(first)
            def _():
                for p in plans:
                    _start(p)

            inner(*ins, *outs, *own_scratch)

            @pl.when(last)
            def _():
                for p in plans:
                    _finish(p)

        in_specs += [_HBM] * n_ops
        args += [x for _, x in comm]
        for kind, sh in zip(kinds, shapes):
            out_shape += sh
            out_specs += [_HBM] * len(sh)
            scratch += _exchange_sems(kind)
        sem = ("arbitrary",) * len(grid)
    limit = int(min(max(2 * nbytes + (8 << 20), 32 << 20), VMEM_CAP))
    return pl.pallas_call(
        body, name=name, grid=grid, in_specs=in_specs, out_specs=out_specs, out_shape=out_shape,
        scratch_shapes=scratch, input_output_aliases=aliases or {},
        compiler_params=pltpu.CompilerParams(dimension_semantics=sem, vmem_limit_bytes=limit),
    )(*args)


def _sum_slots(x, tr, name):
    n, r, c = x.shape

    def body(x_ref, o_ref):
        acc = x_ref[0].astype(F32)
        for d in range(1, n):
            acc = acc + x_ref[d].astype(F32)
        o_ref[...] = acc

    return _call(body, name, (r // tr,), [pl.BlockSpec((n, tr, c), lambda i: (0, i, 0))],
                 [pl.BlockSpec((tr, c), lambda i: (i, 0))], [jax.ShapeDtypeStruct((r, c), F32)], [x],
                 sem=("parallel",), nbytes=_nbytes((n, tr, c), x.dtype) + _nbytes((tr, c), F32))[0]


_DIMS = {"nn": (((1,), (0,)), ((), ())), "nt": (((1,), (1,)), ((), ())), "tn": (((0,), (0,)), ((), ()))}


def _matmul(a, b, mode, out_dtype, name, comm=(), deps=(), halves=False):
    a_shape = (a.shape[1], 2 * a.shape[2]) if halves else a.shape
    if mode == "nn":
        (m, k), (k2, n) = a_shape, b.shape
    elif mode == "nt":
        (m, k), (n, k2) = a_shape, b.shape
    else:
        (k, m), (k2, n) = a_shape, b.shape
    assert k == k2, (a.shape, b.shape, mode)
    tm, tn, tk = _tile(m, MM_TM), _tile(n, MM_TN), _tile(k, 2 * MM_TK if mode == "tn" else MM_TK)
    if halves and mode == "nn":
        tk = _tile(k // 2, MM_TK)
    if halves and mode == "tn":
        tm = _tile(m // 2, MM_TM)
    nk = k // tk
    dims = _DIMS[mode]
    if halves:
        per_half = (k // 2) // tk if mode == "nn" else (m // 2) // tm
        a_spec = {"nn": pl.BlockSpec((None, tm, tk), lambda i, j, kk: (kk // per_half, i, kk % per_half)),
                  "tn": pl.BlockSpec((None, tk, tm), lambda i, j, kk: (i // per_half, kk, i % per_half))}[mode]
    else:
        a_spec = {"nn": pl.BlockSpec((tm, tk), lambda i, j, kk: (i, kk)),
                  "nt": pl.BlockSpec((tm, tk), lambda i, j, kk: (i, kk)),
                  "tn": pl.BlockSpec((tk, tm), lambda i, j, kk: (kk, i))}[mode]
    b_spec = {"nn": pl.BlockSpec((tk, tn), lambda i, j, kk: (kk, j)),
              "nt": pl.BlockSpec((tn, tk), lambda i, j, kk: (j, kk)),
              "tn": pl.BlockSpec((tk, tn), lambda i, j, kk: (kk, j))}[mode]

    def partial_product(a_ref, b_ref):
        return lax.dot_general(a_ref[...].astype(BF16), b_ref[...].astype(BF16), dims, preferred_element_type=F32)

    if nk == 1:
        def body(a_ref, b_ref, o_ref):
            o_ref[...] = partial_product(a_ref, b_ref).astype(o_ref.dtype)
        scratch = []
    else:
        def body(a_ref, b_ref, o_ref, acc_ref):
            kk = pl.program_id(2)

            @pl.when(kk == 0)
            def _():
                acc_ref[...] = jnp.zeros_like(acc_ref)

            acc_ref[...] += partial_product(a_ref, b_ref)

            @pl.when(kk == nk - 1)
            def _():
                o_ref[...] = acc_ref[...].astype(o_ref.dtype)
        scratch = [pltpu.VMEM((tm, tn), F32)]

    nbytes = (_nbytes((tm, tk), a.dtype) + _nbytes((tk, tn), b.dtype) + _nbytes((tm, tn), out_dtype)
              + _nbytes((tm, tn), F32))
    res = _call(body, name, (m // tm, n // tn, nk), [a_spec, b_spec],
                [pl.BlockSpec((tm, tn), lambda i, j, kk: (i, j))], [jax.ShapeDtypeStruct((m, n), out_dtype)], [a, b],
                scratch=scratch, sem=("parallel", "parallel", "arbitrary"), nbytes=nbytes, comm=comm, deps=deps)
    return res if comm else res[0]


def _matmul_post_norm(a, b, x, g_post, name, g_next=None, target=None):
    (m, k), (k2, d) = a.shape, b.shape
    assert k == k2 and d == D_MODEL
    with_loss = target is not None
    tm, tk = _tile(m, ROW_BLOCK if with_loss else 2 * ROW_BLOCK), _tile(k, MM_TK)
    nk = k // tk

    def body(a_ref, b_ref, x_ref, gp_ref, last_ref, o_ref, first_ref, second_ref, *rest):
        acc_ref = rest[-1]
        i, kk = pl.program_id(0), pl.program_id(1)

        @pl.when(kk == 0)
        def _():
            acc_ref[...] = jnp.zeros_like(acc_ref)

        acc_ref[...] += jnp.dot(a_ref[...], b_ref[...], preferred_element_type=F32)

        @pl.when(kk == nk - 1)
        def _():
            ov = acc_ref[...]
            o_ref[...] = ov
            xn = x_ref[...] + (ov * _rms(ov)) * gp_ref[...]
            if with_loss:
                dd_ref, dg_ref = rest[:2]
                err = xn - last_ref[...]
                dy = err / d
                first_ref[...] = dy
                dd, dg_rows = _rms_bwd(dy, ov, gp_ref[...])
                dd_ref[...] = dd.astype(BF16)

                @pl.when(i == 0)
                def _():
                    second_ref[...] = jnp.zeros_like(second_ref)
                    dg_ref[...] = jnp.zeros_like(dg_ref)

                second_ref[...] += 0.5 * jnp.sum(jnp.mean(err * err, axis=-1, keepdims=True), axis=0, keepdims=True)
                dg_ref[...] += jnp.sum(dg_rows, axis=0, keepdims=True)
            else:
                first_ref[...] = xn
                second_ref[...] = ((xn * _rms(xn)) * last_ref[...]).astype(BF16)

    rows = pl.BlockSpec((tm, d), lambda i, kk: (i, 0))
    vec = pl.BlockSpec((1, d), lambda i, kk: (0, 0))
    second = ((pl.BlockSpec((1, 1), lambda i, kk: (0, 0)), jax.ShapeDtypeStruct((1, 1), F32)) if with_loss
              else (rows, jax.ShapeDtypeStruct((m, d), BF16)))
    nbytes = _nbytes((tm, tk), BF16) + _nbytes((tk, d), BF16) + 6 * _nbytes((tm, d), F32)
    extra_specs = [rows, vec] if with_loss else []
    extra_shapes = [jax.ShapeDtypeStruct((m, d), BF16), jax.ShapeDtypeStruct((1, d), F32)] if with_loss else []
    return _call(body, name, (m // tm, nk),
                 [pl.BlockSpec((tm, tk), lambda i, kk: (i, kk)), pl.BlockSpec((tk, d), lambda i, kk: (kk, 0)), rows, vec,
                  rows if with_loss else vec],
                 [rows, rows, second[0]] + extra_specs,
                 [jax.ShapeDtypeStruct((m, d), F32), jax.ShapeDtypeStruct((m, d), F32), second[1]] + extra_shapes,
                 [a, b, x, g_post, target if with_loss else g_next],
                 scratch=[pltpu.VMEM((tm, d), F32)], sem=("arbitrary", "arbitrary"), nbytes=nbytes)


def _matmul_norm_bwd(a, b, dskip, xin, g_pre, name, post=None, halves=False, deps=()):
    m, k = (a.shape[1], 2 * a.shape[2]) if halves else a.shape
    (k2, d) = b.shape
    assert k == k2 and d == D_MODEL
    tm = _tile(m, ROW_BLOCK if post else 2 * ROW_BLOCK)
    tk = _tile(k // 2 if halves else k, MM_TK)
    nk = k // tk
    n_in = 5 + (2 if post else 0)

    def body(*refs):
        (a_ref, b_ref, dskip_ref, xin_ref, gpre_ref), rest = refs[:5], list(refs[5:n_in])
        outs, acc_ref = list(refs[n_in:-1]), refs[-1]
        i, kk = pl.program_id(0), pl.program_id(1)

        @pl.when(kk == 0)
        def _():
            acc_ref[...] = jnp.zeros_like(acc_ref)

        acc_ref[...] += jnp.dot(a_ref[...], b_ref[...], preferred_element_type=F32)

        @pl.when(kk == nk - 1)
        def _():
            dxin, dg_rows = _rms_bwd(acc_ref[...], xin_ref[...], gpre_ref[...])
            dx = dskip_ref[...] + dxin
            dx_ref, dgpre_ref = outs[0], outs[1]
            dx_ref[...] = dx

            @pl.when(i == 0)
            def _():
                for ref in outs[1::2]:
                    ref[...] = jnp.zeros_like(ref)

            dgpre_ref[...] += jnp.sum(dg_rows, axis=0, keepdims=True)
            if post:
                da, dg_rows = _rms_bwd(dx, rest[0][...], rest[1][...])
                outs[2][...] = da.astype(BF16)
                outs[3][...] += jnp.sum(dg_rows, axis=0, keepdims=True)

    if halves:
        per_half = (k // 2) // tk
        a_spec = pl.BlockSpec((None, tm, tk), lambda i, kk: (kk // per_half, i, kk % per_half))
    else:
        a_spec = pl.BlockSpec((tm, tk), lambda i, kk: (i, kk))
    rows = pl.BlockSpec((tm, d), lambda i, kk: (i, 0))
    vec = pl.BlockSpec((1, d), lambda i, kk: (0, 0))
    args, in_specs = [a, b, dskip, xin, g_pre], [a_spec, pl.BlockSpec((tk, d), lambda i, kk: (kk, 0)), rows, rows, vec]
    out_specs, out_shape = [rows, vec], [jax.ShapeDtypeStruct((m, d), F32), jax.ShapeDtypeStruct((1, d), F32)]
    if post:
        args += list(post)
        in_specs += [rows, vec]
        out_specs += [rows, vec]
        out_shape += [jax.ShapeDtypeStruct((m, d), BF16), jax.ShapeDtypeStruct((1, d), F32)]
    nbytes = _nbytes((tm, tk), BF16) + _nbytes((tk, d), BF16) + 7 * _nbytes((tm, d), F32)
    return _call(body, name, (m // tm, nk), in_specs, out_specs, out_shape, args, scratch=[pltpu.VMEM((tm, d), F32)],
                 sem=("arbitrary", "arbitrary"), nbytes=nbytes, deps=deps)


def _rms(x):
    return lax.rsqrt(jnp.mean(x * x, axis=-1, keepdims=True) + EPS)


def _rms_bwd(dz, a, g):
    r = _rms(a)
    dzg = dz * g
    da = r * dzg - a * (r * r * r) * jnp.mean(dzg * a, axis=-1, keepdims=True)
    return da, dz * (a * r)


def _row_spec(tr, width):
    return pl.BlockSpec((tr, width), lambda i: (i, 0))


def _vec_spec(width):
    return pl.BlockSpec((1, width), lambda i: (0, 0))


def _rms_fwd(x, g, name):
    t, d = x.shape
    tr = min(ROW_BLOCK, t)

    def body(x_ref, g_ref, h_ref):
        xv = x_ref[...]
        h_ref[...] = ((xv * _rms(xv)) * g_ref[...]).astype(BF16)

    return _call(body, name, (t // tr,), [_row_spec(tr, d), _vec_spec(d)], [_row_spec(tr, d)],
                 [jax.ShapeDtypeStruct((t, d), BF16)], [x, g], sem=("parallel",), nbytes=3 * _nbytes((tr, d), F32))[0]


def _sigmoid(x):
    return 1.0 / (1.0 + jnp.exp(-x))


def _ffn_in(h, w_gu_t, name):
    t, d = h.shape
    tm, tn = _tile(t, 1024), _tile(D_FF, MM_TN)
    n_tiles = D_FF // tn

    def body(h_ref, wg_ref, wu_ref, g_ref, u_ref, a_ref):
        hv = h_ref[...]
        gate = lax.dot_general(hv, wg_ref[...], _DIMS["nt"], preferred_element_type=F32).astype(BF16)
        up = lax.dot_general(hv, wu_ref[...], _DIMS["nt"], preferred_element_type=F32).astype(BF16)
        g_ref[...] = gate
        u_ref[...] = up
        gate, up = gate.astype(F32), up.astype(F32)
        a_ref[...] = ((gate * _sigmoid(gate)) * up).astype(BF16)

    tile = pl.BlockSpec((tm, tn), lambda i, j: (i, j))
    shape = jax.ShapeDtypeStruct((t, D_FF), BF16)
    return _call(body, name, (t // tm, n_tiles),
                 [pl.BlockSpec((tm, d), lambda i, j: (i, 0)), pl.BlockSpec((tn, d), lambda i, j: (j, 0)),
                  pl.BlockSpec((tn, d), lambda i, j: (j + n_tiles, 0))],
                 [tile, tile, tile], [shape, shape, shape], [h, w_gu_t, w_gu_t], sem=("parallel", "parallel"),
                 nbytes=_nbytes((tm, d), BF16) + 2 * _nbytes((tn, d), BF16) + 5 * _nbytes((tm, tn), F32))


def _ffn_in_bwd(dd, w_d, gate, up, name):
    t, d = dd.shape
    tm, tn = _tile(t, 1024), _tile(D_FF, MM_TN)

    def body(dd_ref, w_ref, g_ref, u_ref, o_ref):
        da = lax.dot_general(dd_ref[...], w_ref[...], _DIMS["nt"], preferred_element_type=F32)
        gate, up = g_ref[...].astype(F32), u_ref[...].astype(F32)
        s = _sigmoid(gate)
        o_ref[0] = (da * up * (s * (1.0 + gate * (1.0 - s)))).astype(BF16)
        o_ref[1] = (da * (gate * s)).astype(BF16)

    tile = pl.BlockSpec((tm, tn), lambda i, j: (i, j))
    return _call(body, name, (t // tm, D_FF // tn),
                 [pl.BlockSpec((tm, d), lambda i, j: (i, 0)), pl.BlockSpec((tn, d), lambda i, j: (j, 0)), tile, tile],
                 [pl.BlockSpec((2, tm, tn), lambda i, j: (0, i, j))], [jax.ShapeDtypeStruct((2, t, D_FF), BF16)],
                 [dd, w_d, gate, up], sem=("parallel", "parallel"),
                 nbytes=_nbytes((tm, d), BF16) + _nbytes((tn, d), BF16) + 6 * _nbytes((tm, tn), F32))[0]


def _split3(x):
    hi = x.astype(BF16)
    r1 = x - hi.astype(F32)
    mid = r1.astype(BF16)
    lo = (r1 - mid.astype(F32)).astype(BF16)
    return hi, mid, lo


def _dot_exact(mat01, x):
    hi, mid, lo = _split3(x)
    out = jnp.dot(mat01, hi, preferred_element_type=F32)
    out = out + jnp.dot(mat01, mid, preferred_element_type=F32)
    return out + jnp.dot(mat01, lo, preferred_element_type=F32)


def _dot_exact_rhs(x, mat01):
    hi, mid, lo = _split3(x)
    out = jnp.dot(hi, mat01, preferred_element_type=F32)
    out = out + jnp.dot(mid, mat01, preferred_element_type=F32)
    return out + jnp.dot(lo, mat01, preferred_element_type=F32)


def _tri(lower):
    r = lax.broadcasted_iota(jnp.int32, (CHUNK, CHUNK), 0)
    c = lax.broadcasted_iota(jnp.int32, (CHUNK, CHUNK), 1)
    return jnp.where(r >= c if lower else r <= c, 1.0, 0.0).astype(BF16)


def _log_sigmoid(z):
    return jnp.minimum(z, 0.0) - jnp.log(1.0 + jnp.exp(-jnp.abs(z)))


def _forget_cumsum(proj, b_pad, name):
    t = proj.shape[0]
    nb = t // CHUNK

    def body(f_ref, b_ref, c_ref):
        tri = _tri(True)
        b = b_ref[...]

        def blk(i, carry):
            rows = pl.ds(pl.multiple_of(i * CHUNK, CHUNK), CHUNK)
            cs = _dot_exact(tri, _log_sigmoid(f_ref[rows, :] + b)) + carry
            c_ref[rows, :] = cs
            return cs[CHUNK - 1:CHUNK, :]

        lax.fori_loop(0, nb, blk, jnp.zeros((1, HEAD_DIM), F32))

    return _call(body, name, (1,),
                 [pl.BlockSpec((t, HEAD_DIM), lambda i: (0, CB_F)), pl.BlockSpec((1, HEAD_DIM), lambda i: (0, 0))],
                 [pl.BlockSpec((t, HEAD_DIM), lambda i: (0, 0))], [jax.ShapeDtypeStruct((t, HEAD_DIM), F32)],
                 [proj, b_pad], sem=("arbitrary",), nbytes=2 * _nbytes((t, HEAD_DIM), F32))[0]


def _causal(block):
    r = lax.broadcasted_iota(jnp.int32, (block, block), 0)
    c = lax.broadcasted_iota(jnp.int32, (block, block), 1)
    return c <= r


def _lanes(x, width):
    return jnp.concatenate([x] * (width // HEAD_DIM), axis=1)


def _chunks(t, blk):
    return [slice(r * blk, (r + 1) * blk) for r in range(t // blk)]


def _lane_of_head(x, h):
    lane = lax.broadcasted_iota(jnp.int32, x.shape, 1)
    return jnp.broadcast_to(jnp.sum(jnp.where(lane == h, x, 0.0), axis=1, keepdims=True), x.shape)


def _causal_t(block):
    r = lax.broadcasted_iota(jnp.int32, (block, block), 0)
    c = lax.broadcasted_iota(jnp.int32, (block, block), 1)
    return r <= c


def _attn_fwd_t(proj, c_pad, c_row, name, comm=()):
    t = proj.shape[0]
    blk = min(ATTN_BLOCK, t)
    nq = t // blk
    scale = HEAD_DIM ** -0.5

    def body(q_ref, k_ref, v_ref, cp_ref, cr_ref, y_ref, e_ref, kb_s, vt_s, cb_s):
        h, i = pl.program_id(0), pl.program_id(1)

        @pl.when(i == 0)
        def _():
            for rows in _chunks(t, blk):
                kb_s[rows, :] = k_ref[rows, :].astype(BF16)
                vt_s[:, rows] = v_ref[rows, :].T.astype(BF16)
                cb_s[rows, :] = _lane_of_head(cp_ref[rows, :], h)

        q = (q_ref[...] * scale).astype(BF16)
        ci = cr_ref[:, pl.ds(pl.multiple_of(i * blk, blk), blk)]

        def step(j, carry, diagonal):
            m, l, acc = carry
            rows = pl.ds(pl.multiple_of(j * blk, blk), blk)
            s = lax.dot_general(kb_s[rows, :], q, _DIMS["nt"], preferred_element_type=F32)
            s = s + ci - _lanes(cb_s[rows, :], blk)
            if diagonal:
                s = jnp.where(_causal_t(blk), s, NEG)
            m_new = jnp.maximum(m, jnp.max(s, axis=0, keepdims=True))
            alpha = jnp.exp(m - m_new)
            p = jnp.exp(s - m_new)
            l = alpha * l + jnp.sum(p, axis=0, keepdims=True)
            acc = alpha * acc + jnp.dot(vt_s[:, rows], p.astype(BF16), preferred_element_type=F32)
            return m_new, l, acc

        init = (jnp.full((1, blk), NEG, F32), jnp.zeros((1, blk), F32), jnp.zeros((HEAD_DIM, blk), F32))
        carry = lax.fori_loop(0, i, lambda j, c: step(j, c, False), init)
        m, l, acc = step(i, carry, True)
        y_ref[...] = (acc / l).T
        e_ref[...] = ci - (m + jnp.log(l))

    nbytes = 3 * _nbytes((t, HEAD_DIM), F32) + 4 * _nbytes((blk, HEAD_DIM), F32) + 4 * _nbytes((blk, blk), F32)
    return _call(body, name, (N_HEADS, nq),
                 [pl.BlockSpec((blk, HEAD_DIM), lambda h, i: (i, CB_QKV + 3 * h)),
                  pl.BlockSpec((t, HEAD_DIM), lambda h, i: (0, CB_QKV + 3 * h + 1)),
                  pl.BlockSpec((t, HEAD_DIM), lambda h, i: (0, CB_QKV + 3 * h + 2)),
                  pl.BlockSpec((t, HEAD_DIM), lambda h, i: (0, 0)),
                  pl.BlockSpec((None, 1, t), lambda h, i: (h, 0, 0))],
                 [pl.BlockSpec((blk, HEAD_DIM), lambda h, i: (i, h)),
                  pl.BlockSpec((None, 1, blk), lambda h, i: (h, 0, i))],
                 [jax.ShapeDtypeStruct((t, D_MODEL), F32), jax.ShapeDtypeStruct((N_HEADS, 1, t), F32)],
                 [proj, proj, proj, c_pad, c_row],
                 scratch=[pltpu.VMEM((t, HEAD_DIM), BF16), pltpu.VMEM((HEAD_DIM, t), BF16),
                          pltpu.VMEM((t, HEAD_DIM), F32)],
                 sem=("parallel", "arbitrary"), nbytes=nbytes, comm=comm)


def _attn_bwd_t(proj, dy, e_row, delta_b, c_pad, dproj, name, deps=()):
    t = proj.shape[0]
    blk = min(ATTN_BLOCK, t)
    nb = t // blk
    scale = HEAD_DIM ** -0.5

    def body(q_ref, k_ref, v_ref, dy_ref, e_ref, dl_ref, cp_ref, dproj_in, dqkv_ref, dck_ref, dcq_ref,
             qb_s, qt_s, dyb_s, dyt_s, dl_s, dqt_acc):
        del dproj_in
        h, j = pl.program_id(0), pl.program_id(1)

        @pl.when(j == 0)
        def _():
            for rows in _chunks(t, blk):
                qs = q_ref[rows, :] * scale
                qb_s[rows, :] = qs.astype(BF16)
                qt_s[:, rows] = qs.T.astype(BF16)
                dyr = dy_ref[rows, :]
                dyb_s[rows, :] = dyr.astype(BF16)
                dyt_s[:, rows] = dyr.T.astype(BF16)
                dl_s[:, rows] = dl_ref[rows, :].T[0:1, :]
            dqt_acc[...] = jnp.zeros_like(dqt_acc)
            dcq_ref[...] = jnp.zeros_like(dcq_ref)

        kf = k_ref[...]
        kb, kt = kf.astype(BF16), kf.T.astype(BF16)
        vb = v_ref[...].astype(BF16)
        cj = _lanes(_lane_of_head(cp_ref[...], h), blk)

        def step(i, carry, diagonal):
            dkt, dvt, dkey = carry
            cols = pl.ds(pl.multiple_of(i * blk, blk), blk)
            s = lax.dot_general(kb, qb_s[cols, :], _DIMS["nt"], preferred_element_type=F32) + e_ref[:, cols] - cj
            if diagonal:
                s = jnp.where(_causal_t(blk), s, NEG)
            p = jnp.exp(s)
            dp = lax.dot_general(vb, dyb_s[cols, :], _DIMS["nt"], preferred_element_type=F32)
            ds = p * (dp - dl_s[:, cols])
            pb, dsb = p.astype(BF16), ds.astype(BF16)
            dvt = dvt + lax.dot_general(dyt_s[:, cols], pb, _DIMS["nt"], preferred_element_type=F32)
            dkt = dkt + lax.dot_general(qt_s[:, cols], dsb, _DIMS["nt"], preferred_element_type=F32)
            dqt_acc[:, cols] += jnp.dot(kt, dsb, preferred_element_type=F32) * scale
            dcq_ref[:, cols] += jnp.sum(ds, axis=0, keepdims=True)
            for k in range(blk // HEAD_DIM):
                dkey = dkey + ds[:, k * HEAD_DIM:(k + 1) * HEAD_DIM]
            return dkt, dvt, dkey

        zeros = jnp.zeros((HEAD_DIM, blk), F32)
        carry = step(j, (zeros, zeros, jnp.zeros((blk, HEAD_DIM), F32)), True)
        dkt, dvt, dkey = lax.fori_loop(j + 1, nb, lambda i, c: step(i, c, False), carry)
        mine = pl.ds(pl.multiple_of(j * blk, blk), blk)
        dqkv_ref[:, :HEAD_DIM] = dqt_acc[:, mine].T.astype(BF16)
        dqkv_ref[:, HEAD_DIM:2 * HEAD_DIM] = dkt.T.astype(BF16)
        dqkv_ref[:, 2 * HEAD_DIM:] = dvt.T.astype(BF16)
        dck_ref[...] = -jnp.sum(dkey.T, axis=0, keepdims=True)

    full = lambda cb: pl.BlockSpec((t, HEAD_DIM), lambda h, j: (0, cb(h)))
    head = lambda h: h
    row = pl.BlockSpec((None, 1, t), lambda h, j: (h, 0, 0))
    nbytes = (8 * _nbytes((t, HEAD_DIM), F32) + 4 * _nbytes((blk, HEAD_DIM), F32) + 6 * _nbytes((blk, blk), F32))
    return _call(body, name, (N_HEADS, nb),
                 [full(lambda h: CB_QKV + 3 * h),
                  pl.BlockSpec((blk, HEAD_DIM), lambda h, j: (j, CB_QKV + 3 * h + 1)),
                  pl.BlockSpec((blk, HEAD_DIM), lambda h, j: (j, CB_QKV + 3 * h + 2)),
                  full(head), row, full(head),
                  pl.BlockSpec((blk, HEAD_DIM), lambda h, j: (j, 0)),
                  pl.BlockSpec(memory_space=pl.ANY)],
                 [pl.BlockSpec((blk, QKV_WIDTH), lambda h, j: (j, (GATED + F_WIDTH) // QKV_WIDTH + h)),
                  pl.BlockSpec((None, 1, blk), lambda h, j: (h, 0, j)),
                  row],
                 [jax.ShapeDtypeStruct(dproj.shape, dproj.dtype), jax.ShapeDtypeStruct((N_HEADS, 1, t), F32),
                  jax.ShapeDtypeStruct((N_HEADS, 1, t), F32)],
                 [proj, proj, proj, dy, e_row, delta_b, c_pad, dproj],
                 scratch=[pltpu.VMEM((t, HEAD_DIM), BF16), pltpu.VMEM((HEAD_DIM, t), BF16),
                          pltpu.VMEM((t, HEAD_DIM), BF16), pltpu.VMEM((HEAD_DIM, t), BF16),
                          pltpu.VMEM((1, t), F32), pltpu.VMEM((HEAD_DIM, t), F32)],
                 sem=("parallel", "arbitrary"), nbytes=nbytes, aliases={7: 0}, deps=deps)


def _forget_bwd_t(proj, b_pad, dc, dproj, name):
    t = proj.shape[0]
    tr = min(ROW_BLOCK, t)
    nb = t // tr
    rev = lambda i: nb - 1 - i

    def body(f_ref, b_ref, dc_ref, dproj_in, df_ref, db_ref, run_ref):
        del dproj_in

        @pl.when(pl.program_id(0) == 0)
        def _():
            run_ref[...] = jnp.zeros_like(run_ref)
            db_ref[...] = jnp.zeros_like(db_ref)

        tri = _tri(False)
        b = b_ref[...]
        lane = lax.broadcasted_iota(jnp.int32, (CHUNK, HEAD_DIM), 1)
        df_ref[...] = jnp.zeros_like(df_ref)
        for n in reversed(range(tr // CHUNK)):
            rows = slice(n * CHUNK, (n + 1) * CHUNK)
            dlogf = _dot_exact(tri, dc_ref[rows, :]) + run_ref[...]
            run_ref[...] = dlogf[0:1, :]
            z = f_ref[rows, :] + b
            e = jnp.exp(-jnp.abs(z))
            sig_neg = jnp.where(z >= 0.0, e, 1.0) / (1.0 + e)
            df = jnp.where(lane < N_HEADS, dlogf * sig_neg, 0.0)
            df_ref[rows, :HEAD_DIM] = df.astype(BF16)
            db_ref[...] += jnp.sum(df, axis=0, keepdims=True)

    return _call(body, name, (nb,),
                 [pl.BlockSpec((tr, HEAD_DIM), lambda i: (rev(i), CB_F)), pl.BlockSpec((1, HEAD_DIM), lambda i: (0, 0)),
                  pl.BlockSpec((tr, HEAD_DIM), lambda i: (rev(i), 0)), pl.BlockSpec(memory_space=pl.ANY)],
                 [pl.BlockSpec((tr, F_WIDTH), lambda i: (rev(i), GATED // F_WIDTH)),
                  pl.BlockSpec((1, HEAD_DIM), lambda i: (0, 0))],
                 [jax.ShapeDtypeStruct(dproj.shape, dproj.dtype), jax.ShapeDtypeStruct((1, HEAD_DIM), F32)],
                 [proj, b_pad, dc, dproj], scratch=[pltpu.VMEM((1, HEAD_DIM), F32)],
                 sem=("arbitrary",), nbytes=4 * _nbytes((tr, HEAD_DIM), F32), aliases={3: 0})


_GELU_K = math.sqrt(2.0 / math.pi)
_GELU_C = 0.044715


def _gelu(x):
    t = jnp.tanh(_GELU_K * (x + _GELU_C * (x * x * x)))
    return 0.5 * x * (1.0 + t), t


def _gelu_grad(x, t):
    return 0.5 * (1.0 + t) + 0.5 * x * (1.0 - t * t) * (_GELU_K * (1.0 + 3.0 * _GELU_C * (x * x)))


def _layernorm_stats(a):
    mu = jnp.mean(a, axis=-1, keepdims=True)
    xc = a - mu
    r = lax.rsqrt(jnp.mean(xc * xc, axis=-1, keepdims=True) + EPS)
    return xc * r, r


def _group(g):
    return slice(g * CHUNK, (g + 1) * CHUNK)


def _fixed(shape):
    return pl.BlockSpec(shape, lambda i: (0,) * len(shape))


def _mix_fwd(proj, y_b, w_s, b_cols, g_v, name):
    t = proj.shape[0]
    step_rows = min(MIX_ROWS, t)

    def body(u_ref, vs_ref, ga_ref, gb_ref, yb_ref, w_ref, b_ref, gv_ref, o_ref):
        mask = _causal(CHUNK)
        w = [jnp.where(mask, w_ref[g], 0.0).astype(BF16) for g in range(N_GROUPS)]
        for rows in _chunks(step_rows, CHUNK):
            a_u, _ = _gelu(u_ref[rows, :])
            a_v, _ = _gelu(vs_ref[rows, :])
            xhat, _ = _layernorm_stats(a_v)
            vn = (xhat * gv_ref[...]).astype(BF16)
            sa, sb = _sigmoid(ga_ref[rows, :]), _sigmoid(gb_ref[rows, :])
            yb = yb_ref[rows, :]
            for g in range(N_GROUPS):
                cols = _group(g)
                mixed = jnp.dot(w[g], vn[:, cols], preferred_element_type=F32) + b_ref[:, g:g + 1]
                o_ref[rows, cols] = (sa[:, cols] * (a_u[:, cols] * mixed) + sb[:, cols] * yb[:, cols]).astype(BF16)

    gated = lambda k: pl.BlockSpec((step_rows, D_MODEL), lambda i: (i, k))
    return _call(body, name, (t // step_rows,),
                 [gated(0), gated(1), gated(2), gated(3), _row_spec(step_rows, D_MODEL),
                  _fixed((N_GROUPS, CHUNK, CHUNK)), _fixed((CHUNK, N_GROUPS)), _fixed((1, D_MODEL))],
                 [_row_spec(step_rows, D_MODEL)], [jax.ShapeDtypeStruct((t, D_MODEL), BF16)],
                 [proj, proj, proj, proj, y_b, w_s, b_cols, g_v], sem=("parallel",),
                 nbytes=16 * _nbytes((step_rows, D_MODEL), F32))[0]


def _mix_bwd(proj, y_b, dmerged, w_s, b_cols, g_v, sel, name, deps=()):
    t = proj.shape[0]
    step_rows = min(MIX_BWD_ROWS, t)

    def body(u_ref, vs_ref, ga_ref, gb_ref, yb_ref, dm_ref, w_ref, b_ref, gv_ref, sel_ref,
             dg_ref, dyb_ref, delta_ref, dw_ref, dbt_ref, dgv_ref):
        @pl.when(pl.program_id(0) == 0)
        def _():
            dw_ref[...] = jnp.zeros_like(dw_ref)
            dbt_ref[...] = jnp.zeros_like(dbt_ref)
            dgv_ref[...] = jnp.zeros_like(dgv_ref)

        gv = gv_ref[...]
        mask = _causal(CHUNK)
        w = [jnp.where(mask, w_ref[g], 0.0).astype(BF16) for g in range(N_GROUPS)]
        for rows in _chunks(step_rows, CHUNK):
            u, vs = u_ref[rows, :], vs_ref[rows, :]
            a_u, t_u = _gelu(u)
            a_v, t_v = _gelu(vs)
            xhat, r = _layernorm_stats(a_v)
            vn = (xhat * gv).astype(BF16)
            sa, sb = _sigmoid(ga_ref[rows, :]), _sigmoid(gb_ref[rows, :])
            yb, dm = yb_ref[rows, :], dm_ref[rows, :].astype(F32)
            dyb = dm * sb
            dyb_ref[rows, :] = dyb
            dyb_yb = dyb * yb
            dg_ref[rows, 3 * D_MODEL:] = (dm * yb * (sb * (1.0 - sb))).astype(BF16)
            dya = dm * sa
            dmixed_parts, dvn_parts = [], []
            for g in range(N_GROUPS):
                cols = _group(g)
                delta_ref[rows, cols] = jnp.broadcast_to(jnp.sum(dyb_yb[:, cols], axis=1, keepdims=True),
                                                         (CHUNK, CHUNK))
                mixed = jnp.dot(w[g], vn[:, cols], preferred_element_type=F32) + b_ref[:, g:g + 1]
                y_a = a_u[:, cols] * mixed
                dg_ref[rows, 2 * D_MODEL + g * CHUNK:2 * D_MODEL + (g + 1) * CHUNK] = (
                    dm[:, cols] * y_a * (sa[:, cols] * (1.0 - sa[:, cols]))).astype(BF16)
                dg_ref[rows, cols] = (dya[:, cols] * mixed * _gelu_grad(u[:, cols], t_u[:, cols])).astype(BF16)
                dmixed = dya[:, cols] * a_u[:, cols]
                dmb = dmixed.astype(BF16)
                dw = lax.dot_general(dmb, vn[:, cols], _DIMS["nt"], preferred_element_type=F32)
                dw_ref[g] += jnp.where(mask, dw, 0.0)
                dvn_parts.append(lax.dot_general(w[g], dmb, _DIMS["tn"], preferred_element_type=F32))
                dmixed_parts.append(dmixed)
            dmixed_all = jnp.concatenate(dmixed_parts, axis=1)
            dvn = jnp.concatenate(dvn_parts, axis=1)
            dbt_ref[...] += _dot_exact_rhs(dmixed_all, sel_ref[...])
            dgv_ref[...] += jnp.sum(dvn * xhat, axis=0, keepdims=True)
            dxh = dvn * gv
            da_v = r * (dxh - jnp.mean(dxh, axis=-1, keepdims=True)
                        - xhat * jnp.mean(dxh * xhat, axis=-1, keepdims=True))
            dg_ref[rows, D_MODEL:2 * D_MODEL] = (da_v * _gelu_grad(vs, t_v)).astype(BF16)

    row = lambda width: _row_spec(step_rows, width)
    gated = lambda k: pl.BlockSpec((step_rows, D_MODEL), lambda i: (i, k))
    return _call(body, name, (t // step_rows,),
                 [gated(0), gated(1), gated(2), gated(3), row(D_MODEL), row(D_MODEL),
                  _fixed((N_GROUPS, CHUNK, CHUNK)), _fixed((CHUNK, N_GROUPS)), _fixed((1, D_MODEL)),
                  _fixed((D_MODEL, HEAD_DIM))],
                 [row(GATED), row(D_MODEL), row(D_MODEL),
                  _fixed((N_GROUPS, CHUNK, CHUNK)), _fixed((CHUNK, HEAD_DIM)), _fixed((1, D_MODEL))],
                 [jax.ShapeDtypeStruct((t, IN_PAD), BF16), jax.ShapeDtypeStruct((t, D_MODEL), F32),
                  jax.ShapeDtypeStruct((t, D_MODEL), F32), jax.ShapeDtypeStruct((N_GROUPS, CHUNK, CHUNK), F32),
                  jax.ShapeDtypeStruct((CHUNK, HEAD_DIM), F32), jax.ShapeDtypeStruct((1, D_MODEL), F32)],
                 [proj, proj, proj, proj, y_b, dmerged, w_s, b_cols, g_v, sel], sem=("arbitrary",),
                 nbytes=24 * _nbytes((step_rows, D_MODEL), F32), deps=deps)


def _adamw_update(w_ref, g_ref, m_ref, v_ref, d_ref, nm_ref, nv_ref):
    gv = g_ref[...]
    nm = ADAM_B1 * m_ref[...] + (1.0 - ADAM_B1) * gv
    nv = ADAM_B2 * v_ref[...] + (1.0 - ADAM_B2) * (gv * gv)
    m_hat = nm / (1.0 - ADAM_B1 ** ADAM_STEP)
    v_hat = nv / (1.0 - ADAM_B2 ** ADAM_STEP)
    d_ref[...] = -ADAM_LR * (m_hat / (jnp.sqrt(v_hat) + ADAM_EPS) + ADAM_WD * w_ref[...])
    nm_ref[...] = nm
    nv_ref[...] = nv


def _adamw(w, g, m, v, name):
    r, c = w.shape
    tr = _tile(r, 256, unit=8)

    def body(w_ref, g_ref, m_ref, v_ref, d_ref, nm_ref, nv_ref):
        _adamw_update(w_ref, g_ref, m_ref, v_ref, d_ref, nm_ref, nv_ref)

    spec = pl.BlockSpec((tr, c), lambda i: (i, 0))
    shape = jax.ShapeDtypeStruct((r, c), F32)
    return _call(body, name, (r // tr,), [spec] * 4, [spec] * 3, [shape] * 3, [w, g, m, v], sem=("parallel",),
                 nbytes=7 * _nbytes((tr, max(c, 128)), F32))


def _adamw_cols(w, g, m, v, name):
    layers, r, c = w.shape
    tc = 256

    def body(w_ref, g_ref, m_ref, v_ref, d_ref, nm_ref, nv_ref):
        _adamw_update(w_ref, g_ref, m_ref, v_ref, d_ref, nm_ref, nv_ref)

    spec = pl.BlockSpec((None, r, tc), lambda l, j: (l, 0, j))
    shape = jax.ShapeDtypeStruct(w.shape, F32)
    return _call(body, name, (layers, c // tc), [spec] * 4, [spec] * 3, [shape] * 3, [w, g, m, v],
                 sem=("parallel", "parallel"), nbytes=7 * _nbytes((r, tc), F32))


def _pad_rows(a, rows):
    return jnp.pad(a, ((0, rows - a.shape[0]), (0, 0)))


def _unpack_in(gathered):
    i = jnp.arange(16)[None, :, None]
    b = jnp.arange(N_DEV)[:, None, None]
    head = jnp.where(i < b, jnp.roll(gathered[:, IN_STRIDE:IN_STRIDE + 16], 1, axis=0), gathered[:, :16])
    nat = jnp.concatenate([head, gathered[:, 16:IN_STRIDE]], axis=1).reshape(N_DEV * IN_STRIDE, D_MODEL)
    qkv = nat[2 * D_MODEL:5 * D_MODEL].reshape(3, N_HEADS, HEAD_DIM, D_MODEL)
    qkv = jnp.transpose(qkv, (1, 0, 2, 3)).reshape(3 * D_MODEL, D_MODEL)
    f = _pad_rows(gathered[N_DEV - 1, IN_STRIDE:IN_STRIDE + N_HEADS], F_WIDTH)
    return jnp.concatenate([nat[:2 * D_MODEL], nat[5 * D_MODEL:7 * D_MODEL], f, qkv], axis=0)


def _pack_grad_in(dw_in_t):
    qkv = dw_in_t[GATED + F_WIDTH:].reshape(N_HEADS, 3, HEAD_DIM, D_MODEL)
    qkv = jnp.transpose(qkv, (1, 0, 2, 3)).reshape(3 * D_MODEL, D_MODEL)
    return jnp.concatenate([dw_in_t[:2 * D_MODEL], qkv, dw_in_t[2 * D_MODEL:GATED],
                            dw_in_t[GATED:GATED + IN_ROWS - N_DEV * IN_STRIDE]], axis=0)


def kernel(x, mix_pre_g, w_in, b_forget, sgu_norm_g, w_spatial, b_spatial, w_out, mix_post_g, ffn_pre_g, w_gate, w_up, w_down, ffn_post_g, loss_target, m_mix_pre_g, m_w_in, m_b_forget, m_sgu_norm_g, m_w_spatial, m_b_spatial, m_w_out, m_mix_post_g, m_ffn_pre_g, m_w_gate, m_w_up, m_w_down, m_ffn_post_g, v_mix_pre_g, v_w_in, v_b_forget, v_sgu_norm_g, v_w_spatial, v_b_spatial, v_w_out, v_mix_post_g, v_ffn_pre_g, v_w_gate, v_w_up, v_w_down, v_ffn_post_g):
    depth = w_in.shape[0]
    assert depth == 2
    t = x.shape[1]
    x0 = x.reshape(t, D_MODEL)
    target = loss_target.reshape(t, D_MODEL)
    sel = (jnp.arange(D_MODEL)[:, None] // HEAD_DIM == jnp.arange(HEAD_DIM)[None, :]).astype(BF16)
    vec = lambda a, l: a[l][None, :]
    zero_row = jnp.zeros((1, D_MODEL), F32)

    me = 4 * lax.axis_index("x") + 2 * lax.axis_index("y") + lax.axis_index("c")
    p_in = [lax.dynamic_update_slice(jnp.zeros((P_IN, D_MODEL), BF16), w_in[l].T.astype(BF16), (me, 0))
            for l in range(depth)]
    p_rest = [[w_out[l].astype(BF16), jnp.stack([w_gate[l].T, w_up[l].T]).astype(BF16), w_down[l].astype(BF16)]
              for l in range(depth)]
    w_in_t = [None] * depth
    w_rest = [None] * depth
    w_in_t[0] = _unpack_in(_exchange(("gather", 0), p_in[0], "gather_w_in_0")[0])

    saved = []
    xl = x0
    h = _rms_fwd(xl, vec(mix_pre_g, 0), "rms_in_0")
    dy = loss = None
    for l in range(depth):
        b_pad = jnp.pad(b_forget[l], (0, HEAD_DIM - N_HEADS))[None, :]
        b_cols = b_spatial[l].T
        proj = _matmul(h, w_in_t[l], "nt", F32, f"proj_{l}")
        c_pad = _forget_cumsum(proj, b_pad, f"forget_cumsum_{l}")
        c_row = c_pad[:, :N_HEADS].T[:, None, :]
        riders = [(("gather", 0), p_rest[l][0]), (("gather", 1), p_rest[l][1]), (("gather", 0), p_rest[l][2])]
        riders += [(("gather", 0), p_in[l + 1])] if l + 1 < depth else []
        y_b, e_row, *arrived = _attn_fwd_t(proj, c_pad, c_row, f"attn_fwd_{l}", comm=riders)
        w_rest[l] = (arrived[0].reshape(D_MODEL, D_MODEL), arrived[1].reshape(2 * D_FF, D_MODEL),
                     arrived[2].reshape(D_FF, D_MODEL))
        if l + 1 < depth:
            w_in_t[l + 1] = _unpack_in(arrived[3])
        w_o, w_gu_t, w_d = w_rest[l]
        merged = _mix_fwd(proj, y_b, w_spatial[l], b_cols, vec(sgu_norm_g, l), f"mix_fwd_{l}")
        o, x1, h2 = _matmul_post_norm(merged, w_o, xl, vec(mix_post_g, l), f"out_proj_{l}", g_next=vec(ffn_pre_g, l))
        gate, up, act = _ffn_in(h2, w_gu_t, f"ffn_gu_{l}")
        if l + 1 < depth:
            dn, x_next, h_next = _matmul_post_norm(act, w_d, x1, vec(ffn_post_g, l), f"ffn_down_{l}",
                                                   g_next=vec(mix_pre_g, l + 1))
        else:
            dn, dy, loss, dd, dg_ffn_post = _matmul_post_norm(act, w_d, x1, vec(ffn_post_g, l), f"ffn_down_{l}",
                                                              target=target)
        saved.append(dict(x0=xl, h=h, proj=proj, b_pad=b_pad, b_cols=b_cols, c_pad=c_pad, y_b=y_b, e_row=e_row,
                          merged=merged, o=o, x1=x1, h2=h2, gate=gate, up=up, act=act, dn=dn))
        if l + 1 < depth:
            xl, h = x_next, h_next

    g_small = [None] * depth
    flights = {}
    late_token = None
    dx = dy
    for l in reversed(range(depth)):
        s = saved[l]
        w_o, w_gu_t, w_d = w_rest[l]
        dgu = _ffn_in_bwd(dd, w_d, s["gate"], s["up"], f"d_act_{l}")
        dw_d = _matmul(s["act"], dd, "tn", BF16, f"dw_down_{l}")
        dw_gu_t = _matmul(dgu, s["h2"], "tn", BF16, f"dw_gu_{l}", halves=True)
        dx1, dg_ffn_pre, do, dg_mix_post = _matmul_norm_bwd(
            dgu, w_gu_t, dx, s["x1"], vec(ffn_pre_g, l), f"d_h2_{l}", post=(s["o"], vec(mix_post_g, l)), halves=True)
        dmerged = _matmul(do, w_o, "nt", F32, f"d_merged_{l}")
        dw_o = _matmul(s["merged"], do, "tn", BF16, f"dw_out_{l}")
        tokens = []
        for key, dw, base, rows in [("out", dw_o, 0, SH_OUT), ("gate", dw_gu_t, 0, SH_FF), ("up", dw_gu_t, D_FF, SH_FF),
                                    ("down", dw_d, 0, SH_FF)]:
            flights[(key, l)], token = _exchange_start(("windows", base, rows, rows), dw, f"start_grad_{key}_{l}")
            tokens.append(token)
        dproj, dyb, delta_b, dw_s, dbt, dg_v = _mix_bwd(
            s["proj"], s["y_b"], dmerged, w_spatial[l], s["b_cols"], vec(sgu_norm_g, l), sel, f"mix_bwd_{l}",
            deps=tokens)
        db_s = dbt[:, :N_GROUPS].T.reshape(1, D_MODEL)
        tokens = []
        if l == 0:
            g_small[0] = jnp.concatenate([zero_row, dg_v, dg_mix_post, dg_ffn_pre, dg_ffn_post, db_s, zero_row,
                                          zero_row, dw_s.reshape(CHUNK, D_MODEL)], axis=0)
            flights["small"], token = _exchange_start(("spread",),jnp.concatenate([g_small[1], g_small[0]], axis=0),
                                                      "start_small_grads")
            tokens = [token]
        dproj, dc_key, dc_query = _attn_bwd_t(s["proj"], dyb, s["e_row"], delta_b, s["c_pad"], dproj,
                                              f"attn_bwd_{l}", deps=tokens)
        dc = jnp.pad((dc_key + dc_query)[:, 0, :].T, ((0, 0), (0, HEAD_DIM - N_HEADS)))
        dproj, db_f = _forget_bwd_t(s["proj"], s["b_pad"], dc, dproj, f"forget_bwd_{l}")
        db_f_row = jnp.pad(db_f, ((0, 0), (0, D_MODEL - HEAD_DIM)))
        dw_in_t = _matmul(dproj, s["h"], "tn", BF16, f"dw_in_{l}")
        flights[("in", l)], token = _exchange_start(("windows", 0, IN_STRIDE, P_IN), _pack_grad_in(dw_in_t),
                                                    f"start_grad_in_{l}")
        if l > 0:
            dx, dg_mix_pre, dd, dg_ffn_post_below = _matmul_norm_bwd(
                dproj, w_in_t[l], dx1, s["x0"], vec(mix_pre_g, l), f"d_h_{l}",
                post=(saved[l - 1]["dn"], vec(ffn_post_g, l - 1)), deps=[token])
            loss_row = jnp.pad(loss, ((0, 0), (0, D_MODEL - 1)))
            g_small[l] = jnp.concatenate([dg_mix_pre, dg_v, dg_mix_post, dg_ffn_pre, dg_ffn_post, db_s, db_f_row,
                                          loss_row, dw_s.reshape(CHUNK, D_MODEL)], axis=0)
            dg_ffn_post = dg_ffn_post_below
        else:
            dx, dg_mix_pre = _matmul_norm_bwd(dproj, w_in_t[l], dx1, s["x0"], vec(mix_pre_g, l), f"d_h_{l}",
                                              deps=[token])
            late_rows = jnp.concatenate([dg_mix_pre, db_f_row] + [zero_row] * 6, axis=0)
            flights["late"], late_token = _exchange_start(("spread",),late_rows, "start_late_small_grads")
    grad_x = dx.reshape(x.shape)

    landed, after = {}, [grad_x, late_token]
    rest = ["out", "gate", "up", "down"]
    for key in [(n, 1) for n in rest] + [("in", 1)] + [(n, 0) for n in rest] + ["small"]:
        label = key if isinstance(key, str) else f"grad_{key[0]}_{key[1]}"
        landed[key] = _exchange_wait(flights[key], after, f"wait_{label}")
        after = [landed[key]]
    small_sum = _sum_slots(landed["small"], S_ROWS, "sum_small_grads")
    sm = small_sum.reshape(depth, S_ROWS, D_MODEL)[::-1]
    loss = sm[depth - 1, 7, 0]
    g = {
        "sgu_norm_g": sm[:, 1], "mix_post_g": sm[:, 2],
        "ffn_pre_g": sm[:, 3], "ffn_post_g": sm[:, 4], "b_spatial": sm[:, 5].reshape(depth, N_GROUPS, CHUNK),
        "w_spatial": sm[:, 8:].reshape(depth, N_GROUPS, CHUNK, CHUNK),
    }
    for n in rest:
        g["w_" + n] = jnp.stack([_sum_slots(landed[(n, l)], landed[(n, l)].shape[1], f"sum_grad_{n}_{l}")
                                 for l in range(depth)])

    names = ["mix_pre_g", "w_in", "b_forget", "sgu_norm_g", "w_spatial", "b_spatial", "w_out", "mix_post_g",
             "ffn_pre_g", "w_gate", "w_up", "w_down", "ffn_post_g"]
    ws = dict(mix_pre_g=mix_pre_g, w_in=w_in, b_forget=b_forget, sgu_norm_g=sgu_norm_g, w_spatial=w_spatial,
              b_spatial=b_spatial, w_out=w_out, mix_post_g=mix_post_g, ffn_pre_g=ffn_pre_g, w_gate=w_gate, w_up=w_up,
              w_down=w_down, ffn_post_g=ffn_post_g)
    ms = dict(mix_pre_g=m_mix_pre_g, w_in=m_w_in, b_forget=m_b_forget, sgu_norm_g=m_sgu_norm_g, w_spatial=m_w_spatial,
              b_spatial=m_b_spatial, w_out=m_w_out, mix_post_g=m_mix_post_g, ffn_pre_g=m_ffn_pre_g, w_gate=m_w_gate,
              w_up=m_w_up, w_down=m_w_down, ffn_post_g=m_ffn_post_g)
    vs = dict(mix_pre_g=v_mix_pre_g, w_in=v_w_in, b_forget=v_b_forget, sgu_norm_g=v_sgu_norm_g, w_spatial=v_w_spatial,
              b_spatial=v_b_spatial, w_out=v_w_out, mix_post_g=v_mix_post_g, ffn_pre_g=v_ffn_pre_g, w_gate=v_w_gate,
              w_up=v_w_up, w_down=v_w_down, ffn_post_g=v_ffn_post_g)
    updates = {}
    last = ["w_in", "mix_pre_g", "b_forget"]
    transposed = ["w_in", "w_gate", "w_up"]
    for n in [n for n in names if n not in last] + last:
        if n == last[0]:
            sum_in_1 = _sum_slots(landed[("in", 1)], 304, "sum_grad_in_1")
            done = [u[0] for u in updates.values()] + [sum_in_1]
            sum_in_0 = _sum_slots(_exchange_wait(flights[("in", 0)], done, "wait_grad_in_0"), 304, "sum_grad_in_0")
            late = _sum_slots(_exchange_wait(flights["late"], [sum_in_0], "wait_late"), 8, "sum_late_small_grads")
            g["w_in"] = lax.dynamic_slice(jnp.stack([sum_in_0, sum_in_1]), (0, me, 0), (depth, SH_IN, D_MODEL))
            g["mix_pre_g"] = jnp.stack([late[0], sm[1, 0]])
            g["b_forget"] = jnp.stack([late[1, :N_HEADS], sm[1, 6, :N_HEADS]])
        if n in transposed:
            view = lambda a: jnp.swapaxes(a, 1, 2)
            updates[n] = [view(u) for u in _adamw_cols(view(ws[n]), g[n], view(ms[n]), view(vs[n]), f"adamw_{n}")]
            g[n] = view(g[n])
        else:
            flat = (-1, ws[n].shape[-1])
            updates[n] = [u.reshape(ws[n].shape) for u in
                          _adamw(ws[n].reshape(flat), g[n].reshape(flat), ms[n].reshape(flat), vs[n].reshape(flat),
                                 f"adamw_{n}")]
    deltas = [updates[n][0] for n in names]
    new_m = [updates[n][1] for n in names]
    new_v = [updates[n][2] for n in names]
    grads = [g[n].reshape(ws[n].shape) for n in names]
    return (loss, grad_x, *grads, *deltas, *new_m, *new_v)
```

```python
import math

import jax
import jax.numpy as jnp
from jax import lax
from jax.experimental import pallas as pl
from jax.experimental.pallas import tpu as pltpu

F32 = jnp.float32
BF16 = jnp.bfloat16

N_DEV = 8
D_MODEL = 1024
N_HEADS = 8
HEAD_DIM = 128
CHUNK = 128
N_GROUPS = 8
D_FF = 2816
IN_WIDTH = 7 * D_MODEL + N_HEADS
IN_PAD = 7680
EPS = 1e-6
GATED = 4 * D_MODEL
F_WIDTH = 512
QKV_WIDTH = 3 * HEAD_DIM
CB_F = GATED // HEAD_DIM
CB_QKV = (GATED + F_WIDTH) // HEAD_DIM
assert GATED + F_WIDTH + N_HEADS * QKV_WIDTH == IN_PAD and (GATED + F_WIDTH) % QKV_WIDTH == 0

ADAM_LR, ADAM_B1, ADAM_B2, ADAM_EPS, ADAM_WD, ADAM_STEP = 0.001, 0.9, 0.999, 1e-08, 0.01, 10

SH_IN = IN_WIDTH // N_DEV
SH_OUT = D_MODEL // N_DEV
SH_FF = D_FF // N_DEV
P_IN = 912
IN_STRIDE = 7 * HEAD_DIM
IN_ROWS = IN_STRIDE * (N_DEV - 1) + P_IN
S_ROWS = 136

ATTN_BLOCK = 1024
ROW_BLOCK = 512
MIX_ROWS = 1024
MIX_BWD_ROWS = 256
MM_TM, MM_TN, MM_TK = 1536, 1536, 2048
VMEM_CAP = 56 << 20
NEG = -1e30


def _tile(n, cap, unit=128):
    if n <= cap:
        return n
    best = None
    for t in range(unit, cap + 1, unit):
        if n % t == 0:
            best = t
    assert best is not None, (n, cap)
    return best


def _nbytes(shape, dtype):
    return math.prod(shape) * jnp.dtype(dtype).itemsize


_HBM = pl.BlockSpec(memory_space=pltpu.HBM)
MESH = pl.DeviceIdType.MESH

_N_REMOTE = N_DEV - 1


def _exchange_shapes(kind, x):
    if kind[0] == "gather":
        shape = x.shape[:kind[1]] + (N_DEV,) + x.shape[kind[1]:]
    elif kind[0] == "spread":
        shape = (N_DEV,) + x.shape
    else:
        shape = (N_DEV, kind[3]) + x.shape[1:]
    return [jax.ShapeDtypeStruct(shape, x.dtype)]


def _exchange_sems(kind):
    del kind
    return [pltpu.SemaphoreType.DMA((_N_REMOTE,)), pltpu.SemaphoreType.DMA((_N_REMOTE,)), pltpu.SemaphoreType.DMA]


def _plan(kind, x_ref, outs, sems):
    send_sems, recv_sems, local_sem = sems
    x, y, c = lax.axis_index("x"), lax.axis_index("y"), lax.axis_index("c")

    def remote(src, dst, k, to):
        return pltpu.make_async_remote_copy(src_ref=src, dst_ref=dst, send_sem=send_sems.at[k], recv_sem=recv_sems.at[k],
                                            device_id=to, device_id_type=MESH)

    sibling = (x, y, 1 - c)
    if kind[0] == "gather":
        out, = outs
        slot = lambda px, py, pc: out.at[(slice(None),) * kind[1] + (4 * px + 2 * py + pc,)]
        chips = [(1 - x, y), (x, 1 - y), (1 - x, 1 - y)]
        local = [pltpu.make_async_copy(x_ref, slot(x, y, c), local_sem)]
        first = [remote(x_ref, slot(x, y, c), 0, sibling)]
        first += [remote(x_ref, slot(x, y, c), 1 + k, (*chip, c)) for k, chip in enumerate(chips)]
        relays = [(remote(x_ref, slot(*chip, c), 1 + k, (*chip, c)), remote(slot(*chip, c), slot(*chip, c), 4 + k, sibling))
                  for k, chip in enumerate(chips)]
        arrivals = [remote(x_ref, slot(x, y, 1 - c), 0, sibling)]
        arrivals += [remote(x_ref, slot(*chip, 1 - c), 4 + k, sibling) for k, chip in enumerate(chips)]
        return local, first, relays, arrivals
    out, = outs
    mine = 4 * x + 2 * y + c
    if kind[0] == "spread":
        src = lambda d: x_ref
    else:
        _, base, stride, length = kind
        src = lambda d: x_ref.at[pl.ds(pl.multiple_of(base + stride * d, 16), length)]
    first, arrivals = [], []
    for j in range(1, N_DEV):
        peer = (1 - x if j & 4 else x, 1 - y if j & 2 else y, 1 - c if j & 1 else c)
        theirs = 4 * peer[0] + 2 * peer[1] + peer[2]
        first.append(remote(src(theirs), out.at[mine], j - 1, peer))
        arrivals.append(remote(src(theirs), out.at[theirs], j - 1, peer))
    local = [] if local_sem is None else [pltpu.make_async_copy(src(mine), out.at[mine], local_sem)]
    return local, first, [], arrivals


def _start(plan):
    local, first, _, _ = plan
    for cp in local + first:
        cp.start()


def _finish(plan):
    local, first, relays, arrivals = plan
    for arrival, onward in relays:
        arrival.wait_recv()
        onward.start()
    for cp in arrivals:
        cp.wait_recv()
    for cp in first + [onward for _, onward in relays]:
        cp.wait_send()
    for cp in local:
        cp.wait()


def _direct_copies(kind, x_ref, land_ref, send_sems, recv_sems):
    _, first, _, arrivals = _plan(kind, x_ref, [land_ref], (send_sems, recv_sems, None))
    return first, arrivals


_SEM = pl.BlockSpec(memory_space=pltpu.SEMAPHORE)
_DATAFLOW = pltpu.SideEffectType.DATAFLOW_SIDE_EFFECTING


def _exchange_start(kind, x, name):
    me = 4 * lax.axis_index("x") + 2 * lax.axis_index("y") + lax.axis_index("c")
    own = x if kind[0] == "spread" else lax.dynamic_slice_in_dim(x, kind[1] + kind[2] * me, kind[3], 0)
    shape = _exchange_shapes(kind, x)[0].shape
    land = lax.dynamic_update_slice_in_dim(lax.empty(shape, x.dtype), own[None], me, 0)

    def body(x_ref, land_ref, send_sems, recv_sems, x_thru, land_thru, token):
        del x_thru, land_thru
        for cp in _direct_copies(kind, x_ref, land_ref, send_sems, recv_sems)[0]:
            cp.start()
        token[...] = jnp.zeros_like(token)

    *handle, token = pl.pallas_call(
        body, name=name,
        out_shape=(pltpu.SemaphoreType.DMA((_N_REMOTE,)), pltpu.SemaphoreType.DMA((_N_REMOTE,)),
                   pltpu.HBM(x.shape, x.dtype), pltpu.HBM(land.shape, land.dtype), jax.ShapeDtypeStruct((8, 128), F32)),
        in_specs=(_HBM, _HBM), out_specs=(_SEM, _SEM, _HBM, _HBM, pl.BlockSpec(memory_space=pltpu.VMEM)),
        input_output_aliases={0: 2, 1: 3}, compiler_params=pltpu.CompilerParams(has_side_effects=_DATAFLOW),
    )(pltpu.with_memory_space_constraint(x, pltpu.HBM), pltpu.with_memory_space_constraint(land, pltpu.HBM))
    return (kind, *handle), token


def _exchange_wait(handle, after, name):
    kind, send_sems, recv_sems, x_thru, land_thru = handle

    def body(x_ref, land_ref, send_sems, recv_sems, *rest):
        first, arrivals = _direct_copies(kind, x_ref, land_ref, send_sems, recv_sems)
        for cp in first:
            cp.wait_send()
        for cp in arrivals:
            cp.wait_recv()

    return pl.pallas_call(
        body, name=name,
        out_shape=(pltpu.HBM(x_thru.shape, x_thru.dtype), pltpu.HBM(land_thru.shape, land_thru.dtype)),
        in_specs=(_HBM, _HBM, _SEM, _SEM) + (pl.BlockSpec(memory_space=pl.ANY),) * len(after), out_specs=(_HBM, _HBM),
        input_output_aliases={0: 0, 1: 1}, compiler_params=pltpu.CompilerParams(has_side_effects=_DATAFLOW),
    )(x_thru, land_thru, send_sems, recv_sems, *after)[1]


def _exchange(kind, x, name):
    n_out = len(_exchange_shapes(kind, x))

    def body(x_ref, *refs):
        plan = _plan(kind, x_ref, refs[:n_out], refs[n_out:])
        _start(plan)
        _finish(plan)

    return pl.pallas_call(body, name=name, out_shape=_exchange_shapes(kind, x), in_specs=[_HBM],
                          out_specs=[_HBM] * n_out, scratch_shapes=_exchange_sems(kind))(x)


def _call(body, name, grid, in_specs, out_specs, out_shape, args, scratch=(), sem=None, nbytes=0, aliases=None,
          comm=(), deps=()):
    in_specs, out_specs, out_shape, args, scratch = (list(in_specs), list(out_specs), list(out_shape), list(args),
                                                     list(scratch))
    if deps:
        n_real, n_deps, unordered = len(args), len(deps), body

        def body(*refs):
            unordered(*refs[:n_real], *refs[n_real + n_deps:])

        in_specs += [pl.BlockSpec(memory_space=pl.ANY)] * n_deps
        args += list(deps)
    if comm:
        n_in, n_out, n_scr, n_ops = len(args), len(out_shape), len(scratch), len(comm)
        kinds = [kind for kind, _ in comm]
        shapes = [_exchange_shapes(kind, x) for kind, x in comm]
        inner = body

        def body(*refs):
            ins, refs = refs[:n_in], refs[n_in:]
            cins, refs = refs[:n_ops], refs[n_ops:]
            outs, refs = refs[:n_out], refs[n_out:]
            couts = []
            for sh in shapes:
                couts.append(refs[:len(sh)])
                refs = refs[len(sh):]
            own_scratch, sems = refs[:n_scr], refs[n_scr:]
            first = last = None
            for axis, size in enumerate(grid):
                at_start, at_end = pl.program_id(axis) == 0, pl.program_id(axis) == size - 1
                first = at_start if first is None else first & at_start
                last = at_end if last is None else last & at_end
            plans = [_plan(kinds[o], cins[o], couts[o], sems[3 * o:3 * o + 3]) for o in range(n_ops)]

            @pl.when(first)
            def _():
                for p in plans:
                    _start(p)

            inner(*ins, *outs, *own_scratch)

            @pl.when(last)
            def _():
                for p in plans:
                    _finish(p)

        in_specs += [_HBM] * n_ops
        args += [x for _, x in comm]
        for kind, sh in zip(kinds, shapes):
            out_shape += sh
            out_specs += [_HBM] * len(sh)
            scratch += _exchange_sems(kind)
        sem = ("arbitrary",) * len(grid)
    limit = int(min(max(2 * nbytes + (8 << 20), 32 << 20), VMEM_CAP))
    return pl.pallas_call(
        body, name=name, grid=grid, in_specs=in_specs, out_specs=out_specs, out_shape=out_shape,
        scratch_shapes=scratch, input_output_aliases=aliases or {},
        compiler_params=pltpu.CompilerParams(dimension_semantics=sem, vmem_limit_bytes=limit),
    )(*args)


def _sum_slots(x, tr, name):
    n, r, c = x.shape

    def body(x_ref, o_ref):
        acc = x_ref[0].astype(F32)
        for d in range(1, n):
            acc = acc + x_ref[d].astype(F32)
        o_ref[...] = acc

    return _call(body, name, (r // tr,), [pl.BlockSpec((n, tr, c), lambda i: (0, i, 0))],
                 [pl.BlockSpec((tr, c), lambda i: (i, 0))], [jax.ShapeDtypeStruct((r, c), F32)], [x],
                 sem=("parallel",), nbytes=_nbytes((n, tr, c), x.dtype) + _nbytes((tr, c), F32))[0]


_DIMS = {"nn": (((1,), (0,)), ((), ())), "nt": (((1,), (1,)), ((), ())), "tn": (((0,), (0,)), ((), ()))}


def _matmul(a, b, mode, out_dtype, name, comm=(), deps=(), halves=False):
    a_shape = (a.shape[1], 2 * a.shape[2]) if halves else a.shape
    if mode == "nn":
        (m, k), (k2, n) = a_shape, b.shape
    elif mode == "nt":
        (m, k), (n, k2) = a_shape, b.shape
    else:
        (k, m), (k2, n) = a_shape, b.shape
    assert k == k2, (a.shape, b.shape, mode)
    tm, tn, tk = _tile(m, MM_TM), _tile(n, MM_TN), _tile(k, MM_TK)
    if halves and mode == "nn":
        tk = _tile(k // 2, MM_TK)
    if halves and mode == "tn":
        tm = _tile(m // 2, MM_TM)
    nk = k // tk
    dims = _DIMS[mode]
    if halves:
        per_half = (k // 2) // tk if mode == "nn" else (m // 2) // tm
        a_spec = {"nn": pl.BlockSpec((None, tm, tk), lambda i, j, kk: (kk // per_half, i, kk % per_half)),
                  "tn": pl.BlockSpec((None, tk, tm), lambda i, j, kk: (i // per_half, kk, i % per_half))}[mode]
    else:
        a_spec = {"nn": pl.BlockSpec((tm, tk), lambda i, j, kk: (i, kk)),
                  "nt": pl.BlockSpec((tm, tk), lambda i, j, kk: (i, kk)),
                  "tn": pl.BlockSpec((tk, tm), lambda i, j, kk: (kk, i))}[mode]
    b_spec = {"nn": pl.BlockSpec((tk, tn), lambda i, j, kk: (kk, j)),
              "nt": pl.BlockSpec((tn, tk), lambda i, j, kk: (j, kk)),
              "tn": pl.BlockSpec((tk, tn), lambda i, j, kk: (kk, j))}[mode]

    def partial_product(a_ref, b_ref):
        return lax.dot_general(a_ref[...].astype(BF16), b_ref[...].astype(BF16), dims, preferred_element_type=F32)

    if nk == 1:
        def body(a_ref, b_ref, o_ref):
            o_ref[...] = partial_product(a_ref, b_ref).astype(o_ref.dtype)
        scratch = []
    else:
        def body(a_ref, b_ref, o_ref, acc_ref):
            kk = pl.program_id(2)

            @pl.when(kk == 0)
            def _():
                acc_ref[...] = jnp.zeros_like(acc_ref)

            acc_ref[...] += partial_product(a_ref, b_ref)

            @pl.when(kk == nk - 1)
            def _():
                o_ref[...] = acc_ref[...].astype(o_ref.dtype)
        scratch = [pltpu.VMEM((tm, tn), F32)]

    nbytes = (_nbytes((tm, tk), a.dtype) + _nbytes((tk, tn), b.dtype) + _nbytes((tm, tn), out_dtype)
              + _nbytes((tm, tn), F32))
    res = _call(body, name, (m // tm, n // tn, nk), [a_spec, b_spec],
                [pl.BlockSpec((tm, tn), lambda i, j, kk: (i, j))], [jax.ShapeDtypeStruct((m, n), out_dtype)], [a, b],
                scratch=scratch, sem=("parallel", "parallel", "arbitrary"), nbytes=nbytes, comm=comm, deps=deps)
    return res if comm else res[0]


def _matmul_post_norm(a, b, x, g_post, name, g_next=None, target=None):
    (m, k), (k2, d) = a.shape, b.shape
    assert k == k2 and d == D_MODEL
    with_loss = target is not None
    tm, tk = _tile(m, ROW_BLOCK if with_loss else 2 * ROW_BLOCK), _tile(k, MM_TK)
    nk = k // tk

    def body(a_ref, b_ref, x_ref, gp_ref, last_ref, o_ref, first_ref, second_ref, *rest):
        acc_ref = rest[-1]
        i, kk = pl.program_id(0), pl.program_id(1)

        @pl.when(kk == 0)
        def _():
            acc_ref[...] = jnp.zeros_like(acc_ref)

        acc_ref[...] += jnp.dot(a_ref[...], b_ref[...], preferred_element_type=F32)

        @pl.when(kk == nk - 1)
        def _():
            ov = acc_ref[...]
            o_ref[...] = ov
            xn = x_ref[...] + (ov * _rms(ov)) * gp_ref[...]
            if with_loss:
                dd_ref, dg_ref = rest[:2]
                err = xn - last_ref[...]
                dy = err / d
                first_ref[...] = dy
                dd, dg_rows = _rms_bwd(dy, ov, gp_ref[...])
                dd_ref[...] = dd.astype(BF16)

                @pl.when(i == 0)
                def _():
                    second_ref[...] = jnp.zeros_like(second_ref)
                    dg_ref[...] = jnp.zeros_like(dg_ref)

                second_ref[...] += 0.5 * jnp.sum(jnp.mean(err * err, axis=-1, keepdims=True), axis=0, keepdims=True)
                dg_ref[...] += jnp.sum(dg_rows, axis=0, keepdims=True)
            else:
                first_ref[...] = xn
                second_ref[...] = ((xn * _rms(xn)) * last_ref[...]).astype(BF16)

    rows = pl.BlockSpec((tm, d), lambda i, kk: (i, 0))
    vec = pl.BlockSpec((1, d), lambda i, kk: (0, 0))
    second = ((pl.BlockSpec((1, 1), lambda i, kk: (0, 0)), jax.ShapeDtypeStruct((1, 1), F32)) if with_loss
              else (rows, jax.ShapeDtypeStruct((m, d), BF16)))
    nbytes = _nbytes((tm, tk), BF16) + _nbytes((tk, d), BF16) + 6 * _nbytes((tm, d), F32)
    extra_specs = [rows, vec] if with_loss else []
    extra_shapes = [jax.ShapeDtypeStruct((m, d), BF16), jax.ShapeDtypeStruct((1, d), F32)] if with_loss else []
    return _call(body, name, (m // tm, nk),
                 [pl.BlockSpec((tm, tk), lambda i, kk: (i, kk)), pl.BlockSpec((tk, d), lambda i, kk: (kk, 0)), rows, vec,
                  rows if with_loss else vec],
                 [rows, rows, second[0]] + extra_specs,
                 [jax.ShapeDtypeStruct((m, d), F32), jax.ShapeDtypeStruct((m, d), F32), second[1]] + extra_shapes,
                 [a, b, x, g_post, target if with_loss else g_next],
                 scratch=[pltpu.VMEM((tm, d), F32)], sem=("arbitrary", "arbitrary"), nbytes=nbytes)


def _matmul_norm_bwd(a, b, dskip, xin, g_pre, name, post=None, halves=False, deps=()):
    m, k = (a.shape[1], 2 * a.shape[2]) if halves else a.shape
    (k2, d) = b.shape
    assert k == k2 and d == D_MODEL
    tm = _tile(m, ROW_BLOCK if post else 2 * ROW_BLOCK)
    tk = _tile(k // 2 if halves else k, MM_TK)
    nk = k // tk
    n_in = 5 + (2 if post else 0)

    def body(*refs):
        (a_ref, b_ref, dskip_ref, xin_ref, gpre_ref), rest = refs[:5], list(refs[5:n_in])
        outs, acc_ref = list(refs[n_in:-1]), refs[-1]
        i, kk = pl.program_id(0), pl.program_id(1)

        @pl.when(kk == 0)
        def _():
            acc_ref[...] = jnp.zeros_like(acc_ref)

        acc_ref[...] += jnp.dot(a_ref[...], b_ref[...], preferred_element_type=F32)

        @pl.when(kk == nk - 1)
        def _():
            dxin, dg_rows = _rms_bwd(acc_ref[...], xin_ref[...], gpre_ref[...])
            dx = dskip_ref[...] + dxin
            dx_ref, dgpre_ref = outs[0], outs[1]
            dx_ref[...] = dx

            @pl.when(i == 0)
            def _():
                for ref in outs[1::2]:
                    ref[...] = jnp.zeros_like(ref)

            dgpre_ref[...] += jnp.sum(dg_rows, axis=0, keepdims=True)
            if post:
                da, dg_rows = _rms_bwd(dx, rest[0][...], rest[1][...])
                outs[2][...] = da.astype(BF16)
                outs[3][...] += jnp.sum(dg_rows, axis=0, keepdims=True)

    if halves:
        per_half = (k // 2) // tk
        a_spec = pl.BlockSpec((None, tm, tk), lambda i, kk: (kk // per_half, i, kk % per_half))
    else:
        a_spec = pl.BlockSpec((tm, tk), lambda i, kk: (i, kk))
    rows = pl.BlockSpec((tm, d), lambda i, kk: (i, 0))
    vec = pl.BlockSpec((1, d), lambda i, kk: (0, 0))
    args, in_specs = [a, b, dskip, xin, g_pre], [a_spec, pl.BlockSpec((tk, d), lambda i, kk: (kk, 0)), rows, rows, vec]
    out_specs, out_shape = [rows, vec], [jax.ShapeDtypeStruct((m, d), F32), jax.ShapeDtypeStruct((1, d), F32)]
    if post:
        args += list(post)
        in_specs += [rows, vec]
        out_specs += [rows, vec]
        out_shape += [jax.ShapeDtypeStruct((m, d), BF16), jax.ShapeDtypeStruct((1, d), F32)]
    nbytes = _nbytes((tm, tk), BF16) + _nbytes((tk, d), BF16) + 7 * _nbytes((tm, d), F32)
    return _call(body, name, (m // tm, nk), in_specs, out_specs, out_shape, args, scratch=[pltpu.VMEM((tm, d), F32)],
                 sem=("arbitrary", "arbitrary"), nbytes=nbytes, deps=deps)


def _rms(x):
    return lax.rsqrt(jnp.mean(x * x, axis=-1, keepdims=True) + EPS)


def _rms_bwd(dz, a, g):
    r = _rms(a)
    dzg = dz * g
    da = r * dzg - a * (r * r * r) * jnp.mean(dzg * a, axis=-1, keepdims=True)
    return da, dz * (a * r)


def _row_spec(tr, width):
    return pl.BlockSpec((tr, width), lambda i: (i, 0))


def _vec_spec(width):
    return pl.BlockSpec((1, width), lambda i: (0, 0))


def _rms_fwd(x, g, name, comm=()):
    t, d = x.shape
    tr = min(ROW_BLOCK, t)

    def body(x_ref, g_ref, h_ref):
        xv = x_ref[...]
        h_ref[...] = ((xv * _rms(xv)) * g_ref[...]).astype(BF16)

    return _call(body, name, (t // tr,), [_row_spec(tr, d), _vec_spec(d)], [_row_spec(tr, d)],
                 [jax.ShapeDtypeStruct((t, d), BF16)], [x, g], sem=("parallel",), nbytes=3 * _nbytes((tr, d), F32),
                 comm=comm)


def _sigmoid(x):
    return 1.0 / (1.0 + jnp.exp(-x))


def _ffn_in(h, w_gu_t, name):
    t, d = h.shape
    tm, tn = _tile(t, 1024), _tile(D_FF, MM_TN)
    n_tiles = D_FF // tn

    def body(h_ref, wg_ref, wu_ref, g_ref, u_ref, a_ref):
        hv = h_ref[...]
        gate = lax.dot_general(hv, wg_ref[...], _DIMS["nt"], preferred_element_type=F32).astype(BF16)
        up = lax.dot_general(hv, wu_ref[...], _DIMS["nt"], preferred_element_type=F32).astype(BF16)
        g_ref[...] = gate
        u_ref[...] = up
        gate, up = gate.astype(F32), up.astype(F32)
        a_ref[...] = ((gate * _sigmoid(gate)) * up).astype(BF16)

    tile = pl.BlockSpec((tm, tn), lambda i, j: (i, j))
    shape = jax.ShapeDtypeStruct((t, D_FF), BF16)
    return _call(body, name, (t // tm, n_tiles),
                 [pl.BlockSpec((tm, d), lambda i, j: (i, 0)), pl.BlockSpec((tn, d), lambda i, j: (j, 0)),
                  pl.BlockSpec((tn, d), lambda i, j: (j + n_tiles, 0))],
                 [tile, tile, tile], [shape, shape, shape], [h, w_gu_t, w_gu_t], sem=("parallel", "parallel"),
                 nbytes=_nbytes((tm, d), BF16) + 2 * _nbytes((tn, d), BF16) + 5 * _nbytes((tm, tn), F32))


def _ffn_in_bwd(dd, w_d, gate, up, name):
    t, d = dd.shape
    tm, tn = _tile(t, 1024), _tile(D_FF, MM_TN)

    def body(dd_ref, w_ref, g_ref, u_ref, o_ref):
        da = lax.dot_general(dd_ref[...], w_ref[...], _DIMS["nt"], preferred_element_type=F32)
        gate, up = g_ref[...].astype(F32), u_ref[...].astype(F32)
        s = _sigmoid(gate)
        o_ref[0] = (da * up * (s * (1.0 + gate * (1.0 - s)))).astype(BF16)
        o_ref[1] = (da * (gate * s)).astype(BF16)

    tile = pl.BlockSpec((tm, tn), lambda i, j: (i, j))
    return _call(body, name, (t // tm, D_FF // tn),
                 [pl.BlockSpec((tm, d), lambda i, j: (i, 0)), pl.BlockSpec((tn, d), lambda i, j: (j, 0)), tile, tile],
                 [pl.BlockSpec((2, tm, tn), lambda i, j: (0, i, j))], [jax.ShapeDtypeStruct((2, t, D_FF), BF16)],
                 [dd, w_d, gate, up], sem=("parallel", "parallel"),
                 nbytes=_nbytes((tm, d), BF16) + _nbytes((tn, d), BF16) + 6 * _nbytes((tm, tn), F32))[0]


def _split3(x):
    hi = x.astype(BF16)
    r1 = x - hi.astype(F32)
    mid = r1.astype(BF16)
    lo = (r1 - mid.astype(F32)).astype(BF16)
    return hi, mid, lo


def _dot_exact(mat01, x):
    hi, mid, lo = _split3(x)
    out = jnp.dot(mat01, hi, preferred_element_type=F32)
    out = out + jnp.dot(mat01, mid, preferred_element_type=F32)
    return out + jnp.dot(mat01, lo, preferred_element_type=F32)


def _dot_exact_rhs(x, mat01):
    hi, mid, lo = _split3(x)
    out = jnp.dot(hi, mat01, preferred_element_type=F32)
    out = out + jnp.dot(mid, mat01, preferred_element_type=F32)
    return out + jnp.dot(lo, mat01, preferred_element_type=F32)


def _tri(lower):
    r = lax.broadcasted_iota(jnp.int32, (CHUNK, CHUNK), 0)
    c = lax.broadcasted_iota(jnp.int32, (CHUNK, CHUNK), 1)
    return jnp.where(r >= c if lower else r <= c, 1.0, 0.0).astype(BF16)


def _log_sigmoid(z):
    return jnp.minimum(z, 0.0) - jnp.log(1.0 + jnp.exp(-jnp.abs(z)))


def _forget_cumsum(proj, b_pad, name):
    t = proj.shape[0]
    nb = t // CHUNK

    def body(f_ref, b_ref, c_ref):
        tri = _tri(True)
        b = b_ref[...]

        def blk(i, carry):
            rows = pl.ds(pl.multiple_of(i * CHUNK, CHUNK), CHUNK)
            cs = _dot_exact(tri, _log_sigmoid(f_ref[rows, :] + b)) + carry
            c_ref[rows, :] = cs
            return cs[CHUNK - 1:CHUNK, :]

        lax.fori_loop(0, nb, blk, jnp.zeros((1, HEAD_DIM), F32))

    return _call(body, name, (1,),
                 [pl.BlockSpec((t, HEAD_DIM), lambda i: (0, CB_F)), pl.BlockSpec((1, HEAD_DIM), lambda i: (0, 0))],
                 [pl.BlockSpec((t, HEAD_DIM), lambda i: (0, 0))], [jax.ShapeDtypeStruct((t, HEAD_DIM), F32)],
                 [proj, b_pad], sem=("arbitrary",), nbytes=2 * _nbytes((t, HEAD_DIM), F32))[0]


def _causal(block):
    r = lax.broadcasted_iota(jnp.int32, (block, block), 0)
    c = lax.broadcasted_iota(jnp.int32, (block, block), 1)
    return c <= r


def _lanes(x, width):
    return jnp.concatenate([x] * (width // HEAD_DIM), axis=1)


def _chunks(t, blk):
    return [slice(r * blk, (r + 1) * blk) for r in range(t // blk)]


def _lane_of_head(x, h):
    lane = lax.broadcasted_iota(jnp.int32, x.shape, 1)
    return jnp.broadcast_to(jnp.sum(jnp.where(lane == h, x, 0.0), axis=1, keepdims=True), x.shape)


def _causal_t(block):
    r = lax.broadcasted_iota(jnp.int32, (block, block), 0)
    c = lax.broadcasted_iota(jnp.int32, (block, block), 1)
    return r <= c


def _attn_fwd_t(proj, c_pad, c_row, name, comm=()):
    t = proj.shape[0]
    blk = min(ATTN_BLOCK, t)
    nq = t // blk
    scale = HEAD_DIM ** -0.5

    def body(q_ref, k_ref, v_ref, cp_ref, cr_ref, y_ref, e_ref, kb_s, vt_s, cb_s):
        h, i = pl.program_id(0), pl.program_id(1)

        @pl.when(i == 0)
        def _():
            for rows in _chunks(t, blk):
                kb_s[rows, :] = k_ref[rows, :].astype(BF16)
                vt_s[:, rows] = v_ref[rows, :].T.astype(BF16)
                cb_s[rows, :] = _lane_of_head(cp_ref[rows, :], h)

        q = (q_ref[...] * scale).astype(BF16)
        ci = cr_ref[:, pl.ds(pl.multiple_of(i * blk, blk), blk)]

        def step(j, carry, diagonal):
            m, l, acc = carry
            rows = pl.ds(pl.multiple_of(j * blk, blk), blk)
            s = lax.dot_general(kb_s[rows, :], q, _DIMS["nt"], preferred_element_type=F32)
            s = s + ci - _lanes(cb_s[rows, :], blk)
            if diagonal:
                s = jnp.where(_causal_t(blk), s, NEG)
            m_new = jnp.maximum(m, jnp.max(s, axis=0, keepdims=True))
            alpha = jnp.exp(m - m_new)
            p = jnp.exp(s - m_new)
            l = alpha * l + jnp.sum(p, axis=0, keepdims=True)
            acc = alpha * acc + jnp.dot(vt_s[:, rows], p.astype(BF16), preferred_element_type=F32)
            return m_new, l, acc

        init = (jnp.full((1, blk), NEG, F32), jnp.zeros((1, blk), F32), jnp.zeros((HEAD_DIM, blk), F32))
        carry = lax.fori_loop(0, i, lambda j, c: step(j, c, False), init)
        m, l, acc = step(i, carry, True)
        y_ref[...] = (acc / l).T
        e_ref[...] = ci - (m + jnp.log(l))

    nbytes = 3 * _nbytes((t, HEAD_DIM), F32) + 4 * _nbytes((blk, HEAD_DIM), F32) + 4 * _nbytes((blk, blk), F32)
    return _call(body, name, (N_HEADS, nq),
                 [pl.BlockSpec((blk, HEAD_DIM), lambda h, i: (i, CB_QKV + 3 * h)),
                  pl.BlockSpec((t, HEAD_DIM), lambda h, i: (0, CB_QKV + 3 * h + 1)),
                  pl.BlockSpec((t, HEAD_DIM), lambda h, i: (0, CB_QKV + 3 * h + 2)),
                  pl.BlockSpec((t, HEAD_DIM), lambda h, i: (0, 0)),
                  pl.BlockSpec((None, 1, t), lambda h, i: (h, 0, 0))],
                 [pl.BlockSpec((blk, HEAD_DIM), lambda h, i: (i, h)),
                  pl.BlockSpec((None, 1, blk), lambda h, i: (h, 0, i))],
                 [jax.ShapeDtypeStruct((t, D_MODEL), F32), jax.ShapeDtypeStruct((N_HEADS, 1, t), F32)],
                 [proj, proj, proj, c_pad, c_row],
                 scratch=[pltpu.VMEM((t, HEAD_DIM), BF16), pltpu.VMEM((HEAD_DIM, t), BF16),
                          pltpu.VMEM((t, HEAD_DIM), F32)],
                 sem=("parallel", "arbitrary"), nbytes=nbytes, comm=comm)


def _attn_bwd_t(proj, dy, e_row, delta_b, c_pad, dproj, name, deps=()):
    t = proj.shape[0]
    blk = min(ATTN_BLOCK, t)
    nb = t // blk
    scale = HEAD_DIM ** -0.5

    def body(q_ref, k_ref, v_ref, dy_ref, e_ref, dl_ref, cp_ref, dproj_in, dqkv_ref, dck_ref, dcq_ref,
             qb_s, qt_s, dyb_s, dyt_s, dl_s, dqt_acc):
        del dproj_in
        h, j = pl.program_id(0), pl.program_id(1)

        @pl.when(j == 0)
        def _():
            for rows in _chunks(t, blk):
                qs = q_ref[rows, :] * scale
                qb_s[rows, :] = qs.astype(BF16)
                qt_s[:, rows] = qs.T.astype(BF16)
                dyr = dy_ref[rows, :]
                dyb_s[rows, :] = dyr.astype(BF16)
                dyt_s[:, rows] = dyr.T.astype(BF16)
                dl_s[:, rows] = dl_ref[rows, :].T[0:1, :]
            dqt_acc[...] = jnp.zeros_like(dqt_acc)
            dcq_ref[...] = jnp.zeros_like(dcq_ref)

        kf = k_ref[...]
        kb, kt = kf.astype(BF16), kf.T.astype(BF16)
        vb = v_ref[...].astype(BF16)
        cj = _lanes(_lane_of_head(cp_ref[...], h), blk)

        def step(i, carry, diagonal):
            dkt, dvt, dkey = carry
            cols = pl.ds(pl.multiple_of(i * blk, blk), blk)
            s = lax.dot_general(kb, qb_s[cols, :], _DIMS["nt"], preferred_element_type=F32) + e_ref[:, cols] - cj
            if diagonal:
                s = jnp.where(_causal_t(blk), s, NEG)
            p = jnp.exp(s)
            dp = lax.dot_general(vb, dyb_s[cols, :], _DIMS["nt"], preferred_element_type=F32)
            ds = p * (dp - dl_s[:, cols])
            pb, dsb = p.astype(BF16), ds.astype(BF16)
            dvt = dvt + lax.dot_general(dyt_s[:, cols], pb, _DIMS["nt"], preferred_element_type=F32)
            dkt = dkt + lax.dot_general(qt_s[:, cols], dsb, _DIMS["nt"], preferred_element_type=F32)
            dqt_acc[:, cols] += jnp.dot(kt, dsb, preferred_element_type=F32) * scale
            dcq_ref[:, cols] += jnp.sum(ds, axis=0, keepdims=True)
            for k in range(blk // HEAD_DIM):
                dkey = dkey + ds[:, k * HEAD_DIM:(k + 1) * HEAD_DIM]
            return dkt, dvt, dkey

        zeros = jnp.zeros((HEAD_DIM, blk), F32)
        carry = step(j, (zeros, zeros, jnp.zeros((blk, HEAD_DIM), F32)), True)
        dkt, dvt, dkey = lax.fori_loop(j + 1, nb, lambda i, c: step(i, c, False), carry)
        mine = pl.ds(pl.multiple_of(j * blk, blk), blk)
        dqkv_ref[:, :HEAD_DIM] = dqt_acc[:, mine].T.astype(BF16)
        dqkv_ref[:, HEAD_DIM:2 * HEAD_DIM] = dkt.T.astype(BF16)
        dqkv_ref[:, 2 * HEAD_DIM:] = dvt.T.astype(BF16)
        dck_ref[...] = -jnp.sum(dkey.T, axis=0, keepdims=True)

    full = lambda cb: pl.BlockSpec((t, HEAD_DIM), lambda h, j: (0, cb(h)))
    head = lambda h: h
    row = pl.BlockSpec((None, 1, t), lambda h, j: (h, 0, 0))
    nbytes = (8 * _nbytes((t, HEAD_DIM), F32) + 4 * _nbytes((blk, HEAD_DIM), F32) + 6 * _nbytes((blk, blk), F32))
    return _call(body, name, (N_HEADS, nb),
                 [full(lambda h: CB_QKV + 3 * h),
                  pl.BlockSpec((blk, HEAD_DIM), lambda h, j: (j, CB_QKV + 3 * h + 1)),
                  pl.BlockSpec((blk, HEAD_DIM), lambda h, j: (j, CB_QKV + 3 * h + 2)),
                  full(head), row, full(head),
                  pl.BlockSpec((blk, HEAD_DIM), lambda h, j: (j, 0)),
                  pl.BlockSpec(memory_space=pl.ANY)],
                 [pl.BlockSpec((blk, QKV_WIDTH), lambda h, j: (j, (GATED + F_WIDTH) // QKV_WIDTH + h)),
                  pl.BlockSpec((None, 1, blk), lambda h, j: (h, 0, j)),
                  row],
                 [jax.ShapeDtypeStruct(dproj.shape, dproj.dtype), jax.ShapeDtypeStruct((N_HEADS, 1, t), F32),
                  jax.ShapeDtypeStruct((N_HEADS, 1, t), F32)],
                 [proj, proj, proj, dy, e_row, delta_b, c_pad, dproj],
                 scratch=[pltpu.VMEM((t, HEAD_DIM), BF16), pltpu.VMEM((HEAD_DIM, t), BF16),
                          pltpu.VMEM((t, HEAD_DIM), BF16), pltpu.VMEM((HEAD_DIM, t), BF16),
                          pltpu.VMEM((1, t), F32), pltpu.VMEM((HEAD_DIM, t), F32)],
                 sem=("parallel", "arbitrary"), nbytes=nbytes, aliases={7: 0}, deps=deps)


def _forget_bwd_t(proj, b_pad, dc, dproj, name):
    t = proj.shape[0]
    tr = min(ROW_BLOCK, t)
    nb = t // tr
    rev = lambda i: nb - 1 - i

    def body(f_ref, b_ref, dc_ref, dproj_in, df_ref, db_ref, run_ref):
        del dproj_in

        @pl.when(pl.program_id(0) == 0)
        def _():
            run_ref[...] = jnp.zeros_like(run_ref)
            db_ref[...] = jnp.zeros_like(db_ref)

        tri = _tri(False)
        b = b_ref[...]
        lane = lax.broadcasted_iota(jnp.int32, (CHUNK, HEAD_DIM), 1)
        df_ref[...] = jnp.zeros_like(df_ref)
        for n in reversed(range(tr // CHUNK)):
            rows = slice(n * CHUNK, (n + 1) * CHUNK)
            dlogf = _dot_exact(tri, dc_ref[rows, :]) + run_ref[...]
            run_ref[...] = dlogf[0:1, :]
            z = f_ref[rows, :] + b
            e = jnp.exp(-jnp.abs(z))
            sig_neg = jnp.where(z >= 0.0, e, 1.0) / (1.0 + e)
            df = jnp.where(lane < N_HEADS, dlogf * sig_neg, 0.0)
            df_ref[rows, :HEAD_DIM] = df.astype(BF16)
            db_ref[...] += jnp.sum(df, axis=0, keepdims=True)

    return _call(body, name, (nb,),
                 [pl.BlockSpec((tr, HEAD_DIM), lambda i: (rev(i), CB_F)), pl.BlockSpec((1, HEAD_DIM), lambda i: (0, 0)),
                  pl.BlockSpec((tr, HEAD_DIM), lambda i: (rev(i), 0)), pl.BlockSpec(memory_space=pl.ANY)],
                 [pl.BlockSpec((tr, F_WIDTH), lambda i: (rev(i), GATED // F_WIDTH)),
                  pl.BlockSpec((1, HEAD_DIM), lambda i: (0, 0))],
                 [jax.ShapeDtypeStruct(dproj.shape, dproj.dtype), jax.ShapeDtypeStruct((1, HEAD_DIM), F32)],
                 [proj, b_pad, dc, dproj], scratch=[pltpu.VMEM((1, HEAD_DIM), F32)],
                 sem=("arbitrary",), nbytes=4 * _nbytes((tr, HEAD_DIM), F32), aliases={3: 0})


_GELU_K = math.sqrt(2.0 / math.pi)
_GELU_C = 0.044715


def _gelu(x):
    t = jnp.tanh(_GELU_K * (x + _GELU_C * (x * x * x)))
    return 0.5 * x * (1.0 + t), t


def _gelu_grad(x, t):
    return 0.5 * (1.0 + t) + 0.5 * x * (1.0 - t * t) * (_GELU_K * (1.0 + 3.0 * _GELU_C * (x * x)))


def _layernorm_stats(a):
    mu = jnp.mean(a, axis=-1, keepdims=True)
    xc = a - mu
    r = lax.rsqrt(jnp.mean(xc * xc, axis=-1, keepdims=True) + EPS)
    return xc * r, r


def _group(g):
    return slice(g * CHUNK, (g + 1) * CHUNK)


def _fixed(shape):
    return pl.BlockSpec(shape, lambda i: (0,) * len(shape))


def _mix_fwd(proj, y_b, w_s, b_cols, g_v, name):
    t = proj.shape[0]
    step_rows = min(MIX_ROWS, t)

    def body(u_ref, vs_ref, ga_ref, gb_ref, yb_ref, w_ref, b_ref, gv_ref, o_ref):
        mask = _causal(CHUNK)
        w = [jnp.where(mask, w_ref[g], 0.0).astype(BF16) for g in range(N_GROUPS)]
        for rows in _chunks(step_rows, CHUNK):
            a_u, _ = _gelu(u_ref[rows, :])
            a_v, _ = _gelu(vs_ref[rows, :])
            xhat, _ = _layernorm_stats(a_v)
            vn = (xhat * gv_ref[...]).astype(BF16)
            sa, sb = _sigmoid(ga_ref[rows, :]), _sigmoid(gb_ref[rows, :])
            yb = yb_ref[rows, :]
            for g in range(N_GROUPS):
                cols = _group(g)
                mixed = jnp.dot(w[g], vn[:, cols], preferred_element_type=F32) + b_ref[:, g:g + 1]
                o_ref[rows, cols] = (sa[:, cols] * (a_u[:, cols] * mixed) + sb[:, cols] * yb[:, cols]).astype(BF16)

    gated = lambda k: pl.BlockSpec((step_rows, D_MODEL), lambda i: (i, k))
    return _call(body, name, (t // step_rows,),
                 [gated(0), gated(1), gated(2), gated(3), _row_spec(step_rows, D_MODEL),
                  _fixed((N_GROUPS, CHUNK, CHUNK)), _fixed((CHUNK, N_GROUPS)), _fixed((1, D_MODEL))],
                 [_row_spec(step_rows, D_MODEL)], [jax.ShapeDtypeStruct((t, D_MODEL), BF16)],
                 [proj, proj, proj, proj, y_b, w_s, b_cols, g_v], sem=("parallel",),
                 nbytes=16 * _nbytes((step_rows, D_MODEL), F32))[0]


def _mix_bwd(proj, y_b, dmerged, w_s, b_cols, g_v, sel, name, deps=()):
    t = proj.shape[0]
    step_rows = min(MIX_BWD_ROWS, t)

    def body(u_ref, vs_ref, ga_ref, gb_ref, yb_ref, dm_ref, w_ref, b_ref, gv_ref, sel_ref,
             dg_ref, dyb_ref, delta_ref, dw_ref, dbt_ref, dgv_ref):
        @pl.when(pl.program_id(0) == 0)
        def _():
            dw_ref[...] = jnp.zeros_like(dw_ref)
            dbt_ref[...] = jnp.zeros_like(dbt_ref)
            dgv_ref[...] = jnp.zeros_like(dgv_ref)

        gv = gv_ref[...]
        mask = _causal(CHUNK)
        w = [jnp.where(mask, w_ref[g], 0.0).astype(BF16) for g in range(N_GROUPS)]
        for rows in _chunks(step_rows, CHUNK):
            u, vs = u_ref[rows, :], vs_ref[rows, :]
            a_u, t_u = _gelu(u)
            a_v, t_v = _gelu(vs)
            xhat, r = _layernorm_stats(a_v)
            vn = (xhat * gv).astype(BF16)
            sa, sb = _sigmoid(ga_ref[rows, :]), _sigmoid(gb_ref[rows, :])
            yb, dm = yb_ref[rows, :], dm_ref[rows, :].astype(F32)
            dyb = dm * sb
            dyb_ref[rows, :] = dyb
            dyb_yb = dyb * yb
            dg_ref[rows, 3 * D_MODEL:] = (dm * yb * (sb * (1.0 - sb))).astype(BF16)
            dya = dm * sa
            dmixed_parts, dvn_parts = [], []
            for g in range(N_GROUPS):
                cols = _group(g)
                delta_ref[rows, cols] = jnp.broadcast_to(jnp.sum(dyb_yb[:, cols], axis=1, keepdims=True),
                                                         (CHUNK, CHUNK))
                mixed = jnp.dot(w[g], vn[:, cols], preferred_element_type=F32) + b_ref[:, g:g + 1]
                y_a = a_u[:, cols] * mixed
                dg_ref[rows, 2 * D_MODEL + g * CHUNK:2 * D_MODEL + (g + 1) * CHUNK] = (
                    dm[:, cols] * y_a * (sa[:, cols] * (1.0 - sa[:, cols]))).astype(BF16)
                dg_ref[rows, cols] = (dya[:, cols] * mixed * _gelu_grad(u[:, cols], t_u[:, cols])).astype(BF16)
                dmixed = dya[:, cols] * a_u[:, cols]
                dmb = dmixed.astype(BF16)
                dw = lax.dot_general(dmb, vn[:, cols], _DIMS["nt"], preferred_element_type=F32)
                dw_ref[g] += jnp.where(mask, dw, 0.0)
                dvn_parts.append(lax.dot_general(w[g], dmb, _DIMS["tn"], preferred_element_type=F32))
                dmixed_parts.append(dmixed)
            dmixed_all = jnp.concatenate(dmixed_parts, axis=1)
            dvn = jnp.concatenate(dvn_parts, axis=1)
            dbt_ref[...] += _dot_exact_rhs(dmixed_all, sel_ref[...])
            dgv_ref[...] += jnp.sum(dvn * xhat, axis=0, keepdims=True)
            dxh = dvn * gv
            da_v = r * (dxh - jnp.mean(dxh, axis=-1, keepdims=True)
                        - xhat * jnp.mean(dxh * xhat, axis=-1, keepdims=True))
            dg_ref[rows, D_MODEL:2 * D_MODEL] = (da_v * _gelu_grad(vs, t_v)).astype(BF16)

    row = lambda width: _row_spec(step_rows, width)
    gated = lambda k: pl.BlockSpec((step_rows, D_MODEL), lambda i: (i, k))
    return _call(body, name, (t // step_rows,),
                 [gated(0), gated(1), gated(2), gated(3), row(D_MODEL), row(D_MODEL),
                  _fixed((N_GROUPS, CHUNK, CHUNK)), _fixed((CHUNK, N_GROUPS)), _fixed((1, D_MODEL)),
                  _fixed((D_MODEL, HEAD_DIM))],
                 [row(GATED), row(D_MODEL), row(D_MODEL),
                  _fixed((N_GROUPS, CHUNK, CHUNK)), _fixed((CHUNK, HEAD_DIM)), _fixed((1, D_MODEL))],
                 [jax.ShapeDtypeStruct((t, IN_PAD), BF16), jax.ShapeDtypeStruct((t, D_MODEL), F32),
                  jax.ShapeDtypeStruct((t, D_MODEL), F32), jax.ShapeDtypeStruct((N_GROUPS, CHUNK, CHUNK), F32),
                  jax.ShapeDtypeStruct((CHUNK, HEAD_DIM), F32), jax.ShapeDtypeStruct((1, D_MODEL), F32)],
                 [proj, proj, proj, proj, y_b, dmerged, w_s, b_cols, g_v, sel], sem=("arbitrary",),
                 nbytes=24 * _nbytes((step_rows, D_MODEL), F32), deps=deps)


def _adamw_update(w_ref, g_ref, m_ref, v_ref, d_ref, nm_ref, nv_ref):
    gv = g_ref[...]
    nm = ADAM_B1 * m_ref[...] + (1.0 - ADAM_B1) * gv
    nv = ADAM_B2 * v_ref[...] + (1.0 - ADAM_B2) * (gv * gv)
    m_hat = nm / (1.0 - ADAM_B1 ** ADAM_STEP)
    v_hat = nv / (1.0 - ADAM_B2 ** ADAM_STEP)
    d_ref[...] = -ADAM_LR * (m_hat / (jnp.sqrt(v_hat) + ADAM_EPS) + ADAM_WD * w_ref[...])
    nm_ref[...] = nm
    nv_ref[...] = nv


def _adamw(w, g, m, v, name):
    r, c = w.shape
    tr = _tile(r, 256, unit=8)

    def body(w_ref, g_ref, m_ref, v_ref, d_ref, nm_ref, nv_ref):
        _adamw_update(w_ref, g_ref, m_ref, v_ref, d_ref, nm_ref, nv_ref)

    spec = pl.BlockSpec((tr, c), lambda i: (i, 0))
    shape = jax.ShapeDtypeStruct((r, c), F32)
    return _call(body, name, (r // tr,), [spec] * 4, [spec] * 3, [shape] * 3, [w, g, m, v], sem=("parallel",),
                 nbytes=7 * _nbytes((tr, max(c, 128)), F32))


def _adamw_cols(w, g, m, v, name):
    layers, r, c = w.shape
    tc = 256

    def body(w_ref, g_ref, m_ref, v_ref, d_ref, nm_ref, nv_ref):
        _adamw_update(w_ref, g_ref, m_ref, v_ref, d_ref, nm_ref, nv_ref)

    spec = pl.BlockSpec((None, r, tc), lambda l, j: (l, 0, j))
    shape = jax.ShapeDtypeStruct(w.shape, F32)
    return _call(body, name, (layers, c // tc), [spec] * 4, [spec] * 3, [shape] * 3, [w, g, m, v],
                 sem=("parallel", "parallel"), nbytes=7 * _nbytes((r, tc), F32))


def _pad_rows(a, rows):
    return jnp.pad(a, ((0, rows - a.shape[0]), (0, 0)))


def _unpack_in(gathered):
    i = jnp.arange(16)[None, :, None]
    b = jnp.arange(N_DEV)[:, None, None]
    head = jnp.where(i < b, jnp.roll(gathered[:, IN_STRIDE:IN_STRIDE + 16], 1, axis=0), gathered[:, :16])
    nat = jnp.concatenate([head, gathered[:, 16:IN_STRIDE]], axis=1).reshape(N_DEV * IN_STRIDE, D_MODEL)
    qkv = nat[2 * D_MODEL:5 * D_MODEL].reshape(3, N_HEADS, HEAD_DIM, D_MODEL)
    qkv = jnp.transpose(qkv, (1, 0, 2, 3)).reshape(3 * D_MODEL, D_MODEL)
    f = _pad_rows(gathered[N_DEV - 1, IN_STRIDE:IN_STRIDE + N_HEADS], F_WIDTH)
    return jnp.concatenate([nat[:2 * D_MODEL], nat[5 * D_MODEL:7 * D_MODEL], f, qkv], axis=0)


def _pack_grad_in(dw_in_t):
    qkv = dw_in_t[GATED + F_WIDTH:].reshape(N_HEADS, 3, HEAD_DIM, D_MODEL)
    qkv = jnp.transpose(qkv, (1, 0, 2, 3)).reshape(3 * D_MODEL, D_MODEL)
    return jnp.concatenate([dw_in_t[:2 * D_MODEL], qkv, dw_in_t[2 * D_MODEL:GATED],
                            dw_in_t[GATED:GATED + IN_ROWS - N_DEV * IN_STRIDE]], axis=0)


def kernel(x, mix_pre_g, w_in, b_forget, sgu_norm_g, w_spatial, b_spatial, w_out, mix_post_g, ffn_pre_g, w_gate, w_up, w_down, ffn_post_g, loss_target, m_mix_pre_g, m_w_in, m_b_forget, m_sgu_norm_g, m_w_spatial, m_b_spatial, m_w_out, m_mix_post_g, m_ffn_pre_g, m_w_gate, m_w_up, m_w_down, m_ffn_post_g, v_mix_pre_g, v_w_in, v_b_forget, v_sgu_norm_g, v_w_spatial, v_b_spatial, v_w_out, v_mix_post_g, v_ffn_pre_g, v_w_gate, v_w_up, v_w_down, v_ffn_post_g):
    depth = w_in.shape[0]
    assert depth == 2
    t = x.shape[1]
    x0 = x.reshape(t, D_MODEL)
    target = loss_target.reshape(t, D_MODEL)
    sel = (jnp.arange(D_MODEL)[:, None] // HEAD_DIM == jnp.arange(HEAD_DIM)[None, :]).astype(BF16)
    vec = lambda a, l: a[l][None, :]
    zero_row = jnp.zeros((1, D_MODEL), F32)

    me = 4 * lax.axis_index("x") + 2 * lax.axis_index("y") + lax.axis_index("c")
    p_in = [lax.dynamic_update_slice(jnp.zeros((P_IN, D_MODEL), BF16), w_in[l].T.astype(BF16), (me, 0))
            for l in range(depth)]
    p_rest = [[w_out[l].astype(BF16), jnp.stack([w_gate[l].T, w_up[l].T]).astype(BF16), w_down[l].astype(BF16)]
              for l in range(depth)]
    w_in_t = [None] * depth
    w_rest = [None] * depth

    saved = []
    xl = x0
    h, gathered = _rms_fwd(xl, vec(mix_pre_g, 0), "rms_in_0", comm=[(("gather", 0), p_in[0])])
    w_in_t[0] = _unpack_in(gathered)
    dy = loss = None
    for l in range(depth):
        b_pad = jnp.pad(b_forget[l], (0, HEAD_DIM - N_HEADS))[None, :]
        b_cols = b_spatial[l].T
        proj = _matmul(h, w_in_t[l], "nt", F32, f"proj_{l}")
        c_pad = _forget_cumsum(proj, b_pad, f"forget_cumsum_{l}")
        c_row = c_pad[:, :N_HEADS].T[:, None, :]
        riders = [(("gather", 0), p_rest[l][0]), (("gather", 1), p_rest[l][1]), (("gather", 0), p_rest[l][2])]
        riders += [(("gather", 0), p_in[l + 1])] if l + 1 < depth else []
        y_b, e_row, *arrived = _attn_fwd_t(proj, c_pad, c_row, f"attn_fwd_{l}", comm=riders)
        w_rest[l] = (arrived[0].reshape(D_MODEL, D_MODEL), arrived[1].reshape(2 * D_FF, D_MODEL),
                     arrived[2].reshape(D_FF, D_MODEL))
        if l + 1 < depth:
            w_in_t[l + 1] = _unpack_in(arrived[3])
        w_o, w_gu_t, w_d = w_rest[l]
        merged = _mix_fwd(proj, y_b, w_spatial[l], b_cols, vec(sgu_norm_g, l), f"mix_fwd_{l}")
        o, x1, h2 = _matmul_post_norm(merged, w_o, xl, vec(mix_post_g, l), f"out_proj_{l}", g_next=vec(ffn_pre_g, l))
        gate, up, act = _ffn_in(h2, w_gu_t, f"ffn_gu_{l}")
        if l + 1 < depth:
            dn, x_next, h_next = _matmul_post_norm(act, w_d, x1, vec(ffn_post_g, l), f"ffn_down_{l}",
                                                   g_next=vec(mix_pre_g, l + 1))
        else:
            dn, dy, loss, dd, dg_ffn_post = _matmul_post_norm(act, w_d, x1, vec(ffn_post_g, l), f"ffn_down_{l}",
                                                              target=target)
        saved.append(dict(x0=xl, h=h, proj=proj, b_pad=b_pad, b_cols=b_cols, c_pad=c_pad, y_b=y_b, e_row=e_row,
                          merged=merged, o=o, x1=x1, h2=h2, gate=gate, up=up, act=act, dn=dn))
        if l + 1 < depth:
            xl, h = x_next, h_next

    g_small = [None] * depth
    flights = {}
    late_token = None
    dx = dy
    for l in reversed(range(depth)):
        s = saved[l]
        w_o, w_gu_t, w_d = w_rest[l]
        dgu = _ffn_in_bwd(dd, w_d, s["gate"], s["up"], f"d_act_{l}")
        dw_d = _matmul(s["act"], dd, "tn", BF16, f"dw_down_{l}")
        dw_gu_t = _matmul(dgu, s["h2"], "tn", BF16, f"dw_gu_{l}", halves=True)
        dx1, dg_ffn_pre, do, dg_mix_post = _matmul_norm_bwd(
            dgu, w_gu_t, dx, s["x1"], vec(ffn_pre_g, l), f"d_h2_{l}", post=(s["o"], vec(mix_post_g, l)), halves=True)
        dmerged = _matmul(do, w_o, "nt", F32, f"d_merged_{l}")
        dw_o = _matmul(s["merged"], do, "tn", BF16, f"dw_out_{l}")
        tokens = []
        for key, dw, base, rows in [("out", dw_o, 0, SH_OUT), ("gate", dw_gu_t, 0, SH_FF), ("up", dw_gu_t, D_FF, SH_FF),
                                    ("down", dw_d, 0, SH_FF)]:
            flights[(key, l)], token = _exchange_start(("windows", base, rows, rows), dw, f"start_grad_{key}_{l}")
            tokens.append(token)
        dproj, dyb, delta_b, dw_s, dbt, dg_v = _mix_bwd(
            s["proj"], s["y_b"], dmerged, w_spatial[l], s["b_cols"], vec(sgu_norm_g, l), sel, f"mix_bwd_{l}",
            deps=tokens)
        db_s = dbt[:, :N_GROUPS].T.reshape(1, D_MODEL)
        tokens = []
        if l == 0:
            g_small[0] = jnp.concatenate([zero_row, dg_v, dg_mix_post, dg_ffn_pre, dg_ffn_post, db_s, zero_row,
                                          zero_row, dw_s.reshape(CHUNK, D_MODEL)], axis=0)
            flights["small"], token = _exchange_start(("spread",),jnp.concatenate([g_small[1], g_small[0]], axis=0),
                                                      "start_small_grads")
            tokens = [token]
        dproj, dc_key, dc_query = _attn_bwd_t(s["proj"], dyb, s["e_row"], delta_b, s["c_pad"], dproj,
                                              f"attn_bwd_{l}", deps=tokens)
        dc = jnp.pad((dc_key + dc_query)[:, 0, :].T, ((0, 0), (0, HEAD_DIM - N_HEADS)))
        dproj, db_f = _forget_bwd_t(s["proj"], s["b_pad"], dc, dproj, f"forget_bwd_{l}")
        db_f_row = jnp.pad(db_f, ((0, 0), (0, D_MODEL - HEAD_DIM)))
        dw_in_t = _matmul(dproj, s["h"], "tn", BF16, f"dw_in_{l}")
        flights[("in", l)], token = _exchange_start(("windows", 0, IN_STRIDE, P_IN), _pack_grad_in(dw_in_t),
                                                    f"start_grad_in_{l}")
        if l > 0:
            dx, dg_mix_pre, dd, dg_ffn_post_below = _matmul_norm_bwd(
                dproj, w_in_t[l], dx1, s["x0"], vec(mix_pre_g, l), f"d_h_{l}",
                post=(saved[l - 1]["dn"], vec(ffn_post_g, l - 1)), deps=[token])
            loss_row = jnp.pad(loss, ((0, 0), (0, D_MODEL - 1)))
            g_small[l] = jnp.concatenate([dg_mix_pre, dg_v, dg_mix_post, dg_ffn_pre, dg_ffn_post, db_s, db_f_row,
                                          loss_row, dw_s.reshape(CHUNK, D_MODEL)], axis=0)
            dg_ffn_post = dg_ffn_post_below
        else:
            dx, dg_mix_pre = _matmul_norm_bwd(dproj, w_in_t[l], dx1, s["x0"], vec(mix_pre_g, l), f"d_h_{l}",
                                              deps=[token])
            late_rows = jnp.concatenate([dg_mix_pre, db_f_row] + [zero_row] * 6, axis=0)
            flights["late"], late_token = _exchange_start(("spread",),late_rows, "start_late_small_grads")
    grad_x = dx.reshape(x.shape)

    landed, after = {}, [grad_x, late_token]
    rest = ["out", "gate", "up", "down"]
    for key in [(n, 1) for n in rest] + [("in", 1)] + [(n, 0) for n in rest] + ["small"]:
        label = key if isinstance(key, str) else f"grad_{key[0]}_{key[1]}"
        landed[key] = _exchange_wait(flights[key], after, f"wait_{label}")
        after = [landed[key]]
    small_sum = _sum_slots(landed["small"], S_ROWS, "sum_small_grads")
    sm = small_sum.reshape(depth, S_ROWS, D_MODEL)[::-1]
    loss = sm[depth - 1, 7, 0]
    g = {
        "sgu_norm_g": sm[:, 1], "mix_post_g": sm[:, 2],
        "ffn_pre_g": sm[:, 3], "ffn_post_g": sm[:, 4], "b_spatial": sm[:, 5].reshape(depth, N_GROUPS, CHUNK),
        "w_spatial": sm[:, 8:].reshape(depth, N_GROUPS, CHUNK, CHUNK),
    }
    for n in rest:
        g["w_" + n] = jnp.stack([_sum_slots(landed[(n, l)], landed[(n, l)].shape[1], f"sum_grad_{n}_{l}")
                                 for l in range(depth)])

    names = ["mix_pre_g", "w_in", "b_forget", "sgu_norm_g", "w_spatial", "b_spatial", "w_out", "mix_post_g",
             "ffn_pre_g", "w_gate", "w_up", "w_down", "ffn_post_g"]
    ws = dict(mix_pre_g=mix_pre_g, w_in=w_in, b_forget=b_forget, sgu_norm_g=sgu_norm_g, w_spatial=w_spatial,
              b_spatial=b_spatial, w_out=w_out, mix_post_g=mix_post_g, ffn_pre_g=ffn_pre_g, w_gate=w_gate, w_up=w_up,
              w_down=w_down, ffn_post_g=ffn_post_g)
    ms = dict(mix_pre_g=m_mix_pre_g, w_in=m_w_in, b_forget=m_b_forget, sgu_norm_g=m_sgu_norm_g, w_spatial=m_w_spatial,
              b_spatial=m_b_spatial, w_out=m_w_out, mix_post_g=m_mix_post_g, ffn_pre_g=m_ffn_pre_g, w_gate=m_w_gate,
              w_up=m_w_up, w_down=m_w_down, ffn_post_g=m_ffn_post_g)
    vs = dict(mix_pre_g=v_mix_pre_g, w_in=v_w_in, b_forget=v_b_forget, sgu_norm_g=v_sgu_norm_g, w_spatial=v_w_spatial,
              b_spatial=v_b_spatial, w_out=v_w_out, mix_post_g=v_mix_post_g, ffn_pre_g=v_ffn_pre_g, w_gate=v_w_gate,
              w_up=v_w_up, w_down=v_w_down, ffn_post_g=v_ffn_post_g)
    updates = {}
    last = ["w_in", "mix_pre_g", "b_forget"]
    transposed = ["w_in", "w_gate", "w_up"]
    for n in [n for n in names if n not in last] + last:
        if n == last[0]:
            sum_in_1 = _sum_slots(landed[("in", 1)], 304, "sum_grad_in_1")
            done = [u[0] for u in updates.values()] + [sum_in_1]
            sum_in_0 = _sum_slots(_exchange_wait(flights[("in", 0)], done, "wait_grad_in_0"), 304, "sum_grad_in_0")
            late = _sum_slots(_exchange_wait(flights["late"], [sum_in_0], "wait_late"), 8, "sum_late_small_grads")
            g["w_in"] = lax.dynamic_slice(jnp.stack([sum_in_0, sum_in_1]), (0, me, 0), (depth, SH_IN, D_MODEL))
            g["mix_pre_g"] = jnp.stack([late[0], sm[1, 0]])
            g["b_forget"] = jnp.stack([late[1, :N_HEADS], sm[1, 6, :N_HEADS]])
        if n in transposed:
            view = lambda a: jnp.swapaxes(a, 1, 2)
            updates[n] = [view(u) for u in _adamw_cols(view(ws[n]), g[n], view(ms[n]), view(vs[n]), f"adamw_{n}")]
            g[n] = view(g[n])
        else:
            flat = (-1, ws[n].shape[-1])
            updates[n] = [u.reshape(ws[n].shape) for u in
                          _adamw(ws[n].reshape(flat), g[n].reshape(flat), ms[n].reshape(flat), vs[n].reshape(flat),
                                 f"adamw_{n}")]
    deltas = [updates[n][0] for n in names]
    new_m = [updates[n][1] for n in names]
    new_v = [updates[n][2] for n in names]
    grads = [g[n].reshape(ws[n].shape) for n in names]
    return (loss, grad_x, *grads, *deltas, *new_m, *new_v)
```
